```python
import math
import jax, jax.numpy as jnp
from jax import lax
import numpy as np

D_MODEL = 1024
BATCH = 8
SEQ = 4096
DEPTH = 1

GRID_W = 64
DILATED_GROUPS = ((128, 1), (512, 4), (2048, 16))
N_GROUPS_A = len(DILATED_GROUPS)
HEADS_PER_GROUP_A = 8
HEAD_DIM_A = 64
N_HEADS_A = N_GROUPS_A * HEADS_PER_GROUP_A
GROUP_WIDTH_A = HEADS_PER_GROUP_A * HEAD_DIM_A
A_QKV_WIDTH = 3 * N_HEADS_A * HEAD_DIM_A
BAND_BLK = 64
N_HEADS_B = 8
N_KV_B = 2
GQA_GROUP_B = N_HEADS_B // N_KV_B
HEAD_DIM_B = 128
B_Q_WIDTH = N_HEADS_B * HEAD_DIM_B
B_KV_WIDTH = N_KV_B * HEAD_DIM_B
ROPE_THETA = 10000.0
Q_BLOCK = 128
N_BRANCHES = 2
GATE_WIDTH = N_BRANCHES * D_MODEL
IN_WIDTH = A_QKV_WIDTH + B_Q_WIDTH + 2 * B_KV_WIDTH + GATE_WIDTH
N_BUCKETS = 32
MAX_DISTANCE = 1024
D_FF = 2816
EPS = 1e-6
NEG_INF = -1e30

kernel_name = 'hybrid_dilated_axial_gqa_macaron'


def rmsnorm(x, g):
    xf = x.astype(jnp.float32)
    y = xf * lax.rsqrt(jnp.mean(xf * xf, axis=-1, keepdims=True) + EPS)
    return (y * g.astype(jnp.float32)).astype(x.dtype)


def swiglu(x, w1, w3, w2):
    return (jax.nn.silu(x @ w1) * (x @ w3)) @ w2


def t5_bucket(rel):
    n = N_BUCKETS // 2
    max_exact = n // 2
    ret = jnp.where(rel > 0, n, 0)
    a = jnp.abs(rel)
    af = jnp.maximum(a, 1).astype(jnp.float32)
    large = max_exact + (jnp.log(af / max_exact) / math.log(MAX_DISTANCE / max_exact)
                         * (n - max_exact)).astype(jnp.int32)
    large = jnp.minimum(large, n - 1)
    return ret + jnp.where(a < max_exact, a, large)


def dilated_group_attention(q, k, v, bias_tab, dilation, half):
    B, S, H, hd = q.shape
    L = S // dilation
    nblk = -(-L // BAND_BLK)
    Lp = nblk * BAND_BLK

    def to_sub(a):
        a = a.reshape(B, L, dilation, H, hd).transpose(0, 2, 1, 3, 4)
        return jnp.pad(a, ((0, 0), (0, 0), (0, Lp - L), (0, 0), (0, 0)))

    def band(a):
        a = jnp.pad(a, ((0, 0), (0, 0), (BAND_BLK, BAND_BLK), (0, 0), (0, 0)))
        blocks = a.reshape(B, dilation, nblk + 2, BAND_BLK, H, hd)
        return jnp.concatenate([blocks[:, :, :-2], blocks[:, :, 1:-1], blocks[:, :, 2:]], axis=3)

    qs = to_sub(q).reshape(B, dilation, nblk, BAND_BLK, H, hd)
    kb = band(to_sub(k))
    vb = band(to_sub(v))

    scores = jnp.einsum('brnqhd,brnkhd->brnhqk', qs, kb,
                        preferred_element_type=jnp.float32) * (hd ** -0.5)
    qi = jnp.arange(BAND_BLK, dtype=jnp.int32)
    ki = jnp.arange(3 * BAND_BLK, dtype=jnp.int32) - BAND_BLK
    rel_steps = ki[None, :] - qi[:, None]
    key_m = jnp.arange(nblk, dtype=jnp.int32)[:, None] * BAND_BLK + ki[None, :]
    valid = ((jnp.abs(rel_steps) <= half)[None]
             & ((key_m >= 0) & (key_m < L))[:, None, :])
    bias = bias_tab[t5_bucket(rel_steps * dilation)].transpose(2, 0, 1)
    scores = scores + bias.astype(jnp.float32)
    scores = jnp.where(valid[None, None, :, None], scores, NEG_INF)
    lse = jax.nn.logsumexp(scores, axis=-1)
    p = jnp.exp(scores - lse[..., None])
    out = jnp.einsum('brnhqk,brnkhd->brnqhd', p.astype(v.dtype), vb)
    out = out.reshape(B, dilation, Lp, H, hd)[:, :, :L]
    out = out.transpose(0, 2, 1, 3, 4).reshape(B, S, H, hd)
    lse = lse.transpose(0, 1, 2, 4, 3).reshape(B, dilation, Lp, H)[:, :, :L]
    lse = lse.transpose(0, 2, 1, 3).reshape(B, S, H)
    return out, lse


def axial_rope_tables(rows):
    row = jnp.repeat(jnp.arange(rows, dtype=jnp.float32), GRID_W)
    col = jnp.tile(jnp.arange(GRID_W, dtype=jnp.float32), rows)
    n_freq = HEAD_DIM_B // 4
    freq = ROPE_THETA ** (-jnp.arange(n_freq, dtype=jnp.float32) / n_freq)
    ang = jnp.concatenate([row[:, None] * freq, col[:, None] * freq], axis=-1)
    return jnp.cos(ang), jnp.sin(ang)


def apply_rope(x, cos, sin):
    xf = x.astype(jnp.float32).reshape(*x.shape[:-1], x.shape[-1] // 2, 2)
    x0, x1 = xf[..., 0], xf[..., 1]
    c = cos[None, :, None, :]
    s = sin[None, :, None, :]
    out = jnp.stack([x0 * c - x1 * s, x0 * s + x1 * c], axis=-1)
    return out.reshape(x.shape).astype(x.dtype)


def gqa_axial_attention(q, k, v, q_norm, k_norm, cos, sin):
    B, S = q.shape[0], q.shape[1]
    q = apply_rope(rmsnorm(q, q_norm), cos, sin)
    k = apply_rope(rmsnorm(k, k_norm), cos, sin)
    scale = HEAD_DIM_B ** -0.5
    qblocks = q.reshape(B, S // Q_BLOCK, Q_BLOCK, N_KV_B, GQA_GROUP_B, HEAD_DIM_B).swapaxes(0, 1)

    def attn_block(qb):
        s = jnp.einsum('bqkgd,bskd->bkgqs', qb, k, preferred_element_type=jnp.float32) * scale
        p = jax.nn.softmax(s, axis=-1)
        return jnp.einsum('bkgqs,bskd->bqkgd', p.astype(v.dtype), v)

    ob = lax.map(attn_block, qblocks)
    return ob.swapaxes(0, 1).reshape(B, S, B_Q_WIDTH)


def hybrid_mixer(h, w_in, b_gate, q_norm, k_norm, rel_bias, w_branch_a, w_branch_b, w_out, cos, sin):
    B, S, D = h.shape
    proj = h @ w_in
    o1 = A_QKV_WIDTH
    o2 = o1 + B_Q_WIDTH
    o3 = o2 + B_KV_WIDTH
    o4 = o3 + B_KV_WIDTH
    pa, pq, pk, pv, pg = proj[..., :o1], proj[..., o1:o2], proj[..., o2:o3], proj[..., o3:o4], proj[..., o4:]

    a = pa.reshape(B, S, 3, N_GROUPS_A, HEADS_PER_GROUP_A, HEAD_DIM_A)
    bias_groups = rel_bias.reshape(N_BUCKETS, N_GROUPS_A, HEADS_PER_GROUP_A)
    outs, lses = [], []
    for g, (window, dil) in enumerate(DILATED_GROUPS):
        o, lse = dilated_group_attention(a[:, :, 0, g], a[:, :, 1, g], a[:, :, 2, g],
                                         bias_groups[:, g], dil, window // (2 * dil))
        outs.append(o)
        lses.append(lse)
    wgt = jax.nn.softmax(jnp.stack(lses, axis=0), axis=0)
    o_a = jnp.sum(wgt[..., None] * jnp.stack(outs, axis=0).astype(jnp.float32), axis=0)
    o_a = o_a.astype(h.dtype).reshape(B, S, GROUP_WIDTH_A)

    o_b = gqa_axial_attention(pq.reshape(B, S, N_HEADS_B, HEAD_DIM_B),
                              pk.reshape(B, S, N_KV_B, HEAD_DIM_B),
                              pv.reshape(B, S, N_KV_B, HEAD_DIM_B),
                              q_norm, k_norm, cos, sin)

    gates = jax.nn.sigmoid((pg + b_gate).reshape(B, S, N_BRANCHES, D))
    merged = gates[:, :, 0] * (o_a @ w_branch_a) + gates[:, :, 1] * (o_b @ w_branch_b)
    return merged @ w_out


def _fwd_setup_inputs(seed: int = 0) -> dict:
    key = jax.random.key(seed)
    ks = jax.random.split(key, 24)
    f32 = jnp.float32

    def w(k, shape, fan_in):
        return jax.random.normal(k, shape, f32) * (fan_in ** -0.5)

    def gain(k, shape):
        return 1.0 + 0.05 * jax.random.normal(k, shape, f32)

    L, D = DEPTH, D_MODEL
    return {
        'x': jax.random.normal(ks[0], (BATCH, SEQ, D), f32),
        'ffn1_norm': gain(ks[1], (L, D)),
        'ffn1_w1': w(ks[2], (L, D, D_FF), D),
        'ffn1_w3': w(ks[3], (L, D, D_FF), D),
        'ffn1_w2': w(ks[4], (L, D_FF, D), D_FF),
        'mix_norm': gain(ks[5], (L, D)),
        'w_in': w(ks[6], (L, D, IN_WIDTH), D),
        'b_gate': 0.02 * jax.random.normal(ks[7], (L, GATE_WIDTH), f32),
        'q_norm': gain(ks[8], (L, HEAD_DIM_B)),
        'k_norm': gain(ks[9], (L, HEAD_DIM_B)),
        'rel_bias': 0.5 * jax.random.normal(ks[10], (N_BUCKETS, N_HEADS_A), f32),
        'w_branch_a': w(ks[11], (L, GROUP_WIDTH_A, D), GROUP_WIDTH_A),
        'w_branch_b': w(ks[12], (L, B_Q_WIDTH, D), B_Q_WIDTH),
        'w_out': w(ks[13], (L, D, D), D),
        'ffn2_norm': gain(ks[14], (L, D)),
        'ffn2_w1': w(ks[15], (L, D, D_FF), D),
        'ffn2_w3': w(ks[16], (L, D, D_FF), D),
        'ffn2_w2': w(ks[17], (L, D_FF, D), D_FF),
        'final_norm': gain(ks[18], (D,)),
    }


def _fwd_reference(x, ffn1_norm, ffn1_w1, ffn1_w3, ffn1_w2, mix_norm, w_in, b_gate, q_norm, k_norm,
              rel_bias, w_branch_a, w_branch_b, w_out, ffn2_norm, ffn2_w1, ffn2_w3, ffn2_w2,
              final_norm):
    S = x.shape[1]
    rows = S // GRID_W
    cos, sin = axial_rope_tables(rows)
    for l in range(DEPTH):
        x = x + 0.5 * swiglu(rmsnorm(x, ffn1_norm[l]), ffn1_w1[l], ffn1_w3[l], ffn1_w2[l])
        h = rmsnorm(x, mix_norm[l])
        x = x + hybrid_mixer(h, w_in[l], b_gate[l], q_norm[l], k_norm[l], rel_bias,
                             w_branch_a[l], w_branch_b[l], w_out[l], cos, sin)
        x = x + 0.5 * swiglu(rmsnorm(x, ffn2_norm[l]), ffn2_w1[l], ffn2_w3[l], ffn2_w2[l])
    return rmsnorm(x, final_norm)


import jax as _jax
import jax.numpy as _jnp

TWIN_FORMAT = 'train_step'
FWD_PARAMS = ['x', 'ffn1_norm', 'ffn1_w1', 'ffn1_w3', 'ffn1_w2', 'mix_norm', 'w_in', 'b_gate', 'q_norm', 'k_norm', 'rel_bias', 'w_branch_a', 'w_branch_b', 'w_out', 'ffn2_norm', 'ffn2_w1', 'ffn2_w3', 'ffn2_w2', 'final_norm']
TWIN_WEIGHTS = ['ffn1_norm', 'ffn1_w1', 'ffn1_w3', 'ffn1_w2', 'mix_norm', 'w_in', 'b_gate', 'q_norm', 'k_norm', 'rel_bias', 'w_branch_a', 'w_branch_b', 'w_out', 'ffn2_norm', 'ffn2_w1', 'ffn2_w3', 'ffn2_w2', 'final_norm']
TWIN_DIFF_INPUT = 'x'
TWIN_INPUTS = ['x', 'ffn1_norm', 'ffn1_w1', 'ffn1_w3', 'ffn1_w2', 'mix_norm', 'w_in', 'b_gate', 'q_norm', 'k_norm', 'rel_bias', 'w_branch_a', 'w_branch_b', 'w_out', 'ffn2_norm', 'ffn2_w1', 'ffn2_w3', 'ffn2_w2', 'final_norm', 'loss_target', 'm_ffn1_norm', 'm_ffn1_w1', 'm_ffn1_w3', 'm_ffn1_w2', 'm_mix_norm', 'm_w_in', 'm_b_gate', 'm_q_norm', 'm_k_norm', 'm_rel_bias', 'm_w_branch_a', 'm_w_branch_b', 'm_w_out', 'm_ffn2_norm', 'm_ffn2_w1', 'm_ffn2_w3', 'm_ffn2_w2', 'm_final_norm', 'v_ffn1_norm', 'v_ffn1_w1', 'v_ffn1_w3', 'v_ffn1_w2', 'v_mix_norm', 'v_w_in', 'v_b_gate', 'v_q_norm', 'v_k_norm', 'v_rel_bias', 'v_w_branch_a', 'v_w_branch_b', 'v_w_out', 'v_ffn2_norm', 'v_ffn2_w1', 'v_ffn2_w3', 'v_ffn2_w2', 'v_final_norm']
TWIN_OUTPUTS = ['loss', 'grad_x', 'grad_ffn1_norm', 'grad_ffn1_w1', 'grad_ffn1_w3', 'grad_ffn1_w2', 'grad_mix_norm', 'grad_w_in', 'grad_b_gate', 'grad_q_norm', 'grad_k_norm', 'grad_rel_bias', 'grad_w_branch_a', 'grad_w_branch_b', 'grad_w_out', 'grad_ffn2_norm', 'grad_ffn2_w1', 'grad_ffn2_w3', 'grad_ffn2_w2', 'grad_final_norm', 'delta_ffn1_norm', 'delta_ffn1_w1', 'delta_ffn1_w3', 'delta_ffn1_w2', 'delta_mix_norm', 'delta_w_in', 'delta_b_gate', 'delta_q_norm', 'delta_k_norm', 'delta_rel_bias', 'delta_w_branch_a', 'delta_w_branch_b', 'delta_w_out', 'delta_ffn2_norm', 'delta_ffn2_w1', 'delta_ffn2_w3', 'delta_ffn2_w2', 'delta_final_norm', 'new_m_ffn1_norm', 'new_m_ffn1_w1', 'new_m_ffn1_w3', 'new_m_ffn1_w2', 'new_m_mix_norm', 'new_m_w_in', 'new_m_b_gate', 'new_m_q_norm', 'new_m_k_norm', 'new_m_rel_bias', 'new_m_w_branch_a', 'new_m_w_branch_b', 'new_m_w_out', 'new_m_ffn2_norm', 'new_m_ffn2_w1', 'new_m_ffn2_w3', 'new_m_ffn2_w2', 'new_m_final_norm', 'new_v_ffn1_norm', 'new_v_ffn1_w1', 'new_v_ffn1_w3', 'new_v_ffn1_w2', 'new_v_mix_norm', 'new_v_w_in', 'new_v_b_gate', 'new_v_q_norm', 'new_v_k_norm', 'new_v_rel_bias', 'new_v_w_branch_a', 'new_v_w_branch_b', 'new_v_w_out', 'new_v_ffn2_norm', 'new_v_ffn2_w1', 'new_v_ffn2_w3', 'new_v_ffn2_w2', 'new_v_final_norm']
TWIN_LEAF_KINDS = {'loss': 'loss', 'grad_x': 'grad_x', 'grad_ffn1_norm': 'grad_w', 'grad_ffn1_w1': 'grad_w', 'grad_ffn1_w3': 'grad_w', 'grad_ffn1_w2': 'grad_w', 'grad_mix_norm': 'grad_w', 'grad_w_in': 'grad_w', 'grad_b_gate': 'grad_w', 'grad_q_norm': 'grad_w', 'grad_k_norm': 'grad_w', 'grad_rel_bias': 'grad_w', 'grad_w_branch_a': 'grad_w', 'grad_w_branch_b': 'grad_w', 'grad_w_out': 'grad_w', 'grad_ffn2_norm': 'grad_w', 'grad_ffn2_w1': 'grad_w', 'grad_ffn2_w3': 'grad_w', 'grad_ffn2_w2': 'grad_w', 'grad_final_norm': 'grad_w', 'delta_ffn1_norm': 'delta_w', 'delta_ffn1_w1': 'delta_w', 'delta_ffn1_w3': 'delta_w', 'delta_ffn1_w2': 'delta_w', 'delta_mix_norm': 'delta_w', 'delta_w_in': 'delta_w', 'delta_b_gate': 'delta_w', 'delta_q_norm': 'delta_w', 'delta_k_norm': 'delta_w', 'delta_rel_bias': 'delta_w', 'delta_w_branch_a': 'delta_w', 'delta_w_branch_b': 'delta_w', 'delta_w_out': 'delta_w', 'delta_ffn2_norm': 'delta_w', 'delta_ffn2_w1': 'delta_w', 'delta_ffn2_w3': 'delta_w', 'delta_ffn2_w2': 'delta_w', 'delta_final_norm': 'delta_w', 'new_m_ffn1_norm': 'new_m', 'new_m_ffn1_w1': 'new_m', 'new_m_ffn1_w3': 'new_m', 'new_m_ffn1_w2': 'new_m', 'new_m_mix_norm': 'new_m', 'new_m_w_in': 'new_m', 'new_m_b_gate': 'new_m', 'new_m_q_norm': 'new_m', 'new_m_k_norm': 'new_m', 'new_m_rel_bias': 'new_m', 'new_m_w_branch_a': 'new_m', 'new_m_w_branch_b': 'new_m', 'new_m_w_out': 'new_m', 'new_m_ffn2_norm': 'new_m', 'new_m_ffn2_w1': 'new_m', 'new_m_ffn2_w3': 'new_m', 'new_m_ffn2_w2': 'new_m', 'new_m_final_norm': 'new_m', 'new_v_ffn1_norm': 'new_v', 'new_v_ffn1_w1': 'new_v', 'new_v_ffn1_w3': 'new_v', 'new_v_ffn1_w2': 'new_v', 'new_v_mix_norm': 'new_v', 'new_v_w_in': 'new_v', 'new_v_b_gate': 'new_v', 'new_v_q_norm': 'new_v', 'new_v_k_norm': 'new_v', 'new_v_rel_bias': 'new_v', 'new_v_w_branch_a': 'new_v', 'new_v_w_branch_b': 'new_v', 'new_v_w_out': 'new_v', 'new_v_ffn2_norm': 'new_v', 'new_v_ffn2_w1': 'new_v', 'new_v_ffn2_w3': 'new_v', 'new_v_ffn2_w2': 'new_v', 'new_v_final_norm': 'new_v'}


def _forward(args):
    return _fwd_reference(*[args[k] for k in FWD_PARAMS])


def _output_shape():
    out = _jax.eval_shape(lambda: _forward(_fwd_setup_inputs(0)))
    return out.shape, out.dtype

N_MICROBATCH = 1
ADAM_LR = 0.001
ADAM_B1 = 0.9
ADAM_B2 = 0.999
ADAM_EPS = 1e-08
ADAM_WD = 0.01
ADAM_STEP = 10
PER_EXAMPLE_BATCH_AXIS = {'x': 0, 'loss_target': 0}
SHARED_INPUTS = []
_WEIGHT_DTYPES = {'ffn1_norm': _jnp.float32, 'ffn1_w1': _jnp.float32, 'ffn1_w3': _jnp.float32, 'ffn1_w2': _jnp.float32, 'mix_norm': _jnp.float32, 'w_in': _jnp.float32, 'b_gate': _jnp.float32, 'q_norm': _jnp.float32, 'k_norm': _jnp.float32, 'rel_bias': _jnp.float32, 'w_branch_a': _jnp.float32, 'w_branch_b': _jnp.float32, 'w_out': _jnp.float32, 'ffn2_norm': _jnp.float32, 'ffn2_w1': _jnp.float32, 'ffn2_w3': _jnp.float32, 'ffn2_w2': _jnp.float32, 'final_norm': _jnp.float32}
MOMENT_SCALE = {'ffn1_norm': 9.038968e-02, 'ffn1_w1': 3.326683e-02, 'ffn1_w3': 3.217465e-02, 'ffn1_w2': 5.339055e-02, 'mix_norm': 2.853455e-02, 'w_in': 1.005427e-02, 'b_gate': 4.356280e-03, 'q_norm': 2.867726e-02, 'k_norm': 3.034029e-02, 'rel_bias': 1.411976e-02, 'w_branch_a': 1.287859e-02, 'w_branch_b': 9.803314e-03, 'w_out': 1.624557e-02, 'ffn2_norm': 7.033584e-02, 'ffn2_w1': 3.058521e-02, 'ffn2_w3': 2.976044e-02, 'ffn2_w2': 4.936715e-02, 'final_norm': 3.201978e+01}


def _to_microbatches(a, axis):
    t = _jnp.moveaxis(a, axis, 0)
    t = t.reshape((N_MICROBATCH, t.shape[0] // N_MICROBATCH) + t.shape[1:])
    return _jnp.moveaxis(t, 1, axis + 1)


def setup_inputs(seed: int = 0) -> dict:
    inp = _fwd_setup_inputs(seed)
    key = _jax.random.fold_in(_jax.random.key(seed), 7919)
    shape, _ = _output_shape()
    out = dict(inp)
    out["loss_target"] = _jax.random.normal(_jax.random.fold_in(key, 0), shape, _jnp.float32)
    for i, name in enumerate(TWIN_WEIGHTS):
        w = inp[name].astype(_jnp.float32)
        if MOMENT_SCALE is None:
            s = _jnp.sqrt(_jnp.mean(_jnp.square(w)) + 1e-30)
        else:
            s = MOMENT_SCALE[name]
        km, kv = _jax.random.split(_jax.random.fold_in(key, i + 1))
        out[name] = w
        out["m_" + name] = s * _jax.random.normal(km, w.shape, _jnp.float32)
        out["v_" + name] = (s * s) * _jax.random.uniform(kv, w.shape, _jnp.float32, 0.5, 1.5)
    if N_MICROBATCH > 1:
        for name, axis in PER_EXAMPLE_BATCH_AXIS.items():
            out[name] = _to_microbatches(out[name], axis)
    return {'x': out['x'], 'ffn1_norm': out['ffn1_norm'], 'ffn1_w1': out['ffn1_w1'], 'ffn1_w3': out['ffn1_w3'], 'ffn1_w2': out['ffn1_w2'], 'mix_norm': out['mix_norm'], 'w_in': out['w_in'], 'b_gate': out['b_gate'], 'q_norm': out['q_norm'], 'k_norm': out['k_norm'], 'rel_bias': out['rel_bias'], 'w_branch_a': out['w_branch_a'], 'w_branch_b': out['w_branch_b'], 'w_out': out['w_out'], 'ffn2_norm': out['ffn2_norm'], 'ffn2_w1': out['ffn2_w1'], 'ffn2_w3': out['ffn2_w3'], 'ffn2_w2': out['ffn2_w2'], 'final_norm': out['final_norm'], 'loss_target': out['loss_target'], 'm_ffn1_norm': out['m_ffn1_norm'], 'm_ffn1_w1': out['m_ffn1_w1'], 'm_ffn1_w3': out['m_ffn1_w3'], 'm_ffn1_w2': out['m_ffn1_w2'], 'm_mix_norm': out['m_mix_norm'], 'm_w_in': out['m_w_in'], 'm_b_gate': out['m_b_gate'], 'm_q_norm': out['m_q_norm'], 'm_k_norm': out['m_k_norm'], 'm_rel_bias': out['m_rel_bias'], 'm_w_branch_a': out['m_w_branch_a'], 'm_w_branch_b': out['m_w_branch_b'], 'm_w_out': out['m_w_out'], 'm_ffn2_norm': out['m_ffn2_norm'], 'm_ffn2_w1': out['m_ffn2_w1'], 'm_ffn2_w3': out['m_ffn2_w3'], 'm_ffn2_w2': out['m_ffn2_w2'], 'm_final_norm': out['m_final_norm'], 'v_ffn1_norm': out['v_ffn1_norm'], 'v_ffn1_w1': out['v_ffn1_w1'], 'v_ffn1_w3': out['v_ffn1_w3'], 'v_ffn1_w2': out['v_ffn1_w2'], 'v_mix_norm': out['v_mix_norm'], 'v_w_in': out['v_w_in'], 'v_b_gate': out['v_b_gate'], 'v_q_norm': out['v_q_norm'], 'v_k_norm': out['v_k_norm'], 'v_rel_bias': out['v_rel_bias'], 'v_w_branch_a': out['v_w_branch_a'], 'v_w_branch_b': out['v_w_branch_b'], 'v_w_out': out['v_w_out'], 'v_ffn2_norm': out['v_ffn2_norm'], 'v_ffn2_w1': out['v_ffn2_w1'], 'v_ffn2_w3': out['v_ffn2_w3'], 'v_ffn2_w2': out['v_ffn2_w2'], 'v_final_norm': out['v_final_norm']}


def _loss(weights, diff, rest, loss_target):
    with _jax.named_scope("forward"):
        args = {**rest, TWIN_DIFF_INPUT: diff, **{k: w.astype(_WEIGHT_DTYPES[k]) for k, w in weights.items()}}
        y = _forward(args)
    with _jax.named_scope("loss_head"):
        err = _jnp.square(y.astype(_jnp.float32) - loss_target)
        return 0.5 * _jnp.sum(_jnp.mean(err, axis=-1)) if err.ndim else 0.5 * err


def _adamw(w, g, m, v):
    m = ADAM_B1 * m + (1.0 - ADAM_B1) * g
    v = ADAM_B2 * v + (1.0 - ADAM_B2) * _jnp.square(g)
    m_hat = m / (1.0 - ADAM_B1 ** ADAM_STEP)
    v_hat = v / (1.0 - ADAM_B2 ** ADAM_STEP)
    delta = -ADAM_LR * (m_hat / (_jnp.sqrt(v_hat) + ADAM_EPS) + ADAM_WD * w)
    return delta, m, v


def reference(x, ffn1_norm, ffn1_w1, ffn1_w3, ffn1_w2, mix_norm, w_in, b_gate, q_norm, k_norm, rel_bias, w_branch_a, w_branch_b, w_out, ffn2_norm, ffn2_w1, ffn2_w3, ffn2_w2, final_norm, loss_target, m_ffn1_norm, m_ffn1_w1, m_ffn1_w3, m_ffn1_w2, m_mix_norm, m_w_in, m_b_gate, m_q_norm, m_k_norm, m_rel_bias, m_w_branch_a, m_w_branch_b, m_w_out, m_ffn2_norm, m_ffn2_w1, m_ffn2_w3, m_ffn2_w2, m_final_norm, v_ffn1_norm, v_ffn1_w1, v_ffn1_w3, v_ffn1_w2, v_mix_norm, v_w_in, v_b_gate, v_q_norm, v_k_norm, v_rel_bias, v_w_branch_a, v_w_branch_b, v_w_out, v_ffn2_norm, v_ffn2_w1, v_ffn2_w3, v_ffn2_w2, v_final_norm):
    given = dict(x=x, ffn1_norm=ffn1_norm, ffn1_w1=ffn1_w1, ffn1_w3=ffn1_w3, ffn1_w2=ffn1_w2, mix_norm=mix_norm, w_in=w_in, b_gate=b_gate, q_norm=q_norm, k_norm=k_norm, rel_bias=rel_bias, w_branch_a=w_branch_a, w_branch_b=w_branch_b, w_out=w_out, ffn2_norm=ffn2_norm, ffn2_w1=ffn2_w1, ffn2_w3=ffn2_w3, ffn2_w2=ffn2_w2, final_norm=final_norm, loss_target=loss_target, m_ffn1_norm=m_ffn1_norm, m_ffn1_w1=m_ffn1_w1, m_ffn1_w3=m_ffn1_w3, m_ffn1_w2=m_ffn1_w2, m_mix_norm=m_mix_norm, m_w_in=m_w_in, m_b_gate=m_b_gate, m_q_norm=m_q_norm, m_k_norm=m_k_norm, m_rel_bias=m_rel_bias, m_w_branch_a=m_w_branch_a, m_w_branch_b=m_w_branch_b, m_w_out=m_w_out, m_ffn2_norm=m_ffn2_norm, m_ffn2_w1=m_ffn2_w1, m_ffn2_w3=m_ffn2_w3, m_ffn2_w2=m_ffn2_w2, m_final_norm=m_final_norm, v_ffn1_norm=v_ffn1_norm, v_ffn1_w1=v_ffn1_w1, v_ffn1_w3=v_ffn1_w3, v_ffn1_w2=v_ffn1_w2, v_mix_norm=v_mix_norm, v_w_in=v_w_in, v_b_gate=v_b_gate, v_q_norm=v_q_norm, v_k_norm=v_k_norm, v_rel_bias=v_rel_bias, v_w_branch_a=v_w_branch_a, v_w_branch_b=v_w_branch_b, v_w_out=v_w_out, v_ffn2_norm=v_ffn2_norm, v_ffn2_w1=v_ffn2_w1, v_ffn2_w3=v_ffn2_w3, v_ffn2_w2=v_ffn2_w2, v_final_norm=v_final_norm)
    weights = {n: given[n] for n in TWIN_WEIGHTS}
    shared = {n: given[n] for n in SHARED_INPUTS}
    per_example = {n: given[n] for n in ['x']}
    grad_fn = _jax.value_and_grad(_loss, argnums=(0, 1))

    def one_microbatch(ex, loss_target):
        ex = dict(ex)
        diff = ex.pop(TWIN_DIFF_INPUT)
        return grad_fn(weights, diff, {**shared, **ex}, loss_target)

    if N_MICROBATCH == 1:
        loss, (grad_w, grad_x) = one_microbatch(per_example, given["loss_target"])
    else:
        def body(carry, xs):
            loss_sum, grad_sum = carry
            l_k, (gw_k, gx_k) = one_microbatch(xs[0], xs[1])
            with _jax.named_scope("update"):
                return (loss_sum + l_k, _jax.tree.map(_jnp.add, grad_sum, gw_k)), gx_k

        init = (_jnp.zeros((), _jnp.float32), _jax.tree.map(_jnp.zeros_like, weights))
        (loss, grad_w), grad_x = _jax.lax.scan(body, init, (per_example, given["loss_target"]))
    with _jax.named_scope("update"):
        delta_w, new_m, new_v = {}, {}, {}
        for n in TWIN_WEIGHTS:
            delta_w[n], new_m[n], new_v[n] = _adamw(weights[n], grad_w[n], given["m_" + n], given["v_" + n])
    return (loss, grad_x, *[grad_w[n] for n in TWIN_WEIGHTS], *[delta_w[n] for n in TWIN_WEIGHTS],
            *[new_m[n] for n in TWIN_WEIGHTS], *[new_v[n] for n in TWIN_WEIGHTS])
```

```python
import math

import jax
import jax.numpy as jnp
from jax import lax
from jax.experimental import pallas as pl
from jax.experimental.pallas import tpu as pltpu

F32 = jnp.float32
BF16 = jnp.bfloat16
MESH = pl.DeviceIdType.MESH

V7X_VMEM_BYTES = 64 * 1024 * 1024
VMEM_LIMIT = V7X_VMEM_BYTES - 8 * 1024 * 1024
LANES = 128

N_DEV = 8
EPS = 1e-6
NEG_INF = -1e30

DILATIONS = (1, 4, 16)
HALF_WINDOW = 64
HEAD_DIM_A = 64
HEADS_PER_GROUP_A = 8
GROUP_WIDTH_A = 512
A_QKV_WIDTH = 4608
A_BLOCKS_PER_TOKEN = A_QKV_WIDTH // LANES
A_TQ = 128
A_WIN = A_TQ + 2 * HALF_WINDOW
HEAD_DIM_B = 128
N_HEADS_B = 8
N_KV_B = 2
GQA_GROUP_B = 4
GRID_W = 64
ROPE_THETA = 10000.0
B_TQ = 128
N_BUCKETS = 32
MAX_DISTANCE = 1024
PB_WIDTH = 3584
PB_GATE_A = 1536
PB_GATE_B = 2560

ADAM_LR = 0.001
ADAM_B1 = 0.9
ADAM_B2 = 0.999
ADAM_EPS = 1e-08
ADAM_WD = 0.01
ADAM_STEP = 10

FFN_SHARD = 352
MIX_WIN, MIX_WB, MIX_WOUT, MIX_WA = 0, 1024, 1152, 1280
MIX_ROWS = 1344


def _dot(a, b, ca=1, cb=0):
    return lax.dot_general(a, b, (((ca,), (cb,)), ((), ())), preferred_element_type=F32)


def _call(name, body, grid, ins, outs, scratch=(), sem=None, aliases=None):
    res = pl.pallas_call(
        body,
        out_shape=[jax.ShapeDtypeStruct(s, d) for (s, d, _, _) in outs],
        grid=grid,
        in_specs=[pl.BlockSpec(bs, im) for (_, bs, im) in ins],
        out_specs=[pl.BlockSpec(bs, im) for (_, _, bs, im) in outs],
        scratch_shapes=list(scratch),
        name=name,
        input_output_aliases=aliases or {},
        compiler_params=pltpu.CompilerParams(dimension_semantics=sem, vmem_limit_bytes=VMEM_LIMIT),
    )(*[a for (a, _, _) in ins])
    return res


def _sigmoid(x):
    return 1.0 / (1.0 + jnp.exp(-x))


def _position():
    return lax.axis_index("x"), lax.axis_index("y"), lax.axis_index("c")


def _hbm_specs(n):
    return [pl.BlockSpec(memory_space=pl.ANY) for _ in range(n)]


def all_gather_groups(groups):
    n = len(groups)

    def body(*refs):
        ins, outs = refs[:n], refs[n:2 * n]
        send_sems, recv_sems, local_sems = refs[2 * n:]
        x, y, c = _position()
        sibling = (x, y, 1 - c)
        chips = [(1 - x, y), (x, 1 - y), (1 - x, 1 - y)]

        def copy(i, k, block, to, src=None):
            px, py, pc = block
            dst = outs[i].at[4 * px + 2 * py + pc]
            return pltpu.make_async_remote_copy(
                src_ref=dst if src is None else src, dst_ref=dst,
                send_sem=send_sems.at[i, k], recv_sem=recv_sems.at[i, k],
                device_id=to, device_id_type=MESH)

        started = []
        for i in range(n):
            mine = pltpu.make_async_copy(ins[i], outs[i].at[4 * x + 2 * y + c], local_sems.at[i])
            mine.start()
            started.append(mine)
        sends = []
        for i in range(n):
            first = [copy(i, 0, (x, y, c), sibling, src=ins[i])]
            first += [copy(i, 1 + j, (x, y, c), (*chip, c), src=ins[i]) for j, chip in enumerate(chips)]
            for cp in first:
                cp.start()
            sends += first
        for i in range(n):
            for j, chip in enumerate(chips):
                copy(i, 1 + j, (*chip, c), (x, y, c)).wait_recv()
                passed = copy(i, 4 + j, (*chip, c), sibling)
                passed.start()
                sends.append(passed)
        for i in range(n):
            copy(i, 0, sibling, (x, y, c)).wait_recv()
            for j, chip in enumerate(chips):
                copy(i, 4 + j, (*chip, 1 - c), (x, y, c)).wait_recv()
        for cp in sends:
            cp.wait_send()
        for mine in started:
            mine.wait()

    return pl.pallas_call(
        body,
        out_shape=[jax.ShapeDtypeStruct((N_DEV,) + g.shape, g.dtype) for g in groups],
        in_specs=_hbm_specs(n),
        out_specs=_hbm_specs(n),
        scratch_shapes=[pltpu.SemaphoreType.DMA((n, 7)), pltpu.SemaphoreType.DMA((n, 7)),
                        pltpu.SemaphoreType.DMA((n,))],
        name="all_gather_weights",
    )(*groups)


def reduce_scatter_pair(grads):
    n = len(grads)

    def body(*refs):
        ins, mine, theirs = refs[:n], refs[n:2 * n], refs[2 * n:3 * n]
        send_sems, recv_sems, local_sems = refs[3 * n:]
        x, y, c = _position()
        sibling = (x, y, 1 - c)
        remote, local = [], []
        for i in range(n):
            for q in range(4):
                lc = pltpu.make_async_copy(ins[i].at[2 * q + c], mine[i].at[q], local_sems.at[i, q])
                lc.start()
                local.append(lc)
                rc = pltpu.make_async_remote_copy(
                    src_ref=ins[i].at[2 * q + (1 - c)], dst_ref=theirs[i].at[q],
                    send_sem=send_sems.at[i, q], recv_sem=recv_sems.at[i, q],
                    device_id=sibling, device_id_type=MESH)
                rc.start()
                remote.append(rc)
        for rc in remote:
            rc.wait()
        for lc in local:
            lc.wait()

    shapes = [jax.ShapeDtypeStruct((4,) + g.shape[1:], g.dtype) for g in grads]
    res = pl.pallas_call(
        body,
        out_shape=shapes + shapes,
        in_specs=_hbm_specs(n),
        out_specs=_hbm_specs(2 * n),
        scratch_shapes=[pltpu.SemaphoreType.DMA((n, 4)), pltpu.SemaphoreType.DMA((n, 4)),
                        pltpu.SemaphoreType.DMA((n, 4))],
        name="reduce_scatter_pair",
    )(*grads)
    return res[:n], res[n:]


def reduce_scatter_chips(parts, small):
    n = len(parts)

    def body(*refs):
        ins, small_ref = refs[:n], refs[n]
        outs, smalls = refs[n + 1:2 * n + 1], refs[2 * n + 1]
        send_sems, recv_sems, local_sems, s_send, s_recv, s_local = refs[2 * n + 2:]
        x, y, c = _position()
        chip = 2 * x + y
        me = 4 * x + 2 * y + c
        others = [(1 - x, y), (x, 1 - y), (1 - x, 1 - y)]
        remote, local = [], []
        for i in range(n):
            lc = pltpu.make_async_copy(ins[i].at[chip], outs[i].at[chip], local_sems.at[i])
            lc.start()
            local.append(lc)
            for j, (px, py) in enumerate(others):
                rc = pltpu.make_async_remote_copy(
                    src_ref=ins[i].at[2 * px + py], dst_ref=outs[i].at[chip],
                    send_sem=send_sems.at[i, j], recv_sem=recv_sems.at[i, j],
                    device_id=(px, py, c), device_id_type=MESH)
                rc.start()
                remote.append(rc)
        lc = pltpu.make_async_copy(small_ref, smalls.at[me], s_local)
        lc.start()
        local.append(lc)
        k = 0
        for dx in (0, 1):
            for dy in (0, 1):
                for dc in (0, 1):
                    if dx + dy + dc == 0:
                        continue
                    peer = (1 - x if dx else x, 1 - y if dy else y, 1 - c if dc else c)
                    rc = pltpu.make_async_remote_copy(
                        src_ref=small_ref, dst_ref=smalls.at[me],
                        send_sem=s_send.at[k], recv_sem=s_recv.at[k],
                        device_id=peer, device_id_type=MESH)
                    rc.start()
                    remote.append(rc)
                    k += 1
        for rc in remote:
            rc.wait()
        for lc in local:
            lc.wait()

    res = pl.pallas_call(
        body,
        out_shape=[jax.ShapeDtypeStruct(p.shape, p.dtype) for p in parts]
        + [jax.ShapeDtypeStruct((N_DEV,) + small.shape, small.dtype)],
        in_specs=_hbm_specs(n + 1),
        out_specs=_hbm_specs(n + 1),
        scratch_shapes=[pltpu.SemaphoreType.DMA((n, 3)), pltpu.SemaphoreType.DMA((n, 3)),
                        pltpu.SemaphoreType.DMA((n,)), pltpu.SemaphoreType.DMA((7,)),
                        pltpu.SemaphoreType.DMA((7,)), pltpu.SemaphoreType.DMA],
        name="reduce_scatter_chips",
    )(*parts, small)
    return res[:n], res[n]


def pair_add(a, b, name):
    _, R, C = a.shape
    tr = R // 2 if R % 32 == 0 else R

    def body(a_ref, b_ref, o_ref):
        o_ref[...] = (a_ref[...].astype(F32) + b_ref[...].astype(F32)).astype(BF16)

    spec = ((None, tr, C), lambda q, i: (q, i, 0))
    return _call(name, body, (4, R // tr), [(a,) + spec, (b,) + spec], [(a.shape, BF16) + spec],
                 sem=("parallel", "parallel"))[0]


def sum_slots(recv, off, rows, blk, name):
    nq, _, C = recv.shape
    ob = off // blk

    def body(r_ref, o_ref):
        acc = r_ref[0].astype(F32)
        for q in range(1, nq):
            acc = acc + r_ref[q].astype(F32)
        o_ref[...] = acc

    return _call(name, body, (rows // blk,),
                 [(recv, (nq, blk, C), lambda i: (0, ob + i, 0))],
                 [((rows, C), F32, (blk, C), lambda i: (i, 0))], sem=("parallel",))[0]


def adamw(w, g, m, v, name):
    R, C = w.shape
    tr = R
    for cand in (256, 128, 64, 32, 16, 8):
        if R % cand == 0 and R > cand:
            tr = cand
            break
    c1 = 1.0 / (1.0 - ADAM_B1 ** ADAM_STEP)
    c2 = 1.0 / (1.0 - ADAM_B2 ** ADAM_STEP)

    def body(w_ref, g_ref, m_ref, v_ref, d_ref, nm_ref, nv_ref):
        gv = g_ref[...]
        nm = ADAM_B1 * m_ref[...] + (1.0 - ADAM_B1) * gv
        nv = ADAM_B2 * v_ref[...] + (1.0 - ADAM_B2) * (gv * gv)
        d_ref[...] = -ADAM_LR * ((nm * c1) / (jnp.sqrt(nv * c2) + ADAM_EPS) + ADAM_WD * w_ref[...])
        nm_ref[...] = nm
        nv_ref[...] = nv

    spec = ((tr, C), lambda i: (i, 0))
    out = ((R, C), F32) + spec
    return _call(name, body, (R // tr,), [(w,) + spec, (g,) + spec, (m,) + spec, (v,) + spec],
                 [out, out, out], sem=("parallel",))


def rms_fwd(x, g, name):
    S, D = x.shape
    tr = 512

    def body(x_ref, g_ref, o_ref):
        xv = x_ref[...]
        r = lax.rsqrt(jnp.mean(xv * xv, axis=-1, keepdims=True) + EPS)
        o_ref[...] = (xv * r * g_ref[...]).astype(BF16)

    return _call(name, body, (S // tr,),
                 [(x, (tr, D), lambda i: (i, 0)), (g, (1, D), lambda i: (0, 0))],
                 [((S, D), BF16, (tr, D), lambda i: (i, 0))], sem=("parallel",))[0]


def _rms_bwd_tile(dn, xv, gv):
    r = lax.rsqrt(jnp.mean(xv * xv, axis=-1, keepdims=True) + EPS)
    xh = xv * r
    dxh = dn * gv
    dx = r * (dxh - xh * jnp.mean(dxh * xh, axis=-1, keepdims=True))
    return dx, dn * xh


def final_loss(x, tgt, g, name):
    S, D = x.shape
    tr = 256

    def body(x_ref, t_ref, g_ref, l_ref, dx_ref, dg_ref):
        i = pl.program_id(0)
        xv, gv = x_ref[...], g_ref[...]
        r = lax.rsqrt(jnp.mean(xv * xv, axis=-1, keepdims=True) + EPS)
        xh = xv * r
        e = xh * gv - t_ref[...]
        part = 0.5 * jnp.sum(jnp.sum(e * e, axis=-1, keepdims=True) * (1.0 / D), axis=0, keepdims=True)
        dy = e * (1.0 / D)
        dxh = dy * gv
        dx_ref[...] = r * (dxh - xh * jnp.mean(dxh * xh, axis=-1, keepdims=True))
        dgp = jnp.sum(dy * xh, axis=0, keepdims=True)

        @pl.when(i == 0)
        def _():
            l_ref[...] = jnp.broadcast_to(part, l_ref.shape)
            dg_ref[...] = dgp

        @pl.when(i > 0)
        def _():
            l_ref[...] += jnp.broadcast_to(part, l_ref.shape)
            dg_ref[...] += dgp

    row = ((tr, D), lambda i: (i, 0))
    return _call(name, body, (S // tr,),
                 [(x,) + row, (tgt,) + row, (g, (1, D), lambda i: (0, 0))],
                 [((1, LANES), F32, (1, LANES), lambda i: (0, 0)), ((S, D), F32) + row,
                  ((1, D), F32, (1, D), lambda i: (0, 0))], sem=("arbitrary",))


FFN_TF = 4 * FFN_SHARD


def _ffn_w_spec(G, which, imap):
    D = G.shape[2]
    return (G, (4, FFN_SHARD, D), lambda *idx: (imap(*idx), which, 0))


def ffn_up(n, G, name):
    S, D = n.shape
    F = N_DEV * FFN_SHARD
    tm = 512

    def body(n_ref, w1_ref, w3_ref, ab_ref):
        nv = n_ref[...]
        ab_ref[0] = _dot(nv, w1_ref[...].reshape(FFN_TF, D), 1, 1).astype(BF16)
        ab_ref[1] = _dot(nv, w3_ref[...].reshape(FFN_TF, D), 1, 1).astype(BF16)

    return _call(name, body, (F // FFN_TF, S // tm),
                 [(n, (tm, D), lambda j, i: (i, 0)),
                  _ffn_w_spec(G, 0, lambda j, i: j), _ffn_w_spec(G, 1, lambda j, i: j)],
                 [((2, S, F), BF16, (2, tm, FFN_TF), lambda j, i: (0, i, j))],
                 sem=("parallel", "parallel"))[0]


def ffn_down(ab, G, x, name):
    _, S, F = ab.shape
    D = x.shape[1]
    tm = 512
    nk = F // FFN_TF

    def body(ab_ref, w2_ref, x_ref, o_ref, acc_ref):
        k = pl.program_id(1)
        av, bv = ab_ref[0].astype(F32), ab_ref[1].astype(F32)
        h = (av * _sigmoid(av) * bv).astype(BF16)
        p = _dot(h, w2_ref[...].reshape(FFN_TF, D))

        @pl.when(k == 0)
        def _():
            acc_ref[...] = p

        @pl.when(k > 0)
        def _():
            acc_ref[...] += p

        @pl.when(k == nk - 1)
        def _():
            o_ref[...] = x_ref[...] + 0.5 * acc_ref[...]

    return _call(name, body, (S // tm, nk),
                 [(ab, (2, tm, FFN_TF), lambda i, k: (0, i, k)), _ffn_w_spec(G, 2, lambda i, k: k),
                  (x, (tm, D), lambda i, k: (i, 0))],
                 [((S, D), F32, (tm, D), lambda i, k: (i, 0))],
                 scratch=[pltpu.VMEM((tm, D), F32)], sem=("parallel", "arbitrary"))[0]


def ffn_bwd(dxo, ab, n, G, x_in, g, name):
    _, S, F = ab.shape
    D = x_in.shape[1]
    tm = 512
    nf = F // FFN_TF

    def down_body(d_ref, w2_ref, ab_ref, o_ref):
        dh = 0.5 * _dot(d_ref[...].astype(BF16), w2_ref[...].reshape(FFN_TF, D), 1, 1)
        av, bv = ab_ref[0].astype(F32), ab_ref[1].astype(F32)
        sig = _sigmoid(av)
        silu = av * sig
        o_ref[0] = (dh * bv * (sig * (1.0 + av * (1.0 - sig)))).astype(BF16)
        o_ref[1] = (dh * silu).astype(BF16)
        o_ref[2] = (silu * bv).astype(BF16)

    dabh = _call(name + "_down_bwd", down_body, (nf, S // tm),
                 [(dxo, (tm, D), lambda j, i: (i, 0)), _ffn_w_spec(G, 2, lambda j, i: j),
                  (ab, (2, tm, FFN_TF), lambda j, i: (0, i, j))],
                 [((3, S, F), BF16, (3, tm, FFN_TF), lambda j, i: (0, i, j))],
                 sem=("parallel", "parallel"))[0]

    tk = 512
    nk = S // tk
    gshape = (N_DEV, 3 * FFN_SHARD, D)

    def dw2_body(h_ref, d_ref, o_ref, acc_ref):
        k = pl.program_id(1)
        p = _dot(h_ref[...], d_ref[...].astype(BF16), 0, 0)

        @pl.when(k == 0)
        def _():
            acc_ref[...] = p

        @pl.when(k > 0)
        def _():
            acc_ref[...] += p

        @pl.when(k == nk - 1)
        def _():
            o_ref[...] = (0.5 * acc_ref[...]).astype(BF16).reshape(4, FFN_SHARD, D)

    gw = _call(name + "_dw2", dw2_body, (nf, nk),
               [(dabh, (None, tk, FFN_TF), lambda j, k: (2, k, j)), (dxo, (tk, D), lambda j, k: (k, 0))],
               [(gshape, BF16, (4, FFN_SHARD, D), lambda j, k: (j, 2, 0))],
               scratch=[pltpu.VMEM((FFN_TF, D), F32)], sem=("parallel", "arbitrary"))[0]

    def dw13_body(gw_ref, dab_ref, n_ref, o_ref, acc_ref):
        k = pl.program_id(2)
        p = _dot(dab_ref[...], n_ref[...], 0, 0)

        @pl.when(k == 0)
        def _():
            acc_ref[...] = p

        @pl.when(k > 0)
        def _():
            acc_ref[...] += p

        @pl.when(k == nk - 1)
        def _():
            o_ref[...] = acc_ref[...].astype(BF16).reshape(4, FFN_SHARD, D)

    gw = pl.pallas_call(
        dw13_body,
        out_shape=jax.ShapeDtypeStruct(gshape, BF16),
        grid=(2, nf, nk),
        in_specs=[pl.BlockSpec(memory_space=pl.ANY),
                  pl.BlockSpec((None, tk, FFN_TF), lambda w, j, k: (w, k, j)),
                  pl.BlockSpec((tk, D), lambda w, j, k: (k, 0))],
        out_specs=pl.BlockSpec((4, FFN_SHARD, D), lambda w, j, k: (j, w, 0)),
        scratch_shapes=[pltpu.VMEM((FFN_TF, D), F32)],
        input_output_aliases={0: 0},
        name=name + "_dw13",
        compiler_params=pltpu.CompilerParams(dimension_semantics=("parallel", "parallel", "arbitrary"),
                                             vmem_limit_bytes=VMEM_LIMIT),
    )(gw, dabh, n)

    def dn_body(dab_ref, w1_ref, w3_ref, x_ref, d_ref, g_ref, dx_ref, dg_ref, acc_ref):
        i, k = pl.program_id(0), pl.program_id(1)
        p = _dot(dab_ref[0], w1_ref[...].reshape(FFN_TF, D)) + _dot(dab_ref[1], w3_ref[...].reshape(FFN_TF, D))

        @pl.when(k == 0)
        def _():
            acc_ref[...] = p

        @pl.when(k > 0)
        def _():
            acc_ref[...] += p

        @pl.when(k == nf - 1)
        def _():
            dx, dgt = _rms_bwd_tile(acc_ref[...], x_ref[...], g_ref[...])
            dx_ref[...] = d_ref[...] + dx
            dgp = jnp.sum(dgt, axis=0, keepdims=True)

            @pl.when(i == 0)
            def _():
                dg_ref[...] = dgp

            @pl.when(i > 0)
            def _():
                dg_ref[...] += dgp

    dx, dg = _call(name + "_dn", dn_body, (S // tm, nf),
                   [(dabh, (2, tm, FFN_TF), lambda i, k: (0, i, k)),
                    _ffn_w_spec(G, 0, lambda i, k: k), _ffn_w_spec(G, 1, lambda i, k: k),
                    (x_in, (tm, D), lambda i, k: (i, 0)), (dxo, (tm, D), lambda i, k: (i, 0)),
                    (g, (1, D), lambda i, k: (0, 0))],
                   [((S, D), F32, (tm, D), lambda i, k: (i, 0)), ((1, D), F32, (1, D), lambda i, k: (0, 0))],
                   scratch=[pltpu.VMEM((tm, D), F32)], sem=("arbitrary", "arbitrary"))
    return dx, dg, gw


PROJ_TN = 512


def in_proj(h, Gm, first_tile, n_tiles, dtype, name):
    S, D = h.shape
    tm = 1024

    def body(h_ref, w_ref, o_ref):
        o_ref[...] = _dot(h_ref[...], w_ref[...]).astype(dtype)

    return _call(name, body, (n_tiles, S // tm),
                 [(h, (tm, D), lambda j, i: (i, 0)),
                  (Gm, (None, D, PROJ_TN), lambda j, i: ((first_tile + j) // 2, 0, (first_tile + j) % 2))],
                 [((S, n_tiles * PROJ_TN), dtype, (tm, PROJ_TN), lambda j, i: (i, j))],
                 sem=("parallel", "parallel"))[0]


def in_proj_bwd(dproj, h, Gm, x_in, g, dres, gm_grads, name):
    S, D = h.shape
    NT = dproj.shape[1] // PROJ_TN
    tk = 512
    nk = S // tk

    def dw_body(gm_ref, h_ref, d_ref, o_ref, acc_ref):
        k = pl.program_id(1)
        p = _dot(h_ref[...], d_ref[...], 0, 0)

        @pl.when(k == 0)
        def _():
            acc_ref[...] = p

        @pl.when(k > 0)
        def _():
            acc_ref[...] += p

        @pl.when(k == nk - 1)
        def _():
            o_ref[...] = acc_ref[...].astype(BF16)

    gm_grads = pl.pallas_call(
        dw_body,
        out_shape=jax.ShapeDtypeStruct(gm_grads.shape, BF16),
        grid=(NT, nk),
        in_specs=[pl.BlockSpec(memory_space=pl.ANY),
                  pl.BlockSpec((tk, D), lambda j, k: (k, 0)),
                  pl.BlockSpec((tk, PROJ_TN), lambda j, k: (k, j))],
        out_specs=pl.BlockSpec((None, D, PROJ_TN), lambda j, k: (j // 2, 0, j % 2)),
        scratch_shapes=[pltpu.VMEM((D, PROJ_TN), F32)],
        input_output_aliases={0: 0},
        name=name + "_dw",
        compiler_params=pltpu.CompilerParams(dimension_semantics=("parallel", "arbitrary"),
                                             vmem_limit_bytes=VMEM_LIMIT),
    )(gm_grads, h, dproj)

    tm = 512

    def dh_body(d_ref, w_ref, x_ref, r_ref, g_ref, dx_ref, dg_ref, acc_ref):
        i, k = pl.program_id(0), pl.program_id(1)
        p = _dot(d_ref[...], w_ref[...], 1, 1)

        @pl.when(k == 0)
        def _():
            acc_ref[...] = p

        @pl.when(k > 0)
        def _():
            acc_ref[...] += p

        @pl.when(k == NT - 1)
        def _():
            dx, dgt = _rms_bwd_tile(acc_ref[...], x_ref[...], g_ref[...])
            dx_ref[...] = r_ref[...] + dx
            dgp = jnp.sum(dgt, axis=0, keepdims=True)

            @pl.when(i == 0)
            def _():
                dg_ref[...] = dgp

            @pl.when(i > 0)
            def _():
                dg_ref[...] += dgp

    dx, dg = _call(name + "_dh", dh_body, (S // tm, NT),
                   [(dproj, (tm, PROJ_TN), lambda i, k: (i, k)),
                    (Gm, (None, D, PROJ_TN), lambda i, k: (k // 2, 0, k % 2)),
                    (x_in, (tm, D), lambda i, k: (i, 0)), (dres, (tm, D), lambda i, k: (i, 0)),
                    (g, (1, D), lambda i, k: (0, 0))],
                   [((S, D), F32, (tm, D), lambda i, k: (i, 0)), ((1, D), F32, (1, D), lambda i, k: (0, 0))],
                   scratch=[pltpu.VMEM((tm, D), F32)], sem=("arbitrary", "arbitrary"))
    return dx, dg, gm_grads


def _t5_bucket(rel):
    n = N_BUCKETS // 2
    max_exact = n // 2
    ret = jnp.where(rel > 0, n, 0)
    a = jnp.abs(rel)
    af = jnp.maximum(a, 1).astype(F32)
    large = max_exact + (jnp.log(af / max_exact) / math.log(MAX_DISTANCE / max_exact)
                         * (n - max_exact)).astype(jnp.int32)
    large = jnp.minimum(large, n - 1)
    return ret + jnp.where(a < max_exact, a, large)


def _bucket_tables():
    qi = jnp.arange(A_TQ, dtype=jnp.int32)[:, None]
    kj = jnp.arange(A_WIN, dtype=jnp.int32)[None, :]
    rel = kj - HALF_WINDOW - qi
    return jnp.stack([_t5_bucket(rel * d) for d in DILATIONS], axis=0)


def bias_build(rel_bias, buckets):
    def body(tab_ref, bk_ref, o_ref):
        col = pl.program_id(0) * HEADS_PER_GROUP_A + pl.program_id(1)
        bk = bk_ref[...]
        acc = jnp.zeros(bk.shape, F32)
        for b in range(N_BUCKETS):
            acc = jnp.where(bk == b, tab_ref[b, col], acc)
        qi = lax.broadcasted_iota(jnp.int32, bk.shape, 0)
        kj = lax.broadcasted_iota(jnp.int32, bk.shape, 1)
        o_ref[...] = jnp.where(jnp.abs(kj - HALF_WINDOW - qi) <= HALF_WINDOW, acc, NEG_INF)

    return pl.pallas_call(
        body,
        out_shape=jax.ShapeDtypeStruct((3, HEADS_PER_GROUP_A, A_TQ, A_WIN), F32),
        grid=(3, HEADS_PER_GROUP_A),
        in_specs=[pl.BlockSpec(memory_space=pltpu.SMEM),
                  pl.BlockSpec((None, A_TQ, A_WIN), lambda g, h: (g, 0, 0))],
        out_specs=pl.BlockSpec((None, None, A_TQ, A_WIN), lambda g, h: (g, h, 0, 0)),
        name="a_bias_build",
        compiler_params=pltpu.CompilerParams(dimension_semantics=("parallel", "parallel")),
    )(rel_bias, buckets)


def bias_bwd(dbias, buckets):
    def body(d_ref, bk_ref, o_ref):
        bk = bk_ref[...]
        dv = d_ref[...]
        for b in range(N_BUCKETS):
            part = jnp.sum(jnp.where(bk == b, dv, 0.0), axis=1, keepdims=True)
            o_ref[b:b + 1, :] = jnp.broadcast_to(jnp.sum(part, axis=0, keepdims=True), (1, LANES))

    out = pl.pallas_call(
        body,
        out_shape=jax.ShapeDtypeStruct((3, HEADS_PER_GROUP_A, N_BUCKETS, LANES), F32),
        grid=(3, HEADS_PER_GROUP_A),
        in_specs=[pl.BlockSpec((None, None, A_TQ, A_WIN), lambda g, h: (g, h, 0, 0)),
                  pl.BlockSpec((None, A_TQ, A_WIN), lambda g, h: (g, 0, 0))],
        out_specs=pl.BlockSpec((None, None, N_BUCKETS, LANES), lambda g, h: (g, h, 0, 0)),
        name="a_bias_bwd",
        compiler_params=pltpu.CompilerParams(dimension_semantics=("parallel", "parallel")),
    )(dbias, buckets)
    return out[:, :, :, 0].transpose(2, 0, 1).reshape(N_BUCKETS, 3 * HEADS_PER_GROUP_A)


def _a_fill_padded(pad_ref, src_ref, L):
    zeros = jnp.zeros((HALF_WINDOW, LANES), pad_ref.dtype)
    pad_ref[0:HALF_WINDOW, :] = zeros
    pad_ref[HALF_WINDOW + L:2 * HALF_WINDOW + L, :] = zeros
    pad_ref[HALF_WINDOW:HALF_WINDOW + L, :] = src_ref[...]


def _a_key_valid(qb, L):
    kidx = qb * A_TQ - HALF_WINDOW + lax.broadcasted_iota(jnp.int32, (A_TQ, A_WIN), 1)
    return (kidx >= 0) & (kidx < L)


def a_fwd(proj_a, bias_g, g, name):
    S = proj_a.shape[0]
    d = DILATIONS[g]
    L = S // d
    nqb = L // A_TQ
    nb = A_BLOCKS_PER_TOKEN
    pv = proj_a.reshape(L, d * A_QKV_WIDTH)

    def body(q_ref, k_ref, v_ref, b_ref, o_ref, l_ref, kpad, vpad):
        qb = pl.program_id(2)

        @pl.when(qb == 0)
        def _():
            _a_fill_padded(kpad, k_ref, L)
            _a_fill_padded(vpad, v_ref, L)

        start = pl.multiple_of(qb * A_TQ, A_TQ)
        kw = kpad[pl.ds(start, A_WIN), :]
        vw = vpad[pl.ds(start, A_WIN), :]
        q = q_ref[...]
        lane = lax.broadcasted_iota(jnp.int32, (A_TQ, LANES), 1)
        valid = _a_key_valid(qb, L)
        outs, lses = [], []
        for h in range(2):
            qh = jnp.where((lane >= HEAD_DIM_A * h) & (lane < HEAD_DIM_A * (h + 1)), q, jnp.zeros_like(q))
            s = _dot(qh, kw, 1, 1) * (HEAD_DIM_A ** -0.5) + b_ref[h]
            s = jnp.where(valid, s, NEG_INF)
            m = jnp.max(s, axis=-1, keepdims=True)
            e = jnp.exp(s - m)
            l = jnp.sum(e, axis=-1, keepdims=True)
            outs.append(_dot(e.astype(BF16), vw) / l)
            lses.append(m + jnp.log(l))
        o_ref[...] = jnp.where(lane < HEAD_DIM_A, outs[0], outs[1])
        l_ref[...] = jnp.where(lane < HEAD_DIM_A, lses[0], lses[1])

    col = lambda which: (lambda r, hp, qb: r * nb + (which * 3 + g) * 4 + hp)
    out_spec = ((L, d * GROUP_WIDTH_A), F32, (A_TQ, LANES), lambda r, hp, qb: (qb, r * 4 + hp))
    o, lse = _call(name, body, (d, 4, nqb),
                   [(pv, (A_TQ, LANES), lambda r, hp, qb: (qb, col(0)(r, hp, qb))),
                    (pv, (L, LANES), lambda r, hp, qb: (0, col(1)(r, hp, qb))),
                    (pv, (L, LANES), lambda r, hp, qb: (0, col(2)(r, hp, qb))),
                    (bias_g, (2, A_TQ, A_WIN), lambda r, hp, qb: (hp, 0, 0))],
                   [out_spec, out_spec],
                   scratch=[pltpu.VMEM((L + 2 * HALF_WINDOW, LANES), BF16)] * 2,
                   sem=("parallel", "parallel", "arbitrary"))
    return o.reshape(S, GROUP_WIDTH_A), lse.reshape(S, GROUP_WIDTH_A)


def a_combine(outs, lses, name):
    S, W = outs[0].shape
    tr = 512

    def body(o0, o1, o2, l0, l1, l2, oa_ref, lt_ref):
        a, b, c = l0[...], l1[...], l2[...]
        m = jnp.maximum(jnp.maximum(a, b), c)
        ea, eb, ec = jnp.exp(a - m), jnp.exp(b - m), jnp.exp(c - m)
        z = ea + eb + ec
        oa_ref[...] = ((ea * o0[...] + eb * o1[...] + ec * o2[...]) / z).astype(BF16)
        lt_ref[...] = m + jnp.log(z)

    spec = ((tr, W), lambda i: (i, 0))
    return _call(name, body, (S // tr,), [(a,) + spec for a in (*outs, *lses)],
                 [((S, W), BF16) + spec, ((S, W), F32) + spec], sem=("parallel",))


def a_bwd(proj_a, bias_g, do_a, o_a, lse_tot, g, name):
    S = proj_a.shape[0]
    d = DILATIONS[g]
    L = S // d
    nqb = L // A_TQ
    nb = A_BLOCKS_PER_TOKEN
    pv = proj_a.reshape(L, d * A_QKV_WIDTH)
    view = lambda a: a.reshape(L, d * GROUP_WIDTH_A)
    scale = HEAD_DIM_A ** -0.5

    def body(q_ref, k_ref, v_ref, b_ref, do_ref, o_ref, l_ref, dq_ref, dk_ref, dv_ref, db_ref,
             kpad, vpad, dkacc, dvacc):
        r, qb = pl.program_id(1), pl.program_id(2)

        @pl.when(qb == 0)
        def _():
            _a_fill_padded(kpad, k_ref, L)
            _a_fill_padded(vpad, v_ref, L)
            dkacc[...] = jnp.zeros(dkacc.shape, F32)
            dvacc[...] = jnp.zeros(dvacc.shape, F32)

        @pl.when((qb == 0) & (r == 0))
        def _():
            db_ref[...] = jnp.zeros(db_ref.shape, F32)

        start = pl.multiple_of(qb * A_TQ, A_TQ)
        kw = kpad[pl.ds(start, A_WIN), :]
        vw = vpad[pl.ds(start, A_WIN), :]
        q = q_ref[...]
        do = do_ref[...]
        ov = o_ref[...].astype(F32)
        lt = l_ref[...]
        lane = lax.broadcasted_iota(jnp.int32, (A_TQ, LANES), 1)
        valid = _a_key_valid(qb, L)
        dqs = []
        dk_win = jnp.zeros((A_WIN, LANES), F32)
        dv_win = jnp.zeros((A_WIN, LANES), F32)
        for h in range(2):
            mh = (lane >= HEAD_DIM_A * h) & (lane < HEAD_DIM_A * (h + 1))
            qh = jnp.where(mh, q, jnp.zeros_like(q))
            doh = jnp.where(mh, do, 0.0)
            s = _dot(qh, kw, 1, 1) * scale + b_ref[h]
            s = jnp.where(valid, s, NEG_INF)
            p = jnp.exp(s - lt[:, HEAD_DIM_A * h:HEAD_DIM_A * h + 1])
            t = jnp.sum(doh * ov, axis=-1, keepdims=True)
            dob = doh.astype(BF16)
            ds = p * (_dot(dob, vw, 1, 1) - t)
            db_ref[h] += ds
            dsb = (ds * scale).astype(BF16)
            dqs.append(_dot(dsb, kw))
            dk_win = dk_win + _dot(dsb, qh, 0, 0)
            dv_win = dv_win + _dot(p.astype(BF16), dob, 0, 0)
        dq_ref[...] = jnp.where(lane < HEAD_DIM_A, dqs[0], dqs[1]).astype(BF16)
        dkacc[pl.ds(start, A_WIN), :] += dk_win
        dvacc[pl.ds(start, A_WIN), :] += dv_win

        @pl.when(qb == nqb - 1)
        def _():
            dk_ref[...] = dkacc[HALF_WINDOW:HALF_WINDOW + L, :].astype(BF16)
            dv_ref[...] = dvacc[HALF_WINDOW:HALF_WINDOW + L, :].astype(BF16)

    col = lambda which: (lambda hp, r, qb: r * nb + (which * 3 + g) * 4 + hp)
    tile = ((A_TQ, LANES), lambda hp, r, qb: (qb, r * 4 + hp))
    slab = ((L, LANES), lambda hp, r, qb: (0, r * 4 + hp))
    oshape = (L, d * GROUP_WIDTH_A)
    dq, dk, dv, db = _call(
        name, body, (4, d, nqb),
        [(pv, (A_TQ, LANES), lambda hp, r, qb: (qb, col(0)(hp, r, qb))),
         (pv, (L, LANES), lambda hp, r, qb: (0, col(1)(hp, r, qb))),
         (pv, (L, LANES), lambda hp, r, qb: (0, col(2)(hp, r, qb))),
         (bias_g, (2, A_TQ, A_WIN), lambda hp, r, qb: (hp, 0, 0)),
         (view(do_a),) + tile, (view(o_a),) + tile, (view(lse_tot),) + tile],
        [(oshape, BF16) + tile, (oshape, BF16) + slab, (oshape, BF16) + slab,
         ((HEADS_PER_GROUP_A, A_TQ, A_WIN), F32, (2, A_TQ, A_WIN), lambda hp, r, qb: (hp, 0, 0))],
        scratch=[pltpu.VMEM((L + 2 * HALF_WINDOW, LANES), BF16)] * 2
        + [pltpu.VMEM((L + 2 * HALF_WINDOW, LANES), F32)] * 2,
        sem=("parallel", "arbitrary", "arbitrary"))
    return dq.reshape(S, GROUP_WIDTH_A), dk.reshape(S, GROUP_WIDTH_A), dv.reshape(S, GROUP_WIDTH_A), db


def _rope_tables(S):
    rows = S // GRID_W
    row = jnp.repeat(jnp.arange(rows, dtype=F32), GRID_W)
    col = jnp.tile(jnp.arange(GRID_W, dtype=F32), rows)
    n_freq = HEAD_DIM_B // 4
    freq = ROPE_THETA ** (-jnp.arange(n_freq, dtype=F32) / n_freq)
    ang = jnp.concatenate([row[:, None] * freq, col[:, None] * freq], axis=-1)
    cos, sin = jnp.cos(ang), jnp.sin(ang)
    return jnp.repeat(cos, 2, axis=-1), jnp.stack([-sin, sin], axis=-1).reshape(S, HEAD_DIM_B)


def _swap_pairs(y):
    lane = lax.broadcasted_iota(jnp.int32, y.shape, 1)
    return jnp.where(lane % 2 == 0, pltpu.roll(y, LANES - 1, 1), pltpu.roll(y, 1, 1))


def qkv_prep(proj_b, gains, cos_t, sin_t, name):
    S = proj_b.shape[0]
    ts = 512
    n_rot = N_HEADS_B + N_KV_B

    def body(x_ref, g_ref, c_ref, s_ref, o_ref):
        hb = pl.program_id(0)

        @pl.when(hb < n_rot)
        def _():
            xv = x_ref[...]
            r = lax.rsqrt(jnp.mean(xv * xv, axis=-1, keepdims=True) + EPS)
            yv = xv * r * g_ref[...]
            o_ref[...] = (yv * c_ref[...] + _swap_pairs(yv) * s_ref[...]).astype(BF16)

        @pl.when(hb >= n_rot)
        def _():
            o_ref[...] = x_ref[...].astype(BF16)

    nh = n_rot + N_KV_B
    return _call(name, body, (nh, S // ts),
                 [(proj_b, (ts, LANES), lambda hb, i: (i, hb)), (gains, (1, LANES), lambda hb, i: (0, hb)),
                  (cos_t, (ts, LANES), lambda hb, i: (i, 0)), (sin_t, (ts, LANES), lambda hb, i: (i, 0))],
                 [((S, nh * LANES), BF16, (ts, LANES), lambda hb, i: (i, hb))],
                 sem=("parallel", "parallel"))[0]


def qk_prep_bwd(dr, proj_b, col0, gain, cos_t, sin_t, name):
    S, W = dr.shape
    H = W // LANES
    ts = 512

    def body(d_ref, x_ref, g_ref, c_ref, s_ref, dx_ref, dg_ref):
        hb, i = pl.program_id(0), pl.program_id(1)
        dout = d_ref[...]
        dy = dout * c_ref[...] + _swap_pairs(dout * s_ref[...])
        dx, dgt = _rms_bwd_tile(dy, x_ref[...], g_ref[...])
        dx_ref[...] = dx.astype(BF16)
        dgp = jnp.sum(dgt, axis=0, keepdims=True)

        @pl.when((hb == 0) & (i == 0))
        def _():
            dg_ref[...] = dgp

        @pl.when((hb > 0) | (i > 0))
        def _():
            dg_ref[...] += dgp

    return _call(name, body, (H, S // ts),
                 [(dr, (ts, LANES), lambda hb, i: (i, hb)), (proj_b, (ts, LANES), lambda hb, i: (i, col0 + hb)),
                  (gain, (1, LANES), lambda hb, i: (0, 0)),
                  (cos_t, (ts, LANES), lambda hb, i: (i, 0)), (sin_t, (ts, LANES), lambda hb, i: (i, 0))],
                 [((S, W), BF16, (ts, LANES), lambda hb, i: (i, hb)),
                  ((1, LANES), F32, (1, LANES), lambda hb, i: (0, 0))],
                 sem=("arbitrary", "arbitrary"))


def flash_fwd(qkv, name):
    S = qkv.shape[0]
    tq = B_TQ
    scale = HEAD_DIM_B ** -0.5

    def body(q_ref, k_ref, v_ref, o_ref):
        s = _dot(q_ref[...], k_ref[...], 1, 1) * scale
        m = jnp.max(s, axis=-1, keepdims=True)
        e = jnp.exp(s - m)
        l = jnp.sum(e, axis=-1, keepdims=True)
        o_ref[...] = (_dot(e.astype(BF16), v_ref[...]) / l).astype(BF16)

    return _call(name, body, (N_KV_B, GQA_GROUP_B, S // tq),
                 [(qkv, (tq, LANES), lambda g, h, i: (i, g * GQA_GROUP_B + h)),
                  (qkv, (S, LANES), lambda g, h, i: (0, N_HEADS_B + g)),
                  (qkv, (S, LANES), lambda g, h, i: (0, N_HEADS_B + N_KV_B + g))],
                 [((S, N_HEADS_B * LANES), BF16, (tq, LANES), lambda g, h, i: (i, g * GQA_GROUP_B + h))],
                 sem=("parallel", "parallel", "parallel"))[0]


def flash_bwd(qkv, k_t, do_b, o_b, name):
    S = qkv.shape[0]
    tq = B_TQ
    nq = S // tq
    scale = HEAD_DIM_B ** -0.5

    def body(q_ref, k_ref, v_ref, kt_ref, do_ref, o_ref, dq_ref, dk_ref, dv_ref, dkacc, dvacc):
        h, i = pl.program_id(1), pl.program_id(2)

        @pl.when((h == 0) & (i == 0))
        def _():
            dkacc[...] = jnp.zeros(dkacc.shape, F32)
            dvacc[...] = jnp.zeros(dvacc.shape, F32)

        q, k, v = q_ref[...], k_ref[...], v_ref[...]
        do = do_ref[...]
        st = _dot(k, q, 1, 1) * scale
        m = jnp.max(st, axis=0, keepdims=True)
        e = jnp.exp(st - m)
        pt = e / jnp.sum(e, axis=0, keepdims=True)
        prod = do * o_ref[...].astype(F32)
        hi = prod.astype(BF16)
        lo = (prod - hi.astype(F32)).astype(BF16)
        ones = jnp.ones((8, LANES), BF16)
        t = (_dot(ones, hi, 1, 1) + _dot(ones, lo, 1, 1))[0:1, :]
        dob = do.astype(BF16)
        dst = pt * (_dot(v, dob, 1, 1) - t) * scale
        ptb, dsb = pt.astype(BF16), dst.astype(BF16)
        dvacc[...] += _dot(ptb, dob)
        dkacc[...] += _dot(dsb, q)
        dq_ref[...] = _dot(kt_ref[...], dsb).T

        @pl.when((h == GQA_GROUP_B - 1) & (i == nq - 1))
        def _():
            dk_ref[...] = dkacc[...]
            dv_ref[...] = dvacc[...].astype(BF16)

    head = lambda g, h, i: (i, g * GQA_GROUP_B + h)
    return _call(name, body, (N_KV_B, GQA_GROUP_B, nq),
                 [(qkv, (tq, LANES), head),
                  (qkv, (S, LANES), lambda g, h, i: (0, N_HEADS_B + g)),
                  (qkv, (S, LANES), lambda g, h, i: (0, N_HEADS_B + N_KV_B + g)),
                  (k_t, (LANES, S), lambda g, h, i: (g, 0)),
                  (do_b, (tq, LANES), head), (o_b, (tq, LANES), head)],
                 [((S, N_HEADS_B * LANES), F32, (tq, LANES), head),
                  ((S, N_KV_B * LANES), F32, (S, LANES), lambda g, h, i: (0, g)),
                  ((S, N_KV_B * LANES), BF16, (S, LANES), lambda g, h, i: (0, g))],
                 scratch=[pltpu.VMEM((S, LANES), F32)] * 2,
                 sem=("parallel", "arbitrary", "arbitrary"))


MERGE_TN = 512


def _mix_rows_spec(Gm, row0, n_slots, slot_map, cols=None, col_map=None):
    C = Gm.shape[2] if cols is None else cols
    cm = (lambda *idx: 0) if col_map is None else col_map
    return (Gm, (n_slots, LANES, C), lambda *idx: (slot_map(*idx), row0 // LANES, cm(*idx)))


def merge_fwd(o_a, o_b, w_a, Gm, proj_b, b_gate, name):
    S = o_a.shape[0]
    D = w_a.shape[1]
    tm, tn = 512, MERGE_TN
    ga0, gb0 = PB_GATE_A // tn, PB_GATE_B // tn

    def body(oa_ref, ob_ref, wa_ref, wb_ref, pa_ref, pb_ref, ba_ref, bb_ref, m_ref, ya_ref, yb_ref):
        ya = _dot(oa_ref[...], wa_ref[...])
        yb = _dot(ob_ref[...], wb_ref[...].reshape(N_DEV * LANES, tn))
        ga = _sigmoid(pa_ref[...] + ba_ref[...])
        gb = _sigmoid(pb_ref[...] + bb_ref[...])
        m_ref[...] = (ga * ya + gb * yb).astype(BF16)
        ya_ref[...] = ya.astype(BF16)
        yb_ref[...] = yb.astype(BF16)

    out = ((S, D), BF16, (tm, tn), lambda j, i: (i, j))
    return _call(name, body, (D // tn, S // tm),
                 [(o_a, (tm, o_a.shape[1]), lambda j, i: (i, 0)), (o_b, (tm, o_b.shape[1]), lambda j, i: (i, 0)),
                  (w_a, (w_a.shape[0], tn), lambda j, i: (0, j)),
                  _mix_rows_spec(Gm, MIX_WB, N_DEV, lambda j, i: 0, cols=tn, col_map=lambda j, i: j),
                  (proj_b, (tm, tn), lambda j, i: (i, ga0 + j)), (proj_b, (tm, tn), lambda j, i: (i, gb0 + j)),
                  (b_gate, (1, tn), lambda j, i: (0, j)), (b_gate, (1, tn), lambda j, i: (0, D // tn + j))],
                 [out, out, out], sem=("parallel", "parallel"))


def out_proj(merged, Gm, x, name):
    S, D = x.shape
    tm, tn = 512, MERGE_TN

    def body(m_ref, w_ref, x_ref, o_ref):
        o_ref[...] = x_ref[...] + _dot(m_ref[...], w_ref[...].reshape(N_DEV * LANES, tn))

    return _call(name, body, (D // tn, S // tm),
                 [(merged, (tm, D), lambda j, i: (i, 0)),
                  _mix_rows_spec(Gm, MIX_WOUT, N_DEV, lambda j, i: 0, cols=tn, col_map=lambda j, i: j),
                  (x, (tm, tn), lambda j, i: (i, j))],
                 [((S, D), F32, (tm, tn), lambda j, i: (i, j))], sem=("parallel", "parallel"))[0]


def merge_bwd(dx2, Gm, ya, yb, proj_b, b_gate, name):
    S, D = dx2.shape
    tm, tn = 512, MERGE_TN
    nn = D // tn
    ga0, gb0 = PB_GATE_A // tn, PB_GATE_B // tn

    def body(d_ref, w_ref, ya_ref, yb_ref, pa_ref, pb_ref, ba_ref, bb_ref, dya_ref, dyb_ref, dg_ref, dbg_ref):
        i = pl.program_id(1)
        dm = _dot(d_ref[...].astype(BF16), w_ref[...].reshape(tn, D), 1, 1)
        ga = _sigmoid(pa_ref[...] + ba_ref[...])
        gb = _sigmoid(pb_ref[...] + bb_ref[...])
        dya_ref[...] = (dm * ga).astype(BF16)
        dyb_ref[...] = (dm * gb).astype(BF16)
        dpa = dm * ya_ref[...].astype(F32) * ga * (1.0 - ga)
        dpb = dm * yb_ref[...].astype(F32) * gb * (1.0 - gb)
        dg_ref[0] = dpa.astype(BF16)
        dg_ref[1] = dpb.astype(BF16)
        sa = jnp.sum(dpa, axis=0, keepdims=True)
        sb = jnp.sum(dpb, axis=0, keepdims=True)

        @pl.when(i == 0)
        def _():
            dbg_ref[0] = sa
            dbg_ref[1] = sb

        @pl.when(i > 0)
        def _():
            dbg_ref[0] += sa
            dbg_ref[1] += sb

    tile = ((tm, tn), lambda j, i: (i, j))
    dya, dyb, dgate, dbg = _call(
        name, body, (nn, S // tm),
        [(dx2, (tm, D), lambda j, i: (i, 0)),
         _mix_rows_spec(Gm, MIX_WOUT, tn // LANES, lambda j, i: j),
         (ya,) + tile, (yb,) + tile,
         (proj_b, (tm, tn), lambda j, i: (i, ga0 + j)), (proj_b, (tm, tn), lambda j, i: (i, gb0 + j)),
         (b_gate, (1, tn), lambda j, i: (0, j)), (b_gate, (1, tn), lambda j, i: (0, nn + j))],
        [((S, D), BF16) + tile, ((S, D), BF16) + tile,
         ((2, S, D), BF16, (2, tm, tn), lambda j, i: (0, i, j)),
         ((2, 1, D), F32, (2, 1, tn), lambda j, i: (0, 0, j))],
        sem=("parallel", "arbitrary"))
    return dya, dyb, dgate, dbg


def matmul_nt(a, b_spec_fn, N, name, tn=512):
    S, K = a.shape
    tm = 512

    def body(a_ref, b_ref, o_ref):
        b = b_ref[...]
        o_ref[...] = _dot(a_ref[...], b.reshape(-1, b.shape[-1]), 1, 1)

    return _call(name, body, (N // tn, S // tm),
                 [(a, (tm, K), lambda j, i: (i, 0)), b_spec_fn(lambda j, i: j)],
                 [((S, N), F32, (tm, tn), lambda j, i: (i, j))], sem=("parallel", "parallel"))[0]


def weight_grad_rows(a, b, grads, row0, name):
    S, M = a.shape
    N = b.shape[1]
    tmm = 512
    tk = 512
    nk = S // tk

    def body(g_ref, a_ref, b_ref, o_ref, acc_ref):
        k = pl.program_id(1)
        p = _dot(a_ref[...], b_ref[...].astype(BF16), 0, 0)

        @pl.when(k == 0)
        def _():
            acc_ref[...] = p

        @pl.when(k > 0)
        def _():
            acc_ref[...] += p

        @pl.when(k == nk - 1)
        def _():
            o_ref[...] = acc_ref[...].astype(BF16).reshape(tmm // LANES, LANES, N)

    return pl.pallas_call(
        body,
        out_shape=jax.ShapeDtypeStruct(grads.shape, BF16),
        grid=(M // tmm, nk),
        in_specs=[pl.BlockSpec(memory_space=pl.ANY),
                  pl.BlockSpec((tk, tmm), lambda j, k: (k, j)),
                  pl.BlockSpec((tk, N), lambda j, k: (k, 0))],
        out_specs=pl.BlockSpec((tmm // LANES, LANES, N), lambda j, k: (j, row0 // LANES, 0)),
        scratch_shapes=[pltpu.VMEM((tmm, N), F32)],
        input_output_aliases={0: 0},
        name=name,
        compiler_params=pltpu.CompilerParams(dimension_semantics=("parallel", "arbitrary"),
                                             vmem_limit_bytes=VMEM_LIMIT),
    )(grads, a, b)


def weight_grad_plain(a, b, name):
    S, M = a.shape
    N = b.shape[1]
    tk = 512
    nk = S // tk

    def body(a_ref, b_ref, o_ref, acc_ref):
        k = pl.program_id(0)
        p = _dot(a_ref[...], b_ref[...], 0, 0)

        @pl.when(k == 0)
        def _():
            acc_ref[...] = p

        @pl.when(k > 0)
        def _():
            acc_ref[...] += p

        @pl.when(k == nk - 1)
        def _():
            o_ref[...] = acc_ref[...].astype(BF16)

    return _call(name, body, (nk,),
                 [(a, (tk, M), lambda k: (k, 0)), (b, (tk, N), lambda k: (k, 0))],
                 [((M, N), BF16, (M, N), lambda k: (0, 0))],
                 scratch=[pltpu.VMEM((M, N), F32)], sem=("arbitrary",))[0]


def local_step(x, tgt, p, G1, Gm, G2):
    S, D = x.shape
    w_a = Gm[:, MIX_WA:MIX_ROWS, :].reshape(N_DEV, GROUP_WIDTH_A, LANES).transpose(1, 0, 2).reshape(GROUP_WIDTH_A, D)
    buckets = _bucket_tables()
    cos_t, sin_t = _rope_tables(S)
    gains = jnp.concatenate([jnp.tile(p["q_norm"], (1, N_HEADS_B)), jnp.tile(p["k_norm"], (1, N_KV_B)),
                             jnp.ones((1, N_KV_B * LANES), F32)], axis=1)

    n1 = rms_fwd(x, p["ffn1_norm"], "ffn1_norm")
    ab1 = ffn_up(n1, G1, "ffn1_up")
    x1 = ffn_down(ab1, G1, x, "ffn1_down")

    hm = rms_fwd(x1, p["mix_norm"], "mix_norm")
    n_a = A_QKV_WIDTH // PROJ_TN
    proj_a = in_proj(hm, Gm, 0, n_a, BF16, "in_proj_a")
    proj_b = in_proj(hm, Gm, n_a, PB_WIDTH // PROJ_TN, F32, "in_proj_b")

    bias = bias_build(p["rel_bias"], buckets)
    outs, lses = [], []
    for g in range(3):
        o, l = a_fwd(proj_a, bias[g], g, "a_fwd_%d" % g)
        outs.append(o)
        lses.append(l)
    o_a, lse_tot = a_combine(outs, lses, "a_combine")

    qkv = qkv_prep(proj_b, gains, cos_t, sin_t, "qkv_prep")
    o_b = flash_fwd(qkv, "flash_fwd")

    merged, ya, yb = merge_fwd(o_a, o_b, w_a, Gm, proj_b, p["b_gate"], "merge_fwd")
    x2 = out_proj(merged, Gm, x1, "out_proj")

    n2 = rms_fwd(x2, p["ffn2_norm"], "ffn2_norm")
    ab2 = ffn_up(n2, G2, "ffn2_up")
    x3 = ffn_down(ab2, G2, x2, "ffn2_down")

    loss, dx3, d_final = final_loss(x3, tgt, p["final_norm"], "final_loss")

    dx2, d_ffn2_norm, gw2 = ffn_bwd(dx3, ab2, n2, G2, x2, p["ffn2_norm"], "ffn2_bwd")

    dya, dyb, dgate, dbg = merge_bwd(dx2, Gm, ya, yb, proj_b, p["b_gate"], "merge_bwd")
    gm_grads = jnp.zeros(Gm.shape, BF16)
    gm_grads = weight_grad_rows(merged, dx2, gm_grads, MIX_WOUT, "dw_out")
    gm_grads = weight_grad_rows(o_b, dyb, gm_grads, MIX_WB, "dw_branch_b")
    dw_a = weight_grad_plain(o_a, dya, "dw_branch_a")
    do_a = matmul_nt(dya, lambda jm: (w_a, (MERGE_TN, D), lambda j, i: (jm(j, i), 0)), GROUP_WIDTH_A, "do_a")
    do_b = matmul_nt(dyb, lambda jm: _mix_rows_spec(Gm, MIX_WB, MERGE_TN // LANES, jm), N_HEADS_B * LANES, "do_b")

    k_t = qkv[:, N_HEADS_B * LANES:(N_HEADS_B + N_KV_B) * LANES].T
    dq_r, dk_r, dv_b = flash_bwd(qkv, k_t, do_b, o_b, "flash_bwd")
    dq_b, d_q_norm = qk_prep_bwd(dq_r, proj_b, 0, p["q_norm"], cos_t, sin_t, "q_prep_bwd")
    dk_b, d_k_norm = qk_prep_bwd(dk_r, proj_b, N_HEADS_B, p["k_norm"], cos_t, sin_t, "k_prep_bwd")

    dqs, dks, dvs, dbs = [], [], [], []
    for g in range(3):
        dq, dk, dv, db = a_bwd(proj_a, bias[g], do_a, o_a, lse_tot, g, "a_bwd_%d" % g)
        dqs.append(dq)
        dks.append(dk)
        dvs.append(dv)
        dbs.append(db)
    d_rel_bias = bias_bwd(jnp.stack(dbs, axis=0), buckets)

    dproj = jnp.concatenate(dqs + dks + dvs + [dq_b, dk_b, dv_b, dgate[0], dgate[1]], axis=1)
    dx1, d_mix_norm, gm_grads = in_proj_bwd(dproj, hm, Gm, x1, p["mix_norm"], dx2, gm_grads, "in_proj_bwd")
    dw_a_sh = dw_a.reshape(GROUP_WIDTH_A, N_DEV, LANES).transpose(1, 0, 2).reshape(N_DEV, MIX_ROWS - MIX_WA, D)
    gm_grads = lax.dynamic_update_slice(gm_grads, dw_a_sh, (0, MIX_WA, 0))

    dx0, d_ffn1_norm, gw1 = ffn_bwd(dx1, ab1, n1, G1, x, p["ffn1_norm"], "ffn1_bwd")

    small = dict(ffn1_norm=d_ffn1_norm, mix_norm=d_mix_norm, b_gate=dbg.reshape(1, 2 * D),
                 q_norm=d_q_norm, k_norm=d_k_norm, rel_bias=d_rel_bias, ffn2_norm=d_ffn2_norm,
                 final_norm=d_final)
    return loss, dx0, small, gw1, gm_grads, gw2


def _pack_small(t, loss_row):
    row6 = jnp.concatenate([t["q_norm"].reshape(1, -1), t["k_norm"].reshape(1, -1), t["rel_bias"].reshape(1, -1)], axis=1)
    return jnp.concatenate([t["ffn1_norm"].reshape(1, -1), t["mix_norm"].reshape(1, -1), t["b_gate"].reshape(2, -1),
                            t["ffn2_norm"].reshape(1, -1), t["final_norm"].reshape(1, -1), row6, loss_row], axis=0)


def _unpack_small(a, shapes):
    return dict(ffn1_norm=a[0:1].reshape(shapes["ffn1_norm"]), mix_norm=a[1:2].reshape(shapes["mix_norm"]),
                b_gate=a[2:4].reshape(shapes["b_gate"]), ffn2_norm=a[4:5].reshape(shapes["ffn2_norm"]),
                final_norm=a[5].reshape(shapes["final_norm"]), q_norm=a[6:7, 0:128].reshape(shapes["q_norm"]),
                k_norm=a[6:7, 128:256].reshape(shapes["k_norm"]), rel_bias=a[6, 256:1024].reshape(shapes["rel_bias"]))


SMALL = ("ffn1_norm", "mix_norm", "b_gate", "q_norm", "k_norm", "rel_bias", "ffn2_norm", "final_norm")
ORDER = ("ffn1_norm", "ffn1_w1", "ffn1_w3", "ffn1_w2", "mix_norm", "w_in", "b_gate", "q_norm", "k_norm", "rel_bias",
         "w_branch_a", "w_branch_b", "w_out", "ffn2_norm", "ffn2_w1", "ffn2_w3", "ffn2_w2", "final_norm")


def kernel(x, ffn1_norm, ffn1_w1, ffn1_w3, ffn1_w2, mix_norm, w_in, b_gate, q_norm, k_norm, rel_bias, w_branch_a, w_branch_b, w_out, ffn2_norm, ffn2_w1, ffn2_w3, ffn2_w2, final_norm, loss_target, m_ffn1_norm, m_ffn1_w1, m_ffn1_w3, m_ffn1_w2, m_mix_norm, m_w_in, m_b_gate, m_q_norm, m_k_norm, m_rel_bias, m_w_branch_a, m_w_branch_b, m_w_out, m_ffn2_norm, m_ffn2_w1, m_ffn2_w3, m_ffn2_w2, m_final_norm, v_ffn1_norm, v_ffn1_w1, v_ffn1_w3, v_ffn1_w2, v_mix_norm, v_w_in, v_b_gate, v_q_norm, v_k_norm, v_rel_bias, v_w_branch_a, v_w_branch_b, v_w_out, v_ffn2_norm, v_ffn2_w1, v_ffn2_w3, v_ffn2_w2, v_final_norm):
    args = dict(locals())
    w = {n: args[n] for n in ORDER}
    m = {n: args["m_" + n] for n in ORDER}
    v = {n: args["v_" + n] for n in ORDER}
    D = x.shape[2]

    def ffn_group(w1, w3, w2):
        return jnp.concatenate([w1[0].T, w3[0].T, w2[0]], axis=0).astype(BF16)

    g1 = ffn_group(ffn1_w1, ffn1_w3, ffn1_w2)
    g2 = ffn_group(ffn2_w1, ffn2_w3, ffn2_w2)
    gm = jnp.concatenate([w_in[0], w_branch_b[0], w_out[0], w_branch_a[0].reshape(MIX_ROWS - MIX_WA, D)],
                         axis=0).astype(BF16)
    G1, Gm, G2 = all_gather_groups([g1, gm, g2])

    small_p = dict(ffn1_norm=ffn1_norm, mix_norm=mix_norm, b_gate=b_gate, q_norm=q_norm, k_norm=k_norm,
                   rel_bias=rel_bias, ffn2_norm=ffn2_norm, final_norm=final_norm.reshape(1, D))
    loss_p, grad_x, small_g, gw1, gwm, gw2 = local_step(x[0], loss_target[0], small_p, G1, Gm, G2)

    mine, theirs = reduce_scatter_pair([gw1, gwm, gw2])
    parts = [pair_add(a, b, "pair_add_%d" % i) for i, (a, b) in enumerate(zip(mine, theirs))]
    loss_row = jnp.pad(loss_p, ((0, 0), (0, D - LANES)))
    recv, smalls = reduce_scatter_chips(parts, _pack_small(small_g, loss_row))
    r1, rm, r2 = recv

    grads = {}
    for tag, r in (("ffn1", r1), ("ffn2", r2)):
        grads[tag + "_w1"] = sum_slots(r, 0, FFN_SHARD, FFN_SHARD, tag + "_w1_sum").T[None]
        grads[tag + "_w3"] = sum_slots(r, FFN_SHARD, FFN_SHARD, FFN_SHARD, tag + "_w3_sum").T[None]
        grads[tag + "_w2"] = sum_slots(r, 2 * FFN_SHARD, FFN_SHARD, FFN_SHARD, tag + "_w2_sum")[None]
    grads["w_in"] = sum_slots(rm, MIX_WIN, MIX_WB - MIX_WIN, LANES, "w_in_sum")[None]
    grads["w_branch_b"] = sum_slots(rm, MIX_WB, LANES, LANES, "w_branch_b_sum")[None]
    grads["w_out"] = sum_slots(rm, MIX_WOUT, LANES, LANES, "w_out_sum")[None]
    grads["w_branch_a"] = sum_slots(rm, MIX_WA, MIX_ROWS - MIX_WA, MIX_ROWS - MIX_WA,
                                    "w_branch_a_sum").reshape(w_branch_a.shape)
    small_sum = sum_slots(smalls, 0, N_DEV, N_DEV, "small_sum")
    small_shapes = {n: w[n].shape for n in SMALL}
    grads.update(_unpack_small(small_sum, small_shapes))
    loss = small_sum[7, 0]

    delta, new_m, new_v = {}, {}, {}
    for n in ORDER:
        if n in SMALL:
            continue
        shp = w[n].shape
        two_d = lambda a: a.reshape(shp[-2], shp[-1])
        d_, m_, v_ = adamw(two_d(w[n]), two_d(grads[n]), two_d(m[n]), two_d(v[n]), "adamw_" + n)
        delta[n], new_m[n], new_v[n] = d_.reshape(shp), m_.reshape(shp), v_.reshape(shp)
    zero_row = jnp.zeros((1, D), F32)
    pack = lambda t: _pack_small({n: t[n] for n in SMALL}, zero_row)
    d_, m_, v_ = adamw(pack(w), small_sum, pack(m), pack(v), "adamw_small")
    for src, dst in ((d_, delta), (m_, new_m), (v_, new_v)):
        dst.update(_unpack_small(src, small_shapes))

    return (loss, grad_x[None], *[grads[n] for n in ORDER], *[delta[n] for n in ORDER],
            *[new_m[n] for n in ORDER], *[new_v[n] for n in ORDER])
```

```python
import math

import jax
import jax.numpy as jnp
from jax import lax
from jax.experimental import pallas as pl
from jax.experimental.pallas import tpu as pltpu

F32 = jnp.float32
BF16 = jnp.bfloat16
MESH = pl.DeviceIdType.MESH

V7X_VMEM_BYTES = 64 * 1024 * 1024
VMEM_LIMIT = V7X_VMEM_BYTES - 8 * 1024 * 1024
LANES = 128

N_DEV = 8
EPS = 1e-6
NEG_INF = -1e30

DILATIONS = (1, 4, 16)
HALF_WINDOW = 64
HEAD_DIM_A = 64
HEADS_PER_GROUP_A = 8
GROUP_WIDTH_A = 512
A_QKV_WIDTH = 4608
A_BLOCKS_PER_TOKEN = A_QKV_WIDTH // LANES
A_TQ = 128
A_WIN = A_TQ + 2 * HALF_WINDOW
HEAD_DIM_B = 128
N_HEADS_B = 8
N_KV_B = 2
GQA_GROUP_B = 4
GRID_W = 64
ROPE_THETA = 10000.0
B_TQ_FWD = 256
B_TQ_BWD = 512
N_BUCKETS = 32
MAX_DISTANCE = 1024
PB_WIDTH = 3584
PB_GATE_A = 1536
PB_GATE_B = 2560

ADAM_LR = 0.001
ADAM_B1 = 0.9
ADAM_B2 = 0.999
ADAM_EPS = 1e-08
ADAM_WD = 0.01
ADAM_STEP = 10

FFN_SHARD = 352
MIX_WIN, MIX_WB, MIX_WOUT, MIX_WA = 0, 1024, 1152, 1280
MIX_ROWS = 1344


def _dot(a, b, ca=1, cb=0):
    return lax.dot_general(a, b, (((ca,), (cb,)), ((), ())), preferred_element_type=F32)


def _call(name, body, grid, ins, outs, scratch=(), sem=None, aliases=None):
    res = pl.pallas_call(
        body,
        out_shape=[jax.ShapeDtypeStruct(s, d) for (s, d, _, _) in outs],
        grid=grid,
        in_specs=[pl.BlockSpec(bs, im) for (_, bs, im) in ins],
        out_specs=[pl.BlockSpec(bs, im) for (_, _, bs, im) in outs],
        scratch_shapes=list(scratch),
        name=name,
        input_output_aliases=aliases or {},
        compiler_params=pltpu.CompilerParams(dimension_semantics=sem, vmem_limit_bytes=VMEM_LIMIT),
    )(*[a for (a, _, _) in ins])
    return res


def _sigmoid(x):
    return 1.0 / (1.0 + jnp.exp(-x))


def _position():
    return lax.axis_index("x"), lax.axis_index("y"), lax.axis_index("c")


def _hbm_specs(n):
    return [pl.BlockSpec(memory_space=pl.ANY) for _ in range(n)]


def all_gather_groups(groups):
    n = len(groups)

    def body(*refs):
        ins, outs = refs[:n], refs[n:2 * n]
        stage = refs[2 * n:3 * n]
        send_sems, recv_sems, local_sems = refs[3 * n:]
        x, y, c = _position()
        sibling = (x, y, 1 - c)
        chips = [(1 - x, y), (x, 1 - y), (1 - x, 1 - y)]

        def copy(i, k, block, to, src=None):
            px, py, pc = block
            dst = outs[i].at[4 * px + 2 * py + pc]
            return pltpu.make_async_remote_copy(
                src_ref=dst if src is None else src, dst_ref=dst,
                send_sem=send_sems.at[i, k], recv_sem=recv_sems.at[i, k],
                device_id=to, device_id_type=MESH)

        loads = [pltpu.make_async_copy(ins[i], stage[i], local_sems.at[i, 0]) for i in range(n)]
        for ld in loads:
            ld.start()
        sends, stores = [], []
        for i in range(n):
            loads[i].wait()
            first = [copy(i, 0, (x, y, c), sibling, src=stage[i])]
            first += [copy(i, 1 + j, (x, y, c), (*chip, c), src=stage[i]) for j, chip in enumerate(chips)]
            for cp in first:
                cp.start()
            sends += first
            st = pltpu.make_async_copy(stage[i], outs[i].at[4 * x + 2 * y + c], local_sems.at[i, 1])
            st.start()
            stores.append(st)
        for i in range(n):
            for j, chip in enumerate(chips):
                copy(i, 1 + j, (*chip, c), (x, y, c)).wait_recv()
                passed = copy(i, 4 + j, (*chip, c), sibling)
                passed.start()
                sends.append(passed)
        for i in range(n):
            copy(i, 0, sibling, (x, y, c)).wait_recv()
            for j, chip in enumerate(chips):
                copy(i, 4 + j, (*chip, 1 - c), (x, y, c)).wait_recv()
        for cp in sends:
            cp.wait_send()
        for st in stores:
            st.wait()

    return pl.pallas_call(
        body,
        out_shape=[jax.ShapeDtypeStruct((N_DEV,) + g.shape, g.dtype) for g in groups],
        in_specs=_hbm_specs(n),
        out_specs=_hbm_specs(n),
        scratch_shapes=[pltpu.VMEM(g.shape, g.dtype) for g in groups]
        + [pltpu.SemaphoreType.DMA((n, 7)), pltpu.SemaphoreType.DMA((n, 7)), pltpu.SemaphoreType.DMA((n, 2))],
        name="all_gather_weights",
        compiler_params=pltpu.CompilerParams(vmem_limit_bytes=VMEM_LIMIT),
    )(*groups)


PAIR_BUFFERS = 4


def reduce_scatter_pair(grads):
    n = len(grads)
    C = grads[0].shape[2]
    half = [g.shape[1] // 2 for g in grads]
    chunks = [(i, q, hf) for i in range(n) for q in range(4) for hf in range(2)]
    nb = PAIR_BUFFERS

    def body(*refs):
        ins, theirs = refs[:n], refs[n:2 * n]
        buf, load_sems, send_sems, recv_sems = refs[2 * n:]
        x, y, c = _position()
        sibling = (x, y, 1 - c)

        def load(k):
            i, q, hf = chunks[k]
            r = half[i]
            return pltpu.make_async_copy(ins[i].at[2 * q + (1 - c), pl.ds(hf * r, r), :],
                                         buf.at[k % nb, pl.ds(0, r), :], load_sems.at[k % nb])

        def send(k):
            i, q, hf = chunks[k]
            r = half[i]
            return pltpu.make_async_remote_copy(
                src_ref=buf.at[k % nb, pl.ds(0, r), :], dst_ref=theirs[i].at[q, pl.ds(hf * r, r), :],
                send_sem=send_sems.at[k % nb], recv_sem=recv_sems.at[i],
                device_id=sibling, device_id_type=MESH)

        for k in range(len(chunks) + 1):
            if k < len(chunks):
                if k >= nb:
                    send(k - nb).wait_send()
                load(k).start()
            if k >= 1:
                load(k - 1).wait()
                send(k - 1).start()
        for k in range(max(0, len(chunks) - nb), len(chunks)):
            send(k).wait_send()
        for i in range(n):
            pltpu.make_async_remote_copy(
                src_ref=theirs[i], dst_ref=theirs[i], send_sem=send_sems.at[0], recv_sem=recv_sems.at[i],
                device_id=sibling, device_id_type=MESH).wait_recv()

    return pl.pallas_call(
        body,
        out_shape=[jax.ShapeDtypeStruct((4,) + g.shape[1:], g.dtype) for g in grads],
        in_specs=_hbm_specs(n),
        out_specs=_hbm_specs(n),
        scratch_shapes=[pltpu.VMEM((nb, max(half), C), grads[0].dtype), pltpu.SemaphoreType.DMA((nb,)),
                        pltpu.SemaphoreType.DMA((nb,)), pltpu.SemaphoreType.DMA((n,))],
        name="reduce_scatter_pair",
        compiler_params=pltpu.CompilerParams(vmem_limit_bytes=VMEM_LIMIT),
    )(*grads)


def reduce_scatter_chips(parts, small):
    n = len(parts)

    def body(*refs):
        ins, small_ref = refs[:n], refs[n]
        outs, smalls = refs[n + 1:2 * n + 1], refs[2 * n + 1]
        stage = refs[2 * n + 2:3 * n + 2]
        send_sems, recv_sems, local_sems, s_send, s_recv, s_local = refs[3 * n + 2:]
        x, y, c = _position()
        chip = 2 * x + y
        me = 4 * x + 2 * y + c
        others = [(1 - x, y), (x, 1 - y), (1 - x, 1 - y)]
        remote, local = [], []
        loads = [pltpu.make_async_copy(ins[i].at[chip], stage[i], local_sems.at[i, 0]) for i in range(n)]
        for ld in loads:
            ld.start()
        for i in range(n):
            for j, (px, py) in enumerate(others):
                rc = pltpu.make_async_remote_copy(
                    src_ref=ins[i].at[2 * px + py], dst_ref=outs[i].at[chip],
                    send_sem=send_sems.at[i, j], recv_sem=recv_sems.at[i, j],
                    device_id=(px, py, c), device_id_type=MESH)
                rc.start()
                remote.append(rc)
        for i in range(n):
            loads[i].wait()
            st = pltpu.make_async_copy(stage[i], outs[i].at[chip], local_sems.at[i, 1])
            st.start()
            local.append(st)
        lc = pltpu.make_async_copy(small_ref, smalls.at[me], s_local)
        lc.start()
        local.append(lc)
        k = 0
        for dx in (0, 1):
            for dy in (0, 1):
                for dc in (0, 1):
                    if dx + dy + dc == 0:
                        continue
                    peer = (1 - x if dx else x, 1 - y if dy else y, 1 - c if dc else c)
                    rc = pltpu.make_async_remote_copy(
                        src_ref=small_ref, dst_ref=smalls.at[me],
                        send_sem=s_send.at[k], recv_sem=s_recv.at[k],
                        device_id=peer, device_id_type=MESH)
                    rc.start()
                    remote.append(rc)
                    k += 1
        for rc in remote:
            rc.wait()
        for lc in local:
            lc.wait()

    res = pl.pallas_call(
        body,
        out_shape=[jax.ShapeDtypeStruct(p.shape, p.dtype) for p in parts]
        + [jax.ShapeDtypeStruct((N_DEV,) + small.shape, small.dtype)],
        in_specs=_hbm_specs(n + 1),
        out_specs=_hbm_specs(n + 1),
        scratch_shapes=[pltpu.VMEM(p.shape[1:], p.dtype) for p in parts]
        + [pltpu.SemaphoreType.DMA((n, 3)), pltpu.SemaphoreType.DMA((n, 3)),
           pltpu.SemaphoreType.DMA((n, 2)), pltpu.SemaphoreType.DMA((7,)),
           pltpu.SemaphoreType.DMA((7,)), pltpu.SemaphoreType.DMA],
        name="reduce_scatter_chips",
        compiler_params=pltpu.CompilerParams(vmem_limit_bytes=VMEM_LIMIT),
    )(*parts, small)
    return res[:n], res[n]


def pair_add(grads, theirs, core, name):
    _, R, C = theirs.shape
    tr = R // 2

    def body(c_ref, a_ref, b_ref, o_ref):
        o_ref[...] = (a_ref[...].astype(F32) + b_ref[...].astype(F32)).astype(BF16)

    return pl.pallas_call(
        body,
        out_shape=jax.ShapeDtypeStruct(theirs.shape, BF16),
        grid_spec=pltpu.PrefetchScalarGridSpec(
            num_scalar_prefetch=1, grid=(4, R // tr),
            in_specs=[pl.BlockSpec((None, tr, C), lambda q, i, c: (2 * q + c[0], i, 0)),
                      pl.BlockSpec((None, tr, C), lambda q, i, c: (q, i, 0))],
            out_specs=pl.BlockSpec((None, tr, C), lambda q, i, c: (q, i, 0))),
        name=name,
        compiler_params=pltpu.CompilerParams(dimension_semantics=("parallel", "parallel"),
                                             vmem_limit_bytes=VMEM_LIMIT),
    )(core, grads, theirs)


def sum_slots(recv, off, rows, blk, name):
    nq, _, C = recv.shape
    ob = off // blk

    def body(r_ref, o_ref):
        acc = r_ref[0].astype(F32)
        for q in range(1, nq):
            acc = acc + r_ref[q].astype(F32)
        o_ref[...] = acc

    return _call(name, body, (rows // blk,),
                 [(recv, (nq, blk, C), lambda i: (0, ob + i, 0))],
                 [((rows, C), F32, (blk, C), lambda i: (i, 0))], sem=("parallel",))[0]


def adamw(w, g, m, v, name):
    R, C = w.shape
    tr = R
    for cand in (256, 128, 64, 32, 16, 8):
        if R % cand == 0 and R > cand:
            tr = cand
            break
    c1 = 1.0 / (1.0 - ADAM_B1 ** ADAM_STEP)
    c2 = 1.0 / (1.0 - ADAM_B2 ** ADAM_STEP)

    def body(w_ref, g_ref, m_ref, v_ref, d_ref, nm_ref, nv_ref):
        gv = g_ref[...]
        nm = ADAM_B1 * m_ref[...] + (1.0 - ADAM_B1) * gv
        nv = ADAM_B2 * v_ref[...] + (1.0 - ADAM_B2) * (gv * gv)
        d_ref[...] = -ADAM_LR * ((nm * c1) / (jnp.sqrt(nv * c2) + ADAM_EPS) + ADAM_WD * w_ref[...])
        nm_ref[...] = nm
        nv_ref[...] = nv

    spec = ((tr, C), lambda i: (i, 0))
    out = ((R, C), F32) + spec
    return _call(name, body, (R // tr,), [(w,) + spec, (g,) + spec, (m,) + spec, (v,) + spec],
                 [out, out, out], sem=("parallel",))


def rms_fwd(x, g, name):
    S, D = x.shape
    tr = 512

    def body(x_ref, g_ref, o_ref):
        xv = x_ref[...]
        r = lax.rsqrt(jnp.mean(xv * xv, axis=-1, keepdims=True) + EPS)
        o_ref[...] = (xv * r * g_ref[...]).astype(BF16)

    return _call(name, body, (S // tr,),
                 [(x, (tr, D), lambda i: (i, 0)), (g, (1, D), lambda i: (0, 0))],
                 [((S, D), BF16, (tr, D), lambda i: (i, 0))], sem=("parallel",))[0]


def _rms_bwd_tile(dn, xv, gv):
    r = lax.rsqrt(jnp.mean(xv * xv, axis=-1, keepdims=True) + EPS)
    xh = xv * r
    dxh = dn * gv
    dx = r * (dxh - xh * jnp.mean(dxh * xh, axis=-1, keepdims=True))
    return dx, dn * xh


def final_loss(x, tgt, g, name):
    S, D = x.shape
    tr = 256

    def body(x_ref, t_ref, g_ref, l_ref, dx_ref, dg_ref):
        i = pl.program_id(0)
        xv, gv = x_ref[...], g_ref[...]
        r = lax.rsqrt(jnp.mean(xv * xv, axis=-1, keepdims=True) + EPS)
        xh = xv * r
        e = xh * gv - t_ref[...]
        part = 0.5 * jnp.sum(jnp.sum(e * e, axis=-1, keepdims=True) * (1.0 / D), axis=0, keepdims=True)
        dy = e * (1.0 / D)
        dxh = dy * gv
        dx_ref[...] = r * (dxh - xh * jnp.mean(dxh * xh, axis=-1, keepdims=True))
        dgp = jnp.sum(dy * xh, axis=0, keepdims=True)

        @pl.when(i == 0)
        def _():
            l_ref[...] = jnp.broadcast_to(part, l_ref.shape)
            dg_ref[...] = dgp

        @pl.when(i > 0)
        def _():
            l_ref[...] += jnp.broadcast_to(part, l_ref.shape)
            dg_ref[...] += dgp

    row = ((tr, D), lambda i: (i, 0))
    return _call(name, body, (S // tr,),
                 [(x,) + row, (tgt,) + row, (g, (1, D), lambda i: (0, 0))],
                 [((1, LANES), F32, (1, LANES), lambda i: (0, 0)), ((S, D), F32) + row,
                  ((1, D), F32, (1, D), lambda i: (0, 0))], sem=("arbitrary",))


FFN_TF = 4 * FFN_SHARD


def _ffn_w_spec(G, which, imap):
    D = G.shape[2]
    return (G, (4, FFN_SHARD, D), lambda *idx: (imap(*idx), which, 0))


def ffn_up(n, G, name):
    S, D = n.shape
    F = N_DEV * FFN_SHARD
    tm = 512

    def body(n_ref, w1_ref, w3_ref, ab_ref):
        nv = n_ref[...]
        ab_ref[0] = _dot(nv, w1_ref[...].reshape(FFN_TF, D), 1, 1).astype(BF16)
        ab_ref[1] = _dot(nv, w3_ref[...].reshape(FFN_TF, D), 1, 1).astype(BF16)

    return _call(name, body, (F // FFN_TF, S // tm),
                 [(n, (tm, D), lambda j, i: (i, 0)),
                  _ffn_w_spec(G, 0, lambda j, i: j), _ffn_w_spec(G, 1, lambda j, i: j)],
                 [((2, S, F), BF16, (2, tm, FFN_TF), lambda j, i: (0, i, j))],
                 sem=("parallel", "parallel"))[0]


def ffn_down(ab, G, x, name):
    _, S, F = ab.shape
    D = x.shape[1]
    tm = 512
    nk = F // FFN_TF

    def body(ab_ref, w2_ref, x_ref, o_ref, acc_ref):
        k = pl.program_id(1)
        av, bv = ab_ref[0].astype(F32), ab_ref[1].astype(F32)
        h = (av * _sigmoid(av) * bv).astype(BF16)
        p = _dot(h, w2_ref[...].reshape(FFN_TF, D))

        @pl.when(k == 0)
        def _():
            acc_ref[...] = p

        @pl.when(k > 0)
        def _():
            acc_ref[...] += p

        @pl.when(k == nk - 1)
        def _():
            o_ref[...] = x_ref[...] + 0.5 * acc_ref[...]

    return _call(name, body, (S // tm, nk),
                 [(ab, (2, tm, FFN_TF), lambda i, k: (0, i, k)), _ffn_w_spec(G, 2, lambda i, k: k),
                  (x, (tm, D), lambda i, k: (i, 0))],
                 [((S, D), F32, (tm, D), lambda i, k: (i, 0))],
                 scratch=[pltpu.VMEM((tm, D), F32)], sem=("parallel", "arbitrary"))[0]


def ffn_bwd(dxo, ab, n, G, x_in, g, name):
    _, S, F = ab.shape
    D = x_in.shape[1]
    tm = 512
    nf = F // FFN_TF

    def down_body(d_ref, w2_ref, ab_ref, o_ref):
        dh = 0.5 * _dot(d_ref[...].astype(BF16), w2_ref[...].reshape(FFN_TF, D), 1, 1)
        av, bv = ab_ref[0].astype(F32), ab_ref[1].astype(F32)
        sig = _sigmoid(av)
        silu = av * sig
        o_ref[0] = (dh * bv * (sig * (1.0 + av * (1.0 - sig)))).astype(BF16)
        o_ref[1] = (dh * silu).astype(BF16)
        o_ref[2] = (silu * bv).astype(BF16)

    dabh = _call(name + "_down_bwd", down_body, (nf, S // tm),
                 [(dxo, (tm, D), lambda j, i: (i, 0)), _ffn_w_spec(G, 2, lambda j, i: j),
                  (ab, (2, tm, FFN_TF), lambda j, i: (0, i, j))],
                 [((3, S, F), BF16, (3, tm, FFN_TF), lambda j, i: (0, i, j))],
                 sem=("parallel", "parallel"))[0]

    tk = 512
    nk = S // tk
    gshape = (N_DEV, 3 * FFN_SHARD, D)

    def dw2_body(h_ref, d_ref, o_ref, acc_ref):
        k = pl.program_id(1)
        p = _dot(h_ref[...], d_ref[...].astype(BF16), 0, 0)

        @pl.when(k == 0)
        def _():
            acc_ref[...] = p

        @pl.when(k > 0)
        def _():
            acc_ref[...] += p

        @pl.when(k == nk - 1)
        def _():
            o_ref[...] = (0.5 * acc_ref[...]).astype(BF16).reshape(4, FFN_SHARD, D)

    gw = _call(name + "_dw2", dw2_body, (nf, nk),
               [(dabh, (None, tk, FFN_TF), lambda j, k: (2, k, j)), (dxo, (tk, D), lambda j, k: (k, 0))],
               [(gshape, BF16, (4, FFN_SHARD, D), lambda j, k: (j, 2, 0))],
               scratch=[pltpu.VMEM((FFN_TF, D), F32)], sem=("parallel", "arbitrary"))[0]

    def dw13_body(gw_ref, dab_ref, n_ref, o_ref, acc_ref):
        k = pl.program_id(2)
        p = _dot(dab_ref[...], n_ref[...], 0, 0)

        @pl.when(k == 0)
        def _():
            acc_ref[...] = p

        @pl.when(k > 0)
        def _():
            acc_ref[...] += p

        @pl.when(k == nk - 1)
        def _():
            o_ref[...] = acc_ref[...].astype(BF16).reshape(4, FFN_SHARD, D)

    gw = pl.pallas_call(
        dw13_body,
        out_shape=jax.ShapeDtypeStruct(gshape, BF16),
        grid=(2, nf, nk),
        in_specs=[pl.BlockSpec(memory_space=pl.ANY),
                  pl.BlockSpec((None, tk, FFN_TF), lambda w, j, k: (w, k, j)),
                  pl.BlockSpec((tk, D), lambda w, j, k: (k, 0))],
        out_specs=pl.BlockSpec((4, FFN_SHARD, D), lambda w, j, k: (j, w, 0)),
        scratch_shapes=[pltpu.VMEM((FFN_TF, D), F32)],
        input_output_aliases={0: 0},
        name=name + "_dw13",
        compiler_params=pltpu.CompilerParams(dimension_semantics=("parallel", "parallel", "arbitrary"),
                                             vmem_limit_bytes=VMEM_LIMIT),
    )(gw, dabh, n)

    def dn_body(dab_ref, w1_ref, w3_ref, x_ref, d_ref, g_ref, dx_ref, dg_ref, acc_ref):
        i, k = pl.program_id(0), pl.program_id(1)
        p = _dot(dab_ref[0], w1_ref[...].reshape(FFN_TF, D)) + _dot(dab_ref[1], w3_ref[...].reshape(FFN_TF, D))

        @pl.when(k == 0)
        def _():
            acc_ref[...] = p

        @pl.when(k > 0)
        def _():
            acc_ref[...] += p

        @pl.when(k == nf - 1)
        def _():
            dx, dgt = _rms_bwd_tile(acc_ref[...], x_ref[...], g_ref[...])
            dx_ref[...] = d_ref[...] + dx
            dgp = jnp.sum(dgt, axis=0, keepdims=True)

            @pl.when(i == 0)
            def _():
                dg_ref[...] = dgp

            @pl.when(i > 0)
            def _():
                dg_ref[...] += dgp

    dx, dg = _call(name + "_dn", dn_body, (S // tm, nf),
                   [(dabh, (2, tm, FFN_TF), lambda i, k: (0, i, k)),
                    _ffn_w_spec(G, 0, lambda i, k: k), _ffn_w_spec(G, 1, lambda i, k: k),
                    (x_in, (tm, D), lambda i, k: (i, 0)), (dxo, (tm, D), lambda i, k: (i, 0)),
                    (g, (1, D), lambda i, k: (0, 0))],
                   [((S, D), F32, (tm, D), lambda i, k: (i, 0)), ((1, D), F32, (1, D), lambda i, k: (0, 0))],
                   scratch=[pltpu.VMEM((tm, D), F32)], sem=("arbitrary", "arbitrary"))
    return dx, dg, gw


PROJ_TN = 512


def in_proj(h, Gm, first_tile, n_tiles, dtype, name):
    S, D = h.shape
    tm = 1024

    def body(h_ref, w_ref, o_ref):
        o_ref[...] = _dot(h_ref[...], w_ref[...]).astype(dtype)

    return _call(name, body, (n_tiles, S // tm),
                 [(h, (tm, D), lambda j, i: (i, 0)),
                  (Gm, (None, D, PROJ_TN), lambda j, i: ((first_tile + j) // 2, 0, (first_tile + j) % 2))],
                 [((S, n_tiles * PROJ_TN), dtype, (tm, PROJ_TN), lambda j, i: (i, j))],
                 sem=("parallel", "parallel"))[0]


def in_proj_bwd(dproj, h, Gm, x_in, g, dres, gm_grads, name):
    S, D = h.shape
    NT = dproj.shape[1] // PROJ_TN
    tk = 512
    nk = S // tk

    def dw_body(gm_ref, h_ref, d_ref, o_ref, acc_ref):
        k = pl.program_id(1)
        p = _dot(h_ref[...], d_ref[...], 0, 0)

        @pl.when(k == 0)
        def _():
            acc_ref[...] = p

        @pl.when(k > 0)
        def _():
            acc_ref[...] += p

        @pl.when(k == nk - 1)
        def _():
            o_ref[...] = acc_ref[...].astype(BF16)

    gm_grads = pl.pallas_call(
        dw_body,
        out_shape=jax.ShapeDtypeStruct(gm_grads.shape, BF16),
        grid=(NT, nk),
        in_specs=[pl.BlockSpec(memory_space=pl.ANY),
                  pl.BlockSpec((tk, D), lambda j, k: (k, 0)),
                  pl.BlockSpec((tk, PROJ_TN), lambda j, k: (k, j))],
        out_specs=pl.BlockSpec((None, D, PROJ_TN), lambda j, k: (j // 2, 0, j % 2)),
        scratch_shapes=[pltpu.VMEM((D, PROJ_TN), F32)],
        input_output_aliases={0: 0},
        name=name + "_dw",
        compiler_params=pltpu.CompilerParams(dimension_semantics=("parallel", "arbitrary"),
                                             vmem_limit_bytes=VMEM_LIMIT),
    )(gm_grads, h, dproj)

    tm = 512

    def dh_body(d_ref, w_ref, x_ref, r_ref, g_ref, dx_ref, dg_ref, acc_ref):
        i, k = pl.program_id(0), pl.program_id(1)
        p = _dot(d_ref[...], w_ref[...], 1, 1)

        @pl.when(k == 0)
        def _():
            acc_ref[...] = p

        @pl.when(k > 0)
        def _():
            acc_ref[...] += p

        @pl.when(k == NT - 1)
        def _():
            dx, dgt = _rms_bwd_tile(acc_ref[...], x_ref[...], g_ref[...])
            dx_ref[...] = r_ref[...] + dx
            dgp = jnp.sum(dgt, axis=0, keepdims=True)

            @pl.when(i == 0)
            def _():
                dg_ref[...] = dgp

            @pl.when(i > 0)
            def _():
                dg_ref[...] += dgp

    dx, dg = _call(name + "_dh", dh_body, (S // tm, NT),
                   [(dproj, (tm, PROJ_TN), lambda i, k: (i, k)),
                    (Gm, (None, D, PROJ_TN), lambda i, k: (k // 2, 0, k % 2)),
                    (x_in, (tm, D), lambda i, k: (i, 0)), (dres, (tm, D), lambda i, k: (i, 0)),
                    (g, (1, D), lambda i, k: (0, 0))],
                   [((S, D), F32, (tm, D), lambda i, k: (i, 0)), ((1, D), F32, (1, D), lambda i, k: (0, 0))],
                   scratch=[pltpu.VMEM((tm, D), F32)], sem=("arbitrary", "arbitrary"))
    return dx, dg, gm_grads


def _t5_bucket(rel):
    n = N_BUCKETS // 2
    max_exact = n // 2
    ret = jnp.where(rel > 0, n, 0)
    a = jnp.abs(rel)
    af = jnp.maximum(a, 1).astype(F32)
    large = max_exact + (jnp.log(af / max_exact) / math.log(MAX_DISTANCE / max_exact)
                         * (n - max_exact)).astype(jnp.int32)
    large = jnp.minimum(large, n - 1)
    return ret + jnp.where(a < max_exact, a, large)


def _bucket_tables():
    qi = jnp.arange(A_TQ, dtype=jnp.int32)[:, None]
    kj = jnp.arange(A_WIN, dtype=jnp.int32)[None, :]
    rel = kj - HALF_WINDOW - qi
    return jnp.stack([_t5_bucket(rel * d) for d in DILATIONS], axis=0)


def bias_build(rel_bias, buckets):
    def body(tab_ref, bk_ref, o_ref):
        col = pl.program_id(0) * HEADS_PER_GROUP_A + pl.program_id(1)
        bk = bk_ref[...]
        acc = jnp.zeros(bk.shape, F32)
        for b in range(N_BUCKETS):
            acc = jnp.where(bk == b, tab_ref[b, col], acc)
        qi = lax.broadcasted_iota(jnp.int32, bk.shape, 0)
        kj = lax.broadcasted_iota(jnp.int32, bk.shape, 1)
        o_ref[...] = jnp.where(jnp.abs(kj - HALF_WINDOW - qi) <= HALF_WINDOW, acc, NEG_INF)

    return pl.pallas_call(
        body,
        out_shape=jax.ShapeDtypeStruct((3, HEADS_PER_GROUP_A, A_TQ, A_WIN), F32),
        grid=(3, HEADS_PER_GROUP_A),
        in_specs=[pl.BlockSpec(memory_space=pltpu.SMEM),
                  pl.BlockSpec((None, A_TQ, A_WIN), lambda g, h: (g, 0, 0))],
        out_specs=pl.BlockSpec((None, None, A_TQ, A_WIN), lambda g, h: (g, h, 0, 0)),
        name="a_bias_build",
        compiler_params=pltpu.CompilerParams(dimension_semantics=("parallel", "parallel")),
    )(rel_bias, buckets)


def bias_bwd(dbias, buckets):
    def body(d_ref, bk_ref, o_ref):
        bk = bk_ref[...]
        dv = d_ref[...]
        for b in range(N_BUCKETS):
            part = jnp.sum(jnp.where(bk == b, dv, 0.0), axis=1, keepdims=True)
            o_ref[b:b + 1, :] = jnp.broadcast_to(jnp.sum(part, axis=0, keepdims=True), (1, LANES))

    out = pl.pallas_call(
        body,
        out_shape=jax.ShapeDtypeStruct((3, HEADS_PER_GROUP_A, N_BUCKETS, LANES), F32),
        grid=(3, HEADS_PER_GROUP_A),
        in_specs=[pl.BlockSpec((None, None, A_TQ, A_WIN), lambda g, h: (g, h, 0, 0)),
                  pl.BlockSpec((None, A_TQ, A_WIN), lambda g, h: (g, 0, 0))],
        out_specs=pl.BlockSpec((None, None, N_BUCKETS, LANES), lambda g, h: (g, h, 0, 0)),
        name="a_bias_bwd",
        compiler_params=pltpu.CompilerParams(dimension_semantics=("parallel", "parallel")),
    )(dbias, buckets)
    return out[:, :, :, 0].transpose(2, 0, 1).reshape(N_BUCKETS, 3 * HEADS_PER_GROUP_A)


def _a_fill_padded(pad_ref, src_ref, L):
    zeros = jnp.zeros((HALF_WINDOW, LANES), pad_ref.dtype)
    pad_ref[0:HALF_WINDOW, :] = zeros
    pad_ref[HALF_WINDOW + L:2 * HALF_WINDOW + L, :] = zeros
    pad_ref[HALF_WINDOW:HALF_WINDOW + L, :] = src_ref[...]


def _a_key_valid(qb, L):
    kidx = qb * A_TQ - HALF_WINDOW + lax.broadcasted_iota(jnp.int32, (A_TQ, A_WIN), 1)
    return (kidx >= 0) & (kidx < L)


def a_fwd(proj_a, bias_g, g, name):
    S = proj_a.shape[0]
    d = DILATIONS[g]
    L = S // d
    nqb = L // A_TQ
    nb = A_BLOCKS_PER_TOKEN
    pv = proj_a.reshape(L, d * A_QKV_WIDTH)

    def body(q_ref, k_ref, v_ref, b_ref, o_ref, l_ref, kpad, vpad):
        qb = pl.program_id(2)

        @pl.when(qb == 0)
        def _():
            _a_fill_padded(kpad, k_ref, L)
            _a_fill_padded(vpad, v_ref, L)

        start = pl.multiple_of(qb * A_TQ, A_TQ)
        kw = kpad[pl.ds(start, A_WIN), :]
        vw = vpad[pl.ds(start, A_WIN), :]
        q = q_ref[...]
        lane = lax.broadcasted_iota(jnp.int32, (A_TQ, LANES), 1)
        valid = _a_key_valid(qb, L)
        outs, lses = [], []
        for h in range(2):
            qh = jnp.where((lane >= HEAD_DIM_A * h) & (lane < HEAD_DIM_A * (h + 1)), q, jnp.zeros_like(q))
            s = _dot(qh, kw, 1, 1) * (HEAD_DIM_A ** -0.5) + b_ref[h]
            s = jnp.where(valid, s, NEG_INF)
            m = jnp.max(s, axis=-1, keepdims=True)
            e = jnp.exp(s - m)
            l = jnp.sum(e, axis=-1, keepdims=True)
            outs.append(_dot(e.astype(BF16), vw) / l)
            lses.append(m + jnp.log(l))
        o_ref[...] = jnp.where(lane < HEAD_DIM_A, outs[0], outs[1])
        l_ref[...] = jnp.where(lane < HEAD_DIM_A, lses[0], lses[1])

    col = lambda which: (lambda r, hp, qb: r * nb + (which * 3 + g) * 4 + hp)
    out_spec = ((L, d * GROUP_WIDTH_A), F32, (A_TQ, LANES), lambda r, hp, qb: (qb, r * 4 + hp))
    o, lse = _call(name, body, (d, 4, nqb),
                   [(pv, (A_TQ, LANES), lambda r, hp, qb: (qb, col(0)(r, hp, qb))),
                    (pv, (L, LANES), lambda r, hp, qb: (0, col(1)(r, hp, qb))),
                    (pv, (L, LANES), lambda r, hp, qb: (0, col(2)(r, hp, qb))),
                    (bias_g, (2, A_TQ, A_WIN), lambda r, hp, qb: (hp, 0, 0))],
                   [out_spec, out_spec],
                   scratch=[pltpu.VMEM((L + 2 * HALF_WINDOW, LANES), BF16)] * 2,
                   sem=("parallel", "parallel", "arbitrary"))
    return o.reshape(S, GROUP_WIDTH_A), lse.reshape(S, GROUP_WIDTH_A)


def a_combine(outs, lses, name):
    S, W = outs[0].shape
    tr = 512

    def body(o0, o1, o2, l0, l1, l2, oa_ref, lt_ref):
        a, b, c = l0[...], l1[...], l2[...]
        m = jnp.maximum(jnp.maximum(a, b), c)
        ea, eb, ec = jnp.exp(a - m), jnp.exp(b - m), jnp.exp(c - m)
        z = ea + eb + ec
        oa_ref[...] = ((ea * o0[...] + eb * o1[...] + ec * o2[...]) / z).astype(BF16)
        lt_ref[...] = m + jnp.log(z)

    spec = ((tr, W), lambda i: (i, 0))
    return _call(name, body, (S // tr,), [(a,) + spec for a in (*outs, *lses)],
                 [((S, W), BF16) + spec, ((S, W), F32) + spec], sem=("parallel",))


def a_bwd(proj_a, bias_g, do_a, o_a, lse_tot, g, name):
    S = proj_a.shape[0]
    d = DILATIONS[g]
    L = S // d
    nqb = L // A_TQ
    nb = A_BLOCKS_PER_TOKEN
    pv = proj_a.reshape(L, d * A_QKV_WIDTH)
    view = lambda a: a.reshape(L, d * GROUP_WIDTH_A)
    scale = HEAD_DIM_A ** -0.5

    def body(q_ref, k_ref, v_ref, b_ref, do_ref, o_ref, l_ref, dq_ref, dk_ref, dv_ref, db_ref,
             kpad, vpad, dkacc, dvacc):
        r, qb = pl.program_id(1), pl.program_id(2)

        @pl.when(qb == 0)
        def _():
            _a_fill_padded(kpad, k_ref, L)
            _a_fill_padded(vpad, v_ref, L)
            dkacc[...] = jnp.zeros(dkacc.shape, F32)
            dvacc[...] = jnp.zeros(dvacc.shape, F32)

        @pl.when((qb == 0) & (r == 0))
        def _():
            db_ref[...] = jnp.zeros(db_ref.shape, F32)

        start = pl.multiple_of(qb * A_TQ, A_TQ)
        kw = kpad[pl.ds(start, A_WIN), :]
        vw = vpad[pl.ds(start, A_WIN), :]
        q = q_ref[...]
        do = do_ref[...]
        ov = o_ref[...].astype(F32)
        lt = l_ref[...]
        lane = lax.broadcasted_iota(jnp.int32, (A_TQ, LANES), 1)
        valid = _a_key_valid(qb, L)
        dqs = []
        dk_win = jnp.zeros((A_WIN, LANES), F32)
        dv_win = jnp.zeros((A_WIN, LANES), F32)
        for h in range(2):
            mh = (lane >= HEAD_DIM_A * h) & (lane < HEAD_DIM_A * (h + 1))
            qh = jnp.where(mh, q, jnp.zeros_like(q))
            doh = jnp.where(mh, do, 0.0)
            s = _dot(qh, kw, 1, 1) * scale + b_ref[h]
            s = jnp.where(valid, s, NEG_INF)
            p = jnp.exp(s - lt[:, HEAD_DIM_A * h:HEAD_DIM_A * h + 1])
            t = jnp.sum(doh * ov, axis=-1, keepdims=True)
            dob = doh.astype(BF16)
            ds = p * (_dot(dob, vw, 1, 1) - t)
            db_ref[h] += ds
            dsb = (ds * scale).astype(BF16)
            dqs.append(_dot(dsb, kw))
            dk_win = dk_win + _dot(dsb, qh, 0, 0)
            dv_win = dv_win + _dot(p.astype(BF16), dob, 0, 0)
        dq_ref[...] = jnp.where(lane < HEAD_DIM_A, dqs[0], dqs[1]).astype(BF16)
        dkacc[pl.ds(start, A_WIN), :] += dk_win
        dvacc[pl.ds(start, A_WIN), :] += dv_win

        @pl.when(qb == nqb - 1)
        def _():
            dk_ref[...] = dkacc[HALF_WINDOW:HALF_WINDOW + L, :].astype(BF16)
            dv_ref[...] = dvacc[HALF_WINDOW:HALF_WINDOW + L, :].astype(BF16)

    col = lambda which: (lambda hp, r, qb: r * nb + (which * 3 + g) * 4 + hp)
    tile = ((A_TQ, LANES), lambda hp, r, qb: (qb, r * 4 + hp))
    slab = ((L, LANES), lambda hp, r, qb: (0, r * 4 + hp))
    oshape = (L, d * GROUP_WIDTH_A)
    dq, dk, dv, db = _call(
        name, body, (4, d, nqb),
        [(pv, (A_TQ, LANES), lambda hp, r, qb: (qb, col(0)(hp, r, qb))),
         (pv, (L, LANES), lambda hp, r, qb: (0, col(1)(hp, r, qb))),
         (pv, (L, LANES), lambda hp, r, qb: (0, col(2)(hp, r, qb))),
         (bias_g, (2, A_TQ, A_WIN), lambda hp, r, qb: (hp, 0, 0)),
         (view(do_a),) + tile, (view(o_a),) + tile, (view(lse_tot),) + tile],
        [(oshape, BF16) + tile, (oshape, BF16) + slab, (oshape, BF16) + slab,
         ((HEADS_PER_GROUP_A, A_TQ, A_WIN), F32, (2, A_TQ, A_WIN), lambda hp, r, qb: (hp, 0, 0))],
        scratch=[pltpu.VMEM((L + 2 * HALF_WINDOW, LANES), BF16)] * 2
        + [pltpu.VMEM((L + 2 * HALF_WINDOW, LANES), F32)] * 2,
        sem=("parallel", "arbitrary", "arbitrary"))
    return dq.reshape(S, GROUP_WIDTH_A), dk.reshape(S, GROUP_WIDTH_A), dv.reshape(S, GROUP_WIDTH_A), db


def _rope_tables(S):
    rows = S // GRID_W
    row = jnp.repeat(jnp.arange(rows, dtype=F32), GRID_W)
    col = jnp.tile(jnp.arange(GRID_W, dtype=F32), rows)
    n_freq = HEAD_DIM_B // 4
    freq = ROPE_THETA ** (-jnp.arange(n_freq, dtype=F32) / n_freq)
    ang = jnp.concatenate([row[:, None] * freq, col[:, None] * freq], axis=-1)
    cos, sin = jnp.cos(ang), jnp.sin(ang)
    return jnp.repeat(cos, 2, axis=-1), jnp.stack([-sin, sin], axis=-1).reshape(S, HEAD_DIM_B)


def _swap_pairs(y):
    lane = lax.broadcasted_iota(jnp.int32, y.shape, 1)
    return jnp.where(lane % 2 == 0, pltpu.roll(y, LANES - 1, 1), pltpu.roll(y, 1, 1))


def qkv_prep(proj_b, gains, cos_t, sin_t, name):
    S = proj_b.shape[0]
    ts = 512
    n_rot = N_HEADS_B + N_KV_B

    def body(x_ref, g_ref, c_ref, s_ref, o_ref):
        hb = pl.program_id(0)

        @pl.when(hb < n_rot)
        def _():
            xv = x_ref[...]
            r = lax.rsqrt(jnp.mean(xv * xv, axis=-1, keepdims=True) + EPS)
            yv = xv * r * g_ref[...]
            o_ref[...] = (yv * c_ref[...] + _swap_pairs(yv) * s_ref[...]).astype(BF16)

        @pl.when(hb >= n_rot)
        def _():
            o_ref[...] = x_ref[...].astype(BF16)

    nh = n_rot + N_KV_B
    return _call(name, body, (nh, S // ts),
                 [(proj_b, (ts, LANES), lambda hb, i: (i, hb)), (gains, (1, LANES), lambda hb, i: (0, hb)),
                  (cos_t, (ts, LANES), lambda hb, i: (i, 0)), (sin_t, (ts, LANES), lambda hb, i: (i, 0))],
                 [((S, nh * LANES), BF16, (ts, LANES), lambda hb, i: (i, hb))],
                 sem=("parallel", "parallel"))[0]


def qk_prep_bwd(dr, proj_b, col0, gain, cos_t, sin_t, name):
    S, W = dr.shape
    H = W // LANES
    ts = 512

    def body(d_ref, x_ref, g_ref, c_ref, s_ref, dx_ref, dg_ref):
        hb, i = pl.program_id(0), pl.program_id(1)
        dout = d_ref[...]
        dy = dout * c_ref[...] + _swap_pairs(dout * s_ref[...])
        dx, dgt = _rms_bwd_tile(dy, x_ref[...], g_ref[...])
        dx_ref[...] = dx.astype(BF16)
        dgp = jnp.sum(dgt, axis=0, keepdims=True)

        @pl.when((hb == 0) & (i == 0))
        def _():
            dg_ref[...] = dgp

        @pl.when((hb > 0) | (i > 0))
        def _():
            dg_ref[...] += dgp

    return _call(name, body, (H, S // ts),
                 [(dr, (ts, LANES), lambda hb, i: (i, hb)), (proj_b, (ts, LANES), lambda hb, i: (i, col0 + hb)),
                  (gain, (1, LANES), lambda hb, i: (0, 0)),
                  (cos_t, (ts, LANES), lambda hb, i: (i, 0)), (sin_t, (ts, LANES), lambda hb, i: (i, 0))],
                 [((S, W), BF16, (ts, LANES), lambda hb, i: (i, hb)),
                  ((1, LANES), F32, (1, LANES), lambda hb, i: (0, 0))],
                 sem=("arbitrary", "arbitrary"))


def _row_sums(x):
    hi = x.astype(BF16)
    lo = (x - hi.astype(F32)).astype(BF16)
    ones = jnp.ones((8, LANES), BF16)
    return (_dot(ones, hi, 1, 1) + _dot(ones, lo, 1, 1))[0:1, :]


def flash_fwd(qkv, name):
    S = qkv.shape[0]
    tq = B_TQ_FWD
    scale = HEAD_DIM_B ** -0.5

    def body(q_ref, k_ref, v_ref, o_ref, l_ref):
        s = _dot(q_ref[...], k_ref[...], 1, 1) * scale
        m = jnp.max(s, axis=-1, keepdims=True)
        e = jnp.exp(s - m)
        l = jnp.sum(e, axis=-1, keepdims=True)
        o_ref[...] = (_dot(e.astype(BF16), v_ref[...]) / l).astype(BF16)
        lse = jnp.broadcast_to(m + jnp.log(l), (tq, LANES))
        l_ref[...] = _row_sums(lse) * (1.0 / LANES)

    head = lambda g, h, i: (i, g * GQA_GROUP_B + h)
    return _call(name, body, (N_KV_B, GQA_GROUP_B, S // tq),
                 [(qkv, (tq, LANES), head),
                  (qkv, (S, LANES), lambda g, h, i: (0, N_HEADS_B + g)),
                  (qkv, (S, LANES), lambda g, h, i: (0, N_HEADS_B + N_KV_B + g))],
                 [((S, N_HEADS_B * LANES), BF16, (tq, LANES), head),
                  ((N_HEADS_B, 1, S), F32, (None, 1, tq), lambda g, h, i: (g * GQA_GROUP_B + h, 0, i))],
                 sem=("parallel", "parallel", "parallel"))


def flash_bwd(qkv, k_t, do_b, o_b, lse, name):
    S = qkv.shape[0]
    tq = B_TQ_BWD
    nq = S // tq
    scale = HEAD_DIM_B ** -0.5

    def body(q_ref, k_ref, v_ref, kt_ref, do_ref, o_ref, l_ref, dq_ref, dk_ref, dv_ref, dkacc, dvacc):
        h, i = pl.program_id(1), pl.program_id(2)

        @pl.when((h == 0) & (i == 0))
        def _():
            dkacc[...] = jnp.zeros(dkacc.shape, F32)
            dvacc[...] = jnp.zeros(dvacc.shape, F32)

        q = q_ref[...]
        do = do_ref[...]
        dob = do.astype(BF16)
        t = _row_sums(do * o_ref[...].astype(F32))
        pt = jnp.exp(_dot(k_ref[...], q, 1, 1) * scale - l_ref[...])
        dst = pt * (_dot(v_ref[...], dob, 1, 1) - t) * scale
        dsb = dst.astype(BF16)
        dvacc[...] += _dot(pt.astype(BF16), dob)
        dkacc[...] += _dot(dsb, q)
        dq_ref[...] = _dot(kt_ref[...], dsb).T

        @pl.when((h == GQA_GROUP_B - 1) & (i == nq - 1))
        def _():
            dk_ref[...] = dkacc[...]
            dv_ref[...] = dvacc[...].astype(BF16)

    head = lambda g, h, i: (i, g * GQA_GROUP_B + h)
    return _call(name, body, (N_KV_B, GQA_GROUP_B, nq),
                 [(qkv, (tq, LANES), head),
                  (qkv, (S, LANES), lambda g, h, i: (0, N_HEADS_B + g)),
                  (qkv, (S, LANES), lambda g, h, i: (0, N_HEADS_B + N_KV_B + g)),
                  (k_t, (LANES, S), lambda g, h, i: (g, 0)),
                  (do_b, (tq, LANES), head), (o_b, (tq, LANES), head),
                  (lse, (None, 1, tq), lambda g, h, i: (g * GQA_GROUP_B + h, 0, i))],
                 [((S, N_HEADS_B * LANES), F32, (tq, LANES), head),
                  ((S, N_KV_B * LANES), F32, (S, LANES), lambda g, h, i: (0, g)),
                  ((S, N_KV_B * LANES), BF16, (S, LANES), lambda g, h, i: (0, g))],
                 scratch=[pltpu.VMEM((S, LANES), F32)] * 2,
                 sem=("parallel", "arbitrary", "arbitrary"))


MERGE_TN = 512


def _mix_rows_spec(Gm, row0, n_slots, slot_map, cols=None, col_map=None):
    C = Gm.shape[2] if cols is None else cols
    cm = (lambda *idx: 0) if col_map is None else col_map
    return (Gm, (n_slots, LANES, C), lambda *idx: (slot_map(*idx), row0 // LANES, cm(*idx)))


def merge_fwd(o_a, o_b, w_a, Gm, proj_b, b_gate, name):
    S = o_a.shape[0]
    D = w_a.shape[1]
    tm, tn = 512, MERGE_TN
    ga0, gb0 = PB_GATE_A // tn, PB_GATE_B // tn

    def body(oa_ref, ob_ref, wa_ref, wb_ref, pa_ref, pb_ref, ba_ref, bb_ref, m_ref, ya_ref, yb_ref):
        ya = _dot(oa_ref[...], wa_ref[...])
        yb = _dot(ob_ref[...], wb_ref[...].reshape(N_DEV * LANES, tn))
        ga = _sigmoid(pa_ref[...] + ba_ref[...])
        gb = _sigmoid(pb_ref[...] + bb_ref[...])
        m_ref[...] = (ga * ya + gb * yb).astype(BF16)
        ya_ref[...] = ya.astype(BF16)
        yb_ref[...] = yb.astype(BF16)

    out = ((S, D), BF16, (tm, tn), lambda j, i: (i, j))
    return _call(name, body, (D // tn, S // tm),
                 [(o_a, (tm, o_a.shape[1]), lambda j, i: (i, 0)), (o_b, (tm, o_b.shape[1]), lambda j, i: (i, 0)),
                  (w_a, (w_a.shape[0], tn), lambda j, i: (0, j)),
                  _mix_rows_spec(Gm, MIX_WB, N_DEV, lambda j, i: 0, cols=tn, col_map=lambda j, i: j),
                  (proj_b, (tm, tn), lambda j, i: (i, ga0 + j)), (proj_b, (tm, tn), lambda j, i: (i, gb0 + j)),
                  (b_gate, (1, tn), lambda j, i: (0, j)), (b_gate, (1, tn), lambda j, i: (0, D // tn + j))],
                 [out, out, out], sem=("parallel", "parallel"))


def out_proj(merged, Gm, x, name):
    S, D = x.shape
    tm, tn = 512, MERGE_TN

    def body(m_ref, w_ref, x_ref, o_ref):
        o_ref[...] = x_ref[...] + _dot(m_ref[...], w_ref[...].reshape(N_DEV * LANES, tn))

    return _call(name, body, (D // tn, S // tm),
                 [(merged, (tm, D), lambda j, i: (i, 0)),
                  _mix_rows_spec(Gm, MIX_WOUT, N_DEV, lambda j, i: 0, cols=tn, col_map=lambda j, i: j),
                  (x, (tm, tn), lambda j, i: (i, j))],
                 [((S, D), F32, (tm, tn), lambda j, i: (i, j))], sem=("parallel", "parallel"))[0]


def merge_bwd(dx2, Gm, ya, yb, proj_b, b_gate, name):
    S, D = dx2.shape
    tm, tn = 512, MERGE_TN
    nn = D // tn
    ga0, gb0 = PB_GATE_A // tn, PB_GATE_B // tn

    def body(d_ref, w_ref, ya_ref, yb_ref, pa_ref, pb_ref, ba_ref, bb_ref, dya_ref, dyb_ref, dg_ref, dbg_ref):
        i = pl.program_id(1)
        dm = _dot(d_ref[...].astype(BF16), w_ref[...].reshape(tn, D), 1, 1)
        ga = _sigmoid(pa_ref[...] + ba_ref[...])
        gb = _sigmoid(pb_ref[...] + bb_ref[...])
        dya_ref[...] = (dm * ga).astype(BF16)
        dyb_ref[...] = (dm * gb).astype(BF16)
        dpa = dm * ya_ref[...].astype(F32) * ga * (1.0 - ga)
        dpb = dm * yb_ref[...].astype(F32) * gb * (1.0 - gb)
        dg_ref[0] = dpa.astype(BF16)
        dg_ref[1] = dpb.astype(BF16)
        sa = jnp.sum(dpa, axis=0, keepdims=True)
        sb = jnp.sum(dpb, axis=0, keepdims=True)

        @pl.when(i == 0)
        def _():
            dbg_ref[0] = sa
            dbg_ref[1] = sb

        @pl.when(i > 0)
        def _():
            dbg_ref[0] += sa
            dbg_ref[1] += sb

    tile = ((tm, tn), lambda j, i: (i, j))
    dya, dyb, dgate, dbg = _call(
        name, body, (nn, S // tm),
        [(dx2, (tm, D), lambda j, i: (i, 0)),
         _mix_rows_spec(Gm, MIX_WOUT, tn // LANES, lambda j, i: j),
         (ya,) + tile, (yb,) + tile,
         (proj_b, (tm, tn), lambda j, i: (i, ga0 + j)), (proj_b, (tm, tn), lambda j, i: (i, gb0 + j)),
         (b_gate, (1, tn), lambda j, i: (0, j)), (b_gate, (1, tn), lambda j, i: (0, nn + j))],
        [((S, D), BF16) + tile, ((S, D), BF16) + tile,
         ((2, S, D), BF16, (2, tm, tn), lambda j, i: (0, i, j)),
         ((2, 1, D), F32, (2, 1, tn), lambda j, i: (0, 0, j))],
        sem=("parallel", "arbitrary"))
    return dya, dyb, dgate, dbg


def matmul_nt(a, b_spec_fn, N, name, tn=512):
    S, K = a.shape
    tm = 512

    def body(a_ref, b_ref, o_ref):
        b = b_ref[...]
        o_ref[...] = _dot(a_ref[...], b.reshape(-1, b.shape[-1]), 1, 1)

    return _call(name, body, (N // tn, S // tm),
                 [(a, (tm, K), lambda j, i: (i, 0)), b_spec_fn(lambda j, i: j)],
                 [((S, N), F32, (tm, tn), lambda j, i: (i, j))], sem=("parallel", "parallel"))[0]


def weight_grad_rows(a, b, grads, row0, name):
    S, M = a.shape
    N = b.shape[1]
    tmm = 512
    tk = 512
    nk = S // tk

    def body(g_ref, a_ref, b_ref, o_ref, acc_ref):
        k = pl.program_id(1)
        p = _dot(a_ref[...], b_ref[...].astype(BF16), 0, 0)

        @pl.when(k == 0)
        def _():
            acc_ref[...] = p

        @pl.when(k > 0)
        def _():
            acc_ref[...] += p

        @pl.when(k == nk - 1)
        def _():
            o_ref[...] = acc_ref[...].astype(BF16).reshape(tmm // LANES, LANES, N)

    return pl.pallas_call(
        body,
        out_shape=jax.ShapeDtypeStruct(grads.shape, BF16),
        grid=(M // tmm, nk),
        in_specs=[pl.BlockSpec(memory_space=pl.ANY),
                  pl.BlockSpec((tk, tmm), lambda j, k: (k, j)),
                  pl.BlockSpec((tk, N), lambda j, k: (k, 0))],
        out_specs=pl.BlockSpec((tmm // LANES, LANES, N), lambda j, k: (j, row0 // LANES, 0)),
        scratch_shapes=[pltpu.VMEM((tmm, N), F32)],
        input_output_aliases={0: 0},
        name=name,
        compiler_params=pltpu.CompilerParams(dimension_semantics=("parallel", "arbitrary"),
                                             vmem_limit_bytes=VMEM_LIMIT),
    )(grads, a, b)


def weight_grad_plain(a, b, name):
    S, M = a.shape
    N = b.shape[1]
    tk = 512
    nk = S // tk

    def body(a_ref, b_ref, o_ref, acc_ref):
        k = pl.program_id(0)
        p = _dot(a_ref[...], b_ref[...], 0, 0)

        @pl.when(k == 0)
        def _():
            acc_ref[...] = p

        @pl.when(k > 0)
        def _():
            acc_ref[...] += p

        @pl.when(k == nk - 1)
        def _():
            o_ref[...] = acc_ref[...].astype(BF16)

    return _call(name, body, (nk,),
                 [(a, (tk, M), lambda k: (k, 0)), (b, (tk, N), lambda k: (k, 0))],
                 [((M, N), BF16, (M, N), lambda k: (0, 0))],
                 scratch=[pltpu.VMEM((M, N), F32)], sem=("arbitrary",))[0]


def local_step(x, tgt, p, G1, Gm, G2):
    S, D = x.shape
    w_a = Gm[:, MIX_WA:MIX_ROWS, :].reshape(N_DEV, GROUP_WIDTH_A, LANES).transpose(1, 0, 2).reshape(GROUP_WIDTH_A, D)
    buckets = _bucket_tables()
    cos_t, sin_t = _rope_tables(S)
    gains = jnp.concatenate([jnp.tile(p["q_norm"], (1, N_HEADS_B)), jnp.tile(p["k_norm"], (1, N_KV_B)),
                             jnp.ones((1, N_KV_B * LANES), F32)], axis=1)

    n1 = rms_fwd(x, p["ffn1_norm"], "ffn1_norm")
    ab1 = ffn_up(n1, G1, "ffn1_up")
    x1 = ffn_down(ab1, G1, x, "ffn1_down")

    hm = rms_fwd(x1, p["mix_norm"], "mix_norm")
    n_a = A_QKV_WIDTH // PROJ_TN
    proj_a = in_proj(hm, Gm, 0, n_a, BF16, "in_proj_a")
    proj_b = in_proj(hm, Gm, n_a, PB_WIDTH // PROJ_TN, F32, "in_proj_b")

    bias = bias_build(p["rel_bias"], buckets)
    outs, lses = [], []
    for g in range(3):
        o, l = a_fwd(proj_a, bias[g], g, "a_fwd_%d" % g)
        outs.append(o)
        lses.append(l)
    o_a, lse_tot = a_combine(outs, lses, "a_combine")

    qkv = qkv_prep(proj_b, gains, cos_t, sin_t, "qkv_prep")
    k_t = qkv[:, N_HEADS_B * LANES:(N_HEADS_B + N_KV_B) * LANES].T
    o_b, lse_b = flash_fwd(qkv, "flash_fwd")

    merged, ya, yb = merge_fwd(o_a, o_b, w_a, Gm, proj_b, p["b_gate"], "merge_fwd")
    x2 = out_proj(merged, Gm, x1, "out_proj")

    n2 = rms_fwd(x2, p["ffn2_norm"], "ffn2_norm")
    ab2 = ffn_up(n2, G2, "ffn2_up")
    x3 = ffn_down(ab2, G2, x2, "ffn2_down")

    loss, dx3, d_final = final_loss(x3, tgt, p["final_norm"], "final_loss")

    dx2, d_ffn2_norm, gw2 = ffn_bwd(dx3, ab2, n2, G2, x2, p["ffn2_norm"], "ffn2_bwd")

    dya, dyb, dgate, dbg = merge_bwd(dx2, Gm, ya, yb, proj_b, p["b_gate"], "merge_bwd")
    gm_grads = jnp.zeros(Gm.shape, BF16)
    gm_grads = weight_grad_rows(merged, dx2, gm_grads, MIX_WOUT, "dw_out")
    gm_grads = weight_grad_rows(o_b, dyb, gm_grads, MIX_WB, "dw_branch_b")
    dw_a = weight_grad_plain(o_a, dya, "dw_branch_a")
    do_a = matmul_nt(dya, lambda jm: (w_a, (MERGE_TN, D), lambda j, i: (jm(j, i), 0)), GROUP_WIDTH_A, "do_a")
    do_b = matmul_nt(dyb, lambda jm: _mix_rows_spec(Gm, MIX_WB, MERGE_TN // LANES, jm), N_HEADS_B * LANES, "do_b")

    dq_r, dk_r, dv_b = flash_bwd(qkv, k_t, do_b, o_b, lse_b, "flash_bwd")
    dq_b, d_q_norm = qk_prep_bwd(dq_r, proj_b, 0, p["q_norm"], cos_t, sin_t, "q_prep_bwd")
    dk_b, d_k_norm = qk_prep_bwd(dk_r, proj_b, N_HEADS_B, p["k_norm"], cos_t, sin_t, "k_prep_bwd")

    dqs, dks, dvs, dbs = [], [], [], []
    for g in range(3):
        dq, dk, dv, db = a_bwd(proj_a, bias[g], do_a, o_a, lse_tot, g, "a_bwd_%d" % g)
        dqs.append(dq)
        dks.append(dk)
        dvs.append(dv)
        dbs.append(db)
    d_rel_bias = bias_bwd(jnp.stack(dbs, axis=0), buckets)

    dproj = jnp.concatenate(dqs + dks + dvs + [dq_b, dk_b, dv_b, dgate[0], dgate[1]], axis=1)
    dx1, d_mix_norm, gm_grads = in_proj_bwd(dproj, hm, Gm, x1, p["mix_norm"], dx2, gm_grads, "in_proj_bwd")
    dw_a_sh = dw_a.reshape(GROUP_WIDTH_A, N_DEV, LANES).transpose(1, 0, 2).reshape(N_DEV, MIX_ROWS - MIX_WA, D)
    gm_grads = lax.dynamic_update_slice(gm_grads, dw_a_sh, (0, MIX_WA, 0))

    dx0, d_ffn1_norm, gw1 = ffn_bwd(dx1, ab1, n1, G1, x, p["ffn1_norm"], "ffn1_bwd")

    small = dict(ffn1_norm=d_ffn1_norm, mix_norm=d_mix_norm, b_gate=dbg.reshape(1, 2 * D),
                 q_norm=d_q_norm, k_norm=d_k_norm, rel_bias=d_rel_bias, ffn2_norm=d_ffn2_norm,
                 final_norm=d_final)
    return loss, dx0, small, gw1, gm_grads, gw2


def _pack_small(t, loss_row):
    row6 = jnp.concatenate([t["q_norm"].reshape(1, -1), t["k_norm"].reshape(1, -1), t["rel_bias"].reshape(1, -1)], axis=1)
    return jnp.concatenate([t["ffn1_norm"].reshape(1, -1), t["mix_norm"].reshape(1, -1), t["b_gate"].reshape(2, -1),
                            t["ffn2_norm"].reshape(1, -1), t["final_norm"].reshape(1, -1), row6, loss_row], axis=0)


def _unpack_small(a, shapes):
    return dict(ffn1_norm=a[0:1].reshape(shapes["ffn1_norm"]), mix_norm=a[1:2].reshape(shapes["mix_norm"]),
                b_gate=a[2:4].reshape(shapes["b_gate"]), ffn2_norm=a[4:5].reshape(shapes["ffn2_norm"]),
                final_norm=a[5].reshape(shapes["final_norm"]), q_norm=a[6:7, 0:128].reshape(shapes["q_norm"]),
                k_norm=a[6:7, 128:256].reshape(shapes["k_norm"]), rel_bias=a[6, 256:1024].reshape(shapes["rel_bias"]))


SMALL = ("ffn1_norm", "mix_norm", "b_gate", "q_norm", "k_norm", "rel_bias", "ffn2_norm", "final_norm")
ORDER = ("ffn1_norm", "ffn1_w1", "ffn1_w3", "ffn1_w2", "mix_norm", "w_in", "b_gate", "q_norm", "k_norm", "rel_bias",
         "w_branch_a", "w_branch_b", "w_out", "ffn2_norm", "ffn2_w1", "ffn2_w3", "ffn2_w2", "final_norm")


def kernel(x, ffn1_norm, ffn1_w1, ffn1_w3, ffn1_w2, mix_norm, w_in, b_gate, q_norm, k_norm, rel_bias, w_branch_a, w_branch_b, w_out, ffn2_norm, ffn2_w1, ffn2_w3, ffn2_w2, final_norm, loss_target, m_ffn1_norm, m_ffn1_w1, m_ffn1_w3, m_ffn1_w2, m_mix_norm, m_w_in, m_b_gate, m_q_norm, m_k_norm, m_rel_bias, m_w_branch_a, m_w_branch_b, m_w_out, m_ffn2_norm, m_ffn2_w1, m_ffn2_w3, m_ffn2_w2, m_final_norm, v_ffn1_norm, v_ffn1_w1, v_ffn1_w3, v_ffn1_w2, v_mix_norm, v_w_in, v_b_gate, v_q_norm, v_k_norm, v_rel_bias, v_w_branch_a, v_w_branch_b, v_w_out, v_ffn2_norm, v_ffn2_w1, v_ffn2_w3, v_ffn2_w2, v_final_norm):
    args = dict(locals())
    w = {n: args[n] for n in ORDER}
    m = {n: args["m_" + n] for n in ORDER}
    v = {n: args["v_" + n] for n in ORDER}
    D = x.shape[2]

    def ffn_group(w1, w3, w2):
        return jnp.concatenate([w1[0].T, w3[0].T, w2[0]], axis=0).astype(BF16)

    g1 = ffn_group(ffn1_w1, ffn1_w3, ffn1_w2)
    g2 = ffn_group(ffn2_w1, ffn2_w3, ffn2_w2)
    gm = jnp.concatenate([w_in[0], w_branch_b[0], w_out[0], w_branch_a[0].reshape(MIX_ROWS - MIX_WA, D)],
                         axis=0).astype(BF16)
    G1, Gm, G2 = all_gather_groups([g1, gm, g2])

    small_p = dict(ffn1_norm=ffn1_norm, mix_norm=mix_norm, b_gate=b_gate, q_norm=q_norm, k_norm=k_norm,
                   rel_bias=rel_bias, ffn2_norm=ffn2_norm, final_norm=final_norm.reshape(1, D))
    loss_p, grad_x, small_g, gw1, gwm, gw2 = local_step(x[0], loss_target[0], small_p, G1, Gm, G2)

    theirs = reduce_scatter_pair([gw1, gwm, gw2])
    core = lax.axis_index("c").astype(jnp.int32).reshape(1)
    parts = [pair_add(a, b, core, "pair_add_%d" % i) for i, (a, b) in enumerate(zip([gw1, gwm, gw2], theirs))]
    loss_row = jnp.pad(loss_p, ((0, 0), (0, D - LANES)))
    recv, smalls = reduce_scatter_chips(parts, _pack_small(small_g, loss_row))
    r1, rm, r2 = recv

    grads = {}
    for tag, r in (("ffn1", r1), ("ffn2", r2)):
        grads[tag + "_w1"] = sum_slots(r, 0, FFN_SHARD, FFN_SHARD, tag + "_w1_sum").T[None]
        grads[tag + "_w3"] = sum_slots(r, FFN_SHARD, FFN_SHARD, FFN_SHARD, tag + "_w3_sum").T[None]
        grads[tag + "_w2"] = sum_slots(r, 2 * FFN_SHARD, FFN_SHARD, FFN_SHARD, tag + "_w2_sum")[None]
    grads["w_in"] = sum_slots(rm, MIX_WIN, MIX_WB - MIX_WIN, LANES, "w_in_sum")[None]
    grads["w_branch_b"] = sum_slots(rm, MIX_WB, LANES, LANES, "w_branch_b_sum")[None]
    grads["w_out"] = sum_slots(rm, MIX_WOUT, LANES, LANES, "w_out_sum")[None]
    grads["w_branch_a"] = sum_slots(rm, MIX_WA, MIX_ROWS - MIX_WA, MIX_ROWS - MIX_WA,
                                    "w_branch_a_sum").reshape(w_branch_a.shape)
    small_sum = sum_slots(smalls, 0, N_DEV, N_DEV, "small_sum")
    small_shapes = {n: w[n].shape for n in SMALL}
    grads.update(_unpack_small(small_sum, small_shapes))
    loss = small_sum[7, 0]

    delta, new_m, new_v = {}, {}, {}
    for n in ORDER:
        if n in SMALL:
            continue
        shp = w[n].shape
        two_d = lambda a: a.reshape(shp[-2], shp[-1])
        d_, m_, v_ = adamw(two_d(w[n]), two_d(grads[n]), two_d(m[n]), two_d(v[n]), "adamw_" + n)
        delta[n], new_m[n], new_v[n] = d_.reshape(shp), m_.reshape(shp), v_.reshape(shp)
    zero_row = jnp.zeros((1, D), F32)
    pack = lambda t: _pack_small({n: t[n] for n in SMALL}, zero_row)
    d_, m_, v_ = adamw(pack(w), small_sum, pack(m), pack(v), "adamw_small")
    for src, dst in ((d_, delta), (m_, new_m), (v_, new_v)):
        dst.update(_unpack_small(src, small_shapes))

    return (loss, grad_x[None], *[grads[n] for n in ORDER], *[delta[n] for n in ORDER],
            *[new_m[n] for n in ORDER], *[new_v[n] for n in ORDER])
```

```python
import math

import jax
import jax.numpy as jnp
from jax import lax
from jax.experimental import pallas as pl
from jax.experimental.pallas import tpu as pltpu

F32 = jnp.float32
BF16 = jnp.bfloat16
MESH = pl.DeviceIdType.MESH

V7X_VMEM_BYTES = 64 * 1024 * 1024
VMEM_LIMIT = V7X_VMEM_BYTES - 8 * 1024 * 1024
LANES = 128

N_DEV = 8
EPS = 1e-6
NEG_INF = -1e30

DILATIONS = (1, 4, 16)
HALF_WINDOW = 64
HEAD_DIM_A = 64
HEADS_PER_GROUP_A = 8
GROUP_WIDTH_A = 512
A_QKV_WIDTH = 4608
A_BLOCKS_PER_TOKEN = A_QKV_WIDTH // LANES
A_TQ = 128
A_WIN = A_TQ + 2 * HALF_WINDOW
A_UNROLL = 4
WGRAD_TK = 2048
HEAD_DIM_B = 128
N_HEADS_B = 8
N_KV_B = 2
GQA_GROUP_B = 4
GRID_W = 64
ROPE_THETA = 10000.0
B_TQ_FWD = 256
B_TQ_BWD = 512
N_BUCKETS = 32
MAX_DISTANCE = 1024
PB_WIDTH = 3584
PB_GATE_A = 1536
PB_GATE_B = 2560

ADAM_LR = 0.001
ADAM_B1 = 0.9
ADAM_B2 = 0.999
ADAM_EPS = 1e-08
ADAM_WD = 0.01
ADAM_STEP = 10

FFN_SHARD = 352
MIX_WIN, MIX_WB, MIX_WOUT, MIX_WA = 0, 1024, 1152, 1280
MIX_ROWS = 1344


def _dot(a, b, ca=1, cb=0):
    return lax.dot_general(a, b, (((ca,), (cb,)), ((), ())), preferred_element_type=F32)


def _call(name, body, grid, ins, outs, scratch=(), sem=None, aliases=None):
    res = pl.pallas_call(
        body,
        out_shape=[jax.ShapeDtypeStruct(s, d) for (s, d, _, _) in outs],
        grid=grid,
        in_specs=[pl.BlockSpec(bs, im) for (_, bs, im) in ins],
        out_specs=[pl.BlockSpec(bs, im) for (_, _, bs, im) in outs],
        scratch_shapes=list(scratch),
        name=name,
        input_output_aliases=aliases or {},
        compiler_params=pltpu.CompilerParams(dimension_semantics=sem, vmem_limit_bytes=VMEM_LIMIT),
    )(*[a for (a, _, _) in ins])
    return res


def _sigmoid(x):
    return 1.0 / (1.0 + jnp.exp(-x))


def _position():
    return lax.axis_index("x"), lax.axis_index("y"), lax.axis_index("c")


def _hbm_specs(n):
    return [pl.BlockSpec(memory_space=pl.ANY) for _ in range(n)]


def all_gather_groups(groups):
    n = len(groups)

    def body(*refs):
        ins, outs = refs[:n], refs[n:2 * n]
        stage = refs[2 * n:3 * n]
        send_sems, recv_sems, local_sems = refs[3 * n:]
        x, y, c = _position()
        sibling = (x, y, 1 - c)
        chips = [(1 - x, y), (x, 1 - y), (1 - x, 1 - y)]

        def copy(i, k, block, to, src=None):
            px, py, pc = block
            dst = outs[i].at[4 * px + 2 * py + pc]
            return pltpu.make_async_remote_copy(
                src_ref=dst if src is None else src, dst_ref=dst,
                send_sem=send_sems.at[i, k], recv_sem=recv_sems.at[i, k],
                device_id=to, device_id_type=MESH)

        loads = [pltpu.make_async_copy(ins[i], stage[i], local_sems.at[i, 0]) for i in range(n)]
        for ld in loads:
            ld.start()
        sends, stores = [], []
        for i in range(n):
            loads[i].wait()
            first = [copy(i, 0, (x, y, c), sibling, src=stage[i])]
            first += [copy(i, 1 + j, (x, y, c), (*chip, c), src=stage[i]) for j, chip in enumerate(chips)]
            for cp in first:
                cp.start()
            sends += first
            st = pltpu.make_async_copy(stage[i], outs[i].at[4 * x + 2 * y + c], local_sems.at[i, 1])
            st.start()
            stores.append(st)
        for i in range(n):
            for j, chip in enumerate(chips):
                copy(i, 1 + j, (*chip, c), (x, y, c)).wait_recv()
                passed = copy(i, 4 + j, (*chip, c), sibling)
                passed.start()
                sends.append(passed)
        for i in range(n):
            copy(i, 0, sibling, (x, y, c)).wait_recv()
            for j, chip in enumerate(chips):
                copy(i, 4 + j, (*chip, 1 - c), (x, y, c)).wait_recv()
        for cp in sends:
            cp.wait_send()
        for st in stores:
            st.wait()

    return pl.pallas_call(
        body,
        out_shape=[jax.ShapeDtypeStruct((N_DEV,) + g.shape, g.dtype) for g in groups],
        in_specs=_hbm_specs(n),
        out_specs=_hbm_specs(n),
        scratch_shapes=[pltpu.VMEM(g.shape, g.dtype) for g in groups]
        + [pltpu.SemaphoreType.DMA((n, 7)), pltpu.SemaphoreType.DMA((n, 7)), pltpu.SemaphoreType.DMA((n, 2))],
        name="all_gather_weights",
        compiler_params=pltpu.CompilerParams(vmem_limit_bytes=VMEM_LIMIT),
    )(*groups)


PAIR_BUFFERS = 4


def reduce_scatter_pair(grads):
    n = len(grads)
    C = grads[0].shape[2]
    half = [g.shape[1] // 2 for g in grads]
    chunks = [(i, q, hf) for i in range(n) for q in range(4) for hf in range(2)]
    nb = PAIR_BUFFERS

    def body(*refs):
        ins, theirs = refs[:n], refs[n:2 * n]
        buf, load_sems, send_sems, recv_sems = refs[2 * n:]
        x, y, c = _position()
        sibling = (x, y, 1 - c)

        def load(k):
            i, q, hf = chunks[k]
            r = half[i]
            return pltpu.make_async_copy(ins[i].at[2 * q + (1 - c), pl.ds(hf * r, r), :],
                                         buf.at[k % nb, pl.ds(0, r), :], load_sems.at[k % nb])

        def send(k):
            i, q, hf = chunks[k]
            r = half[i]
            return pltpu.make_async_remote_copy(
                src_ref=buf.at[k % nb, pl.ds(0, r), :], dst_ref=theirs[i].at[q, pl.ds(hf * r, r), :],
                send_sem=send_sems.at[k % nb], recv_sem=recv_sems.at[i],
                device_id=sibling, device_id_type=MESH)

        for k in range(len(chunks) + 1):
            if k < len(chunks):
                if k >= nb:
                    send(k - nb).wait_send()
                load(k).start()
            if k >= 1:
                load(k - 1).wait()
                send(k - 1).start()
        for k in range(max(0, len(chunks) - nb), len(chunks)):
            send(k).wait_send()
        for i in range(n):
            pltpu.make_async_remote_copy(
                src_ref=theirs[i], dst_ref=theirs[i], send_sem=send_sems.at[0], recv_sem=recv_sems.at[i],
                device_id=sibling, device_id_type=MESH).wait_recv()

    return pl.pallas_call(
        body,
        out_shape=[jax.ShapeDtypeStruct((4,) + g.shape[1:], g.dtype) for g in grads],
        in_specs=_hbm_specs(n),
        out_specs=_hbm_specs(n),
        scratch_shapes=[pltpu.VMEM((nb, max(half), C), grads[0].dtype), pltpu.SemaphoreType.DMA((nb,)),
                        pltpu.SemaphoreType.DMA((nb,)), pltpu.SemaphoreType.DMA((n,))],
        name="reduce_scatter_pair",
        compiler_params=pltpu.CompilerParams(vmem_limit_bytes=VMEM_LIMIT),
    )(*grads)


def reduce_scatter_chips(parts, small):
    n = len(parts)

    def body(*refs):
        ins, small_ref = refs[:n], refs[n]
        outs, smalls = refs[n + 1:2 * n + 1], refs[2 * n + 1]
        stage = refs[2 * n + 2:3 * n + 2]
        send_sems, recv_sems, local_sems, s_send, s_recv, s_local = refs[3 * n + 2:]
        x, y, c = _position()
        chip = 2 * x + y
        me = 4 * x + 2 * y + c
        others = [(1 - x, y), (x, 1 - y), (1 - x, 1 - y)]
        remote, local = [], []
        loads = [pltpu.make_async_copy(ins[i].at[chip], stage[i], local_sems.at[i, 0]) for i in range(n)]
        for ld in loads:
            ld.start()
        for i in range(n):
            for j, (px, py) in enumerate(others):
                rc = pltpu.make_async_remote_copy(
                    src_ref=ins[i].at[2 * px + py], dst_ref=outs[i].at[chip],
                    send_sem=send_sems.at[i, j], recv_sem=recv_sems.at[i, j],
                    device_id=(px, py, c), device_id_type=MESH)
                rc.start()
                remote.append(rc)
        for i in range(n):
            loads[i].wait()
            st = pltpu.make_async_copy(stage[i], outs[i].at[chip], local_sems.at[i, 1])
            st.start()
            local.append(st)
        lc = pltpu.make_async_copy(small_ref, smalls.at[me], s_local)
        lc.start()
        local.append(lc)
        k = 0
        for dx in (0, 1):
            for dy in (0, 1):
                for dc in (0, 1):
                    if dx + dy + dc == 0:
                        continue
                    peer = (1 - x if dx else x, 1 - y if dy else y, 1 - c if dc else c)
                    rc = pltpu.make_async_remote_copy(
                        src_ref=small_ref, dst_ref=smalls.at[me],
                        send_sem=s_send.at[k], recv_sem=s_recv.at[k],
                        device_id=peer, device_id_type=MESH)
                    rc.start()
                    remote.append(rc)
                    k += 1
        for rc in remote:
            rc.wait()
        for lc in local:
            lc.wait()

    res = pl.pallas_call(
        body,
        out_shape=[jax.ShapeDtypeStruct(p.shape, p.dtype) for p in parts]
        + [jax.ShapeDtypeStruct((N_DEV,) + small.shape, small.dtype)],
        in_specs=_hbm_specs(n + 1),
        out_specs=_hbm_specs(n + 1),
        scratch_shapes=[pltpu.VMEM(p.shape[1:], p.dtype) for p in parts]
        + [pltpu.SemaphoreType.DMA((n, 3)), pltpu.SemaphoreType.DMA((n, 3)),
           pltpu.SemaphoreType.DMA((n, 2)), pltpu.SemaphoreType.DMA((7,)),
           pltpu.SemaphoreType.DMA((7,)), pltpu.SemaphoreType.DMA],
        name="reduce_scatter_chips",
        compiler_params=pltpu.CompilerParams(vmem_limit_bytes=VMEM_LIMIT),
    )(*parts, small)
    return res[:n], res[n]


def pair_add(grads, theirs, core, name):
    _, R, C = theirs.shape
    tr = R // 2

    def body(c_ref, a_ref, b_ref, o_ref):
        o_ref[...] = (a_ref[...].astype(F32) + b_ref[...].astype(F32)).astype(BF16)

    return pl.pallas_call(
        body,
        out_shape=jax.ShapeDtypeStruct(theirs.shape, BF16),
        grid_spec=pltpu.PrefetchScalarGridSpec(
            num_scalar_prefetch=1, grid=(4, R // tr),
            in_specs=[pl.BlockSpec((None, tr, C), lambda q, i, c: (2 * q + c[0], i, 0)),
                      pl.BlockSpec((None, tr, C), lambda q, i, c: (q, i, 0))],
            out_specs=pl.BlockSpec((None, tr, C), lambda q, i, c: (q, i, 0))),
        name=name,
        compiler_params=pltpu.CompilerParams(dimension_semantics=("parallel", "parallel"),
                                             vmem_limit_bytes=VMEM_LIMIT),
    )(core, grads, theirs)


def sum_slots(recv, off, rows, blk, name):
    nq, _, C = recv.shape
    ob = off // blk

    def body(r_ref, o_ref):
        acc = r_ref[0].astype(F32)
        for q in range(1, nq):
            acc = acc + r_ref[q].astype(F32)
        o_ref[...] = acc

    return _call(name, body, (rows // blk,),
                 [(recv, (nq, blk, C), lambda i: (0, ob + i, 0))],
                 [((rows, C), F32, (blk, C), lambda i: (i, 0))], sem=("parallel",))[0]


def adamw(w, g, m, v, name):
    R, C = w.shape
    tr = R
    for cand in (256, 128, 64, 32, 16, 8):
        if R % cand == 0 and R > cand:
            tr = cand
            break
    c1 = 1.0 / (1.0 - ADAM_B1 ** ADAM_STEP)
    c2 = 1.0 / (1.0 - ADAM_B2 ** ADAM_STEP)

    def body(w_ref, g_ref, m_ref, v_ref, d_ref, nm_ref, nv_ref):
        gv = g_ref[...]
        nm = ADAM_B1 * m_ref[...] + (1.0 - ADAM_B1) * gv
        nv = ADAM_B2 * v_ref[...] + (1.0 - ADAM_B2) * (gv * gv)
        d_ref[...] = -ADAM_LR * ((nm * c1) / (jnp.sqrt(nv * c2) + ADAM_EPS) + ADAM_WD * w_ref[...])
        nm_ref[...] = nm
        nv_ref[...] = nv

    spec = ((tr, C), lambda i: (i, 0))
    out = ((R, C), F32) + spec
    return _call(name, body, (R // tr,), [(w,) + spec, (g,) + spec, (m,) + spec, (v,) + spec],
                 [out, out, out], sem=("parallel",))


def rms_fwd(x, g, name):
    S, D = x.shape
    tr = 512

    def body(x_ref, g_ref, o_ref):
        xv = x_ref[...]
        r = lax.rsqrt(jnp.mean(xv * xv, axis=-1, keepdims=True) + EPS)
        o_ref[...] = (xv * r * g_ref[...]).astype(BF16)

    return _call(name, body, (S // tr,),
                 [(x, (tr, D), lambda i: (i, 0)), (g, (1, D), lambda i: (0, 0))],
                 [((S, D), BF16, (tr, D), lambda i: (i, 0))], sem=("parallel",))[0]


def _rms_bwd_tile(dn, xv, gv):
    r = lax.rsqrt(jnp.mean(xv * xv, axis=-1, keepdims=True) + EPS)
    xh = xv * r
    dxh = dn * gv
    dx = r * (dxh - xh * jnp.mean(dxh * xh, axis=-1, keepdims=True))
    return dx, dn * xh


def final_loss(x, tgt, g, name):
    S, D = x.shape
    tr = 256

    def body(x_ref, t_ref, g_ref, l_ref, dx_ref, dg_ref):
        i = pl.program_id(0)
        xv, gv = x_ref[...], g_ref[...]
        r = lax.rsqrt(jnp.mean(xv * xv, axis=-1, keepdims=True) + EPS)
        xh = xv * r
        e = xh * gv - t_ref[...]
        part = 0.5 * jnp.sum(jnp.sum(e * e, axis=-1, keepdims=True) * (1.0 / D), axis=0, keepdims=True)
        dy = e * (1.0 / D)
        dxh = dy * gv
        dx_ref[...] = r * (dxh - xh * jnp.mean(dxh * xh, axis=-1, keepdims=True))
        dgp = jnp.sum(dy * xh, axis=0, keepdims=True)

        @pl.when(i == 0)
        def _():
            l_ref[...] = jnp.broadcast_to(part, l_ref.shape)
            dg_ref[...] = dgp

        @pl.when(i > 0)
        def _():
            l_ref[...] += jnp.broadcast_to(part, l_ref.shape)
            dg_ref[...] += dgp

    row = ((tr, D), lambda i: (i, 0))
    return _call(name, body, (S // tr,),
                 [(x,) + row, (tgt,) + row, (g, (1, D), lambda i: (0, 0))],
                 [((1, LANES), F32, (1, LANES), lambda i: (0, 0)), ((S, D), F32) + row,
                  ((1, D), F32, (1, D), lambda i: (0, 0))], sem=("arbitrary",))


FFN_TF = 4 * FFN_SHARD


def _ffn_w_spec(G, which, imap):
    D = G.shape[2]
    return (G, (4, FFN_SHARD, D), lambda *idx: (imap(*idx), which, 0))


def ffn_up(n, G, name):
    S, D = n.shape
    F = N_DEV * FFN_SHARD
    tm = 512

    def body(n_ref, w1_ref, w3_ref, ab_ref):
        nv = n_ref[...]
        ab_ref[0] = _dot(nv, w1_ref[...].reshape(FFN_TF, D), 1, 1).astype(BF16)
        ab_ref[1] = _dot(nv, w3_ref[...].reshape(FFN_TF, D), 1, 1).astype(BF16)

    return _call(name, body, (F // FFN_TF, S // tm),
                 [(n, (tm, D), lambda j, i: (i, 0)),
                  _ffn_w_spec(G, 0, lambda j, i: j), _ffn_w_spec(G, 1, lambda j, i: j)],
                 [((2, S, F), BF16, (2, tm, FFN_TF), lambda j, i: (0, i, j))],
                 sem=("parallel", "parallel"))[0]


def ffn_down(ab, G, x, name):
    _, S, F = ab.shape
    D = x.shape[1]
    tm = 512
    nk = F // FFN_TF

    def body(ab_ref, w2_ref, x_ref, o_ref, acc_ref):
        k = pl.program_id(1)
        av, bv = ab_ref[0].astype(F32), ab_ref[1].astype(F32)
        h = (av * _sigmoid(av) * bv).astype(BF16)
        p = _dot(h, w2_ref[...].reshape(FFN_TF, D))

        @pl.when(k == 0)
        def _():
            acc_ref[...] = p

        @pl.when(k > 0)
        def _():
            acc_ref[...] += p

        @pl.when(k == nk - 1)
        def _():
            o_ref[...] = x_ref[...] + 0.5 * acc_ref[...]

    return _call(name, body, (S // tm, nk),
                 [(ab, (2, tm, FFN_TF), lambda i, k: (0, i, k)), _ffn_w_spec(G, 2, lambda i, k: k),
                  (x, (tm, D), lambda i, k: (i, 0))],
                 [((S, D), F32, (tm, D), lambda i, k: (i, 0))],
                 scratch=[pltpu.VMEM((tm, D), F32)], sem=("parallel", "arbitrary"))[0]


def ffn_bwd(dxo, ab, n, G, x_in, g, name):
    _, S, F = ab.shape
    D = x_in.shape[1]
    tm = 512
    nf = F // FFN_TF

    def down_body(d_ref, w2_ref, ab_ref, o_ref):
        dh = 0.5 * _dot(d_ref[...].astype(BF16), w2_ref[...].reshape(FFN_TF, D), 1, 1)
        av, bv = ab_ref[0].astype(F32), ab_ref[1].astype(F32)
        sig = _sigmoid(av)
        silu = av * sig
        o_ref[0] = (dh * bv * (sig * (1.0 + av * (1.0 - sig)))).astype(BF16)
        o_ref[1] = (dh * silu).astype(BF16)
        o_ref[2] = (silu * bv).astype(BF16)

    dabh = _call(name + "_down_bwd", down_body, (nf, S // tm),
                 [(dxo, (tm, D), lambda j, i: (i, 0)), _ffn_w_spec(G, 2, lambda j, i: j),
                  (ab, (2, tm, FFN_TF), lambda j, i: (0, i, j))],
                 [((3, S, F), BF16, (3, tm, FFN_TF), lambda j, i: (0, i, j))],
                 sem=("parallel", "parallel"))[0]

    tk = WGRAD_TK
    nk = S // tk
    gshape = (N_DEV, 3 * FFN_SHARD, D)

    def dw2_body(h_ref, d_ref, o_ref, acc_ref):
        k = pl.program_id(1)
        p = _dot(h_ref[...], d_ref[...].astype(BF16), 0, 0)

        @pl.when(k == 0)
        def _():
            acc_ref[...] = p

        @pl.when(k > 0)
        def _():
            acc_ref[...] += p

        @pl.when(k == nk - 1)
        def _():
            o_ref[...] = (0.5 * acc_ref[...]).astype(BF16).reshape(4, FFN_SHARD, D)

    gw = _call(name + "_dw2", dw2_body, (nf, nk),
               [(dabh, (None, tk, FFN_TF), lambda j, k: (2, k, j)), (dxo, (tk, D), lambda j, k: (k, 0))],
               [(gshape, BF16, (4, FFN_SHARD, D), lambda j, k: (j, 2, 0))],
               scratch=[pltpu.VMEM((FFN_TF, D), F32)], sem=("parallel", "arbitrary"))[0]

    def dw13_body(gw_ref, dab_ref, n_ref, o_ref, acc_ref):
        k = pl.program_id(2)
        p = _dot(dab_ref[...], n_ref[...], 0, 0)

        @pl.when(k == 0)
        def _():
            acc_ref[...] = p

        @pl.when(k > 0)
        def _():
            acc_ref[...] += p

        @pl.when(k == nk - 1)
        def _():
            o_ref[...] = acc_ref[...].astype(BF16).reshape(4, FFN_SHARD, D)

    gw = pl.pallas_call(
        dw13_body,
        out_shape=jax.ShapeDtypeStruct(gshape, BF16),
        grid=(2, nf, nk),
        in_specs=[pl.BlockSpec(memory_space=pl.ANY),
                  pl.BlockSpec((None, tk, FFN_TF), lambda w, j, k: (w, k, j)),
                  pl.BlockSpec((tk, D), lambda w, j, k: (k, 0))],
        out_specs=pl.BlockSpec((4, FFN_SHARD, D), lambda w, j, k: (j, w, 0)),
        scratch_shapes=[pltpu.VMEM((FFN_TF, D), F32)],
        input_output_aliases={0: 0},
        name=name + "_dw13",
        compiler_params=pltpu.CompilerParams(dimension_semantics=("parallel", "parallel", "arbitrary"),
                                             vmem_limit_bytes=VMEM_LIMIT),
    )(gw, dabh, n)

    def dn_body(dab_ref, w1_ref, w3_ref, x_ref, d_ref, g_ref, dx_ref, dg_ref, acc_ref):
        i, k = pl.program_id(0), pl.program_id(1)
        p = _dot(dab_ref[0], w1_ref[...].reshape(FFN_TF, D)) + _dot(dab_ref[1], w3_ref[...].reshape(FFN_TF, D))

        @pl.when(k == 0)
        def _():
            acc_ref[...] = p

        @pl.when(k > 0)
        def _():
            acc_ref[...] += p

        @pl.when(k == nf - 1)
        def _():
            dx, dgt = _rms_bwd_tile(acc_ref[...], x_ref[...], g_ref[...])
            dx_ref[...] = d_ref[...] + dx
            dgp = jnp.sum(dgt, axis=0, keepdims=True)

            @pl.when(i == 0)
            def _():
                dg_ref[...] = dgp

            @pl.when(i > 0)
            def _():
                dg_ref[...] += dgp

    dx, dg = _call(name + "_dn", dn_body, (S // tm, nf),
                   [(dabh, (2, tm, FFN_TF), lambda i, k: (0, i, k)),
                    _ffn_w_spec(G, 0, lambda i, k: k), _ffn_w_spec(G, 1, lambda i, k: k),
                    (x_in, (tm, D), lambda i, k: (i, 0)), (dxo, (tm, D), lambda i, k: (i, 0)),
                    (g, (1, D), lambda i, k: (0, 0))],
                   [((S, D), F32, (tm, D), lambda i, k: (i, 0)), ((1, D), F32, (1, D), lambda i, k: (0, 0))],
                   scratch=[pltpu.VMEM((tm, D), F32)], sem=("arbitrary", "arbitrary"))
    return dx, dg, gw


PROJ_TN = 512


def in_proj(h, Gm, first_tile, n_tiles, dtype, name):
    S, D = h.shape
    tm = 1024

    def body(h_ref, w_ref, o_ref):
        o_ref[...] = _dot(h_ref[...], w_ref[...]).astype(dtype)

    return _call(name, body, (n_tiles, S // tm),
                 [(h, (tm, D), lambda j, i: (i, 0)),
                  (Gm, (None, D, PROJ_TN), lambda j, i: ((first_tile + j) // 2, 0, (first_tile + j) % 2))],
                 [((S, n_tiles * PROJ_TN), dtype, (tm, PROJ_TN), lambda j, i: (i, j))],
                 sem=("parallel", "parallel"))[0]


def in_proj_bwd(dproj, h, Gm, x_in, g, dres, gm_grads, name):
    S, D = h.shape
    NT = dproj.shape[1] // PROJ_TN
    tk = WGRAD_TK
    nk = S // tk

    def dw_body(gm_ref, h_ref, d_ref, o_ref, acc_ref):
        k = pl.program_id(1)
        p = _dot(h_ref[...], d_ref[...], 0, 0)

        @pl.when(k == 0)
        def _():
            acc_ref[...] = p

        @pl.when(k > 0)
        def _():
            acc_ref[...] += p

        @pl.when(k == nk - 1)
        def _():
            o_ref[...] = acc_ref[...].astype(BF16)

    gm_grads = pl.pallas_call(
        dw_body,
        out_shape=jax.ShapeDtypeStruct(gm_grads.shape, BF16),
        grid=(NT, nk),
        in_specs=[pl.BlockSpec(memory_space=pl.ANY),
                  pl.BlockSpec((tk, D), lambda j, k: (k, 0)),
                  pl.BlockSpec((tk, PROJ_TN), lambda j, k: (k, j))],
        out_specs=pl.BlockSpec((None, D, PROJ_TN), lambda j, k: (j // 2, 0, j % 2)),
        scratch_shapes=[pltpu.VMEM((D, PROJ_TN), F32)],
        input_output_aliases={0: 0},
        name=name + "_dw",
        compiler_params=pltpu.CompilerParams(dimension_semantics=("parallel", "arbitrary"),
                                             vmem_limit_bytes=VMEM_LIMIT),
    )(gm_grads, h, dproj)

    tm = 512
    C = Gm.shape[2]
    n_sh = dproj.shape[1] // C

    def dh_body(d_ref, w_ref, x_ref, r_ref, g_ref, dx_ref, dg_ref, acc_ref):
        i, k = pl.program_id(0), pl.program_id(1)
        p = _dot(d_ref[...], w_ref[...], 1, 1)

        @pl.when(k == 0)
        def _():
            acc_ref[...] = p

        @pl.when(k > 0)
        def _():
            acc_ref[...] += p

        @pl.when(k == n_sh - 1)
        def _():
            dx, dgt = _rms_bwd_tile(acc_ref[...], x_ref[...], g_ref[...])
            dx_ref[...] = r_ref[...] + dx
            dgp = jnp.sum(dgt, axis=0, keepdims=True)

            @pl.when(i == 0)
            def _():
                dg_ref[...] = dgp

            @pl.when(i > 0)
            def _():
                dg_ref[...] += dgp

    dx, dg = _call(name + "_dh", dh_body, (S // tm, n_sh),
                   [(dproj, (tm, C), lambda i, k: (i, k)),
                    (Gm, (None, D, C), lambda i, k: (k, 0, 0)),
                    (x_in, (tm, D), lambda i, k: (i, 0)), (dres, (tm, D), lambda i, k: (i, 0)),
                    (g, (1, D), lambda i, k: (0, 0))],
                   [((S, D), F32, (tm, D), lambda i, k: (i, 0)), ((1, D), F32, (1, D), lambda i, k: (0, 0))],
                   scratch=[pltpu.VMEM((tm, D), F32)], sem=("arbitrary", "arbitrary"))
    return dx, dg, gm_grads


def _t5_bucket(rel):
    n = N_BUCKETS // 2
    max_exact = n // 2
    ret = jnp.where(rel > 0, n, 0)
    a = jnp.abs(rel)
    af = jnp.maximum(a, 1).astype(F32)
    large = max_exact + (jnp.log(af / max_exact) / math.log(MAX_DISTANCE / max_exact)
                         * (n - max_exact)).astype(jnp.int32)
    large = jnp.minimum(large, n - 1)
    return ret + jnp.where(a < max_exact, a, large)


def _bucket_tables():
    qi = jnp.arange(A_TQ, dtype=jnp.int32)[:, None]
    kj = jnp.arange(A_WIN, dtype=jnp.int32)[None, :]
    rel = kj - HALF_WINDOW - qi
    return jnp.stack([_t5_bucket(rel * d) for d in DILATIONS], axis=0)


def bias_build(rel_bias, buckets):
    def body(tab_ref, bk_ref, o_ref):
        col = pl.program_id(0) * HEADS_PER_GROUP_A + pl.program_id(1)
        bk = bk_ref[...]
        acc = jnp.zeros(bk.shape, F32)
        for b in range(N_BUCKETS):
            acc = jnp.where(bk == b, tab_ref[b, col], acc)
        qi = lax.broadcasted_iota(jnp.int32, bk.shape, 0)
        kj = lax.broadcasted_iota(jnp.int32, bk.shape, 1)
        o_ref[...] = jnp.where(jnp.abs(kj - HALF_WINDOW - qi) <= HALF_WINDOW, acc, NEG_INF)

    return pl.pallas_call(
        body,
        out_shape=jax.ShapeDtypeStruct((3, HEADS_PER_GROUP_A, A_TQ, A_WIN), F32),
        grid=(3, HEADS_PER_GROUP_A),
        in_specs=[pl.BlockSpec(memory_space=pltpu.SMEM),
                  pl.BlockSpec((None, A_TQ, A_WIN), lambda g, h: (g, 0, 0))],
        out_specs=pl.BlockSpec((None, None, A_TQ, A_WIN), lambda g, h: (g, h, 0, 0)),
        name="a_bias_build",
        compiler_params=pltpu.CompilerParams(dimension_semantics=("parallel", "parallel")),
    )(rel_bias, buckets)


def bias_bwd(dbias, buckets):
    def body(d_ref, bk_ref, o_ref):
        bk = bk_ref[...]
        dv = d_ref[...]
        for b in range(N_BUCKETS):
            part = jnp.sum(jnp.where(bk == b, dv, 0.0), axis=1, keepdims=True)
            o_ref[b:b + 1, :] = jnp.broadcast_to(jnp.sum(part, axis=0, keepdims=True), (1, LANES))

    out = pl.pallas_call(
        body,
        out_shape=jax.ShapeDtypeStruct((3, HEADS_PER_GROUP_A, N_BUCKETS, LANES), F32),
        grid=(3, HEADS_PER_GROUP_A),
        in_specs=[pl.BlockSpec((None, None, A_TQ, A_WIN), lambda g, h: (g, h, 0, 0)),
                  pl.BlockSpec((None, A_TQ, A_WIN), lambda g, h: (g, 0, 0))],
        out_specs=pl.BlockSpec((None, None, N_BUCKETS, LANES), lambda g, h: (g, h, 0, 0)),
        name="a_bias_bwd",
        compiler_params=pltpu.CompilerParams(dimension_semantics=("parallel", "parallel")),
    )(dbias, buckets)
    return out[:, :, :, 0].transpose(2, 0, 1).reshape(N_BUCKETS, 3 * HEADS_PER_GROUP_A)


def _a_fill_padded(pad_ref, src_ref, L):
    zeros = jnp.zeros((HALF_WINDOW, LANES), pad_ref.dtype)
    pad_ref[0:HALF_WINDOW, :] = zeros
    pad_ref[HALF_WINDOW + L:2 * HALF_WINDOW + L, :] = zeros
    pad_ref[HALF_WINDOW:HALF_WINDOW + L, :] = src_ref[...]


def _a_key_valid(qb, L):
    kidx = qb * A_TQ - HALF_WINDOW + lax.broadcasted_iota(jnp.int32, (A_TQ, A_WIN), 1)
    return (kidx >= 0) & (kidx < L)


def a_fwd(proj_a, bias_g, g, name):
    S = proj_a.shape[0]
    d = DILATIONS[g]
    L = S // d
    nqb = L // A_TQ
    nb = A_BLOCKS_PER_TOKEN
    pv = proj_a.reshape(L, d * A_QKV_WIDTH)

    def body(q_ref, k_ref, v_ref, b_ref, o_ref, l_ref, kpad, vpad):
        _a_fill_padded(kpad, k_ref, L)
        _a_fill_padded(vpad, v_ref, L)
        lane = lax.broadcasted_iota(jnp.int32, (A_TQ, LANES), 1)

        def block(qb, carry):
            start = pl.multiple_of(qb * A_TQ, A_TQ)
            kw = kpad[pl.ds(start, A_WIN), :]
            vw = vpad[pl.ds(start, A_WIN), :]
            q = q_ref[pl.ds(start, A_TQ), :]
            valid = _a_key_valid(qb, L)
            outs, lses = [], []
            for h in range(2):
                qh = jnp.where((lane >= HEAD_DIM_A * h) & (lane < HEAD_DIM_A * (h + 1)), q, jnp.zeros_like(q))
                s = _dot(qh, kw, 1, 1) * (HEAD_DIM_A ** -0.5) + b_ref[h]
                s = jnp.where(valid, s, NEG_INF)
                m = jnp.max(s, axis=-1, keepdims=True)
                e = jnp.exp(s - m)
                l = jnp.sum(e, axis=-1, keepdims=True)
                outs.append(_dot(e.astype(BF16), vw) / l)
                lses.append(m + jnp.log(l))
            o_ref[pl.ds(start, A_TQ), :] = jnp.where(lane < HEAD_DIM_A, outs[0], outs[1])
            l_ref[pl.ds(start, A_TQ), :] = jnp.where(lane < HEAD_DIM_A, lses[0], lses[1])
            return carry

        lax.fori_loop(0, nqb, block, 0, unroll=min(A_UNROLL, nqb))

    col = lambda which: (lambda r, hp: r * nb + (which * 3 + g) * 4 + hp)
    out_spec = ((L, d * GROUP_WIDTH_A), F32, (L, LANES), lambda r, hp: (0, r * 4 + hp))
    o, lse = _call(name, body, (d, 4),
                   [(pv, (L, LANES), lambda r, hp: (0, col(0)(r, hp))),
                    (pv, (L, LANES), lambda r, hp: (0, col(1)(r, hp))),
                    (pv, (L, LANES), lambda r, hp: (0, col(2)(r, hp))),
                    (bias_g, (2, A_TQ, A_WIN), lambda r, hp: (hp, 0, 0))],
                   [out_spec, out_spec],
                   scratch=[pltpu.VMEM((L + 2 * HALF_WINDOW, LANES), BF16)] * 2,
                   sem=("parallel", "parallel"))
    return o.reshape(S, GROUP_WIDTH_A), lse.reshape(S, GROUP_WIDTH_A)


def a_combine(outs, lses, name):
    S, W = outs[0].shape
    tr = 512

    def body(o0, o1, o2, l0, l1, l2, oa_ref, lt_ref):
        a, b, c = l0[...], l1[...], l2[...]
        m = jnp.maximum(jnp.maximum(a, b), c)
        ea, eb, ec = jnp.exp(a - m), jnp.exp(b - m), jnp.exp(c - m)
        z = ea + eb + ec
        oa_ref[...] = ((ea * o0[...] + eb * o1[...] + ec * o2[...]) / z).astype(BF16)
        lt_ref[...] = m + jnp.log(z)

    spec = ((tr, W), lambda i: (i, 0))
    return _call(name, body, (S // tr,), [(a,) + spec for a in (*outs, *lses)],
                 [((S, W), BF16) + spec, ((S, W), F32) + spec], sem=("parallel",))


def a_bwd(proj_a, bias_g, do_a, o_a, lse_tot, g, name):
    S = proj_a.shape[0]
    d = DILATIONS[g]
    L = S // d
    nqb = L // A_TQ
    nb = A_BLOCKS_PER_TOKEN
    pv = proj_a.reshape(L, d * A_QKV_WIDTH)
    view = lambda a: a.reshape(L, d * GROUP_WIDTH_A)
    scale = HEAD_DIM_A ** -0.5

    def body(q_ref, k_ref, v_ref, b_ref, do_ref, o_ref, l_ref, dq_ref, dk_ref, dv_ref, db_ref,
             kpad, vpad, dkacc, dvacc):
        r = pl.program_id(1)
        _a_fill_padded(kpad, k_ref, L)
        _a_fill_padded(vpad, v_ref, L)
        dkacc[...] = jnp.zeros(dkacc.shape, F32)
        dvacc[...] = jnp.zeros(dvacc.shape, F32)

        @pl.when(r == 0)
        def _():
            db_ref[...] = jnp.zeros(db_ref.shape, F32)

        lane = lax.broadcasted_iota(jnp.int32, (A_TQ, LANES), 1)

        def block(qb, carry):
            start = pl.multiple_of(qb * A_TQ, A_TQ)
            rows = pl.ds(start, A_TQ)
            kw = kpad[pl.ds(start, A_WIN), :]
            vw = vpad[pl.ds(start, A_WIN), :]
            q = q_ref[rows, :]
            do = do_ref[rows, :]
            ov = o_ref[rows, :].astype(F32)
            lt = l_ref[rows, :]
            valid = _a_key_valid(qb, L)
            dqs = []
            dk_win = jnp.zeros((A_WIN, LANES), F32)
            dv_win = jnp.zeros((A_WIN, LANES), F32)
            for h in range(2):
                mh = (lane >= HEAD_DIM_A * h) & (lane < HEAD_DIM_A * (h + 1))
                qh = jnp.where(mh, q, jnp.zeros_like(q))
                doh = jnp.where(mh, do, 0.0)
                s = _dot(qh, kw, 1, 1) * scale + b_ref[h]
                s = jnp.where(valid, s, NEG_INF)
                p = jnp.exp(s - lt[:, HEAD_DIM_A * h:HEAD_DIM_A * h + 1])
                t = jnp.sum(doh * ov, axis=-1, keepdims=True)
                dob = doh.astype(BF16)
                ds = p * (_dot(dob, vw, 1, 1) - t)
                db_ref[h] += ds
                dsb = (ds * scale).astype(BF16)
                dqs.append(_dot(dsb, kw))
                dk_win = dk_win + _dot(dsb, qh, 0, 0)
                dv_win = dv_win + _dot(p.astype(BF16), dob, 0, 0)
            dq_ref[rows, :] = jnp.where(lane < HEAD_DIM_A, dqs[0], dqs[1]).astype(BF16)
            dkacc[pl.ds(start, A_WIN), :] += dk_win
            dvacc[pl.ds(start, A_WIN), :] += dv_win
            return carry

        lax.fori_loop(0, nqb, block, 0, unroll=min(A_UNROLL, nqb))
        dk_ref[...] = dkacc[HALF_WINDOW:HALF_WINDOW + L, :].astype(BF16)
        dv_ref[...] = dvacc[HALF_WINDOW:HALF_WINDOW + L, :].astype(BF16)

    col = lambda which: (lambda hp, r: r * nb + (which * 3 + g) * 4 + hp)
    slab = ((L, LANES), lambda hp, r: (0, r * 4 + hp))
    oshape = (L, d * GROUP_WIDTH_A)
    dq, dk, dv, db = _call(
        name, body, (4, d),
        [(pv, (L, LANES), lambda hp, r: (0, col(0)(hp, r))),
         (pv, (L, LANES), lambda hp, r: (0, col(1)(hp, r))),
         (pv, (L, LANES), lambda hp, r: (0, col(2)(hp, r))),
         (bias_g, (2, A_TQ, A_WIN), lambda hp, r: (hp, 0, 0)),
         (view(do_a),) + slab, (view(o_a),) + slab, (view(lse_tot),) + slab],
        [(oshape, BF16) + slab, (oshape, BF16) + slab, (oshape, BF16) + slab,
         ((HEADS_PER_GROUP_A, A_TQ, A_WIN), F32, (2, A_TQ, A_WIN), lambda hp, r: (hp, 0, 0))],
        scratch=[pltpu.VMEM((L + 2 * HALF_WINDOW, LANES), BF16)] * 2
        + [pltpu.VMEM((L + 2 * HALF_WINDOW, LANES), F32)] * 2,
        sem=("parallel", "arbitrary"))
    return dq.reshape(S, GROUP_WIDTH_A), dk.reshape(S, GROUP_WIDTH_A), dv.reshape(S, GROUP_WIDTH_A), db


def _rope_tables(S):
    rows = S // GRID_W
    row = jnp.repeat(jnp.arange(rows, dtype=F32), GRID_W)
    col = jnp.tile(jnp.arange(GRID_W, dtype=F32), rows)
    n_freq = HEAD_DIM_B // 4
    freq = ROPE_THETA ** (-jnp.arange(n_freq, dtype=F32) / n_freq)
    ang = jnp.concatenate([row[:, None] * freq, col[:, None] * freq], axis=-1)
    cos, sin = jnp.cos(ang), jnp.sin(ang)
    return jnp.repeat(cos, 2, axis=-1), jnp.stack([-sin, sin], axis=-1).reshape(S, HEAD_DIM_B)


def _swap_pairs(y):
    lane = lax.broadcasted_iota(jnp.int32, y.shape, 1)
    return jnp.where(lane % 2 == 0, pltpu.roll(y, LANES - 1, 1), pltpu.roll(y, 1, 1))


def qkv_prep(proj_b, gains, cos_t, sin_t, name):
    S = proj_b.shape[0]
    ts = 512
    n_rot = N_HEADS_B + N_KV_B

    def body(x_ref, g_ref, c_ref, s_ref, o_ref):
        hb = pl.program_id(0)

        @pl.when(hb < n_rot)
        def _():
            xv = x_ref[...]
            r = lax.rsqrt(jnp.mean(xv * xv, axis=-1, keepdims=True) + EPS)
            yv = xv * r * g_ref[...]
            o_ref[...] = (yv * c_ref[...] + _swap_pairs(yv) * s_ref[...]).astype(BF16)

        @pl.when(hb >= n_rot)
        def _():
            o_ref[...] = x_ref[...].astype(BF16)

    nh = n_rot + N_KV_B
    return _call(name, body, (nh, S // ts),
                 [(proj_b, (ts, LANES), lambda hb, i: (i, hb)), (gains, (1, LANES), lambda hb, i: (0, hb)),
                  (cos_t, (ts, LANES), lambda hb, i: (i, 0)), (sin_t, (ts, LANES), lambda hb, i: (i, 0))],
                 [((S, nh * LANES), BF16, (ts, LANES), lambda hb, i: (i, hb))],
                 sem=("parallel", "parallel"))[0]


def qk_prep_bwd(dr, proj_b, col0, gain, cos_t, sin_t, name):
    S, W = dr.shape
    H = W // LANES
    ts = 512

    def body(d_ref, x_ref, g_ref, c_ref, s_ref, dx_ref, dg_ref):
        hb, i = pl.program_id(0), pl.program_id(1)
        dout = d_ref[...]
        dy = dout * c_ref[...] + _swap_pairs(dout * s_ref[...])
        dx, dgt = _rms_bwd_tile(dy, x_ref[...], g_ref[...])
        dx_ref[...] = dx.astype(BF16)
        dgp = jnp.sum(dgt, axis=0, keepdims=True)

        @pl.when((hb == 0) & (i == 0))
        def _():
            dg_ref[...] = dgp

        @pl.when((hb > 0) | (i > 0))
        def _():
            dg_ref[...] += dgp

    return _call(name, body, (H, S // ts),
                 [(dr, (ts, LANES), lambda hb, i: (i, hb)), (proj_b, (ts, LANES), lambda hb, i: (i, col0 + hb)),
                  (gain, (1, LANES), lambda hb, i: (0, 0)),
                  (cos_t, (ts, LANES), lambda hb, i: (i, 0)), (sin_t, (ts, LANES), lambda hb, i: (i, 0))],
                 [((S, W), BF16, (ts, LANES), lambda hb, i: (i, hb)),
                  ((1, LANES), F32, (1, LANES), lambda hb, i: (0, 0))],
                 sem=("arbitrary", "arbitrary"))


def _row_sums(x):
    hi = x.astype(BF16)
    lo = (x - hi.astype(F32)).astype(BF16)
    ones = jnp.ones((8, LANES), BF16)
    return (_dot(ones, hi, 1, 1) + _dot(ones, lo, 1, 1))[0:1, :]


def flash_fwd(qkv, name):
    S = qkv.shape[0]
    tq = B_TQ_FWD
    scale = HEAD_DIM_B ** -0.5

    def body(q_ref, k_ref, v_ref, o_ref, l_ref):
        s = _dot(q_ref[...], k_ref[...], 1, 1) * scale
        m = jnp.max(s, axis=-1, keepdims=True)
        e = jnp.exp(s - m)
        l = jnp.sum(e, axis=-1, keepdims=True)
        o_ref[...] = (_dot(e.astype(BF16), v_ref[...]) / l).astype(BF16)
        lse = jnp.broadcast_to(m + jnp.log(l), (tq, LANES))
        l_ref[...] = _row_sums(lse) * (1.0 / LANES)

    head = lambda g, h, i: (i, g * GQA_GROUP_B + h)
    return _call(name, body, (N_KV_B, GQA_GROUP_B, S // tq),
                 [(qkv, (tq, LANES), head),
                  (qkv, (S, LANES), lambda g, h, i: (0, N_HEADS_B + g)),
                  (qkv, (S, LANES), lambda g, h, i: (0, N_HEADS_B + N_KV_B + g))],
                 [((S, N_HEADS_B * LANES), BF16, (tq, LANES), head),
                  ((N_HEADS_B, 1, S), F32, (None, 1, tq), lambda g, h, i: (g * GQA_GROUP_B + h, 0, i))],
                 sem=("parallel", "parallel", "parallel"))


def flash_bwd(qkv, k_t, do_b, o_b, lse, name):
    S = qkv.shape[0]
    tq = B_TQ_BWD
    nq = S // tq
    scale = HEAD_DIM_B ** -0.5

    def body(q_ref, k_ref, v_ref, kt_ref, do_ref, o_ref, l_ref, dq_ref, dk_ref, dv_ref, dkacc, dvacc):
        h, i = pl.program_id(1), pl.program_id(2)

        @pl.when((h == 0) & (i == 0))
        def _():
            dkacc[...] = jnp.zeros(dkacc.shape, F32)
            dvacc[...] = jnp.zeros(dvacc.shape, F32)

        q = q_ref[...]
        do = do_ref[...]
        dob = do.astype(BF16)
        t = _row_sums(do * o_ref[...].astype(F32))
        pt = jnp.exp(_dot(k_ref[...], q, 1, 1) * scale - l_ref[...])
        dst = pt * (_dot(v_ref[...], dob, 1, 1) - t) * scale
        dsb = dst.astype(BF16)
        dvacc[...] += _dot(pt.astype(BF16), dob)
        dkacc[...] += _dot(dsb, q)
        dq_ref[...] = _dot(kt_ref[...], dsb).T

        @pl.when((h == GQA_GROUP_B - 1) & (i == nq - 1))
        def _():
            dk_ref[...] = dkacc[...]
            dv_ref[...] = dvacc[...].astype(BF16)

    head = lambda g, h, i: (i, g * GQA_GROUP_B + h)
    return _call(name, body, (N_KV_B, GQA_GROUP_B, nq),
                 [(qkv, (tq, LANES), head),
                  (qkv, (S, LANES), lambda g, h, i: (0, N_HEADS_B + g)),
                  (qkv, (S, LANES), lambda g, h, i: (0, N_HEADS_B + N_KV_B + g)),
                  (k_t, (LANES, S), lambda g, h, i: (g, 0)),
                  (do_b, (tq, LANES), head), (o_b, (tq, LANES), head),
                  (lse, (None, 1, tq), lambda g, h, i: (g * GQA_GROUP_B + h, 0, i))],
                 [((S, N_HEADS_B * LANES), F32, (tq, LANES), head),
                  ((S, N_KV_B * LANES), F32, (S, LANES), lambda g, h, i: (0, g)),
                  ((S, N_KV_B * LANES), BF16, (S, LANES), lambda g, h, i: (0, g))],
                 scratch=[pltpu.VMEM((S, LANES), F32)] * 2,
                 sem=("parallel", "arbitrary", "arbitrary"))


MERGE_TN = 512


def _mix_rows_spec(Gm, row0, n_slots, slot_map, cols=None, col_map=None):
    C = Gm.shape[2] if cols is None else cols
    cm = (lambda *idx: 0) if col_map is None else col_map
    return (Gm, (n_slots, LANES, C), lambda *idx: (slot_map(*idx), row0 // LANES, cm(*idx)))


def merge_fwd(o_a, o_b, w_a, Gm, proj_b, b_gate, name):
    S = o_a.shape[0]
    D = w_a.shape[1]
    tm, tn = 512, MERGE_TN
    ga0, gb0 = PB_GATE_A // tn, PB_GATE_B // tn

    def body(oa_ref, ob_ref, wa_ref, wb_ref, pa_ref, pb_ref, ba_ref, bb_ref, m_ref, ya_ref, yb_ref):
        ya = _dot(oa_ref[...], wa_ref[...])
        yb = _dot(ob_ref[...], wb_ref[...].reshape(N_DEV * LANES, tn))
        ga = _sigmoid(pa_ref[...] + ba_ref[...])
        gb = _sigmoid(pb_ref[...] + bb_ref[...])
        m_ref[...] = (ga * ya + gb * yb).astype(BF16)
        ya_ref[...] = ya.astype(BF16)
        yb_ref[...] = yb.astype(BF16)

    out = ((S, D), BF16, (tm, tn), lambda j, i: (i, j))
    return _call(name, body, (D // tn, S // tm),
                 [(o_a, (tm, o_a.shape[1]), lambda j, i: (i, 0)), (o_b, (tm, o_b.shape[1]), lambda j, i: (i, 0)),
                  (w_a, (w_a.shape[0], tn), lambda j, i: (0, j)),
                  _mix_rows_spec(Gm, MIX_WB, N_DEV, lambda j, i: 0, cols=tn, col_map=lambda j, i: j),
                  (proj_b, (tm, tn), lambda j, i: (i, ga0 + j)), (proj_b, (tm, tn), lambda j, i: (i, gb0 + j)),
                  (b_gate, (1, tn), lambda j, i: (0, j)), (b_gate, (1, tn), lambda j, i: (0, D // tn + j))],
                 [out, out, out], sem=("parallel", "parallel"))


def out_proj(merged, Gm, x, name):
    S, D = x.shape
    tm, tn = 512, MERGE_TN

    def body(m_ref, w_ref, x_ref, o_ref):
        o_ref[...] = x_ref[...] + _dot(m_ref[...], w_ref[...].reshape(N_DEV * LANES, tn))

    return _call(name, body, (D // tn, S // tm),
                 [(merged, (tm, D), lambda j, i: (i, 0)),
                  _mix_rows_spec(Gm, MIX_WOUT, N_DEV, lambda j, i: 0, cols=tn, col_map=lambda j, i: j),
                  (x, (tm, tn), lambda j, i: (i, j))],
                 [((S, D), F32, (tm, tn), lambda j, i: (i, j))], sem=("parallel", "parallel"))[0]


def merge_bwd(dx2, Gm, ya, yb, proj_b, b_gate, name):
    S, D = dx2.shape
    tm, tn = 512, MERGE_TN
    nn = D // tn
    ga0, gb0 = PB_GATE_A // tn, PB_GATE_B // tn

    def body(d_ref, w_ref, ya_ref, yb_ref, pa_ref, pb_ref, ba_ref, bb_ref, dya_ref, dyb_ref, dg_ref, dbg_ref):
        i = pl.program_id(1)
        dm = _dot(d_ref[...].astype(BF16), w_ref[...].reshape(tn, D), 1, 1)
        ga = _sigmoid(pa_ref[...] + ba_ref[...])
        gb = _sigmoid(pb_ref[...] + bb_ref[...])
        dya_ref[...] = (dm * ga).astype(BF16)
        dyb_ref[...] = (dm * gb).astype(BF16)
        dpa = dm * ya_ref[...].astype(F32) * ga * (1.0 - ga)
        dpb = dm * yb_ref[...].astype(F32) * gb * (1.0 - gb)
        dg_ref[0] = dpa.astype(BF16)
        dg_ref[1] = dpb.astype(BF16)
        sa = jnp.sum(dpa, axis=0, keepdims=True)
        sb = jnp.sum(dpb, axis=0, keepdims=True)

        @pl.when(i == 0)
        def _():
            dbg_ref[0] = sa
            dbg_ref[1] = sb

        @pl.when(i > 0)
        def _():
            dbg_ref[0] += sa
            dbg_ref[1] += sb

    tile = ((tm, tn), lambda j, i: (i, j))
    dya, dyb, dgate, dbg = _call(
        name, body, (nn, S // tm),
        [(dx2, (tm, D), lambda j, i: (i, 0)),
         _mix_rows_spec(Gm, MIX_WOUT, tn // LANES, lambda j, i: j),
         (ya,) + tile, (yb,) + tile,
         (proj_b, (tm, tn), lambda j, i: (i, ga0 + j)), (proj_b, (tm, tn), lambda j, i: (i, gb0 + j)),
         (b_gate, (1, tn), lambda j, i: (0, j)), (b_gate, (1, tn), lambda j, i: (0, nn + j))],
        [((S, D), BF16) + tile, ((S, D), BF16) + tile,
         ((2, S, D), BF16, (2, tm, tn), lambda j, i: (0, i, j)),
         ((2, 1, D), F32, (2, 1, tn), lambda j, i: (0, 0, j))],
        sem=("parallel", "arbitrary"))
    return dya, dyb, dgate, dbg


def matmul_nt(a, b_spec_fn, N, name, tn=512):
    S, K = a.shape
    tm = 512

    def body(a_ref, b_ref, o_ref):
        b = b_ref[...]
        o_ref[...] = _dot(a_ref[...], b.reshape(-1, b.shape[-1]), 1, 1)

    return _call(name, body, (N // tn, S // tm),
                 [(a, (tm, K), lambda j, i: (i, 0)), b_spec_fn(lambda j, i: j)],
                 [((S, N), F32, (tm, tn), lambda j, i: (i, j))], sem=("parallel", "parallel"))[0]


def weight_grad_rows(a, b, grads, row0, name):
    S, M = a.shape
    N = b.shape[1]
    tmm = 512
    tk = WGRAD_TK
    nk = S // tk

    def body(g_ref, a_ref, b_ref, o_ref, acc_ref):
        k = pl.program_id(1)
        p = _dot(a_ref[...], b_ref[...].astype(BF16), 0, 0)

        @pl.when(k == 0)
        def _():
            acc_ref[...] = p

        @pl.when(k > 0)
        def _():
            acc_ref[...] += p

        @pl.when(k == nk - 1)
        def _():
            o_ref[...] = acc_ref[...].astype(BF16).reshape(tmm // LANES, LANES, N)

    return pl.pallas_call(
        body,
        out_shape=jax.ShapeDtypeStruct(grads.shape, BF16),
        grid=(M // tmm, nk),
        in_specs=[pl.BlockSpec(memory_space=pl.ANY),
                  pl.BlockSpec((tk, tmm), lambda j, k: (k, j)),
                  pl.BlockSpec((tk, N), lambda j, k: (k, 0))],
        out_specs=pl.BlockSpec((tmm // LANES, LANES, N), lambda j, k: (j, row0 // LANES, 0)),
        scratch_shapes=[pltpu.VMEM((tmm, N), F32)],
        input_output_aliases={0: 0},
        name=name,
        compiler_params=pltpu.CompilerParams(dimension_semantics=("parallel", "arbitrary"),
                                             vmem_limit_bytes=VMEM_LIMIT),
    )(grads, a, b)


def weight_grad_plain(a, b, name):
    S, M = a.shape
    N = b.shape[1]
    tk = WGRAD_TK
    nk = S // tk

    def body(a_ref, b_ref, o_ref, acc_ref):
        k = pl.program_id(0)
        p = _dot(a_ref[...], b_ref[...], 0, 0)

        @pl.when(k == 0)
        def _():
            acc_ref[...] = p

        @pl.when(k > 0)
        def _():
            acc_ref[...] += p

        @pl.when(k == nk - 1)
        def _():
            o_ref[...] = acc_ref[...].astype(BF16)

    return _call(name, body, (nk,),
                 [(a, (tk, M), lambda k: (k, 0)), (b, (tk, N), lambda k: (k, 0))],
                 [((M, N), BF16, (M, N), lambda k: (0, 0))],
                 scratch=[pltpu.VMEM((M, N), F32)], sem=("arbitrary",))[0]


def local_step(x, tgt, p, G1, Gm, G2):
    S, D = x.shape
    w_a = Gm[:, MIX_WA:MIX_ROWS, :].reshape(N_DEV, GROUP_WIDTH_A, LANES).transpose(1, 0, 2).reshape(GROUP_WIDTH_A, D)
    buckets = _bucket_tables()
    cos_t, sin_t = _rope_tables(S)
    gains = jnp.concatenate([jnp.tile(p["q_norm"], (1, N_HEADS_B)), jnp.tile(p["k_norm"], (1, N_KV_B)),
                             jnp.ones((1, N_KV_B * LANES), F32)], axis=1)

    n1 = rms_fwd(x, p["ffn1_norm"], "ffn1_norm")
    ab1 = ffn_up(n1, G1, "ffn1_up")
    x1 = ffn_down(ab1, G1, x, "ffn1_down")

    hm = rms_fwd(x1, p["mix_norm"], "mix_norm")
    n_a = A_QKV_WIDTH // PROJ_TN
    proj_a = in_proj(hm, Gm, 0, n_a, BF16, "in_proj_a")
    proj_b = in_proj(hm, Gm, n_a, PB_WIDTH // PROJ_TN, F32, "in_proj_b")

    bias = bias_build(p["rel_bias"], buckets)
    outs, lses = [], []
    for g in range(3):
        o, l = a_fwd(proj_a, bias[g], g, "a_fwd_%d" % g)
        outs.append(o)
        lses.append(l)
    o_a, lse_tot = a_combine(outs, lses, "a_combine")

    qkv = qkv_prep(proj_b, gains, cos_t, sin_t, "qkv_prep")
    k_t = qkv[:, N_HEADS_B * LANES:(N_HEADS_B + N_KV_B) * LANES].T
    o_b, lse_b = flash_fwd(qkv, "flash_fwd")

    merged, ya, yb = merge_fwd(o_a, o_b, w_a, Gm, proj_b, p["b_gate"], "merge_fwd")
    x2 = out_proj(merged, Gm, x1, "out_proj")

    n2 = rms_fwd(x2, p["ffn2_norm"], "ffn2_norm")
    ab2 = ffn_up(n2, G2, "ffn2_up")
    x3 = ffn_down(ab2, G2, x2, "ffn2_down")

    loss, dx3, d_final = final_loss(x3, tgt, p["final_norm"], "final_loss")

    dx2, d_ffn2_norm, gw2 = ffn_bwd(dx3, ab2, n2, G2, x2, p["ffn2_norm"], "ffn2_bwd")

    dya, dyb, dgate, dbg = merge_bwd(dx2, Gm, ya, yb, proj_b, p["b_gate"], "merge_bwd")
    gm_grads = jnp.zeros(Gm.shape, BF16)
    gm_grads = weight_grad_rows(merged, dx2, gm_grads, MIX_WOUT, "dw_out")
    gm_grads = weight_grad_rows(o_b, dyb, gm_grads, MIX_WB, "dw_branch_b")
    dw_a = weight_grad_plain(o_a, dya, "dw_branch_a")
    do_a = matmul_nt(dya, lambda jm: (w_a, (MERGE_TN, D), lambda j, i: (jm(j, i), 0)), GROUP_WIDTH_A, "do_a")
    do_b = matmul_nt(dyb, lambda jm: _mix_rows_spec(Gm, MIX_WB, MERGE_TN // LANES, jm), N_HEADS_B * LANES, "do_b")

    dq_r, dk_r, dv_b = flash_bwd(qkv, k_t, do_b, o_b, lse_b, "flash_bwd")
    dq_b, d_q_norm = qk_prep_bwd(dq_r, proj_b, 0, p["q_norm"], cos_t, sin_t, "q_prep_bwd")
    dk_b, d_k_norm = qk_prep_bwd(dk_r, proj_b, N_HEADS_B, p["k_norm"], cos_t, sin_t, "k_prep_bwd")

    dqs, dks, dvs, dbs = [], [], [], []
    for g in range(3):
        dq, dk, dv, db = a_bwd(proj_a, bias[g], do_a, o_a, lse_tot, g, "a_bwd_%d" % g)
        dqs.append(dq)
        dks.append(dk)
        dvs.append(dv)
        dbs.append(db)
    d_rel_bias = bias_bwd(jnp.stack(dbs, axis=0), buckets)

    dproj = jnp.concatenate(dqs + dks + dvs + [dq_b, dk_b, dv_b, dgate[0], dgate[1]], axis=1)
    dx1, d_mix_norm, gm_grads = in_proj_bwd(dproj, hm, Gm, x1, p["mix_norm"], dx2, gm_grads, "in_proj_bwd")
    dw_a_sh = dw_a.reshape(GROUP_WIDTH_A, N_DEV, LANES).transpose(1, 0, 2).reshape(N_DEV, MIX_ROWS - MIX_WA, D)
    gm_grads = lax.dynamic_update_slice(gm_grads, dw_a_sh, (0, MIX_WA, 0))

    dx0, d_ffn1_norm, gw1 = ffn_bwd(dx1, ab1, n1, G1, x, p["ffn1_norm"], "ffn1_bwd")

    small = dict(ffn1_norm=d_ffn1_norm, mix_norm=d_mix_norm, b_gate=dbg.reshape(1, 2 * D),
                 q_norm=d_q_norm, k_norm=d_k_norm, rel_bias=d_rel_bias, ffn2_norm=d_ffn2_norm,
                 final_norm=d_final)
    return loss, dx0, small, gw1, gm_grads, gw2


def _pack_small(t, loss_row):
    row6 = jnp.concatenate([t["q_norm"].reshape(1, -1), t["k_norm"].reshape(1, -1), t["rel_bias"].reshape(1, -1)], axis=1)
    return jnp.concatenate([t["ffn1_norm"].reshape(1, -1), t["mix_norm"].reshape(1, -1), t["b_gate"].reshape(2, -1),
                            t["ffn2_norm"].reshape(1, -1), t["final_norm"].reshape(1, -1), row6, loss_row], axis=0)


def _unpack_small(a, shapes):
    return dict(ffn1_norm=a[0:1].reshape(shapes["ffn1_norm"]), mix_norm=a[1:2].reshape(shapes["mix_norm"]),
                b_gate=a[2:4].reshape(shapes["b_gate"]), ffn2_norm=a[4:5].reshape(shapes["ffn2_norm"]),
                final_norm=a[5].reshape(shapes["final_norm"]), q_norm=a[6:7, 0:128].reshape(shapes["q_norm"]),
                k_norm=a[6:7, 128:256].reshape(shapes["k_norm"]), rel_bias=a[6, 256:1024].reshape(shapes["rel_bias"]))


SMALL = ("ffn1_norm", "mix_norm", "b_gate", "q_norm", "k_norm", "rel_bias", "ffn2_norm", "final_norm")
ORDER = ("ffn1_norm", "ffn1_w1", "ffn1_w3", "ffn1_w2", "mix_norm", "w_in", "b_gate", "q_norm", "k_norm", "rel_bias",
         "w_branch_a", "w_branch_b", "w_out", "ffn2_norm", "ffn2_w1", "ffn2_w3", "ffn2_w2", "final_norm")


def kernel(x, ffn1_norm, ffn1_w1, ffn1_w3, ffn1_w2, mix_norm, w_in, b_gate, q_norm, k_norm, rel_bias, w_branch_a, w_branch_b, w_out, ffn2_norm, ffn2_w1, ffn2_w3, ffn2_w2, final_norm, loss_target, m_ffn1_norm, m_ffn1_w1, m_ffn1_w3, m_ffn1_w2, m_mix_norm, m_w_in, m_b_gate, m_q_norm, m_k_norm, m_rel_bias, m_w_branch_a, m_w_branch_b, m_w_out, m_ffn2_norm, m_ffn2_w1, m_ffn2_w3, m_ffn2_w2, m_final_norm, v_ffn1_norm, v_ffn1_w1, v_ffn1_w3, v_ffn1_w2, v_mix_norm, v_w_in, v_b_gate, v_q_norm, v_k_norm, v_rel_bias, v_w_branch_a, v_w_branch_b, v_w_out, v_ffn2_norm, v_ffn2_w1, v_ffn2_w3, v_ffn2_w2, v_final_norm):
    args = dict(locals())
    w = {n: args[n] for n in ORDER}
    m = {n: args["m_" + n] for n in ORDER}
    v = {n: args["v_" + n] for n in ORDER}
    D = x.shape[2]

    def ffn_group(w1, w3, w2):
        return jnp.concatenate([w1[0].T, w3[0].T, w2[0]], axis=0).astype(BF16)

    g1 = ffn_group(ffn1_w1, ffn1_w3, ffn1_w2)
    g2 = ffn_group(ffn2_w1, ffn2_w3, ffn2_w2)
    gm = jnp.concatenate([w_in[0], w_branch_b[0], w_out[0], w_branch_a[0].reshape(MIX_ROWS - MIX_WA, D)],
                         axis=0).astype(BF16)
    G1, Gm, G2 = all_gather_groups([g1, gm, g2])

    small_p = dict(ffn1_norm=ffn1_norm, mix_norm=mix_norm, b_gate=b_gate, q_norm=q_norm, k_norm=k_norm,
                   rel_bias=rel_bias, ffn2_norm=ffn2_norm, final_norm=final_norm.reshape(1, D))
    loss_p, grad_x, small_g, gw1, gwm, gw2 = local_step(x[0], loss_target[0], small_p, G1, Gm, G2)

    theirs = reduce_scatter_pair([gw1, gwm, gw2])
    core = lax.axis_index("c").astype(jnp.int32).reshape(1)
    parts = [pair_add(a, b, core, "pair_add_%d" % i) for i, (a, b) in enumerate(zip([gw1, gwm, gw2], theirs))]
    loss_row = jnp.pad(loss_p, ((0, 0), (0, D - LANES)))
    recv, smalls = reduce_scatter_chips(parts, _pack_small(small_g, loss_row))
    r1, rm, r2 = recv

    grads = {}
    for tag, r in (("ffn1", r1), ("ffn2", r2)):
        grads[tag + "_w1"] = sum_slots(r, 0, FFN_SHARD, FFN_SHARD, tag + "_w1_sum").T[None]
        grads[tag + "_w3"] = sum_slots(r, FFN_SHARD, FFN_SHARD, FFN_SHARD, tag + "_w3_sum").T[None]
        grads[tag + "_w2"] = sum_slots(r, 2 * FFN_SHARD, FFN_SHARD, FFN_SHARD, tag + "_w2_sum")[None]
    grads["w_in"] = sum_slots(rm, MIX_WIN, MIX_WB - MIX_WIN, LANES, "w_in_sum")[None]
    grads["w_branch_b"] = sum_slots(rm, MIX_WB, LANES, LANES, "w_branch_b_sum")[None]
    grads["w_out"] = sum_slots(rm, MIX_WOUT, LANES, LANES, "w_out_sum")[None]
    grads["w_branch_a"] = sum_slots(rm, MIX_WA, MIX_ROWS - MIX_WA, MIX_ROWS - MIX_WA,
                                    "w_branch_a_sum").reshape(w_branch_a.shape)
    small_sum = sum_slots(smalls, 0, N_DEV, N_DEV, "small_sum")
    small_shapes = {n: w[n].shape for n in SMALL}
    grads.update(_unpack_small(small_sum, small_shapes))
    loss = small_sum[7, 0]

    delta, new_m, new_v = {}, {}, {}
    for n in ORDER:
        if n in SMALL:
            continue
        shp = w[n].shape
        two_d = lambda a: a.reshape(shp[-2], shp[-1])
        d_, m_, v_ = adamw(two_d(w[n]), two_d(grads[n]), two_d(m[n]), two_d(v[n]), "adamw_" + n)
        delta[n], new_m[n], new_v[n] = d_.reshape(shp), m_.reshape(shp), v_.reshape(shp)
    zero_row = jnp.zeros((1, D), F32)
    pack = lambda t: _pack_small({n: t[n] for n in SMALL}, zero_row)
    d_, m_, v_ = adamw(pack(w), small_sum, pack(m), pack(v), "adamw_small")
    for src, dst in ((d_, delta), (m_, new_m), (v_, new_v)):
        dst.update(_unpack_small(src, small_shapes))

    return (loss, grad_x[None], *[grads[n] for n in ORDER], *[delta[n] for n in ORDER],
            *[new_m[n] for n in ORDER], *[new_v[n] for n in ORDER])
```

```python
import math

import jax
import jax.numpy as jnp
from jax import lax
from jax.experimental import pallas as pl
from jax.experimental.pallas import tpu as pltpu

F32 = jnp.float32
BF16 = jnp.bfloat16
MESH = pl.DeviceIdType.MESH

V7X_VMEM_BYTES = 64 * 1024 * 1024
VMEM_LIMIT = V7X_VMEM_BYTES - 8 * 1024 * 1024
LANES = 128

N_DEV = 8
EPS = 1e-6
NEG_INF = -1e30

DILATIONS = (1, 4, 16)
HALF_WINDOW = 64
HEAD_DIM_A = 64
HEADS_PER_GROUP_A = 8
GROUP_WIDTH_A = 512
A_QKV_WIDTH = 4608
A_BLOCKS_PER_TOKEN = A_QKV_WIDTH // LANES
A_TQ = 128
A_WIN = A_TQ + 2 * HALF_WINDOW
A_UNROLL = 4
WGRAD_TK = 2048
HEAD_DIM_B = 128
N_HEADS_B = 8
N_KV_B = 2
GQA_GROUP_B = 4
GRID_W = 64
ROPE_THETA = 10000.0
B_TQ_FWD = 256
B_TQ_BWD = 512
N_BUCKETS = 32
MAX_DISTANCE = 1024
PB_WIDTH = 3584
PB_GATE_A = 1536
PB_GATE_B = 2560

ADAM_LR = 0.001
ADAM_B1 = 0.9
ADAM_B2 = 0.999
ADAM_EPS = 1e-08
ADAM_WD = 0.01
ADAM_STEP = 10

FFN_SHARD = 352
MIX_WIN, MIX_WB, MIX_WOUT, MIX_WA = 0, 1024, 1152, 1280
MIX_ROWS = 1344


def _dot(a, b, ca=1, cb=0):
    return lax.dot_general(a, b, (((ca,), (cb,)), ((), ())), preferred_element_type=F32)


def _call(name, body, grid, ins, outs, scratch=(), sem=None, aliases=None):
    res = pl.pallas_call(
        body,
        out_shape=[jax.ShapeDtypeStruct(s, d) for (s, d, _, _) in outs],
        grid=grid,
        in_specs=[pl.BlockSpec(bs, im) for (_, bs, im) in ins],
        out_specs=[pl.BlockSpec(bs, im) for (_, _, bs, im) in outs],
        scratch_shapes=list(scratch),
        name=name,
        input_output_aliases=aliases or {},
        compiler_params=pltpu.CompilerParams(dimension_semantics=sem, vmem_limit_bytes=VMEM_LIMIT),
    )(*[a for (a, _, _) in ins])
    return res


def _sigmoid(x):
    return 1.0 / (1.0 + jnp.exp(-x))


def _position():
    return lax.axis_index("x"), lax.axis_index("y"), lax.axis_index("c")


def _hbm_specs(n):
    return [pl.BlockSpec(memory_space=pl.ANY) for _ in range(n)]


def all_gather_groups(groups, later=()):
    n = len(groups)
    n_later = len(later)

    def body(*refs):
        ins, outs = refs[:n], refs[n + n_later:2 * n + n_later]
        refs = refs[2 * n_later:]
        stage = refs[2 * n:3 * n]
        send_sems, recv_sems, local_sems = refs[3 * n:]
        x, y, c = _position()
        sibling = (x, y, 1 - c)
        chips = [(1 - x, y), (x, 1 - y), (1 - x, 1 - y)]

        def copy(i, k, block, to, src=None):
            px, py, pc = block
            dst = outs[i].at[4 * px + 2 * py + pc]
            return pltpu.make_async_remote_copy(
                src_ref=dst if src is None else src, dst_ref=dst,
                send_sem=send_sems.at[i, k], recv_sem=recv_sems.at[i, k],
                device_id=to, device_id_type=MESH)

        loads = [pltpu.make_async_copy(ins[i], stage[i], local_sems.at[i, 0]) for i in range(n)]
        for ld in loads:
            ld.start()
        sends, stores = [], []
        for i in range(n):
            loads[i].wait()
            first = [copy(i, 0, (x, y, c), sibling, src=stage[i])]
            first += [copy(i, 1 + j, (x, y, c), (*chip, c), src=stage[i]) for j, chip in enumerate(chips)]
            for cp in first:
                cp.start()
            sends += first
            st = pltpu.make_async_copy(stage[i], outs[i].at[4 * x + 2 * y + c], local_sems.at[i, 1])
            st.start()
            stores.append(st)
        for i in range(n):
            for j, chip in enumerate(chips):
                copy(i, 1 + j, (*chip, c), (x, y, c)).wait_recv()
                passed = copy(i, 4 + j, (*chip, c), sibling)
                passed.start()
                sends.append(passed)
        for i in range(n):
            copy(i, 0, sibling, (x, y, c)).wait_recv()
            for j, chip in enumerate(chips):
                copy(i, 4 + j, (*chip, 1 - c), (x, y, c)).wait_recv()
        for cp in sends:
            cp.wait_send()
        for st in stores:
            st.wait()

    res = pl.pallas_call(
        body,
        out_shape=[jax.ShapeDtypeStruct((N_DEV,) + g.shape, g.dtype) for g in groups]
        + [jax.ShapeDtypeStruct(a.shape, a.dtype) for a in later],
        in_specs=_hbm_specs(n + n_later),
        out_specs=_hbm_specs(n + n_later),
        scratch_shapes=[pltpu.VMEM(g.shape, g.dtype) for g in groups]
        + [pltpu.SemaphoreType.DMA((n, 7)), pltpu.SemaphoreType.DMA((n, 7)), pltpu.SemaphoreType.DMA((n, 2))],
        input_output_aliases={n + i: n + i for i in range(n_later)},
        name="all_gather_weights",
        compiler_params=pltpu.CompilerParams(vmem_limit_bytes=VMEM_LIMIT),
    )(*groups, *later)
    return res[:n], res[n:]


PAIR_BUFFERS = 4


def reduce_scatter_pair(grads, name):
    n = len(grads)
    C = grads[0].shape[2]
    half = [g.shape[1] // 2 for g in grads]
    chunks = [(i, q, hf) for i in range(n) for q in range(4) for hf in range(2)]
    nb = PAIR_BUFFERS

    def body(*refs):
        ins, theirs = refs[:n], refs[n:2 * n]
        buf, load_sems, send_sems, recv_sems = refs[2 * n:]
        x, y, c = _position()
        sibling = (x, y, 1 - c)

        def load(k):
            i, q, hf = chunks[k]
            r = half[i]
            return pltpu.make_async_copy(ins[i].at[2 * q + (1 - c), pl.ds(hf * r, r), :],
                                         buf.at[k % nb, pl.ds(0, r), :], load_sems.at[k % nb])

        def send(k):
            i, q, hf = chunks[k]
            r = half[i]
            return pltpu.make_async_remote_copy(
                src_ref=buf.at[k % nb, pl.ds(0, r), :], dst_ref=theirs[i].at[q, pl.ds(hf * r, r), :],
                send_sem=send_sems.at[k % nb], recv_sem=recv_sems.at[i],
                device_id=sibling, device_id_type=MESH)

        for k in range(len(chunks) + 1):
            if k < len(chunks):
                if k >= nb:
                    send(k - nb).wait_send()
                load(k).start()
            if k >= 1:
                load(k - 1).wait()
                send(k - 1).start()
        for k in range(max(0, len(chunks) - nb), len(chunks)):
            send(k).wait_send()
        for i in range(n):
            pltpu.make_async_remote_copy(
                src_ref=theirs[i], dst_ref=theirs[i], send_sem=send_sems.at[0], recv_sem=recv_sems.at[i],
                device_id=sibling, device_id_type=MESH).wait_recv()

    return pl.pallas_call(
        body,
        out_shape=[jax.ShapeDtypeStruct((4,) + g.shape[1:], g.dtype) for g in grads],
        in_specs=_hbm_specs(n),
        out_specs=_hbm_specs(n),
        scratch_shapes=[pltpu.VMEM((nb, max(half), C), grads[0].dtype), pltpu.SemaphoreType.DMA((nb,)),
                        pltpu.SemaphoreType.DMA((nb,)), pltpu.SemaphoreType.DMA((n,))],
        name=name,
        compiler_params=pltpu.CompilerParams(vmem_limit_bytes=VMEM_LIMIT),
    )(*grads)


_HBM_SPEC = pl.BlockSpec(memory_space=pltpu.HBM)
_SEM_SPEC = pl.BlockSpec(memory_space=pltpu.SEMAPHORE)
_TOKEN_SPEC = pl.BlockSpec(memory_space=pltpu.VMEM)
_DATAFLOW = pltpu.SideEffectType.DATAFLOW_SIDE_EFFECTING


def _split_start(name, body, src, land_shape):
    def full_body(src_ref, land_ref, send_sem, recv_sem, src_thru, land_thru, token):
        body(src_ref, land_ref, send_sem, recv_sem)
        token[...] = jnp.zeros_like(token)

    land = pltpu.with_memory_space_constraint(lax.empty(land_shape, src.dtype), pltpu.HBM)
    return pl.pallas_call(
        full_body, name=name,
        out_shape=(pltpu.SemaphoreType.DMA(()), pltpu.SemaphoreType.DMA(()),
                   pltpu.HBM(src.shape, src.dtype), pltpu.HBM(land_shape, src.dtype),
                   jax.ShapeDtypeStruct((8, LANES), F32)),
        in_specs=(_HBM_SPEC, _HBM_SPEC),
        out_specs=(_SEM_SPEC, _SEM_SPEC, _HBM_SPEC, _HBM_SPEC, _TOKEN_SPEC),
        input_output_aliases={0: 2, 1: 3},
        compiler_params=pltpu.CompilerParams(has_side_effects=_DATAFLOW),
    )(pltpu.with_memory_space_constraint(src, pltpu.HBM), land)


def _split_wait(name, started, n_blocks, after):
    send_sem, recv_sem, src_thru, land_thru, _ = started

    def body(src_ref, land_ref, send_sem, recv_sem, after_ref, src_dead, got_ref):
        x, y, c = _position()
        blocks = land_ref.at[pl.ds(0, n_blocks)]
        copy = pltpu.make_async_remote_copy(src_ref=blocks, dst_ref=blocks, send_sem=send_sem, recv_sem=recv_sem,
                                            device_id=(x, y, c), device_id_type=MESH)
        copy.wait_send()
        copy.wait_recv()

    return pl.pallas_call(
        body, name=name,
        out_shape=(pltpu.HBM(src_thru.shape, src_thru.dtype), pltpu.HBM(land_thru.shape, land_thru.dtype)),
        in_specs=(_HBM_SPEC, _HBM_SPEC, _SEM_SPEC, _SEM_SPEC, pl.BlockSpec(memory_space=pl.ANY)),
        out_specs=(_HBM_SPEC, _HBM_SPEC),
        input_output_aliases={0: 0, 1: 1},
        compiler_params=pltpu.CompilerParams(has_side_effects=_DATAFLOW),
    )(src_thru, land_thru, send_sem, recv_sem, after)


def all_gather_start(block, name):
    def body(b_ref, land_ref, send_sem, recv_sem):
        x, y, c = _position()
        for peer in [(x, y, 1 - c), (1 - x, y, c), (x, 1 - y, c), (1 - x, 1 - y, c)]:
            pltpu.make_async_remote_copy(src_ref=b_ref, dst_ref=land_ref.at[4 * x + 2 * y + c],
                                         send_sem=send_sem, recv_sem=recv_sem,
                                         device_id=peer, device_id_type=MESH).start()

    return _split_start(name, body, block, (N_DEV,) + block.shape)


def all_gather_finish(block, land, name):
    R, C = block.shape

    def body(b_ref, land_in, land_ref, stage, load_sems, send_sems, recv_sems, own_sem):
        x, y, c = _position()
        sibling = (x, y, 1 - c)
        chips = [(1 - x, y), (x, 1 - y), (1 - x, 1 - y)]
        own_in = pltpu.make_async_copy(b_ref, stage.at[3], load_sems.at[3])
        own_in.start()
        loads = [pltpu.make_async_copy(land_in.at[4 * px + 2 * py + c], stage.at[j], load_sems.at[j])
                 for j, (px, py) in enumerate(chips)]
        for ld in loads:
            ld.start()
        sends = []
        for j, (px, py) in enumerate(chips):
            loads[j].wait()
            dst = land_ref.at[4 * px + 2 * py + c]
            cp = pltpu.make_async_remote_copy(src_ref=stage.at[j], dst_ref=dst, send_sem=send_sems.at[j],
                                              recv_sem=recv_sems.at[j], device_id=sibling, device_id_type=MESH)
            cp.start()
            sends.append(cp)
        own_in.wait()
        own_out = pltpu.make_async_copy(stage.at[3], land_ref.at[4 * x + 2 * y + c], own_sem)
        own_out.start()
        for j, (px, py) in enumerate(chips):
            dst = land_ref.at[4 * px + 2 * py + (1 - c)]
            pltpu.make_async_remote_copy(src_ref=stage.at[j], dst_ref=dst, send_sem=send_sems.at[j],
                                         recv_sem=recv_sems.at[j], device_id=sibling,
                                         device_id_type=MESH).wait_recv()
        for cp in sends:
            cp.wait_send()
        own_out.wait()

    return pl.pallas_call(
        body,
        out_shape=jax.ShapeDtypeStruct(land.shape, land.dtype),
        in_specs=_hbm_specs(2),
        out_specs=pl.BlockSpec(memory_space=pl.ANY),
        scratch_shapes=[pltpu.VMEM((4, R, C), block.dtype), pltpu.SemaphoreType.DMA((4,)),
                        pltpu.SemaphoreType.DMA((3,)), pltpu.SemaphoreType.DMA((3,)), pltpu.SemaphoreType.DMA],
        input_output_aliases={1: 0},
        name=name,
        compiler_params=pltpu.CompilerParams(vmem_limit_bytes=VMEM_LIMIT),
    )(block, land)


def reduce_scatter_start(parts, name):
    def body(p_ref, land_ref, send_sem, recv_sem):
        x, y, c = _position()
        for px, py in [(1 - x, y), (x, 1 - y), (1 - x, 1 - y)]:
            pltpu.make_async_remote_copy(src_ref=p_ref.at[2 * px + py], dst_ref=land_ref.at[2 * x + y],
                                         send_sem=send_sem, recv_sem=recv_sem,
                                         device_id=(px, py, c), device_id_type=MESH).start()

    return _split_start(name, body, parts, parts.shape)


def small_all_gather(small):
    def body(small_ref, smalls, s_send, s_recv, s_local):
        x, y, c = _position()
        me = 4 * x + 2 * y + c
        lc = pltpu.make_async_copy(small_ref, smalls.at[me], s_local)
        lc.start()
        remote = []
        k = 0
        for dx in (0, 1):
            for dy in (0, 1):
                for dc in (0, 1):
                    if dx + dy + dc == 0:
                        continue
                    peer = (1 - x if dx else x, 1 - y if dy else y, 1 - c if dc else c)
                    rc = pltpu.make_async_remote_copy(
                        src_ref=small_ref, dst_ref=smalls.at[me],
                        send_sem=s_send.at[k], recv_sem=s_recv.at[k],
                        device_id=peer, device_id_type=MESH)
                    rc.start()
                    remote.append(rc)
                    k += 1
        for rc in remote:
            rc.wait()
        lc.wait()

    return pl.pallas_call(
        body,
        out_shape=jax.ShapeDtypeStruct((N_DEV,) + small.shape, small.dtype),
        in_specs=_hbm_specs(1),
        out_specs=pl.BlockSpec(memory_space=pl.ANY),
        scratch_shapes=[pltpu.SemaphoreType.DMA((7,)), pltpu.SemaphoreType.DMA((7,)), pltpu.SemaphoreType.DMA],
        name="small_all_gather",
    )(small)


def pair_add(grads, theirs, core, name):
    _, R, C = theirs.shape
    tr = R // 2

    def body(c_ref, a_ref, b_ref, o_ref):
        o_ref[...] = (a_ref[...].astype(F32) + b_ref[...].astype(F32)).astype(BF16)

    return pl.pallas_call(
        body,
        out_shape=jax.ShapeDtypeStruct(theirs.shape, BF16),
        grid_spec=pltpu.PrefetchScalarGridSpec(
            num_scalar_prefetch=1, grid=(4, R // tr),
            in_specs=[pl.BlockSpec((None, tr, C), lambda q, i, c: (2 * q + c[0], i, 0)),
                      pl.BlockSpec((None, tr, C), lambda q, i, c: (q, i, 0))],
            out_specs=pl.BlockSpec((None, tr, C), lambda q, i, c: (q, i, 0))),
        name=name,
        compiler_params=pltpu.CompilerParams(dimension_semantics=("parallel", "parallel"),
                                             vmem_limit_bytes=VMEM_LIMIT),
    )(core, grads, theirs)


def sum_slots(recv, off, rows, blk, name):
    nq, _, C = recv.shape
    ob = off // blk

    def body(r_ref, o_ref):
        acc = r_ref[0].astype(F32)
        for q in range(1, nq):
            acc = acc + r_ref[q].astype(F32)
        o_ref[...] = acc

    return _call(name, body, (rows // blk,),
                 [(recv, (nq, blk, C), lambda i: (0, ob + i, 0))],
                 [((rows, C), F32, (blk, C), lambda i: (i, 0))], sem=("parallel",))[0]


def sum_chips(parts, land, chip, off, rows, blk, name):
    C = parts.shape[2]
    ob = off // blk

    def body(c_ref, own_ref, a_ref, b_ref, d_ref, o_ref):
        o_ref[...] = ((own_ref[...].astype(F32) + a_ref[...].astype(F32)) + b_ref[...].astype(F32)) \
            + d_ref[...].astype(F32)

    def entry(flip):
        return pl.BlockSpec((None, blk, C), lambda i, c: (c[0] ^ flip, ob + i, 0))

    return pl.pallas_call(
        body,
        out_shape=jax.ShapeDtypeStruct((rows, C), F32),
        grid_spec=pltpu.PrefetchScalarGridSpec(
            num_scalar_prefetch=1, grid=(rows // blk,),
            in_specs=[entry(0), entry(1), entry(2), entry(3)],
            out_specs=pl.BlockSpec((blk, C), lambda i, c: (i, 0))),
        name=name,
        compiler_params=pltpu.CompilerParams(dimension_semantics=("parallel",), vmem_limit_bytes=VMEM_LIMIT),
    )(chip, parts, land, land, land)


def adamw(w, g, m, v, name):
    R, C = w.shape
    tr = R
    for cand in (256, 128, 64, 32, 16, 8):
        if R % cand == 0 and R > cand:
            tr = cand
            break
    c1 = 1.0 / (1.0 - ADAM_B1 ** ADAM_STEP)
    c2 = 1.0 / (1.0 - ADAM_B2 ** ADAM_STEP)

    def body(w_ref, g_ref, m_ref, v_ref, d_ref, nm_ref, nv_ref):
        gv = g_ref[...]
        nm = ADAM_B1 * m_ref[...] + (1.0 - ADAM_B1) * gv
        nv = ADAM_B2 * v_ref[...] + (1.0 - ADAM_B2) * (gv * gv)
        d_ref[...] = -ADAM_LR * ((nm * c1) / (jnp.sqrt(nv * c2) + ADAM_EPS) + ADAM_WD * w_ref[...])
        nm_ref[...] = nm
        nv_ref[...] = nv

    spec = ((tr, C), lambda i: (i, 0))
    out = ((R, C), F32) + spec
    return _call(name, body, (R // tr,), [(w,) + spec, (g,) + spec, (m,) + spec, (v,) + spec],
                 [out, out, out], sem=("parallel",))


def rms_fwd(x, g, name):
    S, D = x.shape
    tr = 512

    def body(x_ref, g_ref, o_ref):
        xv = x_ref[...]
        r = lax.rsqrt(jnp.mean(xv * xv, axis=-1, keepdims=True) + EPS)
        o_ref[...] = (xv * r * g_ref[...]).astype(BF16)

    return _call(name, body, (S // tr,),
                 [(x, (tr, D), lambda i: (i, 0)), (g, (1, D), lambda i: (0, 0))],
                 [((S, D), BF16, (tr, D), lambda i: (i, 0))], sem=("parallel",))[0]


def _rms_bwd_tile(dn, xv, gv):
    r = lax.rsqrt(jnp.mean(xv * xv, axis=-1, keepdims=True) + EPS)
    xh = xv * r
    dxh = dn * gv
    dx = r * (dxh - xh * jnp.mean(dxh * xh, axis=-1, keepdims=True))
    return dx, dn * xh


def final_loss(x, tgt, g, name):
    S, D = x.shape
    tr = 256

    def body(x_ref, t_ref, g_ref, l_ref, dx_ref, dg_ref):
        i = pl.program_id(0)
        xv, gv = x_ref[...], g_ref[...]
        r = lax.rsqrt(jnp.mean(xv * xv, axis=-1, keepdims=True) + EPS)
        xh = xv * r
        e = xh * gv - t_ref[...]
        part = 0.5 * jnp.sum(jnp.sum(e * e, axis=-1, keepdims=True) * (1.0 / D), axis=0, keepdims=True)
        dy = e * (1.0 / D)
        dxh = dy * gv
        dx_ref[...] = r * (dxh - xh * jnp.mean(dxh * xh, axis=-1, keepdims=True))
        dgp = jnp.sum(dy * xh, axis=0, keepdims=True)

        @pl.when(i == 0)
        def _():
            l_ref[...] = jnp.broadcast_to(part, l_ref.shape)
            dg_ref[...] = dgp

        @pl.when(i > 0)
        def _():
            l_ref[...] += jnp.broadcast_to(part, l_ref.shape)
            dg_ref[...] += dgp

    row = ((tr, D), lambda i: (i, 0))
    return _call(name, body, (S // tr,),
                 [(x,) + row, (tgt,) + row, (g, (1, D), lambda i: (0, 0))],
                 [((1, LANES), F32, (1, LANES), lambda i: (0, 0)), ((S, D), F32) + row,
                  ((1, D), F32, (1, D), lambda i: (0, 0))], sem=("arbitrary",))


FFN_TF = 4 * FFN_SHARD


def _ffn_w_spec(G, which, imap):
    D = G.shape[2]
    return (G, (4, FFN_SHARD, D), lambda *idx: (imap(*idx), which, 0))


def ffn_up(n, G, name):
    S, D = n.shape
    F = N_DEV * FFN_SHARD
    tm = 512

    def body(n_ref, w1_ref, w3_ref, ab_ref):
        nv = n_ref[...]
        ab_ref[0] = _dot(nv, w1_ref[...].reshape(FFN_TF, D), 1, 1).astype(BF16)
        ab_ref[1] = _dot(nv, w3_ref[...].reshape(FFN_TF, D), 1, 1).astype(BF16)

    return _call(name, body, (F // FFN_TF, S // tm),
                 [(n, (tm, D), lambda j, i: (i, 0)),
                  _ffn_w_spec(G, 0, lambda j, i: j), _ffn_w_spec(G, 1, lambda j, i: j)],
                 [((2, S, F), BF16, (2, tm, FFN_TF), lambda j, i: (0, i, j))],
                 sem=("parallel", "parallel"))[0]


def ffn_down(ab, G, x, name):
    _, S, F = ab.shape
    D = x.shape[1]
    tm = 512
    nk = F // FFN_TF

    def body(ab_ref, w2_ref, x_ref, o_ref, acc_ref):
        k = pl.program_id(1)
        av, bv = ab_ref[0].astype(F32), ab_ref[1].astype(F32)
        h = (av * _sigmoid(av) * bv).astype(BF16)
        p = _dot(h, w2_ref[...].reshape(FFN_TF, D))

        @pl.when(k == 0)
        def _():
            acc_ref[...] = p

        @pl.when(k > 0)
        def _():
            acc_ref[...] += p

        @pl.when(k == nk - 1)
        def _():
            o_ref[...] = x_ref[...] + 0.5 * acc_ref[...]

    return _call(name, body, (S // tm, nk),
                 [(ab, (2, tm, FFN_TF), lambda i, k: (0, i, k)), _ffn_w_spec(G, 2, lambda i, k: k),
                  (x, (tm, D), lambda i, k: (i, 0))],
                 [((S, D), F32, (tm, D), lambda i, k: (i, 0))],
                 scratch=[pltpu.VMEM((tm, D), F32)], sem=("parallel", "arbitrary"))[0]


def ffn_bwd_weights(dxo, ab, n, G, name):
    _, S, F = ab.shape
    D = dxo.shape[1]
    tm = 512
    nf = F // FFN_TF

    def down_body(d_ref, w2_ref, ab_ref, o_ref):
        dh = 0.5 * _dot(d_ref[...].astype(BF16), w2_ref[...].reshape(FFN_TF, D), 1, 1)
        av, bv = ab_ref[0].astype(F32), ab_ref[1].astype(F32)
        sig = _sigmoid(av)
        silu = av * sig
        o_ref[0] = (dh * bv * (sig * (1.0 + av * (1.0 - sig)))).astype(BF16)
        o_ref[1] = (dh * silu).astype(BF16)
        o_ref[2] = (silu * bv).astype(BF16)

    dabh = _call(name + "_down_bwd", down_body, (nf, S // tm),
                 [(dxo, (tm, D), lambda j, i: (i, 0)), _ffn_w_spec(G, 2, lambda j, i: j),
                  (ab, (2, tm, FFN_TF), lambda j, i: (0, i, j))],
                 [((3, S, F), BF16, (3, tm, FFN_TF), lambda j, i: (0, i, j))],
                 sem=("parallel", "parallel"))[0]

    tk = WGRAD_TK
    nk = S // tk
    gshape = (N_DEV, 3 * FFN_SHARD, D)

    def dw2_body(h_ref, d_ref, o_ref, acc_ref):
        k = pl.program_id(1)
        p = _dot(h_ref[...], d_ref[...].astype(BF16), 0, 0)

        @pl.when(k == 0)
        def _():
            acc_ref[...] = p

        @pl.when(k > 0)
        def _():
            acc_ref[...] += p

        @pl.when(k == nk - 1)
        def _():
            o_ref[...] = (0.5 * acc_ref[...]).astype(BF16).reshape(4, FFN_SHARD, D)

    gw = _call(name + "_dw2", dw2_body, (nf, nk),
               [(dabh, (None, tk, FFN_TF), lambda j, k: (2, k, j)), (dxo, (tk, D), lambda j, k: (k, 0))],
               [(gshape, BF16, (4, FFN_SHARD, D), lambda j, k: (j, 2, 0))],
               scratch=[pltpu.VMEM((FFN_TF, D), F32)], sem=("parallel", "arbitrary"))[0]

    def dw13_body(gw_ref, dab_ref, n_ref, o_ref, acc_ref):
        k = pl.program_id(2)
        p = _dot(dab_ref[...], n_ref[...], 0, 0)

        @pl.when(k == 0)
        def _():
            acc_ref[...] = p

        @pl.when(k > 0)
        def _():
            acc_ref[...] += p

        @pl.when(k == nk - 1)
        def _():
            o_ref[...] = acc_ref[...].astype(BF16).reshape(4, FFN_SHARD, D)

    gw = pl.pallas_call(
        dw13_body,
        out_shape=jax.ShapeDtypeStruct(gshape, BF16),
        grid=(2, nf, nk),
        in_specs=[pl.BlockSpec(memory_space=pl.ANY),
                  pl.BlockSpec((None, tk, FFN_TF), lambda w, j, k: (w, k, j)),
                  pl.BlockSpec((tk, D), lambda w, j, k: (k, 0))],
        out_specs=pl.BlockSpec((4, FFN_SHARD, D), lambda w, j, k: (j, w, 0)),
        scratch_shapes=[pltpu.VMEM((FFN_TF, D), F32)],
        input_output_aliases={0: 0},
        name=name + "_dw13",
        compiler_params=pltpu.CompilerParams(dimension_semantics=("parallel", "parallel", "arbitrary"),
                                             vmem_limit_bytes=VMEM_LIMIT),
    )(gw, dabh, n)
    return dabh, gw


def ffn_bwd_input(dabh, G, x_in, g, dxo, name):
    _, S, F = dabh.shape
    D = x_in.shape[1]
    tm = 512
    nf = F // FFN_TF

    def dn_body(dab_ref, w1_ref, w3_ref, x_ref, d_ref, g_ref, dx_ref, dg_ref, acc_ref):
        i, k = pl.program_id(0), pl.program_id(1)
        p = _dot(dab_ref[0], w1_ref[...].reshape(FFN_TF, D)) + _dot(dab_ref[1], w3_ref[...].reshape(FFN_TF, D))

        @pl.when(k == 0)
        def _():
            acc_ref[...] = p

        @pl.when(k > 0)
        def _():
            acc_ref[...] += p

        @pl.when(k == nf - 1)
        def _():
            dx, dgt = _rms_bwd_tile(acc_ref[...], x_ref[...], g_ref[...])
            dx_ref[...] = d_ref[...] + dx
            dgp = jnp.sum(dgt, axis=0, keepdims=True)

            @pl.when(i == 0)
            def _():
                dg_ref[...] = dgp

            @pl.when(i > 0)
            def _():
                dg_ref[...] += dgp

    dx, dg = _call(name + "_dn", dn_body, (S // tm, nf),
                   [(dabh, (2, tm, FFN_TF), lambda i, k: (0, i, k)),
                    _ffn_w_spec(G, 0, lambda i, k: k), _ffn_w_spec(G, 1, lambda i, k: k),
                    (x_in, (tm, D), lambda i, k: (i, 0)), (dxo, (tm, D), lambda i, k: (i, 0)),
                    (g, (1, D), lambda i, k: (0, 0))],
                   [((S, D), F32, (tm, D), lambda i, k: (i, 0)), ((1, D), F32, (1, D), lambda i, k: (0, 0))],
                   scratch=[pltpu.VMEM((tm, D), F32)], sem=("arbitrary", "arbitrary"))
    return dx, dg


PROJ_TN = 512


def in_proj(h, Gm, first_tile, n_tiles, dtype, name):
    S, D = h.shape
    tm = 1024

    def body(h_ref, w_ref, o_ref):
        o_ref[...] = _dot(h_ref[...], w_ref[...]).astype(dtype)

    return _call(name, body, (n_tiles, S // tm),
                 [(h, (tm, D), lambda j, i: (i, 0)),
                  (Gm, (None, D, PROJ_TN), lambda j, i: ((first_tile + j) // 2, 0, (first_tile + j) % 2))],
                 [((S, n_tiles * PROJ_TN), dtype, (tm, PROJ_TN), lambda j, i: (i, j))],
                 sem=("parallel", "parallel"))[0]


def in_proj_bwd_dw(dproj, h, gm_grads, name):
    S, D = h.shape
    NT = dproj.shape[1] // PROJ_TN
    tk = WGRAD_TK
    nk = S // tk

    def dw_body(gm_ref, h_ref, d_ref, o_ref, acc_ref):
        k = pl.program_id(1)
        p = _dot(h_ref[...], d_ref[...], 0, 0)

        @pl.when(k == 0)
        def _():
            acc_ref[...] = p

        @pl.when(k > 0)
        def _():
            acc_ref[...] += p

        @pl.when(k == nk - 1)
        def _():
            o_ref[...] = acc_ref[...].astype(BF16)

    return pl.pallas_call(
        dw_body,
        out_shape=jax.ShapeDtypeStruct(gm_grads.shape, BF16),
        grid=(NT, nk),
        in_specs=[pl.BlockSpec(memory_space=pl.ANY),
                  pl.BlockSpec((tk, D), lambda j, k: (k, 0)),
                  pl.BlockSpec((tk, PROJ_TN), lambda j, k: (k, j))],
        out_specs=pl.BlockSpec((None, D, PROJ_TN), lambda j, k: (j // 2, 0, j % 2)),
        scratch_shapes=[pltpu.VMEM((D, PROJ_TN), F32)],
        input_output_aliases={0: 0},
        name=name + "_dw",
        compiler_params=pltpu.CompilerParams(dimension_semantics=("parallel", "arbitrary"),
                                             vmem_limit_bytes=VMEM_LIMIT),
    )(gm_grads, h, dproj)


def in_proj_bwd_dh(dproj, Gm, x_in, g, dres, name):
    S, D = x_in.shape
    tm = 512
    C = Gm.shape[2]
    n_sh = dproj.shape[1] // C

    def dh_body(d_ref, w_ref, x_ref, r_ref, g_ref, dx_ref, dg_ref, acc_ref):
        i, k = pl.program_id(0), pl.program_id(1)
        p = _dot(d_ref[...], w_ref[...], 1, 1)

        @pl.when(k == 0)
        def _():
            acc_ref[...] = p

        @pl.when(k > 0)
        def _():
            acc_ref[...] += p

        @pl.when(k == n_sh - 1)
        def _():
            dx, dgt = _rms_bwd_tile(acc_ref[...], x_ref[...], g_ref[...])
            dx_ref[...] = r_ref[...] + dx
            dgp = jnp.sum(dgt, axis=0, keepdims=True)

            @pl.when(i == 0)
            def _():
                dg_ref[...] = dgp

            @pl.when(i > 0)
            def _():
                dg_ref[...] += dgp

    dx, dg = _call(name + "_dh", dh_body, (S // tm, n_sh),
                   [(dproj, (tm, C), lambda i, k: (i, k)),
                    (Gm, (None, D, C), lambda i, k: (k, 0, 0)),
                    (x_in, (tm, D), lambda i, k: (i, 0)), (dres, (tm, D), lambda i, k: (i, 0)),
                    (g, (1, D), lambda i, k: (0, 0))],
                   [((S, D), F32, (tm, D), lambda i, k: (i, 0)), ((1, D), F32, (1, D), lambda i, k: (0, 0))],
                   scratch=[pltpu.VMEM((tm, D), F32)], sem=("arbitrary", "arbitrary"))
    return dx, dg


def _t5_bucket(rel):
    n = N_BUCKETS // 2
    max_exact = n // 2
    ret = jnp.where(rel > 0, n, 0)
    a = jnp.abs(rel)
    af = jnp.maximum(a, 1).astype(F32)
    large = max_exact + (jnp.log(af / max_exact) / math.log(MAX_DISTANCE / max_exact)
                         * (n - max_exact)).astype(jnp.int32)
    large = jnp.minimum(large, n - 1)
    return ret + jnp.where(a < max_exact, a, large)


def _bucket_tables():
    qi = jnp.arange(A_TQ, dtype=jnp.int32)[:, None]
    kj = jnp.arange(A_WIN, dtype=jnp.int32)[None, :]
    rel = kj - HALF_WINDOW - qi
    return jnp.stack([_t5_bucket(rel * d) for d in DILATIONS], axis=0)


def bias_build(rel_bias, buckets):
    def body(tab_ref, bk_ref, o_ref):
        col = pl.program_id(0) * HEADS_PER_GROUP_A + pl.program_id(1)
        bk = bk_ref[...]
        acc = jnp.zeros(bk.shape, F32)
        for b in range(N_BUCKETS):
            acc = jnp.where(bk == b, tab_ref[b, col], acc)
        qi = lax.broadcasted_iota(jnp.int32, bk.shape, 0)
        kj = lax.broadcasted_iota(jnp.int32, bk.shape, 1)
        o_ref[...] = jnp.where(jnp.abs(kj - HALF_WINDOW - qi) <= HALF_WINDOW, acc, NEG_INF)

    return pl.pallas_call(
        body,
        out_shape=jax.ShapeDtypeStruct((3, HEADS_PER_GROUP_A, A_TQ, A_WIN), F32),
        grid=(3, HEADS_PER_GROUP_A),
        in_specs=[pl.BlockSpec(memory_space=pltpu.SMEM),
                  pl.BlockSpec((None, A_TQ, A_WIN), lambda g, h: (g, 0, 0))],
        out_specs=pl.BlockSpec((None, None, A_TQ, A_WIN), lambda g, h: (g, h, 0, 0)),
        name="a_bias_build",
        compiler_params=pltpu.CompilerParams(dimension_semantics=("parallel", "parallel")),
    )(rel_bias, buckets)


def bias_bwd(dbias, buckets):
    def body(d_ref, bk_ref, o_ref):
        bk = bk_ref[...]
        dv = d_ref[...]
        for b in range(N_BUCKETS):
            part = jnp.sum(jnp.where(bk == b, dv, 0.0), axis=1, keepdims=True)
            o_ref[b:b + 1, :] = jnp.broadcast_to(jnp.sum(part, axis=0, keepdims=True), (1, LANES))

    out = pl.pallas_call(
        body,
        out_shape=jax.ShapeDtypeStruct((3, HEADS_PER_GROUP_A, N_BUCKETS, LANES), F32),
        grid=(3, HEADS_PER_GROUP_A),
        in_specs=[pl.BlockSpec((None, None, A_TQ, A_WIN), lambda g, h: (g, h, 0, 0)),
                  pl.BlockSpec((None, A_TQ, A_WIN), lambda g, h: (g, 0, 0))],
        out_specs=pl.BlockSpec((None, None, N_BUCKETS, LANES), lambda g, h: (g, h, 0, 0)),
        name="a_bias_bwd",
        compiler_params=pltpu.CompilerParams(dimension_semantics=("parallel", "parallel")),
    )(dbias, buckets)
    return out[:, :, :, 0].transpose(2, 0, 1).reshape(N_BUCKETS, 3 * HEADS_PER_GROUP_A)


def _a_fill_padded(pad_ref, src_ref, L):
    zeros = jnp.zeros((HALF_WINDOW, LANES), pad_ref.dtype)
    pad_ref[0:HALF_WINDOW, :] = zeros
    pad_ref[HALF_WINDOW + L:2 * HALF_WINDOW + L, :] = zeros
    pad_ref[HALF_WINDOW:HALF_WINDOW + L, :] = src_ref[...]


def _a_key_valid(qb, L):
    kidx = qb * A_TQ - HALF_WINDOW + lax.broadcasted_iota(jnp.int32, (A_TQ, A_WIN), 1)
    return (kidx >= 0) & (kidx < L)


def a_fwd(proj_a, bias_g, g, name):
    S = proj_a.shape[0]
    d = DILATIONS[g]
    L = S // d
    nqb = L // A_TQ
    nb = A_BLOCKS_PER_TOKEN
    pv = proj_a.reshape(L, d * A_QKV_WIDTH)

    def body(q_ref, k_ref, v_ref, b_ref, o_ref, l_ref, kpad, vpad):
        _a_fill_padded(kpad, k_ref, L)
        _a_fill_padded(vpad, v_ref, L)
        lane = lax.broadcasted_iota(jnp.int32, (A_TQ, LANES), 1)

        def block(qb, carry):
            start = pl.multiple_of(qb * A_TQ, A_TQ)
            kw = kpad[pl.ds(start, A_WIN), :]
            vw = vpad[pl.ds(start, A_WIN), :]
            q = q_ref[pl.ds(start, A_TQ), :]
            valid = _a_key_valid(qb, L)
            outs, lses = [], []
            for h in range(2):
                qh = jnp.where((lane >= HEAD_DIM_A * h) & (lane < HEAD_DIM_A * (h + 1)), q, jnp.zeros_like(q))
                s = _dot(qh, kw, 1, 1) * (HEAD_DIM_A ** -0.5) + b_ref[h]
                s = jnp.where(valid, s, NEG_INF)
                m = jnp.max(s, axis=-1, keepdims=True)
                e = jnp.exp(s - m)
                l = jnp.sum(e, axis=-1, keepdims=True)
                outs.append(_dot(e.astype(BF16), vw) / l)
                lses.append(m + jnp.log(l))
            o_ref[pl.ds(start, A_TQ), :] = jnp.where(lane < HEAD_DIM_A, outs[0], outs[1])
            l_ref[pl.ds(start, A_TQ), :] = jnp.where(lane < HEAD_DIM_A, lses[0], lses[1])
            return carry

        lax.fori_loop(0, nqb, block, 0, unroll=min(A_UNROLL, nqb))

    col = lambda which: (lambda r, hp: r * nb + (which * 3 + g) * 4 + hp)
    out_spec = ((L, d * GROUP_WIDTH_A), F32, (L, LANES), lambda r, hp: (0, r * 4 + hp))
    o, lse = _call(name, body, (d, 4),
                   [(pv, (L, LANES), lambda r, hp: (0, col(0)(r, hp))),
                    (pv, (L, LANES), lambda r, hp: (0, col(1)(r, hp))),
                    (pv, (L, LANES), lambda r, hp: (0, col(2)(r, hp))),
                    (bias_g, (2, A_TQ, A_WIN), lambda r, hp: (hp, 0, 0))],
                   [out_spec, out_spec],
                   scratch=[pltpu.VMEM((L + 2 * HALF_WINDOW, LANES), BF16)] * 2,
                   sem=("parallel", "parallel"))
    return o.reshape(S, GROUP_WIDTH_A), lse.reshape(S, GROUP_WIDTH_A)


def a_combine(outs, lses, name):
    S, W = outs[0].shape
    tr = 512

    def body(o0, o1, o2, l0, l1, l2, oa_ref, lt_ref):
        a, b, c = l0[...], l1[...], l2[...]
        m = jnp.maximum(jnp.maximum(a, b), c)
        ea, eb, ec = jnp.exp(a - m), jnp.exp(b - m), jnp.exp(c - m)
        z = ea + eb + ec
        oa_ref[...] = ((ea * o0[...] + eb * o1[...] + ec * o2[...]) / z).astype(BF16)
        lt_ref[...] = m + jnp.log(z)

    spec = ((tr, W), lambda i: (i, 0))
    return _call(name, body, (S // tr,), [(a,) + spec for a in (*outs, *lses)],
                 [((S, W), BF16) + spec, ((S, W), F32) + spec], sem=("parallel",))


def a_bwd(proj_a, bias_g, do_a, o_a, lse_tot, g, name):
    S = proj_a.shape[0]
    d = DILATIONS[g]
    L = S // d
    nqb = L // A_TQ
    nb = A_BLOCKS_PER_TOKEN
    pv = proj_a.reshape(L, d * A_QKV_WIDTH)
    view = lambda a: a.reshape(L, d * GROUP_WIDTH_A)
    scale = HEAD_DIM_A ** -0.5

    def body(q_ref, k_ref, v_ref, b_ref, do_ref, o_ref, l_ref, dq_ref, dk_ref, dv_ref, db_ref,
             kpad, vpad, dkacc, dvacc):
        r = pl.program_id(1)
        _a_fill_padded(kpad, k_ref, L)
        _a_fill_padded(vpad, v_ref, L)
        dkacc[...] = jnp.zeros(dkacc.shape, F32)
        dvacc[...] = jnp.zeros(dvacc.shape, F32)

        @pl.when(r == 0)
        def _():
            db_ref[...] = jnp.zeros(db_ref.shape, F32)

        lane = lax.broadcasted_iota(jnp.int32, (A_TQ, LANES), 1)

        def block(qb, carry):
            start = pl.multiple_of(qb * A_TQ, A_TQ)
            rows = pl.ds(start, A_TQ)
            kw = kpad[pl.ds(start, A_WIN), :]
            vw = vpad[pl.ds(start, A_WIN), :]
            q = q_ref[rows, :]
            do = do_ref[rows, :]
            ov = o_ref[rows, :].astype(F32)
            lt = l_ref[rows, :]
            valid = _a_key_valid(qb, L)
            dqs = []
            dk_win = jnp.zeros((A_WIN, LANES), F32)
            dv_win = jnp.zeros((A_WIN, LANES), F32)
            for h in range(2):
                mh = (lane >= HEAD_DIM_A * h) & (lane < HEAD_DIM_A * (h + 1))
                qh = jnp.where(mh, q, jnp.zeros_like(q))
                doh = jnp.where(mh, do, 0.0)
                s = _dot(qh, kw, 1, 1) * scale + b_ref[h]
                s = jnp.where(valid, s, NEG_INF)
                p = jnp.exp(s - lt[:, HEAD_DIM_A * h:HEAD_DIM_A * h + 1])
                t = jnp.sum(doh * ov, axis=-1, keepdims=True)
                dob = doh.astype(BF16)
                ds = p * (_dot(dob, vw, 1, 1) - t)
                db_ref[h] += ds
                dsb = (ds * scale).astype(BF16)
                dqs.append(_dot(dsb, kw))
                dk_win = dk_win + _dot(dsb, qh, 0, 0)
                dv_win = dv_win + _dot(p.astype(BF16), dob, 0, 0)
            dq_ref[rows, :] = jnp.where(lane < HEAD_DIM_A, dqs[0], dqs[1]).astype(BF16)
            dkacc[pl.ds(start, A_WIN), :] += dk_win
            dvacc[pl.ds(start, A_WIN), :] += dv_win
            return carry

        lax.fori_loop(0, nqb, block, 0, unroll=min(A_UNROLL, nqb))
        dk_ref[...] = dkacc[HALF_WINDOW:HALF_WINDOW + L, :].astype(BF16)
        dv_ref[...] = dvacc[HALF_WINDOW:HALF_WINDOW + L, :].astype(BF16)

    col = lambda which: (lambda hp, r: r * nb + (which * 3 + g) * 4 + hp)
    slab = ((L, LANES), lambda hp, r: (0, r * 4 + hp))
    oshape = (L, d * GROUP_WIDTH_A)
    dq, dk, dv, db = _call(
        name, body, (4, d),
        [(pv, (L, LANES), lambda hp, r: (0, col(0)(hp, r))),
         (pv, (L, LANES), lambda hp, r: (0, col(1)(hp, r))),
         (pv, (L, LANES), lambda hp, r: (0, col(2)(hp, r))),
         (bias_g, (2, A_TQ, A_WIN), lambda hp, r: (hp, 0, 0)),
         (view(do_a),) + slab, (view(o_a),) + slab, (view(lse_tot),) + slab],
        [(oshape, BF16) + slab, (oshape, BF16) + slab, (oshape, BF16) + slab,
         ((HEADS_PER_GROUP_A, A_TQ, A_WIN), F32, (2, A_TQ, A_WIN), lambda hp, r: (hp, 0, 0))],
        scratch=[pltpu.VMEM((L + 2 * HALF_WINDOW, LANES), BF16)] * 2
        + [pltpu.VMEM((L + 2 * HALF_WINDOW, LANES), F32)] * 2,
        sem=("parallel", "arbitrary"))
    return dq.reshape(S, GROUP_WIDTH_A), dk.reshape(S, GROUP_WIDTH_A), dv.reshape(S, GROUP_WIDTH_A), db


def _rope_tables(S):
    rows = S // GRID_W
    row = jnp.repeat(jnp.arange(rows, dtype=F32), GRID_W)
    col = jnp.tile(jnp.arange(GRID_W, dtype=F32), rows)
    n_freq = HEAD_DIM_B // 4
    freq = ROPE_THETA ** (-jnp.arange(n_freq, dtype=F32) / n_freq)
    ang = jnp.concatenate([row[:, None] * freq, col[:, None] * freq], axis=-1)
    cos, sin = jnp.cos(ang), jnp.sin(ang)
    return jnp.repeat(cos, 2, axis=-1), jnp.stack([-sin, sin], axis=-1).reshape(S, HEAD_DIM_B)


def _swap_pairs(y):
    lane = lax.broadcasted_iota(jnp.int32, y.shape, 1)
    return jnp.where(lane % 2 == 0, pltpu.roll(y, LANES - 1, 1), pltpu.roll(y, 1, 1))


def qkv_prep(proj_b, gains, cos_t, sin_t, name):
    S = proj_b.shape[0]
    ts = 512
    n_rot = N_HEADS_B + N_KV_B

    def body(x_ref, g_ref, c_ref, s_ref, o_ref):
        hb = pl.program_id(0)

        @pl.when(hb < n_rot)
        def _():
            xv = x_ref[...]
            r = lax.rsqrt(jnp.mean(xv * xv, axis=-1, keepdims=True) + EPS)
            yv = xv * r * g_ref[...]
            o_ref[...] = (yv * c_ref[...] + _swap_pairs(yv) * s_ref[...]).astype(BF16)

        @pl.when(hb >= n_rot)
        def _():
            o_ref[...] = x_ref[...].astype(BF16)

    nh = n_rot + N_KV_B
    return _call(name, body, (nh, S // ts),
                 [(proj_b, (ts, LANES), lambda hb, i: (i, hb)), (gains, (1, LANES), lambda hb, i: (0, hb)),
                  (cos_t, (ts, LANES), lambda hb, i: (i, 0)), (sin_t, (ts, LANES), lambda hb, i: (i, 0))],
                 [((S, nh * LANES), BF16, (ts, LANES), lambda hb, i: (i, hb))],
                 sem=("parallel", "parallel"))[0]


def qk_prep_bwd(dr, proj_b, col0, gain, cos_t, sin_t, name):
    S, W = dr.shape
    H = W // LANES
    ts = 512

    def body(d_ref, x_ref, g_ref, c_ref, s_ref, dx_ref, dg_ref):
        hb, i = pl.program_id(0), pl.program_id(1)
        dout = d_ref[...]
        dy = dout * c_ref[...] + _swap_pairs(dout * s_ref[...])
        dx, dgt = _rms_bwd_tile(dy, x_ref[...], g_ref[...])
        dx_ref[...] = dx.astype(BF16)
        dgp = jnp.sum(dgt, axis=0, keepdims=True)

        @pl.when((hb == 0) & (i == 0))
        def _():
            dg_ref[...] = dgp

        @pl.when((hb > 0) | (i > 0))
        def _():
            dg_ref[...] += dgp

    return _call(name, body, (H, S // ts),
                 [(dr, (ts, LANES), lambda hb, i: (i, hb)), (proj_b, (ts, LANES), lambda hb, i: (i, col0 + hb)),
                  (gain, (1, LANES), lambda hb, i: (0, 0)),
                  (cos_t, (ts, LANES), lambda hb, i: (i, 0)), (sin_t, (ts, LANES), lambda hb, i: (i, 0))],
                 [((S, W), BF16, (ts, LANES), lambda hb, i: (i, hb)),
                  ((1, LANES), F32, (1, LANES), lambda hb, i: (0, 0))],
                 sem=("arbitrary", "arbitrary"))


def _row_sums(x):
    hi = x.astype(BF16)
    lo = (x - hi.astype(F32)).astype(BF16)
    ones = jnp.ones((8, LANES), BF16)
    return (_dot(ones, hi, 1, 1) + _dot(ones, lo, 1, 1))[0:1, :]


def flash_fwd(qkv, name):
    S = qkv.shape[0]
    tq = B_TQ_FWD
    scale = HEAD_DIM_B ** -0.5

    def body(q_ref, k_ref, v_ref, o_ref, l_ref):
        s = _dot(q_ref[...], k_ref[...], 1, 1) * scale
        m = jnp.max(s, axis=-1, keepdims=True)
        e = jnp.exp(s - m)
        l = jnp.sum(e, axis=-1, keepdims=True)
        o_ref[...] = (_dot(e.astype(BF16), v_ref[...]) / l).astype(BF16)
        lse = jnp.broadcast_to(m + jnp.log(l), (tq, LANES))
        l_ref[...] = _row_sums(lse) * (1.0 / LANES)

    head = lambda g, h, i: (i, g * GQA_GROUP_B + h)
    return _call(name, body, (N_KV_B, GQA_GROUP_B, S // tq),
                 [(qkv, (tq, LANES), head),
                  (qkv, (S, LANES), lambda g, h, i: (0, N_HEADS_B + g)),
                  (qkv, (S, LANES), lambda g, h, i: (0, N_HEADS_B + N_KV_B + g))],
                 [((S, N_HEADS_B * LANES), BF16, (tq, LANES), head),
                  ((N_HEADS_B, 1, S), F32, (None, 1, tq), lambda g, h, i: (g * GQA_GROUP_B + h, 0, i))],
                 sem=("parallel", "parallel", "parallel"))


def flash_bwd(qkv, k_t, do_b, o_b, lse, name):
    S = qkv.shape[0]
    tq = B_TQ_BWD
    nq = S // tq
    scale = HEAD_DIM_B ** -0.5

    def body(q_ref, k_ref, v_ref, kt_ref, do_ref, o_ref, l_ref, dq_ref, dk_ref, dv_ref, dkacc, dvacc):
        h, i = pl.program_id(1), pl.program_id(2)

        @pl.when((h == 0) & (i == 0))
        def _():
            dkacc[...] = jnp.zeros(dkacc.shape, F32)
            dvacc[...] = jnp.zeros(dvacc.shape, F32)

        q = q_ref[...]
        do = do_ref[...]
        dob = do.astype(BF16)
        t = _row_sums(do * o_ref[...].astype(F32))
        pt = jnp.exp(_dot(k_ref[...], q, 1, 1) * scale - l_ref[...])
        dst = pt * (_dot(v_ref[...], dob, 1, 1) - t) * scale
        dsb = dst.astype(BF16)
        dvacc[...] += _dot(pt.astype(BF16), dob)
        dkacc[...] += _dot(dsb, q)
        dq_ref[...] = _dot(kt_ref[...], dsb).T

        @pl.when((h == GQA_GROUP_B - 1) & (i == nq - 1))
        def _():
            dk_ref[...] = dkacc[...]
            dv_ref[...] = dvacc[...].astype(BF16)

    head = lambda g, h, i: (i, g * GQA_GROUP_B + h)
    return _call(name, body, (N_KV_B, GQA_GROUP_B, nq),
                 [(qkv, (tq, LANES), head),
                  (qkv, (S, LANES), lambda g, h, i: (0, N_HEADS_B + g)),
                  (qkv, (S, LANES), lambda g, h, i: (0, N_HEADS_B + N_KV_B + g)),
                  (k_t, (LANES, S), lambda g, h, i: (g, 0)),
                  (do_b, (tq, LANES), head), (o_b, (tq, LANES), head),
                  (lse, (None, 1, tq), lambda g, h, i: (g * GQA_GROUP_B + h, 0, i))],
                 [((S, N_HEADS_B * LANES), F32, (tq, LANES), head),
                  ((S, N_KV_B * LANES), F32, (S, LANES), lambda g, h, i: (0, g)),
                  ((S, N_KV_B * LANES), BF16, (S, LANES), lambda g, h, i: (0, g))],
                 scratch=[pltpu.VMEM((S, LANES), F32)] * 2,
                 sem=("parallel", "arbitrary", "arbitrary"))


MERGE_TN = 512


def _mix_rows_spec(Gm, row0, n_slots, slot_map, cols=None, col_map=None):
    C = Gm.shape[2] if cols is None else cols
    cm = (lambda *idx: 0) if col_map is None else col_map
    return (Gm, (n_slots, LANES, C), lambda *idx: (slot_map(*idx), row0 // LANES, cm(*idx)))


def merge_fwd(o_a, o_b, w_a, Gm, proj_b, b_gate, name):
    S = o_a.shape[0]
    D = w_a.shape[1]
    tm, tn = 512, MERGE_TN
    ga0, gb0 = PB_GATE_A // tn, PB_GATE_B // tn

    def body(oa_ref, ob_ref, wa_ref, wb_ref, pa_ref, pb_ref, ba_ref, bb_ref, m_ref, ya_ref, yb_ref):
        ya = _dot(oa_ref[...], wa_ref[...])
        yb = _dot(ob_ref[...], wb_ref[...].reshape(N_DEV * LANES, tn))
        ga = _sigmoid(pa_ref[...] + ba_ref[...])
        gb = _sigmoid(pb_ref[...] + bb_ref[...])
        m_ref[...] = (ga * ya + gb * yb).astype(BF16)
        ya_ref[...] = ya.astype(BF16)
        yb_ref[...] = yb.astype(BF16)

    out = ((S, D), BF16, (tm, tn), lambda j, i: (i, j))
    return _call(name, body, (D // tn, S // tm),
                 [(o_a, (tm, o_a.shape[1]), lambda j, i: (i, 0)), (o_b, (tm, o_b.shape[1]), lambda j, i: (i, 0)),
                  (w_a, (w_a.shape[0], tn), lambda j, i: (0, j)),
                  _mix_rows_spec(Gm, MIX_WB, N_DEV, lambda j, i: 0, cols=tn, col_map=lambda j, i: j),
                  (proj_b, (tm, tn), lambda j, i: (i, ga0 + j)), (proj_b, (tm, tn), lambda j, i: (i, gb0 + j)),
                  (b_gate, (1, tn), lambda j, i: (0, j)), (b_gate, (1, tn), lambda j, i: (0, D // tn + j))],
                 [out, out, out], sem=("parallel", "parallel"))


def out_proj(merged, Gm, x, name):
    S, D = x.shape
    tm, tn = 512, MERGE_TN

    def body(m_ref, w_ref, x_ref, o_ref):
        o_ref[...] = x_ref[...] + _dot(m_ref[...], w_ref[...].reshape(N_DEV * LANES, tn))

    return _call(name, body, (D // tn, S // tm),
                 [(merged, (tm, D), lambda j, i: (i, 0)),
                  _mix_rows_spec(Gm, MIX_WOUT, N_DEV, lambda j, i: 0, cols=tn, col_map=lambda j, i: j),
                  (x, (tm, tn), lambda j, i: (i, j))],
                 [((S, D), F32, (tm, tn), lambda j, i: (i, j))], sem=("parallel", "parallel"))[0]


def merge_bwd(dx2, Gm, ya, yb, proj_b, b_gate, name):
    S, D = dx2.shape
    tm, tn = 512, MERGE_TN
    nn = D // tn
    ga0, gb0 = PB_GATE_A // tn, PB_GATE_B // tn

    def body(d_ref, w_ref, ya_ref, yb_ref, pa_ref, pb_ref, ba_ref, bb_ref, dya_ref, dyb_ref, dg_ref, dbg_ref):
        i = pl.program_id(1)
        dm = _dot(d_ref[...].astype(BF16), w_ref[...].reshape(tn, D), 1, 1)
        ga = _sigmoid(pa_ref[...] + ba_ref[...])
        gb = _sigmoid(pb_ref[...] + bb_ref[...])
        dya_ref[...] = (dm * ga).astype(BF16)
        dyb_ref[...] = (dm * gb).astype(BF16)
        dpa = dm * ya_ref[...].astype(F32) * ga * (1.0 - ga)
        dpb = dm * yb_ref[...].astype(F32) * gb * (1.0 - gb)
        dg_ref[0] = dpa.astype(BF16)
        dg_ref[1] = dpb.astype(BF16)
        sa = jnp.sum(dpa, axis=0, keepdims=True)
        sb = jnp.sum(dpb, axis=0, keepdims=True)

        @pl.when(i == 0)
        def _():
            dbg_ref[0] = sa
            dbg_ref[1] = sb

        @pl.when(i > 0)
        def _():
            dbg_ref[0] += sa
            dbg_ref[1] += sb

    tile = ((tm, tn), lambda j, i: (i, j))
    dya, dyb, dgate, dbg = _call(
        name, body, (nn, S // tm),
        [(dx2, (tm, D), lambda j, i: (i, 0)),
         _mix_rows_spec(Gm, MIX_WOUT, tn // LANES, lambda j, i: j),
         (ya,) + tile, (yb,) + tile,
         (proj_b, (tm, tn), lambda j, i: (i, ga0 + j)), (proj_b, (tm, tn), lambda j, i: (i, gb0 + j)),
         (b_gate, (1, tn), lambda j, i: (0, j)), (b_gate, (1, tn), lambda j, i: (0, nn + j))],
        [((S, D), BF16) + tile, ((S, D), BF16) + tile,
         ((2, S, D), BF16, (2, tm, tn), lambda j, i: (0, i, j)),
         ((2, 1, D), F32, (2, 1, tn), lambda j, i: (0, 0, j))],
        sem=("parallel", "arbitrary"))
    return dya, dyb, dgate, dbg


def matmul_nt(a, b_spec_fn, N, name, tn=512):
    S, K = a.shape
    tm = 512

    def body(a_ref, b_ref, o_ref):
        b = b_ref[...]
        o_ref[...] = _dot(a_ref[...], b.reshape(-1, b.shape[-1]), 1, 1)

    return _call(name, body, (N // tn, S // tm),
                 [(a, (tm, K), lambda j, i: (i, 0)), b_spec_fn(lambda j, i: j)],
                 [((S, N), F32, (tm, tn), lambda j, i: (i, j))], sem=("parallel", "parallel"))[0]


def weight_grad_rows(a, b, grads, row0, name):
    S, M = a.shape
    N = b.shape[1]
    tmm = 512
    tk = WGRAD_TK
    nk = S // tk

    def body(g_ref, a_ref, b_ref, o_ref, acc_ref):
        k = pl.program_id(1)
        p = _dot(a_ref[...], b_ref[...].astype(BF16), 0, 0)

        @pl.when(k == 0)
        def _():
            acc_ref[...] = p

        @pl.when(k > 0)
        def _():
            acc_ref[...] += p

        @pl.when(k == nk - 1)
        def _():
            o_ref[...] = acc_ref[...].astype(BF16).reshape(tmm // LANES, LANES, N)

    return pl.pallas_call(
        body,
        out_shape=jax.ShapeDtypeStruct(grads.shape, BF16),
        grid=(M // tmm, nk),
        in_specs=[pl.BlockSpec(memory_space=pl.ANY),
                  pl.BlockSpec((tk, tmm), lambda j, k: (k, j)),
                  pl.BlockSpec((tk, N), lambda j, k: (k, 0))],
        out_specs=pl.BlockSpec((tmm // LANES, LANES, N), lambda j, k: (j, row0 // LANES, 0)),
        scratch_shapes=[pltpu.VMEM((tmm, N), F32)],
        input_output_aliases={0: 0},
        name=name,
        compiler_params=pltpu.CompilerParams(dimension_semantics=("parallel", "arbitrary"),
                                             vmem_limit_bytes=VMEM_LIMIT),
    )(grads, a, b)


def weight_grad_plain(a, b, name):
    S, M = a.shape
    N = b.shape[1]
    tk = WGRAD_TK
    nk = S // tk

    def body(a_ref, b_ref, o_ref, acc_ref):
        k = pl.program_id(0)
        p = _dot(a_ref[...], b_ref[...], 0, 0)

        @pl.when(k == 0)
        def _():
            acc_ref[...] = p

        @pl.when(k > 0)
        def _():
            acc_ref[...] += p

        @pl.when(k == nk - 1)
        def _():
            o_ref[...] = acc_ref[...].astype(BF16)

    return _call(name, body, (nk,),
                 [(a, (tk, M), lambda k: (k, 0)), (b, (tk, N), lambda k: (k, 0))],
                 [((M, N), BF16, (M, N), lambda k: (0, 0))],
                 scratch=[pltpu.VMEM((M, N), F32)], sem=("arbitrary",))[0]


def local_step(x, tgt, p, G1, get_gm, get_g2, emit, start_token):
    S, D = x.shape
    after = lambda t: t[0:1, 0:1]
    buckets = _bucket_tables()
    cos_t, sin_t = _rope_tables(S)
    gains = jnp.concatenate([jnp.tile(p["q_norm"], (1, N_HEADS_B)), jnp.tile(p["k_norm"], (1, N_KV_B)),
                             jnp.ones((1, N_KV_B * LANES), F32)], axis=1)

    n1 = rms_fwd(x, p["ffn1_norm"] + after(start_token), "ffn1_norm")
    ab1 = ffn_up(n1, G1, "ffn1_up")
    x1 = ffn_down(ab1, G1, x, "ffn1_down")

    Gm = get_gm(x1)
    w_a = Gm[:, MIX_WA:MIX_ROWS, :].reshape(N_DEV, GROUP_WIDTH_A, LANES).transpose(1, 0, 2).reshape(GROUP_WIDTH_A, D)
    hm = rms_fwd(x1, p["mix_norm"], "mix_norm")
    n_a = A_QKV_WIDTH // PROJ_TN
    proj_a = in_proj(hm, Gm, 0, n_a, BF16, "in_proj_a")
    proj_b = in_proj(hm, Gm, n_a, PB_WIDTH // PROJ_TN, F32, "in_proj_b")

    bias = bias_build(p["rel_bias"], buckets)
    outs, lses = [], []
    for g in range(3):
        o, l = a_fwd(proj_a, bias[g], g, "a_fwd_%d" % g)
        outs.append(o)
        lses.append(l)
    o_a, lse_tot = a_combine(outs, lses, "a_combine")

    qkv = qkv_prep(proj_b, gains, cos_t, sin_t, "qkv_prep")
    k_t = qkv[:, N_HEADS_B * LANES:(N_HEADS_B + N_KV_B) * LANES].T
    o_b, lse_b = flash_fwd(qkv, "flash_fwd")

    merged, ya, yb = merge_fwd(o_a, o_b, w_a, Gm, proj_b, p["b_gate"], "merge_fwd")
    x2 = out_proj(merged, Gm, x1, "out_proj")

    G2 = get_g2(x2)
    n2 = rms_fwd(x2, p["ffn2_norm"], "ffn2_norm")
    ab2 = ffn_up(n2, G2, "ffn2_up")
    x3 = ffn_down(ab2, G2, x2, "ffn2_down")

    loss, dx3, d_final = final_loss(x3, tgt, p["final_norm"], "final_loss")

    dabh2, gw2 = ffn_bwd_weights(dx3, ab2, n2, G2, "ffn2_bwd")
    t2 = emit("ffn2", gw2)
    dx2, d_ffn2_norm = ffn_bwd_input(dabh2, G2, x2, p["ffn2_norm"] + after(t2), dx3, "ffn2_bwd")

    dya, dyb, dgate, dbg = merge_bwd(dx2, Gm, ya, yb, proj_b, p["b_gate"], "merge_bwd")
    gm_grads = jnp.zeros(Gm.shape, BF16)
    gm_grads = weight_grad_rows(merged, dx2, gm_grads, MIX_WOUT, "dw_out")
    gm_grads = weight_grad_rows(o_b, dyb, gm_grads, MIX_WB, "dw_branch_b")
    dw_a = weight_grad_plain(o_a, dya, "dw_branch_a")
    do_a = matmul_nt(dya, lambda jm: (w_a, (MERGE_TN, D), lambda j, i: (jm(j, i), 0)), GROUP_WIDTH_A, "do_a")
    do_b = matmul_nt(dyb, lambda jm: _mix_rows_spec(Gm, MIX_WB, MERGE_TN // LANES, jm), N_HEADS_B * LANES, "do_b")

    dq_r, dk_r, dv_b = flash_bwd(qkv, k_t, do_b, o_b, lse_b, "flash_bwd")
    dq_b, d_q_norm = qk_prep_bwd(dq_r, proj_b, 0, p["q_norm"], cos_t, sin_t, "q_prep_bwd")
    dk_b, d_k_norm = qk_prep_bwd(dk_r, proj_b, N_HEADS_B, p["k_norm"], cos_t, sin_t, "k_prep_bwd")

    dqs, dks, dvs, dbs = [], [], [], []
    for g in range(3):
        dq, dk, dv, db = a_bwd(proj_a, bias[g], do_a, o_a, lse_tot, g, "a_bwd_%d" % g)
        dqs.append(dq)
        dks.append(dk)
        dvs.append(dv)
        dbs.append(db)
    d_rel_bias = bias_bwd(jnp.stack(dbs, axis=0), buckets)

    dproj = jnp.concatenate(dqs + dks + dvs + [dq_b, dk_b, dv_b, dgate[0], dgate[1]], axis=1)
    gm_grads = in_proj_bwd_dw(dproj, hm, gm_grads, "in_proj_bwd")
    dw_a_sh = dw_a.reshape(GROUP_WIDTH_A, N_DEV, LANES).transpose(1, 0, 2).reshape(N_DEV, MIX_ROWS - MIX_WA, D)
    gm_grads = lax.dynamic_update_slice(gm_grads, dw_a_sh, (0, MIX_WA, 0))
    tm = emit("mix", gm_grads)
    dx1, d_mix_norm = in_proj_bwd_dh(dproj, Gm, x1, p["mix_norm"] + after(tm), dx2, "in_proj_bwd")

    dabh1, gw1 = ffn_bwd_weights(dx1, ab1, n1, G1, "ffn1_bwd")
    t1 = emit("ffn1", gw1)
    dx0, d_ffn1_norm = ffn_bwd_input(dabh1, G1, x, p["ffn1_norm"] + after(t1), dx1, "ffn1_bwd")

    small = dict(ffn1_norm=d_ffn1_norm, mix_norm=d_mix_norm, b_gate=dbg.reshape(1, 2 * D),
                 q_norm=d_q_norm, k_norm=d_k_norm, rel_bias=d_rel_bias, ffn2_norm=d_ffn2_norm,
                 final_norm=d_final)
    return loss, dx0, small


def _pack_small(t, loss_row):
    row6 = jnp.concatenate([t["q_norm"].reshape(1, -1), t["k_norm"].reshape(1, -1), t["rel_bias"].reshape(1, -1)], axis=1)
    return jnp.concatenate([t["ffn1_norm"].reshape(1, -1), t["mix_norm"].reshape(1, -1), t["b_gate"].reshape(2, -1),
                            t["ffn2_norm"].reshape(1, -1), t["final_norm"].reshape(1, -1), row6, loss_row], axis=0)


def _unpack_small(a, shapes):
    return dict(ffn1_norm=a[0:1].reshape(shapes["ffn1_norm"]), mix_norm=a[1:2].reshape(shapes["mix_norm"]),
                b_gate=a[2:4].reshape(shapes["b_gate"]), ffn2_norm=a[4:5].reshape(shapes["ffn2_norm"]),
                final_norm=a[5].reshape(shapes["final_norm"]), q_norm=a[6:7, 0:128].reshape(shapes["q_norm"]),
                k_norm=a[6:7, 128:256].reshape(shapes["k_norm"]), rel_bias=a[6, 256:1024].reshape(shapes["rel_bias"]))


SMALL = ("ffn1_norm", "mix_norm", "b_gate", "q_norm", "k_norm", "rel_bias", "ffn2_norm", "final_norm")
ORDER = ("ffn1_norm", "ffn1_w1", "ffn1_w3", "ffn1_w2", "mix_norm", "w_in", "b_gate", "q_norm", "k_norm", "rel_bias",
         "w_branch_a", "w_branch_b", "w_out", "ffn2_norm", "ffn2_w1", "ffn2_w3", "ffn2_w2", "final_norm")


def kernel(x, ffn1_norm, ffn1_w1, ffn1_w3, ffn1_w2, mix_norm, w_in, b_gate, q_norm, k_norm, rel_bias, w_branch_a, w_branch_b, w_out, ffn2_norm, ffn2_w1, ffn2_w3, ffn2_w2, final_norm, loss_target, m_ffn1_norm, m_ffn1_w1, m_ffn1_w3, m_ffn1_w2, m_mix_norm, m_w_in, m_b_gate, m_q_norm, m_k_norm, m_rel_bias, m_w_branch_a, m_w_branch_b, m_w_out, m_ffn2_norm, m_ffn2_w1, m_ffn2_w3, m_ffn2_w2, m_final_norm, v_ffn1_norm, v_ffn1_w1, v_ffn1_w3, v_ffn1_w2, v_mix_norm, v_w_in, v_b_gate, v_q_norm, v_k_norm, v_rel_bias, v_w_branch_a, v_w_branch_b, v_w_out, v_ffn2_norm, v_ffn2_w1, v_ffn2_w3, v_ffn2_w2, v_final_norm):
    args = dict(locals())
    w = {n: args[n] for n in ORDER}
    m = {n: args["m_" + n] for n in ORDER}
    v = {n: args["v_" + n] for n in ORDER}
    D = x.shape[2]

    def ffn_group(w1, w3, w2):
        return jnp.concatenate([w1[0].T, w3[0].T, w2[0]], axis=0).astype(BF16)

    g1 = ffn_group(ffn1_w1, ffn1_w3, ffn1_w2)
    g2 = ffn_group(ffn2_w1, ffn2_w3, ffn2_w2)
    gm = jnp.concatenate([w_in[0], w_branch_b[0], w_out[0], w_branch_a[0].reshape(MIX_ROWS - MIX_WA, D)],
                         axis=0).astype(BF16)
    (G1,), (gm, g2) = all_gather_groups([g1], later=(gm, g2))
    ag_m = all_gather_start(gm, "all_gather_mix_start")
    ag_2 = all_gather_start(g2, "all_gather_ffn2_start")

    def get_gm(after):
        return all_gather_finish(*_split_wait("all_gather_mix_wait", ag_m, 4, after), "all_gather_mix_finish")

    def get_g2(after):
        return all_gather_finish(*_split_wait("all_gather_ffn2_wait", ag_2, 4, after), "all_gather_ffn2_finish")

    core = lax.axis_index("c").astype(jnp.int32).reshape(1)
    chip = (2 * lax.axis_index("x") + lax.axis_index("y")).astype(jnp.int32).reshape(1)
    exchanges = {}

    def emit(tag, gw):
        (theirs,) = reduce_scatter_pair([gw], "reduce_scatter_pair_" + tag)
        part = pair_add(gw, theirs, core, "pair_add_" + tag)
        exchanges[tag] = reduce_scatter_start(part, "reduce_scatter_" + tag + "_start")
        return exchanges[tag][4]

    small_p = dict(ffn1_norm=ffn1_norm, mix_norm=mix_norm, b_gate=b_gate, q_norm=q_norm, k_norm=k_norm,
                   rel_bias=rel_bias, ffn2_norm=ffn2_norm, final_norm=final_norm.reshape(1, D))
    loss_p, grad_x, small_g = local_step(x[0], loss_target[0], small_p, G1, get_gm, get_g2, emit, ag_m[4] + ag_2[4])

    def landed(tag, after):
        return _split_wait("reduce_scatter_" + tag + "_wait", exchanges[tag], 3, after)

    grads = {}
    last_token = exchanges["ffn1"][4]
    for tag, after in (("ffn2", last_token), ("ffn1", grad_x)):
        part, land = landed(tag, after)
        ssum = lambda off, nm: sum_chips(part, land, chip, off, FFN_SHARD, FFN_SHARD, tag + nm)
        grads[tag + "_w1"] = ssum(0, "_w1_sum").T[None]
        grads[tag + "_w3"] = ssum(FFN_SHARD, "_w3_sum").T[None]
        grads[tag + "_w2"] = ssum(2 * FFN_SHARD, "_w2_sum")[None]
        if tag == "ffn2":
            part_m, land_m = landed("mix", last_token)
            msum = lambda off, rows, blk, nm: sum_chips(part_m, land_m, chip, off, rows, blk, nm)
            grads["w_in"] = msum(MIX_WIN, MIX_WB - MIX_WIN, LANES, "w_in_sum")[None]
            grads["w_branch_b"] = msum(MIX_WB, LANES, LANES, "w_branch_b_sum")[None]
            grads["w_out"] = msum(MIX_WOUT, LANES, LANES, "w_out_sum")[None]
            grads["w_branch_a"] = msum(MIX_WA, MIX_ROWS - MIX_WA, MIX_ROWS - MIX_WA,
                                       "w_branch_a_sum").reshape(w_branch_a.shape)
    loss_row = jnp.pad(loss_p, ((0, 0), (0, D - LANES)))
    smalls = small_all_gather(_pack_small(small_g, loss_row))
    small_sum = sum_slots(smalls, 0, N_DEV, N_DEV, "small_sum")
    small_shapes = {n: w[n].shape for n in SMALL}
    grads.update(_unpack_small(small_sum, small_shapes))
    loss = small_sum[7, 0]

    delta, new_m, new_v = {}, {}, {}
    for n in ORDER:
        if n in SMALL:
            continue
        shp = w[n].shape
        two_d = lambda a: a.reshape(shp[-2], shp[-1])
        d_, m_, v_ = adamw(two_d(w[n]), two_d(grads[n]), two_d(m[n]), two_d(v[n]), "adamw_" + n)
        delta[n], new_m[n], new_v[n] = d_.reshape(shp), m_.reshape(shp), v_.reshape(shp)
    zero_row = jnp.zeros((1, D), F32)
    pack = lambda t: _pack_small({n: t[n] for n in SMALL}, zero_row)
    d_, m_, v_ = adamw(pack(w), small_sum, pack(m), pack(v), "adamw_small")
    for src, dst in ((d_, delta), (m_, new_m), (v_, new_v)):
        dst.update(_unpack_small(src, small_shapes))

    return (loss, grad_x[None], *[grads[n] for n in ORDER], *[delta[n] for n in ORDER],
            *[new_m[n] for n in ORDER], *[new_v[n] for n in ORDER])
```

```python
import math

import jax
import jax.numpy as jnp
from jax import lax
from jax.experimental import pallas as pl
from jax.experimental.pallas import tpu as pltpu

F32 = jnp.float32
BF16 = jnp.bfloat16
MESH = pl.DeviceIdType.MESH

V7X_VMEM_BYTES = 64 * 1024 * 1024
VMEM_LIMIT = V7X_VMEM_BYTES - 8 * 1024 * 1024
LANES = 128

N_DEV = 8
EPS = 1e-6
NEG_INF = -1e30

DILATIONS = (1, 4, 16)
HALF_WINDOW = 64
HEAD_DIM_A = 64
HEADS_PER_GROUP_A = 8
GROUP_WIDTH_A = 512
A_QKV_WIDTH = 4608
A_GROUP_QKV = A_QKV_WIDTH // 3
A_TQ = 128
A_WIN = A_TQ + 2 * HALF_WINDOW
A_UNROLL = 4
WGRAD_TK = 2048
HEAD_DIM_B = 128
N_HEADS_B = 8
N_KV_B = 2
GQA_GROUP_B = 4
GRID_W = 64
ROPE_THETA = 10000.0
B_TQ_FWD = 256
B_TQ_BWD = 512
N_BUCKETS = 32
MAX_DISTANCE = 1024
PB_WIDTH = 3584
PB_GATE_A = 1536
PB_GATE_B = 2560

ADAM_LR = 0.001
ADAM_B1 = 0.9
ADAM_B2 = 0.999
ADAM_EPS = 1e-08
ADAM_WD = 0.01
ADAM_STEP = 10

FFN_SHARD = 352
MIX_WIN, MIX_WB, MIX_WOUT, MIX_WA = 0, 1024, 1152, 1280
MIX_ROWS = 1344


def _dot(a, b, ca=1, cb=0):
    return lax.dot_general(a, b, (((ca,), (cb,)), ((), ())), preferred_element_type=F32)


def _call(name, body, grid, ins, outs, scratch=(), sem=None, aliases=None):
    res = pl.pallas_call(
        body,
        out_shape=[jax.ShapeDtypeStruct(s, d) for (s, d, _, _) in outs],
        grid=grid,
        in_specs=[pl.BlockSpec(bs, im) for (_, bs, im) in ins],
        out_specs=[pl.BlockSpec(bs, im) for (_, _, bs, im) in outs],
        scratch_shapes=list(scratch),
        name=name,
        input_output_aliases=aliases or {},
        compiler_params=pltpu.CompilerParams(dimension_semantics=sem, vmem_limit_bytes=VMEM_LIMIT),
    )(*[a for (a, _, _) in ins])
    return res


def _sigmoid(x):
    return 1.0 / (1.0 + jnp.exp(-x))


def _position():
    return lax.axis_index("x"), lax.axis_index("y"), lax.axis_index("c")


def _hbm_specs(n):
    return [pl.BlockSpec(memory_space=pl.ANY) for _ in range(n)]


def all_gather_groups(groups, later=()):
    n = len(groups)
    n_later = len(later)

    def body(*refs):
        ins, outs = refs[:n], refs[n + n_later:2 * n + n_later]
        refs = refs[2 * n_later:]
        stage = refs[2 * n:3 * n]
        send_sems, recv_sems, local_sems = refs[3 * n:]
        x, y, c = _position()
        sibling = (x, y, 1 - c)
        chips = [(1 - x, y), (x, 1 - y), (1 - x, 1 - y)]

        def copy(i, k, block, to, src=None):
            px, py, pc = block
            dst = outs[i].at[4 * px + 2 * py + pc]
            return pltpu.make_async_remote_copy(
                src_ref=dst if src is None else src, dst_ref=dst,
                send_sem=send_sems.at[i, k], recv_sem=recv_sems.at[i, k],
                device_id=to, device_id_type=MESH)

        loads = [pltpu.make_async_copy(ins[i], stage[i], local_sems.at[i, 0]) for i in range(n)]
        for ld in loads:
            ld.start()
        sends, stores = [], []
        for i in range(n):
            loads[i].wait()
            first = [copy(i, 0, (x, y, c), sibling, src=stage[i])]
            first += [copy(i, 1 + j, (x, y, c), (*chip, c), src=stage[i]) for j, chip in enumerate(chips)]
            for cp in first:
                cp.start()
            sends += first
            st = pltpu.make_async_copy(stage[i], outs[i].at[4 * x + 2 * y + c], local_sems.at[i, 1])
            st.start()
            stores.append(st)
        for i in range(n):
            for j, chip in enumerate(chips):
                copy(i, 1 + j, (*chip, c), (x, y, c)).wait_recv()
                passed = copy(i, 4 + j, (*chip, c), sibling)
                passed.start()
                sends.append(passed)
        for i in range(n):
            copy(i, 0, sibling, (x, y, c)).wait_recv()
            for j, chip in enumerate(chips):
                copy(i, 4 + j, (*chip, 1 - c), (x, y, c)).wait_recv()
        for cp in sends:
            cp.wait_send()
        for st in stores:
            st.wait()

    res = pl.pallas_call(
        body,
        out_shape=[jax.ShapeDtypeStruct((N_DEV,) + g.shape, g.dtype) for g in groups]
        + [jax.ShapeDtypeStruct(a.shape, a.dtype) for a in later],
        in_specs=_hbm_specs(n + n_later),
        out_specs=_hbm_specs(n + n_later),
        scratch_shapes=[pltpu.VMEM(g.shape, g.dtype) for g in groups]
        + [pltpu.SemaphoreType.DMA((n, 7)), pltpu.SemaphoreType.DMA((n, 7)), pltpu.SemaphoreType.DMA((n, 2))],
        input_output_aliases={n + i: n + i for i in range(n_later)},
        name="all_gather_weights",
        compiler_params=pltpu.CompilerParams(vmem_limit_bytes=VMEM_LIMIT),
    )(*groups, *later)
    return res[:n], res[n:]


PAIR_BUFFERS = 4


def reduce_scatter_pair(grads, name):
    n = len(grads)
    C = grads[0].shape[2]
    half = [g.shape[1] // 2 for g in grads]
    chunks = [(i, q, hf) for i in range(n) for q in range(4) for hf in range(2)]
    nb = PAIR_BUFFERS

    def body(*refs):
        ins, theirs = refs[:n], refs[n:2 * n]
        buf, load_sems, send_sems, recv_sems = refs[2 * n:]
        x, y, c = _position()
        sibling = (x, y, 1 - c)

        def load(k):
            i, q, hf = chunks[k]
            r = half[i]
            return pltpu.make_async_copy(ins[i].at[2 * q + (1 - c), pl.ds(hf * r, r), :],
                                         buf.at[k % nb, pl.ds(0, r), :], load_sems.at[k % nb])

        def send(k):
            i, q, hf = chunks[k]
            r = half[i]
            return pltpu.make_async_remote_copy(
                src_ref=buf.at[k % nb, pl.ds(0, r), :], dst_ref=theirs[i].at[q, pl.ds(hf * r, r), :],
                send_sem=send_sems.at[k % nb], recv_sem=recv_sems.at[i],
                device_id=sibling, device_id_type=MESH)

        for k in range(len(chunks) + 1):
            if k < len(chunks):
                if k >= nb:
                    send(k - nb).wait_send()
                load(k).start()
            if k >= 1:
                load(k - 1).wait()
                send(k - 1).start()
        for k in range(max(0, len(chunks) - nb), len(chunks)):
            send(k).wait_send()
        for i in range(n):
            pltpu.make_async_remote_copy(
                src_ref=theirs[i], dst_ref=theirs[i], send_sem=send_sems.at[0], recv_sem=recv_sems.at[i],
                device_id=sibling, device_id_type=MESH).wait_recv()

    return pl.pallas_call(
        body,
        out_shape=[jax.ShapeDtypeStruct((4,) + g.shape[1:], g.dtype) for g in grads],
        in_specs=_hbm_specs(n),
        out_specs=_hbm_specs(n),
        scratch_shapes=[pltpu.VMEM((nb, max(half), C), grads[0].dtype), pltpu.SemaphoreType.DMA((nb,)),
                        pltpu.SemaphoreType.DMA((nb,)), pltpu.SemaphoreType.DMA((n,))],
        name=name,
        compiler_params=pltpu.CompilerParams(vmem_limit_bytes=VMEM_LIMIT),
    )(*grads)


_HBM_SPEC = pl.BlockSpec(memory_space=pltpu.HBM)
_SEM_SPEC = pl.BlockSpec(memory_space=pltpu.SEMAPHORE)
_TOKEN_SPEC = pl.BlockSpec(memory_space=pltpu.VMEM)
_DATAFLOW = pltpu.SideEffectType.DATAFLOW_SIDE_EFFECTING


def _split_start(name, body, src, land_shape):
    def full_body(src_ref, land_ref, send_sem, recv_sem, src_thru, land_thru, token):
        body(src_ref, land_ref, send_sem, recv_sem)
        token[...] = jnp.zeros_like(token)

    land = pltpu.with_memory_space_constraint(lax.empty(land_shape, src.dtype), pltpu.HBM)
    return pl.pallas_call(
        full_body, name=name,
        out_shape=(pltpu.SemaphoreType.DMA(()), pltpu.SemaphoreType.DMA(()),
                   pltpu.HBM(src.shape, src.dtype), pltpu.HBM(land_shape, src.dtype),
                   jax.ShapeDtypeStruct((8, LANES), F32)),
        in_specs=(_HBM_SPEC, _HBM_SPEC),
        out_specs=(_SEM_SPEC, _SEM_SPEC, _HBM_SPEC, _HBM_SPEC, _TOKEN_SPEC),
        input_output_aliases={0: 2, 1: 3},
        compiler_params=pltpu.CompilerParams(has_side_effects=_DATAFLOW),
    )(pltpu.with_memory_space_constraint(src, pltpu.HBM), land)


def _split_wait(name, started, n_blocks, after):
    send_sem, recv_sem, src_thru, land_thru, _ = started

    def body(src_ref, land_ref, send_sem, recv_sem, after_ref, src_dead, got_ref):
        x, y, c = _position()
        blocks = land_ref.at[pl.ds(0, n_blocks)]
        copy = pltpu.make_async_remote_copy(src_ref=blocks, dst_ref=blocks, send_sem=send_sem, recv_sem=recv_sem,
                                            device_id=(x, y, c), device_id_type=MESH)
        copy.wait_send()
        copy.wait_recv()

    return pl.pallas_call(
        body, name=name,
        out_shape=(pltpu.HBM(src_thru.shape, src_thru.dtype), pltpu.HBM(land_thru.shape, land_thru.dtype)),
        in_specs=(_HBM_SPEC, _HBM_SPEC, _SEM_SPEC, _SEM_SPEC, pl.BlockSpec(memory_space=pl.ANY)),
        out_specs=(_HBM_SPEC, _HBM_SPEC),
        input_output_aliases={0: 0, 1: 1},
        compiler_params=pltpu.CompilerParams(has_side_effects=_DATAFLOW),
    )(src_thru, land_thru, send_sem, recv_sem, after)


def all_gather_start(block, name):
    def body(b_ref, land_ref, send_sem, recv_sem):
        x, y, c = _position()
        for peer in [(x, y, 1 - c), (1 - x, y, c), (x, 1 - y, c), (1 - x, 1 - y, c)]:
            pltpu.make_async_remote_copy(src_ref=b_ref, dst_ref=land_ref.at[4 * x + 2 * y + c],
                                         send_sem=send_sem, recv_sem=recv_sem,
                                         device_id=peer, device_id_type=MESH).start()

    return _split_start(name, body, block, (N_DEV,) + block.shape)


def all_gather_finish(block, land, name):
    R, C = block.shape

    def body(b_ref, land_in, land_ref, stage, load_sems, send_sems, recv_sems, own_sem):
        x, y, c = _position()
        sibling = (x, y, 1 - c)
        chips = [(1 - x, y), (x, 1 - y), (1 - x, 1 - y)]
        own_in = pltpu.make_async_copy(b_ref, stage.at[3], load_sems.at[3])
        own_in.start()
        loads = [pltpu.make_async_copy(land_in.at[4 * px + 2 * py + c], stage.at[j], load_sems.at[j])
                 for j, (px, py) in enumerate(chips)]
        for ld in loads:
            ld.start()
        sends = []
        for j, (px, py) in enumerate(chips):
            loads[j].wait()
            dst = land_ref.at[4 * px + 2 * py + c]
            cp = pltpu.make_async_remote_copy(src_ref=stage.at[j], dst_ref=dst, send_sem=send_sems.at[j],
                                              recv_sem=recv_sems.at[j], device_id=sibling, device_id_type=MESH)
            cp.start()
            sends.append(cp)
        own_in.wait()
        own_out = pltpu.make_async_copy(stage.at[3], land_ref.at[4 * x + 2 * y + c], own_sem)
        own_out.start()
        for j, (px, py) in enumerate(chips):
            dst = land_ref.at[4 * px + 2 * py + (1 - c)]
            pltpu.make_async_remote_copy(src_ref=stage.at[j], dst_ref=dst, send_sem=send_sems.at[j],
                                         recv_sem=recv_sems.at[j], device_id=sibling,
                                         device_id_type=MESH).wait_recv()
        for cp in sends:
            cp.wait_send()
        own_out.wait()

    return pl.pallas_call(
        body,
        out_shape=jax.ShapeDtypeStruct(land.shape, land.dtype),
        in_specs=_hbm_specs(2),
        out_specs=pl.BlockSpec(memory_space=pl.ANY),
        scratch_shapes=[pltpu.VMEM((4, R, C), block.dtype), pltpu.SemaphoreType.DMA((4,)),
                        pltpu.SemaphoreType.DMA((3,)), pltpu.SemaphoreType.DMA((3,)), pltpu.SemaphoreType.DMA],
        input_output_aliases={1: 0},
        name=name,
        compiler_params=pltpu.CompilerParams(vmem_limit_bytes=VMEM_LIMIT),
    )(block, land)


def reduce_scatter_start(parts, name):
    def body(p_ref, land_ref, send_sem, recv_sem):
        x, y, c = _position()
        for px, py in [(1 - x, y), (x, 1 - y), (1 - x, 1 - y)]:
            pltpu.make_async_remote_copy(src_ref=p_ref.at[2 * px + py], dst_ref=land_ref.at[2 * x + y],
                                         send_sem=send_sem, recv_sem=recv_sem,
                                         device_id=(px, py, c), device_id_type=MESH).start()

    return _split_start(name, body, parts, parts.shape)


def small_all_gather(small):
    def body(small_ref, smalls, s_send, s_recv, s_local):
        x, y, c = _position()
        me = 4 * x + 2 * y + c
        lc = pltpu.make_async_copy(small_ref, smalls.at[me], s_local)
        lc.start()
        remote = []
        k = 0
        for dx in (0, 1):
            for dy in (0, 1):
                for dc in (0, 1):
                    if dx + dy + dc == 0:
                        continue
                    peer = (1 - x if dx else x, 1 - y if dy else y, 1 - c if dc else c)
                    rc = pltpu.make_async_remote_copy(
                        src_ref=small_ref, dst_ref=smalls.at[me],
                        send_sem=s_send.at[k], recv_sem=s_recv.at[k],
                        device_id=peer, device_id_type=MESH)
                    rc.start()
                    remote.append(rc)
                    k += 1
        for rc in remote:
            rc.wait()
        lc.wait()

    return pl.pallas_call(
        body,
        out_shape=jax.ShapeDtypeStruct((N_DEV,) + small.shape, small.dtype),
        in_specs=_hbm_specs(1),
        out_specs=pl.BlockSpec(memory_space=pl.ANY),
        scratch_shapes=[pltpu.SemaphoreType.DMA((7,)), pltpu.SemaphoreType.DMA((7,)), pltpu.SemaphoreType.DMA],
        name="small_all_gather",
    )(small)


def pair_add(grads, theirs, core, name):
    _, R, C = theirs.shape
    tr = R // 2

    def body(c_ref, a_ref, b_ref, o_ref):
        o_ref[...] = (a_ref[...].astype(F32) + b_ref[...].astype(F32)).astype(BF16)

    return pl.pallas_call(
        body,
        out_shape=jax.ShapeDtypeStruct(theirs.shape, BF16),
        grid_spec=pltpu.PrefetchScalarGridSpec(
            num_scalar_prefetch=1, grid=(4, R // tr),
            in_specs=[pl.BlockSpec((None, tr, C), lambda q, i, c: (2 * q + c[0], i, 0)),
                      pl.BlockSpec((None, tr, C), lambda q, i, c: (q, i, 0))],
            out_specs=pl.BlockSpec((None, tr, C), lambda q, i, c: (q, i, 0))),
        name=name,
        compiler_params=pltpu.CompilerParams(dimension_semantics=("parallel", "parallel"),
                                             vmem_limit_bytes=VMEM_LIMIT),
    )(core, grads, theirs)


def sum_slots(recv, off, rows, blk, name):
    nq, _, C = recv.shape
    ob = off // blk

    def body(r_ref, o_ref):
        acc = r_ref[0].astype(F32)
        for q in range(1, nq):
            acc = acc + r_ref[q].astype(F32)
        o_ref[...] = acc

    return _call(name, body, (rows // blk,),
                 [(recv, (nq, blk, C), lambda i: (0, ob + i, 0))],
                 [((rows, C), F32, (blk, C), lambda i: (i, 0))], sem=("parallel",))[0]


def sum_chips(parts, land, chip, off, rows, blk, name):
    C = parts.shape[2]
    ob = off // blk

    def body(c_ref, own_ref, a_ref, b_ref, d_ref, o_ref):
        o_ref[...] = ((own_ref[...].astype(F32) + a_ref[...].astype(F32)) + b_ref[...].astype(F32)) \
            + d_ref[...].astype(F32)

    def entry(flip):
        return pl.BlockSpec((None, blk, C), lambda i, c: (c[0] ^ flip, ob + i, 0))

    return pl.pallas_call(
        body,
        out_shape=jax.ShapeDtypeStruct((rows, C), F32),
        grid_spec=pltpu.PrefetchScalarGridSpec(
            num_scalar_prefetch=1, grid=(rows // blk,),
            in_specs=[entry(0), entry(1), entry(2), entry(3)],
            out_specs=pl.BlockSpec((blk, C), lambda i, c: (i, 0))),
        name=name,
        compiler_params=pltpu.CompilerParams(dimension_semantics=("parallel",), vmem_limit_bytes=VMEM_LIMIT),
    )(chip, parts, land, land, land)


def adamw(w, g, m, v, name):
    R, C = w.shape
    tr = R
    for cand in (256, 128, 64, 32, 16, 8):
        if R % cand == 0 and R > cand:
            tr = cand
            break
    c1 = 1.0 / (1.0 - ADAM_B1 ** ADAM_STEP)
    c2 = 1.0 / (1.0 - ADAM_B2 ** ADAM_STEP)

    def body(w_ref, g_ref, m_ref, v_ref, d_ref, nm_ref, nv_ref):
        gv = g_ref[...]
        nm = ADAM_B1 * m_ref[...] + (1.0 - ADAM_B1) * gv
        nv = ADAM_B2 * v_ref[...] + (1.0 - ADAM_B2) * (gv * gv)
        d_ref[...] = -ADAM_LR * ((nm * c1) / (jnp.sqrt(nv * c2) + ADAM_EPS) + ADAM_WD * w_ref[...])
        nm_ref[...] = nm
        nv_ref[...] = nv

    spec = ((tr, C), lambda i: (i, 0))
    out = ((R, C), F32) + spec
    return _call(name, body, (R // tr,), [(w,) + spec, (g,) + spec, (m,) + spec, (v,) + spec],
                 [out, out, out], sem=("parallel",))


def rms_fwd(x, g, name):
    S, D = x.shape
    tr = 512

    def body(x_ref, g_ref, o_ref):
        xv = x_ref[...]
        r = lax.rsqrt(jnp.mean(xv * xv, axis=-1, keepdims=True) + EPS)
        o_ref[...] = (xv * r * g_ref[...]).astype(BF16)

    return _call(name, body, (S // tr,),
                 [(x, (tr, D), lambda i: (i, 0)), (g, (1, D), lambda i: (0, 0))],
                 [((S, D), BF16, (tr, D), lambda i: (i, 0))], sem=("parallel",))[0]


def _rms_bwd_tile(dn, xv, gv):
    r = lax.rsqrt(jnp.mean(xv * xv, axis=-1, keepdims=True) + EPS)
    xh = xv * r
    dxh = dn * gv
    dx = r * (dxh - xh * jnp.mean(dxh * xh, axis=-1, keepdims=True))
    return dx, dn * xh


def final_loss(x, tgt, g, name):
    S, D = x.shape
    tr = 256

    def body(x_ref, t_ref, g_ref, l_ref, dx_ref, dg_ref):
        i = pl.program_id(0)
        xv, gv = x_ref[...], g_ref[...]
        r = lax.rsqrt(jnp.mean(xv * xv, axis=-1, keepdims=True) + EPS)
        xh = xv * r
        e = xh * gv - t_ref[...]
        part = 0.5 * jnp.sum(jnp.sum(e * e, axis=-1, keepdims=True) * (1.0 / D), axis=0, keepdims=True)
        dy = e * (1.0 / D)
        dxh = dy * gv
        dx_ref[...] = r * (dxh - xh * jnp.mean(dxh * xh, axis=-1, keepdims=True))
        dgp = jnp.sum(dy * xh, axis=0, keepdims=True)

        @pl.when(i == 0)
        def _():
            l_ref[...] = jnp.broadcast_to(part, l_ref.shape)
            dg_ref[...] = dgp

        @pl.when(i > 0)
        def _():
            l_ref[...] += jnp.broadcast_to(part, l_ref.shape)
            dg_ref[...] += dgp

    row = ((tr, D), lambda i: (i, 0))
    return _call(name, body, (S // tr,),
                 [(x,) + row, (tgt,) + row, (g, (1, D), lambda i: (0, 0))],
                 [((1, LANES), F32, (1, LANES), lambda i: (0, 0)), ((S, D), F32) + row,
                  ((1, D), F32, (1, D), lambda i: (0, 0))], sem=("arbitrary",))


FFN_TF = 4 * FFN_SHARD


def _ffn_w_spec(G, which, imap):
    D = G.shape[2]
    return (G, (4, FFN_SHARD, D), lambda *idx: (imap(*idx), which, 0))


def _ffn_whole_w_spec(G, which):
    D = G.shape[2]
    return (G, (N_DEV, FFN_SHARD, D), lambda *idx: (0, which, 0))


def ffn_up(n, G, name):
    S, D = n.shape
    F = N_DEV * FFN_SHARD
    tm = 512

    def body(n_ref, w1_ref, w3_ref, abh_ref):
        nv = n_ref[...]
        a = _dot(nv, w1_ref[...].reshape(FFN_TF, D), 1, 1).astype(BF16)
        b = _dot(nv, w3_ref[...].reshape(FFN_TF, D), 1, 1).astype(BF16)
        abh_ref[0] = a
        abh_ref[1] = b
        av, bv = a.astype(F32), b.astype(F32)
        abh_ref[2] = (av * _sigmoid(av) * bv).astype(BF16)

    return _call(name, body, (F // FFN_TF, S // tm),
                 [(n, (tm, D), lambda j, i: (i, 0)),
                  _ffn_w_spec(G, 0, lambda j, i: j), _ffn_w_spec(G, 1, lambda j, i: j)],
                 [((3, S, F), BF16, (3, tm, FFN_TF), lambda j, i: (0, i, j))],
                 sem=("parallel", "parallel"))[0]


def ffn_down(abh, G, x, name):
    _, S, F = abh.shape
    D = x.shape[1]
    tm = 512

    def body(h_ref, w2_ref, x_ref, o_ref):
        o_ref[...] = x_ref[...] + 0.5 * _dot(h_ref[...], w2_ref[...].reshape(F, D))

    return _call(name, body, (S // tm,),
                 [(abh, (None, tm, F), lambda i: (2, i, 0)), _ffn_whole_w_spec(G, 2),
                  (x, (tm, D), lambda i: (i, 0))],
                 [((S, D), F32, (tm, D), lambda i: (i, 0))], sem=("parallel",))[0]


def ffn_bwd_weights(dxo, abh, n, G, name):
    _, S, F = abh.shape
    D = dxo.shape[1]
    tm = 512
    nf = F // FFN_TF

    def down_body(d_ref, w2_ref, ab_ref, o_ref):
        dh = 0.5 * _dot(d_ref[...].astype(BF16), w2_ref[...].reshape(FFN_TF, D), 1, 1)
        av, bv = ab_ref[0].astype(F32), ab_ref[1].astype(F32)
        sig = _sigmoid(av)
        o_ref[0] = (dh * bv * (sig * (1.0 + av * (1.0 - sig)))).astype(BF16)
        o_ref[1] = (dh * (av * sig)).astype(BF16)

    dab = _call(name + "_down_bwd", down_body, (nf, S // tm),
                [(dxo, (tm, D), lambda j, i: (i, 0)), _ffn_w_spec(G, 2, lambda j, i: j),
                 (abh, (2, tm, FFN_TF), lambda j, i: (0, i, j))],
                [((2, S, F), BF16, (2, tm, FFN_TF), lambda j, i: (0, i, j))],
                sem=("parallel", "parallel"))[0]

    tk = WGRAD_TK
    nk = S // tk
    gshape = (N_DEV, 3 * FFN_SHARD, D)

    def dw2_body(h_ref, d_ref, o_ref, acc_ref):
        k = pl.program_id(1)
        p = _dot(h_ref[...], d_ref[...].astype(BF16), 0, 0)

        @pl.when(k == 0)
        def _():
            acc_ref[...] = p

        @pl.when(k > 0)
        def _():
            acc_ref[...] += p

        @pl.when(k == nk - 1)
        def _():
            o_ref[...] = (0.5 * acc_ref[...]).astype(BF16).reshape(4, FFN_SHARD, D)

    gw = _call(name + "_dw2", dw2_body, (nf, nk),
               [(abh, (None, tk, FFN_TF), lambda j, k: (2, k, j)), (dxo, (tk, D), lambda j, k: (k, 0))],
               [(gshape, BF16, (4, FFN_SHARD, D), lambda j, k: (j, 2, 0))],
               scratch=[pltpu.VMEM((FFN_TF, D), F32)], sem=("parallel", "arbitrary"))[0]

    def dw13_body(gw_ref, dab_ref, n_ref, o_ref, acc_ref):
        k = pl.program_id(2)
        p = _dot(dab_ref[...], n_ref[...], 0, 0)

        @pl.when(k == 0)
        def _():
            acc_ref[...] = p

        @pl.when(k > 0)
        def _():
            acc_ref[...] += p

        @pl.when(k == nk - 1)
        def _():
            o_ref[...] = acc_ref[...].astype(BF16).reshape(4, FFN_SHARD, D)

    gw = pl.pallas_call(
        dw13_body,
        out_shape=jax.ShapeDtypeStruct(gshape, BF16),
        grid=(2, nf, nk),
        in_specs=[pl.BlockSpec(memory_space=pl.ANY),
                  pl.BlockSpec((None, tk, FFN_TF), lambda w, j, k: (w, k, j)),
                  pl.BlockSpec((tk, D), lambda w, j, k: (k, 0))],
        out_specs=pl.BlockSpec((4, FFN_SHARD, D), lambda w, j, k: (j, w, 0)),
        scratch_shapes=[pltpu.VMEM((FFN_TF, D), F32)],
        input_output_aliases={0: 0},
        name=name + "_dw13",
        compiler_params=pltpu.CompilerParams(dimension_semantics=("parallel", "parallel", "arbitrary"),
                                             vmem_limit_bytes=VMEM_LIMIT),
    )(gw, dab, n)
    return dab, gw


def ffn_bwd_input(dab, G, x_in, g, dxo, name):
    _, S, F = dab.shape
    D = x_in.shape[1]
    tm = 256

    def dn_body(dab_ref, w1_ref, w3_ref, x_ref, d_ref, g_ref, dx_ref, dg_ref):
        i = pl.program_id(0)
        dn = _dot(dab_ref[0], w1_ref[...].reshape(F, D)) + _dot(dab_ref[1], w3_ref[...].reshape(F, D))
        dx, dgt = _rms_bwd_tile(dn, x_ref[...], g_ref[...])
        dx_ref[...] = d_ref[...] + dx
        dgp = jnp.sum(dgt, axis=0, keepdims=True)

        @pl.when(i == 0)
        def _():
            dg_ref[...] = dgp

        @pl.when(i > 0)
        def _():
            dg_ref[...] += dgp

    dx, dg = _call(name + "_dn", dn_body, (S // tm,),
                   [(dab, (2, tm, F), lambda i: (0, i, 0)),
                    _ffn_whole_w_spec(G, 0), _ffn_whole_w_spec(G, 1),
                    (x_in, (tm, D), lambda i: (i, 0)), (dxo, (tm, D), lambda i: (i, 0)),
                    (g, (1, D), lambda i: (0, 0))],
                   [((S, D), F32, (tm, D), lambda i: (i, 0)), ((1, D), F32, (1, D), lambda i: (0, 0))],
                   sem=("arbitrary",))
    return dx, dg


PROJ_TN = 512


def in_proj(h, Gm, first_tile, n_tiles, dtype, name, tile_stride=1):
    S, D = h.shape
    tm = 1024
    tile = lambda j: first_tile + tile_stride * j

    def body(h_ref, w_ref, o_ref):
        o_ref[...] = _dot(h_ref[...], w_ref[...]).astype(dtype)

    return _call(name, body, (n_tiles, S // tm),
                 [(h, (tm, D), lambda j, i: (i, 0)),
                  (Gm, (None, D, PROJ_TN), lambda j, i: (tile(j) // 2, 0, tile(j) % 2))],
                 [((S, n_tiles * PROJ_TN), dtype, (tm, PROJ_TN), lambda j, i: (i, j))],
                 sem=("parallel", "parallel"))[0]


def in_proj_bwd_dw(dproj, h, gm_grads, name):
    S, D = h.shape
    NT = dproj.shape[1] // PROJ_TN
    tk = WGRAD_TK
    nk = S // tk

    def dw_body(gm_ref, h_ref, d_ref, o_ref, acc_ref):
        k = pl.program_id(1)
        p = _dot(h_ref[...], d_ref[...], 0, 0)

        @pl.when(k == 0)
        def _():
            acc_ref[...] = p

        @pl.when(k > 0)
        def _():
            acc_ref[...] += p

        @pl.when(k == nk - 1)
        def _():
            o_ref[...] = acc_ref[...].astype(BF16)

    return pl.pallas_call(
        dw_body,
        out_shape=jax.ShapeDtypeStruct(gm_grads.shape, BF16),
        grid=(NT, nk),
        in_specs=[pl.BlockSpec(memory_space=pl.ANY),
                  pl.BlockSpec((tk, D), lambda j, k: (k, 0)),
                  pl.BlockSpec((tk, PROJ_TN), lambda j, k: (k, j))],
        out_specs=pl.BlockSpec((None, D, PROJ_TN), lambda j, k: (j // 2, 0, j % 2)),
        scratch_shapes=[pltpu.VMEM((D, PROJ_TN), F32)],
        input_output_aliases={0: 0},
        name=name + "_dw",
        compiler_params=pltpu.CompilerParams(dimension_semantics=("parallel", "arbitrary"),
                                             vmem_limit_bytes=VMEM_LIMIT),
    )(gm_grads, h, dproj)


def in_proj_bwd_dh(dproj, Gm, x_in, g, dres, name):
    S, D = x_in.shape
    tm = 512
    C = Gm.shape[2]
    n_sh = dproj.shape[1] // C

    def dh_body(d_ref, w_ref, x_ref, r_ref, g_ref, dx_ref, dg_ref, acc_ref):
        i, k = pl.program_id(0), pl.program_id(1)
        p = _dot(d_ref[...], w_ref[...], 1, 1)

        @pl.when(k == 0)
        def _():
            acc_ref[...] = p

        @pl.when(k > 0)
        def _():
            acc_ref[...] += p

        @pl.when(k == n_sh - 1)
        def _():
            dx, dgt = _rms_bwd_tile(acc_ref[...], x_ref[...], g_ref[...])
            dx_ref[...] = r_ref[...] + dx
            dgp = jnp.sum(dgt, axis=0, keepdims=True)

            @pl.when(i == 0)
            def _():
                dg_ref[...] = dgp

            @pl.when(i > 0)
            def _():
                dg_ref[...] += dgp

    dx, dg = _call(name + "_dh", dh_body, (S // tm, n_sh),
                   [(dproj, (tm, C), lambda i, k: (i, k)),
                    (Gm, (None, D, C), lambda i, k: (k, 0, 0)),
                    (x_in, (tm, D), lambda i, k: (i, 0)), (dres, (tm, D), lambda i, k: (i, 0)),
                    (g, (1, D), lambda i, k: (0, 0))],
                   [((S, D), F32, (tm, D), lambda i, k: (i, 0)), ((1, D), F32, (1, D), lambda i, k: (0, 0))],
                   scratch=[pltpu.VMEM((tm, D), F32)], sem=("arbitrary", "arbitrary"))
    return dx, dg


def _t5_bucket(rel):
    n = N_BUCKETS // 2
    max_exact = n // 2
    ret = jnp.where(rel > 0, n, 0)
    a = jnp.abs(rel)
    af = jnp.maximum(a, 1).astype(F32)
    large = max_exact + (jnp.log(af / max_exact) / math.log(MAX_DISTANCE / max_exact)
                         * (n - max_exact)).astype(jnp.int32)
    large = jnp.minimum(large, n - 1)
    return ret + jnp.where(a < max_exact, a, large)


def _bucket_tables():
    qi = jnp.arange(A_TQ, dtype=jnp.int32)[:, None]
    kj = jnp.arange(A_WIN, dtype=jnp.int32)[None, :]
    rel = kj - HALF_WINDOW - qi
    return jnp.stack([_t5_bucket(rel * d) for d in DILATIONS], axis=0)


def bias_build(rel_bias, buckets):
    def body(tab_ref, bk_ref, o_ref):
        col = pl.program_id(0) * HEADS_PER_GROUP_A + pl.program_id(1)
        bk = bk_ref[...]
        acc = jnp.zeros(bk.shape, F32)
        for b in range(N_BUCKETS):
            acc = jnp.where(bk == b, tab_ref[b, col], acc)
        qi = lax.broadcasted_iota(jnp.int32, bk.shape, 0)
        kj = lax.broadcasted_iota(jnp.int32, bk.shape, 1)
        o_ref[...] = jnp.where(jnp.abs(kj - HALF_WINDOW - qi) <= HALF_WINDOW, acc, NEG_INF)

    return pl.pallas_call(
        body,
        out_shape=jax.ShapeDtypeStruct((3, HEADS_PER_GROUP_A, A_TQ, A_WIN), F32),
        grid=(3, HEADS_PER_GROUP_A),
        in_specs=[pl.BlockSpec(memory_space=pltpu.SMEM),
                  pl.BlockSpec((None, A_TQ, A_WIN), lambda g, h: (g, 0, 0))],
        out_specs=pl.BlockSpec((None, None, A_TQ, A_WIN), lambda g, h: (g, h, 0, 0)),
        name="a_bias_build",
        compiler_params=pltpu.CompilerParams(dimension_semantics=("parallel", "parallel")),
    )(rel_bias, buckets)


def bias_bwd(dbias, buckets):
    def body(d_ref, bk_ref, o_ref):
        bk = bk_ref[...]
        dv = d_ref[...]
        for b in range(N_BUCKETS):
            part = jnp.sum(jnp.where(bk == b, dv, 0.0), axis=1, keepdims=True)
            o_ref[b:b + 1, :] = jnp.broadcast_to(jnp.sum(part, axis=0, keepdims=True), (1, LANES))

    out = pl.pallas_call(
        body,
        out_shape=jax.ShapeDtypeStruct((3, HEADS_PER_GROUP_A, N_BUCKETS, LANES), F32),
        grid=(3, HEADS_PER_GROUP_A),
        in_specs=[pl.BlockSpec((None, None, A_TQ, A_WIN), lambda g, h: (g, h, 0, 0)),
                  pl.BlockSpec((None, A_TQ, A_WIN), lambda g, h: (g, 0, 0))],
        out_specs=pl.BlockSpec((None, None, N_BUCKETS, LANES), lambda g, h: (g, h, 0, 0)),
        name="a_bias_bwd",
        compiler_params=pltpu.CompilerParams(dimension_semantics=("parallel", "parallel")),
    )(dbias, buckets)
    return out[:, :, :, 0].transpose(2, 0, 1).reshape(N_BUCKETS, 3 * HEADS_PER_GROUP_A)


def _a_fill_padded(pad_ref, src_ref, L):
    zeros = jnp.zeros((HALF_WINDOW, LANES), pad_ref.dtype)
    pad_ref[0:HALF_WINDOW, :] = zeros
    pad_ref[HALF_WINDOW + L:2 * HALF_WINDOW + L, :] = zeros
    pad_ref[HALF_WINDOW:HALF_WINDOW + L, :] = src_ref[...]


def _a_key_valid(qb, L):
    kidx = qb * A_TQ - HALF_WINDOW + lax.broadcasted_iota(jnp.int32, (A_TQ, A_WIN), 1)
    return (kidx >= 0) & (kidx < L)


def a_fwd(proj_g, bias_g, g, name):
    S = proj_g.shape[0]
    d = DILATIONS[g]
    L = S // d
    nqb = L // A_TQ
    nb = A_GROUP_QKV // LANES
    pv = proj_g.reshape(L, d * A_GROUP_QKV)

    def body(q_ref, k_ref, v_ref, b_ref, o_ref, l_ref, kpad, vpad):
        _a_fill_padded(kpad, k_ref, L)
        _a_fill_padded(vpad, v_ref, L)
        lane = lax.broadcasted_iota(jnp.int32, (A_TQ, LANES), 1)

        def block(qb, carry):
            start = pl.multiple_of(qb * A_TQ, A_TQ)
            kw = kpad[pl.ds(start, A_WIN), :]
            vw = vpad[pl.ds(start, A_WIN), :]
            q = q_ref[pl.ds(start, A_TQ), :]
            valid = _a_key_valid(qb, L)
            outs, lses = [], []
            for h in range(2):
                qh = jnp.where((lane >= HEAD_DIM_A * h) & (lane < HEAD_DIM_A * (h + 1)), q, jnp.zeros_like(q))
                s = _dot(qh, kw, 1, 1) * (HEAD_DIM_A ** -0.5) + b_ref[h]
                s = jnp.where(valid, s, NEG_INF)
                m = jnp.max(s, axis=-1, keepdims=True)
                e = jnp.exp(s - m)
                l = jnp.sum(e, axis=-1, keepdims=True)
                outs.append(_dot(e.astype(BF16), vw) / l)
                lses.append(m + jnp.log(l))
            o_ref[pl.ds(start, A_TQ), :] = jnp.where(lane < HEAD_DIM_A, outs[0], outs[1])
            l_ref[pl.ds(start, A_TQ), :] = jnp.where(lane < HEAD_DIM_A, lses[0], lses[1])
            return carry

        lax.fori_loop(0, nqb, block, 0, unroll=min(A_UNROLL, nqb))

    col = lambda which: (lambda r, hp: r * nb + which * 4 + hp)
    out_spec = ((L, d * GROUP_WIDTH_A), F32, (L, LANES), lambda r, hp: (0, r * 4 + hp))
    o, lse = _call(name, body, (d, 4),
                   [(pv, (L, LANES), lambda r, hp: (0, col(0)(r, hp))),
                    (pv, (L, LANES), lambda r, hp: (0, col(1)(r, hp))),
                    (pv, (L, LANES), lambda r, hp: (0, col(2)(r, hp))),
                    (bias_g, (2, A_TQ, A_WIN), lambda r, hp: (hp, 0, 0))],
                   [out_spec, out_spec],
                   scratch=[pltpu.VMEM((L + 2 * HALF_WINDOW, LANES), BF16)] * 2,
                   sem=("parallel", "parallel"))
    return o.reshape(S, GROUP_WIDTH_A), lse.reshape(S, GROUP_WIDTH_A)


def a_combine(outs, lses, name):
    S, W = outs[0].shape
    tr = 512

    def body(o0, o1, o2, l0, l1, l2, oa_ref, lt_ref):
        a, b, c = l0[...], l1[...], l2[...]
        m = jnp.maximum(jnp.maximum(a, b), c)
        ea, eb, ec = jnp.exp(a - m), jnp.exp(b - m), jnp.exp(c - m)
        z = ea + eb + ec
        oa_ref[...] = ((ea * o0[...] + eb * o1[...] + ec * o2[...]) / z).astype(BF16)
        lt_ref[...] = m + jnp.log(z)

    spec = ((tr, W), lambda i: (i, 0))
    return _call(name, body, (S // tr,), [(a,) + spec for a in (*outs, *lses)],
                 [((S, W), BF16) + spec, ((S, W), F32) + spec], sem=("parallel",))


def a_bwd(proj_g, bias_g, do_a, o_a, lse_tot, g, name):
    S = proj_g.shape[0]
    d = DILATIONS[g]
    L = S // d
    nqb = L // A_TQ
    nb = A_GROUP_QKV // LANES
    pv = proj_g.reshape(L, d * A_GROUP_QKV)
    view = lambda a: a.reshape(L, d * GROUP_WIDTH_A)
    scale = HEAD_DIM_A ** -0.5

    def body(q_ref, k_ref, v_ref, b_ref, do_ref, o_ref, l_ref, dq_ref, dk_ref, dv_ref, db_ref,
             kpad, vpad, dkacc, dvacc):
        r = pl.program_id(1)
        _a_fill_padded(kpad, k_ref, L)
        _a_fill_padded(vpad, v_ref, L)
        dkacc[...] = jnp.zeros(dkacc.shape, F32)
        dvacc[...] = jnp.zeros(dvacc.shape, F32)

        @pl.when(r == 0)
        def _():
            db_ref[...] = jnp.zeros(db_ref.shape, F32)

        lane = lax.broadcasted_iota(jnp.int32, (A_TQ, LANES), 1)

        def block(qb, carry):
            start = pl.multiple_of(qb * A_TQ, A_TQ)
            rows = pl.ds(start, A_TQ)
            kw = kpad[pl.ds(start, A_WIN), :]
            vw = vpad[pl.ds(start, A_WIN), :]
            q = q_ref[rows, :]
            do = do_ref[rows, :]
            ov = o_ref[rows, :].astype(F32)
            lt = l_ref[rows, :]
            valid = _a_key_valid(qb, L)
            dqs = []
            dk_win = jnp.zeros((A_WIN, LANES), F32)
            dv_win = jnp.zeros((A_WIN, LANES), F32)
            for h in range(2):
                mh = (lane >= HEAD_DIM_A * h) & (lane < HEAD_DIM_A * (h + 1))
                qh = jnp.where(mh, q, jnp.zeros_like(q))
                doh = jnp.where(mh, do, 0.0)
                s = _dot(qh, kw, 1, 1) * scale + b_ref[h]
                s = jnp.where(valid, s, NEG_INF)
                p = jnp.exp(s - lt[:, HEAD_DIM_A * h:HEAD_DIM_A * h + 1])
                t = jnp.sum(doh * ov, axis=-1, keepdims=True)
                dob = doh.astype(BF16)
                ds = p * (_dot(dob, vw, 1, 1) - t)
                db_ref[h] += ds
                dsb = (ds * scale).astype(BF16)
                dqs.append(_dot(dsb, kw))
                dk_win = dk_win + _dot(dsb, qh, 0, 0)
                dv_win = dv_win + _dot(p.astype(BF16), dob, 0, 0)
            dq_ref[rows, :] = jnp.where(lane < HEAD_DIM_A, dqs[0], dqs[1]).astype(BF16)
            dkacc[pl.ds(start, A_WIN), :] += dk_win
            dvacc[pl.ds(start, A_WIN), :] += dv_win
            return carry

        lax.fori_loop(0, nqb, block, 0, unroll=min(A_UNROLL, nqb))
        dk_ref[...] = dkacc[HALF_WINDOW:HALF_WINDOW + L, :].astype(BF16)
        dv_ref[...] = dvacc[HALF_WINDOW:HALF_WINDOW + L, :].astype(BF16)

    col = lambda which: (lambda hp, r: r * nb + which * 4 + hp)
    slab = ((L, LANES), lambda hp, r: (0, r * 4 + hp))
    oshape = (L, d * GROUP_WIDTH_A)
    dq, dk, dv, db = _call(
        name, body, (4, d),
        [(pv, (L, LANES), lambda hp, r: (0, col(0)(hp, r))),
         (pv, (L, LANES), lambda hp, r: (0, col(1)(hp, r))),
         (pv, (L, LANES), lambda hp, r: (0, col(2)(hp, r))),
         (bias_g, (2, A_TQ, A_WIN), lambda hp, r: (hp, 0, 0)),
         (view(do_a),) + slab, (view(o_a),) + slab, (view(lse_tot),) + slab],
        [(oshape, BF16) + slab, (oshape, BF16) + slab, (oshape, BF16) + slab,
         ((HEADS_PER_GROUP_A, A_TQ, A_WIN), F32, (2, A_TQ, A_WIN), lambda hp, r: (hp, 0, 0))],
        scratch=[pltpu.VMEM((L + 2 * HALF_WINDOW, LANES), BF16)] * 2
        + [pltpu.VMEM((L + 2 * HALF_WINDOW, LANES), F32)] * 2,
        sem=("parallel", "arbitrary"))
    return dq.reshape(S, GROUP_WIDTH_A), dk.reshape(S, GROUP_WIDTH_A), dv.reshape(S, GROUP_WIDTH_A), db


def _rope_tables(S):
    rows = S // GRID_W
    row = jnp.repeat(jnp.arange(rows, dtype=F32), GRID_W)
    col = jnp.tile(jnp.arange(GRID_W, dtype=F32), rows)
    n_freq = HEAD_DIM_B // 4
    freq = ROPE_THETA ** (-jnp.arange(n_freq, dtype=F32) / n_freq)
    ang = jnp.concatenate([row[:, None] * freq, col[:, None] * freq], axis=-1)
    cos, sin = jnp.cos(ang), jnp.sin(ang)
    return jnp.repeat(cos, 2, axis=-1), jnp.stack([-sin, sin], axis=-1).reshape(S, HEAD_DIM_B)


def _swap_pairs(y):
    lane = lax.broadcasted_iota(jnp.int32, y.shape, 1)
    return jnp.where(lane % 2 == 0, pltpu.roll(y, LANES - 1, 1), pltpu.roll(y, 1, 1))


def qkv_prep(proj_b, gains, cos_t, sin_t, name):
    S = proj_b.shape[0]
    ts = 256
    n_rot = N_HEADS_B + N_KV_B
    nh = n_rot + N_KV_B
    W = nh * LANES

    def body(x_ref, g_ref, c_ref, s_ref, o_ref):
        cv, sv = c_ref[...], s_ref[...]
        for hb in range(nh):
            cols = slice(hb * LANES, (hb + 1) * LANES)
            xv = x_ref[:, cols]
            if hb < n_rot:
                r = lax.rsqrt(jnp.mean(xv * xv, axis=-1, keepdims=True) + EPS)
                yv = xv * r * g_ref[:, cols]
                o_ref[:, cols] = (yv * cv + _swap_pairs(yv) * sv).astype(BF16)
            else:
                o_ref[:, cols] = xv.astype(BF16)

    return _call(name, body, (S // ts,),
                 [(proj_b, (ts, W), lambda i: (i, 0)), (gains, (1, W), lambda i: (0, 0)),
                  (cos_t, (ts, LANES), lambda i: (i, 0)), (sin_t, (ts, LANES), lambda i: (i, 0))],
                 [((S, W), BF16, (ts, W), lambda i: (i, 0))],
                 sem=("parallel",))[0]


def qk_prep_bwd(dr, proj_b, col0, gain, cos_t, sin_t, name):
    S, W = dr.shape
    H = W // LANES
    ts = 256
    xb = (col0 * LANES) // W

    def body(d_ref, x_ref, g_ref, c_ref, s_ref, dx_ref, dg_ref):
        i = pl.program_id(0)
        cv, sv, gv = c_ref[...], s_ref[...], g_ref[...]
        dgp = jnp.zeros((1, LANES), F32)
        for hb in range(H):
            cols = slice(hb * LANES, (hb + 1) * LANES)
            dout = d_ref[:, cols]
            dy = dout * cv + _swap_pairs(dout * sv)
            dx, dgt = _rms_bwd_tile(dy, x_ref[:, cols], gv)
            dx_ref[:, cols] = dx.astype(BF16)
            dgp = dgp + jnp.sum(dgt, axis=0, keepdims=True)

        @pl.when(i == 0)
        def _():
            dg_ref[...] = dgp

        @pl.when(i > 0)
        def _():
            dg_ref[...] += dgp

    return _call(name, body, (S // ts,),
                 [(dr, (ts, W), lambda i: (i, 0)), (proj_b, (ts, W), lambda i: (i, xb)),
                  (gain, (1, LANES), lambda i: (0, 0)),
                  (cos_t, (ts, LANES), lambda i: (i, 0)), (sin_t, (ts, LANES), lambda i: (i, 0))],
                 [((S, W), BF16, (ts, W), lambda i: (i, 0)),
                  ((1, LANES), F32, (1, LANES), lambda i: (0, 0))],
                 sem=("arbitrary",))


def _row_sums(x):
    hi = x.astype(BF16)
    lo = (x - hi.astype(F32)).astype(BF16)
    ones = jnp.ones((8, LANES), BF16)
    return (_dot(ones, hi, 1, 1) + _dot(ones, lo, 1, 1))[0:1, :]


def flash_fwd(qkv, name):
    S = qkv.shape[0]
    tq = B_TQ_FWD
    scale = HEAD_DIM_B ** -0.5

    def body(q_ref, k_ref, v_ref, o_ref, l_ref):
        s = _dot(q_ref[...], k_ref[...], 1, 1) * scale
        m = jnp.max(s, axis=-1, keepdims=True)
        e = jnp.exp(s - m)
        l = jnp.sum(e, axis=-1, keepdims=True)
        o_ref[...] = (_dot(e.astype(BF16), v_ref[...]) / l).astype(BF16)
        lse = jnp.broadcast_to(m + jnp.log(l), (tq, LANES))
        l_ref[...] = _row_sums(lse) * (1.0 / LANES)

    head = lambda g, h, i: (i, g * GQA_GROUP_B + h)
    return _call(name, body, (N_KV_B, GQA_GROUP_B, S // tq),
                 [(qkv, (tq, LANES), head),
                  (qkv, (S, LANES), lambda g, h, i: (0, N_HEADS_B + g)),
                  (qkv, (S, LANES), lambda g, h, i: (0, N_HEADS_B + N_KV_B + g))],
                 [((S, N_HEADS_B * LANES), BF16, (tq, LANES), head),
                  ((N_HEADS_B, 1, S), F32, (None, 1, tq), lambda g, h, i: (g * GQA_GROUP_B + h, 0, i))],
                 sem=("parallel", "parallel", "parallel"))


def flash_bwd(qkv, k_t, do_b, o_b, lse, name):
    S = qkv.shape[0]
    tq = B_TQ_BWD
    nq = S // tq
    scale = HEAD_DIM_B ** -0.5

    def body(q_ref, k_ref, v_ref, kt_ref, do_ref, o_ref, l_ref, dq_ref, dk_ref, dv_ref, dkacc, dvacc):
        h, i = pl.program_id(1), pl.program_id(2)

        @pl.when((h == 0) & (i == 0))
        def _():
            dkacc[...] = jnp.zeros(dkacc.shape, F32)
            dvacc[...] = jnp.zeros(dvacc.shape, F32)

        q = q_ref[...]
        do = do_ref[...]
        dob = do.astype(BF16)
        t = _row_sums(do * o_ref[...].astype(F32))
        pt = jnp.exp(_dot(k_ref[...], q, 1, 1) * scale - l_ref[...])
        dst = pt * (_dot(v_ref[...], dob, 1, 1) - t) * scale
        dsb = dst.astype(BF16)
        dvacc[...] += _dot(pt.astype(BF16), dob)
        dkacc[...] += _dot(dsb, q)
        dq_ref[...] = _dot(kt_ref[...], dsb).T

        @pl.when((h == GQA_GROUP_B - 1) & (i == nq - 1))
        def _():
            dk_ref[...] = dkacc[...]
            dv_ref[...] = dvacc[...].astype(BF16)

    head = lambda g, h, i: (i, g * GQA_GROUP_B + h)
    return _call(name, body, (N_KV_B, GQA_GROUP_B, nq),
                 [(qkv, (tq, LANES), head),
                  (qkv, (S, LANES), lambda g, h, i: (0, N_HEADS_B + g)),
                  (qkv, (S, LANES), lambda g, h, i: (0, N_HEADS_B + N_KV_B + g)),
                  (k_t, (LANES, S), lambda g, h, i: (g, 0)),
                  (do_b, (tq, LANES), head), (o_b, (tq, LANES), head),
                  (lse, (None, 1, tq), lambda g, h, i: (g * GQA_GROUP_B + h, 0, i))],
                 [((S, N_HEADS_B * LANES), F32, (tq, LANES), head),
                  ((S, N_KV_B * LANES), F32, (S, LANES), lambda g, h, i: (0, g)),
                  ((S, N_KV_B * LANES), BF16, (S, LANES), lambda g, h, i: (0, g))],
                 scratch=[pltpu.VMEM((S, LANES), F32)] * 2,
                 sem=("parallel", "arbitrary", "arbitrary"))


MERGE_TN = 512


def _mix_rows_spec(Gm, row0, n_slots, slot_map, cols=None, col_map=None):
    C = Gm.shape[2] if cols is None else cols
    cm = (lambda *idx: 0) if col_map is None else col_map
    return (Gm, (n_slots, LANES, C), lambda *idx: (slot_map(*idx), row0 // LANES, cm(*idx)))


def merge_fwd(o_a, o_b, w_a, Gm, proj_b, b_gate, name):
    S = o_a.shape[0]
    D = w_a.shape[1]
    tm, tn = 512, MERGE_TN
    ga0, gb0 = PB_GATE_A // tn, PB_GATE_B // tn

    def body(oa_ref, ob_ref, wa_ref, wb_ref, pa_ref, pb_ref, ba_ref, bb_ref, m_ref, ya_ref, yb_ref):
        ya = _dot(oa_ref[...], wa_ref[...])
        yb = _dot(ob_ref[...], wb_ref[...].reshape(N_DEV * LANES, tn))
        ga = _sigmoid(pa_ref[...] + ba_ref[...])
        gb = _sigmoid(pb_ref[...] + bb_ref[...])
        m_ref[...] = (ga * ya + gb * yb).astype(BF16)
        ya_ref[...] = ya.astype(BF16)
        yb_ref[...] = yb.astype(BF16)

    out = ((S, D), BF16, (tm, tn), lambda j, i: (i, j))
    return _call(name, body, (D // tn, S // tm),
                 [(o_a, (tm, o_a.shape[1]), lambda j, i: (i, 0)), (o_b, (tm, o_b.shape[1]), lambda j, i: (i, 0)),
                  (w_a, (w_a.shape[0], tn), lambda j, i: (0, j)),
                  _mix_rows_spec(Gm, MIX_WB, N_DEV, lambda j, i: 0, cols=tn, col_map=lambda j, i: j),
                  (proj_b, (tm, tn), lambda j, i: (i, ga0 + j)), (proj_b, (tm, tn), lambda j, i: (i, gb0 + j)),
                  (b_gate, (1, tn), lambda j, i: (0, j)), (b_gate, (1, tn), lambda j, i: (0, D // tn + j))],
                 [out, out, out], sem=("parallel", "parallel"))


def out_proj(merged, Gm, x, name):
    S, D = x.shape
    tm, tn = 512, MERGE_TN

    def body(m_ref, w_ref, x_ref, o_ref):
        o_ref[...] = x_ref[...] + _dot(m_ref[...], w_ref[...].reshape(N_DEV * LANES, tn))

    return _call(name, body, (D // tn, S // tm),
                 [(merged, (tm, D), lambda j, i: (i, 0)),
                  _mix_rows_spec(Gm, MIX_WOUT, N_DEV, lambda j, i: 0, cols=tn, col_map=lambda j, i: j),
                  (x, (tm, tn), lambda j, i: (i, j))],
                 [((S, D), F32, (tm, tn), lambda j, i: (i, j))], sem=("parallel", "parallel"))[0]


def merge_bwd(dx2, Gm, ya, yb, proj_b, b_gate, name):
    S, D = dx2.shape
    tm, tn = 512, MERGE_TN
    nn = D // tn
    ga0, gb0 = PB_GATE_A // tn, PB_GATE_B // tn

    def body(d_ref, w_ref, ya_ref, yb_ref, pa_ref, pb_ref, ba_ref, bb_ref, dya_ref, dyb_ref, dg_ref, dbg_ref):
        i = pl.program_id(1)
        dm = _dot(d_ref[...].astype(BF16), w_ref[...].reshape(tn, D), 1, 1)
        ga = _sigmoid(pa_ref[...] + ba_ref[...])
        gb = _sigmoid(pb_ref[...] + bb_ref[...])
        dya_ref[...] = (dm * ga).astype(BF16)
        dyb_ref[...] = (dm * gb).astype(BF16)
        dpa = dm * ya_ref[...].astype(F32) * ga * (1.0 - ga)
        dpb = dm * yb_ref[...].astype(F32) * gb * (1.0 - gb)
        dg_ref[0] = dpa.astype(BF16)
        dg_ref[1] = dpb.astype(BF16)
        sa = jnp.sum(dpa, axis=0, keepdims=True)
        sb = jnp.sum(dpb, axis=0, keepdims=True)

        @pl.when(i == 0)
        def _():
            dbg_ref[0] = sa
            dbg_ref[1] = sb

        @pl.when(i > 0)
        def _():
            dbg_ref[0] += sa
            dbg_ref[1] += sb

    tile = ((tm, tn), lambda j, i: (i, j))
    dya, dyb, dgate, dbg = _call(
        name, body, (nn, S // tm),
        [(dx2, (tm, D), lambda j, i: (i, 0)),
         _mix_rows_spec(Gm, MIX_WOUT, tn // LANES, lambda j, i: j),
         (ya,) + tile, (yb,) + tile,
         (proj_b, (tm, tn), lambda j, i: (i, ga0 + j)), (proj_b, (tm, tn), lambda j, i: (i, gb0 + j)),
         (b_gate, (1, tn), lambda j, i: (0, j)), (b_gate, (1, tn), lambda j, i: (0, nn + j))],
        [((S, D), BF16) + tile, ((S, D), BF16) + tile,
         ((2, S, D), BF16, (2, tm, tn), lambda j, i: (0, i, j)),
         ((2, 1, D), F32, (2, 1, tn), lambda j, i: (0, 0, j))],
        sem=("parallel", "arbitrary"))
    return dya, dyb, dgate, dbg


def matmul_nt(a, b_spec_fn, N, name, tn=512):
    S, K = a.shape
    tm = 512

    def body(a_ref, b_ref, o_ref):
        b = b_ref[...]
        o_ref[...] = _dot(a_ref[...], b.reshape(-1, b.shape[-1]), 1, 1)

    return _call(name, body, (N // tn, S // tm),
                 [(a, (tm, K), lambda j, i: (i, 0)), b_spec_fn(lambda j, i: j)],
                 [((S, N), F32, (tm, tn), lambda j, i: (i, j))], sem=("parallel", "parallel"))[0]


def weight_grad_rows(a, b, grads, row0, name):
    S, M = a.shape
    N = b.shape[1]
    tmm = 512
    tk = WGRAD_TK
    nk = S // tk

    def body(g_ref, a_ref, b_ref, o_ref, acc_ref):
        k = pl.program_id(1)
        p = _dot(a_ref[...], b_ref[...].astype(BF16), 0, 0)

        @pl.when(k == 0)
        def _():
            acc_ref[...] = p

        @pl.when(k > 0)
        def _():
            acc_ref[...] += p

        @pl.when(k == nk - 1)
        def _():
            o_ref[...] = acc_ref[...].astype(BF16).reshape(tmm // LANES, LANES, N)

    return pl.pallas_call(
        body,
        out_shape=jax.ShapeDtypeStruct(grads.shape, BF16),
        grid=(M // tmm, nk),
        in_specs=[pl.BlockSpec(memory_space=pl.ANY),
                  pl.BlockSpec((tk, tmm), lambda j, k: (k, j)),
                  pl.BlockSpec((tk, N), lambda j, k: (k, 0))],
        out_specs=pl.BlockSpec((tmm // LANES, LANES, N), lambda j, k: (j, row0 // LANES, 0)),
        scratch_shapes=[pltpu.VMEM((tmm, N), F32)],
        input_output_aliases={0: 0},
        name=name,
        compiler_params=pltpu.CompilerParams(dimension_semantics=("parallel", "arbitrary"),
                                             vmem_limit_bytes=VMEM_LIMIT),
    )(grads, a, b)


def weight_grad_plain(a, b, name):
    S, M = a.shape
    N = b.shape[1]
    tk = WGRAD_TK
    nk = S // tk

    def body(a_ref, b_ref, o_ref, acc_ref):
        k = pl.program_id(0)
        p = _dot(a_ref[...], b_ref[...], 0, 0)

        @pl.when(k == 0)
        def _():
            acc_ref[...] = p

        @pl.when(k > 0)
        def _():
            acc_ref[...] += p

        @pl.when(k == nk - 1)
        def _():
            o_ref[...] = acc_ref[...].astype(BF16)

    return _call(name, body, (nk,),
                 [(a, (tk, M), lambda k: (k, 0)), (b, (tk, N), lambda k: (k, 0))],
                 [((M, N), BF16, (M, N), lambda k: (0, 0))],
                 scratch=[pltpu.VMEM((M, N), F32)], sem=("arbitrary",))[0]


def local_step(x, tgt, p, G1, get_gm, get_g2, emit, start_token):
    S, D = x.shape
    after = lambda t: t[0:1, 0:1]
    buckets = _bucket_tables()
    cos_t, sin_t = _rope_tables(S)
    gains = jnp.concatenate([jnp.tile(p["q_norm"], (1, N_HEADS_B)), jnp.tile(p["k_norm"], (1, N_KV_B)),
                             jnp.ones((1, N_KV_B * LANES), F32)], axis=1)

    n1 = rms_fwd(x, p["ffn1_norm"] + after(start_token), "ffn1_norm")
    ab1 = ffn_up(n1, G1, "ffn1_up")
    x1 = ffn_down(ab1, G1, x, "ffn1_down")

    Gm = get_gm(x1)
    w_a = Gm[:, MIX_WA:MIX_ROWS, :].reshape(N_DEV, GROUP_WIDTH_A, LANES).transpose(1, 0, 2).reshape(GROUP_WIDTH_A, D)
    hm = rms_fwd(x1, p["mix_norm"], "mix_norm")
    n_a = A_QKV_WIDTH // PROJ_TN
    proj_a = [in_proj(hm, Gm, g, 3, BF16, "in_proj_a%d" % g, tile_stride=3) for g in range(3)]
    proj_b = in_proj(hm, Gm, n_a, PB_WIDTH // PROJ_TN, F32, "in_proj_b")

    bias = bias_build(p["rel_bias"], buckets)
    outs, lses = [], []
    for g in range(3):
        o, l = a_fwd(proj_a[g], bias[g], g, "a_fwd_%d" % g)
        outs.append(o)
        lses.append(l)
    o_a, lse_tot = a_combine(outs, lses, "a_combine")

    qkv = qkv_prep(proj_b, gains, cos_t, sin_t, "qkv_prep")
    k_t = qkv[:, N_HEADS_B * LANES:(N_HEADS_B + N_KV_B) * LANES].T
    o_b, lse_b = flash_fwd(qkv, "flash_fwd")

    merged, ya, yb = merge_fwd(o_a, o_b, w_a, Gm, proj_b, p["b_gate"], "merge_fwd")
    x2 = out_proj(merged, Gm, x1, "out_proj")

    G2 = get_g2(x2)
    n2 = rms_fwd(x2, p["ffn2_norm"], "ffn2_norm")
    ab2 = ffn_up(n2, G2, "ffn2_up")
    x3 = ffn_down(ab2, G2, x2, "ffn2_down")

    loss, dx3, d_final = final_loss(x3, tgt, p["final_norm"], "final_loss")

    dabh2, gw2 = ffn_bwd_weights(dx3, ab2, n2, G2, "ffn2_bwd")
    t2 = emit("ffn2", gw2)
    dx2, d_ffn2_norm = ffn_bwd_input(dabh2, G2, x2, p["ffn2_norm"] + after(t2), dx3, "ffn2_bwd")

    dya, dyb, dgate, dbg = merge_bwd(dx2, Gm, ya, yb, proj_b, p["b_gate"], "merge_bwd")
    gm_grads = jnp.zeros(Gm.shape, BF16)
    gm_grads = weight_grad_rows(merged, dx2, gm_grads, MIX_WOUT, "dw_out")
    gm_grads = weight_grad_rows(o_b, dyb, gm_grads, MIX_WB, "dw_branch_b")
    dw_a = weight_grad_plain(o_a, dya, "dw_branch_a")
    do_a = matmul_nt(dya, lambda jm: (w_a, (MERGE_TN, D), lambda j, i: (jm(j, i), 0)), GROUP_WIDTH_A, "do_a")
    do_b = matmul_nt(dyb, lambda jm: _mix_rows_spec(Gm, MIX_WB, MERGE_TN // LANES, jm), N_HEADS_B * LANES, "do_b")

    dq_r, dk_r, dv_b = flash_bwd(qkv, k_t, do_b, o_b, lse_b, "flash_bwd")
    dq_b, d_q_norm = qk_prep_bwd(dq_r, proj_b, 0, p["q_norm"], cos_t, sin_t, "q_prep_bwd")
    dk_b, d_k_norm = qk_prep_bwd(dk_r, proj_b, N_HEADS_B, p["k_norm"], cos_t, sin_t, "k_prep_bwd")

    dqs, dks, dvs, dbs = [], [], [], []
    for g in range(3):
        dq, dk, dv, db = a_bwd(proj_a[g], bias[g], do_a, o_a, lse_tot, g, "a_bwd_%d" % g)
        dqs.append(dq)
        dks.append(dk)
        dvs.append(dv)
        dbs.append(db)
    d_rel_bias = bias_bwd(jnp.stack(dbs, axis=0), buckets)

    dproj = jnp.concatenate(dqs + dks + dvs + [dq_b, dk_b, dv_b, dgate[0], dgate[1]], axis=1)
    gm_grads = in_proj_bwd_dw(dproj, hm, gm_grads, "in_proj_bwd")
    dw_a_sh = dw_a.reshape(GROUP_WIDTH_A, N_DEV, LANES).transpose(1, 0, 2).reshape(N_DEV, MIX_ROWS - MIX_WA, D)
    gm_grads = lax.dynamic_update_slice(gm_grads, dw_a_sh, (0, MIX_WA, 0))
    tm = emit("mix", gm_grads)
    dx1, d_mix_norm = in_proj_bwd_dh(dproj, Gm, x1, p["mix_norm"] + after(tm), dx2, "in_proj_bwd")

    dabh1, gw1 = ffn_bwd_weights(dx1, ab1, n1, G1, "ffn1_bwd")
    t1 = emit("ffn1", gw1)
    dx0, d_ffn1_norm = ffn_bwd_input(dabh1, G1, x, p["ffn1_norm"] + after(t1), dx1, "ffn1_bwd")

    small = dict(ffn1_norm=d_ffn1_norm, mix_norm=d_mix_norm, b_gate=dbg.reshape(1, 2 * D),
                 q_norm=d_q_norm, k_norm=d_k_norm, rel_bias=d_rel_bias, ffn2_norm=d_ffn2_norm,
                 final_norm=d_final)
    return loss, dx0, small


def _pack_small(t, loss_row):
    row6 = jnp.concatenate([t["q_norm"].reshape(1, -1), t["k_norm"].reshape(1, -1), t["rel_bias"].reshape(1, -1)], axis=1)
    return jnp.concatenate([t["ffn1_norm"].reshape(1, -1), t["mix_norm"].reshape(1, -1), t["b_gate"].reshape(2, -1),
                            t["ffn2_norm"].reshape(1, -1), t["final_norm"].reshape(1, -1), row6, loss_row], axis=0)


def _unpack_small(a, shapes):
    return dict(ffn1_norm=a[0:1].reshape(shapes["ffn1_norm"]), mix_norm=a[1:2].reshape(shapes["mix_norm"]),
                b_gate=a[2:4].reshape(shapes["b_gate"]), ffn2_norm=a[4:5].reshape(shapes["ffn2_norm"]),
                final_norm=a[5].reshape(shapes["final_norm"]), q_norm=a[6:7, 0:128].reshape(shapes["q_norm"]),
                k_norm=a[6:7, 128:256].reshape(shapes["k_norm"]), rel_bias=a[6, 256:1024].reshape(shapes["rel_bias"]))


SMALL = ("ffn1_norm", "mix_norm", "b_gate", "q_norm", "k_norm", "rel_bias", "ffn2_norm", "final_norm")
ORDER = ("ffn1_norm", "ffn1_w1", "ffn1_w3", "ffn1_w2", "mix_norm", "w_in", "b_gate", "q_norm", "k_norm", "rel_bias",
         "w_branch_a", "w_branch_b", "w_out", "ffn2_norm", "ffn2_w1", "ffn2_w3", "ffn2_w2", "final_norm")


def kernel(x, ffn1_norm, ffn1_w1, ffn1_w3, ffn1_w2, mix_norm, w_in, b_gate, q_norm, k_norm, rel_bias, w_branch_a, w_branch_b, w_out, ffn2_norm, ffn2_w1, ffn2_w3, ffn2_w2, final_norm, loss_target, m_ffn1_norm, m_ffn1_w1, m_ffn1_w3, m_ffn1_w2, m_mix_norm, m_w_in, m_b_gate, m_q_norm, m_k_norm, m_rel_bias, m_w_branch_a, m_w_branch_b, m_w_out, m_ffn2_norm, m_ffn2_w1, m_ffn2_w3, m_ffn2_w2, m_final_norm, v_ffn1_norm, v_ffn1_w1, v_ffn1_w3, v_ffn1_w2, v_mix_norm, v_w_in, v_b_gate, v_q_norm, v_k_norm, v_rel_bias, v_w_branch_a, v_w_branch_b, v_w_out, v_ffn2_norm, v_ffn2_w1, v_ffn2_w3, v_ffn2_w2, v_final_norm):
    args = dict(locals())
    w = {n: args[n] for n in ORDER}
    m = {n: args["m_" + n] for n in ORDER}
    v = {n: args["v_" + n] for n in ORDER}
    D = x.shape[2]

    def ffn_group(w1, w3, w2):
        return jnp.concatenate([w1[0].T, w3[0].T, w2[0]], axis=0).astype(BF16)

    g1 = ffn_group(ffn1_w1, ffn1_w3, ffn1_w2)
    g2 = ffn_group(ffn2_w1, ffn2_w3, ffn2_w2)
    gm = jnp.concatenate([w_in[0], w_branch_b[0], w_out[0], w_branch_a[0].reshape(MIX_ROWS - MIX_WA, D)],
                         axis=0).astype(BF16)
    (G1,), (gm, g2) = all_gather_groups([g1], later=(gm, g2))
    ag_m = all_gather_start(gm, "all_gather_mix_start")
    ag_2 = all_gather_start(g2, "all_gather_ffn2_start")

    def get_gm(after):
        return all_gather_finish(*_split_wait("all_gather_mix_wait", ag_m, 4, after), "all_gather_mix_finish")

    def get_g2(after):
        return all_gather_finish(*_split_wait("all_gather_ffn2_wait", ag_2, 4, after), "all_gather_ffn2_finish")

    core = lax.axis_index("c").astype(jnp.int32).reshape(1)
    chip = (2 * lax.axis_index("x") + lax.axis_index("y")).astype(jnp.int32).reshape(1)
    exchanges = {}

    def emit(tag, gw):
        (theirs,) = reduce_scatter_pair([gw], "reduce_scatter_pair_" + tag)
        part = pair_add(gw, theirs, core, "pair_add_" + tag)
        exchanges[tag] = reduce_scatter_start(part, "reduce_scatter_" + tag + "_start")
        return exchanges[tag][4]

    small_p = dict(ffn1_norm=ffn1_norm, mix_norm=mix_norm, b_gate=b_gate, q_norm=q_norm, k_norm=k_norm,
                   rel_bias=rel_bias, ffn2_norm=ffn2_norm, final_norm=final_norm.reshape(1, D))
    loss_p, grad_x, small_g = local_step(x[0], loss_target[0], small_p, G1, get_gm, get_g2, emit, ag_m[4] + ag_2[4])

    def landed(tag, after):
        return _split_wait("reduce_scatter_" + tag + "_wait", exchanges[tag], 3, after)

    grads = {}
    last_token = exchanges["ffn1"][4]
    for tag, after in (("ffn2", last_token), ("ffn1", grad_x)):
        part, land = landed(tag, after)
        ssum = lambda off, nm: sum_chips(part, land, chip, off, FFN_SHARD, FFN_SHARD, tag + nm)
        grads[tag + "_w1"] = ssum(0, "_w1_sum").T[None]
        grads[tag + "_w3"] = ssum(FFN_SHARD, "_w3_sum").T[None]
        grads[tag + "_w2"] = ssum(2 * FFN_SHARD, "_w2_sum")[None]
        if tag == "ffn2":
            part_m, land_m = landed("mix", last_token)
            msum = lambda off, rows, blk, nm: sum_chips(part_m, land_m, chip, off, rows, blk, nm)
            grads["w_in"] = msum(MIX_WIN, MIX_WB - MIX_WIN, LANES, "w_in_sum")[None]
            grads["w_branch_b"] = msum(MIX_WB, LANES, LANES, "w_branch_b_sum")[None]
            grads["w_out"] = msum(MIX_WOUT, LANES, LANES, "w_out_sum")[None]
            grads["w_branch_a"] = msum(MIX_WA, MIX_ROWS - MIX_WA, MIX_ROWS - MIX_WA,
                                       "w_branch_a_sum").reshape(w_branch_a.shape)
    loss_row = jnp.pad(loss_p, ((0, 0), (0, D - LANES)))
    smalls = small_all_gather(_pack_small(small_g, loss_row))
    small_sum = sum_slots(smalls, 0, N_DEV, N_DEV, "small_sum")
    small_shapes = {n: w[n].shape for n in SMALL}
    grads.update(_unpack_small(small_sum, small_shapes))
    loss = small_sum[7, 0]

    delta, new_m, new_v = {}, {}, {}
    for n in ORDER:
        if n in SMALL:
            continue
        shp = w[n].shape
        two_d = lambda a: a.reshape(shp[-2], shp[-1])
        d_, m_, v_ = adamw(two_d(w[n]), two_d(grads[n]), two_d(m[n]), two_d(v[n]), "adamw_" + n)
        delta[n], new_m[n], new_v[n] = d_.reshape(shp), m_.reshape(shp), v_.reshape(shp)
    zero_row = jnp.zeros((1, D), F32)
    pack = lambda t: _pack_small({n: t[n] for n in SMALL}, zero_row)
    d_, m_, v_ = adamw(pack(w), small_sum, pack(m), pack(v), "adamw_small")
    for src, dst in ((d_, delta), (m_, new_m), (v_, new_v)):
        dst.update(_unpack_small(src, small_shapes))

    return (loss, grad_x[None], *[grads[n] for n in ORDER], *[delta[n] for n in ORDER],
            *[new_m[n] for n in ORDER], *[new_v[n] for n in ORDER])
```

```python
import math

import jax
import jax.numpy as jnp
from jax import lax
from jax.experimental import pallas as pl
from jax.experimental.pallas import tpu as pltpu

F32 = jnp.float32
BF16 = jnp.bfloat16
MESH = pl.DeviceIdType.MESH

V7X_VMEM_BYTES = 64 * 1024 * 1024
VMEM_LIMIT = V7X_VMEM_BYTES - 8 * 1024 * 1024
LANES = 128

N_DEV = 8
EPS = 1e-6
NEG_INF = -1e30

DILATIONS = (1, 4, 16)
HALF_WINDOW = 64
HEAD_DIM_A = 64
HEADS_PER_GROUP_A = 8
GROUP_WIDTH_A = 512
A_QKV_WIDTH = 4608
A_GROUP_QKV = A_QKV_WIDTH // 3
A_TQ = 128
A_WIN = A_TQ + 2 * HALF_WINDOW
A_UNROLL = 4
WGRAD_TK = 2048
HEAD_DIM_B = 128
N_HEADS_B = 8
N_KV_B = 2
GQA_GROUP_B = 4
GRID_W = 64
ROPE_THETA = 10000.0
B_TQ_FWD = 256
B_TQ_BWD = 512
N_BUCKETS = 32
MAX_DISTANCE = 1024
PB_WIDTH = 3584
PB_GATE_A = 1536
PB_GATE_B = 2560

ADAM_LR = 0.001
ADAM_B1 = 0.9
ADAM_B2 = 0.999
ADAM_EPS = 1e-08
ADAM_WD = 0.01
ADAM_STEP = 10

FFN_SHARD = 352
MIX_WIN, MIX_WB, MIX_WOUT, MIX_WA = 0, 1024, 1152, 1280
MIX_ROWS = 1344


def _dot(a, b, ca=1, cb=0):
    return lax.dot_general(a, b, (((ca,), (cb,)), ((), ())), preferred_element_type=F32)


def _call(name, body, grid, ins, outs, scratch=(), sem=None, aliases=None):
    res = pl.pallas_call(
        body,
        out_shape=[jax.ShapeDtypeStruct(s, d) for (s, d, _, _) in outs],
        grid=grid,
        in_specs=[pl.BlockSpec(bs, im) for (_, bs, im) in ins],
        out_specs=[pl.BlockSpec(bs, im) for (_, _, bs, im) in outs],
        scratch_shapes=list(scratch),
        name=name,
        input_output_aliases=aliases or {},
        compiler_params=pltpu.CompilerParams(dimension_semantics=sem, vmem_limit_bytes=VMEM_LIMIT),
    )(*[a for (a, _, _) in ins])
    return res


def _sigmoid(x):
    return 1.0 / (1.0 + jnp.exp(-x))


def _position():
    return lax.axis_index("x"), lax.axis_index("y"), lax.axis_index("c")


def _hbm_specs(n):
    return [pl.BlockSpec(memory_space=pl.ANY) for _ in range(n)]


def all_gather_groups(groups, later=()):
    n = len(groups)
    n_later = len(later)

    def body(*refs):
        ins, outs = refs[:n], refs[n + n_later:2 * n + n_later]
        refs = refs[2 * n_later:]
        stage = refs[2 * n:3 * n]
        send_sems, recv_sems, local_sems = refs[3 * n:]
        x, y, c = _position()
        sibling = (x, y, 1 - c)
        chips = [(1 - x, y), (x, 1 - y), (1 - x, 1 - y)]

        def copy(i, k, block, to, src=None):
            px, py, pc = block
            dst = outs[i].at[4 * px + 2 * py + pc]
            return pltpu.make_async_remote_copy(
                src_ref=dst if src is None else src, dst_ref=dst,
                send_sem=send_sems.at[i, k], recv_sem=recv_sems.at[i, k],
                device_id=to, device_id_type=MESH)

        loads = [pltpu.make_async_copy(ins[i], stage[i], local_sems.at[i, 0]) for i in range(n)]
        for ld in loads:
            ld.start()
        sends, stores = [], []
        for i in range(n):
            loads[i].wait()
            first = [copy(i, 0, (x, y, c), sibling, src=stage[i])]
            first += [copy(i, 1 + j, (x, y, c), (*chip, c), src=stage[i]) for j, chip in enumerate(chips)]
            for cp in first:
                cp.start()
            sends += first
            st = pltpu.make_async_copy(stage[i], outs[i].at[4 * x + 2 * y + c], local_sems.at[i, 1])
            st.start()
            stores.append(st)
        for i in range(n):
            for j, chip in enumerate(chips):
                copy(i, 1 + j, (*chip, c), (x, y, c)).wait_recv()
                passed = copy(i, 4 + j, (*chip, c), sibling)
                passed.start()
                sends.append(passed)
        for i in range(n):
            copy(i, 0, sibling, (x, y, c)).wait_recv()
            for j, chip in enumerate(chips):
                copy(i, 4 + j, (*chip, 1 - c), (x, y, c)).wait_recv()
        for cp in sends:
            cp.wait_send()
        for st in stores:
            st.wait()

    res = pl.pallas_call(
        body,
        out_shape=[jax.ShapeDtypeStruct((N_DEV,) + g.shape, g.dtype) for g in groups]
        + [jax.ShapeDtypeStruct(a.shape, a.dtype) for a in later],
        in_specs=_hbm_specs(n + n_later),
        out_specs=_hbm_specs(n + n_later),
        scratch_shapes=[pltpu.VMEM(g.shape, g.dtype) for g in groups]
        + [pltpu.SemaphoreType.DMA((n, 7)), pltpu.SemaphoreType.DMA((n, 7)), pltpu.SemaphoreType.DMA((n, 2))],
        input_output_aliases={n + i: n + i for i in range(n_later)},
        name="all_gather_weights",
        compiler_params=pltpu.CompilerParams(vmem_limit_bytes=VMEM_LIMIT),
    )(*groups, *later)
    return res[:n], res[n:]


PAIR_BUFFERS = 4


def reduce_scatter_pair(grads, name):
    n = len(grads)
    C = grads[0].shape[2]
    half = [g.shape[1] // 2 for g in grads]
    chunks = [(i, q, hf) for i in range(n) for q in range(4) for hf in range(2)]
    nb = PAIR_BUFFERS

    def body(*refs):
        ins, theirs = refs[:n], refs[n:2 * n]
        buf, load_sems, send_sems, recv_sems = refs[2 * n:]
        x, y, c = _position()
        sibling = (x, y, 1 - c)

        def load(k):
            i, q, hf = chunks[k]
            r = half[i]
            return pltpu.make_async_copy(ins[i].at[2 * q + (1 - c), pl.ds(hf * r, r), :],
                                         buf.at[k % nb, pl.ds(0, r), :], load_sems.at[k % nb])

        def send(k):
            i, q, hf = chunks[k]
            r = half[i]
            return pltpu.make_async_remote_copy(
                src_ref=buf.at[k % nb, pl.ds(0, r), :], dst_ref=theirs[i].at[q, pl.ds(hf * r, r), :],
                send_sem=send_sems.at[k % nb], recv_sem=recv_sems.at[i],
                device_id=sibling, device_id_type=MESH)

        for k in range(len(chunks) + 1):
            if k < len(chunks):
                if k >= nb:
                    send(k - nb).wait_send()
                load(k).start()
            if k >= 1:
                load(k - 1).wait()
                send(k - 1).start()
        for k in range(max(0, len(chunks) - nb), len(chunks)):
            send(k).wait_send()
        for i in range(n):
            pltpu.make_async_remote_copy(
                src_ref=theirs[i], dst_ref=theirs[i], send_sem=send_sems.at[0], recv_sem=recv_sems.at[i],
                device_id=sibling, device_id_type=MESH).wait_recv()

    return pl.pallas_call(
        body,
        out_shape=[jax.ShapeDtypeStruct((4,) + g.shape[1:], g.dtype) for g in grads],
        in_specs=_hbm_specs(n),
        out_specs=_hbm_specs(n),
        scratch_shapes=[pltpu.VMEM((nb, max(half), C), grads[0].dtype), pltpu.SemaphoreType.DMA((nb,)),
                        pltpu.SemaphoreType.DMA((nb,)), pltpu.SemaphoreType.DMA((n,))],
        name=name,
        compiler_params=pltpu.CompilerParams(vmem_limit_bytes=VMEM_LIMIT),
    )(*grads)


_HBM_SPEC = pl.BlockSpec(memory_space=pltpu.HBM)
_SEM_SPEC = pl.BlockSpec(memory_space=pltpu.SEMAPHORE)
_TOKEN_SPEC = pl.BlockSpec(memory_space=pltpu.VMEM)
_DATAFLOW = pltpu.SideEffectType.DATAFLOW_SIDE_EFFECTING


def _split_start(name, body, src, land_shape):
    def full_body(src_ref, land_ref, send_sem, recv_sem, src_thru, land_thru, token):
        body(src_ref, land_ref, send_sem, recv_sem)
        token[...] = jnp.zeros_like(token)

    land = pltpu.with_memory_space_constraint(lax.empty(land_shape, src.dtype), pltpu.HBM)
    return pl.pallas_call(
        full_body, name=name,
        out_shape=(pltpu.SemaphoreType.DMA(()), pltpu.SemaphoreType.DMA(()),
                   pltpu.HBM(src.shape, src.dtype), pltpu.HBM(land_shape, src.dtype),
                   jax.ShapeDtypeStruct((8, LANES), F32)),
        in_specs=(_HBM_SPEC, _HBM_SPEC),
        out_specs=(_SEM_SPEC, _SEM_SPEC, _HBM_SPEC, _HBM_SPEC, _TOKEN_SPEC),
        input_output_aliases={0: 2, 1: 3},
        compiler_params=pltpu.CompilerParams(has_side_effects=_DATAFLOW),
    )(pltpu.with_memory_space_constraint(src, pltpu.HBM), land)


def _split_wait(name, started, n_blocks, after):
    send_sem, recv_sem, src_thru, land_thru, _ = started

    def body(src_ref, land_ref, send_sem, recv_sem, after_ref, src_dead, got_ref):
        x, y, c = _position()
        blocks = land_ref.at[pl.ds(0, n_blocks)]
        copy = pltpu.make_async_remote_copy(src_ref=blocks, dst_ref=blocks, send_sem=send_sem, recv_sem=recv_sem,
                                            device_id=(x, y, c), device_id_type=MESH)
        copy.wait_send()
        copy.wait_recv()

    return pl.pallas_call(
        body, name=name,
        out_shape=(pltpu.HBM(src_thru.shape, src_thru.dtype), pltpu.HBM(land_thru.shape, land_thru.dtype)),
        in_specs=(_HBM_SPEC, _HBM_SPEC, _SEM_SPEC, _SEM_SPEC, pl.BlockSpec(memory_space=pl.ANY)),
        out_specs=(_HBM_SPEC, _HBM_SPEC),
        input_output_aliases={0: 0, 1: 1},
        compiler_params=pltpu.CompilerParams(has_side_effects=_DATAFLOW),
    )(src_thru, land_thru, send_sem, recv_sem, after)


def all_gather_start(block, name):
    def body(b_ref, land_ref, send_sem, recv_sem):
        x, y, c = _position()
        for peer in [(x, y, 1 - c), (1 - x, y, c), (x, 1 - y, c), (1 - x, 1 - y, c)]:
            pltpu.make_async_remote_copy(src_ref=b_ref, dst_ref=land_ref.at[4 * x + 2 * y + c],
                                         send_sem=send_sem, recv_sem=recv_sem,
                                         device_id=peer, device_id_type=MESH).start()

    return _split_start(name, body, block, (N_DEV,) + block.shape)


def all_gather_finish(block, land, name):
    R, C = block.shape

    def body(b_ref, land_in, land_ref, stage, load_sems, send_sems, recv_sems, own_sem):
        x, y, c = _position()
        sibling = (x, y, 1 - c)
        chips = [(1 - x, y), (x, 1 - y), (1 - x, 1 - y)]
        own_in = pltpu.make_async_copy(b_ref, stage.at[3], load_sems.at[3])
        own_in.start()
        loads = [pltpu.make_async_copy(land_in.at[4 * px + 2 * py + c], stage.at[j], load_sems.at[j])
                 for j, (px, py) in enumerate(chips)]
        for ld in loads:
            ld.start()
        sends = []
        for j, (px, py) in enumerate(chips):
            loads[j].wait()
            dst = land_ref.at[4 * px + 2 * py + c]
            cp = pltpu.make_async_remote_copy(src_ref=stage.at[j], dst_ref=dst, send_sem=send_sems.at[j],
                                              recv_sem=recv_sems.at[j], device_id=sibling, device_id_type=MESH)
            cp.start()
            sends.append(cp)
        own_in.wait()
        own_out = pltpu.make_async_copy(stage.at[3], land_ref.at[4 * x + 2 * y + c], own_sem)
        own_out.start()
        for j, (px, py) in enumerate(chips):
            dst = land_ref.at[4 * px + 2 * py + (1 - c)]
            pltpu.make_async_remote_copy(src_ref=stage.at[j], dst_ref=dst, send_sem=send_sems.at[j],
                                         recv_sem=recv_sems.at[j], device_id=sibling,
                                         device_id_type=MESH).wait_recv()
        for cp in sends:
            cp.wait_send()
        own_out.wait()

    return pl.pallas_call(
        body,
        out_shape=jax.ShapeDtypeStruct(land.shape, land.dtype),
        in_specs=_hbm_specs(2),
        out_specs=pl.BlockSpec(memory_space=pl.ANY),
        scratch_shapes=[pltpu.VMEM((4, R, C), block.dtype), pltpu.SemaphoreType.DMA((4,)),
                        pltpu.SemaphoreType.DMA((3,)), pltpu.SemaphoreType.DMA((3,)), pltpu.SemaphoreType.DMA],
        input_output_aliases={1: 0},
        name=name,
        compiler_params=pltpu.CompilerParams(vmem_limit_bytes=VMEM_LIMIT),
    )(block, land)


def reduce_scatter_start(parts, name):
    def body(p_ref, land_ref, send_sem, recv_sem):
        x, y, c = _position()
        for px, py in [(1 - x, y), (x, 1 - y), (1 - x, 1 - y)]:
            pltpu.make_async_remote_copy(src_ref=p_ref.at[2 * px + py], dst_ref=land_ref.at[2 * x + y],
                                         send_sem=send_sem, recv_sem=recv_sem,
                                         device_id=(px, py, c), device_id_type=MESH).start()

    return _split_start(name, body, parts, parts.shape)


def small_all_gather(small):
    def body(small_ref, smalls, s_send, s_recv, s_local):
        x, y, c = _position()
        me = 4 * x + 2 * y + c
        lc = pltpu.make_async_copy(small_ref, smalls.at[me], s_local)
        lc.start()
        remote = []
        k = 0
        for dx in (0, 1):
            for dy in (0, 1):
                for dc in (0, 1):
                    if dx + dy + dc == 0:
                        continue
                    peer = (1 - x if dx else x, 1 - y if dy else y, 1 - c if dc else c)
                    rc = pltpu.make_async_remote_copy(
                        src_ref=small_ref, dst_ref=smalls.at[me],
                        send_sem=s_send.at[k], recv_sem=s_recv.at[k],
                        device_id=peer, device_id_type=MESH)
                    rc.start()
                    remote.append(rc)
                    k += 1
        for rc in remote:
            rc.wait()
        lc.wait()

    return pl.pallas_call(
        body,
        out_shape=jax.ShapeDtypeStruct((N_DEV,) + small.shape, small.dtype),
        in_specs=_hbm_specs(1),
        out_specs=pl.BlockSpec(memory_space=pl.ANY),
        scratch_shapes=[pltpu.SemaphoreType.DMA((7,)), pltpu.SemaphoreType.DMA((7,)), pltpu.SemaphoreType.DMA],
        name="small_all_gather",
    )(small)


def pair_add(grads, theirs, core, name):
    _, R, C = theirs.shape
    tr = R // 2

    def body(c_ref, a_ref, b_ref, o_ref):
        o_ref[...] = (a_ref[...].astype(F32) + b_ref[...].astype(F32)).astype(BF16)

    return pl.pallas_call(
        body,
        out_shape=jax.ShapeDtypeStruct(theirs.shape, BF16),
        grid_spec=pltpu.PrefetchScalarGridSpec(
            num_scalar_prefetch=1, grid=(4, R // tr),
            in_specs=[pl.BlockSpec((None, tr, C), lambda q, i, c: (2 * q + c[0], i, 0)),
                      pl.BlockSpec((None, tr, C), lambda q, i, c: (q, i, 0))],
            out_specs=pl.BlockSpec((None, tr, C), lambda q, i, c: (q, i, 0))),
        name=name,
        compiler_params=pltpu.CompilerParams(dimension_semantics=("parallel", "parallel"),
                                             vmem_limit_bytes=VMEM_LIMIT),
    )(core, grads, theirs)


def sum_slots(recv, off, rows, blk, name):
    nq, _, C = recv.shape
    ob = off // blk

    def body(r_ref, o_ref):
        acc = r_ref[0].astype(F32)
        for q in range(1, nq):
            acc = acc + r_ref[q].astype(F32)
        o_ref[...] = acc

    return _call(name, body, (rows // blk,),
                 [(recv, (nq, blk, C), lambda i: (0, ob + i, 0))],
                 [((rows, C), F32, (blk, C), lambda i: (i, 0))], sem=("parallel",))[0]


def sum_chips(parts, land, chip, off, rows, blk, name):
    C = parts.shape[2]
    ob = off // blk

    def body(c_ref, own_ref, a_ref, b_ref, d_ref, o_ref):
        o_ref[...] = ((own_ref[...].astype(F32) + a_ref[...].astype(F32)) + b_ref[...].astype(F32)) \
            + d_ref[...].astype(F32)

    def entry(flip):
        return pl.BlockSpec((None, blk, C), lambda i, c: (c[0] ^ flip, ob + i, 0))

    return pl.pallas_call(
        body,
        out_shape=jax.ShapeDtypeStruct((rows, C), F32),
        grid_spec=pltpu.PrefetchScalarGridSpec(
            num_scalar_prefetch=1, grid=(rows // blk,),
            in_specs=[entry(0), entry(1), entry(2), entry(3)],
            out_specs=pl.BlockSpec((blk, C), lambda i, c: (i, 0))),
        name=name,
        compiler_params=pltpu.CompilerParams(dimension_semantics=("parallel",), vmem_limit_bytes=VMEM_LIMIT),
    )(chip, parts, land, land, land)


def adamw(w, g, m, v, name):
    R, C = w.shape
    tr = R
    for cand in (256, 128, 64, 32, 16, 8):
        if R % cand == 0 and R > cand:
            tr = cand
            break
    c1 = 1.0 / (1.0 - ADAM_B1 ** ADAM_STEP)
    c2 = 1.0 / (1.0 - ADAM_B2 ** ADAM_STEP)

    def body(w_ref, g_ref, m_ref, v_ref, d_ref, nm_ref, nv_ref):
        gv = g_ref[...]
        nm = ADAM_B1 * m_ref[...] + (1.0 - ADAM_B1) * gv
        nv = ADAM_B2 * v_ref[...] + (1.0 - ADAM_B2) * (gv * gv)
        d_ref[...] = -ADAM_LR * ((nm * c1) / (jnp.sqrt(nv * c2) + ADAM_EPS) + ADAM_WD * w_ref[...])
        nm_ref[...] = nm
        nv_ref[...] = nv

    spec = ((tr, C), lambda i: (i, 0))
    out = ((R, C), F32) + spec
    return _call(name, body, (R // tr,), [(w,) + spec, (g,) + spec, (m,) + spec, (v,) + spec],
                 [out, out, out], sem=("parallel",))


def rms_fwd(x, g, name):
    S, D = x.shape
    tr = 512

    def body(x_ref, g_ref, o_ref):
        xv = x_ref[...]
        r = lax.rsqrt(jnp.mean(xv * xv, axis=-1, keepdims=True) + EPS)
        o_ref[...] = (xv * r * g_ref[...]).astype(BF16)

    return _call(name, body, (S // tr,),
                 [(x, (tr, D), lambda i: (i, 0)), (g, (1, D), lambda i: (0, 0))],
                 [((S, D), BF16, (tr, D), lambda i: (i, 0))], sem=("parallel",))[0]


def _rms_bwd_tile(dn, xv, gv):
    r = lax.rsqrt(jnp.mean(xv * xv, axis=-1, keepdims=True) + EPS)
    xh = xv * r
    dxh = dn * gv
    dx = r * (dxh - xh * jnp.mean(dxh * xh, axis=-1, keepdims=True))
    return dx, dn * xh


def final_loss(x, tgt, g, name):
    S, D = x.shape
    tr = 256

    def body(x_ref, t_ref, g_ref, l_ref, dx_ref, dg_ref):
        i = pl.program_id(0)
        xv, gv = x_ref[...], g_ref[...]
        r = lax.rsqrt(jnp.mean(xv * xv, axis=-1, keepdims=True) + EPS)
        xh = xv * r
        e = xh * gv - t_ref[...]
        part = 0.5 * jnp.sum(jnp.sum(e * e, axis=-1, keepdims=True) * (1.0 / D), axis=0, keepdims=True)
        dy = e * (1.0 / D)
        dxh = dy * gv
        dx_ref[...] = r * (dxh - xh * jnp.mean(dxh * xh, axis=-1, keepdims=True))
        dgp = jnp.sum(dy * xh, axis=0, keepdims=True)

        @pl.when(i == 0)
        def _():
            l_ref[...] = jnp.broadcast_to(part, l_ref.shape)
            dg_ref[...] = dgp

        @pl.when(i > 0)
        def _():
            l_ref[...] += jnp.broadcast_to(part, l_ref.shape)
            dg_ref[...] += dgp

    row = ((tr, D), lambda i: (i, 0))
    return _call(name, body, (S // tr,),
                 [(x,) + row, (tgt,) + row, (g, (1, D), lambda i: (0, 0))],
                 [((1, LANES), F32, (1, LANES), lambda i: (0, 0)), ((S, D), F32) + row,
                  ((1, D), F32, (1, D), lambda i: (0, 0))], sem=("arbitrary",))


FFN_TF = 4 * FFN_SHARD


def _ffn_w_spec(G, which, imap):
    D = G.shape[2]
    return (G, (4, FFN_SHARD, D), lambda *idx: (imap(*idx), which, 0))


def _ffn_whole_w_spec(G, which):
    D = G.shape[2]
    return (G, (N_DEV, FFN_SHARD, D), lambda *idx: (0, which, 0))


def ffn_up(n, G, name):
    S, D = n.shape
    F = N_DEV * FFN_SHARD
    tm = 1024

    def body(n_ref, w1_ref, w3_ref, abh_ref):
        nv = n_ref[...]
        a = _dot(nv, w1_ref[...].reshape(FFN_TF, D), 1, 1).astype(BF16)
        b = _dot(nv, w3_ref[...].reshape(FFN_TF, D), 1, 1).astype(BF16)
        abh_ref[0] = a
        abh_ref[1] = b
        av, bv = a.astype(F32), b.astype(F32)
        abh_ref[2] = (av * _sigmoid(av) * bv).astype(BF16)

    return _call(name, body, (F // FFN_TF, S // tm),
                 [(n, (tm, D), lambda j, i: (i, 0)),
                  _ffn_w_spec(G, 0, lambda j, i: j), _ffn_w_spec(G, 1, lambda j, i: j)],
                 [((3, S, F), BF16, (3, tm, FFN_TF), lambda j, i: (0, i, j))],
                 sem=("parallel", "parallel"))[0]


def ffn_down(abh, G, x, name):
    _, S, F = abh.shape
    D = x.shape[1]
    tm = 512

    def body(h_ref, w2_ref, x_ref, o_ref):
        o_ref[...] = x_ref[...] + 0.5 * _dot(h_ref[...], w2_ref[...].reshape(F, D))

    return _call(name, body, (S // tm,),
                 [(abh, (None, tm, F), lambda i: (2, i, 0)), _ffn_whole_w_spec(G, 2),
                  (x, (tm, D), lambda i: (i, 0))],
                 [((S, D), F32, (tm, D), lambda i: (i, 0))], sem=("parallel",))[0]


def ffn_bwd_weights(dxo, abh, n, G, name):
    _, S, F = abh.shape
    D = dxo.shape[1]
    tm = 512
    nf = F // FFN_TF

    def down_body(d_ref, w2_ref, ab_ref, o_ref):
        dh = 0.5 * _dot(d_ref[...].astype(BF16), w2_ref[...].reshape(FFN_TF, D), 1, 1)
        av, bv = ab_ref[0].astype(F32), ab_ref[1].astype(F32)
        sig = _sigmoid(av)
        o_ref[0] = (dh * bv * (sig * (1.0 + av * (1.0 - sig)))).astype(BF16)
        o_ref[1] = (dh * (av * sig)).astype(BF16)

    dab = _call(name + "_down_bwd", down_body, (nf, S // tm),
                [(dxo, (tm, D), lambda j, i: (i, 0)), _ffn_w_spec(G, 2, lambda j, i: j),
                 (abh, (2, tm, FFN_TF), lambda j, i: (0, i, j))],
                [((2, S, F), BF16, (2, tm, FFN_TF), lambda j, i: (0, i, j))],
                sem=("parallel", "parallel"))[0]

    tk = WGRAD_TK
    nk = S // tk
    gshape = (N_DEV, 3 * FFN_SHARD, D)

    def dw2_body(h_ref, d_ref, o_ref, acc_ref):
        k = pl.program_id(1)
        p = _dot(h_ref[...], d_ref[...].astype(BF16), 0, 0)

        @pl.when(k == 0)
        def _():
            acc_ref[...] = p

        @pl.when(k > 0)
        def _():
            acc_ref[...] += p

        @pl.when(k == nk - 1)
        def _():
            o_ref[...] = (0.5 * acc_ref[...]).astype(BF16).reshape(4, FFN_SHARD, D)

    gw = _call(name + "_dw2", dw2_body, (nf, nk),
               [(abh, (None, tk, FFN_TF), lambda j, k: (2, k, j)), (dxo, (tk, D), lambda j, k: (k, 0))],
               [(gshape, BF16, (4, FFN_SHARD, D), lambda j, k: (j, 2, 0))],
               scratch=[pltpu.VMEM((FFN_TF, D), F32)], sem=("parallel", "arbitrary"))[0]

    def dw13_body(gw_ref, dab_ref, n_ref, o_ref, acc_ref):
        k = pl.program_id(2)
        p = _dot(dab_ref[...], n_ref[...], 0, 0)

        @pl.when(k == 0)
        def _():
            acc_ref[...] = p

        @pl.when(k > 0)
        def _():
            acc_ref[...] += p

        @pl.when(k == nk - 1)
        def _():
            o_ref[...] = acc_ref[...].astype(BF16).reshape(4, FFN_SHARD, D)

    gw = pl.pallas_call(
        dw13_body,
        out_shape=jax.ShapeDtypeStruct(gshape, BF16),
        grid=(2, nf, nk),
        in_specs=[pl.BlockSpec(memory_space=pl.ANY),
                  pl.BlockSpec((None, tk, FFN_TF), lambda w, j, k: (w, k, j)),
                  pl.BlockSpec((tk, D), lambda w, j, k: (k, 0))],
        out_specs=pl.BlockSpec((4, FFN_SHARD, D), lambda w, j, k: (j, w, 0)),
        scratch_shapes=[pltpu.VMEM((FFN_TF, D), F32)],
        input_output_aliases={0: 0},
        name=name + "_dw13",
        compiler_params=pltpu.CompilerParams(dimension_semantics=("parallel", "parallel", "arbitrary"),
                                             vmem_limit_bytes=VMEM_LIMIT),
    )(gw, dab, n)
    return dab, gw


def ffn_bwd_input(dab, G, x_in, g, dxo, name):
    _, S, F = dab.shape
    D = x_in.shape[1]
    tm = 256

    def dn_body(dab_ref, w1_ref, w3_ref, x_ref, d_ref, g_ref, dx_ref, dg_ref):
        i = pl.program_id(0)
        dn = _dot(dab_ref[0], w1_ref[...].reshape(F, D)) + _dot(dab_ref[1], w3_ref[...].reshape(F, D))
        dx, dgt = _rms_bwd_tile(dn, x_ref[...], g_ref[...])
        dx_ref[...] = d_ref[...] + dx
        dgp = jnp.sum(dgt, axis=0, keepdims=True)

        @pl.when(i == 0)
        def _():
            dg_ref[...] = dgp

        @pl.when(i > 0)
        def _():
            dg_ref[...] += dgp

    dx, dg = _call(name + "_dn", dn_body, (S // tm,),
                   [(dab, (2, tm, F), lambda i: (0, i, 0)),
                    _ffn_whole_w_spec(G, 0), _ffn_whole_w_spec(G, 1),
                    (x_in, (tm, D), lambda i: (i, 0)), (dxo, (tm, D), lambda i: (i, 0)),
                    (g, (1, D), lambda i: (0, 0))],
                   [((S, D), F32, (tm, D), lambda i: (i, 0)), ((1, D), F32, (1, D), lambda i: (0, 0))],
                   sem=("arbitrary",))
    return dx, dg


PROJ_TN = 512


def in_proj(h, Gm, first_tile, n_tiles, dtype, name, tile_stride=1):
    S, D = h.shape
    tm = 1024
    tile = lambda j: first_tile + tile_stride * j

    def body(h_ref, w_ref, o_ref):
        o_ref[...] = _dot(h_ref[...], w_ref[...]).astype(dtype)

    return _call(name, body, (n_tiles, S // tm),
                 [(h, (tm, D), lambda j, i: (i, 0)),
                  (Gm, (None, D, PROJ_TN), lambda j, i: (tile(j) // 2, 0, tile(j) % 2))],
                 [((S, n_tiles * PROJ_TN), dtype, (tm, PROJ_TN), lambda j, i: (i, j))],
                 sem=("parallel", "parallel"))[0]


def in_proj_bwd_dw(dproj, h, gm_grads, name):
    S, D = h.shape
    NT = dproj.shape[1] // PROJ_TN
    tk = WGRAD_TK
    nk = S // tk

    def dw_body(gm_ref, h_ref, d_ref, o_ref, acc_ref):
        k = pl.program_id(1)
        p = _dot(h_ref[...], d_ref[...], 0, 0)

        @pl.when(k == 0)
        def _():
            acc_ref[...] = p

        @pl.when(k > 0)
        def _():
            acc_ref[...] += p

        @pl.when(k == nk - 1)
        def _():
            o_ref[...] = acc_ref[...].astype(BF16)

    return pl.pallas_call(
        dw_body,
        out_shape=jax.ShapeDtypeStruct(gm_grads.shape, BF16),
        grid=(NT, nk),
        in_specs=[pl.BlockSpec(memory_space=pl.ANY),
                  pl.BlockSpec((tk, D), lambda j, k: (k, 0)),
                  pl.BlockSpec((tk, PROJ_TN), lambda j, k: (k, j))],
        out_specs=pl.BlockSpec((None, D, PROJ_TN), lambda j, k: (j // 2, 0, j % 2)),
        scratch_shapes=[pltpu.VMEM((D, PROJ_TN), F32)],
        input_output_aliases={0: 0},
        name=name + "_dw",
        compiler_params=pltpu.CompilerParams(dimension_semantics=("parallel", "arbitrary"),
                                             vmem_limit_bytes=VMEM_LIMIT),
    )(gm_grads, h, dproj)


def in_proj_bwd_dh(dproj, Gm, x_in, g, dres, name):
    S, D = x_in.shape
    tm = 512
    C = Gm.shape[2]
    n_sh = dproj.shape[1] // C

    def dh_body(d_ref, w_ref, x_ref, r_ref, g_ref, dx_ref, dg_ref, acc_ref):
        i, k = pl.program_id(0), pl.program_id(1)
        p = _dot(d_ref[...], w_ref[...], 1, 1)

        @pl.when(k == 0)
        def _():
            acc_ref[...] = p

        @pl.when(k > 0)
        def _():
            acc_ref[...] += p

        @pl.when(k == n_sh - 1)
        def _():
            dx, dgt = _rms_bwd_tile(acc_ref[...], x_ref[...], g_ref[...])
            dx_ref[...] = r_ref[...] + dx
            dgp = jnp.sum(dgt, axis=0, keepdims=True)

            @pl.when(i == 0)
            def _():
                dg_ref[...] = dgp

            @pl.when(i > 0)
            def _():
                dg_ref[...] += dgp

    dx, dg = _call(name + "_dh", dh_body, (S // tm, n_sh),
                   [(dproj, (tm, C), lambda i, k: (i, k)),
                    (Gm, (None, D, C), lambda i, k: (k, 0, 0)),
                    (x_in, (tm, D), lambda i, k: (i, 0)), (dres, (tm, D), lambda i, k: (i, 0)),
                    (g, (1, D), lambda i, k: (0, 0))],
                   [((S, D), F32, (tm, D), lambda i, k: (i, 0)), ((1, D), F32, (1, D), lambda i, k: (0, 0))],
                   scratch=[pltpu.VMEM((tm, D), F32)], sem=("arbitrary", "arbitrary"))
    return dx, dg


def _t5_bucket(rel):
    n = N_BUCKETS // 2
    max_exact = n // 2
    ret = jnp.where(rel > 0, n, 0)
    a = jnp.abs(rel)
    af = jnp.maximum(a, 1).astype(F32)
    large = max_exact + (jnp.log(af / max_exact) / math.log(MAX_DISTANCE / max_exact)
                         * (n - max_exact)).astype(jnp.int32)
    large = jnp.minimum(large, n - 1)
    return ret + jnp.where(a < max_exact, a, large)


def _bucket_tables():
    qi = jnp.arange(A_TQ, dtype=jnp.int32)[:, None]
    kj = jnp.arange(A_WIN, dtype=jnp.int32)[None, :]
    rel = kj - HALF_WINDOW - qi
    return jnp.stack([_t5_bucket(rel * d) for d in DILATIONS], axis=0)


def bias_build(rel_bias, buckets):
    def body(tab_ref, bk_ref, o_ref):
        col = pl.program_id(0) * HEADS_PER_GROUP_A + pl.program_id(1)
        bk = bk_ref[...]
        acc = jnp.zeros(bk.shape, F32)
        for b in range(N_BUCKETS):
            acc = jnp.where(bk == b, tab_ref[b, col], acc)
        qi = lax.broadcasted_iota(jnp.int32, bk.shape, 0)
        kj = lax.broadcasted_iota(jnp.int32, bk.shape, 1)
        o_ref[...] = jnp.where(jnp.abs(kj - HALF_WINDOW - qi) <= HALF_WINDOW, acc, NEG_INF)

    return pl.pallas_call(
        body,
        out_shape=jax.ShapeDtypeStruct((3, HEADS_PER_GROUP_A, A_TQ, A_WIN), F32),
        grid=(3, HEADS_PER_GROUP_A),
        in_specs=[pl.BlockSpec(memory_space=pltpu.SMEM),
                  pl.BlockSpec((None, A_TQ, A_WIN), lambda g, h: (g, 0, 0))],
        out_specs=pl.BlockSpec((None, None, A_TQ, A_WIN), lambda g, h: (g, h, 0, 0)),
        name="a_bias_build",
        compiler_params=pltpu.CompilerParams(dimension_semantics=("parallel", "parallel")),
    )(rel_bias, buckets)


def bias_bwd(dbias, buckets):
    def body(d_ref, bk_ref, o_ref):
        bk = bk_ref[...]
        dv = d_ref[...]
        for b in range(N_BUCKETS):
            part = jnp.sum(jnp.where(bk == b, dv, 0.0), axis=1, keepdims=True)
            o_ref[b:b + 1, :] = jnp.broadcast_to(jnp.sum(part, axis=0, keepdims=True), (1, LANES))

    out = pl.pallas_call(
        body,
        out_shape=jax.ShapeDtypeStruct((3, HEADS_PER_GROUP_A, N_BUCKETS, LANES), F32),
        grid=(3, HEADS_PER_GROUP_A),
        in_specs=[pl.BlockSpec((None, None, A_TQ, A_WIN), lambda g, h: (g, h, 0, 0)),
                  pl.BlockSpec((None, A_TQ, A_WIN), lambda g, h: (g, 0, 0))],
        out_specs=pl.BlockSpec((None, None, N_BUCKETS, LANES), lambda g, h: (g, h, 0, 0)),
        name="a_bias_bwd",
        compiler_params=pltpu.CompilerParams(dimension_semantics=("parallel", "parallel")),
    )(dbias, buckets)
    return out[:, :, :, 0].transpose(2, 0, 1).reshape(N_BUCKETS, 3 * HEADS_PER_GROUP_A)


def _a_fill_padded(pad_ref, src_ref, n, pad):
    zeros = jnp.zeros((pad, LANES), pad_ref.dtype)
    pad_ref[0:pad, :] = zeros
    pad_ref[pad + n:2 * pad + n, :] = zeros
    pad_ref[pad:pad + n, :] = src_ref[...].astype(pad_ref.dtype)


def _a_key_valid(qb, L):
    kidx = qb * A_TQ - HALF_WINDOW + lax.broadcasted_iota(jnp.int32, (A_TQ, A_WIN), 1)
    return (kidx >= 0) & (kidx < L)


def a_fwd(proj_g, bias_g, g, name):
    S = proj_g.shape[0]
    d = DILATIONS[g]
    L = S // d
    nqb = L // A_TQ
    pad = HALF_WINDOW * d

    def body(q_ref, k_ref, v_ref, b_ref, o_ref, l_ref, qf, kpad, vpad):
        qf[...] = q_ref[...].astype(F32)
        _a_fill_padded(kpad, k_ref, S, pad)
        _a_fill_padded(vpad, v_ref, S, pad)
        lane = lax.broadcasted_iota(jnp.int32, (A_TQ, LANES), 1)

        def block(t, carry):
            qb, r = t // d, t % d
            start = qb * (A_TQ * d) + r
            kw = kpad[pl.ds(start, A_WIN, stride=d), :].astype(BF16)
            vw = vpad[pl.ds(start, A_WIN, stride=d), :].astype(BF16)
            q = qf[pl.ds(start, A_TQ, stride=d), :].astype(BF16)
            valid = _a_key_valid(qb, L)
            outs, lses = [], []
            for h in range(2):
                qh = jnp.where((lane >= HEAD_DIM_A * h) & (lane < HEAD_DIM_A * (h + 1)), q, jnp.zeros_like(q))
                s = _dot(qh, kw, 1, 1) * (HEAD_DIM_A ** -0.5) + b_ref[h]
                s = jnp.where(valid, s, NEG_INF)
                m = jnp.max(s, axis=-1, keepdims=True)
                e = jnp.exp(s - m)
                l = jnp.sum(e, axis=-1, keepdims=True)
                outs.append(_dot(e.astype(BF16), vw) / l)
                lses.append(m + jnp.log(l))
            o_ref[pl.ds(start, A_TQ, stride=d), :] = jnp.where(lane < HEAD_DIM_A, outs[0], outs[1])
            l_ref[pl.ds(start, A_TQ, stride=d), :] = jnp.where(lane < HEAD_DIM_A, lses[0], lses[1])
            return carry

        lax.fori_loop(0, nqb * d, block, 0, unroll=A_UNROLL)

    out_spec = ((S, GROUP_WIDTH_A), F32, (S, LANES), lambda hp: (0, hp))
    return _call(name, body, (4,),
                 [(proj_g, (S, LANES), lambda hp: (0, hp)),
                  (proj_g, (S, LANES), lambda hp: (0, 4 + hp)),
                  (proj_g, (S, LANES), lambda hp: (0, 8 + hp)),
                  (bias_g, (2, A_TQ, A_WIN), lambda hp: (hp, 0, 0))],
                 [out_spec, out_spec],
                 scratch=[pltpu.VMEM((S, LANES), F32)] + [pltpu.VMEM((S + 2 * pad, LANES), F32)] * 2,
                 sem=("parallel",))


def a_combine(outs, lses, name):
    S, W = outs[0].shape
    tr = 512

    def body(o0, o1, o2, l0, l1, l2, oa_ref, lt_ref):
        a, b, c = l0[...], l1[...], l2[...]
        m = jnp.maximum(jnp.maximum(a, b), c)
        ea, eb, ec = jnp.exp(a - m), jnp.exp(b - m), jnp.exp(c - m)
        z = ea + eb + ec
        oa_ref[...] = ((ea * o0[...] + eb * o1[...] + ec * o2[...]) / z).astype(BF16)
        lt_ref[...] = m + jnp.log(z)

    spec = ((tr, W), lambda i: (i, 0))
    return _call(name, body, (S // tr,), [(a,) + spec for a in (*outs, *lses)],
                 [((S, W), BF16) + spec, ((S, W), F32) + spec], sem=("parallel",))


def a_bwd(proj_g, bias_g, do_a, o_a, lse_tot, g, name):
    S = proj_g.shape[0]
    d = DILATIONS[g]
    L = S // d
    nqb = L // A_TQ
    pad = HALF_WINDOW * d
    scale = HEAD_DIM_A ** -0.5

    def body(q_ref, k_ref, v_ref, b_ref, do_ref, o_ref, l_ref, dq_ref, dk_ref, dv_ref, db_ref,
             qf, of, dqf, kpad, vpad, dkacc, dvacc):
        qf[...] = q_ref[...].astype(F32)
        of[...] = o_ref[...].astype(F32)
        _a_fill_padded(kpad, k_ref, S, pad)
        _a_fill_padded(vpad, v_ref, S, pad)
        dkacc[...] = jnp.zeros(dkacc.shape, F32)
        dvacc[...] = jnp.zeros(dvacc.shape, F32)
        db_ref[...] = jnp.zeros(db_ref.shape, F32)
        lane = lax.broadcasted_iota(jnp.int32, (A_TQ, LANES), 1)

        def block(t, carry):
            qb, r = t // d, t % d
            start = qb * (A_TQ * d) + r
            rows = pl.ds(start, A_TQ, stride=d)
            win = pl.ds(start, A_WIN, stride=d)
            kw = kpad[win, :].astype(BF16)
            vw = vpad[win, :].astype(BF16)
            q = qf[rows, :].astype(BF16)
            do = do_ref[rows, :]
            ov = of[rows, :]
            lt = l_ref[rows, :]
            valid = _a_key_valid(qb, L)
            dqs = []
            dk_win = jnp.zeros((A_WIN, LANES), F32)
            dv_win = jnp.zeros((A_WIN, LANES), F32)
            for h in range(2):
                mh = (lane >= HEAD_DIM_A * h) & (lane < HEAD_DIM_A * (h + 1))
                qh = jnp.where(mh, q, jnp.zeros_like(q))
                doh = jnp.where(mh, do, 0.0)
                s = _dot(qh, kw, 1, 1) * scale + b_ref[h]
                s = jnp.where(valid, s, NEG_INF)
                p = jnp.exp(s - lt[:, HEAD_DIM_A * h:HEAD_DIM_A * h + 1])
                t = jnp.sum(doh * ov, axis=-1, keepdims=True)
                dob = doh.astype(BF16)
                ds = p * (_dot(dob, vw, 1, 1) - t)
                db_ref[h] += ds
                dsb = (ds * scale).astype(BF16)
                dqs.append(_dot(dsb, kw))
                dk_win = dk_win + _dot(dsb, qh, 0, 0)
                dv_win = dv_win + _dot(p.astype(BF16), dob, 0, 0)
            dqf[rows, :] = jnp.where(lane < HEAD_DIM_A, dqs[0], dqs[1])
            dkacc[win, :] += dk_win
            dvacc[win, :] += dv_win
            return carry

        lax.fori_loop(0, nqb * d, block, 0, unroll=A_UNROLL)
        dq_ref[...] = dqf[...].astype(BF16)
        dk_ref[...] = dkacc[pad:pad + S, :].astype(BF16)
        dv_ref[...] = dvacc[pad:pad + S, :].astype(BF16)

    slab = ((S, LANES), lambda hp: (0, hp))
    oshape = (S, GROUP_WIDTH_A)
    padded = pltpu.VMEM((S + 2 * pad, LANES), F32)
    return _call(
        name, body, (4,),
        [(proj_g, (S, LANES), lambda hp: (0, hp)),
         (proj_g, (S, LANES), lambda hp: (0, 4 + hp)),
         (proj_g, (S, LANES), lambda hp: (0, 8 + hp)),
         (bias_g, (2, A_TQ, A_WIN), lambda hp: (hp, 0, 0)),
         (do_a,) + slab, (o_a,) + slab, (lse_tot,) + slab],
        [(oshape, BF16) + slab, (oshape, BF16) + slab, (oshape, BF16) + slab,
         ((HEADS_PER_GROUP_A, A_TQ, A_WIN), F32, (2, A_TQ, A_WIN), lambda hp: (hp, 0, 0))],
        scratch=[pltpu.VMEM((S, LANES), F32)] * 3 + [padded] * 4,
        sem=("parallel",))


def _rope_tables(S):
    rows = S // GRID_W
    row = jnp.repeat(jnp.arange(rows, dtype=F32), GRID_W)
    col = jnp.tile(jnp.arange(GRID_W, dtype=F32), rows)
    n_freq = HEAD_DIM_B // 4
    freq = ROPE_THETA ** (-jnp.arange(n_freq, dtype=F32) / n_freq)
    ang = jnp.concatenate([row[:, None] * freq, col[:, None] * freq], axis=-1)
    cos, sin = jnp.cos(ang), jnp.sin(ang)
    return jnp.repeat(cos, 2, axis=-1), jnp.stack([-sin, sin], axis=-1).reshape(S, HEAD_DIM_B)


def _swap_pairs(y):
    lane = lax.broadcasted_iota(jnp.int32, y.shape, 1)
    return jnp.where(lane % 2 == 0, pltpu.roll(y, LANES - 1, 1), pltpu.roll(y, 1, 1))


def qkv_prep(proj_b, gains, cos_t, sin_t, name):
    S = proj_b.shape[0]
    ts = 256
    n_rot = N_HEADS_B + N_KV_B
    nh = n_rot + N_KV_B
    W = nh * LANES

    def body(x_ref, g_ref, c_ref, s_ref, o_ref):
        cv, sv = c_ref[...], s_ref[...]
        for hb in range(nh):
            cols = slice(hb * LANES, (hb + 1) * LANES)
            xv = x_ref[:, cols]
            if hb < n_rot:
                r = lax.rsqrt(jnp.mean(xv * xv, axis=-1, keepdims=True) + EPS)
                yv = xv * r * g_ref[:, cols]
                o_ref[:, cols] = (yv * cv + _swap_pairs(yv) * sv).astype(BF16)
            else:
                o_ref[:, cols] = xv.astype(BF16)

    return _call(name, body, (S // ts,),
                 [(proj_b, (ts, W), lambda i: (i, 0)), (gains, (1, W), lambda i: (0, 0)),
                  (cos_t, (ts, LANES), lambda i: (i, 0)), (sin_t, (ts, LANES), lambda i: (i, 0))],
                 [((S, W), BF16, (ts, W), lambda i: (i, 0))],
                 sem=("parallel",))[0]


def qk_prep_bwd(dr, proj_b, col0, gain, cos_t, sin_t, name):
    S, W = dr.shape
    H = W // LANES
    ts = 256
    xb = (col0 * LANES) // W

    def body(d_ref, x_ref, g_ref, c_ref, s_ref, dx_ref, dg_ref):
        i = pl.program_id(0)
        cv, sv, gv = c_ref[...], s_ref[...], g_ref[...]
        dgp = jnp.zeros((1, LANES), F32)
        for hb in range(H):
            cols = slice(hb * LANES, (hb + 1) * LANES)
            dout = d_ref[:, cols]
            dy = dout * cv + _swap_pairs(dout * sv)
            dx, dgt = _rms_bwd_tile(dy, x_ref[:, cols], gv)
            dx_ref[:, cols] = dx.astype(BF16)
            dgp = dgp + jnp.sum(dgt, axis=0, keepdims=True)

        @pl.when(i == 0)
        def _():
            dg_ref[...] = dgp

        @pl.when(i > 0)
        def _():
            dg_ref[...] += dgp

    return _call(name, body, (S // ts,),
                 [(dr, (ts, W), lambda i: (i, 0)), (proj_b, (ts, W), lambda i: (i, xb)),
                  (gain, (1, LANES), lambda i: (0, 0)),
                  (cos_t, (ts, LANES), lambda i: (i, 0)), (sin_t, (ts, LANES), lambda i: (i, 0))],
                 [((S, W), BF16, (ts, W), lambda i: (i, 0)),
                  ((1, LANES), F32, (1, LANES), lambda i: (0, 0))],
                 sem=("arbitrary",))


def _row_sums(x):
    hi = x.astype(BF16)
    lo = (x - hi.astype(F32)).astype(BF16)
    ones = jnp.ones((8, LANES), BF16)
    return (_dot(ones, hi, 1, 1) + _dot(ones, lo, 1, 1))[0:1, :]


def flash_fwd(qkv, name):
    S = qkv.shape[0]
    tq = B_TQ_FWD
    scale = HEAD_DIM_B ** -0.5

    def body(q_ref, k_ref, v_ref, o_ref, l_ref):
        s = _dot(q_ref[...], k_ref[...], 1, 1) * scale
        m = jnp.max(s, axis=-1, keepdims=True)
        e = jnp.exp(s - m)
        l = jnp.sum(e, axis=-1, keepdims=True)
        o_ref[...] = (_dot(e.astype(BF16), v_ref[...]) / l).astype(BF16)
        lse = jnp.broadcast_to(m + jnp.log(l), (tq, LANES))
        l_ref[...] = _row_sums(lse) * (1.0 / LANES)

    head = lambda g, h, i: (i, g * GQA_GROUP_B + h)
    return _call(name, body, (N_KV_B, GQA_GROUP_B, S // tq),
                 [(qkv, (tq, LANES), head),
                  (qkv, (S, LANES), lambda g, h, i: (0, N_HEADS_B + g)),
                  (qkv, (S, LANES), lambda g, h, i: (0, N_HEADS_B + N_KV_B + g))],
                 [((S, N_HEADS_B * LANES), BF16, (tq, LANES), head),
                  ((N_HEADS_B, 1, S), F32, (None, 1, tq), lambda g, h, i: (g * GQA_GROUP_B + h, 0, i))],
                 sem=("parallel", "parallel", "parallel"))


def flash_bwd(qkv, k_t, do_b, o_b, lse, name):
    S = qkv.shape[0]
    tq = B_TQ_BWD
    nq = S // tq
    scale = HEAD_DIM_B ** -0.5

    def body(q_ref, k_ref, v_ref, kt_ref, do_ref, o_ref, l_ref, dq_ref, dk_ref, dv_ref, dkacc, dvacc):
        h, i = pl.program_id(1), pl.program_id(2)

        @pl.when((h == 0) & (i == 0))
        def _():
            dkacc[...] = jnp.zeros(dkacc.shape, F32)
            dvacc[...] = jnp.zeros(dvacc.shape, F32)

        q = q_ref[...]
        do = do_ref[...]
        dob = do.astype(BF16)
        t = _row_sums(do * o_ref[...].astype(F32))
        pt = jnp.exp(_dot(k_ref[...], q, 1, 1) * scale - l_ref[...])
        dst = pt * (_dot(v_ref[...], dob, 1, 1) - t) * scale
        dsb = dst.astype(BF16)
        dvacc[...] += _dot(pt.astype(BF16), dob)
        dkacc[...] += _dot(dsb, q)
        dq_ref[...] = _dot(kt_ref[...], dsb).T

        @pl.when((h == GQA_GROUP_B - 1) & (i == nq - 1))
        def _():
            dk_ref[...] = dkacc[...]
            dv_ref[...] = dvacc[...].astype(BF16)

    head = lambda g, h, i: (i, g * GQA_GROUP_B + h)
    return _call(name, body, (N_KV_B, GQA_GROUP_B, nq),
                 [(qkv, (tq, LANES), head),
                  (qkv, (S, LANES), lambda g, h, i: (0, N_HEADS_B + g)),
                  (qkv, (S, LANES), lambda g, h, i: (0, N_HEADS_B + N_KV_B + g)),
                  (k_t, (LANES, S), lambda g, h, i: (g, 0)),
                  (do_b, (tq, LANES), head), (o_b, (tq, LANES), head),
                  (lse, (None, 1, tq), lambda g, h, i: (g * GQA_GROUP_B + h, 0, i))],
                 [((S, N_HEADS_B * LANES), F32, (tq, LANES), head),
                  ((S, N_KV_B * LANES), F32, (S, LANES), lambda g, h, i: (0, g)),
                  ((S, N_KV_B * LANES), BF16, (S, LANES), lambda g, h, i: (0, g))],
                 scratch=[pltpu.VMEM((S, LANES), F32)] * 2,
                 sem=("parallel", "arbitrary", "arbitrary"))


MERGE_TN = 512


def _mix_rows_spec(Gm, row0, n_slots, slot_map, cols=None, col_map=None):
    C = Gm.shape[2] if cols is None else cols
    cm = (lambda *idx: 0) if col_map is None else col_map
    return (Gm, (n_slots, LANES, C), lambda *idx: (slot_map(*idx), row0 // LANES, cm(*idx)))


def merge_fwd(o_a, o_b, w_a, Gm, proj_b, b_gate, name):
    S = o_a.shape[0]
    D = w_a.shape[1]
    tm, tn = 512, MERGE_TN
    ga0, gb0 = PB_GATE_A // tn, PB_GATE_B // tn

    def body(oa_ref, ob_ref, wa_ref, wb_ref, pa_ref, pb_ref, ba_ref, bb_ref, m_ref, ya_ref, yb_ref):
        ya = _dot(oa_ref[...], wa_ref[...])
        yb = _dot(ob_ref[...], wb_ref[...].reshape(N_DEV * LANES, tn))
        ga = _sigmoid(pa_ref[...] + ba_ref[...])
        gb = _sigmoid(pb_ref[...] + bb_ref[...])
        m_ref[...] = (ga * ya + gb * yb).astype(BF16)
        ya_ref[...] = ya.astype(BF16)
        yb_ref[...] = yb.astype(BF16)

    out = ((S, D), BF16, (tm, tn), lambda j, i: (i, j))
    return _call(name, body, (D // tn, S // tm),
                 [(o_a, (tm, o_a.shape[1]), lambda j, i: (i, 0)), (o_b, (tm, o_b.shape[1]), lambda j, i: (i, 0)),
                  (w_a, (w_a.shape[0], tn), lambda j, i: (0, j)),
                  _mix_rows_spec(Gm, MIX_WB, N_DEV, lambda j, i: 0, cols=tn, col_map=lambda j, i: j),
                  (proj_b, (tm, tn), lambda j, i: (i, ga0 + j)), (proj_b, (tm, tn), lambda j, i: (i, gb0 + j)),
                  (b_gate, (1, tn), lambda j, i: (0, j)), (b_gate, (1, tn), lambda j, i: (0, D // tn + j))],
                 [out, out, out], sem=("parallel", "parallel"))


def out_proj(merged, Gm, x, name):
    S, D = x.shape
    tm, tn = 512, MERGE_TN

    def body(m_ref, w_ref, x_ref, o_ref):
        o_ref[...] = x_ref[...] + _dot(m_ref[...], w_ref[...].reshape(N_DEV * LANES, tn))

    return _call(name, body, (D // tn, S // tm),
                 [(merged, (tm, D), lambda j, i: (i, 0)),
                  _mix_rows_spec(Gm, MIX_WOUT, N_DEV, lambda j, i: 0, cols=tn, col_map=lambda j, i: j),
                  (x, (tm, tn), lambda j, i: (i, j))],
                 [((S, D), F32, (tm, tn), lambda j, i: (i, j))], sem=("parallel", "parallel"))[0]


def merge_bwd(dx2, Gm, ya, yb, proj_b, b_gate, name):
    S, D = dx2.shape
    tm, tn = 512, MERGE_TN
    nn = D // tn
    ga0, gb0 = PB_GATE_A // tn, PB_GATE_B // tn

    def body(d_ref, w_ref, ya_ref, yb_ref, pa_ref, pb_ref, ba_ref, bb_ref, dya_ref, dyb_ref, dg_ref, dbg_ref):
        i = pl.program_id(1)
        dm = _dot(d_ref[...].astype(BF16), w_ref[...].reshape(tn, D), 1, 1)
        ga = _sigmoid(pa_ref[...] + ba_ref[...])
        gb = _sigmoid(pb_ref[...] + bb_ref[...])
        dya_ref[...] = (dm * ga).astype(BF16)
        dyb_ref[...] = (dm * gb).astype(BF16)
        dpa = dm * ya_ref[...].astype(F32) * ga * (1.0 - ga)
        dpb = dm * yb_ref[...].astype(F32) * gb * (1.0 - gb)
        dg_ref[0] = dpa.astype(BF16)
        dg_ref[1] = dpb.astype(BF16)
        sa = jnp.sum(dpa, axis=0, keepdims=True)
        sb = jnp.sum(dpb, axis=0, keepdims=True)

        @pl.when(i == 0)
        def _():
            dbg_ref[0] = sa
            dbg_ref[1] = sb

        @pl.when(i > 0)
        def _():
            dbg_ref[0] += sa
            dbg_ref[1] += sb

    tile = ((tm, tn), lambda j, i: (i, j))
    dya, dyb, dgate, dbg = _call(
        name, body, (nn, S // tm),
        [(dx2, (tm, D), lambda j, i: (i, 0)),
         _mix_rows_spec(Gm, MIX_WOUT, tn // LANES, lambda j, i: j),
         (ya,) + tile, (yb,) + tile,
         (proj_b, (tm, tn), lambda j, i: (i, ga0 + j)), (proj_b, (tm, tn), lambda j, i: (i, gb0 + j)),
         (b_gate, (1, tn), lambda j, i: (0, j)), (b_gate, (1, tn), lambda j, i: (0, nn + j))],
        [((S, D), BF16) + tile, ((S, D), BF16) + tile,
         ((2, S, D), BF16, (2, tm, tn), lambda j, i: (0, i, j)),
         ((2, 1, D), F32, (2, 1, tn), lambda j, i: (0, 0, j))],
        sem=("parallel", "arbitrary"))
    return dya, dyb, dgate, dbg


def matmul_nt(a, b_spec_fn, N, name, tn=512):
    S, K = a.shape
    tm = 512

    def body(a_ref, b_ref, o_ref):
        b = b_ref[...]
        o_ref[...] = _dot(a_ref[...], b.reshape(-1, b.shape[-1]), 1, 1)

    return _call(name, body, (N // tn, S // tm),
                 [(a, (tm, K), lambda j, i: (i, 0)), b_spec_fn(lambda j, i: j)],
                 [((S, N), F32, (tm, tn), lambda j, i: (i, j))], sem=("parallel", "parallel"))[0]


def weight_grad_rows(a, b, grads, row0, name):
    S, M = a.shape
    N = b.shape[1]
    tmm = 512
    tk = WGRAD_TK
    nk = S // tk

    def body(g_ref, a_ref, b_ref, o_ref, acc_ref):
        k = pl.program_id(1)
        p = _dot(a_ref[...], b_ref[...].astype(BF16), 0, 0)

        @pl.when(k == 0)
        def _():
            acc_ref[...] = p

        @pl.when(k > 0)
        def _():
            acc_ref[...] += p

        @pl.when(k == nk - 1)
        def _():
            o_ref[...] = acc_ref[...].astype(BF16).reshape(tmm // LANES, LANES, N)

    return pl.pallas_call(
        body,
        out_shape=jax.ShapeDtypeStruct(grads.shape, BF16),
        grid=(M // tmm, nk),
        in_specs=[pl.BlockSpec(memory_space=pl.ANY),
                  pl.BlockSpec((tk, tmm), lambda j, k: (k, j)),
                  pl.BlockSpec((tk, N), lambda j, k: (k, 0))],
        out_specs=pl.BlockSpec((tmm // LANES, LANES, N), lambda j, k: (j, row0 // LANES, 0)),
        scratch_shapes=[pltpu.VMEM((tmm, N), F32)],
        input_output_aliases={0: 0},
        name=name,
        compiler_params=pltpu.CompilerParams(dimension_semantics=("parallel", "arbitrary"),
                                             vmem_limit_bytes=VMEM_LIMIT),
    )(grads, a, b)


def weight_grad_plain(a, b, name):
    S, M = a.shape
    N = b.shape[1]
    tk = WGRAD_TK
    nk = S // tk

    def body(a_ref, b_ref, o_ref, acc_ref):
        k = pl.program_id(0)
        p = _dot(a_ref[...], b_ref[...], 0, 0)

        @pl.when(k == 0)
        def _():
            acc_ref[...] = p

        @pl.when(k > 0)
        def _():
            acc_ref[...] += p

        @pl.when(k == nk - 1)
        def _():
            o_ref[...] = acc_ref[...].astype(BF16)

    return _call(name, body, (nk,),
                 [(a, (tk, M), lambda k: (k, 0)), (b, (tk, N), lambda k: (k, 0))],
                 [((M, N), BF16, (M, N), lambda k: (0, 0))],
                 scratch=[pltpu.VMEM((M, N), F32)], sem=("arbitrary",))[0]


def local_step(x, tgt, p, G1, get_gm, get_g2, emit, start_token):
    S, D = x.shape
    after = lambda t: t[0:1, 0:1]
    buckets = _bucket_tables()
    cos_t, sin_t = _rope_tables(S)
    gains = jnp.concatenate([jnp.tile(p["q_norm"], (1, N_HEADS_B)), jnp.tile(p["k_norm"], (1, N_KV_B)),
                             jnp.ones((1, N_KV_B * LANES), F32)], axis=1)

    n1 = rms_fwd(x, p["ffn1_norm"] + after(start_token), "ffn1_norm")
    ab1 = ffn_up(n1, G1, "ffn1_up")
    x1 = ffn_down(ab1, G1, x, "ffn1_down")

    Gm = get_gm(x1)
    w_a = Gm[:, MIX_WA:MIX_ROWS, :].reshape(N_DEV, GROUP_WIDTH_A, LANES).transpose(1, 0, 2).reshape(GROUP_WIDTH_A, D)
    hm = rms_fwd(x1, p["mix_norm"], "mix_norm")
    n_a = A_QKV_WIDTH // PROJ_TN
    proj_a = [in_proj(hm, Gm, g, 3, BF16, "in_proj_a%d" % g, tile_stride=3) for g in range(3)]
    proj_b = in_proj(hm, Gm, n_a, PB_WIDTH // PROJ_TN, F32, "in_proj_b")

    bias = bias_build(p["rel_bias"], buckets)
    outs, lses = [], []
    for g in range(3):
        o, l = a_fwd(proj_a[g], bias[g], g, "a_fwd_%d" % g)
        outs.append(o)
        lses.append(l)
    o_a, lse_tot = a_combine(outs, lses, "a_combine")

    qkv = qkv_prep(proj_b, gains, cos_t, sin_t, "qkv_prep")
    k_t = qkv[:, N_HEADS_B * LANES:(N_HEADS_B + N_KV_B) * LANES].T
    o_b, lse_b = flash_fwd(qkv, "flash_fwd")

    merged, ya, yb = merge_fwd(o_a, o_b, w_a, Gm, proj_b, p["b_gate"], "merge_fwd")
    x2 = out_proj(merged, Gm, x1, "out_proj")

    G2 = get_g2(x2)
    n2 = rms_fwd(x2, p["ffn2_norm"], "ffn2_norm")
    ab2 = ffn_up(n2, G2, "ffn2_up")
    x3 = ffn_down(ab2, G2, x2, "ffn2_down")

    loss, dx3, d_final = final_loss(x3, tgt, p["final_norm"], "final_loss")

    dabh2, gw2 = ffn_bwd_weights(dx3, ab2, n2, G2, "ffn2_bwd")
    t2 = emit("ffn2", gw2)
    dx2, d_ffn2_norm = ffn_bwd_input(dabh2, G2, x2, p["ffn2_norm"] + after(t2), dx3, "ffn2_bwd")

    dya, dyb, dgate, dbg = merge_bwd(dx2, Gm, ya, yb, proj_b, p["b_gate"], "merge_bwd")
    gm_grads = jnp.zeros(Gm.shape, BF16)
    gm_grads = weight_grad_rows(merged, dx2, gm_grads, MIX_WOUT, "dw_out")
    gm_grads = weight_grad_rows(o_b, dyb, gm_grads, MIX_WB, "dw_branch_b")
    dw_a = weight_grad_plain(o_a, dya, "dw_branch_a")
    do_a = matmul_nt(dya, lambda jm: (w_a, (MERGE_TN, D), lambda j, i: (jm(j, i), 0)), GROUP_WIDTH_A, "do_a")
    do_b = matmul_nt(dyb, lambda jm: _mix_rows_spec(Gm, MIX_WB, MERGE_TN // LANES, jm), N_HEADS_B * LANES, "do_b")

    dq_r, dk_r, dv_b = flash_bwd(qkv, k_t, do_b, o_b, lse_b, "flash_bwd")
    dq_b, d_q_norm = qk_prep_bwd(dq_r, proj_b, 0, p["q_norm"], cos_t, sin_t, "q_prep_bwd")
    dk_b, d_k_norm = qk_prep_bwd(dk_r, proj_b, N_HEADS_B, p["k_norm"], cos_t, sin_t, "k_prep_bwd")

    dqs, dks, dvs, dbs = [], [], [], []
    for g in range(3):
        dq, dk, dv, db = a_bwd(proj_a[g], bias[g], do_a, o_a, lse_tot, g, "a_bwd_%d" % g)
        dqs.append(dq)
        dks.append(dk)
        dvs.append(dv)
        dbs.append(db)
    d_rel_bias = bias_bwd(jnp.stack(dbs, axis=0), buckets)

    dproj = jnp.concatenate(dqs + dks + dvs + [dq_b, dk_b, dv_b, dgate[0], dgate[1]], axis=1)
    gm_grads = in_proj_bwd_dw(dproj, hm, gm_grads, "in_proj_bwd")
    dw_a_sh = dw_a.reshape(GROUP_WIDTH_A, N_DEV, LANES).transpose(1, 0, 2).reshape(N_DEV, MIX_ROWS - MIX_WA, D)
    gm_grads = lax.dynamic_update_slice(gm_grads, dw_a_sh, (0, MIX_WA, 0))
    tm = emit("mix", gm_grads)
    dx1, d_mix_norm = in_proj_bwd_dh(dproj, Gm, x1, p["mix_norm"] + after(tm), dx2, "in_proj_bwd")

    dabh1, gw1 = ffn_bwd_weights(dx1, ab1, n1, G1, "ffn1_bwd")
    t1 = emit("ffn1", gw1)
    dx0, d_ffn1_norm = ffn_bwd_input(dabh1, G1, x, p["ffn1_norm"] + after(t1), dx1, "ffn1_bwd")

    small = dict(ffn1_norm=d_ffn1_norm, mix_norm=d_mix_norm, b_gate=dbg.reshape(1, 2 * D),
                 q_norm=d_q_norm, k_norm=d_k_norm, rel_bias=d_rel_bias, ffn2_norm=d_ffn2_norm,
                 final_norm=d_final)
    return loss, dx0, small


def _pack_small(t, loss_row):
    row6 = jnp.concatenate([t["q_norm"].reshape(1, -1), t["k_norm"].reshape(1, -1), t["rel_bias"].reshape(1, -1)], axis=1)
    return jnp.concatenate([t["ffn1_norm"].reshape(1, -1), t["mix_norm"].reshape(1, -1), t["b_gate"].reshape(2, -1),
                            t["ffn2_norm"].reshape(1, -1), t["final_norm"].reshape(1, -1), row6, loss_row], axis=0)


def _unpack_small(a, shapes):
    return dict(ffn1_norm=a[0:1].reshape(shapes["ffn1_norm"]), mix_norm=a[1:2].reshape(shapes["mix_norm"]),
                b_gate=a[2:4].reshape(shapes["b_gate"]), ffn2_norm=a[4:5].reshape(shapes["ffn2_norm"]),
                final_norm=a[5].reshape(shapes["final_norm"]), q_norm=a[6:7, 0:128].reshape(shapes["q_norm"]),
                k_norm=a[6:7, 128:256].reshape(shapes["k_norm"]), rel_bias=a[6, 256:1024].reshape(shapes["rel_bias"]))


SMALL = ("ffn1_norm", "mix_norm", "b_gate", "q_norm", "k_norm", "rel_bias", "ffn2_norm", "final_norm")
ORDER = ("ffn1_norm", "ffn1_w1", "ffn1_w3", "ffn1_w2", "mix_norm", "w_in", "b_gate", "q_norm", "k_norm", "rel_bias",
         "w_branch_a", "w_branch_b", "w_out", "ffn2_norm", "ffn2_w1", "ffn2_w3", "ffn2_w2", "final_norm")


def kernel(x, ffn1_norm, ffn1_w1, ffn1_w3, ffn1_w2, mix_norm, w_in, b_gate, q_norm, k_norm, rel_bias, w_branch_a, w_branch_b, w_out, ffn2_norm, ffn2_w1, ffn2_w3, ffn2_w2, final_norm, loss_target, m_ffn1_norm, m_ffn1_w1, m_ffn1_w3, m_ffn1_w2, m_mix_norm, m_w_in, m_b_gate, m_q_norm, m_k_norm, m_rel_bias, m_w_branch_a, m_w_branch_b, m_w_out, m_ffn2_norm, m_ffn2_w1, m_ffn2_w3, m_ffn2_w2, m_final_norm, v_ffn1_norm, v_ffn1_w1, v_ffn1_w3, v_ffn1_w2, v_mix_norm, v_w_in, v_b_gate, v_q_norm, v_k_norm, v_rel_bias, v_w_branch_a, v_w_branch_b, v_w_out, v_ffn2_norm, v_ffn2_w1, v_ffn2_w3, v_ffn2_w2, v_final_norm):
    args = dict(locals())
    w = {n: args[n] for n in ORDER}
    m = {n: args["m_" + n] for n in ORDER}
    v = {n: args["v_" + n] for n in ORDER}
    D = x.shape[2]

    def ffn_group(w1, w3, w2):
        return jnp.concatenate([w1[0].T, w3[0].T, w2[0]], axis=0).astype(BF16)

    g1 = ffn_group(ffn1_w1, ffn1_w3, ffn1_w2)
    g2 = ffn_group(ffn2_w1, ffn2_w3, ffn2_w2)
    gm = jnp.concatenate([w_in[0], w_branch_b[0], w_out[0], w_branch_a[0].reshape(MIX_ROWS - MIX_WA, D)],
                         axis=0).astype(BF16)
    (G1,), (gm, g2) = all_gather_groups([g1], later=(gm, g2))
    ag_m = all_gather_start(gm, "all_gather_mix_start")
    ag_2 = all_gather_start(g2, "all_gather_ffn2_start")

    def get_gm(after):
        return all_gather_finish(*_split_wait("all_gather_mix_wait", ag_m, 4, after), "all_gather_mix_finish")

    def get_g2(after):
        return all_gather_finish(*_split_wait("all_gather_ffn2_wait", ag_2, 4, after), "all_gather_ffn2_finish")

    core = lax.axis_index("c").astype(jnp.int32).reshape(1)
    chip = (2 * lax.axis_index("x") + lax.axis_index("y")).astype(jnp.int32).reshape(1)
    exchanges = {}

    def emit(tag, gw):
        (theirs,) = reduce_scatter_pair([gw], "reduce_scatter_pair_" + tag)
        part = pair_add(gw, theirs, core, "pair_add_" + tag)
        exchanges[tag] = reduce_scatter_start(part, "reduce_scatter_" + tag + "_start")
        return exchanges[tag][4]

    small_p = dict(ffn1_norm=ffn1_norm, mix_norm=mix_norm, b_gate=b_gate, q_norm=q_norm, k_norm=k_norm,
                   rel_bias=rel_bias, ffn2_norm=ffn2_norm, final_norm=final_norm.reshape(1, D))
    loss_p, grad_x, small_g = local_step(x[0], loss_target[0], small_p, G1, get_gm, get_g2, emit, ag_m[4] + ag_2[4])

    def landed(tag, after):
        return _split_wait("reduce_scatter_" + tag + "_wait", exchanges[tag], 3, after)

    grads = {}
    last_token = exchanges["ffn1"][4]
    for tag, after in (("ffn2", last_token), ("ffn1", grad_x)):
        part, land = landed(tag, after)
        ssum = lambda off, nm: sum_chips(part, land, chip, off, FFN_SHARD, FFN_SHARD, tag + nm)
        grads[tag + "_w1"] = ssum(0, "_w1_sum").T[None]
        grads[tag + "_w3"] = ssum(FFN_SHARD, "_w3_sum").T[None]
        grads[tag + "_w2"] = ssum(2 * FFN_SHARD, "_w2_sum")[None]
        if tag == "ffn2":
            part_m, land_m = landed("mix", last_token)
            msum = lambda off, rows, blk, nm: sum_chips(part_m, land_m, chip, off, rows, blk, nm)
            grads["w_in"] = msum(MIX_WIN, MIX_WB - MIX_WIN, LANES, "w_in_sum")[None]
            grads["w_branch_b"] = msum(MIX_WB, LANES, LANES, "w_branch_b_sum")[None]
            grads["w_out"] = msum(MIX_WOUT, LANES, LANES, "w_out_sum")[None]
            grads["w_branch_a"] = msum(MIX_WA, MIX_ROWS - MIX_WA, MIX_ROWS - MIX_WA,
                                       "w_branch_a_sum").reshape(w_branch_a.shape)
    loss_row = jnp.pad(loss_p, ((0, 0), (0, D - LANES)))
    smalls = small_all_gather(_pack_small(small_g, loss_row))
    small_sum = sum_slots(smalls, 0, N_DEV, N_DEV, "small_sum")
    small_shapes = {n: w[n].shape for n in SMALL}
    grads.update(_unpack_small(small_sum, small_shapes))
    loss = small_sum[7, 0]

    delta, new_m, new_v = {}, {}, {}
    for n in ORDER:
        if n in SMALL:
            continue
        shp = w[n].shape
        if n.endswith("_w1") or n.endswith("_w3"):
            to2 = lambda a: a.reshape(shp[-2], shp[-1]).T
            back = lambda a: a.T.reshape(shp)
        else:
            to2 = lambda a: a.reshape(shp[-2], shp[-1])
            back = lambda a: a.reshape(shp)
        d_, m_, v_ = adamw(to2(w[n]), to2(grads[n]), to2(m[n]), to2(v[n]), "adamw_" + n)
        delta[n], new_m[n], new_v[n] = back(d_), back(m_), back(v_)
    zero_row = jnp.zeros((1, D), F32)
    pack = lambda t: _pack_small({n: t[n] for n in SMALL}, zero_row)
    d_, m_, v_ = adamw(pack(w), small_sum, pack(m), pack(v), "adamw_small")
    for src, dst in ((d_, delta), (m_, new_m), (v_, new_v)):
        dst.update(_unpack_small(src, small_shapes))

    return (loss, grad_x[None], *[grads[n] for n in ORDER], *[delta[n] for n in ORDER],
            *[new_m[n] for n in ORDER], *[new_v[n] for n in ORDER])
```

```python
import math

import jax
import jax.numpy as jnp
from jax import lax
from jax.experimental import pallas as pl
from jax.experimental.pallas import tpu as pltpu

F32 = jnp.float32
BF16 = jnp.bfloat16
MESH = pl.DeviceIdType.MESH

V7X_VMEM_BYTES = 64 * 1024 * 1024
VMEM_LIMIT = V7X_VMEM_BYTES - 8 * 1024 * 1024
LANES = 128

N_DEV = 8
EPS = 1e-6
NEG_INF = -1e30

DILATIONS = (1, 4, 16)
HALF_WINDOW = 64
HEAD_DIM_A = 64
HEADS_PER_GROUP_A = 8
GROUP_WIDTH_A = 512
A_QKV_WIDTH = 4608
A_GROUP_QKV = A_QKV_WIDTH // 3
A_TQ = 128
A_WIN = A_TQ + 2 * HALF_WINDOW
A_UNROLL = 4
WGRAD_TK = 2048
HEAD_DIM_B = 128
N_HEADS_B = 8
N_KV_B = 2
GQA_GROUP_B = 4
GRID_W = 64
ROPE_THETA = 10000.0
B_TQ_FWD = 256
B_TQ_BWD = 512
N_BUCKETS = 32
MAX_DISTANCE = 1024
PB_WIDTH = 3584
PB_GATE_A = 1536
PB_GATE_B = 2560

ADAM_LR = 0.001
ADAM_B1 = 0.9
ADAM_B2 = 0.999
ADAM_EPS = 1e-08
ADAM_WD = 0.01
ADAM_STEP = 10

FFN_SHARD = 352
MIX_WIN, MIX_WB, MIX_WOUT, MIX_WA = 0, 1024, 1152, 1280
MIX_ROWS = 1344


def _dot(a, b, ca=1, cb=0):
    return lax.dot_general(a, b, (((ca,), (cb,)), ((), ())), preferred_element_type=F32)


def _call(name, body, grid, ins, outs, scratch=(), sem=None, aliases=None):
    res = pl.pallas_call(
        body,
        out_shape=[jax.ShapeDtypeStruct(s, d) for (s, d, _, _) in outs],
        grid=grid,
        in_specs=[pl.BlockSpec(bs, im) for (_, bs, im) in ins],
        out_specs=[pl.BlockSpec(bs, im) for (_, _, bs, im) in outs],
        scratch_shapes=list(scratch),
        name=name,
        input_output_aliases=aliases or {},
        compiler_params=pltpu.CompilerParams(dimension_semantics=sem, vmem_limit_bytes=VMEM_LIMIT),
    )(*[a for (a, _, _) in ins])
    return res


def _sigmoid(x):
    return 1.0 / (1.0 + jnp.exp(-x))


def _position():
    return lax.axis_index("x"), lax.axis_index("y"), lax.axis_index("c")


def _hbm_specs(n):
    return [pl.BlockSpec(memory_space=pl.ANY) for _ in range(n)]


PAIR_BUFFERS = 4


def reduce_scatter_pair(grads, name):
    n = len(grads)
    C = grads[0].shape[2]
    half = [g.shape[1] // 2 for g in grads]
    chunks = [(i, q, hf) for i in range(n) for q in range(4) for hf in range(2)]
    nb = PAIR_BUFFERS

    def body(*refs):
        ins, theirs = refs[:n], refs[n:2 * n]
        buf, load_sems, send_sems, recv_sems = refs[2 * n:]
        x, y, c = _position()
        sibling = (x, y, 1 - c)

        def load(k):
            i, q, hf = chunks[k]
            r = half[i]
            return pltpu.make_async_copy(ins[i].at[2 * q + (1 - c), pl.ds(hf * r, r), :],
                                         buf.at[k % nb, pl.ds(0, r), :], load_sems.at[k % nb])

        def send(k):
            i, q, hf = chunks[k]
            r = half[i]
            return pltpu.make_async_remote_copy(
                src_ref=buf.at[k % nb, pl.ds(0, r), :], dst_ref=theirs[i].at[q, pl.ds(hf * r, r), :],
                send_sem=send_sems.at[k % nb], recv_sem=recv_sems.at[i],
                device_id=sibling, device_id_type=MESH)

        for k in range(len(chunks) + 1):
            if k < len(chunks):
                if k >= nb:
                    send(k - nb).wait_send()
                load(k).start()
            if k >= 1:
                load(k - 1).wait()
                send(k - 1).start()
        for k in range(max(0, len(chunks) - nb), len(chunks)):
            send(k).wait_send()
        for i in range(n):
            pltpu.make_async_remote_copy(
                src_ref=theirs[i], dst_ref=theirs[i], send_sem=send_sems.at[0], recv_sem=recv_sems.at[i],
                device_id=sibling, device_id_type=MESH).wait_recv()

    return pl.pallas_call(
        body,
        out_shape=[jax.ShapeDtypeStruct((4,) + g.shape[1:], g.dtype) for g in grads],
        in_specs=_hbm_specs(n),
        out_specs=_hbm_specs(n),
        scratch_shapes=[pltpu.VMEM((nb, max(half), C), grads[0].dtype), pltpu.SemaphoreType.DMA((nb,)),
                        pltpu.SemaphoreType.DMA((nb,)), pltpu.SemaphoreType.DMA((n,))],
        name=name,
        compiler_params=pltpu.CompilerParams(vmem_limit_bytes=VMEM_LIMIT),
    )(*grads)


_HBM_SPEC = pl.BlockSpec(memory_space=pltpu.HBM)
_SEM_SPEC = pl.BlockSpec(memory_space=pltpu.SEMAPHORE)
_TOKEN_SPEC = pl.BlockSpec(memory_space=pltpu.VMEM)
_DATAFLOW = pltpu.SideEffectType.DATAFLOW_SIDE_EFFECTING


def _split_start(name, body, src, land_shape):
    def full_body(src_ref, land_ref, send_sem, recv_sem, src_thru, land_thru, token):
        body(src_ref, land_ref, send_sem, recv_sem)
        token[...] = jnp.zeros_like(token)

    land = pltpu.with_memory_space_constraint(lax.empty(land_shape, src.dtype), pltpu.HBM)
    return pl.pallas_call(
        full_body, name=name,
        out_shape=(pltpu.SemaphoreType.DMA(()), pltpu.SemaphoreType.DMA(()),
                   pltpu.HBM(src.shape, src.dtype), pltpu.HBM(land_shape, src.dtype),
                   jax.ShapeDtypeStruct((8, LANES), F32)),
        in_specs=(_HBM_SPEC, _HBM_SPEC),
        out_specs=(_SEM_SPEC, _SEM_SPEC, _HBM_SPEC, _HBM_SPEC, _TOKEN_SPEC),
        input_output_aliases={0: 2, 1: 3},
        compiler_params=pltpu.CompilerParams(has_side_effects=_DATAFLOW),
    )(pltpu.with_memory_space_constraint(src, pltpu.HBM), land)


def _split_wait(name, started, n_blocks, after):
    send_sem, recv_sem, src_thru, land_thru, _ = started
    after = after if isinstance(after, tuple) else (after,)

    def body(src_ref, land_ref, send_sem, recv_sem, *rest):
        x, y, c = _position()
        blocks = land_ref.at[pl.ds(0, n_blocks)]
        copy = pltpu.make_async_remote_copy(src_ref=blocks, dst_ref=blocks, send_sem=send_sem, recv_sem=recv_sem,
                                            device_id=(x, y, c), device_id_type=MESH)
        copy.wait_send()
        copy.wait_recv()

    return pl.pallas_call(
        body, name=name,
        out_shape=(pltpu.HBM(src_thru.shape, src_thru.dtype), pltpu.HBM(land_thru.shape, land_thru.dtype)),
        in_specs=(_HBM_SPEC, _HBM_SPEC, _SEM_SPEC, _SEM_SPEC) + (pl.BlockSpec(memory_space=pl.ANY),) * len(after),
        out_specs=(_HBM_SPEC, _HBM_SPEC),
        input_output_aliases={0: 0, 1: 1},
        compiler_params=pltpu.CompilerParams(has_side_effects=_DATAFLOW),
    )(src_thru, land_thru, send_sem, recv_sem, *after)


def all_gather_start(block, name):
    def body(b_ref, land_ref, send_sem, recv_sem):
        x, y, c = _position()
        for peer in [(x, y, 1 - c), (1 - x, y, c), (x, 1 - y, c), (1 - x, 1 - y, c)]:
            pltpu.make_async_remote_copy(src_ref=b_ref, dst_ref=land_ref.at[4 * x + 2 * y + c],
                                         send_sem=send_sem, recv_sem=recv_sem,
                                         device_id=peer, device_id_type=MESH).start()

    return _split_start(name, body, block, (N_DEV,) + block.shape)


def all_gather_finish(block, land, name):
    R, C = block.shape

    def body(b_ref, land_in, land_ref, stage, load_sems, send_sems, recv_sems, own_sem):
        x, y, c = _position()
        sibling = (x, y, 1 - c)
        chips = [(1 - x, y), (x, 1 - y), (1 - x, 1 - y)]
        own_in = pltpu.make_async_copy(b_ref, stage.at[3], load_sems.at[3])
        own_in.start()
        loads = [pltpu.make_async_copy(land_in.at[4 * px + 2 * py + c], stage.at[j], load_sems.at[j])
                 for j, (px, py) in enumerate(chips)]
        for ld in loads:
            ld.start()
        sends = []
        for j, (px, py) in enumerate(chips):
            loads[j].wait()
            dst = land_ref.at[4 * px + 2 * py + c]
            cp = pltpu.make_async_remote_copy(src_ref=stage.at[j], dst_ref=dst, send_sem=send_sems.at[j],
                                              recv_sem=recv_sems.at[j], device_id=sibling, device_id_type=MESH)
            cp.start()
            sends.append(cp)
        own_in.wait()
        own_out = pltpu.make_async_copy(stage.at[3], land_ref.at[4 * x + 2 * y + c], own_sem)
        own_out.start()
        for j, (px, py) in enumerate(chips):
            dst = land_ref.at[4 * px + 2 * py + (1 - c)]
            pltpu.make_async_remote_copy(src_ref=stage.at[j], dst_ref=dst, send_sem=send_sems.at[j],
                                         recv_sem=recv_sems.at[j], device_id=sibling,
                                         device_id_type=MESH).wait_recv()
        for cp in sends:
            cp.wait_send()
        own_out.wait()

    return pl.pallas_call(
        body,
        out_shape=jax.ShapeDtypeStruct(land.shape, land.dtype),
        in_specs=_hbm_specs(2),
        out_specs=pl.BlockSpec(memory_space=pl.ANY),
        scratch_shapes=[pltpu.VMEM((4, R, C), block.dtype), pltpu.SemaphoreType.DMA((4,)),
                        pltpu.SemaphoreType.DMA((3,)), pltpu.SemaphoreType.DMA((3,)), pltpu.SemaphoreType.DMA],
        input_output_aliases={1: 0},
        name=name,
        compiler_params=pltpu.CompilerParams(vmem_limit_bytes=VMEM_LIMIT),
    )(block, land)


def reduce_scatter_start(parts, name):
    def body(p_ref, land_ref, send_sem, recv_sem):
        x, y, c = _position()
        for px, py in [(1 - x, y), (x, 1 - y), (1 - x, 1 - y)]:
            pltpu.make_async_remote_copy(src_ref=p_ref.at[2 * px + py], dst_ref=land_ref.at[2 * x + y],
                                         send_sem=send_sem, recv_sem=recv_sem,
                                         device_id=(px, py, c), device_id_type=MESH).start()

    return _split_start(name, body, parts, parts.shape)


def small_all_gather(small):
    def body(small_ref, smalls, s_send, s_recv, s_local):
        x, y, c = _position()
        me = 4 * x + 2 * y + c
        lc = pltpu.make_async_copy(small_ref, smalls.at[me], s_local)
        lc.start()
        remote = []
        k = 0
        for dx in (0, 1):
            for dy in (0, 1):
                for dc in (0, 1):
                    if dx + dy + dc == 0:
                        continue
                    peer = (1 - x if dx else x, 1 - y if dy else y, 1 - c if dc else c)
                    rc = pltpu.make_async_remote_copy(
                        src_ref=small_ref, dst_ref=smalls.at[me],
                        send_sem=s_send.at[k], recv_sem=s_recv.at[k],
                        device_id=peer, device_id_type=MESH)
                    rc.start()
                    remote.append(rc)
                    k += 1
        for rc in remote:
            rc.wait()
        lc.wait()

    return pl.pallas_call(
        body,
        out_shape=jax.ShapeDtypeStruct((N_DEV,) + small.shape, small.dtype),
        in_specs=_hbm_specs(1),
        out_specs=pl.BlockSpec(memory_space=pl.ANY),
        scratch_shapes=[pltpu.SemaphoreType.DMA((7,)), pltpu.SemaphoreType.DMA((7,)), pltpu.SemaphoreType.DMA],
        name="small_all_gather",
    )(small)


def pair_add(grads, theirs, core, name):
    _, R, C = theirs.shape
    tr = R // 2

    def body(c_ref, a_ref, b_ref, o_ref):
        o_ref[...] = (a_ref[...].astype(F32) + b_ref[...].astype(F32)).astype(BF16)

    return pl.pallas_call(
        body,
        out_shape=jax.ShapeDtypeStruct(theirs.shape, BF16),
        grid_spec=pltpu.PrefetchScalarGridSpec(
            num_scalar_prefetch=1, grid=(4, R // tr),
            in_specs=[pl.BlockSpec((None, tr, C), lambda q, i, c: (2 * q + c[0], i, 0)),
                      pl.BlockSpec((None, tr, C), lambda q, i, c: (q, i, 0))],
            out_specs=pl.BlockSpec((None, tr, C), lambda q, i, c: (q, i, 0))),
        name=name,
        compiler_params=pltpu.CompilerParams(dimension_semantics=("parallel", "parallel"),
                                             vmem_limit_bytes=VMEM_LIMIT),
    )(core, grads, theirs)


def sum_slots(recv, off, rows, blk, name):
    nq, _, C = recv.shape
    ob = off // blk

    def body(r_ref, o_ref):
        acc = r_ref[0].astype(F32)
        for q in range(1, nq):
            acc = acc + r_ref[q].astype(F32)
        o_ref[...] = acc

    return _call(name, body, (rows // blk,),
                 [(recv, (nq, blk, C), lambda i: (0, ob + i, 0))],
                 [((rows, C), F32, (blk, C), lambda i: (i, 0))], sem=("parallel",))[0]


def sum_chips(parts, land, chip, off, rows, blk, name):
    C = parts.shape[2]
    ob = off // blk

    def body(c_ref, own_ref, a_ref, b_ref, d_ref, o_ref):
        o_ref[...] = ((own_ref[...].astype(F32) + a_ref[...].astype(F32)) + b_ref[...].astype(F32)) \
            + d_ref[...].astype(F32)

    def entry(flip):
        return pl.BlockSpec((None, blk, C), lambda i, c: (c[0] ^ flip, ob + i, 0))

    return pl.pallas_call(
        body,
        out_shape=jax.ShapeDtypeStruct((rows, C), F32),
        grid_spec=pltpu.PrefetchScalarGridSpec(
            num_scalar_prefetch=1, grid=(rows // blk,),
            in_specs=[entry(0), entry(1), entry(2), entry(3)],
            out_specs=pl.BlockSpec((blk, C), lambda i, c: (i, 0))),
        name=name,
        compiler_params=pltpu.CompilerParams(dimension_semantics=("parallel",), vmem_limit_bytes=VMEM_LIMIT),
    )(chip, parts, land, land, land)


def _adamw_update(wv, gv, mv, vv):
    nm = ADAM_B1 * mv + (1.0 - ADAM_B1) * gv
    nv = ADAM_B2 * vv + (1.0 - ADAM_B2) * (gv * gv)
    c1 = 1.0 / (1.0 - ADAM_B1 ** ADAM_STEP)
    c2 = 1.0 / (1.0 - ADAM_B2 ** ADAM_STEP)
    return -ADAM_LR * ((nm * c1) / (jnp.sqrt(nv * c2) + ADAM_EPS) + ADAM_WD * wv), nm, nv


def sum_adamw(parts, land, chip, off, blk, w, m, v, name):
    rows, C = w.shape
    ob = off // blk

    def body(c_ref, own_ref, a_ref, b_ref, d_ref, w_ref, m_ref, v_ref, g_out, d_out, m_out, v_out):
        gv = ((own_ref[...].astype(F32) + a_ref[...].astype(F32)) + b_ref[...].astype(F32)) \
            + d_ref[...].astype(F32)
        g_out[...] = gv
        d_out[...], m_out[...], v_out[...] = _adamw_update(w_ref[...], gv, m_ref[...], v_ref[...])

    def entry(flip):
        return pl.BlockSpec((None, blk, C), lambda i, c: (c[0] ^ flip, ob + i, 0))

    plain = pl.BlockSpec((blk, C), lambda i, c: (i, 0))
    return pl.pallas_call(
        body,
        out_shape=[jax.ShapeDtypeStruct((rows, C), F32)] * 4,
        grid_spec=pltpu.PrefetchScalarGridSpec(
            num_scalar_prefetch=1, grid=(rows // blk,),
            in_specs=[entry(0), entry(1), entry(2), entry(3), plain, plain, plain],
            out_specs=[plain] * 4),
        name=name,
        compiler_params=pltpu.CompilerParams(dimension_semantics=("parallel",), vmem_limit_bytes=VMEM_LIMIT),
    )(chip, parts, land, land, land, w, m, v)


def adamw(w, g, m, v, name):
    R, C = w.shape
    tr = R
    for cand in (256, 128, 64, 32, 16, 8):
        if R % cand == 0 and R > cand:
            tr = cand
            break

    def body(w_ref, g_ref, m_ref, v_ref, d_ref, nm_ref, nv_ref):
        d_ref[...], nm_ref[...], nv_ref[...] = _adamw_update(w_ref[...], g_ref[...], m_ref[...], v_ref[...])

    spec = ((tr, C), lambda i: (i, 0))
    out = ((R, C), F32) + spec
    return _call(name, body, (R // tr,), [(w,) + spec, (g,) + spec, (m,) + spec, (v,) + spec],
                 [out, out, out], sem=("parallel",))


def rms_fwd(x, g, name):
    S, D = x.shape
    tr = 512

    def body(x_ref, g_ref, o_ref):
        xv = x_ref[...]
        r = lax.rsqrt(jnp.mean(xv * xv, axis=-1, keepdims=True) + EPS)
        o_ref[...] = (xv * r * g_ref[...]).astype(BF16)

    return _call(name, body, (S // tr,),
                 [(x, (tr, D), lambda i: (i, 0)), (g, (1, D), lambda i: (0, 0))],
                 [((S, D), BF16, (tr, D), lambda i: (i, 0))], sem=("parallel",))[0]


def _rms_bwd_tile(dn, xv, gv):
    r = lax.rsqrt(jnp.mean(xv * xv, axis=-1, keepdims=True) + EPS)
    xh = xv * r
    dxh = dn * gv
    dx = r * (dxh - xh * jnp.mean(dxh * xh, axis=-1, keepdims=True))
    return dx, dn * xh


def final_loss(x, tgt, g, name):
    S, D = x.shape
    tr = 256

    def body(x_ref, t_ref, g_ref, l_ref, dx_ref, dg_ref):
        i = pl.program_id(0)
        xv, gv = x_ref[...], g_ref[...]
        r = lax.rsqrt(jnp.mean(xv * xv, axis=-1, keepdims=True) + EPS)
        xh = xv * r
        e = xh * gv - t_ref[...]
        part = 0.5 * jnp.sum(jnp.sum(e * e, axis=-1, keepdims=True) * (1.0 / D), axis=0, keepdims=True)
        dy = e * (1.0 / D)
        dxh = dy * gv
        dx_ref[...] = r * (dxh - xh * jnp.mean(dxh * xh, axis=-1, keepdims=True))
        dgp = jnp.sum(dy * xh, axis=0, keepdims=True)

        @pl.when(i == 0)
        def _():
            l_ref[...] = jnp.broadcast_to(part, l_ref.shape)
            dg_ref[...] = dgp

        @pl.when(i > 0)
        def _():
            l_ref[...] += jnp.broadcast_to(part, l_ref.shape)
            dg_ref[...] += dgp

    row = ((tr, D), lambda i: (i, 0))
    return _call(name, body, (S // tr,),
                 [(x,) + row, (tgt,) + row, (g, (1, D), lambda i: (0, 0))],
                 [((1, LANES), F32, (1, LANES), lambda i: (0, 0)), ((S, D), F32) + row,
                  ((1, D), F32, (1, D), lambda i: (0, 0))], sem=("arbitrary",))


FFN_TF = 4 * FFN_SHARD


def _ffn_pick(G, which):
    if isinstance(G, tuple):
        return (G[0], which) if which < 2 else (G[1], 0)
    return G, which


def _ffn_w_spec(G, which, imap):
    arr, blk = _ffn_pick(G, which)
    return (arr, (4, FFN_SHARD, arr.shape[2]), lambda *idx: (imap(*idx), blk, 0))


def _ffn_whole_w_spec(G, which):
    arr, blk = _ffn_pick(G, which)
    return (arr, (N_DEV, FFN_SHARD, arr.shape[2]), lambda *idx: (0, blk, 0))


def ffn_up(n, G, name):
    S, D = n.shape
    F = N_DEV * FFN_SHARD
    tm = 1024

    def body(n_ref, w1_ref, w3_ref, abh_ref):
        nv = n_ref[...]
        a = _dot(nv, w1_ref[...].reshape(FFN_TF, D), 1, 1).astype(BF16)
        b = _dot(nv, w3_ref[...].reshape(FFN_TF, D), 1, 1).astype(BF16)
        abh_ref[0] = a
        abh_ref[1] = b
        av, bv = a.astype(F32), b.astype(F32)
        abh_ref[2] = (av * _sigmoid(av) * bv).astype(BF16)

    return _call(name, body, (F // FFN_TF, S // tm),
                 [(n, (tm, D), lambda j, i: (i, 0)),
                  _ffn_w_spec(G, 0, lambda j, i: j), _ffn_w_spec(G, 1, lambda j, i: j)],
                 [((3, S, F), BF16, (3, tm, FFN_TF), lambda j, i: (0, i, j))],
                 sem=("parallel", "parallel"))[0]


def ffn_down(abh, G, x, name):
    _, S, F = abh.shape
    D = x.shape[1]
    tm = 512

    def body(h_ref, w2_ref, x_ref, o_ref):
        o_ref[...] = x_ref[...] + 0.5 * _dot(h_ref[...], w2_ref[...].reshape(F, D))

    return _call(name, body, (S // tm,),
                 [(abh, (None, tm, F), lambda i: (2, i, 0)), _ffn_whole_w_spec(G, 2),
                  (x, (tm, D), lambda i: (i, 0))],
                 [((S, D), F32, (tm, D), lambda i: (i, 0))], sem=("parallel",))[0]


def ffn_bwd_weights(dxo, abh, n, G, name):
    _, S, F = abh.shape
    D = dxo.shape[1]
    tm = 512
    nf = F // FFN_TF

    def down_body(d_ref, w2_ref, ab_ref, o_ref):
        dh = 0.5 * _dot(d_ref[...].astype(BF16), w2_ref[...].reshape(FFN_TF, D), 1, 1)
        av, bv = ab_ref[0].astype(F32), ab_ref[1].astype(F32)
        sig = _sigmoid(av)
        o_ref[0] = (dh * bv * (sig * (1.0 + av * (1.0 - sig)))).astype(BF16)
        o_ref[1] = (dh * (av * sig)).astype(BF16)

    dab = _call(name + "_down_bwd", down_body, (nf, S // tm),
                [(dxo, (tm, D), lambda j, i: (i, 0)), _ffn_w_spec(G, 2, lambda j, i: j),
                 (abh, (2, tm, FFN_TF), lambda j, i: (0, i, j))],
                [((2, S, F), BF16, (2, tm, FFN_TF), lambda j, i: (0, i, j))],
                sem=("parallel", "parallel"))[0]

    tk = WGRAD_TK
    nk = S // tk
    gshape = (N_DEV, 3 * FFN_SHARD, D)

    def dw2_body(h_ref, d_ref, o_ref, acc_ref):
        k = pl.program_id(1)
        p = _dot(h_ref[...], d_ref[...].astype(BF16), 0, 0)

        @pl.when(k == 0)
        def _():
            acc_ref[...] = p

        @pl.when(k > 0)
        def _():
            acc_ref[...] += p

        @pl.when(k == nk - 1)
        def _():
            o_ref[...] = (0.5 * acc_ref[...]).astype(BF16).reshape(4, FFN_SHARD, D)

    gw = _call(name + "_dw2", dw2_body, (nf, nk),
               [(abh, (None, tk, FFN_TF), lambda j, k: (2, k, j)), (dxo, (tk, D), lambda j, k: (k, 0))],
               [(gshape, BF16, (4, FFN_SHARD, D), lambda j, k: (j, 2, 0))],
               scratch=[pltpu.VMEM((FFN_TF, D), F32)], sem=("parallel", "arbitrary"))[0]

    def dw13_body(gw_ref, dab_ref, n_ref, o_ref, acc_ref):
        k = pl.program_id(2)
        p = _dot(dab_ref[...], n_ref[...], 0, 0)

        @pl.when(k == 0)
        def _():
            acc_ref[...] = p

        @pl.when(k > 0)
        def _():
            acc_ref[...] += p

        @pl.when(k == nk - 1)
        def _():
            o_ref[...] = acc_ref[...].astype(BF16).reshape(4, FFN_SHARD, D)

    gw = pl.pallas_call(
        dw13_body,
        out_shape=jax.ShapeDtypeStruct(gshape, BF16),
        grid=(2, nf, nk),
        in_specs=[pl.BlockSpec(memory_space=pl.ANY),
                  pl.BlockSpec((None, tk, FFN_TF), lambda w, j, k: (w, k, j)),
                  pl.BlockSpec((tk, D), lambda w, j, k: (k, 0))],
        out_specs=pl.BlockSpec((4, FFN_SHARD, D), lambda w, j, k: (j, w, 0)),
        scratch_shapes=[pltpu.VMEM((FFN_TF, D), F32)],
        input_output_aliases={0: 0},
        name=name + "_dw13",
        compiler_params=pltpu.CompilerParams(dimension_semantics=("parallel", "parallel", "arbitrary"),
                                             vmem_limit_bytes=VMEM_LIMIT),
    )(gw, dab, n)
    return dab, gw


def ffn_bwd_input(dab, G, x_in, g, dxo, name):
    _, S, F = dab.shape
    D = x_in.shape[1]
    tm = 256

    def dn_body(dab_ref, w1_ref, w3_ref, x_ref, d_ref, g_ref, dx_ref, dg_ref):
        i = pl.program_id(0)
        dn = _dot(dab_ref[0], w1_ref[...].reshape(F, D)) + _dot(dab_ref[1], w3_ref[...].reshape(F, D))
        dx, dgt = _rms_bwd_tile(dn, x_ref[...], g_ref[...])
        dx_ref[...] = d_ref[...] + dx
        dgp = jnp.sum(dgt, axis=0, keepdims=True)

        @pl.when(i == 0)
        def _():
            dg_ref[...] = dgp

        @pl.when(i > 0)
        def _():
            dg_ref[...] += dgp

    dx, dg = _call(name + "_dn", dn_body, (S // tm,),
                   [(dab, (2, tm, F), lambda i: (0, i, 0)),
                    _ffn_whole_w_spec(G, 0), _ffn_whole_w_spec(G, 1),
                    (x_in, (tm, D), lambda i: (i, 0)), (dxo, (tm, D), lambda i: (i, 0)),
                    (g, (1, D), lambda i: (0, 0))],
                   [((S, D), F32, (tm, D), lambda i: (i, 0)), ((1, D), F32, (1, D), lambda i: (0, 0))],
                   sem=("arbitrary",))
    return dx, dg


PROJ_TN = 512
DH_SHARDS_PER_STEP = 4


def in_proj(h, Gm, first_tile, n_tiles, dtype, name, tile_stride=1):
    S, D = h.shape
    tm = 1024
    tile = lambda j: first_tile + tile_stride * j

    def body(h_ref, w_ref, o_ref):
        o_ref[...] = _dot(h_ref[...], w_ref[...]).astype(dtype)

    return _call(name, body, (n_tiles, S // tm),
                 [(h, (tm, D), lambda j, i: (i, 0)),
                  (Gm, (None, D, PROJ_TN), lambda j, i: (tile(j) // 2, 0, tile(j) % 2))],
                 [((S, n_tiles * PROJ_TN), dtype, (tm, PROJ_TN), lambda j, i: (i, j))],
                 sem=("parallel", "parallel"))[0]


def in_proj_bwd_dw(dproj, h, gm_grads, name):
    S, D = h.shape
    NT = dproj.shape[1] // PROJ_TN
    tk = WGRAD_TK
    nk = S // tk

    def dw_body(gm_ref, h_ref, d_ref, o_ref, acc_ref):
        k = pl.program_id(1)
        p = _dot(h_ref[...], d_ref[...], 0, 0)

        @pl.when(k == 0)
        def _():
            acc_ref[...] = p

        @pl.when(k > 0)
        def _():
            acc_ref[...] += p

        @pl.when(k == nk - 1)
        def _():
            o_ref[...] = acc_ref[...].astype(BF16)

    return pl.pallas_call(
        dw_body,
        out_shape=jax.ShapeDtypeStruct(gm_grads.shape, BF16),
        grid=(NT, nk),
        in_specs=[pl.BlockSpec(memory_space=pl.ANY),
                  pl.BlockSpec((tk, D), lambda j, k: (k, 0)),
                  pl.BlockSpec((tk, PROJ_TN), lambda j, k: (k, j))],
        out_specs=pl.BlockSpec((None, D, PROJ_TN), lambda j, k: (j // 2, 0, j % 2)),
        scratch_shapes=[pltpu.VMEM((D, PROJ_TN), F32)],
        input_output_aliases={0: 0},
        name=name + "_dw",
        compiler_params=pltpu.CompilerParams(dimension_semantics=("parallel", "arbitrary"),
                                             vmem_limit_bytes=VMEM_LIMIT),
    )(gm_grads, h, dproj)


def in_proj_bwd_dh(dproj, Gm, x_in, g, dres, name):
    S, D = x_in.shape
    tm = 512
    C = Gm.shape[2]
    per = DH_SHARDS_PER_STEP
    n_sh = dproj.shape[1] // (C * per)

    def dh_body(d_ref, w_ref, x_ref, r_ref, g_ref, dx_ref, dg_ref, acc_ref):
        i, k = pl.program_id(0), pl.program_id(1)
        p = _dot(d_ref[:, 0:C], w_ref[0], 1, 1)
        for s in range(1, per):
            p = p + _dot(d_ref[:, s * C:(s + 1) * C], w_ref[s], 1, 1)

        @pl.when(k == 0)
        def _():
            acc_ref[...] = p

        @pl.when(k > 0)
        def _():
            acc_ref[...] += p

        @pl.when(k == n_sh - 1)
        def _():
            dx, dgt = _rms_bwd_tile(acc_ref[...], x_ref[...], g_ref[...])
            dx_ref[...] = r_ref[...] + dx
            dgp = jnp.sum(dgt, axis=0, keepdims=True)

            @pl.when(i == 0)
            def _():
                dg_ref[...] = dgp

            @pl.when(i > 0)
            def _():
                dg_ref[...] += dgp

    dx, dg = _call(name + "_dh", dh_body, (S // tm, n_sh),
                   [(dproj, (tm, per * C), lambda i, k: (i, k)),
                    (Gm, (per, D, C), lambda i, k: (k, 0, 0)),
                    (x_in, (tm, D), lambda i, k: (i, 0)), (dres, (tm, D), lambda i, k: (i, 0)),
                    (g, (1, D), lambda i, k: (0, 0))],
                   [((S, D), F32, (tm, D), lambda i, k: (i, 0)), ((1, D), F32, (1, D), lambda i, k: (0, 0))],
                   scratch=[pltpu.VMEM((tm, D), F32)], sem=("arbitrary", "arbitrary"))
    return dx, dg


def _t5_bucket(rel):
    n = N_BUCKETS // 2
    max_exact = n // 2
    ret = jnp.where(rel > 0, n, 0)
    a = jnp.abs(rel)
    af = jnp.maximum(a, 1).astype(F32)
    large = max_exact + (jnp.log(af / max_exact) / math.log(MAX_DISTANCE / max_exact)
                         * (n - max_exact)).astype(jnp.int32)
    large = jnp.minimum(large, n - 1)
    return ret + jnp.where(a < max_exact, a, large)


def _bucket_tables():
    qi = jnp.arange(A_TQ, dtype=jnp.int32)[:, None]
    kj = jnp.arange(A_WIN, dtype=jnp.int32)[None, :]
    rel = kj - HALF_WINDOW - qi
    return jnp.stack([_t5_bucket(rel * d) for d in DILATIONS], axis=0)


def bias_build(rel_bias, buckets):
    def body(tab_ref, bk_ref, o_ref):
        col = pl.program_id(0) * HEADS_PER_GROUP_A + pl.program_id(1)
        bk = bk_ref[...]
        acc = jnp.zeros(bk.shape, F32)
        for b in range(N_BUCKETS):
            acc = jnp.where(bk == b, tab_ref[b, col], acc)
        qi = lax.broadcasted_iota(jnp.int32, bk.shape, 0)
        kj = lax.broadcasted_iota(jnp.int32, bk.shape, 1)
        o_ref[...] = jnp.where(jnp.abs(kj - HALF_WINDOW - qi) <= HALF_WINDOW, acc, NEG_INF)

    return pl.pallas_call(
        body,
        out_shape=jax.ShapeDtypeStruct((3, HEADS_PER_GROUP_A, A_TQ, A_WIN), F32),
        grid=(3, HEADS_PER_GROUP_A),
        in_specs=[pl.BlockSpec(memory_space=pltpu.SMEM),
                  pl.BlockSpec((None, A_TQ, A_WIN), lambda g, h: (g, 0, 0))],
        out_specs=pl.BlockSpec((None, None, A_TQ, A_WIN), lambda g, h: (g, h, 0, 0)),
        name="a_bias_build",
        compiler_params=pltpu.CompilerParams(dimension_semantics=("parallel", "parallel")),
    )(rel_bias, buckets)


def bias_bwd(dbias, buckets):
    def body(d_ref, bk_ref, o_ref):
        bk = bk_ref[...]
        dv = d_ref[...]
        for b in range(N_BUCKETS):
            part = jnp.sum(jnp.where(bk == b, dv, 0.0), axis=1, keepdims=True)
            o_ref[b:b + 1, :] = jnp.broadcast_to(jnp.sum(part, axis=0, keepdims=True), (1, LANES))

    out = pl.pallas_call(
        body,
        out_shape=jax.ShapeDtypeStruct((3, HEADS_PER_GROUP_A, N_BUCKETS, LANES), F32),
        grid=(3, HEADS_PER_GROUP_A),
        in_specs=[pl.BlockSpec((None, None, A_TQ, A_WIN), lambda g, h: (g, h, 0, 0)),
                  pl.BlockSpec((None, A_TQ, A_WIN), lambda g, h: (g, 0, 0))],
        out_specs=pl.BlockSpec((None, None, N_BUCKETS, LANES), lambda g, h: (g, h, 0, 0)),
        name="a_bias_bwd",
        compiler_params=pltpu.CompilerParams(dimension_semantics=("parallel", "parallel")),
    )(dbias, buckets)
    return out[:, :, :, 0].transpose(2, 0, 1).reshape(N_BUCKETS, 3 * HEADS_PER_GROUP_A)


def _a_fill_padded(pad_ref, src_ref, n, pad):
    zeros = jnp.zeros((pad, LANES), pad_ref.dtype)
    pad_ref[0:pad, :] = zeros
    pad_ref[pad + n:2 * pad + n, :] = zeros
    pad_ref[pad:pad + n, :] = src_ref[...].astype(pad_ref.dtype)


def _a_key_valid(qb, L):
    kidx = qb * A_TQ - HALF_WINDOW + lax.broadcasted_iota(jnp.int32, (A_TQ, A_WIN), 1)
    return (kidx >= 0) & (kidx < L)


def a_fwd(proj_g, bias_g, g, name):
    S = proj_g.shape[0]
    d = DILATIONS[g]
    L = S // d
    nqb = L // A_TQ
    pad = HALF_WINDOW * d

    def body(q_ref, k_ref, v_ref, b_ref, o_ref, l_ref, qf, kpad, vpad):
        qf[...] = q_ref[...].astype(F32)
        _a_fill_padded(kpad, k_ref, S, pad)
        _a_fill_padded(vpad, v_ref, S, pad)
        lane = lax.broadcasted_iota(jnp.int32, (A_TQ, LANES), 1)

        def block(t, carry):
            qb, r = t // d, t % d
            start = qb * (A_TQ * d) + r
            kw = kpad[pl.ds(start, A_WIN, stride=d), :].astype(BF16)
            vw = vpad[pl.ds(start, A_WIN, stride=d), :].astype(BF16)
            q = qf[pl.ds(start, A_TQ, stride=d), :].astype(BF16)
            valid = _a_key_valid(qb, L)
            outs, lses = [], []
            for h in range(2):
                qh = jnp.where((lane >= HEAD_DIM_A * h) & (lane < HEAD_DIM_A * (h + 1)), q, jnp.zeros_like(q))
                s = _dot(qh, kw, 1, 1) * (HEAD_DIM_A ** -0.5) + b_ref[h]
                s = jnp.where(valid, s, NEG_INF)
                m = jnp.max(s, axis=-1, keepdims=True)
                e = jnp.exp(s - m)
                l = jnp.sum(e, axis=-1, keepdims=True)
                outs.append(_dot(e.astype(BF16), vw) / l)
                lses.append(m + jnp.log(l))
            o_ref[pl.ds(start, A_TQ, stride=d), :] = jnp.where(lane < HEAD_DIM_A, outs[0], outs[1])
            l_ref[pl.ds(start, A_TQ, stride=d), :] = jnp.where(lane < HEAD_DIM_A, lses[0], lses[1])
            return carry

        lax.fori_loop(0, nqb * d, block, 0, unroll=A_UNROLL)

    out_spec = ((S, GROUP_WIDTH_A), F32, (S, LANES), lambda hp: (0, hp))
    return _call(name, body, (4,),
                 [(proj_g, (S, LANES), lambda hp: (0, hp)),
                  (proj_g, (S, LANES), lambda hp: (0, 4 + hp)),
                  (proj_g, (S, LANES), lambda hp: (0, 8 + hp)),
                  (bias_g, (2, A_TQ, A_WIN), lambda hp: (hp, 0, 0))],
                 [out_spec, out_spec],
                 scratch=[pltpu.VMEM((S, LANES), F32)] + [pltpu.VMEM((S + 2 * pad, LANES), F32)] * 2,
                 sem=("parallel",))


def a_combine(outs, lses, name):
    S, W = outs[0].shape
    tr = 512

    def body(o0, o1, o2, l0, l1, l2, oa_ref, lt_ref):
        a, b, c = l0[...], l1[...], l2[...]
        m = jnp.maximum(jnp.maximum(a, b), c)
        ea, eb, ec = jnp.exp(a - m), jnp.exp(b - m), jnp.exp(c - m)
        z = ea + eb + ec
        oa_ref[...] = ((ea * o0[...] + eb * o1[...] + ec * o2[...]) / z).astype(BF16)
        lt_ref[...] = m + jnp.log(z)

    spec = ((tr, W), lambda i: (i, 0))
    return _call(name, body, (S // tr,), [(a,) + spec for a in (*outs, *lses)],
                 [((S, W), BF16) + spec, ((S, W), F32) + spec], sem=("parallel",))


def a_bwd(proj_g, bias_g, do_a, o_a, lse_tot, g, name):
    S = proj_g.shape[0]
    d = DILATIONS[g]
    L = S // d
    nqb = L // A_TQ
    pad = HALF_WINDOW * d
    scale = HEAD_DIM_A ** -0.5

    def body(q_ref, k_ref, v_ref, b_ref, do_ref, o_ref, l_ref, dq_ref, dk_ref, dv_ref, db_ref,
             qf, of, dqf, kpad, vpad, dkacc, dvacc):
        qf[...] = q_ref[...].astype(F32)
        of[...] = o_ref[...].astype(F32)
        _a_fill_padded(kpad, k_ref, S, pad)
        _a_fill_padded(vpad, v_ref, S, pad)
        dkacc[...] = jnp.zeros(dkacc.shape, F32)
        dvacc[...] = jnp.zeros(dvacc.shape, F32)
        db_ref[...] = jnp.zeros(db_ref.shape, F32)
        lane = lax.broadcasted_iota(jnp.int32, (A_TQ, LANES), 1)

        def block(t, carry):
            qb, r = t // d, t % d
            start = qb * (A_TQ * d) + r
            rows = pl.ds(start, A_TQ, stride=d)
            win = pl.ds(start, A_WIN, stride=d)
            kw = kpad[win, :].astype(BF16)
            vw = vpad[win, :].astype(BF16)
            q = qf[rows, :].astype(BF16)
            do = do_ref[rows, :]
            ov = of[rows, :]
            lt = l_ref[rows, :]
            valid = _a_key_valid(qb, L)
            dqs = []
            dk_win = jnp.zeros((A_WIN, LANES), F32)
            dv_win = jnp.zeros((A_WIN, LANES), F32)
            for h in range(2):
                mh = (lane >= HEAD_DIM_A * h) & (lane < HEAD_DIM_A * (h + 1))
                qh = jnp.where(mh, q, jnp.zeros_like(q))
                doh = jnp.where(mh, do, 0.0)
                s = _dot(qh, kw, 1, 1) * scale + b_ref[h]
                s = jnp.where(valid, s, NEG_INF)
                p = jnp.exp(s - lt[:, HEAD_DIM_A * h:HEAD_DIM_A * h + 1])
                t = jnp.sum(doh * ov, axis=-1, keepdims=True)
                dob = doh.astype(BF16)
                ds = p * (_dot(dob, vw, 1, 1) - t)
                db_ref[h] += ds
                dsb = (ds * scale).astype(BF16)
                dqs.append(_dot(dsb, kw))
                dk_win = dk_win + _dot(dsb, qh, 0, 0)
                dv_win = dv_win + _dot(p.astype(BF16), dob, 0, 0)
            dqf[rows, :] = jnp.where(lane < HEAD_DIM_A, dqs[0], dqs[1])
            dkacc[win, :] += dk_win
            dvacc[win, :] += dv_win
            return carry

        lax.fori_loop(0, nqb * d, block, 0, unroll=A_UNROLL)
        dq_ref[...] = dqf[...].astype(BF16)
        dk_ref[...] = dkacc[pad:pad + S, :].astype(BF16)
        dv_ref[...] = dvacc[pad:pad + S, :].astype(BF16)

    slab = ((S, LANES), lambda hp: (0, hp))
    oshape = (S, GROUP_WIDTH_A)
    padded = pltpu.VMEM((S + 2 * pad, LANES), F32)
    return _call(
        name, body, (4,),
        [(proj_g, (S, LANES), lambda hp: (0, hp)),
         (proj_g, (S, LANES), lambda hp: (0, 4 + hp)),
         (proj_g, (S, LANES), lambda hp: (0, 8 + hp)),
         (bias_g, (2, A_TQ, A_WIN), lambda hp: (hp, 0, 0)),
         (do_a,) + slab, (o_a,) + slab, (lse_tot,) + slab],
        [(oshape, BF16) + slab, (oshape, BF16) + slab, (oshape, BF16) + slab,
         ((HEADS_PER_GROUP_A, A_TQ, A_WIN), F32, (2, A_TQ, A_WIN), lambda hp: (hp, 0, 0))],
        scratch=[pltpu.VMEM((S, LANES), F32)] * 3 + [padded] * 4,
        sem=("parallel",))


def _rope_tables(S):
    rows = S // GRID_W
    row = jnp.repeat(jnp.arange(rows, dtype=F32), GRID_W)
    col = jnp.tile(jnp.arange(GRID_W, dtype=F32), rows)
    n_freq = HEAD_DIM_B // 4
    freq = ROPE_THETA ** (-jnp.arange(n_freq, dtype=F32) / n_freq)
    ang = jnp.concatenate([row[:, None] * freq, col[:, None] * freq], axis=-1)
    cos, sin = jnp.cos(ang), jnp.sin(ang)
    return jnp.repeat(cos, 2, axis=-1), jnp.stack([-sin, sin], axis=-1).reshape(S, HEAD_DIM_B)


def _swap_pairs(y):
    lane = lax.broadcasted_iota(jnp.int32, y.shape, 1)
    return jnp.where(lane % 2 == 0, pltpu.roll(y, LANES - 1, 1), pltpu.roll(y, 1, 1))


def qkv_prep(proj_b, gains, cos_t, sin_t, name):
    S = proj_b.shape[0]
    ts = 256
    n_rot = N_HEADS_B + N_KV_B
    nh = n_rot + N_KV_B
    W = nh * LANES

    def body(x_ref, g_ref, c_ref, s_ref, o_ref):
        cv, sv = c_ref[...], s_ref[...]
        for hb in range(nh):
            cols = slice(hb * LANES, (hb + 1) * LANES)
            xv = x_ref[:, cols]
            if hb < n_rot:
                r = lax.rsqrt(jnp.mean(xv * xv, axis=-1, keepdims=True) + EPS)
                yv = xv * r * g_ref[:, cols]
                o_ref[:, cols] = (yv * cv + _swap_pairs(yv) * sv).astype(BF16)
            else:
                o_ref[:, cols] = xv.astype(BF16)

    return _call(name, body, (S // ts,),
                 [(proj_b, (ts, W), lambda i: (i, 0)), (gains, (1, W), lambda i: (0, 0)),
                  (cos_t, (ts, LANES), lambda i: (i, 0)), (sin_t, (ts, LANES), lambda i: (i, 0))],
                 [((S, W), BF16, (ts, W), lambda i: (i, 0))],
                 sem=("parallel",))[0]


def qk_prep_bwd(dr, proj_b, col0, gain, cos_t, sin_t, name):
    S, W = dr.shape
    H = W // LANES
    ts = 256
    xb = (col0 * LANES) // W

    def body(d_ref, x_ref, g_ref, c_ref, s_ref, dx_ref, dg_ref):
        i = pl.program_id(0)
        cv, sv, gv = c_ref[...], s_ref[...], g_ref[...]
        dgp = jnp.zeros((1, LANES), F32)
        for hb in range(H):
            cols = slice(hb * LANES, (hb + 1) * LANES)
            dout = d_ref[:, cols]
            dy = dout * cv + _swap_pairs(dout * sv)
            dx, dgt = _rms_bwd_tile(dy, x_ref[:, cols], gv)
            dx_ref[:, cols] = dx.astype(BF16)
            dgp = dgp + jnp.sum(dgt, axis=0, keepdims=True)

        @pl.when(i == 0)
        def _():
            dg_ref[...] = dgp

        @pl.when(i > 0)
        def _():
            dg_ref[...] += dgp

    return _call(name, body, (S // ts,),
                 [(dr, (ts, W), lambda i: (i, 0)), (proj_b, (ts, W), lambda i: (i, xb)),
                  (gain, (1, LANES), lambda i: (0, 0)),
                  (cos_t, (ts, LANES), lambda i: (i, 0)), (sin_t, (ts, LANES), lambda i: (i, 0))],
                 [((S, W), BF16, (ts, W), lambda i: (i, 0)),
                  ((1, LANES), F32, (1, LANES), lambda i: (0, 0))],
                 sem=("arbitrary",))


def _row_sums(x):
    hi = x.astype(BF16)
    lo = (x - hi.astype(F32)).astype(BF16)
    ones = jnp.ones((8, LANES), BF16)
    return (_dot(ones, hi, 1, 1) + _dot(ones, lo, 1, 1))[0:1, :]


def flash_fwd(qkv, name):
    S = qkv.shape[0]
    tq = B_TQ_FWD
    scale = HEAD_DIM_B ** -0.5

    def body(q_ref, k_ref, v_ref, o_ref, l_ref):
        s = _dot(q_ref[...], k_ref[...], 1, 1) * scale
        m = jnp.max(s, axis=-1, keepdims=True)
        e = jnp.exp(s - m)
        l = jnp.sum(e, axis=-1, keepdims=True)
        o_ref[...] = (_dot(e.astype(BF16), v_ref[...]) / l).astype(BF16)
        lse = jnp.broadcast_to(m + jnp.log(l), (tq, LANES))
        l_ref[...] = _row_sums(lse) * (1.0 / LANES)

    head = lambda g, h, i: (i, g * GQA_GROUP_B + h)
    return _call(name, body, (N_KV_B, GQA_GROUP_B, S // tq),
                 [(qkv, (tq, LANES), head),
                  (qkv, (S, LANES), lambda g, h, i: (0, N_HEADS_B + g)),
                  (qkv, (S, LANES), lambda g, h, i: (0, N_HEADS_B + N_KV_B + g))],
                 [((S, N_HEADS_B * LANES), BF16, (tq, LANES), head),
                  ((N_HEADS_B, 1, S), F32, (None, 1, tq), lambda g, h, i: (g * GQA_GROUP_B + h, 0, i))],
                 sem=("parallel", "parallel", "parallel"))


def flash_bwd(qkv, k_t, do_b, o_b, lse, name):
    S = qkv.shape[0]
    tq = B_TQ_BWD
    nq = S // tq
    scale = HEAD_DIM_B ** -0.5

    def body(q_ref, k_ref, v_ref, kt_ref, do_ref, o_ref, l_ref, dq_ref, dk_ref, dv_ref, dkacc, dvacc):
        h, i = pl.program_id(1), pl.program_id(2)

        @pl.when((h == 0) & (i == 0))
        def _():
            dkacc[...] = jnp.zeros(dkacc.shape, F32)
            dvacc[...] = jnp.zeros(dvacc.shape, F32)

        q = q_ref[...]
        do = do_ref[...]
        dob = do.astype(BF16)
        t = _row_sums(do * o_ref[...].astype(F32))
        pt = jnp.exp(_dot(k_ref[...], q, 1, 1) * scale - l_ref[...])
        dst = pt * (_dot(v_ref[...], dob, 1, 1) - t) * scale
        dsb = dst.astype(BF16)
        dvacc[...] += _dot(pt.astype(BF16), dob)
        dkacc[...] += _dot(dsb, q)
        dq_ref[...] = _dot(kt_ref[...], dsb).T

        @pl.when((h == GQA_GROUP_B - 1) & (i == nq - 1))
        def _():
            dk_ref[...] = dkacc[...]
            dv_ref[...] = dvacc[...].astype(BF16)

    head = lambda g, h, i: (i, g * GQA_GROUP_B + h)
    return _call(name, body, (N_KV_B, GQA_GROUP_B, nq),
                 [(qkv, (tq, LANES), head),
                  (qkv, (S, LANES), lambda g, h, i: (0, N_HEADS_B + g)),
                  (qkv, (S, LANES), lambda g, h, i: (0, N_HEADS_B + N_KV_B + g)),
                  (k_t, (LANES, S), lambda g, h, i: (g, 0)),
                  (do_b, (tq, LANES), head), (o_b, (tq, LANES), head),
                  (lse, (None, 1, tq), lambda g, h, i: (g * GQA_GROUP_B + h, 0, i))],
                 [((S, N_HEADS_B * LANES), F32, (tq, LANES), head),
                  ((S, N_KV_B * LANES), F32, (S, LANES), lambda g, h, i: (0, g)),
                  ((S, N_KV_B * LANES), BF16, (S, LANES), lambda g, h, i: (0, g))],
                 scratch=[pltpu.VMEM((S, LANES), F32)] * 2,
                 sem=("parallel", "arbitrary", "arbitrary"))


MERGE_TN = 512


def _mix_rows_spec(Gm, row0, n_slots, slot_map, cols=None, col_map=None):
    C = Gm.shape[2] if cols is None else cols
    cm = (lambda *idx: 0) if col_map is None else col_map
    return (Gm, (n_slots, LANES, C), lambda *idx: (slot_map(*idx), row0 // LANES, cm(*idx)))


def merge_fwd(o_a, o_b, w_a, Gm, proj_b, b_gate, name):
    S = o_a.shape[0]
    D = w_a.shape[1]
    tm, tn = 512, MERGE_TN
    ga0, gb0 = PB_GATE_A // tn, PB_GATE_B // tn

    def body(oa_ref, ob_ref, wa_ref, wb_ref, pa_ref, pb_ref, ba_ref, bb_ref, m_ref, ya_ref, yb_ref):
        ya = _dot(oa_ref[...], wa_ref[...])
        yb = _dot(ob_ref[...], wb_ref[...].reshape(N_DEV * LANES, tn))
        ga = _sigmoid(pa_ref[...] + ba_ref[...])
        gb = _sigmoid(pb_ref[...] + bb_ref[...])
        m_ref[...] = (ga * ya + gb * yb).astype(BF16)
        ya_ref[...] = ya.astype(BF16)
        yb_ref[...] = yb.astype(BF16)

    out = ((S, D), BF16, (tm, tn), lambda j, i: (i, j))
    return _call(name, body, (D // tn, S // tm),
                 [(o_a, (tm, o_a.shape[1]), lambda j, i: (i, 0)), (o_b, (tm, o_b.shape[1]), lambda j, i: (i, 0)),
                  (w_a, (w_a.shape[0], tn), lambda j, i: (0, j)),
                  _mix_rows_spec(Gm, MIX_WB, N_DEV, lambda j, i: 0, cols=tn, col_map=lambda j, i: j),
                  (proj_b, (tm, tn), lambda j, i: (i, ga0 + j)), (proj_b, (tm, tn), lambda j, i: (i, gb0 + j)),
                  (b_gate, (1, tn), lambda j, i: (0, j)), (b_gate, (1, tn), lambda j, i: (0, D // tn + j))],
                 [out, out, out], sem=("parallel", "parallel"))


def out_proj(merged, Gm, x, name):
    S, D = x.shape
    tm, tn = 512, MERGE_TN

    def body(m_ref, w_ref, x_ref, o_ref):
        o_ref[...] = x_ref[...] + _dot(m_ref[...], w_ref[...].reshape(N_DEV * LANES, tn))

    return _call(name, body, (D // tn, S // tm),
                 [(merged, (tm, D), lambda j, i: (i, 0)),
                  _mix_rows_spec(Gm, MIX_WOUT, N_DEV, lambda j, i: 0, cols=tn, col_map=lambda j, i: j),
                  (x, (tm, tn), lambda j, i: (i, j))],
                 [((S, D), F32, (tm, tn), lambda j, i: (i, j))], sem=("parallel", "parallel"))[0]


def merge_bwd(dx2, Gm, ya, yb, proj_b, b_gate, name):
    S, D = dx2.shape
    tm, tn = 512, MERGE_TN
    nn = D // tn
    ga0, gb0 = PB_GATE_A // tn, PB_GATE_B // tn

    def body(d_ref, w_ref, ya_ref, yb_ref, pa_ref, pb_ref, ba_ref, bb_ref, dya_ref, dyb_ref, dg_ref, dbg_ref):
        i = pl.program_id(1)
        dm = _dot(d_ref[...].astype(BF16), w_ref[...].reshape(tn, D), 1, 1)
        ga = _sigmoid(pa_ref[...] + ba_ref[...])
        gb = _sigmoid(pb_ref[...] + bb_ref[...])
        dya_ref[...] = (dm * ga).astype(BF16)
        dyb_ref[...] = (dm * gb).astype(BF16)
        dpa = dm * ya_ref[...].astype(F32) * ga * (1.0 - ga)
        dpb = dm * yb_ref[...].astype(F32) * gb * (1.0 - gb)
        dg_ref[0] = dpa.astype(BF16)
        dg_ref[1] = dpb.astype(BF16)
        sa = jnp.sum(dpa, axis=0, keepdims=True)
        sb = jnp.sum(dpb, axis=0, keepdims=True)

        @pl.when(i == 0)
        def _():
            dbg_ref[0] = sa
            dbg_ref[1] = sb

        @pl.when(i > 0)
        def _():
            dbg_ref[0] += sa
            dbg_ref[1] += sb

    tile = ((tm, tn), lambda j, i: (i, j))
    dya, dyb, dgate, dbg = _call(
        name, body, (nn, S // tm),
        [(dx2, (tm, D), lambda j, i: (i, 0)),
         _mix_rows_spec(Gm, MIX_WOUT, tn // LANES, lambda j, i: j),
         (ya,) + tile, (yb,) + tile,
         (proj_b, (tm, tn), lambda j, i: (i, ga0 + j)), (proj_b, (tm, tn), lambda j, i: (i, gb0 + j)),
         (b_gate, (1, tn), lambda j, i: (0, j)), (b_gate, (1, tn), lambda j, i: (0, nn + j))],
        [((S, D), BF16) + tile, ((S, D), BF16) + tile,
         ((2, S, D), BF16, (2, tm, tn), lambda j, i: (0, i, j)),
         ((2, 1, D), F32, (2, 1, tn), lambda j, i: (0, 0, j))],
        sem=("parallel", "arbitrary"))
    return dya, dyb, dgate, dbg


def matmul_nt(a, b_spec_fn, N, name, tn=512):
    S, K = a.shape
    tm = 512

    def body(a_ref, b_ref, o_ref):
        b = b_ref[...]
        o_ref[...] = _dot(a_ref[...], b.reshape(-1, b.shape[-1]), 1, 1)

    return _call(name, body, (N // tn, S // tm),
                 [(a, (tm, K), lambda j, i: (i, 0)), b_spec_fn(lambda j, i: j)],
                 [((S, N), F32, (tm, tn), lambda j, i: (i, j))], sem=("parallel", "parallel"))[0]


def weight_grad_rows(a, b, grads, row0, name):
    S, M = a.shape
    N = b.shape[1]
    tmm = 512
    tk = WGRAD_TK
    nk = S // tk

    def body(g_ref, a_ref, b_ref, o_ref, acc_ref):
        k = pl.program_id(1)
        p = _dot(a_ref[...], b_ref[...].astype(BF16), 0, 0)

        @pl.when(k == 0)
        def _():
            acc_ref[...] = p

        @pl.when(k > 0)
        def _():
            acc_ref[...] += p

        @pl.when(k == nk - 1)
        def _():
            o_ref[...] = acc_ref[...].astype(BF16).reshape(tmm // LANES, LANES, N)

    return pl.pallas_call(
        body,
        out_shape=jax.ShapeDtypeStruct(grads.shape, BF16),
        grid=(M // tmm, nk),
        in_specs=[pl.BlockSpec(memory_space=pl.ANY),
                  pl.BlockSpec((tk, tmm), lambda j, k: (k, j)),
                  pl.BlockSpec((tk, N), lambda j, k: (k, 0))],
        out_specs=pl.BlockSpec((tmm // LANES, LANES, N), lambda j, k: (j, row0 // LANES, 0)),
        scratch_shapes=[pltpu.VMEM((tmm, N), F32)],
        input_output_aliases={0: 0},
        name=name,
        compiler_params=pltpu.CompilerParams(dimension_semantics=("parallel", "arbitrary"),
                                             vmem_limit_bytes=VMEM_LIMIT),
    )(grads, a, b)


def weight_grad_plain(a, b, name):
    S, M = a.shape
    N = b.shape[1]
    tk = WGRAD_TK
    nk = S // tk

    def body(a_ref, b_ref, o_ref, acc_ref):
        k = pl.program_id(0)
        p = _dot(a_ref[...], b_ref[...], 0, 0)

        @pl.when(k == 0)
        def _():
            acc_ref[...] = p

        @pl.when(k > 0)
        def _():
            acc_ref[...] += p

        @pl.when(k == nk - 1)
        def _():
            o_ref[...] = acc_ref[...].astype(BF16)

    return _call(name, body, (nk,),
                 [(a, (tk, M), lambda k: (k, 0)), (b, (tk, N), lambda k: (k, 0))],
                 [((M, N), BF16, (M, N), lambda k: (0, 0))],
                 scratch=[pltpu.VMEM((M, N), F32)], sem=("arbitrary",))[0]


def local_step(x, tgt, p, get_g1_up, get_g1_down, get_gm, get_g2, emit, start_token):
    S, D = x.shape
    after = lambda t: t[0:1, 0:1]
    buckets = _bucket_tables()
    cos_t, sin_t = _rope_tables(S)
    gains = jnp.concatenate([jnp.tile(p["q_norm"], (1, N_HEADS_B)), jnp.tile(p["k_norm"], (1, N_KV_B)),
                             jnp.ones((1, N_KV_B * LANES), F32)], axis=1)

    n1 = rms_fwd(x, p["ffn1_norm"] + after(start_token), "ffn1_norm")
    bias = bias_build(p["rel_bias"] + after(start_token), buckets)
    g1_up = get_g1_up((n1, bias))
    ab1 = ffn_up(n1, (g1_up, None), "ffn1_up")
    G1 = (g1_up, get_g1_down(ab1))
    x1 = ffn_down(ab1, G1, x, "ffn1_down")

    Gm = get_gm(x1)
    w_a = Gm[:, MIX_WA:MIX_ROWS, :].reshape(N_DEV, GROUP_WIDTH_A, LANES).transpose(1, 0, 2).reshape(GROUP_WIDTH_A, D)
    hm = rms_fwd(x1, p["mix_norm"], "mix_norm")
    n_a = A_QKV_WIDTH // PROJ_TN
    proj_a = [in_proj(hm, Gm, g, 3, BF16, "in_proj_a%d" % g, tile_stride=3) for g in range(3)]
    proj_b = in_proj(hm, Gm, n_a, PB_WIDTH // PROJ_TN, F32, "in_proj_b")

    outs, lses = [], []
    for g in range(3):
        o, l = a_fwd(proj_a[g], bias[g], g, "a_fwd_%d" % g)
        outs.append(o)
        lses.append(l)
    o_a, lse_tot = a_combine(outs, lses, "a_combine")

    qkv = qkv_prep(proj_b, gains, cos_t, sin_t, "qkv_prep")
    k_t = qkv[:, N_HEADS_B * LANES:(N_HEADS_B + N_KV_B) * LANES].T
    o_b, lse_b = flash_fwd(qkv, "flash_fwd")

    merged, ya, yb = merge_fwd(o_a, o_b, w_a, Gm, proj_b, p["b_gate"], "merge_fwd")
    x2 = out_proj(merged, Gm, x1, "out_proj")

    G2 = get_g2(x2)
    n2 = rms_fwd(x2, p["ffn2_norm"], "ffn2_norm")
    ab2 = ffn_up(n2, G2, "ffn2_up")
    x3 = ffn_down(ab2, G2, x2, "ffn2_down")

    loss, dx3, d_final = final_loss(x3, tgt, p["final_norm"], "final_loss")

    dabh2, gw2 = ffn_bwd_weights(dx3, ab2, n2, G2, "ffn2_bwd")
    t2 = emit("ffn2", gw2)
    dx2, d_ffn2_norm = ffn_bwd_input(dabh2, G2, x2, p["ffn2_norm"] + after(t2), dx3, "ffn2_bwd")

    dya, dyb, dgate, dbg = merge_bwd(dx2, Gm, ya, yb, proj_b, p["b_gate"], "merge_bwd")
    gm_grads = jnp.zeros(Gm.shape, BF16)
    gm_grads = weight_grad_rows(merged, dx2, gm_grads, MIX_WOUT, "dw_out")
    gm_grads = weight_grad_rows(o_b, dyb, gm_grads, MIX_WB, "dw_branch_b")
    dw_a = weight_grad_plain(o_a, dya, "dw_branch_a")
    do_a = matmul_nt(dya, lambda jm: (w_a, (MERGE_TN, D), lambda j, i: (jm(j, i), 0)), GROUP_WIDTH_A, "do_a")
    do_b = matmul_nt(dyb, lambda jm: _mix_rows_spec(Gm, MIX_WB, MERGE_TN // LANES, jm), N_HEADS_B * LANES, "do_b")

    dq_r, dk_r, dv_b = flash_bwd(qkv, k_t, do_b, o_b, lse_b, "flash_bwd")
    dq_b, d_q_norm = qk_prep_bwd(dq_r, proj_b, 0, p["q_norm"], cos_t, sin_t, "q_prep_bwd")
    dk_b, d_k_norm = qk_prep_bwd(dk_r, proj_b, N_HEADS_B, p["k_norm"], cos_t, sin_t, "k_prep_bwd")

    dqs, dks, dvs, dbs = [], [], [], []
    for g in range(3):
        dq, dk, dv, db = a_bwd(proj_a[g], bias[g], do_a, o_a, lse_tot, g, "a_bwd_%d" % g)
        dqs.append(dq)
        dks.append(dk)
        dvs.append(dv)
        dbs.append(db)
    d_rel_bias = bias_bwd(jnp.stack(dbs, axis=0), buckets)

    dproj = jnp.concatenate(dqs + dks + dvs + [dq_b, dk_b, dv_b, dgate[0], dgate[1]], axis=1)
    gm_grads = in_proj_bwd_dw(dproj, hm, gm_grads, "in_proj_bwd")
    dw_a_sh = dw_a.reshape(GROUP_WIDTH_A, N_DEV, LANES).transpose(1, 0, 2).reshape(N_DEV, MIX_ROWS - MIX_WA, D)
    gm_grads = lax.dynamic_update_slice(gm_grads, dw_a_sh, (0, MIX_WA, 0))
    tm = emit("mix", gm_grads)
    dx1, d_mix_norm = in_proj_bwd_dh(dproj, Gm, x1, p["mix_norm"] + after(tm), dx2, "in_proj_bwd")

    dabh1, gw1 = ffn_bwd_weights(dx1, ab1, n1, G1, "ffn1_bwd")
    t1 = emit("ffn1", gw1)
    dx0, d_ffn1_norm = ffn_bwd_input(dabh1, G1, x, p["ffn1_norm"] + after(t1), dx1, "ffn1_bwd")

    small = dict(ffn1_norm=d_ffn1_norm, mix_norm=d_mix_norm, b_gate=dbg.reshape(1, 2 * D),
                 q_norm=d_q_norm, k_norm=d_k_norm, rel_bias=d_rel_bias, ffn2_norm=d_ffn2_norm,
                 final_norm=d_final)
    return loss, dx0, small


def _pack_small(t, loss_row):
    row6 = jnp.concatenate([t["q_norm"].reshape(1, -1), t["k_norm"].reshape(1, -1), t["rel_bias"].reshape(1, -1)], axis=1)
    return jnp.concatenate([t["ffn1_norm"].reshape(1, -1), t["mix_norm"].reshape(1, -1), t["b_gate"].reshape(2, -1),
                            t["ffn2_norm"].reshape(1, -1), t["final_norm"].reshape(1, -1), row6, loss_row], axis=0)


def _unpack_small(a, shapes):
    return dict(ffn1_norm=a[0:1].reshape(shapes["ffn1_norm"]), mix_norm=a[1:2].reshape(shapes["mix_norm"]),
                b_gate=a[2:4].reshape(shapes["b_gate"]), ffn2_norm=a[4:5].reshape(shapes["ffn2_norm"]),
                final_norm=a[5].reshape(shapes["final_norm"]), q_norm=a[6:7, 0:128].reshape(shapes["q_norm"]),
                k_norm=a[6:7, 128:256].reshape(shapes["k_norm"]), rel_bias=a[6, 256:1024].reshape(shapes["rel_bias"]))


SMALL = ("ffn1_norm", "mix_norm", "b_gate", "q_norm", "k_norm", "rel_bias", "ffn2_norm", "final_norm")
ORDER = ("ffn1_norm", "ffn1_w1", "ffn1_w3", "ffn1_w2", "mix_norm", "w_in", "b_gate", "q_norm", "k_norm", "rel_bias",
         "w_branch_a", "w_branch_b", "w_out", "ffn2_norm", "ffn2_w1", "ffn2_w3", "ffn2_w2", "final_norm")


def kernel(x, ffn1_norm, ffn1_w1, ffn1_w3, ffn1_w2, mix_norm, w_in, b_gate, q_norm, k_norm, rel_bias, w_branch_a, w_branch_b, w_out, ffn2_norm, ffn2_w1, ffn2_w3, ffn2_w2, final_norm, loss_target, m_ffn1_norm, m_ffn1_w1, m_ffn1_w3, m_ffn1_w2, m_mix_norm, m_w_in, m_b_gate, m_q_norm, m_k_norm, m_rel_bias, m_w_branch_a, m_w_branch_b, m_w_out, m_ffn2_norm, m_ffn2_w1, m_ffn2_w3, m_ffn2_w2, m_final_norm, v_ffn1_norm, v_ffn1_w1, v_ffn1_w3, v_ffn1_w2, v_mix_norm, v_w_in, v_b_gate, v_q_norm, v_k_norm, v_rel_bias, v_w_branch_a, v_w_branch_b, v_w_out, v_ffn2_norm, v_ffn2_w1, v_ffn2_w3, v_ffn2_w2, v_final_norm):
    args = dict(locals())
    w = {n: args[n] for n in ORDER}
    m = {n: args["m_" + n] for n in ORDER}
    v = {n: args["v_" + n] for n in ORDER}
    D = x.shape[2]

    blocks = (
        ("ffn1_up", lambda t: jnp.concatenate([ffn1_w1[0].T + t, ffn1_w3[0].T + t], axis=0)),
        ("ffn1_down", lambda t: ffn1_w2[0] + t),
        ("mix", lambda t: jnp.concatenate([w_in[0] + t, w_branch_b[0] + t, w_out[0] + t,
                                           w_branch_a[0].reshape(MIX_ROWS - MIX_WA, D) + t], axis=0)),
        ("ffn2", lambda t: jnp.concatenate([ffn2_w1[0].T + t, ffn2_w3[0].T + t, ffn2_w2[0] + t], axis=0)),
    )
    gathers = {}
    start_token = jnp.zeros((8, LANES), F32)
    for tag, make in blocks:
        gathers[tag] = all_gather_start(make(start_token[0:1, 0:1]).astype(BF16), "all_gather_" + tag + "_start")
        start_token = gathers[tag][4]

    def gathered(tag):
        def get(after):
            return all_gather_finish(*_split_wait("all_gather_" + tag + "_wait", gathers[tag], 4, after),
                                     "all_gather_" + tag + "_finish")
        return get

    core = lax.axis_index("c").astype(jnp.int32).reshape(1)
    chip = (2 * lax.axis_index("x") + lax.axis_index("y")).astype(jnp.int32).reshape(1)
    exchanges = {}

    def emit(tag, gw):
        (theirs,) = reduce_scatter_pair([gw], "reduce_scatter_pair_" + tag)
        part = pair_add(gw, theirs, core, "pair_add_" + tag)
        exchanges[tag] = reduce_scatter_start(part, "reduce_scatter_" + tag + "_start")
        return exchanges[tag][4]

    small_p = dict(ffn1_norm=ffn1_norm, mix_norm=mix_norm, b_gate=b_gate, q_norm=q_norm, k_norm=k_norm,
                   rel_bias=rel_bias, ffn2_norm=ffn2_norm, final_norm=final_norm.reshape(1, D))
    loss_p, grad_x, small_g = local_step(x[0], loss_target[0], small_p, gathered("ffn1_up"), gathered("ffn1_down"),
                                         gathered("mix"), gathered("ffn2"), emit, start_token)

    def landed(tag, after):
        return _split_wait("reduce_scatter_" + tag + "_wait", exchanges[tag], 3, after)

    grads, delta, new_m, new_v = {}, {}, {}, {}

    def finish(n, part, land, off, blk, transposed=False):
        shp = w[n].shape
        if transposed:
            to2 = lambda a: a.reshape(shp[-2], shp[-1]).T
            back = lambda a: a.T.reshape(shp)
        else:
            to2 = lambda a: a.reshape(shp[-2], shp[-1])
            back = lambda a: a.reshape(shp)
        res = sum_adamw(part, land, chip, off, blk, to2(w[n]), to2(m[n]), to2(v[n]), "update_" + n)
        grads[n], delta[n], new_m[n], new_v[n] = [back(a) for a in res]

    last_token = exchanges["ffn1"][4]
    for tag, after in (("ffn2", last_token), ("ffn1", grad_x)):
        part, land = landed(tag, after)
        finish(tag + "_w1", part, land, 0, FFN_SHARD, transposed=True)
        finish(tag + "_w3", part, land, FFN_SHARD, FFN_SHARD, transposed=True)
        finish(tag + "_w2", part, land, 2 * FFN_SHARD, FFN_SHARD)
        if tag == "ffn2":
            part_m, land_m = landed("mix", last_token)
            finish("w_in", part_m, land_m, MIX_WIN, LANES)
            finish("w_branch_b", part_m, land_m, MIX_WB, LANES)
            finish("w_out", part_m, land_m, MIX_WOUT, LANES)
            grads["w_branch_a"] = sum_chips(part_m, land_m, chip, MIX_WA, MIX_ROWS - MIX_WA, MIX_ROWS - MIX_WA,
                                            "w_branch_a_sum").reshape(w_branch_a.shape)
    loss_row = jnp.pad(loss_p, ((0, 0), (0, D - LANES)))
    smalls = small_all_gather(_pack_small(small_g, loss_row))
    small_sum = sum_slots(smalls, 0, N_DEV, N_DEV, "small_sum")
    small_shapes = {n: w[n].shape for n in SMALL}
    grads.update(_unpack_small(small_sum, small_shapes))
    loss = small_sum[7, 0]

    n = "w_branch_a"
    two_d = lambda a: a.reshape(w[n].shape[-2], w[n].shape[-1])
    d_, m_, v_ = adamw(two_d(w[n]), two_d(grads[n]), two_d(m[n]), two_d(v[n]), "adamw_" + n)
    delta[n], new_m[n], new_v[n] = [a.reshape(w[n].shape) for a in (d_, m_, v_)]
    zero_row = jnp.zeros((1, D), F32)
    pack = lambda t: _pack_small({n: t[n] for n in SMALL}, zero_row)
    d_, m_, v_ = adamw(pack(w), small_sum, pack(m), pack(v), "adamw_small")
    for src, dst in ((d_, delta), (m_, new_m), (v_, new_v)):
        dst.update(_unpack_small(src, small_shapes))

    return (loss, grad_x[None], *[grads[n] for n in ORDER], *[delta[n] for n in ORDER],
            *[new_m[n] for n in ORDER], *[new_v[n] for n in ORDER])
```

```python
import math

import jax
import jax.numpy as jnp
from jax import lax
from jax.experimental import pallas as pl
from jax.experimental.pallas import tpu as pltpu

F32 = jnp.float32
BF16 = jnp.bfloat16
MESH = pl.DeviceIdType.MESH

V7X_VMEM_BYTES = 64 * 1024 * 1024
VMEM_LIMIT = V7X_VMEM_BYTES - 8 * 1024 * 1024
LANES = 128

N_DEV = 8
EPS = 1e-6
NEG_INF = -1e30

DILATIONS = (1, 4, 16)
HALF_WINDOW = 64
HEAD_DIM_A = 64
HEADS_PER_GROUP_A = 8
GROUP_WIDTH_A = 512
A_QKV_WIDTH = 4608
A_GROUP_QKV = A_QKV_WIDTH // 3
A_TQ = 128
A_WIN = A_TQ + 2 * HALF_WINDOW
A_UNROLL = 8
A_SCALE = HEAD_DIM_A ** -0.5
WGRAD_TK = 2048
HEAD_DIM_B = 128
N_HEADS_B = 8
N_KV_B = 2
GQA_GROUP_B = 4
GRID_W = 64
ROPE_THETA = 10000.0
B_TQ_FWD = 256
B_TQ_BWD = 512
N_BUCKETS = 32
MAX_DISTANCE = 1024
PB_WIDTH = 3584
PB_GATE_A = 1536
PB_GATE_B = 2560

ADAM_LR = 0.001
ADAM_B1 = 0.9
ADAM_B2 = 0.999
ADAM_EPS = 1e-08
ADAM_WD = 0.01
ADAM_STEP = 10

FFN_SHARD = 352
MIX_WIN, MIX_WB, MIX_WOUT, MIX_WA = 0, 1024, 1152, 1280
MIX_ROWS = 1344


def _dot(a, b, ca=1, cb=0):
    return lax.dot_general(a, b, (((ca,), (cb,)), ((), ())), preferred_element_type=F32)


def _call(name, body, grid, ins, outs, scratch=(), sem=None, aliases=None):
    res = pl.pallas_call(
        body,
        out_shape=[jax.ShapeDtypeStruct(s, d) for (s, d, _, _) in outs],
        grid=grid,
        in_specs=[pl.BlockSpec(bs, im) for (_, bs, im) in ins],
        out_specs=[pl.BlockSpec(bs, im) for (_, _, bs, im) in outs],
        scratch_shapes=list(scratch),
        name=name,
        input_output_aliases=aliases or {},
        compiler_params=pltpu.CompilerParams(dimension_semantics=sem, vmem_limit_bytes=VMEM_LIMIT),
    )(*[a for (a, _, _) in ins])
    return res


def _sigmoid(x):
    return 1.0 / (1.0 + jnp.exp(-x))


def _position():
    return lax.axis_index("x"), lax.axis_index("y"), lax.axis_index("c")


def _hbm_specs(n):
    return [pl.BlockSpec(memory_space=pl.ANY) for _ in range(n)]


PAIR_BUFFERS = 4


def reduce_scatter_pair(grads, name):
    n = len(grads)
    C = grads[0].shape[2]
    half = [g.shape[1] // 2 for g in grads]
    chunks = [(i, q, hf) for i in range(n) for q in range(4) for hf in range(2)]
    nb = PAIR_BUFFERS

    def body(*refs):
        ins, theirs = refs[:n], refs[n:2 * n]
        buf, load_sems, send_sems, recv_sems = refs[2 * n:]
        x, y, c = _position()
        sibling = (x, y, 1 - c)

        def load(k):
            i, q, hf = chunks[k]
            r = half[i]
            return pltpu.make_async_copy(ins[i].at[2 * q + (1 - c), pl.ds(hf * r, r), :],
                                         buf.at[k % nb, pl.ds(0, r), :], load_sems.at[k % nb])

        def send(k):
            i, q, hf = chunks[k]
            r = half[i]
            return pltpu.make_async_remote_copy(
                src_ref=buf.at[k % nb, pl.ds(0, r), :], dst_ref=theirs[i].at[q, pl.ds(hf * r, r), :],
                send_sem=send_sems.at[k % nb], recv_sem=recv_sems.at[i],
                device_id=sibling, device_id_type=MESH)

        for k in range(len(chunks) + 1):
            if k < len(chunks):
                if k >= nb:
                    send(k - nb).wait_send()
                load(k).start()
            if k >= 1:
                load(k - 1).wait()
                send(k - 1).start()
        for k in range(max(0, len(chunks) - nb), len(chunks)):
            send(k).wait_send()
        for i in range(n):
            pltpu.make_async_remote_copy(
                src_ref=theirs[i], dst_ref=theirs[i], send_sem=send_sems.at[0], recv_sem=recv_sems.at[i],
                device_id=sibling, device_id_type=MESH).wait_recv()

    return pl.pallas_call(
        body,
        out_shape=[jax.ShapeDtypeStruct((4,) + g.shape[1:], g.dtype) for g in grads],
        in_specs=_hbm_specs(n),
        out_specs=_hbm_specs(n),
        scratch_shapes=[pltpu.VMEM((nb, max(half), C), grads[0].dtype), pltpu.SemaphoreType.DMA((nb,)),
                        pltpu.SemaphoreType.DMA((nb,)), pltpu.SemaphoreType.DMA((n,))],
        name=name,
        compiler_params=pltpu.CompilerParams(vmem_limit_bytes=VMEM_LIMIT),
    )(*grads)


_HBM_SPEC = pl.BlockSpec(memory_space=pltpu.HBM)
_SEM_SPEC = pl.BlockSpec(memory_space=pltpu.SEMAPHORE)
_TOKEN_SPEC = pl.BlockSpec(memory_space=pltpu.VMEM)
_DATAFLOW = pltpu.SideEffectType.DATAFLOW_SIDE_EFFECTING


def _split_start(name, body, src, land_shape):
    def full_body(src_ref, land_ref, send_sem, recv_sem, src_thru, land_thru, token):
        body(src_ref, land_ref, send_sem, recv_sem)
        token[...] = jnp.zeros_like(token)

    land = pltpu.with_memory_space_constraint(lax.empty(land_shape, src.dtype), pltpu.HBM)
    return pl.pallas_call(
        full_body, name=name,
        out_shape=(pltpu.SemaphoreType.DMA(()), pltpu.SemaphoreType.DMA(()),
                   pltpu.HBM(src.shape, src.dtype), pltpu.HBM(land_shape, src.dtype),
                   jax.ShapeDtypeStruct((8, LANES), F32)),
        in_specs=(_HBM_SPEC, _HBM_SPEC),
        out_specs=(_SEM_SPEC, _SEM_SPEC, _HBM_SPEC, _HBM_SPEC, _TOKEN_SPEC),
        input_output_aliases={0: 2, 1: 3},
        compiler_params=pltpu.CompilerParams(has_side_effects=_DATAFLOW),
    )(pltpu.with_memory_space_constraint(src, pltpu.HBM), land)


def _split_wait(name, started, n_blocks, after):
    send_sem, recv_sem, src_thru, land_thru, _ = started
    after = after if isinstance(after, tuple) else (after,)

    def body(src_ref, land_ref, send_sem, recv_sem, *rest):
        x, y, c = _position()
        blocks = land_ref.at[pl.ds(0, n_blocks)]
        copy = pltpu.make_async_remote_copy(src_ref=blocks, dst_ref=blocks, send_sem=send_sem, recv_sem=recv_sem,
                                            device_id=(x, y, c), device_id_type=MESH)
        copy.wait_send()
        copy.wait_recv()

    return pl.pallas_call(
        body, name=name,
        out_shape=(pltpu.HBM(src_thru.shape, src_thru.dtype), pltpu.HBM(land_thru.shape, land_thru.dtype)),
        in_specs=(_HBM_SPEC, _HBM_SPEC, _SEM_SPEC, _SEM_SPEC) + (pl.BlockSpec(memory_space=pl.ANY),) * len(after),
        out_specs=(_HBM_SPEC, _HBM_SPEC),
        input_output_aliases={0: 0, 1: 1},
        compiler_params=pltpu.CompilerParams(has_side_effects=_DATAFLOW),
    )(src_thru, land_thru, send_sem, recv_sem, *after)


def all_gather_start(block, name):
    def body(b_ref, land_ref, send_sem, recv_sem):
        x, y, c = _position()
        for peer in [(x, y, 1 - c), (1 - x, y, c), (x, 1 - y, c), (1 - x, 1 - y, c)]:
            pltpu.make_async_remote_copy(src_ref=b_ref, dst_ref=land_ref.at[4 * x + 2 * y + c],
                                         send_sem=send_sem, recv_sem=recv_sem,
                                         device_id=peer, device_id_type=MESH).start()

    return _split_start(name, body, block, (N_DEV,) + block.shape)


def all_gather_finish(block, land, name):
    R, C = block.shape

    def body(b_ref, land_in, land_ref, stage, load_sems, send_sems, recv_sems, own_sem):
        x, y, c = _position()
        sibling = (x, y, 1 - c)
        chips = [(1 - x, y), (x, 1 - y), (1 - x, 1 - y)]
        own_in = pltpu.make_async_copy(b_ref, stage.at[3], load_sems.at[3])
        own_in.start()
        loads = [pltpu.make_async_copy(land_in.at[4 * px + 2 * py + c], stage.at[j], load_sems.at[j])
                 for j, (px, py) in enumerate(chips)]
        for ld in loads:
            ld.start()
        sends = []
        for j, (px, py) in enumerate(chips):
            loads[j].wait()
            dst = land_ref.at[4 * px + 2 * py + c]
            cp = pltpu.make_async_remote_copy(src_ref=stage.at[j], dst_ref=dst, send_sem=send_sems.at[j],
                                              recv_sem=recv_sems.at[j], device_id=sibling, device_id_type=MESH)
            cp.start()
            sends.append(cp)
        own_in.wait()
        own_out = pltpu.make_async_copy(stage.at[3], land_ref.at[4 * x + 2 * y + c], own_sem)
        own_out.start()
        for j, (px, py) in enumerate(chips):
            dst = land_ref.at[4 * px + 2 * py + (1 - c)]
            pltpu.make_async_remote_copy(src_ref=stage.at[j], dst_ref=dst, send_sem=send_sems.at[j],
                                         recv_sem=recv_sems.at[j], device_id=sibling,
                                         device_id_type=MESH).wait_recv()
        for cp in sends:
            cp.wait_send()
        own_out.wait()

    return pl.pallas_call(
        body,
        out_shape=jax.ShapeDtypeStruct(land.shape, land.dtype),
        in_specs=_hbm_specs(2),
        out_specs=pl.BlockSpec(memory_space=pl.ANY),
        scratch_shapes=[pltpu.VMEM((4, R, C), block.dtype), pltpu.SemaphoreType.DMA((4,)),
                        pltpu.SemaphoreType.DMA((3,)), pltpu.SemaphoreType.DMA((3,)), pltpu.SemaphoreType.DMA],
        input_output_aliases={1: 0},
        name=name,
        compiler_params=pltpu.CompilerParams(vmem_limit_bytes=VMEM_LIMIT),
    )(block, land)


def reduce_scatter_start(parts, name):
    def body(p_ref, land_ref, send_sem, recv_sem):
        x, y, c = _position()
        for px, py in [(1 - x, y), (x, 1 - y), (1 - x, 1 - y)]:
            pltpu.make_async_remote_copy(src_ref=p_ref.at[2 * px + py], dst_ref=land_ref.at[2 * x + y],
                                         send_sem=send_sem, recv_sem=recv_sem,
                                         device_id=(px, py, c), device_id_type=MESH).start()

    return _split_start(name, body, parts, parts.shape)


def small_all_gather(small):
    def body(small_ref, smalls, s_send, s_recv, s_local):
        x, y, c = _position()
        me = 4 * x + 2 * y + c
        lc = pltpu.make_async_copy(small_ref, smalls.at[me], s_local)
        lc.start()
        remote = []
        k = 0
        for dx in (0, 1):
            for dy in (0, 1):
                for dc in (0, 1):
                    if dx + dy + dc == 0:
                        continue
                    peer = (1 - x if dx else x, 1 - y if dy else y, 1 - c if dc else c)
                    rc = pltpu.make_async_remote_copy(
                        src_ref=small_ref, dst_ref=smalls.at[me],
                        send_sem=s_send.at[k], recv_sem=s_recv.at[k],
                        device_id=peer, device_id_type=MESH)
                    rc.start()
                    remote.append(rc)
                    k += 1
        for rc in remote:
            rc.wait()
        lc.wait()

    return pl.pallas_call(
        body,
        out_shape=jax.ShapeDtypeStruct((N_DEV,) + small.shape, small.dtype),
        in_specs=_hbm_specs(1),
        out_specs=pl.BlockSpec(memory_space=pl.ANY),
        scratch_shapes=[pltpu.SemaphoreType.DMA((7,)), pltpu.SemaphoreType.DMA((7,)), pltpu.SemaphoreType.DMA],
        name="small_all_gather",
    )(small)


def pair_add(grads, theirs, core, name):
    _, R, C = theirs.shape
    tr = R // 2

    def body(c_ref, a_ref, b_ref, o_ref):
        o_ref[...] = (a_ref[...].astype(F32) + b_ref[...].astype(F32)).astype(BF16)

    return pl.pallas_call(
        body,
        out_shape=jax.ShapeDtypeStruct(theirs.shape, BF16),
        grid_spec=pltpu.PrefetchScalarGridSpec(
            num_scalar_prefetch=1, grid=(4, R // tr),
            in_specs=[pl.BlockSpec((None, tr, C), lambda q, i, c: (2 * q + c[0], i, 0)),
                      pl.BlockSpec((None, tr, C), lambda q, i, c: (q, i, 0))],
            out_specs=pl.BlockSpec((None, tr, C), lambda q, i, c: (q, i, 0))),
        name=name,
        compiler_params=pltpu.CompilerParams(dimension_semantics=("parallel", "parallel"),
                                             vmem_limit_bytes=VMEM_LIMIT),
    )(core, grads, theirs)


def sum_slots(recv, off, rows, blk, name):
    nq, _, C = recv.shape
    ob = off // blk

    def body(r_ref, o_ref):
        acc = r_ref[0].astype(F32)
        for q in range(1, nq):
            acc = acc + r_ref[q].astype(F32)
        o_ref[...] = acc

    return _call(name, body, (rows // blk,),
                 [(recv, (nq, blk, C), lambda i: (0, ob + i, 0))],
                 [((rows, C), F32, (blk, C), lambda i: (i, 0))], sem=("parallel",))[0]


def sum_chips(parts, land, chip, off, rows, blk, name):
    C = parts.shape[2]
    ob = off // blk

    def body(c_ref, own_ref, a_ref, b_ref, d_ref, o_ref):
        o_ref[...] = ((own_ref[...].astype(F32) + a_ref[...].astype(F32)) + b_ref[...].astype(F32)) \
            + d_ref[...].astype(F32)

    def entry(flip):
        return pl.BlockSpec((None, blk, C), lambda i, c: (c[0] ^ flip, ob + i, 0))

    return pl.pallas_call(
        body,
        out_shape=jax.ShapeDtypeStruct((rows, C), F32),
        grid_spec=pltpu.PrefetchScalarGridSpec(
            num_scalar_prefetch=1, grid=(rows // blk,),
            in_specs=[entry(0), entry(1), entry(2), entry(3)],
            out_specs=pl.BlockSpec((blk, C), lambda i, c: (i, 0))),
        name=name,
        compiler_params=pltpu.CompilerParams(dimension_semantics=("parallel",), vmem_limit_bytes=VMEM_LIMIT),
    )(chip, parts, land, land, land)


def _adamw_update(wv, gv, mv, vv):
    nm = ADAM_B1 * mv + (1.0 - ADAM_B1) * gv
    nv = ADAM_B2 * vv + (1.0 - ADAM_B2) * (gv * gv)
    c1 = 1.0 / (1.0 - ADAM_B1 ** ADAM_STEP)
    c2 = 1.0 / (1.0 - ADAM_B2 ** ADAM_STEP)
    return -ADAM_LR * ((nm * c1) / (jnp.sqrt(nv * c2) + ADAM_EPS) + ADAM_WD * wv), nm, nv


def sum_adamw(parts, land, chip, off, blk, w, m, v, name):
    rows, C = w.shape
    ob = off // blk

    def body(c_ref, own_ref, a_ref, b_ref, d_ref, w_ref, m_ref, v_ref, g_out, d_out, m_out, v_out):
        gv = ((own_ref[...].astype(F32) + a_ref[...].astype(F32)) + b_ref[...].astype(F32)) \
            + d_ref[...].astype(F32)
        g_out[...] = gv
        d_out[...], m_out[...], v_out[...] = _adamw_update(w_ref[...], gv, m_ref[...], v_ref[...])

    def entry(flip):
        return pl.BlockSpec((None, blk, C), lambda i, c: (c[0] ^ flip, ob + i, 0))

    plain = pl.BlockSpec((blk, C), lambda i, c: (i, 0))
    return pl.pallas_call(
        body,
        out_shape=[jax.ShapeDtypeStruct((rows, C), F32)] * 4,
        grid_spec=pltpu.PrefetchScalarGridSpec(
            num_scalar_prefetch=1, grid=(rows // blk,),
            in_specs=[entry(0), entry(1), entry(2), entry(3), plain, plain, plain],
            out_specs=[plain] * 4),
        name=name,
        compiler_params=pltpu.CompilerParams(dimension_semantics=("parallel",), vmem_limit_bytes=VMEM_LIMIT),
    )(chip, parts, land, land, land, w, m, v)


def adamw(w, g, m, v, name):
    R, C = w.shape
    tr = R
    for cand in (256, 128, 64, 32, 16, 8):
        if R % cand == 0 and R > cand:
            tr = cand
            break

    def body(w_ref, g_ref, m_ref, v_ref, d_ref, nm_ref, nv_ref):
        d_ref[...], nm_ref[...], nv_ref[...] = _adamw_update(w_ref[...], g_ref[...], m_ref[...], v_ref[...])

    spec = ((tr, C), lambda i: (i, 0))
    out = ((R, C), F32) + spec
    return _call(name, body, (R // tr,), [(w,) + spec, (g,) + spec, (m,) + spec, (v,) + spec],
                 [out, out, out], sem=("parallel",))


def rms_fwd(x, g, name):
    S, D = x.shape
    tr = 512

    def body(x_ref, g_ref, o_ref):
        xv = x_ref[...]
        r = lax.rsqrt(jnp.mean(xv * xv, axis=-1, keepdims=True) + EPS)
        o_ref[...] = (xv * r * g_ref[...]).astype(BF16)

    return _call(name, body, (S // tr,),
                 [(x, (tr, D), lambda i: (i, 0)), (g, (1, D), lambda i: (0, 0))],
                 [((S, D), BF16, (tr, D), lambda i: (i, 0))], sem=("parallel",))[0]


def _rms_bwd_tile(dn, xv, gv):
    r = lax.rsqrt(jnp.mean(xv * xv, axis=-1, keepdims=True) + EPS)
    xh = xv * r
    dxh = dn * gv
    dx = r * (dxh - xh * jnp.mean(dxh * xh, axis=-1, keepdims=True))
    return dx, dn * xh


def final_loss(x, tgt, g, name):
    S, D = x.shape
    tr = 256

    def body(x_ref, t_ref, g_ref, l_ref, dx_ref, dg_ref):
        i = pl.program_id(0)
        xv, gv = x_ref[...], g_ref[...]
        r = lax.rsqrt(jnp.mean(xv * xv, axis=-1, keepdims=True) + EPS)
        xh = xv * r
        e = xh * gv - t_ref[...]
        part = 0.5 * jnp.sum(jnp.sum(e * e, axis=-1, keepdims=True) * (1.0 / D), axis=0, keepdims=True)
        dy = e * (1.0 / D)
        dxh = dy * gv
        dx_ref[...] = r * (dxh - xh * jnp.mean(dxh * xh, axis=-1, keepdims=True))
        dgp = jnp.sum(dy * xh, axis=0, keepdims=True)

        @pl.when(i == 0)
        def _():
            l_ref[...] = jnp.broadcast_to(part, l_ref.shape)
            dg_ref[...] = dgp

        @pl.when(i > 0)
        def _():
            l_ref[...] += jnp.broadcast_to(part, l_ref.shape)
            dg_ref[...] += dgp

    row = ((tr, D), lambda i: (i, 0))
    return _call(name, body, (S // tr,),
                 [(x,) + row, (tgt,) + row, (g, (1, D), lambda i: (0, 0))],
                 [((1, LANES), F32, (1, LANES), lambda i: (0, 0)), ((S, D), F32) + row,
                  ((1, D), F32, (1, D), lambda i: (0, 0))], sem=("arbitrary",))


FFN_TF = 4 * FFN_SHARD


def _ffn_pick(G, which):
    if isinstance(G, tuple):
        return (G[0], which) if which < 2 else (G[1], 0)
    return G, which


def _ffn_w_spec(G, which, imap):
    arr, blk = _ffn_pick(G, which)
    return (arr, (4, FFN_SHARD, arr.shape[2]), lambda *idx: (imap(*idx), blk, 0))


def _ffn_whole_w_spec(G, which):
    arr, blk = _ffn_pick(G, which)
    return (arr, (N_DEV, FFN_SHARD, arr.shape[2]), lambda *idx: (0, blk, 0))


def ffn_up(n, G, name):
    S, D = n.shape
    F = N_DEV * FFN_SHARD
    tm = 1024

    def body(n_ref, w1_ref, w3_ref, abh_ref):
        nv = n_ref[...]
        a = _dot(nv, w1_ref[...].reshape(FFN_TF, D), 1, 1).astype(BF16)
        b = _dot(nv, w3_ref[...].reshape(FFN_TF, D), 1, 1).astype(BF16)
        abh_ref[0] = a
        abh_ref[1] = b
        av, bv = a.astype(F32), b.astype(F32)
        abh_ref[2] = (av * _sigmoid(av) * bv).astype(BF16)

    return _call(name, body, (F // FFN_TF, S // tm),
                 [(n, (tm, D), lambda j, i: (i, 0)),
                  _ffn_w_spec(G, 0, lambda j, i: j), _ffn_w_spec(G, 1, lambda j, i: j)],
                 [((3, S, F), BF16, (3, tm, FFN_TF), lambda j, i: (0, i, j))],
                 sem=("parallel", "parallel"))[0]


def ffn_down(abh, G, x, name):
    _, S, F = abh.shape
    D = x.shape[1]
    tm = 512

    def body(h_ref, w2_ref, x_ref, o_ref):
        o_ref[...] = x_ref[...] + 0.5 * _dot(h_ref[...], w2_ref[...].reshape(F, D))

    return _call(name, body, (S // tm,),
                 [(abh, (None, tm, F), lambda i: (2, i, 0)), _ffn_whole_w_spec(G, 2),
                  (x, (tm, D), lambda i: (i, 0))],
                 [((S, D), F32, (tm, D), lambda i: (i, 0))], sem=("parallel",))[0]


def ffn_bwd_weights(dxo, abh, n, G, name):
    _, S, F = abh.shape
    D = dxo.shape[1]
    tm = 512
    nf = F // FFN_TF

    def down_body(d_ref, w2_ref, ab_ref, o_ref):
        dh = 0.5 * _dot(d_ref[...].astype(BF16), w2_ref[...].reshape(FFN_TF, D), 1, 1)
        av, bv = ab_ref[0].astype(F32), ab_ref[1].astype(F32)
        sig = _sigmoid(av)
        o_ref[0] = (dh * bv * (sig * (1.0 + av * (1.0 - sig)))).astype(BF16)
        o_ref[1] = (dh * (av * sig)).astype(BF16)

    dab = _call(name + "_down_bwd", down_body, (nf, S // tm),
                [(dxo, (tm, D), lambda j, i: (i, 0)), _ffn_w_spec(G, 2, lambda j, i: j),
                 (abh, (2, tm, FFN_TF), lambda j, i: (0, i, j))],
                [((2, S, F), BF16, (2, tm, FFN_TF), lambda j, i: (0, i, j))],
                sem=("parallel", "parallel"))[0]

    tk = WGRAD_TK
    nk = S // tk
    gshape = (N_DEV, 3 * FFN_SHARD, D)

    def dw2_body(h_ref, d_ref, o_ref, acc_ref):
        k = pl.program_id(1)
        p = _dot(h_ref[...], d_ref[...].astype(BF16), 0, 0)

        @pl.when(k == 0)
        def _():
            acc_ref[...] = p

        @pl.when(k > 0)
        def _():
            acc_ref[...] += p

        @pl.when(k == nk - 1)
        def _():
            o_ref[...] = (0.5 * acc_ref[...]).astype(BF16).reshape(4, FFN_SHARD, D)

    gw = _call(name + "_dw2", dw2_body, (nf, nk),
               [(abh, (None, tk, FFN_TF), lambda j, k: (2, k, j)), (dxo, (tk, D), lambda j, k: (k, 0))],
               [(gshape, BF16, (4, FFN_SHARD, D), lambda j, k: (j, 2, 0))],
               scratch=[pltpu.VMEM((FFN_TF, D), F32)], sem=("parallel", "arbitrary"))[0]

    def dw13_body(gw_ref, dab_ref, n_ref, o_ref, acc_ref):
        k = pl.program_id(2)
        p = _dot(dab_ref[...], n_ref[...], 0, 0)

        @pl.when(k == 0)
        def _():
            acc_ref[...] = p

        @pl.when(k > 0)
        def _():
            acc_ref[...] += p

        @pl.when(k == nk - 1)
        def _():
            o_ref[...] = acc_ref[...].astype(BF16).reshape(4, FFN_SHARD, D)

    gw = pl.pallas_call(
        dw13_body,
        out_shape=jax.ShapeDtypeStruct(gshape, BF16),
        grid=(2, nf, nk),
        in_specs=[pl.BlockSpec(memory_space=pl.ANY),
                  pl.BlockSpec((None, tk, FFN_TF), lambda w, j, k: (w, k, j)),
                  pl.BlockSpec((tk, D), lambda w, j, k: (k, 0))],
        out_specs=pl.BlockSpec((4, FFN_SHARD, D), lambda w, j, k: (j, w, 0)),
        scratch_shapes=[pltpu.VMEM((FFN_TF, D), F32)],
        input_output_aliases={0: 0},
        name=name + "_dw13",
        compiler_params=pltpu.CompilerParams(dimension_semantics=("parallel", "parallel", "arbitrary"),
                                             vmem_limit_bytes=VMEM_LIMIT),
    )(gw, dab, n)
    return dab, gw


def ffn_bwd_input(dab, G, x_in, g, dxo, name):
    _, S, F = dab.shape
    D = x_in.shape[1]
    tm = 256

    def dn_body(dab_ref, w1_ref, w3_ref, x_ref, d_ref, g_ref, dx_ref, dg_ref):
        i = pl.program_id(0)
        dn = _dot(dab_ref[0], w1_ref[...].reshape(F, D)) + _dot(dab_ref[1], w3_ref[...].reshape(F, D))
        dx, dgt = _rms_bwd_tile(dn, x_ref[...], g_ref[...])
        dx_ref[...] = d_ref[...] + dx
        dgp = jnp.sum(dgt, axis=0, keepdims=True)

        @pl.when(i == 0)
        def _():
            dg_ref[...] = dgp

        @pl.when(i > 0)
        def _():
            dg_ref[...] += dgp

    dx, dg = _call(name + "_dn", dn_body, (S // tm,),
                   [(dab, (2, tm, F), lambda i: (0, i, 0)),
                    _ffn_whole_w_spec(G, 0), _ffn_whole_w_spec(G, 1),
                    (x_in, (tm, D), lambda i: (i, 0)), (dxo, (tm, D), lambda i: (i, 0)),
                    (g, (1, D), lambda i: (0, 0))],
                   [((S, D), F32, (tm, D), lambda i: (i, 0)), ((1, D), F32, (1, D), lambda i: (0, 0))],
                   sem=("arbitrary",))
    return dx, dg


PROJ_TN = 512
DH_SHARDS_PER_STEP = 4


def in_proj(h, Gm, first_tile, n_tiles, dtype, name, tile_stride=1):
    S, D = h.shape
    tm = 1024
    tile = lambda j: first_tile + tile_stride * j

    def body(h_ref, w_ref, o_ref):
        o_ref[...] = _dot(h_ref[...], w_ref[...]).astype(dtype)

    return _call(name, body, (n_tiles, S // tm),
                 [(h, (tm, D), lambda j, i: (i, 0)),
                  (Gm, (None, D, PROJ_TN), lambda j, i: (tile(j) // 2, 0, tile(j) % 2))],
                 [((S, n_tiles * PROJ_TN), dtype, (tm, PROJ_TN), lambda j, i: (i, j))],
                 sem=("parallel", "parallel"))[0]


def in_proj_bwd_dw(dproj, h, gm_grads, name):
    S, D = h.shape
    NT = dproj.shape[1] // PROJ_TN
    tk = WGRAD_TK
    nk = S // tk

    def dw_body(gm_ref, h_ref, d_ref, o_ref, acc_ref):
        k = pl.program_id(1)
        p = _dot(h_ref[...], d_ref[...], 0, 0)

        @pl.when(k == 0)
        def _():
            acc_ref[...] = p

        @pl.when(k > 0)
        def _():
            acc_ref[...] += p

        @pl.when(k == nk - 1)
        def _():
            o_ref[...] = acc_ref[...].astype(BF16)

    return pl.pallas_call(
        dw_body,
        out_shape=jax.ShapeDtypeStruct(gm_grads.shape, BF16),
        grid=(NT, nk),
        in_specs=[pl.BlockSpec(memory_space=pl.ANY),
                  pl.BlockSpec((tk, D), lambda j, k: (k, 0)),
                  pl.BlockSpec((tk, PROJ_TN), lambda j, k: (k, j))],
        out_specs=pl.BlockSpec((None, D, PROJ_TN), lambda j, k: (j // 2, 0, j % 2)),
        scratch_shapes=[pltpu.VMEM((D, PROJ_TN), F32)],
        input_output_aliases={0: 0},
        name=name + "_dw",
        compiler_params=pltpu.CompilerParams(dimension_semantics=("parallel", "arbitrary"),
                                             vmem_limit_bytes=VMEM_LIMIT),
    )(gm_grads, h, dproj)


def in_proj_bwd_dh(dproj, Gm, x_in, g, dres, name):
    S, D = x_in.shape
    tm = 512
    C = Gm.shape[2]
    per = DH_SHARDS_PER_STEP
    n_sh = dproj.shape[1] // (C * per)

    def dh_body(d_ref, w_ref, x_ref, r_ref, g_ref, dx_ref, dg_ref, acc_ref):
        i, k = pl.program_id(0), pl.program_id(1)
        p = _dot(d_ref[:, 0:C], w_ref[0], 1, 1)
        for s in range(1, per):
            p = p + _dot(d_ref[:, s * C:(s + 1) * C], w_ref[s], 1, 1)

        @pl.when(k == 0)
        def _():
            acc_ref[...] = p

        @pl.when(k > 0)
        def _():
            acc_ref[...] += p

        @pl.when(k == n_sh - 1)
        def _():
            dx, dgt = _rms_bwd_tile(acc_ref[...], x_ref[...], g_ref[...])
            dx_ref[...] = r_ref[...] + dx
            dgp = jnp.sum(dgt, axis=0, keepdims=True)

            @pl.when(i == 0)
            def _():
                dg_ref[...] = dgp

            @pl.when(i > 0)
            def _():
                dg_ref[...] += dgp

    dx, dg = _call(name + "_dh", dh_body, (S // tm, n_sh),
                   [(dproj, (tm, per * C), lambda i, k: (i, k)),
                    (Gm, (per, D, C), lambda i, k: (k, 0, 0)),
                    (x_in, (tm, D), lambda i, k: (i, 0)), (dres, (tm, D), lambda i, k: (i, 0)),
                    (g, (1, D), lambda i, k: (0, 0))],
                   [((S, D), F32, (tm, D), lambda i, k: (i, 0)), ((1, D), F32, (1, D), lambda i, k: (0, 0))],
                   scratch=[pltpu.VMEM((tm, D), F32)], sem=("arbitrary", "arbitrary"))
    return dx, dg


def _t5_bucket(rel):
    n = N_BUCKETS // 2
    max_exact = n // 2
    ret = jnp.where(rel > 0, n, 0)
    a = jnp.abs(rel)
    af = jnp.maximum(a, 1).astype(F32)
    large = max_exact + (jnp.log(af / max_exact) / math.log(MAX_DISTANCE / max_exact)
                         * (n - max_exact)).astype(jnp.int32)
    large = jnp.minimum(large, n - 1)
    return ret + jnp.where(a < max_exact, a, large)


def _bucket_tables():
    qi = jnp.arange(A_TQ, dtype=jnp.int32)[:, None]
    kj = jnp.arange(A_WIN, dtype=jnp.int32)[None, :]
    rel = kj - HALF_WINDOW - qi
    return jnp.stack([_t5_bucket(rel * d) for d in DILATIONS], axis=0)


def bias_build(rel_bias, buckets):
    def body(tab_ref, bk_ref, o_ref):
        col = pl.program_id(0) * HEADS_PER_GROUP_A + pl.program_id(1)
        bk = bk_ref[...]
        acc = jnp.zeros(bk.shape, F32)
        for b in range(N_BUCKETS):
            acc = jnp.where(bk == b, tab_ref[b, col], acc)
        qi = lax.broadcasted_iota(jnp.int32, bk.shape, 0)
        kj = lax.broadcasted_iota(jnp.int32, bk.shape, 1)
        band = jnp.where(jnp.abs(kj - HALF_WINDOW - qi) <= HALF_WINDOW, acc, NEG_INF)
        o_ref[0] = jnp.where(kj >= HALF_WINDOW, band, NEG_INF)
        o_ref[1] = band
        o_ref[2] = jnp.where(kj < A_TQ + HALF_WINDOW, band, NEG_INF)

    out = pl.pallas_call(
        body,
        out_shape=jax.ShapeDtypeStruct((3, HEADS_PER_GROUP_A // 2, 3, 2, A_TQ, A_WIN), F32),
        grid=(3, HEADS_PER_GROUP_A),
        in_specs=[pl.BlockSpec(memory_space=pltpu.SMEM),
                  pl.BlockSpec((None, A_TQ, A_WIN), lambda g, h: (g, 0, 0))],
        out_specs=pl.BlockSpec((None, None, 3, None, A_TQ, A_WIN), lambda g, h: (g, h // 2, 0, h % 2, 0, 0)),
        name="a_bias_build",
        compiler_params=pltpu.CompilerParams(dimension_semantics=("parallel", "parallel")),
    )(rel_bias, buckets)
    return out.reshape(3, HEADS_PER_GROUP_A // 2, 3, 2 * A_TQ, A_WIN)


def bias_bwd(dbias, buckets):
    def body(d_ref, bk_ref, o_ref):
        bk = bk_ref[...]
        dv = d_ref[...]
        for b in range(N_BUCKETS):
            part = jnp.sum(jnp.where(bk == b, dv, 0.0), axis=1, keepdims=True)
            o_ref[b:b + 1, :] = jnp.broadcast_to(jnp.sum(part, axis=0, keepdims=True), (1, LANES))

    out = pl.pallas_call(
        body,
        out_shape=jax.ShapeDtypeStruct((3, HEADS_PER_GROUP_A, N_BUCKETS, LANES), F32),
        grid=(3, HEADS_PER_GROUP_A),
        in_specs=[pl.BlockSpec((None, None, A_TQ, A_WIN), lambda g, h: (g, h, 0, 0)),
                  pl.BlockSpec((None, A_TQ, A_WIN), lambda g, h: (g, 0, 0))],
        out_specs=pl.BlockSpec((None, None, N_BUCKETS, LANES), lambda g, h: (g, h, 0, 0)),
        name="a_bias_bwd",
        compiler_params=pltpu.CompilerParams(dimension_semantics=("parallel", "parallel")),
    )(dbias, buckets)
    return out[:, :, :, 0].transpose(2, 0, 1).reshape(N_BUCKETS, 3 * HEADS_PER_GROUP_A)


def _a_fill_padded(pad_ref, src_ref, n, pad):
    zeros = jnp.zeros((pad, LANES), pad_ref.dtype)
    pad_ref[0:pad, :] = zeros
    pad_ref[pad + n:2 * pad + n, :] = zeros
    pad_ref[pad:pad + n, :] = src_ref[...].astype(pad_ref.dtype)


def _a_stack_heads(x, lane):
    zero = jnp.zeros_like(x)
    return jnp.concatenate([jnp.where(lane < HEAD_DIM_A, x, zero), jnp.where(lane >= HEAD_DIM_A, x, zero)], axis=0)


def _a_bias_variant(qb, nqb):
    return jnp.where(qb == 0, 0, jnp.where(qb == nqb - 1, 2, 1))


def a_fwd(proj_g, bias_g, g, name):
    S = proj_g.shape[0]
    d = DILATIONS[g]
    L = S // d
    nqb = L // A_TQ
    pad = HALF_WINDOW * d

    def body(q_ref, k_ref, v_ref, b_ref, o_ref, l_ref, qf, kpad, vpad):
        qf[...] = q_ref[...].astype(F32) * A_SCALE
        _a_fill_padded(kpad, k_ref, S, pad)
        _a_fill_padded(vpad, v_ref, S, pad)
        lane = lax.broadcasted_iota(jnp.int32, (A_TQ, LANES), 1)

        def block(t, carry):
            qb, r = t // d, t % d
            start = qb * (A_TQ * d) + r
            kw = kpad[pl.ds(start, A_WIN, stride=d), :].astype(BF16)
            vw = vpad[pl.ds(start, A_WIN, stride=d), :].astype(BF16)
            q = qf[pl.ds(start, A_TQ, stride=d), :].astype(BF16)
            q2 = _a_stack_heads(q, lane)
            s = _dot(q2, kw, 1, 1) + b_ref[_a_bias_variant(qb, nqb)]
            m = jnp.max(s, axis=-1, keepdims=True)
            e = jnp.exp(s - m)
            l = jnp.sum(e, axis=-1, keepdims=True)
            o2 = _dot(e.astype(BF16), vw) / l
            lse2 = m + jnp.log(l)
            o_ref[pl.ds(start, A_TQ, stride=d), :] = jnp.where(lane < HEAD_DIM_A, o2[0:A_TQ], o2[A_TQ:])
            l_ref[pl.ds(start, A_TQ, stride=d), :] = jnp.where(lane < HEAD_DIM_A, lse2[0:A_TQ], lse2[A_TQ:])
            return carry

        lax.fori_loop(0, nqb * d, block, 0, unroll=A_UNROLL)

    out_spec = ((S, GROUP_WIDTH_A), F32, (S, LANES), lambda hp: (0, hp))
    return _call(name, body, (4,),
                 [(proj_g, (S, LANES), lambda hp: (0, hp)),
                  (proj_g, (S, LANES), lambda hp: (0, 4 + hp)),
                  (proj_g, (S, LANES), lambda hp: (0, 8 + hp)),
                  (bias_g, (None, 3, 2 * A_TQ, A_WIN), lambda hp: (hp, 0, 0, 0))],
                 [out_spec, out_spec],
                 scratch=[pltpu.VMEM((S, LANES), F32)] + [pltpu.VMEM((S + 2 * pad, LANES), F32)] * 2,
                 sem=("parallel",))


def a_combine(outs, lses, name):
    S, W = outs[0].shape
    tr = 512

    def body(o0, o1, o2, l0, l1, l2, oa_ref, lt_ref):
        a, b, c = l0[...], l1[...], l2[...]
        m = jnp.maximum(jnp.maximum(a, b), c)
        ea, eb, ec = jnp.exp(a - m), jnp.exp(b - m), jnp.exp(c - m)
        z = ea + eb + ec
        oa_ref[...] = ((ea * o0[...] + eb * o1[...] + ec * o2[...]) / z).astype(BF16)
        lt_ref[...] = m + jnp.log(z)

    spec = ((tr, W), lambda i: (i, 0))
    return _call(name, body, (S // tr,), [(a,) + spec for a in (*outs, *lses)],
                 [((S, W), BF16) + spec, ((S, W), F32) + spec], sem=("parallel",))


def a_bwd(proj_g, bias_g, do_a, o_a, lse_tot, g, name):
    S = proj_g.shape[0]
    d = DILATIONS[g]
    L = S // d
    nqb = L // A_TQ
    pad = HALF_WINDOW * d

    def body(q_ref, k_ref, v_ref, b_ref, do_ref, o_ref, l_ref, dq_ref, dk_ref, dv_ref, db_ref,
             qf, of, dqf, kpad, vpad, dkacc, dvacc):
        qf[...] = q_ref[...].astype(F32) * A_SCALE
        of[...] = o_ref[...].astype(F32)
        _a_fill_padded(kpad, k_ref, S, pad)
        _a_fill_padded(vpad, v_ref, S, pad)
        dkacc[...] = jnp.zeros(dkacc.shape, F32)
        dvacc[...] = jnp.zeros(dvacc.shape, F32)
        db_ref[...] = jnp.zeros(db_ref.shape, F32)
        lane = lax.broadcasted_iota(jnp.int32, (A_TQ, LANES), 1)

        def block(t, carry):
            qb, r = t // d, t % d
            start = qb * (A_TQ * d) + r
            rows = pl.ds(start, A_TQ, stride=d)
            win = pl.ds(start, A_WIN, stride=d)
            kw = kpad[win, :].astype(BF16)
            vw = vpad[win, :].astype(BF16)
            q = qf[rows, :].astype(BF16)
            do = do_ref[rows, :]
            ov = of[rows, :]
            lt = l_ref[rows, :]
            q2 = _a_stack_heads(q, lane)
            do2 = _a_stack_heads(do, lane)
            lt2 = jnp.concatenate([lt[:, 0:1], lt[:, HEAD_DIM_A:HEAD_DIM_A + 1]], axis=0)
            s = _dot(q2, kw, 1, 1) + b_ref[_a_bias_variant(qb, nqb)]
            p = jnp.exp(s - lt2)
            t = jnp.sum(do2 * jnp.concatenate([ov, ov], axis=0), axis=-1, keepdims=True)
            dob2 = do2.astype(BF16)
            ds = p * (_dot(dob2, vw, 1, 1) - t)
            db_ref[...] += ds
            dsb = ds.astype(BF16)
            dq2 = _dot(dsb, kw)
            dqf[rows, :] = jnp.where(lane < HEAD_DIM_A, dq2[0:A_TQ], dq2[A_TQ:]) * A_SCALE
            dkacc[win, :] += _dot(dsb, q2, 0, 0)
            dvacc[win, :] += _dot(p.astype(BF16), dob2, 0, 0)
            return carry

        lax.fori_loop(0, nqb * d, block, 0, unroll=A_UNROLL)
        dq_ref[...] = dqf[...].astype(BF16)
        dk_ref[...] = dkacc[pad:pad + S, :].astype(BF16)
        dv_ref[...] = dvacc[pad:pad + S, :].astype(BF16)

    slab = ((S, LANES), lambda hp: (0, hp))
    oshape = (S, GROUP_WIDTH_A)
    padded = pltpu.VMEM((S + 2 * pad, LANES), F32)
    return _call(
        name, body, (4,),
        [(proj_g, (S, LANES), lambda hp: (0, hp)),
         (proj_g, (S, LANES), lambda hp: (0, 4 + hp)),
         (proj_g, (S, LANES), lambda hp: (0, 8 + hp)),
         (bias_g, (None, 3, 2 * A_TQ, A_WIN), lambda hp: (hp, 0, 0, 0)),
         (do_a,) + slab, (o_a,) + slab, (lse_tot,) + slab],
        [(oshape, BF16) + slab, (oshape, BF16) + slab, (oshape, BF16) + slab,
         ((4, 2 * A_TQ, A_WIN), F32, (None, 2 * A_TQ, A_WIN), lambda hp: (hp, 0, 0))],
        scratch=[pltpu.VMEM((S, LANES), F32)] * 3 + [padded] * 4,
        sem=("parallel",))


def _rope_tables(S):
    rows = S // GRID_W
    row = jnp.repeat(jnp.arange(rows, dtype=F32), GRID_W)
    col = jnp.tile(jnp.arange(GRID_W, dtype=F32), rows)
    n_freq = HEAD_DIM_B // 4
    freq = ROPE_THETA ** (-jnp.arange(n_freq, dtype=F32) / n_freq)
    ang = jnp.concatenate([row[:, None] * freq, col[:, None] * freq], axis=-1)
    cos, sin = jnp.cos(ang), jnp.sin(ang)
    return jnp.repeat(cos, 2, axis=-1), jnp.stack([-sin, sin], axis=-1).reshape(S, HEAD_DIM_B)


def _swap_pairs(y):
    lane = lax.broadcasted_iota(jnp.int32, y.shape, 1)
    return jnp.where(lane % 2 == 0, pltpu.roll(y, LANES - 1, 1), pltpu.roll(y, 1, 1))


def qkv_prep(proj_b, gains, cos_t, sin_t, name):
    S = proj_b.shape[0]
    ts = 256
    n_rot = N_HEADS_B + N_KV_B
    nh = n_rot + N_KV_B
    W = nh * LANES

    def body(x_ref, g_ref, c_ref, s_ref, o_ref):
        cv, sv = c_ref[...], s_ref[...]
        for hb in range(nh):
            cols = slice(hb * LANES, (hb + 1) * LANES)
            xv = x_ref[:, cols]
            if hb < n_rot:
                r = lax.rsqrt(jnp.mean(xv * xv, axis=-1, keepdims=True) + EPS)
                yv = xv * r * g_ref[:, cols]
                o_ref[:, cols] = (yv * cv + _swap_pairs(yv) * sv).astype(BF16)
            else:
                o_ref[:, cols] = xv.astype(BF16)

    return _call(name, body, (S // ts,),
                 [(proj_b, (ts, W), lambda i: (i, 0)), (gains, (1, W), lambda i: (0, 0)),
                  (cos_t, (ts, LANES), lambda i: (i, 0)), (sin_t, (ts, LANES), lambda i: (i, 0))],
                 [((S, W), BF16, (ts, W), lambda i: (i, 0))],
                 sem=("parallel",))[0]


def qk_prep_bwd(dr, proj_b, col0, gain, cos_t, sin_t, name):
    S, W = dr.shape
    H = W // LANES
    ts = 256
    xb = (col0 * LANES) // W

    def body(d_ref, x_ref, g_ref, c_ref, s_ref, dx_ref, dg_ref):
        i = pl.program_id(0)
        cv, sv, gv = c_ref[...], s_ref[...], g_ref[...]
        dgp = jnp.zeros((1, LANES), F32)
        for hb in range(H):
            cols = slice(hb * LANES, (hb + 1) * LANES)
            dout = d_ref[:, cols]
            dy = dout * cv + _swap_pairs(dout * sv)
            dx, dgt = _rms_bwd_tile(dy, x_ref[:, cols], gv)
            dx_ref[:, cols] = dx.astype(BF16)
            dgp = dgp + jnp.sum(dgt, axis=0, keepdims=True)

        @pl.when(i == 0)
        def _():
            dg_ref[...] = dgp

        @pl.when(i > 0)
        def _():
            dg_ref[...] += dgp

    return _call(name, body, (S // ts,),
                 [(dr, (ts, W), lambda i: (i, 0)), (proj_b, (ts, W), lambda i: (i, xb)),
                  (gain, (1, LANES), lambda i: (0, 0)),
                  (cos_t, (ts, LANES), lambda i: (i, 0)), (sin_t, (ts, LANES), lambda i: (i, 0))],
                 [((S, W), BF16, (ts, W), lambda i: (i, 0)),
                  ((1, LANES), F32, (1, LANES), lambda i: (0, 0))],
                 sem=("arbitrary",))


def _row_sums(x):
    hi = x.astype(BF16)
    lo = (x - hi.astype(F32)).astype(BF16)
    ones = jnp.ones((8, LANES), BF16)
    return (_dot(ones, hi, 1, 1) + _dot(ones, lo, 1, 1))[0:1, :]


def flash_fwd(qkv, name):
    S = qkv.shape[0]
    tq = B_TQ_FWD
    scale = HEAD_DIM_B ** -0.5

    def body(q_ref, k_ref, v_ref, o_ref, l_ref):
        s = _dot(q_ref[...], k_ref[...], 1, 1) * scale
        m = jnp.max(s, axis=-1, keepdims=True)
        e = jnp.exp(s - m)
        l = jnp.sum(e, axis=-1, keepdims=True)
        o_ref[...] = (_dot(e.astype(BF16), v_ref[...]) / l).astype(BF16)
        lse = jnp.broadcast_to(m + jnp.log(l), (tq, LANES))
        l_ref[...] = _row_sums(lse) * (1.0 / LANES)

    head = lambda g, h, i: (i, g * GQA_GROUP_B + h)
    return _call(name, body, (N_KV_B, GQA_GROUP_B, S // tq),
                 [(qkv, (tq, LANES), head),
                  (qkv, (S, LANES), lambda g, h, i: (0, N_HEADS_B + g)),
                  (qkv, (S, LANES), lambda g, h, i: (0, N_HEADS_B + N_KV_B + g))],
                 [((S, N_HEADS_B * LANES), BF16, (tq, LANES), head),
                  ((N_HEADS_B, 1, S), F32, (None, 1, tq), lambda g, h, i: (g * GQA_GROUP_B + h, 0, i))],
                 sem=("parallel", "parallel", "parallel"))


def flash_bwd(qkv, k_t, do_b, o_b, lse, name):
    S = qkv.shape[0]
    tq = B_TQ_BWD
    nq = S // tq
    scale = HEAD_DIM_B ** -0.5

    def body(q_ref, k_ref, v_ref, kt_ref, do_ref, o_ref, l_ref, dq_ref, dk_ref, dv_ref, dkacc, dvacc):
        h, i = pl.program_id(1), pl.program_id(2)

        @pl.when((h == 0) & (i == 0))
        def _():
            dkacc[...] = jnp.zeros(dkacc.shape, F32)
            dvacc[...] = jnp.zeros(dvacc.shape, F32)

        q = q_ref[...]
        do = do_ref[...]
        dob = do.astype(BF16)
        t = _row_sums(do * o_ref[...].astype(F32))
        pt = jnp.exp(_dot(k_ref[...], q, 1, 1) * scale - l_ref[...])
        dst = pt * (_dot(v_ref[...], dob, 1, 1) - t) * scale
        dsb = dst.astype(BF16)
        dvacc[...] += _dot(pt.astype(BF16), dob)
        dkacc[...] += _dot(dsb, q)
        dq_ref[...] = _dot(kt_ref[...], dsb).T

        @pl.when((h == GQA_GROUP_B - 1) & (i == nq - 1))
        def _():
            dk_ref[...] = dkacc[...]
            dv_ref[...] = dvacc[...].astype(BF16)

    head = lambda g, h, i: (i, g * GQA_GROUP_B + h)
    return _call(name, body, (N_KV_B, GQA_GROUP_B, nq),
                 [(qkv, (tq, LANES), head),
                  (qkv, (S, LANES), lambda g, h, i: (0, N_HEADS_B + g)),
                  (qkv, (S, LANES), lambda g, h, i: (0, N_HEADS_B + N_KV_B + g)),
                  (k_t, (LANES, S), lambda g, h, i: (g, 0)),
                  (do_b, (tq, LANES), head), (o_b, (tq, LANES), head),
                  (lse, (None, 1, tq), lambda g, h, i: (g * GQA_GROUP_B + h, 0, i))],
                 [((S, N_HEADS_B * LANES), F32, (tq, LANES), head),
                  ((S, N_KV_B * LANES), F32, (S, LANES), lambda g, h, i: (0, g)),
                  ((S, N_KV_B * LANES), BF16, (S, LANES), lambda g, h, i: (0, g))],
                 scratch=[pltpu.VMEM((S, LANES), F32)] * 2,
                 sem=("parallel", "arbitrary", "arbitrary"))


MERGE_TN = 512


def _mix_rows_spec(Gm, row0, n_slots, slot_map, cols=None, col_map=None):
    C = Gm.shape[2] if cols is None else cols
    cm = (lambda *idx: 0) if col_map is None else col_map
    return (Gm, (n_slots, LANES, C), lambda *idx: (slot_map(*idx), row0 // LANES, cm(*idx)))


def merge_fwd(o_a, o_b, w_a, Gm, proj_b, b_gate, name):
    S = o_a.shape[0]
    D = w_a.shape[1]
    tm, tn = 512, MERGE_TN
    ga0, gb0 = PB_GATE_A // tn, PB_GATE_B // tn

    def body(oa_ref, ob_ref, wa_ref, wb_ref, pa_ref, pb_ref, ba_ref, bb_ref, m_ref, ya_ref, yb_ref):
        ya = _dot(oa_ref[...], wa_ref[...])
        yb = _dot(ob_ref[...], wb_ref[...].reshape(N_DEV * LANES, tn))
        ga = _sigmoid(pa_ref[...] + ba_ref[...])
        gb = _sigmoid(pb_ref[...] + bb_ref[...])
        m_ref[...] = (ga * ya + gb * yb).astype(BF16)
        ya_ref[...] = ya.astype(BF16)
        yb_ref[...] = yb.astype(BF16)

    out = ((S, D), BF16, (tm, tn), lambda j, i: (i, j))
    return _call(name, body, (D // tn, S // tm),
                 [(o_a, (tm, o_a.shape[1]), lambda j, i: (i, 0)), (o_b, (tm, o_b.shape[1]), lambda j, i: (i, 0)),
                  (w_a, (w_a.shape[0], tn), lambda j, i: (0, j)),
                  _mix_rows_spec(Gm, MIX_WB, N_DEV, lambda j, i: 0, cols=tn, col_map=lambda j, i: j),
                  (proj_b, (tm, tn), lambda j, i: (i, ga0 + j)), (proj_b, (tm, tn), lambda j, i: (i, gb0 + j)),
                  (b_gate, (1, tn), lambda j, i: (0, j)), (b_gate, (1, tn), lambda j, i: (0, D // tn + j))],
                 [out, out, out], sem=("parallel", "parallel"))


def out_proj(merged, Gm, x, name):
    S, D = x.shape
    tm, tn = 512, MERGE_TN

    def body(m_ref, w_ref, x_ref, o_ref):
        o_ref[...] = x_ref[...] + _dot(m_ref[...], w_ref[...].reshape(N_DEV * LANES, tn))

    return _call(name, body, (D // tn, S // tm),
                 [(merged, (tm, D), lambda j, i: (i, 0)),
                  _mix_rows_spec(Gm, MIX_WOUT, N_DEV, lambda j, i: 0, cols=tn, col_map=lambda j, i: j),
                  (x, (tm, tn), lambda j, i: (i, j))],
                 [((S, D), F32, (tm, tn), lambda j, i: (i, j))], sem=("parallel", "parallel"))[0]


def merge_bwd(dx2, Gm, ya, yb, proj_b, b_gate, name):
    S, D = dx2.shape
    tm, tn = 512, MERGE_TN
    nn = D // tn
    ga0, gb0 = PB_GATE_A // tn, PB_GATE_B // tn

    def body(d_ref, w_ref, ya_ref, yb_ref, pa_ref, pb_ref, ba_ref, bb_ref, dya_ref, dyb_ref, dg_ref, dbg_ref):
        i = pl.program_id(1)
        dm = _dot(d_ref[...].astype(BF16), w_ref[...].reshape(tn, D), 1, 1)
        ga = _sigmoid(pa_ref[...] + ba_ref[...])
        gb = _sigmoid(pb_ref[...] + bb_ref[...])
        dya_ref[...] = (dm * ga).astype(BF16)
        dyb_ref[...] = (dm * gb).astype(BF16)
        dpa = dm * ya_ref[...].astype(F32) * ga * (1.0 - ga)
        dpb = dm * yb_ref[...].astype(F32) * gb * (1.0 - gb)
        dg_ref[0] = dpa.astype(BF16)
        dg_ref[1] = dpb.astype(BF16)
        sa = jnp.sum(dpa, axis=0, keepdims=True)
        sb = jnp.sum(dpb, axis=0, keepdims=True)

        @pl.when(i == 0)
        def _():
            dbg_ref[0] = sa
            dbg_ref[1] = sb

        @pl.when(i > 0)
        def _():
            dbg_ref[0] += sa
            dbg_ref[1] += sb

    tile = ((tm, tn), lambda j, i: (i, j))
    dya, dyb, dgate, dbg = _call(
        name, body, (nn, S // tm),
        [(dx2, (tm, D), lambda j, i: (i, 0)),
         _mix_rows_spec(Gm, MIX_WOUT, tn // LANES, lambda j, i: j),
         (ya,) + tile, (yb,) + tile,
         (proj_b, (tm, tn), lambda j, i: (i, ga0 + j)), (proj_b, (tm, tn), lambda j, i: (i, gb0 + j)),
         (b_gate, (1, tn), lambda j, i: (0, j)), (b_gate, (1, tn), lambda j, i: (0, nn + j))],
        [((S, D), BF16) + tile, ((S, D), BF16) + tile,
         ((2, S, D), BF16, (2, tm, tn), lambda j, i: (0, i, j)),
         ((2, 1, D), F32, (2, 1, tn), lambda j, i: (0, 0, j))],
        sem=("parallel", "arbitrary"))
    return dya, dyb, dgate, dbg


def matmul_nt(a, b_spec_fn, N, name, tn=512):
    S, K = a.shape
    tm = 512

    def body(a_ref, b_ref, o_ref):
        b = b_ref[...]
        o_ref[...] = _dot(a_ref[...], b.reshape(-1, b.shape[-1]), 1, 1)

    return _call(name, body, (N // tn, S // tm),
                 [(a, (tm, K), lambda j, i: (i, 0)), b_spec_fn(lambda j, i: j)],
                 [((S, N), F32, (tm, tn), lambda j, i: (i, j))], sem=("parallel", "parallel"))[0]


def weight_grad_rows(a, b, grads, row0, name):
    S, M = a.shape
    N = b.shape[1]
    tmm = 512
    tk = WGRAD_TK
    nk = S // tk

    def body(g_ref, a_ref, b_ref, o_ref, acc_ref):
        k = pl.program_id(1)
        p = _dot(a_ref[...], b_ref[...].astype(BF16), 0, 0)

        @pl.when(k == 0)
        def _():
            acc_ref[...] = p

        @pl.when(k > 0)
        def _():
            acc_ref[...] += p

        @pl.when(k == nk - 1)
        def _():
            o_ref[...] = acc_ref[...].astype(BF16).reshape(tmm // LANES, LANES, N)

    return pl.pallas_call(
        body,
        out_shape=jax.ShapeDtypeStruct(grads.shape, BF16),
        grid=(M // tmm, nk),
        in_specs=[pl.BlockSpec(memory_space=pl.ANY),
                  pl.BlockSpec((tk, tmm), lambda j, k: (k, j)),
                  pl.BlockSpec((tk, N), lambda j, k: (k, 0))],
        out_specs=pl.BlockSpec((tmm // LANES, LANES, N), lambda j, k: (j, row0 // LANES, 0)),
        scratch_shapes=[pltpu.VMEM((tmm, N), F32)],
        input_output_aliases={0: 0},
        name=name,
        compiler_params=pltpu.CompilerParams(dimension_semantics=("parallel", "arbitrary"),
                                             vmem_limit_bytes=VMEM_LIMIT),
    )(grads, a, b)


def weight_grad_plain(a, b, name):
    S, M = a.shape
    N = b.shape[1]
    tk = WGRAD_TK
    nk = S // tk

    def body(a_ref, b_ref, o_ref, acc_ref):
        k = pl.program_id(0)
        p = _dot(a_ref[...], b_ref[...], 0, 0)

        @pl.when(k == 0)
        def _():
            acc_ref[...] = p

        @pl.when(k > 0)
        def _():
            acc_ref[...] += p

        @pl.when(k == nk - 1)
        def _():
            o_ref[...] = acc_ref[...].astype(BF16)

    return _call(name, body, (nk,),
                 [(a, (tk, M), lambda k: (k, 0)), (b, (tk, N), lambda k: (k, 0))],
                 [((M, N), BF16, (M, N), lambda k: (0, 0))],
                 scratch=[pltpu.VMEM((M, N), F32)], sem=("arbitrary",))[0]


def local_step(x, tgt, p, get_g1_up, get_g1_down, get_gm, get_g2, emit, start_token):
    S, D = x.shape
    after = lambda t: t[0:1, 0:1]
    buckets = _bucket_tables()
    cos_t, sin_t = _rope_tables(S)
    gains = jnp.concatenate([jnp.tile(p["q_norm"], (1, N_HEADS_B)), jnp.tile(p["k_norm"], (1, N_KV_B)),
                             jnp.ones((1, N_KV_B * LANES), F32)], axis=1)

    n1 = rms_fwd(x, p["ffn1_norm"] + after(start_token), "ffn1_norm")
    bias = bias_build(p["rel_bias"] + after(start_token), buckets)
    g1_up = get_g1_up((n1, bias))
    ab1 = ffn_up(n1, (g1_up, None), "ffn1_up")
    G1 = (g1_up, get_g1_down(ab1))
    x1 = ffn_down(ab1, G1, x, "ffn1_down")

    Gm = get_gm(x1)
    w_a = Gm[:, MIX_WA:MIX_ROWS, :].reshape(N_DEV, GROUP_WIDTH_A, LANES).transpose(1, 0, 2).reshape(GROUP_WIDTH_A, D)
    hm = rms_fwd(x1, p["mix_norm"], "mix_norm")
    n_a = A_QKV_WIDTH // PROJ_TN
    proj_a = [in_proj(hm, Gm, g, 3, BF16, "in_proj_a%d" % g, tile_stride=3) for g in range(3)]
    proj_b = in_proj(hm, Gm, n_a, PB_WIDTH // PROJ_TN, F32, "in_proj_b")

    outs, lses = [], []
    for g in range(3):
        o, l = a_fwd(proj_a[g], bias[g], g, "a_fwd_%d" % g)
        outs.append(o)
        lses.append(l)
    o_a, lse_tot = a_combine(outs, lses, "a_combine")

    qkv = qkv_prep(proj_b, gains, cos_t, sin_t, "qkv_prep")
    k_t = qkv[:, N_HEADS_B * LANES:(N_HEADS_B + N_KV_B) * LANES].T
    o_b, lse_b = flash_fwd(qkv, "flash_fwd")

    merged, ya, yb = merge_fwd(o_a, o_b, w_a, Gm, proj_b, p["b_gate"], "merge_fwd")
    x2 = out_proj(merged, Gm, x1, "out_proj")

    G2 = get_g2(x2)
    n2 = rms_fwd(x2, p["ffn2_norm"], "ffn2_norm")
    ab2 = ffn_up(n2, G2, "ffn2_up")
    x3 = ffn_down(ab2, G2, x2, "ffn2_down")

    loss, dx3, d_final = final_loss(x3, tgt, p["final_norm"], "final_loss")

    dabh2, gw2 = ffn_bwd_weights(dx3, ab2, n2, G2, "ffn2_bwd")
    t2 = emit("ffn2", gw2)
    dx2, d_ffn2_norm = ffn_bwd_input(dabh2, G2, x2, p["ffn2_norm"] + after(t2), dx3, "ffn2_bwd")

    dya, dyb, dgate, dbg = merge_bwd(dx2, Gm, ya, yb, proj_b, p["b_gate"], "merge_bwd")
    gm_grads = jnp.zeros(Gm.shape, BF16)
    gm_grads = weight_grad_rows(merged, dx2, gm_grads, MIX_WOUT, "dw_out")
    gm_grads = weight_grad_rows(o_b, dyb, gm_grads, MIX_WB, "dw_branch_b")
    dw_a = weight_grad_plain(o_a, dya, "dw_branch_a")
    do_a = matmul_nt(dya, lambda jm: (w_a, (MERGE_TN, D), lambda j, i: (jm(j, i), 0)), GROUP_WIDTH_A, "do_a")
    do_b = matmul_nt(dyb, lambda jm: _mix_rows_spec(Gm, MIX_WB, MERGE_TN // LANES, jm), N_HEADS_B * LANES, "do_b")

    dq_r, dk_r, dv_b = flash_bwd(qkv, k_t, do_b, o_b, lse_b, "flash_bwd")
    dq_b, d_q_norm = qk_prep_bwd(dq_r, proj_b, 0, p["q_norm"], cos_t, sin_t, "q_prep_bwd")
    dk_b, d_k_norm = qk_prep_bwd(dk_r, proj_b, N_HEADS_B, p["k_norm"], cos_t, sin_t, "k_prep_bwd")

    dqs, dks, dvs, dbs = [], [], [], []
    for g in range(3):
        dq, dk, dv, db = a_bwd(proj_a[g], bias[g], do_a, o_a, lse_tot, g, "a_bwd_%d" % g)
        dqs.append(dq)
        dks.append(dk)
        dvs.append(dv)
        dbs.append(db)
    d_rel_bias = bias_bwd(jnp.stack(dbs, axis=0).reshape(3, HEADS_PER_GROUP_A, A_TQ, A_WIN), buckets)

    dproj = jnp.concatenate(dqs + dks + dvs + [dq_b, dk_b, dv_b, dgate[0], dgate[1]], axis=1)
    gm_grads = in_proj_bwd_dw(dproj, hm, gm_grads, "in_proj_bwd")
    dw_a_sh = dw_a.reshape(GROUP_WIDTH_A, N_DEV, LANES).transpose(1, 0, 2).reshape(N_DEV, MIX_ROWS - MIX_WA, D)
    gm_grads = lax.dynamic_update_slice(gm_grads, dw_a_sh, (0, MIX_WA, 0))
    tm = emit("mix", gm_grads)
    dx1, d_mix_norm = in_proj_bwd_dh(dproj, Gm, x1, p["mix_norm"] + after(tm), dx2, "in_proj_bwd")

    dabh1, gw1 = ffn_bwd_weights(dx1, ab1, n1, G1, "ffn1_bwd")
    t1 = emit("ffn1", gw1)
    dx0, d_ffn1_norm = ffn_bwd_input(dabh1, G1, x, p["ffn1_norm"] + after(t1), dx1, "ffn1_bwd")

    small = dict(ffn1_norm=d_ffn1_norm, mix_norm=d_mix_norm, b_gate=dbg.reshape(1, 2 * D),
                 q_norm=d_q_norm, k_norm=d_k_norm, rel_bias=d_rel_bias, ffn2_norm=d_ffn2_norm,
                 final_norm=d_final)
    return loss, dx0, small


def _pack_small(t, loss_row):
    row6 = jnp.concatenate([t["q_norm"].reshape(1, -1), t["k_norm"].reshape(1, -1), t["rel_bias"].reshape(1, -1)], axis=1)
    return jnp.concatenate([t["ffn1_norm"].reshape(1, -1), t["mix_norm"].reshape(1, -1), t["b_gate"].reshape(2, -1),
                            t["ffn2_norm"].reshape(1, -1), t["final_norm"].reshape(1, -1), row6, loss_row], axis=0)


def _unpack_small(a, shapes):
    return dict(ffn1_norm=a[0:1].reshape(shapes["ffn1_norm"]), mix_norm=a[1:2].reshape(shapes["mix_norm"]),
                b_gate=a[2:4].reshape(shapes["b_gate"]), ffn2_norm=a[4:5].reshape(shapes["ffn2_norm"]),
                final_norm=a[5].reshape(shapes["final_norm"]), q_norm=a[6:7, 0:128].reshape(shapes["q_norm"]),
                k_norm=a[6:7, 128:256].reshape(shapes["k_norm"]), rel_bias=a[6, 256:1024].reshape(shapes["rel_bias"]))


SMALL = ("ffn1_norm", "mix_norm", "b_gate", "q_norm", "k_norm", "rel_bias", "ffn2_norm", "final_norm")
ORDER = ("ffn1_norm", "ffn1_w1", "ffn1_w3", "ffn1_w2", "mix_norm", "w_in", "b_gate", "q_norm", "k_norm", "rel_bias",
         "w_branch_a", "w_branch_b", "w_out", "ffn2_norm", "ffn2_w1", "ffn2_w3", "ffn2_w2", "final_norm")


def kernel(x, ffn1_norm, ffn1_w1, ffn1_w3, ffn1_w2, mix_norm, w_in, b_gate, q_norm, k_norm, rel_bias, w_branch_a, w_branch_b, w_out, ffn2_norm, ffn2_w1, ffn2_w3, ffn2_w2, final_norm, loss_target, m_ffn1_norm, m_ffn1_w1, m_ffn1_w3, m_ffn1_w2, m_mix_norm, m_w_in, m_b_gate, m_q_norm, m_k_norm, m_rel_bias, m_w_branch_a, m_w_branch_b, m_w_out, m_ffn2_norm, m_ffn2_w1, m_ffn2_w3, m_ffn2_w2, m_final_norm, v_ffn1_norm, v_ffn1_w1, v_ffn1_w3, v_ffn1_w2, v_mix_norm, v_w_in, v_b_gate, v_q_norm, v_k_norm, v_rel_bias, v_w_branch_a, v_w_branch_b, v_w_out, v_ffn2_norm, v_ffn2_w1, v_ffn2_w3, v_ffn2_w2, v_final_norm):
    args = dict(locals())
    w = {n: args[n] for n in ORDER}
    m = {n: args["m_" + n] for n in ORDER}
    v = {n: args["v_" + n] for n in ORDER}
    D = x.shape[2]

    blocks = (
        ("ffn1_up", lambda t: jnp.concatenate([ffn1_w1[0].T + t, ffn1_w3[0].T + t], axis=0)),
        ("ffn1_down", lambda t: ffn1_w2[0] + t),
        ("mix", lambda t: jnp.concatenate([w_in[0] + t, w_branch_b[0] + t, w_out[0] + t,
                                           w_branch_a[0].reshape(MIX_ROWS - MIX_WA, D) + t], axis=0)),
        ("ffn2", lambda t: jnp.concatenate([ffn2_w1[0].T + t, ffn2_w3[0].T + t, ffn2_w2[0] + t], axis=0)),
    )
    gathers = {}
    start_token = jnp.zeros((8, LANES), F32)
    for tag, make in blocks:
        gathers[tag] = all_gather_start(make(start_token[0:1, 0:1]).astype(BF16), "all_gather_" + tag + "_start")
        start_token = gathers[tag][4]

    def gathered(tag):
        def get(after):
            return all_gather_finish(*_split_wait("all_gather_" + tag + "_wait", gathers[tag], 4, after),
                                     "all_gather_" + tag + "_finish")
        return get

    core = lax.axis_index("c").astype(jnp.int32).reshape(1)
    chip = (2 * lax.axis_index("x") + lax.axis_index("y")).astype(jnp.int32).reshape(1)
    exchanges = {}

    def emit(tag, gw):
        (theirs,) = reduce_scatter_pair([gw], "reduce_scatter_pair_" + tag)
        part = pair_add(gw, theirs, core, "pair_add_" + tag)
        exchanges[tag] = reduce_scatter_start(part, "reduce_scatter_" + tag + "_start")
        return exchanges[tag][4]

    small_p = dict(ffn1_norm=ffn1_norm, mix_norm=mix_norm, b_gate=b_gate, q_norm=q_norm, k_norm=k_norm,
                   rel_bias=rel_bias, ffn2_norm=ffn2_norm, final_norm=final_norm.reshape(1, D))
    loss_p, grad_x, small_g = local_step(x[0], loss_target[0], small_p, gathered("ffn1_up"), gathered("ffn1_down"),
                                         gathered("mix"), gathered("ffn2"), emit, start_token)

    def landed(tag, after):
        return _split_wait("reduce_scatter_" + tag + "_wait", exchanges[tag], 3, after)

    grads, delta, new_m, new_v = {}, {}, {}, {}

    def finish(n, part, land, off, blk, transposed=False):
        shp = w[n].shape
        if transposed:
            to2 = lambda a: a.reshape(shp[-2], shp[-1]).T
            back = lambda a: a.T.reshape(shp)
        else:
            to2 = lambda a: a.reshape(shp[-2], shp[-1])
            back = lambda a: a.reshape(shp)
        res = sum_adamw(part, land, chip, off, blk, to2(w[n]), to2(m[n]), to2(v[n]), "update_" + n)
        grads[n], delta[n], new_m[n], new_v[n] = [back(a) for a in res]

    last_token = exchanges["ffn1"][4]
    for tag, after in (("ffn2", last_token), ("ffn1", grad_x)):
        part, land = landed(tag, after)
        finish(tag + "_w1", part, land, 0, FFN_SHARD, transposed=True)
        finish(tag + "_w3", part, land, FFN_SHARD, FFN_SHARD, transposed=True)
        finish(tag + "_w2", part, land, 2 * FFN_SHARD, FFN_SHARD)
        if tag == "ffn2":
            part_m, land_m = landed("mix", last_token)
            finish("w_in", part_m, land_m, MIX_WIN, LANES)
            finish("w_branch_b", part_m, land_m, MIX_WB, LANES)
            finish("w_out", part_m, land_m, MIX_WOUT, LANES)
            grads["w_branch_a"] = sum_chips(part_m, land_m, chip, MIX_WA, MIX_ROWS - MIX_WA, MIX_ROWS - MIX_WA,
                                            "w_branch_a_sum").reshape(w_branch_a.shape)
    loss_row = jnp.pad(loss_p, ((0, 0), (0, D - LANES)))
    smalls = small_all_gather(_pack_small(small_g, loss_row))
    small_sum = sum_slots(smalls, 0, N_DEV, N_DEV, "small_sum")
    small_shapes = {n: w[n].shape for n in SMALL}
    grads.update(_unpack_small(small_sum, small_shapes))
    loss = small_sum[7, 0]

    n = "w_branch_a"
    two_d = lambda a: a.reshape(w[n].shape[-2], w[n].shape[-1])
    d_, m_, v_ = adamw(two_d(w[n]), two_d(grads[n]), two_d(m[n]), two_d(v[n]), "adamw_" + n)
    delta[n], new_m[n], new_v[n] = [a.reshape(w[n].shape) for a in (d_, m_, v_)]
    zero_row = jnp.zeros((1, D), F32)
    pack = lambda t: _pack_small({n: t[n] for n in SMALL}, zero_row)
    d_, m_, v_ = adamw(pack(w), small_sum, pack(m), pack(v), "adamw_small")
    for src, dst in ((d_, delta), (m_, new_m), (v_, new_v)):
        dst.update(_unpack_small(src, small_shapes))

    return (loss, grad_x[None], *[grads[n] for n in ORDER], *[delta[n] for n in ORDER],
            *[new_m[n] for n in ORDER], *[new_v[n] for n in ORDER])
```

```python
import math

import jax
import jax.numpy as jnp
from jax import lax
from jax.experimental import pallas as pl
from jax.experimental.pallas import tpu as pltpu

F32 = jnp.float32
BF16 = jnp.bfloat16
MESH = pl.DeviceIdType.MESH

V7X_VMEM_BYTES = 64 * 1024 * 1024
VMEM_LIMIT = V7X_VMEM_BYTES - 8 * 1024 * 1024
LANES = 128

N_DEV = 8
EPS = 1e-6
NEG_INF = -1e30

DILATIONS = (1, 4, 16)
HALF_WINDOW = 64
HEAD_DIM_A = 64
HEADS_PER_GROUP_A = 8
GROUP_WIDTH_A = 512
A_QKV_WIDTH = 4608
A_GROUP_QKV = A_QKV_WIDTH // 3
A_TQ = 128
A_WIN = A_TQ + 2 * HALF_WINDOW
A_UNROLL = 8
A_SCALE = HEAD_DIM_A ** -0.5
WGRAD_TK = 2048
HEAD_DIM_B = 128
N_HEADS_B = 8
N_KV_B = 2
GQA_GROUP_B = 4
GRID_W = 64
ROPE_THETA = 10000.0
B_TQ_FWD = 256
B_TQ_BWD = 512
LOG2E = 1.4426950408889634
N_BUCKETS = 32
MAX_DISTANCE = 1024
PB_WIDTH = 3584
PB_GATE_A = 1536
PB_GATE_B = 2560

ADAM_LR = 0.001
ADAM_B1 = 0.9
ADAM_B2 = 0.999
ADAM_EPS = 1e-08
ADAM_WD = 0.01
ADAM_STEP = 10

FFN_SHARD = 352
MIX_WIN, MIX_WB, MIX_WOUT, MIX_WA = 0, 1024, 1152, 1280
MIX_ROWS = 1344


def _dot(a, b, ca=1, cb=0):
    return lax.dot_general(a, b, (((ca,), (cb,)), ((), ())), preferred_element_type=F32)


def _call(name, body, grid, ins, outs, scratch=(), sem=None, aliases=None):
    ins = [tuple(i) + (None,) * (4 - len(i)) for i in ins]
    res = pl.pallas_call(
        body,
        out_shape=[jax.ShapeDtypeStruct(s, d) for (s, d, _, _) in outs],
        grid=grid,
        in_specs=[pl.BlockSpec(bs, im, pipeline_mode=pm) for (_, bs, im, pm) in ins],
        out_specs=[pl.BlockSpec(bs, im) for (_, _, bs, im) in outs],
        scratch_shapes=list(scratch),
        name=name,
        input_output_aliases=aliases or {},
        compiler_params=pltpu.CompilerParams(dimension_semantics=sem, vmem_limit_bytes=VMEM_LIMIT),
    )(*[i[0] for i in ins])
    return res


def _sigmoid(x):
    return 1.0 / (1.0 + jnp.exp(-x))


def _position():
    return lax.axis_index("x"), lax.axis_index("y"), lax.axis_index("c")


def _hbm_specs(n):
    return [pl.BlockSpec(memory_space=pl.ANY) for _ in range(n)]


PAIR_BUFFERS = 4


def reduce_scatter_pair(grads, name):
    n = len(grads)
    C = grads[0].shape[2]
    half = [g.shape[1] // 2 for g in grads]
    chunks = [(i, q, hf) for i in range(n) for q in range(4) for hf in range(2)]
    nb = PAIR_BUFFERS

    def body(*refs):
        ins, theirs = refs[:n], refs[n:2 * n]
        buf, load_sems, send_sems, recv_sems = refs[2 * n:]
        x, y, c = _position()
        sibling = (x, y, 1 - c)

        def load(k):
            i, q, hf = chunks[k]
            r = half[i]
            return pltpu.make_async_copy(ins[i].at[2 * q + (1 - c), pl.ds(hf * r, r), :],
                                         buf.at[k % nb, pl.ds(0, r), :], load_sems.at[k % nb])

        def send(k):
            i, q, hf = chunks[k]
            r = half[i]
            return pltpu.make_async_remote_copy(
                src_ref=buf.at[k % nb, pl.ds(0, r), :], dst_ref=theirs[i].at[q, pl.ds(hf * r, r), :],
                send_sem=send_sems.at[k % nb], recv_sem=recv_sems.at[i],
                device_id=sibling, device_id_type=MESH)

        for k in range(len(chunks) + 1):
            if k < len(chunks):
                if k >= nb:
                    send(k - nb).wait_send()
                load(k).start()
            if k >= 1:
                load(k - 1).wait()
                send(k - 1).start()
        for k in range(max(0, len(chunks) - nb), len(chunks)):
            send(k).wait_send()
        for i in range(n):
            pltpu.make_async_remote_copy(
                src_ref=theirs[i], dst_ref=theirs[i], send_sem=send_sems.at[0], recv_sem=recv_sems.at[i],
                device_id=sibling, device_id_type=MESH).wait_recv()

    return pl.pallas_call(
        body,
        out_shape=[jax.ShapeDtypeStruct((4,) + g.shape[1:], g.dtype) for g in grads],
        in_specs=_hbm_specs(n),
        out_specs=_hbm_specs(n),
        scratch_shapes=[pltpu.VMEM((nb, max(half), C), grads[0].dtype), pltpu.SemaphoreType.DMA((nb,)),
                        pltpu.SemaphoreType.DMA((nb,)), pltpu.SemaphoreType.DMA((n,))],
        name=name,
        compiler_params=pltpu.CompilerParams(vmem_limit_bytes=VMEM_LIMIT),
    )(*grads)


_HBM_SPEC = pl.BlockSpec(memory_space=pltpu.HBM)
_SEM_SPEC = pl.BlockSpec(memory_space=pltpu.SEMAPHORE)
_TOKEN_SPEC = pl.BlockSpec(memory_space=pltpu.VMEM)
_DATAFLOW = pltpu.SideEffectType.DATAFLOW_SIDE_EFFECTING


def _split_start(name, body, src, land_shape):
    def full_body(src_ref, land_ref, send_sem, recv_sem, src_thru, land_thru, token):
        body(src_ref, land_ref, send_sem, recv_sem)
        token[...] = jnp.zeros_like(token)

    land = pltpu.with_memory_space_constraint(lax.empty(land_shape, src.dtype), pltpu.HBM)
    return pl.pallas_call(
        full_body, name=name,
        out_shape=(pltpu.SemaphoreType.DMA(()), pltpu.SemaphoreType.DMA(()),
                   pltpu.HBM(src.shape, src.dtype), pltpu.HBM(land_shape, src.dtype),
                   jax.ShapeDtypeStruct((8, LANES), F32)),
        in_specs=(_HBM_SPEC, _HBM_SPEC),
        out_specs=(_SEM_SPEC, _SEM_SPEC, _HBM_SPEC, _HBM_SPEC, _TOKEN_SPEC),
        input_output_aliases={0: 2, 1: 3},
        compiler_params=pltpu.CompilerParams(has_side_effects=_DATAFLOW),
    )(pltpu.with_memory_space_constraint(src, pltpu.HBM), land)


def _split_wait(name, started, n_blocks, after):
    send_sem, recv_sem, src_thru, land_thru, _ = started
    after = after if isinstance(after, tuple) else (after,)

    def body(src_ref, land_ref, send_sem, recv_sem, *rest):
        x, y, c = _position()
        blocks = land_ref.at[pl.ds(0, n_blocks)]
        copy = pltpu.make_async_remote_copy(src_ref=blocks, dst_ref=blocks, send_sem=send_sem, recv_sem=recv_sem,
                                            device_id=(x, y, c), device_id_type=MESH)
        copy.wait_send()
        copy.wait_recv()

    return pl.pallas_call(
        body, name=name,
        out_shape=(pltpu.HBM(src_thru.shape, src_thru.dtype), pltpu.HBM(land_thru.shape, land_thru.dtype)),
        in_specs=(_HBM_SPEC, _HBM_SPEC, _SEM_SPEC, _SEM_SPEC) + (pl.BlockSpec(memory_space=pl.ANY),) * len(after),
        out_specs=(_HBM_SPEC, _HBM_SPEC),
        input_output_aliases={0: 0, 1: 1},
        compiler_params=pltpu.CompilerParams(has_side_effects=_DATAFLOW),
    )(src_thru, land_thru, send_sem, recv_sem, *after)


def all_gather_start(block, name):
    def body(b_ref, land_ref, send_sem, recv_sem):
        x, y, c = _position()
        for peer in [(x, y, 1 - c), (1 - x, y, c), (x, 1 - y, c), (1 - x, 1 - y, c)]:
            pltpu.make_async_remote_copy(src_ref=b_ref, dst_ref=land_ref.at[4 * x + 2 * y + c],
                                         send_sem=send_sem, recv_sem=recv_sem,
                                         device_id=peer, device_id_type=MESH).start()

    return _split_start(name, body, block, (N_DEV,) + block.shape)


def all_gather_finish(block, land, name):
    R, C = block.shape

    def body(b_ref, land_in, land_ref, stage, load_sems, send_sems, recv_sems, own_sem):
        x, y, c = _position()
        sibling = (x, y, 1 - c)
        chips = [(1 - x, y), (x, 1 - y), (1 - x, 1 - y)]
        own_in = pltpu.make_async_copy(b_ref, stage.at[3], load_sems.at[3])
        own_in.start()
        loads = [pltpu.make_async_copy(land_in.at[4 * px + 2 * py + c], stage.at[j], load_sems.at[j])
                 for j, (px, py) in enumerate(chips)]
        for ld in loads:
            ld.start()
        sends = []
        for j, (px, py) in enumerate(chips):
            loads[j].wait()
            dst = land_ref.at[4 * px + 2 * py + c]
            cp = pltpu.make_async_remote_copy(src_ref=stage.at[j], dst_ref=dst, send_sem=send_sems.at[j],
                                              recv_sem=recv_sems.at[j], device_id=sibling, device_id_type=MESH)
            cp.start()
            sends.append(cp)
        own_in.wait()
        own_out = pltpu.make_async_copy(stage.at[3], land_ref.at[4 * x + 2 * y + c], own_sem)
        own_out.start()
        for j, (px, py) in enumerate(chips):
            dst = land_ref.at[4 * px + 2 * py + (1 - c)]
            pltpu.make_async_remote_copy(src_ref=stage.at[j], dst_ref=dst, send_sem=send_sems.at[j],
                                         recv_sem=recv_sems.at[j], device_id=sibling,
                                         device_id_type=MESH).wait_recv()
        for cp in sends:
            cp.wait_send()
        own_out.wait()

    return pl.pallas_call(
        body,
        out_shape=jax.ShapeDtypeStruct(land.shape, land.dtype),
        in_specs=_hbm_specs(2),
        out_specs=pl.BlockSpec(memory_space=pl.ANY),
        scratch_shapes=[pltpu.VMEM((4, R, C), block.dtype), pltpu.SemaphoreType.DMA((4,)),
                        pltpu.SemaphoreType.DMA((3,)), pltpu.SemaphoreType.DMA((3,)), pltpu.SemaphoreType.DMA],
        input_output_aliases={1: 0},
        name=name,
        compiler_params=pltpu.CompilerParams(vmem_limit_bytes=VMEM_LIMIT),
    )(block, land)


def reduce_scatter_start(parts, name):
    def body(p_ref, land_ref, send_sem, recv_sem):
        x, y, c = _position()
        for px, py in [(1 - x, y), (x, 1 - y), (1 - x, 1 - y)]:
            pltpu.make_async_remote_copy(src_ref=p_ref.at[2 * px + py], dst_ref=land_ref.at[2 * x + y],
                                         send_sem=send_sem, recv_sem=recv_sem,
                                         device_id=(px, py, c), device_id_type=MESH).start()

    return _split_start(name, body, parts, parts.shape)


def small_all_gather(small):
    def body(small_ref, smalls, s_send, s_recv, s_local):
        x, y, c = _position()
        me = 4 * x + 2 * y + c
        lc = pltpu.make_async_copy(small_ref, smalls.at[me], s_local)
        lc.start()
        remote = []
        k = 0
        for dx in (0, 1):
            for dy in (0, 1):
                for dc in (0, 1):
                    if dx + dy + dc == 0:
                        continue
                    peer = (1 - x if dx else x, 1 - y if dy else y, 1 - c if dc else c)
                    rc = pltpu.make_async_remote_copy(
                        src_ref=small_ref, dst_ref=smalls.at[me],
                        send_sem=s_send.at[k], recv_sem=s_recv.at[k],
                        device_id=peer, device_id_type=MESH)
                    rc.start()
                    remote.append(rc)
                    k += 1
        for rc in remote:
            rc.wait()
        lc.wait()

    return pl.pallas_call(
        body,
        out_shape=jax.ShapeDtypeStruct((N_DEV,) + small.shape, small.dtype),
        in_specs=_hbm_specs(1),
        out_specs=pl.BlockSpec(memory_space=pl.ANY),
        scratch_shapes=[pltpu.SemaphoreType.DMA((7,)), pltpu.SemaphoreType.DMA((7,)), pltpu.SemaphoreType.DMA],
        name="small_all_gather",
    )(small)


def pair_add(grads, theirs, core, name):
    _, R, C = theirs.shape
    tr = R // 2

    def body(c_ref, a_ref, b_ref, o_ref):
        o_ref[...] = (a_ref[...].astype(F32) + b_ref[...].astype(F32)).astype(BF16)

    return pl.pallas_call(
        body,
        out_shape=jax.ShapeDtypeStruct(theirs.shape, BF16),
        grid_spec=pltpu.PrefetchScalarGridSpec(
            num_scalar_prefetch=1, grid=(4, R // tr),
            in_specs=[pl.BlockSpec((None, tr, C), lambda q, i, c: (2 * q + c[0], i, 0)),
                      pl.BlockSpec((None, tr, C), lambda q, i, c: (q, i, 0))],
            out_specs=pl.BlockSpec((None, tr, C), lambda q, i, c: (q, i, 0))),
        name=name,
        compiler_params=pltpu.CompilerParams(dimension_semantics=("parallel", "parallel"),
                                             vmem_limit_bytes=VMEM_LIMIT),
    )(core, grads, theirs)


def sum_slots(recv, off, rows, blk, name):
    nq, _, C = recv.shape
    ob = off // blk

    def body(r_ref, o_ref):
        acc = r_ref[0].astype(F32)
        for q in range(1, nq):
            acc = acc + r_ref[q].astype(F32)
        o_ref[...] = acc

    return _call(name, body, (rows // blk,),
                 [(recv, (nq, blk, C), lambda i: (0, ob + i, 0))],
                 [((rows, C), F32, (blk, C), lambda i: (i, 0))], sem=("parallel",))[0]


def sum_chips(parts, land, chip, off, rows, blk, name):
    C = parts.shape[2]
    ob = off // blk

    def body(c_ref, own_ref, a_ref, b_ref, d_ref, o_ref):
        o_ref[...] = ((own_ref[...].astype(F32) + a_ref[...].astype(F32)) + b_ref[...].astype(F32)) \
            + d_ref[...].astype(F32)

    def entry(flip):
        return pl.BlockSpec((None, blk, C), lambda i, c: (c[0] ^ flip, ob + i, 0))

    return pl.pallas_call(
        body,
        out_shape=jax.ShapeDtypeStruct((rows, C), F32),
        grid_spec=pltpu.PrefetchScalarGridSpec(
            num_scalar_prefetch=1, grid=(rows // blk,),
            in_specs=[entry(0), entry(1), entry(2), entry(3)],
            out_specs=pl.BlockSpec((blk, C), lambda i, c: (i, 0))),
        name=name,
        compiler_params=pltpu.CompilerParams(dimension_semantics=("parallel",), vmem_limit_bytes=VMEM_LIMIT),
    )(chip, parts, land, land, land)


def _adamw_update(wv, gv, mv, vv):
    nm = ADAM_B1 * mv + (1.0 - ADAM_B1) * gv
    nv = ADAM_B2 * vv + (1.0 - ADAM_B2) * (gv * gv)
    c1 = 1.0 / (1.0 - ADAM_B1 ** ADAM_STEP)
    c2 = 1.0 / (1.0 - ADAM_B2 ** ADAM_STEP)
    return -ADAM_LR * ((nm * c1) / (jnp.sqrt(nv * c2) + ADAM_EPS) + ADAM_WD * wv), nm, nv


def sum_adamw(parts, land, chip, off, blk, w, m, v, name):
    rows, C = w.shape
    ob = off // blk

    def body(c_ref, own_ref, a_ref, b_ref, d_ref, w_ref, m_ref, v_ref, g_out, d_out, m_out, v_out):
        gv = ((own_ref[...].astype(F32) + a_ref[...].astype(F32)) + b_ref[...].astype(F32)) \
            + d_ref[...].astype(F32)
        g_out[...] = gv
        d_out[...], m_out[...], v_out[...] = _adamw_update(w_ref[...], gv, m_ref[...], v_ref[...])

    def entry(flip):
        return pl.BlockSpec((None, blk, C), lambda i, c: (c[0] ^ flip, ob + i, 0))

    plain = pl.BlockSpec((blk, C), lambda i, c: (i, 0))
    return pl.pallas_call(
        body,
        out_shape=[jax.ShapeDtypeStruct((rows, C), F32)] * 4,
        grid_spec=pltpu.PrefetchScalarGridSpec(
            num_scalar_prefetch=1, grid=(rows // blk,),
            in_specs=[entry(0), entry(1), entry(2), entry(3), plain, plain, plain],
            out_specs=[plain] * 4),
        name=name,
        compiler_params=pltpu.CompilerParams(dimension_semantics=("parallel",), vmem_limit_bytes=VMEM_LIMIT),
    )(chip, parts, land, land, land, w, m, v)


def adamw(w, g, m, v, name):
    R, C = w.shape
    tr = R
    for cand in (256, 128, 64, 32, 16, 8):
        if R % cand == 0 and R > cand:
            tr = cand
            break

    def body(w_ref, g_ref, m_ref, v_ref, d_ref, nm_ref, nv_ref):
        d_ref[...], nm_ref[...], nv_ref[...] = _adamw_update(w_ref[...], g_ref[...], m_ref[...], v_ref[...])

    spec = ((tr, C), lambda i: (i, 0))
    out = ((R, C), F32) + spec
    return _call(name, body, (R // tr,), [(w,) + spec, (g,) + spec, (m,) + spec, (v,) + spec],
                 [out, out, out], sem=("parallel",))


def rms_fwd(x, g, name):
    S, D = x.shape
    tr = 512

    def body(x_ref, g_ref, o_ref):
        xv = x_ref[...]
        r = lax.rsqrt(jnp.mean(xv * xv, axis=-1, keepdims=True) + EPS)
        o_ref[...] = (xv * r * g_ref[...]).astype(BF16)

    return _call(name, body, (S // tr,),
                 [(x, (tr, D), lambda i: (i, 0)), (g, (1, D), lambda i: (0, 0))],
                 [((S, D), BF16, (tr, D), lambda i: (i, 0))], sem=("parallel",))[0]


def _rms_bwd_tile(dn, xv, gv):
    r = lax.rsqrt(jnp.mean(xv * xv, axis=-1, keepdims=True) + EPS)
    xh = xv * r
    dxh = dn * gv
    dx = r * (dxh - xh * jnp.mean(dxh * xh, axis=-1, keepdims=True))
    return dx, dn * xh


def final_loss(x, tgt, g, name):
    S, D = x.shape
    tr = 256

    def body(x_ref, t_ref, g_ref, l_ref, dx_ref, dg_ref):
        i = pl.program_id(0)
        xv, gv = x_ref[...], g_ref[...]
        r = lax.rsqrt(jnp.mean(xv * xv, axis=-1, keepdims=True) + EPS)
        xh = xv * r
        e = xh * gv - t_ref[...]
        part = 0.5 * jnp.sum(jnp.sum(e * e, axis=-1, keepdims=True) * (1.0 / D), axis=0, keepdims=True)
        dy = e * (1.0 / D)
        dxh = dy * gv
        dx_ref[...] = r * (dxh - xh * jnp.mean(dxh * xh, axis=-1, keepdims=True))
        dgp = jnp.sum(dy * xh, axis=0, keepdims=True)

        @pl.when(i == 0)
        def _():
            l_ref[...] = jnp.broadcast_to(part, l_ref.shape)
            dg_ref[...] = dgp

        @pl.when(i > 0)
        def _():
            l_ref[...] += jnp.broadcast_to(part, l_ref.shape)
            dg_ref[...] += dgp

    row = ((tr, D), lambda i: (i, 0))
    return _call(name, body, (S // tr,),
                 [(x,) + row, (tgt,) + row, (g, (1, D), lambda i: (0, 0))],
                 [((1, LANES), F32, (1, LANES), lambda i: (0, 0)), ((S, D), F32) + row,
                  ((1, D), F32, (1, D), lambda i: (0, 0))], sem=("arbitrary",))


FFN_TF = 4 * FFN_SHARD


def _ffn_pick(G, which):
    if isinstance(G, tuple):
        return (G[0], which) if which < 2 else (G[1], 0)
    return G, which


def _ffn_w_spec(G, which, imap):
    arr, blk = _ffn_pick(G, which)
    return (arr, (4, FFN_SHARD, arr.shape[2]), lambda *idx: (imap(*idx), blk, 0))


def _ffn_whole_w_spec(G, which):
    arr, blk = _ffn_pick(G, which)
    return (arr, (N_DEV, FFN_SHARD, arr.shape[2]), lambda *idx: (0, blk, 0))


def ffn_up(n, G, name):
    S, D = n.shape
    F = N_DEV * FFN_SHARD
    tm = 1024

    def body(n_ref, w1_ref, w3_ref, abh_ref):
        nv = n_ref[...]
        a = _dot(nv, w1_ref[...].reshape(FFN_TF, D), 1, 1).astype(BF16)
        b = _dot(nv, w3_ref[...].reshape(FFN_TF, D), 1, 1).astype(BF16)
        abh_ref[0] = a
        abh_ref[1] = b
        av, bv = a.astype(F32), b.astype(F32)
        abh_ref[2] = (av * _sigmoid(av) * bv).astype(BF16)

    return _call(name, body, (F // FFN_TF, S // tm),
                 [(n, (tm, D), lambda j, i: (i, 0)),
                  _ffn_w_spec(G, 0, lambda j, i: j), _ffn_w_spec(G, 1, lambda j, i: j)],
                 [((3, S, F), BF16, (3, tm, FFN_TF), lambda j, i: (0, i, j))],
                 sem=("parallel", "parallel"))[0]


def ffn_down(abh, G, x, name):
    _, S, F = abh.shape
    D = x.shape[1]
    tm = 512

    def body(h_ref, w2_ref, x_ref, o_ref):
        o_ref[...] = x_ref[...] + 0.5 * _dot(h_ref[...], w2_ref[...].reshape(F, D))

    return _call(name, body, (S // tm,),
                 [(abh, (None, tm, F), lambda i: (2, i, 0)), _ffn_whole_w_spec(G, 2),
                  (x, (tm, D), lambda i: (i, 0))],
                 [((S, D), F32, (tm, D), lambda i: (i, 0))], sem=("parallel",))[0]


def ffn_bwd_weights(dxo, abh, n, G, name):
    _, S, F = abh.shape
    D = dxo.shape[1]
    tm = 512
    nf = F // FFN_TF

    def down_body(d_ref, w2_ref, ab_ref, o_ref):
        dh = 0.5 * _dot(d_ref[...].astype(BF16), w2_ref[...].reshape(FFN_TF, D), 1, 1)
        av, bv = ab_ref[0].astype(F32), ab_ref[1].astype(F32)
        sig = _sigmoid(av)
        o_ref[0] = (dh * bv * (sig * (1.0 + av * (1.0 - sig)))).astype(BF16)
        o_ref[1] = (dh * (av * sig)).astype(BF16)

    dab = _call(name + "_down_bwd", down_body, (nf, S // tm),
                [(dxo, (tm, D), lambda j, i: (i, 0)), _ffn_w_spec(G, 2, lambda j, i: j),
                 (abh, (2, tm, FFN_TF), lambda j, i: (0, i, j))],
                [((2, S, F), BF16, (2, tm, FFN_TF), lambda j, i: (0, i, j))],
                sem=("parallel", "parallel"))[0]

    tk = WGRAD_TK
    nk = S // tk
    gshape = (N_DEV, 3 * FFN_SHARD, D)

    def dw2_body(h_ref, d_ref, o_ref, acc_ref):
        k = pl.program_id(1)
        p = _dot(h_ref[...], d_ref[...].astype(BF16), 0, 0)

        @pl.when(k == 0)
        def _():
            acc_ref[...] = p

        @pl.when(k > 0)
        def _():
            acc_ref[...] += p

        @pl.when(k == nk - 1)
        def _():
            o_ref[...] = (0.5 * acc_ref[...]).astype(BF16).reshape(4, FFN_SHARD, D)

    gw = _call(name + "_dw2", dw2_body, (nf, nk),
               [(abh, (None, tk, FFN_TF), lambda j, k: (2, k, j)), (dxo, (tk, D), lambda j, k: (k, 0))],
               [(gshape, BF16, (4, FFN_SHARD, D), lambda j, k: (j, 2, 0))],
               scratch=[pltpu.VMEM((FFN_TF, D), F32)], sem=("parallel", "arbitrary"))[0]

    def dw13_body(gw_ref, dab_ref, n_ref, o_ref):
        o_ref[...] = _dot(dab_ref[...], n_ref[...], 0, 0).astype(BF16).reshape(4, FFN_SHARD, D)

    gw = pl.pallas_call(
        dw13_body,
        out_shape=jax.ShapeDtypeStruct(gshape, BF16),
        grid=(2, nf),
        in_specs=[pl.BlockSpec(memory_space=pl.ANY),
                  pl.BlockSpec((None, S, FFN_TF), lambda w, j: (w, 0, j)),
                  pl.BlockSpec((S, D), lambda w, j: (0, 0))],
        out_specs=pl.BlockSpec((4, FFN_SHARD, D), lambda w, j: (j, w, 0)),
        input_output_aliases={0: 0},
        name=name + "_dw13",
        compiler_params=pltpu.CompilerParams(dimension_semantics=("parallel", "parallel"),
                                             vmem_limit_bytes=VMEM_LIMIT),
    )(gw, dab, n)
    return dab, gw


def ffn_bwd_input(dab, G, x_in, g, dxo, name):
    _, S, F = dab.shape
    D = x_in.shape[1]
    tm = 256

    def dn_body(dab_ref, w1_ref, w3_ref, x_ref, d_ref, g_ref, dx_ref, dg_ref):
        i = pl.program_id(0)
        dn = _dot(dab_ref[0], w1_ref[...].reshape(F, D)) + _dot(dab_ref[1], w3_ref[...].reshape(F, D))
        dx, dgt = _rms_bwd_tile(dn, x_ref[...], g_ref[...])
        dx_ref[...] = d_ref[...] + dx
        dgp = jnp.sum(dgt, axis=0, keepdims=True)

        @pl.when(i == 0)
        def _():
            dg_ref[...] = dgp

        @pl.when(i > 0)
        def _():
            dg_ref[...] += dgp

    dx, dg = _call(name + "_dn", dn_body, (S // tm,),
                   [(dab, (2, tm, F), lambda i: (0, i, 0)),
                    _ffn_whole_w_spec(G, 0), _ffn_whole_w_spec(G, 1),
                    (x_in, (tm, D), lambda i: (i, 0)), (dxo, (tm, D), lambda i: (i, 0)),
                    (g, (1, D), lambda i: (0, 0))],
                   [((S, D), F32, (tm, D), lambda i: (i, 0)), ((1, D), F32, (1, D), lambda i: (0, 0))],
                   sem=("arbitrary",))
    return dx, dg


PROJ_TN = 512
DH_SHARDS_PER_STEP = 4


def in_proj(h, Gm, first_tile, n_tiles, dtype, name, tile_stride=1):
    S, D = h.shape
    tile = lambda j: first_tile + tile_stride * j

    def body(h_ref, w_ref, o_ref):
        o_ref[...] = _dot(h_ref[...], w_ref[...]).astype(dtype)

    return _call(name, body, (n_tiles,),
                 [(h, (S, D), lambda j: (0, 0)),
                  (Gm, (None, D, PROJ_TN), lambda j: (tile(j) // 2, 0, tile(j) % 2))],
                 [((S, n_tiles * PROJ_TN), dtype, (S, PROJ_TN), lambda j: (0, j))],
                 sem=("parallel",))[0]


def in_proj_bwd_dw(dproj, h, gm_grads, name):
    S, D = h.shape
    NT = dproj.shape[1] // PROJ_TN

    def dw_body(gm_ref, h_ref, d_ref, o_ref):
        o_ref[...] = _dot(h_ref[...], d_ref[...], 0, 0).astype(BF16)

    return pl.pallas_call(
        dw_body,
        out_shape=jax.ShapeDtypeStruct(gm_grads.shape, BF16),
        grid=(NT,),
        in_specs=[pl.BlockSpec(memory_space=pl.ANY),
                  pl.BlockSpec((S, D), lambda j: (0, 0)),
                  pl.BlockSpec((S, PROJ_TN), lambda j: (0, j))],
        out_specs=pl.BlockSpec((None, D, PROJ_TN), lambda j: (j // 2, 0, j % 2)),
        input_output_aliases={0: 0},
        name=name + "_dw",
        compiler_params=pltpu.CompilerParams(dimension_semantics=("parallel",), vmem_limit_bytes=VMEM_LIMIT),
    )(gm_grads, h, dproj)


def in_proj_bwd_dh(dproj, Gm, x_in, g, dres, name):
    S, D = x_in.shape
    tm = 256
    C = Gm.shape[2]
    n_sh = dproj.shape[1] // C

    def dh_body(d_ref, w_ref, x_ref, r_ref, g_ref, dx_ref, dg_ref):
        i = pl.program_id(0)
        p = _dot(d_ref[:, 0:C], w_ref[0], 1, 1)
        for s in range(1, n_sh):
            p = p + _dot(d_ref[:, s * C:(s + 1) * C], w_ref[s], 1, 1)
        dx, dgt = _rms_bwd_tile(p, x_ref[...], g_ref[...])
        dx_ref[...] = r_ref[...] + dx
        dgp = jnp.sum(dgt, axis=0, keepdims=True)

        @pl.when(i == 0)
        def _():
            dg_ref[...] = dgp

        @pl.when(i > 0)
        def _():
            dg_ref[...] += dgp

    dx, dg = _call(name + "_dh", dh_body, (S // tm,),
                   [(dproj, (tm, n_sh * C), lambda i: (i, 0)),
                    (Gm, (n_sh, D, C), lambda i: (0, 0, 0), pl.Buffered(1)),
                    (x_in, (tm, D), lambda i: (i, 0)), (dres, (tm, D), lambda i: (i, 0)),
                    (g, (1, D), lambda i: (0, 0))],
                   [((S, D), F32, (tm, D), lambda i: (i, 0)), ((1, D), F32, (1, D), lambda i: (0, 0))],
                   sem=("arbitrary",))
    return dx, dg


def _t5_bucket(rel):
    n = N_BUCKETS // 2
    max_exact = n // 2
    ret = jnp.where(rel > 0, n, 0)
    a = jnp.abs(rel)
    af = jnp.maximum(a, 1).astype(F32)
    large = max_exact + (jnp.log(af / max_exact) / math.log(MAX_DISTANCE / max_exact)
                         * (n - max_exact)).astype(jnp.int32)
    large = jnp.minimum(large, n - 1)
    return ret + jnp.where(a < max_exact, a, large)


def _bucket_tables():
    qi = jnp.arange(A_TQ, dtype=jnp.int32)[:, None]
    kj = jnp.arange(A_WIN, dtype=jnp.int32)[None, :]
    rel = kj - HALF_WINDOW - qi
    return jnp.stack([_t5_bucket(rel * d) for d in DILATIONS], axis=0)


def bias_build(rel_bias, buckets):
    def body(tab_ref, bk_ref, o_ref):
        col = pl.program_id(0) * HEADS_PER_GROUP_A + pl.program_id(1)
        bk = bk_ref[...]
        acc = jnp.zeros(bk.shape, F32)
        for b in range(N_BUCKETS):
            acc = jnp.where(bk == b, tab_ref[b, col], acc)
        qi = lax.broadcasted_iota(jnp.int32, bk.shape, 0)
        kj = lax.broadcasted_iota(jnp.int32, bk.shape, 1)
        band = jnp.where(jnp.abs(kj - HALF_WINDOW - qi) <= HALF_WINDOW, acc, NEG_INF)
        o_ref[0] = jnp.where(kj >= HALF_WINDOW, band, NEG_INF)
        o_ref[1] = band
        o_ref[2] = jnp.where(kj < A_TQ + HALF_WINDOW, band, NEG_INF)

    out = pl.pallas_call(
        body,
        out_shape=jax.ShapeDtypeStruct((3, HEADS_PER_GROUP_A // 2, 3, 2, A_TQ, A_WIN), F32),
        grid=(3, HEADS_PER_GROUP_A),
        in_specs=[pl.BlockSpec(memory_space=pltpu.SMEM),
                  pl.BlockSpec((None, A_TQ, A_WIN), lambda g, h: (g, 0, 0))],
        out_specs=pl.BlockSpec((None, None, 3, None, A_TQ, A_WIN), lambda g, h: (g, h // 2, 0, h % 2, 0, 0)),
        name="a_bias_build",
        compiler_params=pltpu.CompilerParams(dimension_semantics=("parallel", "parallel")),
    )(rel_bias, buckets)
    return out.reshape(3, HEADS_PER_GROUP_A // 2, 3, 2 * A_TQ, A_WIN)


def bias_bwd(dbias, buckets):
    def body(d_ref, bk_ref, o_ref):
        bk = bk_ref[...]
        dv = d_ref[...]
        for b in range(N_BUCKETS):
            part = jnp.sum(jnp.where(bk == b, dv, 0.0), axis=1, keepdims=True)
            o_ref[b:b + 1, :] = jnp.broadcast_to(jnp.sum(part, axis=0, keepdims=True), (1, LANES))

    out = pl.pallas_call(
        body,
        out_shape=jax.ShapeDtypeStruct((3, HEADS_PER_GROUP_A, N_BUCKETS, LANES), F32),
        grid=(3, HEADS_PER_GROUP_A),
        in_specs=[pl.BlockSpec((None, None, A_TQ, A_WIN), lambda g, h: (g, h, 0, 0)),
                  pl.BlockSpec((None, A_TQ, A_WIN), lambda g, h: (g, 0, 0))],
        out_specs=pl.BlockSpec((None, None, N_BUCKETS, LANES), lambda g, h: (g, h, 0, 0)),
        name="a_bias_bwd",
        compiler_params=pltpu.CompilerParams(dimension_semantics=("parallel", "parallel")),
    )(dbias, buckets)
    return out[:, :, :, 0].transpose(2, 0, 1).reshape(N_BUCKETS, 3 * HEADS_PER_GROUP_A)


def _a_fill_padded(pad_ref, src_ref, n, pad):
    zeros = jnp.zeros((pad, LANES), pad_ref.dtype)
    pad_ref[0:pad, :] = zeros
    pad_ref[pad + n:2 * pad + n, :] = zeros
    pad_ref[pad:pad + n, :] = src_ref[...].astype(pad_ref.dtype)


def _a_stack_heads(x, lane):
    zero = jnp.zeros_like(x)
    return jnp.concatenate([jnp.where(lane < HEAD_DIM_A, x, zero), jnp.where(lane >= HEAD_DIM_A, x, zero)], axis=0)


def _a_bias_variant(qb, nqb):
    return jnp.where(qb == 0, 0, jnp.where(qb == nqb - 1, 2, 1))


def a_fwd(proj_g, bias_g, g, name):
    S = proj_g.shape[0]
    d = DILATIONS[g]
    L = S // d
    nqb = L // A_TQ
    pad = HALF_WINDOW * d

    def body(q_ref, k_ref, v_ref, b_ref, o_ref, l_ref, qf, kpad, vpad):
        qf[...] = q_ref[...].astype(F32) * A_SCALE
        _a_fill_padded(kpad, k_ref, S, pad)
        _a_fill_padded(vpad, v_ref, S, pad)
        lane = lax.broadcasted_iota(jnp.int32, (A_TQ, LANES), 1)

        def block(t, carry):
            qb, r = t // d, t % d
            start = qb * (A_TQ * d) + r
            kw = kpad[pl.ds(start, A_WIN, stride=d), :].astype(BF16)
            vw = vpad[pl.ds(start, A_WIN, stride=d), :].astype(BF16)
            q = qf[pl.ds(start, A_TQ, stride=d), :].astype(BF16)
            q2 = _a_stack_heads(q, lane)
            s = _dot(q2, kw, 1, 1) + b_ref[_a_bias_variant(qb, nqb)]
            m = jnp.max(s, axis=-1, keepdims=True)
            e = jnp.exp(s - m)
            l = jnp.sum(e, axis=-1, keepdims=True)
            o2 = _dot(e.astype(BF16), vw) / l
            lse2 = m + jnp.log(l)
            o_ref[pl.ds(start, A_TQ, stride=d), :] = jnp.where(lane < HEAD_DIM_A, o2[0:A_TQ], o2[A_TQ:])
            l_ref[pl.ds(start, A_TQ, stride=d), :] = jnp.where(lane < HEAD_DIM_A, lse2[0:A_TQ], lse2[A_TQ:])
            return carry

        lax.fori_loop(0, nqb * d, block, 0, unroll=A_UNROLL)

    out_spec = ((S, GROUP_WIDTH_A), F32, (S, LANES), lambda hp: (0, hp))
    return _call(name, body, (4,),
                 [(proj_g, (S, LANES), lambda hp: (0, hp)),
                  (proj_g, (S, LANES), lambda hp: (0, 4 + hp)),
                  (proj_g, (S, LANES), lambda hp: (0, 8 + hp)),
                  (bias_g, (None, 3, 2 * A_TQ, A_WIN), lambda hp: (hp, 0, 0, 0))],
                 [out_spec, out_spec],
                 scratch=[pltpu.VMEM((S, LANES), F32)] + [pltpu.VMEM((S + 2 * pad, LANES), F32)] * 2,
                 sem=("parallel",))


def a_combine(outs, lses, name):
    S, W = outs[0].shape
    tr = 512

    def body(o0, o1, o2, l0, l1, l2, oa_ref, lt_ref):
        a, b, c = l0[...], l1[...], l2[...]
        m = jnp.maximum(jnp.maximum(a, b), c)
        ea, eb, ec = jnp.exp(a - m), jnp.exp(b - m), jnp.exp(c - m)
        z = ea + eb + ec
        oa_ref[...] = ((ea * o0[...] + eb * o1[...] + ec * o2[...]) / z).astype(BF16)
        lt_ref[...] = m + jnp.log(z)

    spec = ((tr, W), lambda i: (i, 0))
    return _call(name, body, (S // tr,), [(a,) + spec for a in (*outs, *lses)],
                 [((S, W), BF16) + spec, ((S, W), F32) + spec], sem=("parallel",))


def a_bwd(proj_g, bias_g, do_a, o_a, lse_tot, g, name):
    S = proj_g.shape[0]
    d = DILATIONS[g]
    L = S // d
    nqb = L // A_TQ
    pad = HALF_WINDOW * d

    def body(q_ref, k_ref, v_ref, b_ref, do_ref, o_ref, l_ref, dq_ref, dk_ref, dv_ref, db_ref,
             qf, of, dqf, kpad, vpad, dkacc, dvacc):
        qf[...] = q_ref[...].astype(F32) * A_SCALE
        of[...] = o_ref[...].astype(F32)
        _a_fill_padded(kpad, k_ref, S, pad)
        _a_fill_padded(vpad, v_ref, S, pad)
        dkacc[...] = jnp.zeros(dkacc.shape, F32)
        dvacc[...] = jnp.zeros(dvacc.shape, F32)
        db_ref[...] = jnp.zeros(db_ref.shape, F32)
        lane = lax.broadcasted_iota(jnp.int32, (A_TQ, LANES), 1)

        def block(t, carry):
            qb, r = t // d, t % d
            start = qb * (A_TQ * d) + r
            rows = pl.ds(start, A_TQ, stride=d)
            win = pl.ds(start, A_WIN, stride=d)
            kw = kpad[win, :].astype(BF16)
            vw = vpad[win, :].astype(BF16)
            q = qf[rows, :].astype(BF16)
            do = do_ref[rows, :]
            ov = of[rows, :]
            lt = l_ref[rows, :]
            q2 = _a_stack_heads(q, lane)
            do2 = _a_stack_heads(do, lane)
            lt2 = jnp.concatenate([lt[:, 0:1], lt[:, HEAD_DIM_A:HEAD_DIM_A + 1]], axis=0)
            s = _dot(q2, kw, 1, 1) + b_ref[_a_bias_variant(qb, nqb)]
            p = jnp.exp(s - lt2)
            t = jnp.sum(do2 * jnp.concatenate([ov, ov], axis=0), axis=-1, keepdims=True)
            dob2 = do2.astype(BF16)
            ds = p * (_dot(dob2, vw, 1, 1) - t)
            db_ref[...] += ds
            dsb = ds.astype(BF16)
            dq2 = _dot(dsb, kw)
            dqf[rows, :] = jnp.where(lane < HEAD_DIM_A, dq2[0:A_TQ], dq2[A_TQ:]) * A_SCALE
            dkacc[win, :] += _dot(dsb, q2, 0, 0)
            dvacc[win, :] += _dot(p.astype(BF16), dob2, 0, 0)
            return carry

        lax.fori_loop(0, nqb * d, block, 0, unroll=A_UNROLL)
        dq_ref[...] = dqf[...].astype(BF16)
        dk_ref[...] = dkacc[pad:pad + S, :].astype(BF16)
        dv_ref[...] = dvacc[pad:pad + S, :].astype(BF16)

    slab = ((S, LANES), lambda hp: (0, hp))
    oshape = (S, GROUP_WIDTH_A)
    padded = pltpu.VMEM((S + 2 * pad, LANES), F32)
    return _call(
        name, body, (4,),
        [(proj_g, (S, LANES), lambda hp: (0, hp)),
         (proj_g, (S, LANES), lambda hp: (0, 4 + hp)),
         (proj_g, (S, LANES), lambda hp: (0, 8 + hp)),
         (bias_g, (None, 3, 2 * A_TQ, A_WIN), lambda hp: (hp, 0, 0, 0)),
         (do_a,) + slab, (o_a,) + slab, (lse_tot,) + slab],
        [(oshape, BF16) + slab, (oshape, BF16) + slab, (oshape, BF16) + slab,
         ((4, 2 * A_TQ, A_WIN), F32, (None, 2 * A_TQ, A_WIN), lambda hp: (hp, 0, 0))],
        scratch=[pltpu.VMEM((S, LANES), F32)] * 3 + [padded] * 4,
        sem=("parallel",))


def _rope_tables(S):
    rows = S // GRID_W
    row = jnp.repeat(jnp.arange(rows, dtype=F32), GRID_W)
    col = jnp.tile(jnp.arange(GRID_W, dtype=F32), rows)
    n_freq = HEAD_DIM_B // 4
    freq = ROPE_THETA ** (-jnp.arange(n_freq, dtype=F32) / n_freq)
    ang = jnp.concatenate([row[:, None] * freq, col[:, None] * freq], axis=-1)
    cos, sin = jnp.cos(ang), jnp.sin(ang)
    return jnp.repeat(cos, 2, axis=-1), jnp.stack([-sin, sin], axis=-1).reshape(S, HEAD_DIM_B)


def _swap_pairs(y):
    lane = lax.broadcasted_iota(jnp.int32, y.shape, 1)
    return jnp.where(lane % 2 == 0, pltpu.roll(y, LANES - 1, 1), pltpu.roll(y, 1, 1))


def qkv_prep(proj_b, gains, cos_t, sin_t, name):
    S = proj_b.shape[0]
    ts = 256
    n_rot = N_HEADS_B + N_KV_B
    nh = n_rot + N_KV_B
    W = nh * LANES

    def body(x_ref, g_ref, c_ref, s_ref, o_ref):
        cv, sv = c_ref[...], s_ref[...]
        for hb in range(nh):
            cols = slice(hb * LANES, (hb + 1) * LANES)
            xv = x_ref[:, cols]
            if hb < n_rot:
                r = lax.rsqrt(jnp.mean(xv * xv, axis=-1, keepdims=True) + EPS)
                yv = xv * r * g_ref[:, cols]
                o_ref[:, cols] = (yv * cv + _swap_pairs(yv) * sv).astype(BF16)
            else:
                o_ref[:, cols] = xv.astype(BF16)

    return _call(name, body, (S // ts,),
                 [(proj_b, (ts, W), lambda i: (i, 0)), (gains, (1, W), lambda i: (0, 0)),
                  (cos_t, (ts, LANES), lambda i: (i, 0)), (sin_t, (ts, LANES), lambda i: (i, 0))],
                 [((S, W), BF16, (ts, W), lambda i: (i, 0))],
                 sem=("parallel",))[0]


def qk_prep_bwd(dr, proj_b, col0, gain, cos_t, sin_t, name):
    S, W = dr.shape
    H = W // LANES
    ts = 256
    xb = (col0 * LANES) // W

    def body(d_ref, x_ref, g_ref, c_ref, s_ref, dx_ref, dg_ref):
        i = pl.program_id(0)
        cv, sv, gv = c_ref[...], s_ref[...], g_ref[...]
        dgp = jnp.zeros((1, LANES), F32)
        for hb in range(H):
            cols = slice(hb * LANES, (hb + 1) * LANES)
            dout = d_ref[:, cols]
            dy = dout * cv + _swap_pairs(dout * sv)
            dx, dgt = _rms_bwd_tile(dy, x_ref[:, cols], gv)
            dx_ref[:, cols] = dx.astype(BF16)
            dgp = dgp + jnp.sum(dgt, axis=0, keepdims=True)

        @pl.when(i == 0)
        def _():
            dg_ref[...] = dgp

        @pl.when(i > 0)
        def _():
            dg_ref[...] += dgp

    return _call(name, body, (S // ts,),
                 [(dr, (ts, W), lambda i: (i, 0)), (proj_b, (ts, W), lambda i: (i, xb)),
                  (gain, (1, LANES), lambda i: (0, 0)),
                  (cos_t, (ts, LANES), lambda i: (i, 0)), (sin_t, (ts, LANES), lambda i: (i, 0))],
                 [((S, W), BF16, (ts, W), lambda i: (i, 0)),
                  ((1, LANES), F32, (1, LANES), lambda i: (0, 0))],
                 sem=("arbitrary",))


def _row_sums(x):
    hi = x.astype(BF16)
    lo = (x - hi.astype(F32)).astype(BF16)
    ones = jnp.ones((8, LANES), BF16)
    return (_dot(ones, hi, 1, 1) + _dot(ones, lo, 1, 1))[0:1, :]


def flash_fwd(qkv, name):
    S = qkv.shape[0]
    tq = B_TQ_FWD
    scale = HEAD_DIM_B ** -0.5

    def body(q_ref, k_ref, v_ref, o_ref, l_ref):
        s = _dot(q_ref[...], k_ref[...], 1, 1)
        m = jnp.max(s, axis=-1, keepdims=True)
        e = jnp.exp2((s - m) * (scale * LOG2E))
        l = jnp.sum(e, axis=-1, keepdims=True)
        o_ref[...] = (_dot(e.astype(BF16), v_ref[...]) / l).astype(BF16)
        lse = jnp.broadcast_to(m * scale + jnp.log(l), (tq, LANES))
        l_ref[...] = _row_sums(lse) * (1.0 / LANES)

    head = lambda g, h, i: (i, g * GQA_GROUP_B + h)
    return _call(name, body, (N_KV_B, GQA_GROUP_B, S // tq),
                 [(qkv, (tq, LANES), head),
                  (qkv, (S, LANES), lambda g, h, i: (0, N_HEADS_B + g)),
                  (qkv, (S, LANES), lambda g, h, i: (0, N_HEADS_B + N_KV_B + g))],
                 [((S, N_HEADS_B * LANES), BF16, (tq, LANES), head),
                  ((N_HEADS_B, 1, S), F32, (None, 1, tq), lambda g, h, i: (g * GQA_GROUP_B + h, 0, i))],
                 sem=("parallel", "parallel", "parallel"))


def flash_bwd(qkv, k_t, do_b, o_b, lse, name):
    S = qkv.shape[0]
    tq = B_TQ_BWD
    nq = S // tq
    scale = HEAD_DIM_B ** -0.5

    def body(q_ref, k_ref, v_ref, kt_ref, do_ref, o_ref, l_ref, dq_ref, dk_ref, dv_ref, dkacc, dvacc):
        h, i = pl.program_id(1), pl.program_id(2)

        @pl.when((h == 0) & (i == 0))
        def _():
            dkacc[...] = jnp.zeros(dkacc.shape, F32)
            dvacc[...] = jnp.zeros(dvacc.shape, F32)

        q = q_ref[...]
        do = do_ref[...]
        dob = do.astype(BF16)
        t = _row_sums(do * o_ref[...].astype(F32))
        pt = jnp.exp2(_dot(k_ref[...], q, 1, 1) * (scale * LOG2E) - l_ref[...] * LOG2E)
        dsb = (pt * (_dot(v_ref[...], dob, 1, 1) - t)).astype(BF16)
        dvacc[...] += _dot(pt.astype(BF16), dob)
        dkacc[...] += _dot(dsb, q)
        dq_ref[...] = _dot(kt_ref[...], dsb).T * scale

        @pl.when((h == GQA_GROUP_B - 1) & (i == nq - 1))
        def _():
            dk_ref[...] = dkacc[...] * scale
            dv_ref[...] = dvacc[...].astype(BF16)

    head = lambda g, h, i: (i, g * GQA_GROUP_B + h)
    return _call(name, body, (N_KV_B, GQA_GROUP_B, nq),
                 [(qkv, (tq, LANES), head),
                  (qkv, (S, LANES), lambda g, h, i: (0, N_HEADS_B + g)),
                  (qkv, (S, LANES), lambda g, h, i: (0, N_HEADS_B + N_KV_B + g)),
                  (k_t, (LANES, S), lambda g, h, i: (g, 0)),
                  (do_b, (tq, LANES), head), (o_b, (tq, LANES), head),
                  (lse, (None, 1, tq), lambda g, h, i: (g * GQA_GROUP_B + h, 0, i))],
                 [((S, N_HEADS_B * LANES), F32, (tq, LANES), head),
                  ((S, N_KV_B * LANES), F32, (S, LANES), lambda g, h, i: (0, g)),
                  ((S, N_KV_B * LANES), BF16, (S, LANES), lambda g, h, i: (0, g))],
                 scratch=[pltpu.VMEM((S, LANES), F32)] * 2,
                 sem=("parallel", "arbitrary", "arbitrary"))


MERGE_TN = 512


def _mix_rows_spec(Gm, row0, n_slots, slot_map, cols=None, col_map=None):
    C = Gm.shape[2] if cols is None else cols
    cm = (lambda *idx: 0) if col_map is None else col_map
    return (Gm, (n_slots, LANES, C), lambda *idx: (slot_map(*idx), row0 // LANES, cm(*idx)))


def merge_fwd(o_a, o_b, w_a, Gm, proj_b, b_gate, name):
    S = o_a.shape[0]
    D = w_a.shape[1]
    tm, tn = 512, MERGE_TN
    ga0, gb0 = PB_GATE_A // tn, PB_GATE_B // tn

    def body(oa_ref, ob_ref, wa_ref, wb_ref, pa_ref, pb_ref, ba_ref, bb_ref, m_ref, ya_ref, yb_ref):
        ya = _dot(oa_ref[...], wa_ref[...])
        yb = _dot(ob_ref[...], wb_ref[...].reshape(N_DEV * LANES, tn))
        ga = _sigmoid(pa_ref[...] + ba_ref[...])
        gb = _sigmoid(pb_ref[...] + bb_ref[...])
        m_ref[...] = (ga * ya + gb * yb).astype(BF16)
        ya_ref[...] = ya.astype(BF16)
        yb_ref[...] = yb.astype(BF16)

    out = ((S, D), BF16, (tm, tn), lambda j, i: (i, j))
    return _call(name, body, (D // tn, S // tm),
                 [(o_a, (tm, o_a.shape[1]), lambda j, i: (i, 0)), (o_b, (tm, o_b.shape[1]), lambda j, i: (i, 0)),
                  (w_a, (w_a.shape[0], tn), lambda j, i: (0, j)),
                  _mix_rows_spec(Gm, MIX_WB, N_DEV, lambda j, i: 0, cols=tn, col_map=lambda j, i: j),
                  (proj_b, (tm, tn), lambda j, i: (i, ga0 + j)), (proj_b, (tm, tn), lambda j, i: (i, gb0 + j)),
                  (b_gate, (1, tn), lambda j, i: (0, j)), (b_gate, (1, tn), lambda j, i: (0, D // tn + j))],
                 [out, out, out], sem=("parallel", "parallel"))


def out_proj(merged, Gm, x, name):
    S, D = x.shape
    tm, tn = 512, MERGE_TN

    def body(m_ref, w_ref, x_ref, o_ref):
        o_ref[...] = x_ref[...] + _dot(m_ref[...], w_ref[...].reshape(N_DEV * LANES, tn))

    return _call(name, body, (D // tn, S // tm),
                 [(merged, (tm, D), lambda j, i: (i, 0)),
                  _mix_rows_spec(Gm, MIX_WOUT, N_DEV, lambda j, i: 0, cols=tn, col_map=lambda j, i: j),
                  (x, (tm, tn), lambda j, i: (i, j))],
                 [((S, D), F32, (tm, tn), lambda j, i: (i, j))], sem=("parallel", "parallel"))[0]


def merge_bwd(dx2, Gm, ya, yb, proj_b, b_gate, name):
    S, D = dx2.shape
    tm, tn = 512, MERGE_TN
    nn = D // tn
    ga0, gb0 = PB_GATE_A // tn, PB_GATE_B // tn

    def body(d_ref, w_ref, ya_ref, yb_ref, pa_ref, pb_ref, ba_ref, bb_ref, dya_ref, dyb_ref, dg_ref, dbg_ref):
        i = pl.program_id(1)
        dm = _dot(d_ref[...].astype(BF16), w_ref[...].reshape(tn, D), 1, 1)
        ga = _sigmoid(pa_ref[...] + ba_ref[...])
        gb = _sigmoid(pb_ref[...] + bb_ref[...])
        dya_ref[...] = (dm * ga).astype(BF16)
        dyb_ref[...] = (dm * gb).astype(BF16)
        dpa = dm * ya_ref[...].astype(F32) * ga * (1.0 - ga)
        dpb = dm * yb_ref[...].astype(F32) * gb * (1.0 - gb)
        dg_ref[0] = dpa.astype(BF16)
        dg_ref[1] = dpb.astype(BF16)
        sa = jnp.sum(dpa, axis=0, keepdims=True)
        sb = jnp.sum(dpb, axis=0, keepdims=True)

        @pl.when(i == 0)
        def _():
            dbg_ref[0] = sa
            dbg_ref[1] = sb

        @pl.when(i > 0)
        def _():
            dbg_ref[0] += sa
            dbg_ref[1] += sb

    tile = ((tm, tn), lambda j, i: (i, j))
    dya, dyb, dgate, dbg = _call(
        name, body, (nn, S // tm),
        [(dx2, (tm, D), lambda j, i: (i, 0)),
         _mix_rows_spec(Gm, MIX_WOUT, tn // LANES, lambda j, i: j),
         (ya,) + tile, (yb,) + tile,
         (proj_b, (tm, tn), lambda j, i: (i, ga0 + j)), (proj_b, (tm, tn), lambda j, i: (i, gb0 + j)),
         (b_gate, (1, tn), lambda j, i: (0, j)), (b_gate, (1, tn), lambda j, i: (0, nn + j))],
        [((S, D), BF16) + tile, ((S, D), BF16) + tile,
         ((2, S, D), BF16, (2, tm, tn), lambda j, i: (0, i, j)),
         ((2, 1, D), F32, (2, 1, tn), lambda j, i: (0, 0, j))],
        sem=("parallel", "arbitrary"))
    return dya, dyb, dgate, dbg


def matmul_nt(a, b_spec_fn, N, name, tn=512):
    S, K = a.shape
    tm = 512

    def body(a_ref, b_ref, o_ref):
        b = b_ref[...]
        o_ref[...] = _dot(a_ref[...], b.reshape(-1, b.shape[-1]), 1, 1)

    return _call(name, body, (N // tn, S // tm),
                 [(a, (tm, K), lambda j, i: (i, 0)), b_spec_fn(lambda j, i: j)],
                 [((S, N), F32, (tm, tn), lambda j, i: (i, j))], sem=("parallel", "parallel"))[0]


def weight_grad_rows(a, b, grads, row0, name):
    S, M = a.shape
    N = b.shape[1]
    tmm = 512
    tk = WGRAD_TK
    nk = S // tk

    def body(g_ref, a_ref, b_ref, o_ref, acc_ref):
        k = pl.program_id(1)
        p = _dot(a_ref[...], b_ref[...].astype(BF16), 0, 0)

        @pl.when(k == 0)
        def _():
            acc_ref[...] = p

        @pl.when(k > 0)
        def _():
            acc_ref[...] += p

        @pl.when(k == nk - 1)
        def _():
            o_ref[...] = acc_ref[...].astype(BF16).reshape(tmm // LANES, LANES, N)

    return pl.pallas_call(
        body,
        out_shape=jax.ShapeDtypeStruct(grads.shape, BF16),
        grid=(M // tmm, nk),
        in_specs=[pl.BlockSpec(memory_space=pl.ANY),
                  pl.BlockSpec((tk, tmm), lambda j, k: (k, j)),
                  pl.BlockSpec((tk, N), lambda j, k: (k, 0))],
        out_specs=pl.BlockSpec((tmm // LANES, LANES, N), lambda j, k: (j, row0 // LANES, 0)),
        scratch_shapes=[pltpu.VMEM((tmm, N), F32)],
        input_output_aliases={0: 0},
        name=name,
        compiler_params=pltpu.CompilerParams(dimension_semantics=("parallel", "arbitrary"),
                                             vmem_limit_bytes=VMEM_LIMIT),
    )(grads, a, b)


def weight_grad_plain(a, b, name):
    S, M = a.shape
    N = b.shape[1]
    tk = WGRAD_TK
    nk = S // tk

    def body(a_ref, b_ref, o_ref, acc_ref):
        k = pl.program_id(0)
        p = _dot(a_ref[...], b_ref[...], 0, 0)

        @pl.when(k == 0)
        def _():
            acc_ref[...] = p

        @pl.when(k > 0)
        def _():
            acc_ref[...] += p

        @pl.when(k == nk - 1)
        def _():
            o_ref[...] = acc_ref[...].astype(BF16)

    return _call(name, body, (nk,),
                 [(a, (tk, M), lambda k: (k, 0)), (b, (tk, N), lambda k: (k, 0))],
                 [((M, N), BF16, (M, N), lambda k: (0, 0))],
                 scratch=[pltpu.VMEM((M, N), F32)], sem=("arbitrary",))[0]


def local_step(x, tgt, p, get_g1_up, get_g1_down, get_gm, get_g2, emit, start_token):
    S, D = x.shape
    after = lambda t: t[0:1, 0:1]
    buckets = _bucket_tables()
    cos_t, sin_t = _rope_tables(S)
    gains = jnp.concatenate([jnp.tile(p["q_norm"], (1, N_HEADS_B)), jnp.tile(p["k_norm"], (1, N_KV_B)),
                             jnp.ones((1, N_KV_B * LANES), F32)], axis=1)

    n1 = rms_fwd(x, p["ffn1_norm"] + after(start_token), "ffn1_norm")
    bias = bias_build(p["rel_bias"] + after(start_token), buckets)
    g1_up = get_g1_up((n1, bias))
    ab1 = ffn_up(n1, (g1_up, None), "ffn1_up")
    G1 = (g1_up, get_g1_down(ab1))
    x1 = ffn_down(ab1, G1, x, "ffn1_down")

    Gm = get_gm(x1)
    w_a = Gm[:, MIX_WA:MIX_ROWS, :].reshape(N_DEV, GROUP_WIDTH_A, LANES).transpose(1, 0, 2).reshape(GROUP_WIDTH_A, D)
    hm = rms_fwd(x1, p["mix_norm"], "mix_norm")
    n_a = A_QKV_WIDTH // PROJ_TN
    proj_a = [in_proj(hm, Gm, g, 3, BF16, "in_proj_a%d" % g, tile_stride=3) for g in range(3)]
    proj_b = in_proj(hm, Gm, n_a, PB_WIDTH // PROJ_TN, F32, "in_proj_b")

    outs, lses = [], []
    for g in range(3):
        o, l = a_fwd(proj_a[g], bias[g], g, "a_fwd_%d" % g)
        outs.append(o)
        lses.append(l)
    o_a, lse_tot = a_combine(outs, lses, "a_combine")

    qkv = qkv_prep(proj_b, gains, cos_t, sin_t, "qkv_prep")
    k_t = qkv[:, N_HEADS_B * LANES:(N_HEADS_B + N_KV_B) * LANES].T
    o_b, lse_b = flash_fwd(qkv, "flash_fwd")

    merged, ya, yb = merge_fwd(o_a, o_b, w_a, Gm, proj_b, p["b_gate"], "merge_fwd")
    x2 = out_proj(merged, Gm, x1, "out_proj")

    G2 = get_g2(x2)
    n2 = rms_fwd(x2, p["ffn2_norm"], "ffn2_norm")
    ab2 = ffn_up(n2, G2, "ffn2_up")
    x3 = ffn_down(ab2, G2, x2, "ffn2_down")

    loss, dx3, d_final = final_loss(x3, tgt, p["final_norm"], "final_loss")

    dabh2, gw2 = ffn_bwd_weights(dx3, ab2, n2, G2, "ffn2_bwd")
    t2 = emit("ffn2", gw2)
    dx2, d_ffn2_norm = ffn_bwd_input(dabh2, G2, x2, p["ffn2_norm"] + after(t2), dx3, "ffn2_bwd")

    dya, dyb, dgate, dbg = merge_bwd(dx2, Gm, ya, yb, proj_b, p["b_gate"], "merge_bwd")
    gm_grads = jnp.zeros(Gm.shape, BF16)
    gm_grads = weight_grad_rows(merged, dx2, gm_grads, MIX_WOUT, "dw_out")
    gm_grads = weight_grad_rows(o_b, dyb, gm_grads, MIX_WB, "dw_branch_b")
    dw_a = weight_grad_plain(o_a, dya, "dw_branch_a")
    do_a = matmul_nt(dya, lambda jm: (w_a, (MERGE_TN, D), lambda j, i: (jm(j, i), 0)), GROUP_WIDTH_A, "do_a")
    do_b = matmul_nt(dyb, lambda jm: _mix_rows_spec(Gm, MIX_WB, MERGE_TN // LANES, jm), N_HEADS_B * LANES, "do_b")

    dq_r, dk_r, dv_b = flash_bwd(qkv, k_t, do_b, o_b, lse_b, "flash_bwd")
    dq_b, d_q_norm = qk_prep_bwd(dq_r, proj_b, 0, p["q_norm"], cos_t, sin_t, "q_prep_bwd")
    dk_b, d_k_norm = qk_prep_bwd(dk_r, proj_b, N_HEADS_B, p["k_norm"], cos_t, sin_t, "k_prep_bwd")

    dqs, dks, dvs, dbs = [], [], [], []
    for g in range(3):
        dq, dk, dv, db = a_bwd(proj_a[g], bias[g], do_a, o_a, lse_tot, g, "a_bwd_%d" % g)
        dqs.append(dq)
        dks.append(dk)
        dvs.append(dv)
        dbs.append(db)
    d_rel_bias = bias_bwd(jnp.stack(dbs, axis=0).reshape(3, HEADS_PER_GROUP_A, A_TQ, A_WIN), buckets)

    dproj = jnp.concatenate(dqs + dks + dvs + [dq_b, dk_b, dv_b, dgate[0], dgate[1]], axis=1)
    gm_grads = in_proj_bwd_dw(dproj, hm, gm_grads, "in_proj_bwd")
    dw_a_sh = dw_a.reshape(GROUP_WIDTH_A, N_DEV, LANES).transpose(1, 0, 2).reshape(N_DEV, MIX_ROWS - MIX_WA, D)
    gm_grads = lax.dynamic_update_slice(gm_grads, dw_a_sh, (0, MIX_WA, 0))
    tm = emit("mix", gm_grads)
    dx1, d_mix_norm = in_proj_bwd_dh(dproj, Gm, x1, p["mix_norm"] + after(tm), dx2, "in_proj_bwd")

    dabh1, gw1 = ffn_bwd_weights(dx1, ab1, n1, G1, "ffn1_bwd")
    t1 = emit("ffn1", gw1)
    dx0, d_ffn1_norm = ffn_bwd_input(dabh1, G1, x, p["ffn1_norm"] + after(t1), dx1, "ffn1_bwd")

    small = dict(ffn1_norm=d_ffn1_norm, mix_norm=d_mix_norm, b_gate=dbg.reshape(1, 2 * D),
                 q_norm=d_q_norm, k_norm=d_k_norm, rel_bias=d_rel_bias, ffn2_norm=d_ffn2_norm,
                 final_norm=d_final)
    return loss, dx0, small


def _pack_small(t, loss_row):
    row6 = jnp.concatenate([t["q_norm"].reshape(1, -1), t["k_norm"].reshape(1, -1), t["rel_bias"].reshape(1, -1)], axis=1)
    return jnp.concatenate([t["ffn1_norm"].reshape(1, -1), t["mix_norm"].reshape(1, -1), t["b_gate"].reshape(2, -1),
                            t["ffn2_norm"].reshape(1, -1), t["final_norm"].reshape(1, -1), row6, loss_row], axis=0)


def _unpack_small(a, shapes):
    return dict(ffn1_norm=a[0:1].reshape(shapes["ffn1_norm"]), mix_norm=a[1:2].reshape(shapes["mix_norm"]),
                b_gate=a[2:4].reshape(shapes["b_gate"]), ffn2_norm=a[4:5].reshape(shapes["ffn2_norm"]),
                final_norm=a[5].reshape(shapes["final_norm"]), q_norm=a[6:7, 0:128].reshape(shapes["q_norm"]),
                k_norm=a[6:7, 128:256].reshape(shapes["k_norm"]), rel_bias=a[6, 256:1024].reshape(shapes["rel_bias"]))


SMALL = ("ffn1_norm", "mix_norm", "b_gate", "q_norm", "k_norm", "rel_bias", "ffn2_norm", "final_norm")
ORDER = ("ffn1_norm", "ffn1_w1", "ffn1_w3", "ffn1_w2", "mix_norm", "w_in", "b_gate", "q_norm", "k_norm", "rel_bias",
         "w_branch_a", "w_branch_b", "w_out", "ffn2_norm", "ffn2_w1", "ffn2_w3", "ffn2_w2", "final_norm")


def kernel(x, ffn1_norm, ffn1_w1, ffn1_w3, ffn1_w2, mix_norm, w_in, b_gate, q_norm, k_norm, rel_bias, w_branch_a, w_branch_b, w_out, ffn2_norm, ffn2_w1, ffn2_w3, ffn2_w2, final_norm, loss_target, m_ffn1_norm, m_ffn1_w1, m_ffn1_w3, m_ffn1_w2, m_mix_norm, m_w_in, m_b_gate, m_q_norm, m_k_norm, m_rel_bias, m_w_branch_a, m_w_branch_b, m_w_out, m_ffn2_norm, m_ffn2_w1, m_ffn2_w3, m_ffn2_w2, m_final_norm, v_ffn1_norm, v_ffn1_w1, v_ffn1_w3, v_ffn1_w2, v_mix_norm, v_w_in, v_b_gate, v_q_norm, v_k_norm, v_rel_bias, v_w_branch_a, v_w_branch_b, v_w_out, v_ffn2_norm, v_ffn2_w1, v_ffn2_w3, v_ffn2_w2, v_final_norm):
    args = dict(locals())
    w = {n: args[n] for n in ORDER}
    m = {n: args["m_" + n] for n in ORDER}
    v = {n: args["v_" + n] for n in ORDER}
    D = x.shape[2]

    blocks = (
        ("ffn1_up", lambda t: jnp.concatenate([ffn1_w1[0].T + t, ffn1_w3[0].T + t], axis=0)),
        ("ffn1_down", lambda t: ffn1_w2[0] + t),
        ("mix", lambda t: jnp.concatenate([w_in[0] + t, w_branch_b[0] + t, w_out[0] + t,
                                           w_branch_a[0].reshape(MIX_ROWS - MIX_WA, D) + t], axis=0)),
        ("ffn2", lambda t: jnp.concatenate([ffn2_w1[0].T + t, ffn2_w3[0].T + t, ffn2_w2[0] + t], axis=0)),
    )
    gathers = {}
    start_token = jnp.zeros((8, LANES), F32)
    for tag, make in blocks:
        gathers[tag] = all_gather_start(make(start_token[0:1, 0:1]).astype(BF16), "all_gather_" + tag + "_start")
        start_token = gathers[tag][4]

    def gathered(tag):
        def get(after):
            return all_gather_finish(*_split_wait("all_gather_" + tag + "_wait", gathers[tag], 4, after),
                                     "all_gather_" + tag + "_finish")
        return get

    core = lax.axis_index("c").astype(jnp.int32).reshape(1)
    chip = (2 * lax.axis_index("x") + lax.axis_index("y")).astype(jnp.int32).reshape(1)
    exchanges = {}

    def emit(tag, gw):
        (theirs,) = reduce_scatter_pair([gw], "reduce_scatter_pair_" + tag)
        part = pair_add(gw, theirs, core, "pair_add_" + tag)
        exchanges[tag] = reduce_scatter_start(part, "reduce_scatter_" + tag + "_start")
        return exchanges[tag][4]

    small_p = dict(ffn1_norm=ffn1_norm, mix_norm=mix_norm, b_gate=b_gate, q_norm=q_norm, k_norm=k_norm,
                   rel_bias=rel_bias, ffn2_norm=ffn2_norm, final_norm=final_norm.reshape(1, D))
    loss_p, grad_x, small_g = local_step(x[0], loss_target[0], small_p, gathered("ffn1_up"), gathered("ffn1_down"),
                                         gathered("mix"), gathered("ffn2"), emit, start_token)

    def landed(tag, after):
        return _split_wait("reduce_scatter_" + tag + "_wait", exchanges[tag], 3, after)

    grads, delta, new_m, new_v = {}, {}, {}, {}

    def finish(n, part, land, off, blk, transposed=False):
        shp = w[n].shape
        if transposed:
            to2 = lambda a: a.reshape(shp[-2], shp[-1]).T
            back = lambda a: a.T.reshape(shp)
        else:
            to2 = lambda a: a.reshape(shp[-2], shp[-1])
            back = lambda a: a.reshape(shp)
        res = sum_adamw(part, land, chip, off, blk, to2(w[n]), to2(m[n]), to2(v[n]), "update_" + n)
        grads[n], delta[n], new_m[n], new_v[n] = [back(a) for a in res]

    last_token = exchanges["ffn1"][4]
    for tag, after in (("ffn2", last_token), ("ffn1", grad_x)):
        part, land = landed(tag, after)
        finish(tag + "_w1", part, land, 0, FFN_SHARD, transposed=True)
        finish(tag + "_w3", part, land, FFN_SHARD, FFN_SHARD, transposed=True)
        finish(tag + "_w2", part, land, 2 * FFN_SHARD, FFN_SHARD)
        if tag == "ffn2":
            part_m, land_m = landed("mix", last_token)
            finish("w_in", part_m, land_m, MIX_WIN, LANES)
            finish("w_branch_b", part_m, land_m, MIX_WB, LANES)
            finish("w_out", part_m, land_m, MIX_WOUT, LANES)
            grads["w_branch_a"] = sum_chips(part_m, land_m, chip, MIX_WA, MIX_ROWS - MIX_WA, MIX_ROWS - MIX_WA,
                                            "w_branch_a_sum").reshape(w_branch_a.shape)
    loss_row = jnp.pad(loss_p, ((0, 0), (0, D - LANES)))
    smalls = small_all_gather(_pack_small(small_g, loss_row))
    small_sum = sum_slots(smalls, 0, N_DEV, N_DEV, "small_sum")
    small_shapes = {n: w[n].shape for n in SMALL}
    grads.update(_unpack_small(small_sum, small_shapes))
    loss = small_sum[7, 0]

    n = "w_branch_a"
    two_d = lambda a: a.reshape(w[n].shape[-2], w[n].shape[-1])
    d_, m_, v_ = adamw(two_d(w[n]), two_d(grads[n]), two_d(m[n]), two_d(v[n]), "adamw_" + n)
    delta[n], new_m[n], new_v[n] = [a.reshape(w[n].shape) for a in (d_, m_, v_)]
    zero_row = jnp.zeros((1, D), F32)
    pack = lambda t: _pack_small({n: t[n] for n in SMALL}, zero_row)
    d_, m_, v_ = adamw(pack(w), small_sum, pack(m), pack(v), "adamw_small")
    for src, dst in ((d_, delta), (m_, new_m), (v_, new_v)):
        dst.update(_unpack_small(src, small_shapes))

    return (loss, grad_x[None], *[grads[n] for n in ORDER], *[delta[n] for n in ORDER],
            *[new_m[n] for n in ORDER], *[new_v[n] for n in ORDER])
```

```python
import math

import jax
import jax.numpy as jnp
from jax import lax
from jax.experimental import pallas as pl
from jax.experimental.pallas import tpu as pltpu

F32 = jnp.float32
BF16 = jnp.bfloat16
MESH = pl.DeviceIdType.MESH

V7X_VMEM_BYTES = 64 * 1024 * 1024
VMEM_LIMIT = V7X_VMEM_BYTES - 8 * 1024 * 1024
LANES = 128

N_DEV = 8
EPS = 1e-6
NEG_INF = -1e30

DILATIONS = (1, 4, 16)
HALF_WINDOW = 64
HEAD_DIM_A = 64
HEADS_PER_GROUP_A = 8
GROUP_WIDTH_A = 512
A_QKV_WIDTH = 4608
A_GROUP_QKV = A_QKV_WIDTH // 3
A_TQ = 128
A_WIN = A_TQ + 2 * HALF_WINDOW
A_UNROLL = 8
A_SCALE = HEAD_DIM_A ** -0.5
WGRAD_TK = 2048
HEAD_DIM_B = 128
N_HEADS_B = 8
N_KV_B = 2
GQA_GROUP_B = 4
GRID_W = 64
ROPE_THETA = 10000.0
B_TQ_FWD = 256
B_TQ_BWD = 512
LOG2E = 1.4426950408889634
N_BUCKETS = 32
MAX_DISTANCE = 1024
PB_WIDTH = 3584
PB_GATE_A = 1536
PB_GATE_B = 2560

ADAM_LR = 0.001
ADAM_B1 = 0.9
ADAM_B2 = 0.999
ADAM_EPS = 1e-08
ADAM_WD = 0.01
ADAM_STEP = 10

FFN_SHARD = 352
MIX_WIN, MIX_WB, MIX_WOUT, MIX_WA = 0, 1024, 1152, 1280
MIX_ROWS = 1344


def _dot(a, b, ca=1, cb=0):
    return lax.dot_general(a, b, (((ca,), (cb,)), ((), ())), preferred_element_type=F32)


def _call(name, body, grid, ins, outs, scratch=(), sem=None, aliases=None):
    ins = [tuple(i) + (None,) * (4 - len(i)) for i in ins]
    res = pl.pallas_call(
        body,
        out_shape=[jax.ShapeDtypeStruct(s, d) for (s, d, _, _) in outs],
        grid=grid,
        in_specs=[pl.BlockSpec(bs, im, pipeline_mode=pm) for (_, bs, im, pm) in ins],
        out_specs=[pl.BlockSpec(bs, im) for (_, _, bs, im) in outs],
        scratch_shapes=list(scratch),
        name=name,
        input_output_aliases=aliases or {},
        compiler_params=pltpu.CompilerParams(dimension_semantics=sem, vmem_limit_bytes=VMEM_LIMIT),
    )(*[i[0] for i in ins])
    return res


def _sigmoid(x):
    return 0.5 * jnp.tanh(0.5 * x) + 0.5


def _position():
    return lax.axis_index("x"), lax.axis_index("y"), lax.axis_index("c")


def _hbm_specs(n):
    return [pl.BlockSpec(memory_space=pl.ANY) for _ in range(n)]


PAIR_BUFFERS = 4


def reduce_scatter_pair(grads, name):
    n = len(grads)
    C = grads[0].shape[2]
    half = [g.shape[1] // 2 for g in grads]
    chunks = [(i, q, hf) for i in range(n) for q in range(4) for hf in range(2)]
    nb = PAIR_BUFFERS

    def body(*refs):
        ins, theirs = refs[:n], refs[n:2 * n]
        buf, load_sems, send_sems, recv_sems = refs[2 * n:]
        x, y, c = _position()
        sibling = (x, y, 1 - c)

        def load(k):
            i, q, hf = chunks[k]
            r = half[i]
            return pltpu.make_async_copy(ins[i].at[2 * q + (1 - c), pl.ds(hf * r, r), :],
                                         buf.at[k % nb, pl.ds(0, r), :], load_sems.at[k % nb])

        def send(k):
            i, q, hf = chunks[k]
            r = half[i]
            return pltpu.make_async_remote_copy(
                src_ref=buf.at[k % nb, pl.ds(0, r), :], dst_ref=theirs[i].at[q, pl.ds(hf * r, r), :],
                send_sem=send_sems.at[k % nb], recv_sem=recv_sems.at[i],
                device_id=sibling, device_id_type=MESH)

        for k in range(len(chunks) + 1):
            if k < len(chunks):
                if k >= nb:
                    send(k - nb).wait_send()
                load(k).start()
            if k >= 1:
                load(k - 1).wait()
                send(k - 1).start()
        for k in range(max(0, len(chunks) - nb), len(chunks)):
            send(k).wait_send()
        for i in range(n):
            pltpu.make_async_remote_copy(
                src_ref=theirs[i], dst_ref=theirs[i], send_sem=send_sems.at[0], recv_sem=recv_sems.at[i],
                device_id=sibling, device_id_type=MESH).wait_recv()

    return pl.pallas_call(
        body,
        out_shape=[jax.ShapeDtypeStruct((4,) + g.shape[1:], g.dtype) for g in grads],
        in_specs=_hbm_specs(n),
        out_specs=_hbm_specs(n),
        scratch_shapes=[pltpu.VMEM((nb, max(half), C), grads[0].dtype), pltpu.SemaphoreType.DMA((nb,)),
                        pltpu.SemaphoreType.DMA((nb,)), pltpu.SemaphoreType.DMA((n,))],
        name=name,
        compiler_params=pltpu.CompilerParams(vmem_limit_bytes=VMEM_LIMIT),
    )(*grads)


_HBM_SPEC = pl.BlockSpec(memory_space=pltpu.HBM)
_SEM_SPEC = pl.BlockSpec(memory_space=pltpu.SEMAPHORE)
_TOKEN_SPEC = pl.BlockSpec(memory_space=pltpu.VMEM)
_DATAFLOW = pltpu.SideEffectType.DATAFLOW_SIDE_EFFECTING


def _split_start(name, body, src, land_shape):
    def full_body(src_ref, land_ref, send_sem, recv_sem, src_thru, land_thru, token):
        body(src_ref, land_ref, send_sem, recv_sem)
        token[...] = jnp.zeros_like(token)

    land = pltpu.with_memory_space_constraint(lax.empty(land_shape, src.dtype), pltpu.HBM)
    return pl.pallas_call(
        full_body, name=name,
        out_shape=(pltpu.SemaphoreType.DMA(()), pltpu.SemaphoreType.DMA(()),
                   pltpu.HBM(src.shape, src.dtype), pltpu.HBM(land_shape, src.dtype),
                   jax.ShapeDtypeStruct((8, LANES), F32)),
        in_specs=(_HBM_SPEC, _HBM_SPEC),
        out_specs=(_SEM_SPEC, _SEM_SPEC, _HBM_SPEC, _HBM_SPEC, _TOKEN_SPEC),
        input_output_aliases={0: 2, 1: 3},
        compiler_params=pltpu.CompilerParams(has_side_effects=_DATAFLOW),
    )(pltpu.with_memory_space_constraint(src, pltpu.HBM), land)


def _split_wait(name, started, n_blocks, after):
    send_sem, recv_sem, src_thru, land_thru, _ = started
    after = after if isinstance(after, tuple) else (after,)

    def body(src_ref, land_ref, send_sem, recv_sem, *rest):
        x, y, c = _position()
        blocks = land_ref.at[pl.ds(0, n_blocks)]
        copy = pltpu.make_async_remote_copy(src_ref=blocks, dst_ref=blocks, send_sem=send_sem, recv_sem=recv_sem,
                                            device_id=(x, y, c), device_id_type=MESH)
        copy.wait_send()
        copy.wait_recv()

    return pl.pallas_call(
        body, name=name,
        out_shape=(pltpu.HBM(src_thru.shape, src_thru.dtype), pltpu.HBM(land_thru.shape, land_thru.dtype)),
        in_specs=(_HBM_SPEC, _HBM_SPEC, _SEM_SPEC, _SEM_SPEC) + (pl.BlockSpec(memory_space=pl.ANY),) * len(after),
        out_specs=(_HBM_SPEC, _HBM_SPEC),
        input_output_aliases={0: 0, 1: 1},
        compiler_params=pltpu.CompilerParams(has_side_effects=_DATAFLOW),
    )(src_thru, land_thru, send_sem, recv_sem, *after)


def all_gather_start(block, name):
    def body(b_ref, land_ref, send_sem, recv_sem):
        x, y, c = _position()
        for peer in [(x, y, 1 - c), (1 - x, y, c), (x, 1 - y, c), (1 - x, 1 - y, c)]:
            pltpu.make_async_remote_copy(src_ref=b_ref, dst_ref=land_ref.at[4 * x + 2 * y + c],
                                         send_sem=send_sem, recv_sem=recv_sem,
                                         device_id=peer, device_id_type=MESH).start()

    return _split_start(name, body, block, (N_DEV,) + block.shape)


def all_gather_finish(block, land, name):
    R, C = block.shape

    def body(b_ref, land_in, land_ref, stage, load_sems, send_sems, recv_sems, own_sem):
        x, y, c = _position()
        sibling = (x, y, 1 - c)
        chips = [(1 - x, y), (x, 1 - y), (1 - x, 1 - y)]
        own_in = pltpu.make_async_copy(b_ref, stage.at[3], load_sems.at[3])
        own_in.start()
        loads = [pltpu.make_async_copy(land_in.at[4 * px + 2 * py + c], stage.at[j], load_sems.at[j])
                 for j, (px, py) in enumerate(chips)]
        for ld in loads:
            ld.start()
        sends = []
        for j, (px, py) in enumerate(chips):
            loads[j].wait()
            dst = land_ref.at[4 * px + 2 * py + c]
            cp = pltpu.make_async_remote_copy(src_ref=stage.at[j], dst_ref=dst, send_sem=send_sems.at[j],
                                              recv_sem=recv_sems.at[j], device_id=sibling, device_id_type=MESH)
            cp.start()
            sends.append(cp)
        own_in.wait()
        own_out = pltpu.make_async_copy(stage.at[3], land_ref.at[4 * x + 2 * y + c], own_sem)
        own_out.start()
        for j, (px, py) in enumerate(chips):
            dst = land_ref.at[4 * px + 2 * py + (1 - c)]
            pltpu.make_async_remote_copy(src_ref=stage.at[j], dst_ref=dst, send_sem=send_sems.at[j],
                                         recv_sem=recv_sems.at[j], device_id=sibling,
                                         device_id_type=MESH).wait_recv()
        for cp in sends:
            cp.wait_send()
        own_out.wait()

    return pl.pallas_call(
        body,
        out_shape=jax.ShapeDtypeStruct(land.shape, land.dtype),
        in_specs=_hbm_specs(2),
        out_specs=pl.BlockSpec(memory_space=pl.ANY),
        scratch_shapes=[pltpu.VMEM((4, R, C), block.dtype), pltpu.SemaphoreType.DMA((4,)),
                        pltpu.SemaphoreType.DMA((3,)), pltpu.SemaphoreType.DMA((3,)), pltpu.SemaphoreType.DMA],
        input_output_aliases={1: 0},
        name=name,
        compiler_params=pltpu.CompilerParams(vmem_limit_bytes=VMEM_LIMIT),
    )(block, land)


def reduce_scatter_start(parts, name):
    def body(p_ref, land_ref, send_sem, recv_sem):
        x, y, c = _position()
        for px, py in [(1 - x, y), (x, 1 - y), (1 - x, 1 - y)]:
            pltpu.make_async_remote_copy(src_ref=p_ref.at[2 * px + py], dst_ref=land_ref.at[2 * x + y],
                                         send_sem=send_sem, recv_sem=recv_sem,
                                         device_id=(px, py, c), device_id_type=MESH).start()

    return _split_start(name, body, parts, parts.shape)


def small_all_gather(small):
    def body(small_ref, smalls, s_send, s_recv, s_local):
        x, y, c = _position()
        me = 4 * x + 2 * y + c
        lc = pltpu.make_async_copy(small_ref, smalls.at[me], s_local)
        lc.start()
        remote = []
        k = 0
        for dx in (0, 1):
            for dy in (0, 1):
                for dc in (0, 1):
                    if dx + dy + dc == 0:
                        continue
                    peer = (1 - x if dx else x, 1 - y if dy else y, 1 - c if dc else c)
                    rc = pltpu.make_async_remote_copy(
                        src_ref=small_ref, dst_ref=smalls.at[me],
                        send_sem=s_send.at[k], recv_sem=s_recv.at[k],
                        device_id=peer, device_id_type=MESH)
                    rc.start()
                    remote.append(rc)
                    k += 1
        for rc in remote:
            rc.wait()
        lc.wait()

    return pl.pallas_call(
        body,
        out_shape=jax.ShapeDtypeStruct((N_DEV,) + small.shape, small.dtype),
        in_specs=_hbm_specs(1),
        out_specs=pl.BlockSpec(memory_space=pl.ANY),
        scratch_shapes=[pltpu.SemaphoreType.DMA((7,)), pltpu.SemaphoreType.DMA((7,)), pltpu.SemaphoreType.DMA],
        name="small_all_gather",
    )(small)


def pair_add(grads, theirs, core, name):
    _, R, C = theirs.shape
    tr = R // 2

    def body(c_ref, a_ref, b_ref, o_ref):
        o_ref[...] = (a_ref[...].astype(F32) + b_ref[...].astype(F32)).astype(BF16)

    return pl.pallas_call(
        body,
        out_shape=jax.ShapeDtypeStruct(theirs.shape, BF16),
        grid_spec=pltpu.PrefetchScalarGridSpec(
            num_scalar_prefetch=1, grid=(4, R // tr),
            in_specs=[pl.BlockSpec((None, tr, C), lambda q, i, c: (2 * q + c[0], i, 0)),
                      pl.BlockSpec((None, tr, C), lambda q, i, c: (q, i, 0))],
            out_specs=pl.BlockSpec((None, tr, C), lambda q, i, c: (q, i, 0))),
        name=name,
        compiler_params=pltpu.CompilerParams(dimension_semantics=("parallel", "parallel"),
                                             vmem_limit_bytes=VMEM_LIMIT),
    )(core, grads, theirs)


def sum_slots(recv, off, rows, blk, name):
    nq, _, C = recv.shape
    ob = off // blk

    def body(r_ref, o_ref):
        acc = r_ref[0].astype(F32)
        for q in range(1, nq):
            acc = acc + r_ref[q].astype(F32)
        o_ref[...] = acc

    return _call(name, body, (rows // blk,),
                 [(recv, (nq, blk, C), lambda i: (0, ob + i, 0))],
                 [((rows, C), F32, (blk, C), lambda i: (i, 0))], sem=("parallel",))[0]


def sum_chips(parts, land, chip, off, rows, blk, name):
    C = parts.shape[2]
    ob = off // blk

    def body(c_ref, own_ref, a_ref, b_ref, d_ref, o_ref):
        o_ref[...] = ((own_ref[...].astype(F32) + a_ref[...].astype(F32)) + b_ref[...].astype(F32)) \
            + d_ref[...].astype(F32)

    def entry(flip):
        return pl.BlockSpec((None, blk, C), lambda i, c: (c[0] ^ flip, ob + i, 0))

    return pl.pallas_call(
        body,
        out_shape=jax.ShapeDtypeStruct((rows, C), F32),
        grid_spec=pltpu.PrefetchScalarGridSpec(
            num_scalar_prefetch=1, grid=(rows // blk,),
            in_specs=[entry(0), entry(1), entry(2), entry(3)],
            out_specs=pl.BlockSpec((blk, C), lambda i, c: (i, 0))),
        name=name,
        compiler_params=pltpu.CompilerParams(dimension_semantics=("parallel",), vmem_limit_bytes=VMEM_LIMIT),
    )(chip, parts, land, land, land)


def _adamw_update(wv, gv, mv, vv):
    nm = ADAM_B1 * mv + (1.0 - ADAM_B1) * gv
    nv = ADAM_B2 * vv + (1.0 - ADAM_B2) * (gv * gv)
    c1 = 1.0 / (1.0 - ADAM_B1 ** ADAM_STEP)
    c2 = 1.0 / (1.0 - ADAM_B2 ** ADAM_STEP)
    return -ADAM_LR * ((nm * c1) / (jnp.sqrt(nv * c2) + ADAM_EPS) + ADAM_WD * wv), nm, nv


def sum_adamw(parts, land, chip, off, blk, w, m, v, name):
    rows, C = w.shape
    ob = off // blk

    def body(c_ref, own_ref, a_ref, b_ref, d_ref, w_ref, m_ref, v_ref, g_out, d_out, m_out, v_out):
        gv = ((own_ref[...].astype(F32) + a_ref[...].astype(F32)) + b_ref[...].astype(F32)) \
            + d_ref[...].astype(F32)
        g_out[...] = gv
        d_out[...], m_out[...], v_out[...] = _adamw_update(w_ref[...], gv, m_ref[...], v_ref[...])

    def entry(flip):
        return pl.BlockSpec((None, blk, C), lambda i, c: (c[0] ^ flip, ob + i, 0))

    plain = pl.BlockSpec((blk, C), lambda i, c: (i, 0))
    return pl.pallas_call(
        body,
        out_shape=[jax.ShapeDtypeStruct((rows, C), F32)] * 4,
        grid_spec=pltpu.PrefetchScalarGridSpec(
            num_scalar_prefetch=1, grid=(rows // blk,),
            in_specs=[entry(0), entry(1), entry(2), entry(3), plain, plain, plain],
            out_specs=[plain] * 4),
        name=name,
        compiler_params=pltpu.CompilerParams(dimension_semantics=("parallel",), vmem_limit_bytes=VMEM_LIMIT),
    )(chip, parts, land, land, land, w, m, v)


def adamw(w, g, m, v, name):
    R, C = w.shape
    tr = R
    for cand in (256, 128, 64, 32, 16, 8):
        if R % cand == 0 and R > cand:
            tr = cand
            break

    def body(w_ref, g_ref, m_ref, v_ref, d_ref, nm_ref, nv_ref):
        d_ref[...], nm_ref[...], nv_ref[...] = _adamw_update(w_ref[...], g_ref[...], m_ref[...], v_ref[...])

    spec = ((tr, C), lambda i: (i, 0))
    out = ((R, C), F32) + spec
    return _call(name, body, (R // tr,), [(w,) + spec, (g,) + spec, (m,) + spec, (v,) + spec],
                 [out, out, out], sem=("parallel",))


def rms_fwd(x, g, name):
    S, D = x.shape
    tr = 512

    def body(x_ref, g_ref, o_ref):
        xv = x_ref[...]
        r = lax.rsqrt(jnp.mean(xv * xv, axis=-1, keepdims=True) + EPS)
        o_ref[...] = (xv * r * g_ref[...]).astype(BF16)

    return _call(name, body, (S // tr,),
                 [(x, (tr, D), lambda i: (i, 0)), (g, (1, D), lambda i: (0, 0))],
                 [((S, D), BF16, (tr, D), lambda i: (i, 0))], sem=("parallel",))[0]


def _rms_bwd_tile(dn, xv, gv):
    r = lax.rsqrt(jnp.mean(xv * xv, axis=-1, keepdims=True) + EPS)
    xh = xv * r
    dxh = dn * gv
    dx = r * (dxh - xh * jnp.mean(dxh * xh, axis=-1, keepdims=True))
    return dx, dn * xh


def final_loss(x, tgt, g, name):
    S, D = x.shape
    tr = 256

    def body(x_ref, t_ref, g_ref, l_ref, dx_ref, dxb_ref, dg_ref):
        i = pl.program_id(0)
        xv, gv = x_ref[...], g_ref[...]
        r = lax.rsqrt(jnp.mean(xv * xv, axis=-1, keepdims=True) + EPS)
        xh = xv * r
        e = xh * gv - t_ref[...]
        part = 0.5 * jnp.sum(jnp.sum(e * e, axis=-1, keepdims=True) * (1.0 / D), axis=0, keepdims=True)
        dy = e * (1.0 / D)
        dxh = dy * gv
        dx = r * (dxh - xh * jnp.mean(dxh * xh, axis=-1, keepdims=True))
        dx_ref[...] = dx
        dxb_ref[...] = dx.astype(BF16)
        dgp = jnp.sum(dy * xh, axis=0, keepdims=True)

        @pl.when(i == 0)
        def _():
            l_ref[...] = jnp.broadcast_to(part, l_ref.shape)
            dg_ref[...] = dgp

        @pl.when(i > 0)
        def _():
            l_ref[...] += jnp.broadcast_to(part, l_ref.shape)
            dg_ref[...] += dgp

    row = ((tr, D), lambda i: (i, 0))
    return _call(name, body, (S // tr,),
                 [(x,) + row, (tgt,) + row, (g, (1, D), lambda i: (0, 0))],
                 [((1, LANES), F32, (1, LANES), lambda i: (0, 0)), ((S, D), F32) + row, ((S, D), BF16) + row,
                  ((1, D), F32, (1, D), lambda i: (0, 0))], sem=("arbitrary",))


FFN_TF = 4 * FFN_SHARD


def _ffn_pick(G, which):
    if isinstance(G, tuple):
        return (G[0], which) if which < 2 else (G[1], 0)
    return G, which


def _ffn_w_spec(G, which, imap):
    arr, blk = _ffn_pick(G, which)
    return (arr, (4, FFN_SHARD, arr.shape[2]), lambda *idx: (imap(*idx), blk, 0))


def _ffn_whole_w_spec(G, which):
    arr, blk = _ffn_pick(G, which)
    return (arr, (N_DEV, FFN_SHARD, arr.shape[2]), lambda *idx: (0, blk, 0))


def ffn_up(n, G, name):
    S, D = n.shape
    F = N_DEV * FFN_SHARD
    tm = 1024

    def body(n_ref, w1_ref, w3_ref, abh_ref):
        nv = n_ref[...]
        a = _dot(nv, w1_ref[...].reshape(FFN_TF, D), 1, 1).astype(BF16)
        b = _dot(nv, w3_ref[...].reshape(FFN_TF, D), 1, 1).astype(BF16)
        abh_ref[0] = a
        abh_ref[1] = b
        av, bv = a.astype(F32), b.astype(F32)
        abh_ref[2] = (av * _sigmoid(av) * bv).astype(BF16)

    return _call(name, body, (F // FFN_TF, S // tm),
                 [(n, (tm, D), lambda j, i: (i, 0)),
                  _ffn_w_spec(G, 0, lambda j, i: j), _ffn_w_spec(G, 1, lambda j, i: j)],
                 [((3, S, F), BF16, (3, tm, FFN_TF), lambda j, i: (0, i, j))],
                 sem=("parallel", "parallel"))[0]


def ffn_down(abh, G, x, name):
    _, S, F = abh.shape
    D = x.shape[1]
    tm = 512

    def body(h_ref, w2_ref, x_ref, o_ref):
        o_ref[...] = x_ref[...] + 0.5 * _dot(h_ref[...], w2_ref[...].reshape(F, D))

    return _call(name, body, (S // tm,),
                 [(abh, (None, tm, F), lambda i: (2, i, 0)), _ffn_whole_w_spec(G, 2),
                  (x, (tm, D), lambda i: (i, 0))],
                 [((S, D), F32, (tm, D), lambda i: (i, 0))], sem=("parallel",))[0]


def ffn_bwd_weights(dxo, abh, n, G, name):
    _, S, F = abh.shape
    D = dxo.shape[1]
    tm = 512
    nf = F // FFN_TF

    def down_body(d_ref, w2_ref, ab_ref, o_ref):
        dh = 0.5 * _dot(d_ref[...].astype(BF16), w2_ref[...].reshape(FFN_TF, D), 1, 1)
        av, bv = ab_ref[0].astype(F32), ab_ref[1].astype(F32)
        sig = _sigmoid(av)
        o_ref[0] = (dh * bv * (sig * (1.0 + av * (1.0 - sig)))).astype(BF16)
        o_ref[1] = (dh * (av * sig)).astype(BF16)

    dab = _call(name + "_down_bwd", down_body, (nf, S // tm),
                [(dxo, (tm, D), lambda j, i: (i, 0)), _ffn_w_spec(G, 2, lambda j, i: j),
                 (abh, (2, tm, FFN_TF), lambda j, i: (0, i, j))],
                [((2, S, F), BF16, (2, tm, FFN_TF), lambda j, i: (0, i, j))],
                sem=("parallel", "parallel"))[0]

    tk = WGRAD_TK
    nk = S // tk
    gshape = (N_DEV, 3 * FFN_SHARD, D)

    def dw2_body(h_ref, d_ref, o_ref, acc_ref):
        k = pl.program_id(1)
        p = _dot(h_ref[...], d_ref[...].astype(BF16), 0, 0)

        @pl.when(k == 0)
        def _():
            acc_ref[...] = p

        @pl.when(k > 0)
        def _():
            acc_ref[...] += p

        @pl.when(k == nk - 1)
        def _():
            o_ref[...] = (0.5 * acc_ref[...]).astype(BF16).reshape(4, FFN_SHARD, D)

    gw = _call(name + "_dw2", dw2_body, (nf, nk),
               [(abh, (None, tk, FFN_TF), lambda j, k: (2, k, j)), (dxo, (tk, D), lambda j, k: (k, 0))],
               [(gshape, BF16, (4, FFN_SHARD, D), lambda j, k: (j, 2, 0))],
               scratch=[pltpu.VMEM((FFN_TF, D), F32)], sem=("parallel", "arbitrary"))[0]

    def dw13_body(gw_ref, dab_ref, n_ref, o_ref):
        o_ref[...] = _dot(dab_ref[...], n_ref[...], 0, 0).astype(BF16).reshape(4, FFN_SHARD, D)

    gw = pl.pallas_call(
        dw13_body,
        out_shape=jax.ShapeDtypeStruct(gshape, BF16),
        grid=(2, nf),
        in_specs=[pl.BlockSpec(memory_space=pl.ANY),
                  pl.BlockSpec((None, S, FFN_TF), lambda w, j: (w, 0, j)),
                  pl.BlockSpec((S, D), lambda w, j: (0, 0))],
        out_specs=pl.BlockSpec((4, FFN_SHARD, D), lambda w, j: (j, w, 0)),
        input_output_aliases={0: 0},
        name=name + "_dw13",
        compiler_params=pltpu.CompilerParams(dimension_semantics=("parallel", "parallel"),
                                             vmem_limit_bytes=VMEM_LIMIT),
    )(gw, dab, n)
    return dab, gw


def ffn_bwd_input(dab, G, x_in, g, dxo, name, as_operand=True):
    _, S, F = dab.shape
    D = x_in.shape[1]
    tm = 256

    def dn_body(dab_ref, w1_ref, w3_ref, x_ref, d_ref, g_ref, dx_ref, *rest):
        dg_ref = rest[-1]
        i = pl.program_id(0)
        dn = _dot(dab_ref[0], w1_ref[...].reshape(F, D)) + _dot(dab_ref[1], w3_ref[...].reshape(F, D))
        dx, dgt = _rms_bwd_tile(dn, x_ref[...], g_ref[...])
        dx = d_ref[...] + dx
        dx_ref[...] = dx
        if as_operand:
            rest[0][...] = dx.astype(BF16)
        dgp = jnp.sum(dgt, axis=0, keepdims=True)

        @pl.when(i == 0)
        def _():
            dg_ref[...] = dgp

        @pl.when(i > 0)
        def _():
            dg_ref[...] += dgp

    tile = ((tm, D), lambda i: (i, 0))
    return _call(name + "_dn", dn_body, (S // tm,),
                 [(dab, (2, tm, F), lambda i: (0, i, 0)),
                  _ffn_whole_w_spec(G, 0), _ffn_whole_w_spec(G, 1),
                  (x_in,) + tile, (dxo,) + tile, (g, (1, D), lambda i: (0, 0))],
                 [((S, D), F32) + tile] + ([((S, D), BF16) + tile] if as_operand else [])
                 + [((1, D), F32, (1, D), lambda i: (0, 0))],
                 sem=("arbitrary",))


PROJ_TN = 512
DH_SHARDS_PER_STEP = 4


def in_proj(h, Gm, first_tile, n_tiles, dtype, name, tile_stride=1):
    S, D = h.shape
    tile = lambda j: first_tile + tile_stride * j

    def body(h_ref, w_ref, o_ref):
        o_ref[...] = _dot(h_ref[...], w_ref[...]).astype(dtype)

    return _call(name, body, (n_tiles,),
                 [(h, (S, D), lambda j: (0, 0)),
                  (Gm, (None, D, PROJ_TN), lambda j: (tile(j) // 2, 0, tile(j) % 2))],
                 [((S, n_tiles * PROJ_TN), dtype, (S, PROJ_TN), lambda j: (0, j))],
                 sem=("parallel",))[0]


def in_proj_bwd_dw(dproj, h, gm_grads, name):
    S, D = h.shape
    NT = dproj.shape[1] // PROJ_TN

    def dw_body(gm_ref, h_ref, d_ref, o_ref):
        o_ref[...] = _dot(h_ref[...], d_ref[...], 0, 0).astype(BF16)

    return pl.pallas_call(
        dw_body,
        out_shape=jax.ShapeDtypeStruct(gm_grads.shape, BF16),
        grid=(NT,),
        in_specs=[pl.BlockSpec(memory_space=pl.ANY),
                  pl.BlockSpec((S, D), lambda j: (0, 0)),
                  pl.BlockSpec((S, PROJ_TN), lambda j: (0, j))],
        out_specs=pl.BlockSpec((None, D, PROJ_TN), lambda j: (j // 2, 0, j % 2)),
        input_output_aliases={0: 0},
        name=name + "_dw",
        compiler_params=pltpu.CompilerParams(dimension_semantics=("parallel",), vmem_limit_bytes=VMEM_LIMIT),
    )(gm_grads, h, dproj)


def in_proj_bwd_dh(dproj, Gm, x_in, g, dres, name):
    S, D = x_in.shape
    tm = 256
    C = Gm.shape[2]
    n_sh = dproj.shape[1] // C

    def dh_body(d_ref, w_ref, x_ref, r_ref, g_ref, dx_ref, dxb_ref, dg_ref):
        i = pl.program_id(0)
        p = _dot(d_ref[:, 0:C], w_ref[0], 1, 1)
        for s in range(1, n_sh):
            p = p + _dot(d_ref[:, s * C:(s + 1) * C], w_ref[s], 1, 1)
        dx, dgt = _rms_bwd_tile(p, x_ref[...], g_ref[...])
        dx = r_ref[...] + dx
        dx_ref[...] = dx
        dxb_ref[...] = dx.astype(BF16)
        dgp = jnp.sum(dgt, axis=0, keepdims=True)

        @pl.when(i == 0)
        def _():
            dg_ref[...] = dgp

        @pl.when(i > 0)
        def _():
            dg_ref[...] += dgp

    tile = ((tm, D), lambda i: (i, 0))
    return _call(name + "_dh", dh_body, (S // tm,),
                 [(dproj, (tm, n_sh * C), lambda i: (i, 0)),
                  (Gm, (n_sh, D, C), lambda i: (0, 0, 0), pl.Buffered(1)),
                  (x_in,) + tile, (dres,) + tile, (g, (1, D), lambda i: (0, 0))],
                 [((S, D), F32) + tile, ((S, D), BF16) + tile, ((1, D), F32, (1, D), lambda i: (0, 0))],
                 sem=("arbitrary",))


def _t5_bucket(rel):
    n = N_BUCKETS // 2
    max_exact = n // 2
    ret = jnp.where(rel > 0, n, 0)
    a = jnp.abs(rel)
    af = jnp.maximum(a, 1).astype(F32)
    large = max_exact + (jnp.log(af / max_exact) / math.log(MAX_DISTANCE / max_exact)
                         * (n - max_exact)).astype(jnp.int32)
    large = jnp.minimum(large, n - 1)
    return ret + jnp.where(a < max_exact, a, large)


def _bucket_tables():
    qi = jnp.arange(A_TQ, dtype=jnp.int32)[:, None]
    kj = jnp.arange(A_WIN, dtype=jnp.int32)[None, :]
    rel = kj - HALF_WINDOW - qi
    return jnp.stack([_t5_bucket(rel * d) for d in DILATIONS], axis=0)


def bias_build(rel_bias, buckets):
    def body(tab_ref, bk_ref, o_ref):
        col = pl.program_id(0) * HEADS_PER_GROUP_A + pl.program_id(1)
        bk = bk_ref[...]
        acc = jnp.zeros(bk.shape, F32)
        for b in range(N_BUCKETS):
            acc = jnp.where(bk == b, tab_ref[b, col], acc)
        qi = lax.broadcasted_iota(jnp.int32, bk.shape, 0)
        kj = lax.broadcasted_iota(jnp.int32, bk.shape, 1)
        band = jnp.where(jnp.abs(kj - HALF_WINDOW - qi) <= HALF_WINDOW, acc, NEG_INF)
        o_ref[0] = jnp.where(kj >= HALF_WINDOW, band, NEG_INF)
        o_ref[1] = band
        o_ref[2] = jnp.where(kj < A_TQ + HALF_WINDOW, band, NEG_INF)

    out = pl.pallas_call(
        body,
        out_shape=jax.ShapeDtypeStruct((3, HEADS_PER_GROUP_A // 2, 3, 2, A_TQ, A_WIN), F32),
        grid=(3, HEADS_PER_GROUP_A),
        in_specs=[pl.BlockSpec(memory_space=pltpu.SMEM),
                  pl.BlockSpec((None, A_TQ, A_WIN), lambda g, h: (g, 0, 0))],
        out_specs=pl.BlockSpec((None, None, 3, None, A_TQ, A_WIN), lambda g, h: (g, h // 2, 0, h % 2, 0, 0)),
        name="a_bias_build",
        compiler_params=pltpu.CompilerParams(dimension_semantics=("parallel", "parallel")),
    )(rel_bias, buckets)
    return out.reshape(3, HEADS_PER_GROUP_A // 2, 3, 2 * A_TQ, A_WIN)


def bias_bwd(dbias, buckets):
    def body(d_ref, bk_ref, o_ref):
        bk = bk_ref[...]
        dv = d_ref[...]
        for b in range(N_BUCKETS):
            part = jnp.sum(jnp.where(bk == b, dv, 0.0), axis=1, keepdims=True)
            o_ref[b:b + 1, :] = jnp.broadcast_to(jnp.sum(part, axis=0, keepdims=True), (1, LANES))

    out = pl.pallas_call(
        body,
        out_shape=jax.ShapeDtypeStruct((3, HEADS_PER_GROUP_A, N_BUCKETS, LANES), F32),
        grid=(3, HEADS_PER_GROUP_A),
        in_specs=[pl.BlockSpec((None, None, A_TQ, A_WIN), lambda g, h: (g, h, 0, 0)),
                  pl.BlockSpec((None, A_TQ, A_WIN), lambda g, h: (g, 0, 0))],
        out_specs=pl.BlockSpec((None, None, N_BUCKETS, LANES), lambda g, h: (g, h, 0, 0)),
        name="a_bias_bwd",
        compiler_params=pltpu.CompilerParams(dimension_semantics=("parallel", "parallel")),
    )(dbias, buckets)
    return out[:, :, :, 0].transpose(2, 0, 1).reshape(N_BUCKETS, 3 * HEADS_PER_GROUP_A)


def _a_fill_padded(pad_ref, src_ref, n, pad):
    zeros = jnp.zeros((pad, LANES), pad_ref.dtype)
    pad_ref[0:pad, :] = zeros
    pad_ref[pad + n:2 * pad + n, :] = zeros
    pad_ref[pad:pad + n, :] = src_ref[...].astype(pad_ref.dtype)


def _a_stack_heads(x, lane):
    zero = jnp.zeros_like(x)
    return jnp.concatenate([jnp.where(lane < HEAD_DIM_A, x, zero), jnp.where(lane >= HEAD_DIM_A, x, zero)], axis=0)


def _a_bias_variant(qb, nqb):
    return jnp.where(qb == 0, 0, jnp.where(qb == nqb - 1, 2, 1))


def a_fwd(proj_g, bias_g, g, name):
    S = proj_g.shape[0]
    d = DILATIONS[g]
    L = S // d
    nqb = L // A_TQ
    pad = HALF_WINDOW * d

    def body(q_ref, k_ref, v_ref, b_ref, o_ref, l_ref, qf, kpad, vpad):
        qf[...] = q_ref[...].astype(F32) * A_SCALE
        _a_fill_padded(kpad, k_ref, S, pad)
        _a_fill_padded(vpad, v_ref, S, pad)
        lane = lax.broadcasted_iota(jnp.int32, (A_TQ, LANES), 1)

        def block(t, carry):
            qb, r = t // d, t % d
            start = qb * (A_TQ * d) + r
            kw = kpad[pl.ds(start, A_WIN, stride=d), :].astype(BF16)
            vw = vpad[pl.ds(start, A_WIN, stride=d), :].astype(BF16)
            q = qf[pl.ds(start, A_TQ, stride=d), :].astype(BF16)
            q2 = _a_stack_heads(q, lane)
            s = _dot(q2, kw, 1, 1) + b_ref[_a_bias_variant(qb, nqb)]
            m = jnp.max(s, axis=-1, keepdims=True)
            e = jnp.exp(s - m)
            l = jnp.sum(e, axis=-1, keepdims=True)
            o2 = _dot(e.astype(BF16), vw) / l
            lse2 = m + jnp.log(l)
            o_ref[pl.ds(start, A_TQ, stride=d), :] = jnp.where(lane < HEAD_DIM_A, o2[0:A_TQ], o2[A_TQ:])
            l_ref[pl.ds(start, A_TQ, stride=d), :] = jnp.where(lane < HEAD_DIM_A, lse2[0:A_TQ], lse2[A_TQ:])
            return carry

        lax.fori_loop(0, nqb * d, block, 0, unroll=A_UNROLL)

    out_spec = ((S, GROUP_WIDTH_A), F32, (S, LANES), lambda hp: (0, hp))
    return _call(name, body, (4,),
                 [(proj_g, (S, LANES), lambda hp: (0, hp)),
                  (proj_g, (S, LANES), lambda hp: (0, 4 + hp)),
                  (proj_g, (S, LANES), lambda hp: (0, 8 + hp)),
                  (bias_g, (None, 3, 2 * A_TQ, A_WIN), lambda hp: (hp, 0, 0, 0))],
                 [out_spec, out_spec],
                 scratch=[pltpu.VMEM((S, LANES), F32)] + [pltpu.VMEM((S + 2 * pad, LANES), F32)] * 2,
                 sem=("parallel",))


def a_combine(outs, lses, name):
    S, W = outs[0].shape
    tr = 512

    def body(o0, o1, o2, l0, l1, l2, oa_ref, lt_ref):
        a, b, c = l0[...], l1[...], l2[...]
        m = jnp.maximum(jnp.maximum(a, b), c)
        ea, eb, ec = jnp.exp(a - m), jnp.exp(b - m), jnp.exp(c - m)
        z = ea + eb + ec
        oa_ref[...] = ((ea * o0[...] + eb * o1[...] + ec * o2[...]) / z).astype(BF16)
        lt_ref[...] = m + jnp.log(z)

    spec = ((tr, W), lambda i: (i, 0))
    return _call(name, body, (S // tr,), [(a,) + spec for a in (*outs, *lses)],
                 [((S, W), BF16) + spec, ((S, W), F32) + spec], sem=("parallel",))


def a_bwd(proj_g, bias_g, do_a, o_a, lse_tot, g, name):
    S = proj_g.shape[0]
    d = DILATIONS[g]
    L = S // d
    nqb = L // A_TQ
    pad = HALF_WINDOW * d

    def body(q_ref, k_ref, v_ref, b_ref, do_ref, o_ref, l_ref, dq_ref, dk_ref, dv_ref, db_ref,
             qf, of, dqf, kpad, vpad, dkacc, dvacc):
        qf[...] = q_ref[...].astype(F32) * A_SCALE
        of[...] = o_ref[...].astype(F32)
        _a_fill_padded(kpad, k_ref, S, pad)
        _a_fill_padded(vpad, v_ref, S, pad)
        dkacc[...] = jnp.zeros(dkacc.shape, F32)
        dvacc[...] = jnp.zeros(dvacc.shape, F32)
        db_ref[...] = jnp.zeros(db_ref.shape, F32)
        lane = lax.broadcasted_iota(jnp.int32, (A_TQ, LANES), 1)

        def block(t, carry):
            qb, r = t // d, t % d
            start = qb * (A_TQ * d) + r
            rows = pl.ds(start, A_TQ, stride=d)
            win = pl.ds(start, A_WIN, stride=d)
            kw = kpad[win, :].astype(BF16)
            vw = vpad[win, :].astype(BF16)
            q = qf[rows, :].astype(BF16)
            do = do_ref[rows, :]
            ov = of[rows, :]
            lt = l_ref[rows, :]
            q2 = _a_stack_heads(q, lane)
            do2 = _a_stack_heads(do, lane)
            lt2 = jnp.concatenate([lt[:, 0:1], lt[:, HEAD_DIM_A:HEAD_DIM_A + 1]], axis=0)
            s = _dot(q2, kw, 1, 1) + b_ref[_a_bias_variant(qb, nqb)]
            p = jnp.exp(s - lt2)
            t = jnp.sum(do2 * jnp.concatenate([ov, ov], axis=0), axis=-1, keepdims=True)
            dob2 = do2.astype(BF16)
            ds = p * (_dot(dob2, vw, 1, 1) - t)
            db_ref[...] += ds
            dsb = ds.astype(BF16)
            dq2 = _dot(dsb, kw)
            dqf[rows, :] = jnp.where(lane < HEAD_DIM_A, dq2[0:A_TQ], dq2[A_TQ:]) * A_SCALE
            dkacc[win, :] += _dot(dsb, q2, 0, 0)
            dvacc[win, :] += _dot(p.astype(BF16), dob2, 0, 0)
            return carry

        lax.fori_loop(0, nqb * d, block, 0, unroll=A_UNROLL)
        dq_ref[...] = dqf[...].astype(BF16)
        dk_ref[...] = dkacc[pad:pad + S, :].astype(BF16)
        dv_ref[...] = dvacc[pad:pad + S, :].astype(BF16)

    slab = ((S, LANES), lambda hp: (0, hp))
    oshape = (S, GROUP_WIDTH_A)
    padded = pltpu.VMEM((S + 2 * pad, LANES), F32)
    return _call(
        name, body, (4,),
        [(proj_g, (S, LANES), lambda hp: (0, hp)),
         (proj_g, (S, LANES), lambda hp: (0, 4 + hp)),
         (proj_g, (S, LANES), lambda hp: (0, 8 + hp)),
         (bias_g, (None, 3, 2 * A_TQ, A_WIN), lambda hp: (hp, 0, 0, 0)),
         (do_a,) + slab, (o_a,) + slab, (lse_tot,) + slab],
        [(oshape, BF16) + slab, (oshape, BF16) + slab, (oshape, BF16) + slab,
         ((4, 2 * A_TQ, A_WIN), F32, (None, 2 * A_TQ, A_WIN), lambda hp: (hp, 0, 0))],
        scratch=[pltpu.VMEM((S, LANES), F32)] * 3 + [padded] * 4,
        sem=("parallel",))


def _rope_tables(S):
    rows = S // GRID_W
    row = jnp.repeat(jnp.arange(rows, dtype=F32), GRID_W)
    col = jnp.tile(jnp.arange(GRID_W, dtype=F32), rows)
    n_freq = HEAD_DIM_B // 4
    freq = ROPE_THETA ** (-jnp.arange(n_freq, dtype=F32) / n_freq)
    ang = jnp.concatenate([row[:, None] * freq, col[:, None] * freq], axis=-1)
    cos, sin = jnp.cos(ang), jnp.sin(ang)
    return jnp.repeat(cos, 2, axis=-1), jnp.stack([-sin, sin], axis=-1).reshape(S, HEAD_DIM_B)


def _swap_pairs(y):
    lane = lax.broadcasted_iota(jnp.int32, y.shape, 1)
    return jnp.where(lane % 2 == 0, pltpu.roll(y, LANES - 1, 1), pltpu.roll(y, 1, 1))


def qkv_prep(proj_b, gains, cos_t, sin_t, name):
    S = proj_b.shape[0]
    ts = 256
    n_rot = N_HEADS_B + N_KV_B
    nh = n_rot + N_KV_B
    W = nh * LANES

    def body(x_ref, g_ref, c_ref, s_ref, o_ref):
        cv, sv = c_ref[...], s_ref[...]
        for hb in range(nh):
            cols = slice(hb * LANES, (hb + 1) * LANES)
            xv = x_ref[:, cols]
            if hb < n_rot:
                r = lax.rsqrt(jnp.mean(xv * xv, axis=-1, keepdims=True) + EPS)
                yv = xv * r * g_ref[:, cols]
                o_ref[:, cols] = (yv * cv + _swap_pairs(yv) * sv).astype(BF16)
            else:
                o_ref[:, cols] = xv.astype(BF16)

    return _call(name, body, (S // ts,),
                 [(proj_b, (ts, W), lambda i: (i, 0)), (gains, (1, W), lambda i: (0, 0)),
                  (cos_t, (ts, LANES), lambda i: (i, 0)), (sin_t, (ts, LANES), lambda i: (i, 0))],
                 [((S, W), BF16, (ts, W), lambda i: (i, 0))],
                 sem=("parallel",))[0]


def qk_prep_bwd(dr, proj_b, col0, gain, cos_t, sin_t, name):
    S, W = dr.shape
    H = W // LANES
    ts = 256
    xb = (col0 * LANES) // W

    def body(d_ref, x_ref, g_ref, c_ref, s_ref, dx_ref, dg_ref):
        i = pl.program_id(0)
        cv, sv, gv = c_ref[...], s_ref[...], g_ref[...]
        dgp = jnp.zeros((1, LANES), F32)
        for hb in range(H):
            cols = slice(hb * LANES, (hb + 1) * LANES)
            dout = d_ref[:, cols]
            dy = dout * cv + _swap_pairs(dout * sv)
            dx, dgt = _rms_bwd_tile(dy, x_ref[:, cols], gv)
            dx_ref[:, cols] = dx.astype(BF16)
            dgp = dgp + jnp.sum(dgt, axis=0, keepdims=True)

        @pl.when(i == 0)
        def _():
            dg_ref[...] = dgp

        @pl.when(i > 0)
        def _():
            dg_ref[...] += dgp

    return _call(name, body, (S // ts,),
                 [(dr, (ts, W), lambda i: (i, 0)), (proj_b, (ts, W), lambda i: (i, xb)),
                  (gain, (1, LANES), lambda i: (0, 0)),
                  (cos_t, (ts, LANES), lambda i: (i, 0)), (sin_t, (ts, LANES), lambda i: (i, 0))],
                 [((S, W), BF16, (ts, W), lambda i: (i, 0)),
                  ((1, LANES), F32, (1, LANES), lambda i: (0, 0))],
                 sem=("arbitrary",))


def _row_sums(x):
    hi = x.astype(BF16)
    lo = (x - hi.astype(F32)).astype(BF16)
    ones = jnp.ones((8, LANES), BF16)
    return (_dot(ones, hi, 1, 1) + _dot(ones, lo, 1, 1))[0:1, :]


def flash_fwd(qkv, name):
    S = qkv.shape[0]
    tq = B_TQ_FWD
    scale = HEAD_DIM_B ** -0.5

    def body(q_ref, k_ref, v_ref, o_ref, l_ref):
        s = _dot(q_ref[...], k_ref[...], 1, 1)
        m = jnp.max(s, axis=-1, keepdims=True)
        e = jnp.exp2((s - m) * (scale * LOG2E))
        l = jnp.sum(e, axis=-1, keepdims=True)
        o_ref[...] = (_dot(e.astype(BF16), v_ref[...]) / l).astype(BF16)
        lse = jnp.broadcast_to(m * scale + jnp.log(l), (tq, LANES))
        l_ref[...] = _row_sums(lse) * (1.0 / LANES)

    head = lambda g, h, i: (i, g * GQA_GROUP_B + h)
    return _call(name, body, (N_KV_B, GQA_GROUP_B, S // tq),
                 [(qkv, (tq, LANES), head),
                  (qkv, (S, LANES), lambda g, h, i: (0, N_HEADS_B + g)),
                  (qkv, (S, LANES), lambda g, h, i: (0, N_HEADS_B + N_KV_B + g))],
                 [((S, N_HEADS_B * LANES), BF16, (tq, LANES), head),
                  ((N_HEADS_B, 1, S), F32, (None, 1, tq), lambda g, h, i: (g * GQA_GROUP_B + h, 0, i))],
                 sem=("parallel", "parallel", "parallel"))


def flash_bwd(qkv, k_t, do_b, o_b, lse, name):
    S = qkv.shape[0]
    tq = B_TQ_BWD
    nq = S // tq
    scale = HEAD_DIM_B ** -0.5

    def body(q_ref, k_ref, v_ref, kt_ref, do_ref, o_ref, l_ref, dq_ref, dk_ref, dv_ref, dkacc, dvacc):
        h, i = pl.program_id(1), pl.program_id(2)

        @pl.when((h == 0) & (i == 0))
        def _():
            dkacc[...] = jnp.zeros(dkacc.shape, F32)
            dvacc[...] = jnp.zeros(dvacc.shape, F32)

        q = q_ref[...]
        do = do_ref[...]
        dob = do.astype(BF16)
        t = _row_sums(do * o_ref[...].astype(F32))
        pt = jnp.exp2(_dot(k_ref[...], q, 1, 1) * (scale * LOG2E) - l_ref[...] * LOG2E)
        dsb = (pt * (_dot(v_ref[...], dob, 1, 1) - t)).astype(BF16)
        dvacc[...] += _dot(pt.astype(BF16), dob)
        dkacc[...] += _dot(dsb, q)
        dq_ref[...] = _dot(kt_ref[...], dsb).T * scale

        @pl.when((h == GQA_GROUP_B - 1) & (i == nq - 1))
        def _():
            dk_ref[...] = dkacc[...] * scale
            dv_ref[...] = dvacc[...].astype(BF16)

    head = lambda g, h, i: (i, g * GQA_GROUP_B + h)
    return _call(name, body, (N_KV_B, GQA_GROUP_B, nq),
                 [(qkv, (tq, LANES), head),
                  (qkv, (S, LANES), lambda g, h, i: (0, N_HEADS_B + g)),
                  (qkv, (S, LANES), lambda g, h, i: (0, N_HEADS_B + N_KV_B + g)),
                  (k_t, (LANES, S), lambda g, h, i: (g, 0)),
                  (do_b, (tq, LANES), head), (o_b, (tq, LANES), head),
                  (lse, (None, 1, tq), lambda g, h, i: (g * GQA_GROUP_B + h, 0, i))],
                 [((S, N_HEADS_B * LANES), F32, (tq, LANES), head),
                  ((S, N_KV_B * LANES), F32, (S, LANES), lambda g, h, i: (0, g)),
                  ((S, N_KV_B * LANES), BF16, (S, LANES), lambda g, h, i: (0, g))],
                 scratch=[pltpu.VMEM((S, LANES), F32)] * 2,
                 sem=("parallel", "arbitrary", "arbitrary"))


MERGE_TN = 512


def _mix_rows_spec(Gm, row0, n_slots, slot_map, cols=None, col_map=None):
    C = Gm.shape[2] if cols is None else cols
    cm = (lambda *idx: 0) if col_map is None else col_map
    return (Gm, (n_slots, LANES, C), lambda *idx: (slot_map(*idx), row0 // LANES, cm(*idx)))


def merge_fwd(o_a, o_b, w_a, Gm, proj_b, b_gate, name):
    S = o_a.shape[0]
    D = w_a.shape[1]
    tm, tn = 512, MERGE_TN
    ga0, gb0 = PB_GATE_A // tn, PB_GATE_B // tn

    def body(oa_ref, ob_ref, wa_ref, wb_ref, pa_ref, pb_ref, ba_ref, bb_ref, m_ref, ya_ref, yb_ref):
        ya = _dot(oa_ref[...], wa_ref[...])
        yb = _dot(ob_ref[...], wb_ref[...].reshape(N_DEV * LANES, tn))
        ga = _sigmoid(pa_ref[...] + ba_ref[...])
        gb = _sigmoid(pb_ref[...] + bb_ref[...])
        m_ref[...] = (ga * ya + gb * yb).astype(BF16)
        ya_ref[...] = ya.astype(BF16)
        yb_ref[...] = yb.astype(BF16)

    out = ((S, D), BF16, (tm, tn), lambda j, i: (i, j))
    return _call(name, body, (D // tn, S // tm),
                 [(o_a, (tm, o_a.shape[1]), lambda j, i: (i, 0)), (o_b, (tm, o_b.shape[1]), lambda j, i: (i, 0)),
                  (w_a, (w_a.shape[0], tn), lambda j, i: (0, j)),
                  _mix_rows_spec(Gm, MIX_WB, N_DEV, lambda j, i: 0, cols=tn, col_map=lambda j, i: j),
                  (proj_b, (tm, tn), lambda j, i: (i, ga0 + j)), (proj_b, (tm, tn), lambda j, i: (i, gb0 + j)),
                  (b_gate, (1, tn), lambda j, i: (0, j)), (b_gate, (1, tn), lambda j, i: (0, D // tn + j))],
                 [out, out, out], sem=("parallel", "parallel"))


def out_proj(merged, Gm, x, name):
    S, D = x.shape
    tm, tn = 512, MERGE_TN

    def body(m_ref, w_ref, x_ref, o_ref):
        o_ref[...] = x_ref[...] + _dot(m_ref[...], w_ref[...].reshape(N_DEV * LANES, tn))

    return _call(name, body, (D // tn, S // tm),
                 [(merged, (tm, D), lambda j, i: (i, 0)),
                  _mix_rows_spec(Gm, MIX_WOUT, N_DEV, lambda j, i: 0, cols=tn, col_map=lambda j, i: j),
                  (x, (tm, tn), lambda j, i: (i, j))],
                 [((S, D), F32, (tm, tn), lambda j, i: (i, j))], sem=("parallel", "parallel"))[0]


def merge_bwd(dx2, Gm, ya, yb, proj_b, b_gate, name):
    S, D = dx2.shape
    tm, tn = 512, MERGE_TN
    nn = D // tn
    ga0, gb0 = PB_GATE_A // tn, PB_GATE_B // tn

    def body(d_ref, w_ref, ya_ref, yb_ref, pa_ref, pb_ref, ba_ref, bb_ref, dya_ref, dyb_ref, dg_ref, dbg_ref):
        i = pl.program_id(1)
        dm = _dot(d_ref[...].astype(BF16), w_ref[...].reshape(tn, D), 1, 1)
        ga = _sigmoid(pa_ref[...] + ba_ref[...])
        gb = _sigmoid(pb_ref[...] + bb_ref[...])
        dya_ref[...] = (dm * ga).astype(BF16)
        dyb_ref[...] = (dm * gb).astype(BF16)
        dpa = dm * ya_ref[...].astype(F32) * ga * (1.0 - ga)
        dpb = dm * yb_ref[...].astype(F32) * gb * (1.0 - gb)
        dg_ref[0] = dpa.astype(BF16)
        dg_ref[1] = dpb.astype(BF16)
        sa = jnp.sum(dpa, axis=0, keepdims=True)
        sb = jnp.sum(dpb, axis=0, keepdims=True)

        @pl.when(i == 0)
        def _():
            dbg_ref[0] = sa
            dbg_ref[1] = sb

        @pl.when(i > 0)
        def _():
            dbg_ref[0] += sa
            dbg_ref[1] += sb

    tile = ((tm, tn), lambda j, i: (i, j))
    dya, dyb, dgate, dbg = _call(
        name, body, (nn, S // tm),
        [(dx2, (tm, D), lambda j, i: (i, 0)),
         _mix_rows_spec(Gm, MIX_WOUT, tn // LANES, lambda j, i: j),
         (ya,) + tile, (yb,) + tile,
         (proj_b, (tm, tn), lambda j, i: (i, ga0 + j)), (proj_b, (tm, tn), lambda j, i: (i, gb0 + j)),
         (b_gate, (1, tn), lambda j, i: (0, j)), (b_gate, (1, tn), lambda j, i: (0, nn + j))],
        [((S, D), BF16) + tile, ((S, D), BF16) + tile,
         ((2, S, D), BF16, (2, tm, tn), lambda j, i: (0, i, j)),
         ((2, 1, D), F32, (2, 1, tn), lambda j, i: (0, 0, j))],
        sem=("parallel", "arbitrary"))
    return dya, dyb, dgate, dbg


def matmul_nt(a, b_spec_fn, N, name, tn=512):
    S, K = a.shape
    tm = 512

    def body(a_ref, b_ref, o_ref):
        b = b_ref[...]
        o_ref[...] = _dot(a_ref[...], b.reshape(-1, b.shape[-1]), 1, 1)

    return _call(name, body, (N // tn, S // tm),
                 [(a, (tm, K), lambda j, i: (i, 0)), b_spec_fn(lambda j, i: j)],
                 [((S, N), F32, (tm, tn), lambda j, i: (i, j))], sem=("parallel", "parallel"))[0]


def weight_grad_rows(a, b, grads, row0, name):
    S, M = a.shape
    N = b.shape[1]
    tmm = 512
    tk = WGRAD_TK
    nk = S // tk

    def body(g_ref, a_ref, b_ref, o_ref, acc_ref):
        k = pl.program_id(1)
        p = _dot(a_ref[...], b_ref[...].astype(BF16), 0, 0)

        @pl.when(k == 0)
        def _():
            acc_ref[...] = p

        @pl.when(k > 0)
        def _():
            acc_ref[...] += p

        @pl.when(k == nk - 1)
        def _():
            o_ref[...] = acc_ref[...].astype(BF16).reshape(tmm // LANES, LANES, N)

    return pl.pallas_call(
        body,
        out_shape=jax.ShapeDtypeStruct(grads.shape, BF16),
        grid=(M // tmm, nk),
        in_specs=[pl.BlockSpec(memory_space=pl.ANY),
                  pl.BlockSpec((tk, tmm), lambda j, k: (k, j)),
                  pl.BlockSpec((tk, N), lambda j, k: (k, 0))],
        out_specs=pl.BlockSpec((tmm // LANES, LANES, N), lambda j, k: (j, row0 // LANES, 0)),
        scratch_shapes=[pltpu.VMEM((tmm, N), F32)],
        input_output_aliases={0: 0},
        name=name,
        compiler_params=pltpu.CompilerParams(dimension_semantics=("parallel", "arbitrary"),
                                             vmem_limit_bytes=VMEM_LIMIT),
    )(grads, a, b)


def weight_grad_plain(a, b, name):
    S, M = a.shape
    N = b.shape[1]
    tk = WGRAD_TK
    nk = S // tk

    def body(a_ref, b_ref, o_ref, acc_ref):
        k = pl.program_id(0)
        p = _dot(a_ref[...], b_ref[...], 0, 0)

        @pl.when(k == 0)
        def _():
            acc_ref[...] = p

        @pl.when(k > 0)
        def _():
            acc_ref[...] += p

        @pl.when(k == nk - 1)
        def _():
            o_ref[...] = acc_ref[...].astype(BF16)

    return _call(name, body, (nk,),
                 [(a, (tk, M), lambda k: (k, 0)), (b, (tk, N), lambda k: (k, 0))],
                 [((M, N), BF16, (M, N), lambda k: (0, 0))],
                 scratch=[pltpu.VMEM((M, N), F32)], sem=("arbitrary",))[0]


def local_step(x, tgt, p, get_g1_up, get_g1_down, get_gm, get_g2, emit, start_token):
    S, D = x.shape
    after = lambda t: t[0:1, 0:1]
    buckets = _bucket_tables()
    cos_t, sin_t = _rope_tables(S)
    gains = jnp.concatenate([jnp.tile(p["q_norm"], (1, N_HEADS_B)), jnp.tile(p["k_norm"], (1, N_KV_B)),
                             jnp.ones((1, N_KV_B * LANES), F32)], axis=1)

    n1 = rms_fwd(x, p["ffn1_norm"] + after(start_token), "ffn1_norm")
    bias = bias_build(p["rel_bias"] + after(start_token), buckets)
    g1_up = get_g1_up((n1, bias))
    ab1 = ffn_up(n1, (g1_up, None), "ffn1_up")
    G1 = (g1_up, get_g1_down(ab1))
    x1 = ffn_down(ab1, G1, x, "ffn1_down")

    Gm = get_gm(x1)
    w_a = Gm[:, MIX_WA:MIX_ROWS, :].reshape(N_DEV, GROUP_WIDTH_A, LANES).transpose(1, 0, 2).reshape(GROUP_WIDTH_A, D)
    hm = rms_fwd(x1, p["mix_norm"], "mix_norm")
    n_a = A_QKV_WIDTH // PROJ_TN
    proj_a = [in_proj(hm, Gm, g, 3, BF16, "in_proj_a%d" % g, tile_stride=3) for g in range(3)]
    proj_b = in_proj(hm, Gm, n_a, PB_WIDTH // PROJ_TN, F32, "in_proj_b")

    outs, lses = [], []
    for g in range(3):
        o, l = a_fwd(proj_a[g], bias[g], g, "a_fwd_%d" % g)
        outs.append(o)
        lses.append(l)
    o_a, lse_tot = a_combine(outs, lses, "a_combine")

    qkv = qkv_prep(proj_b, gains, cos_t, sin_t, "qkv_prep")
    k_t = qkv[:, N_HEADS_B * LANES:(N_HEADS_B + N_KV_B) * LANES].T
    o_b, lse_b = flash_fwd(qkv, "flash_fwd")

    merged, ya, yb = merge_fwd(o_a, o_b, w_a, Gm, proj_b, p["b_gate"], "merge_fwd")
    x2 = out_proj(merged, Gm, x1, "out_proj")

    G2 = get_g2(x2)
    n2 = rms_fwd(x2, p["ffn2_norm"], "ffn2_norm")
    ab2 = ffn_up(n2, G2, "ffn2_up")
    x3 = ffn_down(ab2, G2, x2, "ffn2_down")

    loss, dx3, dx3_b, d_final = final_loss(x3, tgt, p["final_norm"], "final_loss")

    dabh2, gw2 = ffn_bwd_weights(dx3_b, ab2, n2, G2, "ffn2_bwd")
    t2 = emit("ffn2", gw2)
    dx2, dx2_b, d_ffn2_norm = ffn_bwd_input(dabh2, G2, x2, p["ffn2_norm"] + after(t2), dx3, "ffn2_bwd")

    dya, dyb, dgate, dbg = merge_bwd(dx2_b, Gm, ya, yb, proj_b, p["b_gate"], "merge_bwd")
    gm_grads = jnp.zeros(Gm.shape, BF16)
    gm_grads = weight_grad_rows(merged, dx2_b, gm_grads, MIX_WOUT, "dw_out")
    gm_grads = weight_grad_rows(o_b, dyb, gm_grads, MIX_WB, "dw_branch_b")
    dw_a = weight_grad_plain(o_a, dya, "dw_branch_a")
    do_a = matmul_nt(dya, lambda jm: (w_a, (MERGE_TN, D), lambda j, i: (jm(j, i), 0)), GROUP_WIDTH_A, "do_a")
    do_b = matmul_nt(dyb, lambda jm: _mix_rows_spec(Gm, MIX_WB, MERGE_TN // LANES, jm), N_HEADS_B * LANES, "do_b")

    dq_r, dk_r, dv_b = flash_bwd(qkv, k_t, do_b, o_b, lse_b, "flash_bwd")
    dq_b, d_q_norm = qk_prep_bwd(dq_r, proj_b, 0, p["q_norm"], cos_t, sin_t, "q_prep_bwd")
    dk_b, d_k_norm = qk_prep_bwd(dk_r, proj_b, N_HEADS_B, p["k_norm"], cos_t, sin_t, "k_prep_bwd")

    dqs, dks, dvs, dbs = [], [], [], []
    for g in range(3):
        dq, dk, dv, db = a_bwd(proj_a[g], bias[g], do_a, o_a, lse_tot, g, "a_bwd_%d" % g)
        dqs.append(dq)
        dks.append(dk)
        dvs.append(dv)
        dbs.append(db)
    d_rel_bias = bias_bwd(jnp.stack(dbs, axis=0).reshape(3, HEADS_PER_GROUP_A, A_TQ, A_WIN), buckets)

    dproj = jnp.concatenate(dqs + dks + dvs + [dq_b, dk_b, dv_b, dgate[0], dgate[1]], axis=1)
    gm_grads = in_proj_bwd_dw(dproj, hm, gm_grads, "in_proj_bwd")
    dw_a_sh = dw_a.reshape(GROUP_WIDTH_A, N_DEV, LANES).transpose(1, 0, 2).reshape(N_DEV, MIX_ROWS - MIX_WA, D)
    gm_grads = lax.dynamic_update_slice(gm_grads, dw_a_sh, (0, MIX_WA, 0))
    tm = emit("mix", gm_grads)
    dx1, dx1_b, d_mix_norm = in_proj_bwd_dh(dproj, Gm, x1, p["mix_norm"] + after(tm), dx2, "in_proj_bwd")

    dabh1, gw1 = ffn_bwd_weights(dx1_b, ab1, n1, G1, "ffn1_bwd")
    t1 = emit("ffn1", gw1)
    dx0, d_ffn1_norm = ffn_bwd_input(dabh1, G1, x, p["ffn1_norm"] + after(t1), dx1, "ffn1_bwd", as_operand=False)

    small = dict(ffn1_norm=d_ffn1_norm, mix_norm=d_mix_norm, b_gate=dbg.reshape(1, 2 * D),
                 q_norm=d_q_norm, k_norm=d_k_norm, rel_bias=d_rel_bias, ffn2_norm=d_ffn2_norm,
                 final_norm=d_final)
    return loss, dx0, small


def _pack_small(t, loss_row):
    row6 = jnp.concatenate([t["q_norm"].reshape(1, -1), t["k_norm"].reshape(1, -1), t["rel_bias"].reshape(1, -1)], axis=1)
    return jnp.concatenate([t["ffn1_norm"].reshape(1, -1), t["mix_norm"].reshape(1, -1), t["b_gate"].reshape(2, -1),
                            t["ffn2_norm"].reshape(1, -1), t["final_norm"].reshape(1, -1), row6, loss_row], axis=0)


def _unpack_small(a, shapes):
    return dict(ffn1_norm=a[0:1].reshape(shapes["ffn1_norm"]), mix_norm=a[1:2].reshape(shapes["mix_norm"]),
                b_gate=a[2:4].reshape(shapes["b_gate"]), ffn2_norm=a[4:5].reshape(shapes["ffn2_norm"]),
                final_norm=a[5].reshape(shapes["final_norm"]), q_norm=a[6:7, 0:128].reshape(shapes["q_norm"]),
                k_norm=a[6:7, 128:256].reshape(shapes["k_norm"]), rel_bias=a[6, 256:1024].reshape(shapes["rel_bias"]))


SMALL = ("ffn1_norm", "mix_norm", "b_gate", "q_norm", "k_norm", "rel_bias", "ffn2_norm", "final_norm")
ORDER = ("ffn1_norm", "ffn1_w1", "ffn1_w3", "ffn1_w2", "mix_norm", "w_in", "b_gate", "q_norm", "k_norm", "rel_bias",
         "w_branch_a", "w_branch_b", "w_out", "ffn2_norm", "ffn2_w1", "ffn2_w3", "ffn2_w2", "final_norm")


def kernel(x, ffn1_norm, ffn1_w1, ffn1_w3, ffn1_w2, mix_norm, w_in, b_gate, q_norm, k_norm, rel_bias, w_branch_a, w_branch_b, w_out, ffn2_norm, ffn2_w1, ffn2_w3, ffn2_w2, final_norm, loss_target, m_ffn1_norm, m_ffn1_w1, m_ffn1_w3, m_ffn1_w2, m_mix_norm, m_w_in, m_b_gate, m_q_norm, m_k_norm, m_rel_bias, m_w_branch_a, m_w_branch_b, m_w_out, m_ffn2_norm, m_ffn2_w1, m_ffn2_w3, m_ffn2_w2, m_final_norm, v_ffn1_norm, v_ffn1_w1, v_ffn1_w3, v_ffn1_w2, v_mix_norm, v_w_in, v_b_gate, v_q_norm, v_k_norm, v_rel_bias, v_w_branch_a, v_w_branch_b, v_w_out, v_ffn2_norm, v_ffn2_w1, v_ffn2_w3, v_ffn2_w2, v_final_norm):
    args = dict(locals())
    w = {n: args[n] for n in ORDER}
    m = {n: args["m_" + n] for n in ORDER}
    v = {n: args["v_" + n] for n in ORDER}
    D = x.shape[2]

    blocks = (
        ("ffn1_up", lambda t: jnp.concatenate([ffn1_w1[0].T + t, ffn1_w3[0].T + t], axis=0)),
        ("ffn1_down", lambda t: ffn1_w2[0] + t),
        ("mix", lambda t: jnp.concatenate([w_in[0] + t, w_branch_b[0] + t, w_out[0] + t,
                                           w_branch_a[0].reshape(MIX_ROWS - MIX_WA, D) + t], axis=0)),
        ("ffn2", lambda t: jnp.concatenate([ffn2_w1[0].T + t, ffn2_w3[0].T + t, ffn2_w2[0] + t], axis=0)),
    )
    gathers = {}
    start_token = jnp.zeros((8, LANES), F32)
    for tag, make in blocks:
        gathers[tag] = all_gather_start(make(start_token[0:1, 0:1]).astype(BF16), "all_gather_" + tag + "_start")
        start_token = gathers[tag][4]

    def gathered(tag):
        def get(after):
            return all_gather_finish(*_split_wait("all_gather_" + tag + "_wait", gathers[tag], 4, after),
                                     "all_gather_" + tag + "_finish")
        return get

    core = lax.axis_index("c").astype(jnp.int32).reshape(1)
    chip = (2 * lax.axis_index("x") + lax.axis_index("y")).astype(jnp.int32).reshape(1)
    exchanges = {}

    def emit(tag, gw):
        (theirs,) = reduce_scatter_pair([gw], "reduce_scatter_pair_" + tag)
        part = pair_add(gw, theirs, core, "pair_add_" + tag)
        exchanges[tag] = reduce_scatter_start(part, "reduce_scatter_" + tag + "_start")
        return exchanges[tag][4]

    small_p = dict(ffn1_norm=ffn1_norm, mix_norm=mix_norm, b_gate=b_gate, q_norm=q_norm, k_norm=k_norm,
                   rel_bias=rel_bias, ffn2_norm=ffn2_norm, final_norm=final_norm.reshape(1, D))
    loss_p, grad_x, small_g = local_step(x[0], loss_target[0], small_p, gathered("ffn1_up"), gathered("ffn1_down"),
                                         gathered("mix"), gathered("ffn2"), emit, start_token)

    def landed(tag, after):
        return _split_wait("reduce_scatter_" + tag + "_wait", exchanges[tag], 3, after)

    grads, delta, new_m, new_v = {}, {}, {}, {}

    def finish(n, part, land, off, blk, transposed=False):
        shp = w[n].shape
        if transposed:
            to2 = lambda a: a.reshape(shp[-2], shp[-1]).T
            back = lambda a: a.T.reshape(shp)
        else:
            to2 = lambda a: a.reshape(shp[-2], shp[-1])
            back = lambda a: a.reshape(shp)
        res = sum_adamw(part, land, chip, off, blk, to2(w[n]), to2(m[n]), to2(v[n]), "update_" + n)
        grads[n], delta[n], new_m[n], new_v[n] = [back(a) for a in res]

    last_token = exchanges["ffn1"][4]
    for tag, after in (("ffn2", last_token), ("ffn1", grad_x)):
        part, land = landed(tag, after)
        finish(tag + "_w1", part, land, 0, FFN_SHARD, transposed=True)
        finish(tag + "_w3", part, land, FFN_SHARD, FFN_SHARD, transposed=True)
        finish(tag + "_w2", part, land, 2 * FFN_SHARD, FFN_SHARD)
        if tag == "ffn2":
            part_m, land_m = landed("mix", last_token)
            finish("w_in", part_m, land_m, MIX_WIN, LANES)
            finish("w_branch_b", part_m, land_m, MIX_WB, LANES)
            finish("w_out", part_m, land_m, MIX_WOUT, LANES)
            grads["w_branch_a"] = sum_chips(part_m, land_m, chip, MIX_WA, MIX_ROWS - MIX_WA, MIX_ROWS - MIX_WA,
                                            "w_branch_a_sum").reshape(w_branch_a.shape)
    loss_row = jnp.pad(loss_p, ((0, 0), (0, D - LANES)))
    smalls = small_all_gather(_pack_small(small_g, loss_row))
    small_sum = sum_slots(smalls, 0, N_DEV, N_DEV, "small_sum")
    small_shapes = {n: w[n].shape for n in SMALL}
    grads.update(_unpack_small(small_sum, small_shapes))
    loss = small_sum[7, 0]

    n = "w_branch_a"
    two_d = lambda a: a.reshape(w[n].shape[-2], w[n].shape[-1])
    d_, m_, v_ = adamw(two_d(w[n]), two_d(grads[n]), two_d(m[n]), two_d(v[n]), "adamw_" + n)
    delta[n], new_m[n], new_v[n] = [a.reshape(w[n].shape) for a in (d_, m_, v_)]
    zero_row = jnp.zeros((1, D), F32)
    pack = lambda t: _pack_small({n: t[n] for n in SMALL}, zero_row)
    d_, m_, v_ = adamw(pack(w), small_sum, pack(m), pack(v), "adamw_small")
    for src, dst in ((d_, delta), (m_, new_m), (v_, new_v)):
        dst.update(_unpack_small(src, small_shapes))

    return (loss, grad_x[None], *[grads[n] for n in ORDER], *[delta[n] for n in ORDER],
            *[new_m[n] for n in ORDER], *[new_v[n] for n in ORDER])
```

```python
import math

import jax
import jax.numpy as jnp
from jax import lax
from jax.experimental import pallas as pl
from jax.experimental.pallas import tpu as pltpu

F32 = jnp.float32
BF16 = jnp.bfloat16
MESH = pl.DeviceIdType.MESH

V7X_VMEM_BYTES = 64 * 1024 * 1024
VMEM_LIMIT = V7X_VMEM_BYTES - 8 * 1024 * 1024
LANES = 128

N_DEV = 8
EPS = 1e-6
NEG_INF = -1e30

DILATIONS = (1, 4, 16)
HALF_WINDOW = 64
HEAD_DIM_A = 64
HEADS_PER_GROUP_A = 8
GROUP_WIDTH_A = 512
A_QKV_WIDTH = 4608
A_GROUP_QKV = A_QKV_WIDTH // 3
A_TQ = 128
A_WIN = A_TQ + 2 * HALF_WINDOW
A_UNROLL = 8
A_SCALE = HEAD_DIM_A ** -0.5
WGRAD_TK = 2048
HEAD_DIM_B = 128
N_HEADS_B = 8
N_KV_B = 2
GQA_GROUP_B = 4
GRID_W = 64
ROPE_THETA = 10000.0
B_TQ_FWD = 256
B_TQ_BWD = 512
B_HEADS_PER_STEP = 4
LOG2E = 1.4426950408889634
N_BUCKETS = 32
MAX_DISTANCE = 1024
PB_WIDTH = 3584
PB_GATE_A = 1536
PB_GATE_B = 2560

ADAM_LR = 0.001
ADAM_B1 = 0.9
ADAM_B2 = 0.999
ADAM_EPS = 1e-08
ADAM_WD = 0.01
ADAM_STEP = 10

FFN_SHARD = 352
MIX_WIN, MIX_WB, MIX_WOUT, MIX_WA = 0, 1024, 1152, 1280
MIX_ROWS = 1344


def _dot(a, b, ca=1, cb=0):
    return lax.dot_general(a, b, (((ca,), (cb,)), ((), ())), preferred_element_type=F32)


def _call(name, body, grid, ins, outs, scratch=(), sem=None, aliases=None):
    ins = [tuple(i) + (None,) * (4 - len(i)) for i in ins]
    res = pl.pallas_call(
        body,
        out_shape=[jax.ShapeDtypeStruct(s, d) for (s, d, _, _) in outs],
        grid=grid,
        in_specs=[pl.BlockSpec(bs, im, pipeline_mode=pm) for (_, bs, im, pm) in ins],
        out_specs=[pl.BlockSpec(bs, im) for (_, _, bs, im) in outs],
        scratch_shapes=list(scratch),
        name=name,
        input_output_aliases=aliases or {},
        compiler_params=pltpu.CompilerParams(dimension_semantics=sem, vmem_limit_bytes=VMEM_LIMIT),
    )(*[i[0] for i in ins])
    return res


def _sigmoid(x):
    return 0.5 * jnp.tanh(0.5 * x) + 0.5


def _position():
    return lax.axis_index("x"), lax.axis_index("y"), lax.axis_index("c")


def _hbm_specs(n):
    return [pl.BlockSpec(memory_space=pl.ANY) for _ in range(n)]


PAIR_BUFFERS = 4


def reduce_scatter_pair(grads, name):
    n = len(grads)
    C = grads[0].shape[2]
    half = [g.shape[1] // 2 for g in grads]
    chunks = [(i, q, hf) for i in range(n) for q in range(4) for hf in range(2)]
    nb = PAIR_BUFFERS

    def body(*refs):
        ins, theirs = refs[:n], refs[n:2 * n]
        buf, load_sems, send_sems, recv_sems = refs[2 * n:]
        x, y, c = _position()
        sibling = (x, y, 1 - c)

        def load(k):
            i, q, hf = chunks[k]
            r = half[i]
            return pltpu.make_async_copy(ins[i].at[2 * q + (1 - c), pl.ds(hf * r, r), :],
                                         buf.at[k % nb, pl.ds(0, r), :], load_sems.at[k % nb])

        def send(k):
            i, q, hf = chunks[k]
            r = half[i]
            return pltpu.make_async_remote_copy(
                src_ref=buf.at[k % nb, pl.ds(0, r), :], dst_ref=theirs[i].at[q, pl.ds(hf * r, r), :],
                send_sem=send_sems.at[k % nb], recv_sem=recv_sems.at[i],
                device_id=sibling, device_id_type=MESH)

        for k in range(len(chunks) + 1):
            if k < len(chunks):
                if k >= nb:
                    send(k - nb).wait_send()
                load(k).start()
            if k >= 1:
                load(k - 1).wait()
                send(k - 1).start()
        for k in range(max(0, len(chunks) - nb), len(chunks)):
            send(k).wait_send()
        for i in range(n):
            pltpu.make_async_remote_copy(
                src_ref=theirs[i], dst_ref=theirs[i], send_sem=send_sems.at[0], recv_sem=recv_sems.at[i],
                device_id=sibling, device_id_type=MESH).wait_recv()

    return pl.pallas_call(
        body,
        out_shape=[jax.ShapeDtypeStruct((4,) + g.shape[1:], g.dtype) for g in grads],
        in_specs=_hbm_specs(n),
        out_specs=_hbm_specs(n),
        scratch_shapes=[pltpu.VMEM((nb, max(half), C), grads[0].dtype), pltpu.SemaphoreType.DMA((nb,)),
                        pltpu.SemaphoreType.DMA((nb,)), pltpu.SemaphoreType.DMA((n,))],
        name=name,
        compiler_params=pltpu.CompilerParams(vmem_limit_bytes=VMEM_LIMIT),
    )(*grads)


_HBM_SPEC = pl.BlockSpec(memory_space=pltpu.HBM)
_SEM_SPEC = pl.BlockSpec(memory_space=pltpu.SEMAPHORE)
_TOKEN_SPEC = pl.BlockSpec(memory_space=pltpu.VMEM)
_DATAFLOW = pltpu.SideEffectType.DATAFLOW_SIDE_EFFECTING


def _split_start(name, body, src, land_shape):
    def full_body(src_ref, land_ref, send_sem, recv_sem, src_thru, land_thru, token):
        body(src_ref, land_ref, send_sem, recv_sem)
        token[...] = jnp.zeros_like(token)

    land = pltpu.with_memory_space_constraint(lax.empty(land_shape, src.dtype), pltpu.HBM)
    return pl.pallas_call(
        full_body, name=name,
        out_shape=(pltpu.SemaphoreType.DMA(()), pltpu.SemaphoreType.DMA(()),
                   pltpu.HBM(src.shape, src.dtype), pltpu.HBM(land_shape, src.dtype),
                   jax.ShapeDtypeStruct((8, LANES), F32)),
        in_specs=(_HBM_SPEC, _HBM_SPEC),
        out_specs=(_SEM_SPEC, _SEM_SPEC, _HBM_SPEC, _HBM_SPEC, _TOKEN_SPEC),
        input_output_aliases={0: 2, 1: 3},
        compiler_params=pltpu.CompilerParams(has_side_effects=_DATAFLOW),
    )(pltpu.with_memory_space_constraint(src, pltpu.HBM), land)


def _split_wait(name, started, n_blocks, after):
    send_sem, recv_sem, src_thru, land_thru, _ = started
    after = after if isinstance(after, tuple) else (after,)

    def body(src_ref, land_ref, send_sem, recv_sem, *rest):
        x, y, c = _position()
        blocks = land_ref.at[pl.ds(0, n_blocks)]
        copy = pltpu.make_async_remote_copy(src_ref=blocks, dst_ref=blocks, send_sem=send_sem, recv_sem=recv_sem,
                                            device_id=(x, y, c), device_id_type=MESH)
        copy.wait_send()
        copy.wait_recv()

    return pl.pallas_call(
        body, name=name,
        out_shape=(pltpu.HBM(src_thru.shape, src_thru.dtype), pltpu.HBM(land_thru.shape, land_thru.dtype)),
        in_specs=(_HBM_SPEC, _HBM_SPEC, _SEM_SPEC, _SEM_SPEC) + (pl.BlockSpec(memory_space=pl.ANY),) * len(after),
        out_specs=(_HBM_SPEC, _HBM_SPEC),
        input_output_aliases={0: 0, 1: 1},
        compiler_params=pltpu.CompilerParams(has_side_effects=_DATAFLOW),
    )(src_thru, land_thru, send_sem, recv_sem, *after)


def all_gather_start(block, name):
    def body(b_ref, land_ref, send_sem, recv_sem):
        x, y, c = _position()
        for peer in [(x, y, 1 - c), (1 - x, y, c), (x, 1 - y, c), (1 - x, 1 - y, c)]:
            pltpu.make_async_remote_copy(src_ref=b_ref, dst_ref=land_ref.at[4 * x + 2 * y + c],
                                         send_sem=send_sem, recv_sem=recv_sem,
                                         device_id=peer, device_id_type=MESH).start()

    return _split_start(name, body, block, (N_DEV,) + block.shape)


def all_gather_finish(block, land, name):
    R, C = block.shape

    def body(b_ref, land_in, land_ref, stage, load_sems, send_sems, recv_sems, own_sem):
        x, y, c = _position()
        sibling = (x, y, 1 - c)
        chips = [(1 - x, y), (x, 1 - y), (1 - x, 1 - y)]
        own_in = pltpu.make_async_copy(b_ref, stage.at[3], load_sems.at[3])
        own_in.start()
        loads = [pltpu.make_async_copy(land_in.at[4 * px + 2 * py + c], stage.at[j], load_sems.at[j])
                 for j, (px, py) in enumerate(chips)]
        for ld in loads:
            ld.start()
        sends = []
        for j, (px, py) in enumerate(chips):
            loads[j].wait()
            dst = land_ref.at[4 * px + 2 * py + c]
            cp = pltpu.make_async_remote_copy(src_ref=stage.at[j], dst_ref=dst, send_sem=send_sems.at[j],
                                              recv_sem=recv_sems.at[j], device_id=sibling, device_id_type=MESH)
            cp.start()
            sends.append(cp)
        own_in.wait()
        own_out = pltpu.make_async_copy(stage.at[3], land_ref.at[4 * x + 2 * y + c], own_sem)
        own_out.start()
        for j, (px, py) in enumerate(chips):
            dst = land_ref.at[4 * px + 2 * py + (1 - c)]
            pltpu.make_async_remote_copy(src_ref=stage.at[j], dst_ref=dst, send_sem=send_sems.at[j],
                                         recv_sem=recv_sems.at[j], device_id=sibling,
                                         device_id_type=MESH).wait_recv()
        for cp in sends:
            cp.wait_send()
        own_out.wait()

    return pl.pallas_call(
        body,
        out_shape=jax.ShapeDtypeStruct(land.shape, land.dtype),
        in_specs=_hbm_specs(2),
        out_specs=pl.BlockSpec(memory_space=pl.ANY),
        scratch_shapes=[pltpu.VMEM((4, R, C), block.dtype), pltpu.SemaphoreType.DMA((4,)),
                        pltpu.SemaphoreType.DMA((3,)), pltpu.SemaphoreType.DMA((3,)), pltpu.SemaphoreType.DMA],
        input_output_aliases={1: 0},
        name=name,
        compiler_params=pltpu.CompilerParams(vmem_limit_bytes=VMEM_LIMIT),
    )(block, land)


def reduce_scatter_start(parts, name):
    def body(p_ref, land_ref, send_sem, recv_sem):
        x, y, c = _position()
        for px, py in [(1 - x, y), (x, 1 - y), (1 - x, 1 - y)]:
            pltpu.make_async_remote_copy(src_ref=p_ref.at[2 * px + py], dst_ref=land_ref.at[2 * x + y],
                                         send_sem=send_sem, recv_sem=recv_sem,
                                         device_id=(px, py, c), device_id_type=MESH).start()

    return _split_start(name, body, parts, parts.shape)


def small_all_gather(small):
    def body(small_ref, smalls, s_send, s_recv, s_local):
        x, y, c = _position()
        me = 4 * x + 2 * y + c
        lc = pltpu.make_async_copy(small_ref, smalls.at[me], s_local)
        lc.start()
        remote = []
        k = 0
        for dx in (0, 1):
            for dy in (0, 1):
                for dc in (0, 1):
                    if dx + dy + dc == 0:
                        continue
                    peer = (1 - x if dx else x, 1 - y if dy else y, 1 - c if dc else c)
                    rc = pltpu.make_async_remote_copy(
                        src_ref=small_ref, dst_ref=smalls.at[me],
                        send_sem=s_send.at[k], recv_sem=s_recv.at[k],
                        device_id=peer, device_id_type=MESH)
                    rc.start()
                    remote.append(rc)
                    k += 1
        for rc in remote:
            rc.wait()
        lc.wait()

    return pl.pallas_call(
        body,
        out_shape=jax.ShapeDtypeStruct((N_DEV,) + small.shape, small.dtype),
        in_specs=_hbm_specs(1),
        out_specs=pl.BlockSpec(memory_space=pl.ANY),
        scratch_shapes=[pltpu.SemaphoreType.DMA((7,)), pltpu.SemaphoreType.DMA((7,)), pltpu.SemaphoreType.DMA],
        name="small_all_gather",
    )(small)


def pair_add(grads, theirs, core, name):
    _, R, C = theirs.shape
    tr = R // 2

    def body(c_ref, a_ref, b_ref, o_ref):
        o_ref[...] = (a_ref[...].astype(F32) + b_ref[...].astype(F32)).astype(BF16)

    return pl.pallas_call(
        body,
        out_shape=jax.ShapeDtypeStruct(theirs.shape, BF16),
        grid_spec=pltpu.PrefetchScalarGridSpec(
            num_scalar_prefetch=1, grid=(4, R // tr),
            in_specs=[pl.BlockSpec((None, tr, C), lambda q, i, c: (2 * q + c[0], i, 0)),
                      pl.BlockSpec((None, tr, C), lambda q, i, c: (q, i, 0))],
            out_specs=pl.BlockSpec((None, tr, C), lambda q, i, c: (q, i, 0))),
        name=name,
        compiler_params=pltpu.CompilerParams(dimension_semantics=("parallel", "parallel"),
                                             vmem_limit_bytes=VMEM_LIMIT),
    )(core, grads, theirs)


def sum_slots(recv, off, rows, blk, name):
    nq, _, C = recv.shape
    ob = off // blk

    def body(r_ref, o_ref):
        acc = r_ref[0].astype(F32)
        for q in range(1, nq):
            acc = acc + r_ref[q].astype(F32)
        o_ref[...] = acc

    return _call(name, body, (rows // blk,),
                 [(recv, (nq, blk, C), lambda i: (0, ob + i, 0))],
                 [((rows, C), F32, (blk, C), lambda i: (i, 0))], sem=("parallel",))[0]


def sum_chips(parts, land, chip, off, rows, blk, name):
    C = parts.shape[2]
    ob = off // blk

    def body(c_ref, own_ref, a_ref, b_ref, d_ref, o_ref):
        o_ref[...] = ((own_ref[...].astype(F32) + a_ref[...].astype(F32)) + b_ref[...].astype(F32)) \
            + d_ref[...].astype(F32)

    def entry(flip):
        return pl.BlockSpec((None, blk, C), lambda i, c: (c[0] ^ flip, ob + i, 0))

    return pl.pallas_call(
        body,
        out_shape=jax.ShapeDtypeStruct((rows, C), F32),
        grid_spec=pltpu.PrefetchScalarGridSpec(
            num_scalar_prefetch=1, grid=(rows // blk,),
            in_specs=[entry(0), entry(1), entry(2), entry(3)],
            out_specs=pl.BlockSpec((blk, C), lambda i, c: (i, 0))),
        name=name,
        compiler_params=pltpu.CompilerParams(dimension_semantics=("parallel",), vmem_limit_bytes=VMEM_LIMIT),
    )(chip, parts, land, land, land)


def _adamw_update(wv, gv, mv, vv):
    nm = ADAM_B1 * mv + (1.0 - ADAM_B1) * gv
    nv = ADAM_B2 * vv + (1.0 - ADAM_B2) * (gv * gv)
    c1 = 1.0 / (1.0 - ADAM_B1 ** ADAM_STEP)
    c2 = 1.0 / (1.0 - ADAM_B2 ** ADAM_STEP)
    return -ADAM_LR * ((nm * c1) / (jnp.sqrt(nv * c2) + ADAM_EPS) + ADAM_WD * wv), nm, nv


def sum_adamw(parts, land, chip, off, blk, w, m, v, name):
    rows, C = w.shape
    ob = off // blk

    def body(c_ref, own_ref, a_ref, b_ref, d_ref, w_ref, m_ref, v_ref, g_out, d_out, m_out, v_out):
        gv = ((own_ref[...].astype(F32) + a_ref[...].astype(F32)) + b_ref[...].astype(F32)) \
            + d_ref[...].astype(F32)
        g_out[...] = gv
        d_out[...], m_out[...], v_out[...] = _adamw_update(w_ref[...], gv, m_ref[...], v_ref[...])

    def entry(flip):
        return pl.BlockSpec((None, blk, C), lambda i, c: (c[0] ^ flip, ob + i, 0))

    plain = pl.BlockSpec((blk, C), lambda i, c: (i, 0))
    return pl.pallas_call(
        body,
        out_shape=[jax.ShapeDtypeStruct((rows, C), F32)] * 4,
        grid_spec=pltpu.PrefetchScalarGridSpec(
            num_scalar_prefetch=1, grid=(rows // blk,),
            in_specs=[entry(0), entry(1), entry(2), entry(3), plain, plain, plain],
            out_specs=[plain] * 4),
        name=name,
        compiler_params=pltpu.CompilerParams(dimension_semantics=("parallel",), vmem_limit_bytes=VMEM_LIMIT),
    )(chip, parts, land, land, land, w, m, v)


def adamw(w, g, m, v, name):
    R, C = w.shape
    tr = R
    for cand in (256, 128, 64, 32, 16, 8):
        if R % cand == 0 and R > cand:
            tr = cand
            break

    def body(w_ref, g_ref, m_ref, v_ref, d_ref, nm_ref, nv_ref):
        d_ref[...], nm_ref[...], nv_ref[...] = _adamw_update(w_ref[...], g_ref[...], m_ref[...], v_ref[...])

    spec = ((tr, C), lambda i: (i, 0))
    out = ((R, C), F32) + spec
    return _call(name, body, (R // tr,), [(w,) + spec, (g,) + spec, (m,) + spec, (v,) + spec],
                 [out, out, out], sem=("parallel",))


def rms_fwd(x, g, name):
    S, D = x.shape
    tr = 512

    def body(x_ref, g_ref, o_ref):
        xv = x_ref[...]
        r = lax.rsqrt(jnp.mean(xv * xv, axis=-1, keepdims=True) + EPS)
        o_ref[...] = (xv * r * g_ref[...]).astype(BF16)

    return _call(name, body, (S // tr,),
                 [(x, (tr, D), lambda i: (i, 0)), (g, (1, D), lambda i: (0, 0))],
                 [((S, D), BF16, (tr, D), lambda i: (i, 0))], sem=("parallel",))[0]


def _rms_bwd_tile(dn, xv, gv):
    r = lax.rsqrt(jnp.mean(xv * xv, axis=-1, keepdims=True) + EPS)
    xh = xv * r
    dxh = dn * gv
    dx = r * (dxh - xh * jnp.mean(dxh * xh, axis=-1, keepdims=True))
    return dx, dn * xh


def final_loss(x, tgt, g, name):
    S, D = x.shape
    tr = 256

    def body(x_ref, t_ref, g_ref, l_ref, dx_ref, dxb_ref, dg_ref):
        i = pl.program_id(0)
        xv, gv = x_ref[...], g_ref[...]
        r = lax.rsqrt(jnp.mean(xv * xv, axis=-1, keepdims=True) + EPS)
        xh = xv * r
        e = xh * gv - t_ref[...]
        part = 0.5 * jnp.sum(jnp.sum(e * e, axis=-1, keepdims=True) * (1.0 / D), axis=0, keepdims=True)
        dy = e * (1.0 / D)
        dxh = dy * gv
        dx = r * (dxh - xh * jnp.mean(dxh * xh, axis=-1, keepdims=True))
        dx_ref[...] = dx
        dxb_ref[...] = dx.astype(BF16)
        dgp = jnp.sum(dy * xh, axis=0, keepdims=True)

        @pl.when(i == 0)
        def _():
            l_ref[...] = jnp.broadcast_to(part, l_ref.shape)
            dg_ref[...] = dgp

        @pl.when(i > 0)
        def _():
            l_ref[...] += jnp.broadcast_to(part, l_ref.shape)
            dg_ref[...] += dgp

    row = ((tr, D), lambda i: (i, 0))
    return _call(name, body, (S // tr,),
                 [(x,) + row, (tgt,) + row, (g, (1, D), lambda i: (0, 0))],
                 [((1, LANES), F32, (1, LANES), lambda i: (0, 0)), ((S, D), F32) + row, ((S, D), BF16) + row,
                  ((1, D), F32, (1, D), lambda i: (0, 0))], sem=("arbitrary",))


FFN_TF = 4 * FFN_SHARD


def _ffn_pick(G, which):
    if isinstance(G, tuple):
        return (G[0], which) if which < 2 else (G[1], 0)
    return G, which


def _ffn_w_spec(G, which, imap):
    arr, blk = _ffn_pick(G, which)
    return (arr, (4, FFN_SHARD, arr.shape[2]), lambda *idx: (imap(*idx), blk, 0))


def _ffn_whole_w_spec(G, which):
    arr, blk = _ffn_pick(G, which)
    return (arr, (N_DEV, FFN_SHARD, arr.shape[2]), lambda *idx: (0, blk, 0))


def ffn_up(n, G, name):
    S, D = n.shape
    F = N_DEV * FFN_SHARD
    tm = 1024

    def body(n_ref, w1_ref, w3_ref, abh_ref):
        nv = n_ref[...]
        a = _dot(nv, w1_ref[...].reshape(FFN_TF, D), 1, 1).astype(BF16)
        b = _dot(nv, w3_ref[...].reshape(FFN_TF, D), 1, 1).astype(BF16)
        abh_ref[0] = a
        abh_ref[1] = b
        av, bv = a.astype(F32), b.astype(F32)
        abh_ref[2] = (av * _sigmoid(av) * bv).astype(BF16)

    return _call(name, body, (F // FFN_TF, S // tm),
                 [(n, (tm, D), lambda j, i: (i, 0)),
                  _ffn_w_spec(G, 0, lambda j, i: j), _ffn_w_spec(G, 1, lambda j, i: j)],
                 [((3, S, F), BF16, (3, tm, FFN_TF), lambda j, i: (0, i, j))],
                 sem=("parallel", "parallel"))[0]


def ffn_down(abh, G, x, name):
    _, S, F = abh.shape
    D = x.shape[1]
    tm = 512

    def body(h_ref, w2_ref, x_ref, o_ref):
        o_ref[...] = x_ref[...] + 0.5 * _dot(h_ref[...], w2_ref[...].reshape(F, D))

    return _call(name, body, (S // tm,),
                 [(abh, (None, tm, F), lambda i: (2, i, 0)), _ffn_whole_w_spec(G, 2),
                  (x, (tm, D), lambda i: (i, 0))],
                 [((S, D), F32, (tm, D), lambda i: (i, 0))], sem=("parallel",))[0]


def ffn_bwd_weights(dxo, abh, n, G, name):
    _, S, F = abh.shape
    D = dxo.shape[1]
    tm = 512
    nf = F // FFN_TF

    def down_body(d_ref, w2_ref, ab_ref, o_ref):
        dh = 0.5 * _dot(d_ref[...].astype(BF16), w2_ref[...].reshape(FFN_TF, D), 1, 1)
        av, bv = ab_ref[0].astype(F32), ab_ref[1].astype(F32)
        sig = _sigmoid(av)
        o_ref[0] = (dh * bv * (sig * (1.0 + av * (1.0 - sig)))).astype(BF16)
        o_ref[1] = (dh * (av * sig)).astype(BF16)

    dab = _call(name + "_down_bwd", down_body, (nf, S // tm),
                [(dxo, (tm, D), lambda j, i: (i, 0)), _ffn_w_spec(G, 2, lambda j, i: j),
                 (abh, (2, tm, FFN_TF), lambda j, i: (0, i, j))],
                [((2, S, F), BF16, (2, tm, FFN_TF), lambda j, i: (0, i, j))],
                sem=("parallel", "parallel"))[0]

    tk = WGRAD_TK
    nk = S // tk
    gshape = (N_DEV, 3 * FFN_SHARD, D)

    def dw2_body(h_ref, d_ref, o_ref, acc_ref):
        k = pl.program_id(1)
        p = _dot(h_ref[...], d_ref[...].astype(BF16), 0, 0)

        @pl.when(k == 0)
        def _():
            acc_ref[...] = p

        @pl.when(k > 0)
        def _():
            acc_ref[...] += p

        @pl.when(k == nk - 1)
        def _():
            o_ref[...] = (0.5 * acc_ref[...]).astype(BF16).reshape(4, FFN_SHARD, D)

    gw = _call(name + "_dw2", dw2_body, (nf, nk),
               [(abh, (None, tk, FFN_TF), lambda j, k: (2, k, j)), (dxo, (tk, D), lambda j, k: (k, 0))],
               [(gshape, BF16, (4, FFN_SHARD, D), lambda j, k: (j, 2, 0))],
               scratch=[pltpu.VMEM((FFN_TF, D), F32)], sem=("parallel", "arbitrary"))[0]

    def dw13_body(gw_ref, dab_ref, n_ref, o_ref):
        o_ref[...] = _dot(dab_ref[...], n_ref[...], 0, 0).astype(BF16).reshape(4, FFN_SHARD, D)

    gw = pl.pallas_call(
        dw13_body,
        out_shape=jax.ShapeDtypeStruct(gshape, BF16),
        grid=(2, nf),
        in_specs=[pl.BlockSpec(memory_space=pl.ANY),
                  pl.BlockSpec((None, S, FFN_TF), lambda w, j: (w, 0, j)),
                  pl.BlockSpec((S, D), lambda w, j: (0, 0))],
        out_specs=pl.BlockSpec((4, FFN_SHARD, D), lambda w, j: (j, w, 0)),
        input_output_aliases={0: 0},
        name=name + "_dw13",
        compiler_params=pltpu.CompilerParams(dimension_semantics=("parallel", "parallel"),
                                             vmem_limit_bytes=VMEM_LIMIT),
    )(gw, dab, n)
    return dab, gw


def ffn_bwd_input(dab, G, x_in, g, dxo, name, as_operand=True):
    _, S, F = dab.shape
    D = x_in.shape[1]
    tm = 256

    def dn_body(dab_ref, w1_ref, w3_ref, x_ref, d_ref, g_ref, dx_ref, *rest):
        dg_ref = rest[-1]
        i = pl.program_id(0)
        dn = _dot(dab_ref[0], w1_ref[...].reshape(F, D)) + _dot(dab_ref[1], w3_ref[...].reshape(F, D))
        dx, dgt = _rms_bwd_tile(dn, x_ref[...], g_ref[...])
        dx = d_ref[...] + dx
        dx_ref[...] = dx
        if as_operand:
            rest[0][...] = dx.astype(BF16)
        dgp = jnp.sum(dgt, axis=0, keepdims=True)

        @pl.when(i == 0)
        def _():
            dg_ref[...] = dgp

        @pl.when(i > 0)
        def _():
            dg_ref[...] += dgp

    tile = ((tm, D), lambda i: (i, 0))
    return _call(name + "_dn", dn_body, (S // tm,),
                 [(dab, (2, tm, F), lambda i: (0, i, 0)),
                  _ffn_whole_w_spec(G, 0), _ffn_whole_w_spec(G, 1),
                  (x_in,) + tile, (dxo,) + tile, (g, (1, D), lambda i: (0, 0))],
                 [((S, D), F32) + tile] + ([((S, D), BF16) + tile] if as_operand else [])
                 + [((1, D), F32, (1, D), lambda i: (0, 0))],
                 sem=("arbitrary",))


PROJ_TN = 512
DH_SHARDS_PER_STEP = 4


def in_proj(h, Gm, first_tile, n_tiles, dtype, name, tile_stride=1):
    S, D = h.shape
    tile = lambda j: first_tile + tile_stride * j

    def body(h_ref, w_ref, o_ref):
        o_ref[...] = _dot(h_ref[...], w_ref[...]).astype(dtype)

    return _call(name, body, (n_tiles,),
                 [(h, (S, D), lambda j: (0, 0)),
                  (Gm, (None, D, PROJ_TN), lambda j: (tile(j) // 2, 0, tile(j) % 2))],
                 [((S, n_tiles * PROJ_TN), dtype, (S, PROJ_TN), lambda j: (0, j))],
                 sem=("parallel",))[0]


def _dproj_pieces(dqkv, dq_b, dkv_b, dgate):
    pieces = [(dqkv[g], [(3 * which + g, (which, 0)) for which in range(3)]) for g in range(3)]
    pieces.append((dq_b, [(9, (None, 0)), (10, (None, 1))]))
    pieces.append((dkv_b, [(11, (None, 0))]))
    pieces.append((dgate, [(12 + 2 * a + b, (a, b)) for a in range(2) for b in range(2)]))
    return pieces


def in_proj_bwd_dw(pieces, h, gm_grads, name):
    S, D = h.shape

    for n_piece, (arr, tiles) in enumerate(pieces):
        w_tile = [t for t, _ in tiles]
        lead = [ix[0] for _, ix in tiles]
        colb = [ix[1] for _, ix in tiles]

        def pick(table, j):
            out = table[-1]
            for k in range(len(table) - 2, -1, -1):
                out = jnp.where(j == k, table[k], out)
            return out

        def dw_body(gm_ref, h_ref, d_ref, o_ref):
            o_ref[...] = _dot(h_ref[...], d_ref[...], 0, 0).astype(BF16)

        if arr.ndim == 3:
            d_spec = pl.BlockSpec((None, S, PROJ_TN), lambda j, lead=lead, colb=colb: (pick(lead, j), 0, pick(colb, j)))
        else:
            d_spec = pl.BlockSpec((S, PROJ_TN), lambda j, colb=colb: (0, pick(colb, j)))
        gm_grads = pl.pallas_call(
            dw_body,
            out_shape=jax.ShapeDtypeStruct(gm_grads.shape, BF16),
            grid=(len(tiles),),
            in_specs=[pl.BlockSpec(memory_space=pl.ANY), pl.BlockSpec((S, D), lambda j: (0, 0)), d_spec],
            out_specs=pl.BlockSpec((None, D, PROJ_TN),
                                   lambda j, w_tile=w_tile: (pick(w_tile, j) // 2, 0, pick(w_tile, j) % 2)),
            input_output_aliases={0: 0},
            name="%s_dw%d" % (name, n_piece),
            compiler_params=pltpu.CompilerParams(dimension_semantics=("parallel",), vmem_limit_bytes=VMEM_LIMIT),
        )(gm_grads, h, arr)
    return gm_grads


def in_proj_bwd_dh(pieces, Gm, x_in, g, dres, name):
    S, D = x_in.shape
    tm = 256
    C = Gm.shape[2]
    n_sh = N_DEV
    n_p = len(pieces)

    def dh_body(*refs):
        d_refs = refs[:n_p]
        w_ref, x_ref, r_ref, g_ref, dx_ref, dxb_ref, dg_ref = refs[n_p:]
        i = pl.program_id(0)
        p = None
        for d_ref, (arr, tiles) in zip(d_refs, pieces):
            for t, (lead, colb) in tiles:
                cols = slice(colb * PROJ_TN, (colb + 1) * PROJ_TN)
                d = d_ref[:, cols] if lead is None else d_ref[lead, :, cols]
                wcol = (t % 2) * PROJ_TN
                term = _dot(d, w_ref[t // 2, :, wcol:wcol + PROJ_TN], 1, 1)
                p = term if p is None else p + term
        dx, dgt = _rms_bwd_tile(p, x_ref[...], g_ref[...])
        dx = r_ref[...] + dx
        dx_ref[...] = dx
        dxb_ref[...] = dx.astype(BF16)
        dgp = jnp.sum(dgt, axis=0, keepdims=True)

        @pl.when(i == 0)
        def _():
            dg_ref[...] = dgp

        @pl.when(i > 0)
        def _():
            dg_ref[...] += dgp

    tile = ((tm, D), lambda i: (i, 0))

    def rows_of(arr):
        if arr.ndim == 3:
            return (arr, (arr.shape[0], tm, arr.shape[2]), lambda i: (0, i, 0))
        return (arr, (tm, arr.shape[1]), lambda i: (i, 0))

    return _call(name + "_dh", dh_body, (S // tm,),
                 [rows_of(arr) for arr, _ in pieces]
                 + [(Gm, (n_sh, D, C), lambda i: (0, 0, 0), pl.Buffered(1)),
                    (x_in,) + tile, (dres,) + tile, (g, (1, D), lambda i: (0, 0))],
                 [((S, D), F32) + tile, ((S, D), BF16) + tile, ((1, D), F32, (1, D), lambda i: (0, 0))],
                 sem=("arbitrary",))


def _t5_bucket(rel):
    n = N_BUCKETS // 2
    max_exact = n // 2
    ret = jnp.where(rel > 0, n, 0)
    a = jnp.abs(rel)
    af = jnp.maximum(a, 1).astype(F32)
    large = max_exact + (jnp.log(af / max_exact) / math.log(MAX_DISTANCE / max_exact)
                         * (n - max_exact)).astype(jnp.int32)
    large = jnp.minimum(large, n - 1)
    return ret + jnp.where(a < max_exact, a, large)


def _bucket_tables():
    qi = jnp.arange(A_TQ, dtype=jnp.int32)[:, None]
    kj = jnp.arange(A_WIN, dtype=jnp.int32)[None, :]
    rel = kj - HALF_WINDOW - qi
    return jnp.stack([_t5_bucket(rel * d) for d in DILATIONS], axis=0)


def bias_build(rel_bias, buckets):
    def body(tab_ref, bk_ref, o_ref):
        col = pl.program_id(0) * HEADS_PER_GROUP_A + pl.program_id(1)
        bk = bk_ref[...]
        acc = jnp.zeros(bk.shape, F32)
        for b in range(N_BUCKETS):
            acc = jnp.where(bk == b, tab_ref[b, col], acc)
        qi = lax.broadcasted_iota(jnp.int32, bk.shape, 0)
        kj = lax.broadcasted_iota(jnp.int32, bk.shape, 1)
        band = jnp.where(jnp.abs(kj - HALF_WINDOW - qi) <= HALF_WINDOW, acc, NEG_INF)
        o_ref[0] = jnp.where(kj >= HALF_WINDOW, band, NEG_INF)
        o_ref[1] = band
        o_ref[2] = jnp.where(kj < A_TQ + HALF_WINDOW, band, NEG_INF)

    out = pl.pallas_call(
        body,
        out_shape=jax.ShapeDtypeStruct((3, HEADS_PER_GROUP_A // 2, 3, 2, A_TQ, A_WIN), F32),
        grid=(3, HEADS_PER_GROUP_A),
        in_specs=[pl.BlockSpec(memory_space=pltpu.SMEM),
                  pl.BlockSpec((None, A_TQ, A_WIN), lambda g, h: (g, 0, 0))],
        out_specs=pl.BlockSpec((None, None, 3, None, A_TQ, A_WIN), lambda g, h: (g, h // 2, 0, h % 2, 0, 0)),
        name="a_bias_build",
        compiler_params=pltpu.CompilerParams(dimension_semantics=("parallel", "parallel")),
    )(rel_bias, buckets)
    return out.reshape(3, HEADS_PER_GROUP_A // 2, 3, 2 * A_TQ, A_WIN)


def bias_bwd(dbias, buckets):
    def body(d_ref, bk_ref, o_ref):
        bk = bk_ref[...]
        dv = d_ref[...]
        for b in range(N_BUCKETS):
            part = jnp.sum(jnp.where(bk == b, dv, 0.0), axis=1, keepdims=True)
            o_ref[b:b + 1, :] = jnp.broadcast_to(jnp.sum(part, axis=0, keepdims=True), (1, LANES))

    out = pl.pallas_call(
        body,
        out_shape=jax.ShapeDtypeStruct((3, HEADS_PER_GROUP_A, N_BUCKETS, LANES), F32),
        grid=(3, HEADS_PER_GROUP_A),
        in_specs=[pl.BlockSpec((None, None, A_TQ, A_WIN), lambda g, h: (g, h, 0, 0)),
                  pl.BlockSpec((None, A_TQ, A_WIN), lambda g, h: (g, 0, 0))],
        out_specs=pl.BlockSpec((None, None, N_BUCKETS, LANES), lambda g, h: (g, h, 0, 0)),
        name="a_bias_bwd",
        compiler_params=pltpu.CompilerParams(dimension_semantics=("parallel", "parallel")),
    )(dbias, buckets)
    return out[:, :, :, 0].transpose(2, 0, 1).reshape(N_BUCKETS, 3 * HEADS_PER_GROUP_A)


def _a_fill_padded(pad_ref, src_ref, n, pad):
    zeros = jnp.zeros((pad, LANES), pad_ref.dtype)
    pad_ref[0:pad, :] = zeros
    pad_ref[pad + n:2 * pad + n, :] = zeros
    pad_ref[pad:pad + n, :] = src_ref[...].astype(pad_ref.dtype)


def _a_stack_heads(x, lane):
    zero = jnp.zeros_like(x)
    return jnp.concatenate([jnp.where(lane < HEAD_DIM_A, x, zero), jnp.where(lane >= HEAD_DIM_A, x, zero)], axis=0)


def _a_bias_variant(qb, nqb):
    return jnp.where(qb == 0, 0, jnp.where(qb == nqb - 1, 2, 1))


def a_fwd(proj_g, bias_g, g, name):
    S = proj_g.shape[0]
    d = DILATIONS[g]
    L = S // d
    nqb = L // A_TQ
    pad = HALF_WINDOW * d

    def body(q_ref, k_ref, v_ref, b_ref, o_ref, l_ref, qf, kpad, vpad):
        qf[...] = q_ref[...].astype(F32) * A_SCALE
        _a_fill_padded(kpad, k_ref, S, pad)
        _a_fill_padded(vpad, v_ref, S, pad)
        lane = lax.broadcasted_iota(jnp.int32, (A_TQ, LANES), 1)

        def block(t, carry):
            qb, r = t // d, t % d
            start = qb * (A_TQ * d) + r
            kw = kpad[pl.ds(start, A_WIN, stride=d), :].astype(BF16)
            vw = vpad[pl.ds(start, A_WIN, stride=d), :].astype(BF16)
            q = qf[pl.ds(start, A_TQ, stride=d), :].astype(BF16)
            q2 = _a_stack_heads(q, lane)
            s = _dot(q2, kw, 1, 1) + b_ref[_a_bias_variant(qb, nqb)]
            m = jnp.max(s, axis=-1, keepdims=True)
            e = jnp.exp(s - m)
            l = jnp.sum(e, axis=-1, keepdims=True)
            o2 = _dot(e.astype(BF16), vw) / l
            lse2 = m + jnp.log(l)
            o_ref[pl.ds(start, A_TQ, stride=d), :] = jnp.where(lane < HEAD_DIM_A, o2[0:A_TQ], o2[A_TQ:])
            l_ref[pl.ds(start, A_TQ, stride=d), :] = jnp.where(lane < HEAD_DIM_A, lse2[0:A_TQ], lse2[A_TQ:])
            return carry

        lax.fori_loop(0, nqb * d, block, 0, unroll=A_UNROLL)

    out_spec = ((S, GROUP_WIDTH_A), F32, (S, LANES), lambda hp: (0, hp))
    return _call(name, body, (4,),
                 [(proj_g, (S, LANES), lambda hp: (0, hp)),
                  (proj_g, (S, LANES), lambda hp: (0, 4 + hp)),
                  (proj_g, (S, LANES), lambda hp: (0, 8 + hp)),
                  (bias_g, (None, 3, 2 * A_TQ, A_WIN), lambda hp: (hp, 0, 0, 0))],
                 [out_spec, out_spec],
                 scratch=[pltpu.VMEM((S, LANES), F32)] + [pltpu.VMEM((S + 2 * pad, LANES), F32)] * 2,
                 sem=("parallel",))


def a_combine(outs, lses, name):
    S, W = outs[0].shape
    tr = 512

    def body(o0, o1, o2, l0, l1, l2, oa_ref, lt_ref):
        a, b, c = l0[...], l1[...], l2[...]
        m = jnp.maximum(jnp.maximum(a, b), c)
        ea, eb, ec = jnp.exp(a - m), jnp.exp(b - m), jnp.exp(c - m)
        z = ea + eb + ec
        oa_ref[...] = ((ea * o0[...] + eb * o1[...] + ec * o2[...]) / z).astype(BF16)
        lt_ref[...] = m + jnp.log(z)

    spec = ((tr, W), lambda i: (i, 0))
    return _call(name, body, (S // tr,), [(a,) + spec for a in (*outs, *lses)],
                 [((S, W), BF16) + spec, ((S, W), F32) + spec], sem=("parallel",))


def a_bwd(proj_g, bias_g, do_a, o_a, lse_tot, g, name):
    S = proj_g.shape[0]
    d = DILATIONS[g]
    L = S // d
    nqb = L // A_TQ
    pad = HALF_WINDOW * d

    def body(q_ref, k_ref, v_ref, b_ref, do_ref, o_ref, l_ref, dqkv_ref, db_ref,
             qf, of, dqf, kpad, vpad, dkacc, dvacc):
        qf[...] = q_ref[...].astype(F32) * A_SCALE
        of[...] = o_ref[...].astype(F32)
        _a_fill_padded(kpad, k_ref, S, pad)
        _a_fill_padded(vpad, v_ref, S, pad)
        dkacc[...] = jnp.zeros(dkacc.shape, F32)
        dvacc[...] = jnp.zeros(dvacc.shape, F32)
        db_ref[...] = jnp.zeros(db_ref.shape, F32)
        lane = lax.broadcasted_iota(jnp.int32, (A_TQ, LANES), 1)

        def block(t, carry):
            qb, r = t // d, t % d
            start = qb * (A_TQ * d) + r
            rows = pl.ds(start, A_TQ, stride=d)
            win = pl.ds(start, A_WIN, stride=d)
            kw = kpad[win, :].astype(BF16)
            vw = vpad[win, :].astype(BF16)
            q = qf[rows, :].astype(BF16)
            do = do_ref[rows, :]
            ov = of[rows, :]
            lt = l_ref[rows, :]
            q2 = _a_stack_heads(q, lane)
            do2 = _a_stack_heads(do, lane)
            lt2 = jnp.concatenate([lt[:, 0:1], lt[:, HEAD_DIM_A:HEAD_DIM_A + 1]], axis=0)
            s = _dot(q2, kw, 1, 1) + b_ref[_a_bias_variant(qb, nqb)]
            p = jnp.exp(s - lt2)
            t = jnp.sum(do2 * jnp.concatenate([ov, ov], axis=0), axis=-1, keepdims=True)
            dob2 = do2.astype(BF16)
            ds = p * (_dot(dob2, vw, 1, 1) - t)
            db_ref[...] += ds
            dsb = ds.astype(BF16)
            dq2 = _dot(dsb, kw)
            dqf[rows, :] = jnp.where(lane < HEAD_DIM_A, dq2[0:A_TQ], dq2[A_TQ:]) * A_SCALE
            dkacc[win, :] += _dot(dsb, q2, 0, 0)
            dvacc[win, :] += _dot(p.astype(BF16), dob2, 0, 0)
            return carry

        lax.fori_loop(0, nqb * d, block, 0, unroll=A_UNROLL)
        dqkv_ref[0] = dqf[...].astype(BF16)
        dqkv_ref[1] = dkacc[pad:pad + S, :].astype(BF16)
        dqkv_ref[2] = dvacc[pad:pad + S, :].astype(BF16)

    slab = ((S, LANES), lambda hp: (0, hp))
    padded = pltpu.VMEM((S + 2 * pad, LANES), F32)
    return _call(
        name, body, (4,),
        [(proj_g, (S, LANES), lambda hp: (0, hp)),
         (proj_g, (S, LANES), lambda hp: (0, 4 + hp)),
         (proj_g, (S, LANES), lambda hp: (0, 8 + hp)),
         (bias_g, (None, 3, 2 * A_TQ, A_WIN), lambda hp: (hp, 0, 0, 0)),
         (do_a,) + slab, (o_a,) + slab, (lse_tot,) + slab],
        [((3, S, GROUP_WIDTH_A), BF16, (3, S, LANES), lambda hp: (0, 0, hp)),
         ((4, 2 * A_TQ, A_WIN), F32, (None, 2 * A_TQ, A_WIN), lambda hp: (hp, 0, 0))],
        scratch=[pltpu.VMEM((S, LANES), F32)] * 3 + [padded] * 4,
        sem=("parallel",))


def _rope_tables(S):
    rows = S // GRID_W
    row = jnp.repeat(jnp.arange(rows, dtype=F32), GRID_W)
    col = jnp.tile(jnp.arange(GRID_W, dtype=F32), rows)
    n_freq = HEAD_DIM_B // 4
    freq = ROPE_THETA ** (-jnp.arange(n_freq, dtype=F32) / n_freq)
    ang = jnp.concatenate([row[:, None] * freq, col[:, None] * freq], axis=-1)
    cos, sin = jnp.cos(ang), jnp.sin(ang)
    return jnp.repeat(cos, 2, axis=-1), jnp.stack([-sin, sin], axis=-1).reshape(S, HEAD_DIM_B)


def _swap_pairs(y):
    lane = lax.broadcasted_iota(jnp.int32, y.shape, 1)
    return jnp.where(lane % 2 == 0, pltpu.roll(y, LANES - 1, 1), pltpu.roll(y, 1, 1))


def qkv_prep(proj_b, gains, cos_t, sin_t, name):
    S = proj_b.shape[0]
    ts = 256
    n_rot = N_HEADS_B + N_KV_B
    nh = n_rot + N_KV_B
    W = nh * LANES

    def body(x_ref, g_ref, c_ref, s_ref, o_ref):
        cv, sv = c_ref[...], s_ref[...]
        for hb in range(nh):
            cols = slice(hb * LANES, (hb + 1) * LANES)
            xv = x_ref[:, cols]
            if hb < n_rot:
                r = lax.rsqrt(jnp.mean(xv * xv, axis=-1, keepdims=True) + EPS)
                yv = xv * r * g_ref[:, cols]
                o_ref[:, cols] = (yv * cv + _swap_pairs(yv) * sv).astype(BF16)
            else:
                o_ref[:, cols] = xv.astype(BF16)

    return _call(name, body, (S // ts,),
                 [(proj_b, (ts, W), lambda i: (i, 0)), (gains, (1, W), lambda i: (0, 0)),
                  (cos_t, (ts, LANES), lambda i: (i, 0)), (sin_t, (ts, LANES), lambda i: (i, 0))],
                 [((S, W), BF16, (ts, W), lambda i: (i, 0))],
                 sem=("parallel",))[0]


def qk_prep_bwd(dr, proj_b, col0, gain, cos_t, sin_t, name):
    S, W = dr.shape
    H = W // LANES
    ts = 256
    xb = (col0 * LANES) // W

    def body(d_ref, x_ref, g_ref, c_ref, s_ref, dx_ref, dg_ref):
        i = pl.program_id(0)
        cv, sv, gv = c_ref[...], s_ref[...], g_ref[...]
        dgp = jnp.zeros((1, LANES), F32)
        for hb in range(H):
            cols = slice(hb * LANES, (hb + 1) * LANES)
            dout = d_ref[:, cols]
            dy = dout * cv + _swap_pairs(dout * sv)
            dx, dgt = _rms_bwd_tile(dy, x_ref[:, cols], gv)
            dx_ref[:, cols] = dx.astype(BF16)
            dgp = dgp + jnp.sum(dgt, axis=0, keepdims=True)

        @pl.when(i == 0)
        def _():
            dg_ref[...] = dgp

        @pl.when(i > 0)
        def _():
            dg_ref[...] += dgp

    return _call(name, body, (S // ts,),
                 [(dr, (ts, W), lambda i: (i, 0)), (proj_b, (ts, W), lambda i: (i, xb)),
                  (gain, (1, LANES), lambda i: (0, 0)),
                  (cos_t, (ts, LANES), lambda i: (i, 0)), (sin_t, (ts, LANES), lambda i: (i, 0))],
                 [((S, W), BF16, (ts, W), lambda i: (i, 0)),
                  ((1, LANES), F32, (1, LANES), lambda i: (0, 0))],
                 sem=("arbitrary",))


def _row_sums(x):
    hi = x.astype(BF16)
    lo = (x - hi.astype(F32)).astype(BF16)
    ones = jnp.ones((8, LANES), BF16)
    return (_dot(ones, hi, 1, 1) + _dot(ones, lo, 1, 1))[0:1, :]


def flash_fwd(qkv, name):
    S = qkv.shape[0]
    tq = B_TQ_FWD
    scale = HEAD_DIM_B ** -0.5

    hps = B_HEADS_PER_STEP

    def body(q_ref, k_ref, v_ref, o_ref, l_ref):
        k, v = k_ref[...], v_ref[...]
        for j in range(hps):
            cols = slice(j * LANES, (j + 1) * LANES)
            s = _dot(q_ref[:, cols], k, 1, 1)
            m = jnp.max(s, axis=-1, keepdims=True)
            e = jnp.exp2((s - m) * (scale * LOG2E))
            l = jnp.sum(e, axis=-1, keepdims=True)
            o_ref[:, cols] = (_dot(e.astype(BF16), v) / l).astype(BF16)
            lse = jnp.broadcast_to(m * scale + jnp.log(l), (tq, LANES))
            l_ref[j] = _row_sums(lse) * (1.0 / LANES)

    per = GQA_GROUP_B // hps
    heads = lambda g, h, i: (i, g * per + h)
    return _call(name, body, (N_KV_B, per, S // tq),
                 [(qkv, (tq, hps * LANES), heads),
                  (qkv, (S, LANES), lambda g, h, i: (0, N_HEADS_B + g)),
                  (qkv, (S, LANES), lambda g, h, i: (0, N_HEADS_B + N_KV_B + g))],
                 [((S, N_HEADS_B * LANES), BF16, (tq, hps * LANES), heads),
                  ((N_HEADS_B, 1, S), F32, (hps, 1, tq), lambda g, h, i: (g * per + h, 0, i))],
                 sem=("parallel", "parallel", "parallel"))


def flash_bwd(qkv, k_t, do_b, o_b, lse, name):
    S = qkv.shape[0]
    tq = B_TQ_BWD
    nq = S // tq
    scale = HEAD_DIM_B ** -0.5

    def body(q_ref, k_ref, v_ref, kt_ref, do_ref, o_ref, l_ref, dq_ref, dk_ref, dv_ref, dkacc, dvacc):
        h, i = pl.program_id(1), pl.program_id(2)

        @pl.when((h == 0) & (i == 0))
        def _():
            dkacc[...] = jnp.zeros(dkacc.shape, F32)
            dvacc[...] = jnp.zeros(dvacc.shape, F32)

        q = q_ref[...]
        do = do_ref[...]
        dob = do.astype(BF16)
        t = _row_sums(do * o_ref[...].astype(F32))
        pt = jnp.exp2(_dot(k_ref[...], q, 1, 1) * (scale * LOG2E) - l_ref[...] * LOG2E)
        dsb = (pt * (_dot(v_ref[...], dob, 1, 1) - t)).astype(BF16)
        dvacc[...] += _dot(pt.astype(BF16), dob)
        dkacc[...] += _dot(dsb, q)
        dq_ref[...] = _dot(kt_ref[...], dsb).T * scale

        @pl.when((h == GQA_GROUP_B - 1) & (i == nq - 1))
        def _():
            dk_ref[...] = dkacc[...] * scale
            dv_ref[...] = dvacc[...].astype(BF16)

    head = lambda g, h, i: (i, g * GQA_GROUP_B + h)
    return _call(name, body, (N_KV_B, GQA_GROUP_B, nq),
                 [(qkv, (tq, LANES), head),
                  (qkv, (S, LANES), lambda g, h, i: (0, N_HEADS_B + g)),
                  (qkv, (S, LANES), lambda g, h, i: (0, N_HEADS_B + N_KV_B + g)),
                  (k_t, (LANES, S), lambda g, h, i: (g, 0)),
                  (do_b, (tq, LANES), head), (o_b, (tq, LANES), head),
                  (lse, (None, 1, tq), lambda g, h, i: (g * GQA_GROUP_B + h, 0, i))],
                 [((S, N_HEADS_B * LANES), F32, (tq, LANES), head),
                  ((S, N_KV_B * LANES), F32, (S, LANES), lambda g, h, i: (0, g)),
                  ((S, N_KV_B * LANES), BF16, (S, LANES), lambda g, h, i: (0, g))],
                 scratch=[pltpu.VMEM((S, LANES), F32)] * 2,
                 sem=("parallel", "arbitrary", "arbitrary"))


MERGE_TN = 512


def _mix_rows_spec(Gm, row0, n_slots, slot_map, cols=None, col_map=None):
    C = Gm.shape[2] if cols is None else cols
    cm = (lambda *idx: 0) if col_map is None else col_map
    return (Gm, (n_slots, LANES, C), lambda *idx: (slot_map(*idx), row0 // LANES, cm(*idx)))


def merge_fwd(o_a, o_b, w_a, Gm, proj_b, b_gate, name):
    S = o_a.shape[0]
    D = w_a.shape[1]
    tm, tn = 512, MERGE_TN
    ga0, gb0 = PB_GATE_A // tn, PB_GATE_B // tn

    def body(oa_ref, ob_ref, wa_ref, wb_ref, pa_ref, pb_ref, ba_ref, bb_ref, m_ref, ya_ref, yb_ref):
        ya = _dot(oa_ref[...], wa_ref[...])
        yb = _dot(ob_ref[...], wb_ref[...].reshape(N_DEV * LANES, tn))
        ga = _sigmoid(pa_ref[...] + ba_ref[...])
        gb = _sigmoid(pb_ref[...] + bb_ref[...])
        m_ref[...] = (ga * ya + gb * yb).astype(BF16)
        ya_ref[...] = ya.astype(BF16)
        yb_ref[...] = yb.astype(BF16)

    out = ((S, D), BF16, (tm, tn), lambda j, i: (i, j))
    return _call(name, body, (D // tn, S // tm),
                 [(o_a, (tm, o_a.shape[1]), lambda j, i: (i, 0)), (o_b, (tm, o_b.shape[1]), lambda j, i: (i, 0)),
                  (w_a, (w_a.shape[0], tn), lambda j, i: (0, j)),
                  _mix_rows_spec(Gm, MIX_WB, N_DEV, lambda j, i: 0, cols=tn, col_map=lambda j, i: j),
                  (proj_b, (tm, tn), lambda j, i: (i, ga0 + j)), (proj_b, (tm, tn), lambda j, i: (i, gb0 + j)),
                  (b_gate, (1, tn), lambda j, i: (0, j)), (b_gate, (1, tn), lambda j, i: (0, D // tn + j))],
                 [out, out, out], sem=("parallel", "parallel"))


def out_proj(merged, Gm, x, name):
    S, D = x.shape
    tm, tn = 512, MERGE_TN

    def body(m_ref, w_ref, x_ref, o_ref):
        o_ref[...] = x_ref[...] + _dot(m_ref[...], w_ref[...].reshape(N_DEV * LANES, tn))

    return _call(name, body, (D // tn, S // tm),
                 [(merged, (tm, D), lambda j, i: (i, 0)),
                  _mix_rows_spec(Gm, MIX_WOUT, N_DEV, lambda j, i: 0, cols=tn, col_map=lambda j, i: j),
                  (x, (tm, tn), lambda j, i: (i, j))],
                 [((S, D), F32, (tm, tn), lambda j, i: (i, j))], sem=("parallel", "parallel"))[0]


def merge_bwd(dx2, Gm, ya, yb, proj_b, b_gate, name):
    S, D = dx2.shape
    tm, tn = 512, MERGE_TN
    nn = D // tn
    ga0, gb0 = PB_GATE_A // tn, PB_GATE_B // tn

    def body(d_ref, w_ref, ya_ref, yb_ref, pa_ref, pb_ref, ba_ref, bb_ref, dya_ref, dyb_ref, dg_ref, dbg_ref):
        i = pl.program_id(1)
        dm = _dot(d_ref[...].astype(BF16), w_ref[...].reshape(tn, D), 1, 1)
        ga = _sigmoid(pa_ref[...] + ba_ref[...])
        gb = _sigmoid(pb_ref[...] + bb_ref[...])
        dya_ref[...] = (dm * ga).astype(BF16)
        dyb_ref[...] = (dm * gb).astype(BF16)
        dpa = dm * ya_ref[...].astype(F32) * ga * (1.0 - ga)
        dpb = dm * yb_ref[...].astype(F32) * gb * (1.0 - gb)
        dg_ref[0] = dpa.astype(BF16)
        dg_ref[1] = dpb.astype(BF16)
        sa = jnp.sum(dpa, axis=0, keepdims=True)
        sb = jnp.sum(dpb, axis=0, keepdims=True)

        @pl.when(i == 0)
        def _():
            dbg_ref[0] = sa
            dbg_ref[1] = sb

        @pl.when(i > 0)
        def _():
            dbg_ref[0] += sa
            dbg_ref[1] += sb

    tile = ((tm, tn), lambda j, i: (i, j))
    dya, dyb, dgate, dbg = _call(
        name, body, (nn, S // tm),
        [(dx2, (tm, D), lambda j, i: (i, 0)),
         _mix_rows_spec(Gm, MIX_WOUT, tn // LANES, lambda j, i: j),
         (ya,) + tile, (yb,) + tile,
         (proj_b, (tm, tn), lambda j, i: (i, ga0 + j)), (proj_b, (tm, tn), lambda j, i: (i, gb0 + j)),
         (b_gate, (1, tn), lambda j, i: (0, j)), (b_gate, (1, tn), lambda j, i: (0, nn + j))],
        [((S, D), BF16) + tile, ((S, D), BF16) + tile,
         ((2, S, D), BF16, (2, tm, tn), lambda j, i: (0, i, j)),
         ((2, 1, D), F32, (2, 1, tn), lambda j, i: (0, 0, j))],
        sem=("parallel", "arbitrary"))
    return dya, dyb, dgate, dbg


def matmul_nt(a, b_spec_fn, N, name, tn=512):
    S, K = a.shape
    tm = 512

    def body(a_ref, b_ref, o_ref):
        b = b_ref[...]
        o_ref[...] = _dot(a_ref[...], b.reshape(-1, b.shape[-1]), 1, 1)

    return _call(name, body, (N // tn, S // tm),
                 [(a, (tm, K), lambda j, i: (i, 0)), b_spec_fn(lambda j, i: j)],
                 [((S, N), F32, (tm, tn), lambda j, i: (i, j))], sem=("parallel", "parallel"))[0]


def weight_grad_rows(a, b, grads, row0, name):
    S, M = a.shape
    N = b.shape[1]
    tmm = 512
    tk = WGRAD_TK
    nk = S // tk

    def body(g_ref, a_ref, b_ref, o_ref, acc_ref):
        k = pl.program_id(1)
        p = _dot(a_ref[...], b_ref[...].astype(BF16), 0, 0)

        @pl.when(k == 0)
        def _():
            acc_ref[...] = p

        @pl.when(k > 0)
        def _():
            acc_ref[...] += p

        @pl.when(k == nk - 1)
        def _():
            o_ref[...] = acc_ref[...].astype(BF16).reshape(tmm // LANES, LANES, N)

    return pl.pallas_call(
        body,
        out_shape=jax.ShapeDtypeStruct(grads.shape, BF16),
        grid=(M // tmm, nk),
        in_specs=[pl.BlockSpec(memory_space=pl.ANY),
                  pl.BlockSpec((tk, tmm), lambda j, k: (k, j)),
                  pl.BlockSpec((tk, N), lambda j, k: (k, 0))],
        out_specs=pl.BlockSpec((tmm // LANES, LANES, N), lambda j, k: (j, row0 // LANES, 0)),
        scratch_shapes=[pltpu.VMEM((tmm, N), F32)],
        input_output_aliases={0: 0},
        name=name,
        compiler_params=pltpu.CompilerParams(dimension_semantics=("parallel", "arbitrary"),
                                             vmem_limit_bytes=VMEM_LIMIT),
    )(grads, a, b)


def weight_grad_plain(a, b, name):
    S, M = a.shape
    N = b.shape[1]
    tk = WGRAD_TK
    nk = S // tk

    def body(a_ref, b_ref, o_ref, acc_ref):
        k = pl.program_id(0)
        p = _dot(a_ref[...], b_ref[...], 0, 0)

        @pl.when(k == 0)
        def _():
            acc_ref[...] = p

        @pl.when(k > 0)
        def _():
            acc_ref[...] += p

        @pl.when(k == nk - 1)
        def _():
            o_ref[...] = acc_ref[...].astype(BF16)

    return _call(name, body, (nk,),
                 [(a, (tk, M), lambda k: (k, 0)), (b, (tk, N), lambda k: (k, 0))],
                 [((M, N), BF16, (M, N), lambda k: (0, 0))],
                 scratch=[pltpu.VMEM((M, N), F32)], sem=("arbitrary",))[0]


def local_step(x, tgt, p, get_g1_up, get_g1_down, get_gm, get_g2, emit, start_token):
    S, D = x.shape
    after = lambda t: t[0:1, 0:1]
    buckets = _bucket_tables()
    cos_t, sin_t = _rope_tables(S)
    gains = jnp.concatenate([jnp.tile(p["q_norm"], (1, N_HEADS_B)), jnp.tile(p["k_norm"], (1, N_KV_B)),
                             jnp.ones((1, N_KV_B * LANES), F32)], axis=1)

    n1 = rms_fwd(x, p["ffn1_norm"] + after(start_token), "ffn1_norm")
    bias = bias_build(p["rel_bias"] + after(start_token), buckets)
    g1_up = get_g1_up((n1, bias))
    ab1 = ffn_up(n1, (g1_up, None), "ffn1_up")
    G1 = (g1_up, get_g1_down(ab1))
    x1 = ffn_down(ab1, G1, x, "ffn1_down")

    Gm = get_gm(x1)
    w_a = Gm[:, MIX_WA:MIX_ROWS, :].reshape(N_DEV, GROUP_WIDTH_A, LANES).transpose(1, 0, 2).reshape(GROUP_WIDTH_A, D)
    hm = rms_fwd(x1, p["mix_norm"], "mix_norm")
    n_a = A_QKV_WIDTH // PROJ_TN
    proj_a = [in_proj(hm, Gm, g, 3, BF16, "in_proj_a%d" % g, tile_stride=3) for g in range(3)]
    proj_b = in_proj(hm, Gm, n_a, PB_WIDTH // PROJ_TN, F32, "in_proj_b")

    outs, lses = [], []
    for g in range(3):
        o, l = a_fwd(proj_a[g], bias[g], g, "a_fwd_%d" % g)
        outs.append(o)
        lses.append(l)
    o_a, lse_tot = a_combine(outs, lses, "a_combine")

    qkv = qkv_prep(proj_b, gains, cos_t, sin_t, "qkv_prep")
    k_t = qkv[:, N_HEADS_B * LANES:(N_HEADS_B + N_KV_B) * LANES].T
    o_b, lse_b = flash_fwd(qkv, "flash_fwd")

    merged, ya, yb = merge_fwd(o_a, o_b, w_a, Gm, proj_b, p["b_gate"], "merge_fwd")
    x2 = out_proj(merged, Gm, x1, "out_proj")

    G2 = get_g2(x2)
    n2 = rms_fwd(x2, p["ffn2_norm"], "ffn2_norm")
    ab2 = ffn_up(n2, G2, "ffn2_up")
    x3 = ffn_down(ab2, G2, x2, "ffn2_down")

    loss, dx3, dx3_b, d_final = final_loss(x3, tgt, p["final_norm"], "final_loss")

    dabh2, gw2 = ffn_bwd_weights(dx3_b, ab2, n2, G2, "ffn2_bwd")
    t2 = emit("ffn2", gw2)
    dx2, dx2_b, d_ffn2_norm = ffn_bwd_input(dabh2, G2, x2, p["ffn2_norm"] + after(t2), dx3, "ffn2_bwd")

    dya, dyb, dgate, dbg = merge_bwd(dx2_b, Gm, ya, yb, proj_b, p["b_gate"], "merge_bwd")
    gm_grads = jnp.zeros(Gm.shape, BF16)
    gm_grads = weight_grad_rows(merged, dx2_b, gm_grads, MIX_WOUT, "dw_out")
    gm_grads = weight_grad_rows(o_b, dyb, gm_grads, MIX_WB, "dw_branch_b")
    dw_a = weight_grad_plain(o_a, dya, "dw_branch_a")
    do_a = matmul_nt(dya, lambda jm: (w_a, (MERGE_TN, D), lambda j, i: (jm(j, i), 0)), GROUP_WIDTH_A, "do_a")
    do_b = matmul_nt(dyb, lambda jm: _mix_rows_spec(Gm, MIX_WB, MERGE_TN // LANES, jm), N_HEADS_B * LANES, "do_b")

    dq_r, dk_r, dv_b = flash_bwd(qkv, k_t, do_b, o_b, lse_b, "flash_bwd")
    dq_b, d_q_norm = qk_prep_bwd(dq_r, proj_b, 0, p["q_norm"], cos_t, sin_t, "q_prep_bwd")
    dk_b, d_k_norm = qk_prep_bwd(dk_r, proj_b, N_HEADS_B, p["k_norm"], cos_t, sin_t, "k_prep_bwd")

    dqkv, dbs = [], []
    for g in range(3):
        dg_, db = a_bwd(proj_a[g], bias[g], do_a, o_a, lse_tot, g, "a_bwd_%d" % g)
        dqkv.append(dg_)
        dbs.append(db)
    d_rel_bias = bias_bwd(jnp.stack(dbs, axis=0).reshape(3, HEADS_PER_GROUP_A, A_TQ, A_WIN), buckets)

    dproj = _dproj_pieces(dqkv, dq_b, jnp.concatenate([dk_b, dv_b], axis=1), dgate)
    gm_grads = in_proj_bwd_dw(dproj, hm, gm_grads, "in_proj_bwd")
    dw_a_sh = dw_a.reshape(GROUP_WIDTH_A, N_DEV, LANES).transpose(1, 0, 2).reshape(N_DEV, MIX_ROWS - MIX_WA, D)
    gm_grads = lax.dynamic_update_slice(gm_grads, dw_a_sh, (0, MIX_WA, 0))
    tm = emit("mix", gm_grads)
    dx1, dx1_b, d_mix_norm = in_proj_bwd_dh(dproj, Gm, x1, p["mix_norm"] + after(tm), dx2, "in_proj_bwd")

    dabh1, gw1 = ffn_bwd_weights(dx1_b, ab1, n1, G1, "ffn1_bwd")
    t1 = emit("ffn1", gw1)
    dx0, d_ffn1_norm = ffn_bwd_input(dabh1, G1, x, p["ffn1_norm"] + after(t1), dx1, "ffn1_bwd", as_operand=False)

    small = dict(ffn1_norm=d_ffn1_norm, mix_norm=d_mix_norm, b_gate=dbg.reshape(1, 2 * D),
                 q_norm=d_q_norm, k_norm=d_k_norm, rel_bias=d_rel_bias, ffn2_norm=d_ffn2_norm,
                 final_norm=d_final)
    return loss, dx0, small


def _pack_small(t, loss_row):
    row6 = jnp.concatenate([t["q_norm"].reshape(1, -1), t["k_norm"].reshape(1, -1), t["rel_bias"].reshape(1, -1)], axis=1)
    return jnp.concatenate([t["ffn1_norm"].reshape(1, -1), t["mix_norm"].reshape(1, -1), t["b_gate"].reshape(2, -1),
                            t["ffn2_norm"].reshape(1, -1), t["final_norm"].reshape(1, -1), row6, loss_row], axis=0)


def _unpack_small(a, shapes):
    return dict(ffn1_norm=a[0:1].reshape(shapes["ffn1_norm"]), mix_norm=a[1:2].reshape(shapes["mix_norm"]),
                b_gate=a[2:4].reshape(shapes["b_gate"]), ffn2_norm=a[4:5].reshape(shapes["ffn2_norm"]),
                final_norm=a[5].reshape(shapes["final_norm"]), q_norm=a[6:7, 0:128].reshape(shapes["q_norm"]),
                k_norm=a[6:7, 128:256].reshape(shapes["k_norm"]), rel_bias=a[6, 256:1024].reshape(shapes["rel_bias"]))


SMALL = ("ffn1_norm", "mix_norm", "b_gate", "q_norm", "k_norm", "rel_bias", "ffn2_norm", "final_norm")
ORDER = ("ffn1_norm", "ffn1_w1", "ffn1_w3", "ffn1_w2", "mix_norm", "w_in", "b_gate", "q_norm", "k_norm", "rel_bias",
         "w_branch_a", "w_branch_b", "w_out", "ffn2_norm", "ffn2_w1", "ffn2_w3", "ffn2_w2", "final_norm")


def kernel(x, ffn1_norm, ffn1_w1, ffn1_w3, ffn1_w2, mix_norm, w_in, b_gate, q_norm, k_norm, rel_bias, w_branch_a, w_branch_b, w_out, ffn2_norm, ffn2_w1, ffn2_w3, ffn2_w2, final_norm, loss_target, m_ffn1_norm, m_ffn1_w1, m_ffn1_w3, m_ffn1_w2, m_mix_norm, m_w_in, m_b_gate, m_q_norm, m_k_norm, m_rel_bias, m_w_branch_a, m_w_branch_b, m_w_out, m_ffn2_norm, m_ffn2_w1, m_ffn2_w3, m_ffn2_w2, m_final_norm, v_ffn1_norm, v_ffn1_w1, v_ffn1_w3, v_ffn1_w2, v_mix_norm, v_w_in, v_b_gate, v_q_norm, v_k_norm, v_rel_bias, v_w_branch_a, v_w_branch_b, v_w_out, v_ffn2_norm, v_ffn2_w1, v_ffn2_w3, v_ffn2_w2, v_final_norm):
    args = dict(locals())
    w = {n: args[n] for n in ORDER}
    m = {n: args["m_" + n] for n in ORDER}
    v = {n: args["v_" + n] for n in ORDER}
    D = x.shape[2]

    blocks = (
        ("ffn1_up", lambda t: jnp.concatenate([ffn1_w1[0].T + t, ffn1_w3[0].T + t], axis=0)),
        ("ffn1_down", lambda t: ffn1_w2[0] + t),
        ("mix", lambda t: jnp.concatenate([w_in[0] + t, w_branch_b[0] + t, w_out[0] + t,
                                           w_branch_a[0].reshape(MIX_ROWS - MIX_WA, D) + t], axis=0)),
        ("ffn2", lambda t: jnp.concatenate([ffn2_w1[0].T + t, ffn2_w3[0].T + t, ffn2_w2[0] + t], axis=0)),
    )
    gathers = {}
    start_token = jnp.zeros((8, LANES), F32)
    for tag, make in blocks:
        gathers[tag] = all_gather_start(make(start_token[0:1, 0:1]).astype(BF16), "all_gather_" + tag + "_start")
        start_token = gathers[tag][4]

    def gathered(tag):
        def get(after):
            return all_gather_finish(*_split_wait("all_gather_" + tag + "_wait", gathers[tag], 4, after),
                                     "all_gather_" + tag + "_finish")
        return get

    core = lax.axis_index("c").astype(jnp.int32).reshape(1)
    chip = (2 * lax.axis_index("x") + lax.axis_index("y")).astype(jnp.int32).reshape(1)
    exchanges = {}

    def emit(tag, gw):
        (theirs,) = reduce_scatter_pair([gw], "reduce_scatter_pair_" + tag)
        part = pair_add(gw, theirs, core, "pair_add_" + tag)
        exchanges[tag] = reduce_scatter_start(part, "reduce_scatter_" + tag + "_start")
        return exchanges[tag][4]

    small_p = dict(ffn1_norm=ffn1_norm, mix_norm=mix_norm, b_gate=b_gate, q_norm=q_norm, k_norm=k_norm,
                   rel_bias=rel_bias, ffn2_norm=ffn2_norm, final_norm=final_norm.reshape(1, D))
    loss_p, grad_x, small_g = local_step(x[0], loss_target[0], small_p, gathered("ffn1_up"), gathered("ffn1_down"),
                                         gathered("mix"), gathered("ffn2"), emit, start_token)

    def landed(tag, after):
        return _split_wait("reduce_scatter_" + tag + "_wait", exchanges[tag], 3, after)

    grads, delta, new_m, new_v = {}, {}, {}, {}

    def finish(n, part, land, off, blk, transposed=False):
        shp = w[n].shape
        if transposed:
            to2 = lambda a: a.reshape(shp[-2], shp[-1]).T
            back = lambda a: a.T.reshape(shp)
        else:
            to2 = lambda a: a.reshape(shp[-2], shp[-1])
            back = lambda a: a.reshape(shp)
        res = sum_adamw(part, land, chip, off, blk, to2(w[n]), to2(m[n]), to2(v[n]), "update_" + n)
        grads[n], delta[n], new_m[n], new_v[n] = [back(a) for a in res]

    last_token = exchanges["ffn1"][4]
    for tag, after in (("ffn2", last_token), ("ffn1", grad_x)):
        part, land = landed(tag, after)
        finish(tag + "_w1", part, land, 0, FFN_SHARD, transposed=True)
        finish(tag + "_w3", part, land, FFN_SHARD, FFN_SHARD, transposed=True)
        finish(tag + "_w2", part, land, 2 * FFN_SHARD, FFN_SHARD)
        if tag == "ffn2":
            part_m, land_m = landed("mix", last_token)
            finish("w_in", part_m, land_m, MIX_WIN, LANES)
            finish("w_branch_b", part_m, land_m, MIX_WB, LANES)
            finish("w_out", part_m, land_m, MIX_WOUT, LANES)
            grads["w_branch_a"] = sum_chips(part_m, land_m, chip, MIX_WA, MIX_ROWS - MIX_WA, MIX_ROWS - MIX_WA,
                                            "w_branch_a_sum").reshape(w_branch_a.shape)
    loss_row = jnp.pad(loss_p, ((0, 0), (0, D - LANES)))
    smalls = small_all_gather(_pack_small(small_g, loss_row))
    small_sum = sum_slots(smalls, 0, N_DEV, N_DEV, "small_sum")
    small_shapes = {n: w[n].shape for n in SMALL}
    grads.update(_unpack_small(small_sum, small_shapes))
    loss = small_sum[7, 0]

    n = "w_branch_a"
    two_d = lambda a: a.reshape(w[n].shape[-2], w[n].shape[-1])
    d_, m_, v_ = adamw(two_d(w[n]), two_d(grads[n]), two_d(m[n]), two_d(v[n]), "adamw_" + n)
    delta[n], new_m[n], new_v[n] = [a.reshape(w[n].shape) for a in (d_, m_, v_)]
    zero_row = jnp.zeros((1, D), F32)
    pack = lambda t: _pack_small({n: t[n] for n in SMALL}, zero_row)
    d_, m_, v_ = adamw(pack(w), small_sum, pack(m), pack(v), "adamw_small")
    for src, dst in ((d_, delta), (m_, new_m), (v_, new_v)):
        dst.update(_unpack_small(src, small_shapes))

    return (loss, grad_x[None], *[grads[n] for n in ORDER], *[delta[n] for n in ORDER],
            *[new_m[n] for n in ORDER], *[new_v[n] for n in ORDER])
```

```python
import math

import jax
import jax.numpy as jnp
from jax import lax
from jax.experimental import pallas as pl
from jax.experimental.pallas import tpu as pltpu

F32 = jnp.float32
BF16 = jnp.bfloat16
MESH = pl.DeviceIdType.MESH

V7X_VMEM_BYTES = 64 * 1024 * 1024
VMEM_LIMIT = V7X_VMEM_BYTES - 8 * 1024 * 1024
LANES = 128

N_DEV = 8
EPS = 1e-6
NEG_INF = -1e30

DILATIONS = (1, 4, 16)
HALF_WINDOW = 64
HEAD_DIM_A = 64
HEADS_PER_GROUP_A = 8
GROUP_WIDTH_A = 512
A_QKV_WIDTH = 4608
A_GROUP_QKV = A_QKV_WIDTH // 3
A_TQ = 128
A_WIN = A_TQ + 2 * HALF_WINDOW
A_UNROLL = 8
A_SCALE = HEAD_DIM_A ** -0.5
WGRAD_TK = 2048
HEAD_DIM_B = 128
N_HEADS_B = 8
N_KV_B = 2
GQA_GROUP_B = 4
GRID_W = 64
ROPE_THETA = 10000.0
B_TQ_FWD = 256
B_TQ_BWD = 512
B_HEADS_PER_STEP = 4
LOG2E = 1.4426950408889634
N_BUCKETS = 32
MAX_DISTANCE = 1024
PB_WIDTH = 3584
PB_GATE_A = 1536
PB_GATE_B = 2560

ADAM_LR = 0.001
ADAM_B1 = 0.9
ADAM_B2 = 0.999
ADAM_EPS = 1e-08
ADAM_WD = 0.01
ADAM_STEP = 10

FFN_SHARD = 352
MIX_WIN, MIX_WB, MIX_WOUT, MIX_WA = 0, 1024, 1152, 1280
MIX_ROWS = 1344


def _dot(a, b, ca=1, cb=0):
    return lax.dot_general(a, b, (((ca,), (cb,)), ((), ())), preferred_element_type=F32)


def _call(name, body, grid, ins, outs, scratch=(), sem=None, aliases=None):
    ins = [tuple(i) + (None,) * (4 - len(i)) for i in ins]
    res = pl.pallas_call(
        body,
        out_shape=[jax.ShapeDtypeStruct(s, d) for (s, d, _, _) in outs],
        grid=grid,
        in_specs=[pl.BlockSpec(bs, im, pipeline_mode=pm) for (_, bs, im, pm) in ins],
        out_specs=[pl.BlockSpec(bs, im) for (_, _, bs, im) in outs],
        scratch_shapes=list(scratch),
        name=name,
        input_output_aliases=aliases or {},
        compiler_params=pltpu.CompilerParams(dimension_semantics=sem, vmem_limit_bytes=VMEM_LIMIT),
    )(*[i[0] for i in ins])
    return res


def _sigmoid(x):
    return 0.5 * jnp.tanh(0.5 * x) + 0.5


def _position():
    return lax.axis_index("x"), lax.axis_index("y"), lax.axis_index("c")


def _hbm_specs(n):
    return [pl.BlockSpec(memory_space=pl.ANY) for _ in range(n)]


PAIR_BUFFERS = 4


def reduce_scatter_pair(grads, name):
    n = len(grads)
    C = grads[0].shape[2]
    half = [g.shape[1] // 2 for g in grads]
    chunks = [(i, q, hf) for i in range(n) for q in range(4) for hf in range(2)]
    nb = PAIR_BUFFERS

    def body(*refs):
        ins, theirs = refs[:n], refs[n:2 * n]
        buf, load_sems, send_sems, recv_sems = refs[2 * n:]
        x, y, c = _position()
        sibling = (x, y, 1 - c)

        def load(k):
            i, q, hf = chunks[k]
            r = half[i]
            return pltpu.make_async_copy(ins[i].at[2 * q + (1 - c), pl.ds(hf * r, r), :],
                                         buf.at[k % nb, pl.ds(0, r), :], load_sems.at[k % nb])

        def send(k):
            i, q, hf = chunks[k]
            r = half[i]
            return pltpu.make_async_remote_copy(
                src_ref=buf.at[k % nb, pl.ds(0, r), :], dst_ref=theirs[i].at[q, pl.ds(hf * r, r), :],
                send_sem=send_sems.at[k % nb], recv_sem=recv_sems.at[i],
                device_id=sibling, device_id_type=MESH)

        for k in range(len(chunks) + 1):
            if k < len(chunks):
                if k >= nb:
                    send(k - nb).wait_send()
                load(k).start()
            if k >= 1:
                load(k - 1).wait()
                send(k - 1).start()
        for k in range(max(0, len(chunks) - nb), len(chunks)):
            send(k).wait_send()
        for i in range(n):
            pltpu.make_async_remote_copy(
                src_ref=theirs[i], dst_ref=theirs[i], send_sem=send_sems.at[0], recv_sem=recv_sems.at[i],
                device_id=sibling, device_id_type=MESH).wait_recv()

    return pl.pallas_call(
        body,
        out_shape=[jax.ShapeDtypeStruct((4,) + g.shape[1:], g.dtype) for g in grads],
        in_specs=_hbm_specs(n),
        out_specs=_hbm_specs(n),
        scratch_shapes=[pltpu.VMEM((nb, max(half), C), grads[0].dtype), pltpu.SemaphoreType.DMA((nb,)),
                        pltpu.SemaphoreType.DMA((nb,)), pltpu.SemaphoreType.DMA((n,))],
        name=name,
        compiler_params=pltpu.CompilerParams(vmem_limit_bytes=VMEM_LIMIT),
    )(*grads)


_HBM_SPEC = pl.BlockSpec(memory_space=pltpu.HBM)
_SEM_SPEC = pl.BlockSpec(memory_space=pltpu.SEMAPHORE)
_TOKEN_SPEC = pl.BlockSpec(memory_space=pltpu.VMEM)
_DATAFLOW = pltpu.SideEffectType.DATAFLOW_SIDE_EFFECTING


def _split_start(name, body, src, land_shape):
    def full_body(src_ref, land_ref, send_sem, recv_sem, src_thru, land_thru, token):
        body(src_ref, land_ref, send_sem, recv_sem)
        token[...] = jnp.zeros_like(token)

    land = pltpu.with_memory_space_constraint(lax.empty(land_shape, src.dtype), pltpu.HBM)
    return pl.pallas_call(
        full_body, name=name,
        out_shape=(pltpu.SemaphoreType.DMA(()), pltpu.SemaphoreType.DMA(()),
                   pltpu.HBM(src.shape, src.dtype), pltpu.HBM(land_shape, src.dtype),
                   jax.ShapeDtypeStruct((8, LANES), F32)),
        in_specs=(_HBM_SPEC, _HBM_SPEC),
        out_specs=(_SEM_SPEC, _SEM_SPEC, _HBM_SPEC, _HBM_SPEC, _TOKEN_SPEC),
        input_output_aliases={0: 2, 1: 3},
        compiler_params=pltpu.CompilerParams(has_side_effects=_DATAFLOW),
    )(pltpu.with_memory_space_constraint(src, pltpu.HBM), land)


def _split_wait(name, started, n_blocks, after):
    send_sem, recv_sem, src_thru, land_thru, _ = started
    after = after if isinstance(after, tuple) else (after,)

    def body(src_ref, land_ref, send_sem, recv_sem, *rest):
        x, y, c = _position()
        blocks = land_ref.at[pl.ds(0, n_blocks)]
        copy = pltpu.make_async_remote_copy(src_ref=blocks, dst_ref=blocks, send_sem=send_sem, recv_sem=recv_sem,
                                            device_id=(x, y, c), device_id_type=MESH)
        copy.wait_send()
        copy.wait_recv()

    return pl.pallas_call(
        body, name=name,
        out_shape=(pltpu.HBM(src_thru.shape, src_thru.dtype), pltpu.HBM(land_thru.shape, land_thru.dtype)),
        in_specs=(_HBM_SPEC, _HBM_SPEC, _SEM_SPEC, _SEM_SPEC) + (pl.BlockSpec(memory_space=pl.ANY),) * len(after),
        out_specs=(_HBM_SPEC, _HBM_SPEC),
        input_output_aliases={0: 0, 1: 1},
        compiler_params=pltpu.CompilerParams(has_side_effects=_DATAFLOW),
    )(src_thru, land_thru, send_sem, recv_sem, *after)


def all_gather_start(block, name):
    def body(b_ref, land_ref, send_sem, recv_sem):
        x, y, c = _position()
        for peer in [(x, y, 1 - c), (1 - x, y, c), (x, 1 - y, c), (1 - x, 1 - y, c)]:
            pltpu.make_async_remote_copy(src_ref=b_ref, dst_ref=land_ref.at[4 * x + 2 * y + c],
                                         send_sem=send_sem, recv_sem=recv_sem,
                                         device_id=peer, device_id_type=MESH).start()

    return _split_start(name, body, block, (N_DEV,) + block.shape)


def all_gather_finish(block, land, name):
    R, C = block.shape

    def body(b_ref, land_in, land_ref, stage, load_sems, send_sems, recv_sems, own_sem):
        x, y, c = _position()
        sibling = (x, y, 1 - c)
        chips = [(1 - x, y), (x, 1 - y), (1 - x, 1 - y)]
        own_in = pltpu.make_async_copy(b_ref, stage.at[3], load_sems.at[3])
        own_in.start()
        loads = [pltpu.make_async_copy(land_in.at[4 * px + 2 * py + c], stage.at[j], load_sems.at[j])
                 for j, (px, py) in enumerate(chips)]
        for ld in loads:
            ld.start()
        sends = []
        for j, (px, py) in enumerate(chips):
            loads[j].wait()
            dst = land_ref.at[4 * px + 2 * py + c]
            cp = pltpu.make_async_remote_copy(src_ref=stage.at[j], dst_ref=dst, send_sem=send_sems.at[j],
                                              recv_sem=recv_sems.at[j], device_id=sibling, device_id_type=MESH)
            cp.start()
            sends.append(cp)
        own_in.wait()
        own_out = pltpu.make_async_copy(stage.at[3], land_ref.at[4 * x + 2 * y + c], own_sem)
        own_out.start()
        for j, (px, py) in enumerate(chips):
            dst = land_ref.at[4 * px + 2 * py + (1 - c)]
            pltpu.make_async_remote_copy(src_ref=stage.at[j], dst_ref=dst, send_sem=send_sems.at[j],
                                         recv_sem=recv_sems.at[j], device_id=sibling,
                                         device_id_type=MESH).wait_recv()
        for cp in sends:
            cp.wait_send()
        own_out.wait()

    return pl.pallas_call(
        body,
        out_shape=jax.ShapeDtypeStruct(land.shape, land.dtype),
        in_specs=_hbm_specs(2),
        out_specs=pl.BlockSpec(memory_space=pl.ANY),
        scratch_shapes=[pltpu.VMEM((4, R, C), block.dtype), pltpu.SemaphoreType.DMA((4,)),
                        pltpu.SemaphoreType.DMA((3,)), pltpu.SemaphoreType.DMA((3,)), pltpu.SemaphoreType.DMA],
        input_output_aliases={1: 0},
        name=name,
        compiler_params=pltpu.CompilerParams(vmem_limit_bytes=VMEM_LIMIT),
    )(block, land)


def reduce_scatter_start(parts, name):
    def body(p_ref, land_ref, send_sem, recv_sem):
        x, y, c = _position()
        for px, py in [(1 - x, y), (x, 1 - y), (1 - x, 1 - y)]:
            pltpu.make_async_remote_copy(src_ref=p_ref.at[2 * px + py], dst_ref=land_ref.at[2 * x + y],
                                         send_sem=send_sem, recv_sem=recv_sem,
                                         device_id=(px, py, c), device_id_type=MESH).start()

    return _split_start(name, body, parts, parts.shape)


def _other_devices(x, y, c):
    return [(1 - x if k & 4 else x, 1 - y if k & 2 else y, 1 - c if k & 1 else c) for k in range(1, N_DEV)]


def all_gather_start_direct(block, name):
    def body(b_ref, land_ref, send_sem, recv_sem):
        x, y, c = _position()
        for peer in _other_devices(x, y, c):
            pltpu.make_async_remote_copy(src_ref=b_ref, dst_ref=land_ref.at[4 * x + 2 * y + c],
                                         send_sem=send_sem, recv_sem=recv_sem,
                                         device_id=peer, device_id_type=MESH).start()

    return _split_start(name, body, block, (N_DEV,) + block.shape)


def all_gather_place_own(block, land, name):
    R, C = block.shape

    def body(b_ref, land_in, land_ref, stage, sems):
        x, y, c = _position()
        load = pltpu.make_async_copy(b_ref, stage, sems.at[0])
        load.start()
        load.wait()
        store = pltpu.make_async_copy(stage, land_ref.at[4 * x + 2 * y + c], sems.at[1])
        store.start()
        store.wait()

    return pl.pallas_call(
        body,
        out_shape=jax.ShapeDtypeStruct(land.shape, land.dtype),
        in_specs=_hbm_specs(2),
        out_specs=pl.BlockSpec(memory_space=pl.ANY),
        scratch_shapes=[pltpu.VMEM((R, C), block.dtype), pltpu.SemaphoreType.DMA((2,))],
        input_output_aliases={1: 0},
        name=name,
    )(block, land)


def reduce_scatter_start_direct(grads, name):
    def body(g_ref, land_ref, send_sem, recv_sem):
        x, y, c = _position()
        for px, py, pc in _other_devices(x, y, c):
            pltpu.make_async_remote_copy(src_ref=g_ref.at[4 * px + 2 * py + pc],
                                         dst_ref=land_ref.at[4 * x + 2 * y + c],
                                         send_sem=send_sem, recv_sem=recv_sem,
                                         device_id=(px, py, pc), device_id_type=MESH).start()

    return _split_start(name, body, grads, grads.shape)


def small_all_gather(small):
    def body(small_ref, smalls, s_send, s_recv, s_local):
        x, y, c = _position()
        me = 4 * x + 2 * y + c
        lc = pltpu.make_async_copy(small_ref, smalls.at[me], s_local)
        lc.start()
        remote = []
        k = 0
        for dx in (0, 1):
            for dy in (0, 1):
                for dc in (0, 1):
                    if dx + dy + dc == 0:
                        continue
                    peer = (1 - x if dx else x, 1 - y if dy else y, 1 - c if dc else c)
                    rc = pltpu.make_async_remote_copy(
                        src_ref=small_ref, dst_ref=smalls.at[me],
                        send_sem=s_send.at[k], recv_sem=s_recv.at[k],
                        device_id=peer, device_id_type=MESH)
                    rc.start()
                    remote.append(rc)
                    k += 1
        for rc in remote:
            rc.wait()
        lc.wait()

    return pl.pallas_call(
        body,
        out_shape=jax.ShapeDtypeStruct((N_DEV,) + small.shape, small.dtype),
        in_specs=_hbm_specs(1),
        out_specs=pl.BlockSpec(memory_space=pl.ANY),
        scratch_shapes=[pltpu.SemaphoreType.DMA((7,)), pltpu.SemaphoreType.DMA((7,)), pltpu.SemaphoreType.DMA],
        name="small_all_gather",
    )(small)


def pair_add(grads, theirs, core, name):
    _, R, C = theirs.shape
    tr = R // 2

    def body(c_ref, a_ref, b_ref, o_ref):
        o_ref[...] = (a_ref[...].astype(F32) + b_ref[...].astype(F32)).astype(BF16)

    return pl.pallas_call(
        body,
        out_shape=jax.ShapeDtypeStruct(theirs.shape, BF16),
        grid_spec=pltpu.PrefetchScalarGridSpec(
            num_scalar_prefetch=1, grid=(4, R // tr),
            in_specs=[pl.BlockSpec((None, tr, C), lambda q, i, c: (2 * q + c[0], i, 0)),
                      pl.BlockSpec((None, tr, C), lambda q, i, c: (q, i, 0))],
            out_specs=pl.BlockSpec((None, tr, C), lambda q, i, c: (q, i, 0))),
        name=name,
        compiler_params=pltpu.CompilerParams(dimension_semantics=("parallel", "parallel"),
                                             vmem_limit_bytes=VMEM_LIMIT),
    )(core, grads, theirs)


def sum_slots(recv, off, rows, blk, name):
    nq, _, C = recv.shape
    ob = off // blk

    def body(r_ref, o_ref):
        acc = r_ref[0].astype(F32)
        for q in range(1, nq):
            acc = acc + r_ref[q].astype(F32)
        o_ref[...] = acc

    return _call(name, body, (rows // blk,),
                 [(recv, (nq, blk, C), lambda i: (0, ob + i, 0))],
                 [((rows, C), F32, (blk, C), lambda i: (i, 0))], sem=("parallel",))[0]


def _sum_terms(refs):
    acc = refs[0][...].astype(F32)
    for r in refs[1:]:
        acc = acc + r[...].astype(F32)
    return acc


def sum_landed(own, land, me, off, rows, blk, name):
    n, _, C = land.shape
    ob = off // blk

    def body(c_ref, *refs):
        refs[n][...] = _sum_terms(refs[:n])

    def entry(flip):
        return pl.BlockSpec((None, blk, C), lambda i, c: (c[0] ^ flip, ob + i, 0))

    return pl.pallas_call(
        body,
        out_shape=jax.ShapeDtypeStruct((rows, C), F32),
        grid_spec=pltpu.PrefetchScalarGridSpec(
            num_scalar_prefetch=1, grid=(rows // blk,),
            in_specs=[entry(k) for k in range(n)],
            out_specs=pl.BlockSpec((blk, C), lambda i, c: (i, 0))),
        name=name,
        compiler_params=pltpu.CompilerParams(dimension_semantics=("parallel",), vmem_limit_bytes=VMEM_LIMIT),
    )(me, own, *([land] * (n - 1)))


def _adamw_update(wv, gv, mv, vv):
    nm = ADAM_B1 * mv + (1.0 - ADAM_B1) * gv
    nv = ADAM_B2 * vv + (1.0 - ADAM_B2) * (gv * gv)
    c1 = 1.0 / (1.0 - ADAM_B1 ** ADAM_STEP)
    c2 = 1.0 / (1.0 - ADAM_B2 ** ADAM_STEP)
    return -ADAM_LR * ((nm * c1) / (jnp.sqrt(nv * c2) + ADAM_EPS) + ADAM_WD * wv), nm, nv


def sum_adamw(own, land, me, off, blk, w, m, v, name):
    rows, C = w.shape
    n = land.shape[0]
    ob = off // blk

    def body(c_ref, *refs):
        w_ref, m_ref, v_ref, g_out, d_out, m_out, v_out = refs[n:]
        gv = _sum_terms(refs[:n])
        g_out[...] = gv
        d_out[...], m_out[...], v_out[...] = _adamw_update(w_ref[...], gv, m_ref[...], v_ref[...])

    def entry(flip):
        return pl.BlockSpec((None, blk, C), lambda i, c: (c[0] ^ flip, ob + i, 0))

    plain = pl.BlockSpec((blk, C), lambda i, c: (i, 0))
    return pl.pallas_call(
        body,
        out_shape=[jax.ShapeDtypeStruct((rows, C), F32)] * 4,
        grid_spec=pltpu.PrefetchScalarGridSpec(
            num_scalar_prefetch=1, grid=(rows // blk,),
            in_specs=[entry(k) for k in range(n)] + [plain, plain, plain],
            out_specs=[plain] * 4),
        name=name,
        compiler_params=pltpu.CompilerParams(dimension_semantics=("parallel",), vmem_limit_bytes=VMEM_LIMIT),
    )(me, own, *([land] * (n - 1)), w, m, v)


def adamw(w, g, m, v, name):
    R, C = w.shape
    tr = R
    for cand in (256, 128, 64, 32, 16, 8):
        if R % cand == 0 and R > cand:
            tr = cand
            break

    def body(w_ref, g_ref, m_ref, v_ref, d_ref, nm_ref, nv_ref):
        d_ref[...], nm_ref[...], nv_ref[...] = _adamw_update(w_ref[...], g_ref[...], m_ref[...], v_ref[...])

    spec = ((tr, C), lambda i: (i, 0))
    out = ((R, C), F32) + spec
    return _call(name, body, (R // tr,), [(w,) + spec, (g,) + spec, (m,) + spec, (v,) + spec],
                 [out, out, out], sem=("parallel",))


def rms_fwd(x, g, name):
    S, D = x.shape
    tr = 512

    def body(x_ref, g_ref, o_ref):
        xv = x_ref[...]
        r = lax.rsqrt(jnp.mean(xv * xv, axis=-1, keepdims=True) + EPS)
        o_ref[...] = (xv * r * g_ref[...]).astype(BF16)

    return _call(name, body, (S // tr,),
                 [(x, (tr, D), lambda i: (i, 0)), (g, (1, D), lambda i: (0, 0))],
                 [((S, D), BF16, (tr, D), lambda i: (i, 0))], sem=("parallel",))[0]


def _rms_bwd_tile(dn, xv, gv):
    r = lax.rsqrt(jnp.mean(xv * xv, axis=-1, keepdims=True) + EPS)
    xh = xv * r
    dxh = dn * gv
    dx = r * (dxh - xh * jnp.mean(dxh * xh, axis=-1, keepdims=True))
    return dx, dn * xh


def final_loss(x, tgt, g, name):
    S, D = x.shape
    tr = 256

    def body(x_ref, t_ref, g_ref, l_ref, dx_ref, dxb_ref, dg_ref):
        i = pl.program_id(0)
        xv, gv = x_ref[...], g_ref[...]
        r = lax.rsqrt(jnp.mean(xv * xv, axis=-1, keepdims=True) + EPS)
        xh = xv * r
        e = xh * gv - t_ref[...]
        part = 0.5 * jnp.sum(jnp.sum(e * e, axis=-1, keepdims=True) * (1.0 / D), axis=0, keepdims=True)
        dy = e * (1.0 / D)
        dxh = dy * gv
        dx = r * (dxh - xh * jnp.mean(dxh * xh, axis=-1, keepdims=True))
        dx_ref[...] = dx
        dxb_ref[...] = dx.astype(BF16)
        dgp = jnp.sum(dy * xh, axis=0, keepdims=True)

        @pl.when(i == 0)
        def _():
            l_ref[...] = jnp.broadcast_to(part, l_ref.shape)
            dg_ref[...] = dgp

        @pl.when(i > 0)
        def _():
            l_ref[...] += jnp.broadcast_to(part, l_ref.shape)
            dg_ref[...] += dgp

    row = ((tr, D), lambda i: (i, 0))
    return _call(name, body, (S // tr,),
                 [(x,) + row, (tgt,) + row, (g, (1, D), lambda i: (0, 0))],
                 [((1, LANES), F32, (1, LANES), lambda i: (0, 0)), ((S, D), F32) + row, ((S, D), BF16) + row,
                  ((1, D), F32, (1, D), lambda i: (0, 0))], sem=("arbitrary",))


FFN_TF = 4 * FFN_SHARD


def _ffn_pick(G, which):
    if isinstance(G, tuple):
        return (G[0], which) if which < 2 else (G[1], 0)
    return G, which


def _ffn_w_spec(G, which, imap):
    arr, blk = _ffn_pick(G, which)
    return (arr, (4, FFN_SHARD, arr.shape[2]), lambda *idx: (imap(*idx), blk, 0))


def _ffn_whole_w_spec(G, which):
    arr, blk = _ffn_pick(G, which)
    return (arr, (N_DEV, FFN_SHARD, arr.shape[2]), lambda *idx: (0, blk, 0))


def ffn_up(n, G, name):
    S, D = n.shape
    F = N_DEV * FFN_SHARD
    tm = 1024

    def body(n_ref, w1_ref, w3_ref, abh_ref):
        nv = n_ref[...]
        a = _dot(nv, w1_ref[...].reshape(FFN_TF, D), 1, 1).astype(BF16)
        b = _dot(nv, w3_ref[...].reshape(FFN_TF, D), 1, 1).astype(BF16)
        abh_ref[0] = a
        abh_ref[1] = b
        av, bv = a.astype(F32), b.astype(F32)
        abh_ref[2] = (av * _sigmoid(av) * bv).astype(BF16)

    return _call(name, body, (F // FFN_TF, S // tm),
                 [(n, (tm, D), lambda j, i: (i, 0)),
                  _ffn_w_spec(G, 0, lambda j, i: j), _ffn_w_spec(G, 1, lambda j, i: j)],
                 [((3, S, F), BF16, (3, tm, FFN_TF), lambda j, i: (0, i, j))],
                 sem=("parallel", "parallel"))[0]


def ffn_down(abh, G, x, name):
    _, S, F = abh.shape
    D = x.shape[1]
    tm = 512

    def body(h_ref, w2_ref, x_ref, o_ref):
        o_ref[...] = x_ref[...] + 0.5 * _dot(h_ref[...], w2_ref[...].reshape(F, D))

    return _call(name, body, (S // tm,),
                 [(abh, (None, tm, F), lambda i: (2, i, 0)), _ffn_whole_w_spec(G, 2),
                  (x, (tm, D), lambda i: (i, 0))],
                 [((S, D), F32, (tm, D), lambda i: (i, 0))], sem=("parallel",))[0]


def ffn_bwd_weights(dxo, abh, n, G, name):
    _, S, F = abh.shape
    D = dxo.shape[1]
    tm = 512
    nf = F // FFN_TF

    def down_body(d_ref, w2_ref, ab_ref, o_ref):
        dh = 0.5 * _dot(d_ref[...].astype(BF16), w2_ref[...].reshape(FFN_TF, D), 1, 1)
        av, bv = ab_ref[0].astype(F32), ab_ref[1].astype(F32)
        sig = _sigmoid(av)
        o_ref[0] = (dh * bv * (sig * (1.0 + av * (1.0 - sig)))).astype(BF16)
        o_ref[1] = (dh * (av * sig)).astype(BF16)

    dab = _call(name + "_down_bwd", down_body, (nf, S // tm),
                [(dxo, (tm, D), lambda j, i: (i, 0)), _ffn_w_spec(G, 2, lambda j, i: j),
                 (abh, (2, tm, FFN_TF), lambda j, i: (0, i, j))],
                [((2, S, F), BF16, (2, tm, FFN_TF), lambda j, i: (0, i, j))],
                sem=("parallel", "parallel"))[0]

    tk = WGRAD_TK
    nk = S // tk
    gshape = (N_DEV, 3 * FFN_SHARD, D)

    def dw2_body(h_ref, d_ref, o_ref, acc_ref):
        k = pl.program_id(1)
        p = _dot(h_ref[...], d_ref[...].astype(BF16), 0, 0)

        @pl.when(k == 0)
        def _():
            acc_ref[...] = p

        @pl.when(k > 0)
        def _():
            acc_ref[...] += p

        @pl.when(k == nk - 1)
        def _():
            o_ref[...] = (0.5 * acc_ref[...]).astype(BF16).reshape(4, FFN_SHARD, D)

    gw = _call(name + "_dw2", dw2_body, (nf, nk),
               [(abh, (None, tk, FFN_TF), lambda j, k: (2, k, j)), (dxo, (tk, D), lambda j, k: (k, 0))],
               [(gshape, BF16, (4, FFN_SHARD, D), lambda j, k: (j, 2, 0))],
               scratch=[pltpu.VMEM((FFN_TF, D), F32)], sem=("parallel", "arbitrary"))[0]

    def dw13_body(gw_ref, dab_ref, n_ref, o_ref):
        o_ref[...] = _dot(dab_ref[...], n_ref[...], 0, 0).astype(BF16).reshape(4, FFN_SHARD, D)

    gw = pl.pallas_call(
        dw13_body,
        out_shape=jax.ShapeDtypeStruct(gshape, BF16),
        grid=(2, nf),
        in_specs=[pl.BlockSpec(memory_space=pl.ANY),
                  pl.BlockSpec((None, S, FFN_TF), lambda w, j: (w, 0, j)),
                  pl.BlockSpec((S, D), lambda w, j: (0, 0))],
        out_specs=pl.BlockSpec((4, FFN_SHARD, D), lambda w, j: (j, w, 0)),
        input_output_aliases={0: 0},
        name=name + "_dw13",
        compiler_params=pltpu.CompilerParams(dimension_semantics=("parallel", "parallel"),
                                             vmem_limit_bytes=VMEM_LIMIT),
    )(gw, dab, n)
    return dab, gw


def ffn_bwd_input(dab, G, x_in, g, dxo, name, as_operand=True):
    _, S, F = dab.shape
    D = x_in.shape[1]
    tm = 256

    def dn_body(dab_ref, w1_ref, w3_ref, x_ref, d_ref, g_ref, dx_ref, *rest):
        dg_ref = rest[-1]
        i = pl.program_id(0)
        dn = _dot(dab_ref[0], w1_ref[...].reshape(F, D)) + _dot(dab_ref[1], w3_ref[...].reshape(F, D))
        dx, dgt = _rms_bwd_tile(dn, x_ref[...], g_ref[...])
        dx = d_ref[...] + dx
        dx_ref[...] = dx
        if as_operand:
            rest[0][...] = dx.astype(BF16)
        dgp = jnp.sum(dgt, axis=0, keepdims=True)

        @pl.when(i == 0)
        def _():
            dg_ref[...] = dgp

        @pl.when(i > 0)
        def _():
            dg_ref[...] += dgp

    tile = ((tm, D), lambda i: (i, 0))
    return _call(name + "_dn", dn_body, (S // tm,),
                 [(dab, (2, tm, F), lambda i: (0, i, 0)),
                  _ffn_whole_w_spec(G, 0), _ffn_whole_w_spec(G, 1),
                  (x_in,) + tile, (dxo,) + tile, (g, (1, D), lambda i: (0, 0))],
                 [((S, D), F32) + tile] + ([((S, D), BF16) + tile] if as_operand else [])
                 + [((1, D), F32, (1, D), lambda i: (0, 0))],
                 sem=("arbitrary",))


PROJ_TN = 512
DH_SHARDS_PER_STEP = 4


def in_proj(h, Gm, first_tile, n_tiles, dtype, name, tile_stride=1):
    S, D = h.shape
    tile = lambda j: first_tile + tile_stride * j

    def body(h_ref, w_ref, o_ref):
        o_ref[...] = _dot(h_ref[...], w_ref[...]).astype(dtype)

    return _call(name, body, (n_tiles,),
                 [(h, (S, D), lambda j: (0, 0)),
                  (Gm, (None, D, PROJ_TN), lambda j: (tile(j) // 2, 0, tile(j) % 2))],
                 [((S, n_tiles * PROJ_TN), dtype, (S, PROJ_TN), lambda j: (0, j))],
                 sem=("parallel",))[0]


def _dproj_pieces(dqkv, dq_b, dkv_b, dgate):
    pieces = [(dqkv[g], [(3 * which + g, (which, 0)) for which in range(3)]) for g in range(3)]
    pieces.append((dq_b, [(9, (None, 0)), (10, (None, 1))]))
    pieces.append((dkv_b, [(11, (None, 0))]))
    pieces.append((dgate, [(12 + 2 * a + b, (a, b)) for a in range(2) for b in range(2)]))
    return pieces


def in_proj_bwd_dw(pieces, h, gm_grads, name):
    S, D = h.shape

    for n_piece, (arr, tiles) in enumerate(pieces):
        w_tile = [t for t, _ in tiles]
        lead = [ix[0] for _, ix in tiles]
        colb = [ix[1] for _, ix in tiles]

        def pick(table, j):
            out = table[-1]
            for k in range(len(table) - 2, -1, -1):
                out = jnp.where(j == k, table[k], out)
            return out

        def dw_body(gm_ref, h_ref, d_ref, o_ref):
            o_ref[...] = _dot(h_ref[...], d_ref[...], 0, 0).astype(BF16)

        if arr.ndim == 3:
            d_spec = pl.BlockSpec((None, S, PROJ_TN), lambda j, lead=lead, colb=colb: (pick(lead, j), 0, pick(colb, j)))
        else:
            d_spec = pl.BlockSpec((S, PROJ_TN), lambda j, colb=colb: (0, pick(colb, j)))
        gm_grads = pl.pallas_call(
            dw_body,
            out_shape=jax.ShapeDtypeStruct(gm_grads.shape, BF16),
            grid=(len(tiles),),
            in_specs=[pl.BlockSpec(memory_space=pl.ANY), pl.BlockSpec((S, D), lambda j: (0, 0)), d_spec],
            out_specs=pl.BlockSpec((None, D, PROJ_TN),
                                   lambda j, w_tile=w_tile: (pick(w_tile, j) // 2, 0, pick(w_tile, j) % 2)),
            input_output_aliases={0: 0},
            name="%s_dw%d" % (name, n_piece),
            compiler_params=pltpu.CompilerParams(dimension_semantics=("parallel",), vmem_limit_bytes=VMEM_LIMIT),
        )(gm_grads, h, arr)
    return gm_grads


def in_proj_bwd_dh(pieces, Gm, x_in, g, dres, name):
    S, D = x_in.shape
    tm = 256
    C = Gm.shape[2]
    n_sh = N_DEV
    n_p = len(pieces)

    def dh_body(*refs):
        d_refs = refs[:n_p]
        w_ref, x_ref, r_ref, g_ref, dx_ref, dxb_ref, dg_ref = refs[n_p:]
        i = pl.program_id(0)
        p = None
        for d_ref, (arr, tiles) in zip(d_refs, pieces):
            for t, (lead, colb) in tiles:
                cols = slice(colb * PROJ_TN, (colb + 1) * PROJ_TN)
                d = d_ref[:, cols] if lead is None else d_ref[lead, :, cols]
                wcol = (t % 2) * PROJ_TN
                term = _dot(d, w_ref[t // 2, :, wcol:wcol + PROJ_TN], 1, 1)
                p = term if p is None else p + term
        dx, dgt = _rms_bwd_tile(p, x_ref[...], g_ref[...])
        dx = r_ref[...] + dx
        dx_ref[...] = dx
        dxb_ref[...] = dx.astype(BF16)
        dgp = jnp.sum(dgt, axis=0, keepdims=True)

        @pl.when(i == 0)
        def _():
            dg_ref[...] = dgp

        @pl.when(i > 0)
        def _():
            dg_ref[...] += dgp

    tile = ((tm, D), lambda i: (i, 0))

    def rows_of(arr):
        if arr.ndim == 3:
            return (arr, (arr.shape[0], tm, arr.shape[2]), lambda i: (0, i, 0))
        return (arr, (tm, arr.shape[1]), lambda i: (i, 0))

    return _call(name + "_dh", dh_body, (S // tm,),
                 [rows_of(arr) for arr, _ in pieces]
                 + [(Gm, (n_sh, D, C), lambda i: (0, 0, 0), pl.Buffered(1)),
                    (x_in,) + tile, (dres,) + tile, (g, (1, D), lambda i: (0, 0))],
                 [((S, D), F32) + tile, ((S, D), BF16) + tile, ((1, D), F32, (1, D), lambda i: (0, 0))],
                 sem=("arbitrary",))


def _t5_bucket(rel):
    n = N_BUCKETS // 2
    max_exact = n // 2
    ret = jnp.where(rel > 0, n, 0)
    a = jnp.abs(rel)
    af = jnp.maximum(a, 1).astype(F32)
    large = max_exact + (jnp.log(af / max_exact) / math.log(MAX_DISTANCE / max_exact)
                         * (n - max_exact)).astype(jnp.int32)
    large = jnp.minimum(large, n - 1)
    return ret + jnp.where(a < max_exact, a, large)


def _bucket_tables():
    qi = jnp.arange(A_TQ, dtype=jnp.int32)[:, None]
    kj = jnp.arange(A_WIN, dtype=jnp.int32)[None, :]
    rel = kj - HALF_WINDOW - qi
    return jnp.stack([_t5_bucket(rel * d) for d in DILATIONS], axis=0)


def bias_build(rel_bias, buckets):
    def body(tab_ref, bk_ref, o_ref):
        col = pl.program_id(0) * HEADS_PER_GROUP_A + pl.program_id(1)
        bk = bk_ref[...]
        acc = jnp.zeros(bk.shape, F32)
        for b in range(N_BUCKETS):
            acc = jnp.where(bk == b, tab_ref[b, col], acc)
        qi = lax.broadcasted_iota(jnp.int32, bk.shape, 0)
        kj = lax.broadcasted_iota(jnp.int32, bk.shape, 1)
        band = jnp.where(jnp.abs(kj - HALF_WINDOW - qi) <= HALF_WINDOW, acc, NEG_INF)
        o_ref[0] = jnp.where(kj >= HALF_WINDOW, band, NEG_INF)
        o_ref[1] = band
        o_ref[2] = jnp.where(kj < A_TQ + HALF_WINDOW, band, NEG_INF)

    out = pl.pallas_call(
        body,
        out_shape=jax.ShapeDtypeStruct((3, HEADS_PER_GROUP_A // 2, 3, 2, A_TQ, A_WIN), F32),
        grid=(3, HEADS_PER_GROUP_A),
        in_specs=[pl.BlockSpec(memory_space=pltpu.SMEM),
                  pl.BlockSpec((None, A_TQ, A_WIN), lambda g, h: (g, 0, 0))],
        out_specs=pl.BlockSpec((None, None, 3, None, A_TQ, A_WIN), lambda g, h: (g, h // 2, 0, h % 2, 0, 0)),
        name="a_bias_build",
        compiler_params=pltpu.CompilerParams(dimension_semantics=("parallel", "parallel")),
    )(rel_bias, buckets)
    return out.reshape(3, HEADS_PER_GROUP_A // 2, 3, 2 * A_TQ, A_WIN)


def bias_bwd(dbias, buckets):
    def body(d_ref, bk_ref, o_ref):
        bk = bk_ref[...]
        dv = d_ref[...]
        for b in range(N_BUCKETS):
            part = jnp.sum(jnp.where(bk == b, dv, 0.0), axis=1, keepdims=True)
            o_ref[b:b + 1, :] = jnp.broadcast_to(jnp.sum(part, axis=0, keepdims=True), (1, LANES))

    out = pl.pallas_call(
        body,
        out_shape=jax.ShapeDtypeStruct((3, HEADS_PER_GROUP_A, N_BUCKETS, LANES), F32),
        grid=(3, HEADS_PER_GROUP_A),
        in_specs=[pl.BlockSpec((None, None, A_TQ, A_WIN), lambda g, h: (g, h, 0, 0)),
                  pl.BlockSpec((None, A_TQ, A_WIN), lambda g, h: (g, 0, 0))],
        out_specs=pl.BlockSpec((None, None, N_BUCKETS, LANES), lambda g, h: (g, h, 0, 0)),
        name="a_bias_bwd",
        compiler_params=pltpu.CompilerParams(dimension_semantics=("parallel", "parallel")),
    )(dbias, buckets)
    return out[:, :, :, 0].transpose(2, 0, 1).reshape(N_BUCKETS, 3 * HEADS_PER_GROUP_A)


def _a_fill_padded(pad_ref, src_ref, n, pad):
    zeros = jnp.zeros((pad, LANES), pad_ref.dtype)
    pad_ref[0:pad, :] = zeros
    pad_ref[pad + n:2 * pad + n, :] = zeros
    pad_ref[pad:pad + n, :] = src_ref[...].astype(pad_ref.dtype)


def _a_stack_heads(x, lane):
    zero = jnp.zeros_like(x)
    return jnp.concatenate([jnp.where(lane < HEAD_DIM_A, x, zero), jnp.where(lane >= HEAD_DIM_A, x, zero)], axis=0)


def _a_bias_variant(qb, nqb):
    return jnp.where(qb == 0, 0, jnp.where(qb == nqb - 1, 2, 1))


def a_fwd(proj_g, bias_g, g, name):
    S = proj_g.shape[0]
    d = DILATIONS[g]
    L = S // d
    nqb = L // A_TQ
    pad = HALF_WINDOW * d

    def body(q_ref, k_ref, v_ref, b_ref, o_ref, l_ref, qf, kpad, vpad):
        qf[...] = q_ref[...].astype(F32) * A_SCALE
        _a_fill_padded(kpad, k_ref, S, pad)
        _a_fill_padded(vpad, v_ref, S, pad)
        lane = lax.broadcasted_iota(jnp.int32, (A_TQ, LANES), 1)

        def block(t, carry):
            qb, r = t // d, t % d
            start = qb * (A_TQ * d) + r
            kw = kpad[pl.ds(start, A_WIN, stride=d), :].astype(BF16)
            vw = vpad[pl.ds(start, A_WIN, stride=d), :].astype(BF16)
            q = qf[pl.ds(start, A_TQ, stride=d), :].astype(BF16)
            q2 = _a_stack_heads(q, lane)
            s = _dot(q2, kw, 1, 1) + b_ref[_a_bias_variant(qb, nqb)]
            m = jnp.max(s, axis=-1, keepdims=True)
            e = jnp.exp(s - m)
            l = jnp.sum(e, axis=-1, keepdims=True)
            o2 = _dot(e.astype(BF16), vw) / l
            lse2 = m + jnp.log(l)
            o_ref[pl.ds(start, A_TQ, stride=d), :] = jnp.where(lane < HEAD_DIM_A, o2[0:A_TQ], o2[A_TQ:])
            l_ref[pl.ds(start, A_TQ, stride=d), :] = jnp.where(lane < HEAD_DIM_A, lse2[0:A_TQ], lse2[A_TQ:])
            return carry

        lax.fori_loop(0, nqb * d, block, 0, unroll=A_UNROLL)

    out_spec = ((S, GROUP_WIDTH_A), F32, (S, LANES), lambda hp: (0, hp))
    return _call(name, body, (4,),
                 [(proj_g, (S, LANES), lambda hp: (0, hp)),
                  (proj_g, (S, LANES), lambda hp: (0, 4 + hp)),
                  (proj_g, (S, LANES), lambda hp: (0, 8 + hp)),
                  (bias_g, (None, 3, 2 * A_TQ, A_WIN), lambda hp: (hp, 0, 0, 0))],
                 [out_spec, out_spec],
                 scratch=[pltpu.VMEM((S, LANES), F32)] + [pltpu.VMEM((S + 2 * pad, LANES), F32)] * 2,
                 sem=("parallel",))


def a_combine(outs, lses, name):
    S, W = outs[0].shape
    tr = 512

    def body(o0, o1, o2, l0, l1, l2, oa_ref, lt_ref):
        a, b, c = l0[...], l1[...], l2[...]
        m = jnp.maximum(jnp.maximum(a, b), c)
        ea, eb, ec = jnp.exp(a - m), jnp.exp(b - m), jnp.exp(c - m)
        z = ea + eb + ec
        oa_ref[...] = ((ea * o0[...] + eb * o1[...] + ec * o2[...]) / z).astype(BF16)
        lt_ref[...] = m + jnp.log(z)

    spec = ((tr, W), lambda i: (i, 0))
    return _call(name, body, (S // tr,), [(a,) + spec for a in (*outs, *lses)],
                 [((S, W), BF16) + spec, ((S, W), F32) + spec], sem=("parallel",))


def a_bwd(proj_g, bias_g, do_a, o_a, lse_tot, g, name):
    S = proj_g.shape[0]
    d = DILATIONS[g]
    L = S // d
    nqb = L // A_TQ
    pad = HALF_WINDOW * d

    def body(q_ref, k_ref, v_ref, b_ref, do_ref, o_ref, l_ref, dqkv_ref, db_ref,
             qf, of, dqf, kpad, vpad, dkacc, dvacc):
        qf[...] = q_ref[...].astype(F32) * A_SCALE
        of[...] = o_ref[...].astype(F32)
        _a_fill_padded(kpad, k_ref, S, pad)
        _a_fill_padded(vpad, v_ref, S, pad)
        dkacc[...] = jnp.zeros(dkacc.shape, F32)
        dvacc[...] = jnp.zeros(dvacc.shape, F32)
        db_ref[...] = jnp.zeros(db_ref.shape, F32)
        lane = lax.broadcasted_iota(jnp.int32, (A_TQ, LANES), 1)

        def block(t, carry):
            qb, r = t // d, t % d
            start = qb * (A_TQ * d) + r
            rows = pl.ds(start, A_TQ, stride=d)
            win = pl.ds(start, A_WIN, stride=d)
            kw = kpad[win, :].astype(BF16)
            vw = vpad[win, :].astype(BF16)
            q = qf[rows, :].astype(BF16)
            do = do_ref[rows, :]
            ov = of[rows, :]
            lt = l_ref[rows, :]
            q2 = _a_stack_heads(q, lane)
            do2 = _a_stack_heads(do, lane)
            lt2 = jnp.concatenate([lt[:, 0:1], lt[:, HEAD_DIM_A:HEAD_DIM_A + 1]], axis=0)
            s = _dot(q2, kw, 1, 1) + b_ref[_a_bias_variant(qb, nqb)]
            p = jnp.exp(s - lt2)
            t = jnp.sum(do2 * jnp.concatenate([ov, ov], axis=0), axis=-1, keepdims=True)
            dob2 = do2.astype(BF16)
            ds = p * (_dot(dob2, vw, 1, 1) - t)
            db_ref[...] += ds
            dsb = ds.astype(BF16)
            dq2 = _dot(dsb, kw)
            dqf[rows, :] = jnp.where(lane < HEAD_DIM_A, dq2[0:A_TQ], dq2[A_TQ:]) * A_SCALE
            dkacc[win, :] += _dot(dsb, q2, 0, 0)
            dvacc[win, :] += _dot(p.astype(BF16), dob2, 0, 0)
            return carry

        lax.fori_loop(0, nqb * d, block, 0, unroll=A_UNROLL)
        dqkv_ref[0] = dqf[...].astype(BF16)
        dqkv_ref[1] = dkacc[pad:pad + S, :].astype(BF16)
        dqkv_ref[2] = dvacc[pad:pad + S, :].astype(BF16)

    slab = ((S, LANES), lambda hp: (0, hp))
    padded = pltpu.VMEM((S + 2 * pad, LANES), F32)
    return _call(
        name, body, (4,),
        [(proj_g, (S, LANES), lambda hp: (0, hp)),
         (proj_g, (S, LANES), lambda hp: (0, 4 + hp)),
         (proj_g, (S, LANES), lambda hp: (0, 8 + hp)),
         (bias_g, (None, 3, 2 * A_TQ, A_WIN), lambda hp: (hp, 0, 0, 0)),
         (do_a,) + slab, (o_a,) + slab, (lse_tot,) + slab],
        [((3, S, GROUP_WIDTH_A), BF16, (3, S, LANES), lambda hp: (0, 0, hp)),
         ((4, 2 * A_TQ, A_WIN), F32, (None, 2 * A_TQ, A_WIN), lambda hp: (hp, 0, 0))],
        scratch=[pltpu.VMEM((S, LANES), F32)] * 3 + [padded] * 4,
        sem=("parallel",))


def _rope_tables(S):
    rows = S // GRID_W
    row = jnp.repeat(jnp.arange(rows, dtype=F32), GRID_W)
    col = jnp.tile(jnp.arange(GRID_W, dtype=F32), rows)
    n_freq = HEAD_DIM_B // 4
    freq = ROPE_THETA ** (-jnp.arange(n_freq, dtype=F32) / n_freq)
    ang = jnp.concatenate([row[:, None] * freq, col[:, None] * freq], axis=-1)
    cos, sin = jnp.cos(ang), jnp.sin(ang)
    return jnp.repeat(cos, 2, axis=-1), jnp.stack([-sin, sin], axis=-1).reshape(S, HEAD_DIM_B)


def _swap_pairs(y):
    lane = lax.broadcasted_iota(jnp.int32, y.shape, 1)
    return jnp.where(lane % 2 == 0, pltpu.roll(y, LANES - 1, 1), pltpu.roll(y, 1, 1))


def qkv_prep(proj_b, gains, cos_t, sin_t, name):
    S = proj_b.shape[0]
    ts = 256
    n_rot = N_HEADS_B + N_KV_B
    nh = n_rot + N_KV_B
    W = nh * LANES

    def body(x_ref, g_ref, c_ref, s_ref, o_ref):
        cv, sv = c_ref[...], s_ref[...]
        for hb in range(nh):
            cols = slice(hb * LANES, (hb + 1) * LANES)
            xv = x_ref[:, cols]
            if hb < n_rot:
                r = lax.rsqrt(jnp.mean(xv * xv, axis=-1, keepdims=True) + EPS)
                yv = xv * r * g_ref[:, cols]
                o_ref[:, cols] = (yv * cv + _swap_pairs(yv) * sv).astype(BF16)
            else:
                o_ref[:, cols] = xv.astype(BF16)

    return _call(name, body, (S // ts,),
                 [(proj_b, (ts, W), lambda i: (i, 0)), (gains, (1, W), lambda i: (0, 0)),
                  (cos_t, (ts, LANES), lambda i: (i, 0)), (sin_t, (ts, LANES), lambda i: (i, 0))],
                 [((S, W), BF16, (ts, W), lambda i: (i, 0))],
                 sem=("parallel",))[0]


def qk_prep_bwd(dr, proj_b, col0, gain, cos_t, sin_t, name):
    S, W = dr.shape
    H = W // LANES
    ts = 256
    xb = (col0 * LANES) // W

    def body(d_ref, x_ref, g_ref, c_ref, s_ref, dx_ref, dg_ref):
        i = pl.program_id(0)
        cv, sv, gv = c_ref[...], s_ref[...], g_ref[...]
        dgp = jnp.zeros((1, LANES), F32)
        for hb in range(H):
            cols = slice(hb * LANES, (hb + 1) * LANES)
            dout = d_ref[:, cols]
            dy = dout * cv + _swap_pairs(dout * sv)
            dx, dgt = _rms_bwd_tile(dy, x_ref[:, cols], gv)
            dx_ref[:, cols] = dx.astype(BF16)
            dgp = dgp + jnp.sum(dgt, axis=0, keepdims=True)

        @pl.when(i == 0)
        def _():
            dg_ref[...] = dgp

        @pl.when(i > 0)
        def _():
            dg_ref[...] += dgp

    return _call(name, body, (S // ts,),
                 [(dr, (ts, W), lambda i: (i, 0)), (proj_b, (ts, W), lambda i: (i, xb)),
                  (gain, (1, LANES), lambda i: (0, 0)),
                  (cos_t, (ts, LANES), lambda i: (i, 0)), (sin_t, (ts, LANES), lambda i: (i, 0))],
                 [((S, W), BF16, (ts, W), lambda i: (i, 0)),
                  ((1, LANES), F32, (1, LANES), lambda i: (0, 0))],
                 sem=("arbitrary",))


def _row_sums(x):
    hi = x.astype(BF16)
    lo = (x - hi.astype(F32)).astype(BF16)
    ones = jnp.ones((8, LANES), BF16)
    return (_dot(ones, hi, 1, 1) + _dot(ones, lo, 1, 1))[0:1, :]


def flash_fwd(qkv, name):
    S = qkv.shape[0]
    tq = B_TQ_FWD
    scale = HEAD_DIM_B ** -0.5

    hps = B_HEADS_PER_STEP

    def body(q_ref, k_ref, v_ref, o_ref, l_ref):
        k, v = k_ref[...], v_ref[...]
        for j in range(hps):
            cols = slice(j * LANES, (j + 1) * LANES)
            s = _dot(q_ref[:, cols], k, 1, 1)
            m = jnp.max(s, axis=-1, keepdims=True)
            e = jnp.exp2((s - m) * (scale * LOG2E))
            l = jnp.sum(e, axis=-1, keepdims=True)
            o_ref[:, cols] = (_dot(e.astype(BF16), v) / l).astype(BF16)
            lse = jnp.broadcast_to(m * scale + jnp.log(l), (tq, LANES))
            l_ref[j] = _row_sums(lse) * (1.0 / LANES)

    per = GQA_GROUP_B // hps
    heads = lambda g, h, i: (i, g * per + h)
    return _call(name, body, (N_KV_B, per, S // tq),
                 [(qkv, (tq, hps * LANES), heads),
                  (qkv, (S, LANES), lambda g, h, i: (0, N_HEADS_B + g)),
                  (qkv, (S, LANES), lambda g, h, i: (0, N_HEADS_B + N_KV_B + g))],
                 [((S, N_HEADS_B * LANES), BF16, (tq, hps * LANES), heads),
                  ((N_HEADS_B, 1, S), F32, (hps, 1, tq), lambda g, h, i: (g * per + h, 0, i))],
                 sem=("parallel", "parallel", "parallel"))


def flash_bwd(qkv, k_t, do_b, o_b, lse, name):
    S = qkv.shape[0]
    tq = B_TQ_BWD
    nq = S // tq
    scale = HEAD_DIM_B ** -0.5

    def body(q_ref, k_ref, v_ref, kt_ref, do_ref, o_ref, l_ref, dq_ref, dk_ref, dv_ref, dkacc, dvacc):
        h, i = pl.program_id(1), pl.program_id(2)

        @pl.when((h == 0) & (i == 0))
        def _():
            dkacc[...] = jnp.zeros(dkacc.shape, F32)
            dvacc[...] = jnp.zeros(dvacc.shape, F32)

        q = q_ref[...]
        do = do_ref[...]
        dob = do.astype(BF16)
        t = _row_sums(do * o_ref[...].astype(F32))
        pt = jnp.exp2(_dot(k_ref[...], q, 1, 1) * (scale * LOG2E) - l_ref[...] * LOG2E)
        dsb = (pt * (_dot(v_ref[...], dob, 1, 1) - t)).astype(BF16)
        dvacc[...] += _dot(pt.astype(BF16), dob)
        dkacc[...] += _dot(dsb, q)
        dq_ref[...] = _dot(kt_ref[...], dsb).T * scale

        @pl.when((h == GQA_GROUP_B - 1) & (i == nq - 1))
        def _():
            dk_ref[...] = dkacc[...] * scale
            dv_ref[...] = dvacc[...].astype(BF16)

    head = lambda g, h, i: (i, g * GQA_GROUP_B + h)
    return _call(name, body, (N_KV_B, GQA_GROUP_B, nq),
                 [(qkv, (tq, LANES), head),
                  (qkv, (S, LANES), lambda g, h, i: (0, N_HEADS_B + g)),
                  (qkv, (S, LANES), lambda g, h, i: (0, N_HEADS_B + N_KV_B + g)),
                  (k_t, (LANES, S), lambda g, h, i: (g, 0)),
                  (do_b, (tq, LANES), head), (o_b, (tq, LANES), head),
                  (lse, (None, 1, tq), lambda g, h, i: (g * GQA_GROUP_B + h, 0, i))],
                 [((S, N_HEADS_B * LANES), F32, (tq, LANES), head),
                  ((S, N_KV_B * LANES), F32, (S, LANES), lambda g, h, i: (0, g)),
                  ((S, N_KV_B * LANES), BF16, (S, LANES), lambda g, h, i: (0, g))],
                 scratch=[pltpu.VMEM((S, LANES), F32)] * 2,
                 sem=("parallel", "arbitrary", "arbitrary"))


MERGE_TN = 512


def _mix_rows_spec(Gm, row0, n_slots, slot_map, cols=None, col_map=None):
    C = Gm.shape[2] if cols is None else cols
    cm = (lambda *idx: 0) if col_map is None else col_map
    return (Gm, (n_slots, LANES, C), lambda *idx: (slot_map(*idx), row0 // LANES, cm(*idx)))


def merge_fwd(o_a, o_b, w_a, Gm, proj_b, b_gate, name):
    S = o_a.shape[0]
    D = w_a.shape[1]
    tm, tn = 512, MERGE_TN
    ga0, gb0 = PB_GATE_A // tn, PB_GATE_B // tn

    def body(oa_ref, ob_ref, wa_ref, wb_ref, pa_ref, pb_ref, ba_ref, bb_ref, m_ref, ya_ref, yb_ref):
        ya = _dot(oa_ref[...], wa_ref[...])
        yb = _dot(ob_ref[...], wb_ref[...].reshape(N_DEV * LANES, tn))
        ga = _sigmoid(pa_ref[...] + ba_ref[...])
        gb = _sigmoid(pb_ref[...] + bb_ref[...])
        m_ref[...] = (ga * ya + gb * yb).astype(BF16)
        ya_ref[...] = ya.astype(BF16)
        yb_ref[...] = yb.astype(BF16)

    out = ((S, D), BF16, (tm, tn), lambda j, i: (i, j))
    return _call(name, body, (D // tn, S // tm),
                 [(o_a, (tm, o_a.shape[1]), lambda j, i: (i, 0)), (o_b, (tm, o_b.shape[1]), lambda j, i: (i, 0)),
                  (w_a, (w_a.shape[0], tn), lambda j, i: (0, j)),
                  _mix_rows_spec(Gm, MIX_WB, N_DEV, lambda j, i: 0, cols=tn, col_map=lambda j, i: j),
                  (proj_b, (tm, tn), lambda j, i: (i, ga0 + j)), (proj_b, (tm, tn), lambda j, i: (i, gb0 + j)),
                  (b_gate, (1, tn), lambda j, i: (0, j)), (b_gate, (1, tn), lambda j, i: (0, D // tn + j))],
                 [out, out, out], sem=("parallel", "parallel"))


def out_proj(merged, Gm, x, name):
    S, D = x.shape
    tm, tn = 512, MERGE_TN

    def body(m_ref, w_ref, x_ref, o_ref):
        o_ref[...] = x_ref[...] + _dot(m_ref[...], w_ref[...].reshape(N_DEV * LANES, tn))

    return _call(name, body, (D // tn, S // tm),
                 [(merged, (tm, D), lambda j, i: (i, 0)),
                  _mix_rows_spec(Gm, MIX_WOUT, N_DEV, lambda j, i: 0, cols=tn, col_map=lambda j, i: j),
                  (x, (tm, tn), lambda j, i: (i, j))],
                 [((S, D), F32, (tm, tn), lambda j, i: (i, j))], sem=("parallel", "parallel"))[0]


def merge_bwd(dx2, Gm, ya, yb, proj_b, b_gate, name):
    S, D = dx2.shape
    tm, tn = 512, MERGE_TN
    nn = D // tn
    ga0, gb0 = PB_GATE_A // tn, PB_GATE_B // tn

    def body(d_ref, w_ref, ya_ref, yb_ref, pa_ref, pb_ref, ba_ref, bb_ref, dya_ref, dyb_ref, dg_ref, dbg_ref):
        i = pl.program_id(1)
        dm = _dot(d_ref[...].astype(BF16), w_ref[...].reshape(tn, D), 1, 1)
        ga = _sigmoid(pa_ref[...] + ba_ref[...])
        gb = _sigmoid(pb_ref[...] + bb_ref[...])
        dya_ref[...] = (dm * ga).astype(BF16)
        dyb_ref[...] = (dm * gb).astype(BF16)
        dpa = dm * ya_ref[...].astype(F32) * ga * (1.0 - ga)
        dpb = dm * yb_ref[...].astype(F32) * gb * (1.0 - gb)
        dg_ref[0] = dpa.astype(BF16)
        dg_ref[1] = dpb.astype(BF16)
        sa = jnp.sum(dpa, axis=0, keepdims=True)
        sb = jnp.sum(dpb, axis=0, keepdims=True)

        @pl.when(i == 0)
        def _():
            dbg_ref[0] = sa
            dbg_ref[1] = sb

        @pl.when(i > 0)
        def _():
            dbg_ref[0] += sa
            dbg_ref[1] += sb

    tile = ((tm, tn), lambda j, i: (i, j))
    dya, dyb, dgate, dbg = _call(
        name, body, (nn, S // tm),
        [(dx2, (tm, D), lambda j, i: (i, 0)),
         _mix_rows_spec(Gm, MIX_WOUT, tn // LANES, lambda j, i: j),
         (ya,) + tile, (yb,) + tile,
         (proj_b, (tm, tn), lambda j, i: (i, ga0 + j)), (proj_b, (tm, tn), lambda j, i: (i, gb0 + j)),
         (b_gate, (1, tn), lambda j, i: (0, j)), (b_gate, (1, tn), lambda j, i: (0, nn + j))],
        [((S, D), BF16) + tile, ((S, D), BF16) + tile,
         ((2, S, D), BF16, (2, tm, tn), lambda j, i: (0, i, j)),
         ((2, 1, D), F32, (2, 1, tn), lambda j, i: (0, 0, j))],
        sem=("parallel", "arbitrary"))
    return dya, dyb, dgate, dbg


def matmul_nt(a, b_spec_fn, N, name, tn=512):
    S, K = a.shape
    tm = 512

    def body(a_ref, b_ref, o_ref):
        b = b_ref[...]
        o_ref[...] = _dot(a_ref[...], b.reshape(-1, b.shape[-1]), 1, 1)

    return _call(name, body, (N // tn, S // tm),
                 [(a, (tm, K), lambda j, i: (i, 0)), b_spec_fn(lambda j, i: j)],
                 [((S, N), F32, (tm, tn), lambda j, i: (i, j))], sem=("parallel", "parallel"))[0]


def weight_grad_rows(a, b, grads, row0, name):
    S, M = a.shape
    N = b.shape[1]
    tmm = 512
    tk = WGRAD_TK
    nk = S // tk

    def body(g_ref, a_ref, b_ref, o_ref, acc_ref):
        k = pl.program_id(1)
        p = _dot(a_ref[...], b_ref[...].astype(BF16), 0, 0)

        @pl.when(k == 0)
        def _():
            acc_ref[...] = p

        @pl.when(k > 0)
        def _():
            acc_ref[...] += p

        @pl.when(k == nk - 1)
        def _():
            o_ref[...] = acc_ref[...].astype(BF16).reshape(tmm // LANES, LANES, N)

    return pl.pallas_call(
        body,
        out_shape=jax.ShapeDtypeStruct(grads.shape, BF16),
        grid=(M // tmm, nk),
        in_specs=[pl.BlockSpec(memory_space=pl.ANY),
                  pl.BlockSpec((tk, tmm), lambda j, k: (k, j)),
                  pl.BlockSpec((tk, N), lambda j, k: (k, 0))],
        out_specs=pl.BlockSpec((tmm // LANES, LANES, N), lambda j, k: (j, row0 // LANES, 0)),
        scratch_shapes=[pltpu.VMEM((tmm, N), F32)],
        input_output_aliases={0: 0},
        name=name,
        compiler_params=pltpu.CompilerParams(dimension_semantics=("parallel", "arbitrary"),
                                             vmem_limit_bytes=VMEM_LIMIT),
    )(grads, a, b)


def weight_grad_plain(a, b, name):
    S, M = a.shape
    N = b.shape[1]
    tk = WGRAD_TK
    nk = S // tk

    def body(a_ref, b_ref, o_ref, acc_ref):
        k = pl.program_id(0)
        p = _dot(a_ref[...], b_ref[...], 0, 0)

        @pl.when(k == 0)
        def _():
            acc_ref[...] = p

        @pl.when(k > 0)
        def _():
            acc_ref[...] += p

        @pl.when(k == nk - 1)
        def _():
            o_ref[...] = acc_ref[...].astype(BF16)

    return _call(name, body, (nk,),
                 [(a, (tk, M), lambda k: (k, 0)), (b, (tk, N), lambda k: (k, 0))],
                 [((M, N), BF16, (M, N), lambda k: (0, 0))],
                 scratch=[pltpu.VMEM((M, N), F32)], sem=("arbitrary",))[0]


def local_step(x, tgt, p, get_g1_up, get_g1_down, get_gm, get_g2, emit, start_token):
    S, D = x.shape
    after = lambda t: t[0:1, 0:1]
    buckets = _bucket_tables()
    cos_t, sin_t = _rope_tables(S)
    gains = jnp.concatenate([jnp.tile(p["q_norm"], (1, N_HEADS_B)), jnp.tile(p["k_norm"], (1, N_KV_B)),
                             jnp.ones((1, N_KV_B * LANES), F32)], axis=1)

    n1 = rms_fwd(x, p["ffn1_norm"] + after(start_token), "ffn1_norm")
    bias = bias_build(p["rel_bias"] + after(start_token), buckets)
    g1_up = get_g1_up((n1, bias))
    ab1 = ffn_up(n1, (g1_up, None), "ffn1_up")
    G1 = (g1_up, get_g1_down(ab1))
    x1 = ffn_down(ab1, G1, x, "ffn1_down")

    Gm = get_gm(x1)
    w_a = Gm[:, MIX_WA:MIX_ROWS, :].reshape(N_DEV, GROUP_WIDTH_A, LANES).transpose(1, 0, 2).reshape(GROUP_WIDTH_A, D)
    hm = rms_fwd(x1, p["mix_norm"], "mix_norm")
    n_a = A_QKV_WIDTH // PROJ_TN
    proj_a = [in_proj(hm, Gm, g, 3, BF16, "in_proj_a%d" % g, tile_stride=3) for g in range(3)]
    proj_b = in_proj(hm, Gm, n_a, PB_WIDTH // PROJ_TN, F32, "in_proj_b")

    outs, lses = [], []
    for g in range(3):
        o, l = a_fwd(proj_a[g], bias[g], g, "a_fwd_%d" % g)
        outs.append(o)
        lses.append(l)
    o_a, lse_tot = a_combine(outs, lses, "a_combine")

    qkv = qkv_prep(proj_b, gains, cos_t, sin_t, "qkv_prep")
    k_t = qkv[:, N_HEADS_B * LANES:(N_HEADS_B + N_KV_B) * LANES].T
    o_b, lse_b = flash_fwd(qkv, "flash_fwd")

    merged, ya, yb = merge_fwd(o_a, o_b, w_a, Gm, proj_b, p["b_gate"], "merge_fwd")
    x2 = out_proj(merged, Gm, x1, "out_proj")

    G2 = get_g2(x2)
    n2 = rms_fwd(x2, p["ffn2_norm"], "ffn2_norm")
    ab2 = ffn_up(n2, G2, "ffn2_up")
    x3 = ffn_down(ab2, G2, x2, "ffn2_down")

    loss, dx3, dx3_b, d_final = final_loss(x3, tgt, p["final_norm"], "final_loss")

    dabh2, gw2 = ffn_bwd_weights(dx3_b, ab2, n2, G2, "ffn2_bwd")
    t2 = emit("ffn2", gw2)
    dx2, dx2_b, d_ffn2_norm = ffn_bwd_input(dabh2, G2, x2, p["ffn2_norm"] + after(t2), dx3, "ffn2_bwd")

    dya, dyb, dgate, dbg = merge_bwd(dx2_b, Gm, ya, yb, proj_b, p["b_gate"], "merge_bwd")
    gm_grads = jnp.zeros(Gm.shape, BF16)
    gm_grads = weight_grad_rows(merged, dx2_b, gm_grads, MIX_WOUT, "dw_out")
    gm_grads = weight_grad_rows(o_b, dyb, gm_grads, MIX_WB, "dw_branch_b")
    dw_a = weight_grad_plain(o_a, dya, "dw_branch_a")
    do_a = matmul_nt(dya, lambda jm: (w_a, (MERGE_TN, D), lambda j, i: (jm(j, i), 0)), GROUP_WIDTH_A, "do_a")
    do_b = matmul_nt(dyb, lambda jm: _mix_rows_spec(Gm, MIX_WB, MERGE_TN // LANES, jm), N_HEADS_B * LANES, "do_b")

    dq_r, dk_r, dv_b = flash_bwd(qkv, k_t, do_b, o_b, lse_b, "flash_bwd")
    dq_b, d_q_norm = qk_prep_bwd(dq_r, proj_b, 0, p["q_norm"], cos_t, sin_t, "q_prep_bwd")
    dk_b, d_k_norm = qk_prep_bwd(dk_r, proj_b, N_HEADS_B, p["k_norm"], cos_t, sin_t, "k_prep_bwd")

    dqkv, dbs = [], []
    for g in range(3):
        dg_, db = a_bwd(proj_a[g], bias[g], do_a, o_a, lse_tot, g, "a_bwd_%d" % g)
        dqkv.append(dg_)
        dbs.append(db)
    d_rel_bias = bias_bwd(jnp.stack(dbs, axis=0).reshape(3, HEADS_PER_GROUP_A, A_TQ, A_WIN), buckets)

    dproj = _dproj_pieces(dqkv, dq_b, jnp.concatenate([dk_b, dv_b], axis=1), dgate)
    gm_grads = in_proj_bwd_dw(dproj, hm, gm_grads, "in_proj_bwd")
    dw_a_sh = dw_a.reshape(GROUP_WIDTH_A, N_DEV, LANES).transpose(1, 0, 2).reshape(N_DEV, MIX_ROWS - MIX_WA, D)
    gm_grads = lax.dynamic_update_slice(gm_grads, dw_a_sh, (0, MIX_WA, 0))
    tm = emit("mix", gm_grads)
    dx1, dx1_b, d_mix_norm = in_proj_bwd_dh(dproj, Gm, x1, p["mix_norm"] + after(tm), dx2, "in_proj_bwd")

    dabh1, gw1 = ffn_bwd_weights(dx1_b, ab1, n1, G1, "ffn1_bwd")
    t1 = emit("ffn1", gw1)
    dx0, d_ffn1_norm = ffn_bwd_input(dabh1, G1, x, p["ffn1_norm"] + after(t1), dx1, "ffn1_bwd", as_operand=False)

    small = dict(ffn1_norm=d_ffn1_norm, mix_norm=d_mix_norm, b_gate=dbg.reshape(1, 2 * D),
                 q_norm=d_q_norm, k_norm=d_k_norm, rel_bias=d_rel_bias, ffn2_norm=d_ffn2_norm,
                 final_norm=d_final)
    return loss, dx0, small


def _pack_small(t, loss_row):
    row6 = jnp.concatenate([t["q_norm"].reshape(1, -1), t["k_norm"].reshape(1, -1), t["rel_bias"].reshape(1, -1)], axis=1)
    return jnp.concatenate([t["ffn1_norm"].reshape(1, -1), t["mix_norm"].reshape(1, -1), t["b_gate"].reshape(2, -1),
                            t["ffn2_norm"].reshape(1, -1), t["final_norm"].reshape(1, -1), row6, loss_row], axis=0)


def _unpack_small(a, shapes):
    return dict(ffn1_norm=a[0:1].reshape(shapes["ffn1_norm"]), mix_norm=a[1:2].reshape(shapes["mix_norm"]),
                b_gate=a[2:4].reshape(shapes["b_gate"]), ffn2_norm=a[4:5].reshape(shapes["ffn2_norm"]),
                final_norm=a[5].reshape(shapes["final_norm"]), q_norm=a[6:7, 0:128].reshape(shapes["q_norm"]),
                k_norm=a[6:7, 128:256].reshape(shapes["k_norm"]), rel_bias=a[6, 256:1024].reshape(shapes["rel_bias"]))


SMALL = ("ffn1_norm", "mix_norm", "b_gate", "q_norm", "k_norm", "rel_bias", "ffn2_norm", "final_norm")
ORDER = ("ffn1_norm", "ffn1_w1", "ffn1_w3", "ffn1_w2", "mix_norm", "w_in", "b_gate", "q_norm", "k_norm", "rel_bias",
         "w_branch_a", "w_branch_b", "w_out", "ffn2_norm", "ffn2_w1", "ffn2_w3", "ffn2_w2", "final_norm")


def kernel(x, ffn1_norm, ffn1_w1, ffn1_w3, ffn1_w2, mix_norm, w_in, b_gate, q_norm, k_norm, rel_bias, w_branch_a, w_branch_b, w_out, ffn2_norm, ffn2_w1, ffn2_w3, ffn2_w2, final_norm, loss_target, m_ffn1_norm, m_ffn1_w1, m_ffn1_w3, m_ffn1_w2, m_mix_norm, m_w_in, m_b_gate, m_q_norm, m_k_norm, m_rel_bias, m_w_branch_a, m_w_branch_b, m_w_out, m_ffn2_norm, m_ffn2_w1, m_ffn2_w3, m_ffn2_w2, m_final_norm, v_ffn1_norm, v_ffn1_w1, v_ffn1_w3, v_ffn1_w2, v_mix_norm, v_w_in, v_b_gate, v_q_norm, v_k_norm, v_rel_bias, v_w_branch_a, v_w_branch_b, v_w_out, v_ffn2_norm, v_ffn2_w1, v_ffn2_w3, v_ffn2_w2, v_final_norm):
    args = dict(locals())
    w = {n: args[n] for n in ORDER}
    m = {n: args["m_" + n] for n in ORDER}
    v = {n: args["v_" + n] for n in ORDER}
    D = x.shape[2]

    blocks = (
        ("ffn1_up", lambda t: jnp.concatenate([ffn1_w1[0].T + t, ffn1_w3[0].T + t], axis=0)),
        ("ffn1_down", lambda t: ffn1_w2[0] + t),
        ("mix", lambda t: jnp.concatenate([w_in[0] + t, w_branch_b[0] + t, w_out[0] + t,
                                           w_branch_a[0].reshape(MIX_ROWS - MIX_WA, D) + t], axis=0)),
        ("ffn2", lambda t: jnp.concatenate([ffn2_w1[0].T + t, ffn2_w3[0].T + t, ffn2_w2[0] + t], axis=0)),
    )
    gathers = {}
    start_token = jnp.zeros((8, LANES), F32)
    for tag, make in blocks:
        start = all_gather_start_direct if tag == "ffn2" else all_gather_start
        gathers[tag] = start(make(start_token[0:1, 0:1]).astype(BF16), "all_gather_" + tag + "_start")
        start_token = gathers[tag][4]

    def gathered(tag):
        def get(after):
            if tag == "ffn2":
                return all_gather_place_own(*_split_wait("all_gather_" + tag + "_wait", gathers[tag], N_DEV - 1, after),
                                            "all_gather_" + tag + "_own")
            return all_gather_finish(*_split_wait("all_gather_" + tag + "_wait", gathers[tag], 4, after),
                                     "all_gather_" + tag + "_finish")
        return get

    core = lax.axis_index("c").astype(jnp.int32).reshape(1)
    chip = (2 * lax.axis_index("x") + lax.axis_index("y")).astype(jnp.int32).reshape(1)
    device = 2 * chip + core
    exchanges = {}

    def emit(tag, gw):
        if tag == "ffn1":
            (theirs,) = reduce_scatter_pair([gw], "reduce_scatter_pair_" + tag)
            part = pair_add(gw, theirs, core, "pair_add_" + tag)
            exchanges[tag] = reduce_scatter_start(part, "reduce_scatter_" + tag + "_start")
        else:
            exchanges[tag] = reduce_scatter_start_direct(gw, "reduce_scatter_" + tag + "_start")
        return exchanges[tag][4]

    small_p = dict(ffn1_norm=ffn1_norm, mix_norm=mix_norm, b_gate=b_gate, q_norm=q_norm, k_norm=k_norm,
                   rel_bias=rel_bias, ffn2_norm=ffn2_norm, final_norm=final_norm.reshape(1, D))
    loss_p, grad_x, small_g = local_step(x[0], loss_target[0], small_p, gathered("ffn1_up"), gathered("ffn1_down"),
                                         gathered("mix"), gathered("ffn2"), emit, start_token)

    def landed(tag, after):
        n_others, me = (3, chip) if tag == "ffn1" else (N_DEV - 1, device)
        return tuple(_split_wait("reduce_scatter_" + tag + "_wait", exchanges[tag], n_others, after)) + (me,)

    grads, delta, new_m, new_v = {}, {}, {}, {}

    def finish(n, part, land, me, off, blk, transposed=False):
        shp = w[n].shape
        if transposed:
            to2 = lambda a: a.reshape(shp[-2], shp[-1]).T
            back = lambda a: a.T.reshape(shp)
        else:
            to2 = lambda a: a.reshape(shp[-2], shp[-1])
            back = lambda a: a.reshape(shp)
        res = sum_adamw(part, land, me, off, blk, to2(w[n]), to2(m[n]), to2(v[n]), "update_" + n)
        grads[n], delta[n], new_m[n], new_v[n] = [back(a) for a in res]

    last_token = exchanges["ffn1"][4]
    for tag, after in (("ffn2", last_token), ("ffn1", grad_x)):
        group = landed(tag, after)
        finish(tag + "_w1", *group, 0, FFN_SHARD, transposed=True)
        finish(tag + "_w3", *group, FFN_SHARD, FFN_SHARD, transposed=True)
        finish(tag + "_w2", *group, 2 * FFN_SHARD, FFN_SHARD)
        if tag == "ffn2":
            group_m = landed("mix", last_token)
            finish("w_in", *group_m, MIX_WIN, LANES)
            finish("w_branch_b", *group_m, MIX_WB, LANES)
            finish("w_out", *group_m, MIX_WOUT, LANES)
            grads["w_branch_a"] = sum_landed(*group_m, MIX_WA, MIX_ROWS - MIX_WA, MIX_ROWS - MIX_WA,
                                             "w_branch_a_sum").reshape(w_branch_a.shape)
    loss_row = jnp.pad(loss_p, ((0, 0), (0, D - LANES)))
    smalls = small_all_gather(_pack_small(small_g, loss_row))
    small_sum = sum_slots(smalls, 0, N_DEV, N_DEV, "small_sum")
    small_shapes = {n: w[n].shape for n in SMALL}
    grads.update(_unpack_small(small_sum, small_shapes))
    loss = small_sum[7, 0]

    n = "w_branch_a"
    two_d = lambda a: a.reshape(w[n].shape[-2], w[n].shape[-1])
    d_, m_, v_ = adamw(two_d(w[n]), two_d(grads[n]), two_d(m[n]), two_d(v[n]), "adamw_" + n)
    delta[n], new_m[n], new_v[n] = [a.reshape(w[n].shape) for a in (d_, m_, v_)]
    zero_row = jnp.zeros((1, D), F32)
    pack = lambda t: _pack_small({n: t[n] for n in SMALL}, zero_row)
    d_, m_, v_ = adamw(pack(w), small_sum, pack(m), pack(v), "adamw_small")
    for src, dst in ((d_, delta), (m_, new_m), (v_, new_v)):
        dst.update(_unpack_small(src, small_shapes))

    return (loss, grad_x[None], *[grads[n] for n in ORDER], *[delta[n] for n in ORDER],
            *[new_m[n] for n in ORDER], *[new_v[n] for n in ORDER])
```

```python
import math

import jax
import jax.numpy as jnp
from jax import lax
from jax.experimental import pallas as pl
from jax.experimental.pallas import tpu as pltpu

F32 = jnp.float32
BF16 = jnp.bfloat16
MESH = pl.DeviceIdType.MESH

V7X_VMEM_BYTES = 64 * 1024 * 1024
VMEM_LIMIT = V7X_VMEM_BYTES - 8 * 1024 * 1024
LANES = 128

N_DEV = 8
EPS = 1e-6
NEG_INF = -1e30

DILATIONS = (1, 4, 16)
HALF_WINDOW = 64
HEAD_DIM_A = 64
HEADS_PER_GROUP_A = 8
GROUP_WIDTH_A = 512
A_QKV_WIDTH = 4608
A_GROUP_QKV = A_QKV_WIDTH // 3
A_TQ = 128
A_WIN = A_TQ + 2 * HALF_WINDOW
A_UNROLL = 8
A_SCALE = HEAD_DIM_A ** -0.5
WGRAD_TK = 2048
HEAD_DIM_B = 128
N_HEADS_B = 8
N_KV_B = 2
GQA_GROUP_B = 4
GRID_W = 64
ROPE_THETA = 10000.0
B_TQ_FWD = 256
B_TQ_BWD = 512
B_HEADS_PER_STEP = 4
LOG2E = 1.4426950408889634
N_BUCKETS = 32
MAX_DISTANCE = 1024
PB_WIDTH = 3584
PB_GATE_A = 1536
PB_GATE_B = 2560

ADAM_LR = 0.001
ADAM_B1 = 0.9
ADAM_B2 = 0.999
ADAM_EPS = 1e-08
ADAM_WD = 0.01
ADAM_STEP = 10

FFN_SHARD = 352
MIX_WIN, MIX_WB, MIX_WOUT, MIX_WA = 0, 1024, 1152, 1280
MIX_ROWS = 1344
REST_WB, REST_WOUT, REST_WA, REST_ROWS = 0, 128, 256, 320


def _dot(a, b, ca=1, cb=0):
    return lax.dot_general(a, b, (((ca,), (cb,)), ((), ())), preferred_element_type=F32)


def _call(name, body, grid, ins, outs, scratch=(), sem=None, aliases=None):
    ins = [tuple(i) + (None,) * (4 - len(i)) for i in ins]
    res = pl.pallas_call(
        body,
        out_shape=[jax.ShapeDtypeStruct(s, d) for (s, d, _, _) in outs],
        grid=grid,
        in_specs=[pl.BlockSpec(bs, im, pipeline_mode=pm) for (_, bs, im, pm) in ins],
        out_specs=[pl.BlockSpec(bs, im) for (_, _, bs, im) in outs],
        scratch_shapes=list(scratch),
        name=name,
        input_output_aliases=aliases or {},
        compiler_params=pltpu.CompilerParams(dimension_semantics=sem, vmem_limit_bytes=VMEM_LIMIT),
    )(*[i[0] for i in ins])
    return res


def _sigmoid(x):
    return 0.5 * jnp.tanh(0.5 * x) + 0.5


def _position():
    return lax.axis_index("x"), lax.axis_index("y"), lax.axis_index("c")


def _hbm_specs(n):
    return [pl.BlockSpec(memory_space=pl.ANY) for _ in range(n)]


PAIR_BUFFERS = 4


def reduce_scatter_pair(grads, name):
    n = len(grads)
    C = grads[0].shape[2]
    half = [g.shape[1] // 2 for g in grads]
    chunks = [(i, q, hf) for i in range(n) for q in range(4) for hf in range(2)]
    nb = PAIR_BUFFERS

    def body(*refs):
        ins, theirs = refs[:n], refs[n:2 * n]
        buf, load_sems, send_sems, recv_sems = refs[2 * n:]
        x, y, c = _position()
        sibling = (x, y, 1 - c)

        def load(k):
            i, q, hf = chunks[k]
            r = half[i]
            return pltpu.make_async_copy(ins[i].at[2 * q + (1 - c), pl.ds(hf * r, r), :],
                                         buf.at[k % nb, pl.ds(0, r), :], load_sems.at[k % nb])

        def send(k):
            i, q, hf = chunks[k]
            r = half[i]
            return pltpu.make_async_remote_copy(
                src_ref=buf.at[k % nb, pl.ds(0, r), :], dst_ref=theirs[i].at[q, pl.ds(hf * r, r), :],
                send_sem=send_sems.at[k % nb], recv_sem=recv_sems.at[i],
                device_id=sibling, device_id_type=MESH)

        for k in range(len(chunks) + 1):
            if k < len(chunks):
                if k >= nb:
                    send(k - nb).wait_send()
                load(k).start()
            if k >= 1:
                load(k - 1).wait()
                send(k - 1).start()
        for k in range(max(0, len(chunks) - nb), len(chunks)):
            send(k).wait_send()
        for i in range(n):
            pltpu.make_async_remote_copy(
                src_ref=theirs[i], dst_ref=theirs[i], send_sem=send_sems.at[0], recv_sem=recv_sems.at[i],
                device_id=sibling, device_id_type=MESH).wait_recv()

    return pl.pallas_call(
        body,
        out_shape=[jax.ShapeDtypeStruct((4,) + g.shape[1:], g.dtype) for g in grads],
        in_specs=_hbm_specs(n),
        out_specs=_hbm_specs(n),
        scratch_shapes=[pltpu.VMEM((nb, max(half), C), grads[0].dtype), pltpu.SemaphoreType.DMA((nb,)),
                        pltpu.SemaphoreType.DMA((nb,)), pltpu.SemaphoreType.DMA((n,))],
        name=name,
        compiler_params=pltpu.CompilerParams(vmem_limit_bytes=VMEM_LIMIT),
    )(*grads)


_HBM_SPEC = pl.BlockSpec(memory_space=pltpu.HBM)
_SEM_SPEC = pl.BlockSpec(memory_space=pltpu.SEMAPHORE)
_TOKEN_SPEC = pl.BlockSpec(memory_space=pltpu.VMEM)
_DATAFLOW = pltpu.SideEffectType.DATAFLOW_SIDE_EFFECTING


def _split_start(name, body, src, land_shape):
    def full_body(src_ref, land_ref, send_sem, recv_sem, src_thru, land_thru, token):
        body(src_ref, land_ref, send_sem, recv_sem)
        token[...] = jnp.zeros_like(token)

    land = pltpu.with_memory_space_constraint(lax.empty(land_shape, src.dtype), pltpu.HBM)
    return pl.pallas_call(
        full_body, name=name,
        out_shape=(pltpu.SemaphoreType.DMA(()), pltpu.SemaphoreType.DMA(()),
                   pltpu.HBM(src.shape, src.dtype), pltpu.HBM(land_shape, src.dtype),
                   jax.ShapeDtypeStruct((8, LANES), F32)),
        in_specs=(_HBM_SPEC, _HBM_SPEC),
        out_specs=(_SEM_SPEC, _SEM_SPEC, _HBM_SPEC, _HBM_SPEC, _TOKEN_SPEC),
        input_output_aliases={0: 2, 1: 3},
        compiler_params=pltpu.CompilerParams(has_side_effects=_DATAFLOW),
    )(pltpu.with_memory_space_constraint(src, pltpu.HBM), land)


def _split_wait(name, started, n_blocks, after):
    send_sem, recv_sem, src_thru, land_thru, _ = started
    after = after if isinstance(after, tuple) else (after,)

    def body(src_ref, land_ref, send_sem, recv_sem, *rest):
        x, y, c = _position()
        blocks = land_ref.at[pl.ds(0, n_blocks)]
        copy = pltpu.make_async_remote_copy(src_ref=blocks, dst_ref=blocks, send_sem=send_sem, recv_sem=recv_sem,
                                            device_id=(x, y, c), device_id_type=MESH)
        copy.wait_send()
        copy.wait_recv()

    return pl.pallas_call(
        body, name=name,
        out_shape=(pltpu.HBM(src_thru.shape, src_thru.dtype), pltpu.HBM(land_thru.shape, land_thru.dtype)),
        in_specs=(_HBM_SPEC, _HBM_SPEC, _SEM_SPEC, _SEM_SPEC) + (pl.BlockSpec(memory_space=pl.ANY),) * len(after),
        out_specs=(_HBM_SPEC, _HBM_SPEC),
        input_output_aliases={0: 0, 1: 1},
        compiler_params=pltpu.CompilerParams(has_side_effects=_DATAFLOW),
    )(src_thru, land_thru, send_sem, recv_sem, *after)


def all_gather_start(block, name):
    def body(b_ref, land_ref, send_sem, recv_sem):
        x, y, c = _position()
        for peer in [(x, y, 1 - c), (1 - x, y, c), (x, 1 - y, c), (1 - x, 1 - y, c)]:
            pltpu.make_async_remote_copy(src_ref=b_ref, dst_ref=land_ref.at[4 * x + 2 * y + c],
                                         send_sem=send_sem, recv_sem=recv_sem,
                                         device_id=peer, device_id_type=MESH).start()

    return _split_start(name, body, block, (N_DEV,) + block.shape)


def all_gather_finish(block, land, name):
    R, C = block.shape

    def body(b_ref, land_in, land_ref, stage, load_sems, send_sems, recv_sems, own_sem):
        x, y, c = _position()
        sibling = (x, y, 1 - c)
        chips = [(1 - x, y), (x, 1 - y), (1 - x, 1 - y)]
        own_in = pltpu.make_async_copy(b_ref, stage.at[3], load_sems.at[3])
        own_in.start()
        loads = [pltpu.make_async_copy(land_in.at[4 * px + 2 * py + c], stage.at[j], load_sems.at[j])
                 for j, (px, py) in enumerate(chips)]
        for ld in loads:
            ld.start()
        sends = []
        for j, (px, py) in enumerate(chips):
            loads[j].wait()
            dst = land_ref.at[4 * px + 2 * py + c]
            cp = pltpu.make_async_remote_copy(src_ref=stage.at[j], dst_ref=dst, send_sem=send_sems.at[j],
                                              recv_sem=recv_sems.at[j], device_id=sibling, device_id_type=MESH)
            cp.start()
            sends.append(cp)
        own_in.wait()
        own_out = pltpu.make_async_copy(stage.at[3], land_ref.at[4 * x + 2 * y + c], own_sem)
        own_out.start()
        for j, (px, py) in enumerate(chips):
            dst = land_ref.at[4 * px + 2 * py + (1 - c)]
            pltpu.make_async_remote_copy(src_ref=stage.at[j], dst_ref=dst, send_sem=send_sems.at[j],
                                         recv_sem=recv_sems.at[j], device_id=sibling,
                                         device_id_type=MESH).wait_recv()
        for cp in sends:
            cp.wait_send()
        own_out.wait()

    return pl.pallas_call(
        body,
        out_shape=jax.ShapeDtypeStruct(land.shape, land.dtype),
        in_specs=_hbm_specs(2),
        out_specs=pl.BlockSpec(memory_space=pl.ANY),
        scratch_shapes=[pltpu.VMEM((4, R, C), block.dtype), pltpu.SemaphoreType.DMA((4,)),
                        pltpu.SemaphoreType.DMA((3,)), pltpu.SemaphoreType.DMA((3,)), pltpu.SemaphoreType.DMA],
        input_output_aliases={1: 0},
        name=name,
        compiler_params=pltpu.CompilerParams(vmem_limit_bytes=VMEM_LIMIT),
    )(block, land)


def reduce_scatter_start(parts, name):
    def body(p_ref, land_ref, send_sem, recv_sem):
        x, y, c = _position()
        for px, py in [(1 - x, y), (x, 1 - y), (1 - x, 1 - y)]:
            pltpu.make_async_remote_copy(src_ref=p_ref.at[2 * px + py], dst_ref=land_ref.at[2 * x + y],
                                         send_sem=send_sem, recv_sem=recv_sem,
                                         device_id=(px, py, c), device_id_type=MESH).start()

    return _split_start(name, body, parts, parts.shape)


def _other_devices(x, y, c):
    return [(1 - x if k & 4 else x, 1 - y if k & 2 else y, 1 - c if k & 1 else c) for k in range(1, N_DEV)]


def all_gather_start_direct(block, name):
    def body(b_ref, land_ref, send_sem, recv_sem):
        x, y, c = _position()
        for peer in _other_devices(x, y, c):
            pltpu.make_async_remote_copy(src_ref=b_ref, dst_ref=land_ref.at[4 * x + 2 * y + c],
                                         send_sem=send_sem, recv_sem=recv_sem,
                                         device_id=peer, device_id_type=MESH).start()

    return _split_start(name, body, block, (N_DEV,) + block.shape)


def all_gather_place_own(block, land, name):
    R, C = block.shape

    def body(b_ref, land_in, land_ref, stage, sems):
        x, y, c = _position()
        load = pltpu.make_async_copy(b_ref, stage, sems.at[0])
        load.start()
        load.wait()
        store = pltpu.make_async_copy(stage, land_ref.at[4 * x + 2 * y + c], sems.at[1])
        store.start()
        store.wait()

    return pl.pallas_call(
        body,
        out_shape=jax.ShapeDtypeStruct(land.shape, land.dtype),
        in_specs=_hbm_specs(2),
        out_specs=pl.BlockSpec(memory_space=pl.ANY),
        scratch_shapes=[pltpu.VMEM((R, C), block.dtype), pltpu.SemaphoreType.DMA((2,))],
        input_output_aliases={1: 0},
        name=name,
    )(block, land)


def reduce_scatter_start_direct(grads, name):
    def body(g_ref, land_ref, send_sem, recv_sem):
        x, y, c = _position()
        for px, py, pc in _other_devices(x, y, c):
            pltpu.make_async_remote_copy(src_ref=g_ref.at[4 * px + 2 * py + pc],
                                         dst_ref=land_ref.at[4 * x + 2 * y + c],
                                         send_sem=send_sem, recv_sem=recv_sem,
                                         device_id=(px, py, pc), device_id_type=MESH).start()

    return _split_start(name, body, grads, grads.shape)


def small_all_gather(small):
    def body(small_ref, smalls, s_send, s_recv, s_local):
        x, y, c = _position()
        me = 4 * x + 2 * y + c
        lc = pltpu.make_async_copy(small_ref, smalls.at[me], s_local)
        lc.start()
        remote = []
        k = 0
        for dx in (0, 1):
            for dy in (0, 1):
                for dc in (0, 1):
                    if dx + dy + dc == 0:
                        continue
                    peer = (1 - x if dx else x, 1 - y if dy else y, 1 - c if dc else c)
                    rc = pltpu.make_async_remote_copy(
                        src_ref=small_ref, dst_ref=smalls.at[me],
                        send_sem=s_send.at[k], recv_sem=s_recv.at[k],
                        device_id=peer, device_id_type=MESH)
                    rc.start()
                    remote.append(rc)
                    k += 1
        for rc in remote:
            rc.wait()
        lc.wait()

    return pl.pallas_call(
        body,
        out_shape=jax.ShapeDtypeStruct((N_DEV,) + small.shape, small.dtype),
        in_specs=_hbm_specs(1),
        out_specs=pl.BlockSpec(memory_space=pl.ANY),
        scratch_shapes=[pltpu.SemaphoreType.DMA((7,)), pltpu.SemaphoreType.DMA((7,)), pltpu.SemaphoreType.DMA],
        name="small_all_gather",
    )(small)


def pair_add(grads, theirs, core, name):
    _, R, C = theirs.shape
    tr = R // 2

    def body(c_ref, a_ref, b_ref, o_ref):
        o_ref[...] = (a_ref[...].astype(F32) + b_ref[...].astype(F32)).astype(BF16)

    return pl.pallas_call(
        body,
        out_shape=jax.ShapeDtypeStruct(theirs.shape, BF16),
        grid_spec=pltpu.PrefetchScalarGridSpec(
            num_scalar_prefetch=1, grid=(4, R // tr),
            in_specs=[pl.BlockSpec((None, tr, C), lambda q, i, c: (2 * q + c[0], i, 0)),
                      pl.BlockSpec((None, tr, C), lambda q, i, c: (q, i, 0))],
            out_specs=pl.BlockSpec((None, tr, C), lambda q, i, c: (q, i, 0))),
        name=name,
        compiler_params=pltpu.CompilerParams(dimension_semantics=("parallel", "parallel"),
                                             vmem_limit_bytes=VMEM_LIMIT),
    )(core, grads, theirs)


def sum_slots(recv, off, rows, blk, name):
    nq, _, C = recv.shape
    ob = off // blk

    def body(r_ref, o_ref):
        acc = r_ref[0].astype(F32)
        for q in range(1, nq):
            acc = acc + r_ref[q].astype(F32)
        o_ref[...] = acc

    return _call(name, body, (rows // blk,),
                 [(recv, (nq, blk, C), lambda i: (0, ob + i, 0))],
                 [((rows, C), F32, (blk, C), lambda i: (i, 0))], sem=("parallel",))[0]


def _sum_terms(refs):
    acc = refs[0][...].astype(F32)
    for r in refs[1:]:
        acc = acc + r[...].astype(F32)
    return acc


def sum_landed(own, land, me, off, rows, blk, name):
    n, _, C = land.shape
    ob = off // blk

    def body(c_ref, *refs):
        refs[n][...] = _sum_terms(refs[:n])

    def entry(flip):
        return pl.BlockSpec((None, blk, C), lambda i, c: (c[0] ^ flip, ob + i, 0))

    return pl.pallas_call(
        body,
        out_shape=jax.ShapeDtypeStruct((rows, C), F32),
        grid_spec=pltpu.PrefetchScalarGridSpec(
            num_scalar_prefetch=1, grid=(rows // blk,),
            in_specs=[entry(k) for k in range(n)],
            out_specs=pl.BlockSpec((blk, C), lambda i, c: (i, 0))),
        name=name,
        compiler_params=pltpu.CompilerParams(dimension_semantics=("parallel",), vmem_limit_bytes=VMEM_LIMIT),
    )(me, own, *([land] * (n - 1)))


def _adamw_update(wv, gv, mv, vv):
    nm = ADAM_B1 * mv + (1.0 - ADAM_B1) * gv
    nv = ADAM_B2 * vv + (1.0 - ADAM_B2) * (gv * gv)
    c1 = 1.0 / (1.0 - ADAM_B1 ** ADAM_STEP)
    c2 = 1.0 / (1.0 - ADAM_B2 ** ADAM_STEP)
    return -ADAM_LR * ((nm * c1) / (jnp.sqrt(nv * c2) + ADAM_EPS) + ADAM_WD * wv), nm, nv


def sum_adamw(own, land, me, off, blk, w, m, v, name):
    rows, C = w.shape
    n = land.shape[0]
    ob = off // blk

    def body(c_ref, *refs):
        w_ref, m_ref, v_ref, g_out, d_out, m_out, v_out = refs[n:]
        gv = _sum_terms(refs[:n])
        g_out[...] = gv
        d_out[...], m_out[...], v_out[...] = _adamw_update(w_ref[...], gv, m_ref[...], v_ref[...])

    def entry(flip):
        return pl.BlockSpec((None, blk, C), lambda i, c: (c[0] ^ flip, ob + i, 0))

    plain = pl.BlockSpec((blk, C), lambda i, c: (i, 0))
    return pl.pallas_call(
        body,
        out_shape=[jax.ShapeDtypeStruct((rows, C), F32)] * 4,
        grid_spec=pltpu.PrefetchScalarGridSpec(
            num_scalar_prefetch=1, grid=(rows // blk,),
            in_specs=[entry(k) for k in range(n)] + [plain, plain, plain],
            out_specs=[plain] * 4),
        name=name,
        compiler_params=pltpu.CompilerParams(dimension_semantics=("parallel",), vmem_limit_bytes=VMEM_LIMIT),
    )(me, own, *([land] * (n - 1)), w, m, v)


def adamw(w, g, m, v, name):
    R, C = w.shape
    tr = R
    for cand in (256, 128, 64, 32, 16, 8):
        if R % cand == 0 and R > cand:
            tr = cand
            break

    def body(w_ref, g_ref, m_ref, v_ref, d_ref, nm_ref, nv_ref):
        d_ref[...], nm_ref[...], nv_ref[...] = _adamw_update(w_ref[...], g_ref[...], m_ref[...], v_ref[...])

    spec = ((tr, C), lambda i: (i, 0))
    out = ((R, C), F32) + spec
    return _call(name, body, (R // tr,), [(w,) + spec, (g,) + spec, (m,) + spec, (v,) + spec],
                 [out, out, out], sem=("parallel",))


def rms_fwd(x, g, name):
    S, D = x.shape
    tr = 512

    def body(x_ref, g_ref, o_ref):
        xv = x_ref[...]
        r = lax.rsqrt(jnp.mean(xv * xv, axis=-1, keepdims=True) + EPS)
        o_ref[...] = (xv * r * g_ref[...]).astype(BF16)

    return _call(name, body, (S // tr,),
                 [(x, (tr, D), lambda i: (i, 0)), (g, (1, D), lambda i: (0, 0))],
                 [((S, D), BF16, (tr, D), lambda i: (i, 0))], sem=("parallel",))[0]


def _rms_bwd_tile(dn, xv, gv):
    r = lax.rsqrt(jnp.mean(xv * xv, axis=-1, keepdims=True) + EPS)
    xh = xv * r
    dxh = dn * gv
    dx = r * (dxh - xh * jnp.mean(dxh * xh, axis=-1, keepdims=True))
    return dx, dn * xh


def final_loss(x, tgt, g, name):
    S, D = x.shape
    tr = 256

    def body(x_ref, t_ref, g_ref, l_ref, dx_ref, dxb_ref, dg_ref):
        i = pl.program_id(0)
        xv, gv = x_ref[...], g_ref[...]
        r = lax.rsqrt(jnp.mean(xv * xv, axis=-1, keepdims=True) + EPS)
        xh = xv * r
        e = xh * gv - t_ref[...]
        part = 0.5 * jnp.sum(jnp.sum(e * e, axis=-1, keepdims=True) * (1.0 / D), axis=0, keepdims=True)
        dy = e * (1.0 / D)
        dxh = dy * gv
        dx = r * (dxh - xh * jnp.mean(dxh * xh, axis=-1, keepdims=True))
        dx_ref[...] = dx
        dxb_ref[...] = dx.astype(BF16)
        dgp = jnp.sum(dy * xh, axis=0, keepdims=True)

        @pl.when(i == 0)
        def _():
            l_ref[...] = jnp.broadcast_to(part, l_ref.shape)
            dg_ref[...] = dgp

        @pl.when(i > 0)
        def _():
            l_ref[...] += jnp.broadcast_to(part, l_ref.shape)
            dg_ref[...] += dgp

    row = ((tr, D), lambda i: (i, 0))
    return _call(name, body, (S // tr,),
                 [(x,) + row, (tgt,) + row, (g, (1, D), lambda i: (0, 0))],
                 [((1, LANES), F32, (1, LANES), lambda i: (0, 0)), ((S, D), F32) + row, ((S, D), BF16) + row,
                  ((1, D), F32, (1, D), lambda i: (0, 0))], sem=("arbitrary",))


FFN_TF = 4 * FFN_SHARD


def _ffn_pick(G, which):
    if isinstance(G, tuple):
        return (G[0], which) if which < 2 else (G[1], 0)
    return G, which


def _ffn_w_spec(G, which, imap):
    arr, blk = _ffn_pick(G, which)
    return (arr, (4, FFN_SHARD, arr.shape[2]), lambda *idx: (imap(*idx), blk, 0))


def _ffn_whole_w_spec(G, which):
    arr, blk = _ffn_pick(G, which)
    return (arr, (N_DEV, FFN_SHARD, arr.shape[2]), lambda *idx: (0, blk, 0))


def ffn_up(n, G, name):
    S, D = n.shape
    F = N_DEV * FFN_SHARD
    tm = 1024

    def body(n_ref, w1_ref, w3_ref, abh_ref):
        nv = n_ref[...]
        a = _dot(nv, w1_ref[...].reshape(FFN_TF, D), 1, 1).astype(BF16)
        b = _dot(nv, w3_ref[...].reshape(FFN_TF, D), 1, 1).astype(BF16)
        abh_ref[0] = a
        abh_ref[1] = b
        av, bv = a.astype(F32), b.astype(F32)
        abh_ref[2] = (av * _sigmoid(av) * bv).astype(BF16)

    return _call(name, body, (F // FFN_TF, S // tm),
                 [(n, (tm, D), lambda j, i: (i, 0)),
                  _ffn_w_spec(G, 0, lambda j, i: j), _ffn_w_spec(G, 1, lambda j, i: j)],
                 [((3, S, F), BF16, (3, tm, FFN_TF), lambda j, i: (0, i, j))],
                 sem=("parallel", "parallel"))[0]


def ffn_down(abh, G, x, name):
    _, S, F = abh.shape
    D = x.shape[1]
    tm = 512

    def body(h_ref, w2_ref, x_ref, o_ref):
        o_ref[...] = x_ref[...] + 0.5 * _dot(h_ref[...], w2_ref[...].reshape(F, D))

    return _call(name, body, (S // tm,),
                 [(abh, (None, tm, F), lambda i: (2, i, 0)), _ffn_whole_w_spec(G, 2),
                  (x, (tm, D), lambda i: (i, 0))],
                 [((S, D), F32, (tm, D), lambda i: (i, 0))], sem=("parallel",))[0]


def ffn_bwd_weights(dxo, abh, n, G, name):
    _, S, F = abh.shape
    D = dxo.shape[1]
    tm = 512
    nf = F // FFN_TF

    def down_body(d_ref, w2_ref, ab_ref, o_ref):
        dh = 0.5 * _dot(d_ref[...].astype(BF16), w2_ref[...].reshape(FFN_TF, D), 1, 1)
        av, bv = ab_ref[0].astype(F32), ab_ref[1].astype(F32)
        sig = _sigmoid(av)
        o_ref[0] = (dh * bv * (sig * (1.0 + av * (1.0 - sig)))).astype(BF16)
        o_ref[1] = (dh * (av * sig)).astype(BF16)

    dab = _call(name + "_down_bwd", down_body, (nf, S // tm),
                [(dxo, (tm, D), lambda j, i: (i, 0)), _ffn_w_spec(G, 2, lambda j, i: j),
                 (abh, (2, tm, FFN_TF), lambda j, i: (0, i, j))],
                [((2, S, F), BF16, (2, tm, FFN_TF), lambda j, i: (0, i, j))],
                sem=("parallel", "parallel"))[0]

    tk = WGRAD_TK
    nk = S // tk
    gshape = (N_DEV, 3 * FFN_SHARD, D)

    def dw2_body(h_ref, d_ref, o_ref, acc_ref):
        k = pl.program_id(1)
        p = _dot(h_ref[...], d_ref[...].astype(BF16), 0, 0)

        @pl.when(k == 0)
        def _():
            acc_ref[...] = p

        @pl.when(k > 0)
        def _():
            acc_ref[...] += p

        @pl.when(k == nk - 1)
        def _():
            o_ref[...] = (0.5 * acc_ref[...]).astype(BF16).reshape(4, FFN_SHARD, D)

    gw = _call(name + "_dw2", dw2_body, (nf, nk),
               [(abh, (None, tk, FFN_TF), lambda j, k: (2, k, j)), (dxo, (tk, D), lambda j, k: (k, 0))],
               [(gshape, BF16, (4, FFN_SHARD, D), lambda j, k: (j, 2, 0))],
               scratch=[pltpu.VMEM((FFN_TF, D), F32)], sem=("parallel", "arbitrary"))[0]

    def dw13_body(gw_ref, dab_ref, n_ref, o_ref):
        o_ref[...] = _dot(dab_ref[...], n_ref[...], 0, 0).astype(BF16).reshape(4, FFN_SHARD, D)

    gw = pl.pallas_call(
        dw13_body,
        out_shape=jax.ShapeDtypeStruct(gshape, BF16),
        grid=(2, nf),
        in_specs=[pl.BlockSpec(memory_space=pl.ANY),
                  pl.BlockSpec((None, S, FFN_TF), lambda w, j: (w, 0, j)),
                  pl.BlockSpec((S, D), lambda w, j: (0, 0))],
        out_specs=pl.BlockSpec((4, FFN_SHARD, D), lambda w, j: (j, w, 0)),
        input_output_aliases={0: 0},
        name=name + "_dw13",
        compiler_params=pltpu.CompilerParams(dimension_semantics=("parallel", "parallel"),
                                             vmem_limit_bytes=VMEM_LIMIT),
    )(gw, dab, n)
    return dab, gw


def ffn_bwd_input(dab, G, x_in, g, dxo, name, as_operand=True):
    _, S, F = dab.shape
    D = x_in.shape[1]
    tm = 256

    def dn_body(dab_ref, w1_ref, w3_ref, x_ref, d_ref, g_ref, dx_ref, *rest):
        dg_ref = rest[-1]
        i = pl.program_id(0)
        dn = _dot(dab_ref[0], w1_ref[...].reshape(F, D)) + _dot(dab_ref[1], w3_ref[...].reshape(F, D))
        dx, dgt = _rms_bwd_tile(dn, x_ref[...], g_ref[...])
        dx = d_ref[...] + dx
        dx_ref[...] = dx
        if as_operand:
            rest[0][...] = dx.astype(BF16)
        dgp = jnp.sum(dgt, axis=0, keepdims=True)

        @pl.when(i == 0)
        def _():
            dg_ref[...] = dgp

        @pl.when(i > 0)
        def _():
            dg_ref[...] += dgp

    tile = ((tm, D), lambda i: (i, 0))
    return _call(name + "_dn", dn_body, (S // tm,),
                 [(dab, (2, tm, F), lambda i: (0, i, 0)),
                  _ffn_whole_w_spec(G, 0), _ffn_whole_w_spec(G, 1),
                  (x_in,) + tile, (dxo,) + tile, (g, (1, D), lambda i: (0, 0))],
                 [((S, D), F32) + tile] + ([((S, D), BF16) + tile] if as_operand else [])
                 + [((1, D), F32, (1, D), lambda i: (0, 0))],
                 sem=("arbitrary",))


PROJ_TN = 512
DH_SHARDS_PER_STEP = 4


def in_proj(h, Gm, first_tile, n_tiles, dtype, name, tile_stride=1):
    S, D = h.shape
    tile = lambda j: first_tile + tile_stride * j

    def body(h_ref, w_ref, o_ref):
        o_ref[...] = _dot(h_ref[...], w_ref[...]).astype(dtype)

    return _call(name, body, (n_tiles,),
                 [(h, (S, D), lambda j: (0, 0)),
                  (Gm, (None, D, PROJ_TN), lambda j: (tile(j) // 2, 0, tile(j) % 2))],
                 [((S, n_tiles * PROJ_TN), dtype, (S, PROJ_TN), lambda j: (0, j))],
                 sem=("parallel",))[0]


def _dproj_pieces(dqkv, dq_b, dkv_b, dgate):
    pieces = [(dqkv[g], [(3 * which + g, (which, 0)) for which in range(3)]) for g in range(3)]
    pieces.append((dq_b, [(9, (None, 0)), (10, (None, 1))]))
    pieces.append((dkv_b, [(11, (None, 0))]))
    pieces.append((dgate, [(12 + 2 * a + b, (a, b)) for a in range(2) for b in range(2)]))
    return pieces


def in_proj_bwd_dw(pieces, h, gm_grads, name):
    S, D = h.shape

    for n_piece, (arr, tiles) in enumerate(pieces):
        w_tile = [t for t, _ in tiles]
        lead = [ix[0] for _, ix in tiles]
        colb = [ix[1] for _, ix in tiles]

        def pick(table, j):
            out = table[-1]
            for k in range(len(table) - 2, -1, -1):
                out = jnp.where(j == k, table[k], out)
            return out

        def dw_body(gm_ref, h_ref, d_ref, o_ref):
            o_ref[...] = _dot(h_ref[...], d_ref[...], 0, 0).astype(BF16)

        if arr.ndim == 3:
            d_spec = pl.BlockSpec((None, S, PROJ_TN), lambda j, lead=lead, colb=colb: (pick(lead, j), 0, pick(colb, j)))
        else:
            d_spec = pl.BlockSpec((S, PROJ_TN), lambda j, colb=colb: (0, pick(colb, j)))
        gm_grads = pl.pallas_call(
            dw_body,
            out_shape=jax.ShapeDtypeStruct(gm_grads.shape, BF16),
            grid=(len(tiles),),
            in_specs=[pl.BlockSpec(memory_space=pl.ANY), pl.BlockSpec((S, D), lambda j: (0, 0)), d_spec],
            out_specs=pl.BlockSpec((None, D, PROJ_TN),
                                   lambda j, w_tile=w_tile: (pick(w_tile, j) // 2, 0, pick(w_tile, j) % 2)),
            input_output_aliases={0: 0},
            name="%s_dw%d" % (name, n_piece),
            compiler_params=pltpu.CompilerParams(dimension_semantics=("parallel",), vmem_limit_bytes=VMEM_LIMIT),
        )(gm_grads, h, arr)
    return gm_grads


def in_proj_bwd_dh(pieces, Gm, x_in, g, dres, name):
    S, D = x_in.shape
    tm = 256
    C = Gm.shape[2]
    n_sh = N_DEV
    n_p = len(pieces)

    def dh_body(*refs):
        d_refs = refs[:n_p]
        w_ref, x_ref, r_ref, g_ref, dx_ref, dxb_ref, dg_ref = refs[n_p:]
        i = pl.program_id(0)
        p = None
        for d_ref, (arr, tiles) in zip(d_refs, pieces):
            for t, (lead, colb) in tiles:
                cols = slice(colb * PROJ_TN, (colb + 1) * PROJ_TN)
                d = d_ref[:, cols] if lead is None else d_ref[lead, :, cols]
                wcol = (t % 2) * PROJ_TN
                term = _dot(d, w_ref[t // 2, :, wcol:wcol + PROJ_TN], 1, 1)
                p = term if p is None else p + term
        dx, dgt = _rms_bwd_tile(p, x_ref[...], g_ref[...])
        dx = r_ref[...] + dx
        dx_ref[...] = dx
        dxb_ref[...] = dx.astype(BF16)
        dgp = jnp.sum(dgt, axis=0, keepdims=True)

        @pl.when(i == 0)
        def _():
            dg_ref[...] = dgp

        @pl.when(i > 0)
        def _():
            dg_ref[...] += dgp

    tile = ((tm, D), lambda i: (i, 0))

    def rows_of(arr):
        if arr.ndim == 3:
            return (arr, (arr.shape[0], tm, arr.shape[2]), lambda i: (0, i, 0))
        return (arr, (tm, arr.shape[1]), lambda i: (i, 0))

    return _call(name + "_dh", dh_body, (S // tm,),
                 [rows_of(arr) for arr, _ in pieces]
                 + [(Gm, (n_sh, D, C), lambda i: (0, 0, 0), pl.Buffered(1)),
                    (x_in,) + tile, (dres,) + tile, (g, (1, D), lambda i: (0, 0))],
                 [((S, D), F32) + tile, ((S, D), BF16) + tile, ((1, D), F32, (1, D), lambda i: (0, 0))],
                 sem=("arbitrary",))


def _t5_bucket(rel):
    n = N_BUCKETS // 2
    max_exact = n // 2
    ret = jnp.where(rel > 0, n, 0)
    a = jnp.abs(rel)
    af = jnp.maximum(a, 1).astype(F32)
    large = max_exact + (jnp.log(af / max_exact) / math.log(MAX_DISTANCE / max_exact)
                         * (n - max_exact)).astype(jnp.int32)
    large = jnp.minimum(large, n - 1)
    return ret + jnp.where(a < max_exact, a, large)


def _bucket_tables():
    qi = jnp.arange(A_TQ, dtype=jnp.int32)[:, None]
    kj = jnp.arange(A_WIN, dtype=jnp.int32)[None, :]
    rel = kj - HALF_WINDOW - qi
    return jnp.stack([_t5_bucket(rel * d) for d in DILATIONS], axis=0)


def bias_build(rel_bias, buckets):
    def body(tab_ref, bk_ref, o_ref):
        col = pl.program_id(0) * HEADS_PER_GROUP_A + pl.program_id(1)
        bk = bk_ref[...]
        acc = jnp.zeros(bk.shape, F32)
        for b in range(N_BUCKETS):
            acc = jnp.where(bk == b, tab_ref[b, col], acc)
        qi = lax.broadcasted_iota(jnp.int32, bk.shape, 0)
        kj = lax.broadcasted_iota(jnp.int32, bk.shape, 1)
        band = jnp.where(jnp.abs(kj - HALF_WINDOW - qi) <= HALF_WINDOW, acc, NEG_INF)
        o_ref[0] = jnp.where(kj >= HALF_WINDOW, band, NEG_INF)
        o_ref[1] = band
        o_ref[2] = jnp.where(kj < A_TQ + HALF_WINDOW, band, NEG_INF)

    out = pl.pallas_call(
        body,
        out_shape=jax.ShapeDtypeStruct((3, HEADS_PER_GROUP_A // 2, 3, 2, A_TQ, A_WIN), F32),
        grid=(3, HEADS_PER_GROUP_A),
        in_specs=[pl.BlockSpec(memory_space=pltpu.SMEM),
                  pl.BlockSpec((None, A_TQ, A_WIN), lambda g, h: (g, 0, 0))],
        out_specs=pl.BlockSpec((None, None, 3, None, A_TQ, A_WIN), lambda g, h: (g, h // 2, 0, h % 2, 0, 0)),
        name="a_bias_build",
        compiler_params=pltpu.CompilerParams(dimension_semantics=("parallel", "parallel")),
    )(rel_bias, buckets)
    return out.reshape(3, HEADS_PER_GROUP_A // 2, 3, 2 * A_TQ, A_WIN)


def bias_bwd(dbias, buckets):
    def body(d_ref, bk_ref, o_ref):
        bk = bk_ref[...]
        dv = d_ref[...]
        for b in range(N_BUCKETS):
            part = jnp.sum(jnp.where(bk == b, dv, 0.0), axis=1, keepdims=True)
            o_ref[b:b + 1, :] = jnp.broadcast_to(jnp.sum(part, axis=0, keepdims=True), (1, LANES))

    out = pl.pallas_call(
        body,
        out_shape=jax.ShapeDtypeStruct((3, HEADS_PER_GROUP_A, N_BUCKETS, LANES), F32),
        grid=(3, HEADS_PER_GROUP_A),
        in_specs=[pl.BlockSpec((None, None, A_TQ, A_WIN), lambda g, h: (g, h, 0, 0)),
                  pl.BlockSpec((None, A_TQ, A_WIN), lambda g, h: (g, 0, 0))],
        out_specs=pl.BlockSpec((None, None, N_BUCKETS, LANES), lambda g, h: (g, h, 0, 0)),
        name="a_bias_bwd",
        compiler_params=pltpu.CompilerParams(dimension_semantics=("parallel", "parallel")),
    )(dbias, buckets)
    return out[:, :, :, 0].transpose(2, 0, 1).reshape(N_BUCKETS, 3 * HEADS_PER_GROUP_A)


def _a_fill_padded(pad_ref, src_ref, n, pad):
    zeros = jnp.zeros((pad, LANES), pad_ref.dtype)
    pad_ref[0:pad, :] = zeros
    pad_ref[pad + n:2 * pad + n, :] = zeros
    pad_ref[pad:pad + n, :] = src_ref[...].astype(pad_ref.dtype)


def _a_stack_heads(x, lane):
    zero = jnp.zeros_like(x)
    return jnp.concatenate([jnp.where(lane < HEAD_DIM_A, x, zero), jnp.where(lane >= HEAD_DIM_A, x, zero)], axis=0)


def _a_bias_variant(qb, nqb):
    return jnp.where(qb == 0, 0, jnp.where(qb == nqb - 1, 2, 1))


def a_fwd(proj_g, bias_g, g, name):
    S = proj_g.shape[0]
    d = DILATIONS[g]
    L = S // d
    nqb = L // A_TQ
    pad = HALF_WINDOW * d

    def body(q_ref, k_ref, v_ref, b_ref, o_ref, l_ref, qf, kpad, vpad):
        qf[...] = q_ref[...].astype(F32) * A_SCALE
        _a_fill_padded(kpad, k_ref, S, pad)
        _a_fill_padded(vpad, v_ref, S, pad)
        lane = lax.broadcasted_iota(jnp.int32, (A_TQ, LANES), 1)

        def block(t, carry):
            qb, r = t // d, t % d
            start = qb * (A_TQ * d) + r
            kw = kpad[pl.ds(start, A_WIN, stride=d), :].astype(BF16)
            vw = vpad[pl.ds(start, A_WIN, stride=d), :].astype(BF16)
            q = qf[pl.ds(start, A_TQ, stride=d), :].astype(BF16)
            q2 = _a_stack_heads(q, lane)
            s = _dot(q2, kw, 1, 1) + b_ref[_a_bias_variant(qb, nqb)]
            m = jnp.max(s, axis=-1, keepdims=True)
            e = jnp.exp(s - m)
            l = jnp.sum(e, axis=-1, keepdims=True)
            o2 = _dot(e.astype(BF16), vw) / l
            lse2 = m + jnp.log(l)
            o_ref[pl.ds(start, A_TQ, stride=d), :] = jnp.where(lane < HEAD_DIM_A, o2[0:A_TQ], o2[A_TQ:])
            l_ref[pl.ds(start, A_TQ, stride=d), :] = jnp.where(lane < HEAD_DIM_A, lse2[0:A_TQ], lse2[A_TQ:])
            return carry

        lax.fori_loop(0, nqb * d, block, 0, unroll=A_UNROLL)

    out_spec = ((S, GROUP_WIDTH_A), F32, (S, LANES), lambda hp: (0, hp))
    return _call(name, body, (4,),
                 [(proj_g, (S, LANES), lambda hp: (0, hp)),
                  (proj_g, (S, LANES), lambda hp: (0, 4 + hp)),
                  (proj_g, (S, LANES), lambda hp: (0, 8 + hp)),
                  (bias_g, (None, 3, 2 * A_TQ, A_WIN), lambda hp: (hp, 0, 0, 0))],
                 [out_spec, out_spec],
                 scratch=[pltpu.VMEM((S, LANES), F32)] + [pltpu.VMEM((S + 2 * pad, LANES), F32)] * 2,
                 sem=("parallel",))


def a_combine(outs, lses, name):
    S, W = outs[0].shape
    tr = 512

    def body(o0, o1, o2, l0, l1, l2, oa_ref, lt_ref):
        a, b, c = l0[...], l1[...], l2[...]
        m = jnp.maximum(jnp.maximum(a, b), c)
        ea, eb, ec = jnp.exp(a - m), jnp.exp(b - m), jnp.exp(c - m)
        z = ea + eb + ec
        oa_ref[...] = ((ea * o0[...] + eb * o1[...] + ec * o2[...]) / z).astype(BF16)
        lt_ref[...] = m + jnp.log(z)

    spec = ((tr, W), lambda i: (i, 0))
    return _call(name, body, (S // tr,), [(a,) + spec for a in (*outs, *lses)],
                 [((S, W), BF16) + spec, ((S, W), F32) + spec], sem=("parallel",))


def a_bwd(proj_g, bias_g, do_a, o_a, lse_tot, g, name):
    S = proj_g.shape[0]
    d = DILATIONS[g]
    L = S // d
    nqb = L // A_TQ
    pad = HALF_WINDOW * d

    def body(q_ref, k_ref, v_ref, b_ref, do_ref, o_ref, l_ref, dqkv_ref, db_ref,
             qf, of, dqf, kpad, vpad, dkacc, dvacc):
        qf[...] = q_ref[...].astype(F32) * A_SCALE
        of[...] = o_ref[...].astype(F32)
        _a_fill_padded(kpad, k_ref, S, pad)
        _a_fill_padded(vpad, v_ref, S, pad)
        dkacc[...] = jnp.zeros(dkacc.shape, F32)
        dvacc[...] = jnp.zeros(dvacc.shape, F32)
        db_ref[...] = jnp.zeros(db_ref.shape, F32)
        lane = lax.broadcasted_iota(jnp.int32, (A_TQ, LANES), 1)

        def block(t, carry):
            qb, r = t // d, t % d
            start = qb * (A_TQ * d) + r
            rows = pl.ds(start, A_TQ, stride=d)
            win = pl.ds(start, A_WIN, stride=d)
            kw = kpad[win, :].astype(BF16)
            vw = vpad[win, :].astype(BF16)
            q = qf[rows, :].astype(BF16)
            do = do_ref[rows, :]
            ov = of[rows, :]
            lt = l_ref[rows, :]
            q2 = _a_stack_heads(q, lane)
            do2 = _a_stack_heads(do, lane)
            lt2 = jnp.concatenate([lt[:, 0:1], lt[:, HEAD_DIM_A:HEAD_DIM_A + 1]], axis=0)
            s = _dot(q2, kw, 1, 1) + b_ref[_a_bias_variant(qb, nqb)]
            p = jnp.exp(s - lt2)
            t = jnp.sum(do2 * jnp.concatenate([ov, ov], axis=0), axis=-1, keepdims=True)
            dob2 = do2.astype(BF16)
            ds = p * (_dot(dob2, vw, 1, 1) - t)
            db_ref[...] += ds
            dsb = ds.astype(BF16)
            dq2 = _dot(dsb, kw)
            dqf[rows, :] = jnp.where(lane < HEAD_DIM_A, dq2[0:A_TQ], dq2[A_TQ:]) * A_SCALE
            dkacc[win, :] += _dot(dsb, q2, 0, 0)
            dvacc[win, :] += _dot(p.astype(BF16), dob2, 0, 0)
            return carry

        lax.fori_loop(0, nqb * d, block, 0, unroll=A_UNROLL)
        dqkv_ref[0] = dqf[...].astype(BF16)
        dqkv_ref[1] = dkacc[pad:pad + S, :].astype(BF16)
        dqkv_ref[2] = dvacc[pad:pad + S, :].astype(BF16)

    slab = ((S, LANES), lambda hp: (0, hp))
    padded = pltpu.VMEM((S + 2 * pad, LANES), F32)
    return _call(
        name, body, (4,),
        [(proj_g, (S, LANES), lambda hp: (0, hp)),
         (proj_g, (S, LANES), lambda hp: (0, 4 + hp)),
         (proj_g, (S, LANES), lambda hp: (0, 8 + hp)),
         (bias_g, (None, 3, 2 * A_TQ, A_WIN), lambda hp: (hp, 0, 0, 0)),
         (do_a,) + slab, (o_a,) + slab, (lse_tot,) + slab],
        [((3, S, GROUP_WIDTH_A), BF16, (3, S, LANES), lambda hp: (0, 0, hp)),
         ((4, 2 * A_TQ, A_WIN), F32, (None, 2 * A_TQ, A_WIN), lambda hp: (hp, 0, 0))],
        scratch=[pltpu.VMEM((S, LANES), F32)] * 3 + [padded] * 4,
        sem=("parallel",))


def _rope_tables(S):
    rows = S // GRID_W
    row = jnp.repeat(jnp.arange(rows, dtype=F32), GRID_W)
    col = jnp.tile(jnp.arange(GRID_W, dtype=F32), rows)
    n_freq = HEAD_DIM_B // 4
    freq = ROPE_THETA ** (-jnp.arange(n_freq, dtype=F32) / n_freq)
    ang = jnp.concatenate([row[:, None] * freq, col[:, None] * freq], axis=-1)
    cos, sin = jnp.cos(ang), jnp.sin(ang)
    return jnp.repeat(cos, 2, axis=-1), jnp.stack([-sin, sin], axis=-1).reshape(S, HEAD_DIM_B)


def _swap_pairs(y):
    lane = lax.broadcasted_iota(jnp.int32, y.shape, 1)
    return jnp.where(lane % 2 == 0, pltpu.roll(y, LANES - 1, 1), pltpu.roll(y, 1, 1))


def qkv_prep(proj_b, gains, cos_t, sin_t, name):
    S = proj_b.shape[0]
    ts = 256
    n_rot = N_HEADS_B + N_KV_B
    nh = n_rot + N_KV_B
    W = nh * LANES

    def body(x_ref, g_ref, c_ref, s_ref, o_ref):
        cv, sv = c_ref[...], s_ref[...]
        for hb in range(nh):
            cols = slice(hb * LANES, (hb + 1) * LANES)
            xv = x_ref[:, cols]
            if hb < n_rot:
                r = lax.rsqrt(jnp.mean(xv * xv, axis=-1, keepdims=True) + EPS)
                yv = xv * r * g_ref[:, cols]
                o_ref[:, cols] = (yv * cv + _swap_pairs(yv) * sv).astype(BF16)
            else:
                o_ref[:, cols] = xv.astype(BF16)

    return _call(name, body, (S // ts,),
                 [(proj_b, (ts, W), lambda i: (i, 0)), (gains, (1, W), lambda i: (0, 0)),
                  (cos_t, (ts, LANES), lambda i: (i, 0)), (sin_t, (ts, LANES), lambda i: (i, 0))],
                 [((S, W), BF16, (ts, W), lambda i: (i, 0))],
                 sem=("parallel",))[0]


def qk_prep_bwd(dr, proj_b, col0, gain, cos_t, sin_t, name):
    S, W = dr.shape
    H = W // LANES
    ts = 256
    xb = (col0 * LANES) // W

    def body(d_ref, x_ref, g_ref, c_ref, s_ref, dx_ref, dg_ref):
        i = pl.program_id(0)
        cv, sv, gv = c_ref[...], s_ref[...], g_ref[...]
        dgp = jnp.zeros((1, LANES), F32)
        for hb in range(H):
            cols = slice(hb * LANES, (hb + 1) * LANES)
            dout = d_ref[:, cols]
            dy = dout * cv + _swap_pairs(dout * sv)
            dx, dgt = _rms_bwd_tile(dy, x_ref[:, cols], gv)
            dx_ref[:, cols] = dx.astype(BF16)
            dgp = dgp + jnp.sum(dgt, axis=0, keepdims=True)

        @pl.when(i == 0)
        def _():
            dg_ref[...] = dgp

        @pl.when(i > 0)
        def _():
            dg_ref[...] += dgp

    return _call(name, body, (S // ts,),
                 [(dr, (ts, W), lambda i: (i, 0)), (proj_b, (ts, W), lambda i: (i, xb)),
                  (gain, (1, LANES), lambda i: (0, 0)),
                  (cos_t, (ts, LANES), lambda i: (i, 0)), (sin_t, (ts, LANES), lambda i: (i, 0))],
                 [((S, W), BF16, (ts, W), lambda i: (i, 0)),
                  ((1, LANES), F32, (1, LANES), lambda i: (0, 0))],
                 sem=("arbitrary",))


def _row_sums(x):
    hi = x.astype(BF16)
    lo = (x - hi.astype(F32)).astype(BF16)
    ones = jnp.ones((8, LANES), BF16)
    return (_dot(ones, hi, 1, 1) + _dot(ones, lo, 1, 1))[0:1, :]


def flash_fwd(qkv, name):
    S = qkv.shape[0]
    tq = B_TQ_FWD
    scale = HEAD_DIM_B ** -0.5

    hps = B_HEADS_PER_STEP

    def body(q_ref, k_ref, v_ref, o_ref, l_ref):
        k, v = k_ref[...], v_ref[...]
        for j in range(hps):
            cols = slice(j * LANES, (j + 1) * LANES)
            s = _dot(q_ref[:, cols], k, 1, 1)
            m = jnp.max(s, axis=-1, keepdims=True)
            e = jnp.exp2((s - m) * (scale * LOG2E))
            l = jnp.sum(e, axis=-1, keepdims=True)
            o_ref[:, cols] = (_dot(e.astype(BF16), v) / l).astype(BF16)
            lse = jnp.broadcast_to(m * scale + jnp.log(l), (tq, LANES))
            l_ref[j] = _row_sums(lse) * (1.0 / LANES)

    per = GQA_GROUP_B // hps
    heads = lambda g, h, i: (i, g * per + h)
    return _call(name, body, (N_KV_B, per, S // tq),
                 [(qkv, (tq, hps * LANES), heads),
                  (qkv, (S, LANES), lambda g, h, i: (0, N_HEADS_B + g)),
                  (qkv, (S, LANES), lambda g, h, i: (0, N_HEADS_B + N_KV_B + g))],
                 [((S, N_HEADS_B * LANES), BF16, (tq, hps * LANES), heads),
                  ((N_HEADS_B, 1, S), F32, (hps, 1, tq), lambda g, h, i: (g * per + h, 0, i))],
                 sem=("parallel", "parallel", "parallel"))


def flash_bwd(qkv, k_t, do_b, o_b, lse, name):
    S = qkv.shape[0]
    tq = B_TQ_BWD
    nq = S // tq
    scale = HEAD_DIM_B ** -0.5

    def body(q_ref, k_ref, v_ref, kt_ref, do_ref, o_ref, l_ref, dq_ref, dk_ref, dv_ref, dkacc, dvacc):
        h, i = pl.program_id(1), pl.program_id(2)

        @pl.when((h == 0) & (i == 0))
        def _():
            dkacc[...] = jnp.zeros(dkacc.shape, F32)
            dvacc[...] = jnp.zeros(dvacc.shape, F32)

        q = q_ref[...]
        do = do_ref[...]
        dob = do.astype(BF16)
        t = _row_sums(do * o_ref[...].astype(F32))
        pt = jnp.exp2(_dot(k_ref[...], q, 1, 1) * (scale * LOG2E) - l_ref[...] * LOG2E)
        dsb = (pt * (_dot(v_ref[...], dob, 1, 1) - t)).astype(BF16)
        dvacc[...] += _dot(pt.astype(BF16), dob)
        dkacc[...] += _dot(dsb, q)
        dq_ref[...] = _dot(kt_ref[...], dsb).T * scale

        @pl.when((h == GQA_GROUP_B - 1) & (i == nq - 1))
        def _():
            dk_ref[...] = dkacc[...] * scale
            dv_ref[...] = dvacc[...].astype(BF16)

    head = lambda g, h, i: (i, g * GQA_GROUP_B + h)
    return _call(name, body, (N_KV_B, GQA_GROUP_B, nq),
                 [(qkv, (tq, LANES), head),
                  (qkv, (S, LANES), lambda g, h, i: (0, N_HEADS_B + g)),
                  (qkv, (S, LANES), lambda g, h, i: (0, N_HEADS_B + N_KV_B + g)),
                  (k_t, (LANES, S), lambda g, h, i: (g, 0)),
                  (do_b, (tq, LANES), head), (o_b, (tq, LANES), head),
                  (lse, (None, 1, tq), lambda g, h, i: (g * GQA_GROUP_B + h, 0, i))],
                 [((S, N_HEADS_B * LANES), F32, (tq, LANES), head),
                  ((S, N_KV_B * LANES), F32, (S, LANES), lambda g, h, i: (0, g)),
                  ((S, N_KV_B * LANES), BF16, (S, LANES), lambda g, h, i: (0, g))],
                 scratch=[pltpu.VMEM((S, LANES), F32)] * 2,
                 sem=("parallel", "arbitrary", "arbitrary"))


MERGE_TN = 512


def _mix_rows_spec(Gm, row0, n_slots, slot_map, cols=None, col_map=None):
    C = Gm.shape[2] if cols is None else cols
    cm = (lambda *idx: 0) if col_map is None else col_map
    return (Gm, (n_slots, LANES, C), lambda *idx: (slot_map(*idx), row0 // LANES, cm(*idx)))


def merge_fwd(o_a, o_b, w_a, Gm, proj_b, b_gate, name):
    S = o_a.shape[0]
    D = w_a.shape[1]
    tm, tn = 512, MERGE_TN
    ga0, gb0 = PB_GATE_A // tn, PB_GATE_B // tn

    def body(oa_ref, ob_ref, wa_ref, wb_ref, pa_ref, pb_ref, ba_ref, bb_ref, m_ref, ya_ref, yb_ref):
        ya = _dot(oa_ref[...], wa_ref[...])
        yb = _dot(ob_ref[...], wb_ref[...].reshape(N_DEV * LANES, tn))
        ga = _sigmoid(pa_ref[...] + ba_ref[...])
        gb = _sigmoid(pb_ref[...] + bb_ref[...])
        m_ref[...] = (ga * ya + gb * yb).astype(BF16)
        ya_ref[...] = ya.astype(BF16)
        yb_ref[...] = yb.astype(BF16)

    out = ((S, D), BF16, (tm, tn), lambda j, i: (i, j))
    return _call(name, body, (D // tn, S // tm),
                 [(o_a, (tm, o_a.shape[1]), lambda j, i: (i, 0)), (o_b, (tm, o_b.shape[1]), lambda j, i: (i, 0)),
                  (w_a, (w_a.shape[0], tn), lambda j, i: (0, j)),
                  _mix_rows_spec(Gm, REST_WB, N_DEV, lambda j, i: 0, cols=tn, col_map=lambda j, i: j),
                  (proj_b, (tm, tn), lambda j, i: (i, ga0 + j)), (proj_b, (tm, tn), lambda j, i: (i, gb0 + j)),
                  (b_gate, (1, tn), lambda j, i: (0, j)), (b_gate, (1, tn), lambda j, i: (0, D // tn + j))],
                 [out, out, out], sem=("parallel", "parallel"))


def out_proj(merged, Gm, x, name):
    S, D = x.shape
    tm, tn = 512, MERGE_TN

    def body(m_ref, w_ref, x_ref, o_ref):
        o_ref[...] = x_ref[...] + _dot(m_ref[...], w_ref[...].reshape(N_DEV * LANES, tn))

    return _call(name, body, (D // tn, S // tm),
                 [(merged, (tm, D), lambda j, i: (i, 0)),
                  _mix_rows_spec(Gm, REST_WOUT, N_DEV, lambda j, i: 0, cols=tn, col_map=lambda j, i: j),
                  (x, (tm, tn), lambda j, i: (i, j))],
                 [((S, D), F32, (tm, tn), lambda j, i: (i, j))], sem=("parallel", "parallel"))[0]


def merge_bwd(dx2, Gm, ya, yb, proj_b, b_gate, name):
    S, D = dx2.shape
    tm, tn = 512, MERGE_TN
    nn = D // tn
    ga0, gb0 = PB_GATE_A // tn, PB_GATE_B // tn

    def body(d_ref, w_ref, ya_ref, yb_ref, pa_ref, pb_ref, ba_ref, bb_ref, dya_ref, dyb_ref, dg_ref, dbg_ref):
        i = pl.program_id(1)
        dm = _dot(d_ref[...].astype(BF16), w_ref[...].reshape(tn, D), 1, 1)
        ga = _sigmoid(pa_ref[...] + ba_ref[...])
        gb = _sigmoid(pb_ref[...] + bb_ref[...])
        dya_ref[...] = (dm * ga).astype(BF16)
        dyb_ref[...] = (dm * gb).astype(BF16)
        dpa = dm * ya_ref[...].astype(F32) * ga * (1.0 - ga)
        dpb = dm * yb_ref[...].astype(F32) * gb * (1.0 - gb)
        dg_ref[0] = dpa.astype(BF16)
        dg_ref[1] = dpb.astype(BF16)
        sa = jnp.sum(dpa, axis=0, keepdims=True)
        sb = jnp.sum(dpb, axis=0, keepdims=True)

        @pl.when(i == 0)
        def _():
            dbg_ref[0] = sa
            dbg_ref[1] = sb

        @pl.when(i > 0)
        def _():
            dbg_ref[0] += sa
            dbg_ref[1] += sb

    tile = ((tm, tn), lambda j, i: (i, j))
    dya, dyb, dgate, dbg = _call(
        name, body, (nn, S // tm),
        [(dx2, (tm, D), lambda j, i: (i, 0)),
         _mix_rows_spec(Gm, REST_WOUT, tn // LANES, lambda j, i: j),
         (ya,) + tile, (yb,) + tile,
         (proj_b, (tm, tn), lambda j, i: (i, ga0 + j)), (proj_b, (tm, tn), lambda j, i: (i, gb0 + j)),
         (b_gate, (1, tn), lambda j, i: (0, j)), (b_gate, (1, tn), lambda j, i: (0, nn + j))],
        [((S, D), BF16) + tile, ((S, D), BF16) + tile,
         ((2, S, D), BF16, (2, tm, tn), lambda j, i: (0, i, j)),
         ((2, 1, D), F32, (2, 1, tn), lambda j, i: (0, 0, j))],
        sem=("parallel", "arbitrary"))
    return dya, dyb, dgate, dbg


def matmul_nt(a, b_spec_fn, N, name, tn=512):
    S, K = a.shape
    tm = 512

    def body(a_ref, b_ref, o_ref):
        b = b_ref[...]
        o_ref[...] = _dot(a_ref[...], b.reshape(-1, b.shape[-1]), 1, 1)

    return _call(name, body, (N // tn, S // tm),
                 [(a, (tm, K), lambda j, i: (i, 0)), b_spec_fn(lambda j, i: j)],
                 [((S, N), F32, (tm, tn), lambda j, i: (i, j))], sem=("parallel", "parallel"))[0]


def weight_grad_rows(a, b, grads, row0, name):
    S, M = a.shape
    N = b.shape[1]
    tmm = 512
    tk = WGRAD_TK
    nk = S // tk

    def body(g_ref, a_ref, b_ref, o_ref, acc_ref):
        k = pl.program_id(1)
        p = _dot(a_ref[...], b_ref[...].astype(BF16), 0, 0)

        @pl.when(k == 0)
        def _():
            acc_ref[...] = p

        @pl.when(k > 0)
        def _():
            acc_ref[...] += p

        @pl.when(k == nk - 1)
        def _():
            o_ref[...] = acc_ref[...].astype(BF16).reshape(tmm // LANES, LANES, N)

    return pl.pallas_call(
        body,
        out_shape=jax.ShapeDtypeStruct(grads.shape, BF16),
        grid=(M // tmm, nk),
        in_specs=[pl.BlockSpec(memory_space=pl.ANY),
                  pl.BlockSpec((tk, tmm), lambda j, k: (k, j)),
                  pl.BlockSpec((tk, N), lambda j, k: (k, 0))],
        out_specs=pl.BlockSpec((tmm // LANES, LANES, N), lambda j, k: (j, row0 // LANES, 0)),
        scratch_shapes=[pltpu.VMEM((tmm, N), F32)],
        input_output_aliases={0: 0},
        name=name,
        compiler_params=pltpu.CompilerParams(dimension_semantics=("parallel", "arbitrary"),
                                             vmem_limit_bytes=VMEM_LIMIT),
    )(grads, a, b)


def weight_grad_plain(a, b, name):
    S, M = a.shape
    N = b.shape[1]
    tk = WGRAD_TK
    nk = S // tk

    def body(a_ref, b_ref, o_ref, acc_ref):
        k = pl.program_id(0)
        p = _dot(a_ref[...], b_ref[...], 0, 0)

        @pl.when(k == 0)
        def _():
            acc_ref[...] = p

        @pl.when(k > 0)
        def _():
            acc_ref[...] += p

        @pl.when(k == nk - 1)
        def _():
            o_ref[...] = acc_ref[...].astype(BF16)

    return _call(name, body, (nk,),
                 [(a, (tk, M), lambda k: (k, 0)), (b, (tk, N), lambda k: (k, 0))],
                 [((M, N), BF16, (M, N), lambda k: (0, 0))],
                 scratch=[pltpu.VMEM((M, N), F32)], sem=("arbitrary",))[0]


def local_step(x, tgt, p, get_g1_up, get_g1_down, get_gm_in, get_gm_rest, get_g2, emit, start_token):
    S, D = x.shape
    after = lambda t: t[0:1, 0:1]
    buckets = _bucket_tables()
    cos_t, sin_t = _rope_tables(S)
    gains = jnp.concatenate([jnp.tile(p["q_norm"], (1, N_HEADS_B)), jnp.tile(p["k_norm"], (1, N_KV_B)),
                             jnp.ones((1, N_KV_B * LANES), F32)], axis=1)

    n1 = rms_fwd(x, p["ffn1_norm"] + after(start_token), "ffn1_norm")
    bias = bias_build(p["rel_bias"] + after(start_token), buckets)
    g1_up = get_g1_up((n1, bias))
    ab1 = ffn_up(n1, (g1_up, None), "ffn1_up")
    G1 = (g1_up, get_g1_down(ab1))
    x1 = ffn_down(ab1, G1, x, "ffn1_down")

    hm = rms_fwd(x1, p["mix_norm"], "mix_norm")
    Gw = get_gm_in(hm)
    n_a = A_QKV_WIDTH // PROJ_TN
    proj_a = [in_proj(hm, Gw, g, 3, BF16, "in_proj_a%d" % g, tile_stride=3) for g in range(3)]
    proj_b = in_proj(hm, Gw, n_a, PB_WIDTH // PROJ_TN, F32, "in_proj_b")

    outs, lses = [], []
    for g in range(3):
        o, l = a_fwd(proj_a[g], bias[g], g, "a_fwd_%d" % g)
        outs.append(o)
        lses.append(l)
    o_a, lse_tot = a_combine(outs, lses, "a_combine")

    qkv = qkv_prep(proj_b, gains, cos_t, sin_t, "qkv_prep")
    k_t = qkv[:, N_HEADS_B * LANES:(N_HEADS_B + N_KV_B) * LANES].T
    o_b, lse_b = flash_fwd(qkv, "flash_fwd")

    Gm = get_gm_rest(o_b)
    w_a = Gm[:, REST_WA:REST_ROWS, :].reshape(N_DEV, GROUP_WIDTH_A, LANES).transpose(1, 0, 2).reshape(GROUP_WIDTH_A, D)
    merged, ya, yb = merge_fwd(o_a, o_b, w_a, Gm, proj_b, p["b_gate"], "merge_fwd")
    x2 = out_proj(merged, Gm, x1, "out_proj")

    G2 = get_g2(x2)
    n2 = rms_fwd(x2, p["ffn2_norm"], "ffn2_norm")
    ab2 = ffn_up(n2, G2, "ffn2_up")
    x3 = ffn_down(ab2, G2, x2, "ffn2_down")

    loss, dx3, dx3_b, d_final = final_loss(x3, tgt, p["final_norm"], "final_loss")

    dabh2, gw2 = ffn_bwd_weights(dx3_b, ab2, n2, G2, "ffn2_bwd")
    t2 = emit("ffn2", gw2)
    dx2, dx2_b, d_ffn2_norm = ffn_bwd_input(dabh2, G2, x2, p["ffn2_norm"] + after(t2), dx3, "ffn2_bwd")

    dya, dyb, dgate, dbg = merge_bwd(dx2_b, Gm, ya, yb, proj_b, p["b_gate"], "merge_bwd")
    gm_grads = jnp.zeros((N_DEV, MIX_ROWS, D), BF16)
    gm_grads = weight_grad_rows(merged, dx2_b, gm_grads, MIX_WOUT, "dw_out")
    gm_grads = weight_grad_rows(o_b, dyb, gm_grads, MIX_WB, "dw_branch_b")
    dw_a = weight_grad_plain(o_a, dya, "dw_branch_a")
    do_a = matmul_nt(dya, lambda jm: (w_a, (MERGE_TN, D), lambda j, i: (jm(j, i), 0)), GROUP_WIDTH_A, "do_a")
    do_b = matmul_nt(dyb, lambda jm: _mix_rows_spec(Gm, REST_WB, MERGE_TN // LANES, jm), N_HEADS_B * LANES, "do_b")

    dq_r, dk_r, dv_b = flash_bwd(qkv, k_t, do_b, o_b, lse_b, "flash_bwd")
    dq_b, d_q_norm = qk_prep_bwd(dq_r, proj_b, 0, p["q_norm"], cos_t, sin_t, "q_prep_bwd")
    dk_b, d_k_norm = qk_prep_bwd(dk_r, proj_b, N_HEADS_B, p["k_norm"], cos_t, sin_t, "k_prep_bwd")

    dqkv, dbs = [], []
    for g in range(3):
        dg_, db = a_bwd(proj_a[g], bias[g], do_a, o_a, lse_tot, g, "a_bwd_%d" % g)
        dqkv.append(dg_)
        dbs.append(db)
    d_rel_bias = bias_bwd(jnp.stack(dbs, axis=0).reshape(3, HEADS_PER_GROUP_A, A_TQ, A_WIN), buckets)

    dproj = _dproj_pieces(dqkv, dq_b, jnp.concatenate([dk_b, dv_b], axis=1), dgate)
    gm_grads = in_proj_bwd_dw(dproj, hm, gm_grads, "in_proj_bwd")
    dw_a_sh = dw_a.reshape(GROUP_WIDTH_A, N_DEV, LANES).transpose(1, 0, 2).reshape(N_DEV, MIX_ROWS - MIX_WA, D)
    gm_grads = lax.dynamic_update_slice(gm_grads, dw_a_sh, (0, MIX_WA, 0))
    tm = emit("mix", gm_grads)
    dx1, dx1_b, d_mix_norm = in_proj_bwd_dh(dproj, Gw, x1, p["mix_norm"] + after(tm), dx2, "in_proj_bwd")

    dabh1, gw1 = ffn_bwd_weights(dx1_b, ab1, n1, G1, "ffn1_bwd")
    t1 = emit("ffn1", gw1)
    dx0, d_ffn1_norm = ffn_bwd_input(dabh1, G1, x, p["ffn1_norm"] + after(t1), dx1, "ffn1_bwd", as_operand=False)

    small = dict(ffn1_norm=d_ffn1_norm, mix_norm=d_mix_norm, b_gate=dbg.reshape(1, 2 * D),
                 q_norm=d_q_norm, k_norm=d_k_norm, rel_bias=d_rel_bias, ffn2_norm=d_ffn2_norm,
                 final_norm=d_final)
    return loss, dx0, small


def _pack_small(t, loss_row):
    row6 = jnp.concatenate([t["q_norm"].reshape(1, -1), t["k_norm"].reshape(1, -1), t["rel_bias"].reshape(1, -1)], axis=1)
    return jnp.concatenate([t["ffn1_norm"].reshape(1, -1), t["mix_norm"].reshape(1, -1), t["b_gate"].reshape(2, -1),
                            t["ffn2_norm"].reshape(1, -1), t["final_norm"].reshape(1, -1), row6, loss_row], axis=0)


def _unpack_small(a, shapes):
    return dict(ffn1_norm=a[0:1].reshape(shapes["ffn1_norm"]), mix_norm=a[1:2].reshape(shapes["mix_norm"]),
                b_gate=a[2:4].reshape(shapes["b_gate"]), ffn2_norm=a[4:5].reshape(shapes["ffn2_norm"]),
                final_norm=a[5].reshape(shapes["final_norm"]), q_norm=a[6:7, 0:128].reshape(shapes["q_norm"]),
                k_norm=a[6:7, 128:256].reshape(shapes["k_norm"]), rel_bias=a[6, 256:1024].reshape(shapes["rel_bias"]))


SMALL = ("ffn1_norm", "mix_norm", "b_gate", "q_norm", "k_norm", "rel_bias", "ffn2_norm", "final_norm")
ORDER = ("ffn1_norm", "ffn1_w1", "ffn1_w3", "ffn1_w2", "mix_norm", "w_in", "b_gate", "q_norm", "k_norm", "rel_bias",
         "w_branch_a", "w_branch_b", "w_out", "ffn2_norm", "ffn2_w1", "ffn2_w3", "ffn2_w2", "final_norm")


def kernel(x, ffn1_norm, ffn1_w1, ffn1_w3, ffn1_w2, mix_norm, w_in, b_gate, q_norm, k_norm, rel_bias, w_branch_a, w_branch_b, w_out, ffn2_norm, ffn2_w1, ffn2_w3, ffn2_w2, final_norm, loss_target, m_ffn1_norm, m_ffn1_w1, m_ffn1_w3, m_ffn1_w2, m_mix_norm, m_w_in, m_b_gate, m_q_norm, m_k_norm, m_rel_bias, m_w_branch_a, m_w_branch_b, m_w_out, m_ffn2_norm, m_ffn2_w1, m_ffn2_w3, m_ffn2_w2, m_final_norm, v_ffn1_norm, v_ffn1_w1, v_ffn1_w3, v_ffn1_w2, v_mix_norm, v_w_in, v_b_gate, v_q_norm, v_k_norm, v_rel_bias, v_w_branch_a, v_w_branch_b, v_w_out, v_ffn2_norm, v_ffn2_w1, v_ffn2_w3, v_ffn2_w2, v_final_norm):
    args = dict(locals())
    w = {n: args[n] for n in ORDER}
    m = {n: args["m_" + n] for n in ORDER}
    v = {n: args["v_" + n] for n in ORDER}
    D = x.shape[2]

    blocks = (
        ("ffn1_up", lambda t: jnp.concatenate([ffn1_w1[0].T + t, ffn1_w3[0].T + t], axis=0)),
        ("ffn1_down", lambda t: ffn1_w2[0] + t),
        ("mix_in", lambda t: w_in[0] + t),
        ("mix_rest", lambda t: jnp.concatenate([w_branch_b[0] + t, w_out[0] + t,
                                                w_branch_a[0].reshape(REST_ROWS - REST_WA, D) + t], axis=0)),
        ("ffn2", lambda t: jnp.concatenate([ffn2_w1[0].T + t, ffn2_w3[0].T + t, ffn2_w2[0] + t], axis=0)),
    )
    direct = ("mix_rest", "ffn2")
    gathers = {}
    start_token = jnp.zeros((8, LANES), F32)
    for tag, make in blocks:
        start = all_gather_start_direct if tag in direct else all_gather_start
        gathers[tag] = start(make(start_token[0:1, 0:1]).astype(BF16), "all_gather_" + tag + "_start")
        start_token = gathers[tag][4]

    def gathered(tag):
        def get(after):
            if tag in direct:
                return all_gather_place_own(*_split_wait("all_gather_" + tag + "_wait", gathers[tag], N_DEV - 1, after),
                                            "all_gather_" + tag + "_own")
            return all_gather_finish(*_split_wait("all_gather_" + tag + "_wait", gathers[tag], 4, after),
                                     "all_gather_" + tag + "_finish")
        return get

    core = lax.axis_index("c").astype(jnp.int32).reshape(1)
    chip = (2 * lax.axis_index("x") + lax.axis_index("y")).astype(jnp.int32).reshape(1)
    device = 2 * chip + core
    exchanges = {}

    def emit(tag, gw):
        if tag == "ffn1":
            (theirs,) = reduce_scatter_pair([gw], "reduce_scatter_pair_" + tag)
            part = pair_add(gw, theirs, core, "pair_add_" + tag)
            exchanges[tag] = reduce_scatter_start(part, "reduce_scatter_" + tag + "_start")
        else:
            exchanges[tag] = reduce_scatter_start_direct(gw, "reduce_scatter_" + tag + "_start")
        return exchanges[tag][4]

    small_p = dict(ffn1_norm=ffn1_norm, mix_norm=mix_norm, b_gate=b_gate, q_norm=q_norm, k_norm=k_norm,
                   rel_bias=rel_bias, ffn2_norm=ffn2_norm, final_norm=final_norm.reshape(1, D))
    loss_p, grad_x, small_g = local_step(x[0], loss_target[0], small_p, gathered("ffn1_up"), gathered("ffn1_down"),
                                         gathered("mix_in"), gathered("mix_rest"), gathered("ffn2"), emit, start_token)

    def landed(tag, after):
        n_others, me = (3, chip) if tag == "ffn1" else (N_DEV - 1, device)
        return tuple(_split_wait("reduce_scatter_" + tag + "_wait", exchanges[tag], n_others, after)) + (me,)

    grads, delta, new_m, new_v = {}, {}, {}, {}

    def finish(n, part, land, me, off, blk, transposed=False):
        shp = w[n].shape
        if transposed:
            to2 = lambda a: a.reshape(shp[-2], shp[-1]).T
            back = lambda a: a.T.reshape(shp)
        else:
            to2 = lambda a: a.reshape(shp[-2], shp[-1])
            back = lambda a: a.reshape(shp)
        res = sum_adamw(part, land, me, off, blk, to2(w[n]), to2(m[n]), to2(v[n]), "update_" + n)
        grads[n], delta[n], new_m[n], new_v[n] = [back(a) for a in res]

    last_token = exchanges["ffn1"][4]
    for tag, after in (("ffn2", last_token), ("ffn1", grad_x)):
        group = landed(tag, after)
        finish(tag + "_w1", *group, 0, FFN_SHARD, transposed=True)
        finish(tag + "_w3", *group, FFN_SHARD, FFN_SHARD, transposed=True)
        finish(tag + "_w2", *group, 2 * FFN_SHARD, FFN_SHARD)
        if tag == "ffn2":
            group_m = landed("mix", last_token)
            finish("w_in", *group_m, MIX_WIN, LANES)
            finish("w_branch_b", *group_m, MIX_WB, LANES)
            finish("w_out", *group_m, MIX_WOUT, LANES)
            grads["w_branch_a"] = sum_landed(*group_m, MIX_WA, MIX_ROWS - MIX_WA, MIX_ROWS - MIX_WA,
                                             "w_branch_a_sum").reshape(w_branch_a.shape)
    loss_row = jnp.pad(loss_p, ((0, 0), (0, D - LANES)))
    smalls = small_all_gather(_pack_small(small_g, loss_row))
    small_sum = sum_slots(smalls, 0, N_DEV, N_DEV, "small_sum")
    small_shapes = {n: w[n].shape for n in SMALL}
    grads.update(_unpack_small(small_sum, small_shapes))
    loss = small_sum[7, 0]

    n = "w_branch_a"
    two_d = lambda a: a.reshape(w[n].shape[-2], w[n].shape[-1])
    d_, m_, v_ = adamw(two_d(w[n]), two_d(grads[n]), two_d(m[n]), two_d(v[n]), "adamw_" + n)
    delta[n], new_m[n], new_v[n] = [a.reshape(w[n].shape) for a in (d_, m_, v_)]
    zero_row = jnp.zeros((1, D), F32)
    pack = lambda t: _pack_small({n: t[n] for n in SMALL}, zero_row)
    d_, m_, v_ = adamw(pack(w), small_sum, pack(m), pack(v), "adamw_small")
    for src, dst in ((d_, delta), (m_, new_m), (v_, new_v)):
        dst.update(_unpack_small(src, small_shapes))

    return (loss, grad_x[None], *[grads[n] for n in ORDER], *[delta[n] for n in ORDER],
            *[new_m[n] for n in ORDER], *[new_v[n] for n in ORDER])
```

```python
import math

import jax
import jax.numpy as jnp
from jax import lax
from jax.experimental import pallas as pl
from jax.experimental.pallas import tpu as pltpu

F32 = jnp.float32
BF16 = jnp.bfloat16
MESH = pl.DeviceIdType.MESH

V7X_VMEM_BYTES = 64 * 1024 * 1024
VMEM_LIMIT = V7X_VMEM_BYTES - 8 * 1024 * 1024
LANES = 128

N_DEV = 8
EPS = 1e-6
NEG_INF = -1e30

DILATIONS = (1, 4, 16)
HALF_WINDOW = 64
HEAD_DIM_A = 64
HEADS_PER_GROUP_A = 8
GROUP_WIDTH_A = 512
A_QKV_WIDTH = 4608
A_GROUP_QKV = A_QKV_WIDTH // 3
A_TQ = 128
A_WIN = A_TQ + 2 * HALF_WINDOW
A_UNROLL = 8
A_SCALE = HEAD_DIM_A ** -0.5
WGRAD_TK = 2048
HEAD_DIM_B = 128
N_HEADS_B = 8
N_KV_B = 2
GQA_GROUP_B = 4
GRID_W = 64
ROPE_THETA = 10000.0
B_TQ_FWD = 256
B_TQ_BWD = 512
B_HEADS_PER_STEP = 4
LOG2E = 1.4426950408889634
N_BUCKETS = 32
MAX_DISTANCE = 1024
PB_WIDTH = 3584
PB_GATE_A = 1536
PB_GATE_B = 2560

ADAM_LR = 0.001
ADAM_B1 = 0.9
ADAM_B2 = 0.999
ADAM_EPS = 1e-08
ADAM_WD = 0.01
ADAM_STEP = 10

FFN_SHARD = 352
MIX_WIN, MIX_WB, MIX_WOUT, MIX_WA = 0, 1024, 1152, 1280
MIX_ROWS = 1344
REST_WB, REST_WOUT, REST_WA, REST_ROWS = 0, 128, 256, 320


def _dot(a, b, ca=1, cb=0):
    return lax.dot_general(a, b, (((ca,), (cb,)), ((), ())), preferred_element_type=F32)


def _call(name, body, grid, ins, outs, scratch=(), sem=None, aliases=None):
    ins = [tuple(i) + (None,) * (4 - len(i)) for i in ins]
    res = pl.pallas_call(
        body,
        out_shape=[jax.ShapeDtypeStruct(s, d) for (s, d, _, _) in outs],
        grid=grid,
        in_specs=[pl.BlockSpec(bs, im, pipeline_mode=pm) for (_, bs, im, pm) in ins],
        out_specs=[pl.BlockSpec(bs, im) for (_, _, bs, im) in outs],
        scratch_shapes=list(scratch),
        name=name,
        input_output_aliases=aliases or {},
        compiler_params=pltpu.CompilerParams(dimension_semantics=sem, vmem_limit_bytes=VMEM_LIMIT),
    )(*[i[0] for i in ins])
    return res


def _sigmoid(x):
    return 0.5 * jnp.tanh(0.5 * x) + 0.5


def _position():
    return lax.axis_index("x"), lax.axis_index("y"), lax.axis_index("c")


def _hbm_specs(n):
    return [pl.BlockSpec(memory_space=pl.ANY) for _ in range(n)]


PAIR_BUFFERS = 4


def reduce_scatter_pair(grads, name):
    n = len(grads)
    C = grads[0].shape[2]
    half = [g.shape[1] // 2 for g in grads]
    chunks = [(i, q, hf) for i in range(n) for q in range(4) for hf in range(2)]
    nb = PAIR_BUFFERS

    def body(*refs):
        ins, theirs = refs[:n], refs[n:2 * n]
        buf, load_sems, send_sems, recv_sems = refs[2 * n:]
        x, y, c = _position()
        sibling = (x, y, 1 - c)

        def load(k):
            i, q, hf = chunks[k]
            r = half[i]
            return pltpu.make_async_copy(ins[i].at[2 * q + (1 - c), pl.ds(hf * r, r), :],
                                         buf.at[k % nb, pl.ds(0, r), :], load_sems.at[k % nb])

        def send(k):
            i, q, hf = chunks[k]
            r = half[i]
            return pltpu.make_async_remote_copy(
                src_ref=buf.at[k % nb, pl.ds(0, r), :], dst_ref=theirs[i].at[q, pl.ds(hf * r, r), :],
                send_sem=send_sems.at[k % nb], recv_sem=recv_sems.at[i],
                device_id=sibling, device_id_type=MESH)

        for k in range(len(chunks) + 1):
            if k < len(chunks):
                if k >= nb:
                    send(k - nb).wait_send()
                load(k).start()
            if k >= 1:
                load(k - 1).wait()
                send(k - 1).start()
        for k in range(max(0, len(chunks) - nb), len(chunks)):
            send(k).wait_send()
        for i in range(n):
            pltpu.make_async_remote_copy(
                src_ref=theirs[i], dst_ref=theirs[i], send_sem=send_sems.at[0], recv_sem=recv_sems.at[i],
                device_id=sibling, device_id_type=MESH).wait_recv()

    return pl.pallas_call(
        body,
        out_shape=[jax.ShapeDtypeStruct((4,) + g.shape[1:], g.dtype) for g in grads],
        in_specs=_hbm_specs(n),
        out_specs=_hbm_specs(n),
        scratch_shapes=[pltpu.VMEM((nb, max(half), C), grads[0].dtype), pltpu.SemaphoreType.DMA((nb,)),
                        pltpu.SemaphoreType.DMA((nb,)), pltpu.SemaphoreType.DMA((n,))],
        name=name,
        compiler_params=pltpu.CompilerParams(vmem_limit_bytes=VMEM_LIMIT),
    )(*grads)


_HBM_SPEC = pl.BlockSpec(memory_space=pltpu.HBM)
_SEM_SPEC = pl.BlockSpec(memory_space=pltpu.SEMAPHORE)
_TOKEN_SPEC = pl.BlockSpec(memory_space=pltpu.VMEM)
_DATAFLOW = pltpu.SideEffectType.DATAFLOW_SIDE_EFFECTING


def _split_start_many(name, exchanges):
    n = len(exchanges)

    def full_body(*refs):
        srcs, lands = refs[:n], refs[n:2 * n]
        sems = refs[2 * n:4 * n]
        token = refs[-1]
        for i, (body, _, _) in enumerate(exchanges):
            body(srcs[i], lands[i], sems[2 * i], sems[2 * i + 1])
        token[...] = jnp.zeros_like(token)

    srcs = [pltpu.with_memory_space_constraint(src, pltpu.HBM) for _, src, _ in exchanges]
    lands = [pltpu.with_memory_space_constraint(lax.empty(shape, src.dtype), pltpu.HBM)
             for _, src, shape in exchanges]
    res = pl.pallas_call(
        full_body, name=name,
        out_shape=(pltpu.SemaphoreType.DMA(()),) * (2 * n)
        + tuple(pltpu.HBM(a.shape, a.dtype) for a in srcs + lands) + (jax.ShapeDtypeStruct((8, LANES), F32),),
        in_specs=(_HBM_SPEC,) * (2 * n),
        out_specs=(_SEM_SPEC,) * (2 * n) + (_HBM_SPEC,) * (2 * n) + (_TOKEN_SPEC,),
        input_output_aliases={i: 2 * n + i for i in range(2 * n)},
        compiler_params=pltpu.CompilerParams(has_side_effects=_DATAFLOW),
    )(*srcs, *lands)
    return [(res[2 * i], res[2 * i + 1], res[2 * n + i], res[3 * n + i], res[-1]) for i in range(n)]


def _split_start(name, body, src, land_shape):
    return _split_start_many(name, [(body, src, land_shape)])[0]


def _split_wait(name, started, n_blocks, after):
    send_sem, recv_sem, src_thru, land_thru, _ = started
    after = after if isinstance(after, tuple) else (after,)

    def body(src_ref, land_ref, send_sem, recv_sem, *rest):
        x, y, c = _position()
        blocks = land_ref.at[pl.ds(0, n_blocks)]
        copy = pltpu.make_async_remote_copy(src_ref=blocks, dst_ref=blocks, send_sem=send_sem, recv_sem=recv_sem,
                                            device_id=(x, y, c), device_id_type=MESH)
        copy.wait_send()
        copy.wait_recv()

    return pl.pallas_call(
        body, name=name,
        out_shape=(pltpu.HBM(src_thru.shape, src_thru.dtype), pltpu.HBM(land_thru.shape, land_thru.dtype)),
        in_specs=(_HBM_SPEC, _HBM_SPEC, _SEM_SPEC, _SEM_SPEC) + (pl.BlockSpec(memory_space=pl.ANY),) * len(after),
        out_specs=(_HBM_SPEC, _HBM_SPEC),
        input_output_aliases={0: 0, 1: 1},
        compiler_params=pltpu.CompilerParams(has_side_effects=_DATAFLOW),
    )(src_thru, land_thru, send_sem, recv_sem, *after)


def all_gather_start_all(blocks, name):
    def starter(direct):
        def body(b_ref, land_ref, send_sem, recv_sem):
            x, y, c = _position()
            peers = _other_devices(x, y, c) if direct else [(x, y, 1 - c), (1 - x, y, c), (x, 1 - y, c),
                                                            (1 - x, 1 - y, c)]
            for peer in peers:
                pltpu.make_async_remote_copy(src_ref=b_ref, dst_ref=land_ref.at[4 * x + 2 * y + c],
                                             send_sem=send_sem, recv_sem=recv_sem,
                                             device_id=peer, device_id_type=MESH).start()
        return body

    return _split_start_many(name, [(starter(direct), block, (N_DEV,) + block.shape) for block, direct in blocks])


def all_gather_finish(block, land, name):
    R, C = block.shape

    def body(b_ref, land_in, land_ref, stage, load_sems, send_sems, recv_sems, own_sem):
        x, y, c = _position()
        sibling = (x, y, 1 - c)
        chips = [(1 - x, y), (x, 1 - y), (1 - x, 1 - y)]
        own_in = pltpu.make_async_copy(b_ref, stage.at[3], load_sems.at[3])
        own_in.start()
        loads = [pltpu.make_async_copy(land_in.at[4 * px + 2 * py + c], stage.at[j], load_sems.at[j])
                 for j, (px, py) in enumerate(chips)]
        for ld in loads:
            ld.start()
        sends = []
        for j, (px, py) in enumerate(chips):
            loads[j].wait()
            dst = land_ref.at[4 * px + 2 * py + c]
            cp = pltpu.make_async_remote_copy(src_ref=stage.at[j], dst_ref=dst, send_sem=send_sems.at[j],
                                              recv_sem=recv_sems.at[j], device_id=sibling, device_id_type=MESH)
            cp.start()
            sends.append(cp)
        own_in.wait()
        own_out = pltpu.make_async_copy(stage.at[3], land_ref.at[4 * x + 2 * y + c], own_sem)
        own_out.start()
        for j, (px, py) in enumerate(chips):
            dst = land_ref.at[4 * px + 2 * py + (1 - c)]
            pltpu.make_async_remote_copy(src_ref=stage.at[j], dst_ref=dst, send_sem=send_sems.at[j],
                                         recv_sem=recv_sems.at[j], device_id=sibling,
                                         device_id_type=MESH).wait_recv()
        for cp in sends:
            cp.wait_send()
        own_out.wait()

    return pl.pallas_call(
        body,
        out_shape=jax.ShapeDtypeStruct(land.shape, land.dtype),
        in_specs=_hbm_specs(2),
        out_specs=pl.BlockSpec(memory_space=pl.ANY),
        scratch_shapes=[pltpu.VMEM((4, R, C), block.dtype), pltpu.SemaphoreType.DMA((4,)),
                        pltpu.SemaphoreType.DMA((3,)), pltpu.SemaphoreType.DMA((3,)), pltpu.SemaphoreType.DMA],
        input_output_aliases={1: 0},
        name=name,
        compiler_params=pltpu.CompilerParams(vmem_limit_bytes=VMEM_LIMIT),
    )(block, land)


def reduce_scatter_start(parts, name):
    def body(p_ref, land_ref, send_sem, recv_sem):
        x, y, c = _position()
        for px, py in [(1 - x, y), (x, 1 - y), (1 - x, 1 - y)]:
            pltpu.make_async_remote_copy(src_ref=p_ref.at[2 * px + py], dst_ref=land_ref.at[2 * x + y],
                                         send_sem=send_sem, recv_sem=recv_sem,
                                         device_id=(px, py, c), device_id_type=MESH).start()

    return _split_start(name, body, parts, parts.shape)


def _other_devices(x, y, c):
    return [(1 - x if k & 4 else x, 1 - y if k & 2 else y, 1 - c if k & 1 else c) for k in range(1, N_DEV)]


def all_gather_place_own(block, land, name):
    R, C = block.shape

    def body(b_ref, land_in, land_ref, stage, sems):
        x, y, c = _position()
        load = pltpu.make_async_copy(b_ref, stage, sems.at[0])
        load.start()
        load.wait()
        store = pltpu.make_async_copy(stage, land_ref.at[4 * x + 2 * y + c], sems.at[1])
        store.start()
        store.wait()

    return pl.pallas_call(
        body,
        out_shape=jax.ShapeDtypeStruct(land.shape, land.dtype),
        in_specs=_hbm_specs(2),
        out_specs=pl.BlockSpec(memory_space=pl.ANY),
        scratch_shapes=[pltpu.VMEM((R, C), block.dtype), pltpu.SemaphoreType.DMA((2,))],
        input_output_aliases={1: 0},
        name=name,
    )(block, land)


def reduce_scatter_start_direct(grads, name):
    def body(g_ref, land_ref, send_sem, recv_sem):
        x, y, c = _position()
        for px, py, pc in _other_devices(x, y, c):
            pltpu.make_async_remote_copy(src_ref=g_ref.at[4 * px + 2 * py + pc],
                                         dst_ref=land_ref.at[4 * x + 2 * y + c],
                                         send_sem=send_sem, recv_sem=recv_sem,
                                         device_id=(px, py, pc), device_id_type=MESH).start()

    return _split_start(name, body, grads, grads.shape)


def small_all_gather(small):
    def body(small_ref, smalls, s_send, s_recv, s_local):
        x, y, c = _position()
        me = 4 * x + 2 * y + c
        lc = pltpu.make_async_copy(small_ref, smalls.at[me], s_local)
        lc.start()
        remote = []
        k = 0
        for dx in (0, 1):
            for dy in (0, 1):
                for dc in (0, 1):
                    if dx + dy + dc == 0:
                        continue
                    peer = (1 - x if dx else x, 1 - y if dy else y, 1 - c if dc else c)
                    rc = pltpu.make_async_remote_copy(
                        src_ref=small_ref, dst_ref=smalls.at[me],
                        send_sem=s_send.at[k], recv_sem=s_recv.at[k],
                        device_id=peer, device_id_type=MESH)
                    rc.start()
                    remote.append(rc)
                    k += 1
        for rc in remote:
            rc.wait()
        lc.wait()

    return pl.pallas_call(
        body,
        out_shape=jax.ShapeDtypeStruct((N_DEV,) + small.shape, small.dtype),
        in_specs=_hbm_specs(1),
        out_specs=pl.BlockSpec(memory_space=pl.ANY),
        scratch_shapes=[pltpu.SemaphoreType.DMA((7,)), pltpu.SemaphoreType.DMA((7,)), pltpu.SemaphoreType.DMA],
        name="small_all_gather",
    )(small)


def pair_add(grads, theirs, core, name):
    _, R, C = theirs.shape
    tr = R // 2

    def body(c_ref, a_ref, b_ref, o_ref):
        o_ref[...] = (a_ref[...].astype(F32) + b_ref[...].astype(F32)).astype(BF16)

    return pl.pallas_call(
        body,
        out_shape=jax.ShapeDtypeStruct(theirs.shape, BF16),
        grid_spec=pltpu.PrefetchScalarGridSpec(
            num_scalar_prefetch=1, grid=(4, R // tr),
            in_specs=[pl.BlockSpec((None, tr, C), lambda q, i, c: (2 * q + c[0], i, 0)),
                      pl.BlockSpec((None, tr, C), lambda q, i, c: (q, i, 0))],
            out_specs=pl.BlockSpec((None, tr, C), lambda q, i, c: (q, i, 0))),
        name=name,
        compiler_params=pltpu.CompilerParams(dimension_semantics=("parallel", "parallel"),
                                             vmem_limit_bytes=VMEM_LIMIT),
    )(core, grads, theirs)


def sum_slots(recv, off, rows, blk, name):
    nq, _, C = recv.shape
    ob = off // blk

    def body(r_ref, o_ref):
        acc = r_ref[0].astype(F32)
        for q in range(1, nq):
            acc = acc + r_ref[q].astype(F32)
        o_ref[...] = acc

    return _call(name, body, (rows // blk,),
                 [(recv, (nq, blk, C), lambda i: (0, ob + i, 0))],
                 [((rows, C), F32, (blk, C), lambda i: (i, 0))], sem=("parallel",))[0]


def _sum_terms(refs):
    acc = refs[0][...].astype(F32)
    for r in refs[1:]:
        acc = acc + r[...].astype(F32)
    return acc


def sum_landed(own, land, me, off, rows, blk, name):
    n, _, C = land.shape
    ob = off // blk

    def body(c_ref, *refs):
        refs[n][...] = _sum_terms(refs[:n])

    def entry(flip):
        return pl.BlockSpec((None, blk, C), lambda i, c: (c[0] ^ flip, ob + i, 0))

    return pl.pallas_call(
        body,
        out_shape=jax.ShapeDtypeStruct((rows, C), F32),
        grid_spec=pltpu.PrefetchScalarGridSpec(
            num_scalar_prefetch=1, grid=(rows // blk,),
            in_specs=[entry(k) for k in range(n)],
            out_specs=pl.BlockSpec((blk, C), lambda i, c: (i, 0))),
        name=name,
        compiler_params=pltpu.CompilerParams(dimension_semantics=("parallel",), vmem_limit_bytes=VMEM_LIMIT),
    )(me, own, *([land] * (n - 1)))


def _adamw_update(wv, gv, mv, vv):
    nm = ADAM_B1 * mv + (1.0 - ADAM_B1) * gv
    nv = ADAM_B2 * vv + (1.0 - ADAM_B2) * (gv * gv)
    c1 = 1.0 / (1.0 - ADAM_B1 ** ADAM_STEP)
    c2 = 1.0 / (1.0 - ADAM_B2 ** ADAM_STEP)
    return -ADAM_LR * ((nm * c1) / (jnp.sqrt(nv * c2) + ADAM_EPS) + ADAM_WD * wv), nm, nv


def sum_adamw(own, land, me, off, blk, w, m, v, name):
    rows, C = w.shape
    n = land.shape[0]
    ob = off // blk

    def body(c_ref, *refs):
        w_ref, m_ref, v_ref, g_out, d_out, m_out, v_out = refs[n:]
        gv = _sum_terms(refs[:n])
        g_out[...] = gv
        d_out[...], m_out[...], v_out[...] = _adamw_update(w_ref[...], gv, m_ref[...], v_ref[...])

    def entry(flip):
        return pl.BlockSpec((None, blk, C), lambda i, c: (c[0] ^ flip, ob + i, 0))

    plain = pl.BlockSpec((blk, C), lambda i, c: (i, 0))
    return pl.pallas_call(
        body,
        out_shape=[jax.ShapeDtypeStruct((rows, C), F32)] * 4,
        grid_spec=pltpu.PrefetchScalarGridSpec(
            num_scalar_prefetch=1, grid=(rows // blk,),
            in_specs=[entry(k) for k in range(n)] + [plain, plain, plain],
            out_specs=[plain] * 4),
        name=name,
        compiler_params=pltpu.CompilerParams(dimension_semantics=("parallel",), vmem_limit_bytes=VMEM_LIMIT),
    )(me, own, *([land] * (n - 1)), w, m, v)


def adamw(w, g, m, v, name):
    R, C = w.shape
    tr = R
    for cand in (256, 128, 64, 32, 16, 8):
        if R % cand == 0 and R > cand:
            tr = cand
            break

    def body(w_ref, g_ref, m_ref, v_ref, d_ref, nm_ref, nv_ref):
        d_ref[...], nm_ref[...], nv_ref[...] = _adamw_update(w_ref[...], g_ref[...], m_ref[...], v_ref[...])

    spec = ((tr, C), lambda i: (i, 0))
    out = ((R, C), F32) + spec
    return _call(name, body, (R // tr,), [(w,) + spec, (g,) + spec, (m,) + spec, (v,) + spec],
                 [out, out, out], sem=("parallel",))


def rms_fwd(x, g, name):
    S, D = x.shape
    tr = 512

    def body(x_ref, g_ref, o_ref):
        xv = x_ref[...]
        r = lax.rsqrt(jnp.mean(xv * xv, axis=-1, keepdims=True) + EPS)
        o_ref[...] = (xv * r * g_ref[...]).astype(BF16)

    return _call(name, body, (S // tr,),
                 [(x, (tr, D), lambda i: (i, 0)), (g, (1, D), lambda i: (0, 0))],
                 [((S, D), BF16, (tr, D), lambda i: (i, 0))], sem=("parallel",))[0]


def _rms_bwd_tile(dn, xv, gv):
    r = lax.rsqrt(jnp.mean(xv * xv, axis=-1, keepdims=True) + EPS)
    xh = xv * r
    dxh = dn * gv
    dx = r * (dxh - xh * jnp.mean(dxh * xh, axis=-1, keepdims=True))
    return dx, dn * xh


def final_loss(x, tgt, g, name):
    S, D = x.shape
    tr = 256

    def body(x_ref, t_ref, g_ref, l_ref, dx_ref, dxb_ref, dg_ref):
        i = pl.program_id(0)
        xv, gv = x_ref[...], g_ref[...]
        r = lax.rsqrt(jnp.mean(xv * xv, axis=-1, keepdims=True) + EPS)
        xh = xv * r
        e = xh * gv - t_ref[...]
        part = 0.5 * jnp.sum(jnp.sum(e * e, axis=-1, keepdims=True) * (1.0 / D), axis=0, keepdims=True)
        dy = e * (1.0 / D)
        dxh = dy * gv
        dx = r * (dxh - xh * jnp.mean(dxh * xh, axis=-1, keepdims=True))
        dx_ref[...] = dx
        dxb_ref[...] = dx.astype(BF16)
        dgp = jnp.sum(dy * xh, axis=0, keepdims=True)

        @pl.when(i == 0)
        def _():
            l_ref[...] = jnp.broadcast_to(part, l_ref.shape)
            dg_ref[...] = dgp

        @pl.when(i > 0)
        def _():
            l_ref[...] += jnp.broadcast_to(part, l_ref.shape)
            dg_ref[...] += dgp

    row = ((tr, D), lambda i: (i, 0))
    return _call(name, body, (S // tr,),
                 [(x,) + row, (tgt,) + row, (g, (1, D), lambda i: (0, 0))],
                 [((1, LANES), F32, (1, LANES), lambda i: (0, 0)), ((S, D), F32) + row, ((S, D), BF16) + row,
                  ((1, D), F32, (1, D), lambda i: (0, 0))], sem=("arbitrary",))


FFN_TF = 4 * FFN_SHARD


def _ffn_pick(G, which):
    if isinstance(G, tuple):
        return (G[0], which) if which < 2 else (G[1], 0)
    return G, which


def _ffn_w_spec(G, which, imap):
    arr, blk = _ffn_pick(G, which)
    return (arr, (4, FFN_SHARD, arr.shape[2]), lambda *idx: (imap(*idx), blk, 0))


def _ffn_whole_w_spec(G, which):
    arr, blk = _ffn_pick(G, which)
    return (arr, (N_DEV, FFN_SHARD, arr.shape[2]), lambda *idx: (0, blk, 0))


def ffn_up(n, G, name):
    S, D = n.shape
    F = N_DEV * FFN_SHARD
    tm = 1024

    def body(n_ref, w1_ref, w3_ref, abh_ref):
        nv = n_ref[...]
        a = _dot(nv, w1_ref[...].reshape(FFN_TF, D), 1, 1).astype(BF16)
        b = _dot(nv, w3_ref[...].reshape(FFN_TF, D), 1, 1).astype(BF16)
        abh_ref[0] = a
        abh_ref[1] = b
        av, bv = a.astype(F32), b.astype(F32)
        abh_ref[2] = (av * _sigmoid(av) * bv).astype(BF16)

    return _call(name, body, (F // FFN_TF, S // tm),
                 [(n, (tm, D), lambda j, i: (i, 0)),
                  _ffn_w_spec(G, 0, lambda j, i: j), _ffn_w_spec(G, 1, lambda j, i: j)],
                 [((3, S, F), BF16, (3, tm, FFN_TF), lambda j, i: (0, i, j))],
                 sem=("parallel", "parallel"))[0]


def ffn_down(abh, G, x, name):
    _, S, F = abh.shape
    D = x.shape[1]
    tm = 512

    def body(h_ref, w2_ref, x_ref, o_ref):
        o_ref[...] = x_ref[...] + 0.5 * _dot(h_ref[...], w2_ref[...].reshape(F, D))

    return _call(name, body, (S // tm,),
                 [(abh, (None, tm, F), lambda i: (2, i, 0)), _ffn_whole_w_spec(G, 2),
                  (x, (tm, D), lambda i: (i, 0))],
                 [((S, D), F32, (tm, D), lambda i: (i, 0))], sem=("parallel",))[0]


def ffn_bwd_weights(dxo, abh, n, G, name):
    _, S, F = abh.shape
    D = dxo.shape[1]
    tm = 512
    nf = F // FFN_TF

    def down_body(d_ref, w2_ref, ab_ref, o_ref):
        dh = 0.5 * _dot(d_ref[...].astype(BF16), w2_ref[...].reshape(FFN_TF, D), 1, 1)
        av, bv = ab_ref[0].astype(F32), ab_ref[1].astype(F32)
        sig = _sigmoid(av)
        o_ref[0] = (dh * bv * (sig * (1.0 + av * (1.0 - sig)))).astype(BF16)
        o_ref[1] = (dh * (av * sig)).astype(BF16)

    dab = _call(name + "_down_bwd", down_body, (nf, S // tm),
                [(dxo, (tm, D), lambda j, i: (i, 0)), _ffn_w_spec(G, 2, lambda j, i: j),
                 (abh, (2, tm, FFN_TF), lambda j, i: (0, i, j))],
                [((2, S, F), BF16, (2, tm, FFN_TF), lambda j, i: (0, i, j))],
                sem=("parallel", "parallel"))[0]

    tk = WGRAD_TK
    nk = S // tk
    gshape = (N_DEV, 3 * FFN_SHARD, D)

    def dw2_body(h_ref, d_ref, o_ref, acc_ref):
        k = pl.program_id(1)
        p = _dot(h_ref[...], d_ref[...].astype(BF16), 0, 0)

        @pl.when(k == 0)
        def _():
            acc_ref[...] = p

        @pl.when(k > 0)
        def _():
            acc_ref[...] += p

        @pl.when(k == nk - 1)
        def _():
            o_ref[...] = (0.5 * acc_ref[...]).astype(BF16).reshape(4, FFN_SHARD, D)

    gw = _call(name + "_dw2", dw2_body, (nf, nk),
               [(abh, (None, tk, FFN_TF), lambda j, k: (2, k, j)), (dxo, (tk, D), lambda j, k: (k, 0))],
               [(gshape, BF16, (4, FFN_SHARD, D), lambda j, k: (j, 2, 0))],
               scratch=[pltpu.VMEM((FFN_TF, D), F32)], sem=("parallel", "arbitrary"))[0]

    def dw13_body(gw_ref, dab_ref, n_ref, o_ref):
        o_ref[...] = _dot(dab_ref[...], n_ref[...], 0, 0).astype(BF16).reshape(4, FFN_SHARD, D)

    gw = pl.pallas_call(
        dw13_body,
        out_shape=jax.ShapeDtypeStruct(gshape, BF16),
        grid=(2, nf),
        in_specs=[pl.BlockSpec(memory_space=pl.ANY),
                  pl.BlockSpec((None, S, FFN_TF), lambda w, j: (w, 0, j)),
                  pl.BlockSpec((S, D), lambda w, j: (0, 0))],
        out_specs=pl.BlockSpec((4, FFN_SHARD, D), lambda w, j: (j, w, 0)),
        input_output_aliases={0: 0},
        name=name + "_dw13",
        compiler_params=pltpu.CompilerParams(dimension_semantics=("parallel", "parallel"),
                                             vmem_limit_bytes=VMEM_LIMIT),
    )(gw, dab, n)
    return dab, gw


def ffn_bwd_input(dab, G, x_in, g, dxo, name, as_operand=True):
    _, S, F = dab.shape
    D = x_in.shape[1]
    tm = 256

    def dn_body(dab_ref, w1_ref, w3_ref, x_ref, d_ref, g_ref, dx_ref, *rest):
        dg_ref = rest[-1]
        i = pl.program_id(0)
        dn = _dot(dab_ref[0], w1_ref[...].reshape(F, D)) + _dot(dab_ref[1], w3_ref[...].reshape(F, D))
        dx, dgt = _rms_bwd_tile(dn, x_ref[...], g_ref[...])
        dx = d_ref[...] + dx
        dx_ref[...] = dx
        if as_operand:
            rest[0][...] = dx.astype(BF16)
        dgp = jnp.sum(dgt, axis=0, keepdims=True)

        @pl.when(i == 0)
        def _():
            dg_ref[...] = dgp

        @pl.when(i > 0)
        def _():
            dg_ref[...] += dgp

    tile = ((tm, D), lambda i: (i, 0))
    return _call(name + "_dn", dn_body, (S // tm,),
                 [(dab, (2, tm, F), lambda i: (0, i, 0)),
                  _ffn_whole_w_spec(G, 0), _ffn_whole_w_spec(G, 1),
                  (x_in,) + tile, (dxo,) + tile, (g, (1, D), lambda i: (0, 0))],
                 [((S, D), F32) + tile] + ([((S, D), BF16) + tile] if as_operand else [])
                 + [((1, D), F32, (1, D), lambda i: (0, 0))],
                 sem=("arbitrary",))


PROJ_TN = 512
DH_SHARDS_PER_STEP = 4


def in_proj(h, Gm, first_tile, n_tiles, dtype, name, tile_stride=1):
    S, D = h.shape
    tile = lambda j: first_tile + tile_stride * j

    def body(h_ref, w_ref, o_ref):
        o_ref[...] = _dot(h_ref[...], w_ref[...]).astype(dtype)

    return _call(name, body, (n_tiles,),
                 [(h, (S, D), lambda j: (0, 0)),
                  (Gm, (None, D, PROJ_TN), lambda j: (tile(j) // 2, 0, tile(j) % 2))],
                 [((S, n_tiles * PROJ_TN), dtype, (S, PROJ_TN), lambda j: (0, j))],
                 sem=("parallel",))[0]


def _dproj_pieces(dqkv, dq_b, dkv_b, dgate):
    pieces = [(dqkv[g], [(3 * which + g, (which, 0)) for which in range(3)]) for g in range(3)]
    pieces.append((dq_b, [(9, (None, 0)), (10, (None, 1))]))
    pieces.append((dkv_b, [(11, (None, 0))]))
    pieces.append((dgate, [(12 + 2 * a + b, (a, b)) for a in range(2) for b in range(2)]))
    return pieces


def in_proj_bwd_dw(pieces, h, gm_grads, name):
    S, D = h.shape

    for n_piece, (arr, tiles) in enumerate(pieces):
        w_tile = [t for t, _ in tiles]
        lead = [ix[0] for _, ix in tiles]
        colb = [ix[1] for _, ix in tiles]

        def pick(table, j):
            out = table[-1]
            for k in range(len(table) - 2, -1, -1):
                out = jnp.where(j == k, table[k], out)
            return out

        def dw_body(gm_ref, h_ref, d_ref, o_ref):
            o_ref[...] = _dot(h_ref[...], d_ref[...], 0, 0).astype(BF16)

        if arr.ndim == 3:
            d_spec = pl.BlockSpec((None, S, PROJ_TN), lambda j, lead=lead, colb=colb: (pick(lead, j), 0, pick(colb, j)))
        else:
            d_spec = pl.BlockSpec((S, PROJ_TN), lambda j, colb=colb: (0, pick(colb, j)))
        gm_grads = pl.pallas_call(
            dw_body,
            out_shape=jax.ShapeDtypeStruct(gm_grads.shape, BF16),
            grid=(len(tiles),),
            in_specs=[pl.BlockSpec(memory_space=pl.ANY), pl.BlockSpec((S, D), lambda j: (0, 0)), d_spec],
            out_specs=pl.BlockSpec((None, D, PROJ_TN),
                                   lambda j, w_tile=w_tile: (pick(w_tile, j) // 2, 0, pick(w_tile, j) % 2)),
            input_output_aliases={0: 0},
            name="%s_dw%d" % (name, n_piece),
            compiler_params=pltpu.CompilerParams(dimension_semantics=("parallel",), vmem_limit_bytes=VMEM_LIMIT),
        )(gm_grads, h, arr)
    return gm_grads


def in_proj_bwd_dh(pieces, Gm, x_in, g, dres, name):
    S, D = x_in.shape
    tm = 256
    C = Gm.shape[2]
    n_sh = N_DEV
    n_p = len(pieces)

    def dh_body(*refs):
        d_refs = refs[:n_p]
        w_ref, x_ref, r_ref, g_ref, dx_ref, dxb_ref, dg_ref = refs[n_p:]
        i = pl.program_id(0)
        p = None
        for d_ref, (arr, tiles) in zip(d_refs, pieces):
            for t, (lead, colb) in tiles:
                cols = slice(colb * PROJ_TN, (colb + 1) * PROJ_TN)
                d = d_ref[:, cols] if lead is None else d_ref[lead, :, cols]
                wcol = (t % 2) * PROJ_TN
                term = _dot(d, w_ref[t // 2, :, wcol:wcol + PROJ_TN], 1, 1)
                p = term if p is None else p + term
        dx, dgt = _rms_bwd_tile(p, x_ref[...], g_ref[...])
        dx = r_ref[...] + dx
        dx_ref[...] = dx
        dxb_ref[...] = dx.astype(BF16)
        dgp = jnp.sum(dgt, axis=0, keepdims=True)

        @pl.when(i == 0)
        def _():
            dg_ref[...] = dgp

        @pl.when(i > 0)
        def _():
            dg_ref[...] += dgp

    tile = ((tm, D), lambda i: (i, 0))

    def rows_of(arr):
        if arr.ndim == 3:
            return (arr, (arr.shape[0], tm, arr.shape[2]), lambda i: (0, i, 0))
        return (arr, (tm, arr.shape[1]), lambda i: (i, 0))

    return _call(name + "_dh", dh_body, (S // tm,),
                 [rows_of(arr) for arr, _ in pieces]
                 + [(Gm, (n_sh, D, C), lambda i: (0, 0, 0), pl.Buffered(1)),
                    (x_in,) + tile, (dres,) + tile, (g, (1, D), lambda i: (0, 0))],
                 [((S, D), F32) + tile, ((S, D), BF16) + tile, ((1, D), F32, (1, D), lambda i: (0, 0))],
                 sem=("arbitrary",))


def _t5_bucket(rel):
    n = N_BUCKETS // 2
    max_exact = n // 2
    ret = jnp.where(rel > 0, n, 0)
    a = jnp.abs(rel)
    af = jnp.maximum(a, 1).astype(F32)
    large = max_exact + (jnp.log(af / max_exact) / math.log(MAX_DISTANCE / max_exact)
                         * (n - max_exact)).astype(jnp.int32)
    large = jnp.minimum(large, n - 1)
    return ret + jnp.where(a < max_exact, a, large)


def _bucket_tables():
    qi = jnp.arange(A_TQ, dtype=jnp.int32)[:, None]
    kj = jnp.arange(A_WIN, dtype=jnp.int32)[None, :]
    rel = kj - HALF_WINDOW - qi
    return jnp.stack([_t5_bucket(rel * d) for d in DILATIONS], axis=0)


def bias_build(rel_bias, buckets):
    def body(tab_ref, bk_ref, o_ref):
        col = pl.program_id(0) * HEADS_PER_GROUP_A + pl.program_id(1)
        bk = bk_ref[...]
        acc = jnp.zeros(bk.shape, F32)
        for b in range(N_BUCKETS):
            acc = jnp.where(bk == b, tab_ref[b, col], acc)
        qi = lax.broadcasted_iota(jnp.int32, bk.shape, 0)
        kj = lax.broadcasted_iota(jnp.int32, bk.shape, 1)
        band = jnp.where(jnp.abs(kj - HALF_WINDOW - qi) <= HALF_WINDOW, acc, NEG_INF)
        o_ref[0] = jnp.where(kj >= HALF_WINDOW, band, NEG_INF)
        o_ref[1] = band
        o_ref[2] = jnp.where(kj < A_TQ + HALF_WINDOW, band, NEG_INF)

    out = pl.pallas_call(
        body,
        out_shape=jax.ShapeDtypeStruct((3, HEADS_PER_GROUP_A // 2, 3, 2, A_TQ, A_WIN), F32),
        grid=(3, HEADS_PER_GROUP_A),
        in_specs=[pl.BlockSpec(memory_space=pltpu.SMEM),
                  pl.BlockSpec((None, A_TQ, A_WIN), lambda g, h: (g, 0, 0))],
        out_specs=pl.BlockSpec((None, None, 3, None, A_TQ, A_WIN), lambda g, h: (g, h // 2, 0, h % 2, 0, 0)),
        name="a_bias_build",
        compiler_params=pltpu.CompilerParams(dimension_semantics=("parallel", "parallel")),
    )(rel_bias, buckets)
    return out.reshape(3, HEADS_PER_GROUP_A // 2, 3, 2 * A_TQ, A_WIN)


def bias_bwd(dbias, buckets):
    def body(d_ref, bk_ref, o_ref):
        bk = bk_ref[...]
        dv = d_ref[...]
        for b in range(N_BUCKETS):
            part = jnp.sum(jnp.where(bk == b, dv, 0.0), axis=1, keepdims=True)
            o_ref[b:b + 1, :] = jnp.broadcast_to(jnp.sum(part, axis=0, keepdims=True), (1, LANES))

    out = pl.pallas_call(
        body,
        out_shape=jax.ShapeDtypeStruct((3, HEADS_PER_GROUP_A, N_BUCKETS, LANES), F32),
        grid=(3, HEADS_PER_GROUP_A),
        in_specs=[pl.BlockSpec((None, None, A_TQ, A_WIN), lambda g, h: (g, h, 0, 0)),
                  pl.BlockSpec((None, A_TQ, A_WIN), lambda g, h: (g, 0, 0))],
        out_specs=pl.BlockSpec((None, None, N_BUCKETS, LANES), lambda g, h: (g, h, 0, 0)),
        name="a_bias_bwd",
        compiler_params=pltpu.CompilerParams(dimension_semantics=("parallel", "parallel")),
    )(dbias, buckets)
    return out[:, :, :, 0].transpose(2, 0, 1).reshape(N_BUCKETS, 3 * HEADS_PER_GROUP_A)


def _a_fill_padded(pad_ref, src_ref, n, pad):
    zeros = jnp.zeros((pad, LANES), pad_ref.dtype)
    pad_ref[0:pad, :] = zeros
    pad_ref[pad + n:2 * pad + n, :] = zeros
    pad_ref[pad:pad + n, :] = src_ref[...].astype(pad_ref.dtype)


def _a_stack_heads(x, lane):
    zero = jnp.zeros_like(x)
    return jnp.concatenate([jnp.where(lane < HEAD_DIM_A, x, zero), jnp.where(lane >= HEAD_DIM_A, x, zero)], axis=0)


def _a_bias_variant(qb, nqb):
    return jnp.where(qb == 0, 0, jnp.where(qb == nqb - 1, 2, 1))


def a_fwd(proj_g, bias_g, g, name):
    S = proj_g.shape[0]
    d = DILATIONS[g]
    L = S // d
    nqb = L // A_TQ
    pad = HALF_WINDOW * d

    def body(q_ref, k_ref, v_ref, b_ref, o_ref, l_ref, qf, kpad, vpad):
        qf[...] = q_ref[...].astype(F32) * A_SCALE
        _a_fill_padded(kpad, k_ref, S, pad)
        _a_fill_padded(vpad, v_ref, S, pad)
        lane = lax.broadcasted_iota(jnp.int32, (A_TQ, LANES), 1)

        def block(t, carry):
            qb, r = t // d, t % d
            start = qb * (A_TQ * d) + r
            kw = kpad[pl.ds(start, A_WIN, stride=d), :].astype(BF16)
            vw = vpad[pl.ds(start, A_WIN, stride=d), :].astype(BF16)
            q = qf[pl.ds(start, A_TQ, stride=d), :].astype(BF16)
            q2 = _a_stack_heads(q, lane)
            s = _dot(q2, kw, 1, 1) + b_ref[_a_bias_variant(qb, nqb)]
            m = jnp.max(s, axis=-1, keepdims=True)
            e = jnp.exp(s - m)
            l = jnp.sum(e, axis=-1, keepdims=True)
            o2 = _dot(e.astype(BF16), vw) / l
            lse2 = m + jnp.log(l)
            o_ref[pl.ds(start, A_TQ, stride=d), :] = jnp.where(lane < HEAD_DIM_A, o2[0:A_TQ], o2[A_TQ:])
            l_ref[pl.ds(start, A_TQ, stride=d), :] = jnp.where(lane < HEAD_DIM_A, lse2[0:A_TQ], lse2[A_TQ:])
            return carry

        lax.fori_loop(0, nqb * d, block, 0, unroll=A_UNROLL)

    out_spec = ((S, GROUP_WIDTH_A), F32, (S, LANES), lambda hp: (0, hp))
    return _call(name, body, (4,),
                 [(proj_g, (S, LANES), lambda hp: (0, hp)),
                  (proj_g, (S, LANES), lambda hp: (0, 4 + hp)),
                  (proj_g, (S, LANES), lambda hp: (0, 8 + hp)),
                  (bias_g, (None, 3, 2 * A_TQ, A_WIN), lambda hp: (hp, 0, 0, 0))],
                 [out_spec, out_spec],
                 scratch=[pltpu.VMEM((S, LANES), F32)] + [pltpu.VMEM((S + 2 * pad, LANES), F32)] * 2,
                 sem=("parallel",))


def a_combine(outs, lses, name):
    S, W = outs[0].shape
    tr = 512

    def body(o0, o1, o2, l0, l1, l2, oa_ref, lt_ref):
        a, b, c = l0[...], l1[...], l2[...]
        m = jnp.maximum(jnp.maximum(a, b), c)
        ea, eb, ec = jnp.exp(a - m), jnp.exp(b - m), jnp.exp(c - m)
        z = ea + eb + ec
        oa_ref[...] = ((ea * o0[...] + eb * o1[...] + ec * o2[...]) / z).astype(BF16)
        lt_ref[...] = m + jnp.log(z)

    spec = ((tr, W), lambda i: (i, 0))
    return _call(name, body, (S // tr,), [(a,) + spec for a in (*outs, *lses)],
                 [((S, W), BF16) + spec, ((S, W), F32) + spec], sem=("parallel",))


def a_bwd(proj_g, bias_g, do_a, o_a, lse_tot, g, name):
    S = proj_g.shape[0]
    d = DILATIONS[g]
    L = S // d
    nqb = L // A_TQ
    pad = HALF_WINDOW * d

    def body(q_ref, k_ref, v_ref, b_ref, do_ref, o_ref, l_ref, dqkv_ref, db_ref,
             qf, of, dqf, kpad, vpad, dkacc, dvacc):
        qf[...] = q_ref[...].astype(F32) * A_SCALE
        of[...] = o_ref[...].astype(F32)
        _a_fill_padded(kpad, k_ref, S, pad)
        _a_fill_padded(vpad, v_ref, S, pad)
        dkacc[...] = jnp.zeros(dkacc.shape, F32)
        dvacc[...] = jnp.zeros(dvacc.shape, F32)
        db_ref[...] = jnp.zeros(db_ref.shape, F32)
        lane = lax.broadcasted_iota(jnp.int32, (A_TQ, LANES), 1)

        def block(t, carry):
            qb, r = t // d, t % d
            start = qb * (A_TQ * d) + r
            rows = pl.ds(start, A_TQ, stride=d)
            win = pl.ds(start, A_WIN, stride=d)
            kw = kpad[win, :].astype(BF16)
            vw = vpad[win, :].astype(BF16)
            q = qf[rows, :].astype(BF16)
            do = do_ref[rows, :]
            ov = of[rows, :]
            lt = l_ref[rows, :]
            q2 = _a_stack_heads(q, lane)
            do2 = _a_stack_heads(do, lane)
            lt2 = jnp.concatenate([lt[:, 0:1], lt[:, HEAD_DIM_A:HEAD_DIM_A + 1]], axis=0)
            s = _dot(q2, kw, 1, 1) + b_ref[_a_bias_variant(qb, nqb)]
            p = jnp.exp(s - lt2)
            t = jnp.sum(do2 * jnp.concatenate([ov, ov], axis=0), axis=-1, keepdims=True)
            dob2 = do2.astype(BF16)
            ds = p * (_dot(dob2, vw, 1, 1) - t)
            db_ref[...] += ds
            dsb = ds.astype(BF16)
            dq2 = _dot(dsb, kw)
            dqf[rows, :] = jnp.where(lane < HEAD_DIM_A, dq2[0:A_TQ], dq2[A_TQ:]) * A_SCALE
            dkacc[win, :] += _dot(dsb, q2, 0, 0)
            dvacc[win, :] += _dot(p.astype(BF16), dob2, 0, 0)
            return carry

        lax.fori_loop(0, nqb * d, block, 0, unroll=A_UNROLL)
        dqkv_ref[0] = dqf[...].astype(BF16)
        dqkv_ref[1] = dkacc[pad:pad + S, :].astype(BF16)
        dqkv_ref[2] = dvacc[pad:pad + S, :].astype(BF16)

    slab = ((S, LANES), lambda hp: (0, hp))
    padded = pltpu.VMEM((S + 2 * pad, LANES), F32)
    return _call(
        name, body, (4,),
        [(proj_g, (S, LANES), lambda hp: (0, hp)),
         (proj_g, (S, LANES), lambda hp: (0, 4 + hp)),
         (proj_g, (S, LANES), lambda hp: (0, 8 + hp)),
         (bias_g, (None, 3, 2 * A_TQ, A_WIN), lambda hp: (hp, 0, 0, 0)),
         (do_a,) + slab, (o_a,) + slab, (lse_tot,) + slab],
        [((3, S, GROUP_WIDTH_A), BF16, (3, S, LANES), lambda hp: (0, 0, hp)),
         ((4, 2 * A_TQ, A_WIN), F32, (None, 2 * A_TQ, A_WIN), lambda hp: (hp, 0, 0))],
        scratch=[pltpu.VMEM((S, LANES), F32)] * 3 + [padded] * 4,
        sem=("parallel",))


def _rope_tables(S):
    rows = S // GRID_W
    row = jnp.repeat(jnp.arange(rows, dtype=F32), GRID_W)
    col = jnp.tile(jnp.arange(GRID_W, dtype=F32), rows)
    n_freq = HEAD_DIM_B // 4
    freq = ROPE_THETA ** (-jnp.arange(n_freq, dtype=F32) / n_freq)
    ang = jnp.concatenate([row[:, None] * freq, col[:, None] * freq], axis=-1)
    cos, sin = jnp.cos(ang), jnp.sin(ang)
    return jnp.repeat(cos, 2, axis=-1), jnp.stack([-sin, sin], axis=-1).reshape(S, HEAD_DIM_B)


def _swap_pairs(y):
    lane = lax.broadcasted_iota(jnp.int32, y.shape, 1)
    return jnp.where(lane % 2 == 0, pltpu.roll(y, LANES - 1, 1), pltpu.roll(y, 1, 1))


def qkv_prep(proj_b, gains, cos_t, sin_t, name):
    S = proj_b.shape[0]
    ts = 256
    n_rot = N_HEADS_B + N_KV_B
    nh = n_rot + N_KV_B
    W = nh * LANES

    def body(x_ref, g_ref, c_ref, s_ref, o_ref):
        cv, sv = c_ref[...], s_ref[...]
        for hb in range(nh):
            cols = slice(hb * LANES, (hb + 1) * LANES)
            xv = x_ref[:, cols]
            if hb < n_rot:
                r = lax.rsqrt(jnp.mean(xv * xv, axis=-1, keepdims=True) + EPS)
                yv = xv * r * g_ref[:, cols]
                o_ref[:, cols] = (yv * cv + _swap_pairs(yv) * sv).astype(BF16)
            else:
                o_ref[:, cols] = xv.astype(BF16)

    return _call(name, body, (S // ts,),
                 [(proj_b, (ts, W), lambda i: (i, 0)), (gains, (1, W), lambda i: (0, 0)),
                  (cos_t, (ts, LANES), lambda i: (i, 0)), (sin_t, (ts, LANES), lambda i: (i, 0))],
                 [((S, W), BF16, (ts, W), lambda i: (i, 0))],
                 sem=("parallel",))[0]


def qk_prep_bwd(dr, proj_b, col0, gain, cos_t, sin_t, name):
    S, W = dr.shape
    H = W // LANES
    ts = 256
    xb = (col0 * LANES) // W

    def body(d_ref, x_ref, g_ref, c_ref, s_ref, dx_ref, dg_ref):
        i = pl.program_id(0)
        cv, sv, gv = c_ref[...], s_ref[...], g_ref[...]
        dgp = jnp.zeros((1, LANES), F32)
        for hb in range(H):
            cols = slice(hb * LANES, (hb + 1) * LANES)
            dout = d_ref[:, cols]
            dy = dout * cv + _swap_pairs(dout * sv)
            dx, dgt = _rms_bwd_tile(dy, x_ref[:, cols], gv)
            dx_ref[:, cols] = dx.astype(BF16)
            dgp = dgp + jnp.sum(dgt, axis=0, keepdims=True)

        @pl.when(i == 0)
        def _():
            dg_ref[...] = dgp

        @pl.when(i > 0)
        def _():
            dg_ref[...] += dgp

    return _call(name, body, (S // ts,),
                 [(dr, (ts, W), lambda i: (i, 0)), (proj_b, (ts, W), lambda i: (i, xb)),
                  (gain, (1, LANES), lambda i: (0, 0)),
                  (cos_t, (ts, LANES), lambda i: (i, 0)), (sin_t, (ts, LANES), lambda i: (i, 0))],
                 [((S, W), BF16, (ts, W), lambda i: (i, 0)),
                  ((1, LANES), F32, (1, LANES), lambda i: (0, 0))],
                 sem=("arbitrary",))


def _row_sums(x):
    hi = x.astype(BF16)
    lo = (x - hi.astype(F32)).astype(BF16)
    ones = jnp.ones((8, LANES), BF16)
    return (_dot(ones, hi, 1, 1) + _dot(ones, lo, 1, 1))[0:1, :]


def flash_fwd(qkv, name):
    S = qkv.shape[0]
    tq = B_TQ_FWD
    scale = HEAD_DIM_B ** -0.5

    hps = B_HEADS_PER_STEP

    def body(q_ref, k_ref, v_ref, o_ref, l_ref):
        k, v = k_ref[...], v_ref[...]
        for j in range(hps):
            cols = slice(j * LANES, (j + 1) * LANES)
            s = _dot(q_ref[:, cols], k, 1, 1)
            m = jnp.max(s, axis=-1, keepdims=True)
            e = jnp.exp2((s - m) * (scale * LOG2E))
            l = jnp.sum(e, axis=-1, keepdims=True)
            o_ref[:, cols] = (_dot(e.astype(BF16), v) / l).astype(BF16)
            lse = jnp.broadcast_to(m * scale + jnp.log(l), (tq, LANES))
            l_ref[j] = _row_sums(lse) * (1.0 / LANES)

    per = GQA_GROUP_B // hps
    heads = lambda g, h, i: (i, g * per + h)
    return _call(name, body, (N_KV_B, per, S // tq),
                 [(qkv, (tq, hps * LANES), heads),
                  (qkv, (S, LANES), lambda g, h, i: (0, N_HEADS_B + g)),
                  (qkv, (S, LANES), lambda g, h, i: (0, N_HEADS_B + N_KV_B + g))],
                 [((S, N_HEADS_B * LANES), BF16, (tq, hps * LANES), heads),
                  ((N_HEADS_B, 1, S), F32, (hps, 1, tq), lambda g, h, i: (g * per + h, 0, i))],
                 sem=("parallel", "parallel", "parallel"))


def flash_bwd(qkv, k_t, do_b, o_b, lse, name):
    S = qkv.shape[0]
    tq = B_TQ_BWD
    nq = S // tq
    scale = HEAD_DIM_B ** -0.5

    def body(q_ref, k_ref, v_ref, kt_ref, do_ref, o_ref, l_ref, dq_ref, dk_ref, dv_ref, dkacc, dvacc):
        h, i = pl.program_id(1), pl.program_id(2)

        @pl.when((h == 0) & (i == 0))
        def _():
            dkacc[...] = jnp.zeros(dkacc.shape, F32)
            dvacc[...] = jnp.zeros(dvacc.shape, F32)

        q = q_ref[...]
        do = do_ref[...]
        dob = do.astype(BF16)
        t = _row_sums(do * o_ref[...].astype(F32))
        pt = jnp.exp2(_dot(k_ref[...], q, 1, 1) * (scale * LOG2E) - l_ref[...] * LOG2E)
        dsb = (pt * (_dot(v_ref[...], dob, 1, 1) - t)).astype(BF16)
        dvacc[...] += _dot(pt.astype(BF16), dob)
        dkacc[...] += _dot(dsb, q)
        dq_ref[...] = _dot(kt_ref[...], dsb).T * scale

        @pl.when((h == GQA_GROUP_B - 1) & (i == nq - 1))
        def _():
            dk_ref[...] = dkacc[...] * scale
            dv_ref[...] = dvacc[...].astype(BF16)

    head = lambda g, h, i: (i, g * GQA_GROUP_B + h)
    return _call(name, body, (N_KV_B, GQA_GROUP_B, nq),
                 [(qkv, (tq, LANES), head),
                  (qkv, (S, LANES), lambda g, h, i: (0, N_HEADS_B + g)),
                  (qkv, (S, LANES), lambda g, h, i: (0, N_HEADS_B + N_KV_B + g)),
                  (k_t, (LANES, S), lambda g, h, i: (g, 0)),
                  (do_b, (tq, LANES), head), (o_b, (tq, LANES), head),
                  (lse, (None, 1, tq), lambda g, h, i: (g * GQA_GROUP_B + h, 0, i))],
                 [((S, N_HEADS_B * LANES), F32, (tq, LANES), head),
                  ((S, N_KV_B * LANES), F32, (S, LANES), lambda g, h, i: (0, g)),
                  ((S, N_KV_B * LANES), BF16, (S, LANES), lambda g, h, i: (0, g))],
                 scratch=[pltpu.VMEM((S, LANES), F32)] * 2,
                 sem=("parallel", "arbitrary", "arbitrary"))


MERGE_TN = 512


def _mix_rows_spec(Gm, row0, n_slots, slot_map, cols=None, col_map=None):
    C = Gm.shape[2] if cols is None else cols
    cm = (lambda *idx: 0) if col_map is None else col_map
    return (Gm, (n_slots, LANES, C), lambda *idx: (slot_map(*idx), row0 // LANES, cm(*idx)))


def merge_fwd(o_a, o_b, w_a, Gm, proj_b, b_gate, name):
    S = o_a.shape[0]
    D = w_a.shape[1]
    tm, tn = 512, MERGE_TN
    ga0, gb0 = PB_GATE_A // tn, PB_GATE_B // tn

    def body(oa_ref, ob_ref, wa_ref, wb_ref, pa_ref, pb_ref, ba_ref, bb_ref, m_ref, ya_ref, yb_ref):
        ya = _dot(oa_ref[...], wa_ref[...])
        yb = _dot(ob_ref[...], wb_ref[...].reshape(N_DEV * LANES, tn))
        ga = _sigmoid(pa_ref[...] + ba_ref[...])
        gb = _sigmoid(pb_ref[...] + bb_ref[...])
        m_ref[...] = (ga * ya + gb * yb).astype(BF16)
        ya_ref[...] = ya.astype(BF16)
        yb_ref[...] = yb.astype(BF16)

    out = ((S, D), BF16, (tm, tn), lambda j, i: (i, j))
    return _call(name, body, (D // tn, S // tm),
                 [(o_a, (tm, o_a.shape[1]), lambda j, i: (i, 0)), (o_b, (tm, o_b.shape[1]), lambda j, i: (i, 0)),
                  (w_a, (w_a.shape[0], tn), lambda j, i: (0, j)),
                  _mix_rows_spec(Gm, REST_WB, N_DEV, lambda j, i: 0, cols=tn, col_map=lambda j, i: j),
                  (proj_b, (tm, tn), lambda j, i: (i, ga0 + j)), (proj_b, (tm, tn), lambda j, i: (i, gb0 + j)),
                  (b_gate, (1, tn), lambda j, i: (0, j)), (b_gate, (1, tn), lambda j, i: (0, D // tn + j))],
                 [out, out, out], sem=("parallel", "parallel"))


def out_proj(merged, Gm, x, name):
    S, D = x.shape
    tm, tn = 512, MERGE_TN

    def body(m_ref, w_ref, x_ref, o_ref):
        o_ref[...] = x_ref[...] + _dot(m_ref[...], w_ref[...].reshape(N_DEV * LANES, tn))

    return _call(name, body, (D // tn, S // tm),
                 [(merged, (tm, D), lambda j, i: (i, 0)),
                  _mix_rows_spec(Gm, REST_WOUT, N_DEV, lambda j, i: 0, cols=tn, col_map=lambda j, i: j),
                  (x, (tm, tn), lambda j, i: (i, j))],
                 [((S, D), F32, (tm, tn), lambda j, i: (i, j))], sem=("parallel", "parallel"))[0]


def merge_bwd(dx2, Gm, ya, yb, proj_b, b_gate, name):
    S, D = dx2.shape
    tm, tn = 512, MERGE_TN
    nn = D // tn
    ga0, gb0 = PB_GATE_A // tn, PB_GATE_B // tn

    def body(d_ref, w_ref, ya_ref, yb_ref, pa_ref, pb_ref, ba_ref, bb_ref, dya_ref, dyb_ref, dg_ref, dbg_ref):
        i = pl.program_id(1)
        dm = _dot(d_ref[...].astype(BF16), w_ref[...].reshape(tn, D), 1, 1)
        ga = _sigmoid(pa_ref[...] + ba_ref[...])
        gb = _sigmoid(pb_ref[...] + bb_ref[...])
        dya_ref[...] = (dm * ga).astype(BF16)
        dyb_ref[...] = (dm * gb).astype(BF16)
        dpa = dm * ya_ref[...].astype(F32) * ga * (1.0 - ga)
        dpb = dm * yb_ref[...].astype(F32) * gb * (1.0 - gb)
        dg_ref[0] = dpa.astype(BF16)
        dg_ref[1] = dpb.astype(BF16)
        sa = jnp.sum(dpa, axis=0, keepdims=True)
        sb = jnp.sum(dpb, axis=0, keepdims=True)

        @pl.when(i == 0)
        def _():
            dbg_ref[0] = sa
            dbg_ref[1] = sb

        @pl.when(i > 0)
        def _():
            dbg_ref[0] += sa
            dbg_ref[1] += sb

    tile = ((tm, tn), lambda j, i: (i, j))
    dya, dyb, dgate, dbg = _call(
        name, body, (nn, S // tm),
        [(dx2, (tm, D), lambda j, i: (i, 0)),
         _mix_rows_spec(Gm, REST_WOUT, tn // LANES, lambda j, i: j),
         (ya,) + tile, (yb,) + tile,
         (proj_b, (tm, tn), lambda j, i: (i, ga0 + j)), (proj_b, (tm, tn), lambda j, i: (i, gb0 + j)),
         (b_gate, (1, tn), lambda j, i: (0, j)), (b_gate, (1, tn), lambda j, i: (0, nn + j))],
        [((S, D), BF16) + tile, ((S, D), BF16) + tile,
         ((2, S, D), BF16, (2, tm, tn), lambda j, i: (0, i, j)),
         ((2, 1, D), F32, (2, 1, tn), lambda j, i: (0, 0, j))],
        sem=("parallel", "arbitrary"))
    return dya, dyb, dgate, dbg


def matmul_nt(a, b_spec_fn, N, name, tn=512):
    S, K = a.shape
    tm = 512

    def body(a_ref, b_ref, o_ref):
        b = b_ref[...]
        o_ref[...] = _dot(a_ref[...], b.reshape(-1, b.shape[-1]), 1, 1)

    return _call(name, body, (N // tn, S // tm),
                 [(a, (tm, K), lambda j, i: (i, 0)), b_spec_fn(lambda j, i: j)],
                 [((S, N), F32, (tm, tn), lambda j, i: (i, j))], sem=("parallel", "parallel"))[0]


def weight_grad_rows(a, b, grads, row0, name):
    S, M = a.shape
    N = b.shape[1]
    tmm = 512
    tk = WGRAD_TK
    nk = S // tk

    def body(g_ref, a_ref, b_ref, o_ref, acc_ref):
        k = pl.program_id(1)
        p = _dot(a_ref[...], b_ref[...].astype(BF16), 0, 0)

        @pl.when(k == 0)
        def _():
            acc_ref[...] = p

        @pl.when(k > 0)
        def _():
            acc_ref[...] += p

        @pl.when(k == nk - 1)
        def _():
            o_ref[...] = acc_ref[...].astype(BF16).reshape(tmm // LANES, LANES, N)

    return pl.pallas_call(
        body,
        out_shape=jax.ShapeDtypeStruct(grads.shape, BF16),
        grid=(M // tmm, nk),
        in_specs=[pl.BlockSpec(memory_space=pl.ANY),
                  pl.BlockSpec((tk, tmm), lambda j, k: (k, j)),
                  pl.BlockSpec((tk, N), lambda j, k: (k, 0))],
        out_specs=pl.BlockSpec((tmm // LANES, LANES, N), lambda j, k: (j, row0 // LANES, 0)),
        scratch_shapes=[pltpu.VMEM((tmm, N), F32)],
        input_output_aliases={0: 0},
        name=name,
        compiler_params=pltpu.CompilerParams(dimension_semantics=("parallel", "arbitrary"),
                                             vmem_limit_bytes=VMEM_LIMIT),
    )(grads, a, b)


def weight_grad_plain(a, b, name):
    S, M = a.shape
    N = b.shape[1]
    tk = WGRAD_TK
    nk = S // tk

    def body(a_ref, b_ref, o_ref, acc_ref):
        k = pl.program_id(0)
        p = _dot(a_ref[...], b_ref[...], 0, 0)

        @pl.when(k == 0)
        def _():
            acc_ref[...] = p

        @pl.when(k > 0)
        def _():
            acc_ref[...] += p

        @pl.when(k == nk - 1)
        def _():
            o_ref[...] = acc_ref[...].astype(BF16)

    return _call(name, body, (nk,),
                 [(a, (tk, M), lambda k: (k, 0)), (b, (tk, N), lambda k: (k, 0))],
                 [((M, N), BF16, (M, N), lambda k: (0, 0))],
                 scratch=[pltpu.VMEM((M, N), F32)], sem=("arbitrary",))[0]


def local_step(x, tgt, p, get_g1_up, get_g1_down, get_gm_in, get_gm_rest, get_g2, emit, start_token):
    S, D = x.shape
    after = lambda t: t[0:1, 0:1]
    buckets = _bucket_tables()
    cos_t, sin_t = _rope_tables(S)
    gains = jnp.concatenate([jnp.tile(p["q_norm"], (1, N_HEADS_B)), jnp.tile(p["k_norm"], (1, N_KV_B)),
                             jnp.ones((1, N_KV_B * LANES), F32)], axis=1)

    n1 = rms_fwd(x, p["ffn1_norm"] + after(start_token), "ffn1_norm")
    bias = bias_build(p["rel_bias"] + after(start_token), buckets)
    g1_up = get_g1_up((n1, bias))
    ab1 = ffn_up(n1, (g1_up, None), "ffn1_up")
    G1 = (g1_up, get_g1_down(ab1))
    x1 = ffn_down(ab1, G1, x, "ffn1_down")

    hm = rms_fwd(x1, p["mix_norm"], "mix_norm")
    Gw = get_gm_in(hm)
    n_a = A_QKV_WIDTH // PROJ_TN
    proj_a = [in_proj(hm, Gw, g, 3, BF16, "in_proj_a%d" % g, tile_stride=3) for g in range(3)]
    proj_b = in_proj(hm, Gw, n_a, PB_WIDTH // PROJ_TN, F32, "in_proj_b")

    outs, lses = [], []
    for g in range(3):
        o, l = a_fwd(proj_a[g], bias[g], g, "a_fwd_%d" % g)
        outs.append(o)
        lses.append(l)
    o_a, lse_tot = a_combine(outs, lses, "a_combine")

    qkv = qkv_prep(proj_b, gains, cos_t, sin_t, "qkv_prep")
    k_t = qkv[:, N_HEADS_B * LANES:(N_HEADS_B + N_KV_B) * LANES].T
    o_b, lse_b = flash_fwd(qkv, "flash_fwd")

    Gm = get_gm_rest(o_b)
    w_a = Gm[:, REST_WA:REST_ROWS, :].reshape(N_DEV, GROUP_WIDTH_A, LANES).transpose(1, 0, 2).reshape(GROUP_WIDTH_A, D)
    merged, ya, yb = merge_fwd(o_a, o_b, w_a, Gm, proj_b, p["b_gate"], "merge_fwd")
    x2 = out_proj(merged, Gm, x1, "out_proj")

    G2 = get_g2(x2)
    n2 = rms_fwd(x2, p["ffn2_norm"], "ffn2_norm")
    ab2 = ffn_up(n2, G2, "ffn2_up")
    x3 = ffn_down(ab2, G2, x2, "ffn2_down")

    loss, dx3, dx3_b, d_final = final_loss(x3, tgt, p["final_norm"], "final_loss")

    dabh2, gw2 = ffn_bwd_weights(dx3_b, ab2, n2, G2, "ffn2_bwd")
    t2 = emit("ffn2", gw2)
    dx2, dx2_b, d_ffn2_norm = ffn_bwd_input(dabh2, G2, x2, p["ffn2_norm"] + after(t2), dx3, "ffn2_bwd")

    dya, dyb, dgate, dbg = merge_bwd(dx2_b, Gm, ya, yb, proj_b, p["b_gate"], "merge_bwd")
    gm_grads = jnp.zeros((N_DEV, MIX_ROWS, D), BF16)
    gm_grads = weight_grad_rows(merged, dx2_b, gm_grads, MIX_WOUT, "dw_out")
    gm_grads = weight_grad_rows(o_b, dyb, gm_grads, MIX_WB, "dw_branch_b")
    dw_a = weight_grad_plain(o_a, dya, "dw_branch_a")
    do_a = matmul_nt(dya, lambda jm: (w_a, (MERGE_TN, D), lambda j, i: (jm(j, i), 0)), GROUP_WIDTH_A, "do_a")
    do_b = matmul_nt(dyb, lambda jm: _mix_rows_spec(Gm, REST_WB, MERGE_TN // LANES, jm), N_HEADS_B * LANES, "do_b")

    dq_r, dk_r, dv_b = flash_bwd(qkv, k_t, do_b, o_b, lse_b, "flash_bwd")
    dq_b, d_q_norm = qk_prep_bwd(dq_r, proj_b, 0, p["q_norm"], cos_t, sin_t, "q_prep_bwd")
    dk_b, d_k_norm = qk_prep_bwd(dk_r, proj_b, N_HEADS_B, p["k_norm"], cos_t, sin_t, "k_prep_bwd")

    dqkv, dbs = [], []
    for g in range(3):
        dg_, db = a_bwd(proj_a[g], bias[g], do_a, o_a, lse_tot, g, "a_bwd_%d" % g)
        dqkv.append(dg_)
        dbs.append(db)
    d_rel_bias = bias_bwd(jnp.stack(dbs, axis=0).reshape(3, HEADS_PER_GROUP_A, A_TQ, A_WIN), buckets)

    dproj = _dproj_pieces(dqkv, dq_b, jnp.concatenate([dk_b, dv_b], axis=1), dgate)
    gm_grads = in_proj_bwd_dw(dproj, hm, gm_grads, "in_proj_bwd")
    dw_a_sh = dw_a.reshape(GROUP_WIDTH_A, N_DEV, LANES).transpose(1, 0, 2).reshape(N_DEV, MIX_ROWS - MIX_WA, D)
    gm_grads = lax.dynamic_update_slice(gm_grads, dw_a_sh, (0, MIX_WA, 0))
    tm = emit("mix", gm_grads)
    dx1, dx1_b, d_mix_norm = in_proj_bwd_dh(dproj, Gw, x1, p["mix_norm"] + after(tm), dx2, "in_proj_bwd")

    dabh1, gw1 = ffn_bwd_weights(dx1_b, ab1, n1, G1, "ffn1_bwd")
    t1 = emit("ffn1", gw1)
    dx0, d_ffn1_norm = ffn_bwd_input(dabh1, G1, x, p["ffn1_norm"] + after(t1), dx1, "ffn1_bwd", as_operand=False)

    small = dict(ffn1_norm=d_ffn1_norm, mix_norm=d_mix_norm, b_gate=dbg.reshape(1, 2 * D),
                 q_norm=d_q_norm, k_norm=d_k_norm, rel_bias=d_rel_bias, ffn2_norm=d_ffn2_norm,
                 final_norm=d_final)
    return loss, dx0, small


def _pack_small(t, loss_row):
    row6 = jnp.concatenate([t["q_norm"].reshape(1, -1), t["k_norm"].reshape(1, -1), t["rel_bias"].reshape(1, -1)], axis=1)
    return jnp.concatenate([t["ffn1_norm"].reshape(1, -1), t["mix_norm"].reshape(1, -1), t["b_gate"].reshape(2, -1),
                            t["ffn2_norm"].reshape(1, -1), t["final_norm"].reshape(1, -1), row6, loss_row], axis=0)


def _unpack_small(a, shapes):
    return dict(ffn1_norm=a[0:1].reshape(shapes["ffn1_norm"]), mix_norm=a[1:2].reshape(shapes["mix_norm"]),
                b_gate=a[2:4].reshape(shapes["b_gate"]), ffn2_norm=a[4:5].reshape(shapes["ffn2_norm"]),
                final_norm=a[5].reshape(shapes["final_norm"]), q_norm=a[6:7, 0:128].reshape(shapes["q_norm"]),
                k_norm=a[6:7, 128:256].reshape(shapes["k_norm"]), rel_bias=a[6, 256:1024].reshape(shapes["rel_bias"]))


SMALL = ("ffn1_norm", "mix_norm", "b_gate", "q_norm", "k_norm", "rel_bias", "ffn2_norm", "final_norm")
ORDER = ("ffn1_norm", "ffn1_w1", "ffn1_w3", "ffn1_w2", "mix_norm", "w_in", "b_gate", "q_norm", "k_norm", "rel_bias",
         "w_branch_a", "w_branch_b", "w_out", "ffn2_norm", "ffn2_w1", "ffn2_w3", "ffn2_w2", "final_norm")


def kernel(x, ffn1_norm, ffn1_w1, ffn1_w3, ffn1_w2, mix_norm, w_in, b_gate, q_norm, k_norm, rel_bias, w_branch_a, w_branch_b, w_out, ffn2_norm, ffn2_w1, ffn2_w3, ffn2_w2, final_norm, loss_target, m_ffn1_norm, m_ffn1_w1, m_ffn1_w3, m_ffn1_w2, m_mix_norm, m_w_in, m_b_gate, m_q_norm, m_k_norm, m_rel_bias, m_w_branch_a, m_w_branch_b, m_w_out, m_ffn2_norm, m_ffn2_w1, m_ffn2_w3, m_ffn2_w2, m_final_norm, v_ffn1_norm, v_ffn1_w1, v_ffn1_w3, v_ffn1_w2, v_mix_norm, v_w_in, v_b_gate, v_q_norm, v_k_norm, v_rel_bias, v_w_branch_a, v_w_branch_b, v_w_out, v_ffn2_norm, v_ffn2_w1, v_ffn2_w3, v_ffn2_w2, v_final_norm):
    args = dict(locals())
    w = {n: args[n] for n in ORDER}
    m = {n: args["m_" + n] for n in ORDER}
    v = {n: args["v_" + n] for n in ORDER}
    D = x.shape[2]

    blocks = (
        ("ffn1_up", jnp.concatenate([ffn1_w1[0].T, ffn1_w3[0].T], axis=0)),
        ("ffn1_down", ffn1_w2[0]),
        ("mix_in", w_in[0]),
        ("mix_rest", jnp.concatenate([w_branch_b[0], w_out[0], w_branch_a[0].reshape(REST_ROWS - REST_WA, D)], axis=0)),
        ("ffn2", jnp.concatenate([ffn2_w1[0].T, ffn2_w3[0].T, ffn2_w2[0]], axis=0)),
    )
    direct = ("mix_rest", "ffn2")
    started = all_gather_start_all([(b.astype(BF16), tag in direct) for tag, b in blocks], "all_gather_start")
    gathers = {tag: s for (tag, _), s in zip(blocks, started)}
    start_token = started[0][4]

    def gathered(tag):
        def get(after):
            if tag in direct:
                return all_gather_place_own(*_split_wait("all_gather_" + tag + "_wait", gathers[tag], N_DEV - 1, after),
                                            "all_gather_" + tag + "_own")
            return all_gather_finish(*_split_wait("all_gather_" + tag + "_wait", gathers[tag], 4, after),
                                     "all_gather_" + tag + "_finish")
        return get

    core = lax.axis_index("c").astype(jnp.int32).reshape(1)
    chip = (2 * lax.axis_index("x") + lax.axis_index("y")).astype(jnp.int32).reshape(1)
    device = 2 * chip + core
    exchanges = {}

    def emit(tag, gw):
        if tag == "ffn1":
            (theirs,) = reduce_scatter_pair([gw], "reduce_scatter_pair_" + tag)
            part = pair_add(gw, theirs, core, "pair_add_" + tag)
            exchanges[tag] = reduce_scatter_start(part, "reduce_scatter_" + tag + "_start")
        else:
            exchanges[tag] = reduce_scatter_start_direct(gw, "reduce_scatter_" + tag + "_start")
        return exchanges[tag][4]

    small_p = dict(ffn1_norm=ffn1_norm, mix_norm=mix_norm, b_gate=b_gate, q_norm=q_norm, k_norm=k_norm,
                   rel_bias=rel_bias, ffn2_norm=ffn2_norm, final_norm=final_norm.reshape(1, D))
    loss_p, grad_x, small_g = local_step(x[0], loss_target[0], small_p, gathered("ffn1_up"), gathered("ffn1_down"),
                                         gathered("mix_in"), gathered("mix_rest"), gathered("ffn2"), emit, start_token)

    def landed(tag, after):
        n_others, me = (3, chip) if tag == "ffn1" else (N_DEV - 1, device)
        return tuple(_split_wait("reduce_scatter_" + tag + "_wait", exchanges[tag], n_others, after)) + (me,)

    grads, delta, new_m, new_v = {}, {}, {}, {}

    def finish(n, part, land, me, off, blk, transposed=False):
        shp = w[n].shape
        if transposed:
            to2 = lambda a: a.reshape(shp[-2], shp[-1]).T
            back = lambda a: a.T.reshape(shp)
        else:
            to2 = lambda a: a.reshape(shp[-2], shp[-1])
            back = lambda a: a.reshape(shp)
        res = sum_adamw(part, land, me, off, blk, to2(w[n]), to2(m[n]), to2(v[n]), "update_" + n)
        grads[n], delta[n], new_m[n], new_v[n] = [back(a) for a in res]

    last_token = exchanges["ffn1"][4]
    for tag, after in (("ffn2", last_token), ("ffn1", grad_x)):
        group = landed(tag, after)
        finish(tag + "_w1", *group, 0, FFN_SHARD, transposed=True)
        finish(tag + "_w3", *group, FFN_SHARD, FFN_SHARD, transposed=True)
        finish(tag + "_w2", *group, 2 * FFN_SHARD, FFN_SHARD)
        if tag == "ffn2":
            group_m = landed("mix", last_token)
            finish("w_in", *group_m, MIX_WIN, LANES)
            finish("w_branch_b", *group_m, MIX_WB, LANES)
            finish("w_out", *group_m, MIX_WOUT, LANES)
            grads["w_branch_a"] = sum_landed(*group_m, MIX_WA, MIX_ROWS - MIX_WA, MIX_ROWS - MIX_WA,
                                             "w_branch_a_sum").reshape(w_branch_a.shape)
    loss_row = jnp.pad(loss_p, ((0, 0), (0, D - LANES)))
    smalls = small_all_gather(_pack_small(small_g, loss_row))
    small_sum = sum_slots(smalls, 0, N_DEV, N_DEV, "small_sum")
    small_shapes = {n: w[n].shape for n in SMALL}
    grads.update(_unpack_small(small_sum, small_shapes))
    loss = small_sum[7, 0]

    n = "w_branch_a"
    two_d = lambda a: a.reshape(w[n].shape[-2], w[n].shape[-1])
    d_, m_, v_ = adamw(two_d(w[n]), two_d(grads[n]), two_d(m[n]), two_d(v[n]), "adamw_" + n)
    delta[n], new_m[n], new_v[n] = [a.reshape(w[n].shape) for a in (d_, m_, v_)]
    zero_row = jnp.zeros((1, D), F32)
    pack = lambda t: _pack_small({n: t[n] for n in SMALL}, zero_row)
    d_, m_, v_ = adamw(pack(w), small_sum, pack(m), pack(v), "adamw_small")
    for src, dst in ((d_, delta), (m_, new_m), (v_, new_v)):
        dst.update(_unpack_small(src, small_shapes))

    return (loss, grad_x[None], *[grads[n] for n in ORDER], *[delta[n] for n in ORDER],
            *[new_m[n] for n in ORDER], *[new_v[n] for n in ORDER])
```

```python
import math

import jax
import jax.numpy as jnp
from jax import lax
from jax.experimental import pallas as pl
from jax.experimental.pallas import tpu as pltpu

F32 = jnp.float32
BF16 = jnp.bfloat16
MESH = pl.DeviceIdType.MESH

V7X_VMEM_BYTES = 64 * 1024 * 1024
VMEM_LIMIT = V7X_VMEM_BYTES - 8 * 1024 * 1024
LANES = 128

N_DEV = 8
EPS = 1e-6
NEG_INF = -1e30

DILATIONS = (1, 4, 16)
HALF_WINDOW = 64
HEAD_DIM_A = 64
HEADS_PER_GROUP_A = 8
GROUP_WIDTH_A = 512
A_QKV_WIDTH = 4608
A_GROUP_QKV = A_QKV_WIDTH // 3
A_TQ = 128
A_WIN = A_TQ + 2 * HALF_WINDOW
A_UNROLL = 8
A_SCALE = HEAD_DIM_A ** -0.5
WGRAD_TK = 2048
HEAD_DIM_B = 128
N_HEADS_B = 8
N_KV_B = 2
GQA_GROUP_B = 4
GRID_W = 64
ROPE_THETA = 10000.0
B_TQ_FWD = 256
B_TQ_BWD = 512
B_HEADS_PER_STEP = 4
LOG2E = 1.4426950408889634
N_BUCKETS = 32
MAX_DISTANCE = 1024
PB_WIDTH = 3584
PB_GATE_A = 1536
PB_GATE_B = 2560

ADAM_LR = 0.001
ADAM_B1 = 0.9
ADAM_B2 = 0.999
ADAM_EPS = 1e-08
ADAM_WD = 0.01
ADAM_STEP = 10

FFN_SHARD = 352
MIX_WIN, MIX_WB, MIX_WOUT, MIX_WA = 0, 1024, 1152, 1280
MIX_ROWS = 1344
REST_WB, REST_WOUT, REST_WA, REST_ROWS = 0, 128, 256, 320


def _dot(a, b, ca=1, cb=0):
    return lax.dot_general(a, b, (((ca,), (cb,)), ((), ())), preferred_element_type=F32)


def _call(name, body, grid, ins, outs, scratch=(), sem=None, aliases=None):
    ins = [tuple(i) + (None,) * (4 - len(i)) for i in ins]
    res = pl.pallas_call(
        body,
        out_shape=[jax.ShapeDtypeStruct(s, d) for (s, d, _, _) in outs],
        grid=grid,
        in_specs=[pl.BlockSpec(bs, im, pipeline_mode=pm) for (_, bs, im, pm) in ins],
        out_specs=[pl.BlockSpec(bs, im) for (_, _, bs, im) in outs],
        scratch_shapes=list(scratch),
        name=name,
        input_output_aliases=aliases or {},
        compiler_params=pltpu.CompilerParams(dimension_semantics=sem, vmem_limit_bytes=VMEM_LIMIT),
    )(*[i[0] for i in ins])
    return res


def _sigmoid(x):
    return 0.5 * jnp.tanh(0.5 * x) + 0.5


def _position():
    return lax.axis_index("x"), lax.axis_index("y"), lax.axis_index("c")


def _hbm_specs(n):
    return [pl.BlockSpec(memory_space=pl.ANY) for _ in range(n)]


PAIR_BUFFERS = 4


def reduce_scatter_pair(grads, name):
    n = len(grads)
    C = grads[0].shape[2]
    half = [g.shape[1] // 2 for g in grads]
    chunks = [(i, q, hf) for i in range(n) for q in range(4) for hf in range(2)]
    nb = PAIR_BUFFERS

    def body(*refs):
        ins, theirs = refs[:n], refs[n:2 * n]
        buf, load_sems, send_sems, recv_sems = refs[2 * n:]
        x, y, c = _position()
        sibling = (x, y, 1 - c)

        def load(k):
            i, q, hf = chunks[k]
            r = half[i]
            return pltpu.make_async_copy(ins[i].at[2 * q + (1 - c), pl.ds(hf * r, r), :],
                                         buf.at[k % nb, pl.ds(0, r), :], load_sems.at[k % nb])

        def send(k):
            i, q, hf = chunks[k]
            r = half[i]
            return pltpu.make_async_remote_copy(
                src_ref=buf.at[k % nb, pl.ds(0, r), :], dst_ref=theirs[i].at[q, pl.ds(hf * r, r), :],
                send_sem=send_sems.at[k % nb], recv_sem=recv_sems.at[i],
                device_id=sibling, device_id_type=MESH)

        for k in range(len(chunks) + 1):
            if k < len(chunks):
                if k >= nb:
                    send(k - nb).wait_send()
                load(k).start()
            if k >= 1:
                load(k - 1).wait()
                send(k - 1).start()
        for k in range(max(0, len(chunks) - nb), len(chunks)):
            send(k).wait_send()
        for i in range(n):
            pltpu.make_async_remote_copy(
                src_ref=theirs[i], dst_ref=theirs[i], send_sem=send_sems.at[0], recv_sem=recv_sems.at[i],
                device_id=sibling, device_id_type=MESH).wait_recv()

    return pl.pallas_call(
        body,
        out_shape=[jax.ShapeDtypeStruct((4,) + g.shape[1:], g.dtype) for g in grads],
        in_specs=_hbm_specs(n),
        out_specs=_hbm_specs(n),
        scratch_shapes=[pltpu.VMEM((nb, max(half), C), grads[0].dtype), pltpu.SemaphoreType.DMA((nb,)),
                        pltpu.SemaphoreType.DMA((nb,)), pltpu.SemaphoreType.DMA((n,))],
        name=name,
        compiler_params=pltpu.CompilerParams(vmem_limit_bytes=VMEM_LIMIT),
    )(*grads)


_HBM_SPEC = pl.BlockSpec(memory_space=pltpu.HBM)
_SEM_SPEC = pl.BlockSpec(memory_space=pltpu.SEMAPHORE)
_TOKEN_SPEC = pl.BlockSpec(memory_space=pltpu.VMEM)
_DATAFLOW = pltpu.SideEffectType.DATAFLOW_SIDE_EFFECTING


def _split_start_many(name, exchanges):
    n = len(exchanges)

    def full_body(*refs):
        srcs, lands = refs[:n], refs[n:2 * n]
        sems = refs[2 * n:4 * n]
        token = refs[-1]
        for i, (body, _, _) in enumerate(exchanges):
            body(srcs[i], lands[i], sems[2 * i], sems[2 * i + 1])
        token[...] = jnp.zeros_like(token)

    srcs = [pltpu.with_memory_space_constraint(src, pltpu.HBM) for _, src, _ in exchanges]
    lands = [pltpu.with_memory_space_constraint(lax.empty(shape, src.dtype), pltpu.HBM)
             for _, src, shape in exchanges]
    res = pl.pallas_call(
        full_body, name=name,
        out_shape=(pltpu.SemaphoreType.DMA(()),) * (2 * n)
        + tuple(pltpu.HBM(a.shape, a.dtype) for a in srcs + lands) + (jax.ShapeDtypeStruct((8, LANES), F32),),
        in_specs=(_HBM_SPEC,) * (2 * n),
        out_specs=(_SEM_SPEC,) * (2 * n) + (_HBM_SPEC,) * (2 * n) + (_TOKEN_SPEC,),
        input_output_aliases={i: 2 * n + i for i in range(2 * n)},
        compiler_params=pltpu.CompilerParams(has_side_effects=_DATAFLOW),
    )(*srcs, *lands)
    return [(res[2 * i], res[2 * i + 1], res[2 * n + i], res[3 * n + i], res[-1]) for i in range(n)]


def _split_start(name, body, src, land_shape):
    return _split_start_many(name, [(body, src, land_shape)])[0]


def _split_wait(name, started, n_blocks, after):
    send_sem, recv_sem, src_thru, land_thru, _ = started
    after = after if isinstance(after, tuple) else (after,)

    def body(src_ref, land_ref, send_sem, recv_sem, *rest):
        x, y, c = _position()
        blocks = land_ref.at[pl.ds(0, n_blocks)]
        copy = pltpu.make_async_remote_copy(src_ref=blocks, dst_ref=blocks, send_sem=send_sem, recv_sem=recv_sem,
                                            device_id=(x, y, c), device_id_type=MESH)
        copy.wait_send()
        copy.wait_recv()

    return pl.pallas_call(
        body, name=name,
        out_shape=(pltpu.HBM(src_thru.shape, src_thru.dtype), pltpu.HBM(land_thru.shape, land_thru.dtype)),
        in_specs=(_HBM_SPEC, _HBM_SPEC, _SEM_SPEC, _SEM_SPEC) + (pl.BlockSpec(memory_space=pl.ANY),) * len(after),
        out_specs=(_HBM_SPEC, _HBM_SPEC),
        input_output_aliases={0: 0, 1: 1},
        compiler_params=pltpu.CompilerParams(has_side_effects=_DATAFLOW),
    )(src_thru, land_thru, send_sem, recv_sem, *after)


def all_gather_start_all(blocks, name):
    def starter(direct):
        def body(b_ref, land_ref, send_sem, recv_sem):
            x, y, c = _position()
            peers = _other_devices(x, y, c) if direct else [(x, y, 1 - c), (1 - x, y, c), (x, 1 - y, c),
                                                            (1 - x, 1 - y, c)]
            for peer in peers:
                pltpu.make_async_remote_copy(src_ref=b_ref, dst_ref=land_ref.at[4 * x + 2 * y + c],
                                             send_sem=send_sem, recv_sem=recv_sem,
                                             device_id=peer, device_id_type=MESH).start()
        return body

    return _split_start_many(name, [(starter(direct), block, (N_DEV,) + block.shape) for block, direct in blocks])


def all_gather_finish(block, land, name):
    R, C = block.shape

    def body(b_ref, land_in, land_ref, stage, load_sems, send_sems, recv_sems, own_sem):
        x, y, c = _position()
        sibling = (x, y, 1 - c)
        chips = [(1 - x, y), (x, 1 - y), (1 - x, 1 - y)]
        own_in = pltpu.make_async_copy(b_ref, stage.at[3], load_sems.at[3])
        own_in.start()
        loads = [pltpu.make_async_copy(land_in.at[4 * px + 2 * py + c], stage.at[j], load_sems.at[j])
                 for j, (px, py) in enumerate(chips)]
        for ld in loads:
            ld.start()
        sends = []
        for j, (px, py) in enumerate(chips):
            loads[j].wait()
            dst = land_ref.at[4 * px + 2 * py + c]
            cp = pltpu.make_async_remote_copy(src_ref=stage.at[j], dst_ref=dst, send_sem=send_sems.at[j],
                                              recv_sem=recv_sems.at[j], device_id=sibling, device_id_type=MESH)
            cp.start()
            sends.append(cp)
        own_in.wait()
        own_out = pltpu.make_async_copy(stage.at[3], land_ref.at[4 * x + 2 * y + c], own_sem)
        own_out.start()
        for j, (px, py) in enumerate(chips):
            dst = land_ref.at[4 * px + 2 * py + (1 - c)]
            pltpu.make_async_remote_copy(src_ref=stage.at[j], dst_ref=dst, send_sem=send_sems.at[j],
                                         recv_sem=recv_sems.at[j], device_id=sibling,
                                         device_id_type=MESH).wait_recv()
        for cp in sends:
            cp.wait_send()
        own_out.wait()

    return pl.pallas_call(
        body,
        out_shape=jax.ShapeDtypeStruct(land.shape, land.dtype),
        in_specs=_hbm_specs(2),
        out_specs=pl.BlockSpec(memory_space=pl.ANY),
        scratch_shapes=[pltpu.VMEM((4, R, C), block.dtype), pltpu.SemaphoreType.DMA((4,)),
                        pltpu.SemaphoreType.DMA((3,)), pltpu.SemaphoreType.DMA((3,)), pltpu.SemaphoreType.DMA],
        input_output_aliases={1: 0},
        name=name,
        compiler_params=pltpu.CompilerParams(vmem_limit_bytes=VMEM_LIMIT),
    )(block, land)


def reduce_scatter_start(parts, name):
    def body(p_ref, land_ref, send_sem, recv_sem):
        x, y, c = _position()
        for px, py in [(1 - x, y), (x, 1 - y), (1 - x, 1 - y)]:
            pltpu.make_async_remote_copy(src_ref=p_ref.at[2 * px + py], dst_ref=land_ref.at[2 * x + y],
                                         send_sem=send_sem, recv_sem=recv_sem,
                                         device_id=(px, py, c), device_id_type=MESH).start()

    return _split_start(name, body, parts, parts.shape)


def _other_devices(x, y, c):
    return [(1 - x if k & 4 else x, 1 - y if k & 2 else y, 1 - c if k & 1 else c) for k in range(1, N_DEV)]


def all_gather_place_own(block, land, name):
    R, C = block.shape

    def body(b_ref, land_in, land_ref, stage, sems):
        x, y, c = _position()
        load = pltpu.make_async_copy(b_ref, stage, sems.at[0])
        load.start()
        load.wait()
        store = pltpu.make_async_copy(stage, land_ref.at[4 * x + 2 * y + c], sems.at[1])
        store.start()
        store.wait()

    return pl.pallas_call(
        body,
        out_shape=jax.ShapeDtypeStruct(land.shape, land.dtype),
        in_specs=_hbm_specs(2),
        out_specs=pl.BlockSpec(memory_space=pl.ANY),
        scratch_shapes=[pltpu.VMEM((R, C), block.dtype), pltpu.SemaphoreType.DMA((2,))],
        input_output_aliases={1: 0},
        name=name,
    )(block, land)


def reduce_scatter_start_direct(grads, name):
    def body(g_ref, land_ref, send_sem, recv_sem):
        x, y, c = _position()
        for px, py, pc in _other_devices(x, y, c):
            pltpu.make_async_remote_copy(src_ref=g_ref.at[4 * px + 2 * py + pc],
                                         dst_ref=land_ref.at[4 * x + 2 * y + c],
                                         send_sem=send_sem, recv_sem=recv_sem,
                                         device_id=(px, py, pc), device_id_type=MESH).start()

    return _split_start(name, body, grads, grads.shape)


def small_all_gather(small):
    def body(small_ref, smalls, s_send, s_recv, s_local):
        x, y, c = _position()
        me = 4 * x + 2 * y + c
        lc = pltpu.make_async_copy(small_ref, smalls.at[me], s_local)
        lc.start()
        remote = []
        k = 0
        for dx in (0, 1):
            for dy in (0, 1):
                for dc in (0, 1):
                    if dx + dy + dc == 0:
                        continue
                    peer = (1 - x if dx else x, 1 - y if dy else y, 1 - c if dc else c)
                    rc = pltpu.make_async_remote_copy(
                        src_ref=small_ref, dst_ref=smalls.at[me],
                        send_sem=s_send.at[k], recv_sem=s_recv.at[k],
                        device_id=peer, device_id_type=MESH)
                    rc.start()
                    remote.append(rc)
                    k += 1
        for rc in remote:
            rc.wait()
        lc.wait()

    return pl.pallas_call(
        body,
        out_shape=jax.ShapeDtypeStruct((N_DEV,) + small.shape, small.dtype),
        in_specs=_hbm_specs(1),
        out_specs=pl.BlockSpec(memory_space=pl.ANY),
        scratch_shapes=[pltpu.SemaphoreType.DMA((7,)), pltpu.SemaphoreType.DMA((7,)), pltpu.SemaphoreType.DMA],
        name="small_all_gather",
    )(small)


def pair_add(grads, theirs, core, name):
    _, R, C = theirs.shape
    tr = R // 2

    def body(c_ref, a_ref, b_ref, o_ref):
        o_ref[...] = (a_ref[...].astype(F32) + b_ref[...].astype(F32)).astype(BF16)

    return pl.pallas_call(
        body,
        out_shape=jax.ShapeDtypeStruct(theirs.shape, BF16),
        grid_spec=pltpu.PrefetchScalarGridSpec(
            num_scalar_prefetch=1, grid=(4, R // tr),
            in_specs=[pl.BlockSpec((None, tr, C), lambda q, i, c: (2 * q + c[0], i, 0)),
                      pl.BlockSpec((None, tr, C), lambda q, i, c: (q, i, 0))],
            out_specs=pl.BlockSpec((None, tr, C), lambda q, i, c: (q, i, 0))),
        name=name,
        compiler_params=pltpu.CompilerParams(dimension_semantics=("parallel", "parallel"),
                                             vmem_limit_bytes=VMEM_LIMIT),
    )(core, grads, theirs)


def sum_slots(recv, off, rows, blk, name):
    nq, _, C = recv.shape
    ob = off // blk

    def body(r_ref, o_ref):
        acc = r_ref[0].astype(F32)
        for q in range(1, nq):
            acc = acc + r_ref[q].astype(F32)
        o_ref[...] = acc

    return _call(name, body, (rows // blk,),
                 [(recv, (nq, blk, C), lambda i: (0, ob + i, 0))],
                 [((rows, C), F32, (blk, C), lambda i: (i, 0))], sem=("parallel",))[0]


def _sum_terms(refs):
    acc = refs[0][...].astype(F32)
    for r in refs[1:]:
        acc = acc + r[...].astype(F32)
    return acc


def sum_landed(own, land, me, off, rows, blk, name):
    n, _, C = land.shape
    ob = off // blk

    def body(c_ref, *refs):
        refs[n][...] = _sum_terms(refs[:n])

    def entry(flip):
        return pl.BlockSpec((None, blk, C), lambda i, c: (c[0] ^ flip, ob + i, 0))

    return pl.pallas_call(
        body,
        out_shape=jax.ShapeDtypeStruct((rows, C), F32),
        grid_spec=pltpu.PrefetchScalarGridSpec(
            num_scalar_prefetch=1, grid=(rows // blk,),
            in_specs=[entry(k) for k in range(n)],
            out_specs=pl.BlockSpec((blk, C), lambda i, c: (i, 0))),
        name=name,
        compiler_params=pltpu.CompilerParams(dimension_semantics=("parallel",), vmem_limit_bytes=VMEM_LIMIT),
    )(me, own, *([land] * (n - 1)))


def _adamw_update(wv, gv, mv, vv):
    nm = ADAM_B1 * mv + (1.0 - ADAM_B1) * gv
    nv = ADAM_B2 * vv + (1.0 - ADAM_B2) * (gv * gv)
    c1 = 1.0 / (1.0 - ADAM_B1 ** ADAM_STEP)
    c2 = 1.0 / (1.0 - ADAM_B2 ** ADAM_STEP)
    return -ADAM_LR * ((nm * c1) / (jnp.sqrt(nv * c2) + ADAM_EPS) + ADAM_WD * wv), nm, nv


def sum_adamw(own, land, me, off, blk, w, m, v, name):
    rows, C = w.shape
    n = land.shape[0]
    ob = off // blk

    def body(c_ref, *refs):
        w_ref, m_ref, v_ref, g_out, d_out, m_out, v_out = refs[n:]
        gv = _sum_terms(refs[:n])
        g_out[...] = gv
        d_out[...], m_out[...], v_out[...] = _adamw_update(w_ref[...], gv, m_ref[...], v_ref[...])

    def entry(flip):
        return pl.BlockSpec((None, blk, C), lambda i, c: (c[0] ^ flip, ob + i, 0))

    plain = pl.BlockSpec((blk, C), lambda i, c: (i, 0))
    return pl.pallas_call(
        body,
        out_shape=[jax.ShapeDtypeStruct((rows, C), F32)] * 4,
        grid_spec=pltpu.PrefetchScalarGridSpec(
            num_scalar_prefetch=1, grid=(rows // blk,),
            in_specs=[entry(k) for k in range(n)] + [plain, plain, plain],
            out_specs=[plain] * 4),
        name=name,
        compiler_params=pltpu.CompilerParams(dimension_semantics=("parallel",), vmem_limit_bytes=VMEM_LIMIT),
    )(me, own, *([land] * (n - 1)), w, m, v)


def adamw(w, g, m, v, name):
    R, C = w.shape
    tr = R
    for cand in (256, 128, 64, 32, 16, 8):
        if R % cand == 0 and R > cand:
            tr = cand
            break

    def body(w_ref, g_ref, m_ref, v_ref, d_ref, nm_ref, nv_ref):
        d_ref[...], nm_ref[...], nv_ref[...] = _adamw_update(w_ref[...], g_ref[...], m_ref[...], v_ref[...])

    spec = ((tr, C), lambda i: (i, 0))
    out = ((R, C), F32) + spec
    return _call(name, body, (R // tr,), [(w,) + spec, (g,) + spec, (m,) + spec, (v,) + spec],
                 [out, out, out], sem=("parallel",))


def rms_fwd(x, g, name):
    S, D = x.shape
    tr = 512

    def body(x_ref, g_ref, o_ref):
        xv = x_ref[...]
        r = lax.rsqrt(jnp.mean(xv * xv, axis=-1, keepdims=True) + EPS)
        o_ref[...] = (xv * r * g_ref[...]).astype(BF16)

    return _call(name, body, (S // tr,),
                 [(x, (tr, D), lambda i: (i, 0)), (g, (1, D), lambda i: (0, 0))],
                 [((S, D), BF16, (tr, D), lambda i: (i, 0))], sem=("parallel",))[0]


def _rms_bwd_tile(dn, xv, gv):
    r = lax.rsqrt(jnp.mean(xv * xv, axis=-1, keepdims=True) + EPS)
    xh = xv * r
    dxh = dn * gv
    dx = r * (dxh - xh * jnp.mean(dxh * xh, axis=-1, keepdims=True))
    return dx, dn * xh


def final_loss(x, tgt, g, name):
    S, D = x.shape
    tr = 256

    def body(x_ref, t_ref, g_ref, l_ref, dx_ref, dxb_ref, dg_ref):
        i = pl.program_id(0)
        xv, gv = x_ref[...], g_ref[...]
        r = lax.rsqrt(jnp.mean(xv * xv, axis=-1, keepdims=True) + EPS)
        xh = xv * r
        e = xh * gv - t_ref[...]
        part = 0.5 * jnp.sum(jnp.sum(e * e, axis=-1, keepdims=True) * (1.0 / D), axis=0, keepdims=True)
        dy = e * (1.0 / D)
        dxh = dy * gv
        dx = r * (dxh - xh * jnp.mean(dxh * xh, axis=-1, keepdims=True))
        dx_ref[...] = dx
        dxb_ref[...] = dx.astype(BF16)
        dgp = jnp.sum(dy * xh, axis=0, keepdims=True)

        @pl.when(i == 0)
        def _():
            l_ref[...] = jnp.broadcast_to(part, l_ref.shape)
            dg_ref[...] = dgp

        @pl.when(i > 0)
        def _():
            l_ref[...] += jnp.broadcast_to(part, l_ref.shape)
            dg_ref[...] += dgp

    row = ((tr, D), lambda i: (i, 0))
    return _call(name, body, (S // tr,),
                 [(x,) + row, (tgt,) + row, (g, (1, D), lambda i: (0, 0))],
                 [((1, LANES), F32, (1, LANES), lambda i: (0, 0)), ((S, D), F32) + row, ((S, D), BF16) + row,
                  ((1, D), F32, (1, D), lambda i: (0, 0))], sem=("arbitrary",))


FFN_TF = 4 * FFN_SHARD


def _ffn_pick(G, which):
    if isinstance(G, tuple):
        return (G[0], which) if which < 2 else (G[1], 0)
    return G, which


def _ffn_w_spec(G, which, imap):
    arr, blk = _ffn_pick(G, which)
    return (arr, (4, FFN_SHARD, arr.shape[2]), lambda *idx: (imap(*idx), blk, 0))


def _ffn_whole_w_spec(G, which):
    arr, blk = _ffn_pick(G, which)
    return (arr, (N_DEV, FFN_SHARD, arr.shape[2]), lambda *idx: (0, blk, 0))


def ffn_up(n, G, name):
    S, D = n.shape
    F = N_DEV * FFN_SHARD
    tm = 1024

    def body(n_ref, w1_ref, w3_ref, abh_ref):
        nv = n_ref[...]
        a = _dot(nv, w1_ref[...].reshape(FFN_TF, D), 1, 1).astype(BF16)
        b = _dot(nv, w3_ref[...].reshape(FFN_TF, D), 1, 1).astype(BF16)
        abh_ref[0] = a
        abh_ref[1] = b
        av, bv = a.astype(F32), b.astype(F32)
        abh_ref[2] = (av * _sigmoid(av) * bv).astype(BF16)

    return _call(name, body, (F // FFN_TF, S // tm),
                 [(n, (tm, D), lambda j, i: (i, 0)),
                  _ffn_w_spec(G, 0, lambda j, i: j), _ffn_w_spec(G, 1, lambda j, i: j)],
                 [((3, S, F), BF16, (3, tm, FFN_TF), lambda j, i: (0, i, j))],
                 sem=("parallel", "parallel"))[0]


def ffn_down(abh, G, x, name):
    _, S, F = abh.shape
    D = x.shape[1]
    tm = 512

    def body(h_ref, w2_ref, x_ref, o_ref):
        o_ref[...] = x_ref[...] + 0.5 * _dot(h_ref[...], w2_ref[...].reshape(F, D))

    return _call(name, body, (S // tm,),
                 [(abh, (None, tm, F), lambda i: (2, i, 0)), _ffn_whole_w_spec(G, 2),
                  (x, (tm, D), lambda i: (i, 0))],
                 [((S, D), F32, (tm, D), lambda i: (i, 0))], sem=("parallel",))[0]


def ffn_bwd_weights(dxo, abh, n, G, name):
    _, S, F = abh.shape
    D = dxo.shape[1]
    tm = 512
    nf = F // FFN_TF

    def down_body(d_ref, w2_ref, ab_ref, o_ref):
        dh = 0.5 * _dot(d_ref[...].astype(BF16), w2_ref[...].reshape(FFN_TF, D), 1, 1)
        av, bv = ab_ref[0].astype(F32), ab_ref[1].astype(F32)
        sig = _sigmoid(av)
        o_ref[0] = (dh * bv * (sig * (1.0 + av * (1.0 - sig)))).astype(BF16)
        o_ref[1] = (dh * (av * sig)).astype(BF16)

    dab = _call(name + "_down_bwd", down_body, (nf, S // tm),
                [(dxo, (tm, D), lambda j, i: (i, 0)), _ffn_w_spec(G, 2, lambda j, i: j),
                 (abh, (2, tm, FFN_TF), lambda j, i: (0, i, j))],
                [((2, S, F), BF16, (2, tm, FFN_TF), lambda j, i: (0, i, j))],
                sem=("parallel", "parallel"))[0]

    tk = WGRAD_TK
    nk = S // tk
    gshape = (N_DEV, 3 * FFN_SHARD, D)

    def dw2_body(h_ref, d_ref, o_ref, acc_ref):
        k = pl.program_id(1)
        p = _dot(h_ref[...], d_ref[...].astype(BF16), 0, 0)

        @pl.when(k == 0)
        def _():
            acc_ref[...] = p

        @pl.when(k > 0)
        def _():
            acc_ref[...] += p

        @pl.when(k == nk - 1)
        def _():
            o_ref[...] = (0.5 * acc_ref[...]).astype(BF16).reshape(4, FFN_SHARD, D)

    gw = _call(name + "_dw2", dw2_body, (nf, nk),
               [(abh, (None, tk, FFN_TF), lambda j, k: (2, k, j)), (dxo, (tk, D), lambda j, k: (k, 0))],
               [(gshape, BF16, (4, FFN_SHARD, D), lambda j, k: (j, 2, 0))],
               scratch=[pltpu.VMEM((FFN_TF, D), F32)], sem=("parallel", "arbitrary"))[0]

    def dw13_body(gw_ref, dab_ref, n_ref, o_ref):
        o_ref[...] = _dot(dab_ref[...], n_ref[...], 0, 0).astype(BF16).reshape(4, FFN_SHARD, D)

    gw = pl.pallas_call(
        dw13_body,
        out_shape=jax.ShapeDtypeStruct(gshape, BF16),
        grid=(2, nf),
        in_specs=[pl.BlockSpec(memory_space=pl.ANY),
                  pl.BlockSpec((None, S, FFN_TF), lambda w, j: (w, 0, j)),
                  pl.BlockSpec((S, D), lambda w, j: (0, 0))],
        out_specs=pl.BlockSpec((4, FFN_SHARD, D), lambda w, j: (j, w, 0)),
        input_output_aliases={0: 0},
        name=name + "_dw13",
        compiler_params=pltpu.CompilerParams(dimension_semantics=("parallel", "parallel"),
                                             vmem_limit_bytes=VMEM_LIMIT),
    )(gw, dab, n)
    return dab, gw


def ffn_bwd_input(dab, G, x_in, g, dxo, name, as_operand=True):
    _, S, F = dab.shape
    D = x_in.shape[1]
    tm = 256

    def dn_body(dab_ref, w1_ref, w3_ref, x_ref, d_ref, g_ref, dx_ref, *rest):
        dg_ref = rest[-1]
        i = pl.program_id(0)
        dn = _dot(dab_ref[0], w1_ref[...].reshape(F, D)) + _dot(dab_ref[1], w3_ref[...].reshape(F, D))
        dx, dgt = _rms_bwd_tile(dn, x_ref[...], g_ref[...])
        dx = d_ref[...] + dx
        dx_ref[...] = dx
        if as_operand:
            rest[0][...] = dx.astype(BF16)
        dgp = jnp.sum(dgt, axis=0, keepdims=True)

        @pl.when(i == 0)
        def _():
            dg_ref[...] = dgp

        @pl.when(i > 0)
        def _():
            dg_ref[...] += dgp

    tile = ((tm, D), lambda i: (i, 0))
    return _call(name + "_dn", dn_body, (S // tm,),
                 [(dab, (2, tm, F), lambda i: (0, i, 0)),
                  _ffn_whole_w_spec(G, 0), _ffn_whole_w_spec(G, 1),
                  (x_in,) + tile, (dxo,) + tile, (g, (1, D), lambda i: (0, 0))],
                 [((S, D), F32) + tile] + ([((S, D), BF16) + tile] if as_operand else [])
                 + [((1, D), F32, (1, D), lambda i: (0, 0))],
                 sem=("arbitrary",))


PROJ_TN = 512
DH_SHARDS_PER_STEP = 4


def in_proj(h, Gm, first_tile, n_tiles, dtype, name, tile_stride=1):
    S, D = h.shape
    tile = lambda j: first_tile + tile_stride * j

    def body(h_ref, w_ref, o_ref):
        o_ref[...] = _dot(h_ref[...], w_ref[...]).astype(dtype)

    return _call(name, body, (n_tiles,),
                 [(h, (S, D), lambda j: (0, 0)),
                  (Gm, (None, D, PROJ_TN), lambda j: (tile(j) // 2, 0, tile(j) % 2))],
                 [((S, n_tiles * PROJ_TN), dtype, (S, PROJ_TN), lambda j: (0, j))],
                 sem=("parallel",))[0]


def _dproj_pieces(dqkv, dq_b, dkv_b, dgate):
    pieces = [(dqkv[g], [(3 * which + g, (which, 0)) for which in range(3)]) for g in range(3)]
    pieces.append((dq_b, [(9, (None, 0)), (10, (None, 1))]))
    pieces.append((dkv_b, [(11, (None, 0))]))
    pieces.append((dgate, [(12 + 2 * a + b, (a, b)) for a in range(2) for b in range(2)]))
    return pieces


def in_proj_bwd_dw(pieces, h, gm_grads, name):
    S, D = h.shape

    for n_piece, (arr, tiles) in enumerate(pieces):
        w_tile = [t for t, _ in tiles]
        lead = [ix[0] for _, ix in tiles]
        colb = [ix[1] for _, ix in tiles]

        def pick(table, j):
            out = table[-1]
            for k in range(len(table) - 2, -1, -1):
                out = jnp.where(j == k, table[k], out)
            return out

        def dw_body(gm_ref, h_ref, d_ref, o_ref):
            o_ref[...] = _dot(h_ref[...], d_ref[...], 0, 0).astype(BF16)

        if arr.ndim == 3:
            d_spec = pl.BlockSpec((None, S, PROJ_TN), lambda j, lead=lead, colb=colb: (pick(lead, j), 0, pick(colb, j)))
        else:
            d_spec = pl.BlockSpec((S, PROJ_TN), lambda j, colb=colb: (0, pick(colb, j)))
        gm_grads = pl.pallas_call(
            dw_body,
            out_shape=jax.ShapeDtypeStruct(gm_grads.shape, BF16),
            grid=(len(tiles),),
            in_specs=[pl.BlockSpec(memory_space=pl.ANY), pl.BlockSpec((S, D), lambda j: (0, 0)), d_spec],
            out_specs=pl.BlockSpec((None, D, PROJ_TN),
                                   lambda j, w_tile=w_tile: (pick(w_tile, j) // 2, 0, pick(w_tile, j) % 2)),
            input_output_aliases={0: 0},
            name="%s_dw%d" % (name, n_piece),
            compiler_params=pltpu.CompilerParams(dimension_semantics=("parallel",), vmem_limit_bytes=VMEM_LIMIT),
        )(gm_grads, h, arr)
    return gm_grads


def in_proj_bwd_dh(pieces, Gm, x_in, g, dres, name):
    S, D = x_in.shape
    tm = 256
    C = Gm.shape[2]
    n_sh = N_DEV
    n_p = len(pieces)

    def dh_body(*refs):
        d_refs = refs[:n_p]
        w_ref, x_ref, r_ref, g_ref, dx_ref, dxb_ref, dg_ref = refs[n_p:]
        i = pl.program_id(0)
        p = None
        for d_ref, (arr, tiles) in zip(d_refs, pieces):
            for t, (lead, colb) in tiles:
                cols = slice(colb * PROJ_TN, (colb + 1) * PROJ_TN)
                d = d_ref[:, cols] if lead is None else d_ref[lead, :, cols]
                wcol = (t % 2) * PROJ_TN
                term = _dot(d, w_ref[t // 2, :, wcol:wcol + PROJ_TN], 1, 1)
                p = term if p is None else p + term
        dx, dgt = _rms_bwd_tile(p, x_ref[...], g_ref[...])
        dx = r_ref[...] + dx
        dx_ref[...] = dx
        dxb_ref[...] = dx.astype(BF16)
        dgp = jnp.sum(dgt, axis=0, keepdims=True)

        @pl.when(i == 0)
        def _():
            dg_ref[...] = dgp

        @pl.when(i > 0)
        def _():
            dg_ref[...] += dgp

    tile = ((tm, D), lambda i: (i, 0))

    def rows_of(arr):
        if arr.ndim == 3:
            return (arr, (arr.shape[0], tm, arr.shape[2]), lambda i: (0, i, 0))
        return (arr, (tm, arr.shape[1]), lambda i: (i, 0))

    return _call(name + "_dh", dh_body, (S // tm,),
                 [rows_of(arr) for arr, _ in pieces]
                 + [(Gm, (n_sh, D, C), lambda i: (0, 0, 0), pl.Buffered(1)),
                    (x_in,) + tile, (dres,) + tile, (g, (1, D), lambda i: (0, 0))],
                 [((S, D), F32) + tile, ((S, D), BF16) + tile, ((1, D), F32, (1, D), lambda i: (0, 0))],
                 sem=("arbitrary",))


def _t5_bucket(rel):
    n = N_BUCKETS // 2
    max_exact = n // 2
    ret = jnp.where(rel > 0, n, 0)
    a = jnp.abs(rel)
    af = jnp.maximum(a, 1).astype(F32)
    large = max_exact + (jnp.log(af / max_exact) / math.log(MAX_DISTANCE / max_exact)
                         * (n - max_exact)).astype(jnp.int32)
    large = jnp.minimum(large, n - 1)
    return ret + jnp.where(a < max_exact, a, large)


def _bucket_tables():
    qi = jnp.arange(A_TQ, dtype=jnp.int32)[:, None]
    kj = jnp.arange(A_WIN, dtype=jnp.int32)[None, :]
    rel = kj - HALF_WINDOW - qi
    return jnp.stack([_t5_bucket(rel * d) for d in DILATIONS], axis=0)


def bias_build(rel_bias, buckets):
    def body(tab_ref, bk_ref, o_ref):
        col = pl.program_id(0) * HEADS_PER_GROUP_A + pl.program_id(1)
        bk = bk_ref[...]
        acc = jnp.zeros(bk.shape, F32)
        for b in range(N_BUCKETS):
            acc = jnp.where(bk == b, tab_ref[b, col], acc)
        qi = lax.broadcasted_iota(jnp.int32, bk.shape, 0)
        kj = lax.broadcasted_iota(jnp.int32, bk.shape, 1)
        band = jnp.where(jnp.abs(kj - HALF_WINDOW - qi) <= HALF_WINDOW, acc, NEG_INF)
        o_ref[0] = jnp.where(kj >= HALF_WINDOW, band, NEG_INF)
        o_ref[1] = band
        o_ref[2] = jnp.where(kj < A_TQ + HALF_WINDOW, band, NEG_INF)

    out = pl.pallas_call(
        body,
        out_shape=jax.ShapeDtypeStruct((3, HEADS_PER_GROUP_A // 2, 3, 2, A_TQ, A_WIN), F32),
        grid=(3, HEADS_PER_GROUP_A),
        in_specs=[pl.BlockSpec(memory_space=pltpu.SMEM),
                  pl.BlockSpec((None, A_TQ, A_WIN), lambda g, h: (g, 0, 0))],
        out_specs=pl.BlockSpec((None, None, 3, None, A_TQ, A_WIN), lambda g, h: (g, h // 2, 0, h % 2, 0, 0)),
        name="a_bias_build",
        compiler_params=pltpu.CompilerParams(dimension_semantics=("parallel", "parallel")),
    )(rel_bias, buckets)
    return out.reshape(3, HEADS_PER_GROUP_A // 2, 3, 2 * A_TQ, A_WIN)


def bias_bwd(dbias, buckets):
    def body(d_ref, bk_ref, o_ref):
        bk = bk_ref[...]
        dv = d_ref[...]
        for b in range(N_BUCKETS):
            part = jnp.sum(jnp.where(bk == b, dv, 0.0), axis=1, keepdims=True)
            o_ref[b:b + 1, :] = jnp.broadcast_to(jnp.sum(part, axis=0, keepdims=True), (1, LANES))

    out = pl.pallas_call(
        body,
        out_shape=jax.ShapeDtypeStruct((3, HEADS_PER_GROUP_A, N_BUCKETS, LANES), F32),
        grid=(3, HEADS_PER_GROUP_A),
        in_specs=[pl.BlockSpec((None, None, A_TQ, A_WIN), lambda g, h: (g, h, 0, 0)),
                  pl.BlockSpec((None, A_TQ, A_WIN), lambda g, h: (g, 0, 0))],
        out_specs=pl.BlockSpec((None, None, N_BUCKETS, LANES), lambda g, h: (g, h, 0, 0)),
        name="a_bias_bwd",
        compiler_params=pltpu.CompilerParams(dimension_semantics=("parallel", "parallel")),
    )(dbias, buckets)
    return out[:, :, :, 0].transpose(2, 0, 1).reshape(N_BUCKETS, 3 * HEADS_PER_GROUP_A)


def _a_fill_padded(pad_ref, src_ref, n, pad):
    zeros = jnp.zeros((pad, LANES), pad_ref.dtype)
    pad_ref[0:pad, :] = zeros
    pad_ref[pad + n:2 * pad + n, :] = zeros
    pad_ref[pad:pad + n, :] = src_ref[...].astype(pad_ref.dtype)


def _a_stack_heads(x, lane):
    zero = jnp.zeros_like(x)
    return jnp.concatenate([jnp.where(lane < HEAD_DIM_A, x, zero), jnp.where(lane >= HEAD_DIM_A, x, zero)], axis=0)


def _a_bias_variant(qb, nqb):
    return jnp.where(qb == 0, 0, jnp.where(qb == nqb - 1, 2, 1))


def a_fwd(proj_g, bias_g, g, name):
    S = proj_g.shape[0]
    d = DILATIONS[g]
    L = S // d
    nqb = L // A_TQ
    pad = HALF_WINDOW * d

    def body(q_ref, k_ref, v_ref, b_ref, o_ref, l_ref, qf, kpad, vpad):
        qf[...] = q_ref[...].astype(F32) * A_SCALE
        _a_fill_padded(kpad, k_ref, S, pad)
        _a_fill_padded(vpad, v_ref, S, pad)
        lane = lax.broadcasted_iota(jnp.int32, (A_TQ, LANES), 1)

        def block(t, carry):
            qb, r = t // d, t % d
            start = qb * (A_TQ * d) + r
            kw = kpad[pl.ds(start, A_WIN, stride=d), :].astype(BF16)
            vw = vpad[pl.ds(start, A_WIN, stride=d), :].astype(BF16)
            q = qf[pl.ds(start, A_TQ, stride=d), :].astype(BF16)
            q2 = _a_stack_heads(q, lane)
            s = _dot(q2, kw, 1, 1) + b_ref[_a_bias_variant(qb, nqb)]
            m = jnp.max(s, axis=-1, keepdims=True)
            e = jnp.exp(s - m)
            l = jnp.sum(e, axis=-1, keepdims=True)
            o2 = _dot(e.astype(BF16), vw) / l
            lse2 = m + jnp.log(l)
            o_ref[pl.ds(start, A_TQ, stride=d), :] = jnp.where(lane < HEAD_DIM_A, o2[0:A_TQ], o2[A_TQ:])
            l_ref[pl.ds(start, A_TQ, stride=d), :] = jnp.where(lane < HEAD_DIM_A, lse2[0:A_TQ], lse2[A_TQ:])
            return carry

        lax.fori_loop(0, nqb * d, block, 0, unroll=A_UNROLL)

    out_spec = ((S, GROUP_WIDTH_A), F32, (S, LANES), lambda hp: (0, hp))
    return _call(name, body, (4,),
                 [(proj_g, (S, LANES), lambda hp: (0, hp)),
                  (proj_g, (S, LANES), lambda hp: (0, 4 + hp)),
                  (proj_g, (S, LANES), lambda hp: (0, 8 + hp)),
                  (bias_g, (None, 3, 2 * A_TQ, A_WIN), lambda hp: (hp, 0, 0, 0))],
                 [out_spec, out_spec],
                 scratch=[pltpu.VMEM((S, LANES), F32)] + [pltpu.VMEM((S + 2 * pad, LANES), F32)] * 2,
                 sem=("parallel",))


def a_combine(outs, lses, name):
    S, W = outs[0].shape
    tr = 512

    def body(o0, o1, o2, l0, l1, l2, oa_ref, lt_ref):
        a, b, c = l0[...], l1[...], l2[...]
        m = jnp.maximum(jnp.maximum(a, b), c)
        ea, eb, ec = jnp.exp(a - m), jnp.exp(b - m), jnp.exp(c - m)
        z = ea + eb + ec
        oa_ref[...] = ((ea * o0[...] + eb * o1[...] + ec * o2[...]) / z).astype(BF16)
        lt_ref[...] = m + jnp.log(z)

    spec = ((tr, W), lambda i: (i, 0))
    return _call(name, body, (S // tr,), [(a,) + spec for a in (*outs, *lses)],
                 [((S, W), BF16) + spec, ((S, W), F32) + spec], sem=("parallel",))


def a_bwd(proj_g, bias_g, do_a, o_a, lse_tot, g, name):
    S = proj_g.shape[0]
    d = DILATIONS[g]
    L = S // d
    nqb = L // A_TQ
    pad = HALF_WINDOW * d

    def body(q_ref, k_ref, v_ref, b_ref, do_ref, o_ref, l_ref, dqkv_ref, db_ref,
             qf, of, dqf, kpad, vpad, dkacc, dvacc):
        qf[...] = q_ref[...].astype(F32) * A_SCALE
        of[...] = o_ref[...].astype(F32)
        _a_fill_padded(kpad, k_ref, S, pad)
        _a_fill_padded(vpad, v_ref, S, pad)
        dkacc[...] = jnp.zeros(dkacc.shape, F32)
        dvacc[...] = jnp.zeros(dvacc.shape, F32)
        db_ref[...] = jnp.zeros(db_ref.shape, F32)
        lane = lax.broadcasted_iota(jnp.int32, (A_TQ, LANES), 1)

        def block(t, carry):
            qb, r = t // d, t % d
            start = qb * (A_TQ * d) + r
            rows = pl.ds(start, A_TQ, stride=d)
            win = pl.ds(start, A_WIN, stride=d)
            kw = kpad[win, :].astype(BF16)
            vw = vpad[win, :].astype(BF16)
            q = qf[rows, :].astype(BF16)
            do = do_ref[rows, :]
            ov = of[rows, :]
            lt = l_ref[rows, :]
            q2 = _a_stack_heads(q, lane)
            do2 = _a_stack_heads(do, lane)
            lt2 = jnp.concatenate([lt[:, 0:1], lt[:, HEAD_DIM_A:HEAD_DIM_A + 1]], axis=0)
            s = _dot(q2, kw, 1, 1) + b_ref[_a_bias_variant(qb, nqb)]
            p = jnp.exp(s - lt2)
            t = jnp.sum(do2 * jnp.concatenate([ov, ov], axis=0), axis=-1, keepdims=True)
            dob2 = do2.astype(BF16)
            ds = p * (_dot(dob2, vw, 1, 1) - t)
            db_ref[...] += ds
            dsb = ds.astype(BF16)
            dq2 = _dot(dsb, kw)
            dqf[rows, :] = jnp.where(lane < HEAD_DIM_A, dq2[0:A_TQ], dq2[A_TQ:]) * A_SCALE
            dkacc[win, :] += _dot(dsb, q2, 0, 0)
            dvacc[win, :] += _dot(p.astype(BF16), dob2, 0, 0)
            return carry

        lax.fori_loop(0, nqb * d, block, 0, unroll=A_UNROLL)
        dqkv_ref[0] = dqf[...].astype(BF16)
        dqkv_ref[1] = dkacc[pad:pad + S, :].astype(BF16)
        dqkv_ref[2] = dvacc[pad:pad + S, :].astype(BF16)

    slab = ((S, LANES), lambda hp: (0, hp))
    padded = pltpu.VMEM((S + 2 * pad, LANES), F32)
    return _call(
        name, body, (4,),
        [(proj_g, (S, LANES), lambda hp: (0, hp)),
         (proj_g, (S, LANES), lambda hp: (0, 4 + hp)),
         (proj_g, (S, LANES), lambda hp: (0, 8 + hp)),
         (bias_g, (None, 3, 2 * A_TQ, A_WIN), lambda hp: (hp, 0, 0, 0)),
         (do_a,) + slab, (o_a,) + slab, (lse_tot,) + slab],
        [((3, S, GROUP_WIDTH_A), BF16, (3, S, LANES), lambda hp: (0, 0, hp)),
         ((4, 2 * A_TQ, A_WIN), F32, (None, 2 * A_TQ, A_WIN), lambda hp: (hp, 0, 0))],
        scratch=[pltpu.VMEM((S, LANES), F32)] * 3 + [padded] * 4,
        sem=("parallel",))


def _rope_tables(S):
    rows = S // GRID_W
    row = jnp.repeat(jnp.arange(rows, dtype=F32), GRID_W)
    col = jnp.tile(jnp.arange(GRID_W, dtype=F32), rows)
    n_freq = HEAD_DIM_B // 4
    freq = ROPE_THETA ** (-jnp.arange(n_freq, dtype=F32) / n_freq)
    ang = jnp.concatenate([row[:, None] * freq, col[:, None] * freq], axis=-1)
    cos, sin = jnp.cos(ang), jnp.sin(ang)
    return jnp.repeat(cos, 2, axis=-1), jnp.stack([-sin, sin], axis=-1).reshape(S, HEAD_DIM_B)


def _swap_pairs(y):
    lane = lax.broadcasted_iota(jnp.int32, y.shape, 1)
    return jnp.where(lane % 2 == 0, pltpu.roll(y, LANES - 1, 1), pltpu.roll(y, 1, 1))


def qkv_prep(proj_b, gains, cos_t, sin_t, name):
    S = proj_b.shape[0]
    ts = 256
    n_rot = N_HEADS_B + N_KV_B
    nh = n_rot + N_KV_B
    W = nh * LANES

    def body(x_ref, g_ref, c_ref, s_ref, o_ref):
        cv, sv = c_ref[...], s_ref[...]
        for hb in range(nh):
            cols = slice(hb * LANES, (hb + 1) * LANES)
            xv = x_ref[:, cols]
            if hb < n_rot:
                r = lax.rsqrt(jnp.mean(xv * xv, axis=-1, keepdims=True) + EPS)
                yv = xv * r * g_ref[:, cols]
                o_ref[:, cols] = (yv * cv + _swap_pairs(yv) * sv).astype(BF16)
            else:
                o_ref[:, cols] = xv.astype(BF16)

    return _call(name, body, (S // ts,),
                 [(proj_b, (ts, W), lambda i: (i, 0)), (gains, (1, W), lambda i: (0, 0)),
                  (cos_t, (ts, LANES), lambda i: (i, 0)), (sin_t, (ts, LANES), lambda i: (i, 0))],
                 [((S, W), BF16, (ts, W), lambda i: (i, 0))],
                 sem=("parallel",))[0]


def qk_prep_bwd(dr, proj_b, col0, gain, cos_t, sin_t, name):
    S, W = dr.shape
    H = W // LANES
    ts = 256
    xb = (col0 * LANES) // W

    def body(d_ref, x_ref, g_ref, c_ref, s_ref, dx_ref, dg_ref):
        i = pl.program_id(0)
        cv, sv, gv = c_ref[...], s_ref[...], g_ref[...]
        dgp = jnp.zeros((1, LANES), F32)
        for hb in range(H):
            cols = slice(hb * LANES, (hb + 1) * LANES)
            dout = d_ref[:, cols]
            dy = dout * cv + _swap_pairs(dout * sv)
            dx, dgt = _rms_bwd_tile(dy, x_ref[:, cols], gv)
            dx_ref[:, cols] = dx.astype(BF16)
            dgp = dgp + jnp.sum(dgt, axis=0, keepdims=True)

        @pl.when(i == 0)
        def _():
            dg_ref[...] = dgp

        @pl.when(i > 0)
        def _():
            dg_ref[...] += dgp

    return _call(name, body, (S // ts,),
                 [(dr, (ts, W), lambda i: (i, 0)), (proj_b, (ts, W), lambda i: (i, xb)),
                  (gain, (1, LANES), lambda i: (0, 0)),
                  (cos_t, (ts, LANES), lambda i: (i, 0)), (sin_t, (ts, LANES), lambda i: (i, 0))],
                 [((S, W), BF16, (ts, W), lambda i: (i, 0)),
                  ((1, LANES), F32, (1, LANES), lambda i: (0, 0))],
                 sem=("arbitrary",))


def _row_sums(x):
    hi = x.astype(BF16)
    lo = (x - hi.astype(F32)).astype(BF16)
    ones = jnp.ones((8, LANES), BF16)
    return (_dot(ones, hi, 1, 1) + _dot(ones, lo, 1, 1))[0:1, :]


def flash_fwd(qkv, name):
    S = qkv.shape[0]
    tq = B_TQ_FWD
    scale = HEAD_DIM_B ** -0.5

    hps = B_HEADS_PER_STEP

    def body(q_ref, k_ref, v_ref, o_ref, l_ref):
        k, v = k_ref[...], v_ref[...]
        for j in range(hps):
            cols = slice(j * LANES, (j + 1) * LANES)
            s = _dot(q_ref[:, cols], k, 1, 1)
            m = jnp.max(s, axis=-1, keepdims=True)
            e = jnp.exp2((s - m) * (scale * LOG2E))
            l = jnp.sum(e, axis=-1, keepdims=True)
            o_ref[:, cols] = (_dot(e.astype(BF16), v) / l).astype(BF16)
            lse = jnp.broadcast_to(m * scale + jnp.log(l), (tq, LANES))
            l_ref[j] = _row_sums(lse) * (1.0 / LANES)

    per = GQA_GROUP_B // hps
    heads = lambda g, h, i: (i, g * per + h)
    return _call(name, body, (N_KV_B, per, S // tq),
                 [(qkv, (tq, hps * LANES), heads),
                  (qkv, (S, LANES), lambda g, h, i: (0, N_HEADS_B + g)),
                  (qkv, (S, LANES), lambda g, h, i: (0, N_HEADS_B + N_KV_B + g))],
                 [((S, N_HEADS_B * LANES), BF16, (tq, hps * LANES), heads),
                  ((N_HEADS_B, 1, S), F32, (hps, 1, tq), lambda g, h, i: (g * per + h, 0, i))],
                 sem=("parallel", "parallel", "parallel"))


def flash_bwd(qkv, k_t, do_b, o_b, lse, name):
    S = qkv.shape[0]
    tq = B_TQ_BWD
    nq = S // tq
    scale = HEAD_DIM_B ** -0.5

    def body(q_ref, k_ref, v_ref, kt_ref, do_ref, o_ref, l_ref, dq_ref, dk_ref, dv_ref, dkacc, dvacc):
        h, i = pl.program_id(1), pl.program_id(2)

        @pl.when((h == 0) & (i == 0))
        def _():
            dkacc[...] = jnp.zeros(dkacc.shape, F32)
            dvacc[...] = jnp.zeros(dvacc.shape, F32)

        q = q_ref[...]
        do = do_ref[...]
        dob = do.astype(BF16)
        t = _row_sums(do * o_ref[...].astype(F32))
        pt = jnp.exp2(_dot(k_ref[...], q, 1, 1) * (scale * LOG2E) - l_ref[...] * LOG2E)
        dsb = (pt * (_dot(v_ref[...], dob, 1, 1) - t)).astype(BF16)
        dvacc[...] += _dot(pt.astype(BF16), dob)
        dkacc[...] += _dot(dsb, q)
        dq_ref[...] = _dot(kt_ref[...], dsb).T * scale

        @pl.when((h == GQA_GROUP_B - 1) & (i == nq - 1))
        def _():
            dk_ref[...] = dkacc[...] * scale
            dv_ref[...] = dvacc[...].astype(BF16)

    head = lambda g, h, i: (i, g * GQA_GROUP_B + h)
    return _call(name, body, (N_KV_B, GQA_GROUP_B, nq),
                 [(qkv, (tq, LANES), head),
                  (qkv, (S, LANES), lambda g, h, i: (0, N_HEADS_B + g)),
                  (qkv, (S, LANES), lambda g, h, i: (0, N_HEADS_B + N_KV_B + g)),
                  (k_t, (LANES, S), lambda g, h, i: (g, 0)),
                  (do_b, (tq, LANES), head), (o_b, (tq, LANES), head),
                  (lse, (None, 1, tq), lambda g, h, i: (g * GQA_GROUP_B + h, 0, i))],
                 [((S, N_HEADS_B * LANES), F32, (tq, LANES), head),
                  ((S, N_KV_B * LANES), F32, (S, LANES), lambda g, h, i: (0, g)),
                  ((S, N_KV_B * LANES), BF16, (S, LANES), lambda g, h, i: (0, g))],
                 scratch=[pltpu.VMEM((S, LANES), F32)] * 2,
                 sem=("parallel", "arbitrary", "arbitrary"))


MERGE_TN = 512


def _mix_rows_spec(Gm, row0, n_slots, slot_map, cols=None, col_map=None):
    C = Gm.shape[2] if cols is None else cols
    cm = (lambda *idx: 0) if col_map is None else col_map
    return (Gm, (n_slots, LANES, C), lambda *idx: (slot_map(*idx), row0 // LANES, cm(*idx)))


def _gate_specs(proj_b, tm):
    first = PB_GATE_A // MERGE_TN
    return [(proj_b, (tm, MERGE_TN), lambda i, k=k: (i, first + k)) for k in range(4)]


def _whole_rows_spec(Gm, row0):
    return _mix_rows_spec(Gm, row0, N_DEV, lambda *idx: 0)


def merge_fwd(o_a, o_b, w_a, Gm, proj_b, b_gate, x, name):
    S, D = x.shape
    tm = 256

    def body(oa_ref, ob_ref, wa_ref, wb_ref, wo_ref, g0, g1, g2, g3, bg_ref, x_ref, m_ref, ya_ref, yb_ref, xo_ref):
        ya = _dot(oa_ref[...], wa_ref[...])
        yb = _dot(ob_ref[...], wb_ref[...].reshape(N_DEV * LANES, D))
        ga = _sigmoid(jnp.concatenate([g0[...], g1[...]], axis=1) + bg_ref[:, 0:D])
        gb = _sigmoid(jnp.concatenate([g2[...], g3[...]], axis=1) + bg_ref[:, D:2 * D])
        merged = (ga * ya + gb * yb).astype(BF16)
        m_ref[...] = merged
        ya_ref[...] = ya.astype(BF16)
        yb_ref[...] = yb.astype(BF16)
        xo_ref[...] = x_ref[...] + _dot(merged, wo_ref[...].reshape(N_DEV * LANES, D))

    rows = lambda a: (a, (tm, a.shape[1]), lambda i: (i, 0))
    out = ((S, D), BF16, (tm, D), lambda i: (i, 0))
    return _call(name, body, (S // tm,),
                 [rows(o_a), rows(o_b), (w_a, w_a.shape, lambda i: (0, 0)),
                  _whole_rows_spec(Gm, REST_WB), _whole_rows_spec(Gm, REST_WOUT)]
                 + _gate_specs(proj_b, tm) + [(b_gate, (1, 2 * D), lambda i: (0, 0)), rows(x)],
                 [out, out, out, ((S, D), F32, (tm, D), lambda i: (i, 0))], sem=("parallel",))


def merge_bwd(dx2, w_a, Gm, ya, yb, proj_b, b_gate, name):
    S, D = dx2.shape
    tm = 256

    def body(d_ref, wo_ref, wa_ref, wb_ref, ya_ref, yb_ref, g0, g1, g2, g3, bg_ref,
             dya_ref, dyb_ref, dg_ref, dbg_ref, doa_ref, dob_ref):
        i = pl.program_id(0)
        dm = _dot(d_ref[...].astype(BF16), wo_ref[...].reshape(N_DEV * LANES, D), 1, 1)
        ga = _sigmoid(jnp.concatenate([g0[...], g1[...]], axis=1) + bg_ref[:, 0:D])
        gb = _sigmoid(jnp.concatenate([g2[...], g3[...]], axis=1) + bg_ref[:, D:2 * D])
        dya = (dm * ga).astype(BF16)
        dyb = (dm * gb).astype(BF16)
        dya_ref[...] = dya
        dyb_ref[...] = dyb
        dpa = dm * ya_ref[...].astype(F32) * ga * (1.0 - ga)
        dpb = dm * yb_ref[...].astype(F32) * gb * (1.0 - gb)
        dg_ref[0] = dpa.astype(BF16)
        dg_ref[1] = dpb.astype(BF16)
        doa_ref[...] = _dot(dya, wa_ref[...], 1, 1)
        dob_ref[...] = _dot(dyb, wb_ref[...].reshape(N_DEV * LANES, D), 1, 1)
        sa = jnp.sum(dpa, axis=0, keepdims=True)
        sb = jnp.sum(dpb, axis=0, keepdims=True)

        @pl.when(i == 0)
        def _():
            dbg_ref[0] = sa
            dbg_ref[1] = sb

        @pl.when(i > 0)
        def _():
            dbg_ref[0] += sa
            dbg_ref[1] += sb

    tile = ((tm, D), lambda i: (i, 0))
    return _call(
        name, body, (S // tm,),
        [(dx2,) + tile, _whole_rows_spec(Gm, REST_WOUT), (w_a, w_a.shape, lambda i: (0, 0)),
         _whole_rows_spec(Gm, REST_WB), (ya,) + tile, (yb,) + tile]
        + _gate_specs(proj_b, tm) + [(b_gate, (1, 2 * D), lambda i: (0, 0))],
        [((S, D), BF16) + tile, ((S, D), BF16) + tile,
         ((2, S, D), BF16, (2, tm, D), lambda i: (0, i, 0)),
         ((2, 1, D), F32, (2, 1, D), lambda i: (0, 0, 0)),
         ((S, w_a.shape[0]), F32, (tm, w_a.shape[0]), lambda i: (i, 0)),
         ((S, N_HEADS_B * LANES), F32, (tm, N_HEADS_B * LANES), lambda i: (i, 0))],
        sem=("arbitrary",))


def weight_grad_rows(a, b, grads, row0, name):
    S, M = a.shape
    N = b.shape[1]
    tmm = 512
    tk = WGRAD_TK
    nk = S // tk

    def body(g_ref, a_ref, b_ref, o_ref, acc_ref):
        k = pl.program_id(1)
        p = _dot(a_ref[...], b_ref[...].astype(BF16), 0, 0)

        @pl.when(k == 0)
        def _():
            acc_ref[...] = p

        @pl.when(k > 0)
        def _():
            acc_ref[...] += p

        @pl.when(k == nk - 1)
        def _():
            o_ref[...] = acc_ref[...].astype(BF16).reshape(tmm // LANES, LANES, N)

    return pl.pallas_call(
        body,
        out_shape=jax.ShapeDtypeStruct(grads.shape, BF16),
        grid=(M // tmm, nk),
        in_specs=[pl.BlockSpec(memory_space=pl.ANY),
                  pl.BlockSpec((tk, tmm), lambda j, k: (k, j)),
                  pl.BlockSpec((tk, N), lambda j, k: (k, 0))],
        out_specs=pl.BlockSpec((tmm // LANES, LANES, N), lambda j, k: (j, row0 // LANES, 0)),
        scratch_shapes=[pltpu.VMEM((tmm, N), F32)],
        input_output_aliases={0: 0},
        name=name,
        compiler_params=pltpu.CompilerParams(dimension_semantics=("parallel", "arbitrary"),
                                             vmem_limit_bytes=VMEM_LIMIT),
    )(grads, a, b)


def weight_grad_plain(a, b, name):
    S, M = a.shape
    N = b.shape[1]
    tk = WGRAD_TK
    nk = S // tk

    def body(a_ref, b_ref, o_ref, acc_ref):
        k = pl.program_id(0)
        p = _dot(a_ref[...], b_ref[...], 0, 0)

        @pl.when(k == 0)
        def _():
            acc_ref[...] = p

        @pl.when(k > 0)
        def _():
            acc_ref[...] += p

        @pl.when(k == nk - 1)
        def _():
            o_ref[...] = acc_ref[...].astype(BF16)

    return _call(name, body, (nk,),
                 [(a, (tk, M), lambda k: (k, 0)), (b, (tk, N), lambda k: (k, 0))],
                 [((M, N), BF16, (M, N), lambda k: (0, 0))],
                 scratch=[pltpu.VMEM((M, N), F32)], sem=("arbitrary",))[0]


def local_step(x, tgt, p, get_g1_up, get_g1_down, get_gm_in, get_gm_rest, get_g2, emit, start_token):
    S, D = x.shape
    after = lambda t: t[0:1, 0:1]
    buckets = _bucket_tables()
    cos_t, sin_t = _rope_tables(S)
    gains = jnp.concatenate([jnp.tile(p["q_norm"], (1, N_HEADS_B)), jnp.tile(p["k_norm"], (1, N_KV_B)),
                             jnp.ones((1, N_KV_B * LANES), F32)], axis=1)

    n1 = rms_fwd(x, p["ffn1_norm"] + after(start_token), "ffn1_norm")
    bias = bias_build(p["rel_bias"] + after(start_token), buckets)
    g1_up = get_g1_up((n1, bias))
    ab1 = ffn_up(n1, (g1_up, None), "ffn1_up")
    G1 = (g1_up, get_g1_down(ab1))
    x1 = ffn_down(ab1, G1, x, "ffn1_down")

    hm = rms_fwd(x1, p["mix_norm"], "mix_norm")
    Gw = get_gm_in(hm)
    n_a = A_QKV_WIDTH // PROJ_TN
    proj_a = [in_proj(hm, Gw, g, 3, BF16, "in_proj_a%d" % g, tile_stride=3) for g in range(3)]
    proj_b = in_proj(hm, Gw, n_a, PB_WIDTH // PROJ_TN, F32, "in_proj_b")

    outs, lses = [], []
    for g in range(3):
        o, l = a_fwd(proj_a[g], bias[g], g, "a_fwd_%d" % g)
        outs.append(o)
        lses.append(l)
    o_a, lse_tot = a_combine(outs, lses, "a_combine")

    qkv = qkv_prep(proj_b, gains, cos_t, sin_t, "qkv_prep")
    k_t = qkv[:, N_HEADS_B * LANES:(N_HEADS_B + N_KV_B) * LANES].T
    o_b, lse_b = flash_fwd(qkv, "flash_fwd")

    Gm = get_gm_rest(o_b)
    w_a = Gm[:, REST_WA:REST_ROWS, :].reshape(N_DEV, GROUP_WIDTH_A, LANES).transpose(1, 0, 2).reshape(GROUP_WIDTH_A, D)
    merged, ya, yb, x2 = merge_fwd(o_a, o_b, w_a, Gm, proj_b, p["b_gate"], x1, "merge_fwd")

    G2 = get_g2(x2)
    n2 = rms_fwd(x2, p["ffn2_norm"], "ffn2_norm")
    ab2 = ffn_up(n2, G2, "ffn2_up")
    x3 = ffn_down(ab2, G2, x2, "ffn2_down")

    loss, dx3, dx3_b, d_final = final_loss(x3, tgt, p["final_norm"], "final_loss")

    dabh2, gw2 = ffn_bwd_weights(dx3_b, ab2, n2, G2, "ffn2_bwd")
    t2 = emit("ffn2", gw2)
    dx2, dx2_b, d_ffn2_norm = ffn_bwd_input(dabh2, G2, x2, p["ffn2_norm"] + after(t2), dx3, "ffn2_bwd")

    dya, dyb, dgate, dbg, do_a, do_b = merge_bwd(dx2_b, w_a, Gm, ya, yb, proj_b, p["b_gate"], "merge_bwd")
    gm_grads = jnp.zeros((N_DEV, MIX_ROWS, D), BF16)
    gm_grads = weight_grad_rows(merged, dx2_b, gm_grads, MIX_WOUT, "dw_out")
    gm_grads = weight_grad_rows(o_b, dyb, gm_grads, MIX_WB, "dw_branch_b")
    dw_a = weight_grad_plain(o_a, dya, "dw_branch_a")

    dq_r, dk_r, dv_b = flash_bwd(qkv, k_t, do_b, o_b, lse_b, "flash_bwd")
    dq_b, d_q_norm = qk_prep_bwd(dq_r, proj_b, 0, p["q_norm"], cos_t, sin_t, "q_prep_bwd")
    dk_b, d_k_norm = qk_prep_bwd(dk_r, proj_b, N_HEADS_B, p["k_norm"], cos_t, sin_t, "k_prep_bwd")

    dqkv, dbs = [], []
    for g in range(3):
        dg_, db = a_bwd(proj_a[g], bias[g], do_a, o_a, lse_tot, g, "a_bwd_%d" % g)
        dqkv.append(dg_)
        dbs.append(db)
    d_rel_bias = bias_bwd(jnp.stack(dbs, axis=0).reshape(3, HEADS_PER_GROUP_A, A_TQ, A_WIN), buckets)

    dproj = _dproj_pieces(dqkv, dq_b, jnp.concatenate([dk_b, dv_b], axis=1), dgate)
    gm_grads = in_proj_bwd_dw(dproj, hm, gm_grads, "in_proj_bwd")
    dw_a_sh = dw_a.reshape(GROUP_WIDTH_A, N_DEV, LANES).transpose(1, 0, 2).reshape(N_DEV, MIX_ROWS - MIX_WA, D)
    gm_grads = lax.dynamic_update_slice(gm_grads, dw_a_sh, (0, MIX_WA, 0))
    tm = emit("mix", gm_grads)
    dx1, dx1_b, d_mix_norm = in_proj_bwd_dh(dproj, Gw, x1, p["mix_norm"] + after(tm), dx2, "in_proj_bwd")

    dabh1, gw1 = ffn_bwd_weights(dx1_b, ab1, n1, G1, "ffn1_bwd")
    t1 = emit("ffn1", gw1)
    dx0, d_ffn1_norm = ffn_bwd_input(dabh1, G1, x, p["ffn1_norm"] + after(t1), dx1, "ffn1_bwd", as_operand=False)

    small = dict(ffn1_norm=d_ffn1_norm, mix_norm=d_mix_norm, b_gate=dbg.reshape(1, 2 * D),
                 q_norm=d_q_norm, k_norm=d_k_norm, rel_bias=d_rel_bias, ffn2_norm=d_ffn2_norm,
                 final_norm=d_final)
    return loss, dx0, small


def _pack_small(t, loss_row):
    row6 = jnp.concatenate([t["q_norm"].reshape(1, -1), t["k_norm"].reshape(1, -1), t["rel_bias"].reshape(1, -1)], axis=1)
    return jnp.concatenate([t["ffn1_norm"].reshape(1, -1), t["mix_norm"].reshape(1, -1), t["b_gate"].reshape(2, -1),
                            t["ffn2_norm"].reshape(1, -1), t["final_norm"].reshape(1, -1), row6, loss_row], axis=0)


def _unpack_small(a, shapes):
    return dict(ffn1_norm=a[0:1].reshape(shapes["ffn1_norm"]), mix_norm=a[1:2].reshape(shapes["mix_norm"]),
                b_gate=a[2:4].reshape(shapes["b_gate"]), ffn2_norm=a[4:5].reshape(shapes["ffn2_norm"]),
                final_norm=a[5].reshape(shapes["final_norm"]), q_norm=a[6:7, 0:128].reshape(shapes["q_norm"]),
                k_norm=a[6:7, 128:256].reshape(shapes["k_norm"]), rel_bias=a[6, 256:1024].reshape(shapes["rel_bias"]))


SMALL = ("ffn1_norm", "mix_norm", "b_gate", "q_norm", "k_norm", "rel_bias", "ffn2_norm", "final_norm")
ORDER = ("ffn1_norm", "ffn1_w1", "ffn1_w3", "ffn1_w2", "mix_norm", "w_in", "b_gate", "q_norm", "k_norm", "rel_bias",
         "w_branch_a", "w_branch_b", "w_out", "ffn2_norm", "ffn2_w1", "ffn2_w3", "ffn2_w2", "final_norm")


def kernel(x, ffn1_norm, ffn1_w1, ffn1_w3, ffn1_w2, mix_norm, w_in, b_gate, q_norm, k_norm, rel_bias, w_branch_a, w_branch_b, w_out, ffn2_norm, ffn2_w1, ffn2_w3, ffn2_w2, final_norm, loss_target, m_ffn1_norm, m_ffn1_w1, m_ffn1_w3, m_ffn1_w2, m_mix_norm, m_w_in, m_b_gate, m_q_norm, m_k_norm, m_rel_bias, m_w_branch_a, m_w_branch_b, m_w_out, m_ffn2_norm, m_ffn2_w1, m_ffn2_w3, m_ffn2_w2, m_final_norm, v_ffn1_norm, v_ffn1_w1, v_ffn1_w3, v_ffn1_w2, v_mix_norm, v_w_in, v_b_gate, v_q_norm, v_k_norm, v_rel_bias, v_w_branch_a, v_w_branch_b, v_w_out, v_ffn2_norm, v_ffn2_w1, v_ffn2_w3, v_ffn2_w2, v_final_norm):
    args = dict(locals())
    w = {n: args[n] for n in ORDER}
    m = {n: args["m_" + n] for n in ORDER}
    v = {n: args["v_" + n] for n in ORDER}
    D = x.shape[2]

    blocks = (
        ("ffn1_up", jnp.concatenate([ffn1_w1[0].T, ffn1_w3[0].T], axis=0)),
        ("ffn1_down", ffn1_w2[0]),
        ("mix_in", w_in[0]),
        ("mix_rest", jnp.concatenate([w_branch_b[0], w_out[0], w_branch_a[0].reshape(REST_ROWS - REST_WA, D)], axis=0)),
        ("ffn2", jnp.concatenate([ffn2_w1[0].T, ffn2_w3[0].T, ffn2_w2[0]], axis=0)),
    )
    direct = ("mix_rest", "ffn2")
    started = all_gather_start_all([(b.astype(BF16), tag in direct) for tag, b in blocks], "all_gather_start")
    gathers = {tag: s for (tag, _), s in zip(blocks, started)}
    start_token = started[0][4]

    def gathered(tag):
        def get(after):
            if tag in direct:
                return all_gather_place_own(*_split_wait("all_gather_" + tag + "_wait", gathers[tag], N_DEV - 1, after),
                                            "all_gather_" + tag + "_own")
            return all_gather_finish(*_split_wait("all_gather_" + tag + "_wait", gathers[tag], 4, after),
                                     "all_gather_" + tag + "_finish")
        return get

    core = lax.axis_index("c").astype(jnp.int32).reshape(1)
    chip = (2 * lax.axis_index("x") + lax.axis_index("y")).astype(jnp.int32).reshape(1)
    device = 2 * chip + core
    exchanges = {}

    def emit(tag, gw):
        if tag == "ffn1":
            (theirs,) = reduce_scatter_pair([gw], "reduce_scatter_pair_" + tag)
            part = pair_add(gw, theirs, core, "pair_add_" + tag)
            exchanges[tag] = reduce_scatter_start(part, "reduce_scatter_" + tag + "_start")
        else:
            exchanges[tag] = reduce_scatter_start_direct(gw, "reduce_scatter_" + tag + "_start")
        return exchanges[tag][4]

    small_p = dict(ffn1_norm=ffn1_norm, mix_norm=mix_norm, b_gate=b_gate, q_norm=q_norm, k_norm=k_norm,
                   rel_bias=rel_bias, ffn2_norm=ffn2_norm, final_norm=final_norm.reshape(1, D))
    loss_p, grad_x, small_g = local_step(x[0], loss_target[0], small_p, gathered("ffn1_up"), gathered("ffn1_down"),
                                         gathered("mix_in"), gathered("mix_rest"), gathered("ffn2"), emit, start_token)

    def landed(tag, after):
        n_others, me = (3, chip) if tag == "ffn1" else (N_DEV - 1, device)
        return tuple(_split_wait("reduce_scatter_" + tag + "_wait", exchanges[tag], n_others, after)) + (me,)

    grads, delta, new_m, new_v = {}, {}, {}, {}

    def finish(n, part, land, me, off, blk, transposed=False):
        shp = w[n].shape
        if transposed:
            to2 = lambda a: a.reshape(shp[-2], shp[-1]).T
            back = lambda a: a.T.reshape(shp)
        else:
            to2 = lambda a: a.reshape(shp[-2], shp[-1])
            back = lambda a: a.reshape(shp)
        res = sum_adamw(part, land, me, off, blk, to2(w[n]), to2(m[n]), to2(v[n]), "update_" + n)
        grads[n], delta[n], new_m[n], new_v[n] = [back(a) for a in res]

    last_token = exchanges["ffn1"][4]
    for tag, after in (("ffn2", last_token), ("ffn1", grad_x)):
        group = landed(tag, after)
        finish(tag + "_w1", *group, 0, FFN_SHARD, transposed=True)
        finish(tag + "_w3", *group, FFN_SHARD, FFN_SHARD, transposed=True)
        finish(tag + "_w2", *group, 2 * FFN_SHARD, FFN_SHARD)
        if tag == "ffn2":
            group_m = landed("mix", last_token)
            finish("w_in", *group_m, MIX_WIN, LANES)
            finish("w_branch_b", *group_m, MIX_WB, LANES)
            finish("w_out", *group_m, MIX_WOUT, LANES)
            grads["w_branch_a"] = sum_landed(*group_m, MIX_WA, MIX_ROWS - MIX_WA, MIX_ROWS - MIX_WA,
                                             "w_branch_a_sum").reshape(w_branch_a.shape)
    loss_row = jnp.pad(loss_p, ((0, 0), (0, D - LANES)))
    smalls = small_all_gather(_pack_small(small_g, loss_row))
    small_sum = sum_slots(smalls, 0, N_DEV, N_DEV, "small_sum")
    small_shapes = {n: w[n].shape for n in SMALL}
    grads.update(_unpack_small(small_sum, small_shapes))
    loss = small_sum[7, 0]

    n = "w_branch_a"
    two_d = lambda a: a.reshape(w[n].shape[-2], w[n].shape[-1])
    d_, m_, v_ = adamw(two_d(w[n]), two_d(grads[n]), two_d(m[n]), two_d(v[n]), "adamw_" + n)
    delta[n], new_m[n], new_v[n] = [a.reshape(w[n].shape) for a in (d_, m_, v_)]
    zero_row = jnp.zeros((1, D), F32)
    pack = lambda t: _pack_small({n: t[n] for n in SMALL}, zero_row)
    d_, m_, v_ = adamw(pack(w), small_sum, pack(m), pack(v), "adamw_small")
    for src, dst in ((d_, delta), (m_, new_m), (v_, new_v)):
        dst.update(_unpack_small(src, small_shapes))

    return (loss, grad_x[None], *[grads[n] for n in ORDER], *[delta[n] for n in ORDER],
            *[new_m[n] for n in ORDER], *[new_v[n] for n in ORDER])
```

```python
import math

import jax
import jax.numpy as jnp
from jax import lax
from jax.experimental import pallas as pl
from jax.experimental.pallas import tpu as pltpu

F32 = jnp.float32
BF16 = jnp.bfloat16
MESH = pl.DeviceIdType.MESH

V7X_VMEM_BYTES = 64 * 1024 * 1024
VMEM_LIMIT = V7X_VMEM_BYTES - 8 * 1024 * 1024
LANES = 128

N_DEV = 8
EPS = 1e-6
NEG_INF = -1e30

DILATIONS = (1, 4, 16)
HALF_WINDOW = 64
HEAD_DIM_A = 64
HEADS_PER_GROUP_A = 8
GROUP_WIDTH_A = 512
A_QKV_WIDTH = 4608
A_GROUP_QKV = A_QKV_WIDTH // 3
A_TQ = 128
A_WIN = A_TQ + 2 * HALF_WINDOW
A_UNROLL = 8
A_SCALE = HEAD_DIM_A ** -0.5
WGRAD_TK = 2048
HEAD_DIM_B = 128
N_HEADS_B = 8
N_KV_B = 2
GQA_GROUP_B = 4
GRID_W = 64
ROPE_THETA = 10000.0
B_TQ_FWD = 256
B_TQ_BWD = 512
B_HEADS_PER_STEP = 4
LOG2E = 1.4426950408889634
N_BUCKETS = 32
MAX_DISTANCE = 1024
PB_WIDTH = 3584
PB_GATE_A = 1536
PB_GATE_B = 2560

ADAM_LR = 0.001
ADAM_B1 = 0.9
ADAM_B2 = 0.999
ADAM_EPS = 1e-08
ADAM_WD = 0.01
ADAM_STEP = 10

FFN_SHARD = 352
MIX_WIN, MIX_WB, MIX_WOUT, MIX_WA = 0, 1024, 1152, 1280
MIX_ROWS = 1344
REST_WB, REST_WOUT, REST_WA, REST_ROWS = 0, 128, 256, 320


def _dot(a, b, ca=1, cb=0):
    return lax.dot_general(a, b, (((ca,), (cb,)), ((), ())), preferred_element_type=F32)


def _call(name, body, grid, ins, outs, scratch=(), sem=None, aliases=None):
    ins = [tuple(i) + (None,) * (4 - len(i)) for i in ins]
    res = pl.pallas_call(
        body,
        out_shape=[jax.ShapeDtypeStruct(s, d) for (s, d, _, _) in outs],
        grid=grid,
        in_specs=[pl.BlockSpec(bs, im, pipeline_mode=pm) for (_, bs, im, pm) in ins],
        out_specs=[pl.BlockSpec(bs, im) for (_, _, bs, im) in outs],
        scratch_shapes=list(scratch),
        name=name,
        input_output_aliases=aliases or {},
        compiler_params=pltpu.CompilerParams(dimension_semantics=sem, vmem_limit_bytes=VMEM_LIMIT),
    )(*[i[0] for i in ins])
    return res


def _sigmoid(x):
    return 0.5 * jnp.tanh(0.5 * x) + 0.5


def _position():
    return lax.axis_index("x"), lax.axis_index("y"), lax.axis_index("c")


def _hbm_specs(n):
    return [pl.BlockSpec(memory_space=pl.ANY) for _ in range(n)]


PAIR_BUFFERS = 4


def reduce_scatter_pair(grads, name):
    n = len(grads)
    C = grads[0].shape[2]
    half = [g.shape[1] // 2 for g in grads]
    chunks = [(i, q, hf) for i in range(n) for q in range(4) for hf in range(2)]
    nb = PAIR_BUFFERS

    def body(*refs):
        ins, theirs = refs[:n], refs[n:2 * n]
        buf, load_sems, send_sems, recv_sems = refs[2 * n:]
        x, y, c = _position()
        sibling = (x, y, 1 - c)

        def load(k):
            i, q, hf = chunks[k]
            r = half[i]
            return pltpu.make_async_copy(ins[i].at[2 * q + (1 - c), pl.ds(hf * r, r), :],
                                         buf.at[k % nb, pl.ds(0, r), :], load_sems.at[k % nb])

        def send(k):
            i, q, hf = chunks[k]
            r = half[i]
            return pltpu.make_async_remote_copy(
                src_ref=buf.at[k % nb, pl.ds(0, r), :], dst_ref=theirs[i].at[q, pl.ds(hf * r, r), :],
                send_sem=send_sems.at[k % nb], recv_sem=recv_sems.at[i],
                device_id=sibling, device_id_type=MESH)

        for k in range(len(chunks) + 1):
            if k < len(chunks):
                if k >= nb:
                    send(k - nb).wait_send()
                load(k).start()
            if k >= 1:
                load(k - 1).wait()
                send(k - 1).start()
        for k in range(max(0, len(chunks) - nb), len(chunks)):
            send(k).wait_send()
        for i in range(n):
            pltpu.make_async_remote_copy(
                src_ref=theirs[i], dst_ref=theirs[i], send_sem=send_sems.at[0], recv_sem=recv_sems.at[i],
                device_id=sibling, device_id_type=MESH).wait_recv()

    return pl.pallas_call(
        body,
        out_shape=[jax.ShapeDtypeStruct((4,) + g.shape[1:], g.dtype) for g in grads],
        in_specs=_hbm_specs(n),
        out_specs=_hbm_specs(n),
        scratch_shapes=[pltpu.VMEM((nb, max(half), C), grads[0].dtype), pltpu.SemaphoreType.DMA((nb,)),
                        pltpu.SemaphoreType.DMA((nb,)), pltpu.SemaphoreType.DMA((n,))],
        name=name,
        compiler_params=pltpu.CompilerParams(vmem_limit_bytes=VMEM_LIMIT),
    )(*grads)


_HBM_SPEC = pl.BlockSpec(memory_space=pltpu.HBM)
_SEM_SPEC = pl.BlockSpec(memory_space=pltpu.SEMAPHORE)
_TOKEN_SPEC = pl.BlockSpec(memory_space=pltpu.VMEM)
_DATAFLOW = pltpu.SideEffectType.DATAFLOW_SIDE_EFFECTING


def _split_start_many(name, exchanges):
    n = len(exchanges)

    def full_body(*refs):
        srcs, lands = refs[:n], refs[n:2 * n]
        sems = refs[2 * n:4 * n]
        token = refs[-1]
        for i, (body, _, _) in enumerate(exchanges):
            body(srcs[i], lands[i], sems[2 * i], sems[2 * i + 1])
        token[...] = jnp.zeros_like(token)

    srcs = [pltpu.with_memory_space_constraint(src, pltpu.HBM) for _, src, _ in exchanges]
    lands = [pltpu.with_memory_space_constraint(lax.empty(shape, src.dtype), pltpu.HBM)
             for _, src, shape in exchanges]
    res = pl.pallas_call(
        full_body, name=name,
        out_shape=(pltpu.SemaphoreType.DMA(()),) * (2 * n)
        + tuple(pltpu.HBM(a.shape, a.dtype) for a in srcs + lands) + (jax.ShapeDtypeStruct((8, LANES), F32),),
        in_specs=(_HBM_SPEC,) * (2 * n),
        out_specs=(_SEM_SPEC,) * (2 * n) + (_HBM_SPEC,) * (2 * n) + (_TOKEN_SPEC,),
        input_output_aliases={i: 2 * n + i for i in range(2 * n)},
        compiler_params=pltpu.CompilerParams(has_side_effects=_DATAFLOW),
    )(*srcs, *lands)
    return [(res[2 * i], res[2 * i + 1], res[2 * n + i], res[3 * n + i], res[-1]) for i in range(n)]


def _split_start(name, body, src, land_shape):
    return _split_start_many(name, [(body, src, land_shape)])[0]


def _split_wait(name, started, n_blocks, after):
    send_sem, recv_sem, src_thru, land_thru, _ = started
    after = after if isinstance(after, tuple) else (after,)

    def body(src_ref, land_ref, send_sem, recv_sem, *rest):
        x, y, c = _position()
        blocks = land_ref.at[pl.ds(0, n_blocks)]
        copy = pltpu.make_async_remote_copy(src_ref=blocks, dst_ref=blocks, send_sem=send_sem, recv_sem=recv_sem,
                                            device_id=(x, y, c), device_id_type=MESH)
        copy.wait_send()
        copy.wait_recv()

    return pl.pallas_call(
        body, name=name,
        out_shape=(pltpu.HBM(src_thru.shape, src_thru.dtype), pltpu.HBM(land_thru.shape, land_thru.dtype)),
        in_specs=(_HBM_SPEC, _HBM_SPEC, _SEM_SPEC, _SEM_SPEC) + (pl.BlockSpec(memory_space=pl.ANY),) * len(after),
        out_specs=(_HBM_SPEC, _HBM_SPEC),
        input_output_aliases={0: 0, 1: 1},
        compiler_params=pltpu.CompilerParams(has_side_effects=_DATAFLOW),
    )(src_thru, land_thru, send_sem, recv_sem, *after)


def all_gather_start_all(blocks, name):
    def starter(direct):
        def body(b_ref, land_ref, send_sem, recv_sem):
            x, y, c = _position()
            peers = _other_devices(x, y, c) if direct else [(x, y, 1 - c), (1 - x, y, c), (x, 1 - y, c),
                                                            (1 - x, 1 - y, c)]
            for peer in peers:
                pltpu.make_async_remote_copy(src_ref=b_ref, dst_ref=land_ref.at[4 * x + 2 * y + c],
                                             send_sem=send_sem, recv_sem=recv_sem,
                                             device_id=peer, device_id_type=MESH).start()
        return body

    return _split_start_many(name, [(starter(direct), block, (N_DEV,) + block.shape) for block, direct in blocks])


def all_gather_finish(block, land, name):
    R, C = block.shape

    def body(b_ref, land_in, land_ref, stage, load_sems, send_sems, recv_sems, own_sem):
        x, y, c = _position()
        sibling = (x, y, 1 - c)
        chips = [(1 - x, y), (x, 1 - y), (1 - x, 1 - y)]
        own_in = pltpu.make_async_copy(b_ref, stage.at[3], load_sems.at[3])
        own_in.start()
        loads = [pltpu.make_async_copy(land_in.at[4 * px + 2 * py + c], stage.at[j], load_sems.at[j])
                 for j, (px, py) in enumerate(chips)]
        for ld in loads:
            ld.start()
        sends = []
        for j, (px, py) in enumerate(chips):
            loads[j].wait()
            dst = land_ref.at[4 * px + 2 * py + c]
            cp = pltpu.make_async_remote_copy(src_ref=stage.at[j], dst_ref=dst, send_sem=send_sems.at[j],
                                              recv_sem=recv_sems.at[j], device_id=sibling, device_id_type=MESH)
            cp.start()
            sends.append(cp)
        own_in.wait()
        own_out = pltpu.make_async_copy(stage.at[3], land_ref.at[4 * x + 2 * y + c], own_sem)
        own_out.start()
        for j, (px, py) in enumerate(chips):
            dst = land_ref.at[4 * px + 2 * py + (1 - c)]
            pltpu.make_async_remote_copy(src_ref=stage.at[j], dst_ref=dst, send_sem=send_sems.at[j],
                                         recv_sem=recv_sems.at[j], device_id=sibling,
                                         device_id_type=MESH).wait_recv()
        for cp in sends:
            cp.wait_send()
        own_out.wait()

    return pl.pallas_call(
        body,
        out_shape=jax.ShapeDtypeStruct(land.shape, land.dtype),
        in_specs=_hbm_specs(2),
        out_specs=pl.BlockSpec(memory_space=pl.ANY),
        scratch_shapes=[pltpu.VMEM((4, R, C), block.dtype), pltpu.SemaphoreType.DMA((4,)),
                        pltpu.SemaphoreType.DMA((3,)), pltpu.SemaphoreType.DMA((3,)), pltpu.SemaphoreType.DMA],
        input_output_aliases={1: 0},
        name=name,
        compiler_params=pltpu.CompilerParams(vmem_limit_bytes=VMEM_LIMIT),
    )(block, land)


def reduce_scatter_start(parts, name):
    def body(p_ref, land_ref, send_sem, recv_sem):
        x, y, c = _position()
        for px, py in [(1 - x, y), (x, 1 - y), (1 - x, 1 - y)]:
            pltpu.make_async_remote_copy(src_ref=p_ref.at[2 * px + py], dst_ref=land_ref.at[2 * x + y],
                                         send_sem=send_sem, recv_sem=recv_sem,
                                         device_id=(px, py, c), device_id_type=MESH).start()

    return _split_start(name, body, parts, parts.shape)


def _other_devices(x, y, c):
    return [(1 - x if k & 4 else x, 1 - y if k & 2 else y, 1 - c if k & 1 else c) for k in range(1, N_DEV)]


def all_gather_place_own(block, land, name):
    R, C = block.shape

    def body(b_ref, land_in, land_ref, stage, sems):
        x, y, c = _position()
        load = pltpu.make_async_copy(b_ref, stage, sems.at[0])
        load.start()
        load.wait()
        store = pltpu.make_async_copy(stage, land_ref.at[4 * x + 2 * y + c], sems.at[1])
        store.start()
        store.wait()

    return pl.pallas_call(
        body,
        out_shape=jax.ShapeDtypeStruct(land.shape, land.dtype),
        in_specs=_hbm_specs(2),
        out_specs=pl.BlockSpec(memory_space=pl.ANY),
        scratch_shapes=[pltpu.VMEM((R, C), block.dtype), pltpu.SemaphoreType.DMA((2,))],
        input_output_aliases={1: 0},
        name=name,
    )(block, land)


def reduce_scatter_start_direct(grads, name):
    def body(g_ref, land_ref, send_sem, recv_sem):
        x, y, c = _position()
        for px, py, pc in _other_devices(x, y, c):
            pltpu.make_async_remote_copy(src_ref=g_ref.at[4 * px + 2 * py + pc],
                                         dst_ref=land_ref.at[4 * x + 2 * y + c],
                                         send_sem=send_sem, recv_sem=recv_sem,
                                         device_id=(px, py, pc), device_id_type=MESH).start()

    return _split_start(name, body, grads, grads.shape)


def small_all_gather(small):
    def body(small_ref, smalls, s_send, s_recv, s_local):
        x, y, c = _position()
        me = 4 * x + 2 * y + c
        lc = pltpu.make_async_copy(small_ref, smalls.at[me], s_local)
        lc.start()
        remote = []
        k = 0
        for dx in (0, 1):
            for dy in (0, 1):
                for dc in (0, 1):
                    if dx + dy + dc == 0:
                        continue
                    peer = (1 - x if dx else x, 1 - y if dy else y, 1 - c if dc else c)
                    rc = pltpu.make_async_remote_copy(
                        src_ref=small_ref, dst_ref=smalls.at[me],
                        send_sem=s_send.at[k], recv_sem=s_recv.at[k],
                        device_id=peer, device_id_type=MESH)
                    rc.start()
                    remote.append(rc)
                    k += 1
        for rc in remote:
            rc.wait()
        lc.wait()

    return pl.pallas_call(
        body,
        out_shape=jax.ShapeDtypeStruct((N_DEV,) + small.shape, small.dtype),
        in_specs=_hbm_specs(1),
        out_specs=pl.BlockSpec(memory_space=pl.ANY),
        scratch_shapes=[pltpu.SemaphoreType.DMA((7,)), pltpu.SemaphoreType.DMA((7,)), pltpu.SemaphoreType.DMA],
        name="small_all_gather",
    )(small)


def pair_add(grads, theirs, core, name):
    _, R, C = theirs.shape
    tr = R // 2

    def body(c_ref, a_ref, b_ref, o_ref):
        o_ref[...] = (a_ref[...].astype(F32) + b_ref[...].astype(F32)).astype(BF16)

    return pl.pallas_call(
        body,
        out_shape=jax.ShapeDtypeStruct(theirs.shape, BF16),
        grid_spec=pltpu.PrefetchScalarGridSpec(
            num_scalar_prefetch=1, grid=(4, R // tr),
            in_specs=[pl.BlockSpec((None, tr, C), lambda q, i, c: (2 * q + c[0], i, 0)),
                      pl.BlockSpec((None, tr, C), lambda q, i, c: (q, i, 0))],
            out_specs=pl.BlockSpec((None, tr, C), lambda q, i, c: (q, i, 0))),
        name=name,
        compiler_params=pltpu.CompilerParams(dimension_semantics=("parallel", "parallel"),
                                             vmem_limit_bytes=VMEM_LIMIT),
    )(core, grads, theirs)


def sum_slots(recv, off, rows, blk, name):
    nq, _, C = recv.shape
    ob = off // blk

    def body(r_ref, o_ref):
        acc = r_ref[0].astype(F32)
        for q in range(1, nq):
            acc = acc + r_ref[q].astype(F32)
        o_ref[...] = acc

    return _call(name, body, (rows // blk,),
                 [(recv, (nq, blk, C), lambda i: (0, ob + i, 0))],
                 [((rows, C), F32, (blk, C), lambda i: (i, 0))], sem=("parallel",))[0]


def _sum_terms(refs):
    acc = refs[0][...].astype(F32)
    for r in refs[1:]:
        acc = acc + r[...].astype(F32)
    return acc


def sum_landed(own, land, me, off, rows, blk, name):
    n, _, C = land.shape
    ob = off // blk

    def body(c_ref, *refs):
        refs[n][...] = _sum_terms(refs[:n])

    def entry(flip):
        return pl.BlockSpec((None, blk, C), lambda i, c: (c[0] ^ flip, ob + i, 0))

    return pl.pallas_call(
        body,
        out_shape=jax.ShapeDtypeStruct((rows, C), F32),
        grid_spec=pltpu.PrefetchScalarGridSpec(
            num_scalar_prefetch=1, grid=(rows // blk,),
            in_specs=[entry(k) for k in range(n)],
            out_specs=pl.BlockSpec((blk, C), lambda i, c: (i, 0))),
        name=name,
        compiler_params=pltpu.CompilerParams(dimension_semantics=("parallel",), vmem_limit_bytes=VMEM_LIMIT),
    )(me, own, *([land] * (n - 1)))


def _adamw_update(wv, gv, mv, vv):
    nm = ADAM_B1 * mv + (1.0 - ADAM_B1) * gv
    nv = ADAM_B2 * vv + (1.0 - ADAM_B2) * (gv * gv)
    c1 = 1.0 / (1.0 - ADAM_B1 ** ADAM_STEP)
    c2 = 1.0 / (1.0 - ADAM_B2 ** ADAM_STEP)
    return -ADAM_LR * ((nm * c1) / (jnp.sqrt(nv * c2) + ADAM_EPS) + ADAM_WD * wv), nm, nv


def sum_adamw(own, land, me, off, blk, w, m, v, name):
    rows, C = w.shape
    n = land.shape[0]
    ob = off // blk

    def body(c_ref, *refs):
        w_ref, m_ref, v_ref, g_out, d_out, m_out, v_out = refs[n:]
        gv = _sum_terms(refs[:n])
        g_out[...] = gv
        d_out[...], m_out[...], v_out[...] = _adamw_update(w_ref[...], gv, m_ref[...], v_ref[...])

    def entry(flip):
        return pl.BlockSpec((None, blk, C), lambda i, c: (c[0] ^ flip, ob + i, 0))

    plain = pl.BlockSpec((blk, C), lambda i, c: (i, 0))
    return pl.pallas_call(
        body,
        out_shape=[jax.ShapeDtypeStruct((rows, C), F32)] * 4,
        grid_spec=pltpu.PrefetchScalarGridSpec(
            num_scalar_prefetch=1, grid=(rows // blk,),
            in_specs=[entry(k) for k in range(n)] + [plain, plain, plain],
            out_specs=[plain] * 4),
        name=name,
        compiler_params=pltpu.CompilerParams(dimension_semantics=("parallel",), vmem_limit_bytes=VMEM_LIMIT),
    )(me, own, *([land] * (n - 1)), w, m, v)


def adamw(w, g, m, v, name):
    R, C = w.shape
    tr = R
    for cand in (256, 128, 64, 32, 16, 8):
        if R % cand == 0 and R > cand:
            tr = cand
            break

    def body(w_ref, g_ref, m_ref, v_ref, d_ref, nm_ref, nv_ref):
        d_ref[...], nm_ref[...], nv_ref[...] = _adamw_update(w_ref[...], g_ref[...], m_ref[...], v_ref[...])

    spec = ((tr, C), lambda i: (i, 0))
    out = ((R, C), F32) + spec
    return _call(name, body, (R // tr,), [(w,) + spec, (g,) + spec, (m,) + spec, (v,) + spec],
                 [out, out, out], sem=("parallel",))


def rms_fwd(x, g, name):
    S, D = x.shape
    tr = 512

    def body(x_ref, g_ref, o_ref):
        xv = x_ref[...]
        r = lax.rsqrt(jnp.mean(xv * xv, axis=-1, keepdims=True) + EPS)
        o_ref[...] = (xv * r * g_ref[...]).astype(BF16)

    return _call(name, body, (S // tr,),
                 [(x, (tr, D), lambda i: (i, 0)), (g, (1, D), lambda i: (0, 0))],
                 [((S, D), BF16, (tr, D), lambda i: (i, 0))], sem=("parallel",))[0]


def _rms_bwd_tile(dn, xv, gv):
    r = lax.rsqrt(jnp.mean(xv * xv, axis=-1, keepdims=True) + EPS)
    xh = xv * r
    dxh = dn * gv
    dx = r * (dxh - xh * jnp.mean(dxh * xh, axis=-1, keepdims=True))
    return dx, dn * xh


def final_loss(x, tgt, g, name):
    S, D = x.shape
    tr = 256

    def body(x_ref, t_ref, g_ref, l_ref, dx_ref, dxb_ref, dg_ref):
        i = pl.program_id(0)
        xv, gv = x_ref[...], g_ref[...]
        r = lax.rsqrt(jnp.mean(xv * xv, axis=-1, keepdims=True) + EPS)
        xh = xv * r
        e = xh * gv - t_ref[...]
        part = 0.5 * jnp.sum(jnp.sum(e * e, axis=-1, keepdims=True) * (1.0 / D), axis=0, keepdims=True)
        dy = e * (1.0 / D)
        dxh = dy * gv
        dx = r * (dxh - xh * jnp.mean(dxh * xh, axis=-1, keepdims=True))
        dx_ref[...] = dx
        dxb_ref[...] = dx.astype(BF16)
        dgp = jnp.sum(dy * xh, axis=0, keepdims=True)

        @pl.when(i == 0)
        def _():
            l_ref[...] = jnp.broadcast_to(part, l_ref.shape)
            dg_ref[...] = dgp

        @pl.when(i > 0)
        def _():
            l_ref[...] += jnp.broadcast_to(part, l_ref.shape)
            dg_ref[...] += dgp

    row = ((tr, D), lambda i: (i, 0))
    return _call(name, body, (S // tr,),
                 [(x,) + row, (tgt,) + row, (g, (1, D), lambda i: (0, 0))],
                 [((1, LANES), F32, (1, LANES), lambda i: (0, 0)), ((S, D), F32) + row, ((S, D), BF16) + row,
                  ((1, D), F32, (1, D), lambda i: (0, 0))], sem=("arbitrary",))


FFN_TF = 4 * FFN_SHARD


def _ffn_pick(G, which):
    if isinstance(G, tuple):
        return (G[0], which) if which < 2 else (G[1], 0)
    return G, which


def _ffn_w_spec(G, which, imap):
    arr, blk = _ffn_pick(G, which)
    return (arr, (4, FFN_SHARD, arr.shape[2]), lambda *idx: (imap(*idx), blk, 0))


def _ffn_whole_w_spec(G, which):
    arr, blk = _ffn_pick(G, which)
    return (arr, (N_DEV, FFN_SHARD, arr.shape[2]), lambda *idx: (0, blk, 0), pl.Buffered(1))


def ffn_up(n, G, name):
    S, D = n.shape
    F = N_DEV * FFN_SHARD
    tm = 256

    def body(n_ref, w1_ref, w3_ref, abh_ref):
        nv = n_ref[...]
        a = _dot(nv, w1_ref[...].reshape(F, D), 1, 1).astype(BF16)
        b = _dot(nv, w3_ref[...].reshape(F, D), 1, 1).astype(BF16)
        abh_ref[0] = a
        abh_ref[1] = b
        av, bv = a.astype(F32), b.astype(F32)
        abh_ref[2] = (av * _sigmoid(av) * bv).astype(BF16)

    return _call(name, body, (S // tm,),
                 [(n, (tm, D), lambda i: (i, 0)),
                  _ffn_whole_w_spec(G, 0), _ffn_whole_w_spec(G, 1)],
                 [((3, S, F), BF16, (3, tm, F), lambda i: (0, i, 0))],
                 sem=("parallel",))[0]


def ffn_down(abh, G, x, name):
    _, S, F = abh.shape
    D = x.shape[1]
    tm = 512

    def body(h_ref, w2_ref, x_ref, o_ref):
        o_ref[...] = x_ref[...] + 0.5 * _dot(h_ref[...], w2_ref[...].reshape(F, D))

    return _call(name, body, (S // tm,),
                 [(abh, (None, tm, F), lambda i: (2, i, 0)), _ffn_whole_w_spec(G, 2),
                  (x, (tm, D), lambda i: (i, 0))],
                 [((S, D), F32, (tm, D), lambda i: (i, 0))], sem=("parallel",))[0]


def ffn_bwd_weights(dxo, abh, n, G, name):
    _, S, F = abh.shape
    D = dxo.shape[1]
    tm = 512
    nf = F // FFN_TF

    def down_body(d_ref, w2_ref, ab_ref, o_ref):
        dh = 0.5 * _dot(d_ref[...].astype(BF16), w2_ref[...].reshape(F, D), 1, 1)
        av, bv = ab_ref[0].astype(F32), ab_ref[1].astype(F32)
        sig = _sigmoid(av)
        o_ref[0] = (dh * bv * (sig * (1.0 + av * (1.0 - sig)))).astype(BF16)
        o_ref[1] = (dh * (av * sig)).astype(BF16)

    tmd = 256
    dab = _call(name + "_down_bwd", down_body, (S // tmd,),
                [(dxo, (tmd, D), lambda i: (i, 0)), _ffn_whole_w_spec(G, 2),
                 (abh, (2, tmd, F), lambda i: (0, i, 0))],
                [((2, S, F), BF16, (2, tmd, F), lambda i: (0, i, 0))],
                sem=("parallel",))[0]

    tk = WGRAD_TK
    nk = S // tk
    gshape = (N_DEV, 3 * FFN_SHARD, D)

    def dw2_body(h_ref, d_ref, o_ref, acc_ref):
        k = pl.program_id(1)
        p = _dot(h_ref[...], d_ref[...].astype(BF16), 0, 0)

        @pl.when(k == 0)
        def _():
            acc_ref[...] = p

        @pl.when(k > 0)
        def _():
            acc_ref[...] += p

        @pl.when(k == nk - 1)
        def _():
            o_ref[...] = (0.5 * acc_ref[...]).astype(BF16).reshape(4, FFN_SHARD, D)

    gw = _call(name + "_dw2", dw2_body, (nf, nk),
               [(abh, (None, tk, FFN_TF), lambda j, k: (2, k, j)), (dxo, (tk, D), lambda j, k: (k, 0))],
               [(gshape, BF16, (4, FFN_SHARD, D), lambda j, k: (j, 2, 0))],
               scratch=[pltpu.VMEM((FFN_TF, D), F32)], sem=("parallel", "arbitrary"))[0]

    def dw13_body(gw_ref, dab_ref, n_ref, o_ref):
        o_ref[...] = _dot(dab_ref[...], n_ref[...], 0, 0).astype(BF16).reshape(4, FFN_SHARD, D)

    gw = pl.pallas_call(
        dw13_body,
        out_shape=jax.ShapeDtypeStruct(gshape, BF16),
        grid=(2, nf),
        in_specs=[pl.BlockSpec(memory_space=pl.ANY),
                  pl.BlockSpec((None, S, FFN_TF), lambda w, j: (w, 0, j)),
                  pl.BlockSpec((S, D), lambda w, j: (0, 0))],
        out_specs=pl.BlockSpec((4, FFN_SHARD, D), lambda w, j: (j, w, 0)),
        input_output_aliases={0: 0},
        name=name + "_dw13",
        compiler_params=pltpu.CompilerParams(dimension_semantics=("parallel", "parallel"),
                                             vmem_limit_bytes=VMEM_LIMIT),
    )(gw, dab, n)
    return dab, gw


def ffn_bwd_input(dab, G, x_in, g, dxo, name, as_operand=True):
    _, S, F = dab.shape
    D = x_in.shape[1]
    tm = 256

    def dn_body(dab_ref, w1_ref, w3_ref, x_ref, d_ref, g_ref, dx_ref, *rest):
        dg_ref = rest[-1]
        i = pl.program_id(0)
        dn = _dot(dab_ref[0], w1_ref[...].reshape(F, D)) + _dot(dab_ref[1], w3_ref[...].reshape(F, D))
        dx, dgt = _rms_bwd_tile(dn, x_ref[...], g_ref[...])
        dx = d_ref[...] + dx
        dx_ref[...] = dx
        if as_operand:
            rest[0][...] = dx.astype(BF16)
        dgp = jnp.sum(dgt, axis=0, keepdims=True)

        @pl.when(i == 0)
        def _():
            dg_ref[...] = dgp

        @pl.when(i > 0)
        def _():
            dg_ref[...] += dgp

    tile = ((tm, D), lambda i: (i, 0))
    return _call(name + "_dn", dn_body, (S // tm,),
                 [(dab, (2, tm, F), lambda i: (0, i, 0)),
                  _ffn_whole_w_spec(G, 0), _ffn_whole_w_spec(G, 1),
                  (x_in,) + tile, (dxo,) + tile, (g, (1, D), lambda i: (0, 0))],
                 [((S, D), F32) + tile] + ([((S, D), BF16) + tile] if as_operand else [])
                 + [((1, D), F32, (1, D), lambda i: (0, 0))],
                 sem=("arbitrary",))


PROJ_TN = 512
DH_SHARDS_PER_STEP = 4


def in_proj(h, Gm, first_tile, n_tiles, dtype, name, tile_stride=1):
    S, D = h.shape
    tile = lambda j: first_tile + tile_stride * j

    def body(h_ref, w_ref, o_ref):
        o_ref[...] = _dot(h_ref[...], w_ref[...]).astype(dtype)

    return _call(name, body, (n_tiles,),
                 [(h, (S, D), lambda j: (0, 0)),
                  (Gm, (None, D, PROJ_TN), lambda j: (tile(j) // 2, 0, tile(j) % 2))],
                 [((S, n_tiles * PROJ_TN), dtype, (S, PROJ_TN), lambda j: (0, j))],
                 sem=("parallel",))[0]


def _dproj_pieces(dqkv, dq_b, dkv_b, dgate):
    pieces = [(dqkv[g], [(3 * which + g, (which, 0)) for which in range(3)]) for g in range(3)]
    pieces.append((dq_b, [(9, (None, 0)), (10, (None, 1))]))
    pieces.append((dkv_b, [(11, (None, 0))]))
    pieces.append((dgate, [(12 + 2 * a + b, (a, b)) for a in range(2) for b in range(2)]))
    return pieces


def in_proj_bwd_dw(pieces, h, gm_grads, name):
    S, D = h.shape

    for n_piece, (arr, tiles) in enumerate(pieces):
        w_tile = [t for t, _ in tiles]
        lead = [ix[0] for _, ix in tiles]
        colb = [ix[1] for _, ix in tiles]

        def pick(table, j):
            out = table[-1]
            for k in range(len(table) - 2, -1, -1):
                out = jnp.where(j == k, table[k], out)
            return out

        def dw_body(gm_ref, h_ref, d_ref, o_ref):
            o_ref[...] = _dot(h_ref[...], d_ref[...], 0, 0).astype(BF16)

        if arr.ndim == 3:
            d_spec = pl.BlockSpec((None, S, PROJ_TN), lambda j, lead=lead, colb=colb: (pick(lead, j), 0, pick(colb, j)))
        else:
            d_spec = pl.BlockSpec((S, PROJ_TN), lambda j, colb=colb: (0, pick(colb, j)))
        gm_grads = pl.pallas_call(
            dw_body,
            out_shape=jax.ShapeDtypeStruct(gm_grads.shape, BF16),
            grid=(len(tiles),),
            in_specs=[pl.BlockSpec(memory_space=pl.ANY), pl.BlockSpec((S, D), lambda j: (0, 0)), d_spec],
            out_specs=pl.BlockSpec((None, D, PROJ_TN),
                                   lambda j, w_tile=w_tile: (pick(w_tile, j) // 2, 0, pick(w_tile, j) % 2)),
            input_output_aliases={0: 0},
            name="%s_dw%d" % (name, n_piece),
            compiler_params=pltpu.CompilerParams(dimension_semantics=("parallel",), vmem_limit_bytes=VMEM_LIMIT),
        )(gm_grads, h, arr)
    return gm_grads


def in_proj_bwd_dh(pieces, Gm, x_in, g, dres, name):
    S, D = x_in.shape
    tm = 256
    C = Gm.shape[2]
    n_sh = N_DEV
    n_p = len(pieces)

    def dh_body(*refs):
        d_refs = refs[:n_p]
        w_ref, x_ref, r_ref, g_ref, dx_ref, dxb_ref, dg_ref = refs[n_p:]
        i = pl.program_id(0)
        p = None
        for d_ref, (arr, tiles) in zip(d_refs, pieces):
            for t, (lead, colb) in tiles:
                cols = slice(colb * PROJ_TN, (colb + 1) * PROJ_TN)
                d = d_ref[:, cols] if lead is None else d_ref[lead, :, cols]
                wcol = (t % 2) * PROJ_TN
                term = _dot(d, w_ref[t // 2, :, wcol:wcol + PROJ_TN], 1, 1)
                p = term if p is None else p + term
        dx, dgt = _rms_bwd_tile(p, x_ref[...], g_ref[...])
        dx = r_ref[...] + dx
        dx_ref[...] = dx
        dxb_ref[...] = dx.astype(BF16)
        dgp = jnp.sum(dgt, axis=0, keepdims=True)

        @pl.when(i == 0)
        def _():
            dg_ref[...] = dgp

        @pl.when(i > 0)
        def _():
            dg_ref[...] += dgp

    tile = ((tm, D), lambda i: (i, 0))

    def rows_of(arr):
        if arr.ndim == 3:
            return (arr, (arr.shape[0], tm, arr.shape[2]), lambda i: (0, i, 0))
        return (arr, (tm, arr.shape[1]), lambda i: (i, 0))

    return _call(name + "_dh", dh_body, (S // tm,),
                 [rows_of(arr) for arr, _ in pieces]
                 + [(Gm, (n_sh, D, C), lambda i: (0, 0, 0), pl.Buffered(1)),
                    (x_in,) + tile, (dres,) + tile, (g, (1, D), lambda i: (0, 0))],
                 [((S, D), F32) + tile, ((S, D), BF16) + tile, ((1, D), F32, (1, D), lambda i: (0, 0))],
                 sem=("arbitrary",))


def _t5_bucket(rel):
    n = N_BUCKETS // 2
    max_exact = n // 2
    ret = jnp.where(rel > 0, n, 0)
    a = jnp.abs(rel)
    af = jnp.maximum(a, 1).astype(F32)
    large = max_exact + (jnp.log(af / max_exact) / math.log(MAX_DISTANCE / max_exact)
                         * (n - max_exact)).astype(jnp.int32)
    large = jnp.minimum(large, n - 1)
    return ret + jnp.where(a < max_exact, a, large)


def _bucket_tables():
    qi = jnp.arange(A_TQ, dtype=jnp.int32)[:, None]
    kj = jnp.arange(A_WIN, dtype=jnp.int32)[None, :]
    rel = kj - HALF_WINDOW - qi
    return jnp.stack([_t5_bucket(rel * d) for d in DILATIONS], axis=0)


def bias_build(rel_bias, buckets):
    def body(tab_ref, bk_ref, o_ref):
        col = pl.program_id(0) * HEADS_PER_GROUP_A + pl.program_id(1)
        bk = bk_ref[...]
        acc = jnp.zeros(bk.shape, F32)
        for b in range(N_BUCKETS):
            acc = jnp.where(bk == b, tab_ref[b, col], acc)
        qi = lax.broadcasted_iota(jnp.int32, bk.shape, 0)
        kj = lax.broadcasted_iota(jnp.int32, bk.shape, 1)
        band = jnp.where(jnp.abs(kj - HALF_WINDOW - qi) <= HALF_WINDOW, acc, NEG_INF)
        o_ref[0] = jnp.where(kj >= HALF_WINDOW, band, NEG_INF)
        o_ref[1] = band
        o_ref[2] = jnp.where(kj < A_TQ + HALF_WINDOW, band, NEG_INF)

    out = pl.pallas_call(
        body,
        out_shape=jax.ShapeDtypeStruct((3, HEADS_PER_GROUP_A // 2, 3, 2, A_TQ, A_WIN), F32),
        grid=(3, HEADS_PER_GROUP_A),
        in_specs=[pl.BlockSpec(memory_space=pltpu.SMEM),
                  pl.BlockSpec((None, A_TQ, A_WIN), lambda g, h: (g, 0, 0))],
        out_specs=pl.BlockSpec((None, None, 3, None, A_TQ, A_WIN), lambda g, h: (g, h // 2, 0, h % 2, 0, 0)),
        name="a_bias_build",
        compiler_params=pltpu.CompilerParams(dimension_semantics=("parallel", "parallel")),
    )(rel_bias, buckets)
    return out.reshape(3, HEADS_PER_GROUP_A // 2, 3, 2 * A_TQ, A_WIN)


def bias_bwd(dbias, buckets):
    def body(d_ref, bk_ref, o_ref):
        bk = bk_ref[...]
        dv = d_ref[...]
        for b in range(N_BUCKETS):
            part = jnp.sum(jnp.where(bk == b, dv, 0.0), axis=1, keepdims=True)
            o_ref[b:b + 1, :] = jnp.broadcast_to(jnp.sum(part, axis=0, keepdims=True), (1, LANES))

    out = pl.pallas_call(
        body,
        out_shape=jax.ShapeDtypeStruct((3, HEADS_PER_GROUP_A, N_BUCKETS, LANES), F32),
        grid=(3, HEADS_PER_GROUP_A),
        in_specs=[pl.BlockSpec((None, None, A_TQ, A_WIN), lambda g, h: (g, h, 0, 0)),
                  pl.BlockSpec((None, A_TQ, A_WIN), lambda g, h: (g, 0, 0))],
        out_specs=pl.BlockSpec((None, None, N_BUCKETS, LANES), lambda g, h: (g, h, 0, 0)),
        name="a_bias_bwd",
        compiler_params=pltpu.CompilerParams(dimension_semantics=("parallel", "parallel")),
    )(dbias, buckets)
    return out[:, :, :, 0].transpose(2, 0, 1).reshape(N_BUCKETS, 3 * HEADS_PER_GROUP_A)


def _a_fill_padded(pad_ref, src_ref, n, pad):
    zeros = jnp.zeros((pad, LANES), pad_ref.dtype)
    pad_ref[0:pad, :] = zeros
    pad_ref[pad + n:2 * pad + n, :] = zeros
    pad_ref[pad:pad + n, :] = src_ref[...].astype(pad_ref.dtype)


def _a_stack_heads(x, lane):
    zero = jnp.zeros_like(x)
    return jnp.concatenate([jnp.where(lane < HEAD_DIM_A, x, zero), jnp.where(lane >= HEAD_DIM_A, x, zero)], axis=0)


def _a_bias_variant(qb, nqb):
    return jnp.where(qb == 0, 0, jnp.where(qb == nqb - 1, 2, 1))


def a_fwd(proj_g, bias_g, g, name):
    S = proj_g.shape[0]
    d = DILATIONS[g]
    L = S // d
    nqb = L // A_TQ
    pad = HALF_WINDOW * d

    def body(q_ref, k_ref, v_ref, b_ref, o_ref, l_ref, qf, kpad, vpad):
        qf[...] = q_ref[...].astype(F32) * A_SCALE
        _a_fill_padded(kpad, k_ref, S, pad)
        _a_fill_padded(vpad, v_ref, S, pad)
        lane = lax.broadcasted_iota(jnp.int32, (A_TQ, LANES), 1)

        def block(t, carry):
            qb, r = t // d, t % d
            start = qb * (A_TQ * d) + r
            kw = kpad[pl.ds(start, A_WIN, stride=d), :].astype(BF16)
            vw = vpad[pl.ds(start, A_WIN, stride=d), :].astype(BF16)
            q = qf[pl.ds(start, A_TQ, stride=d), :].astype(BF16)
            q2 = _a_stack_heads(q, lane)
            s = _dot(q2, kw, 1, 1) + b_ref[_a_bias_variant(qb, nqb)]
            m = jnp.max(s, axis=-1, keepdims=True)
            e = jnp.exp(s - m)
            l = jnp.sum(e, axis=-1, keepdims=True)
            o2 = _dot(e.astype(BF16), vw) / l
            lse2 = m + jnp.log(l)
            o_ref[pl.ds(start, A_TQ, stride=d), :] = jnp.where(lane < HEAD_DIM_A, o2[0:A_TQ], o2[A_TQ:])
            l_ref[pl.ds(start, A_TQ, stride=d), :] = jnp.where(lane < HEAD_DIM_A, lse2[0:A_TQ], lse2[A_TQ:])
            return carry

        lax.fori_loop(0, nqb * d, block, 0, unroll=A_UNROLL)

    out_spec = ((S, GROUP_WIDTH_A), F32, (S, LANES), lambda hp: (0, hp))
    return _call(name, body, (4,),
                 [(proj_g, (S, LANES), lambda hp: (0, hp)),
                  (proj_g, (S, LANES), lambda hp: (0, 4 + hp)),
                  (proj_g, (S, LANES), lambda hp: (0, 8 + hp)),
                  (bias_g, (None, 3, 2 * A_TQ, A_WIN), lambda hp: (hp, 0, 0, 0))],
                 [out_spec, out_spec],
                 scratch=[pltpu.VMEM((S, LANES), F32)] + [pltpu.VMEM((S + 2 * pad, LANES), F32)] * 2,
                 sem=("parallel",))


def a_combine(outs, lses, name):
    S, W = outs[0].shape
    tr = 512

    def body(o0, o1, o2, l0, l1, l2, oa_ref, lt_ref):
        a, b, c = l0[...], l1[...], l2[...]
        m = jnp.maximum(jnp.maximum(a, b), c)
        ea, eb, ec = jnp.exp(a - m), jnp.exp(b - m), jnp.exp(c - m)
        z = ea + eb + ec
        oa_ref[...] = ((ea * o0[...] + eb * o1[...] + ec * o2[...]) / z).astype(BF16)
        lt_ref[...] = m + jnp.log(z)

    spec = ((tr, W), lambda i: (i, 0))
    return _call(name, body, (S // tr,), [(a,) + spec for a in (*outs, *lses)],
                 [((S, W), BF16) + spec, ((S, W), F32) + spec], sem=("parallel",))


def a_bwd(proj_g, bias_g, do_a, o_a, lse_tot, g, name):
    S = proj_g.shape[0]
    d = DILATIONS[g]
    L = S // d
    nqb = L // A_TQ
    pad = HALF_WINDOW * d

    def body(q_ref, k_ref, v_ref, b_ref, do_ref, o_ref, l_ref, dqkv_ref, db_ref,
             qf, of, dqf, kpad, vpad, dkacc, dvacc):
        qf[...] = q_ref[...].astype(F32) * A_SCALE
        of[...] = o_ref[...].astype(F32)
        _a_fill_padded(kpad, k_ref, S, pad)
        _a_fill_padded(vpad, v_ref, S, pad)
        dkacc[...] = jnp.zeros(dkacc.shape, F32)
        dvacc[...] = jnp.zeros(dvacc.shape, F32)
        db_ref[...] = jnp.zeros(db_ref.shape, F32)
        lane = lax.broadcasted_iota(jnp.int32, (A_TQ, LANES), 1)

        def block(t, carry):
            qb, r = t // d, t % d
            start = qb * (A_TQ * d) + r
            rows = pl.ds(start, A_TQ, stride=d)
            win = pl.ds(start, A_WIN, stride=d)
            kw = kpad[win, :].astype(BF16)
            vw = vpad[win, :].astype(BF16)
            q = qf[rows, :].astype(BF16)
            do = do_ref[rows, :]
            ov = of[rows, :]
            lt = l_ref[rows, :]
            q2 = _a_stack_heads(q, lane)
            do2 = _a_stack_heads(do, lane)
            lt2 = jnp.concatenate([lt[:, 0:1], lt[:, HEAD_DIM_A:HEAD_DIM_A + 1]], axis=0)
            s = _dot(q2, kw, 1, 1) + b_ref[_a_bias_variant(qb, nqb)]
            p = jnp.exp(s - lt2)
            t = jnp.sum(do2 * jnp.concatenate([ov, ov], axis=0), axis=-1, keepdims=True)
            dob2 = do2.astype(BF16)
            ds = p * (_dot(dob2, vw, 1, 1) - t)
            db_ref[...] += ds
            dsb = ds.astype(BF16)
            dq2 = _dot(dsb, kw)
            dqf[rows, :] = jnp.where(lane < HEAD_DIM_A, dq2[0:A_TQ], dq2[A_TQ:]) * A_SCALE
            dkacc[win, :] += _dot(dsb, q2, 0, 0)
            dvacc[win, :] += _dot(p.astype(BF16), dob2, 0, 0)
            return carry

        lax.fori_loop(0, nqb * d, block, 0, unroll=A_UNROLL)
        dqkv_ref[0] = dqf[...].astype(BF16)
        dqkv_ref[1] = dkacc[pad:pad + S, :].astype(BF16)
        dqkv_ref[2] = dvacc[pad:pad + S, :].astype(BF16)

    slab = ((S, LANES), lambda hp: (0, hp))
    padded = pltpu.VMEM((S + 2 * pad, LANES), F32)
    return _call(
        name, body, (4,),
        [(proj_g, (S, LANES), lambda hp: (0, hp)),
         (proj_g, (S, LANES), lambda hp: (0, 4 + hp)),
         (proj_g, (S, LANES), lambda hp: (0, 8 + hp)),
         (bias_g, (None, 3, 2 * A_TQ, A_WIN), lambda hp: (hp, 0, 0, 0)),
         (do_a,) + slab, (o_a,) + slab, (lse_tot,) + slab],
        [((3, S, GROUP_WIDTH_A), BF16, (3, S, LANES), lambda hp: (0, 0, hp)),
         ((4, 2 * A_TQ, A_WIN), F32, (None, 2 * A_TQ, A_WIN), lambda hp: (hp, 0, 0))],
        scratch=[pltpu.VMEM((S, LANES), F32)] * 3 + [padded] * 4,
        sem=("parallel",))


def _rope_tables(S):
    rows = S // GRID_W
    row = jnp.repeat(jnp.arange(rows, dtype=F32), GRID_W)
    col = jnp.tile(jnp.arange(GRID_W, dtype=F32), rows)
    n_freq = HEAD_DIM_B // 4
    freq = ROPE_THETA ** (-jnp.arange(n_freq, dtype=F32) / n_freq)
    ang = jnp.concatenate([row[:, None] * freq, col[:, None] * freq], axis=-1)
    cos, sin = jnp.cos(ang), jnp.sin(ang)
    return jnp.repeat(cos, 2, axis=-1), jnp.stack([-sin, sin], axis=-1).reshape(S, HEAD_DIM_B)


def _swap_pairs(y):
    lane = lax.broadcasted_iota(jnp.int32, y.shape, 1)
    return jnp.where(lane % 2 == 0, pltpu.roll(y, LANES - 1, 1), pltpu.roll(y, 1, 1))


def qkv_prep(proj_b, gains, cos_t, sin_t, name):
    S = proj_b.shape[0]
    ts = 256
    n_rot = N_HEADS_B + N_KV_B
    nh = n_rot + N_KV_B
    W = nh * LANES

    def body(x_ref, g_ref, c_ref, s_ref, o_ref):
        cv, sv = c_ref[...], s_ref[...]
        for hb in range(nh):
            cols = slice(hb * LANES, (hb + 1) * LANES)
            xv = x_ref[:, cols]
            if hb < n_rot:
                r = lax.rsqrt(jnp.mean(xv * xv, axis=-1, keepdims=True) + EPS)
                yv = xv * r * g_ref[:, cols]
                o_ref[:, cols] = (yv * cv + _swap_pairs(yv) * sv).astype(BF16)
            else:
                o_ref[:, cols] = xv.astype(BF16)

    return _call(name, body, (S // ts,),
                 [(proj_b, (ts, W), lambda i: (i, 0)), (gains, (1, W), lambda i: (0, 0)),
                  (cos_t, (ts, LANES), lambda i: (i, 0)), (sin_t, (ts, LANES), lambda i: (i, 0))],
                 [((S, W), BF16, (ts, W), lambda i: (i, 0))],
                 sem=("parallel",))[0]


def qk_prep_bwd(dr, proj_b, col0, gain, cos_t, sin_t, name):
    S, W = dr.shape
    H = W // LANES
    ts = 256
    xb = (col0 * LANES) // W

    def body(d_ref, x_ref, g_ref, c_ref, s_ref, dx_ref, dg_ref):
        i = pl.program_id(0)
        cv, sv, gv = c_ref[...], s_ref[...], g_ref[...]
        dgp = jnp.zeros((1, LANES), F32)
        for hb in range(H):
            cols = slice(hb * LANES, (hb + 1) * LANES)
            dout = d_ref[:, cols]
            dy = dout * cv + _swap_pairs(dout * sv)
            dx, dgt = _rms_bwd_tile(dy, x_ref[:, cols], gv)
            dx_ref[:, cols] = dx.astype(BF16)
            dgp = dgp + jnp.sum(dgt, axis=0, keepdims=True)

        @pl.when(i == 0)
        def _():
            dg_ref[...] = dgp

        @pl.when(i > 0)
        def _():
            dg_ref[...] += dgp

    return _call(name, body, (S // ts,),
                 [(dr, (ts, W), lambda i: (i, 0)), (proj_b, (ts, W), lambda i: (i, xb)),
                  (gain, (1, LANES), lambda i: (0, 0)),
                  (cos_t, (ts, LANES), lambda i: (i, 0)), (sin_t, (ts, LANES), lambda i: (i, 0))],
                 [((S, W), BF16, (ts, W), lambda i: (i, 0)),
                  ((1, LANES), F32, (1, LANES), lambda i: (0, 0))],
                 sem=("arbitrary",))


def _row_sums(x):
    hi = x.astype(BF16)
    lo = (x - hi.astype(F32)).astype(BF16)
    ones = jnp.ones((8, LANES), BF16)
    return (_dot(ones, hi, 1, 1) + _dot(ones, lo, 1, 1))[0:1, :]


def flash_fwd(qkv, name):
    S = qkv.shape[0]
    tq = B_TQ_FWD
    scale = HEAD_DIM_B ** -0.5

    hps = B_HEADS_PER_STEP

    def body(q_ref, k_ref, v_ref, o_ref, l_ref):
        k, v = k_ref[...], v_ref[...]
        for j in range(hps):
            cols = slice(j * LANES, (j + 1) * LANES)
            s = _dot(q_ref[:, cols], k, 1, 1)
            m = jnp.max(s, axis=-1, keepdims=True)
            e = jnp.exp2((s - m) * (scale * LOG2E))
            l = jnp.sum(e, axis=-1, keepdims=True)
            o_ref[:, cols] = (_dot(e.astype(BF16), v) / l).astype(BF16)
            lse = jnp.broadcast_to(m * scale + jnp.log(l), (tq, LANES))
            l_ref[j] = _row_sums(lse) * (1.0 / LANES)

    per = GQA_GROUP_B // hps
    heads = lambda g, h, i: (i, g * per + h)
    return _call(name, body, (N_KV_B, per, S // tq),
                 [(qkv, (tq, hps * LANES), heads),
                  (qkv, (S, LANES), lambda g, h, i: (0, N_HEADS_B + g)),
                  (qkv, (S, LANES), lambda g, h, i: (0, N_HEADS_B + N_KV_B + g))],
                 [((S, N_HEADS_B * LANES), BF16, (tq, hps * LANES), heads),
                  ((N_HEADS_B, 1, S), F32, (hps, 1, tq), lambda g, h, i: (g * per + h, 0, i))],
                 sem=("parallel", "parallel", "parallel"))


def flash_bwd(qkv, k_t, do_b, o_b, lse, name):
    S = qkv.shape[0]
    tq = B_TQ_BWD
    nq = S // tq
    scale = HEAD_DIM_B ** -0.5

    def body(q_ref, k_ref, v_ref, kt_ref, do_ref, o_ref, l_ref, dq_ref, dk_ref, dv_ref, dkacc, dvacc):
        h, i = pl.program_id(1), pl.program_id(2)

        @pl.when((h == 0) & (i == 0))
        def _():
            dkacc[...] = jnp.zeros(dkacc.shape, F32)
            dvacc[...] = jnp.zeros(dvacc.shape, F32)

        q = q_ref[...]
        do = do_ref[...]
        dob = do.astype(BF16)
        t = _row_sums(do * o_ref[...].astype(F32))
        pt = jnp.exp2(_dot(k_ref[...], q, 1, 1) * (scale * LOG2E) - l_ref[...] * LOG2E)
        dsb = (pt * (_dot(v_ref[...], dob, 1, 1) - t)).astype(BF16)
        dvacc[...] += _dot(pt.astype(BF16), dob)
        dkacc[...] += _dot(dsb, q)
        dq_ref[...] = _dot(kt_ref[...], dsb).T * scale

        @pl.when((h == GQA_GROUP_B - 1) & (i == nq - 1))
        def _():
            dk_ref[...] = dkacc[...] * scale
            dv_ref[...] = dvacc[...].astype(BF16)

    head = lambda g, h, i: (i, g * GQA_GROUP_B + h)
    return _call(name, body, (N_KV_B, GQA_GROUP_B, nq),
                 [(qkv, (tq, LANES), head),
                  (qkv, (S, LANES), lambda g, h, i: (0, N_HEADS_B + g)),
                  (qkv, (S, LANES), lambda g, h, i: (0, N_HEADS_B + N_KV_B + g)),
                  (k_t, (LANES, S), lambda g, h, i: (g, 0)),
                  (do_b, (tq, LANES), head), (o_b, (tq, LANES), head),
                  (lse, (None, 1, tq), lambda g, h, i: (g * GQA_GROUP_B + h, 0, i))],
                 [((S, N_HEADS_B * LANES), F32, (tq, LANES), head),
                  ((S, N_KV_B * LANES), F32, (S, LANES), lambda g, h, i: (0, g)),
                  ((S, N_KV_B * LANES), BF16, (S, LANES), lambda g, h, i: (0, g))],
                 scratch=[pltpu.VMEM((S, LANES), F32)] * 2,
                 sem=("parallel", "arbitrary", "arbitrary"))


MERGE_TN = 512


def _mix_rows_spec(Gm, row0, n_slots, slot_map, cols=None, col_map=None):
    C = Gm.shape[2] if cols is None else cols
    cm = (lambda *idx: 0) if col_map is None else col_map
    return (Gm, (n_slots, LANES, C), lambda *idx: (slot_map(*idx), row0 // LANES, cm(*idx)))


def _gate_specs(proj_b, tm):
    first = PB_GATE_A // MERGE_TN
    return [(proj_b, (tm, MERGE_TN), lambda i, k=k: (i, first + k)) for k in range(4)]


def _whole_rows_spec(Gm, row0):
    return _mix_rows_spec(Gm, row0, N_DEV, lambda *idx: 0)


def merge_fwd(o_a, o_b, w_a, Gm, proj_b, b_gate, x, name):
    S, D = x.shape
    tm = 256

    def body(oa_ref, ob_ref, wa_ref, wb_ref, wo_ref, g0, g1, g2, g3, bg_ref, x_ref, m_ref, ya_ref, yb_ref, xo_ref):
        ya = _dot(oa_ref[...], wa_ref[...])
        yb = _dot(ob_ref[...], wb_ref[...].reshape(N_DEV * LANES, D))
        ga = _sigmoid(jnp.concatenate([g0[...], g1[...]], axis=1) + bg_ref[:, 0:D])
        gb = _sigmoid(jnp.concatenate([g2[...], g3[...]], axis=1) + bg_ref[:, D:2 * D])
        merged = (ga * ya + gb * yb).astype(BF16)
        m_ref[...] = merged
        ya_ref[...] = ya.astype(BF16)
        yb_ref[...] = yb.astype(BF16)
        xo_ref[...] = x_ref[...] + _dot(merged, wo_ref[...].reshape(N_DEV * LANES, D))

    rows = lambda a: (a, (tm, a.shape[1]), lambda i: (i, 0))
    out = ((S, D), BF16, (tm, D), lambda i: (i, 0))
    return _call(name, body, (S // tm,),
                 [rows(o_a), rows(o_b), (w_a, w_a.shape, lambda i: (0, 0)),
                  _whole_rows_spec(Gm, REST_WB), _whole_rows_spec(Gm, REST_WOUT)]
                 + _gate_specs(proj_b, tm) + [(b_gate, (1, 2 * D), lambda i: (0, 0)), rows(x)],
                 [out, out, out, ((S, D), F32, (tm, D), lambda i: (i, 0))], sem=("parallel",))


def merge_bwd(dx2, w_a, Gm, ya, yb, proj_b, b_gate, name):
    S, D = dx2.shape
    tm = 256

    def body(d_ref, wo_ref, wa_ref, wb_ref, ya_ref, yb_ref, g0, g1, g2, g3, bg_ref,
             dya_ref, dyb_ref, dg_ref, dbg_ref, doa_ref, dob_ref):
        i = pl.program_id(0)
        dm = _dot(d_ref[...].astype(BF16), wo_ref[...].reshape(N_DEV * LANES, D), 1, 1)
        ga = _sigmoid(jnp.concatenate([g0[...], g1[...]], axis=1) + bg_ref[:, 0:D])
        gb = _sigmoid(jnp.concatenate([g2[...], g3[...]], axis=1) + bg_ref[:, D:2 * D])
        dya = (dm * ga).astype(BF16)
        dyb = (dm * gb).astype(BF16)
        dya_ref[...] = dya
        dyb_ref[...] = dyb
        dpa = dm * ya_ref[...].astype(F32) * ga * (1.0 - ga)
        dpb = dm * yb_ref[...].astype(F32) * gb * (1.0 - gb)
        dg_ref[0] = dpa.astype(BF16)
        dg_ref[1] = dpb.astype(BF16)
        doa_ref[...] = _dot(dya, wa_ref[...], 1, 1)
        dob_ref[...] = _dot(dyb, wb_ref[...].reshape(N_DEV * LANES, D), 1, 1)
        sa = jnp.sum(dpa, axis=0, keepdims=True)
        sb = jnp.sum(dpb, axis=0, keepdims=True)

        @pl.when(i == 0)
        def _():
            dbg_ref[0] = sa
            dbg_ref[1] = sb

        @pl.when(i > 0)
        def _():
            dbg_ref[0] += sa
            dbg_ref[1] += sb

    tile = ((tm, D), lambda i: (i, 0))
    return _call(
        name, body, (S // tm,),
        [(dx2,) + tile, _whole_rows_spec(Gm, REST_WOUT), (w_a, w_a.shape, lambda i: (0, 0)),
         _whole_rows_spec(Gm, REST_WB), (ya,) + tile, (yb,) + tile]
        + _gate_specs(proj_b, tm) + [(b_gate, (1, 2 * D), lambda i: (0, 0))],
        [((S, D), BF16) + tile, ((S, D), BF16) + tile,
         ((2, S, D), BF16, (2, tm, D), lambda i: (0, i, 0)),
         ((2, 1, D), F32, (2, 1, D), lambda i: (0, 0, 0)),
         ((S, w_a.shape[0]), F32, (tm, w_a.shape[0]), lambda i: (i, 0)),
         ((S, N_HEADS_B * LANES), F32, (tm, N_HEADS_B * LANES), lambda i: (i, 0))],
        sem=("arbitrary",))


def weight_grad_rows(a, b, grads, row0, name):
    S, M = a.shape
    N = b.shape[1]
    tmm = 512
    tk = WGRAD_TK
    nk = S // tk

    def body(g_ref, a_ref, b_ref, o_ref, acc_ref):
        k = pl.program_id(1)
        p = _dot(a_ref[...], b_ref[...].astype(BF16), 0, 0)

        @pl.when(k == 0)
        def _():
            acc_ref[...] = p

        @pl.when(k > 0)
        def _():
            acc_ref[...] += p

        @pl.when(k == nk - 1)
        def _():
            o_ref[...] = acc_ref[...].astype(BF16).reshape(tmm // LANES, LANES, N)

    return pl.pallas_call(
        body,
        out_shape=jax.ShapeDtypeStruct(grads.shape, BF16),
        grid=(M // tmm, nk),
        in_specs=[pl.BlockSpec(memory_space=pl.ANY),
                  pl.BlockSpec((tk, tmm), lambda j, k: (k, j)),
                  pl.BlockSpec((tk, N), lambda j, k: (k, 0))],
        out_specs=pl.BlockSpec((tmm // LANES, LANES, N), lambda j, k: (j, row0 // LANES, 0)),
        scratch_shapes=[pltpu.VMEM((tmm, N), F32)],
        input_output_aliases={0: 0},
        name=name,
        compiler_params=pltpu.CompilerParams(dimension_semantics=("parallel", "arbitrary"),
                                             vmem_limit_bytes=VMEM_LIMIT),
    )(grads, a, b)


def weight_grad_plain(a, b, name):
    S, M = a.shape
    N = b.shape[1]
    tk = WGRAD_TK
    nk = S // tk

    def body(a_ref, b_ref, o_ref, acc_ref):
        k = pl.program_id(0)
        p = _dot(a_ref[...], b_ref[...], 0, 0)

        @pl.when(k == 0)
        def _():
            acc_ref[...] = p

        @pl.when(k > 0)
        def _():
            acc_ref[...] += p

        @pl.when(k == nk - 1)
        def _():
            o_ref[...] = acc_ref[...].astype(BF16)

    return _call(name, body, (nk,),
                 [(a, (tk, M), lambda k: (k, 0)), (b, (tk, N), lambda k: (k, 0))],
                 [((M, N), BF16, (M, N), lambda k: (0, 0))],
                 scratch=[pltpu.VMEM((M, N), F32)], sem=("arbitrary",))[0]


def local_step(x, tgt, p, get_g1_up, get_g1_down, get_gm_in, get_gm_rest, get_g2, emit, start_token):
    S, D = x.shape
    after = lambda t: t[0:1, 0:1]
    buckets = _bucket_tables()
    cos_t, sin_t = _rope_tables(S)
    gains = jnp.concatenate([jnp.tile(p["q_norm"], (1, N_HEADS_B)), jnp.tile(p["k_norm"], (1, N_KV_B)),
                             jnp.ones((1, N_KV_B * LANES), F32)], axis=1)

    n1 = rms_fwd(x, p["ffn1_norm"] + after(start_token), "ffn1_norm")
    bias = bias_build(p["rel_bias"] + after(start_token), buckets)
    g1_up = get_g1_up((n1, bias))
    ab1 = ffn_up(n1, (g1_up, None), "ffn1_up")
    G1 = (g1_up, get_g1_down(ab1))
    x1 = ffn_down(ab1, G1, x, "ffn1_down")

    hm = rms_fwd(x1, p["mix_norm"], "mix_norm")
    Gw = get_gm_in(hm)
    n_a = A_QKV_WIDTH // PROJ_TN
    proj_a = [in_proj(hm, Gw, g, 3, BF16, "in_proj_a%d" % g, tile_stride=3) for g in range(3)]
    proj_b = in_proj(hm, Gw, n_a, PB_WIDTH // PROJ_TN, F32, "in_proj_b")

    outs, lses = [], []
    for g in range(3):
        o, l = a_fwd(proj_a[g], bias[g], g, "a_fwd_%d" % g)
        outs.append(o)
        lses.append(l)
    o_a, lse_tot = a_combine(outs, lses, "a_combine")

    qkv = qkv_prep(proj_b, gains, cos_t, sin_t, "qkv_prep")
    k_t = qkv[:, N_HEADS_B * LANES:(N_HEADS_B + N_KV_B) * LANES].T
    o_b, lse_b = flash_fwd(qkv, "flash_fwd")

    Gm = get_gm_rest(o_b)
    w_a = Gm[:, REST_WA:REST_ROWS, :].reshape(N_DEV, GROUP_WIDTH_A, LANES).transpose(1, 0, 2).reshape(GROUP_WIDTH_A, D)
    merged, ya, yb, x2 = merge_fwd(o_a, o_b, w_a, Gm, proj_b, p["b_gate"], x1, "merge_fwd")

    G2 = get_g2(x2)
    n2 = rms_fwd(x2, p["ffn2_norm"], "ffn2_norm")
    ab2 = ffn_up(n2, G2, "ffn2_up")
    x3 = ffn_down(ab2, G2, x2, "ffn2_down")

    loss, dx3, dx3_b, d_final = final_loss(x3, tgt, p["final_norm"], "final_loss")

    dabh2, gw2 = ffn_bwd_weights(dx3_b, ab2, n2, G2, "ffn2_bwd")
    t2 = emit("ffn2", gw2)
    dx2, dx2_b, d_ffn2_norm = ffn_bwd_input(dabh2, G2, x2, p["ffn2_norm"] + after(t2), dx3, "ffn2_bwd")

    dya, dyb, dgate, dbg, do_a, do_b = merge_bwd(dx2_b, w_a, Gm, ya, yb, proj_b, p["b_gate"], "merge_bwd")
    gm_grads = jnp.zeros((N_DEV, MIX_ROWS, D), BF16)
    gm_grads = weight_grad_rows(merged, dx2_b, gm_grads, MIX_WOUT, "dw_out")
    gm_grads = weight_grad_rows(o_b, dyb, gm_grads, MIX_WB, "dw_branch_b")
    dw_a = weight_grad_plain(o_a, dya, "dw_branch_a")

    dq_r, dk_r, dv_b = flash_bwd(qkv, k_t, do_b, o_b, lse_b, "flash_bwd")
    dq_b, d_q_norm = qk_prep_bwd(dq_r, proj_b, 0, p["q_norm"], cos_t, sin_t, "q_prep_bwd")
    dk_b, d_k_norm = qk_prep_bwd(dk_r, proj_b, N_HEADS_B, p["k_norm"], cos_t, sin_t, "k_prep_bwd")

    dqkv, dbs = [], []
    for g in range(3):
        dg_, db = a_bwd(proj_a[g], bias[g], do_a, o_a, lse_tot, g, "a_bwd_%d" % g)
        dqkv.append(dg_)
        dbs.append(db)
    d_rel_bias = bias_bwd(jnp.stack(dbs, axis=0).reshape(3, HEADS_PER_GROUP_A, A_TQ, A_WIN), buckets)

    dproj = _dproj_pieces(dqkv, dq_b, jnp.concatenate([dk_b, dv_b], axis=1), dgate)
    gm_grads = in_proj_bwd_dw(dproj, hm, gm_grads, "in_proj_bwd")
    dw_a_sh = dw_a.reshape(GROUP_WIDTH_A, N_DEV, LANES).transpose(1, 0, 2).reshape(N_DEV, MIX_ROWS - MIX_WA, D)
    gm_grads = lax.dynamic_update_slice(gm_grads, dw_a_sh, (0, MIX_WA, 0))
    tm = emit("mix", gm_grads)
    dx1, dx1_b, d_mix_norm = in_proj_bwd_dh(dproj, Gw, x1, p["mix_norm"] + after(tm), dx2, "in_proj_bwd")

    dabh1, gw1 = ffn_bwd_weights(dx1_b, ab1, n1, G1, "ffn1_bwd")
    t1 = emit("ffn1", gw1)
    dx0, d_ffn1_norm = ffn_bwd_input(dabh1, G1, x, p["ffn1_norm"] + after(t1), dx1, "ffn1_bwd", as_operand=False)

    small = dict(ffn1_norm=d_ffn1_norm, mix_norm=d_mix_norm, b_gate=dbg.reshape(1, 2 * D),
                 q_norm=d_q_norm, k_norm=d_k_norm, rel_bias=d_rel_bias, ffn2_norm=d_ffn2_norm,
                 final_norm=d_final)
    return loss, dx0, small


def _pack_small(t, loss_row):
    row6 = jnp.concatenate([t["q_norm"].reshape(1, -1), t["k_norm"].reshape(1, -1), t["rel_bias"].reshape(1, -1)], axis=1)
    return jnp.concatenate([t["ffn1_norm"].reshape(1, -1), t["mix_norm"].reshape(1, -1), t["b_gate"].reshape(2, -1),
                            t["ffn2_norm"].reshape(1, -1), t["final_norm"].reshape(1, -1), row6, loss_row], axis=0)


def _unpack_small(a, shapes):
    return dict(ffn1_norm=a[0:1].reshape(shapes["ffn1_norm"]), mix_norm=a[1:2].reshape(shapes["mix_norm"]),
                b_gate=a[2:4].reshape(shapes["b_gate"]), ffn2_norm=a[4:5].reshape(shapes["ffn2_norm"]),
                final_norm=a[5].reshape(shapes["final_norm"]), q_norm=a[6:7, 0:128].reshape(shapes["q_norm"]),
                k_norm=a[6:7, 128:256].reshape(shapes["k_norm"]), rel_bias=a[6, 256:1024].reshape(shapes["rel_bias"]))


SMALL = ("ffn1_norm", "mix_norm", "b_gate", "q_norm", "k_norm", "rel_bias", "ffn2_norm", "final_norm")
ORDER = ("ffn1_norm", "ffn1_w1", "ffn1_w3", "ffn1_w2", "mix_norm", "w_in", "b_gate", "q_norm", "k_norm", "rel_bias",
         "w_branch_a", "w_branch_b", "w_out", "ffn2_norm", "ffn2_w1", "ffn2_w3", "ffn2_w2", "final_norm")


def kernel(x, ffn1_norm, ffn1_w1, ffn1_w3, ffn1_w2, mix_norm, w_in, b_gate, q_norm, k_norm, rel_bias, w_branch_a, w_branch_b, w_out, ffn2_norm, ffn2_w1, ffn2_w3, ffn2_w2, final_norm, loss_target, m_ffn1_norm, m_ffn1_w1, m_ffn1_w3, m_ffn1_w2, m_mix_norm, m_w_in, m_b_gate, m_q_norm, m_k_norm, m_rel_bias, m_w_branch_a, m_w_branch_b, m_w_out, m_ffn2_norm, m_ffn2_w1, m_ffn2_w3, m_ffn2_w2, m_final_norm, v_ffn1_norm, v_ffn1_w1, v_ffn1_w3, v_ffn1_w2, v_mix_norm, v_w_in, v_b_gate, v_q_norm, v_k_norm, v_rel_bias, v_w_branch_a, v_w_branch_b, v_w_out, v_ffn2_norm, v_ffn2_w1, v_ffn2_w3, v_ffn2_w2, v_final_norm):
    args = dict(locals())
    w = {n: args[n] for n in ORDER}
    m = {n: args["m_" + n] for n in ORDER}
    v = {n: args["v_" + n] for n in ORDER}
    D = x.shape[2]

    blocks = (
        ("ffn1_up", jnp.concatenate([ffn1_w1[0].T, ffn1_w3[0].T], axis=0)),
        ("ffn1_down", ffn1_w2[0]),
        ("mix_in", w_in[0]),
        ("mix_rest", jnp.concatenate([w_branch_b[0], w_out[0], w_branch_a[0].reshape(REST_ROWS - REST_WA, D)], axis=0)),
        ("ffn2", jnp.concatenate([ffn2_w1[0].T, ffn2_w3[0].T, ffn2_w2[0]], axis=0)),
    )
    direct = ("mix_rest", "ffn2")
    started = all_gather_start_all([(b.astype(BF16), tag in direct) for tag, b in blocks], "all_gather_start")
    gathers = {tag: s for (tag, _), s in zip(blocks, started)}
    start_token = started[0][4]

    def gathered(tag):
        def get(after):
            if tag in direct:
                return all_gather_place_own(*_split_wait("all_gather_" + tag + "_wait", gathers[tag], N_DEV - 1, after),
                                            "all_gather_" + tag + "_own")
            return all_gather_finish(*_split_wait("all_gather_" + tag + "_wait", gathers[tag], 4, after),
                                     "all_gather_" + tag + "_finish")
        return get

    core = lax.axis_index("c").astype(jnp.int32).reshape(1)
    chip = (2 * lax.axis_index("x") + lax.axis_index("y")).astype(jnp.int32).reshape(1)
    device = 2 * chip + core
    exchanges = {}

    def emit(tag, gw):
        if tag == "ffn1":
            (theirs,) = reduce_scatter_pair([gw], "reduce_scatter_pair_" + tag)
            part = pair_add(gw, theirs, core, "pair_add_" + tag)
            exchanges[tag] = reduce_scatter_start(part, "reduce_scatter_" + tag + "_start")
        else:
            exchanges[tag] = reduce_scatter_start_direct(gw, "reduce_scatter_" + tag + "_start")
        return exchanges[tag][4]

    small_p = dict(ffn1_norm=ffn1_norm, mix_norm=mix_norm, b_gate=b_gate, q_norm=q_norm, k_norm=k_norm,
                   rel_bias=rel_bias, ffn2_norm=ffn2_norm, final_norm=final_norm.reshape(1, D))
    loss_p, grad_x, small_g = local_step(x[0], loss_target[0], small_p, gathered("ffn1_up"), gathered("ffn1_down"),
                                         gathered("mix_in"), gathered("mix_rest"), gathered("ffn2"), emit, start_token)

    def landed(tag, after):
        n_others, me = (3, chip) if tag == "ffn1" else (N_DEV - 1, device)
        return tuple(_split_wait("reduce_scatter_" + tag + "_wait", exchanges[tag], n_others, after)) + (me,)

    grads, delta, new_m, new_v = {}, {}, {}, {}

    def finish(n, part, land, me, off, blk, transposed=False):
        shp = w[n].shape
        if transposed:
            to2 = lambda a: a.reshape(shp[-2], shp[-1]).T
            back = lambda a: a.T.reshape(shp)
        else:
            to2 = lambda a: a.reshape(shp[-2], shp[-1])
            back = lambda a: a.reshape(shp)
        res = sum_adamw(part, land, me, off, blk, to2(w[n]), to2(m[n]), to2(v[n]), "update_" + n)
        grads[n], delta[n], new_m[n], new_v[n] = [back(a) for a in res]

    last_token = exchanges["ffn1"][4]
    for tag, after in (("ffn2", last_token), ("ffn1", grad_x)):
        group = landed(tag, after)
        finish(tag + "_w1", *group, 0, FFN_SHARD, transposed=True)
        finish(tag + "_w3", *group, FFN_SHARD, FFN_SHARD, transposed=True)
        finish(tag + "_w2", *group, 2 * FFN_SHARD, FFN_SHARD)
        if tag == "ffn2":
            group_m = landed("mix", last_token)
            finish("w_in", *group_m, MIX_WIN, LANES)
            finish("w_branch_b", *group_m, MIX_WB, LANES)
            finish("w_out", *group_m, MIX_WOUT, LANES)
            grads["w_branch_a"] = sum_landed(*group_m, MIX_WA, MIX_ROWS - MIX_WA, MIX_ROWS - MIX_WA,
                                             "w_branch_a_sum").reshape(w_branch_a.shape)
    loss_row = jnp.pad(loss_p, ((0, 0), (0, D - LANES)))
    smalls = small_all_gather(_pack_small(small_g, loss_row))
    small_sum = sum_slots(smalls, 0, N_DEV, N_DEV, "small_sum")
    small_shapes = {n: w[n].shape for n in SMALL}
    grads.update(_unpack_small(small_sum, small_shapes))
    loss = small_sum[7, 0]

    n = "w_branch_a"
    two_d = lambda a: a.reshape(w[n].shape[-2], w[n].shape[-1])
    d_, m_, v_ = adamw(two_d(w[n]), two_d(grads[n]), two_d(m[n]), two_d(v[n]), "adamw_" + n)
    delta[n], new_m[n], new_v[n] = [a.reshape(w[n].shape) for a in (d_, m_, v_)]
    zero_row = jnp.zeros((1, D), F32)
    pack = lambda t: _pack_small({n: t[n] for n in SMALL}, zero_row)
    d_, m_, v_ = adamw(pack(w), small_sum, pack(m), pack(v), "adamw_small")
    for src, dst in ((d_, delta), (m_, new_m), (v_, new_v)):
        dst.update(_unpack_small(src, small_shapes))

    return (loss, grad_x[None], *[grads[n] for n in ORDER], *[delta[n] for n in ORDER],
            *[new_m[n] for n in ORDER], *[new_v[n] for n in ORDER])
```

```python
import math

import jax
import jax.numpy as jnp
from jax import lax
from jax.experimental import pallas as pl
from jax.experimental.pallas import tpu as pltpu

F32 = jnp.float32
BF16 = jnp.bfloat16
MESH = pl.DeviceIdType.MESH

V7X_VMEM_BYTES = 64 * 1024 * 1024
VMEM_LIMIT = V7X_VMEM_BYTES - 8 * 1024 * 1024
LANES = 128

N_DEV = 8
EPS = 1e-6
NEG_INF = -1e30

DILATIONS = (1, 4, 16)
HALF_WINDOW = 64
HEAD_DIM_A = 64
HEADS_PER_GROUP_A = 8
GROUP_WIDTH_A = 512
A_QKV_WIDTH = 4608
A_GROUP_QKV = A_QKV_WIDTH // 3
A_TQ = 128
A_WIN = A_TQ + 2 * HALF_WINDOW
A_UNROLL = 8
A_SCALE = HEAD_DIM_A ** -0.5
WGRAD_TK = 2048
HEAD_DIM_B = 128
N_HEADS_B = 8
N_KV_B = 2
GQA_GROUP_B = 4
GRID_W = 64
ROPE_THETA = 10000.0
B_TQ_FWD = 256
B_TQ_BWD = 512
B_HEADS_PER_STEP = 4
LOG2E = 1.4426950408889634
N_BUCKETS = 32
MAX_DISTANCE = 1024
PB_WIDTH = 3584
PB_GATE_A = 1536
PB_GATE_B = 2560

ADAM_LR = 0.001
ADAM_B1 = 0.9
ADAM_B2 = 0.999
ADAM_EPS = 1e-08
ADAM_WD = 0.01
ADAM_STEP = 10

FFN_SHARD = 352
MIX_WIN, MIX_WB, MIX_WOUT, MIX_WA = 0, 1024, 1152, 1280
MIX_ROWS = 1344
REST_WB, REST_WOUT, REST_WA, REST_ROWS = 0, 128, 256, 320


def _dot(a, b, ca=1, cb=0):
    return lax.dot_general(a, b, (((ca,), (cb,)), ((), ())), preferred_element_type=F32)


def _call(name, body, grid, ins, outs, scratch=(), sem=None, aliases=None):
    ins = [tuple(i) + (None,) * (4 - len(i)) for i in ins]
    res = pl.pallas_call(
        body,
        out_shape=[jax.ShapeDtypeStruct(s, d) for (s, d, _, _) in outs],
        grid=grid,
        in_specs=[pl.BlockSpec(bs, im, pipeline_mode=pm) for (_, bs, im, pm) in ins],
        out_specs=[pl.BlockSpec(bs, im) for (_, _, bs, im) in outs],
        scratch_shapes=list(scratch),
        name=name,
        input_output_aliases=aliases or {},
        compiler_params=pltpu.CompilerParams(dimension_semantics=sem, vmem_limit_bytes=VMEM_LIMIT),
    )(*[i[0] for i in ins])
    return res


def _sigmoid(x):
    return 0.5 * jnp.tanh(0.5 * x) + 0.5


def _position():
    return lax.axis_index("x"), lax.axis_index("y"), lax.axis_index("c")


def _hbm_specs(n):
    return [pl.BlockSpec(memory_space=pl.ANY) for _ in range(n)]


PAIR_BUFFERS = 4


def reduce_scatter_pair(grads, name):
    n = len(grads)
    C = grads[0].shape[2]
    half = [g.shape[1] // 2 for g in grads]
    chunks = [(i, q, hf) for i in range(n) for q in range(4) for hf in range(2)]
    nb = PAIR_BUFFERS

    def body(*refs):
        ins, theirs = refs[:n], refs[n:2 * n]
        buf, load_sems, send_sems, recv_sems = refs[2 * n:]
        x, y, c = _position()
        sibling = (x, y, 1 - c)

        def load(k):
            i, q, hf = chunks[k]
            r = half[i]
            return pltpu.make_async_copy(ins[i].at[2 * q + (1 - c), pl.ds(hf * r, r), :],
                                         buf.at[k % nb, pl.ds(0, r), :], load_sems.at[k % nb])

        def send(k):
            i, q, hf = chunks[k]
            r = half[i]
            return pltpu.make_async_remote_copy(
                src_ref=buf.at[k % nb, pl.ds(0, r), :], dst_ref=theirs[i].at[q, pl.ds(hf * r, r), :],
                send_sem=send_sems.at[k % nb], recv_sem=recv_sems.at[i],
                device_id=sibling, device_id_type=MESH)

        for k in range(len(chunks) + 1):
            if k < len(chunks):
                if k >= nb:
                    send(k - nb).wait_send()
                load(k).start()
            if k >= 1:
                load(k - 1).wait()
                send(k - 1).start()
        for k in range(max(0, len(chunks) - nb), len(chunks)):
            send(k).wait_send()
        for i in range(n):
            pltpu.make_async_remote_copy(
                src_ref=theirs[i], dst_ref=theirs[i], send_sem=send_sems.at[0], recv_sem=recv_sems.at[i],
                device_id=sibling, device_id_type=MESH).wait_recv()

    return pl.pallas_call(
        body,
        out_shape=[jax.ShapeDtypeStruct((4,) + g.shape[1:], g.dtype) for g in grads],
        in_specs=_hbm_specs(n),
        out_specs=_hbm_specs(n),
        scratch_shapes=[pltpu.VMEM((nb, max(half), C), grads[0].dtype), pltpu.SemaphoreType.DMA((nb,)),
                        pltpu.SemaphoreType.DMA((nb,)), pltpu.SemaphoreType.DMA((n,))],
        name=name,
        compiler_params=pltpu.CompilerParams(vmem_limit_bytes=VMEM_LIMIT),
    )(*grads)


_HBM_SPEC = pl.BlockSpec(memory_space=pltpu.HBM)
_SEM_SPEC = pl.BlockSpec(memory_space=pltpu.SEMAPHORE)
_TOKEN_SPEC = pl.BlockSpec(memory_space=pltpu.VMEM)
_DATAFLOW = pltpu.SideEffectType.DATAFLOW_SIDE_EFFECTING


def _split_start_many(name, exchanges):
    n = len(exchanges)

    def full_body(*refs):
        srcs, lands = refs[:n], refs[n:2 * n]
        sems = refs[2 * n:4 * n]
        token = refs[-1]
        for i, (body, _, _) in enumerate(exchanges):
            body(srcs[i], lands[i], sems[2 * i], sems[2 * i + 1])
        token[...] = jnp.zeros_like(token)

    srcs = [pltpu.with_memory_space_constraint(src, pltpu.HBM) for _, src, _ in exchanges]
    lands = [pltpu.with_memory_space_constraint(lax.empty(shape, src.dtype), pltpu.HBM)
             for _, src, shape in exchanges]
    res = pl.pallas_call(
        full_body, name=name,
        out_shape=(pltpu.SemaphoreType.DMA(()),) * (2 * n)
        + tuple(pltpu.HBM(a.shape, a.dtype) for a in srcs + lands) + (jax.ShapeDtypeStruct((8, LANES), F32),),
        in_specs=(_HBM_SPEC,) * (2 * n),
        out_specs=(_SEM_SPEC,) * (2 * n) + (_HBM_SPEC,) * (2 * n) + (_TOKEN_SPEC,),
        input_output_aliases={i: 2 * n + i for i in range(2 * n)},
        compiler_params=pltpu.CompilerParams(has_side_effects=_DATAFLOW),
    )(*srcs, *lands)
    return [(res[2 * i], res[2 * i + 1], res[2 * n + i], res[3 * n + i], res[-1]) for i in range(n)]


def _split_start(name, body, src, land_shape):
    return _split_start_many(name, [(body, src, land_shape)])[0]


def _split_wait(name, started, n_blocks, after):
    send_sem, recv_sem, src_thru, land_thru, _ = started
    after = after if isinstance(after, tuple) else (after,)

    def body(src_ref, land_ref, send_sem, recv_sem, *rest):
        x, y, c = _position()
        blocks = land_ref.at[pl.ds(0, n_blocks)]
        copy = pltpu.make_async_remote_copy(src_ref=blocks, dst_ref=blocks, send_sem=send_sem, recv_sem=recv_sem,
                                            device_id=(x, y, c), device_id_type=MESH)
        copy.wait_send()
        copy.wait_recv()

    return pl.pallas_call(
        body, name=name,
        out_shape=(pltpu.HBM(src_thru.shape, src_thru.dtype), pltpu.HBM(land_thru.shape, land_thru.dtype)),
        in_specs=(_HBM_SPEC, _HBM_SPEC, _SEM_SPEC, _SEM_SPEC) + (pl.BlockSpec(memory_space=pl.ANY),) * len(after),
        out_specs=(_HBM_SPEC, _HBM_SPEC),
        input_output_aliases={0: 0, 1: 1},
        compiler_params=pltpu.CompilerParams(has_side_effects=_DATAFLOW),
    )(src_thru, land_thru, send_sem, recv_sem, *after)


def all_gather_start_all(blocks, name):
    def starter(direct):
        def body(b_ref, land_ref, send_sem, recv_sem):
            x, y, c = _position()
            peers = _other_devices(x, y, c) if direct else [(x, y, 1 - c), (1 - x, y, c), (x, 1 - y, c),
                                                            (1 - x, 1 - y, c)]
            for peer in peers:
                pltpu.make_async_remote_copy(src_ref=b_ref, dst_ref=land_ref.at[4 * x + 2 * y + c],
                                             send_sem=send_sem, recv_sem=recv_sem,
                                             device_id=peer, device_id_type=MESH).start()
        return body

    return _split_start_many(name, [(starter(direct), block, (N_DEV,) + block.shape) for block, direct in blocks])


def all_gather_finish(block, land, name):
    R, C = block.shape

    def body(b_ref, land_in, land_ref, stage, load_sems, send_sems, recv_sems, own_sem):
        x, y, c = _position()
        sibling = (x, y, 1 - c)
        chips = [(1 - x, y), (x, 1 - y), (1 - x, 1 - y)]
        own_in = pltpu.make_async_copy(b_ref, stage.at[3], load_sems.at[3])
        own_in.start()
        loads = [pltpu.make_async_copy(land_in.at[4 * px + 2 * py + c], stage.at[j], load_sems.at[j])
                 for j, (px, py) in enumerate(chips)]
        for ld in loads:
            ld.start()
        sends = []
        for j, (px, py) in enumerate(chips):
            loads[j].wait()
            dst = land_ref.at[4 * px + 2 * py + c]
            cp = pltpu.make_async_remote_copy(src_ref=stage.at[j], dst_ref=dst, send_sem=send_sems.at[j],
                                              recv_sem=recv_sems.at[j], device_id=sibling, device_id_type=MESH)
            cp.start()
            sends.append(cp)
        own_in.wait()
        own_out = pltpu.make_async_copy(stage.at[3], land_ref.at[4 * x + 2 * y + c], own_sem)
        own_out.start()
        for j, (px, py) in enumerate(chips):
            dst = land_ref.at[4 * px + 2 * py + (1 - c)]
            pltpu.make_async_remote_copy(src_ref=stage.at[j], dst_ref=dst, send_sem=send_sems.at[j],
                                         recv_sem=recv_sems.at[j], device_id=sibling,
                                         device_id_type=MESH).wait_recv()
        for cp in sends:
            cp.wait_send()
        own_out.wait()

    return pl.pallas_call(
        body,
        out_shape=jax.ShapeDtypeStruct(land.shape, land.dtype),
        in_specs=_hbm_specs(2),
        out_specs=pl.BlockSpec(memory_space=pl.ANY),
        scratch_shapes=[pltpu.VMEM((4, R, C), block.dtype), pltpu.SemaphoreType.DMA((4,)),
                        pltpu.SemaphoreType.DMA((3,)), pltpu.SemaphoreType.DMA((3,)), pltpu.SemaphoreType.DMA],
        input_output_aliases={1: 0},
        name=name,
        compiler_params=pltpu.CompilerParams(vmem_limit_bytes=VMEM_LIMIT),
    )(block, land)


def reduce_scatter_start(parts, name):
    def body(p_ref, land_ref, send_sem, recv_sem):
        x, y, c = _position()
        for px, py in [(1 - x, y), (x, 1 - y), (1 - x, 1 - y)]:
            pltpu.make_async_remote_copy(src_ref=p_ref.at[2 * px + py], dst_ref=land_ref.at[2 * x + y],
                                         send_sem=send_sem, recv_sem=recv_sem,
                                         device_id=(px, py, c), device_id_type=MESH).start()

    return _split_start(name, body, parts, parts.shape)


def _other_devices(x, y, c):
    return [(1 - x if k & 4 else x, 1 - y if k & 2 else y, 1 - c if k & 1 else c) for k in range(1, N_DEV)]


def all_gather_place_own(block, land, name):
    R, C = block.shape

    def body(b_ref, land_in, land_ref, stage, sems):
        x, y, c = _position()
        load = pltpu.make_async_copy(b_ref, stage, sems.at[0])
        load.start()
        load.wait()
        store = pltpu.make_async_copy(stage, land_ref.at[4 * x + 2 * y + c], sems.at[1])
        store.start()
        store.wait()

    return pl.pallas_call(
        body,
        out_shape=jax.ShapeDtypeStruct(land.shape, land.dtype),
        in_specs=_hbm_specs(2),
        out_specs=pl.BlockSpec(memory_space=pl.ANY),
        scratch_shapes=[pltpu.VMEM((R, C), block.dtype), pltpu.SemaphoreType.DMA((2,))],
        input_output_aliases={1: 0},
        name=name,
    )(block, land)


def reduce_scatter_start_direct(grads, name):
    def body(g_ref, land_ref, send_sem, recv_sem):
        x, y, c = _position()
        for px, py, pc in _other_devices(x, y, c):
            pltpu.make_async_remote_copy(src_ref=g_ref.at[4 * px + 2 * py + pc],
                                         dst_ref=land_ref.at[4 * x + 2 * y + c],
                                         send_sem=send_sem, recv_sem=recv_sem,
                                         device_id=(px, py, pc), device_id_type=MESH).start()

    return _split_start(name, body, grads, grads.shape)


def small_all_gather(small):
    def body(small_ref, smalls, s_send, s_recv, s_local):
        x, y, c = _position()
        me = 4 * x + 2 * y + c
        lc = pltpu.make_async_copy(small_ref, smalls.at[me], s_local)
        lc.start()
        remote = []
        k = 0
        for dx in (0, 1):
            for dy in (0, 1):
                for dc in (0, 1):
                    if dx + dy + dc == 0:
                        continue
                    peer = (1 - x if dx else x, 1 - y if dy else y, 1 - c if dc else c)
                    rc = pltpu.make_async_remote_copy(
                        src_ref=small_ref, dst_ref=smalls.at[me],
                        send_sem=s_send.at[k], recv_sem=s_recv.at[k],
                        device_id=peer, device_id_type=MESH)
                    rc.start()
                    remote.append(rc)
                    k += 1
        for rc in remote:
            rc.wait()
        lc.wait()

    return pl.pallas_call(
        body,
        out_shape=jax.ShapeDtypeStruct((N_DEV,) + small.shape, small.dtype),
        in_specs=_hbm_specs(1),
        out_specs=pl.BlockSpec(memory_space=pl.ANY),
        scratch_shapes=[pltpu.SemaphoreType.DMA((7,)), pltpu.SemaphoreType.DMA((7,)), pltpu.SemaphoreType.DMA],
        name="small_all_gather",
    )(small)


def pair_add(grads, theirs, core, name):
    _, R, C = theirs.shape
    tr = R // 2

    def body(c_ref, a_ref, b_ref, o_ref):
        o_ref[...] = (a_ref[...].astype(F32) + b_ref[...].astype(F32)).astype(BF16)

    return pl.pallas_call(
        body,
        out_shape=jax.ShapeDtypeStruct(theirs.shape, BF16),
        grid_spec=pltpu.PrefetchScalarGridSpec(
            num_scalar_prefetch=1, grid=(4, R // tr),
            in_specs=[pl.BlockSpec((None, tr, C), lambda q, i, c: (2 * q + c[0], i, 0)),
                      pl.BlockSpec((None, tr, C), lambda q, i, c: (q, i, 0))],
            out_specs=pl.BlockSpec((None, tr, C), lambda q, i, c: (q, i, 0))),
        name=name,
        compiler_params=pltpu.CompilerParams(dimension_semantics=("parallel", "parallel"),
                                             vmem_limit_bytes=VMEM_LIMIT),
    )(core, grads, theirs)


def sum_slots(recv, off, rows, blk, name):
    nq, _, C = recv.shape
    ob = off // blk

    def body(r_ref, o_ref):
        acc = r_ref[0].astype(F32)
        for q in range(1, nq):
            acc = acc + r_ref[q].astype(F32)
        o_ref[...] = acc

    return _call(name, body, (rows // blk,),
                 [(recv, (nq, blk, C), lambda i: (0, ob + i, 0))],
                 [((rows, C), F32, (blk, C), lambda i: (i, 0))], sem=("parallel",))[0]


def _sum_terms(refs):
    acc = refs[0][...].astype(F32)
    for r in refs[1:]:
        acc = acc + r[...].astype(F32)
    return acc


def sum_landed(own, land, me, off, rows, blk, name):
    n, _, C = land.shape
    ob = off // blk

    def body(c_ref, *refs):
        refs[n][...] = _sum_terms(refs[:n])

    def entry(flip):
        return pl.BlockSpec((None, blk, C), lambda i, c: (c[0] ^ flip, ob + i, 0))

    return pl.pallas_call(
        body,
        out_shape=jax.ShapeDtypeStruct((rows, C), F32),
        grid_spec=pltpu.PrefetchScalarGridSpec(
            num_scalar_prefetch=1, grid=(rows // blk,),
            in_specs=[entry(k) for k in range(n)],
            out_specs=pl.BlockSpec((blk, C), lambda i, c: (i, 0))),
        name=name,
        compiler_params=pltpu.CompilerParams(dimension_semantics=("parallel",), vmem_limit_bytes=VMEM_LIMIT),
    )(me, own, *([land] * (n - 1)))


def _adamw_update(wv, gv, mv, vv):
    nm = ADAM_B1 * mv + (1.0 - ADAM_B1) * gv
    nv = ADAM_B2 * vv + (1.0 - ADAM_B2) * (gv * gv)
    c1 = 1.0 / (1.0 - ADAM_B1 ** ADAM_STEP)
    c2 = 1.0 / (1.0 - ADAM_B2 ** ADAM_STEP)
    return -ADAM_LR * ((nm * c1) / (jnp.sqrt(nv * c2) + ADAM_EPS) + ADAM_WD * wv), nm, nv


def sum_adamw(own, land, me, off, blk, w, m, v, name):
    rows, C = w.shape
    n = land.shape[0]
    ob = off // blk

    def body(c_ref, *refs):
        w_ref, m_ref, v_ref, g_out, d_out, m_out, v_out = refs[n:]
        gv = _sum_terms(refs[:n])
        g_out[...] = gv
        d_out[...], m_out[...], v_out[...] = _adamw_update(w_ref[...], gv, m_ref[...], v_ref[...])

    def entry(flip):
        return pl.BlockSpec((None, blk, C), lambda i, c: (c[0] ^ flip, ob + i, 0))

    plain = pl.BlockSpec((blk, C), lambda i, c: (i, 0))
    return pl.pallas_call(
        body,
        out_shape=[jax.ShapeDtypeStruct((rows, C), F32)] * 4,
        grid_spec=pltpu.PrefetchScalarGridSpec(
            num_scalar_prefetch=1, grid=(rows // blk,),
            in_specs=[entry(k) for k in range(n)] + [plain, plain, plain],
            out_specs=[plain] * 4),
        name=name,
        compiler_params=pltpu.CompilerParams(dimension_semantics=("parallel",), vmem_limit_bytes=VMEM_LIMIT),
    )(me, own, *([land] * (n - 1)), w, m, v)


def adamw(w, g, m, v, name):
    R, C = w.shape
    tr = R
    for cand in (256, 128, 64, 32, 16, 8):
        if R % cand == 0 and R > cand:
            tr = cand
            break

    def body(w_ref, g_ref, m_ref, v_ref, d_ref, nm_ref, nv_ref):
        d_ref[...], nm_ref[...], nv_ref[...] = _adamw_update(w_ref[...], g_ref[...], m_ref[...], v_ref[...])

    spec = ((tr, C), lambda i: (i, 0))
    out = ((R, C), F32) + spec
    return _call(name, body, (R // tr,), [(w,) + spec, (g,) + spec, (m,) + spec, (v,) + spec],
                 [out, out, out], sem=("parallel",))


def rms_fwd(x, g, name):
    S, D = x.shape
    tr = 512

    def body(x_ref, g_ref, o_ref):
        xv = x_ref[...]
        r = lax.rsqrt(jnp.mean(xv * xv, axis=-1, keepdims=True) + EPS)
        o_ref[...] = (xv * r * g_ref[...]).astype(BF16)

    return _call(name, body, (S // tr,),
                 [(x, (tr, D), lambda i: (i, 0)), (g, (1, D), lambda i: (0, 0))],
                 [((S, D), BF16, (tr, D), lambda i: (i, 0))], sem=("parallel",))[0]


def _rms_bwd_tile(dn, xv, gv):
    r = lax.rsqrt(jnp.mean(xv * xv, axis=-1, keepdims=True) + EPS)
    xh = xv * r
    dxh = dn * gv
    dx = r * (dxh - xh * jnp.mean(dxh * xh, axis=-1, keepdims=True))
    return dx, dn * xh


def final_loss(x, tgt, g, name):
    S, D = x.shape
    tr = 256

    def body(x_ref, t_ref, g_ref, l_ref, dx_ref, dxb_ref, dg_ref):
        i = pl.program_id(0)
        xv, gv = x_ref[...], g_ref[...]
        r = lax.rsqrt(jnp.mean(xv * xv, axis=-1, keepdims=True) + EPS)
        xh = xv * r
        e = xh * gv - t_ref[...]
        part = 0.5 * jnp.sum(jnp.sum(e * e, axis=-1, keepdims=True) * (1.0 / D), axis=0, keepdims=True)
        dy = e * (1.0 / D)
        dxh = dy * gv
        dx = r * (dxh - xh * jnp.mean(dxh * xh, axis=-1, keepdims=True))
        dx_ref[...] = dx
        dxb_ref[...] = dx.astype(BF16)
        dgp = jnp.sum(dy * xh, axis=0, keepdims=True)

        @pl.when(i == 0)
        def _():
            l_ref[...] = jnp.broadcast_to(part, l_ref.shape)
            dg_ref[...] = dgp

        @pl.when(i > 0)
        def _():
            l_ref[...] += jnp.broadcast_to(part, l_ref.shape)
            dg_ref[...] += dgp

    row = ((tr, D), lambda i: (i, 0))
    return _call(name, body, (S // tr,),
                 [(x,) + row, (tgt,) + row, (g, (1, D), lambda i: (0, 0))],
                 [((1, LANES), F32, (1, LANES), lambda i: (0, 0)), ((S, D), F32) + row, ((S, D), BF16) + row,
                  ((1, D), F32, (1, D), lambda i: (0, 0))], sem=("arbitrary",))


FFN_TF = 4 * FFN_SHARD


def _ffn_pick(G, which):
    if isinstance(G, tuple):
        return (G[0], which) if which < 2 else (G[1], 0)
    return G, which


def _ffn_w_spec(G, which, imap):
    arr, blk = _ffn_pick(G, which)
    return (arr, (4, FFN_SHARD, arr.shape[2]), lambda *idx: (imap(*idx), blk, 0))


def _ffn_whole_w_spec(G, which):
    arr, blk = _ffn_pick(G, which)
    return (arr, (N_DEV, FFN_SHARD, arr.shape[2]), lambda *idx: (0, blk, 0), pl.Buffered(1))


def ffn_up(n, G, name):
    S, D = n.shape
    F = N_DEV * FFN_SHARD
    tm = 256

    def body(n_ref, w1_ref, w3_ref, abh_ref):
        nv = n_ref[...]
        a = _dot(nv, w1_ref[...].reshape(F, D), 1, 1).astype(BF16)
        b = _dot(nv, w3_ref[...].reshape(F, D), 1, 1).astype(BF16)
        abh_ref[0] = a
        abh_ref[1] = b
        av, bv = a.astype(F32), b.astype(F32)
        abh_ref[2] = (av * _sigmoid(av) * bv).astype(BF16)

    return _call(name, body, (S // tm,),
                 [(n, (tm, D), lambda i: (i, 0)),
                  _ffn_whole_w_spec(G, 0), _ffn_whole_w_spec(G, 1)],
                 [((3, S, F), BF16, (3, tm, F), lambda i: (0, i, 0))],
                 sem=("parallel",))[0]


def ffn_down(abh, G, x, name):
    _, S, F = abh.shape
    D = x.shape[1]
    tm = 512

    def body(h_ref, w2_ref, x_ref, o_ref):
        o_ref[...] = x_ref[...] + 0.5 * _dot(h_ref[...], w2_ref[...].reshape(F, D))

    return _call(name, body, (S // tm,),
                 [(abh, (None, tm, F), lambda i: (2, i, 0)), _ffn_whole_w_spec(G, 2),
                  (x, (tm, D), lambda i: (i, 0))],
                 [((S, D), F32, (tm, D), lambda i: (i, 0))], sem=("parallel",))[0]


def ffn_fwd(n, G, x, name):
    S, D = x.shape
    F = N_DEV * FFN_SHARD
    tm = 256

    def body(n_ref, w1_ref, w3_ref, w2_ref, x_ref, abh_ref, o_ref):
        nv = n_ref[...]
        a = _dot(nv, w1_ref[...].reshape(F, D), 1, 1).astype(BF16)
        b = _dot(nv, w3_ref[...].reshape(F, D), 1, 1).astype(BF16)
        av, bv = a.astype(F32), b.astype(F32)
        h = (av * _sigmoid(av) * bv).astype(BF16)
        abh_ref[0] = a
        abh_ref[1] = b
        abh_ref[2] = h
        o_ref[...] = x_ref[...] + 0.5 * _dot(h, w2_ref[...].reshape(F, D))

    tile = ((tm, D), lambda i: (i, 0))
    return _call(name, body, (S // tm,),
                 [(n,) + tile, _ffn_whole_w_spec(G, 0), _ffn_whole_w_spec(G, 1), _ffn_whole_w_spec(G, 2),
                  (x,) + tile],
                 [((3, S, F), BF16, (3, tm, F), lambda i: (0, i, 0)), ((S, D), F32) + tile],
                 sem=("parallel",))


def _ffn_hidden_grads(dh, av, bv):
    sig = _sigmoid(av)
    return dh * bv * (sig * (1.0 + av * (1.0 - sig))), dh * (av * sig)


def ffn_bwd_hidden(dxo, abh, G, name):
    _, S, F = abh.shape
    D = dxo.shape[1]
    tm = 256

    def body(d_ref, w2_ref, ab_ref, o_ref):
        dh = 0.5 * _dot(d_ref[...].astype(BF16), w2_ref[...].reshape(F, D), 1, 1)
        da, db = _ffn_hidden_grads(dh, ab_ref[0].astype(F32), ab_ref[1].astype(F32))
        o_ref[0] = da.astype(BF16)
        o_ref[1] = db.astype(BF16)

    return _call(name + "_down_bwd", body, (S // tm,),
                 [(dxo, (tm, D), lambda i: (i, 0)), _ffn_whole_w_spec(G, 2),
                  (abh, (2, tm, F), lambda i: (0, i, 0))],
                 [((2, S, F), BF16, (2, tm, F), lambda i: (0, i, 0))],
                 sem=("parallel",))[0]


def ffn_bwd_hidden_input(dxo_b, dxo, abh, G, x_in, g, name):
    _, S, F = abh.shape
    D = x_in.shape[1]
    tm = 256

    def body(db_ref, w2_ref, w1_ref, w3_ref, ab_ref, x_ref, d_ref, g_ref, dab_ref, dx_ref, dxb_ref, dg_ref):
        i = pl.program_id(0)
        dh = 0.5 * _dot(db_ref[...], w2_ref[...].reshape(F, D), 1, 1)
        da, db = _ffn_hidden_grads(dh, ab_ref[0].astype(F32), ab_ref[1].astype(F32))
        da, db = da.astype(BF16), db.astype(BF16)
        dab_ref[0] = da
        dab_ref[1] = db
        dn = _dot(da, w1_ref[...].reshape(F, D)) + _dot(db, w3_ref[...].reshape(F, D))
        dx, dgt = _rms_bwd_tile(dn, x_ref[...], g_ref[...])
        dx = d_ref[...] + dx
        dx_ref[...] = dx
        dxb_ref[...] = dx.astype(BF16)
        dgp = jnp.sum(dgt, axis=0, keepdims=True)

        @pl.when(i == 0)
        def _():
            dg_ref[...] = dgp

        @pl.when(i > 0)
        def _():
            dg_ref[...] += dgp

    tile = ((tm, D), lambda i: (i, 0))
    wide = ((2, tm, F), lambda i: (0, i, 0))
    return _call(name + "_hidden_input", body, (S // tm,),
                 [(dxo_b,) + tile, _ffn_whole_w_spec(G, 2), _ffn_whole_w_spec(G, 0), _ffn_whole_w_spec(G, 1),
                  (abh,) + wide, (x_in,) + tile, (dxo,) + tile, (g, (1, D), lambda i: (0, 0))],
                 [((2, S, F), BF16) + wide, ((S, D), F32) + tile, ((S, D), BF16) + tile,
                  ((1, D), F32, (1, D), lambda i: (0, 0))],
                 sem=("arbitrary",))


def ffn_bwd_weights(dxo, abh, dab, n, name):
    _, S, F = abh.shape
    D = dxo.shape[1]
    nf = F // FFN_TF
    tk = WGRAD_TK
    nk = S // tk
    gshape = (N_DEV, 3 * FFN_SHARD, D)

    def dw2_body(h_ref, d_ref, o_ref, acc_ref):
        k = pl.program_id(1)
        p = _dot(h_ref[...], d_ref[...].astype(BF16), 0, 0)

        @pl.when(k == 0)
        def _():
            acc_ref[...] = p

        @pl.when(k > 0)
        def _():
            acc_ref[...] += p

        @pl.when(k == nk - 1)
        def _():
            o_ref[...] = (0.5 * acc_ref[...]).astype(BF16).reshape(4, FFN_SHARD, D)

    gw = _call(name + "_dw2", dw2_body, (nf, nk),
               [(abh, (None, tk, FFN_TF), lambda j, k: (2, k, j)), (dxo, (tk, D), lambda j, k: (k, 0))],
               [(gshape, BF16, (4, FFN_SHARD, D), lambda j, k: (j, 2, 0))],
               scratch=[pltpu.VMEM((FFN_TF, D), F32)], sem=("parallel", "arbitrary"))[0]

    def dw13_body(gw_ref, dab_ref, n_ref, o_ref):
        o_ref[...] = _dot(dab_ref[...], n_ref[...], 0, 0).astype(BF16).reshape(4, FFN_SHARD, D)

    gw = pl.pallas_call(
        dw13_body,
        out_shape=jax.ShapeDtypeStruct(gshape, BF16),
        grid=(2, nf),
        in_specs=[pl.BlockSpec(memory_space=pl.ANY),
                  pl.BlockSpec((None, S, FFN_TF), lambda w, j: (w, 0, j)),
                  pl.BlockSpec((S, D), lambda w, j: (0, 0))],
        out_specs=pl.BlockSpec((4, FFN_SHARD, D), lambda w, j: (j, w, 0)),
        input_output_aliases={0: 0},
        name=name + "_dw13",
        compiler_params=pltpu.CompilerParams(dimension_semantics=("parallel", "parallel"),
                                             vmem_limit_bytes=VMEM_LIMIT),
    )(gw, dab, n)
    return gw


def ffn_bwd_input(dab, G, x_in, g, dxo, name, as_operand=True):
    _, S, F = dab.shape
    D = x_in.shape[1]
    tm = 256

    def dn_body(dab_ref, w1_ref, w3_ref, x_ref, d_ref, g_ref, dx_ref, *rest):
        dg_ref = rest[-1]
        i = pl.program_id(0)
        dn = _dot(dab_ref[0], w1_ref[...].reshape(F, D)) + _dot(dab_ref[1], w3_ref[...].reshape(F, D))
        dx, dgt = _rms_bwd_tile(dn, x_ref[...], g_ref[...])
        dx = d_ref[...] + dx
        dx_ref[...] = dx
        if as_operand:
            rest[0][...] = dx.astype(BF16)
        dgp = jnp.sum(dgt, axis=0, keepdims=True)

        @pl.when(i == 0)
        def _():
            dg_ref[...] = dgp

        @pl.when(i > 0)
        def _():
            dg_ref[...] += dgp

    tile = ((tm, D), lambda i: (i, 0))
    return _call(name + "_dn", dn_body, (S // tm,),
                 [(dab, (2, tm, F), lambda i: (0, i, 0)),
                  _ffn_whole_w_spec(G, 0), _ffn_whole_w_spec(G, 1),
                  (x_in,) + tile, (dxo,) + tile, (g, (1, D), lambda i: (0, 0))],
                 [((S, D), F32) + tile] + ([((S, D), BF16) + tile] if as_operand else [])
                 + [((1, D), F32, (1, D), lambda i: (0, 0))],
                 sem=("arbitrary",))


PROJ_TN = 512
DH_SHARDS_PER_STEP = 4


def in_proj(h, Gm, first_tile, n_tiles, dtype, name, tile_stride=1):
    S, D = h.shape
    tile = lambda j: first_tile + tile_stride * j

    def body(h_ref, w_ref, o_ref):
        o_ref[...] = _dot(h_ref[...], w_ref[...]).astype(dtype)

    return _call(name, body, (n_tiles,),
                 [(h, (S, D), lambda j: (0, 0)),
                  (Gm, (None, D, PROJ_TN), lambda j: (tile(j) // 2, 0, tile(j) % 2))],
                 [((S, n_tiles * PROJ_TN), dtype, (S, PROJ_TN), lambda j: (0, j))],
                 sem=("parallel",))[0]


def _dproj_pieces(dqkv, dq_b, dkv_b, dgate):
    pieces = [(dqkv[g], [(3 * which + g, (which, 0)) for which in range(3)]) for g in range(3)]
    pieces.append((dq_b, [(9, (None, 0)), (10, (None, 1))]))
    pieces.append((dkv_b, [(11, (None, 0))]))
    pieces.append((dgate, [(12 + 2 * a + b, (a, b)) for a in range(2) for b in range(2)]))
    return pieces


def in_proj_bwd_dw(pieces, h, gm_grads, name):
    S, D = h.shape

    for n_piece, (arr, tiles) in enumerate(pieces):
        w_tile = [t for t, _ in tiles]
        lead = [ix[0] for _, ix in tiles]
        colb = [ix[1] for _, ix in tiles]

        def pick(table, j):
            out = table[-1]
            for k in range(len(table) - 2, -1, -1):
                out = jnp.where(j == k, table[k], out)
            return out

        def dw_body(gm_ref, h_ref, d_ref, o_ref):
            o_ref[...] = _dot(h_ref[...], d_ref[...], 0, 0).astype(BF16)

        if arr.ndim == 3:
            d_spec = pl.BlockSpec((None, S, PROJ_TN), lambda j, lead=lead, colb=colb: (pick(lead, j), 0, pick(colb, j)))
        else:
            d_spec = pl.BlockSpec((S, PROJ_TN), lambda j, colb=colb: (0, pick(colb, j)))
        gm_grads = pl.pallas_call(
            dw_body,
            out_shape=jax.ShapeDtypeStruct(gm_grads.shape, BF16),
            grid=(len(tiles),),
            in_specs=[pl.BlockSpec(memory_space=pl.ANY), pl.BlockSpec((S, D), lambda j: (0, 0)), d_spec],
            out_specs=pl.BlockSpec((None, D, PROJ_TN),
                                   lambda j, w_tile=w_tile: (pick(w_tile, j) // 2, 0, pick(w_tile, j) % 2)),
            input_output_aliases={0: 0},
            name="%s_dw%d" % (name, n_piece),
            compiler_params=pltpu.CompilerParams(dimension_semantics=("parallel",), vmem_limit_bytes=VMEM_LIMIT),
        )(gm_grads, h, arr)
    return gm_grads


def in_proj_bwd_dh(pieces, Gm, x_in, g, dres, name):
    S, D = x_in.shape
    tm = 256
    C = Gm.shape[2]
    n_sh = N_DEV
    n_p = len(pieces)

    def dh_body(*refs):
        d_refs = refs[:n_p]
        w_ref, x_ref, r_ref, g_ref, dx_ref, dxb_ref, dg_ref = refs[n_p:]
        i = pl.program_id(0)
        p = None
        for d_ref, (arr, tiles) in zip(d_refs, pieces):
            for t, (lead, colb) in tiles:
                cols = slice(colb * PROJ_TN, (colb + 1) * PROJ_TN)
                d = d_ref[:, cols] if lead is None else d_ref[lead, :, cols]
                wcol = (t % 2) * PROJ_TN
                term = _dot(d, w_ref[t // 2, :, wcol:wcol + PROJ_TN], 1, 1)
                p = term if p is None else p + term
        dx, dgt = _rms_bwd_tile(p, x_ref[...], g_ref[...])
        dx = r_ref[...] + dx
        dx_ref[...] = dx
        dxb_ref[...] = dx.astype(BF16)
        dgp = jnp.sum(dgt, axis=0, keepdims=True)

        @pl.when(i == 0)
        def _():
            dg_ref[...] = dgp

        @pl.when(i > 0)
        def _():
            dg_ref[...] += dgp

    tile = ((tm, D), lambda i: (i, 0))

    def rows_of(arr):
        if arr.ndim == 3:
            return (arr, (arr.shape[0], tm, arr.shape[2]), lambda i: (0, i, 0))
        return (arr, (tm, arr.shape[1]), lambda i: (i, 0))

    return _call(name + "_dh", dh_body, (S // tm,),
                 [rows_of(arr) for arr, _ in pieces]
                 + [(Gm, (n_sh, D, C), lambda i: (0, 0, 0), pl.Buffered(1)),
                    (x_in,) + tile, (dres,) + tile, (g, (1, D), lambda i: (0, 0))],
                 [((S, D), F32) + tile, ((S, D), BF16) + tile, ((1, D), F32, (1, D), lambda i: (0, 0))],
                 sem=("arbitrary",))


def _t5_bucket(rel):
    n = N_BUCKETS // 2
    max_exact = n // 2
    ret = jnp.where(rel > 0, n, 0)
    a = jnp.abs(rel)
    af = jnp.maximum(a, 1).astype(F32)
    large = max_exact + (jnp.log(af / max_exact) / math.log(MAX_DISTANCE / max_exact)
                         * (n - max_exact)).astype(jnp.int32)
    large = jnp.minimum(large, n - 1)
    return ret + jnp.where(a < max_exact, a, large)


def _bucket_tables():
    qi = jnp.arange(A_TQ, dtype=jnp.int32)[:, None]
    kj = jnp.arange(A_WIN, dtype=jnp.int32)[None, :]
    rel = kj - HALF_WINDOW - qi
    return jnp.stack([_t5_bucket(rel * d) for d in DILATIONS], axis=0)


def bias_build(rel_bias, buckets):
    def body(tab_ref, bk_ref, o_ref):
        col = pl.program_id(0) * HEADS_PER_GROUP_A + pl.program_id(1)
        bk = bk_ref[...]
        acc = jnp.zeros(bk.shape, F32)
        for b in range(N_BUCKETS):
            acc = jnp.where(bk == b, tab_ref[b, col], acc)
        qi = lax.broadcasted_iota(jnp.int32, bk.shape, 0)
        kj = lax.broadcasted_iota(jnp.int32, bk.shape, 1)
        band = jnp.where(jnp.abs(kj - HALF_WINDOW - qi) <= HALF_WINDOW, acc, NEG_INF)
        o_ref[0] = jnp.where(kj >= HALF_WINDOW, band, NEG_INF)
        o_ref[1] = band
        o_ref[2] = jnp.where(kj < A_TQ + HALF_WINDOW, band, NEG_INF)

    out = pl.pallas_call(
        body,
        out_shape=jax.ShapeDtypeStruct((3, HEADS_PER_GROUP_A // 2, 3, 2, A_TQ, A_WIN), F32),
        grid=(3, HEADS_PER_GROUP_A),
        in_specs=[pl.BlockSpec(memory_space=pltpu.SMEM),
                  pl.BlockSpec((None, A_TQ, A_WIN), lambda g, h: (g, 0, 0))],
        out_specs=pl.BlockSpec((None, None, 3, None, A_TQ, A_WIN), lambda g, h: (g, h // 2, 0, h % 2, 0, 0)),
        name="a_bias_build",
        compiler_params=pltpu.CompilerParams(dimension_semantics=("parallel", "parallel")),
    )(rel_bias, buckets)
    return out.reshape(3, HEADS_PER_GROUP_A // 2, 3, 2 * A_TQ, A_WIN)


def bias_bwd(dbias, buckets):
    def body(d_ref, bk_ref, o_ref):
        bk = bk_ref[...]
        dv = d_ref[...]
        for b in range(N_BUCKETS):
            part = jnp.sum(jnp.where(bk == b, dv, 0.0), axis=1, keepdims=True)
            o_ref[b:b + 1, :] = jnp.broadcast_to(jnp.sum(part, axis=0, keepdims=True), (1, LANES))

    out = pl.pallas_call(
        body,
        out_shape=jax.ShapeDtypeStruct((3, HEADS_PER_GROUP_A, N_BUCKETS, LANES), F32),
        grid=(3, HEADS_PER_GROUP_A),
        in_specs=[pl.BlockSpec((None, None, A_TQ, A_WIN), lambda g, h: (g, h, 0, 0)),
                  pl.BlockSpec((None, A_TQ, A_WIN), lambda g, h: (g, 0, 0))],
        out_specs=pl.BlockSpec((None, None, N_BUCKETS, LANES), lambda g, h: (g, h, 0, 0)),
        name="a_bias_bwd",
        compiler_params=pltpu.CompilerParams(dimension_semantics=("parallel", "parallel")),
    )(dbias, buckets)
    return out[:, :, :, 0].transpose(2, 0, 1).reshape(N_BUCKETS, 3 * HEADS_PER_GROUP_A)


def _a_fill_padded(pad_ref, src_ref, n, pad):
    zeros = jnp.zeros((pad, LANES), pad_ref.dtype)
    pad_ref[0:pad, :] = zeros
    pad_ref[pad + n:2 * pad + n, :] = zeros
    pad_ref[pad:pad + n, :] = src_ref[...].astype(pad_ref.dtype)


def _a_stack_heads(x, lane):
    zero = jnp.zeros_like(x)
    return jnp.concatenate([jnp.where(lane < HEAD_DIM_A, x, zero), jnp.where(lane >= HEAD_DIM_A, x, zero)], axis=0)


def _a_bias_variant(qb, nqb):
    return jnp.where(qb == 0, 0, jnp.where(qb == nqb - 1, 2, 1))


def a_fwd(proj_g, bias_g, g, name):
    S = proj_g.shape[0]
    d = DILATIONS[g]
    L = S // d
    nqb = L // A_TQ
    pad = HALF_WINDOW * d

    def body(q_ref, k_ref, v_ref, b_ref, o_ref, l_ref, qf, kpad, vpad):
        qf[...] = q_ref[...].astype(F32) * A_SCALE
        _a_fill_padded(kpad, k_ref, S, pad)
        _a_fill_padded(vpad, v_ref, S, pad)
        lane = lax.broadcasted_iota(jnp.int32, (A_TQ, LANES), 1)

        def block(t, carry):
            qb, r = t // d, t % d
            start = qb * (A_TQ * d) + r
            kw = kpad[pl.ds(start, A_WIN, stride=d), :].astype(BF16)
            vw = vpad[pl.ds(start, A_WIN, stride=d), :].astype(BF16)
            q = qf[pl.ds(start, A_TQ, stride=d), :].astype(BF16)
            q2 = _a_stack_heads(q, lane)
            s = _dot(q2, kw, 1, 1) + b_ref[_a_bias_variant(qb, nqb)]
            m = jnp.max(s, axis=-1, keepdims=True)
            e = jnp.exp(s - m)
            l = jnp.sum(e, axis=-1, keepdims=True)
            o2 = _dot(e.astype(BF16), vw) / l
            lse2 = m + jnp.log(l)
            o_ref[pl.ds(start, A_TQ, stride=d), :] = jnp.where(lane < HEAD_DIM_A, o2[0:A_TQ], o2[A_TQ:])
            l_ref[pl.ds(start, A_TQ, stride=d), :] = jnp.where(lane < HEAD_DIM_A, lse2[0:A_TQ], lse2[A_TQ:])
            return carry

        lax.fori_loop(0, nqb * d, block, 0, unroll=A_UNROLL)

    out_spec = ((S, GROUP_WIDTH_A), F32, (S, LANES), lambda hp: (0, hp))
    return _call(name, body, (4,),
                 [(proj_g, (S, LANES), lambda hp: (0, hp)),
                  (proj_g, (S, LANES), lambda hp: (0, 4 + hp)),
                  (proj_g, (S, LANES), lambda hp: (0, 8 + hp)),
                  (bias_g, (None, 3, 2 * A_TQ, A_WIN), lambda hp: (hp, 0, 0, 0))],
                 [out_spec, out_spec],
                 scratch=[pltpu.VMEM((S, LANES), F32)] + [pltpu.VMEM((S + 2 * pad, LANES), F32)] * 2,
                 sem=("parallel",))


def a_combine(outs, lses, name):
    S, W = outs[0].shape
    tr = 512

    def body(o0, o1, o2, l0, l1, l2, oa_ref, lt_ref):
        a, b, c = l0[...], l1[...], l2[...]
        m = jnp.maximum(jnp.maximum(a, b), c)
        ea, eb, ec = jnp.exp(a - m), jnp.exp(b - m), jnp.exp(c - m)
        z = ea + eb + ec
        oa_ref[...] = ((ea * o0[...] + eb * o1[...] + ec * o2[...]) / z).astype(BF16)
        lt_ref[...] = m + jnp.log(z)

    spec = ((tr, W), lambda i: (i, 0))
    return _call(name, body, (S // tr,), [(a,) + spec for a in (*outs, *lses)],
                 [((S, W), BF16) + spec, ((S, W), F32) + spec], sem=("parallel",))


def a_bwd(proj_g, bias_g, do_a, o_a, lse_tot, g, name):
    S = proj_g.shape[0]
    d = DILATIONS[g]
    L = S // d
    nqb = L // A_TQ
    pad = HALF_WINDOW * d

    def body(q_ref, k_ref, v_ref, b_ref, do_ref, o_ref, l_ref, dqkv_ref, db_ref,
             qf, of, dqf, kpad, vpad, dkacc, dvacc):
        qf[...] = q_ref[...].astype(F32) * A_SCALE
        of[...] = o_ref[...].astype(F32)
        _a_fill_padded(kpad, k_ref, S, pad)
        _a_fill_padded(vpad, v_ref, S, pad)
        dkacc[...] = jnp.zeros(dkacc.shape, F32)
        dvacc[...] = jnp.zeros(dvacc.shape, F32)
        db_ref[...] = jnp.zeros(db_ref.shape, F32)
        lane = lax.broadcasted_iota(jnp.int32, (A_TQ, LANES), 1)

        def block(t, carry):
            qb, r = t // d, t % d
            start = qb * (A_TQ * d) + r
            rows = pl.ds(start, A_TQ, stride=d)
            win = pl.ds(start, A_WIN, stride=d)
            kw = kpad[win, :].astype(BF16)
            vw = vpad[win, :].astype(BF16)
            q = qf[rows, :].astype(BF16)
            do = do_ref[rows, :]
            ov = of[rows, :]
            lt = l_ref[rows, :]
            q2 = _a_stack_heads(q, lane)
            do2 = _a_stack_heads(do, lane)
            lt2 = jnp.concatenate([lt[:, 0:1], lt[:, HEAD_DIM_A:HEAD_DIM_A + 1]], axis=0)
            s = _dot(q2, kw, 1, 1) + b_ref[_a_bias_variant(qb, nqb)]
            p = jnp.exp(s - lt2)
            t = jnp.sum(do2 * jnp.concatenate([ov, ov], axis=0), axis=-1, keepdims=True)
            dob2 = do2.astype(BF16)
            ds = p * (_dot(dob2, vw, 1, 1) - t)
            db_ref[...] += ds
            dsb = ds.astype(BF16)
            dq2 = _dot(dsb, kw)
            dqf[rows, :] = jnp.where(lane < HEAD_DIM_A, dq2[0:A_TQ], dq2[A_TQ:]) * A_SCALE
            dkacc[win, :] += _dot(dsb, q2, 0, 0)
            dvacc[win, :] += _dot(p.astype(BF16), dob2, 0, 0)
            return carry

        lax.fori_loop(0, nqb * d, block, 0, unroll=A_UNROLL)
        dqkv_ref[0] = dqf[...].astype(BF16)
        dqkv_ref[1] = dkacc[pad:pad + S, :].astype(BF16)
        dqkv_ref[2] = dvacc[pad:pad + S, :].astype(BF16)

    slab = ((S, LANES), lambda hp: (0, hp))
    padded = pltpu.VMEM((S + 2 * pad, LANES), F32)
    return _call(
        name, body, (4,),
        [(proj_g, (S, LANES), lambda hp: (0, hp)),
         (proj_g, (S, LANES), lambda hp: (0, 4 + hp)),
         (proj_g, (S, LANES), lambda hp: (0, 8 + hp)),
         (bias_g, (None, 3, 2 * A_TQ, A_WIN), lambda hp: (hp, 0, 0, 0)),
         (do_a,) + slab, (o_a,) + slab, (lse_tot,) + slab],
        [((3, S, GROUP_WIDTH_A), BF16, (3, S, LANES), lambda hp: (0, 0, hp)),
         ((4, 2 * A_TQ, A_WIN), F32, (None, 2 * A_TQ, A_WIN), lambda hp: (hp, 0, 0))],
        scratch=[pltpu.VMEM((S, LANES), F32)] * 3 + [padded] * 4,
        sem=("parallel",))


def _rope_tables(S):
    rows = S // GRID_W
    row = jnp.repeat(jnp.arange(rows, dtype=F32), GRID_W)
    col = jnp.tile(jnp.arange(GRID_W, dtype=F32), rows)
    n_freq = HEAD_DIM_B // 4
    freq = ROPE_THETA ** (-jnp.arange(n_freq, dtype=F32) / n_freq)
    ang = jnp.concatenate([row[:, None] * freq, col[:, None] * freq], axis=-1)
    cos, sin = jnp.cos(ang), jnp.sin(ang)
    return jnp.repeat(cos, 2, axis=-1), jnp.stack([-sin, sin], axis=-1).reshape(S, HEAD_DIM_B)


def _swap_pairs(y):
    lane = lax.broadcasted_iota(jnp.int32, y.shape, 1)
    return jnp.where(lane % 2 == 0, pltpu.roll(y, LANES - 1, 1), pltpu.roll(y, 1, 1))


def qkv_prep(proj_b, gains, cos_t, sin_t, name):
    S = proj_b.shape[0]
    ts = 256
    n_rot = N_HEADS_B + N_KV_B
    nh = n_rot + N_KV_B
    W = nh * LANES

    def body(x_ref, g_ref, c_ref, s_ref, o_ref):
        cv, sv = c_ref[...], s_ref[...]
        for hb in range(nh):
            cols = slice(hb * LANES, (hb + 1) * LANES)
            xv = x_ref[:, cols]
            if hb < n_rot:
                r = lax.rsqrt(jnp.mean(xv * xv, axis=-1, keepdims=True) + EPS)
                yv = xv * r * g_ref[:, cols]
                o_ref[:, cols] = (yv * cv + _swap_pairs(yv) * sv).astype(BF16)
            else:
                o_ref[:, cols] = xv.astype(BF16)

    return _call(name, body, (S // ts,),
                 [(proj_b, (ts, W), lambda i: (i, 0)), (gains, (1, W), lambda i: (0, 0)),
                  (cos_t, (ts, LANES), lambda i: (i, 0)), (sin_t, (ts, LANES), lambda i: (i, 0))],
                 [((S, W), BF16, (ts, W), lambda i: (i, 0))],
                 sem=("parallel",))[0]


def qk_prep_bwd(dr, proj_b, col0, gain, cos_t, sin_t, name):
    S, W = dr.shape
    H = W // LANES
    ts = 256
    xb = (col0 * LANES) // W

    def body(d_ref, x_ref, g_ref, c_ref, s_ref, dx_ref, dg_ref):
        i = pl.program_id(0)
        cv, sv, gv = c_ref[...], s_ref[...], g_ref[...]
        dgp = jnp.zeros((1, LANES), F32)
        for hb in range(H):
            cols = slice(hb * LANES, (hb + 1) * LANES)
            dout = d_ref[:, cols]
            dy = dout * cv + _swap_pairs(dout * sv)
            dx, dgt = _rms_bwd_tile(dy, x_ref[:, cols], gv)
            dx_ref[:, cols] = dx.astype(BF16)
            dgp = dgp + jnp.sum(dgt, axis=0, keepdims=True)

        @pl.when(i == 0)
        def _():
            dg_ref[...] = dgp

        @pl.when(i > 0)
        def _():
            dg_ref[...] += dgp

    return _call(name, body, (S // ts,),
                 [(dr, (ts, W), lambda i: (i, 0)), (proj_b, (ts, W), lambda i: (i, xb)),
                  (gain, (1, LANES), lambda i: (0, 0)),
                  (cos_t, (ts, LANES), lambda i: (i, 0)), (sin_t, (ts, LANES), lambda i: (i, 0))],
                 [((S, W), BF16, (ts, W), lambda i: (i, 0)),
                  ((1, LANES), F32, (1, LANES), lambda i: (0, 0))],
                 sem=("arbitrary",))


def _row_sums(x):
    hi = x.astype(BF16)
    lo = (x - hi.astype(F32)).astype(BF16)
    ones = jnp.ones((8, LANES), BF16)
    return (_dot(ones, hi, 1, 1) + _dot(ones, lo, 1, 1))[0:1, :]


def flash_fwd(qkv, name):
    S = qkv.shape[0]
    tq = B_TQ_FWD
    scale = HEAD_DIM_B ** -0.5

    hps = B_HEADS_PER_STEP

    def body(q_ref, k_ref, v_ref, o_ref, l_ref):
        k, v = k_ref[...], v_ref[...]
        for j in range(hps):
            cols = slice(j * LANES, (j + 1) * LANES)
            s = _dot(q_ref[:, cols], k, 1, 1)
            m = jnp.max(s, axis=-1, keepdims=True)
            e = jnp.exp2((s - m) * (scale * LOG2E))
            l = jnp.sum(e, axis=-1, keepdims=True)
            o_ref[:, cols] = (_dot(e.astype(BF16), v) / l).astype(BF16)
            lse = jnp.broadcast_to(m * scale + jnp.log(l), (tq, LANES))
            l_ref[j] = _row_sums(lse) * (1.0 / LANES)

    per = GQA_GROUP_B // hps
    heads = lambda g, h, i: (i, g * per + h)
    return _call(name, body, (N_KV_B, per, S // tq),
                 [(qkv, (tq, hps * LANES), heads),
                  (qkv, (S, LANES), lambda g, h, i: (0, N_HEADS_B + g)),
                  (qkv, (S, LANES), lambda g, h, i: (0, N_HEADS_B + N_KV_B + g))],
                 [((S, N_HEADS_B * LANES), BF16, (tq, hps * LANES), heads),
                  ((N_HEADS_B, 1, S), F32, (hps, 1, tq), lambda g, h, i: (g * per + h, 0, i))],
                 sem=("parallel", "parallel", "parallel"))


def flash_bwd(qkv, k_t, do_b, o_b, lse, name):
    S = qkv.shape[0]
    tq = B_TQ_BWD
    nq = S // tq
    scale = HEAD_DIM_B ** -0.5

    def body(q_ref, k_ref, v_ref, kt_ref, do_ref, o_ref, l_ref, dq_ref, dk_ref, dv_ref, dkacc, dvacc):
        h, i = pl.program_id(1), pl.program_id(2)

        @pl.when((h == 0) & (i == 0))
        def _():
            dkacc[...] = jnp.zeros(dkacc.shape, F32)
            dvacc[...] = jnp.zeros(dvacc.shape, F32)

        q = q_ref[...]
        do = do_ref[...]
        dob = do.astype(BF16)
        t = _row_sums(do * o_ref[...].astype(F32))
        pt = jnp.exp2(_dot(k_ref[...], q, 1, 1) * (scale * LOG2E) - l_ref[...] * LOG2E)
        dsb = (pt * (_dot(v_ref[...], dob, 1, 1) - t)).astype(BF16)
        dvacc[...] += _dot(pt.astype(BF16), dob)
        dkacc[...] += _dot(dsb, q)
        dq_ref[...] = _dot(kt_ref[...], dsb).T * scale

        @pl.when((h == GQA_GROUP_B - 1) & (i == nq - 1))
        def _():
            dk_ref[...] = dkacc[...] * scale
            dv_ref[...] = dvacc[...].astype(BF16)

    head = lambda g, h, i: (i, g * GQA_GROUP_B + h)
    return _call(name, body, (N_KV_B, GQA_GROUP_B, nq),
                 [(qkv, (tq, LANES), head),
                  (qkv, (S, LANES), lambda g, h, i: (0, N_HEADS_B + g)),
                  (qkv, (S, LANES), lambda g, h, i: (0, N_HEADS_B + N_KV_B + g)),
                  (k_t, (LANES, S), lambda g, h, i: (g, 0)),
                  (do_b, (tq, LANES), head), (o_b, (tq, LANES), head),
                  (lse, (None, 1, tq), lambda g, h, i: (g * GQA_GROUP_B + h, 0, i))],
                 [((S, N_HEADS_B * LANES), F32, (tq, LANES), head),
                  ((S, N_KV_B * LANES), F32, (S, LANES), lambda g, h, i: (0, g)),
                  ((S, N_KV_B * LANES), BF16, (S, LANES), lambda g, h, i: (0, g))],
                 scratch=[pltpu.VMEM((S, LANES), F32)] * 2,
                 sem=("parallel", "arbitrary", "arbitrary"))


MERGE_TN = 512


def _mix_rows_spec(Gm, row0, n_slots, slot_map, cols=None, col_map=None):
    C = Gm.shape[2] if cols is None else cols
    cm = (lambda *idx: 0) if col_map is None else col_map
    return (Gm, (n_slots, LANES, C), lambda *idx: (slot_map(*idx), row0 // LANES, cm(*idx)))


def _gate_specs(proj_b, tm):
    first = PB_GATE_A // MERGE_TN
    return [(proj_b, (tm, MERGE_TN), lambda i, k=k: (i, first + k)) for k in range(4)]


def _whole_rows_spec(Gm, row0):
    return _mix_rows_spec(Gm, row0, N_DEV, lambda *idx: 0)


def merge_fwd(o_a, o_b, w_a, Gm, proj_b, b_gate, x, name):
    S, D = x.shape
    tm = 256

    def body(oa_ref, ob_ref, wa_ref, wb_ref, wo_ref, g0, g1, g2, g3, bg_ref, x_ref, m_ref, ya_ref, yb_ref, xo_ref):
        ya = _dot(oa_ref[...], wa_ref[...])
        yb = _dot(ob_ref[...], wb_ref[...].reshape(N_DEV * LANES, D))
        ga = _sigmoid(jnp.concatenate([g0[...], g1[...]], axis=1) + bg_ref[:, 0:D])
        gb = _sigmoid(jnp.concatenate([g2[...], g3[...]], axis=1) + bg_ref[:, D:2 * D])
        merged = (ga * ya + gb * yb).astype(BF16)
        m_ref[...] = merged
        ya_ref[...] = ya.astype(BF16)
        yb_ref[...] = yb.astype(BF16)
        xo_ref[...] = x_ref[...] + _dot(merged, wo_ref[...].reshape(N_DEV * LANES, D))

    rows = lambda a: (a, (tm, a.shape[1]), lambda i: (i, 0))
    out = ((S, D), BF16, (tm, D), lambda i: (i, 0))
    return _call(name, body, (S // tm,),
                 [rows(o_a), rows(o_b), (w_a, w_a.shape, lambda i: (0, 0)),
                  _whole_rows_spec(Gm, REST_WB), _whole_rows_spec(Gm, REST_WOUT)]
                 + _gate_specs(proj_b, tm) + [(b_gate, (1, 2 * D), lambda i: (0, 0)), rows(x)],
                 [out, out, out, ((S, D), F32, (tm, D), lambda i: (i, 0))], sem=("parallel",))


def merge_bwd(dx2, w_a, Gm, ya, yb, proj_b, b_gate, name):
    S, D = dx2.shape
    tm = 256

    def body(d_ref, wo_ref, wa_ref, wb_ref, ya_ref, yb_ref, g0, g1, g2, g3, bg_ref,
             dya_ref, dyb_ref, dg_ref, dbg_ref, doa_ref, dob_ref):
        i = pl.program_id(0)
        dm = _dot(d_ref[...].astype(BF16), wo_ref[...].reshape(N_DEV * LANES, D), 1, 1)
        ga = _sigmoid(jnp.concatenate([g0[...], g1[...]], axis=1) + bg_ref[:, 0:D])
        gb = _sigmoid(jnp.concatenate([g2[...], g3[...]], axis=1) + bg_ref[:, D:2 * D])
        dya = (dm * ga).astype(BF16)
        dyb = (dm * gb).astype(BF16)
        dya_ref[...] = dya
        dyb_ref[...] = dyb
        dpa = dm * ya_ref[...].astype(F32) * ga * (1.0 - ga)
        dpb = dm * yb_ref[...].astype(F32) * gb * (1.0 - gb)
        dg_ref[0] = dpa.astype(BF16)
        dg_ref[1] = dpb.astype(BF16)
        doa_ref[...] = _dot(dya, wa_ref[...], 1, 1)
        dob_ref[...] = _dot(dyb, wb_ref[...].reshape(N_DEV * LANES, D), 1, 1)
        sa = jnp.sum(dpa, axis=0, keepdims=True)
        sb = jnp.sum(dpb, axis=0, keepdims=True)

        @pl.when(i == 0)
        def _():
            dbg_ref[0] = sa
            dbg_ref[1] = sb

        @pl.when(i > 0)
        def _():
            dbg_ref[0] += sa
            dbg_ref[1] += sb

    tile = ((tm, D), lambda i: (i, 0))
    return _call(
        name, body, (S // tm,),
        [(dx2,) + tile, _whole_rows_spec(Gm, REST_WOUT), (w_a, w_a.shape, lambda i: (0, 0)),
         _whole_rows_spec(Gm, REST_WB), (ya,) + tile, (yb,) + tile]
        + _gate_specs(proj_b, tm) + [(b_gate, (1, 2 * D), lambda i: (0, 0))],
        [((S, D), BF16) + tile, ((S, D), BF16) + tile,
         ((2, S, D), BF16, (2, tm, D), lambda i: (0, i, 0)),
         ((2, 1, D), F32, (2, 1, D), lambda i: (0, 0, 0)),
         ((S, w_a.shape[0]), F32, (tm, w_a.shape[0]), lambda i: (i, 0)),
         ((S, N_HEADS_B * LANES), F32, (tm, N_HEADS_B * LANES), lambda i: (i, 0))],
        sem=("arbitrary",))


def weight_grad_rows(a, b, grads, row0, name):
    S, M = a.shape
    N = b.shape[1]
    tmm = 512
    tk = WGRAD_TK
    nk = S // tk

    def body(g_ref, a_ref, b_ref, o_ref, acc_ref):
        k = pl.program_id(1)
        p = _dot(a_ref[...], b_ref[...].astype(BF16), 0, 0)

        @pl.when(k == 0)
        def _():
            acc_ref[...] = p

        @pl.when(k > 0)
        def _():
            acc_ref[...] += p

        @pl.when(k == nk - 1)
        def _():
            o_ref[...] = acc_ref[...].astype(BF16).reshape(tmm // LANES, LANES, N)

    return pl.pallas_call(
        body,
        out_shape=jax.ShapeDtypeStruct(grads.shape, BF16),
        grid=(M // tmm, nk),
        in_specs=[pl.BlockSpec(memory_space=pl.ANY),
                  pl.BlockSpec((tk, tmm), lambda j, k: (k, j)),
                  pl.BlockSpec((tk, N), lambda j, k: (k, 0))],
        out_specs=pl.BlockSpec((tmm // LANES, LANES, N), lambda j, k: (j, row0 // LANES, 0)),
        scratch_shapes=[pltpu.VMEM((tmm, N), F32)],
        input_output_aliases={0: 0},
        name=name,
        compiler_params=pltpu.CompilerParams(dimension_semantics=("parallel", "arbitrary"),
                                             vmem_limit_bytes=VMEM_LIMIT),
    )(grads, a, b)


def weight_grad_plain(a, b, name):
    S, M = a.shape
    N = b.shape[1]
    tk = WGRAD_TK
    nk = S // tk

    def body(a_ref, b_ref, o_ref, acc_ref):
        k = pl.program_id(0)
        p = _dot(a_ref[...], b_ref[...], 0, 0)

        @pl.when(k == 0)
        def _():
            acc_ref[...] = p

        @pl.when(k > 0)
        def _():
            acc_ref[...] += p

        @pl.when(k == nk - 1)
        def _():
            o_ref[...] = acc_ref[...].astype(BF16)

    return _call(name, body, (nk,),
                 [(a, (tk, M), lambda k: (k, 0)), (b, (tk, N), lambda k: (k, 0))],
                 [((M, N), BF16, (M, N), lambda k: (0, 0))],
                 scratch=[pltpu.VMEM((M, N), F32)], sem=("arbitrary",))[0]


def local_step(x, tgt, p, get_g1_up, get_g1_down, get_gm_in, get_gm_rest, get_g2, emit, start_token):
    S, D = x.shape
    after = lambda t: t[0:1, 0:1]
    buckets = _bucket_tables()
    cos_t, sin_t = _rope_tables(S)
    gains = jnp.concatenate([jnp.tile(p["q_norm"], (1, N_HEADS_B)), jnp.tile(p["k_norm"], (1, N_KV_B)),
                             jnp.ones((1, N_KV_B * LANES), F32)], axis=1)

    n1 = rms_fwd(x, p["ffn1_norm"] + after(start_token), "ffn1_norm")
    bias = bias_build(p["rel_bias"] + after(start_token), buckets)
    g1_up = get_g1_up((n1, bias))
    ab1 = ffn_up(n1, (g1_up, None), "ffn1_up")
    G1 = (g1_up, get_g1_down(ab1))
    x1 = ffn_down(ab1, G1, x, "ffn1_down")

    hm = rms_fwd(x1, p["mix_norm"], "mix_norm")
    Gw = get_gm_in(hm)
    n_a = A_QKV_WIDTH // PROJ_TN
    proj_a = [in_proj(hm, Gw, g, 3, BF16, "in_proj_a%d" % g, tile_stride=3) for g in range(3)]
    proj_b = in_proj(hm, Gw, n_a, PB_WIDTH // PROJ_TN, F32, "in_proj_b")

    outs, lses = [], []
    for g in range(3):
        o, l = a_fwd(proj_a[g], bias[g], g, "a_fwd_%d" % g)
        outs.append(o)
        lses.append(l)
    o_a, lse_tot = a_combine(outs, lses, "a_combine")

    qkv = qkv_prep(proj_b, gains, cos_t, sin_t, "qkv_prep")
    k_t = qkv[:, N_HEADS_B * LANES:(N_HEADS_B + N_KV_B) * LANES].T
    o_b, lse_b = flash_fwd(qkv, "flash_fwd")

    Gm = get_gm_rest(o_b)
    w_a = Gm[:, REST_WA:REST_ROWS, :].reshape(N_DEV, GROUP_WIDTH_A, LANES).transpose(1, 0, 2).reshape(GROUP_WIDTH_A, D)
    merged, ya, yb, x2 = merge_fwd(o_a, o_b, w_a, Gm, proj_b, p["b_gate"], x1, "merge_fwd")

    G2 = get_g2(x2)
    n2 = rms_fwd(x2, p["ffn2_norm"], "ffn2_norm")
    ab2, x3 = ffn_fwd(n2, G2, x2, "ffn2_fwd")

    loss, dx3, dx3_b, d_final = final_loss(x3, tgt, p["final_norm"], "final_loss")

    dab2, dx2, dx2_b, d_ffn2_norm = ffn_bwd_hidden_input(dx3_b, dx3, ab2, G2, x2, p["ffn2_norm"], "ffn2_bwd")
    gw2 = ffn_bwd_weights(dx3_b, ab2, dab2, n2, "ffn2_bwd")
    t2 = emit("ffn2", gw2)

    dya, dyb, dgate, dbg, do_a, do_b = merge_bwd(dx2_b, w_a, Gm, ya, yb, proj_b, p["b_gate"] + after(t2),
                                                 "merge_bwd")
    gm_grads = jnp.zeros((N_DEV, MIX_ROWS, D), BF16)
    gm_grads = weight_grad_rows(merged, dx2_b, gm_grads, MIX_WOUT, "dw_out")
    gm_grads = weight_grad_rows(o_b, dyb, gm_grads, MIX_WB, "dw_branch_b")
    dw_a = weight_grad_plain(o_a, dya, "dw_branch_a")

    dq_r, dk_r, dv_b = flash_bwd(qkv, k_t, do_b, o_b, lse_b, "flash_bwd")
    dq_b, d_q_norm = qk_prep_bwd(dq_r, proj_b, 0, p["q_norm"], cos_t, sin_t, "q_prep_bwd")
    dk_b, d_k_norm = qk_prep_bwd(dk_r, proj_b, N_HEADS_B, p["k_norm"], cos_t, sin_t, "k_prep_bwd")

    dqkv, dbs = [], []
    for g in range(3):
        dg_, db = a_bwd(proj_a[g], bias[g], do_a, o_a, lse_tot, g, "a_bwd_%d" % g)
        dqkv.append(dg_)
        dbs.append(db)
    d_rel_bias = bias_bwd(jnp.stack(dbs, axis=0).reshape(3, HEADS_PER_GROUP_A, A_TQ, A_WIN), buckets)

    dproj = _dproj_pieces(dqkv, dq_b, jnp.concatenate([dk_b, dv_b], axis=1), dgate)
    gm_grads = in_proj_bwd_dw(dproj, hm, gm_grads, "in_proj_bwd")
    dw_a_sh = dw_a.reshape(GROUP_WIDTH_A, N_DEV, LANES).transpose(1, 0, 2).reshape(N_DEV, MIX_ROWS - MIX_WA, D)
    gm_grads = lax.dynamic_update_slice(gm_grads, dw_a_sh, (0, MIX_WA, 0))
    tm = emit("mix", gm_grads)
    dx1, dx1_b, d_mix_norm = in_proj_bwd_dh(dproj, Gw, x1, p["mix_norm"] + after(tm), dx2, "in_proj_bwd")

    dab1 = ffn_bwd_hidden(dx1_b, ab1, G1, "ffn1_bwd")
    gw1 = ffn_bwd_weights(dx1_b, ab1, dab1, n1, "ffn1_bwd")
    t1 = emit("ffn1", gw1)
    dx0, d_ffn1_norm = ffn_bwd_input(dab1, G1, x, p["ffn1_norm"] + after(t1), dx1, "ffn1_bwd", as_operand=False)

    small = dict(ffn1_norm=d_ffn1_norm, mix_norm=d_mix_norm, b_gate=dbg.reshape(1, 2 * D),
                 q_norm=d_q_norm, k_norm=d_k_norm, rel_bias=d_rel_bias, ffn2_norm=d_ffn2_norm,
                 final_norm=d_final)
    return loss, dx0, small


def _pack_small(t, loss_row):
    row6 = jnp.concatenate([t["q_norm"].reshape(1, -1), t["k_norm"].reshape(1, -1), t["rel_bias"].reshape(1, -1)], axis=1)
    return jnp.concatenate([t["ffn1_norm"].reshape(1, -1), t["mix_norm"].reshape(1, -1), t["b_gate"].reshape(2, -1),
                            t["ffn2_norm"].reshape(1, -1), t["final_norm"].reshape(1, -1), row6, loss_row], axis=0)


def _unpack_small(a, shapes):
    return dict(ffn1_norm=a[0:1].reshape(shapes["ffn1_norm"]), mix_norm=a[1:2].reshape(shapes["mix_norm"]),
                b_gate=a[2:4].reshape(shapes["b_gate"]), ffn2_norm=a[4:5].reshape(shapes["ffn2_norm"]),
                final_norm=a[5].reshape(shapes["final_norm"]), q_norm=a[6:7, 0:128].reshape(shapes["q_norm"]),
                k_norm=a[6:7, 128:256].reshape(shapes["k_norm"]), rel_bias=a[6, 256:1024].reshape(shapes["rel_bias"]))


SMALL = ("ffn1_norm", "mix_norm", "b_gate", "q_norm", "k_norm", "rel_bias", "ffn2_norm", "final_norm")
ORDER = ("ffn1_norm", "ffn1_w1", "ffn1_w3", "ffn1_w2", "mix_norm", "w_in", "b_gate", "q_norm", "k_norm", "rel_bias",
         "w_branch_a", "w_branch_b", "w_out", "ffn2_norm", "ffn2_w1", "ffn2_w3", "ffn2_w2", "final_norm")


def kernel(x, ffn1_norm, ffn1_w1, ffn1_w3, ffn1_w2, mix_norm, w_in, b_gate, q_norm, k_norm, rel_bias, w_branch_a, w_branch_b, w_out, ffn2_norm, ffn2_w1, ffn2_w3, ffn2_w2, final_norm, loss_target, m_ffn1_norm, m_ffn1_w1, m_ffn1_w3, m_ffn1_w2, m_mix_norm, m_w_in, m_b_gate, m_q_norm, m_k_norm, m_rel_bias, m_w_branch_a, m_w_branch_b, m_w_out, m_ffn2_norm, m_ffn2_w1, m_ffn2_w3, m_ffn2_w2, m_final_norm, v_ffn1_norm, v_ffn1_w1, v_ffn1_w3, v_ffn1_w2, v_mix_norm, v_w_in, v_b_gate, v_q_norm, v_k_norm, v_rel_bias, v_w_branch_a, v_w_branch_b, v_w_out, v_ffn2_norm, v_ffn2_w1, v_ffn2_w3, v_ffn2_w2, v_final_norm):
    args = dict(locals())
    w = {n: args[n] for n in ORDER}
    m = {n: args["m_" + n] for n in ORDER}
    v = {n: args["v_" + n] for n in ORDER}
    D = x.shape[2]

    blocks = (
        ("ffn1_up", jnp.concatenate([ffn1_w1[0].T, ffn1_w3[0].T], axis=0)),
        ("ffn1_down", ffn1_w2[0]),
        ("mix_in", w_in[0]),
        ("mix_rest", jnp.concatenate([w_branch_b[0], w_out[0], w_branch_a[0].reshape(REST_ROWS - REST_WA, D)], axis=0)),
        ("ffn2", jnp.concatenate([ffn2_w1[0].T, ffn2_w3[0].T, ffn2_w2[0]], axis=0)),
    )
    direct = ("mix_rest", "ffn2")
    started = all_gather_start_all([(b.astype(BF16), tag in direct) for tag, b in blocks], "all_gather_start")
    gathers = {tag: s for (tag, _), s in zip(blocks, started)}
    start_token = started[0][4]

    def gathered(tag):
        def get(after):
            if tag in direct:
                return all_gather_place_own(*_split_wait("all_gather_" + tag + "_wait", gathers[tag], N_DEV - 1, after),
                                            "all_gather_" + tag + "_own")
            return all_gather_finish(*_split_wait("all_gather_" + tag + "_wait", gathers[tag], 4, after),
                                     "all_gather_" + tag + "_finish")
        return get

    core = lax.axis_index("c").astype(jnp.int32).reshape(1)
    chip = (2 * lax.axis_index("x") + lax.axis_index("y")).astype(jnp.int32).reshape(1)
    device = 2 * chip + core
    exchanges = {}

    def emit(tag, gw):
        if tag == "ffn1":
            (theirs,) = reduce_scatter_pair([gw], "reduce_scatter_pair_" + tag)
            part = pair_add(gw, theirs, core, "pair_add_" + tag)
            exchanges[tag] = reduce_scatter_start(part, "reduce_scatter_" + tag + "_start")
        else:
            exchanges[tag] = reduce_scatter_start_direct(gw, "reduce_scatter_" + tag + "_start")
        return exchanges[tag][4]

    small_p = dict(ffn1_norm=ffn1_norm, mix_norm=mix_norm, b_gate=b_gate, q_norm=q_norm, k_norm=k_norm,
                   rel_bias=rel_bias, ffn2_norm=ffn2_norm, final_norm=final_norm.reshape(1, D))
    loss_p, grad_x, small_g = local_step(x[0], loss_target[0], small_p, gathered("ffn1_up"), gathered("ffn1_down"),
                                         gathered("mix_in"), gathered("mix_rest"), gathered("ffn2"), emit, start_token)

    def landed(tag, after):
        n_others, me = (3, chip) if tag == "ffn1" else (N_DEV - 1, device)
        return tuple(_split_wait("reduce_scatter_" + tag + "_wait", exchanges[tag], n_others, after)) + (me,)

    grads, delta, new_m, new_v = {}, {}, {}, {}

    def finish(n, part, land, me, off, blk, transposed=False):
        shp = w[n].shape
        if transposed:
            to2 = lambda a: a.reshape(shp[-2], shp[-1]).T
            back = lambda a: a.T.reshape(shp)
        else:
            to2 = lambda a: a.reshape(shp[-2], shp[-1])
            back = lambda a: a.reshape(shp)
        res = sum_adamw(part, land, me, off, blk, to2(w[n]), to2(m[n]), to2(v[n]), "update_" + n)
        grads[n], delta[n], new_m[n], new_v[n] = [back(a) for a in res]

    last_token = exchanges["ffn1"][4]
    for tag, after in (("ffn2", last_token), ("ffn1", grad_x)):
        group = landed(tag, after)
        finish(tag + "_w1", *group, 0, FFN_SHARD, transposed=True)
        finish(tag + "_w3", *group, FFN_SHARD, FFN_SHARD, transposed=True)
        finish(tag + "_w2", *group, 2 * FFN_SHARD, FFN_SHARD)
        if tag == "ffn2":
            group_m = landed("mix", last_token)
            finish("w_in", *group_m, MIX_WIN, LANES)
            finish("w_branch_b", *group_m, MIX_WB, LANES)
            finish("w_out", *group_m, MIX_WOUT, LANES)
            grads["w_branch_a"] = sum_landed(*group_m, MIX_WA, MIX_ROWS - MIX_WA, MIX_ROWS - MIX_WA,
                                             "w_branch_a_sum").reshape(w_branch_a.shape)
    loss_row = jnp.pad(loss_p, ((0, 0), (0, D - LANES)))
    smalls = small_all_gather(_pack_small(small_g, loss_row))
    small_sum = sum_slots(smalls, 0, N_DEV, N_DEV, "small_sum")
    small_shapes = {n: w[n].shape for n in SMALL}
    grads.update(_unpack_small(small_sum, small_shapes))
    loss = small_sum[7, 0]

    n = "w_branch_a"
    two_d = lambda a: a.reshape(w[n].shape[-2], w[n].shape[-1])
    d_, m_, v_ = adamw(two_d(w[n]), two_d(grads[n]), two_d(m[n]), two_d(v[n]), "adamw_" + n)
    delta[n], new_m[n], new_v[n] = [a.reshape(w[n].shape) for a in (d_, m_, v_)]
    zero_row = jnp.zeros((1, D), F32)
    pack = lambda t: _pack_small({n: t[n] for n in SMALL}, zero_row)
    d_, m_, v_ = adamw(pack(w), small_sum, pack(m), pack(v), "adamw_small")
    for src, dst in ((d_, delta), (m_, new_m), (v_, new_v)):
        dst.update(_unpack_small(src, small_shapes))

    return (loss, grad_x[None], *[grads[n] for n in ORDER], *[delta[n] for n in ORDER],
            *[new_m[n] for n in ORDER], *[new_v[n] for n in ORDER])
```

```python
import math

import jax
import jax.numpy as jnp
from jax import lax
from jax.experimental import pallas as pl
from jax.experimental.pallas import tpu as pltpu

F32 = jnp.float32
BF16 = jnp.bfloat16
MESH = pl.DeviceIdType.MESH

V7X_VMEM_BYTES = 64 * 1024 * 1024
VMEM_LIMIT = V7X_VMEM_BYTES - 8 * 1024 * 1024
LANES = 128

N_DEV = 8
EPS = 1e-6
NEG_INF = -1e30

DILATIONS = (1, 4, 16)
HALF_WINDOW = 64
HEAD_DIM_A = 64
HEADS_PER_GROUP_A = 8
GROUP_WIDTH_A = 512
A_QKV_WIDTH = 4608
A_GROUP_QKV = A_QKV_WIDTH // 3
A_TQ = 128
A_WIN = A_TQ + 2 * HALF_WINDOW
A_UNROLL = 8
A_SCALE = HEAD_DIM_A ** -0.5
WGRAD_TK = 2048
HEAD_DIM_B = 128
N_HEADS_B = 8
N_KV_B = 2
GQA_GROUP_B = 4
GRID_W = 64
ROPE_THETA = 10000.0
B_TQ_FWD = 256
B_TQ_BWD = 512
B_HEADS_PER_STEP = 4
LOG2E = 1.4426950408889634
N_BUCKETS = 32
MAX_DISTANCE = 1024
PB_WIDTH = 3584
PB_GATE_A = 1536
PB_GATE_B = 2560

ADAM_LR = 0.001
ADAM_B1 = 0.9
ADAM_B2 = 0.999
ADAM_EPS = 1e-08
ADAM_WD = 0.01
ADAM_STEP = 10

FFN_SHARD = 352
MIX_WIN, MIX_WB, MIX_WOUT, MIX_WA = 0, 1024, 1152, 1280
MIX_ROWS = 1344
REST_WB, REST_WOUT, REST_WA, REST_ROWS = 0, 128, 256, 320


def _dot(a, b, ca=1, cb=0):
    return lax.dot_general(a, b, (((ca,), (cb,)), ((), ())), preferred_element_type=F32)


def _call(name, body, grid, ins, outs, scratch=(), sem=None, aliases=None):
    ins = [tuple(i) + (None,) * (4 - len(i)) for i in ins]
    res = pl.pallas_call(
        body,
        out_shape=[jax.ShapeDtypeStruct(s, d) for (s, d, _, _) in outs],
        grid=grid,
        in_specs=[pl.BlockSpec(bs, im, pipeline_mode=pm) for (_, bs, im, pm) in ins],
        out_specs=[pl.BlockSpec(bs, im) for (_, _, bs, im) in outs],
        scratch_shapes=list(scratch),
        name=name,
        input_output_aliases=aliases or {},
        compiler_params=pltpu.CompilerParams(dimension_semantics=sem, vmem_limit_bytes=VMEM_LIMIT),
    )(*[i[0] for i in ins])
    return res


def _sigmoid(x):
    return 0.5 * jnp.tanh(0.5 * x) + 0.5


def _position():
    return lax.axis_index("x"), lax.axis_index("y"), lax.axis_index("c")


def _hbm_specs(n):
    return [pl.BlockSpec(memory_space=pl.ANY) for _ in range(n)]


PAIR_BUFFERS = 4


def reduce_scatter_pair(grads, name):
    n = len(grads)
    C = grads[0].shape[2]
    half = [g.shape[1] // 2 for g in grads]
    chunks = [(i, q, hf) for i in range(n) for q in range(4) for hf in range(2)]
    nb = PAIR_BUFFERS

    def body(*refs):
        ins, theirs = refs[:n], refs[n:2 * n]
        buf, load_sems, send_sems, recv_sems = refs[2 * n:]
        x, y, c = _position()
        sibling = (x, y, 1 - c)

        def load(k):
            i, q, hf = chunks[k]
            r = half[i]
            return pltpu.make_async_copy(ins[i].at[2 * q + (1 - c), pl.ds(hf * r, r), :],
                                         buf.at[k % nb, pl.ds(0, r), :], load_sems.at[k % nb])

        def send(k):
            i, q, hf = chunks[k]
            r = half[i]
            return pltpu.make_async_remote_copy(
                src_ref=buf.at[k % nb, pl.ds(0, r), :], dst_ref=theirs[i].at[q, pl.ds(hf * r, r), :],
                send_sem=send_sems.at[k % nb], recv_sem=recv_sems.at[i],
                device_id=sibling, device_id_type=MESH)

        for k in range(len(chunks) + 1):
            if k < len(chunks):
                if k >= nb:
                    send(k - nb).wait_send()
                load(k).start()
            if k >= 1:
                load(k - 1).wait()
                send(k - 1).start()
        for k in range(max(0, len(chunks) - nb), len(chunks)):
            send(k).wait_send()
        for i in range(n):
            pltpu.make_async_remote_copy(
                src_ref=theirs[i], dst_ref=theirs[i], send_sem=send_sems.at[0], recv_sem=recv_sems.at[i],
                device_id=sibling, device_id_type=MESH).wait_recv()

    return pl.pallas_call(
        body,
        out_shape=[jax.ShapeDtypeStruct((4,) + g.shape[1:], g.dtype) for g in grads],
        in_specs=_hbm_specs(n),
        out_specs=_hbm_specs(n),
        scratch_shapes=[pltpu.VMEM((nb, max(half), C), grads[0].dtype), pltpu.SemaphoreType.DMA((nb,)),
                        pltpu.SemaphoreType.DMA((nb,)), pltpu.SemaphoreType.DMA((n,))],
        name=name,
        compiler_params=pltpu.CompilerParams(vmem_limit_bytes=VMEM_LIMIT),
    )(*grads)


_HBM_SPEC = pl.BlockSpec(memory_space=pltpu.HBM)
_SEM_SPEC = pl.BlockSpec(memory_space=pltpu.SEMAPHORE)
_TOKEN_SPEC = pl.BlockSpec(memory_space=pltpu.VMEM)
_DATAFLOW = pltpu.SideEffectType.DATAFLOW_SIDE_EFFECTING


def _split_start_many(name, exchanges):
    n = len(exchanges)

    def full_body(*refs):
        srcs, lands = refs[:n], refs[n:2 * n]
        sems = refs[2 * n:4 * n]
        token = refs[-1]
        for i, (body, _, _) in enumerate(exchanges):
            body(srcs[i], lands[i], sems[2 * i], sems[2 * i + 1])
        token[...] = jnp.zeros_like(token)

    srcs = [pltpu.with_memory_space_constraint(src, pltpu.HBM) for _, src, _ in exchanges]
    lands = [pltpu.with_memory_space_constraint(lax.empty(shape, src.dtype), pltpu.HBM)
             for _, src, shape in exchanges]
    res = pl.pallas_call(
        full_body, name=name,
        out_shape=(pltpu.SemaphoreType.DMA(()),) * (2 * n)
        + tuple(pltpu.HBM(a.shape, a.dtype) for a in srcs + lands) + (jax.ShapeDtypeStruct((8, LANES), F32),),
        in_specs=(_HBM_SPEC,) * (2 * n),
        out_specs=(_SEM_SPEC,) * (2 * n) + (_HBM_SPEC,) * (2 * n) + (_TOKEN_SPEC,),
        input_output_aliases={i: 2 * n + i for i in range(2 * n)},
        compiler_params=pltpu.CompilerParams(has_side_effects=_DATAFLOW),
    )(*srcs, *lands)
    return [(res[2 * i], res[2 * i + 1], res[2 * n + i], res[3 * n + i], res[-1]) for i in range(n)]


def _split_start(name, body, src, land_shape):
    return _split_start_many(name, [(body, src, land_shape)])[0]


def _split_wait(name, started, n_blocks, after):
    send_sem, recv_sem, src_thru, land_thru, _ = started
    after = after if isinstance(after, tuple) else (after,)

    def body(src_ref, land_ref, send_sem, recv_sem, *rest):
        x, y, c = _position()
        blocks = land_ref.at[pl.ds(0, n_blocks)]
        copy = pltpu.make_async_remote_copy(src_ref=blocks, dst_ref=blocks, send_sem=send_sem, recv_sem=recv_sem,
                                            device_id=(x, y, c), device_id_type=MESH)
        copy.wait_send()
        copy.wait_recv()

    return pl.pallas_call(
        body, name=name,
        out_shape=(pltpu.HBM(src_thru.shape, src_thru.dtype), pltpu.HBM(land_thru.shape, land_thru.dtype)),
        in_specs=(_HBM_SPEC, _HBM_SPEC, _SEM_SPEC, _SEM_SPEC) + (pl.BlockSpec(memory_space=pl.ANY),) * len(after),
        out_specs=(_HBM_SPEC, _HBM_SPEC),
        input_output_aliases={0: 0, 1: 1},
        compiler_params=pltpu.CompilerParams(has_side_effects=_DATAFLOW),
    )(src_thru, land_thru, send_sem, recv_sem, *after)


def all_gather_start_all(blocks, name):
    def starter(direct):
        def body(b_ref, land_ref, send_sem, recv_sem):
            x, y, c = _position()
            peers = _other_devices(x, y, c) if direct else [(x, y, 1 - c), (1 - x, y, c), (x, 1 - y, c),
                                                            (1 - x, 1 - y, c)]
            for peer in peers:
                pltpu.make_async_remote_copy(src_ref=b_ref, dst_ref=land_ref.at[4 * x + 2 * y + c],
                                             send_sem=send_sem, recv_sem=recv_sem,
                                             device_id=peer, device_id_type=MESH).start()
        return body

    return _split_start_many(name, [(starter(direct), block, (N_DEV,) + block.shape) for block, direct in blocks])


def all_gather_finish(block, land, name):
    R, C = block.shape

    def body(b_ref, land_in, land_ref, stage, load_sems, send_sems, recv_sems, own_sem):
        x, y, c = _position()
        sibling = (x, y, 1 - c)
        chips = [(1 - x, y), (x, 1 - y), (1 - x, 1 - y)]
        own_in = pltpu.make_async_copy(b_ref, stage.at[3], load_sems.at[3])
        own_in.start()
        loads = [pltpu.make_async_copy(land_in.at[4 * px + 2 * py + c], stage.at[j], load_sems.at[j])
                 for j, (px, py) in enumerate(chips)]
        for ld in loads:
            ld.start()
        sends = []
        for j, (px, py) in enumerate(chips):
            loads[j].wait()
            dst = land_ref.at[4 * px + 2 * py + c]
            cp = pltpu.make_async_remote_copy(src_ref=stage.at[j], dst_ref=dst, send_sem=send_sems.at[j],
                                              recv_sem=recv_sems.at[j], device_id=sibling, device_id_type=MESH)
            cp.start()
            sends.append(cp)
        own_in.wait()
        own_out = pltpu.make_async_copy(stage.at[3], land_ref.at[4 * x + 2 * y + c], own_sem)
        own_out.start()
        for j, (px, py) in enumerate(chips):
            dst = land_ref.at[4 * px + 2 * py + (1 - c)]
            pltpu.make_async_remote_copy(src_ref=stage.at[j], dst_ref=dst, send_sem=send_sems.at[j],
                                         recv_sem=recv_sems.at[j], device_id=sibling,
                                         device_id_type=MESH).wait_recv()
        for cp in sends:
            cp.wait_send()
        own_out.wait()

    return pl.pallas_call(
        body,
        out_shape=jax.ShapeDtypeStruct(land.shape, land.dtype),
        in_specs=_hbm_specs(2),
        out_specs=pl.BlockSpec(memory_space=pl.ANY),
        scratch_shapes=[pltpu.VMEM((4, R, C), block.dtype), pltpu.SemaphoreType.DMA((4,)),
                        pltpu.SemaphoreType.DMA((3,)), pltpu.SemaphoreType.DMA((3,)), pltpu.SemaphoreType.DMA],
        input_output_aliases={1: 0},
        name=name,
        compiler_params=pltpu.CompilerParams(vmem_limit_bytes=VMEM_LIMIT),
    )(block, land)


def reduce_scatter_start(parts, name):
    def body(p_ref, land_ref, send_sem, recv_sem):
        x, y, c = _position()
        for px, py in [(1 - x, y), (x, 1 - y), (1 - x, 1 - y)]:
            pltpu.make_async_remote_copy(src_ref=p_ref.at[2 * px + py], dst_ref=land_ref.at[2 * x + y],
                                         send_sem=send_sem, recv_sem=recv_sem,
                                         device_id=(px, py, c), device_id_type=MESH).start()

    return _split_start(name, body, parts, parts.shape)


def _other_devices(x, y, c):
    return [(1 - x if k & 4 else x, 1 - y if k & 2 else y, 1 - c if k & 1 else c) for k in range(1, N_DEV)]


def all_gather_place_own(block, land, name):
    R, C = block.shape

    def body(b_ref, land_in, land_ref, stage, sems):
        x, y, c = _position()
        load = pltpu.make_async_copy(b_ref, stage, sems.at[0])
        load.start()
        load.wait()
        store = pltpu.make_async_copy(stage, land_ref.at[4 * x + 2 * y + c], sems.at[1])
        store.start()
        store.wait()

    return pl.pallas_call(
        body,
        out_shape=jax.ShapeDtypeStruct(land.shape, land.dtype),
        in_specs=_hbm_specs(2),
        out_specs=pl.BlockSpec(memory_space=pl.ANY),
        scratch_shapes=[pltpu.VMEM((R, C), block.dtype), pltpu.SemaphoreType.DMA((2,))],
        input_output_aliases={1: 0},
        name=name,
    )(block, land)


def reduce_scatter_start_direct(grads, name):
    def body(g_ref, land_ref, send_sem, recv_sem):
        x, y, c = _position()
        for px, py, pc in _other_devices(x, y, c):
            pltpu.make_async_remote_copy(src_ref=g_ref.at[4 * px + 2 * py + pc],
                                         dst_ref=land_ref.at[4 * x + 2 * y + c],
                                         send_sem=send_sem, recv_sem=recv_sem,
                                         device_id=(px, py, pc), device_id_type=MESH).start()

    return _split_start(name, body, grads, grads.shape)


def small_all_gather(small):
    def body(small_ref, smalls, s_send, s_recv, s_local):
        x, y, c = _position()
        me = 4 * x + 2 * y + c
        lc = pltpu.make_async_copy(small_ref, smalls.at[me], s_local)
        lc.start()
        remote = []
        k = 0
        for dx in (0, 1):
            for dy in (0, 1):
                for dc in (0, 1):
                    if dx + dy + dc == 0:
                        continue
                    peer = (1 - x if dx else x, 1 - y if dy else y, 1 - c if dc else c)
                    rc = pltpu.make_async_remote_copy(
                        src_ref=small_ref, dst_ref=smalls.at[me],
                        send_sem=s_send.at[k], recv_sem=s_recv.at[k],
                        device_id=peer, device_id_type=MESH)
                    rc.start()
                    remote.append(rc)
                    k += 1
        for rc in remote:
            rc.wait()
        lc.wait()

    return pl.pallas_call(
        body,
        out_shape=jax.ShapeDtypeStruct((N_DEV,) + small.shape, small.dtype),
        in_specs=_hbm_specs(1),
        out_specs=pl.BlockSpec(memory_space=pl.ANY),
        scratch_shapes=[pltpu.SemaphoreType.DMA((7,)), pltpu.SemaphoreType.DMA((7,)), pltpu.SemaphoreType.DMA],
        name="small_all_gather",
    )(small)


def pair_add(grads, theirs, core, name):
    _, R, C = theirs.shape
    tr = R // 2

    def body(c_ref, a_ref, b_ref, o_ref):
        o_ref[...] = (a_ref[...].astype(F32) + b_ref[...].astype(F32)).astype(BF16)

    return pl.pallas_call(
        body,
        out_shape=jax.ShapeDtypeStruct(theirs.shape, BF16),
        grid_spec=pltpu.PrefetchScalarGridSpec(
            num_scalar_prefetch=1, grid=(4, R // tr),
            in_specs=[pl.BlockSpec((None, tr, C), lambda q, i, c: (2 * q + c[0], i, 0)),
                      pl.BlockSpec((None, tr, C), lambda q, i, c: (q, i, 0))],
            out_specs=pl.BlockSpec((None, tr, C), lambda q, i, c: (q, i, 0))),
        name=name,
        compiler_params=pltpu.CompilerParams(dimension_semantics=("parallel", "parallel"),
                                             vmem_limit_bytes=VMEM_LIMIT),
    )(core, grads, theirs)


def sum_slots(recv, off, rows, blk, name):
    nq, _, C = recv.shape
    ob = off // blk

    def body(r_ref, o_ref):
        acc = r_ref[0].astype(F32)
        for q in range(1, nq):
            acc = acc + r_ref[q].astype(F32)
        o_ref[...] = acc

    return _call(name, body, (rows // blk,),
                 [(recv, (nq, blk, C), lambda i: (0, ob + i, 0))],
                 [((rows, C), F32, (blk, C), lambda i: (i, 0))], sem=("parallel",))[0]


def _sum_terms(refs):
    acc = refs[0][...].astype(F32)
    for r in refs[1:]:
        acc = acc + r[...].astype(F32)
    return acc


def sum_landed(own, land, me, off, rows, blk, name):
    n, _, C = land.shape
    ob = off // blk

    def body(c_ref, *refs):
        refs[n][...] = _sum_terms(refs[:n])

    def entry(flip):
        return pl.BlockSpec((None, blk, C), lambda i, c: (c[0] ^ flip, ob + i, 0))

    return pl.pallas_call(
        body,
        out_shape=jax.ShapeDtypeStruct((rows, C), F32),
        grid_spec=pltpu.PrefetchScalarGridSpec(
            num_scalar_prefetch=1, grid=(rows // blk,),
            in_specs=[entry(k) for k in range(n)],
            out_specs=pl.BlockSpec((blk, C), lambda i, c: (i, 0))),
        name=name,
        compiler_params=pltpu.CompilerParams(dimension_semantics=("parallel",), vmem_limit_bytes=VMEM_LIMIT),
    )(me, own, *([land] * (n - 1)))


def _adamw_update(wv, gv, mv, vv):
    nm = ADAM_B1 * mv + (1.0 - ADAM_B1) * gv
    nv = ADAM_B2 * vv + (1.0 - ADAM_B2) * (gv * gv)
    c1 = 1.0 / (1.0 - ADAM_B1 ** ADAM_STEP)
    c2 = 1.0 / (1.0 - ADAM_B2 ** ADAM_STEP)
    return -ADAM_LR * ((nm * c1) / (jnp.sqrt(nv * c2) + ADAM_EPS) + ADAM_WD * wv), nm, nv


def sum_adamw(own, land, me, off, blk, w, m, v, name):
    rows, C = w.shape
    n = land.shape[0]
    ob = off // blk

    def body(c_ref, *refs):
        w_ref, m_ref, v_ref, g_out, d_out, m_out, v_out = refs[n:]
        gv = _sum_terms(refs[:n])
        g_out[...] = gv
        d_out[...], m_out[...], v_out[...] = _adamw_update(w_ref[...], gv, m_ref[...], v_ref[...])

    def entry(flip):
        return pl.BlockSpec((None, blk, C), lambda i, c: (c[0] ^ flip, ob + i, 0))

    plain = pl.BlockSpec((blk, C), lambda i, c: (i, 0))
    return pl.pallas_call(
        body,
        out_shape=[jax.ShapeDtypeStruct((rows, C), F32)] * 4,
        grid_spec=pltpu.PrefetchScalarGridSpec(
            num_scalar_prefetch=1, grid=(rows // blk,),
            in_specs=[entry(k) for k in range(n)] + [plain, plain, plain],
            out_specs=[plain] * 4),
        name=name,
        compiler_params=pltpu.CompilerParams(dimension_semantics=("parallel",), vmem_limit_bytes=VMEM_LIMIT),
    )(me, own, *([land] * (n - 1)), w, m, v)


def adamw(w, g, m, v, name):
    R, C = w.shape
    tr = R
    for cand in (256, 128, 64, 32, 16, 8):
        if R % cand == 0 and R > cand:
            tr = cand
            break

    def body(w_ref, g_ref, m_ref, v_ref, d_ref, nm_ref, nv_ref):
        d_ref[...], nm_ref[...], nv_ref[...] = _adamw_update(w_ref[...], g_ref[...], m_ref[...], v_ref[...])

    spec = ((tr, C), lambda i: (i, 0))
    out = ((R, C), F32) + spec
    return _call(name, body, (R // tr,), [(w,) + spec, (g,) + spec, (m,) + spec, (v,) + spec],
                 [out, out, out], sem=("parallel",))


def rms_fwd(x, g, name):
    S, D = x.shape
    tr = 512

    def body(x_ref, g_ref, o_ref):
        xv = x_ref[...]
        r = lax.rsqrt(jnp.mean(xv * xv, axis=-1, keepdims=True) + EPS)
        o_ref[...] = (xv * r * g_ref[...]).astype(BF16)

    return _call(name, body, (S // tr,),
                 [(x, (tr, D), lambda i: (i, 0)), (g, (1, D), lambda i: (0, 0))],
                 [((S, D), BF16, (tr, D), lambda i: (i, 0))], sem=("parallel",))[0]


def _rms_bwd_tile(dn, xv, gv):
    r = lax.rsqrt(jnp.mean(xv * xv, axis=-1, keepdims=True) + EPS)
    xh = xv * r
    dxh = dn * gv
    dx = r * (dxh - xh * jnp.mean(dxh * xh, axis=-1, keepdims=True))
    return dx, dn * xh


def final_loss(x, tgt, g, name):
    S, D = x.shape
    tr = 256

    def body(x_ref, t_ref, g_ref, l_ref, dx_ref, dxb_ref, dg_ref):
        i = pl.program_id(0)
        xv, gv = x_ref[...], g_ref[...]
        r = lax.rsqrt(jnp.mean(xv * xv, axis=-1, keepdims=True) + EPS)
        xh = xv * r
        e = xh * gv - t_ref[...]
        part = 0.5 * jnp.sum(jnp.sum(e * e, axis=-1, keepdims=True) * (1.0 / D), axis=0, keepdims=True)
        dy = e * (1.0 / D)
        dxh = dy * gv
        dx = r * (dxh - xh * jnp.mean(dxh * xh, axis=-1, keepdims=True))
        dx_ref[...] = dx
        dxb_ref[...] = dx.astype(BF16)
        dgp = jnp.sum(dy * xh, axis=0, keepdims=True)

        @pl.when(i == 0)
        def _():
            l_ref[...] = jnp.broadcast_to(part, l_ref.shape)
            dg_ref[...] = dgp

        @pl.when(i > 0)
        def _():
            l_ref[...] += jnp.broadcast_to(part, l_ref.shape)
            dg_ref[...] += dgp

    row = ((tr, D), lambda i: (i, 0))
    return _call(name, body, (S // tr,),
                 [(x,) + row, (tgt,) + row, (g, (1, D), lambda i: (0, 0))],
                 [((1, LANES), F32, (1, LANES), lambda i: (0, 0)), ((S, D), F32) + row, ((S, D), BF16) + row,
                  ((1, D), F32, (1, D), lambda i: (0, 0))], sem=("arbitrary",))


FFN_TF = 4 * FFN_SHARD


def _ffn_pick(G, which):
    if isinstance(G, tuple):
        return (G[0], which) if which < 2 else (G[1], 0)
    return G, which


def _ffn_w_spec(G, which, imap):
    arr, blk = _ffn_pick(G, which)
    return (arr, (4, FFN_SHARD, arr.shape[2]), lambda *idx: (imap(*idx), blk, 0))


def _ffn_whole_w_spec(G, which):
    arr, blk = _ffn_pick(G, which)
    return (arr, (N_DEV, FFN_SHARD, arr.shape[2]), lambda *idx: (0, blk, 0), pl.Buffered(1))


def ffn_up(n, G, name):
    S, D = n.shape
    F = N_DEV * FFN_SHARD
    tm = 256

    def body(n_ref, w1_ref, w3_ref, abh_ref):
        nv = n_ref[...]
        a = _dot(nv, w1_ref[...].reshape(F, D), 1, 1).astype(BF16)
        b = _dot(nv, w3_ref[...].reshape(F, D), 1, 1).astype(BF16)
        abh_ref[0] = a
        abh_ref[1] = b
        av, bv = a.astype(F32), b.astype(F32)
        abh_ref[2] = (av * _sigmoid(av) * bv).astype(BF16)

    return _call(name, body, (S // tm,),
                 [(n, (tm, D), lambda i: (i, 0)),
                  _ffn_whole_w_spec(G, 0), _ffn_whole_w_spec(G, 1)],
                 [((3, S, F), BF16, (3, tm, F), lambda i: (0, i, 0))],
                 sem=("parallel",))[0]


def ffn_down(abh, G, x, name):
    _, S, F = abh.shape
    D = x.shape[1]
    tm = 512

    def body(h_ref, w2_ref, x_ref, o_ref):
        o_ref[...] = x_ref[...] + 0.5 * _dot(h_ref[...], w2_ref[...].reshape(F, D))

    return _call(name, body, (S // tm,),
                 [(abh, (None, tm, F), lambda i: (2, i, 0)), _ffn_whole_w_spec(G, 2),
                  (x, (tm, D), lambda i: (i, 0))],
                 [((S, D), F32, (tm, D), lambda i: (i, 0))], sem=("parallel",))[0]


def ffn_fwd(n, G, x, name):
    S, D = x.shape
    F = N_DEV * FFN_SHARD
    tm = 256

    def body(n_ref, w1_ref, w3_ref, w2_ref, x_ref, abh_ref, o_ref):
        nv = n_ref[...]
        a = _dot(nv, w1_ref[...].reshape(F, D), 1, 1).astype(BF16)
        b = _dot(nv, w3_ref[...].reshape(F, D), 1, 1).astype(BF16)
        av, bv = a.astype(F32), b.astype(F32)
        h = (av * _sigmoid(av) * bv).astype(BF16)
        abh_ref[0] = a
        abh_ref[1] = b
        abh_ref[2] = h
        o_ref[...] = x_ref[...] + 0.5 * _dot(h, w2_ref[...].reshape(F, D))

    tile = ((tm, D), lambda i: (i, 0))
    return _call(name, body, (S // tm,),
                 [(n,) + tile, _ffn_whole_w_spec(G, 0), _ffn_whole_w_spec(G, 1), _ffn_whole_w_spec(G, 2),
                  (x,) + tile],
                 [((3, S, F), BF16, (3, tm, F), lambda i: (0, i, 0)), ((S, D), F32) + tile],
                 sem=("parallel",))


def _ffn_hidden_grads(dh, av, bv):
    sig = _sigmoid(av)
    return dh * bv * (sig * (1.0 + av * (1.0 - sig))), dh * (av * sig)


def ffn_bwd_hidden(dxo, abh, G, name):
    _, S, F = abh.shape
    D = dxo.shape[1]
    tm = 256

    def body(d_ref, w2_ref, ab_ref, o_ref):
        dh = 0.5 * _dot(d_ref[...].astype(BF16), w2_ref[...].reshape(F, D), 1, 1)
        da, db = _ffn_hidden_grads(dh, ab_ref[0].astype(F32), ab_ref[1].astype(F32))
        o_ref[0] = da.astype(BF16)
        o_ref[1] = db.astype(BF16)

    return _call(name + "_down_bwd", body, (S // tm,),
                 [(dxo, (tm, D), lambda i: (i, 0)), _ffn_whole_w_spec(G, 2),
                  (abh, (2, tm, F), lambda i: (0, i, 0))],
                 [((2, S, F), BF16, (2, tm, F), lambda i: (0, i, 0))],
                 sem=("parallel",))[0]


def ffn_bwd_hidden_input(dxo_b, dxo, abh, G, x_in, g, name):
    _, S, F = abh.shape
    D = x_in.shape[1]
    tm = 256

    def body(db_ref, w2_ref, w1_ref, w3_ref, ab_ref, x_ref, d_ref, g_ref, dab_ref, dx_ref, dxb_ref, dg_ref):
        i = pl.program_id(0)
        dh = 0.5 * _dot(db_ref[...], w2_ref[...].reshape(F, D), 1, 1)
        da, db = _ffn_hidden_grads(dh, ab_ref[0].astype(F32), ab_ref[1].astype(F32))
        da, db = da.astype(BF16), db.astype(BF16)
        dab_ref[0] = da
        dab_ref[1] = db
        dn = _dot(da, w1_ref[...].reshape(F, D)) + _dot(db, w3_ref[...].reshape(F, D))
        dx, dgt = _rms_bwd_tile(dn, x_ref[...], g_ref[...])
        dx = d_ref[...] + dx
        dx_ref[...] = dx
        dxb_ref[...] = dx.astype(BF16)
        dgp = jnp.sum(dgt, axis=0, keepdims=True)

        @pl.when(i == 0)
        def _():
            dg_ref[...] = dgp

        @pl.when(i > 0)
        def _():
            dg_ref[...] += dgp

    tile = ((tm, D), lambda i: (i, 0))
    wide = ((2, tm, F), lambda i: (0, i, 0))
    return _call(name + "_hidden_input", body, (S // tm,),
                 [(dxo_b,) + tile, _ffn_whole_w_spec(G, 2), _ffn_whole_w_spec(G, 0), _ffn_whole_w_spec(G, 1),
                  (abh,) + wide, (x_in,) + tile, (dxo,) + tile, (g, (1, D), lambda i: (0, 0))],
                 [((2, S, F), BF16) + wide, ((S, D), F32) + tile, ((S, D), BF16) + tile,
                  ((1, D), F32, (1, D), lambda i: (0, 0))],
                 sem=("arbitrary",))


def ffn_bwd_weights(dxo, abh, dab, n, name):
    _, S, F = abh.shape
    D = dxo.shape[1]
    nf = F // FFN_TF
    tk = WGRAD_TK
    nk = S // tk
    gshape = (N_DEV, 3 * FFN_SHARD, D)

    def dw2_body(h_ref, d_ref, o_ref, acc_ref):
        k = pl.program_id(1)
        p = _dot(h_ref[...], d_ref[...].astype(BF16), 0, 0)

        @pl.when(k == 0)
        def _():
            acc_ref[...] = p

        @pl.when(k > 0)
        def _():
            acc_ref[...] += p

        @pl.when(k == nk - 1)
        def _():
            o_ref[...] = (0.5 * acc_ref[...]).astype(BF16).reshape(4, FFN_SHARD, D)

    gw = _call(name + "_dw2", dw2_body, (nf, nk),
               [(abh, (None, tk, FFN_TF), lambda j, k: (2, k, j)), (dxo, (tk, D), lambda j, k: (k, 0))],
               [(gshape, BF16, (4, FFN_SHARD, D), lambda j, k: (j, 2, 0))],
               scratch=[pltpu.VMEM((FFN_TF, D), F32)], sem=("parallel", "arbitrary"))[0]

    def dw13_body(gw_ref, dab_ref, n_ref, o_ref):
        o_ref[...] = _dot(dab_ref[...], n_ref[...], 0, 0).astype(BF16).reshape(4, FFN_SHARD, D)

    gw = pl.pallas_call(
        dw13_body,
        out_shape=jax.ShapeDtypeStruct(gshape, BF16),
        grid=(2, nf),
        in_specs=[pl.BlockSpec(memory_space=pl.ANY),
                  pl.BlockSpec((None, S, FFN_TF), lambda w, j: (w, 0, j)),
                  pl.BlockSpec((S, D), lambda w, j: (0, 0))],
        out_specs=pl.BlockSpec((4, FFN_SHARD, D), lambda w, j: (j, w, 0)),
        input_output_aliases={0: 0},
        name=name + "_dw13",
        compiler_params=pltpu.CompilerParams(dimension_semantics=("parallel", "parallel"),
                                             vmem_limit_bytes=VMEM_LIMIT),
    )(gw, dab, n)
    return gw


def ffn_bwd_input(dab, G, x_in, g, dxo, name):
    _, S, F = dab.shape
    D = x_in.shape[1]
    tm = 256

    def dn_body(dab_ref, w1_ref, w3_ref, x_ref, d_ref, g_ref, dx_ref, dg_ref):
        i = pl.program_id(0)
        dn = _dot(dab_ref[0], w1_ref[...].reshape(F, D)) + _dot(dab_ref[1], w3_ref[...].reshape(F, D))
        dx, dgt = _rms_bwd_tile(dn, x_ref[...], g_ref[...])
        dx_ref[...] = d_ref[...] + dx
        dgp = jnp.sum(dgt, axis=0, keepdims=True)

        @pl.when(i == 0)
        def _():
            dg_ref[...] = dgp

        @pl.when(i > 0)
        def _():
            dg_ref[...] += dgp

    tile = ((tm, D), lambda i: (i, 0))
    return _call(name + "_dn", dn_body, (S // tm,),
                 [(dab, (2, tm, F), lambda i: (0, i, 0)),
                  _ffn_whole_w_spec(G, 0), _ffn_whole_w_spec(G, 1),
                  (x_in,) + tile, (dxo,) + tile, (g, (1, D), lambda i: (0, 0))],
                 [((S, D), F32) + tile, ((1, D), F32, (1, D), lambda i: (0, 0))],
                 sem=("arbitrary",))


PROJ_TN = 512
DH_SHARDS_PER_STEP = 4


def in_proj(h, Gm, first_tile, n_tiles, name, tile_stride=1):
    S, D = h.shape
    tile = lambda j: first_tile + tile_stride * j

    def body(h_ref, w_ref, o_ref):
        o_ref[...] = _dot(h_ref[...], w_ref[...]).astype(BF16)

    return _call(name, body, (n_tiles,),
                 [(h, (S, D), lambda j: (0, 0)),
                  (Gm, (None, D, PROJ_TN), lambda j: (tile(j) // 2, 0, tile(j) % 2))],
                 [((S, n_tiles * PROJ_TN), BF16, (S, PROJ_TN), lambda j: (0, j))],
                 sem=("parallel",))[0]


def _dproj_pieces(dqkv, dq_b, dkv_b, dgate):
    pieces = [(dqkv[g], [(3 * which + g, (which, 0)) for which in range(3)]) for g in range(3)]
    pieces.append((dq_b, [(9, (None, 0)), (10, (None, 1))]))
    pieces.append((dkv_b, [(11, (None, 0))]))
    pieces.append((dgate, [(12 + 2 * a + b, (a, b)) for a in range(2) for b in range(2)]))
    return pieces


def in_proj_bwd_dw(pieces, h, gm_grads, name):
    S, D = h.shape

    for n_piece, (arr, tiles) in enumerate(pieces):
        w_tile = [t for t, _ in tiles]
        lead = [ix[0] for _, ix in tiles]
        colb = [ix[1] for _, ix in tiles]

        def pick(table, j):
            out = table[-1]
            for k in range(len(table) - 2, -1, -1):
                out = jnp.where(j == k, table[k], out)
            return out

        def dw_body(gm_ref, h_ref, d_ref, o_ref):
            o_ref[...] = _dot(h_ref[...], d_ref[...], 0, 0).astype(BF16)

        if arr.ndim == 3:
            d_spec = pl.BlockSpec((None, S, PROJ_TN), lambda j, lead=lead, colb=colb: (pick(lead, j), 0, pick(colb, j)))
        else:
            d_spec = pl.BlockSpec((S, PROJ_TN), lambda j, colb=colb: (0, pick(colb, j)))
        gm_grads = pl.pallas_call(
            dw_body,
            out_shape=jax.ShapeDtypeStruct(gm_grads.shape, BF16),
            grid=(len(tiles),),
            in_specs=[pl.BlockSpec(memory_space=pl.ANY), pl.BlockSpec((S, D), lambda j: (0, 0)), d_spec],
            out_specs=pl.BlockSpec((None, D, PROJ_TN),
                                   lambda j, w_tile=w_tile: (pick(w_tile, j) // 2, 0, pick(w_tile, j) % 2)),
            input_output_aliases={0: 0},
            name="%s_dw%d" % (name, n_piece),
            compiler_params=pltpu.CompilerParams(dimension_semantics=("parallel",), vmem_limit_bytes=VMEM_LIMIT),
        )(gm_grads, h, arr)
    return gm_grads


def in_proj_bwd_dh(pieces, Gm, x_in, g, dres, name):
    S, D = x_in.shape
    tm = 256
    C = Gm.shape[2]
    n_sh = N_DEV
    n_p = len(pieces)

    def dh_body(*refs):
        d_refs = refs[:n_p]
        w_ref, x_ref, r_ref, g_ref, dx_ref, dxb_ref, dg_ref = refs[n_p:]
        i = pl.program_id(0)
        p = None
        for d_ref, (arr, tiles) in zip(d_refs, pieces):
            for t, (lead, colb) in tiles:
                cols = slice(colb * PROJ_TN, (colb + 1) * PROJ_TN)
                d = d_ref[:, cols] if lead is None else d_ref[lead, :, cols]
                wcol = (t % 2) * PROJ_TN
                term = _dot(d, w_ref[t // 2, :, wcol:wcol + PROJ_TN], 1, 1)
                p = term if p is None else p + term
        dx, dgt = _rms_bwd_tile(p, x_ref[...], g_ref[...])
        dx = r_ref[...] + dx
        dx_ref[...] = dx
        dxb_ref[...] = dx.astype(BF16)
        dgp = jnp.sum(dgt, axis=0, keepdims=True)

        @pl.when(i == 0)
        def _():
            dg_ref[...] = dgp

        @pl.when(i > 0)
        def _():
            dg_ref[...] += dgp

    tile = ((tm, D), lambda i: (i, 0))

    def rows_of(arr):
        if arr.ndim == 3:
            return (arr, (arr.shape[0], tm, arr.shape[2]), lambda i: (0, i, 0))
        return (arr, (tm, arr.shape[1]), lambda i: (i, 0))

    return _call(name + "_dh", dh_body, (S // tm,),
                 [rows_of(arr) for arr, _ in pieces]
                 + [(Gm, (n_sh, D, C), lambda i: (0, 0, 0), pl.Buffered(1)),
                    (x_in,) + tile, (dres,) + tile, (g, (1, D), lambda i: (0, 0))],
                 [((S, D), F32) + tile, ((S, D), BF16) + tile, ((1, D), F32, (1, D), lambda i: (0, 0))],
                 sem=("arbitrary",))


def _t5_bucket(rel):
    n = N_BUCKETS // 2
    max_exact = n // 2
    ret = jnp.where(rel > 0, n, 0)
    a = jnp.abs(rel)
    af = jnp.maximum(a, 1).astype(F32)
    large = max_exact + (jnp.log(af / max_exact) / math.log(MAX_DISTANCE / max_exact)
                         * (n - max_exact)).astype(jnp.int32)
    large = jnp.minimum(large, n - 1)
    return ret + jnp.where(a < max_exact, a, large)


def _bucket_tables():
    qi = jnp.arange(A_TQ, dtype=jnp.int32)[:, None]
    kj = jnp.arange(A_WIN, dtype=jnp.int32)[None, :]
    rel = kj - HALF_WINDOW - qi
    return jnp.stack([_t5_bucket(rel * d) for d in DILATIONS], axis=0)


def bias_build(rel_bias, buckets):
    def body(tab_ref, bk_ref, o_ref):
        col = pl.program_id(0) * HEADS_PER_GROUP_A + pl.program_id(1)
        bk = bk_ref[...]
        acc = jnp.zeros(bk.shape, F32)
        for b in range(N_BUCKETS):
            acc = jnp.where(bk == b, tab_ref[b, col], acc)
        qi = lax.broadcasted_iota(jnp.int32, bk.shape, 0)
        kj = lax.broadcasted_iota(jnp.int32, bk.shape, 1)
        band = jnp.where(jnp.abs(kj - HALF_WINDOW - qi) <= HALF_WINDOW, acc, NEG_INF)
        o_ref[0] = jnp.where(kj >= HALF_WINDOW, band, NEG_INF)
        o_ref[1] = band
        o_ref[2] = jnp.where(kj < A_TQ + HALF_WINDOW, band, NEG_INF)

    out = pl.pallas_call(
        body,
        out_shape=jax.ShapeDtypeStruct((3, HEADS_PER_GROUP_A // 2, 3, 2, A_TQ, A_WIN), F32),
        grid=(3, HEADS_PER_GROUP_A),
        in_specs=[pl.BlockSpec(memory_space=pltpu.SMEM),
                  pl.BlockSpec((None, A_TQ, A_WIN), lambda g, h: (g, 0, 0))],
        out_specs=pl.BlockSpec((None, None, 3, None, A_TQ, A_WIN), lambda g, h: (g, h // 2, 0, h % 2, 0, 0)),
        name="a_bias_build",
        compiler_params=pltpu.CompilerParams(dimension_semantics=("parallel", "parallel")),
    )(rel_bias, buckets)
    return out.reshape(3, HEADS_PER_GROUP_A // 2, 3, 2 * A_TQ, A_WIN)


def bias_bwd(dbias, buckets):
    def body(d_ref, bk_ref, o_ref):
        bk = bk_ref[...]
        dv = d_ref[...]
        for b in range(N_BUCKETS):
            part = jnp.sum(jnp.where(bk == b, dv, 0.0), axis=1, keepdims=True)
            o_ref[b:b + 1, :] = jnp.broadcast_to(jnp.sum(part, axis=0, keepdims=True), (1, LANES))

    out = pl.pallas_call(
        body,
        out_shape=jax.ShapeDtypeStruct((3, HEADS_PER_GROUP_A, N_BUCKETS, LANES), F32),
        grid=(3, HEADS_PER_GROUP_A),
        in_specs=[pl.BlockSpec((None, None, A_TQ, A_WIN), lambda g, h: (g, h, 0, 0)),
                  pl.BlockSpec((None, A_TQ, A_WIN), lambda g, h: (g, 0, 0))],
        out_specs=pl.BlockSpec((None, None, N_BUCKETS, LANES), lambda g, h: (g, h, 0, 0)),
        name="a_bias_bwd",
        compiler_params=pltpu.CompilerParams(dimension_semantics=("parallel", "parallel")),
    )(dbias, buckets)
    return out[:, :, :, 0].transpose(2, 0, 1).reshape(N_BUCKETS, 3 * HEADS_PER_GROUP_A)


def _a_fill_padded(pad_ref, src_ref, n, pad):
    zeros = jnp.zeros((pad, LANES), pad_ref.dtype)
    pad_ref[0:pad, :] = zeros
    pad_ref[pad + n:2 * pad + n, :] = zeros
    pad_ref[pad:pad + n, :] = src_ref[...].astype(pad_ref.dtype)


def _a_stack_heads(x, lane):
    zero = jnp.zeros_like(x)
    return jnp.concatenate([jnp.where(lane < HEAD_DIM_A, x, zero), jnp.where(lane >= HEAD_DIM_A, x, zero)], axis=0)


def _a_bias_variant(qb, nqb):
    return jnp.where(qb == 0, 0, jnp.where(qb == nqb - 1, 2, 1))


def a_fwd(proj_g, bias, g, name):
    S = proj_g.shape[0]
    d = DILATIONS[g]
    L = S // d
    nqb = L // A_TQ
    pad = HALF_WINDOW * d

    def body(q_ref, k_ref, v_ref, b_ref, o_ref, l_ref, qf, kpad, vpad):
        qf[...] = q_ref[...].astype(F32) * A_SCALE
        _a_fill_padded(kpad, k_ref, S, pad)
        _a_fill_padded(vpad, v_ref, S, pad)
        lane = lax.broadcasted_iota(jnp.int32, (A_TQ, LANES), 1)

        def block(t, carry):
            qb, r = t // d, t % d
            start = qb * (A_TQ * d) + r
            kw = kpad[pl.ds(start, A_WIN, stride=d), :].astype(BF16)
            vw = vpad[pl.ds(start, A_WIN, stride=d), :].astype(BF16)
            q = qf[pl.ds(start, A_TQ, stride=d), :].astype(BF16)
            q2 = _a_stack_heads(q, lane)
            s = _dot(q2, kw, 1, 1) + b_ref[_a_bias_variant(qb, nqb)]
            m = jnp.max(s, axis=-1, keepdims=True)
            e = jnp.exp(s - m)
            l = jnp.sum(e, axis=-1, keepdims=True)
            o2 = _dot(e.astype(BF16), vw) / l
            lse2 = m + jnp.log(l)
            o_ref[pl.ds(start, A_TQ, stride=d), :] = jnp.where(lane < HEAD_DIM_A, o2[0:A_TQ], o2[A_TQ:])
            l_ref[pl.ds(start, A_TQ, stride=d), :] = jnp.where(lane < HEAD_DIM_A, lse2[0:A_TQ], lse2[A_TQ:])
            return carry

        lax.fori_loop(0, nqb * d, block, 0, unroll=A_UNROLL)

    out_spec = ((S, GROUP_WIDTH_A), F32, (S, LANES), lambda hp: (0, hp))
    return _call(name, body, (4,),
                 [(proj_g, (S, LANES), lambda hp: (0, hp)),
                  (proj_g, (S, LANES), lambda hp: (0, 4 + hp)),
                  (proj_g, (S, LANES), lambda hp: (0, 8 + hp)),
                  (bias, (None, None, 3, 2 * A_TQ, A_WIN), lambda hp: (g, hp, 0, 0, 0))],
                 [out_spec, out_spec],
                 scratch=[pltpu.VMEM((S, LANES), F32)] + [pltpu.VMEM((S + 2 * pad, LANES), F32)] * 2,
                 sem=("parallel",))


def a_combine(outs, lses, name):
    S, W = outs[0].shape
    tr = 512

    def body(o0, o1, o2, l0, l1, l2, oa_ref, lt_ref):
        a, b, c = l0[...], l1[...], l2[...]
        m = jnp.maximum(jnp.maximum(a, b), c)
        ea, eb, ec = jnp.exp(a - m), jnp.exp(b - m), jnp.exp(c - m)
        z = ea + eb + ec
        oa_ref[...] = ((ea * o0[...] + eb * o1[...] + ec * o2[...]) / z).astype(BF16)
        lt_ref[...] = m + jnp.log(z)

    spec = ((tr, W), lambda i: (i, 0))
    return _call(name, body, (S // tr,), [(a,) + spec for a in (*outs, *lses)],
                 [((S, W), BF16) + spec, ((S, W), F32) + spec], sem=("parallel",))


def a_bwd(proj_g, bias, do_a, o_a, lse_tot, g, name):
    S = proj_g.shape[0]
    d = DILATIONS[g]
    L = S // d
    nqb = L // A_TQ
    pad = HALF_WINDOW * d

    def body(q_ref, k_ref, v_ref, b_ref, do_ref, o_ref, l_ref, dqkv_ref, db_ref,
             qf, of, dqf, kpad, vpad, dkacc, dvacc):
        qf[...] = q_ref[...].astype(F32) * A_SCALE
        of[...] = o_ref[...].astype(F32)
        _a_fill_padded(kpad, k_ref, S, pad)
        _a_fill_padded(vpad, v_ref, S, pad)
        dkacc[...] = jnp.zeros(dkacc.shape, F32)
        dvacc[...] = jnp.zeros(dvacc.shape, F32)
        db_ref[...] = jnp.zeros(db_ref.shape, F32)
        lane = lax.broadcasted_iota(jnp.int32, (A_TQ, LANES), 1)

        def block(t, carry):
            qb, r = t // d, t % d
            start = qb * (A_TQ * d) + r
            rows = pl.ds(start, A_TQ, stride=d)
            win = pl.ds(start, A_WIN, stride=d)
            kw = kpad[win, :].astype(BF16)
            vw = vpad[win, :].astype(BF16)
            q = qf[rows, :].astype(BF16)
            do = do_ref[rows, :]
            ov = of[rows, :]
            lt = l_ref[rows, :]
            q2 = _a_stack_heads(q, lane)
            do2 = _a_stack_heads(do, lane)
            lt2 = jnp.concatenate([lt[:, 0:1], lt[:, HEAD_DIM_A:HEAD_DIM_A + 1]], axis=0)
            s = _dot(q2, kw, 1, 1) + b_ref[_a_bias_variant(qb, nqb)]
            p = jnp.exp(s - lt2)
            t = jnp.sum(do2 * jnp.concatenate([ov, ov], axis=0), axis=-1, keepdims=True)
            dob2 = do2.astype(BF16)
            ds = p * (_dot(dob2, vw, 1, 1) - t)
            db_ref[...] += ds
            dsb = ds.astype(BF16)
            dq2 = _dot(dsb, kw)
            dqf[rows, :] = jnp.where(lane < HEAD_DIM_A, dq2[0:A_TQ], dq2[A_TQ:]) * A_SCALE
            dkacc[win, :] += _dot(dsb, q2, 0, 0)
            dvacc[win, :] += _dot(p.astype(BF16), dob2, 0, 0)
            return carry

        lax.fori_loop(0, nqb * d, block, 0, unroll=A_UNROLL)
        dqkv_ref[0] = dqf[...].astype(BF16)
        dqkv_ref[1] = dkacc[pad:pad + S, :].astype(BF16)
        dqkv_ref[2] = dvacc[pad:pad + S, :].astype(BF16)

    slab = ((S, LANES), lambda hp: (0, hp))
    padded = pltpu.VMEM((S + 2 * pad, LANES), F32)
    return _call(
        name, body, (4,),
        [(proj_g, (S, LANES), lambda hp: (0, hp)),
         (proj_g, (S, LANES), lambda hp: (0, 4 + hp)),
         (proj_g, (S, LANES), lambda hp: (0, 8 + hp)),
         (bias, (None, None, 3, 2 * A_TQ, A_WIN), lambda hp: (g, hp, 0, 0, 0)),
         (do_a,) + slab, (o_a,) + slab, (lse_tot,) + slab],
        [((3, S, GROUP_WIDTH_A), BF16, (3, S, LANES), lambda hp: (0, 0, hp)),
         ((4, 2 * A_TQ, A_WIN), F32, (None, 2 * A_TQ, A_WIN), lambda hp: (hp, 0, 0))],
        scratch=[pltpu.VMEM((S, LANES), F32)] * 3 + [padded] * 4,
        sem=("parallel",))


def _rope_tables(S):
    rows = S // GRID_W
    row = jnp.repeat(jnp.arange(rows, dtype=F32), GRID_W)
    col = jnp.tile(jnp.arange(GRID_W, dtype=F32), rows)
    n_freq = HEAD_DIM_B // 4
    freq = ROPE_THETA ** (-jnp.arange(n_freq, dtype=F32) / n_freq)
    ang = jnp.concatenate([row[:, None] * freq, col[:, None] * freq], axis=-1)
    cos, sin = jnp.cos(ang), jnp.sin(ang)
    return jnp.repeat(cos, 2, axis=-1), jnp.stack([-sin, sin], axis=-1).reshape(S, HEAD_DIM_B)


def _swap_pairs(y):
    lane = lax.broadcasted_iota(jnp.int32, y.shape, 1)
    return jnp.where(lane % 2 == 0, pltpu.roll(y, LANES - 1, 1), pltpu.roll(y, 1, 1))


def qkv_prep(proj_b, gains, cos_t, sin_t, name):
    S = proj_b.shape[0]
    ts = 256
    n_rot = N_HEADS_B + N_KV_B
    nh = n_rot + N_KV_B
    W = nh * LANES

    def body(x_ref, g_ref, c_ref, s_ref, o_ref):
        cv, sv = c_ref[...], s_ref[...]
        for hb in range(nh):
            cols = slice(hb * LANES, (hb + 1) * LANES)
            if hb < n_rot:
                xv = x_ref[:, cols].astype(F32)
                r = lax.rsqrt(jnp.mean(xv * xv, axis=-1, keepdims=True) + EPS)
                yv = xv * r * g_ref[:, cols]
                o_ref[:, cols] = (yv * cv + _swap_pairs(yv) * sv).astype(BF16)
            else:
                o_ref[:, cols] = x_ref[:, cols]

    return _call(name, body, (S // ts,),
                 [(proj_b, (ts, W), lambda i: (i, 0)), (gains, (1, W), lambda i: (0, 0)),
                  (cos_t, (ts, LANES), lambda i: (i, 0)), (sin_t, (ts, LANES), lambda i: (i, 0))],
                 [((S, W), BF16, (ts, W), lambda i: (i, 0))],
                 sem=("parallel",))[0]


def qk_prep_bwd(dr, proj_b, col0, gain, cos_t, sin_t, name):
    S, W = dr.shape
    H = W // LANES
    ts = 256
    xb = (col0 * LANES) // W

    def body(d_ref, x_ref, g_ref, c_ref, s_ref, dx_ref, dg_ref):
        i = pl.program_id(0)
        cv, sv, gv = c_ref[...], s_ref[...], g_ref[...]
        dgp = jnp.zeros((1, LANES), F32)
        for hb in range(H):
            cols = slice(hb * LANES, (hb + 1) * LANES)
            dout = d_ref[:, cols]
            dy = dout * cv + _swap_pairs(dout * sv)
            dx, dgt = _rms_bwd_tile(dy, x_ref[:, cols].astype(F32), gv)
            dx_ref[:, cols] = dx.astype(BF16)
            dgp = dgp + jnp.sum(dgt, axis=0, keepdims=True)

        @pl.when(i == 0)
        def _():
            dg_ref[...] = dgp

        @pl.when(i > 0)
        def _():
            dg_ref[...] += dgp

    return _call(name, body, (S // ts,),
                 [(dr, (ts, W), lambda i: (i, 0)), (proj_b, (ts, W), lambda i: (i, xb)),
                  (gain, (1, LANES), lambda i: (0, 0)),
                  (cos_t, (ts, LANES), lambda i: (i, 0)), (sin_t, (ts, LANES), lambda i: (i, 0))],
                 [((S, W), BF16, (ts, W), lambda i: (i, 0)),
                  ((1, LANES), F32, (1, LANES), lambda i: (0, 0))],
                 sem=("arbitrary",))


def _row_sums(x):
    hi = x.astype(BF16)
    lo = (x - hi.astype(F32)).astype(BF16)
    ones = jnp.ones((8, LANES), BF16)
    return (_dot(ones, hi, 1, 1) + _dot(ones, lo, 1, 1))[0:1, :]


def flash_fwd(qkv, name):
    S = qkv.shape[0]
    tq = B_TQ_FWD
    scale = HEAD_DIM_B ** -0.5

    hps = B_HEADS_PER_STEP

    def body(q_ref, k_ref, v_ref, o_ref, l_ref):
        k, v = k_ref[...], v_ref[...]
        for j in range(hps):
            cols = slice(j * LANES, (j + 1) * LANES)
            s = _dot(q_ref[:, cols], k, 1, 1)
            m = jnp.max(s, axis=-1, keepdims=True)
            e = jnp.exp2((s - m) * (scale * LOG2E))
            l = jnp.sum(e, axis=-1, keepdims=True)
            o_ref[:, cols] = (_dot(e.astype(BF16), v) / l).astype(BF16)
            lse = jnp.broadcast_to(m * scale + jnp.log(l), (tq, LANES))
            l_ref[j] = _row_sums(lse) * (1.0 / LANES)

    per = GQA_GROUP_B // hps
    heads = lambda g, h, i: (i, g * per + h)
    return _call(name, body, (N_KV_B, per, S // tq),
                 [(qkv, (tq, hps * LANES), heads),
                  (qkv, (S, LANES), lambda g, h, i: (0, N_HEADS_B + g)),
                  (qkv, (S, LANES), lambda g, h, i: (0, N_HEADS_B + N_KV_B + g))],
                 [((S, N_HEADS_B * LANES), BF16, (tq, hps * LANES), heads),
                  ((N_HEADS_B, 1, S), F32, (hps, 1, tq), lambda g, h, i: (g * per + h, 0, i))],
                 sem=("parallel", "parallel", "parallel"))


def flash_bwd(qkv, k_t, do_b, o_b, lse, name):
    S = qkv.shape[0]
    tq = B_TQ_BWD
    nq = S // tq
    scale = HEAD_DIM_B ** -0.5

    def body(q_ref, k_ref, v_ref, kt_ref, do_ref, o_ref, l_ref, dq_ref, dk_ref, dv_ref, dkacc, dvacc):
        h, i = pl.program_id(1), pl.program_id(2)

        @pl.when((h == 0) & (i == 0))
        def _():
            dkacc[...] = jnp.zeros(dkacc.shape, F32)
            dvacc[...] = jnp.zeros(dvacc.shape, F32)

        q = q_ref[...]
        dob = do_ref[...]
        t = _row_sums(dob.astype(F32) * o_ref[...].astype(F32))
        pt = jnp.exp2(_dot(k_ref[...], q, 1, 1) * (scale * LOG2E) - l_ref[...] * LOG2E)
        dsb = (pt * (_dot(v_ref[...], dob, 1, 1) - t)).astype(BF16)
        dvacc[...] += _dot(pt.astype(BF16), dob)
        dkacc[...] += _dot(dsb, q)
        dq_ref[...] = _dot(kt_ref[...], dsb).T * scale

        @pl.when((h == GQA_GROUP_B - 1) & (i == nq - 1))
        def _():
            dk_ref[...] = dkacc[...] * scale
            dv_ref[...] = dvacc[...].astype(BF16)

    head = lambda g, h, i: (i, g * GQA_GROUP_B + h)
    return _call(name, body, (N_KV_B, GQA_GROUP_B, nq),
                 [(qkv, (tq, LANES), head),
                  (qkv, (S, LANES), lambda g, h, i: (0, N_HEADS_B + g)),
                  (qkv, (S, LANES), lambda g, h, i: (0, N_HEADS_B + N_KV_B + g)),
                  (k_t, (LANES, S), lambda g, h, i: (g, 0)),
                  (do_b, (tq, LANES), head), (o_b, (tq, LANES), head),
                  (lse, (None, 1, tq), lambda g, h, i: (g * GQA_GROUP_B + h, 0, i))],
                 [((S, N_HEADS_B * LANES), F32, (tq, LANES), head),
                  ((S, N_KV_B * LANES), F32, (S, LANES), lambda g, h, i: (0, g)),
                  ((S, N_KV_B * LANES), BF16, (S, LANES), lambda g, h, i: (0, g))],
                 scratch=[pltpu.VMEM((S, LANES), F32)] * 2,
                 sem=("parallel", "arbitrary", "arbitrary"))


MERGE_TN = 512


def _mix_rows_spec(Gm, row0, n_slots, slot_map, cols=None, col_map=None):
    C = Gm.shape[2] if cols is None else cols
    cm = (lambda *idx: 0) if col_map is None else col_map
    return (Gm, (n_slots, LANES, C), lambda *idx: (slot_map(*idx), row0 // LANES, cm(*idx)))


def _gate_specs(proj_b, tm):
    first = PB_GATE_A // MERGE_TN
    return [(proj_b, (tm, MERGE_TN), lambda i, k=k: (i, first + k)) for k in range(4)]


def _whole_rows_spec(Gm, row0):
    return _mix_rows_spec(Gm, row0, N_DEV, lambda *idx: 0)


def merge_fwd(o_a, o_b, w_a, Gm, proj_b, b_gate, x, name):
    S, D = x.shape
    tm = 256

    def body(oa_ref, ob_ref, wa_ref, wb_ref, wo_ref, g0, g1, g2, g3, bg_ref, x_ref, m_ref, ya_ref, yb_ref, xo_ref):
        ya = _dot(oa_ref[...], wa_ref[...])
        yb = _dot(ob_ref[...], wb_ref[...].reshape(N_DEV * LANES, D))
        ga = _sigmoid(jnp.concatenate([g0[...], g1[...]], axis=1).astype(F32) + bg_ref[:, 0:D])
        gb = _sigmoid(jnp.concatenate([g2[...], g3[...]], axis=1).astype(F32) + bg_ref[:, D:2 * D])
        merged = (ga * ya + gb * yb).astype(BF16)
        m_ref[...] = merged
        ya_ref[...] = ya.astype(BF16)
        yb_ref[...] = yb.astype(BF16)
        xo_ref[...] = x_ref[...] + _dot(merged, wo_ref[...].reshape(N_DEV * LANES, D))

    rows = lambda a: (a, (tm, a.shape[1]), lambda i: (i, 0))
    out = ((S, D), BF16, (tm, D), lambda i: (i, 0))
    return _call(name, body, (S // tm,),
                 [rows(o_a), rows(o_b), (w_a, w_a.shape, lambda i: (0, 0)),
                  _whole_rows_spec(Gm, REST_WB), _whole_rows_spec(Gm, REST_WOUT)]
                 + _gate_specs(proj_b, tm) + [(b_gate, (1, 2 * D), lambda i: (0, 0)), rows(x)],
                 [out, out, out, ((S, D), F32, (tm, D), lambda i: (i, 0))], sem=("parallel",))


def merge_bwd(dx2, w_a, Gm, ya, yb, proj_b, b_gate, name):
    S, D = dx2.shape
    tm = 256

    def body(d_ref, wo_ref, wa_ref, wb_ref, ya_ref, yb_ref, g0, g1, g2, g3, bg_ref,
             dya_ref, dyb_ref, dg_ref, dbg_ref, doa_ref, dob_ref):
        i = pl.program_id(0)
        dm = _dot(d_ref[...].astype(BF16), wo_ref[...].reshape(N_DEV * LANES, D), 1, 1)
        ga = _sigmoid(jnp.concatenate([g0[...], g1[...]], axis=1).astype(F32) + bg_ref[:, 0:D])
        gb = _sigmoid(jnp.concatenate([g2[...], g3[...]], axis=1).astype(F32) + bg_ref[:, D:2 * D])
        dya = (dm * ga).astype(BF16)
        dyb = (dm * gb).astype(BF16)
        dya_ref[...] = dya
        dyb_ref[...] = dyb
        dpa = dm * ya_ref[...].astype(F32) * ga * (1.0 - ga)
        dpb = dm * yb_ref[...].astype(F32) * gb * (1.0 - gb)
        dg_ref[0] = dpa.astype(BF16)
        dg_ref[1] = dpb.astype(BF16)
        doa_ref[...] = _dot(dya, wa_ref[...], 1, 1)
        dob_ref[...] = _dot(dyb, wb_ref[...].reshape(N_DEV * LANES, D), 1, 1).astype(BF16)
        sa =jnp.sum(dpa, axis=0, keepdims=True)
        sb = jnp.sum(dpb, axis=0, keepdims=True)

        @pl.when(i == 0)
        def _():
            dbg_ref[0] = sa
            dbg_ref[1] = sb

        @pl.when(i > 0)
        def _():
            dbg_ref[0] += sa
            dbg_ref[1] += sb

    tile = ((tm, D), lambda i: (i, 0))
    return _call(
        name, body, (S // tm,),
        [(dx2,) + tile, _whole_rows_spec(Gm, REST_WOUT), (w_a, w_a.shape, lambda i: (0, 0)),
         _whole_rows_spec(Gm, REST_WB), (ya,) + tile, (yb,) + tile]
        + _gate_specs(proj_b, tm) + [(b_gate, (1, 2 * D), lambda i: (0, 0))],
        [((S, D), BF16) + tile, ((S, D), BF16) + tile,
         ((2, S, D), BF16, (2, tm, D), lambda i: (0, i, 0)),
         ((2, 1, D), F32, (2, 1, D), lambda i: (0, 0, 0)),
         ((S, w_a.shape[0]), F32, (tm, w_a.shape[0]), lambda i: (i, 0)),
         ((S, N_HEADS_B * LANES), BF16, (tm, N_HEADS_B * LANES), lambda i: (i, 0))],
        sem=("arbitrary",))


def weight_grad_rows(a, b, grads, row0, name):
    S, M = a.shape
    N = b.shape[1]
    tmm = 512
    tk = WGRAD_TK
    nk = S // tk
    prior = [] if grads is None else [grads]

    def body(*refs):
        a_ref, b_ref, o_ref, acc_ref = refs[len(prior):]
        k = pl.program_id(1)
        p = _dot(a_ref[...], b_ref[...].astype(BF16), 0, 0)

        @pl.when(k == 0)
        def _():
            acc_ref[...] = p

        @pl.when(k > 0)
        def _():
            acc_ref[...] += p

        @pl.when(k == nk - 1)
        def _():
            o_ref[...] = acc_ref[...].astype(BF16).reshape(tmm // LANES, LANES, N)

    return pl.pallas_call(
        body,
        out_shape=jax.ShapeDtypeStruct((N_DEV, MIX_ROWS, N), BF16),
        grid=(M // tmm, nk),
        in_specs=[pl.BlockSpec(memory_space=pl.ANY)] * len(prior)
        + [pl.BlockSpec((tk, tmm), lambda j, k: (k, j)),
           pl.BlockSpec((tk, N), lambda j, k: (k, 0))],
        out_specs=pl.BlockSpec((tmm // LANES, LANES, N), lambda j, k: (j, row0 // LANES, 0)),
        scratch_shapes=[pltpu.VMEM((tmm, N), F32)],
        input_output_aliases={0: 0} if prior else {},
        name=name,
        compiler_params=pltpu.CompilerParams(dimension_semantics=("parallel", "arbitrary"),
                                             vmem_limit_bytes=VMEM_LIMIT),
    )(*prior, a, b)


def weight_grad_plain(a, b, name):
    S, M = a.shape
    N = b.shape[1]
    tk = WGRAD_TK
    nk = S // tk

    def body(a_ref, b_ref, o_ref, acc_ref):
        k = pl.program_id(0)
        p = _dot(a_ref[...], b_ref[...], 0, 0)

        @pl.when(k == 0)
        def _():
            acc_ref[...] = p

        @pl.when(k > 0)
        def _():
            acc_ref[...] += p

        @pl.when(k == nk - 1)
        def _():
            o_ref[...] = acc_ref[...].astype(BF16)

    return _call(name, body, (nk,),
                 [(a, (tk, M), lambda k: (k, 0)), (b, (tk, N), lambda k: (k, 0))],
                 [((M, N), BF16, (M, N), lambda k: (0, 0))],
                 scratch=[pltpu.VMEM((M, N), F32)], sem=("arbitrary",))[0]


def local_step(x, tgt, p, get_g1_up, get_g1_down, get_gm_in, get_gm_rest, get_g2, emit, start_token):
    S, D = x.shape
    after = lambda t: t[0:1, 0:1]
    buckets = _bucket_tables()
    cos_t, sin_t = _rope_tables(S)
    gains = jnp.concatenate([jnp.tile(p["q_norm"], (1, N_HEADS_B)), jnp.tile(p["k_norm"], (1, N_KV_B)),
                             jnp.ones((1, N_KV_B * LANES), F32)], axis=1)

    n1 = rms_fwd(x, p["ffn1_norm"] + after(start_token), "ffn1_norm")
    bias = bias_build(p["rel_bias"] + after(start_token), buckets)
    g1_up = get_g1_up((n1, bias))
    ab1 = ffn_up(n1, (g1_up, None), "ffn1_up")
    G1 = (g1_up, get_g1_down(ab1))
    x1 = ffn_down(ab1, G1, x, "ffn1_down")

    hm = rms_fwd(x1, p["mix_norm"], "mix_norm")
    Gw = get_gm_in(hm)
    n_a = A_QKV_WIDTH // PROJ_TN
    proj_a = [in_proj(hm, Gw, g, 3, "in_proj_a%d" % g, tile_stride=3) for g in range(3)]
    proj_b = in_proj(hm, Gw, n_a, PB_WIDTH // PROJ_TN, "in_proj_b")

    outs, lses = [], []
    for g in range(3):
        o, l = a_fwd(proj_a[g], bias, g, "a_fwd_%d" % g)
        outs.append(o)
        lses.append(l)
    o_a, lse_tot = a_combine(outs, lses, "a_combine")

    qkv = qkv_prep(proj_b, gains, cos_t, sin_t, "qkv_prep")
    k_t = qkv[:, N_HEADS_B * LANES:(N_HEADS_B + N_KV_B) * LANES].T
    o_b, lse_b = flash_fwd(qkv, "flash_fwd")

    Gm = get_gm_rest(o_b)
    w_a = Gm[:, REST_WA:REST_ROWS, :].reshape(N_DEV, GROUP_WIDTH_A, LANES).transpose(1, 0, 2).reshape(GROUP_WIDTH_A, D)
    merged, ya, yb, x2 = merge_fwd(o_a, o_b, w_a, Gm, proj_b, p["b_gate"], x1, "merge_fwd")

    G2 = get_g2(x2)
    n2 = rms_fwd(x2, p["ffn2_norm"], "ffn2_norm")
    ab2, x3 = ffn_fwd(n2, G2, x2, "ffn2_fwd")

    loss, dx3, dx3_b, d_final = final_loss(x3, tgt, p["final_norm"], "final_loss")

    dab2, dx2, dx2_b, d_ffn2_norm = ffn_bwd_hidden_input(dx3_b, dx3, ab2, G2, x2, p["ffn2_norm"], "ffn2_bwd")
    gw2 = ffn_bwd_weights(dx3_b, ab2, dab2, n2, "ffn2_bwd")
    t2 = emit("ffn2", gw2)

    dya, dyb, dgate, dbg, do_a, do_b = merge_bwd(dx2_b, w_a, Gm, ya, yb, proj_b, p["b_gate"] + after(t2),
                                                 "merge_bwd")
    gm_grads = weight_grad_rows(merged, dx2_b, None, MIX_WOUT, "dw_out")
    gm_grads = weight_grad_rows(o_b, dyb, gm_grads, MIX_WB, "dw_branch_b")
    dw_a = weight_grad_plain(o_a, dya, "dw_branch_a")

    dq_r, dk_r, dv_b = flash_bwd(qkv, k_t, do_b, o_b, lse_b, "flash_bwd")
    dq_b, d_q_norm = qk_prep_bwd(dq_r, proj_b, 0, p["q_norm"], cos_t, sin_t, "q_prep_bwd")
    dk_b, d_k_norm = qk_prep_bwd(dk_r, proj_b, N_HEADS_B, p["k_norm"], cos_t, sin_t, "k_prep_bwd")

    dqkv, dbs = [], []
    for g in range(3):
        dg_, db = a_bwd(proj_a[g], bias, do_a, o_a, lse_tot, g, "a_bwd_%d" % g)
        dqkv.append(dg_)
        dbs.append(db)
    d_rel_bias = bias_bwd(jnp.stack(dbs, axis=0).reshape(3, HEADS_PER_GROUP_A, A_TQ, A_WIN), buckets)

    dproj = _dproj_pieces(dqkv, dq_b, jnp.concatenate([dk_b, dv_b], axis=1), dgate)
    gm_grads = in_proj_bwd_dw(dproj, hm, gm_grads, "in_proj_bwd")
    dw_a_sh = dw_a.reshape(GROUP_WIDTH_A, N_DEV, LANES).transpose(1, 0, 2).reshape(N_DEV, MIX_ROWS - MIX_WA, D)
    gm_grads = lax.dynamic_update_slice(gm_grads, dw_a_sh, (0, MIX_WA, 0))
    tm = emit("mix", gm_grads)
    dx1, dx1_b, d_mix_norm = in_proj_bwd_dh(dproj, Gw, x1, p["mix_norm"] + after(tm), dx2, "in_proj_bwd")

    dab1 = ffn_bwd_hidden(dx1_b, ab1, G1, "ffn1_bwd")
    gw1 = ffn_bwd_weights(dx1_b, ab1, dab1, n1, "ffn1_bwd")
    t1 = emit("ffn1", gw1)
    dx0, d_ffn1_norm = ffn_bwd_input(dab1, G1, x, p["ffn1_norm"] + after(t1), dx1, "ffn1_bwd")

    small = dict(ffn1_norm=d_ffn1_norm, mix_norm=d_mix_norm, b_gate=dbg.reshape(1, 2 * D),
                 q_norm=d_q_norm, k_norm=d_k_norm, rel_bias=d_rel_bias, ffn2_norm=d_ffn2_norm,
                 final_norm=d_final)
    return loss, dx0, small


def _pack_small(t, loss_row):
    row6 = jnp.concatenate([t["q_norm"].reshape(1, -1), t["k_norm"].reshape(1, -1), t["rel_bias"].reshape(1, -1)], axis=1)
    return jnp.concatenate([t["ffn1_norm"].reshape(1, -1), t["mix_norm"].reshape(1, -1), t["b_gate"].reshape(2, -1),
                            t["ffn2_norm"].reshape(1, -1), t["final_norm"].reshape(1, -1), row6, loss_row], axis=0)


def _unpack_small(a, shapes):
    return dict(ffn1_norm=a[0:1].reshape(shapes["ffn1_norm"]), mix_norm=a[1:2].reshape(shapes["mix_norm"]),
                b_gate=a[2:4].reshape(shapes["b_gate"]), ffn2_norm=a[4:5].reshape(shapes["ffn2_norm"]),
                final_norm=a[5].reshape(shapes["final_norm"]), q_norm=a[6:7, 0:128].reshape(shapes["q_norm"]),
                k_norm=a[6:7, 128:256].reshape(shapes["k_norm"]), rel_bias=a[6, 256:1024].reshape(shapes["rel_bias"]))


SMALL = ("ffn1_norm", "mix_norm", "b_gate", "q_norm", "k_norm", "rel_bias", "ffn2_norm", "final_norm")
ORDER = ("ffn1_norm", "ffn1_w1", "ffn1_w3", "ffn1_w2", "mix_norm", "w_in", "b_gate", "q_norm", "k_norm", "rel_bias",
         "w_branch_a", "w_branch_b", "w_out", "ffn2_norm", "ffn2_w1", "ffn2_w3", "ffn2_w2", "final_norm")


def kernel(x, ffn1_norm, ffn1_w1, ffn1_w3, ffn1_w2, mix_norm, w_in, b_gate, q_norm, k_norm, rel_bias, w_branch_a, w_branch_b, w_out, ffn2_norm, ffn2_w1, ffn2_w3, ffn2_w2, final_norm, loss_target, m_ffn1_norm, m_ffn1_w1, m_ffn1_w3, m_ffn1_w2, m_mix_norm, m_w_in, m_b_gate, m_q_norm, m_k_norm, m_rel_bias, m_w_branch_a, m_w_branch_b, m_w_out, m_ffn2_norm, m_ffn2_w1, m_ffn2_w3, m_ffn2_w2, m_final_norm, v_ffn1_norm, v_ffn1_w1, v_ffn1_w3, v_ffn1_w2, v_mix_norm, v_w_in, v_b_gate, v_q_norm, v_k_norm, v_rel_bias, v_w_branch_a, v_w_branch_b, v_w_out, v_ffn2_norm, v_ffn2_w1, v_ffn2_w3, v_ffn2_w2, v_final_norm):
    args = dict(locals())
    w = {n: args[n] for n in ORDER}
    m = {n: args["m_" + n] for n in ORDER}
    v = {n: args["v_" + n] for n in ORDER}
    D = x.shape[2]

    blocks = (
        ("ffn1_up", jnp.concatenate([ffn1_w1[0].T, ffn1_w3[0].T], axis=0)),
        ("ffn1_down", ffn1_w2[0]),
        ("mix_in", w_in[0]),
        ("mix_rest", jnp.concatenate([w_branch_b[0], w_out[0], w_branch_a[0].reshape(REST_ROWS - REST_WA, D)], axis=0)),
        ("ffn2", jnp.concatenate([ffn2_w1[0].T, ffn2_w3[0].T, ffn2_w2[0]], axis=0)),
    )
    direct = ("mix_rest", "ffn2")
    started = all_gather_start_all([(b.astype(BF16), tag in direct) for tag, b in blocks], "all_gather_start")
    gathers = {tag: s for (tag, _), s in zip(blocks, started)}
    start_token = started[0][4]

    def gathered(tag):
        def get(after):
            if tag in direct:
                return all_gather_place_own(*_split_wait("all_gather_" + tag + "_wait", gathers[tag], N_DEV - 1, after),
                                            "all_gather_" + tag + "_own")
            return all_gather_finish(*_split_wait("all_gather_" + tag + "_wait", gathers[tag], 4, after),
                                     "all_gather_" + tag + "_finish")
        return get

    core = lax.axis_index("c").astype(jnp.int32).reshape(1)
    chip = (2 * lax.axis_index("x") + lax.axis_index("y")).astype(jnp.int32).reshape(1)
    device = 2 * chip + core
    exchanges = {}

    def emit(tag, gw):
        if tag == "ffn1":
            (theirs,) = reduce_scatter_pair([gw], "reduce_scatter_pair_" + tag)
            part = pair_add(gw, theirs, core, "pair_add_" + tag)
            exchanges[tag] = reduce_scatter_start(part, "reduce_scatter_" + tag + "_start")
        else:
            exchanges[tag] = reduce_scatter_start_direct(gw, "reduce_scatter_" + tag + "_start")
        return exchanges[tag][4]

    small_p = dict(ffn1_norm=ffn1_norm, mix_norm=mix_norm, b_gate=b_gate, q_norm=q_norm, k_norm=k_norm,
                   rel_bias=rel_bias, ffn2_norm=ffn2_norm, final_norm=final_norm.reshape(1, D))
    loss_p, grad_x, small_g = local_step(x[0], loss_target[0], small_p, gathered("ffn1_up"), gathered("ffn1_down"),
                                         gathered("mix_in"), gathered("mix_rest"), gathered("ffn2"), emit, start_token)

    def landed(tag, after):
        n_others, me = (3, chip) if tag == "ffn1" else (N_DEV - 1, device)
        return tuple(_split_wait("reduce_scatter_" + tag + "_wait", exchanges[tag], n_others, after)) + (me,)

    grads, delta, new_m, new_v = {}, {}, {}, {}

    def finish(n, part, land, me, off, blk, transposed=False):
        shp = w[n].shape
        if transposed:
            to2 = lambda a: a.reshape(shp[-2], shp[-1]).T
            back = lambda a: a.T.reshape(shp)
        else:
            to2 = lambda a: a.reshape(shp[-2], shp[-1])
            back = lambda a: a.reshape(shp)
        res = sum_adamw(part, land, me, off, blk, to2(w[n]), to2(m[n]), to2(v[n]), "update_" + n)
        grads[n], delta[n], new_m[n], new_v[n] = [back(a) for a in res]

    last_token = exchanges["ffn1"][4]
    for tag, after in (("ffn2", last_token), ("ffn1", grad_x)):
        group = landed(tag, after)
        finish(tag + "_w1", *group, 0, FFN_SHARD, transposed=True)
        finish(tag + "_w3", *group, FFN_SHARD, FFN_SHARD, transposed=True)
        finish(tag + "_w2", *group, 2 * FFN_SHARD, FFN_SHARD)
        if tag == "ffn2":
            group_m = landed("mix", last_token)
            finish("w_in", *group_m, MIX_WIN, LANES)
            finish("w_branch_b", *group_m, MIX_WB, LANES)
            finish("w_out", *group_m, MIX_WOUT, LANES)
            grads["w_branch_a"] = sum_landed(*group_m, MIX_WA, MIX_ROWS - MIX_WA, MIX_ROWS - MIX_WA,
                                             "w_branch_a_sum").reshape(w_branch_a.shape)
    loss_row = jnp.pad(loss_p, ((0, 0), (0, D - LANES)))
    smalls = small_all_gather(_pack_small(small_g, loss_row))
    small_sum = sum_slots(smalls, 0, N_DEV, N_DEV, "small_sum")
    small_shapes = {n: w[n].shape for n in SMALL}
    grads.update(_unpack_small(small_sum, small_shapes))
    loss = small_sum[7, 0]

    n = "w_branch_a"
    two_d = lambda a: a.reshape(w[n].shape[-2], w[n].shape[-1])
    d_, m_, v_ = adamw(two_d(w[n]), two_d(grads[n]), two_d(m[n]), two_d(v[n]), "adamw_" + n)
    delta[n], new_m[n], new_v[n] = [a.reshape(w[n].shape) for a in (d_, m_, v_)]
    zero_row = jnp.zeros((1, D), F32)
    pack = lambda t: _pack_small({n: t[n] for n in SMALL}, zero_row)
    d_, m_, v_ = adamw(pack(w), small_sum, pack(m), pack(v), "adamw_small")
    for src, dst in ((d_, delta), (m_, new_m), (v_, new_v)):
        dst.update(_unpack_small(src, small_shapes))

    return (loss, grad_x[None], *[grads[n] for n in ORDER], *[delta[n] for n in ORDER],
            *[new_m[n] for n in ORDER], *[new_v[n] for n in ORDER])
```

```python
import math

import jax
import jax.numpy as jnp
from jax import lax
from jax.experimental import pallas as pl
from jax.experimental.pallas import tpu as pltpu

F32 = jnp.float32
BF16 = jnp.bfloat16
MESH = pl.DeviceIdType.MESH

V7X_VMEM_BYTES = 64 * 1024 * 1024
VMEM_LIMIT = V7X_VMEM_BYTES - 8 * 1024 * 1024
LANES = 128

N_DEV = 8
EPS = 1e-6
NEG_INF = -1e30

DILATIONS = (1, 4, 16)
HALF_WINDOW = 64
HEAD_DIM_A = 64
HEADS_PER_GROUP_A = 8
GROUP_WIDTH_A = 512
A_QKV_WIDTH = 4608
A_GROUP_QKV = A_QKV_WIDTH // 3
A_TQ = 128
A_WIN = A_TQ + 2 * HALF_WINDOW
A_UNROLL = 8
A_SCALE = HEAD_DIM_A ** -0.5
WGRAD_TK = 2048
HEAD_DIM_B = 128
N_HEADS_B = 8
N_KV_B = 2
GQA_GROUP_B = 4
GRID_W = 64
ROPE_THETA = 10000.0
B_TQ_FWD = 256
B_TQ_BWD = 512
B_HEADS_PER_STEP = 4
LOG2E = 1.4426950408889634
N_BUCKETS = 32
MAX_DISTANCE = 1024
PB_WIDTH = 3584
PB_GATE_A = 1536
PB_GATE_B = 2560

ADAM_LR = 0.001
ADAM_B1 = 0.9
ADAM_B2 = 0.999
ADAM_EPS = 1e-08
ADAM_WD = 0.01
ADAM_STEP = 10

FFN_SHARD = 352
MIX_WIN, MIX_WB, MIX_WOUT, MIX_WA = 0, 1024, 1152, 1280
MIX_ROWS = 1344
REST_WB, REST_WOUT, REST_WA, REST_ROWS = 0, 128, 256, 320


def _dot(a, b, ca=1, cb=0):
    return lax.dot_general(a, b, (((ca,), (cb,)), ((), ())), preferred_element_type=F32)


def _call(name, body, grid, ins, outs, scratch=(), sem=None, aliases=None):
    ins = [tuple(i) + (None,) * (4 - len(i)) for i in ins]
    res = pl.pallas_call(
        body,
        out_shape=[jax.ShapeDtypeStruct(s, d) for (s, d, _, _) in outs],
        grid=grid,
        in_specs=[pl.BlockSpec(bs, im, pipeline_mode=pm) for (_, bs, im, pm) in ins],
        out_specs=[pl.BlockSpec(bs, im) for (_, _, bs, im) in outs],
        scratch_shapes=list(scratch),
        name=name,
        input_output_aliases=aliases or {},
        compiler_params=pltpu.CompilerParams(dimension_semantics=sem, vmem_limit_bytes=VMEM_LIMIT),
    )(*[i[0] for i in ins])
    return res


def _sigmoid(x):
    return 0.5 * jnp.tanh(0.5 * x) + 0.5


def _position():
    return lax.axis_index("x"), lax.axis_index("y"), lax.axis_index("c")


def _hbm_specs(n):
    return [pl.BlockSpec(memory_space=pl.ANY) for _ in range(n)]


PAIR_BUFFERS = 4


def reduce_scatter_pair(grads, name):
    n = len(grads)
    C = grads[0].shape[2]
    half = [g.shape[1] // 2 for g in grads]
    chunks = [(i, q, hf) for i in range(n) for q in range(4) for hf in range(2)]
    nb = PAIR_BUFFERS

    def body(*refs):
        ins, theirs = refs[:n], refs[n:2 * n]
        buf, load_sems, send_sems, recv_sems = refs[2 * n:]
        x, y, c = _position()
        sibling = (x, y, 1 - c)

        def load(k):
            i, q, hf = chunks[k]
            r = half[i]
            return pltpu.make_async_copy(ins[i].at[2 * q + (1 - c), pl.ds(hf * r, r), :],
                                         buf.at[k % nb, pl.ds(0, r), :], load_sems.at[k % nb])

        def send(k):
            i, q, hf = chunks[k]
            r = half[i]
            return pltpu.make_async_remote_copy(
                src_ref=buf.at[k % nb, pl.ds(0, r), :], dst_ref=theirs[i].at[q, pl.ds(hf * r, r), :],
                send_sem=send_sems.at[k % nb], recv_sem=recv_sems.at[i],
                device_id=sibling, device_id_type=MESH)

        for k in range(len(chunks) + 1):
            if k < len(chunks):
                if k >= nb:
                    send(k - nb).wait_send()
                load(k).start()
            if k >= 1:
                load(k - 1).wait()
                send(k - 1).start()
        for k in range(max(0, len(chunks) - nb), len(chunks)):
            send(k).wait_send()
        for i in range(n):
            pltpu.make_async_remote_copy(
                src_ref=theirs[i], dst_ref=theirs[i], send_sem=send_sems.at[0], recv_sem=recv_sems.at[i],
                device_id=sibling, device_id_type=MESH).wait_recv()

    return pl.pallas_call(
        body,
        out_shape=[jax.ShapeDtypeStruct((4,) + g.shape[1:], g.dtype) for g in grads],
        in_specs=_hbm_specs(n),
        out_specs=_hbm_specs(n),
        scratch_shapes=[pltpu.VMEM((nb, max(half), C), grads[0].dtype), pltpu.SemaphoreType.DMA((nb,)),
                        pltpu.SemaphoreType.DMA((nb,)), pltpu.SemaphoreType.DMA((n,))],
        name=name,
        compiler_params=pltpu.CompilerParams(vmem_limit_bytes=VMEM_LIMIT),
    )(*grads)


_HBM_SPEC = pl.BlockSpec(memory_space=pltpu.HBM)
_SEM_SPEC = pl.BlockSpec(memory_space=pltpu.SEMAPHORE)
_TOKEN_SPEC = pl.BlockSpec(memory_space=pltpu.VMEM)
_DATAFLOW = pltpu.SideEffectType.DATAFLOW_SIDE_EFFECTING


def _split_start_many(name, exchanges):
    n = len(exchanges)

    def full_body(*refs):
        srcs, lands = refs[:n], refs[n:2 * n]
        sems = refs[2 * n:4 * n]
        token = refs[-1]
        for i, (body, _, _) in enumerate(exchanges):
            body(srcs[i], lands[i], sems[2 * i], sems[2 * i + 1])
        token[...] = jnp.zeros_like(token)

    srcs = [pltpu.with_memory_space_constraint(src, pltpu.HBM) for _, src, _ in exchanges]
    lands = [pltpu.with_memory_space_constraint(lax.empty(shape, src.dtype), pltpu.HBM)
             for _, src, shape in exchanges]
    res = pl.pallas_call(
        full_body, name=name,
        out_shape=(pltpu.SemaphoreType.DMA(()),) * (2 * n)
        + tuple(pltpu.HBM(a.shape, a.dtype) for a in srcs + lands) + (jax.ShapeDtypeStruct((8, LANES), F32),),
        in_specs=(_HBM_SPEC,) * (2 * n),
        out_specs=(_SEM_SPEC,) * (2 * n) + (_HBM_SPEC,) * (2 * n) + (_TOKEN_SPEC,),
        input_output_aliases={i: 2 * n + i for i in range(2 * n)},
        compiler_params=pltpu.CompilerParams(has_side_effects=_DATAFLOW),
    )(*srcs, *lands)
    return [(res[2 * i], res[2 * i + 1], res[2 * n + i], res[3 * n + i], res[-1]) for i in range(n)]


def _split_start(name, body, src, land_shape):
    return _split_start_many(name, [(body, src, land_shape)])[0]


def _split_wait(name, started, n_blocks, after):
    send_sem, recv_sem, src_thru, land_thru, _ = started
    after = after if isinstance(after, tuple) else (after,)

    def body(src_ref, land_ref, send_sem, recv_sem, *rest):
        x, y, c = _position()
        blocks = land_ref.at[pl.ds(0, n_blocks)]
        copy = pltpu.make_async_remote_copy(src_ref=blocks, dst_ref=blocks, send_sem=send_sem, recv_sem=recv_sem,
                                            device_id=(x, y, c), device_id_type=MESH)
        copy.wait_send()
        copy.wait_recv()

    return pl.pallas_call(
        body, name=name,
        out_shape=(pltpu.HBM(src_thru.shape, src_thru.dtype), pltpu.HBM(land_thru.shape, land_thru.dtype)),
        in_specs=(_HBM_SPEC, _HBM_SPEC, _SEM_SPEC, _SEM_SPEC) + (pl.BlockSpec(memory_space=pl.ANY),) * len(after),
        out_specs=(_HBM_SPEC, _HBM_SPEC),
        input_output_aliases={0: 0, 1: 1},
        compiler_params=pltpu.CompilerParams(has_side_effects=_DATAFLOW),
    )(src_thru, land_thru, send_sem, recv_sem, *after)


def all_gather_start_all(blocks, name):
    def starter(direct):
        def body(b_ref, land_ref, send_sem, recv_sem):
            x, y, c = _position()
            peers = _other_devices(x, y, c) if direct else [(x, y, 1 - c), (1 - x, y, c), (x, 1 - y, c),
                                                            (1 - x, 1 - y, c)]
            for peer in peers:
                pltpu.make_async_remote_copy(src_ref=b_ref, dst_ref=land_ref.at[4 * x + 2 * y + c],
                                             send_sem=send_sem, recv_sem=recv_sem,
                                             device_id=peer, device_id_type=MESH).start()
        return body

    return _split_start_many(name, [(starter(direct), block, (N_DEV,) + block.shape) for block, direct in blocks])


def all_gather_finish(block, land, name):
    R, C = block.shape

    def body(b_ref, land_in, land_ref, stage, load_sems, send_sems, recv_sems, own_sem):
        x, y, c = _position()
        sibling = (x, y, 1 - c)
        chips = [(1 - x, y), (x, 1 - y), (1 - x, 1 - y)]
        own_in = pltpu.make_async_copy(b_ref, stage.at[3], load_sems.at[3])
        own_in.start()
        loads = [pltpu.make_async_copy(land_in.at[4 * px + 2 * py + c], stage.at[j], load_sems.at[j])
                 for j, (px, py) in enumerate(chips)]
        for ld in loads:
            ld.start()
        sends = []
        for j, (px, py) in enumerate(chips):
            loads[j].wait()
            dst = land_ref.at[4 * px + 2 * py + c]
            cp = pltpu.make_async_remote_copy(src_ref=stage.at[j], dst_ref=dst, send_sem=send_sems.at[j],
                                              recv_sem=recv_sems.at[j], device_id=sibling, device_id_type=MESH)
            cp.start()
            sends.append(cp)
        own_in.wait()
        own_out = pltpu.make_async_copy(stage.at[3], land_ref.at[4 * x + 2 * y + c], own_sem)
        own_out.start()
        for j, (px, py) in enumerate(chips):
            dst = land_ref.at[4 * px + 2 * py + (1 - c)]
            pltpu.make_async_remote_copy(src_ref=stage.at[j], dst_ref=dst, send_sem=send_sems.at[j],
                                         recv_sem=recv_sems.at[j], device_id=sibling,
                                         device_id_type=MESH).wait_recv()
        for cp in sends:
            cp.wait_send()
        own_out.wait()

    return pl.pallas_call(
        body,
        out_shape=jax.ShapeDtypeStruct(land.shape, land.dtype),
        in_specs=_hbm_specs(2),
        out_specs=pl.BlockSpec(memory_space=pl.ANY),
        scratch_shapes=[pltpu.VMEM((4, R, C), block.dtype), pltpu.SemaphoreType.DMA((4,)),
                        pltpu.SemaphoreType.DMA((3,)), pltpu.SemaphoreType.DMA((3,)), pltpu.SemaphoreType.DMA],
        input_output_aliases={1: 0},
        name=name,
        compiler_params=pltpu.CompilerParams(vmem_limit_bytes=VMEM_LIMIT),
    )(block, land)


def reduce_scatter_start(parts, name):
    def body(p_ref, land_ref, send_sem, recv_sem):
        x, y, c = _position()
        for px, py in [(1 - x, y), (x, 1 - y), (1 - x, 1 - y)]:
            pltpu.make_async_remote_copy(src_ref=p_ref.at[2 * px + py], dst_ref=land_ref.at[2 * x + y],
                                         send_sem=send_sem, recv_sem=recv_sem,
                                         device_id=(px, py, c), device_id_type=MESH).start()

    return _split_start(name, body, parts, parts.shape)


def _other_devices(x, y, c):
    return [(1 - x if k & 4 else x, 1 - y if k & 2 else y, 1 - c if k & 1 else c) for k in range(1, N_DEV)]


def all_gather_place_own(block, land, name):
    R, C = block.shape

    def body(b_ref, land_in, land_ref, stage, sems):
        x, y, c = _position()
        load = pltpu.make_async_copy(b_ref, stage, sems.at[0])
        load.start()
        load.wait()
        store = pltpu.make_async_copy(stage, land_ref.at[4 * x + 2 * y + c], sems.at[1])
        store.start()
        store.wait()

    return pl.pallas_call(
        body,
        out_shape=jax.ShapeDtypeStruct(land.shape, land.dtype),
        in_specs=_hbm_specs(2),
        out_specs=pl.BlockSpec(memory_space=pl.ANY),
        scratch_shapes=[pltpu.VMEM((R, C), block.dtype), pltpu.SemaphoreType.DMA((2,))],
        input_output_aliases={1: 0},
        name=name,
    )(block, land)


def reduce_scatter_start_direct(grads, name):
    def body(g_ref, land_ref, send_sem, recv_sem):
        x, y, c = _position()
        for px, py, pc in _other_devices(x, y, c):
            pltpu.make_async_remote_copy(src_ref=g_ref.at[4 * px + 2 * py + pc],
                                         dst_ref=land_ref.at[4 * x + 2 * y + c],
                                         send_sem=send_sem, recv_sem=recv_sem,
                                         device_id=(px, py, pc), device_id_type=MESH).start()

    return _split_start(name, body, grads, grads.shape)


def small_all_gather(small):
    def body(small_ref, smalls, s_send, s_recv, s_local):
        x, y, c = _position()
        me = 4 * x + 2 * y + c
        lc = pltpu.make_async_copy(small_ref, smalls.at[me], s_local)
        lc.start()
        remote = []
        k = 0
        for dx in (0, 1):
            for dy in (0, 1):
                for dc in (0, 1):
                    if dx + dy + dc == 0:
                        continue
                    peer = (1 - x if dx else x, 1 - y if dy else y, 1 - c if dc else c)
                    rc = pltpu.make_async_remote_copy(
                        src_ref=small_ref, dst_ref=smalls.at[me],
                        send_sem=s_send.at[k], recv_sem=s_recv.at[k],
                        device_id=peer, device_id_type=MESH)
                    rc.start()
                    remote.append(rc)
                    k += 1
        for rc in remote:
            rc.wait()
        lc.wait()

    return pl.pallas_call(
        body,
        out_shape=jax.ShapeDtypeStruct((N_DEV,) + small.shape, small.dtype),
        in_specs=_hbm_specs(1),
        out_specs=pl.BlockSpec(memory_space=pl.ANY),
        scratch_shapes=[pltpu.SemaphoreType.DMA((7,)), pltpu.SemaphoreType.DMA((7,)), pltpu.SemaphoreType.DMA],
        name="small_all_gather",
    )(small)


def pair_add(grads, theirs, core, name):
    _, R, C = theirs.shape
    tr = R // 2

    def body(c_ref, a_ref, b_ref, o_ref):
        o_ref[...] = (a_ref[...].astype(F32) + b_ref[...].astype(F32)).astype(BF16)

    return pl.pallas_call(
        body,
        out_shape=jax.ShapeDtypeStruct(theirs.shape, BF16),
        grid_spec=pltpu.PrefetchScalarGridSpec(
            num_scalar_prefetch=1, grid=(4, R // tr),
            in_specs=[pl.BlockSpec((None, tr, C), lambda q, i, c: (2 * q + c[0], i, 0)),
                      pl.BlockSpec((None, tr, C), lambda q, i, c: (q, i, 0))],
            out_specs=pl.BlockSpec((None, tr, C), lambda q, i, c: (q, i, 0))),
        name=name,
        compiler_params=pltpu.CompilerParams(dimension_semantics=("parallel", "parallel"),
                                             vmem_limit_bytes=VMEM_LIMIT),
    )(core, grads, theirs)


def sum_slots(recv, off, rows, blk, name):
    nq, _, C = recv.shape
    ob = off // blk

    def body(r_ref, o_ref):
        acc = r_ref[0].astype(F32)
        for q in range(1, nq):
            acc = acc + r_ref[q].astype(F32)
        o_ref[...] = acc

    return _call(name, body, (rows // blk,),
                 [(recv, (nq, blk, C), lambda i: (0, ob + i, 0))],
                 [((rows, C), F32, (blk, C), lambda i: (i, 0))], sem=("parallel",))[0]


def _sum_terms(refs):
    acc = refs[0][...].astype(F32)
    for r in refs[1:]:
        acc = acc + r[...].astype(F32)
    return acc


def sum_landed(own, land, me, off, rows, blk, name):
    n, _, C = land.shape
    ob = off // blk

    def body(c_ref, *refs):
        refs[n][...] = _sum_terms(refs[:n])

    def entry(flip):
        return pl.BlockSpec((None, blk, C), lambda i, c: (c[0] ^ flip, ob + i, 0))

    return pl.pallas_call(
        body,
        out_shape=jax.ShapeDtypeStruct((rows, C), F32),
        grid_spec=pltpu.PrefetchScalarGridSpec(
            num_scalar_prefetch=1, grid=(rows // blk,),
            in_specs=[entry(k) for k in range(n)],
            out_specs=pl.BlockSpec((blk, C), lambda i, c: (i, 0))),
        name=name,
        compiler_params=pltpu.CompilerParams(dimension_semantics=("parallel",), vmem_limit_bytes=VMEM_LIMIT),
    )(me, own, *([land] * (n - 1)))


def _adamw_update(wv, gv, mv, vv):
    nm = ADAM_B1 * mv + (1.0 - ADAM_B1) * gv
    nv = ADAM_B2 * vv + (1.0 - ADAM_B2) * (gv * gv)
    c1 = 1.0 / (1.0 - ADAM_B1 ** ADAM_STEP)
    c2 = 1.0 / (1.0 - ADAM_B2 ** ADAM_STEP)
    return -ADAM_LR * ((nm * c1) / (jnp.sqrt(nv * c2) + ADAM_EPS) + ADAM_WD * wv), nm, nv


def sum_adamw(own, land, me, off, blk, w, m, v, name):
    rows, C = w.shape
    n = land.shape[0]
    ob = off // blk

    def body(c_ref, *refs):
        w_ref, m_ref, v_ref, g_out, d_out, m_out, v_out = refs[n:]
        gv = _sum_terms(refs[:n])
        g_out[...] = gv
        d_out[...], m_out[...], v_out[...] = _adamw_update(w_ref[...], gv, m_ref[...], v_ref[...])

    def entry(flip):
        return pl.BlockSpec((None, blk, C), lambda i, c: (c[0] ^ flip, ob + i, 0))

    plain = pl.BlockSpec((blk, C), lambda i, c: (i, 0))
    return pl.pallas_call(
        body,
        out_shape=[jax.ShapeDtypeStruct((rows, C), F32)] * 4,
        grid_spec=pltpu.PrefetchScalarGridSpec(
            num_scalar_prefetch=1, grid=(rows // blk,),
            in_specs=[entry(k) for k in range(n)] + [plain, plain, plain],
            out_specs=[plain] * 4),
        name=name,
        compiler_params=pltpu.CompilerParams(dimension_semantics=("parallel",), vmem_limit_bytes=VMEM_LIMIT),
    )(me, own, *([land] * (n - 1)), w, m, v)


def adamw(w, g, m, v, name):
    R, C = w.shape
    tr = R
    for cand in (256, 128, 64, 32, 16, 8):
        if R % cand == 0 and R > cand:
            tr = cand
            break

    def body(w_ref, g_ref, m_ref, v_ref, d_ref, nm_ref, nv_ref):
        d_ref[...], nm_ref[...], nv_ref[...] = _adamw_update(w_ref[...], g_ref[...], m_ref[...], v_ref[...])

    spec = ((tr, C), lambda i: (i, 0))
    out = ((R, C), F32) + spec
    return _call(name, body, (R // tr,), [(w,) + spec, (g,) + spec, (m,) + spec, (v,) + spec],
                 [out, out, out], sem=("parallel",))


def rms_fwd(x, g, name):
    S, D = x.shape
    tr = 512

    def body(x_ref, g_ref, o_ref):
        xv = x_ref[...]
        r = lax.rsqrt(jnp.mean(xv * xv, axis=-1, keepdims=True) + EPS)
        o_ref[...] = (xv * r * g_ref[...]).astype(BF16)

    return _call(name, body, (S // tr,),
                 [(x, (tr, D), lambda i: (i, 0)), (g, (1, D), lambda i: (0, 0))],
                 [((S, D), BF16, (tr, D), lambda i: (i, 0))], sem=("parallel",))[0]


def _rms_bwd_tile(dn, xv, gv):
    r = lax.rsqrt(jnp.mean(xv * xv, axis=-1, keepdims=True) + EPS)
    xh = xv * r
    dxh = dn * gv
    dx = r * (dxh - xh * jnp.mean(dxh * xh, axis=-1, keepdims=True))
    return dx, dn * xh


def final_loss(x, tgt, g, name):
    S, D = x.shape
    tr = 256

    def body(x_ref, t_ref, g_ref, l_ref, dx_ref, dxb_ref, dg_ref):
        i = pl.program_id(0)
        xv, gv = x_ref[...], g_ref[...]
        r = lax.rsqrt(jnp.mean(xv * xv, axis=-1, keepdims=True) + EPS)
        xh = xv * r
        e = xh * gv - t_ref[...]
        part = 0.5 * jnp.sum(jnp.sum(e * e, axis=-1, keepdims=True) * (1.0 / D), axis=0, keepdims=True)
        dy = e * (1.0 / D)
        dxh = dy * gv
        dx = r * (dxh - xh * jnp.mean(dxh * xh, axis=-1, keepdims=True))
        dx_ref[...] = dx
        dxb_ref[...] = dx.astype(BF16)
        dgp = jnp.sum(dy * xh, axis=0, keepdims=True)

        @pl.when(i == 0)
        def _():
            l_ref[...] = jnp.broadcast_to(part, l_ref.shape)
            dg_ref[...] = dgp

        @pl.when(i > 0)
        def _():
            l_ref[...] += jnp.broadcast_to(part, l_ref.shape)
            dg_ref[...] += dgp

    row = ((tr, D), lambda i: (i, 0))
    return _call(name, body, (S // tr,),
                 [(x,) + row, (tgt,) + row, (g, (1, D), lambda i: (0, 0))],
                 [((1, LANES), F32, (1, LANES), lambda i: (0, 0)), ((S, D), F32) + row, ((S, D), BF16) + row,
                  ((1, D), F32, (1, D), lambda i: (0, 0))], sem=("arbitrary",))


FFN_TF = 4 * FFN_SHARD


def _ffn_pick(G, which):
    if isinstance(G, tuple):
        return (G[0], which) if which < 2 else (G[1], 0)
    return G, which


def _ffn_w_spec(G, which, imap):
    arr, blk = _ffn_pick(G, which)
    return (arr, (4, FFN_SHARD, arr.shape[2]), lambda *idx: (imap(*idx), blk, 0))


def _ffn_whole_w_spec(G, which):
    arr, blk = _ffn_pick(G, which)
    return (arr, (N_DEV, FFN_SHARD, arr.shape[2]), lambda *idx: (0, blk, 0), pl.Buffered(1))


def _ffn_hidden(a, b):
    av, bv = a.astype(F32), b.astype(F32)
    return (av * _sigmoid(av) * bv).astype(BF16)


def ffn_up(n, G, name):
    S, D = n.shape
    F = N_DEV * FFN_SHARD
    tm = 256

    def body(n_ref, w1_ref, w3_ref, abh_ref):
        nv = n_ref[...]
        a = _dot(nv, w1_ref[...].reshape(F, D), 1, 1).astype(BF16)
        b = _dot(nv, w3_ref[...].reshape(F, D), 1, 1).astype(BF16)
        abh_ref[0] = a
        abh_ref[1] = b
        abh_ref[2] = _ffn_hidden(a, b)

    return _call(name, body, (S // tm,),
                 [(n, (tm, D), lambda i: (i, 0)),
                  _ffn_whole_w_spec(G, 0), _ffn_whole_w_spec(G, 1)],
                 [((3, S, F), BF16, (3, tm, F), lambda i: (0, i, 0))],
                 sem=("parallel",))[0]


def ffn_down(abh, G, x, name):
    _, S, F = abh.shape
    D = x.shape[1]
    tm = 512

    def body(h_ref, w2_ref, x_ref, o_ref):
        o_ref[...] = x_ref[...] + 0.5 * _dot(h_ref[...], w2_ref[...].reshape(F, D))

    return _call(name, body, (S // tm,),
                 [(abh, (None, tm, F), lambda i: (2, i, 0)), _ffn_whole_w_spec(G, 2),
                  (x, (tm, D), lambda i: (i, 0))],
                 [((S, D), F32, (tm, D), lambda i: (i, 0))], sem=("parallel",))[0]


def ffn_fwd(n, G, x, name):
    S, D = x.shape
    F = N_DEV * FFN_SHARD
    tm = 256

    def body(n_ref, w1_ref, w3_ref, w2_ref, x_ref, ab_ref, o_ref):
        nv = n_ref[...]
        a = _dot(nv, w1_ref[...].reshape(F, D), 1, 1).astype(BF16)
        b = _dot(nv, w3_ref[...].reshape(F, D), 1, 1).astype(BF16)
        ab_ref[0] = a
        ab_ref[1] = b
        o_ref[...] = x_ref[...] + 0.5 * _dot(_ffn_hidden(a, b), w2_ref[...].reshape(F, D))

    tile = ((tm, D), lambda i: (i, 0))
    return _call(name, body, (S // tm,),
                 [(n,) + tile, _ffn_whole_w_spec(G, 0), _ffn_whole_w_spec(G, 1), _ffn_whole_w_spec(G, 2),
                  (x,) + tile],
                 [((2, S, F), BF16, (2, tm, F), lambda i: (0, i, 0)), ((S, D), F32) + tile],
                 sem=("parallel",))


def _ffn_hidden_grads(dh, av, bv):
    sig = _sigmoid(av)
    return dh * bv * (sig * (1.0 + av * (1.0 - sig))), dh * (av * sig)


def ffn_bwd_hidden(dxo, abh, G, name):
    _, S, F = abh.shape
    D = dxo.shape[1]
    tm = 256

    def body(d_ref, w2_ref, ab_ref, o_ref):
        dh = 0.5 * _dot(d_ref[...].astype(BF16), w2_ref[...].reshape(F, D), 1, 1)
        da, db = _ffn_hidden_grads(dh, ab_ref[0].astype(F32), ab_ref[1].astype(F32))
        o_ref[0] = da.astype(BF16)
        o_ref[1] = db.astype(BF16)

    return _call(name + "_down_bwd", body, (S // tm,),
                 [(dxo, (tm, D), lambda i: (i, 0)), _ffn_whole_w_spec(G, 2),
                  (abh, (2, tm, F), lambda i: (0, i, 0))],
                 [((2, S, F), BF16, (2, tm, F), lambda i: (0, i, 0))],
                 sem=("parallel",))[0]


def ffn_bwd_hidden_input(dxo_b, dxo, abh, G, x_in, g, name):
    _, S, F = abh.shape
    D = x_in.shape[1]
    tm = 256

    def body(db_ref, w2_ref, w1_ref, w3_ref, ab_ref, x_ref, d_ref, g_ref, dab_ref, dx_ref, dxb_ref, dg_ref):
        i = pl.program_id(0)
        dh = 0.5 * _dot(db_ref[...], w2_ref[...].reshape(F, D), 1, 1)
        da, db = _ffn_hidden_grads(dh, ab_ref[0].astype(F32), ab_ref[1].astype(F32))
        da, db = da.astype(BF16), db.astype(BF16)
        dab_ref[0] = da
        dab_ref[1] = db
        dn = _dot(da, w1_ref[...].reshape(F, D)) + _dot(db, w3_ref[...].reshape(F, D))
        dx, dgt = _rms_bwd_tile(dn, x_ref[...], g_ref[...])
        dx = d_ref[...] + dx
        dx_ref[...] = dx
        dxb_ref[...] = dx.astype(BF16)
        dgp = jnp.sum(dgt, axis=0, keepdims=True)

        @pl.when(i == 0)
        def _():
            dg_ref[...] = dgp

        @pl.when(i > 0)
        def _():
            dg_ref[...] += dgp

    tile = ((tm, D), lambda i: (i, 0))
    wide = ((2, tm, F), lambda i: (0, i, 0))
    return _call(name + "_hidden_input", body, (S // tm,),
                 [(dxo_b,) + tile, _ffn_whole_w_spec(G, 2), _ffn_whole_w_spec(G, 0), _ffn_whole_w_spec(G, 1),
                  (abh,) + wide, (x_in,) + tile, (dxo,) + tile, (g, (1, D), lambda i: (0, 0))],
                 [((2, S, F), BF16) + wide, ((S, D), F32) + tile, ((S, D), BF16) + tile,
                  ((1, D), F32, (1, D), lambda i: (0, 0))],
                 sem=("arbitrary",))


def ffn_bwd_weights(dxo, abh, dab, n, name):
    n_saved, S, F = abh.shape
    D = dxo.shape[1]
    nf = F // FFN_TF
    saved_h = n_saved == 3
    tk = WGRAD_TK if saved_h else WGRAD_TK // 2
    nk = S // tk
    gshape = (N_DEV, 3 * FFN_SHARD, D)
    chunk = 256

    def dw2_body(h_ref, d_ref, o_ref, acc_ref, *h_buf):
        k = pl.program_id(1)
        if saved_h:
            h = h_ref[...]
        else:
            for r in range(0, tk, chunk):
                h_buf[0][r:r + chunk, :] = _ffn_hidden(h_ref[0, r:r + chunk, :], h_ref[1, r:r + chunk, :])
            h = h_buf[0][...]
        p = _dot(h, d_ref[...].astype(BF16), 0, 0)

        @pl.when(k == 0)
        def _():
            acc_ref[...] = p

        @pl.when(k > 0)
        def _():
            acc_ref[...] += p

        @pl.when(k == nk - 1)
        def _():
            o_ref[...] = (0.5 * acc_ref[...]).astype(BF16).reshape(4, FFN_SHARD, D)

    gw = _call(name + "_dw2", dw2_body, (nf, nk),
               [(abh, (None, tk, FFN_TF), lambda j, k: (2, k, j)) if saved_h else
                (abh, (2, tk, FFN_TF), lambda j, k: (0, k, j)), (dxo, (tk, D), lambda j, k: (k, 0))],
               [(gshape, BF16, (4, FFN_SHARD, D), lambda j, k: (j, 2, 0))],
               scratch=[pltpu.VMEM((FFN_TF, D), F32)] + ([] if saved_h else [pltpu.VMEM((tk, FFN_TF), BF16)]),
               sem=("parallel", "arbitrary"))[0]

    def dw13_body(gw_ref, dab_ref, n_ref, o_ref):
        o_ref[...] = _dot(dab_ref[...], n_ref[...], 0, 0).astype(BF16).reshape(4, FFN_SHARD, D)

    gw = pl.pallas_call(
        dw13_body,
        out_shape=jax.ShapeDtypeStruct(gshape, BF16),
        grid=(2, nf),
        in_specs=[pl.BlockSpec(memory_space=pl.ANY),
                  pl.BlockSpec((None, S, FFN_TF), lambda w, j: (w, 0, j)),
                  pl.BlockSpec((S, D), lambda w, j: (0, 0))],
        out_specs=pl.BlockSpec((4, FFN_SHARD, D), lambda w, j: (j, w, 0)),
        input_output_aliases={0: 0},
        name=name + "_dw13",
        compiler_params=pltpu.CompilerParams(dimension_semantics=("parallel", "parallel"),
                                             vmem_limit_bytes=VMEM_LIMIT),
    )(gw, dab, n)
    return gw


def ffn_bwd_input(dab, G, x_in, g, dxo, name):
    _, S, F = dab.shape
    D = x_in.shape[1]
    tm = 256

    def dn_body(dab_ref, w1_ref, w3_ref, x_ref, d_ref, g_ref, dx_ref, dg_ref):
        i = pl.program_id(0)
        dn = _dot(dab_ref[0], w1_ref[...].reshape(F, D)) + _dot(dab_ref[1], w3_ref[...].reshape(F, D))
        dx, dgt = _rms_bwd_tile(dn, x_ref[...], g_ref[...])
        dx_ref[...] = d_ref[...] + dx
        dgp = jnp.sum(dgt, axis=0, keepdims=True)

        @pl.when(i == 0)
        def _():
            dg_ref[...] = dgp

        @pl.when(i > 0)
        def _():
            dg_ref[...] += dgp

    tile = ((tm, D), lambda i: (i, 0))
    return _call(name + "_dn", dn_body, (S // tm,),
                 [(dab, (2, tm, F), lambda i: (0, i, 0)),
                  _ffn_whole_w_spec(G, 0), _ffn_whole_w_spec(G, 1),
                  (x_in,) + tile, (dxo,) + tile, (g, (1, D), lambda i: (0, 0))],
                 [((S, D), F32) + tile, ((1, D), F32, (1, D), lambda i: (0, 0))],
                 sem=("arbitrary",))


PROJ_TN = 512
DH_SHARDS_PER_STEP = 4


def in_proj(h, Gm, first_tile, n_tiles, name, tile_stride=1):
    S, D = h.shape
    tile = lambda j: first_tile + tile_stride * j

    def body(h_ref, w_ref, o_ref):
        o_ref[...] = _dot(h_ref[...], w_ref[...]).astype(BF16)

    return _call(name, body, (n_tiles,),
                 [(h, (S, D), lambda j: (0, 0)),
                  (Gm, (None, D, PROJ_TN), lambda j: (tile(j) // 2, 0, tile(j) % 2))],
                 [((S, n_tiles * PROJ_TN), BF16, (S, PROJ_TN), lambda j: (0, j))],
                 sem=("parallel",))[0]


def _dproj_pieces(dqkv, dq_b, dkv_b, dgate):
    pieces = [(dqkv[g], [(3 * which + g, (which, 0)) for which in range(3)]) for g in range(3)]
    pieces.append((dq_b, [(9, (None, 0)), (10, (None, 1))]))
    pieces.append((dkv_b, [(11, (None, 0))]))
    pieces.append((dgate, [(12 + 2 * a + b, (a, b)) for a in range(2) for b in range(2)]))
    return pieces


def in_proj_bwd_dw(pieces, h, gm_grads, name):
    S, D = h.shape

    for n_piece, (arr, tiles) in enumerate(pieces):
        w_tile = [t for t, _ in tiles]
        lead = [ix[0] for _, ix in tiles]
        colb = [ix[1] for _, ix in tiles]

        def pick(table, j):
            out = table[-1]
            for k in range(len(table) - 2, -1, -1):
                out = jnp.where(j == k, table[k], out)
            return out

        def dw_body(gm_ref, h_ref, d_ref, o_ref):
            o_ref[...] = _dot(h_ref[...], d_ref[...], 0, 0).astype(BF16)

        if arr.ndim == 3:
            d_spec = pl.BlockSpec((None, S, PROJ_TN), lambda j, lead=lead, colb=colb: (pick(lead, j), 0, pick(colb, j)))
        else:
            d_spec = pl.BlockSpec((S, PROJ_TN), lambda j, colb=colb: (0, pick(colb, j)))
        gm_grads = pl.pallas_call(
            dw_body,
            out_shape=jax.ShapeDtypeStruct(gm_grads.shape, BF16),
            grid=(len(tiles),),
            in_specs=[pl.BlockSpec(memory_space=pl.ANY), pl.BlockSpec((S, D), lambda j: (0, 0)), d_spec],
            out_specs=pl.BlockSpec((None, D, PROJ_TN),
                                   lambda j, w_tile=w_tile: (pick(w_tile, j) // 2, 0, pick(w_tile, j) % 2)),
            input_output_aliases={0: 0},
            name="%s_dw%d" % (name, n_piece),
            compiler_params=pltpu.CompilerParams(dimension_semantics=("parallel",), vmem_limit_bytes=VMEM_LIMIT),
        )(gm_grads, h, arr)
    return gm_grads


def in_proj_bwd_dh(pieces, Gm, x_in, g, dres, name):
    S, D = x_in.shape
    tm = 256
    C = Gm.shape[2]
    n_sh = N_DEV
    n_p = len(pieces)

    def dh_body(*refs):
        d_refs = refs[:n_p]
        w_ref, x_ref, r_ref, g_ref, dx_ref, dxb_ref, dg_ref = refs[n_p:]
        i = pl.program_id(0)
        p = None
        for d_ref, (arr, tiles) in zip(d_refs, pieces):
            for t, (lead, colb) in tiles:
                cols = slice(colb * PROJ_TN, (colb + 1) * PROJ_TN)
                d = d_ref[:, cols] if lead is None else d_ref[lead, :, cols]
                wcol = (t % 2) * PROJ_TN
                term = _dot(d, w_ref[t // 2, :, wcol:wcol + PROJ_TN], 1, 1)
                p = term if p is None else p + term
        dx, dgt = _rms_bwd_tile(p, x_ref[...], g_ref[...])
        dx = r_ref[...] + dx
        dx_ref[...] = dx
        dxb_ref[...] = dx.astype(BF16)
        dgp = jnp.sum(dgt, axis=0, keepdims=True)

        @pl.when(i == 0)
        def _():
            dg_ref[...] = dgp

        @pl.when(i > 0)
        def _():
            dg_ref[...] += dgp

    tile = ((tm, D), lambda i: (i, 0))

    def rows_of(arr):
        if arr.ndim == 3:
            return (arr, (arr.shape[0], tm, arr.shape[2]), lambda i: (0, i, 0))
        return (arr, (tm, arr.shape[1]), lambda i: (i, 0))

    return _call(name + "_dh", dh_body, (S // tm,),
                 [rows_of(arr) for arr, _ in pieces]
                 + [(Gm, (n_sh, D, C), lambda i: (0, 0, 0), pl.Buffered(1)),
                    (x_in,) + tile, (dres,) + tile, (g, (1, D), lambda i: (0, 0))],
                 [((S, D), F32) + tile, ((S, D), BF16) + tile, ((1, D), F32, (1, D), lambda i: (0, 0))],
                 sem=("arbitrary",))


def _t5_bucket(rel):
    n = N_BUCKETS // 2
    max_exact = n // 2
    ret = jnp.where(rel > 0, n, 0)
    a = jnp.abs(rel)
    af = jnp.maximum(a, 1).astype(F32)
    large = max_exact + (jnp.log(af / max_exact) / math.log(MAX_DISTANCE / max_exact)
                         * (n - max_exact)).astype(jnp.int32)
    large = jnp.minimum(large, n - 1)
    return ret + jnp.where(a < max_exact, a, large)


def _bucket_tables():
    qi = jnp.arange(A_TQ, dtype=jnp.int32)[:, None]
    kj = jnp.arange(A_WIN, dtype=jnp.int32)[None, :]
    rel = kj - HALF_WINDOW - qi
    return jnp.stack([_t5_bucket(rel * d) for d in DILATIONS], axis=0)


def bias_build(rel_bias, buckets):
    def body(tab_ref, bk_ref, o_ref):
        col = pl.program_id(0) * HEADS_PER_GROUP_A + pl.program_id(1)
        bk = bk_ref[...]
        acc = jnp.zeros(bk.shape, F32)
        for b in range(N_BUCKETS):
            acc = jnp.where(bk == b, tab_ref[b, col], acc)
        qi = lax.broadcasted_iota(jnp.int32, bk.shape, 0)
        kj = lax.broadcasted_iota(jnp.int32, bk.shape, 1)
        band = jnp.where(jnp.abs(kj - HALF_WINDOW - qi) <= HALF_WINDOW, acc, NEG_INF)
        o_ref[0] = jnp.where(kj >= HALF_WINDOW, band, NEG_INF)
        o_ref[1] = band
        o_ref[2] = jnp.where(kj < A_TQ + HALF_WINDOW, band, NEG_INF)

    out = pl.pallas_call(
        body,
        out_shape=jax.ShapeDtypeStruct((3, HEADS_PER_GROUP_A // 2, 3, 2, A_TQ, A_WIN), F32),
        grid=(3, HEADS_PER_GROUP_A),
        in_specs=[pl.BlockSpec(memory_space=pltpu.SMEM),
                  pl.BlockSpec((None, A_TQ, A_WIN), lambda g, h: (g, 0, 0))],
        out_specs=pl.BlockSpec((None, None, 3, None, A_TQ, A_WIN), lambda g, h: (g, h // 2, 0, h % 2, 0, 0)),
        name="a_bias_build",
        compiler_params=pltpu.CompilerParams(dimension_semantics=("parallel", "parallel")),
    )(rel_bias, buckets)
    return out.reshape(3, HEADS_PER_GROUP_A // 2, 3, 2 * A_TQ, A_WIN)


def bias_bwd(dbias, buckets):
    def body(d_ref, bk_ref, o_ref):
        bk = bk_ref[...]
        dv = d_ref[...]
        for b in range(N_BUCKETS):
            part = jnp.sum(jnp.where(bk == b, dv, 0.0), axis=1, keepdims=True)
            o_ref[b:b + 1, :] = jnp.broadcast_to(jnp.sum(part, axis=0, keepdims=True), (1, LANES))

    out = pl.pallas_call(
        body,
        out_shape=jax.ShapeDtypeStruct((3, HEADS_PER_GROUP_A, N_BUCKETS, LANES), F32),
        grid=(3, HEADS_PER_GROUP_A),
        in_specs=[pl.BlockSpec((None, None, A_TQ, A_WIN), lambda g, h: (g, h, 0, 0)),
                  pl.BlockSpec((None, A_TQ, A_WIN), lambda g, h: (g, 0, 0))],
        out_specs=pl.BlockSpec((None, None, N_BUCKETS, LANES), lambda g, h: (g, h, 0, 0)),
        name="a_bias_bwd",
        compiler_params=pltpu.CompilerParams(dimension_semantics=("parallel", "parallel")),
    )(dbias, buckets)
    return out[:, :, :, 0].transpose(2, 0, 1).reshape(N_BUCKETS, 3 * HEADS_PER_GROUP_A)


def _a_fill_padded(pad_ref, src_ref, n, pad):
    zeros = jnp.zeros((pad, LANES), pad_ref.dtype)
    pad_ref[0:pad, :] = zeros
    pad_ref[pad + n:2 * pad + n, :] = zeros
    pad_ref[pad:pad + n, :] = src_ref[...].astype(pad_ref.dtype)


def _a_stack_heads(x, lane):
    zero = jnp.zeros_like(x)
    return jnp.concatenate([jnp.where(lane < HEAD_DIM_A, x, zero), jnp.where(lane >= HEAD_DIM_A, x, zero)], axis=0)


def _a_bias_variant(qb, nqb):
    return jnp.where(qb == 0, 0, jnp.where(qb == nqb - 1, 2, 1))


def a_fwd(proj_g, bias, g, name):
    S = proj_g.shape[0]
    d = DILATIONS[g]
    L = S // d
    nqb = L // A_TQ
    pad = HALF_WINDOW * d

    def body(q_ref, k_ref, v_ref, b_ref, o_ref, l_ref, qf, kpad, vpad):
        qf[...] = q_ref[...].astype(F32) * A_SCALE
        _a_fill_padded(kpad, k_ref, S, pad)
        _a_fill_padded(vpad, v_ref, S, pad)
        lane = lax.broadcasted_iota(jnp.int32, (A_TQ, LANES), 1)

        def block(t, carry):
            qb, r = t // d, t % d
            start = qb * (A_TQ * d) + r
            kw = kpad[pl.ds(start, A_WIN, stride=d), :].astype(BF16)
            vw = vpad[pl.ds(start, A_WIN, stride=d), :].astype(BF16)
            q = qf[pl.ds(start, A_TQ, stride=d), :].astype(BF16)
            q2 = _a_stack_heads(q, lane)
            s = _dot(q2, kw, 1, 1) + b_ref[_a_bias_variant(qb, nqb)]
            m = jnp.max(s, axis=-1, keepdims=True)
            e = jnp.exp(s - m)
            l = jnp.sum(e, axis=-1, keepdims=True)
            o2 = _dot(e.astype(BF16), vw) / l
            lse2 = m + jnp.log(l)
            o_ref[pl.ds(start, A_TQ, stride=d), :] = jnp.where(lane < HEAD_DIM_A, o2[0:A_TQ], o2[A_TQ:])
            l_ref[pl.ds(start, A_TQ, stride=d), :] = jnp.where(lane < HEAD_DIM_A, lse2[0:A_TQ], lse2[A_TQ:])
            return carry

        lax.fori_loop(0, nqb * d, block, 0, unroll=A_UNROLL)

    out_spec = ((S, GROUP_WIDTH_A), F32, (S, LANES), lambda hp: (0, hp))
    return _call(name, body, (4,),
                 [(proj_g, (S, LANES), lambda hp: (0, hp)),
                  (proj_g, (S, LANES), lambda hp: (0, 4 + hp)),
                  (proj_g, (S, LANES), lambda hp: (0, 8 + hp)),
                  (bias, (None, None, 3, 2 * A_TQ, A_WIN), lambda hp: (g, hp, 0, 0, 0))],
                 [out_spec, out_spec],
                 scratch=[pltpu.VMEM((S, LANES), F32)] + [pltpu.VMEM((S + 2 * pad, LANES), F32)] * 2,
                 sem=("parallel",))


def a_combine(outs, lses, name):
    S, W = outs[0].shape
    tr = 512

    def body(o0, o1, o2, l0, l1, l2, oa_ref, lt_ref):
        a, b, c = l0[...], l1[...], l2[...]
        m = jnp.maximum(jnp.maximum(a, b), c)
        ea, eb, ec = jnp.exp(a - m), jnp.exp(b - m), jnp.exp(c - m)
        z = ea + eb + ec
        oa_ref[...] = ((ea * o0[...] + eb * o1[...] + ec * o2[...]) / z).astype(BF16)
        lt_ref[...] = m + jnp.log(z)

    spec = ((tr, W), lambda i: (i, 0))
    return _call(name, body, (S // tr,), [(a,) + spec for a in (*outs, *lses)],
                 [((S, W), BF16) + spec, ((S, W), F32) + spec], sem=("parallel",))


def a_bwd(proj_g, bias, do_a, o_a, lse_tot, g, name):
    S = proj_g.shape[0]
    d = DILATIONS[g]
    L = S // d
    nqb = L // A_TQ
    pad = HALF_WINDOW * d

    def body(q_ref, k_ref, v_ref, b_ref, do_ref, o_ref, l_ref, dqkv_ref, db_ref,
             qf, of, dqf, kpad, vpad, dkacc, dvacc):
        qf[...] = q_ref[...].astype(F32) * A_SCALE
        of[...] = o_ref[...].astype(F32)
        _a_fill_padded(kpad, k_ref, S, pad)
        _a_fill_padded(vpad, v_ref, S, pad)
        dkacc[...] = jnp.zeros(dkacc.shape, F32)
        dvacc[...] = jnp.zeros(dvacc.shape, F32)
        db_ref[...] = jnp.zeros(db_ref.shape, F32)
        lane = lax.broadcasted_iota(jnp.int32, (A_TQ, LANES), 1)

        def block(t, carry):
            qb, r = t // d, t % d
            start = qb * (A_TQ * d) + r
            rows = pl.ds(start, A_TQ, stride=d)
            win = pl.ds(start, A_WIN, stride=d)
            kw = kpad[win, :].astype(BF16)
            vw = vpad[win, :].astype(BF16)
            q = qf[rows, :].astype(BF16)
            do = do_ref[rows, :]
            ov = of[rows, :]
            lt = l_ref[rows, :]
            q2 = _a_stack_heads(q, lane)
            do2 = _a_stack_heads(do, lane)
            lt2 = jnp.concatenate([lt[:, 0:1], lt[:, HEAD_DIM_A:HEAD_DIM_A + 1]], axis=0)
            s = _dot(q2, kw, 1, 1) + b_ref[_a_bias_variant(qb, nqb)]
            p = jnp.exp(s - lt2)
            t = jnp.sum(do2 * jnp.concatenate([ov, ov], axis=0), axis=-1, keepdims=True)
            dob2 = do2.astype(BF16)
            ds = p * (_dot(dob2, vw, 1, 1) - t)
            db_ref[...] += ds
            dsb = ds.astype(BF16)
            dq2 = _dot(dsb, kw)
            dqf[rows, :] = jnp.where(lane < HEAD_DIM_A, dq2[0:A_TQ], dq2[A_TQ:]) * A_SCALE
            dkacc[win, :] += _dot(dsb, q2, 0, 0)
            dvacc[win, :] += _dot(p.astype(BF16), dob2, 0, 0)
            return carry

        lax.fori_loop(0, nqb * d, block, 0, unroll=A_UNROLL)
        dqkv_ref[0] = dqf[...].astype(BF16)
        dqkv_ref[1] = dkacc[pad:pad + S, :].astype(BF16)
        dqkv_ref[2] = dvacc[pad:pad + S, :].astype(BF16)

    slab = ((S, LANES), lambda hp: (0, hp))
    padded = pltpu.VMEM((S + 2 * pad, LANES), F32)
    return _call(
        name, body, (4,),
        [(proj_g, (S, LANES), lambda hp: (0, hp)),
         (proj_g, (S, LANES), lambda hp: (0, 4 + hp)),
         (proj_g, (S, LANES), lambda hp: (0, 8 + hp)),
         (bias, (None, None, 3, 2 * A_TQ, A_WIN), lambda hp: (g, hp, 0, 0, 0)),
         (do_a,) + slab, (o_a,) + slab, (lse_tot,) + slab],
        [((3, S, GROUP_WIDTH_A), BF16, (3, S, LANES), lambda hp: (0, 0, hp)),
         ((4, 2 * A_TQ, A_WIN), F32, (None, 2 * A_TQ, A_WIN), lambda hp: (hp, 0, 0))],
        scratch=[pltpu.VMEM((S, LANES), F32)] * 3 + [padded] * 4,
        sem=("parallel",))


def _rope_tables(S):
    rows = S // GRID_W
    row = jnp.repeat(jnp.arange(rows, dtype=F32), GRID_W)
    col = jnp.tile(jnp.arange(GRID_W, dtype=F32), rows)
    n_freq = HEAD_DIM_B // 4
    freq = ROPE_THETA ** (-jnp.arange(n_freq, dtype=F32) / n_freq)
    ang = jnp.concatenate([row[:, None] * freq, col[:, None] * freq], axis=-1)
    cos, sin = jnp.cos(ang), jnp.sin(ang)
    return jnp.repeat(cos, 2, axis=-1), jnp.stack([-sin, sin], axis=-1).reshape(S, HEAD_DIM_B)


def _swap_pairs(y):
    lane = lax.broadcasted_iota(jnp.int32, y.shape, 1)
    return jnp.where(lane % 2 == 0, pltpu.roll(y, LANES - 1, 1), pltpu.roll(y, 1, 1))


def qkv_prep(proj_b, gains, cos_t, sin_t, name):
    S = proj_b.shape[0]
    ts = 256
    n_rot = N_HEADS_B + N_KV_B
    nh = n_rot + N_KV_B
    W = nh * LANES

    def body(x_ref, g_ref, c_ref, s_ref, o_ref):
        cv, sv = c_ref[...], s_ref[...]
        for hb in range(nh):
            cols = slice(hb * LANES, (hb + 1) * LANES)
            if hb < n_rot:
                xv = x_ref[:, cols].astype(F32)
                r = lax.rsqrt(jnp.mean(xv * xv, axis=-1, keepdims=True) + EPS)
                yv = xv * r * g_ref[:, cols]
                o_ref[:, cols] = (yv * cv + _swap_pairs(yv) * sv).astype(BF16)
            else:
                o_ref[:, cols] = x_ref[:, cols]

    return _call(name, body, (S // ts,),
                 [(proj_b, (ts, W), lambda i: (i, 0)), (gains, (1, W), lambda i: (0, 0)),
                  (cos_t, (ts, LANES), lambda i: (i, 0)), (sin_t, (ts, LANES), lambda i: (i, 0))],
                 [((S, W), BF16, (ts, W), lambda i: (i, 0))],
                 sem=("parallel",))[0]


def qk_prep_bwd(dr, proj_b, col0, gain, cos_t, sin_t, name):
    S, W = dr.shape
    H = W // LANES
    ts = 256
    xb = (col0 * LANES) // W

    def body(d_ref, x_ref, g_ref, c_ref, s_ref, dx_ref, dg_ref):
        i = pl.program_id(0)
        cv, sv, gv = c_ref[...], s_ref[...], g_ref[...]
        dgp = jnp.zeros((1, LANES), F32)
        for hb in range(H):
            cols = slice(hb * LANES, (hb + 1) * LANES)
            dout = d_ref[:, cols]
            dy = dout * cv + _swap_pairs(dout * sv)
            dx, dgt = _rms_bwd_tile(dy, x_ref[:, cols].astype(F32), gv)
            dx_ref[:, cols] = dx.astype(BF16)
            dgp = dgp + jnp.sum(dgt, axis=0, keepdims=True)

        @pl.when(i == 0)
        def _():
            dg_ref[...] = dgp

        @pl.when(i > 0)
        def _():
            dg_ref[...] += dgp

    return _call(name, body, (S // ts,),
                 [(dr, (ts, W), lambda i: (i, 0)), (proj_b, (ts, W), lambda i: (i, xb)),
                  (gain, (1, LANES), lambda i: (0, 0)),
                  (cos_t, (ts, LANES), lambda i: (i, 0)), (sin_t, (ts, LANES), lambda i: (i, 0))],
                 [((S, W), BF16, (ts, W), lambda i: (i, 0)),
                  ((1, LANES), F32, (1, LANES), lambda i: (0, 0))],
                 sem=("arbitrary",))


def _row_sums(x):
    hi = x.astype(BF16)
    lo = (x - hi.astype(F32)).astype(BF16)
    ones = jnp.ones((8, LANES), BF16)
    return (_dot(ones, hi, 1, 1) + _dot(ones, lo, 1, 1))[0:1, :]


def flash_fwd(qkv, name):
    S = qkv.shape[0]
    tq = B_TQ_FWD
    scale = HEAD_DIM_B ** -0.5

    hps = B_HEADS_PER_STEP

    def body(q_ref, k_ref, v_ref, o_ref, l_ref):
        k, v = k_ref[...], v_ref[...]
        for j in range(hps):
            cols = slice(j * LANES, (j + 1) * LANES)
            s = _dot(q_ref[:, cols], k, 1, 1)
            m = jnp.max(s, axis=-1, keepdims=True)
            e = jnp.exp2((s - m) * (scale * LOG2E))
            l = jnp.sum(e, axis=-1, keepdims=True)
            o_ref[:, cols] = (_dot(e.astype(BF16), v) / l).astype(BF16)
            lse = jnp.broadcast_to(m * scale + jnp.log(l), (tq, LANES))
            l_ref[j] = _row_sums(lse) * (1.0 / LANES)

    per = GQA_GROUP_B // hps
    heads = lambda g, h, i: (i, g * per + h)
    return _call(name, body, (N_KV_B, per, S // tq),
                 [(qkv, (tq, hps * LANES), heads),
                  (qkv, (S, LANES), lambda g, h, i: (0, N_HEADS_B + g)),
                  (qkv, (S, LANES), lambda g, h, i: (0, N_HEADS_B + N_KV_B + g))],
                 [((S, N_HEADS_B * LANES), BF16, (tq, hps * LANES), heads),
                  ((N_HEADS_B, 1, S), F32, (hps, 1, tq), lambda g, h, i: (g * per + h, 0, i))],
                 sem=("parallel", "parallel", "parallel"))


def flash_bwd(qkv, k_t, do_b, o_b, lse, name):
    S = qkv.shape[0]
    tq = B_TQ_BWD
    nq = S // tq
    scale = HEAD_DIM_B ** -0.5

    def body(q_ref, k_ref, v_ref, kt_ref, do_ref, o_ref, l_ref, dq_ref, dk_ref, dv_ref, dkacc, dvacc):
        h, i = pl.program_id(1), pl.program_id(2)

        @pl.when((h == 0) & (i == 0))
        def _():
            dkacc[...] = jnp.zeros(dkacc.shape, F32)
            dvacc[...] = jnp.zeros(dvacc.shape, F32)

        q = q_ref[...]
        dob = do_ref[...]
        t = _row_sums(dob.astype(F32) * o_ref[...].astype(F32))
        pt = jnp.exp2(_dot(k_ref[...], q, 1, 1) * (scale * LOG2E) - l_ref[...] * LOG2E)
        dsb = (pt * (_dot(v_ref[...], dob, 1, 1) - t)).astype(BF16)
        dvacc[...] += _dot(pt.astype(BF16), dob)
        dkacc[...] += _dot(dsb, q)
        dq_ref[...] = _dot(kt_ref[...], dsb).T * scale

        @pl.when((h == GQA_GROUP_B - 1) & (i == nq - 1))
        def _():
            dk_ref[...] = dkacc[...] * scale
            dv_ref[...] = dvacc[...].astype(BF16)

    head = lambda g, h, i: (i, g * GQA_GROUP_B + h)
    return _call(name, body, (N_KV_B, GQA_GROUP_B, nq),
                 [(qkv, (tq, LANES), head),
                  (qkv, (S, LANES), lambda g, h, i: (0, N_HEADS_B + g)),
                  (qkv, (S, LANES), lambda g, h, i: (0, N_HEADS_B + N_KV_B + g)),
                  (k_t, (LANES, S), lambda g, h, i: (g, 0)),
                  (do_b, (tq, LANES), head), (o_b, (tq, LANES), head),
                  (lse, (None, 1, tq), lambda g, h, i: (g * GQA_GROUP_B + h, 0, i))],
                 [((S, N_HEADS_B * LANES), F32, (tq, LANES), head),
                  ((S, N_KV_B * LANES), F32, (S, LANES), lambda g, h, i: (0, g)),
                  ((S, N_KV_B * LANES), BF16, (S, LANES), lambda g, h, i: (0, g))],
                 scratch=[pltpu.VMEM((S, LANES), F32)] * 2,
                 sem=("parallel", "arbitrary", "arbitrary"))


MERGE_TN = 512


def _mix_rows_spec(Gm, row0, n_slots, slot_map, cols=None, col_map=None):
    C = Gm.shape[2] if cols is None else cols
    cm = (lambda *idx: 0) if col_map is None else col_map
    return (Gm, (n_slots, LANES, C), lambda *idx: (slot_map(*idx), row0 // LANES, cm(*idx)))


def _gate_specs(proj_b, tm):
    first = PB_GATE_A // MERGE_TN
    return [(proj_b, (tm, MERGE_TN), lambda i, k=k: (i, first + k)) for k in range(4)]


def _whole_rows_spec(Gm, row0):
    return _mix_rows_spec(Gm, row0, N_DEV, lambda *idx: 0)


def merge_fwd(o_a, o_b, w_a, Gm, proj_b, b_gate, x, name):
    S, D = x.shape
    tm = 256

    def body(oa_ref, ob_ref, wa_ref, wb_ref, wo_ref, g0, g1, g2, g3, bg_ref, x_ref, m_ref, ya_ref, yb_ref, xo_ref):
        ya = _dot(oa_ref[...], wa_ref[...])
        yb = _dot(ob_ref[...], wb_ref[...].reshape(N_DEV * LANES, D))
        ga = _sigmoid(jnp.concatenate([g0[...], g1[...]], axis=1).astype(F32) + bg_ref[:, 0:D])
        gb = _sigmoid(jnp.concatenate([g2[...], g3[...]], axis=1).astype(F32) + bg_ref[:, D:2 * D])
        merged = (ga * ya + gb * yb).astype(BF16)
        m_ref[...] = merged
        ya_ref[...] = ya.astype(BF16)
        yb_ref[...] = yb.astype(BF16)
        xo_ref[...] = x_ref[...] + _dot(merged, wo_ref[...].reshape(N_DEV * LANES, D))

    rows = lambda a: (a, (tm, a.shape[1]), lambda i: (i, 0))
    out = ((S, D), BF16, (tm, D), lambda i: (i, 0))
    return _call(name, body, (S // tm,),
                 [rows(o_a), rows(o_b), (w_a, w_a.shape, lambda i: (0, 0)),
                  _whole_rows_spec(Gm, REST_WB), _whole_rows_spec(Gm, REST_WOUT)]
                 + _gate_specs(proj_b, tm) + [(b_gate, (1, 2 * D), lambda i: (0, 0)), rows(x)],
                 [out, out, out, ((S, D), F32, (tm, D), lambda i: (i, 0))], sem=("parallel",))


def merge_bwd(dx2, w_a, Gm, ya, yb, proj_b, b_gate, name):
    S, D = dx2.shape
    tm = 256

    def body(d_ref, wo_ref, wa_ref, wb_ref, ya_ref, yb_ref, g0, g1, g2, g3, bg_ref,
             dya_ref, dyb_ref, dg_ref, dbg_ref, doa_ref, dob_ref):
        i = pl.program_id(0)
        dm = _dot(d_ref[...].astype(BF16), wo_ref[...].reshape(N_DEV * LANES, D), 1, 1)
        ga = _sigmoid(jnp.concatenate([g0[...], g1[...]], axis=1).astype(F32) + bg_ref[:, 0:D])
        gb = _sigmoid(jnp.concatenate([g2[...], g3[...]], axis=1).astype(F32) + bg_ref[:, D:2 * D])
        dya = (dm * ga).astype(BF16)
        dyb = (dm * gb).astype(BF16)
        dya_ref[...] = dya
        dyb_ref[...] = dyb
        dpa = dm * ya_ref[...].astype(F32) * ga * (1.0 - ga)
        dpb = dm * yb_ref[...].astype(F32) * gb * (1.0 - gb)
        dg_ref[0] = dpa.astype(BF16)
        dg_ref[1] = dpb.astype(BF16)
        doa_ref[...] = _dot(dya, wa_ref[...], 1, 1)
        dob_ref[...] = _dot(dyb, wb_ref[...].reshape(N_DEV * LANES, D), 1, 1).astype(BF16)
        sa =jnp.sum(dpa, axis=0, keepdims=True)
        sb = jnp.sum(dpb, axis=0, keepdims=True)

        @pl.when(i == 0)
        def _():
            dbg_ref[0] = sa
            dbg_ref[1] = sb

        @pl.when(i > 0)
        def _():
            dbg_ref[0] += sa
            dbg_ref[1] += sb

    tile = ((tm, D), lambda i: (i, 0))
    return _call(
        name, body, (S // tm,),
        [(dx2,) + tile, _whole_rows_spec(Gm, REST_WOUT), (w_a, w_a.shape, lambda i: (0, 0)),
         _whole_rows_spec(Gm, REST_WB), (ya,) + tile, (yb,) + tile]
        + _gate_specs(proj_b, tm) + [(b_gate, (1, 2 * D), lambda i: (0, 0))],
        [((S, D), BF16) + tile, ((S, D), BF16) + tile,
         ((2, S, D), BF16, (2, tm, D), lambda i: (0, i, 0)),
         ((2, 1, D), F32, (2, 1, D), lambda i: (0, 0, 0)),
         ((S, w_a.shape[0]), F32, (tm, w_a.shape[0]), lambda i: (i, 0)),
         ((S, N_HEADS_B * LANES), BF16, (tm, N_HEADS_B * LANES), lambda i: (i, 0))],
        sem=("arbitrary",))


def weight_grad_rows(a, b, grads, row0, name):
    S, M = a.shape
    N = b.shape[1]
    tmm = 512
    tk = WGRAD_TK
    nk = S // tk
    prior = [] if grads is None else [grads]

    def body(*refs):
        a_ref, b_ref, o_ref, acc_ref = refs[len(prior):]
        k = pl.program_id(1)
        p = _dot(a_ref[...], b_ref[...].astype(BF16), 0, 0)

        @pl.when(k == 0)
        def _():
            acc_ref[...] = p

        @pl.when(k > 0)
        def _():
            acc_ref[...] += p

        @pl.when(k == nk - 1)
        def _():
            o_ref[...] = acc_ref[...].astype(BF16).reshape(tmm // LANES, LANES, N)

    return pl.pallas_call(
        body,
        out_shape=jax.ShapeDtypeStruct((N_DEV, MIX_ROWS, N), BF16),
        grid=(M // tmm, nk),
        in_specs=[pl.BlockSpec(memory_space=pl.ANY)] * len(prior)
        + [pl.BlockSpec((tk, tmm), lambda j, k: (k, j)),
           pl.BlockSpec((tk, N), lambda j, k: (k, 0))],
        out_specs=pl.BlockSpec((tmm // LANES, LANES, N), lambda j, k: (j, row0 // LANES, 0)),
        scratch_shapes=[pltpu.VMEM((tmm, N), F32)],
        input_output_aliases={0: 0} if prior else {},
        name=name,
        compiler_params=pltpu.CompilerParams(dimension_semantics=("parallel", "arbitrary"),
                                             vmem_limit_bytes=VMEM_LIMIT),
    )(*prior, a, b)


def weight_grad_plain(a, b, name):
    S, M = a.shape
    N = b.shape[1]
    tk = WGRAD_TK
    nk = S // tk

    def body(a_ref, b_ref, o_ref, acc_ref):
        k = pl.program_id(0)
        p = _dot(a_ref[...], b_ref[...], 0, 0)

        @pl.when(k == 0)
        def _():
            acc_ref[...] = p

        @pl.when(k > 0)
        def _():
            acc_ref[...] += p

        @pl.when(k == nk - 1)
        def _():
            o_ref[...] = acc_ref[...].astype(BF16)

    return _call(name, body, (nk,),
                 [(a, (tk, M), lambda k: (k, 0)), (b, (tk, N), lambda k: (k, 0))],
                 [((M, N), BF16, (M, N), lambda k: (0, 0))],
                 scratch=[pltpu.VMEM((M, N), F32)], sem=("arbitrary",))[0]


def local_step(x, tgt, p, get_g1_up, get_g1_down, get_gm_in, get_gm_rest, get_g2, emit, start_token):
    S, D = x.shape
    after = lambda t: t[0:1, 0:1]
    buckets = _bucket_tables()
    cos_t, sin_t = _rope_tables(S)
    gains = jnp.concatenate([jnp.tile(p["q_norm"], (1, N_HEADS_B)), jnp.tile(p["k_norm"], (1, N_KV_B)),
                             jnp.ones((1, N_KV_B * LANES), F32)], axis=1)

    n1 = rms_fwd(x, p["ffn1_norm"] + after(start_token), "ffn1_norm")
    bias = bias_build(p["rel_bias"] + after(start_token), buckets)
    g1_up = get_g1_up((n1, bias))
    ab1 = ffn_up(n1, (g1_up, None), "ffn1_up")
    G1 = (g1_up, get_g1_down(ab1))
    x1 = ffn_down(ab1, G1, x, "ffn1_down")

    hm = rms_fwd(x1, p["mix_norm"], "mix_norm")
    Gw = get_gm_in(hm)
    n_a = A_QKV_WIDTH // PROJ_TN
    proj_a = [in_proj(hm, Gw, g, 3, "in_proj_a%d" % g, tile_stride=3) for g in range(3)]
    proj_b = in_proj(hm, Gw, n_a, PB_WIDTH // PROJ_TN, "in_proj_b")

    outs, lses = [], []
    for g in range(3):
        o, l = a_fwd(proj_a[g], bias, g, "a_fwd_%d" % g)
        outs.append(o)
        lses.append(l)
    o_a, lse_tot = a_combine(outs, lses, "a_combine")

    qkv = qkv_prep(proj_b, gains, cos_t, sin_t, "qkv_prep")
    k_t = qkv[:, N_HEADS_B * LANES:(N_HEADS_B + N_KV_B) * LANES].T
    o_b, lse_b = flash_fwd(qkv, "flash_fwd")

    Gm = get_gm_rest(o_b)
    w_a = Gm[:, REST_WA:REST_ROWS, :].reshape(N_DEV, GROUP_WIDTH_A, LANES).transpose(1, 0, 2).reshape(GROUP_WIDTH_A, D)
    merged, ya, yb, x2 = merge_fwd(o_a, o_b, w_a, Gm, proj_b, p["b_gate"], x1, "merge_fwd")

    G2 = get_g2(x2)
    n2 = rms_fwd(x2, p["ffn2_norm"], "ffn2_norm")
    ab2, x3 = ffn_fwd(n2, G2, x2, "ffn2_fwd")

    loss, dx3, dx3_b, d_final = final_loss(x3, tgt, p["final_norm"], "final_loss")

    dab2, dx2, dx2_b, d_ffn2_norm = ffn_bwd_hidden_input(dx3_b, dx3, ab2, G2, x2, p["ffn2_norm"], "ffn2_bwd")
    gw2 = ffn_bwd_weights(dx3_b, ab2, dab2, n2, "ffn2_bwd")
    t2 = emit("ffn2", gw2)

    dya, dyb, dgate, dbg, do_a, do_b = merge_bwd(dx2_b, w_a, Gm, ya, yb, proj_b, p["b_gate"] + after(t2),
                                                 "merge_bwd")
    gm_grads = weight_grad_rows(merged, dx2_b, None, MIX_WOUT, "dw_out")
    gm_grads = weight_grad_rows(o_b, dyb, gm_grads, MIX_WB, "dw_branch_b")
    dw_a = weight_grad_plain(o_a, dya, "dw_branch_a")

    dq_r, dk_r, dv_b = flash_bwd(qkv, k_t, do_b, o_b, lse_b, "flash_bwd")
    dq_b, d_q_norm = qk_prep_bwd(dq_r, proj_b, 0, p["q_norm"], cos_t, sin_t, "q_prep_bwd")
    dk_b, d_k_norm = qk_prep_bwd(dk_r, proj_b, N_HEADS_B, p["k_norm"], cos_t, sin_t, "k_prep_bwd")

    dqkv, dbs = [], []
    for g in range(3):
        dg_, db = a_bwd(proj_a[g], bias, do_a, o_a, lse_tot, g, "a_bwd_%d" % g)
        dqkv.append(dg_)
        dbs.append(db)
    d_rel_bias = bias_bwd(jnp.stack(dbs, axis=0).reshape(3, HEADS_PER_GROUP_A, A_TQ, A_WIN), buckets)

    dproj = _dproj_pieces(dqkv, dq_b, jnp.concatenate([dk_b, dv_b], axis=1), dgate)
    gm_grads = in_proj_bwd_dw(dproj, hm, gm_grads, "in_proj_bwd")
    dw_a_sh = dw_a.reshape(GROUP_WIDTH_A, N_DEV, LANES).transpose(1, 0, 2).reshape(N_DEV, MIX_ROWS - MIX_WA, D)
    gm_grads = lax.dynamic_update_slice(gm_grads, dw_a_sh, (0, MIX_WA, 0))
    tm = emit("mix", gm_grads)
    dx1, dx1_b, d_mix_norm = in_proj_bwd_dh(dproj, Gw, x1, p["mix_norm"] + after(tm), dx2, "in_proj_bwd")

    dab1 = ffn_bwd_hidden(dx1_b, ab1, G1, "ffn1_bwd")
    gw1 = ffn_bwd_weights(dx1_b, ab1, dab1, n1, "ffn1_bwd")
    t1 = emit("ffn1", gw1)
    dx0, d_ffn1_norm = ffn_bwd_input(dab1, G1, x, p["ffn1_norm"] + after(t1), dx1, "ffn1_bwd")

    small = dict(ffn1_norm=d_ffn1_norm, mix_norm=d_mix_norm, b_gate=dbg.reshape(1, 2 * D),
                 q_norm=d_q_norm, k_norm=d_k_norm, rel_bias=d_rel_bias, ffn2_norm=d_ffn2_norm,
                 final_norm=d_final)
    return loss, dx0, small


def _pack_small(t, loss_row):
    row6 = jnp.concatenate([t["q_norm"].reshape(1, -1), t["k_norm"].reshape(1, -1), t["rel_bias"].reshape(1, -1)], axis=1)
    return jnp.concatenate([t["ffn1_norm"].reshape(1, -1), t["mix_norm"].reshape(1, -1), t["b_gate"].reshape(2, -1),
                            t["ffn2_norm"].reshape(1, -1), t["final_norm"].reshape(1, -1), row6, loss_row], axis=0)


def _unpack_small(a, shapes):
    return dict(ffn1_norm=a[0:1].reshape(shapes["ffn1_norm"]), mix_norm=a[1:2].reshape(shapes["mix_norm"]),
                b_gate=a[2:4].reshape(shapes["b_gate"]), ffn2_norm=a[4:5].reshape(shapes["ffn2_norm"]),
                final_norm=a[5].reshape(shapes["final_norm"]), q_norm=a[6:7, 0:128].reshape(shapes["q_norm"]),
                k_norm=a[6:7, 128:256].reshape(shapes["k_norm"]), rel_bias=a[6, 256:1024].reshape(shapes["rel_bias"]))


SMALL = ("ffn1_norm", "mix_norm", "b_gate", "q_norm", "k_norm", "rel_bias", "ffn2_norm", "final_norm")
ORDER = ("ffn1_norm", "ffn1_w1", "ffn1_w3", "ffn1_w2", "mix_norm", "w_in", "b_gate", "q_norm", "k_norm", "rel_bias",
         "w_branch_a", "w_branch_b", "w_out", "ffn2_norm", "ffn2_w1", "ffn2_w3", "ffn2_w2", "final_norm")


def kernel(x, ffn1_norm, ffn1_w1, ffn1_w3, ffn1_w2, mix_norm, w_in, b_gate, q_norm, k_norm, rel_bias, w_branch_a, w_branch_b, w_out, ffn2_norm, ffn2_w1, ffn2_w3, ffn2_w2, final_norm, loss_target, m_ffn1_norm, m_ffn1_w1, m_ffn1_w3, m_ffn1_w2, m_mix_norm, m_w_in, m_b_gate, m_q_norm, m_k_norm, m_rel_bias, m_w_branch_a, m_w_branch_b, m_w_out, m_ffn2_norm, m_ffn2_w1, m_ffn2_w3, m_ffn2_w2, m_final_norm, v_ffn1_norm, v_ffn1_w1, v_ffn1_w3, v_ffn1_w2, v_mix_norm, v_w_in, v_b_gate, v_q_norm, v_k_norm, v_rel_bias, v_w_branch_a, v_w_branch_b, v_w_out, v_ffn2_norm, v_ffn2_w1, v_ffn2_w3, v_ffn2_w2, v_final_norm):
    args = dict(locals())
    w = {n: args[n] for n in ORDER}
    m = {n: args["m_" + n] for n in ORDER}
    v = {n: args["v_" + n] for n in ORDER}
    D = x.shape[2]

    blocks = (
        ("ffn1_up", jnp.concatenate([ffn1_w1[0].T, ffn1_w3[0].T], axis=0)),
        ("ffn1_down", ffn1_w2[0]),
        ("mix_in", w_in[0]),
        ("mix_rest", jnp.concatenate([w_branch_b[0], w_out[0], w_branch_a[0].reshape(REST_ROWS - REST_WA, D)], axis=0)),
        ("ffn2", jnp.concatenate([ffn2_w1[0].T, ffn2_w3[0].T, ffn2_w2[0]], axis=0)),
    )
    direct = ("mix_rest", "ffn2")
    started = all_gather_start_all([(b.astype(BF16), tag in direct) for tag, b in blocks], "all_gather_start")
    gathers = {tag: s for (tag, _), s in zip(blocks, started)}
    start_token = started[0][4]

    def gathered(tag):
        def get(after):
            if tag in direct:
                return all_gather_place_own(*_split_wait("all_gather_" + tag + "_wait", gathers[tag], N_DEV - 1, after),
                                            "all_gather_" + tag + "_own")
            return all_gather_finish(*_split_wait("all_gather_" + tag + "_wait", gathers[tag], 4, after),
                                     "all_gather_" + tag + "_finish")
        return get

    core = lax.axis_index("c").astype(jnp.int32).reshape(1)
    chip = (2 * lax.axis_index("x") + lax.axis_index("y")).astype(jnp.int32).reshape(1)
    device = 2 * chip + core
    exchanges = {}

    def emit(tag, gw):
        if tag == "ffn1":
            (theirs,) = reduce_scatter_pair([gw], "reduce_scatter_pair_" + tag)
            part = pair_add(gw, theirs, core, "pair_add_" + tag)
            exchanges[tag] = reduce_scatter_start(part, "reduce_scatter_" + tag + "_start")
        else:
            exchanges[tag] = reduce_scatter_start_direct(gw, "reduce_scatter_" + tag + "_start")
        return exchanges[tag][4]

    small_p = dict(ffn1_norm=ffn1_norm, mix_norm=mix_norm, b_gate=b_gate, q_norm=q_norm, k_norm=k_norm,
                   rel_bias=rel_bias, ffn2_norm=ffn2_norm, final_norm=final_norm.reshape(1, D))
    loss_p, grad_x, small_g = local_step(x[0], loss_target[0], small_p, gathered("ffn1_up"), gathered("ffn1_down"),
                                         gathered("mix_in"), gathered("mix_rest"), gathered("ffn2"), emit, start_token)

    def landed(tag, after):
        n_others, me = (3, chip) if tag == "ffn1" else (N_DEV - 1, device)
        return tuple(_split_wait("reduce_scatter_" + tag + "_wait", exchanges[tag], n_others, after)) + (me,)

    grads, delta, new_m, new_v = {}, {}, {}, {}

    def finish(n, part, land, me, off, blk, transposed=False):
        shp = w[n].shape
        if transposed:
            to2 = lambda a: a.reshape(shp[-2], shp[-1]).T
            back = lambda a: a.T.reshape(shp)
        else:
            to2 = lambda a: a.reshape(shp[-2], shp[-1])
            back = lambda a: a.reshape(shp)
        res = sum_adamw(part, land, me, off, blk, to2(w[n]), to2(m[n]), to2(v[n]), "update_" + n)
        grads[n], delta[n], new_m[n], new_v[n] = [back(a) for a in res]

    last_token = exchanges["ffn1"][4]
    for tag, after in (("ffn2", last_token), ("ffn1", grad_x)):
        group = landed(tag, after)
        finish(tag + "_w1", *group, 0, FFN_SHARD, transposed=True)
        finish(tag + "_w3", *group, FFN_SHARD, FFN_SHARD, transposed=True)
        finish(tag + "_w2", *group, 2 * FFN_SHARD, FFN_SHARD)
        if tag == "ffn2":
            group_m = landed("mix", last_token)
            finish("w_in", *group_m, MIX_WIN, LANES)
            finish("w_branch_b", *group_m, MIX_WB, LANES)
            finish("w_out", *group_m, MIX_WOUT, LANES)
            grads["w_branch_a"] = sum_landed(*group_m, MIX_WA, MIX_ROWS - MIX_WA, MIX_ROWS - MIX_WA,
                                             "w_branch_a_sum").reshape(w_branch_a.shape)
    loss_row = jnp.pad(loss_p, ((0, 0), (0, D - LANES)))
    smalls = small_all_gather(_pack_small(small_g, loss_row))
    small_sum = sum_slots(smalls, 0, N_DEV, N_DEV, "small_sum")
    small_shapes = {n: w[n].shape for n in SMALL}
    grads.update(_unpack_small(small_sum, small_shapes))
    loss = small_sum[7, 0]

    n = "w_branch_a"
    two_d = lambda a: a.reshape(w[n].shape[-2], w[n].shape[-1])
    d_, m_, v_ = adamw(two_d(w[n]), two_d(grads[n]), two_d(m[n]), two_d(v[n]), "adamw_" + n)
    delta[n], new_m[n], new_v[n] = [a.reshape(w[n].shape) for a in (d_, m_, v_)]
    zero_row = jnp.zeros((1, D), F32)
    pack = lambda t: _pack_small({n: t[n] for n in SMALL}, zero_row)
    d_, m_, v_ = adamw(pack(w), small_sum, pack(m), pack(v), "adamw_small")
    for src, dst in ((d_, delta), (m_, new_m), (v_, new_v)):
        dst.update(_unpack_small(src, small_shapes))

    return (loss, grad_x[None], *[grads[n] for n in ORDER], *[delta[n] for n in ORDER],
            *[new_m[n] for n in ORDER], *[new_v[n] for n in ORDER])
```

```python
import math

import jax
import jax.numpy as jnp
from jax import lax
from jax.experimental import pallas as pl
from jax.experimental.pallas import tpu as pltpu

F32 = jnp.float32
BF16 = jnp.bfloat16
MESH = pl.DeviceIdType.MESH

V7X_VMEM_BYTES = 64 * 1024 * 1024
VMEM_LIMIT = V7X_VMEM_BYTES - 8 * 1024 * 1024
LANES = 128

N_DEV = 8
EPS = 1e-6
NEG_INF = -1e30

DILATIONS = (1, 4, 16)
HALF_WINDOW = 64
HEAD_DIM_A = 64
HEADS_PER_GROUP_A = 8
GROUP_WIDTH_A = 512
A_QKV_WIDTH = 4608
A_GROUP_QKV = A_QKV_WIDTH // 3
A_TQ = 128
A_WIN = A_TQ + 2 * HALF_WINDOW
A_UNROLL = 8
A_SCALE = HEAD_DIM_A ** -0.5
WGRAD_TK = 2048
HEAD_DIM_B = 128
N_HEADS_B = 8
N_KV_B = 2
GQA_GROUP_B = 4
GRID_W = 64
ROPE_THETA = 10000.0
B_TQ_FWD = 256
B_TQ_BWD = 512
B_HEADS_PER_STEP = 4
LOG2E = 1.4426950408889634
N_BUCKETS = 32
MAX_DISTANCE = 1024
PB_GATE_A = 1536

ADAM_LR = 0.001
ADAM_B1 = 0.9
ADAM_B2 = 0.999
ADAM_EPS = 1e-08
ADAM_WD = 0.01
ADAM_STEP = 10

FFN_SHARD = 352
MIX_WIN, MIX_WB, MIX_WOUT, MIX_WA = 0, 1024, 1152, 1280
MIX_ROWS = 1344
REST_WB, REST_WOUT, REST_WA, REST_ROWS = 0, 128, 256, 320


def _dot(a, b, ca=1, cb=0):
    return lax.dot_general(a, b, (((ca,), (cb,)), ((), ())), preferred_element_type=F32)


def _call(name, body, grid, ins, outs, scratch=(), sem=None, aliases=None):
    ins = [tuple(i) + (None,) * (4 - len(i)) for i in ins]
    res = pl.pallas_call(
        body,
        out_shape=[jax.ShapeDtypeStruct(s, d) for (s, d, _, _) in outs],
        grid=grid,
        in_specs=[pl.BlockSpec(bs, im, pipeline_mode=pm) for (_, bs, im, pm) in ins],
        out_specs=[pl.BlockSpec(bs, im) for (_, _, bs, im) in outs],
        scratch_shapes=list(scratch),
        name=name,
        input_output_aliases=aliases or {},
        compiler_params=pltpu.CompilerParams(dimension_semantics=sem, vmem_limit_bytes=VMEM_LIMIT),
    )(*[i[0] for i in ins])
    return res


def _sigmoid(x):
    return 0.5 * jnp.tanh(0.5 * x) + 0.5


def _position():
    return lax.axis_index("x"), lax.axis_index("y"), lax.axis_index("c")


def _hbm_specs(n):
    return [pl.BlockSpec(memory_space=pl.ANY) for _ in range(n)]


PAIR_BUFFERS = 4


def reduce_scatter_pair(grads, name):
    n = len(grads)
    C = grads[0].shape[2]
    half = [g.shape[1] // 2 for g in grads]
    chunks = [(i, q, hf) for i in range(n) for q in range(4) for hf in range(2)]
    nb = PAIR_BUFFERS

    def body(*refs):
        ins, theirs = refs[:n], refs[n:2 * n]
        buf, load_sems, send_sems, recv_sems = refs[2 * n:]
        x, y, c = _position()
        sibling = (x, y, 1 - c)

        def load(k):
            i, q, hf = chunks[k]
            r = half[i]
            return pltpu.make_async_copy(ins[i].at[2 * q + (1 - c), pl.ds(hf * r, r), :],
                                         buf.at[k % nb, pl.ds(0, r), :], load_sems.at[k % nb])

        def send(k):
            i, q, hf = chunks[k]
            r = half[i]
            return pltpu.make_async_remote_copy(
                src_ref=buf.at[k % nb, pl.ds(0, r), :], dst_ref=theirs[i].at[q, pl.ds(hf * r, r), :],
                send_sem=send_sems.at[k % nb], recv_sem=recv_sems.at[i],
                device_id=sibling, device_id_type=MESH)

        for k in range(len(chunks) + 1):
            if k < len(chunks):
                if k >= nb:
                    send(k - nb).wait_send()
                load(k).start()
            if k >= 1:
                load(k - 1).wait()
                send(k - 1).start()
        for k in range(max(0, len(chunks) - nb), len(chunks)):
            send(k).wait_send()
        for i in range(n):
            pltpu.make_async_remote_copy(
                src_ref=theirs[i], dst_ref=theirs[i], send_sem=send_sems.at[0], recv_sem=recv_sems.at[i],
                device_id=sibling, device_id_type=MESH).wait_recv()

    return pl.pallas_call(
        body,
        out_shape=[jax.ShapeDtypeStruct((4,) + g.shape[1:], g.dtype) for g in grads],
        in_specs=_hbm_specs(n),
        out_specs=_hbm_specs(n),
        scratch_shapes=[pltpu.VMEM((nb, max(half), C), grads[0].dtype), pltpu.SemaphoreType.DMA((nb,)),
                        pltpu.SemaphoreType.DMA((nb,)), pltpu.SemaphoreType.DMA((n,))],
        name=name,
        compiler_params=pltpu.CompilerParams(vmem_limit_bytes=VMEM_LIMIT),
    )(*grads)


_HBM_SPEC = pl.BlockSpec(memory_space=pltpu.HBM)
_SEM_SPEC = pl.BlockSpec(memory_space=pltpu.SEMAPHORE)
_TOKEN_SPEC = pl.BlockSpec(memory_space=pltpu.VMEM)
_DATAFLOW = pltpu.SideEffectType.DATAFLOW_SIDE_EFFECTING


def _split_start_many(name, exchanges):
    n = len(exchanges)

    def full_body(*refs):
        srcs, lands = refs[:n], refs[n:2 * n]
        sems = refs[2 * n:4 * n]
        token = refs[-1]
        for i, (body, _, _) in enumerate(exchanges):
            body(srcs[i], lands[i], sems[2 * i], sems[2 * i + 1])
        token[...] = jnp.zeros_like(token)

    srcs = [pltpu.with_memory_space_constraint(src, pltpu.HBM) for _, src, _ in exchanges]
    lands = [pltpu.with_memory_space_constraint(lax.empty(shape, src.dtype), pltpu.HBM)
             for _, src, shape in exchanges]
    res = pl.pallas_call(
        full_body, name=name,
        out_shape=(pltpu.SemaphoreType.DMA(()),) * (2 * n)
        + tuple(pltpu.HBM(a.shape, a.dtype) for a in srcs + lands) + (jax.ShapeDtypeStruct((8, LANES), F32),),
        in_specs=(_HBM_SPEC,) * (2 * n),
        out_specs=(_SEM_SPEC,) * (2 * n) + (_HBM_SPEC,) * (2 * n) + (_TOKEN_SPEC,),
        input_output_aliases={i: 2 * n + i for i in range(2 * n)},
        compiler_params=pltpu.CompilerParams(has_side_effects=_DATAFLOW),
    )(*srcs, *lands)
    return [(res[2 * i], res[2 * i + 1], res[2 * n + i], res[3 * n + i], res[-1]) for i in range(n)]


def _split_start(name, body, src, land_shape):
    return _split_start_many(name, [(body, src, land_shape)])[0]


def _split_wait(name, started, n_blocks, after):
    send_sem, recv_sem, src_thru, land_thru, _ = started
    after = after if isinstance(after, tuple) else (after,)

    def body(src_ref, land_ref, send_sem, recv_sem, *rest):
        x, y, c = _position()
        blocks = land_ref.at[pl.ds(0, n_blocks)]
        copy = pltpu.make_async_remote_copy(src_ref=blocks, dst_ref=blocks, send_sem=send_sem, recv_sem=recv_sem,
                                            device_id=(x, y, c), device_id_type=MESH)
        copy.wait_send()
        copy.wait_recv()

    return pl.pallas_call(
        body, name=name,
        out_shape=(pltpu.HBM(src_thru.shape, src_thru.dtype), pltpu.HBM(land_thru.shape, land_thru.dtype)),
        in_specs=(_HBM_SPEC, _HBM_SPEC, _SEM_SPEC, _SEM_SPEC) + (pl.BlockSpec(memory_space=pl.ANY),) * len(after),
        out_specs=(_HBM_SPEC, _HBM_SPEC),
        input_output_aliases={0: 0, 1: 1},
        compiler_params=pltpu.CompilerParams(has_side_effects=_DATAFLOW),
    )(src_thru, land_thru, send_sem, recv_sem, *after)


def all_gather_start_all(blocks, name):
    def starter(direct):
        def body(b_ref, land_ref, send_sem, recv_sem):
            x, y, c = _position()
            peers = _other_devices(x, y, c) if direct else [(x, y, 1 - c), (1 - x, y, c), (x, 1 - y, c),
                                                            (1 - x, 1 - y, c)]
            for peer in peers:
                pltpu.make_async_remote_copy(src_ref=b_ref, dst_ref=land_ref.at[4 * x + 2 * y + c],
                                             send_sem=send_sem, recv_sem=recv_sem,
                                             device_id=peer, device_id_type=MESH).start()
        return body

    return _split_start_many(name, [(starter(direct), block, (N_DEV,) + block.shape) for block, direct in blocks])


def all_gather_finish(block, land, name):
    R, C = block.shape

    def body(b_ref, land_in, land_ref, stage, load_sems, send_sems, recv_sems, own_sem):
        x, y, c = _position()
        sibling = (x, y, 1 - c)
        chips = [(1 - x, y), (x, 1 - y), (1 - x, 1 - y)]
        own_in = pltpu.make_async_copy(b_ref, stage.at[3], load_sems.at[3])
        own_in.start()
        loads = [pltpu.make_async_copy(land_in.at[4 * px + 2 * py + c], stage.at[j], load_sems.at[j])
                 for j, (px, py) in enumerate(chips)]
        for ld in loads:
            ld.start()
        sends = []
        for j, (px, py) in enumerate(chips):
            loads[j].wait()
            dst = land_ref.at[4 * px + 2 * py + c]
            cp = pltpu.make_async_remote_copy(src_ref=stage.at[j], dst_ref=dst, send_sem=send_sems.at[j],
                                              recv_sem=recv_sems.at[j], device_id=sibling, device_id_type=MESH)
            cp.start()
            sends.append(cp)
        own_in.wait()
        own_out = pltpu.make_async_copy(stage.at[3], land_ref.at[4 * x + 2 * y + c], own_sem)
        own_out.start()
        for j, (px, py) in enumerate(chips):
            dst = land_ref.at[4 * px + 2 * py + (1 - c)]
            pltpu.make_async_remote_copy(src_ref=stage.at[j], dst_ref=dst, send_sem=send_sems.at[j],
                                         recv_sem=recv_sems.at[j], device_id=sibling,
                                         device_id_type=MESH).wait_recv()
        for cp in sends:
            cp.wait_send()
        own_out.wait()

    return pl.pallas_call(
        body,
        out_shape=jax.ShapeDtypeStruct(land.shape, land.dtype),
        in_specs=_hbm_specs(2),
        out_specs=pl.BlockSpec(memory_space=pl.ANY),
        scratch_shapes=[pltpu.VMEM((4, R, C), block.dtype), pltpu.SemaphoreType.DMA((4,)),
                        pltpu.SemaphoreType.DMA((3,)), pltpu.SemaphoreType.DMA((3,)), pltpu.SemaphoreType.DMA],
        input_output_aliases={1: 0},
        name=name,
        compiler_params=pltpu.CompilerParams(vmem_limit_bytes=VMEM_LIMIT),
    )(block, land)


def reduce_scatter_start(parts, name):
    def body(p_ref, land_ref, send_sem, recv_sem):
        x, y, c = _position()
        for px, py in [(1 - x, y), (x, 1 - y), (1 - x, 1 - y)]:
            pltpu.make_async_remote_copy(src_ref=p_ref.at[2 * px + py], dst_ref=land_ref.at[2 * x + y],
                                         send_sem=send_sem, recv_sem=recv_sem,
                                         device_id=(px, py, c), device_id_type=MESH).start()

    return _split_start(name, body, parts, parts.shape)


def _other_devices(x, y, c):
    return [(1 - x if k & 4 else x, 1 - y if k & 2 else y, 1 - c if k & 1 else c) for k in range(1, N_DEV)]


def all_gather_place_own(block, land, name):
    R, C = block.shape

    def body(b_ref, land_in, land_ref, stage, sems):
        x, y, c = _position()
        load = pltpu.make_async_copy(b_ref, stage, sems.at[0])
        load.start()
        load.wait()
        store = pltpu.make_async_copy(stage, land_ref.at[4 * x + 2 * y + c], sems.at[1])
        store.start()
        store.wait()

    return pl.pallas_call(
        body,
        out_shape=jax.ShapeDtypeStruct(land.shape, land.dtype),
        in_specs=_hbm_specs(2),
        out_specs=pl.BlockSpec(memory_space=pl.ANY),
        scratch_shapes=[pltpu.VMEM((R, C), block.dtype), pltpu.SemaphoreType.DMA((2,))],
        input_output_aliases={1: 0},
        name=name,
    )(block, land)


def reduce_scatter_start_direct(grads, name):
    def body(g_ref, land_ref, send_sem, recv_sem):
        x, y, c = _position()
        for px, py, pc in _other_devices(x, y, c):
            pltpu.make_async_remote_copy(src_ref=g_ref.at[4 * px + 2 * py + pc],
                                         dst_ref=land_ref.at[4 * x + 2 * y + c],
                                         send_sem=send_sem, recv_sem=recv_sem,
                                         device_id=(px, py, pc), device_id_type=MESH).start()

    return _split_start(name, body, grads, grads.shape)


def small_all_gather(small):
    def body(small_ref, smalls, s_send, s_recv, s_local):
        x, y, c = _position()
        me = 4 * x + 2 * y + c
        lc = pltpu.make_async_copy(small_ref, smalls.at[me], s_local)
        lc.start()
        remote = []
        k = 0
        for dx in (0, 1):
            for dy in (0, 1):
                for dc in (0, 1):
                    if dx + dy + dc == 0:
                        continue
                    peer = (1 - x if dx else x, 1 - y if dy else y, 1 - c if dc else c)
                    rc = pltpu.make_async_remote_copy(
                        src_ref=small_ref, dst_ref=smalls.at[me],
                        send_sem=s_send.at[k], recv_sem=s_recv.at[k],
                        device_id=peer, device_id_type=MESH)
                    rc.start()
                    remote.append(rc)
                    k += 1
        for rc in remote:
            rc.wait()
        lc.wait()

    return pl.pallas_call(
        body,
        out_shape=jax.ShapeDtypeStruct((N_DEV,) + small.shape, small.dtype),
        in_specs=_hbm_specs(1),
        out_specs=pl.BlockSpec(memory_space=pl.ANY),
        scratch_shapes=[pltpu.SemaphoreType.DMA((7,)), pltpu.SemaphoreType.DMA((7,)), pltpu.SemaphoreType.DMA],
        name="small_all_gather",
    )(small)


def pair_add(grads, theirs, core, name):
    _, R, C = theirs.shape
    tr = R // 2

    def body(c_ref, a_ref, b_ref, o_ref):
        o_ref[...] = (a_ref[...].astype(F32) + b_ref[...].astype(F32)).astype(BF16)

    return pl.pallas_call(
        body,
        out_shape=jax.ShapeDtypeStruct(theirs.shape, BF16),
        grid_spec=pltpu.PrefetchScalarGridSpec(
            num_scalar_prefetch=1, grid=(4, R // tr),
            in_specs=[pl.BlockSpec((None, tr, C), lambda q, i, c: (2 * q + c[0], i, 0)),
                      pl.BlockSpec((None, tr, C), lambda q, i, c: (q, i, 0))],
            out_specs=pl.BlockSpec((None, tr, C), lambda q, i, c: (q, i, 0))),
        name=name,
        compiler_params=pltpu.CompilerParams(dimension_semantics=("parallel", "parallel"),
                                             vmem_limit_bytes=VMEM_LIMIT),
    )(core, grads, theirs)


def sum_slots(recv, off, rows, blk, name):
    nq, _, C = recv.shape
    ob = off // blk

    def body(r_ref, o_ref):
        acc = r_ref[0].astype(F32)
        for q in range(1, nq):
            acc = acc + r_ref[q].astype(F32)
        o_ref[...] = acc

    return _call(name, body, (rows // blk,),
                 [(recv, (nq, blk, C), lambda i: (0, ob + i, 0))],
                 [((rows, C), F32, (blk, C), lambda i: (i, 0))], sem=("parallel",))[0]


def _sum_terms(refs):
    acc = refs[0][...].astype(F32)
    for r in refs[1:]:
        acc = acc + r[...].astype(F32)
    return acc


def sum_landed(own, land, me, off, rows, blk, name):
    n, _, C = land.shape
    ob = off // blk

    def body(c_ref, *refs):
        refs[n][...] = _sum_terms(refs[:n])

    def entry(flip):
        return pl.BlockSpec((None, blk, C), lambda i, c: (c[0] ^ flip, ob + i, 0))

    return pl.pallas_call(
        body,
        out_shape=jax.ShapeDtypeStruct((rows, C), F32),
        grid_spec=pltpu.PrefetchScalarGridSpec(
            num_scalar_prefetch=1, grid=(rows // blk,),
            in_specs=[entry(k) for k in range(n)],
            out_specs=pl.BlockSpec((blk, C), lambda i, c: (i, 0))),
        name=name,
        compiler_params=pltpu.CompilerParams(dimension_semantics=("parallel",), vmem_limit_bytes=VMEM_LIMIT),
    )(me, own, *([land] * (n - 1)))


def _adamw_update(wv, gv, mv, vv):
    nm = ADAM_B1 * mv + (1.0 - ADAM_B1) * gv
    nv = ADAM_B2 * vv + (1.0 - ADAM_B2) * (gv * gv)
    c1 = 1.0 / (1.0 - ADAM_B1 ** ADAM_STEP)
    c2 = 1.0 / (1.0 - ADAM_B2 ** ADAM_STEP)
    return -ADAM_LR * ((nm * c1) / (jnp.sqrt(nv * c2) + ADAM_EPS) + ADAM_WD * wv), nm, nv


def sum_adamw(own, land, me, off, blk, w, m, v, name):
    rows, C = w.shape
    n = land.shape[0]
    ob = off // blk

    def body(c_ref, *refs):
        w_ref, m_ref, v_ref, g_out, d_out, m_out, v_out = refs[n:]
        gv = _sum_terms(refs[:n])
        g_out[...] = gv
        d_out[...], m_out[...], v_out[...] = _adamw_update(w_ref[...], gv, m_ref[...], v_ref[...])

    def entry(flip):
        return pl.BlockSpec((None, blk, C), lambda i, c: (c[0] ^ flip, ob + i, 0))

    plain = pl.BlockSpec((blk, C), lambda i, c: (i, 0))
    return pl.pallas_call(
        body,
        out_shape=[jax.ShapeDtypeStruct((rows, C), F32)] * 4,
        grid_spec=pltpu.PrefetchScalarGridSpec(
            num_scalar_prefetch=1, grid=(rows // blk,),
            in_specs=[entry(k) for k in range(n)] + [plain, plain, plain],
            out_specs=[plain] * 4),
        name=name,
        compiler_params=pltpu.CompilerParams(dimension_semantics=("parallel",), vmem_limit_bytes=VMEM_LIMIT),
    )(me, own, *([land] * (n - 1)), w, m, v)


def adamw(w, g, m, v, name):
    R, C = w.shape
    tr = R
    for cand in (256, 128, 64, 32, 16, 8):
        if R % cand == 0 and R > cand:
            tr = cand
            break

    def body(w_ref, g_ref, m_ref, v_ref, d_ref, nm_ref, nv_ref):
        d_ref[...], nm_ref[...], nv_ref[...] = _adamw_update(w_ref[...], g_ref[...], m_ref[...], v_ref[...])

    spec = ((tr, C), lambda i: (i, 0))
    out = ((R, C), F32) + spec
    return _call(name, body, (R // tr,), [(w,) + spec, (g,) + spec, (m,) + spec, (v,) + spec],
                 [out, out, out], sem=("parallel",))


def _rms_tile(xv, gv):
    r = lax.rsqrt(jnp.mean(xv * xv, axis=-1, keepdims=True) + EPS)
    return (xv * r * gv).astype(BF16)


def rms_fwd(x, g, name):
    S, D = x.shape
    tr = 512

    def body(x_ref, g_ref, o_ref):
        o_ref[...] = _rms_tile(x_ref[...], g_ref[...])

    return _call(name, body, (S // tr,),
                 [(x, (tr, D), lambda i: (i, 0)), (g, (1, D), lambda i: (0, 0))],
                 [((S, D), BF16, (tr, D), lambda i: (i, 0))], sem=("parallel",))[0]


def _rms_bwd_tile(dn, xv, gv):
    r = lax.rsqrt(jnp.mean(xv * xv, axis=-1, keepdims=True) + EPS)
    xh = xv * r
    dxh = dn * gv
    dx = r * (dxh - xh * jnp.mean(dxh * xh, axis=-1, keepdims=True))
    return dx, dn * xh


def final_loss(x, tgt, g, name):
    S, D = x.shape
    tr = 256

    def body(x_ref, t_ref, g_ref, l_ref, dx_ref, dxb_ref, dg_ref):
        i = pl.program_id(0)
        xv, gv = x_ref[...], g_ref[...]
        r = lax.rsqrt(jnp.mean(xv * xv, axis=-1, keepdims=True) + EPS)
        xh = xv * r
        e = xh * gv - t_ref[...]
        part = 0.5 * jnp.sum(jnp.sum(e * e, axis=-1, keepdims=True) * (1.0 / D), axis=0, keepdims=True)
        dy = e * (1.0 / D)
        dxh = dy * gv
        dx = r * (dxh - xh * jnp.mean(dxh * xh, axis=-1, keepdims=True))
        dx_ref[...] = dx
        dxb_ref[...] = dx.astype(BF16)
        dgp = jnp.sum(dy * xh, axis=0, keepdims=True)

        @pl.when(i == 0)
        def _():
            l_ref[...] = jnp.broadcast_to(part, l_ref.shape)
            dg_ref[...] = dgp

        @pl.when(i > 0)
        def _():
            l_ref[...] += jnp.broadcast_to(part, l_ref.shape)
            dg_ref[...] += dgp

    row = ((tr, D), lambda i: (i, 0))
    return _call(name, body, (S // tr,),
                 [(x,) + row, (tgt,) + row, (g, (1, D), lambda i: (0, 0))],
                 [((1, LANES), F32, (1, LANES), lambda i: (0, 0)), ((S, D), F32) + row, ((S, D), BF16) + row,
                  ((1, D), F32, (1, D), lambda i: (0, 0))], sem=("arbitrary",))


FFN_TF = 4 * FFN_SHARD


def _ffn_pick(G, which):
    if isinstance(G, tuple):
        return (G[0], which) if which < 2 else (G[1], 0)
    return G, which


def _ffn_w_spec(G, which, imap):
    arr, blk = _ffn_pick(G, which)
    return (arr, (4, FFN_SHARD, arr.shape[2]), lambda *idx: (imap(*idx), blk, 0))


def _ffn_whole_w_spec(G, which):
    arr, blk = _ffn_pick(G, which)
    return (arr, (N_DEV, FFN_SHARD, arr.shape[2]), lambda *idx: (0, blk, 0), pl.Buffered(1))


def _ffn_hidden(a, b):
    av, bv = a.astype(F32), b.astype(F32)
    return (av * _sigmoid(av) * bv).astype(BF16)


def ffn_up(n, G, name):
    S, D = n.shape
    F = N_DEV * FFN_SHARD
    tm = 256

    def body(n_ref, w1_ref, w3_ref, abh_ref):
        nv = n_ref[...]
        a = _dot(nv, w1_ref[...].reshape(F, D), 1, 1).astype(BF16)
        b = _dot(nv, w3_ref[...].reshape(F, D), 1, 1).astype(BF16)
        abh_ref[0] = a
        abh_ref[1] = b
        abh_ref[2] = _ffn_hidden(a, b)

    return _call(name, body, (S // tm,),
                 [(n, (tm, D), lambda i: (i, 0)),
                  _ffn_whole_w_spec(G, 0), _ffn_whole_w_spec(G, 1)],
                 [((3, S, F), BF16, (3, tm, F), lambda i: (0, i, 0))],
                 sem=("parallel",))[0]


def ffn_down(abh, G, x, g_next, name):
    _, S, F = abh.shape
    D = x.shape[1]
    tm = 512

    def body(h_ref, w2_ref, x_ref, g_ref, o_ref, n_ref):
        xo = x_ref[...] + 0.5 * _dot(h_ref[...], w2_ref[...].reshape(F, D))
        o_ref[...] = xo
        n_ref[...] = _rms_tile(xo, g_ref[...])

    tile = ((tm, D), lambda i: (i, 0))
    return _call(name, body, (S // tm,),
                 [(abh, (None, tm, F), lambda i: (2, i, 0)), _ffn_whole_w_spec(G, 2),
                  (x,) + tile, (g_next, (1, D), lambda i: (0, 0))],
                 [((S, D), F32) + tile, ((S, D), BF16) + tile], sem=("parallel",))


def ffn_fwd(x, g, G, name):
    S, D = x.shape
    F = N_DEV * FFN_SHARD
    tm = 256

    def body(x_ref, g_ref, w1_ref, w3_ref, w2_ref, n_ref, abh_ref, o_ref):
        xv = x_ref[...]
        nv = _rms_tile(xv, g_ref[...])
        n_ref[...] = nv
        a = _dot(nv, w1_ref[...].reshape(F, D), 1, 1).astype(BF16)
        b = _dot(nv, w3_ref[...].reshape(F, D), 1, 1).astype(BF16)
        h = _ffn_hidden(a, b)
        abh_ref[0] = a
        abh_ref[1] = b
        abh_ref[2] = h
        o_ref[...] = xv + 0.5 * _dot(h, w2_ref[...].reshape(F, D))

    tile = ((tm, D), lambda i: (i, 0))
    return _call(name, body, (S // tm,),
                 [(x,) + tile, (g, (1, D), lambda i: (0, 0)),
                  _ffn_whole_w_spec(G, 0), _ffn_whole_w_spec(G, 1), _ffn_whole_w_spec(G, 2)],
                 [((S, D), BF16) + tile, ((3, S, F), BF16, (3, tm, F), lambda i: (0, i, 0)), ((S, D), F32) + tile],
                 sem=("parallel",))


def _ffn_hidden_grads(dh, av, bv):
    sig = _sigmoid(av)
    return dh * bv * (sig * (1.0 + av * (1.0 - sig))), dh * (av * sig)


def ffn_bwd_hidden(dxo, abh, G, name):
    _, S, F = abh.shape
    D = dxo.shape[1]
    tm = 256

    def body(d_ref, w2_ref, ab_ref, o_ref):
        dh = 0.5 * _dot(d_ref[...].astype(BF16), w2_ref[...].reshape(F, D), 1, 1)
        da, db = _ffn_hidden_grads(dh, ab_ref[0].astype(F32), ab_ref[1].astype(F32))
        o_ref[0] = da.astype(BF16)
        o_ref[1] = db.astype(BF16)

    return _call(name + "_down_bwd", body, (S // tm,),
                 [(dxo, (tm, D), lambda i: (i, 0)), _ffn_whole_w_spec(G, 2),
                  (abh, (2, tm, F), lambda i: (0, i, 0))],
                 [((2, S, F), BF16, (2, tm, F), lambda i: (0, i, 0))],
                 sem=("parallel",))[0]


def ffn_bwd_hidden_input(dxo_b, dxo, abh, G, x_in, g, name):
    _, S, F = abh.shape
    D = x_in.shape[1]
    tm = 256

    def body(db_ref, w2_ref, w1_ref, w3_ref, ab_ref, x_ref, d_ref, g_ref, dab_ref, dx_ref, dxb_ref, dg_ref):
        i = pl.program_id(0)
        dh = 0.5 * _dot(db_ref[...], w2_ref[...].reshape(F, D), 1, 1)
        da, db = _ffn_hidden_grads(dh, ab_ref[0].astype(F32), ab_ref[1].astype(F32))
        da, db = da.astype(BF16), db.astype(BF16)
        dab_ref[0] = da
        dab_ref[1] = db
        dn = _dot(da, w1_ref[...].reshape(F, D)) + _dot(db, w3_ref[...].reshape(F, D))
        dx, dgt = _rms_bwd_tile(dn, x_ref[...], g_ref[...])
        dx = d_ref[...] + dx
        dx_ref[...] = dx
        dxb_ref[...] = dx.astype(BF16)
        dgp = jnp.sum(dgt, axis=0, keepdims=True)

        @pl.when(i == 0)
        def _():
            dg_ref[...] = dgp

        @pl.when(i > 0)
        def _():
            dg_ref[...] += dgp

    tile = ((tm, D), lambda i: (i, 0))
    wide = ((2, tm, F), lambda i: (0, i, 0))
    return _call(name + "_hidden_input", body, (S // tm,),
                 [(dxo_b,) + tile, _ffn_whole_w_spec(G, 2), _ffn_whole_w_spec(G, 0), _ffn_whole_w_spec(G, 1),
                  (abh,) + wide, (x_in,) + tile, (dxo,) + tile, (g, (1, D), lambda i: (0, 0))],
                 [((2, S, F), BF16) + wide, ((S, D), F32) + tile, ((S, D), BF16) + tile,
                  ((1, D), F32, (1, D), lambda i: (0, 0))],
                 sem=("arbitrary",))


def ffn_bwd_weights(dxo, abh, dab, n, name):
    _, S, F = abh.shape
    D = dxo.shape[1]
    nf = F // FFN_TF
    tk = WGRAD_TK
    nk = S // tk
    gshape = (N_DEV, 3 * FFN_SHARD, D)

    def dw2_body(h_ref, d_ref, o_ref, acc_ref):
        k = pl.program_id(1)
        p = _dot(h_ref[...], d_ref[...].astype(BF16), 0, 0)

        @pl.when(k == 0)
        def _():
            acc_ref[...] = p

        @pl.when(k > 0)
        def _():
            acc_ref[...] += p

        @pl.when(k == nk - 1)
        def _():
            o_ref[...] = (0.5 * acc_ref[...]).astype(BF16).reshape(4, FFN_SHARD, D)

    gw = _call(name + "_dw2", dw2_body, (nf, nk),
               [(abh, (None, tk, FFN_TF), lambda j, k: (2, k, j)), (dxo, (tk, D), lambda j, k: (k, 0))],
               [(gshape, BF16, (4, FFN_SHARD, D), lambda j, k: (j, 2, 0))],
               scratch=[pltpu.VMEM((FFN_TF, D), F32)], sem=("parallel", "arbitrary"))[0]

    def dw13_body(gw_ref, dab_ref, n_ref, o_ref):
        o_ref[...] = _dot(dab_ref[...], n_ref[...], 0, 0).astype(BF16).reshape(4, FFN_SHARD, D)

    gw = pl.pallas_call(
        dw13_body,
        out_shape=jax.ShapeDtypeStruct(gshape, BF16),
        grid=(2, nf),
        in_specs=[pl.BlockSpec(memory_space=pl.ANY),
                  pl.BlockSpec((None, S, FFN_TF), lambda w, j: (w, 0, j)),
                  pl.BlockSpec((S, D), lambda w, j: (0, 0))],
        out_specs=pl.BlockSpec((4, FFN_SHARD, D), lambda w, j: (j, w, 0)),
        input_output_aliases={0: 0},
        name=name + "_dw13",
        compiler_params=pltpu.CompilerParams(dimension_semantics=("parallel", "parallel"),
                                             vmem_limit_bytes=VMEM_LIMIT),
    )(gw, dab, n)
    return gw


def ffn_bwd_input(dab, G, x_in, g, dxo, name):
    _, S, F = dab.shape
    D = x_in.shape[1]
    tm = 256

    def dn_body(dab_ref, w1_ref, w3_ref, x_ref, d_ref, g_ref, dx_ref, dg_ref):
        i = pl.program_id(0)
        dn = _dot(dab_ref[0], w1_ref[...].reshape(F, D)) + _dot(dab_ref[1], w3_ref[...].reshape(F, D))
        dx, dgt = _rms_bwd_tile(dn, x_ref[...], g_ref[...])
        dx_ref[...] = d_ref[...] + dx
        dgp = jnp.sum(dgt, axis=0, keepdims=True)

        @pl.when(i == 0)
        def _():
            dg_ref[...] = dgp

        @pl.when(i > 0)
        def _():
            dg_ref[...] += dgp

    tile = ((tm, D), lambda i: (i, 0))
    return _call(name + "_dn", dn_body, (S // tm,),
                 [(dab, (2, tm, F), lambda i: (0, i, 0)),
                  _ffn_whole_w_spec(G, 0), _ffn_whole_w_spec(G, 1),
                  (x_in,) + tile, (dxo,) + tile, (g, (1, D), lambda i: (0, 0))],
                 [((S, D), F32) + tile, ((1, D), F32, (1, D), lambda i: (0, 0))],
                 sem=("arbitrary",))


PROJ_TN = 512
DH_SHARDS_PER_STEP = 4


def in_proj(h, Gm, name):
    S, D = h.shape
    n_tiles = N_DEV * Gm.shape[2] // PROJ_TN

    def body(h_ref, w_ref, o_ref):
        o_ref[...] = _dot(h_ref[...], w_ref[...]).astype(BF16)

    return _call(name, body, (n_tiles,),
                 [(h, (S, D), lambda j: (0, 0)),
                  (Gm, (None, D, PROJ_TN), lambda j: (j // 2, 0, j % 2))],
                 [((S, n_tiles * PROJ_TN), BF16, (S, PROJ_TN), lambda j: (0, j))],
                 sem=("parallel",))[0]


def _dproj_pieces(dqkv, dq_b, dkv_b, dgate):
    pieces = [(dqkv[g], [(3 * which + g, (which, 0)) for which in range(3)]) for g in range(3)]
    pieces.append((dq_b, [(9, (None, 0)), (10, (None, 1))]))
    pieces.append((dkv_b, [(11, (None, 0))]))
    pieces.append((dgate, [(12 + 2 * a + b, (a, b)) for a in range(2) for b in range(2)]))
    return pieces


def in_proj_bwd_dw(pieces, h, gm_grads, name):
    S, D = h.shape

    for n_piece, (arr, tiles) in enumerate(pieces):
        w_tile = [t for t, _ in tiles]
        lead = [ix[0] for _, ix in tiles]
        colb = [ix[1] for _, ix in tiles]

        def pick(table, j):
            out = table[-1]
            for k in range(len(table) - 2, -1, -1):
                out = jnp.where(j == k, table[k], out)
            return out

        def dw_body(gm_ref, h_ref, d_ref, o_ref):
            o_ref[...] = _dot(h_ref[...], d_ref[...], 0, 0).astype(BF16)

        if arr.ndim == 3:
            d_spec = pl.BlockSpec((None, S, PROJ_TN), lambda j, lead=lead, colb=colb: (pick(lead, j), 0, pick(colb, j)))
        else:
            d_spec = pl.BlockSpec((S, PROJ_TN), lambda j, colb=colb: (0, pick(colb, j)))
        gm_grads = pl.pallas_call(
            dw_body,
            out_shape=jax.ShapeDtypeStruct(gm_grads.shape, BF16),
            grid=(len(tiles),),
            in_specs=[pl.BlockSpec(memory_space=pl.ANY), pl.BlockSpec((S, D), lambda j: (0, 0)), d_spec],
            out_specs=pl.BlockSpec((None, D, PROJ_TN),
                                   lambda j, w_tile=w_tile: (pick(w_tile, j) // 2, 0, pick(w_tile, j) % 2)),
            input_output_aliases={0: 0},
            name="%s_dw%d" % (name, n_piece),
            compiler_params=pltpu.CompilerParams(dimension_semantics=("parallel",), vmem_limit_bytes=VMEM_LIMIT),
        )(gm_grads, h, arr)
    return gm_grads


def in_proj_bwd_dh(pieces, Gm, x_in, g, dres, name):
    S, D = x_in.shape
    tm = 256
    C = Gm.shape[2]
    n_sh = N_DEV
    n_p = len(pieces)

    def dh_body(*refs):
        d_refs = refs[:n_p]
        w_ref, x_ref, r_ref, g_ref, dx_ref, dxb_ref, dg_ref = refs[n_p:]
        i = pl.program_id(0)
        p = None
        for d_ref, (arr, tiles) in zip(d_refs, pieces):
            for t, (lead, colb) in tiles:
                cols = slice(colb * PROJ_TN, (colb + 1) * PROJ_TN)
                d = d_ref[:, cols] if lead is None else d_ref[lead, :, cols]
                wcol = (t % 2) * PROJ_TN
                term = _dot(d, w_ref[t // 2, :, wcol:wcol + PROJ_TN], 1, 1)
                p = term if p is None else p + term
        dx, dgt = _rms_bwd_tile(p, x_ref[...], g_ref[...])
        dx = r_ref[...] + dx
        dx_ref[...] = dx
        dxb_ref[...] = dx.astype(BF16)
        dgp = jnp.sum(dgt, axis=0, keepdims=True)

        @pl.when(i == 0)
        def _():
            dg_ref[...] = dgp

        @pl.when(i > 0)
        def _():
            dg_ref[...] += dgp

    tile = ((tm, D), lambda i: (i, 0))

    def rows_of(arr):
        if arr.ndim == 3:
            return (arr, (arr.shape[0], tm, arr.shape[2]), lambda i: (0, i, 0))
        return (arr, (tm, arr.shape[1]), lambda i: (i, 0))

    return _call(name + "_dh", dh_body, (S // tm,),
                 [rows_of(arr) for arr, _ in pieces]
                 + [(Gm, (n_sh, D, C), lambda i: (0, 0, 0), pl.Buffered(1)),
                    (x_in,) + tile, (dres,) + tile, (g, (1, D), lambda i: (0, 0))],
                 [((S, D), F32) + tile, ((S, D), BF16) + tile, ((1, D), F32, (1, D), lambda i: (0, 0))],
                 sem=("arbitrary",))


def _t5_bucket(rel):
    n = N_BUCKETS // 2
    max_exact = n // 2
    ret = jnp.where(rel > 0, n, 0)
    a = jnp.abs(rel)
    af = jnp.maximum(a, 1).astype(F32)
    large = max_exact + (jnp.log(af / max_exact) / math.log(MAX_DISTANCE / max_exact)
                         * (n - max_exact)).astype(jnp.int32)
    large = jnp.minimum(large, n - 1)
    return ret + jnp.where(a < max_exact, a, large)


def _bucket_tables():
    qi = jnp.arange(A_TQ, dtype=jnp.int32)[:, None]
    kj = jnp.arange(A_WIN, dtype=jnp.int32)[None, :]
    rel = kj - HALF_WINDOW - qi
    return jnp.stack([_t5_bucket(rel * d) for d in DILATIONS], axis=0)


def bias_build(rel_bias, buckets):
    def body(tab_ref, bk_ref, o_ref):
        col = pl.program_id(0) * HEADS_PER_GROUP_A + pl.program_id(1)
        bk = bk_ref[...]
        acc = jnp.zeros(bk.shape, F32)
        for b in range(N_BUCKETS):
            acc = jnp.where(bk == b, tab_ref[b, col], acc)
        qi = lax.broadcasted_iota(jnp.int32, bk.shape, 0)
        kj = lax.broadcasted_iota(jnp.int32, bk.shape, 1)
        band = jnp.where(jnp.abs(kj - HALF_WINDOW - qi) <= HALF_WINDOW, acc, NEG_INF)
        o_ref[0] = jnp.where(kj >= HALF_WINDOW, band, NEG_INF)
        o_ref[1] = band
        o_ref[2] = jnp.where(kj < A_TQ + HALF_WINDOW, band, NEG_INF)

    out = pl.pallas_call(
        body,
        out_shape=jax.ShapeDtypeStruct((3, HEADS_PER_GROUP_A // 2, 3, 2, A_TQ, A_WIN), F32),
        grid=(3, HEADS_PER_GROUP_A),
        in_specs=[pl.BlockSpec(memory_space=pltpu.SMEM),
                  pl.BlockSpec((None, A_TQ, A_WIN), lambda g, h: (g, 0, 0))],
        out_specs=pl.BlockSpec((None, None, 3, None, A_TQ, A_WIN), lambda g, h: (g, h // 2, 0, h % 2, 0, 0)),
        name="a_bias_build",
        compiler_params=pltpu.CompilerParams(dimension_semantics=("parallel", "parallel")),
    )(rel_bias, buckets)
    return out.reshape(3, HEADS_PER_GROUP_A // 2, 3, 2 * A_TQ, A_WIN)


def bias_bwd(dbias, buckets):
    def body(d_ref, bk_ref, o_ref):
        bk = bk_ref[...]
        dv = d_ref[...]
        for b in range(N_BUCKETS):
            part = jnp.sum(jnp.where(bk == b, dv, 0.0), axis=1, keepdims=True)
            o_ref[b:b + 1, :] = jnp.broadcast_to(jnp.sum(part, axis=0, keepdims=True), (1, LANES))

    out = pl.pallas_call(
        body,
        out_shape=jax.ShapeDtypeStruct((3, HEADS_PER_GROUP_A, N_BUCKETS, LANES), F32),
        grid=(3, HEADS_PER_GROUP_A),
        in_specs=[pl.BlockSpec((None, None, A_TQ, A_WIN), lambda g, h: (g, h, 0, 0)),
                  pl.BlockSpec((None, A_TQ, A_WIN), lambda g, h: (g, 0, 0))],
        out_specs=pl.BlockSpec((None, None, N_BUCKETS, LANES), lambda g, h: (g, h, 0, 0)),
        name="a_bias_bwd",
        compiler_params=pltpu.CompilerParams(dimension_semantics=("parallel", "parallel")),
    )(dbias, buckets)
    return out[:, :, :, 0].transpose(2, 0, 1).reshape(N_BUCKETS, 3 * HEADS_PER_GROUP_A)


def _a_fill_padded(pad_ref, src_ref, n, pad):
    zeros = jnp.zeros((pad, LANES), pad_ref.dtype)
    pad_ref[0:pad, :] = zeros
    pad_ref[pad + n:2 * pad + n, :] = zeros
    pad_ref[pad:pad + n, :] = src_ref[...].astype(pad_ref.dtype)


def _a_stack_heads(x, lane):
    zero = jnp.zeros_like(x)
    return jnp.concatenate([jnp.where(lane < HEAD_DIM_A, x, zero), jnp.where(lane >= HEAD_DIM_A, x, zero)], axis=0)


def _a_bias_variant(qb, nqb):
    return jnp.where(qb == 0, 0, jnp.where(qb == nqb - 1, 2, 1))


def _a_slab_specs(proj, g):
    S = proj.shape[0]
    per = GROUP_WIDTH_A // LANES
    return [(proj, (S, LANES), lambda hp, w=w: (0, per * (3 * w + g) + hp)) for w in range(3)]


def a_fwd(proj, bias, g, name):
    S = proj.shape[0]
    d = DILATIONS[g]
    L = S // d
    nqb = L // A_TQ
    pad = HALF_WINDOW * d

    def body(q_ref, k_ref, v_ref, b_ref, o_ref, l_ref, qf, kpad, vpad):
        qf[...] = q_ref[...].astype(F32) * A_SCALE
        _a_fill_padded(kpad, k_ref, S, pad)
        _a_fill_padded(vpad, v_ref, S, pad)
        lane = lax.broadcasted_iota(jnp.int32, (A_TQ, LANES), 1)

        def block(t, carry):
            qb, r = t // d, t % d
            start = qb * (A_TQ * d) + r
            kw = kpad[pl.ds(start, A_WIN, stride=d), :].astype(BF16)
            vw = vpad[pl.ds(start, A_WIN, stride=d), :].astype(BF16)
            q = qf[pl.ds(start, A_TQ, stride=d), :].astype(BF16)
            q2 = _a_stack_heads(q, lane)
            s = _dot(q2, kw, 1, 1) + b_ref[_a_bias_variant(qb, nqb)]
            m = jnp.max(s, axis=-1, keepdims=True)
            e = jnp.exp(s - m)
            l = jnp.sum(e, axis=-1, keepdims=True)
            o2 = _dot(e.astype(BF16), vw) / l
            lse2 = m + jnp.log(l)
            o_ref[pl.ds(start, A_TQ, stride=d), :] = jnp.where(lane < HEAD_DIM_A, o2[0:A_TQ], o2[A_TQ:])
            l_ref[pl.ds(start, A_TQ, stride=d), :] = jnp.where(lane < HEAD_DIM_A, lse2[0:A_TQ], lse2[A_TQ:])
            return carry

        lax.fori_loop(0, nqb * d, block, 0, unroll=A_UNROLL)

    out_spec = ((S, GROUP_WIDTH_A), F32, (S, LANES), lambda hp: (0, hp))
    return _call(name, body, (4,),
                 _a_slab_specs(proj, g)
                 + [(bias, (None, None, 3, 2 * A_TQ, A_WIN), lambda hp: (g, hp, 0, 0, 0))],
                 [out_spec, out_spec],
                 scratch=[pltpu.VMEM((S, LANES), F32)] + [pltpu.VMEM((S + 2 * pad, LANES), F32)] * 2,
                 sem=("parallel",))


def a_combine(outs, lses, name):
    S, W = outs[0].shape
    tr = 512

    def body(o0, o1, o2, l0, l1, l2, oa_ref, lt_ref):
        a, b, c = l0[...], l1[...], l2[...]
        m = jnp.maximum(jnp.maximum(a, b), c)
        ea, eb, ec = jnp.exp(a - m), jnp.exp(b - m), jnp.exp(c - m)
        z = ea + eb + ec
        oa_ref[...] = ((ea * o0[...] + eb * o1[...] + ec * o2[...]) / z).astype(BF16)
        lt_ref[...] = m + jnp.log(z)

    spec = ((tr, W), lambda i: (i, 0))
    return _call(name, body, (S // tr,), [(a,) + spec for a in (*outs, *lses)],
                 [((S, W), BF16) + spec, ((S, W), F32) + spec], sem=("parallel",))


def a_bwd(proj, bias, do_a, o_a, lse_tot, g, name):
    S = proj.shape[0]
    d = DILATIONS[g]
    L = S // d
    nqb = L // A_TQ
    pad = HALF_WINDOW * d

    def body(q_ref, k_ref, v_ref, b_ref, do_ref, o_ref, l_ref, dqkv_ref, db_ref,
             qf, of, dqf, kpad, vpad, dkacc, dvacc):
        qf[...] = q_ref[...].astype(F32) * A_SCALE
        of[...] = o_ref[...].astype(F32)
        _a_fill_padded(kpad, k_ref, S, pad)
        _a_fill_padded(vpad, v_ref, S, pad)
        dkacc[...] = jnp.zeros(dkacc.shape, F32)
        dvacc[...] = jnp.zeros(dvacc.shape, F32)
        db_ref[...] = jnp.zeros(db_ref.shape, F32)
        lane = lax.broadcasted_iota(jnp.int32, (A_TQ, LANES), 1)

        def block(t, carry):
            qb, r = t // d, t % d
            start = qb * (A_TQ * d) + r
            rows = pl.ds(start, A_TQ, stride=d)
            win = pl.ds(start, A_WIN, stride=d)
            kw = kpad[win, :].astype(BF16)
            vw = vpad[win, :].astype(BF16)
            q = qf[rows, :].astype(BF16)
            do = do_ref[rows, :]
            ov = of[rows, :]
            lt = l_ref[rows, :]
            q2 = _a_stack_heads(q, lane)
            do2 = _a_stack_heads(do, lane)
            lt2 = jnp.concatenate([lt[:, 0:1], lt[:, HEAD_DIM_A:HEAD_DIM_A + 1]], axis=0)
            s = _dot(q2, kw, 1, 1) + b_ref[_a_bias_variant(qb, nqb)]
            p = jnp.exp(s - lt2)
            t = jnp.sum(do2 * jnp.concatenate([ov, ov], axis=0), axis=-1, keepdims=True)
            dob2 = do2.astype(BF16)
            ds = p * (_dot(dob2, vw, 1, 1) - t)
            db_ref[...] += ds
            dsb = ds.astype(BF16)
            dq2 = _dot(dsb, kw)
            dqf[rows, :] = jnp.where(lane < HEAD_DIM_A, dq2[0:A_TQ], dq2[A_TQ:]) * A_SCALE
            dkacc[win, :] += _dot(dsb, q2, 0, 0)
            dvacc[win, :] += _dot(p.astype(BF16), dob2, 0, 0)
            return carry

        lax.fori_loop(0, nqb * d, block, 0, unroll=A_UNROLL)
        dqkv_ref[0] = dqf[...].astype(BF16)
        dqkv_ref[1] = dkacc[pad:pad + S, :].astype(BF16)
        dqkv_ref[2] = dvacc[pad:pad + S, :].astype(BF16)

    slab = ((S, LANES), lambda hp: (0, hp))
    padded = pltpu.VMEM((S + 2 * pad, LANES), F32)
    return _call(
        name, body, (4,),
        _a_slab_specs(proj, g)
        + [(bias, (None, None, 3, 2 * A_TQ, A_WIN), lambda hp: (g, hp, 0, 0, 0)),
           (do_a,) + slab, (o_a,) + slab, (lse_tot,) + slab],
        [((3, S, GROUP_WIDTH_A), BF16, (3, S, LANES), lambda hp: (0, 0, hp)),
         ((4, 2 * A_TQ, A_WIN), F32, (None, 2 * A_TQ, A_WIN), lambda hp: (hp, 0, 0))],
        scratch=[pltpu.VMEM((S, LANES), F32)] * 3 + [padded] * 4,
        sem=("parallel",))


def _rope_tables(S):
    rows = S // GRID_W
    row = jnp.repeat(jnp.arange(rows, dtype=F32), GRID_W)
    col = jnp.tile(jnp.arange(GRID_W, dtype=F32), rows)
    n_freq = HEAD_DIM_B // 4
    freq = ROPE_THETA ** (-jnp.arange(n_freq, dtype=F32) / n_freq)
    ang = jnp.concatenate([row[:, None] * freq, col[:, None] * freq], axis=-1)
    cos, sin = jnp.cos(ang), jnp.sin(ang)
    return jnp.repeat(cos, 2, axis=-1), jnp.stack([-sin, sin], axis=-1).reshape(S, HEAD_DIM_B)


def _swap_pairs(y):
    lane = lax.broadcasted_iota(jnp.int32, y.shape, 1)
    return jnp.where(lane % 2 == 0, pltpu.roll(y, LANES - 1, 1), pltpu.roll(y, 1, 1))


def qkv_prep(proj, gains, cos_t, sin_t, name):
    S = proj.shape[0]
    ts = 256
    n_rot = N_HEADS_B + N_KV_B
    nh = n_rot + N_KV_B
    W = nh * LANES

    def body(x_ref, g_ref, c_ref, s_ref, o_ref):
        cv, sv = c_ref[...], s_ref[...]
        for hb in range(nh):
            cols = slice(hb * LANES, (hb + 1) * LANES)
            if hb < n_rot:
                xv = x_ref[:, cols].astype(F32)
                r = lax.rsqrt(jnp.mean(xv * xv, axis=-1, keepdims=True) + EPS)
                yv = xv * r * g_ref[:, cols]
                o_ref[:, cols] = (yv * cv + _swap_pairs(yv) * sv).astype(BF16)
            else:
                o_ref[:, cols] = x_ref[:, cols]

    return _call(name, body, (S // ts,),
                 [(proj, (ts, W), lambda i: (i, A_QKV_WIDTH // W)), (gains, (1, W), lambda i: (0, 0)),
                  (cos_t, (ts, LANES), lambda i: (i, 0)), (sin_t, (ts, LANES), lambda i: (i, 0))],
                 [((S, W), BF16, (ts, W), lambda i: (i, 0))],
                 sem=("parallel",))[0]


def qk_prep_bwd(dr, proj, col0, gain, cos_t, sin_t, name):
    S, W = dr.shape
    H = W // LANES
    ts = 256
    wx = math.gcd(W, col0)
    n_x = W // wx

    def body(d_ref, *refs):
        x_refs = refs[:n_x]
        g_ref, c_ref, s_ref, dx_ref, dg_ref = refs[n_x:]
        i = pl.program_id(0)
        cv, sv, gv = c_ref[...], s_ref[...], g_ref[...]
        dgp = jnp.zeros((1, LANES), F32)
        for hb in range(H):
            cols = slice(hb * LANES, (hb + 1) * LANES)
            xc = (hb * LANES) % wx
            xv = x_refs[(hb * LANES) // wx][:, xc:xc + LANES].astype(F32)
            dout = d_ref[:, cols]
            dy = dout * cv + _swap_pairs(dout * sv)
            dx, dgt = _rms_bwd_tile(dy, xv, gv)
            dx_ref[:, cols] = dx.astype(BF16)
            dgp = dgp + jnp.sum(dgt, axis=0, keepdims=True)

        @pl.when(i == 0)
        def _():
            dg_ref[...] = dgp

        @pl.when(i > 0)
        def _():
            dg_ref[...] += dgp

    return _call(name, body, (S // ts,),
                 [(dr, (ts, W), lambda i: (i, 0))]
                 + [(proj, (ts, wx), lambda i, k=k: (i, col0 // wx + k)) for k in range(n_x)]
                 + [(gain, (1, LANES), lambda i: (0, 0)),
                  (cos_t, (ts, LANES), lambda i: (i, 0)), (sin_t, (ts, LANES), lambda i: (i, 0))],
                 [((S, W), BF16, (ts, W), lambda i: (i, 0)),
                  ((1, LANES), F32, (1, LANES), lambda i: (0, 0))],
                 sem=("arbitrary",))


def _row_sums(x):
    hi = x.astype(BF16)
    lo = (x - hi.astype(F32)).astype(BF16)
    ones = jnp.ones((8, LANES), BF16)
    return (_dot(ones, hi, 1, 1) + _dot(ones, lo, 1, 1))[0:1, :]


def flash_fwd(qkv, name):
    S = qkv.shape[0]
    tq = B_TQ_FWD
    scale = HEAD_DIM_B ** -0.5

    hps = B_HEADS_PER_STEP

    def body(q_ref, k_ref, v_ref, o_ref, l_ref):
        k, v = k_ref[...], v_ref[...]
        for j in range(hps):
            cols = slice(j * LANES, (j + 1) * LANES)
            s = _dot(q_ref[:, cols], k, 1, 1)
            m = jnp.max(s, axis=-1, keepdims=True)
            e = jnp.exp2((s - m) * (scale * LOG2E))
            l = jnp.sum(e, axis=-1, keepdims=True)
            o_ref[:, cols] = (_dot(e.astype(BF16), v) / l).astype(BF16)
            lse = jnp.broadcast_to(m * scale + jnp.log(l), (tq, LANES))
            l_ref[j] = _row_sums(lse) * (1.0 / LANES)

    per = GQA_GROUP_B // hps
    heads = lambda g, h, i: (i, g * per + h)
    return _call(name, body, (N_KV_B, per, S // tq),
                 [(qkv, (tq, hps * LANES), heads),
                  (qkv, (S, LANES), lambda g, h, i: (0, N_HEADS_B + g)),
                  (qkv, (S, LANES), lambda g, h, i: (0, N_HEADS_B + N_KV_B + g))],
                 [((S, N_HEADS_B * LANES), BF16, (tq, hps * LANES), heads),
                  ((N_HEADS_B, 1, S), F32, (hps, 1, tq), lambda g, h, i: (g * per + h, 0, i))],
                 sem=("parallel", "parallel", "parallel"))


def flash_bwd(qkv, k_t, do_b, o_b, lse, name):
    S = qkv.shape[0]
    tq = B_TQ_BWD
    nq = S // tq
    scale = HEAD_DIM_B ** -0.5

    def body(q_ref, k_ref, v_ref, kt_ref, do_ref, o_ref, l_ref, dq_ref, dk_ref, dv_ref, dkacc, dvacc):
        h, i = pl.program_id(1), pl.program_id(2)

        @pl.when((h == 0) & (i == 0))
        def _():
            dkacc[...] = jnp.zeros(dkacc.shape, F32)
            dvacc[...] = jnp.zeros(dvacc.shape, F32)

        q = q_ref[...]
        dob = do_ref[...]
        t = _row_sums(dob.astype(F32) * o_ref[...].astype(F32))
        pt = jnp.exp2(_dot(k_ref[...], q, 1, 1) * (scale * LOG2E) - l_ref[...] * LOG2E)
        dsb = (pt * (_dot(v_ref[...], dob, 1, 1) - t)).astype(BF16)
        dvacc[...] += _dot(pt.astype(BF16), dob)
        dkacc[...] += _dot(dsb, q)
        dq_ref[...] = _dot(kt_ref[...], dsb).T * scale

        @pl.when((h == GQA_GROUP_B - 1) & (i == nq - 1))
        def _():
            dk_ref[...] = dkacc[...] * scale
            dv_ref[...] = dvacc[...].astype(BF16)

    head = lambda g, h, i: (i, g * GQA_GROUP_B + h)
    return _call(name, body, (N_KV_B, GQA_GROUP_B, nq),
                 [(qkv, (tq, LANES), head),
                  (qkv, (S, LANES), lambda g, h, i: (0, N_HEADS_B + g)),
                  (qkv, (S, LANES), lambda g, h, i: (0, N_HEADS_B + N_KV_B + g)),
                  (k_t, (LANES, S), lambda g, h, i: (g, 0)),
                  (do_b, (tq, LANES), head), (o_b, (tq, LANES), head),
                  (lse, (None, 1, tq), lambda g, h, i: (g * GQA_GROUP_B + h, 0, i))],
                 [((S, N_HEADS_B * LANES), F32, (tq, LANES), head),
                  ((S, N_KV_B * LANES), F32, (S, LANES), lambda g, h, i: (0, g)),
                  ((S, N_KV_B * LANES), BF16, (S, LANES), lambda g, h, i: (0, g))],
                 scratch=[pltpu.VMEM((S, LANES), F32)] * 2,
                 sem=("parallel", "arbitrary", "arbitrary"))


MERGE_TN = 512


def _mix_rows_spec(Gm, row0, n_slots, slot_map, cols=None, col_map=None):
    C = Gm.shape[2] if cols is None else cols
    cm = (lambda *idx: 0) if col_map is None else col_map
    return (Gm, (n_slots, LANES, C), lambda *idx: (slot_map(*idx), row0 // LANES, cm(*idx)))


def _gate_specs(proj, tm):
    first = (A_QKV_WIDTH + PB_GATE_A) // MERGE_TN
    return [(proj, (tm, MERGE_TN), lambda i, k=k: (i, first + k)) for k in range(4)]


def _whole_rows_spec(Gm, row0):
    return _mix_rows_spec(Gm, row0, N_DEV, lambda *idx: 0)


def merge_fwd(o_a, o_b, w_a, Gm, proj, b_gate, x, name):
    S, D = x.shape
    tm = 256

    def body(oa_ref, ob_ref, wa_ref, wb_ref, wo_ref, g0, g1, g2, g3, bg_ref, x_ref, m_ref, ya_ref, yb_ref, xo_ref):
        ya = _dot(oa_ref[...], wa_ref[...])
        yb = _dot(ob_ref[...], wb_ref[...].reshape(N_DEV * LANES, D))
        ga = _sigmoid(jnp.concatenate([g0[...], g1[...]], axis=1).astype(F32) + bg_ref[:, 0:D])
        gb = _sigmoid(jnp.concatenate([g2[...], g3[...]], axis=1).astype(F32) + bg_ref[:, D:2 * D])
        merged = (ga * ya + gb * yb).astype(BF16)
        m_ref[...] = merged
        ya_ref[...] = ya.astype(BF16)
        yb_ref[...] = yb.astype(BF16)
        xo_ref[...] = x_ref[...] + _dot(merged, wo_ref[...].reshape(N_DEV * LANES, D))

    rows = lambda a: (a, (tm, a.shape[1]), lambda i: (i, 0))
    out = ((S, D), BF16, (tm, D), lambda i: (i, 0))
    return _call(name, body, (S // tm,),
                 [rows(o_a), rows(o_b), (w_a, w_a.shape, lambda i: (0, 0)),
                  _whole_rows_spec(Gm, REST_WB), _whole_rows_spec(Gm, REST_WOUT)]
                 + _gate_specs(proj, tm) + [(b_gate, (1, 2 * D), lambda i: (0, 0)), rows(x)],
                 [out, out, out, ((S, D), F32, (tm, D), lambda i: (i, 0))], sem=("parallel",))


def merge_bwd(dx2, w_a, Gm, ya, yb, proj, b_gate, name):
    S, D = dx2.shape
    tm = 256

    def body(d_ref, wo_ref, wa_ref, wb_ref, ya_ref, yb_ref, g0, g1, g2, g3, bg_ref,
             dya_ref, dyb_ref, dg_ref, dbg_ref, doa_ref, dob_ref):
        i = pl.program_id(0)
        dm = _dot(d_ref[...].astype(BF16), wo_ref[...].reshape(N_DEV * LANES, D), 1, 1)
        ga = _sigmoid(jnp.concatenate([g0[...], g1[...]], axis=1).astype(F32) + bg_ref[:, 0:D])
        gb = _sigmoid(jnp.concatenate([g2[...], g3[...]], axis=1).astype(F32) + bg_ref[:, D:2 * D])
        dya = (dm * ga).astype(BF16)
        dyb = (dm * gb).astype(BF16)
        dya_ref[...] = dya
        dyb_ref[...] = dyb
        dpa = dm * ya_ref[...].astype(F32) * ga * (1.0 - ga)
        dpb = dm * yb_ref[...].astype(F32) * gb * (1.0 - gb)
        dg_ref[0] = dpa.astype(BF16)
        dg_ref[1] = dpb.astype(BF16)
        doa_ref[...] = _dot(dya, wa_ref[...], 1, 1)
        dob_ref[...] = _dot(dyb, wb_ref[...].reshape(N_DEV * LANES, D), 1, 1).astype(BF16)
        sa =jnp.sum(dpa, axis=0, keepdims=True)
        sb = jnp.sum(dpb, axis=0, keepdims=True)

        @pl.when(i == 0)
        def _():
            dbg_ref[0] = sa
            dbg_ref[1] = sb

        @pl.when(i > 0)
        def _():
            dbg_ref[0] += sa
            dbg_ref[1] += sb

    tile = ((tm, D), lambda i: (i, 0))
    return _call(
        name, body, (S // tm,),
        [(dx2,) + tile, _whole_rows_spec(Gm, REST_WOUT), (w_a, w_a.shape, lambda i: (0, 0)),
         _whole_rows_spec(Gm, REST_WB), (ya,) + tile, (yb,) + tile]
        + _gate_specs(proj, tm) + [(b_gate, (1, 2 * D), lambda i: (0, 0))],
        [((S, D), BF16) + tile, ((S, D), BF16) + tile,
         ((2, S, D), BF16, (2, tm, D), lambda i: (0, i, 0)),
         ((2, 1, D), F32, (2, 1, D), lambda i: (0, 0, 0)),
         ((S, w_a.shape[0]), F32, (tm, w_a.shape[0]), lambda i: (i, 0)),
         ((S, N_HEADS_B * LANES), BF16, (tm, N_HEADS_B * LANES), lambda i: (i, 0))],
        sem=("arbitrary",))


def weight_grad_rows(a, b, grads, row0, name):
    S, M = a.shape
    N = b.shape[1]
    tmm = 512
    tk = WGRAD_TK
    nk = S // tk
    prior = [] if grads is None else [grads]

    def body(*refs):
        a_ref, b_ref, o_ref, acc_ref = refs[len(prior):]
        k = pl.program_id(1)
        p = _dot(a_ref[...], b_ref[...].astype(BF16), 0, 0)

        @pl.when(k == 0)
        def _():
            acc_ref[...] = p

        @pl.when(k > 0)
        def _():
            acc_ref[...] += p

        @pl.when(k == nk - 1)
        def _():
            o_ref[...] = acc_ref[...].astype(BF16).reshape(tmm // LANES, LANES, N)

    return pl.pallas_call(
        body,
        out_shape=jax.ShapeDtypeStruct((N_DEV, MIX_ROWS, N), BF16),
        grid=(M // tmm, nk),
        in_specs=[pl.BlockSpec(memory_space=pl.ANY)] * len(prior)
        + [pl.BlockSpec((tk, tmm), lambda j, k: (k, j)),
           pl.BlockSpec((tk, N), lambda j, k: (k, 0))],
        out_specs=pl.BlockSpec((tmm // LANES, LANES, N), lambda j, k: (j, row0 // LANES, 0)),
        scratch_shapes=[pltpu.VMEM((tmm, N), F32)],
        input_output_aliases={0: 0} if prior else {},
        name=name,
        compiler_params=pltpu.CompilerParams(dimension_semantics=("parallel", "arbitrary"),
                                             vmem_limit_bytes=VMEM_LIMIT),
    )(*prior, a, b)


def weight_grad_plain(a, b, name):
    S, M = a.shape
    N = b.shape[1]
    tk = WGRAD_TK
    nk = S // tk

    def body(a_ref, b_ref, o_ref, acc_ref):
        k = pl.program_id(0)
        p = _dot(a_ref[...], b_ref[...], 0, 0)

        @pl.when(k == 0)
        def _():
            acc_ref[...] = p

        @pl.when(k > 0)
        def _():
            acc_ref[...] += p

        @pl.when(k == nk - 1)
        def _():
            o_ref[...] = acc_ref[...].astype(BF16)

    return _call(name, body, (nk,),
                 [(a, (tk, M), lambda k: (k, 0)), (b, (tk, N), lambda k: (k, 0))],
                 [((M, N), BF16, (M, N), lambda k: (0, 0))],
                 scratch=[pltpu.VMEM((M, N), F32)], sem=("arbitrary",))[0]


def local_step(x, tgt, p, get_g1_up, get_g1_down, get_gm_in, get_gm_rest, get_g2, emit, start_token):
    S, D = x.shape
    after = lambda t: t[0:1, 0:1]
    buckets = _bucket_tables()
    cos_t, sin_t = _rope_tables(S)
    gains = jnp.concatenate([jnp.tile(p["q_norm"], (1, N_HEADS_B)), jnp.tile(p["k_norm"], (1, N_KV_B)),
                             jnp.ones((1, N_KV_B * LANES), F32)], axis=1)

    n1 = rms_fwd(x, p["ffn1_norm"] + after(start_token), "ffn1_norm")
    bias = bias_build(p["rel_bias"] + after(start_token), buckets)
    g1_up = get_g1_up((n1, bias))
    ab1 = ffn_up(n1, (g1_up, None), "ffn1_up")
    G1 = (g1_up, get_g1_down(ab1))
    x1, hm = ffn_down(ab1, G1, x, p["mix_norm"], "ffn1_down")
    Gw = get_gm_in(hm)
    proj = in_proj(hm, Gw, "in_proj")

    outs, lses = [], []
    for g in range(3):
        o, l = a_fwd(proj, bias, g, "a_fwd_%d" % g)
        outs.append(o)
        lses.append(l)
    o_a, lse_tot = a_combine(outs, lses, "a_combine")

    qkv = qkv_prep(proj, gains, cos_t, sin_t, "qkv_prep")
    k_t = qkv[:, N_HEADS_B * LANES:(N_HEADS_B + N_KV_B) * LANES].T
    o_b, lse_b = flash_fwd(qkv, "flash_fwd")

    Gm = get_gm_rest(o_b)
    w_a = Gm[:, REST_WA:REST_ROWS, :].reshape(N_DEV, GROUP_WIDTH_A, LANES).transpose(1, 0, 2).reshape(GROUP_WIDTH_A, D)
    merged, ya, yb, x2 = merge_fwd(o_a, o_b, w_a, Gm, proj, p["b_gate"], x1, "merge_fwd")

    G2 = get_g2(x2)
    n2, ab2, x3 = ffn_fwd(x2, p["ffn2_norm"], G2, "ffn2_fwd")

    loss, dx3, dx3_b, d_final = final_loss(x3, tgt, p["final_norm"], "final_loss")

    dab2, dx2, dx2_b, d_ffn2_norm = ffn_bwd_hidden_input(dx3_b, dx3, ab2, G2, x2, p["ffn2_norm"], "ffn2_bwd")
    gw2 = ffn_bwd_weights(dx3_b, ab2, dab2, n2, "ffn2_bwd")
    t2 = emit("ffn2", gw2)

    dya, dyb, dgate, dbg, do_a, do_b = merge_bwd(dx2_b, w_a, Gm, ya, yb, proj, p["b_gate"] + after(t2),
                                                 "merge_bwd")
    gm_grads = weight_grad_rows(merged, dx2_b, None, MIX_WOUT, "dw_out")
    gm_grads = weight_grad_rows(o_b, dyb, gm_grads, MIX_WB, "dw_branch_b")
    dw_a = weight_grad_plain(o_a, dya, "dw_branch_a")

    dq_r, dk_r, dv_b = flash_bwd(qkv, k_t, do_b, o_b, lse_b, "flash_bwd")
    dq_b, d_q_norm = qk_prep_bwd(dq_r, proj, A_QKV_WIDTH, p["q_norm"], cos_t, sin_t, "q_prep_bwd")
    dk_b, d_k_norm = qk_prep_bwd(dk_r, proj, A_QKV_WIDTH + N_HEADS_B * LANES, p["k_norm"], cos_t, sin_t,
                                 "k_prep_bwd")

    dqkv, dbs = [], []
    for g in range(3):
        dg_, db = a_bwd(proj, bias, do_a, o_a, lse_tot, g, "a_bwd_%d" % g)
        dqkv.append(dg_)
        dbs.append(db)
    d_rel_bias = bias_bwd(jnp.stack(dbs, axis=0).reshape(3, HEADS_PER_GROUP_A, A_TQ, A_WIN), buckets)

    dproj = _dproj_pieces(dqkv, dq_b, jnp.concatenate([dk_b, dv_b], axis=1), dgate)
    gm_grads = in_proj_bwd_dw(dproj, hm, gm_grads, "in_proj_bwd")
    dw_a_sh = dw_a.reshape(GROUP_WIDTH_A, N_DEV, LANES).transpose(1, 0, 2).reshape(N_DEV, MIX_ROWS - MIX_WA, D)
    gm_grads = lax.dynamic_update_slice(gm_grads, dw_a_sh, (0, MIX_WA, 0))
    tm = emit("mix", gm_grads)
    dx1, dx1_b, d_mix_norm = in_proj_bwd_dh(dproj, Gw, x1, p["mix_norm"] + after(tm), dx2, "in_proj_bwd")

    dab1 = ffn_bwd_hidden(dx1_b, ab1, G1, "ffn1_bwd")
    gw1 = ffn_bwd_weights(dx1_b, ab1, dab1, n1, "ffn1_bwd")
    t1 = emit("ffn1", gw1)
    dx0, d_ffn1_norm = ffn_bwd_input(dab1, G1, x, p["ffn1_norm"] + after(t1), dx1, "ffn1_bwd")

    small = dict(ffn1_norm=d_ffn1_norm, mix_norm=d_mix_norm, b_gate=dbg.reshape(1, 2 * D),
                 q_norm=d_q_norm, k_norm=d_k_norm, rel_bias=d_rel_bias, ffn2_norm=d_ffn2_norm,
                 final_norm=d_final)
    return loss, dx0, small


def _pack_small(t, loss_row):
    row6 = jnp.concatenate([t["q_norm"].reshape(1, -1), t["k_norm"].reshape(1, -1), t["rel_bias"].reshape(1, -1)], axis=1)
    return jnp.concatenate([t["ffn1_norm"].reshape(1, -1), t["mix_norm"].reshape(1, -1), t["b_gate"].reshape(2, -1),
                            t["ffn2_norm"].reshape(1, -1), t["final_norm"].reshape(1, -1), row6, loss_row], axis=0)


def _unpack_small(a, shapes):
    return dict(ffn1_norm=a[0:1].reshape(shapes["ffn1_norm"]), mix_norm=a[1:2].reshape(shapes["mix_norm"]),
                b_gate=a[2:4].reshape(shapes["b_gate"]), ffn2_norm=a[4:5].reshape(shapes["ffn2_norm"]),
                final_norm=a[5].reshape(shapes["final_norm"]), q_norm=a[6:7, 0:128].reshape(shapes["q_norm"]),
                k_norm=a[6:7, 128:256].reshape(shapes["k_norm"]), rel_bias=a[6, 256:1024].reshape(shapes["rel_bias"]))


SMALL = ("ffn1_norm", "mix_norm", "b_gate", "q_norm", "k_norm", "rel_bias", "ffn2_norm", "final_norm")
ORDER = ("ffn1_norm", "ffn1_w1", "ffn1_w3", "ffn1_w2", "mix_norm", "w_in", "b_gate", "q_norm", "k_norm", "rel_bias",
         "w_branch_a", "w_branch_b", "w_out", "ffn2_norm", "ffn2_w1", "ffn2_w3", "ffn2_w2", "final_norm")


def kernel(x, ffn1_norm, ffn1_w1, ffn1_w3, ffn1_w2, mix_norm, w_in, b_gate, q_norm, k_norm, rel_bias, w_branch_a, w_branch_b, w_out, ffn2_norm, ffn2_w1, ffn2_w3, ffn2_w2, final_norm, loss_target, m_ffn1_norm, m_ffn1_w1, m_ffn1_w3, m_ffn1_w2, m_mix_norm, m_w_in, m_b_gate, m_q_norm, m_k_norm, m_rel_bias, m_w_branch_a, m_w_branch_b, m_w_out, m_ffn2_norm, m_ffn2_w1, m_ffn2_w3, m_ffn2_w2, m_final_norm, v_ffn1_norm, v_ffn1_w1, v_ffn1_w3, v_ffn1_w2, v_mix_norm, v_w_in, v_b_gate, v_q_norm, v_k_norm, v_rel_bias, v_w_branch_a, v_w_branch_b, v_w_out, v_ffn2_norm, v_ffn2_w1, v_ffn2_w3, v_ffn2_w2, v_final_norm):
    args = dict(locals())
    w = {n: args[n] for n in ORDER}
    m = {n: args["m_" + n] for n in ORDER}
    v = {n: args["v_" + n] for n in ORDER}
    D = x.shape[2]

    blocks = (
        ("ffn1_up", jnp.concatenate([ffn1_w1[0].T, ffn1_w3[0].T], axis=0)),
        ("ffn1_down", ffn1_w2[0]),
        ("mix_in", w_in[0]),
        ("mix_rest", jnp.concatenate([w_branch_b[0], w_out[0], w_branch_a[0].reshape(REST_ROWS - REST_WA, D)], axis=0)),
        ("ffn2", jnp.concatenate([ffn2_w1[0].T, ffn2_w3[0].T, ffn2_w2[0]], axis=0)),
    )
    direct = ("mix_rest", "ffn2")
    started = all_gather_start_all([(b.astype(BF16), tag in direct) for tag, b in blocks], "all_gather_start")
    gathers = {tag: s for (tag, _), s in zip(blocks, started)}
    start_token = started[0][4]

    def gathered(tag):
        def get(after):
            if tag in direct:
                return all_gather_place_own(*_split_wait("all_gather_" + tag + "_wait", gathers[tag], N_DEV - 1, after),
                                            "all_gather_" + tag + "_own")
            return all_gather_finish(*_split_wait("all_gather_" + tag + "_wait", gathers[tag], 4, after),
                                     "all_gather_" + tag + "_finish")
        return get

    core = lax.axis_index("c").astype(jnp.int32).reshape(1)
    chip = (2 * lax.axis_index("x") + lax.axis_index("y")).astype(jnp.int32).reshape(1)
    device = 2 * chip + core
    exchanges = {}

    def emit(tag, gw):
        if tag == "ffn1":
            (theirs,) = reduce_scatter_pair([gw], "reduce_scatter_pair_" + tag)
            part = pair_add(gw, theirs, core, "pair_add_" + tag)
            exchanges[tag] = reduce_scatter_start(part, "reduce_scatter_" + tag + "_start")
        else:
            exchanges[tag] = reduce_scatter_start_direct(gw, "reduce_scatter_" + tag + "_start")
        return exchanges[tag][4]

    small_p = dict(ffn1_norm=ffn1_norm, mix_norm=mix_norm, b_gate=b_gate, q_norm=q_norm, k_norm=k_norm,
                   rel_bias=rel_bias, ffn2_norm=ffn2_norm, final_norm=final_norm.reshape(1, D))
    loss_p, grad_x, small_g = local_step(x[0], loss_target[0], small_p, gathered("ffn1_up"), gathered("ffn1_down"),
                                         gathered("mix_in"), gathered("mix_rest"), gathered("ffn2"), emit, start_token)

    def landed(tag, after):
        n_others, me = (3, chip) if tag == "ffn1" else (N_DEV - 1, device)
        return tuple(_split_wait("reduce_scatter_" + tag + "_wait", exchanges[tag], n_others, after)) + (me,)

    grads, delta, new_m, new_v = {}, {}, {}, {}

    def finish(n, part, land, me, off, blk, transposed=False):
        shp = w[n].shape
        if transposed:
            to2 = lambda a: a.reshape(shp[-2], shp[-1]).T
            back = lambda a: a.T.reshape(shp)
        else:
            to2 = lambda a: a.reshape(shp[-2], shp[-1])
            back = lambda a: a.reshape(shp)
        res = sum_adamw(part, land, me, off, blk, to2(w[n]), to2(m[n]), to2(v[n]), "update_" + n)
        grads[n], delta[n], new_m[n], new_v[n] = [back(a) for a in res]

    last_token = exchanges["ffn1"][4]
    for tag, after in (("ffn2", last_token), ("ffn1", grad_x)):
        group = landed(tag, after)
        finish(tag + "_w1", *group, 0, FFN_SHARD, transposed=True)
        finish(tag + "_w3", *group, FFN_SHARD, FFN_SHARD, transposed=True)
        finish(tag + "_w2", *group, 2 * FFN_SHARD, FFN_SHARD)
        if tag == "ffn2":
            group_m = landed("mix", last_token)
            finish("w_in", *group_m, MIX_WIN, LANES)
            finish("w_branch_b", *group_m, MIX_WB, LANES)
            finish("w_out", *group_m, MIX_WOUT, LANES)
            grads["w_branch_a"] = sum_landed(*group_m, MIX_WA, MIX_ROWS - MIX_WA, MIX_ROWS - MIX_WA,
                                             "w_branch_a_sum").reshape(w_branch_a.shape)
    loss_row = jnp.pad(loss_p, ((0, 0), (0, D - LANES)))
    smalls = small_all_gather(_pack_small(small_g, loss_row))
    small_sum = sum_slots(smalls, 0, N_DEV, N_DEV, "small_sum")
    small_shapes = {n: w[n].shape for n in SMALL}
    grads.update(_unpack_small(small_sum, small_shapes))
    loss = small_sum[7, 0]

    n = "w_branch_a"
    two_d = lambda a: a.reshape(w[n].shape[-2], w[n].shape[-1])
    d_, m_, v_ = adamw(two_d(w[n]), two_d(grads[n]), two_d(m[n]), two_d(v[n]), "adamw_" + n)
    delta[n], new_m[n], new_v[n] = [a.reshape(w[n].shape) for a in (d_, m_, v_)]
    zero_row = jnp.zeros((1, D), F32)
    pack = lambda t: _pack_small({n: t[n] for n in SMALL}, zero_row)
    d_, m_, v_ = adamw(pack(w), small_sum, pack(m), pack(v), "adamw_small")
    for src, dst in ((d_, delta), (m_, new_m), (v_, new_v)):
        dst.update(_unpack_small(src, small_shapes))

    return (loss, grad_x[None], *[grads[n] for n in ORDER], *[delta[n] for n in ORDER],
            *[new_m[n] for n in ORDER], *[new_v[n] for n in ORDER])
```

```python
import math

import jax
import jax.numpy as jnp
from jax import lax
from jax.experimental import pallas as pl
from jax.experimental.pallas import tpu as pltpu

F32 = jnp.float32
BF16 = jnp.bfloat16
MESH = pl.DeviceIdType.MESH

V7X_VMEM_BYTES = 64 * 1024 * 1024
VMEM_LIMIT = V7X_VMEM_BYTES - 8 * 1024 * 1024
LANES = 128

N_DEV = 8
EPS = 1e-6
NEG_INF = -1e30

DILATIONS = (1, 4, 16)
HALF_WINDOW = 64
HEAD_DIM_A = 64
HEADS_PER_GROUP_A = 8
GROUP_WIDTH_A = 512
A_QKV_WIDTH = 4608
A_GROUP_QKV = A_QKV_WIDTH // 3
A_TQ = 128
A_WIN = A_TQ + 2 * HALF_WINDOW
A_UNROLL = 8
A_SCALE = HEAD_DIM_A ** -0.5
WGRAD_TK = 2048
HEAD_DIM_B = 128
N_HEADS_B = 8
N_KV_B = 2
GQA_GROUP_B = 4
GRID_W = 64
ROPE_THETA = 10000.0
B_TQ_FWD = 256
B_TQ_BWD = 512
B_HEADS_PER_STEP = 4
LOG2E = 1.4426950408889634
N_BUCKETS = 32
MAX_DISTANCE = 1024
PB_GATE_A = 1536

ADAM_LR = 0.001
ADAM_B1 = 0.9
ADAM_B2 = 0.999
ADAM_EPS = 1e-08
ADAM_WD = 0.01
ADAM_STEP = 10

FFN_SHARD = 352
MIX_WIN, MIX_WB, MIX_WOUT, MIX_WA = 0, 1024, 1152, 1280
MIX_ROWS = 1344
REST_WB, REST_WOUT, REST_WA, REST_ROWS = 0, 128, 256, 320


def _dot(a, b, ca=1, cb=0):
    return lax.dot_general(a, b, (((ca,), (cb,)), ((), ())), preferred_element_type=F32)


def _call(name, body, grid, ins, outs, scratch=(), sem=None, aliases=None):
    ins = [tuple(i) + (None,) * (4 - len(i)) for i in ins]
    res = pl.pallas_call(
        body,
        out_shape=[jax.ShapeDtypeStruct(s, d) for (s, d, _, _) in outs],
        grid=grid,
        in_specs=[pl.BlockSpec(bs, im, pipeline_mode=pm) for (_, bs, im, pm) in ins],
        out_specs=[pl.BlockSpec(bs, im) for (_, _, bs, im) in outs],
        scratch_shapes=list(scratch),
        name=name,
        input_output_aliases=aliases or {},
        compiler_params=pltpu.CompilerParams(dimension_semantics=sem, vmem_limit_bytes=VMEM_LIMIT),
    )(*[i[0] for i in ins])
    return res


def _sigmoid(x):
    return 0.5 * jnp.tanh(0.5 * x) + 0.5


def _position():
    return lax.axis_index("x"), lax.axis_index("y"), lax.axis_index("c")


def _hbm_specs(n):
    return [pl.BlockSpec(memory_space=pl.ANY) for _ in range(n)]


PAIR_BUFFERS = 4


def reduce_scatter_pair(grads, name):
    n = len(grads)
    C = grads[0].shape[2]
    half = [g.shape[1] // 2 for g in grads]
    chunks = [(i, q, hf) for i in range(n) for q in range(4) for hf in range(2)]
    nb = PAIR_BUFFERS

    def body(*refs):
        ins, theirs = refs[:n], refs[n:2 * n]
        buf, load_sems, send_sems, recv_sems = refs[2 * n:]
        x, y, c = _position()
        sibling = (x, y, 1 - c)

        def load(k):
            i, q, hf = chunks[k]
            r = half[i]
            return pltpu.make_async_copy(ins[i].at[2 * q + (1 - c), pl.ds(hf * r, r), :],
                                         buf.at[k % nb, pl.ds(0, r), :], load_sems.at[k % nb])

        def send(k):
            i, q, hf = chunks[k]
            r = half[i]
            return pltpu.make_async_remote_copy(
                src_ref=buf.at[k % nb, pl.ds(0, r), :], dst_ref=theirs[i].at[q, pl.ds(hf * r, r), :],
                send_sem=send_sems.at[k % nb], recv_sem=recv_sems.at[i],
                device_id=sibling, device_id_type=MESH)

        for k in range(len(chunks) + 1):
            if k < len(chunks):
                if k >= nb:
                    send(k - nb).wait_send()
                load(k).start()
            if k >= 1:
                load(k - 1).wait()
                send(k - 1).start()
        for k in range(max(0, len(chunks) - nb), len(chunks)):
            send(k).wait_send()
        for i in range(n):
            pltpu.make_async_remote_copy(
                src_ref=theirs[i], dst_ref=theirs[i], send_sem=send_sems.at[0], recv_sem=recv_sems.at[i],
                device_id=sibling, device_id_type=MESH).wait_recv()

    return pl.pallas_call(
        body,
        out_shape=[jax.ShapeDtypeStruct((4,) + g.shape[1:], g.dtype) for g in grads],
        in_specs=_hbm_specs(n),
        out_specs=_hbm_specs(n),
        scratch_shapes=[pltpu.VMEM((nb, max(half), C), grads[0].dtype), pltpu.SemaphoreType.DMA((nb,)),
                        pltpu.SemaphoreType.DMA((nb,)), pltpu.SemaphoreType.DMA((n,))],
        name=name,
        compiler_params=pltpu.CompilerParams(vmem_limit_bytes=VMEM_LIMIT),
    )(*grads)


_HBM_SPEC = pl.BlockSpec(memory_space=pltpu.HBM)
_SEM_SPEC = pl.BlockSpec(memory_space=pltpu.SEMAPHORE)
_TOKEN_SPEC = pl.BlockSpec(memory_space=pltpu.VMEM)
_DATAFLOW = pltpu.SideEffectType.DATAFLOW_SIDE_EFFECTING


def _split_start_many(name, exchanges):
    n = len(exchanges)

    def full_body(*refs):
        srcs, lands = refs[:n], refs[n:2 * n]
        sems = refs[2 * n:4 * n]
        token = refs[-1]
        for i, (body, _, _) in enumerate(exchanges):
            body(srcs[i], lands[i], sems[2 * i], sems[2 * i + 1])
        token[...] = jnp.zeros_like(token)

    srcs = [pltpu.with_memory_space_constraint(src, pltpu.HBM) for _, src, _ in exchanges]
    lands = [pltpu.with_memory_space_constraint(lax.empty(shape, src.dtype), pltpu.HBM)
             for _, src, shape in exchanges]
    res = pl.pallas_call(
        full_body, name=name,
        out_shape=(pltpu.SemaphoreType.DMA(()),) * (2 * n)
        + tuple(pltpu.HBM(a.shape, a.dtype) for a in srcs + lands) + (jax.ShapeDtypeStruct((8, LANES), F32),),
        in_specs=(_HBM_SPEC,) * (2 * n),
        out_specs=(_SEM_SPEC,) * (2 * n) + (_HBM_SPEC,) * (2 * n) + (_TOKEN_SPEC,),
        input_output_aliases={i: 2 * n + i for i in range(2 * n)},
        compiler_params=pltpu.CompilerParams(has_side_effects=_DATAFLOW),
    )(*srcs, *lands)
    return [(res[2 * i], res[2 * i + 1], res[2 * n + i], res[3 * n + i], res[-1]) for i in range(n)]


def _split_start(name, body, src, land_shape):
    return _split_start_many(name, [(body, src, land_shape)])[0]


def _split_wait(name, started, n_blocks, after):
    send_sem, recv_sem, src_thru, land_thru, _ = started
    after = after if isinstance(after, tuple) else (after,)

    def body(src_ref, land_ref, send_sem, recv_sem, *rest):
        x, y, c = _position()
        blocks = land_ref.at[pl.ds(0, n_blocks)]
        copy = pltpu.make_async_remote_copy(src_ref=blocks, dst_ref=blocks, send_sem=send_sem, recv_sem=recv_sem,
                                            device_id=(x, y, c), device_id_type=MESH)
        copy.wait_send()
        copy.wait_recv()

    return pl.pallas_call(
        body, name=name,
        out_shape=(pltpu.HBM(src_thru.shape, src_thru.dtype), pltpu.HBM(land_thru.shape, land_thru.dtype)),
        in_specs=(_HBM_SPEC, _HBM_SPEC, _SEM_SPEC, _SEM_SPEC) + (pl.BlockSpec(memory_space=pl.ANY),) * len(after),
        out_specs=(_HBM_SPEC, _HBM_SPEC),
        input_output_aliases={0: 0, 1: 1},
        compiler_params=pltpu.CompilerParams(has_side_effects=_DATAFLOW),
    )(src_thru, land_thru, send_sem, recv_sem, *after)


def all_gather_start_all(blocks, name):
    def starter(direct):
        def body(b_ref, land_ref, send_sem, recv_sem):
            x, y, c = _position()
            peers = _other_devices(x, y, c) if direct else [(x, y, 1 - c), (1 - x, y, c), (x, 1 - y, c),
                                                            (1 - x, 1 - y, c)]
            for peer in peers:
                pltpu.make_async_remote_copy(src_ref=b_ref, dst_ref=land_ref.at[4 * x + 2 * y + c],
                                             send_sem=send_sem, recv_sem=recv_sem,
                                             device_id=peer, device_id_type=MESH).start()
        return body

    return _split_start_many(name, [(starter(direct), block, (N_DEV,) + block.shape) for block, direct in blocks])


def all_gather_finish(block, land, name):
    R, C = block.shape

    def body(b_ref, land_in, land_ref, stage, load_sems, send_sems, recv_sems, own_sem):
        x, y, c = _position()
        sibling = (x, y, 1 - c)
        chips = [(1 - x, y), (x, 1 - y), (1 - x, 1 - y)]
        own_in = pltpu.make_async_copy(b_ref, stage.at[3], load_sems.at[3])
        own_in.start()
        loads = [pltpu.make_async_copy(land_in.at[4 * px + 2 * py + c], stage.at[j], load_sems.at[j])
                 for j, (px, py) in enumerate(chips)]
        for ld in loads:
            ld.start()
        sends = []
        for j, (px, py) in enumerate(chips):
            loads[j].wait()
            dst = land_ref.at[4 * px + 2 * py + c]
            cp = pltpu.make_async_remote_copy(src_ref=stage.at[j], dst_ref=dst, send_sem=send_sems.at[j],
                                              recv_sem=recv_sems.at[j], device_id=sibling, device_id_type=MESH)
            cp.start()
            sends.append(cp)
        own_in.wait()
        own_out = pltpu.make_async_copy(stage.at[3], land_ref.at[4 * x + 2 * y + c], own_sem)
        own_out.start()
        for j, (px, py) in enumerate(chips):
            dst = land_ref.at[4 * px + 2 * py + (1 - c)]
            pltpu.make_async_remote_copy(src_ref=stage.at[j], dst_ref=dst, send_sem=send_sems.at[j],
                                         recv_sem=recv_sems.at[j], device_id=sibling,
                                         device_id_type=MESH).wait_recv()
        for cp in sends:
            cp.wait_send()
        own_out.wait()

    return pl.pallas_call(
        body,
        out_shape=jax.ShapeDtypeStruct(land.shape, land.dtype),
        in_specs=_hbm_specs(2),
        out_specs=pl.BlockSpec(memory_space=pl.ANY),
        scratch_shapes=[pltpu.VMEM((4, R, C), block.dtype), pltpu.SemaphoreType.DMA((4,)),
                        pltpu.SemaphoreType.DMA((3,)), pltpu.SemaphoreType.DMA((3,)), pltpu.SemaphoreType.DMA],
        input_output_aliases={1: 0},
        name=name,
        compiler_params=pltpu.CompilerParams(vmem_limit_bytes=VMEM_LIMIT),
    )(block, land)


def reduce_scatter_start(parts, name):
    def body(p_ref, land_ref, send_sem, recv_sem):
        x, y, c = _position()
        for px, py in [(1 - x, y), (x, 1 - y), (1 - x, 1 - y)]:
            pltpu.make_async_remote_copy(src_ref=p_ref.at[2 * px + py], dst_ref=land_ref.at[2 * x + y],
                                         send_sem=send_sem, recv_sem=recv_sem,
                                         device_id=(px, py, c), device_id_type=MESH).start()

    return _split_start(name, body, parts, parts.shape)


def _other_devices(x, y, c):
    return [(1 - x if k & 4 else x, 1 - y if k & 2 else y, 1 - c if k & 1 else c) for k in range(1, N_DEV)]


def all_gather_place_own(block, land, name):
    R, C = block.shape

    def body(b_ref, land_in, land_ref, stage, sems):
        x, y, c = _position()
        load = pltpu.make_async_copy(b_ref, stage, sems.at[0])
        load.start()
        load.wait()
        store = pltpu.make_async_copy(stage, land_ref.at[4 * x + 2 * y + c], sems.at[1])
        store.start()
        store.wait()

    return pl.pallas_call(
        body,
        out_shape=jax.ShapeDtypeStruct(land.shape, land.dtype),
        in_specs=_hbm_specs(2),
        out_specs=pl.BlockSpec(memory_space=pl.ANY),
        scratch_shapes=[pltpu.VMEM((R, C), block.dtype), pltpu.SemaphoreType.DMA((2,))],
        input_output_aliases={1: 0},
        name=name,
    )(block, land)


def reduce_scatter_start_direct(grads, name):
    def body(g_ref, land_ref, send_sem, recv_sem):
        x, y, c = _position()
        for px, py, pc in _other_devices(x, y, c):
            pltpu.make_async_remote_copy(src_ref=g_ref.at[4 * px + 2 * py + pc],
                                         dst_ref=land_ref.at[4 * x + 2 * y + c],
                                         send_sem=send_sem, recv_sem=recv_sem,
                                         device_id=(px, py, pc), device_id_type=MESH).start()

    return _split_start(name, body, grads, grads.shape)


def small_all_gather(small):
    def body(small_ref, smalls, s_send, s_recv, s_local):
        x, y, c = _position()
        me = 4 * x + 2 * y + c
        lc = pltpu.make_async_copy(small_ref, smalls.at[me], s_local)
        lc.start()
        remote = []
        k = 0
        for dx in (0, 1):
            for dy in (0, 1):
                for dc in (0, 1):
                    if dx + dy + dc == 0:
                        continue
                    peer = (1 - x if dx else x, 1 - y if dy else y, 1 - c if dc else c)
                    rc = pltpu.make_async_remote_copy(
                        src_ref=small_ref, dst_ref=smalls.at[me],
                        send_sem=s_send.at[k], recv_sem=s_recv.at[k],
                        device_id=peer, device_id_type=MESH)
                    rc.start()
                    remote.append(rc)
                    k += 1
        for rc in remote:
            rc.wait()
        lc.wait()

    return pl.pallas_call(
        body,
        out_shape=jax.ShapeDtypeStruct((N_DEV,) + small.shape, small.dtype),
        in_specs=_hbm_specs(1),
        out_specs=pl.BlockSpec(memory_space=pl.ANY),
        scratch_shapes=[pltpu.SemaphoreType.DMA((7,)), pltpu.SemaphoreType.DMA((7,)), pltpu.SemaphoreType.DMA],
        name="small_all_gather",
    )(small)


def pair_add(grads, theirs, core, name):
    _, R, C = theirs.shape
    tr = R // 2

    def body(c_ref, a_ref, b_ref, o_ref):
        o_ref[...] = (a_ref[...].astype(F32) + b_ref[...].astype(F32)).astype(BF16)

    return pl.pallas_call(
        body,
        out_shape=jax.ShapeDtypeStruct(theirs.shape, BF16),
        grid_spec=pltpu.PrefetchScalarGridSpec(
            num_scalar_prefetch=1, grid=(4, R // tr),
            in_specs=[pl.BlockSpec((None, tr, C), lambda q, i, c: (2 * q + c[0], i, 0)),
                      pl.BlockSpec((None, tr, C), lambda q, i, c: (q, i, 0))],
            out_specs=pl.BlockSpec((None, tr, C), lambda q, i, c: (q, i, 0))),
        name=name,
        compiler_params=pltpu.CompilerParams(dimension_semantics=("parallel", "parallel"),
                                             vmem_limit_bytes=VMEM_LIMIT),
    )(core, grads, theirs)


def sum_slots(recv, off, rows, blk, name):
    nq, _, C = recv.shape
    ob = off // blk

    def body(r_ref, o_ref):
        acc = r_ref[0].astype(F32)
        for q in range(1, nq):
            acc = acc + r_ref[q].astype(F32)
        o_ref[...] = acc

    return _call(name, body, (rows // blk,),
                 [(recv, (nq, blk, C), lambda i: (0, ob + i, 0))],
                 [((rows, C), F32, (blk, C), lambda i: (i, 0))], sem=("parallel",))[0]


def _sum_terms(refs):
    acc = refs[0][...].astype(F32)
    for r in refs[1:]:
        acc = acc + r[...].astype(F32)
    return acc


def sum_landed(own, land, me, off, rows, blk, name):
    n, _, C = land.shape
    ob = off // blk

    def body(c_ref, *refs):
        refs[n][...] = _sum_terms(refs[:n])

    def entry(flip):
        return pl.BlockSpec((None, blk, C), lambda i, c: (c[0] ^ flip, ob + i, 0))

    return pl.pallas_call(
        body,
        out_shape=jax.ShapeDtypeStruct((rows, C), F32),
        grid_spec=pltpu.PrefetchScalarGridSpec(
            num_scalar_prefetch=1, grid=(rows // blk,),
            in_specs=[entry(k) for k in range(n)],
            out_specs=pl.BlockSpec((blk, C), lambda i, c: (i, 0))),
        name=name,
        compiler_params=pltpu.CompilerParams(dimension_semantics=("parallel",), vmem_limit_bytes=VMEM_LIMIT),
    )(me, own, *([land] * (n - 1)))


def _adamw_update(wv, gv, mv, vv):
    nm = ADAM_B1 * mv + (1.0 - ADAM_B1) * gv
    nv = ADAM_B2 * vv + (1.0 - ADAM_B2) * (gv * gv)
    c1 = 1.0 / (1.0 - ADAM_B1 ** ADAM_STEP)
    c2 = 1.0 / (1.0 - ADAM_B2 ** ADAM_STEP)
    return -ADAM_LR * ((nm * c1) / (jnp.sqrt(nv * c2) + ADAM_EPS) + ADAM_WD * wv), nm, nv


def sum_adamw(own, land, me, off, blk, w, m, v, name):
    rows, C = w.shape
    n = land.shape[0]
    ob = off // blk

    def body(c_ref, *refs):
        w_ref, m_ref, v_ref, g_out, d_out, m_out, v_out = refs[n:]
        gv = _sum_terms(refs[:n])
        g_out[...] = gv
        d_out[...], m_out[...], v_out[...] = _adamw_update(w_ref[...], gv, m_ref[...], v_ref[...])

    def entry(flip):
        return pl.BlockSpec((None, blk, C), lambda i, c: (c[0] ^ flip, ob + i, 0))

    plain = pl.BlockSpec((blk, C), lambda i, c: (i, 0))
    return pl.pallas_call(
        body,
        out_shape=[jax.ShapeDtypeStruct((rows, C), F32)] * 4,
        grid_spec=pltpu.PrefetchScalarGridSpec(
            num_scalar_prefetch=1, grid=(rows // blk,),
            in_specs=[entry(k) for k in range(n)] + [plain, plain, plain],
            out_specs=[plain] * 4),
        name=name,
        compiler_params=pltpu.CompilerParams(dimension_semantics=("parallel",), vmem_limit_bytes=VMEM_LIMIT),
    )(me, own, *([land] * (n - 1)), w, m, v)


def adamw(w, g, m, v, name):
    R, C = w.shape
    tr = R
    for cand in (256, 128, 64, 32, 16, 8):
        if R % cand == 0 and R > cand:
            tr = cand
            break

    def body(w_ref, g_ref, m_ref, v_ref, d_ref, nm_ref, nv_ref):
        d_ref[...], nm_ref[...], nv_ref[...] = _adamw_update(w_ref[...], g_ref[...], m_ref[...], v_ref[...])

    spec = ((tr, C), lambda i: (i, 0))
    out = ((R, C), F32) + spec
    return _call(name, body, (R // tr,), [(w,) + spec, (g,) + spec, (m,) + spec, (v,) + spec],
                 [out, out, out], sem=("parallel",))


def _rms_tile(xv, gv):
    r = lax.rsqrt(jnp.mean(xv * xv, axis=-1, keepdims=True) + EPS)
    return (xv * r * gv).astype(BF16)


def rms_fwd(x, g, name):
    S, D = x.shape
    tr = 512

    def body(x_ref, g_ref, o_ref):
        o_ref[...] = _rms_tile(x_ref[...], g_ref[...])

    return _call(name, body, (S // tr,),
                 [(x, (tr, D), lambda i: (i, 0)), (g, (1, D), lambda i: (0, 0))],
                 [((S, D), BF16, (tr, D), lambda i: (i, 0))], sem=("parallel",))[0]


def _rms_bwd_tile(dn, xv, gv):
    r = lax.rsqrt(jnp.mean(xv * xv, axis=-1, keepdims=True) + EPS)
    xh = xv * r
    dxh = dn * gv
    dx = r * (dxh - xh * jnp.mean(dxh * xh, axis=-1, keepdims=True))
    return dx, dn * xh


def _final_loss_tile(xv, tv, gv):
    D = xv.shape[1]
    r = lax.rsqrt(jnp.mean(xv * xv, axis=-1, keepdims=True) + EPS)
    xh = xv * r
    e = xh * gv - tv
    part = 0.5 * jnp.sum(jnp.sum(e * e, axis=-1, keepdims=True) * (1.0 / D), axis=0, keepdims=True)
    dy = e * (1.0 / D)
    dxh = dy * gv
    dx = r * (dxh - xh * jnp.mean(dxh * xh, axis=-1, keepdims=True))
    return part, dx, jnp.sum(dy * xh, axis=0, keepdims=True)


FFN_TF = 4 * FFN_SHARD


def _ffn_pick(G, which):
    if isinstance(G, tuple):
        return (G[0], which) if which < 2 else (G[1], 0)
    return G, which


def _ffn_w_spec(G, which, imap):
    arr, blk = _ffn_pick(G, which)
    return (arr, (4, FFN_SHARD, arr.shape[2]), lambda *idx: (imap(*idx), blk, 0))


def _ffn_whole_w_spec(G, which):
    arr, blk = _ffn_pick(G, which)
    return (arr, (N_DEV, FFN_SHARD, arr.shape[2]), lambda *idx: (0, blk, 0), pl.Buffered(1))


def _ffn_hidden(a, b):
    av, bv = a.astype(F32), b.astype(F32)
    return (av * _sigmoid(av) * bv).astype(BF16)


def ffn_up(n, G, name):
    S, D = n.shape
    F = N_DEV * FFN_SHARD
    tm = 256

    def body(n_ref, w1_ref, w3_ref, abh_ref):
        nv = n_ref[...]
        a = _dot(nv, w1_ref[...].reshape(F, D), 1, 1).astype(BF16)
        b = _dot(nv, w3_ref[...].reshape(F, D), 1, 1).astype(BF16)
        abh_ref[0] = a
        abh_ref[1] = b
        abh_ref[2] = _ffn_hidden(a, b)

    return _call(name, body, (S // tm,),
                 [(n, (tm, D), lambda i: (i, 0)),
                  _ffn_whole_w_spec(G, 0), _ffn_whole_w_spec(G, 1)],
                 [((3, S, F), BF16, (3, tm, F), lambda i: (0, i, 0))],
                 sem=("parallel",))[0]


def ffn_down(abh, G, x, g_next, name):
    _, S, F = abh.shape
    D = x.shape[1]
    tm = 512

    def body(h_ref, w2_ref, x_ref, g_ref, o_ref, n_ref):
        xo = x_ref[...] + 0.5 * _dot(h_ref[...], w2_ref[...].reshape(F, D))
        o_ref[...] = xo
        n_ref[...] = _rms_tile(xo, g_ref[...])

    tile = ((tm, D), lambda i: (i, 0))
    return _call(name, body, (S // tm,),
                 [(abh, (None, tm, F), lambda i: (2, i, 0)), _ffn_whole_w_spec(G, 2),
                  (x,) + tile, (g_next, (1, D), lambda i: (0, 0))],
                 [((S, D), F32) + tile, ((S, D), BF16) + tile], sem=("parallel",))


def ffn_last(x, g, G, tgt, g_final, name):
    S, D = x.shape
    F = N_DEV * FFN_SHARD
    tm = 256

    def body(x_ref, g_ref, w1_ref, w3_ref, w2_ref, t_ref, gf_ref,
             n_ref, abh_ref, dxo_ref, dab_ref, dx_ref, dxb_ref, dg_ref, l_ref, dgf_ref):
        i = pl.program_id(0)
        xv, gv = x_ref[...], g_ref[...]
        chunks = [(slice(4 * f, 4 * f + 4), slice(f * FFN_TF, (f + 1) * FFN_TF)) for f in range(F // FFN_TF)]
        weight = lambda w_ref, slots: w_ref[slots].reshape(FFN_TF, D)
        nv = _rms_tile(xv, gv)
        n_ref[...] = nv
        y = None
        for slots, cols in chunks:
            a = _dot(nv, weight(w1_ref, slots), 1, 1).astype(BF16)
            b = _dot(nv, weight(w3_ref, slots), 1, 1).astype(BF16)
            h = _ffn_hidden(a, b)
            abh_ref[0, :, cols] = a
            abh_ref[1, :, cols] = b
            abh_ref[2, :, cols] = h
            t = _dot(h, weight(w2_ref, slots))
            y = t if y is None else y + t
        part, dxo, dgfp = _final_loss_tile(xv + 0.5 * y, t_ref[...], gf_ref[...])
        dxo_b = dxo.astype(BF16)
        dxo_ref[...] = dxo_b
        dn = None
        for slots, cols in chunks:
            dh = 0.5 * _dot(dxo_b, weight(w2_ref, slots), 1, 1)
            da, db = _ffn_hidden_grads(dh, abh_ref[0, :, cols].astype(F32), abh_ref[1, :, cols].astype(F32))
            da, db = da.astype(BF16), db.astype(BF16)
            dab_ref[0, :, cols] = da
            dab_ref[1, :, cols] = db
            t = _dot(da, weight(w1_ref, slots)) + _dot(db, weight(w3_ref, slots))
            dn = t if dn is None else dn + t
        dx, dgt = _rms_bwd_tile(dn, xv, gv)
        dx = dxo + dx
        dx_ref[...] = dx
        dxb_ref[...] = dx.astype(BF16)
        dgp = jnp.sum(dgt, axis=0, keepdims=True)

        @pl.when(i == 0)
        def _():
            dg_ref[...] = dgp
            l_ref[...] = jnp.broadcast_to(part, l_ref.shape)
            dgf_ref[...] = dgfp

        @pl.when(i > 0)
        def _():
            dg_ref[...] += dgp
            l_ref[...] += jnp.broadcast_to(part, l_ref.shape)
            dgf_ref[...] += dgfp

    tile = ((tm, D), lambda i: (i, 0))
    gain = ((1, D), lambda i: (0, 0))
    return _call(name, body, (S // tm,),
                 [(x,) + tile, (g,) + gain,
                  _ffn_whole_w_spec(G, 0), _ffn_whole_w_spec(G, 1), _ffn_whole_w_spec(G, 2),
                  (tgt,) + tile, (g_final,) + gain],
                 [((S, D), BF16) + tile, ((3, S, F), BF16, (3, tm, F), lambda i: (0, i, 0)),
                  ((S, D), BF16) + tile, ((2, S, F), BF16, (2, tm, F), lambda i: (0, i, 0)),
                  ((S, D), F32) + tile, ((S, D), BF16) + tile, ((1, D), F32) + gain,
                  ((1, LANES), F32, (1, LANES), lambda i: (0, 0)), ((1, D), F32) + gain],
                 sem=("arbitrary",))


def _ffn_hidden_grads(dh, av, bv):
    sig = _sigmoid(av)
    return dh * bv * (sig * (1.0 + av * (1.0 - sig))), dh * (av * sig)


def ffn_bwd_hidden(dxo, abh, G, name):
    _, S, F = abh.shape
    D = dxo.shape[1]
    tm = 256

    def body(d_ref, w2_ref, ab_ref, o_ref):
        dh = 0.5 * _dot(d_ref[...].astype(BF16), w2_ref[...].reshape(F, D), 1, 1)
        da, db = _ffn_hidden_grads(dh, ab_ref[0].astype(F32), ab_ref[1].astype(F32))
        o_ref[0] = da.astype(BF16)
        o_ref[1] = db.astype(BF16)

    return _call(name + "_down_bwd", body, (S // tm,),
                 [(dxo, (tm, D), lambda i: (i, 0)), _ffn_whole_w_spec(G, 2),
                  (abh, (2, tm, F), lambda i: (0, i, 0))],
                 [((2, S, F), BF16, (2, tm, F), lambda i: (0, i, 0))],
                 sem=("parallel",))[0]


def ffn_bwd_weights(dxo, abh, dab, n, name):
    _, S, F = abh.shape
    D = dxo.shape[1]
    nf = F // FFN_TF
    tk = WGRAD_TK
    nk = S // tk
    gshape = (N_DEV, 3 * FFN_SHARD, D)

    def dw2_body(h_ref, d_ref, o_ref, acc_ref):
        k = pl.program_id(1)
        p = _dot(h_ref[...], d_ref[...].astype(BF16), 0, 0)

        @pl.when(k == 0)
        def _():
            acc_ref[...] = p

        @pl.when(k > 0)
        def _():
            acc_ref[...] += p

        @pl.when(k == nk - 1)
        def _():
            o_ref[...] = (0.5 * acc_ref[...]).astype(BF16).reshape(4, FFN_SHARD, D)

    gw = _call(name + "_dw2", dw2_body, (nf, nk),
               [(abh, (None, tk, FFN_TF), lambda j, k: (2, k, j)), (dxo, (tk, D), lambda j, k: (k, 0))],
               [(gshape, BF16, (4, FFN_SHARD, D), lambda j, k: (j, 2, 0))],
               scratch=[pltpu.VMEM((FFN_TF, D), F32)], sem=("parallel", "arbitrary"))[0]

    def dw13_body(gw_ref, dab_ref, n_ref, o_ref):
        o_ref[...] = _dot(dab_ref[...], n_ref[...], 0, 0).astype(BF16).reshape(4, FFN_SHARD, D)

    gw = pl.pallas_call(
        dw13_body,
        out_shape=jax.ShapeDtypeStruct(gshape, BF16),
        grid=(2, nf),
        in_specs=[pl.BlockSpec(memory_space=pl.ANY),
                  pl.BlockSpec((None, S, FFN_TF), lambda w, j: (w, 0, j)),
                  pl.BlockSpec((S, D), lambda w, j: (0, 0))],
        out_specs=pl.BlockSpec((4, FFN_SHARD, D), lambda w, j: (j, w, 0)),
        input_output_aliases={0: 0},
        name=name + "_dw13",
        compiler_params=pltpu.CompilerParams(dimension_semantics=("parallel", "parallel"),
                                             vmem_limit_bytes=VMEM_LIMIT),
    )(gw, dab, n)
    return gw


def ffn_bwd_input(dab, G, x_in, g, dxo, name):
    _, S, F = dab.shape
    D = x_in.shape[1]
    tm = 256

    def dn_body(dab_ref, w1_ref, w3_ref, x_ref, d_ref, g_ref, dx_ref, dg_ref):
        i = pl.program_id(0)
        dn = _dot(dab_ref[0], w1_ref[...].reshape(F, D)) + _dot(dab_ref[1], w3_ref[...].reshape(F, D))
        dx, dgt = _rms_bwd_tile(dn, x_ref[...], g_ref[...])
        dx_ref[...] = d_ref[...] + dx
        dgp = jnp.sum(dgt, axis=0, keepdims=True)

        @pl.when(i == 0)
        def _():
            dg_ref[...] = dgp

        @pl.when(i > 0)
        def _():
            dg_ref[...] += dgp

    tile = ((tm, D), lambda i: (i, 0))
    return _call(name + "_dn", dn_body, (S // tm,),
                 [(dab, (2, tm, F), lambda i: (0, i, 0)),
                  _ffn_whole_w_spec(G, 0), _ffn_whole_w_spec(G, 1),
                  (x_in,) + tile, (dxo,) + tile, (g, (1, D), lambda i: (0, 0))],
                 [((S, D), F32) + tile, ((1, D), F32, (1, D), lambda i: (0, 0))],
                 sem=("arbitrary",))


PROJ_TN = 512
DH_SHARDS_PER_STEP = 4


def in_proj(h, Gm, name):
    S, D = h.shape
    n_tiles = N_DEV * Gm.shape[2] // PROJ_TN

    def body(h_ref, w_ref, o_ref):
        o_ref[...] = _dot(h_ref[...], w_ref[...]).astype(BF16)

    return _call(name, body, (n_tiles,),
                 [(h, (S, D), lambda j: (0, 0)),
                  (Gm, (None, D, PROJ_TN), lambda j: (j // 2, 0, j % 2))],
                 [((S, n_tiles * PROJ_TN), BF16, (S, PROJ_TN), lambda j: (0, j))],
                 sem=("parallel",))[0]


def _dproj_pieces(dqkv, dq_b, dkv_b, dgate):
    pieces = [(dqkv[g], [(3 * which + g, (which, 0)) for which in range(3)]) for g in range(3)]
    pieces.append((dq_b, [(9, (None, 0)), (10, (None, 1))]))
    pieces.append((dkv_b, [(11, (None, 0))]))
    pieces.append((dgate, [(12 + 2 * a + b, (a, b)) for a in range(2) for b in range(2)]))
    return pieces


def in_proj_bwd_dw(pieces, h, gm_grads, name):
    S, D = h.shape

    for n_piece, (arr, tiles) in enumerate(pieces):
        w_tile = [t for t, _ in tiles]
        lead = [ix[0] for _, ix in tiles]
        colb = [ix[1] for _, ix in tiles]

        def pick(table, j):
            out = table[-1]
            for k in range(len(table) - 2, -1, -1):
                out = jnp.where(j == k, table[k], out)
            return out

        def dw_body(gm_ref, h_ref, d_ref, o_ref):
            o_ref[...] = _dot(h_ref[...], d_ref[...], 0, 0).astype(BF16)

        if arr.ndim == 3:
            d_spec = pl.BlockSpec((None, S, PROJ_TN), lambda j, lead=lead, colb=colb: (pick(lead, j), 0, pick(colb, j)))
        else:
            d_spec = pl.BlockSpec((S, PROJ_TN), lambda j, colb=colb: (0, pick(colb, j)))
        gm_grads = pl.pallas_call(
            dw_body,
            out_shape=jax.ShapeDtypeStruct(gm_grads.shape, BF16),
            grid=(len(tiles),),
            in_specs=[pl.BlockSpec(memory_space=pl.ANY), pl.BlockSpec((S, D), lambda j: (0, 0)), d_spec],
            out_specs=pl.BlockSpec((None, D, PROJ_TN),
                                   lambda j, w_tile=w_tile: (pick(w_tile, j) // 2, 0, pick(w_tile, j) % 2)),
            input_output_aliases={0: 0},
            name="%s_dw%d" % (name, n_piece),
            compiler_params=pltpu.CompilerParams(dimension_semantics=("parallel",), vmem_limit_bytes=VMEM_LIMIT),
        )(gm_grads, h, arr)
    return gm_grads


def in_proj_bwd_dh(pieces, Gm, x_in, g, dres, name):
    S, D = x_in.shape
    tm = 256
    C = Gm.shape[2]
    n_sh = N_DEV
    n_p = len(pieces)

    def dh_body(*refs):
        d_refs = refs[:n_p]
        w_ref, x_ref, r_ref, g_ref, dx_ref, dxb_ref, dg_ref = refs[n_p:]
        i = pl.program_id(0)
        p = None
        for d_ref, (arr, tiles) in zip(d_refs, pieces):
            for t, (lead, colb) in tiles:
                cols = slice(colb * PROJ_TN, (colb + 1) * PROJ_TN)
                d = d_ref[:, cols] if lead is None else d_ref[lead, :, cols]
                wcol = (t % 2) * PROJ_TN
                term = _dot(d, w_ref[t // 2, :, wcol:wcol + PROJ_TN], 1, 1)
                p = term if p is None else p + term
        dx, dgt = _rms_bwd_tile(p, x_ref[...], g_ref[...])
        dx = r_ref[...] + dx
        dx_ref[...] = dx
        dxb_ref[...] = dx.astype(BF16)
        dgp = jnp.sum(dgt, axis=0, keepdims=True)

        @pl.when(i == 0)
        def _():
            dg_ref[...] = dgp

        @pl.when(i > 0)
        def _():
            dg_ref[...] += dgp

    tile = ((tm, D), lambda i: (i, 0))

    def rows_of(arr):
        if arr.ndim == 3:
            return (arr, (arr.shape[0], tm, arr.shape[2]), lambda i: (0, i, 0))
        return (arr, (tm, arr.shape[1]), lambda i: (i, 0))

    return _call(name + "_dh", dh_body, (S // tm,),
                 [rows_of(arr) for arr, _ in pieces]
                 + [(Gm, (n_sh, D, C), lambda i: (0, 0, 0), pl.Buffered(1)),
                    (x_in,) + tile, (dres,) + tile, (g, (1, D), lambda i: (0, 0))],
                 [((S, D), F32) + tile, ((S, D), BF16) + tile, ((1, D), F32, (1, D), lambda i: (0, 0))],
                 sem=("arbitrary",))


def _t5_bucket(rel):
    n = N_BUCKETS // 2
    max_exact = n // 2
    ret = jnp.where(rel > 0, n, 0)
    a = jnp.abs(rel)
    af = jnp.maximum(a, 1).astype(F32)
    large = max_exact + (jnp.log(af / max_exact) / math.log(MAX_DISTANCE / max_exact)
                         * (n - max_exact)).astype(jnp.int32)
    large = jnp.minimum(large, n - 1)
    return ret + jnp.where(a < max_exact, a, large)


def _bucket_tables():
    qi = jnp.arange(A_TQ, dtype=jnp.int32)[:, None]
    kj = jnp.arange(A_WIN, dtype=jnp.int32)[None, :]
    rel = kj - HALF_WINDOW - qi
    return jnp.stack([_t5_bucket(rel * d) for d in DILATIONS], axis=0)


def bias_build(rel_bias, buckets):
    def body(tab_ref, bk_ref, o_ref):
        col = pl.program_id(0) * HEADS_PER_GROUP_A + pl.program_id(1)
        bk = bk_ref[...]
        acc = jnp.zeros(bk.shape, F32)
        for b in range(N_BUCKETS):
            acc = jnp.where(bk == b, tab_ref[b, col], acc)
        qi = lax.broadcasted_iota(jnp.int32, bk.shape, 0)
        kj = lax.broadcasted_iota(jnp.int32, bk.shape, 1)
        band = jnp.where(jnp.abs(kj - HALF_WINDOW - qi) <= HALF_WINDOW, acc, NEG_INF)
        o_ref[0] = jnp.where(kj >= HALF_WINDOW, band, NEG_INF)
        o_ref[1] = band
        o_ref[2] = jnp.where(kj < A_TQ + HALF_WINDOW, band, NEG_INF)

    out = pl.pallas_call(
        body,
        out_shape=jax.ShapeDtypeStruct((3, HEADS_PER_GROUP_A // 2, 3, 2, A_TQ, A_WIN), F32),
        grid=(3, HEADS_PER_GROUP_A),
        in_specs=[pl.BlockSpec(memory_space=pltpu.SMEM),
                  pl.BlockSpec((None, A_TQ, A_WIN), lambda g, h: (g, 0, 0))],
        out_specs=pl.BlockSpec((None, None, 3, None, A_TQ, A_WIN), lambda g, h: (g, h // 2, 0, h % 2, 0, 0)),
        name="a_bias_build",
        compiler_params=pltpu.CompilerParams(dimension_semantics=("parallel", "parallel")),
    )(rel_bias, buckets)
    return out.reshape(3, HEADS_PER_GROUP_A // 2, 3, 2 * A_TQ, A_WIN)


def bias_bwd(dbias, buckets):
    def body(d_ref, bk_ref, o_ref):
        bk = bk_ref[...]
        dv = d_ref[...]
        for b in range(N_BUCKETS):
            part = jnp.sum(jnp.where(bk == b, dv, 0.0), axis=1, keepdims=True)
            o_ref[b:b + 1, :] = jnp.broadcast_to(jnp.sum(part, axis=0, keepdims=True), (1, LANES))

    out = pl.pallas_call(
        body,
        out_shape=jax.ShapeDtypeStruct((3, HEADS_PER_GROUP_A, N_BUCKETS, LANES), F32),
        grid=(3, HEADS_PER_GROUP_A),
        in_specs=[pl.BlockSpec((None, None, A_TQ, A_WIN), lambda g, h: (g, h, 0, 0)),
                  pl.BlockSpec((None, A_TQ, A_WIN), lambda g, h: (g, 0, 0))],
        out_specs=pl.BlockSpec((None, None, N_BUCKETS, LANES), lambda g, h: (g, h, 0, 0)),
        name="a_bias_bwd",
        compiler_params=pltpu.CompilerParams(dimension_semantics=("parallel", "parallel")),
    )(dbias, buckets)
    return out[:, :, :, 0].transpose(2, 0, 1).reshape(N_BUCKETS, 3 * HEADS_PER_GROUP_A)


def _a_fill_padded(pad_ref, src_ref, n, pad):
    zeros = jnp.zeros((pad, LANES), pad_ref.dtype)
    pad_ref[0:pad, :] = zeros
    pad_ref[pad + n:2 * pad + n, :] = zeros
    pad_ref[pad:pad + n, :] = src_ref[...].astype(pad_ref.dtype)


def _a_stack_heads(x, lane):
    zero = jnp.zeros_like(x)
    return jnp.concatenate([jnp.where(lane < HEAD_DIM_A, x, zero), jnp.where(lane >= HEAD_DIM_A, x, zero)], axis=0)


def _a_bias_variant(qb, nqb):
    return jnp.where(qb == 0, 0, jnp.where(qb == nqb - 1, 2, 1))


def _a_slab_specs(proj, g):
    S = proj.shape[0]
    per = GROUP_WIDTH_A // LANES
    return [(proj, (S, LANES), lambda hp, w=w: (0, per * (3 * w + g) + hp)) for w in range(3)]


def a_fwd(proj, bias, g, name):
    S = proj.shape[0]
    d = DILATIONS[g]
    L = S // d
    nqb = L // A_TQ
    pad = HALF_WINDOW * d

    def body(q_ref, k_ref, v_ref, b_ref, o_ref, l_ref, qf, kpad, vpad):
        qf[...] = q_ref[...].astype(F32) * A_SCALE
        _a_fill_padded(kpad, k_ref, S, pad)
        _a_fill_padded(vpad, v_ref, S, pad)
        lane = lax.broadcasted_iota(jnp.int32, (A_TQ, LANES), 1)

        def block(t, carry):
            qb, r = t // d, t % d
            start = qb * (A_TQ * d) + r
            kw = kpad[pl.ds(start, A_WIN, stride=d), :].astype(BF16)
            vw = vpad[pl.ds(start, A_WIN, stride=d), :].astype(BF16)
            q = qf[pl.ds(start, A_TQ, stride=d), :].astype(BF16)
            q2 = _a_stack_heads(q, lane)
            s = _dot(q2, kw, 1, 1) + b_ref[_a_bias_variant(qb, nqb)]
            m = jnp.max(s, axis=-1, keepdims=True)
            e = jnp.exp(s - m)
            l = jnp.sum(e, axis=-1, keepdims=True)
            o2 = _dot(e.astype(BF16), vw) / l
            lse2 = m + jnp.log(l)
            o_ref[pl.ds(start, A_TQ, stride=d), :] = jnp.where(lane < HEAD_DIM_A, o2[0:A_TQ], o2[A_TQ:])
            l_ref[pl.ds(start, A_TQ, stride=d), :] = jnp.where(lane < HEAD_DIM_A, lse2[0:A_TQ], lse2[A_TQ:])
            return carry

        lax.fori_loop(0, nqb * d, block, 0, unroll=A_UNROLL)

    out_spec = ((S, GROUP_WIDTH_A), F32, (S, LANES), lambda hp: (0, hp))
    return _call(name, body, (4,),
                 _a_slab_specs(proj, g)
                 + [(bias, (None, None, 3, 2 * A_TQ, A_WIN), lambda hp: (g, hp, 0, 0, 0))],
                 [out_spec, out_spec],
                 scratch=[pltpu.VMEM((S, LANES), F32)] + [pltpu.VMEM((S + 2 * pad, LANES), F32)] * 2,
                 sem=("parallel",))


def a_combine(outs, lses, name):
    S, W = outs[0].shape
    tr = 512

    def body(o0, o1, o2, l0, l1, l2, oa_ref, lt_ref):
        a, b, c = l0[...], l1[...], l2[...]
        m = jnp.maximum(jnp.maximum(a, b), c)
        ea, eb, ec = jnp.exp(a - m), jnp.exp(b - m), jnp.exp(c - m)
        z = ea + eb + ec
        oa_ref[...] = ((ea * o0[...] + eb * o1[...] + ec * o2[...]) / z).astype(BF16)
        lt_ref[...] = m + jnp.log(z)

    spec = ((tr, W), lambda i: (i, 0))
    return _call(name, body, (S // tr,), [(a,) + spec for a in (*outs, *lses)],
                 [((S, W), BF16) + spec, ((S, W), F32) + spec], sem=("parallel",))


def a_bwd(proj, bias, do_a, o_a, lse_tot, g, name):
    S = proj.shape[0]
    d = DILATIONS[g]
    L = S // d
    nqb = L // A_TQ
    pad = HALF_WINDOW * d

    def body(q_ref, k_ref, v_ref, b_ref, do_ref, o_ref, l_ref, dqkv_ref, db_ref,
             qf, of, dqf, kpad, vpad, dkacc, dvacc):
        qf[...] = q_ref[...].astype(F32) * A_SCALE
        of[...] = o_ref[...].astype(F32)
        _a_fill_padded(kpad, k_ref, S, pad)
        _a_fill_padded(vpad, v_ref, S, pad)
        dkacc[...] = jnp.zeros(dkacc.shape, F32)
        dvacc[...] = jnp.zeros(dvacc.shape, F32)
        db_ref[...] = jnp.zeros(db_ref.shape, F32)
        lane = lax.broadcasted_iota(jnp.int32, (A_TQ, LANES), 1)

        def block(t, carry):
            qb, r = t // d, t % d
            start = qb * (A_TQ * d) + r
            rows = pl.ds(start, A_TQ, stride=d)
            win = pl.ds(start, A_WIN, stride=d)
            kw = kpad[win, :].astype(BF16)
            vw = vpad[win, :].astype(BF16)
            q = qf[rows, :].astype(BF16)
            do = do_ref[rows, :]
            ov = of[rows, :]
            lt = l_ref[rows, :]
            q2 = _a_stack_heads(q, lane)
            do2 = _a_stack_heads(do, lane)
            lt2 = jnp.concatenate([lt[:, 0:1], lt[:, HEAD_DIM_A:HEAD_DIM_A + 1]], axis=0)
            s = _dot(q2, kw, 1, 1) + b_ref[_a_bias_variant(qb, nqb)]
            p = jnp.exp(s - lt2)
            t = jnp.sum(do2 * jnp.concatenate([ov, ov], axis=0), axis=-1, keepdims=True)
            dob2 = do2.astype(BF16)
            ds = p * (_dot(dob2, vw, 1, 1) - t)
            db_ref[...] += ds
            dsb = ds.astype(BF16)
            dq2 = _dot(dsb, kw)
            dqf[rows, :] = jnp.where(lane < HEAD_DIM_A, dq2[0:A_TQ], dq2[A_TQ:]) * A_SCALE
            dkacc[win, :] += _dot(dsb, q2, 0, 0)
            dvacc[win, :] += _dot(p.astype(BF16), dob2, 0, 0)
            return carry

        lax.fori_loop(0, nqb * d, block, 0, unroll=A_UNROLL)
        dqkv_ref[0] = dqf[...].astype(BF16)
        dqkv_ref[1] = dkacc[pad:pad + S, :].astype(BF16)
        dqkv_ref[2] = dvacc[pad:pad + S, :].astype(BF16)

    slab = ((S, LANES), lambda hp: (0, hp))
    padded = pltpu.VMEM((S + 2 * pad, LANES), F32)
    return _call(
        name, body, (4,),
        _a_slab_specs(proj, g)
        + [(bias, (None, None, 3, 2 * A_TQ, A_WIN), lambda hp: (g, hp, 0, 0, 0)),
           (do_a,) + slab, (o_a,) + slab, (lse_tot,) + slab],
        [((3, S, GROUP_WIDTH_A), BF16, (3, S, LANES), lambda hp: (0, 0, hp)),
         ((4, 2 * A_TQ, A_WIN), F32, (None, 2 * A_TQ, A_WIN), lambda hp: (hp, 0, 0))],
        scratch=[pltpu.VMEM((S, LANES), F32)] * 3 + [padded] * 4,
        sem=("parallel",))


def _rope_tables(S):
    rows = S // GRID_W
    row = jnp.repeat(jnp.arange(rows, dtype=F32), GRID_W)
    col = jnp.tile(jnp.arange(GRID_W, dtype=F32), rows)
    n_freq = HEAD_DIM_B // 4
    freq = ROPE_THETA ** (-jnp.arange(n_freq, dtype=F32) / n_freq)
    ang = jnp.concatenate([row[:, None] * freq, col[:, None] * freq], axis=-1)
    cos, sin = jnp.cos(ang), jnp.sin(ang)
    return jnp.repeat(cos, 2, axis=-1), jnp.stack([-sin, sin], axis=-1).reshape(S, HEAD_DIM_B)


def _swap_pairs(y):
    lane = lax.broadcasted_iota(jnp.int32, y.shape, 1)
    return jnp.where(lane % 2 == 0, pltpu.roll(y, LANES - 1, 1), pltpu.roll(y, 1, 1))


def qkv_prep(proj, gains, cos_t, sin_t, name):
    S = proj.shape[0]
    ts = 256
    n_rot = N_HEADS_B + N_KV_B
    nh = n_rot + N_KV_B
    W = nh * LANES

    def body(x_ref, g_ref, c_ref, s_ref, o_ref):
        cv, sv = c_ref[...], s_ref[...]
        for hb in range(nh):
            cols = slice(hb * LANES, (hb + 1) * LANES)
            if hb < n_rot:
                xv = x_ref[:, cols].astype(F32)
                r = lax.rsqrt(jnp.mean(xv * xv, axis=-1, keepdims=True) + EPS)
                yv = xv * r * g_ref[:, cols]
                o_ref[:, cols] = (yv * cv + _swap_pairs(yv) * sv).astype(BF16)
            else:
                o_ref[:, cols] = x_ref[:, cols]

    return _call(name, body, (S // ts,),
                 [(proj, (ts, W), lambda i: (i, A_QKV_WIDTH // W)), (gains, (1, W), lambda i: (0, 0)),
                  (cos_t, (ts, LANES), lambda i: (i, 0)), (sin_t, (ts, LANES), lambda i: (i, 0))],
                 [((S, W), BF16, (ts, W), lambda i: (i, 0))],
                 sem=("parallel",))[0]


def qk_prep_bwd(dr, proj, col0, gain, cos_t, sin_t, name):
    S, W = dr.shape
    H = W // LANES
    ts = 256
    wx = math.gcd(W, col0)
    n_x = W // wx

    def body(d_ref, *refs):
        x_refs = refs[:n_x]
        g_ref, c_ref, s_ref, dx_ref, dg_ref = refs[n_x:]
        i = pl.program_id(0)
        cv, sv, gv = c_ref[...], s_ref[...], g_ref[...]
        dgp = jnp.zeros((1, LANES), F32)
        for hb in range(H):
            cols = slice(hb * LANES, (hb + 1) * LANES)
            xc = (hb * LANES) % wx
            xv = x_refs[(hb * LANES) // wx][:, xc:xc + LANES].astype(F32)
            dout = d_ref[:, cols]
            dy = dout * cv + _swap_pairs(dout * sv)
            dx, dgt = _rms_bwd_tile(dy, xv, gv)
            dx_ref[:, cols] = dx.astype(BF16)
            dgp = dgp + jnp.sum(dgt, axis=0, keepdims=True)

        @pl.when(i == 0)
        def _():
            dg_ref[...] = dgp

        @pl.when(i > 0)
        def _():
            dg_ref[...] += dgp

    return _call(name, body, (S // ts,),
                 [(dr, (ts, W), lambda i: (i, 0))]
                 + [(proj, (ts, wx), lambda i, k=k: (i, col0 // wx + k)) for k in range(n_x)]
                 + [(gain, (1, LANES), lambda i: (0, 0)),
                  (cos_t, (ts, LANES), lambda i: (i, 0)), (sin_t, (ts, LANES), lambda i: (i, 0))],
                 [((S, W), BF16, (ts, W), lambda i: (i, 0)),
                  ((1, LANES), F32, (1, LANES), lambda i: (0, 0))],
                 sem=("arbitrary",))


def _row_sums(x):
    hi = x.astype(BF16)
    lo = (x - hi.astype(F32)).astype(BF16)
    ones = jnp.ones((8, LANES), BF16)
    return (_dot(ones, hi, 1, 1) + _dot(ones, lo, 1, 1))[0:1, :]


def flash_fwd(qkv, name):
    S = qkv.shape[0]
    tq = B_TQ_FWD
    scale = HEAD_DIM_B ** -0.5

    hps = B_HEADS_PER_STEP

    def body(q_ref, k_ref, v_ref, o_ref, l_ref):
        k, v = k_ref[...], v_ref[...]
        for j in range(hps):
            cols = slice(j * LANES, (j + 1) * LANES)
            s = _dot(q_ref[:, cols], k, 1, 1)
            m = jnp.max(s, axis=-1, keepdims=True)
            e = jnp.exp2((s - m) * (scale * LOG2E))
            l = jnp.sum(e, axis=-1, keepdims=True)
            o_ref[:, cols] = (_dot(e.astype(BF16), v) / l).astype(BF16)
            lse = jnp.broadcast_to(m * scale + jnp.log(l), (tq, LANES))
            l_ref[j] = _row_sums(lse) * (1.0 / LANES)

    per = GQA_GROUP_B // hps
    heads = lambda g, h, i: (i, g * per + h)
    return _call(name, body, (N_KV_B, per, S // tq),
                 [(qkv, (tq, hps * LANES), heads),
                  (qkv, (S, LANES), lambda g, h, i: (0, N_HEADS_B + g)),
                  (qkv, (S, LANES), lambda g, h, i: (0, N_HEADS_B + N_KV_B + g))],
                 [((S, N_HEADS_B * LANES), BF16, (tq, hps * LANES), heads),
                  ((N_HEADS_B, 1, S), F32, (hps, 1, tq), lambda g, h, i: (g * per + h, 0, i))],
                 sem=("parallel", "parallel", "parallel"))


def flash_bwd(qkv, k_t, do_b, o_b, lse, name):
    S = qkv.shape[0]
    tq = B_TQ_BWD
    nq = S // tq
    scale = HEAD_DIM_B ** -0.5

    def body(q_ref, k_ref, v_ref, kt_ref, do_ref, o_ref, l_ref, dq_ref, dk_ref, dv_ref, dkacc, dvacc):
        h, i = pl.program_id(1), pl.program_id(2)

        @pl.when((h == 0) & (i == 0))
        def _():
            dkacc[...] = jnp.zeros(dkacc.shape, F32)
            dvacc[...] = jnp.zeros(dvacc.shape, F32)

        q = q_ref[...]
        dob = do_ref[...]
        t = _row_sums(dob.astype(F32) * o_ref[...].astype(F32))
        pt = jnp.exp2(_dot(k_ref[...], q, 1, 1) * (scale * LOG2E) - l_ref[...] * LOG2E)
        dsb = (pt * (_dot(v_ref[...], dob, 1, 1) - t)).astype(BF16)
        dvacc[...] += _dot(pt.astype(BF16), dob)
        dkacc[...] += _dot(dsb, q)
        dq_ref[...] = _dot(kt_ref[...], dsb).T * scale

        @pl.when((h == GQA_GROUP_B - 1) & (i == nq - 1))
        def _():
            dk_ref[...] = dkacc[...] * scale
            dv_ref[...] = dvacc[...].astype(BF16)

    head = lambda g, h, i: (i, g * GQA_GROUP_B + h)
    return _call(name, body, (N_KV_B, GQA_GROUP_B, nq),
                 [(qkv, (tq, LANES), head),
                  (qkv, (S, LANES), lambda g, h, i: (0, N_HEADS_B + g)),
                  (qkv, (S, LANES), lambda g, h, i: (0, N_HEADS_B + N_KV_B + g)),
                  (k_t, (LANES, S), lambda g, h, i: (g, 0)),
                  (do_b, (tq, LANES), head), (o_b, (tq, LANES), head),
                  (lse, (None, 1, tq), lambda g, h, i: (g * GQA_GROUP_B + h, 0, i))],
                 [((S, N_HEADS_B * LANES), F32, (tq, LANES), head),
                  ((S, N_KV_B * LANES), F32, (S, LANES), lambda g, h, i: (0, g)),
                  ((S, N_KV_B * LANES), BF16, (S, LANES), lambda g, h, i: (0, g))],
                 scratch=[pltpu.VMEM((S, LANES), F32)] * 2,
                 sem=("parallel", "arbitrary", "arbitrary"))


MERGE_TN = 512


def _mix_rows_spec(Gm, row0, n_slots, slot_map, cols=None, col_map=None):
    C = Gm.shape[2] if cols is None else cols
    cm = (lambda *idx: 0) if col_map is None else col_map
    return (Gm, (n_slots, LANES, C), lambda *idx: (slot_map(*idx), row0 // LANES, cm(*idx)))


def _gate_specs(proj, tm):
    first = (A_QKV_WIDTH + PB_GATE_A) // MERGE_TN
    return [(proj, (tm, MERGE_TN), lambda i, k=k: (i, first + k)) for k in range(4)]


def _whole_rows_spec(Gm, row0):
    return _mix_rows_spec(Gm, row0, N_DEV, lambda *idx: 0)


def merge_fwd(o_a, o_b, w_a, Gm, proj, b_gate, x, name):
    S, D = x.shape
    tm = 256

    def body(oa_ref, ob_ref, wa_ref, wb_ref, wo_ref, g0, g1, g2, g3, bg_ref, x_ref, m_ref, ya_ref, yb_ref, xo_ref):
        ya = _dot(oa_ref[...], wa_ref[...])
        yb = _dot(ob_ref[...], wb_ref[...].reshape(N_DEV * LANES, D))
        ga = _sigmoid(jnp.concatenate([g0[...], g1[...]], axis=1).astype(F32) + bg_ref[:, 0:D])
        gb = _sigmoid(jnp.concatenate([g2[...], g3[...]], axis=1).astype(F32) + bg_ref[:, D:2 * D])
        merged = (ga * ya + gb * yb).astype(BF16)
        m_ref[...] = merged
        ya_ref[...] = ya.astype(BF16)
        yb_ref[...] = yb.astype(BF16)
        xo_ref[...] = x_ref[...] + _dot(merged, wo_ref[...].reshape(N_DEV * LANES, D))

    rows = lambda a: (a, (tm, a.shape[1]), lambda i: (i, 0))
    out = ((S, D), BF16, (tm, D), lambda i: (i, 0))
    return _call(name, body, (S // tm,),
                 [rows(o_a), rows(o_b), (w_a, w_a.shape, lambda i: (0, 0)),
                  _whole_rows_spec(Gm, REST_WB), _whole_rows_spec(Gm, REST_WOUT)]
                 + _gate_specs(proj, tm) + [(b_gate, (1, 2 * D), lambda i: (0, 0)), rows(x)],
                 [out, out, out, ((S, D), F32, (tm, D), lambda i: (i, 0))], sem=("parallel",))


def merge_bwd(dx2, w_a, Gm, ya, yb, proj, b_gate, name):
    S, D = dx2.shape
    tm = 256

    def body(d_ref, wo_ref, wa_ref, wb_ref, ya_ref, yb_ref, g0, g1, g2, g3, bg_ref,
             dya_ref, dyb_ref, dg_ref, dbg_ref, doa_ref, dob_ref):
        i = pl.program_id(0)
        dm = _dot(d_ref[...].astype(BF16), wo_ref[...].reshape(N_DEV * LANES, D), 1, 1)
        ga = _sigmoid(jnp.concatenate([g0[...], g1[...]], axis=1).astype(F32) + bg_ref[:, 0:D])
        gb = _sigmoid(jnp.concatenate([g2[...], g3[...]], axis=1).astype(F32) + bg_ref[:, D:2 * D])
        dya = (dm * ga).astype(BF16)
        dyb = (dm * gb).astype(BF16)
        dya_ref[...] = dya
        dyb_ref[...] = dyb
        dpa = dm * ya_ref[...].astype(F32) * ga * (1.0 - ga)
        dpb = dm * yb_ref[...].astype(F32) * gb * (1.0 - gb)
        dg_ref[0] = dpa.astype(BF16)
        dg_ref[1] = dpb.astype(BF16)
        doa_ref[...] = _dot(dya, wa_ref[...], 1, 1)
        dob_ref[...] = _dot(dyb, wb_ref[...].reshape(N_DEV * LANES, D), 1, 1).astype(BF16)
        sa =jnp.sum(dpa, axis=0, keepdims=True)
        sb = jnp.sum(dpb, axis=0, keepdims=True)

        @pl.when(i == 0)
        def _():
            dbg_ref[0] = sa
            dbg_ref[1] = sb

        @pl.when(i > 0)
        def _():
            dbg_ref[0] += sa
            dbg_ref[1] += sb

    tile = ((tm, D), lambda i: (i, 0))
    return _call(
        name, body, (S // tm,),
        [(dx2,) + tile, _whole_rows_spec(Gm, REST_WOUT), (w_a, w_a.shape, lambda i: (0, 0)),
         _whole_rows_spec(Gm, REST_WB), (ya,) + tile, (yb,) + tile]
        + _gate_specs(proj, tm) + [(b_gate, (1, 2 * D), lambda i: (0, 0))],
        [((S, D), BF16) + tile, ((S, D), BF16) + tile,
         ((2, S, D), BF16, (2, tm, D), lambda i: (0, i, 0)),
         ((2, 1, D), F32, (2, 1, D), lambda i: (0, 0, 0)),
         ((S, w_a.shape[0]), F32, (tm, w_a.shape[0]), lambda i: (i, 0)),
         ((S, N_HEADS_B * LANES), BF16, (tm, N_HEADS_B * LANES), lambda i: (i, 0))],
        sem=("arbitrary",))


def weight_grad_rows(a, b, grads, row0, name):
    S, M = a.shape
    N = b.shape[1]
    tmm = 512
    tk = WGRAD_TK
    nk = S // tk
    prior = [] if grads is None else [grads]

    def body(*refs):
        a_ref, b_ref, o_ref, acc_ref = refs[len(prior):]
        k = pl.program_id(1)
        p = _dot(a_ref[...], b_ref[...].astype(BF16), 0, 0)

        @pl.when(k == 0)
        def _():
            acc_ref[...] = p

        @pl.when(k > 0)
        def _():
            acc_ref[...] += p

        @pl.when(k == nk - 1)
        def _():
            o_ref[...] = acc_ref[...].astype(BF16).reshape(tmm // LANES, LANES, N)

    return pl.pallas_call(
        body,
        out_shape=jax.ShapeDtypeStruct((N_DEV, MIX_ROWS, N), BF16),
        grid=(M // tmm, nk),
        in_specs=[pl.BlockSpec(memory_space=pl.ANY)] * len(prior)
        + [pl.BlockSpec((tk, tmm), lambda j, k: (k, j)),
           pl.BlockSpec((tk, N), lambda j, k: (k, 0))],
        out_specs=pl.BlockSpec((tmm // LANES, LANES, N), lambda j, k: (j, row0 // LANES, 0)),
        scratch_shapes=[pltpu.VMEM((tmm, N), F32)],
        input_output_aliases={0: 0} if prior else {},
        name=name,
        compiler_params=pltpu.CompilerParams(dimension_semantics=("parallel", "arbitrary"),
                                             vmem_limit_bytes=VMEM_LIMIT),
    )(*prior, a, b)


def weight_grad_plain(a, b, name):
    S, M = a.shape
    N = b.shape[1]
    tk = WGRAD_TK
    nk = S // tk

    def body(a_ref, b_ref, o_ref, acc_ref):
        k = pl.program_id(0)
        p = _dot(a_ref[...], b_ref[...], 0, 0)

        @pl.when(k == 0)
        def _():
            acc_ref[...] = p

        @pl.when(k > 0)
        def _():
            acc_ref[...] += p

        @pl.when(k == nk - 1)
        def _():
            o_ref[...] = acc_ref[...].astype(BF16)

    return _call(name, body, (nk,),
                 [(a, (tk, M), lambda k: (k, 0)), (b, (tk, N), lambda k: (k, 0))],
                 [((M, N), BF16, (M, N), lambda k: (0, 0))],
                 scratch=[pltpu.VMEM((M, N), F32)], sem=("arbitrary",))[0]


def local_step(x, tgt, p, get_g1_up, get_g1_down, get_gm_in, get_gm_rest, get_g2, emit, start_token):
    S, D = x.shape
    after = lambda t: t[0:1, 0:1]
    buckets = _bucket_tables()
    cos_t, sin_t = _rope_tables(S)
    gains = jnp.concatenate([jnp.tile(p["q_norm"], (1, N_HEADS_B)), jnp.tile(p["k_norm"], (1, N_KV_B)),
                             jnp.ones((1, N_KV_B * LANES), F32)], axis=1)

    n1 = rms_fwd(x, p["ffn1_norm"] + after(start_token), "ffn1_norm")
    bias = bias_build(p["rel_bias"] + after(start_token), buckets)
    g1_up = get_g1_up((n1, bias))
    ab1 = ffn_up(n1, (g1_up, None), "ffn1_up")
    G1 = (g1_up, get_g1_down(ab1))
    x1, hm = ffn_down(ab1, G1, x, p["mix_norm"], "ffn1_down")
    Gw = get_gm_in(hm)
    proj = in_proj(hm, Gw, "in_proj")

    outs, lses = [], []
    for g in range(3):
        o, l = a_fwd(proj, bias, g, "a_fwd_%d" % g)
        outs.append(o)
        lses.append(l)
    o_a, lse_tot = a_combine(outs, lses, "a_combine")

    qkv = qkv_prep(proj, gains, cos_t, sin_t, "qkv_prep")
    k_t = qkv[:, N_HEADS_B * LANES:(N_HEADS_B + N_KV_B) * LANES].T
    o_b, lse_b = flash_fwd(qkv, "flash_fwd")

    Gm = get_gm_rest(o_b)
    w_a = Gm[:, REST_WA:REST_ROWS, :].reshape(N_DEV, GROUP_WIDTH_A, LANES).transpose(1, 0, 2).reshape(GROUP_WIDTH_A, D)
    merged, ya, yb, x2 = merge_fwd(o_a, o_b, w_a, Gm, proj, p["b_gate"], x1, "merge_fwd")

    G2 = get_g2(x2)
    n2, ab2, dx3_b, dab2, dx2, dx2_b, d_ffn2_norm, loss, d_final = ffn_last(
        x2, p["ffn2_norm"], G2, tgt, p["final_norm"], "ffn2")
    gw2 = ffn_bwd_weights(dx3_b, ab2, dab2, n2, "ffn2_bwd")
    t2 = emit("ffn2", gw2)

    dya, dyb, dgate, dbg, do_a, do_b = merge_bwd(dx2_b, w_a, Gm, ya, yb, proj, p["b_gate"] + after(t2),
                                                 "merge_bwd")
    gm_grads = weight_grad_rows(merged, dx2_b, None, MIX_WOUT, "dw_out")
    gm_grads = weight_grad_rows(o_b, dyb, gm_grads, MIX_WB, "dw_branch_b")
    dw_a = weight_grad_plain(o_a, dya, "dw_branch_a")

    dq_r, dk_r, dv_b = flash_bwd(qkv, k_t, do_b, o_b, lse_b, "flash_bwd")
    dq_b, d_q_norm = qk_prep_bwd(dq_r, proj, A_QKV_WIDTH, p["q_norm"], cos_t, sin_t, "q_prep_bwd")
    dk_b, d_k_norm = qk_prep_bwd(dk_r, proj, A_QKV_WIDTH + N_HEADS_B * LANES, p["k_norm"], cos_t, sin_t,
                                 "k_prep_bwd")

    dqkv, dbs = [], []
    for g in range(3):
        dg_, db = a_bwd(proj, bias, do_a, o_a, lse_tot, g, "a_bwd_%d" % g)
        dqkv.append(dg_)
        dbs.append(db)
    d_rel_bias = bias_bwd(jnp.stack(dbs, axis=0).reshape(3, HEADS_PER_GROUP_A, A_TQ, A_WIN), buckets)

    dproj = _dproj_pieces(dqkv, dq_b, jnp.concatenate([dk_b, dv_b], axis=1), dgate)
    gm_grads = in_proj_bwd_dw(dproj, hm, gm_grads, "in_proj_bwd")
    dw_a_sh = dw_a.reshape(GROUP_WIDTH_A, N_DEV, LANES).transpose(1, 0, 2).reshape(N_DEV, MIX_ROWS - MIX_WA, D)
    gm_grads = lax.dynamic_update_slice(gm_grads, dw_a_sh, (0, MIX_WA, 0))
    tm = emit("mix", gm_grads)
    dx1, dx1_b, d_mix_norm = in_proj_bwd_dh(dproj, Gw, x1, p["mix_norm"] + after(tm), dx2, "in_proj_bwd")

    dab1 = ffn_bwd_hidden(dx1_b, ab1, G1, "ffn1_bwd")
    gw1 = ffn_bwd_weights(dx1_b, ab1, dab1, n1, "ffn1_bwd")
    t1 = emit("ffn1", gw1)
    dx0, d_ffn1_norm = ffn_bwd_input(dab1, G1, x, p["ffn1_norm"] + after(t1), dx1, "ffn1_bwd")

    small = dict(ffn1_norm=d_ffn1_norm, mix_norm=d_mix_norm, b_gate=dbg.reshape(1, 2 * D),
                 q_norm=d_q_norm, k_norm=d_k_norm, rel_bias=d_rel_bias, ffn2_norm=d_ffn2_norm,
                 final_norm=d_final)
    return loss, dx0, small


def _pack_small(t, loss_row):
    row6 = jnp.concatenate([t["q_norm"].reshape(1, -1), t["k_norm"].reshape(1, -1), t["rel_bias"].reshape(1, -1)], axis=1)
    return jnp.concatenate([t["ffn1_norm"].reshape(1, -1), t["mix_norm"].reshape(1, -1), t["b_gate"].reshape(2, -1),
                            t["ffn2_norm"].reshape(1, -1), t["final_norm"].reshape(1, -1), row6, loss_row], axis=0)


def _unpack_small(a, shapes):
    return dict(ffn1_norm=a[0:1].reshape(shapes["ffn1_norm"]), mix_norm=a[1:2].reshape(shapes["mix_norm"]),
                b_gate=a[2:4].reshape(shapes["b_gate"]), ffn2_norm=a[4:5].reshape(shapes["ffn2_norm"]),
                final_norm=a[5].reshape(shapes["final_norm"]), q_norm=a[6:7, 0:128].reshape(shapes["q_norm"]),
                k_norm=a[6:7, 128:256].reshape(shapes["k_norm"]), rel_bias=a[6, 256:1024].reshape(shapes["rel_bias"]))


SMALL = ("ffn1_norm", "mix_norm", "b_gate", "q_norm", "k_norm", "rel_bias", "ffn2_norm", "final_norm")
ORDER = ("ffn1_norm", "ffn1_w1", "ffn1_w3", "ffn1_w2", "mix_norm", "w_in", "b_gate", "q_norm", "k_norm", "rel_bias",
         "w_branch_a", "w_branch_b", "w_out", "ffn2_norm", "ffn2_w1", "ffn2_w3", "ffn2_w2", "final_norm")


def kernel(x, ffn1_norm, ffn1_w1, ffn1_w3, ffn1_w2, mix_norm, w_in, b_gate, q_norm, k_norm, rel_bias, w_branch_a, w_branch_b, w_out, ffn2_norm, ffn2_w1, ffn2_w3, ffn2_w2, final_norm, loss_target, m_ffn1_norm, m_ffn1_w1, m_ffn1_w3, m_ffn1_w2, m_mix_norm, m_w_in, m_b_gate, m_q_norm, m_k_norm, m_rel_bias, m_w_branch_a, m_w_branch_b, m_w_out, m_ffn2_norm, m_ffn2_w1, m_ffn2_w3, m_ffn2_w2, m_final_norm, v_ffn1_norm, v_ffn1_w1, v_ffn1_w3, v_ffn1_w2, v_mix_norm, v_w_in, v_b_gate, v_q_norm, v_k_norm, v_rel_bias, v_w_branch_a, v_w_branch_b, v_w_out, v_ffn2_norm, v_ffn2_w1, v_ffn2_w3, v_ffn2_w2, v_final_norm):
    args = dict(locals())
    w = {n: args[n] for n in ORDER}
    m = {n: args["m_" + n] for n in ORDER}
    v = {n: args["v_" + n] for n in ORDER}
    D = x.shape[2]

    blocks = (
        ("ffn1_up", jnp.concatenate([ffn1_w1[0].T, ffn1_w3[0].T], axis=0)),
        ("ffn1_down", ffn1_w2[0]),
        ("mix_in", w_in[0]),
        ("mix_rest", jnp.concatenate([w_branch_b[0], w_out[0], w_branch_a[0].reshape(REST_ROWS - REST_WA, D)], axis=0)),
        ("ffn2", jnp.concatenate([ffn2_w1[0].T, ffn2_w3[0].T, ffn2_w2[0]], axis=0)),
    )
    direct = ("mix_rest", "ffn2")
    started = all_gather_start_all([(b.astype(BF16), tag in direct) for tag, b in blocks], "all_gather_start")
    gathers = {tag: s for (tag, _), s in zip(blocks, started)}
    start_token = started[0][4]

    def gathered(tag):
        def get(after):
            if tag in direct:
                return all_gather_place_own(*_split_wait("all_gather_" + tag + "_wait", gathers[tag], N_DEV - 1, after),
                                            "all_gather_" + tag + "_own")
            return all_gather_finish(*_split_wait("all_gather_" + tag + "_wait", gathers[tag], 4, after),
                                     "all_gather_" + tag + "_finish")
        return get

    core = lax.axis_index("c").astype(jnp.int32).reshape(1)
    chip = (2 * lax.axis_index("x") + lax.axis_index("y")).astype(jnp.int32).reshape(1)
    device = 2 * chip + core
    exchanges = {}

    def emit(tag, gw):
        if tag == "ffn1":
            (theirs,) = reduce_scatter_pair([gw], "reduce_scatter_pair_" + tag)
            part = pair_add(gw, theirs, core, "pair_add_" + tag)
            exchanges[tag] = reduce_scatter_start(part, "reduce_scatter_" + tag + "_start")
        else:
            exchanges[tag] = reduce_scatter_start_direct(gw, "reduce_scatter_" + tag + "_start")
        return exchanges[tag][4]

    small_p = dict(ffn1_norm=ffn1_norm, mix_norm=mix_norm, b_gate=b_gate, q_norm=q_norm, k_norm=k_norm,
                   rel_bias=rel_bias, ffn2_norm=ffn2_norm, final_norm=final_norm.reshape(1, D))
    loss_p, grad_x, small_g = local_step(x[0], loss_target[0], small_p, gathered("ffn1_up"), gathered("ffn1_down"),
                                         gathered("mix_in"), gathered("mix_rest"), gathered("ffn2"), emit, start_token)

    def landed(tag, after):
        n_others, me = (3, chip) if tag == "ffn1" else (N_DEV - 1, device)
        return tuple(_split_wait("reduce_scatter_" + tag + "_wait", exchanges[tag], n_others, after)) + (me,)

    grads, delta, new_m, new_v = {}, {}, {}, {}

    def finish(n, part, land, me, off, blk, transposed=False):
        shp = w[n].shape
        if transposed:
            to2 = lambda a: a.reshape(shp[-2], shp[-1]).T
            back = lambda a: a.T.reshape(shp)
        else:
            to2 = lambda a: a.reshape(shp[-2], shp[-1])
            back = lambda a: a.reshape(shp)
        res = sum_adamw(part, land, me, off, blk, to2(w[n]), to2(m[n]), to2(v[n]), "update_" + n)
        grads[n], delta[n], new_m[n], new_v[n] = [back(a) for a in res]

    last_token = exchanges["ffn1"][4]
    for tag, after in (("ffn2", last_token), ("ffn1", grad_x)):
        group = landed(tag, after)
        finish(tag + "_w1", *group, 0, FFN_SHARD, transposed=True)
        finish(tag + "_w3", *group, FFN_SHARD, FFN_SHARD, transposed=True)
        finish(tag + "_w2", *group, 2 * FFN_SHARD, FFN_SHARD)
        if tag == "ffn2":
            group_m = landed("mix", last_token)
            finish("w_in", *group_m, MIX_WIN, LANES)
            finish("w_branch_b", *group_m, MIX_WB, LANES)
            finish("w_out", *group_m, MIX_WOUT, LANES)
            grads["w_branch_a"] = sum_landed(*group_m, MIX_WA, MIX_ROWS - MIX_WA, MIX_ROWS - MIX_WA,
                                             "w_branch_a_sum").reshape(w_branch_a.shape)
    loss_row = jnp.pad(loss_p, ((0, 0), (0, D - LANES)))
    smalls = small_all_gather(_pack_small(small_g, loss_row))
    small_sum = sum_slots(smalls, 0, N_DEV, N_DEV, "small_sum")
    small_shapes = {n: w[n].shape for n in SMALL}
    grads.update(_unpack_small(small_sum, small_shapes))
    loss = small_sum[7, 0]

    n = "w_branch_a"
    two_d = lambda a: a.reshape(w[n].shape[-2], w[n].shape[-1])
    d_, m_, v_ = adamw(two_d(w[n]), two_d(grads[n]), two_d(m[n]), two_d(v[n]), "adamw_" + n)
    delta[n], new_m[n], new_v[n] = [a.reshape(w[n].shape) for a in (d_, m_, v_)]
    zero_row = jnp.zeros((1, D), F32)
    pack = lambda t: _pack_small({n: t[n] for n in SMALL}, zero_row)
    d_, m_, v_ = adamw(pack(w), small_sum, pack(m), pack(v), "adamw_small")
    for src, dst in ((d_, delta), (m_, new_m), (v_, new_v)):
        dst.update(_unpack_small(src, small_shapes))

    return (loss, grad_x[None], *[grads[n] for n in ORDER], *[delta[n] for n in ORDER],
            *[new_m[n] for n in ORDER], *[new_v[n] for n in ORDER])
```

```python
import math

import jax
import jax.numpy as jnp
from jax import lax
from jax.experimental import pallas as pl
from jax.experimental.pallas import tpu as pltpu

F32 = jnp.float32
BF16 = jnp.bfloat16
MESH = pl.DeviceIdType.MESH

V7X_VMEM_BYTES = 64 * 1024 * 1024
VMEM_LIMIT = V7X_VMEM_BYTES - 8 * 1024 * 1024
LANES = 128

N_DEV = 8
EPS = 1e-6
NEG_INF = -1e30

DILATIONS = (1, 4, 16)
HALF_WINDOW = 64
HEAD_DIM_A = 64
HEADS_PER_GROUP_A = 8
GROUP_WIDTH_A = 512
A_QKV_WIDTH = 4608
A_GROUP_QKV = A_QKV_WIDTH // 3
A_TQ = 128
A_WIN = A_TQ + 2 * HALF_WINDOW
A_UNROLL = 8
A_SCALE = HEAD_DIM_A ** -0.5
WGRAD_TK = 2048
HEAD_DIM_B = 128
N_HEADS_B = 8
N_KV_B = 2
GQA_GROUP_B = 4
GRID_W = 64
ROPE_THETA = 10000.0
B_TQ_FWD = 256
B_TQ_BWD = 512
B_HEADS_PER_STEP = 4
LOG2E = 1.4426950408889634
N_BUCKETS = 32
MAX_DISTANCE = 1024
PB_GATE_A = 1536

ADAM_LR = 0.001
ADAM_B1 = 0.9
ADAM_B2 = 0.999
ADAM_EPS = 1e-08
ADAM_WD = 0.01
ADAM_STEP = 10

FFN_SHARD = 352
MIX_WIN, MIX_WB, MIX_WOUT, MIX_WA = 0, 1024, 1152, 1280
MIX_ROWS = 1344
REST_WB, REST_WOUT, REST_WA, REST_ROWS = 0, 128, 256, 320


def _dot(a, b, ca=1, cb=0):
    return lax.dot_general(a, b, (((ca,), (cb,)), ((), ())), preferred_element_type=F32)


def _call(name, body, grid, ins, outs, scratch=(), sem=None, aliases=None):
    ins = [tuple(i) + (None,) * (4 - len(i)) for i in ins]
    res = pl.pallas_call(
        body,
        out_shape=[jax.ShapeDtypeStruct(s, d) for (s, d, _, _) in outs],
        grid=grid,
        in_specs=[pl.BlockSpec(bs, im, pipeline_mode=pm) for (_, bs, im, pm) in ins],
        out_specs=[pl.BlockSpec(bs, im) for (_, _, bs, im) in outs],
        scratch_shapes=list(scratch),
        name=name,
        input_output_aliases=aliases or {},
        compiler_params=pltpu.CompilerParams(dimension_semantics=sem, vmem_limit_bytes=VMEM_LIMIT),
    )(*[i[0] for i in ins])
    return res


def _sigmoid(x):
    return 0.5 * jnp.tanh(0.5 * x) + 0.5


def _position():
    return lax.axis_index("x"), lax.axis_index("y"), lax.axis_index("c")


def _hbm_specs(n):
    return [pl.BlockSpec(memory_space=pl.ANY) for _ in range(n)]


PAIR_BUFFERS = 4


def reduce_scatter_pair(grads, name):
    n = len(grads)
    C = grads[0].shape[2]
    half = [g.shape[1] // 2 for g in grads]
    chunks = [(i, q, hf) for i in range(n) for q in range(4) for hf in range(2)]
    nb = PAIR_BUFFERS

    def body(*refs):
        ins, theirs = refs[:n], refs[n:2 * n]
        buf, load_sems, send_sems, recv_sems = refs[2 * n:]
        x, y, c = _position()
        sibling = (x, y, 1 - c)

        def load(k):
            i, q, hf = chunks[k]
            r = half[i]
            return pltpu.make_async_copy(ins[i].at[2 * q + (1 - c), pl.ds(hf * r, r), :],
                                         buf.at[k % nb, pl.ds(0, r), :], load_sems.at[k % nb])

        def send(k):
            i, q, hf = chunks[k]
            r = half[i]
            return pltpu.make_async_remote_copy(
                src_ref=buf.at[k % nb, pl.ds(0, r), :], dst_ref=theirs[i].at[q, pl.ds(hf * r, r), :],
                send_sem=send_sems.at[k % nb], recv_sem=recv_sems.at[i],
                device_id=sibling, device_id_type=MESH)

        for k in range(len(chunks) + 1):
            if k < len(chunks):
                if k >= nb:
                    send(k - nb).wait_send()
                load(k).start()
            if k >= 1:
                load(k - 1).wait()
                send(k - 1).start()
        for k in range(max(0, len(chunks) - nb), len(chunks)):
            send(k).wait_send()
        for i in range(n):
            pltpu.make_async_remote_copy(
                src_ref=theirs[i], dst_ref=theirs[i], send_sem=send_sems.at[0], recv_sem=recv_sems.at[i],
                device_id=sibling, device_id_type=MESH).wait_recv()

    return pl.pallas_call(
        body,
        out_shape=[jax.ShapeDtypeStruct((4,) + g.shape[1:], g.dtype) for g in grads],
        in_specs=_hbm_specs(n),
        out_specs=_hbm_specs(n),
        scratch_shapes=[pltpu.VMEM((nb, max(half), C), grads[0].dtype), pltpu.SemaphoreType.DMA((nb,)),
                        pltpu.SemaphoreType.DMA((nb,)), pltpu.SemaphoreType.DMA((n,))],
        name=name,
        compiler_params=pltpu.CompilerParams(vmem_limit_bytes=VMEM_LIMIT),
    )(*grads)


_HBM_SPEC = pl.BlockSpec(memory_space=pltpu.HBM)
_SEM_SPEC = pl.BlockSpec(memory_space=pltpu.SEMAPHORE)
_TOKEN_SPEC = pl.BlockSpec(memory_space=pltpu.VMEM)
_DATAFLOW = pltpu.SideEffectType.DATAFLOW_SIDE_EFFECTING


def _split_start_many(name, exchanges):
    n = len(exchanges)

    def full_body(*refs):
        srcs, lands = refs[:n], refs[n:2 * n]
        sems = refs[2 * n:4 * n]
        token = refs[-1]
        for i, (body, _, _) in enumerate(exchanges):
            body(srcs[i], lands[i], sems[2 * i], sems[2 * i + 1])
        token[...] = jnp.zeros_like(token)

    srcs = [pltpu.with_memory_space_constraint(src, pltpu.HBM) for _, src, _ in exchanges]
    lands = [pltpu.with_memory_space_constraint(lax.empty(shape, src.dtype), pltpu.HBM)
             for _, src, shape in exchanges]
    res = pl.pallas_call(
        full_body, name=name,
        out_shape=(pltpu.SemaphoreType.DMA(()),) * (2 * n)
        + tuple(pltpu.HBM(a.shape, a.dtype) for a in srcs + lands) + (jax.ShapeDtypeStruct((8, LANES), F32),),
        in_specs=(_HBM_SPEC,) * (2 * n),
        out_specs=(_SEM_SPEC,) * (2 * n) + (_HBM_SPEC,) * (2 * n) + (_TOKEN_SPEC,),
        input_output_aliases={i: 2 * n + i for i in range(2 * n)},
        compiler_params=pltpu.CompilerParams(has_side_effects=_DATAFLOW),
    )(*srcs, *lands)
    return [(res[2 * i], res[2 * i + 1], res[2 * n + i], res[3 * n + i], res[-1]) for i in range(n)]


def _split_start(name, body, src, land_shape):
    return _split_start_many(name, [(body, src, land_shape)])[0]


def _split_wait(name, started, n_blocks, after):
    send_sem, recv_sem, src_thru, land_thru, _ = started
    after = after if isinstance(after, tuple) else (after,)

    def body(src_ref, land_ref, send_sem, recv_sem, *rest):
        x, y, c = _position()
        blocks = land_ref.at[pl.ds(0, n_blocks)]
        copy = pltpu.make_async_remote_copy(src_ref=blocks, dst_ref=blocks, send_sem=send_sem, recv_sem=recv_sem,
                                            device_id=(x, y, c), device_id_type=MESH)
        copy.wait_send()
        copy.wait_recv()

    return pl.pallas_call(
        body, name=name,
        out_shape=(pltpu.HBM(src_thru.shape, src_thru.dtype), pltpu.HBM(land_thru.shape, land_thru.dtype)),
        in_specs=(_HBM_SPEC, _HBM_SPEC, _SEM_SPEC, _SEM_SPEC) + (pl.BlockSpec(memory_space=pl.ANY),) * len(after),
        out_specs=(_HBM_SPEC, _HBM_SPEC),
        input_output_aliases={0: 0, 1: 1},
        compiler_params=pltpu.CompilerParams(has_side_effects=_DATAFLOW),
    )(src_thru, land_thru, send_sem, recv_sem, *after)


def all_gather_start_all(blocks, name):
    def starter(direct):
        def body(b_ref, land_ref, send_sem, recv_sem):
            x, y, c = _position()
            peers = _other_devices(x, y, c) if direct else [(x, y, 1 - c), (1 - x, y, c), (x, 1 - y, c),
                                                            (1 - x, 1 - y, c)]
            for peer in peers:
                pltpu.make_async_remote_copy(src_ref=b_ref, dst_ref=land_ref.at[4 * x + 2 * y + c],
                                             send_sem=send_sem, recv_sem=recv_sem,
                                             device_id=peer, device_id_type=MESH).start()
        return body

    return _split_start_many(name, [(starter(direct), block, (N_DEV,) + block.shape) for block, direct in blocks])


def all_gather_finish(block, land, name):
    R, C = block.shape

    def body(b_ref, land_in, land_ref, stage, load_sems, send_sems, recv_sems, own_sem):
        x, y, c = _position()
        sibling = (x, y, 1 - c)
        chips = [(1 - x, y), (x, 1 - y), (1 - x, 1 - y)]
        own_in = pltpu.make_async_copy(b_ref, stage.at[3], load_sems.at[3])
        own_in.start()
        loads = [pltpu.make_async_copy(land_in.at[4 * px + 2 * py + c], stage.at[j], load_sems.at[j])
                 for j, (px, py) in enumerate(chips)]
        for ld in loads:
            ld.start()
        sends = []
        for j, (px, py) in enumerate(chips):
            loads[j].wait()
            dst = land_ref.at[4 * px + 2 * py + c]
            cp = pltpu.make_async_remote_copy(src_ref=stage.at[j], dst_ref=dst, send_sem=send_sems.at[j],
                                              recv_sem=recv_sems.at[j], device_id=sibling, device_id_type=MESH)
            cp.start()
            sends.append(cp)
        own_in.wait()
        own_out = pltpu.make_async_copy(stage.at[3], land_ref.at[4 * x + 2 * y + c], own_sem)
        own_out.start()
        for j, (px, py) in enumerate(chips):
            dst = land_ref.at[4 * px + 2 * py + (1 - c)]
            pltpu.make_async_remote_copy(src_ref=stage.at[j], dst_ref=dst, send_sem=send_sems.at[j],
                                         recv_sem=recv_sems.at[j], device_id=sibling,
                                         device_id_type=MESH).wait_recv()
        for cp in sends:
            cp.wait_send()
        own_out.wait()

    return pl.pallas_call(
        body,
        out_shape=jax.ShapeDtypeStruct(land.shape, land.dtype),
        in_specs=_hbm_specs(2),
        out_specs=pl.BlockSpec(memory_space=pl.ANY),
        scratch_shapes=[pltpu.VMEM((4, R, C), block.dtype), pltpu.SemaphoreType.DMA((4,)),
                        pltpu.SemaphoreType.DMA((3,)), pltpu.SemaphoreType.DMA((3,)), pltpu.SemaphoreType.DMA],
        input_output_aliases={1: 0},
        name=name,
        compiler_params=pltpu.CompilerParams(vmem_limit_bytes=VMEM_LIMIT),
    )(block, land)


def reduce_scatter_start(parts, name):
    def body(p_ref, land_ref, send_sem, recv_sem):
        x, y, c = _position()
        for px, py in [(1 - x, y), (x, 1 - y), (1 - x, 1 - y)]:
            pltpu.make_async_remote_copy(src_ref=p_ref.at[2 * px + py], dst_ref=land_ref.at[2 * x + y],
                                         send_sem=send_sem, recv_sem=recv_sem,
                                         device_id=(px, py, c), device_id_type=MESH).start()

    return _split_start(name, body, parts, parts.shape)


def _other_devices(x, y, c):
    return [(1 - x if k & 4 else x, 1 - y if k & 2 else y, 1 - c if k & 1 else c) for k in range(1, N_DEV)]


def all_gather_place_own(block, land, name):
    R, C = block.shape

    def body(b_ref, land_in, land_ref, stage, sems):
        x, y, c = _position()
        load = pltpu.make_async_copy(b_ref, stage, sems.at[0])
        load.start()
        load.wait()
        store = pltpu.make_async_copy(stage, land_ref.at[4 * x + 2 * y + c], sems.at[1])
        store.start()
        store.wait()

    return pl.pallas_call(
        body,
        out_shape=jax.ShapeDtypeStruct(land.shape, land.dtype),
        in_specs=_hbm_specs(2),
        out_specs=pl.BlockSpec(memory_space=pl.ANY),
        scratch_shapes=[pltpu.VMEM((R, C), block.dtype), pltpu.SemaphoreType.DMA((2,))],
        input_output_aliases={1: 0},
        name=name,
    )(block, land)


def reduce_scatter_start_direct(grads, name):
    def body(g_ref, land_ref, send_sem, recv_sem):
        x, y, c = _position()
        for px, py, pc in _other_devices(x, y, c):
            pltpu.make_async_remote_copy(src_ref=g_ref.at[4 * px + 2 * py + pc],
                                         dst_ref=land_ref.at[4 * x + 2 * y + c],
                                         send_sem=send_sem, recv_sem=recv_sem,
                                         device_id=(px, py, pc), device_id_type=MESH).start()

    return _split_start(name, body, grads, grads.shape)


def small_all_gather(small, after):
    def body(small_ref, after_ref, smalls, s_send, s_recv, s_local):
        x, y, c = _position()
        me = 4 * x + 2 * y + c
        lc = pltpu.make_async_copy(small_ref, smalls.at[me], s_local)
        lc.start()
        remote = []
        k = 0
        for dx in (0, 1):
            for dy in (0, 1):
                for dc in (0, 1):
                    if dx + dy + dc == 0:
                        continue
                    peer = (1 - x if dx else x, 1 - y if dy else y, 1 - c if dc else c)
                    rc = pltpu.make_async_remote_copy(
                        src_ref=small_ref, dst_ref=smalls.at[me],
                        send_sem=s_send.at[k], recv_sem=s_recv.at[k],
                        device_id=peer, device_id_type=MESH)
                    rc.start()
                    remote.append(rc)
                    k += 1
        for rc in remote:
            rc.wait()
        lc.wait()

    return pl.pallas_call(
        body,
        out_shape=jax.ShapeDtypeStruct((N_DEV,) + small.shape, small.dtype),
        in_specs=_hbm_specs(2),
        out_specs=pl.BlockSpec(memory_space=pl.ANY),
        scratch_shapes=[pltpu.SemaphoreType.DMA((7,)), pltpu.SemaphoreType.DMA((7,)), pltpu.SemaphoreType.DMA],
        name="small_all_gather",
    )(small, after)


def pair_add(grads, theirs, core, name):
    _, R, C = theirs.shape
    tr = R // 2

    def body(c_ref, a_ref, b_ref, o_ref):
        o_ref[...] = (a_ref[...].astype(F32) + b_ref[...].astype(F32)).astype(BF16)

    return pl.pallas_call(
        body,
        out_shape=jax.ShapeDtypeStruct(theirs.shape, BF16),
        grid_spec=pltpu.PrefetchScalarGridSpec(
            num_scalar_prefetch=1, grid=(4, R // tr),
            in_specs=[pl.BlockSpec((None, tr, C), lambda q, i, c: (2 * q + c[0], i, 0)),
                      pl.BlockSpec((None, tr, C), lambda q, i, c: (q, i, 0))],
            out_specs=pl.BlockSpec((None, tr, C), lambda q, i, c: (q, i, 0))),
        name=name,
        compiler_params=pltpu.CompilerParams(dimension_semantics=("parallel", "parallel"),
                                             vmem_limit_bytes=VMEM_LIMIT),
    )(core, grads, theirs)


def sum_slots(recv, off, rows, blk, name):
    nq, _, C = recv.shape
    ob = off // blk

    def body(r_ref, o_ref):
        acc = r_ref[0].astype(F32)
        for q in range(1, nq):
            acc = acc + r_ref[q].astype(F32)
        o_ref[...] = acc

    return _call(name, body, (rows // blk,),
                 [(recv, (nq, blk, C), lambda i: (0, ob + i, 0))],
                 [((rows, C), F32, (blk, C), lambda i: (i, 0))], sem=("parallel",))[0]


def _sum_terms(refs):
    acc = refs[0][...].astype(F32)
    for r in refs[1:]:
        acc = acc + r[...].astype(F32)
    return acc


def sum_landed(own, land, me, off, rows, blk, name):
    n, _, C = land.shape
    ob = off // blk

    def body(c_ref, *refs):
        refs[n][...] = _sum_terms(refs[:n])

    def entry(flip):
        return pl.BlockSpec((None, blk, C), lambda i, c: (c[0] ^ flip, ob + i, 0))

    return pl.pallas_call(
        body,
        out_shape=jax.ShapeDtypeStruct((rows, C), F32),
        grid_spec=pltpu.PrefetchScalarGridSpec(
            num_scalar_prefetch=1, grid=(rows // blk,),
            in_specs=[entry(k) for k in range(n)],
            out_specs=pl.BlockSpec((blk, C), lambda i, c: (i, 0))),
        name=name,
        compiler_params=pltpu.CompilerParams(dimension_semantics=("parallel",), vmem_limit_bytes=VMEM_LIMIT),
    )(me, own, *([land] * (n - 1)))


def _adamw_update(wv, gv, mv, vv):
    nm = ADAM_B1 * mv + (1.0 - ADAM_B1) * gv
    nv = ADAM_B2 * vv + (1.0 - ADAM_B2) * (gv * gv)
    c1 = 1.0 / (1.0 - ADAM_B1 ** ADAM_STEP)
    c2 = 1.0 / (1.0 - ADAM_B2 ** ADAM_STEP)
    return -ADAM_LR * ((nm * c1) / (jnp.sqrt(nv * c2) + ADAM_EPS) + ADAM_WD * wv), nm, nv


def sum_adamw(own, land, me, off, blk, w, m, v, name):
    rows, C = w.shape
    n = land.shape[0]
    ob = off // blk

    def body(c_ref, *refs):
        w_ref, m_ref, v_ref, g_out, d_out, m_out, v_out = refs[n:]
        gv = _sum_terms(refs[:n])
        g_out[...] = gv
        d_out[...], m_out[...], v_out[...] = _adamw_update(w_ref[...], gv, m_ref[...], v_ref[...])

    def entry(flip):
        return pl.BlockSpec((None, blk, C), lambda i, c: (c[0] ^ flip, ob + i, 0))

    plain = pl.BlockSpec((blk, C), lambda i, c: (i, 0))
    return pl.pallas_call(
        body,
        out_shape=[jax.ShapeDtypeStruct((rows, C), F32)] * 4,
        grid_spec=pltpu.PrefetchScalarGridSpec(
            num_scalar_prefetch=1, grid=(rows // blk,),
            in_specs=[entry(k) for k in range(n)] + [plain, plain, plain],
            out_specs=[plain] * 4),
        name=name,
        compiler_params=pltpu.CompilerParams(dimension_semantics=("parallel",), vmem_limit_bytes=VMEM_LIMIT),
    )(me, own, *([land] * (n - 1)), w, m, v)


def adamw(w, g, m, v, name):
    R, C = w.shape
    tr = R
    for cand in (256, 128, 64, 32, 16, 8):
        if R % cand == 0 and R > cand:
            tr = cand
            break

    def body(w_ref, g_ref, m_ref, v_ref, d_ref, nm_ref, nv_ref):
        d_ref[...], nm_ref[...], nv_ref[...] = _adamw_update(w_ref[...], g_ref[...], m_ref[...], v_ref[...])

    spec = ((tr, C), lambda i: (i, 0))
    out = ((R, C), F32) + spec
    return _call(name, body, (R // tr,), [(w,) + spec, (g,) + spec, (m,) + spec, (v,) + spec],
                 [out, out, out], sem=("parallel",))


def _rms_tile(xv, gv):
    r = lax.rsqrt(jnp.mean(xv * xv, axis=-1, keepdims=True) + EPS)
    return (xv * r * gv).astype(BF16)


def rms_fwd(x, g, name):
    S, D = x.shape
    tr = 512

    def body(x_ref, g_ref, o_ref):
        o_ref[...] = _rms_tile(x_ref[...], g_ref[...])

    return _call(name, body, (S // tr,),
                 [(x, (tr, D), lambda i: (i, 0)), (g, (1, D), lambda i: (0, 0))],
                 [((S, D), BF16, (tr, D), lambda i: (i, 0))], sem=("parallel",))[0]


def _rms_bwd_tile(dn, xv, gv):
    r = lax.rsqrt(jnp.mean(xv * xv, axis=-1, keepdims=True) + EPS)
    xh = xv * r
    dxh = dn * gv
    dx = r * (dxh - xh * jnp.mean(dxh * xh, axis=-1, keepdims=True))
    return dx, dn * xh


def _final_loss_tile(xv, tv, gv):
    D = xv.shape[1]
    r = lax.rsqrt(jnp.mean(xv * xv, axis=-1, keepdims=True) + EPS)
    xh = xv * r
    e = xh * gv - tv
    part = 0.5 * jnp.sum(jnp.sum(e * e, axis=-1, keepdims=True) * (1.0 / D), axis=0, keepdims=True)
    dy = e * (1.0 / D)
    dxh = dy * gv
    dx = r * (dxh - xh * jnp.mean(dxh * xh, axis=-1, keepdims=True))
    return part, dx, jnp.sum(dy * xh, axis=0, keepdims=True)


FFN_TF = 4 * FFN_SHARD


def _ffn_pick(G, which):
    if isinstance(G, tuple):
        return (G[0], which) if which < 2 else (G[1], 0)
    return G, which


def _ffn_w_spec(G, which, imap):
    arr, blk = _ffn_pick(G, which)
    return (arr, (4, FFN_SHARD, arr.shape[2]), lambda *idx: (imap(*idx), blk, 0))


def _ffn_whole_w_spec(G, which):
    arr, blk = _ffn_pick(G, which)
    return (arr, (N_DEV, FFN_SHARD, arr.shape[2]), lambda *idx: (0, blk, 0), pl.Buffered(1))


def _ffn_hidden(a, b):
    av, bv = a.astype(F32), b.astype(F32)
    return (av * _sigmoid(av) * bv).astype(BF16)


def ffn_up(n, G, name):
    S, D = n.shape
    F = N_DEV * FFN_SHARD
    tm = 256

    def body(n_ref, w1_ref, w3_ref, abh_ref):
        nv = n_ref[...]
        a = _dot(nv, w1_ref[...].reshape(F, D), 1, 1).astype(BF16)
        b = _dot(nv, w3_ref[...].reshape(F, D), 1, 1).astype(BF16)
        abh_ref[0] = a
        abh_ref[1] = b
        abh_ref[2] = _ffn_hidden(a, b)

    return _call(name, body, (S // tm,),
                 [(n, (tm, D), lambda i: (i, 0)),
                  _ffn_whole_w_spec(G, 0), _ffn_whole_w_spec(G, 1)],
                 [((3, S, F), BF16, (3, tm, F), lambda i: (0, i, 0))],
                 sem=("parallel",))[0]


def ffn_down(abh, G, x, g_next, name):
    _, S, F = abh.shape
    D = x.shape[1]
    tm = 512

    def body(h_ref, w2_ref, x_ref, g_ref, o_ref, n_ref):
        xo = x_ref[...] + 0.5 * _dot(h_ref[...], w2_ref[...].reshape(F, D))
        o_ref[...] = xo
        n_ref[...] = _rms_tile(xo, g_ref[...])

    tile = ((tm, D), lambda i: (i, 0))
    return _call(name, body, (S // tm,),
                 [(abh, (None, tm, F), lambda i: (2, i, 0)), _ffn_whole_w_spec(G, 2),
                  (x,) + tile, (g_next, (1, D), lambda i: (0, 0))],
                 [((S, D), F32) + tile, ((S, D), BF16) + tile], sem=("parallel",))


def ffn_last(x, g, G, tgt, g_final, name):
    S, D = x.shape
    F = N_DEV * FFN_SHARD
    tm = 256

    def body(x_ref, g_ref, w1_ref, w3_ref, w2_ref, t_ref, gf_ref,
             n_ref, abh_ref, dxo_ref, dab_ref, dx_ref, dxb_ref, dg_ref, l_ref, dgf_ref):
        i = pl.program_id(0)
        xv, gv = x_ref[...], g_ref[...]
        chunks = [(slice(4 * f, 4 * f + 4), slice(f * FFN_TF, (f + 1) * FFN_TF)) for f in range(F // FFN_TF)]
        weight = lambda w_ref, slots: w_ref[slots].reshape(FFN_TF, D)
        nv = _rms_tile(xv, gv)
        n_ref[...] = nv
        y = None
        for slots, cols in chunks:
            a = _dot(nv, weight(w1_ref, slots), 1, 1).astype(BF16)
            b = _dot(nv, weight(w3_ref, slots), 1, 1).astype(BF16)
            h = _ffn_hidden(a, b)
            abh_ref[0, :, cols] = a
            abh_ref[1, :, cols] = b
            abh_ref[2, :, cols] = h
            t = _dot(h, weight(w2_ref, slots))
            y = t if y is None else y + t
        part, dxo, dgfp = _final_loss_tile(xv + 0.5 * y, t_ref[...], gf_ref[...])
        dxo_b = dxo.astype(BF16)
        dxo_ref[...] = dxo_b
        dn = None
        for slots, cols in chunks:
            dh = 0.5 * _dot(dxo_b, weight(w2_ref, slots), 1, 1)
            da, db = _ffn_hidden_grads(dh, abh_ref[0, :, cols].astype(F32), abh_ref[1, :, cols].astype(F32))
            da, db = da.astype(BF16), db.astype(BF16)
            dab_ref[0, :, cols] = da
            dab_ref[1, :, cols] = db
            t = _dot(da, weight(w1_ref, slots)) + _dot(db, weight(w3_ref, slots))
            dn = t if dn is None else dn + t
        dx, dgt = _rms_bwd_tile(dn, xv, gv)
        dx = dxo + dx
        dx_ref[...] = dx
        dxb_ref[...] = dx.astype(BF16)
        dgp = jnp.sum(dgt, axis=0, keepdims=True)

        @pl.when(i == 0)
        def _():
            dg_ref[...] = dgp
            l_ref[...] = jnp.broadcast_to(part, l_ref.shape)
            dgf_ref[...] = dgfp

        @pl.when(i > 0)
        def _():
            dg_ref[...] += dgp
            l_ref[...] += jnp.broadcast_to(part, l_ref.shape)
            dgf_ref[...] += dgfp

    tile = ((tm, D), lambda i: (i, 0))
    gain = ((1, D), lambda i: (0, 0))
    return _call(name, body, (S // tm,),
                 [(x,) + tile, (g,) + gain,
                  _ffn_whole_w_spec(G, 0), _ffn_whole_w_spec(G, 1), _ffn_whole_w_spec(G, 2),
                  (tgt,) + tile, (g_final,) + gain],
                 [((S, D), BF16) + tile, ((3, S, F), BF16, (3, tm, F), lambda i: (0, i, 0)),
                  ((S, D), BF16) + tile, ((2, S, F), BF16, (2, tm, F), lambda i: (0, i, 0)),
                  ((S, D), F32) + tile, ((S, D), BF16) + tile, ((1, D), F32) + gain,
                  ((1, LANES), F32, (1, LANES), lambda i: (0, 0)), ((1, D), F32) + gain],
                 sem=("arbitrary",))


def _ffn_hidden_grads(dh, av, bv):
    sig = _sigmoid(av)
    return dh * bv * (sig * (1.0 + av * (1.0 - sig))), dh * (av * sig)


def ffn_bwd_hidden(dxo, abh, G, name):
    _, S, F = abh.shape
    D = dxo.shape[1]
    tm = 256

    def body(d_ref, w2_ref, ab_ref, o_ref):
        dh = 0.5 * _dot(d_ref[...].astype(BF16), w2_ref[...].reshape(F, D), 1, 1)
        da, db = _ffn_hidden_grads(dh, ab_ref[0].astype(F32), ab_ref[1].astype(F32))
        o_ref[0] = da.astype(BF16)
        o_ref[1] = db.astype(BF16)

    return _call(name + "_down_bwd", body, (S // tm,),
                 [(dxo, (tm, D), lambda i: (i, 0)), _ffn_whole_w_spec(G, 2),
                  (abh, (2, tm, F), lambda i: (0, i, 0))],
                 [((2, S, F), BF16, (2, tm, F), lambda i: (0, i, 0))],
                 sem=("parallel",))[0]


def ffn_bwd_weights(dxo, abh, dab, n, name):
    _, S, F = abh.shape
    D = dxo.shape[1]
    nf = F // FFN_TF
    tk = WGRAD_TK
    nk = S // tk
    gshape = (N_DEV, 3 * FFN_SHARD, D)

    def dw2_body(h_ref, d_ref, o_ref, acc_ref):
        k = pl.program_id(1)
        p = _dot(h_ref[...], d_ref[...].astype(BF16), 0, 0)

        @pl.when(k == 0)
        def _():
            acc_ref[...] = p

        @pl.when(k > 0)
        def _():
            acc_ref[...] += p

        @pl.when(k == nk - 1)
        def _():
            o_ref[...] = (0.5 * acc_ref[...]).astype(BF16).reshape(4, FFN_SHARD, D)

    gw = _call(name + "_dw2", dw2_body, (nf, nk),
               [(abh, (None, tk, FFN_TF), lambda j, k: (2, k, j)), (dxo, (tk, D), lambda j, k: (k, 0))],
               [(gshape, BF16, (4, FFN_SHARD, D), lambda j, k: (j, 2, 0))],
               scratch=[pltpu.VMEM((FFN_TF, D), F32)], sem=("parallel", "arbitrary"))[0]

    def dw13_body(gw_ref, dab_ref, n_ref, o_ref):
        o_ref[...] = _dot(dab_ref[...], n_ref[...], 0, 0).astype(BF16).reshape(4, FFN_SHARD, D)

    gw = pl.pallas_call(
        dw13_body,
        out_shape=jax.ShapeDtypeStruct(gshape, BF16),
        grid=(2, nf),
        in_specs=[pl.BlockSpec(memory_space=pl.ANY),
                  pl.BlockSpec((None, S, FFN_TF), lambda w, j: (w, 0, j)),
                  pl.BlockSpec((S, D), lambda w, j: (0, 0))],
        out_specs=pl.BlockSpec((4, FFN_SHARD, D), lambda w, j: (j, w, 0)),
        input_output_aliases={0: 0},
        name=name + "_dw13",
        compiler_params=pltpu.CompilerParams(dimension_semantics=("parallel", "parallel"),
                                             vmem_limit_bytes=VMEM_LIMIT),
    )(gw, dab, n)
    return gw


def ffn_bwd_input(dab, G, x_in, g, dxo, name):
    _, S, F = dab.shape
    D = x_in.shape[1]
    tm = 256

    def dn_body(dab_ref, w1_ref, w3_ref, x_ref, d_ref, g_ref, dx_ref, dg_ref):
        i = pl.program_id(0)
        dn = _dot(dab_ref[0], w1_ref[...].reshape(F, D)) + _dot(dab_ref[1], w3_ref[...].reshape(F, D))
        dx, dgt = _rms_bwd_tile(dn, x_ref[...], g_ref[...])
        dx_ref[...] = d_ref[...] + dx
        dgp = jnp.sum(dgt, axis=0, keepdims=True)

        @pl.when(i == 0)
        def _():
            dg_ref[...] = dgp

        @pl.when(i > 0)
        def _():
            dg_ref[...] += dgp

    tile = ((tm, D), lambda i: (i, 0))
    return _call(name + "_dn", dn_body, (S // tm,),
                 [(dab, (2, tm, F), lambda i: (0, i, 0)),
                  _ffn_whole_w_spec(G, 0), _ffn_whole_w_spec(G, 1),
                  (x_in,) + tile, (dxo,) + tile, (g, (1, D), lambda i: (0, 0))],
                 [((S, D), F32) + tile, ((1, D), F32, (1, D), lambda i: (0, 0))],
                 sem=("arbitrary",))


PROJ_TN = 512
DH_SHARDS_PER_STEP = 4


def in_proj(h, Gm, name):
    S, D = h.shape
    n_tiles = N_DEV * Gm.shape[2] // PROJ_TN

    def body(h_ref, w_ref, o_ref):
        o_ref[...] = _dot(h_ref[...], w_ref[...]).astype(BF16)

    return _call(name, body, (n_tiles,),
                 [(h, (S, D), lambda j: (0, 0)),
                  (Gm, (None, D, PROJ_TN), lambda j: (j // 2, 0, j % 2))],
                 [((S, n_tiles * PROJ_TN), BF16, (S, PROJ_TN), lambda j: (0, j))],
                 sem=("parallel",))[0]


def _dproj_pieces(dqkv, dq_b, dkv_b, dgate):
    pieces = [(dqkv[g], [(3 * which + g, (which, 0)) for which in range(3)]) for g in range(3)]
    pieces.append((dq_b, [(9, (None, 0)), (10, (None, 1))]))
    pieces.append((dkv_b, [(11, (None, 0))]))
    pieces.append((dgate, [(12 + 2 * a + b, (a, b)) for a in range(2) for b in range(2)]))
    return pieces


def in_proj_bwd_dw(pieces, h, gm_grads, name):
    S, D = h.shape

    for n_piece, (arr, tiles) in enumerate(pieces):
        w_tile = [t for t, _ in tiles]
        lead = [ix[0] for _, ix in tiles]
        colb = [ix[1] for _, ix in tiles]

        def pick(table, j):
            out = table[-1]
            for k in range(len(table) - 2, -1, -1):
                out = jnp.where(j == k, table[k], out)
            return out

        def dw_body(gm_ref, h_ref, d_ref, o_ref):
            o_ref[...] = _dot(h_ref[...], d_ref[...], 0, 0).astype(BF16)

        if arr.ndim == 3:
            d_spec = pl.BlockSpec((None, S, PROJ_TN), lambda j, lead=lead, colb=colb: (pick(lead, j), 0, pick(colb, j)))
        else:
            d_spec = pl.BlockSpec((S, PROJ_TN), lambda j, colb=colb: (0, pick(colb, j)))
        gm_grads = pl.pallas_call(
            dw_body,
            out_shape=jax.ShapeDtypeStruct(gm_grads.shape, BF16),
            grid=(len(tiles),),
            in_specs=[pl.BlockSpec(memory_space=pl.ANY), pl.BlockSpec((S, D), lambda j: (0, 0)), d_spec],
            out_specs=pl.BlockSpec((None, D, PROJ_TN),
                                   lambda j, w_tile=w_tile: (pick(w_tile, j) // 2, 0, pick(w_tile, j) % 2)),
            input_output_aliases={0: 0},
            name="%s_dw%d" % (name, n_piece),
            compiler_params=pltpu.CompilerParams(dimension_semantics=("parallel",), vmem_limit_bytes=VMEM_LIMIT),
        )(gm_grads, h, arr)
    return gm_grads


def in_proj_bwd_dh(pieces, Gm, x_in, g, dres, name):
    S, D = x_in.shape
    tm = 256
    C = Gm.shape[2]
    n_sh = N_DEV
    n_p = len(pieces)

    def dh_body(*refs):
        d_refs = refs[:n_p]
        w_ref, x_ref, r_ref, g_ref, dx_ref, dxb_ref, dg_ref = refs[n_p:]
        i = pl.program_id(0)
        p = None
        for d_ref, (arr, tiles) in zip(d_refs, pieces):
            for t, (lead, colb) in tiles:
                cols = slice(colb * PROJ_TN, (colb + 1) * PROJ_TN)
                d = d_ref[:, cols] if lead is None else d_ref[lead, :, cols]
                wcol = (t % 2) * PROJ_TN
                term = _dot(d, w_ref[t // 2, :, wcol:wcol + PROJ_TN], 1, 1)
                p = term if p is None else p + term
        dx, dgt = _rms_bwd_tile(p, x_ref[...], g_ref[...])
        dx = r_ref[...] + dx
        dx_ref[...] = dx
        dxb_ref[...] = dx.astype(BF16)
        dgp = jnp.sum(dgt, axis=0, keepdims=True)

        @pl.when(i == 0)
        def _():
            dg_ref[...] = dgp

        @pl.when(i > 0)
        def _():
            dg_ref[...] += dgp

    tile = ((tm, D), lambda i: (i, 0))

    def rows_of(arr):
        if arr.ndim == 3:
            return (arr, (arr.shape[0], tm, arr.shape[2]), lambda i: (0, i, 0))
        return (arr, (tm, arr.shape[1]), lambda i: (i, 0))

    return _call(name + "_dh", dh_body, (S // tm,),
                 [rows_of(arr) for arr, _ in pieces]
                 + [(Gm, (n_sh, D, C), lambda i: (0, 0, 0), pl.Buffered(1)),
                    (x_in,) + tile, (dres,) + tile, (g, (1, D), lambda i: (0, 0))],
                 [((S, D), F32) + tile, ((S, D), BF16) + tile, ((1, D), F32, (1, D), lambda i: (0, 0))],
                 sem=("arbitrary",))


def _t5_bucket(rel):
    n = N_BUCKETS // 2
    max_exact = n // 2
    ret = jnp.where(rel > 0, n, 0)
    a = jnp.abs(rel)
    af = jnp.maximum(a, 1).astype(F32)
    large = max_exact + (jnp.log(af / max_exact) / math.log(MAX_DISTANCE / max_exact)
                         * (n - max_exact)).astype(jnp.int32)
    large = jnp.minimum(large, n - 1)
    return ret + jnp.where(a < max_exact, a, large)


def _bucket_tables():
    qi = jnp.arange(A_TQ, dtype=jnp.int32)[:, None]
    kj = jnp.arange(A_WIN, dtype=jnp.int32)[None, :]
    rel = kj - HALF_WINDOW - qi
    return jnp.stack([_t5_bucket(rel * d) for d in DILATIONS], axis=0)


def bias_build(rel_bias, buckets):
    def body(tab_ref, bk_ref, o_ref):
        col = pl.program_id(0) * HEADS_PER_GROUP_A + pl.program_id(1)
        bk = bk_ref[...]
        acc = jnp.zeros(bk.shape, F32)
        for b in range(N_BUCKETS):
            acc = jnp.where(bk == b, tab_ref[b, col], acc)
        qi = lax.broadcasted_iota(jnp.int32, bk.shape, 0)
        kj = lax.broadcasted_iota(jnp.int32, bk.shape, 1)
        band = jnp.where(jnp.abs(kj - HALF_WINDOW - qi) <= HALF_WINDOW, acc, NEG_INF)
        o_ref[0] = jnp.where(kj >= HALF_WINDOW, band, NEG_INF)
        o_ref[1] = band
        o_ref[2] = jnp.where(kj < A_TQ + HALF_WINDOW, band, NEG_INF)

    out = pl.pallas_call(
        body,
        out_shape=jax.ShapeDtypeStruct((3, HEADS_PER_GROUP_A // 2, 3, 2, A_TQ, A_WIN), F32),
        grid=(3, HEADS_PER_GROUP_A),
        in_specs=[pl.BlockSpec(memory_space=pltpu.SMEM),
                  pl.BlockSpec((None, A_TQ, A_WIN), lambda g, h: (g, 0, 0))],
        out_specs=pl.BlockSpec((None, None, 3, None, A_TQ, A_WIN), lambda g, h: (g, h // 2, 0, h % 2, 0, 0)),
        name="a_bias_build",
        compiler_params=pltpu.CompilerParams(dimension_semantics=("parallel", "parallel")),
    )(rel_bias, buckets)
    return out.reshape(3, HEADS_PER_GROUP_A // 2, 3, 2 * A_TQ, A_WIN)


def bias_bwd(dbias, buckets):
    def body(d_ref, bk_ref, o_ref):
        bk = bk_ref[...]
        dv = d_ref[...]
        for b in range(N_BUCKETS):
            part = jnp.sum(jnp.where(bk == b, dv, 0.0), axis=1, keepdims=True)
            o_ref[b:b + 1, :] = jnp.broadcast_to(jnp.sum(part, axis=0, keepdims=True), (1, LANES))

    out = pl.pallas_call(
        body,
        out_shape=jax.ShapeDtypeStruct((3, HEADS_PER_GROUP_A, N_BUCKETS, LANES), F32),
        grid=(3, HEADS_PER_GROUP_A),
        in_specs=[pl.BlockSpec((None, None, A_TQ, A_WIN), lambda g, h: (g, h, 0, 0)),
                  pl.BlockSpec((None, A_TQ, A_WIN), lambda g, h: (g, 0, 0))],
        out_specs=pl.BlockSpec((None, None, N_BUCKETS, LANES), lambda g, h: (g, h, 0, 0)),
        name="a_bias_bwd",
        compiler_params=pltpu.CompilerParams(dimension_semantics=("parallel", "parallel")),
    )(dbias, buckets)
    return out[:, :, :, 0].transpose(2, 0, 1).reshape(N_BUCKETS, 3 * HEADS_PER_GROUP_A)


def _a_fill_padded(pad_ref, src_ref, n, pad):
    zeros = jnp.zeros((pad, LANES), pad_ref.dtype)
    pad_ref[0:pad, :] = zeros
    pad_ref[pad + n:2 * pad + n, :] = zeros
    pad_ref[pad:pad + n, :] = src_ref[...].astype(pad_ref.dtype)


def _a_stack_heads(x, lane):
    zero = jnp.zeros_like(x)
    return jnp.concatenate([jnp.where(lane < HEAD_DIM_A, x, zero), jnp.where(lane >= HEAD_DIM_A, x, zero)], axis=0)


def _a_bias_variant(qb, nqb):
    return jnp.where(qb == 0, 0, jnp.where(qb == nqb - 1, 2, 1))


def _a_slab_specs(proj, g):
    S = proj.shape[0]
    per = GROUP_WIDTH_A // LANES
    return [(proj, (S, LANES), lambda hp, w=w: (0, per * (3 * w + g) + hp)) for w in range(3)]


def a_fwd(proj, bias, g, name):
    S = proj.shape[0]
    d = DILATIONS[g]
    L = S // d
    nqb = L // A_TQ
    pad = HALF_WINDOW * d

    def body(q_ref, k_ref, v_ref, b_ref, o_ref, l_ref, qf, kpad, vpad):
        qf[...] = q_ref[...].astype(F32) * A_SCALE
        _a_fill_padded(kpad, k_ref, S, pad)
        _a_fill_padded(vpad, v_ref, S, pad)
        lane = lax.broadcasted_iota(jnp.int32, (A_TQ, LANES), 1)

        def block(t, carry):
            qb, r = t // d, t % d
            start = qb * (A_TQ * d) + r
            kw = kpad[pl.ds(start, A_WIN, stride=d), :].astype(BF16)
            vw = vpad[pl.ds(start, A_WIN, stride=d), :].astype(BF16)
            q = qf[pl.ds(start, A_TQ, stride=d), :].astype(BF16)
            q2 = _a_stack_heads(q, lane)
            s = _dot(q2, kw, 1, 1) + b_ref[_a_bias_variant(qb, nqb)]
            m = jnp.max(s, axis=-1, keepdims=True)
            e = jnp.exp(s - m)
            l = jnp.sum(e, axis=-1, keepdims=True)
            o2 = _dot(e.astype(BF16), vw) / l
            lse2 = m + jnp.log(l)
            o_ref[pl.ds(start, A_TQ, stride=d), :] = jnp.where(lane < HEAD_DIM_A, o2[0:A_TQ], o2[A_TQ:])
            l_ref[pl.ds(start, A_TQ, stride=d), :] = jnp.where(lane < HEAD_DIM_A, lse2[0:A_TQ], lse2[A_TQ:])
            return carry

        lax.fori_loop(0, nqb * d, block, 0, unroll=A_UNROLL)

    out_spec = ((S, GROUP_WIDTH_A), F32, (S, LANES), lambda hp: (0, hp))
    return _call(name, body, (4,),
                 _a_slab_specs(proj, g)
                 + [(bias, (None, None, 3, 2 * A_TQ, A_WIN), lambda hp: (g, hp, 0, 0, 0))],
                 [out_spec, out_spec],
                 scratch=[pltpu.VMEM((S, LANES), F32)] + [pltpu.VMEM((S + 2 * pad, LANES), F32)] * 2,
                 sem=("parallel",))


def a_combine(outs, lses, name):
    S, W = outs[0].shape
    tr = 512

    def body(o0, o1, o2, l0, l1, l2, oa_ref, lt_ref):
        a, b, c = l0[...], l1[...], l2[...]
        m = jnp.maximum(jnp.maximum(a, b), c)
        ea, eb, ec = jnp.exp(a - m), jnp.exp(b - m), jnp.exp(c - m)
        z = ea + eb + ec
        oa_ref[...] = ((ea * o0[...] + eb * o1[...] + ec * o2[...]) / z).astype(BF16)
        lt_ref[...] = m + jnp.log(z)

    spec = ((tr, W), lambda i: (i, 0))
    return _call(name, body, (S // tr,), [(a,) + spec for a in (*outs, *lses)],
                 [((S, W), BF16) + spec, ((S, W), F32) + spec], sem=("parallel",))


def a_bwd(proj, bias, do_a, o_a, lse_tot, g, name):
    S = proj.shape[0]
    d = DILATIONS[g]
    L = S // d
    nqb = L // A_TQ
    pad = HALF_WINDOW * d

    def body(q_ref, k_ref, v_ref, b_ref, do_ref, o_ref, l_ref, dqkv_ref, db_ref,
             qf, of, dqf, kpad, vpad, dkacc, dvacc):
        qf[...] = q_ref[...].astype(F32) * A_SCALE
        of[...] = o_ref[...].astype(F32)
        _a_fill_padded(kpad, k_ref, S, pad)
        _a_fill_padded(vpad, v_ref, S, pad)
        dkacc[...] = jnp.zeros(dkacc.shape, F32)
        dvacc[...] = jnp.zeros(dvacc.shape, F32)
        db_ref[...] = jnp.zeros(db_ref.shape, F32)
        lane = lax.broadcasted_iota(jnp.int32, (A_TQ, LANES), 1)

        def block(t, carry):
            qb, r = t // d, t % d
            start = qb * (A_TQ * d) + r
            rows = pl.ds(start, A_TQ, stride=d)
            win = pl.ds(start, A_WIN, stride=d)
            kw = kpad[win, :].astype(BF16)
            vw = vpad[win, :].astype(BF16)
            q = qf[rows, :].astype(BF16)
            do = do_ref[rows, :]
            ov = of[rows, :]
            lt = l_ref[rows, :]
            q2 = _a_stack_heads(q, lane)
            do2 = _a_stack_heads(do, lane)
            lt2 = jnp.concatenate([lt[:, 0:1], lt[:, HEAD_DIM_A:HEAD_DIM_A + 1]], axis=0)
            s = _dot(q2, kw, 1, 1) + b_ref[_a_bias_variant(qb, nqb)]
            p = jnp.exp(s - lt2)
            t = jnp.sum(do2 * jnp.concatenate([ov, ov], axis=0), axis=-1, keepdims=True)
            dob2 = do2.astype(BF16)
            ds = p * (_dot(dob2, vw, 1, 1) - t)
            db_ref[...] += ds
            dsb = ds.astype(BF16)
            dq2 = _dot(dsb, kw)
            dqf[rows, :] = jnp.where(lane < HEAD_DIM_A, dq2[0:A_TQ], dq2[A_TQ:]) * A_SCALE
            dkacc[win, :] += _dot(dsb, q2, 0, 0)
            dvacc[win, :] += _dot(p.astype(BF16), dob2, 0, 0)
            return carry

        lax.fori_loop(0, nqb * d, block, 0, unroll=A_UNROLL)
        dqkv_ref[0] = dqf[...].astype(BF16)
        dqkv_ref[1] = dkacc[pad:pad + S, :].astype(BF16)
        dqkv_ref[2] = dvacc[pad:pad + S, :].astype(BF16)

    slab = ((S, LANES), lambda hp: (0, hp))
    padded = pltpu.VMEM((S + 2 * pad, LANES), F32)
    return _call(
        name, body, (4,),
        _a_slab_specs(proj, g)
        + [(bias, (None, None, 3, 2 * A_TQ, A_WIN), lambda hp: (g, hp, 0, 0, 0)),
           (do_a,) + slab, (o_a,) + slab, (lse_tot,) + slab],
        [((3, S, GROUP_WIDTH_A), BF16, (3, S, LANES), lambda hp: (0, 0, hp)),
         ((4, 2 * A_TQ, A_WIN), F32, (None, 2 * A_TQ, A_WIN), lambda hp: (hp, 0, 0))],
        scratch=[pltpu.VMEM((S, LANES), F32)] * 3 + [padded] * 4,
        sem=("parallel",))


def _rope_tables(S):
    rows = S // GRID_W
    row = jnp.repeat(jnp.arange(rows, dtype=F32), GRID_W)
    col = jnp.tile(jnp.arange(GRID_W, dtype=F32), rows)
    n_freq = HEAD_DIM_B // 4
    freq = ROPE_THETA ** (-jnp.arange(n_freq, dtype=F32) / n_freq)
    ang = jnp.concatenate([row[:, None] * freq, col[:, None] * freq], axis=-1)
    cos, sin = jnp.cos(ang), jnp.sin(ang)
    return jnp.repeat(cos, 2, axis=-1), jnp.stack([-sin, sin], axis=-1).reshape(S, HEAD_DIM_B)


def _swap_pairs(y):
    lane = lax.broadcasted_iota(jnp.int32, y.shape, 1)
    return jnp.where(lane % 2 == 0, pltpu.roll(y, LANES - 1, 1), pltpu.roll(y, 1, 1))


def qkv_prep(proj, gains, cos_t, sin_t, name):
    S = proj.shape[0]
    ts = 256
    n_rot = N_HEADS_B + N_KV_B
    nh = n_rot + N_KV_B
    W = nh * LANES

    def body(x_ref, g_ref, c_ref, s_ref, o_ref):
        cv, sv = c_ref[...], s_ref[...]
        for hb in range(nh):
            cols = slice(hb * LANES, (hb + 1) * LANES)
            if hb < n_rot:
                xv = x_ref[:, cols].astype(F32)
                r = lax.rsqrt(jnp.mean(xv * xv, axis=-1, keepdims=True) + EPS)
                yv = xv * r * g_ref[:, cols]
                o_ref[:, cols] = (yv * cv + _swap_pairs(yv) * sv).astype(BF16)
            else:
                o_ref[:, cols] = x_ref[:, cols]

    return _call(name, body, (S // ts,),
                 [(proj, (ts, W), lambda i: (i, A_QKV_WIDTH // W)), (gains, (1, W), lambda i: (0, 0)),
                  (cos_t, (ts, LANES), lambda i: (i, 0)), (sin_t, (ts, LANES), lambda i: (i, 0))],
                 [((S, W), BF16, (ts, W), lambda i: (i, 0))],
                 sem=("parallel",))[0]


def qk_prep_bwd(dr, proj, col0, gain, cos_t, sin_t, name):
    S, W = dr.shape
    H = W // LANES
    ts = 256
    wx = math.gcd(W, col0)
    n_x = W // wx

    def body(d_ref, *refs):
        x_refs = refs[:n_x]
        g_ref, c_ref, s_ref, dx_ref, dg_ref = refs[n_x:]
        i = pl.program_id(0)
        cv, sv, gv = c_ref[...], s_ref[...], g_ref[...]
        dgp = jnp.zeros((1, LANES), F32)
        for hb in range(H):
            cols = slice(hb * LANES, (hb + 1) * LANES)
            xc = (hb * LANES) % wx
            xv = x_refs[(hb * LANES) // wx][:, xc:xc + LANES].astype(F32)
            dout = d_ref[:, cols]
            dy = dout * cv + _swap_pairs(dout * sv)
            dx, dgt = _rms_bwd_tile(dy, xv, gv)
            dx_ref[:, cols] = dx.astype(BF16)
            dgp = dgp + jnp.sum(dgt, axis=0, keepdims=True)

        @pl.when(i == 0)
        def _():
            dg_ref[...] = dgp

        @pl.when(i > 0)
        def _():
            dg_ref[...] += dgp

    return _call(name, body, (S // ts,),
                 [(dr, (ts, W), lambda i: (i, 0))]
                 + [(proj, (ts, wx), lambda i, k=k: (i, col0 // wx + k)) for k in range(n_x)]
                 + [(gain, (1, LANES), lambda i: (0, 0)),
                  (cos_t, (ts, LANES), lambda i: (i, 0)), (sin_t, (ts, LANES), lambda i: (i, 0))],
                 [((S, W), BF16, (ts, W), lambda i: (i, 0)),
                  ((1, LANES), F32, (1, LANES), lambda i: (0, 0))],
                 sem=("arbitrary",))


def _row_sums(x):
    hi = x.astype(BF16)
    lo = (x - hi.astype(F32)).astype(BF16)
    ones = jnp.ones((8, LANES), BF16)
    return (_dot(ones, hi, 1, 1) + _dot(ones, lo, 1, 1))[0:1, :]


def flash_fwd(qkv, name):
    S = qkv.shape[0]
    tq = B_TQ_FWD
    scale = HEAD_DIM_B ** -0.5

    hps = B_HEADS_PER_STEP

    def body(q_ref, k_ref, v_ref, o_ref, l_ref):
        k, v = k_ref[...], v_ref[...]
        for j in range(hps):
            cols = slice(j * LANES, (j + 1) * LANES)
            s = _dot(q_ref[:, cols], k, 1, 1)
            m = jnp.max(s, axis=-1, keepdims=True)
            e = jnp.exp2((s - m) * (scale * LOG2E))
            l = jnp.sum(e, axis=-1, keepdims=True)
            o_ref[:, cols] = (_dot(e.astype(BF16), v) / l).astype(BF16)
            lse = jnp.broadcast_to(m * scale + jnp.log(l), (tq, LANES))
            l_ref[j] = _row_sums(lse) * (1.0 / LANES)

    per = GQA_GROUP_B // hps
    heads = lambda g, h, i: (i, g * per + h)
    return _call(name, body, (N_KV_B, per, S // tq),
                 [(qkv, (tq, hps * LANES), heads),
                  (qkv, (S, LANES), lambda g, h, i: (0, N_HEADS_B + g)),
                  (qkv, (S, LANES), lambda g, h, i: (0, N_HEADS_B + N_KV_B + g))],
                 [((S, N_HEADS_B * LANES), BF16, (tq, hps * LANES), heads),
                  ((N_HEADS_B, 1, S), F32, (hps, 1, tq), lambda g, h, i: (g * per + h, 0, i))],
                 sem=("parallel", "parallel", "parallel"))


def flash_bwd(qkv, k_t, do_b, o_b, lse, name):
    S = qkv.shape[0]
    tq = B_TQ_BWD
    nq = S // tq
    scale = HEAD_DIM_B ** -0.5

    def body(q_ref, k_ref, v_ref, kt_ref, do_ref, o_ref, l_ref, dq_ref, dk_ref, dv_ref, dkacc, dvacc):
        h, i = pl.program_id(1), pl.program_id(2)

        @pl.when((h == 0) & (i == 0))
        def _():
            dkacc[...] = jnp.zeros(dkacc.shape, F32)
            dvacc[...] = jnp.zeros(dvacc.shape, F32)

        q = q_ref[...]
        dob = do_ref[...]
        t = _row_sums(dob.astype(F32) * o_ref[...].astype(F32))
        pt = jnp.exp2(_dot(k_ref[...], q, 1, 1) * (scale * LOG2E) - l_ref[...] * LOG2E)
        dsb = (pt * (_dot(v_ref[...], dob, 1, 1) - t)).astype(BF16)
        dvacc[...] += _dot(pt.astype(BF16), dob)
        dkacc[...] += _dot(dsb, q)
        dq_ref[...] = _dot(kt_ref[...], dsb).T * scale

        @pl.when((h == GQA_GROUP_B - 1) & (i == nq - 1))
        def _():
            dk_ref[...] = dkacc[...] * scale
            dv_ref[...] = dvacc[...].astype(BF16)

    head = lambda g, h, i: (i, g * GQA_GROUP_B + h)
    return _call(name, body, (N_KV_B, GQA_GROUP_B, nq),
                 [(qkv, (tq, LANES), head),
                  (qkv, (S, LANES), lambda g, h, i: (0, N_HEADS_B + g)),
                  (qkv, (S, LANES), lambda g, h, i: (0, N_HEADS_B + N_KV_B + g)),
                  (k_t, (LANES, S), lambda g, h, i: (g, 0)),
                  (do_b, (tq, LANES), head), (o_b, (tq, LANES), head),
                  (lse, (None, 1, tq), lambda g, h, i: (g * GQA_GROUP_B + h, 0, i))],
                 [((S, N_HEADS_B * LANES), F32, (tq, LANES), head),
                  ((S, N_KV_B * LANES), F32, (S, LANES), lambda g, h, i: (0, g)),
                  ((S, N_KV_B * LANES), BF16, (S, LANES), lambda g, h, i: (0, g))],
                 scratch=[pltpu.VMEM((S, LANES), F32)] * 2,
                 sem=("parallel", "arbitrary", "arbitrary"))


MERGE_TN = 512


def _mix_rows_spec(Gm, row0, n_slots, slot_map, cols=None, col_map=None):
    C = Gm.shape[2] if cols is None else cols
    cm = (lambda *idx: 0) if col_map is None else col_map
    return (Gm, (n_slots, LANES, C), lambda *idx: (slot_map(*idx), row0 // LANES, cm(*idx)))


def _gate_specs(proj, tm):
    first = (A_QKV_WIDTH + PB_GATE_A) // MERGE_TN
    return [(proj, (tm, MERGE_TN), lambda i, k=k: (i, first + k)) for k in range(4)]


def _whole_rows_spec(Gm, row0):
    return _mix_rows_spec(Gm, row0, N_DEV, lambda *idx: 0)


def merge_fwd(o_a, o_b, w_a, Gm, proj, b_gate, x, name):
    S, D = x.shape
    tm = 256

    def body(oa_ref, ob_ref, wa_ref, wb_ref, wo_ref, g0, g1, g2, g3, bg_ref, x_ref, m_ref, ya_ref, yb_ref, xo_ref):
        ya = _dot(oa_ref[...], wa_ref[...])
        yb = _dot(ob_ref[...], wb_ref[...].reshape(N_DEV * LANES, D))
        ga = _sigmoid(jnp.concatenate([g0[...], g1[...]], axis=1).astype(F32) + bg_ref[:, 0:D])
        gb = _sigmoid(jnp.concatenate([g2[...], g3[...]], axis=1).astype(F32) + bg_ref[:, D:2 * D])
        merged = (ga * ya + gb * yb).astype(BF16)
        m_ref[...] = merged
        ya_ref[...] = ya.astype(BF16)
        yb_ref[...] = yb.astype(BF16)
        xo_ref[...] = x_ref[...] + _dot(merged, wo_ref[...].reshape(N_DEV * LANES, D))

    rows = lambda a: (a, (tm, a.shape[1]), lambda i: (i, 0))
    out = ((S, D), BF16, (tm, D), lambda i: (i, 0))
    return _call(name, body, (S // tm,),
                 [rows(o_a), rows(o_b), (w_a, w_a.shape, lambda i: (0, 0)),
                  _whole_rows_spec(Gm, REST_WB), _whole_rows_spec(Gm, REST_WOUT)]
                 + _gate_specs(proj, tm) + [(b_gate, (1, 2 * D), lambda i: (0, 0)), rows(x)],
                 [out, out, out, ((S, D), F32, (tm, D), lambda i: (i, 0))], sem=("parallel",))


def merge_bwd(dx2, w_a, Gm, ya, yb, proj, b_gate, name):
    S, D = dx2.shape
    tm = 256

    def body(d_ref, wo_ref, wa_ref, wb_ref, ya_ref, yb_ref, g0, g1, g2, g3, bg_ref,
             dya_ref, dyb_ref, dg_ref, dbg_ref, doa_ref, dob_ref):
        i = pl.program_id(0)
        dm = _dot(d_ref[...].astype(BF16), wo_ref[...].reshape(N_DEV * LANES, D), 1, 1)
        ga = _sigmoid(jnp.concatenate([g0[...], g1[...]], axis=1).astype(F32) + bg_ref[:, 0:D])
        gb = _sigmoid(jnp.concatenate([g2[...], g3[...]], axis=1).astype(F32) + bg_ref[:, D:2 * D])
        dya = (dm * ga).astype(BF16)
        dyb = (dm * gb).astype(BF16)
        dya_ref[...] = dya
        dyb_ref[...] = dyb
        dpa = dm * ya_ref[...].astype(F32) * ga * (1.0 - ga)
        dpb = dm * yb_ref[...].astype(F32) * gb * (1.0 - gb)
        dg_ref[0] = dpa.astype(BF16)
        dg_ref[1] = dpb.astype(BF16)
        doa_ref[...] = _dot(dya, wa_ref[...], 1, 1)
        dob_ref[...] = _dot(dyb, wb_ref[...].reshape(N_DEV * LANES, D), 1, 1).astype(BF16)
        sa =jnp.sum(dpa, axis=0, keepdims=True)
        sb = jnp.sum(dpb, axis=0, keepdims=True)

        @pl.when(i == 0)
        def _():
            dbg_ref[0] = sa
            dbg_ref[1] = sb

        @pl.when(i > 0)
        def _():
            dbg_ref[0] += sa
            dbg_ref[1] += sb

    tile = ((tm, D), lambda i: (i, 0))
    return _call(
        name, body, (S // tm,),
        [(dx2,) + tile, _whole_rows_spec(Gm, REST_WOUT), (w_a, w_a.shape, lambda i: (0, 0)),
         _whole_rows_spec(Gm, REST_WB), (ya,) + tile, (yb,) + tile]
        + _gate_specs(proj, tm) + [(b_gate, (1, 2 * D), lambda i: (0, 0))],
        [((S, D), BF16) + tile, ((S, D), BF16) + tile,
         ((2, S, D), BF16, (2, tm, D), lambda i: (0, i, 0)),
         ((2, 1, D), F32, (2, 1, D), lambda i: (0, 0, 0)),
         ((S, w_a.shape[0]), F32, (tm, w_a.shape[0]), lambda i: (i, 0)),
         ((S, N_HEADS_B * LANES), BF16, (tm, N_HEADS_B * LANES), lambda i: (i, 0))],
        sem=("arbitrary",))


def weight_grad_rows(a, b, grads, row0, name):
    S, M = a.shape
    N = b.shape[1]
    tmm = 512
    tk = WGRAD_TK
    nk = S // tk
    prior = [] if grads is None else [grads]

    def body(*refs):
        a_ref, b_ref, o_ref, acc_ref = refs[len(prior):]
        k = pl.program_id(1)
        p = _dot(a_ref[...], b_ref[...].astype(BF16), 0, 0)

        @pl.when(k == 0)
        def _():
            acc_ref[...] = p

        @pl.when(k > 0)
        def _():
            acc_ref[...] += p

        @pl.when(k == nk - 1)
        def _():
            o_ref[...] = acc_ref[...].astype(BF16).reshape(tmm // LANES, LANES, N)

    return pl.pallas_call(
        body,
        out_shape=jax.ShapeDtypeStruct((N_DEV, MIX_ROWS, N), BF16),
        grid=(M // tmm, nk),
        in_specs=[pl.BlockSpec(memory_space=pl.ANY)] * len(prior)
        + [pl.BlockSpec((tk, tmm), lambda j, k: (k, j)),
           pl.BlockSpec((tk, N), lambda j, k: (k, 0))],
        out_specs=pl.BlockSpec((tmm // LANES, LANES, N), lambda j, k: (j, row0 // LANES, 0)),
        scratch_shapes=[pltpu.VMEM((tmm, N), F32)],
        input_output_aliases={0: 0} if prior else {},
        name=name,
        compiler_params=pltpu.CompilerParams(dimension_semantics=("parallel", "arbitrary"),
                                             vmem_limit_bytes=VMEM_LIMIT),
    )(*prior, a, b)


def weight_grad_plain(a, b, name):
    S, M = a.shape
    N = b.shape[1]
    tk = WGRAD_TK
    nk = S // tk

    def body(a_ref, b_ref, o_ref, acc_ref):
        k = pl.program_id(0)
        p = _dot(a_ref[...], b_ref[...], 0, 0)

        @pl.when(k == 0)
        def _():
            acc_ref[...] = p

        @pl.when(k > 0)
        def _():
            acc_ref[...] += p

        @pl.when(k == nk - 1)
        def _():
            o_ref[...] = acc_ref[...].astype(BF16)

    return _call(name, body, (nk,),
                 [(a, (tk, M), lambda k: (k, 0)), (b, (tk, N), lambda k: (k, 0))],
                 [((M, N), BF16, (M, N), lambda k: (0, 0))],
                 scratch=[pltpu.VMEM((M, N), F32)], sem=("arbitrary",))[0]


def local_step(x, tgt, p, get_g1_up, get_g1_down, get_gm_in, get_gm_rest, get_g2, emit, start_token):
    S, D = x.shape
    after = lambda t: t[0:1, 0:1]
    buckets = _bucket_tables()
    cos_t, sin_t = _rope_tables(S)
    gains = jnp.concatenate([jnp.tile(p["q_norm"], (1, N_HEADS_B)), jnp.tile(p["k_norm"], (1, N_KV_B)),
                             jnp.ones((1, N_KV_B * LANES), F32)], axis=1)

    n1 = rms_fwd(x, p["ffn1_norm"] + after(start_token), "ffn1_norm")
    bias = bias_build(p["rel_bias"] + after(start_token), buckets)
    g1_up = get_g1_up((n1, bias))
    ab1 = ffn_up(n1, (g1_up, None), "ffn1_up")
    G1 = (g1_up, get_g1_down(ab1))
    x1, hm = ffn_down(ab1, G1, x, p["mix_norm"], "ffn1_down")
    Gw = get_gm_in(hm)
    proj = in_proj(hm, Gw, "in_proj")

    outs, lses = [], []
    for g in range(3):
        o, l = a_fwd(proj, bias, g, "a_fwd_%d" % g)
        outs.append(o)
        lses.append(l)
    o_a, lse_tot = a_combine(outs, lses, "a_combine")

    qkv = qkv_prep(proj, gains, cos_t, sin_t, "qkv_prep")
    k_t = qkv[:, N_HEADS_B * LANES:(N_HEADS_B + N_KV_B) * LANES].T
    o_b, lse_b = flash_fwd(qkv, "flash_fwd")

    Gm = get_gm_rest(o_b)
    w_a = Gm[:, REST_WA:REST_ROWS, :].reshape(N_DEV, GROUP_WIDTH_A, LANES).transpose(1, 0, 2).reshape(GROUP_WIDTH_A, D)
    merged, ya, yb, x2 = merge_fwd(o_a, o_b, w_a, Gm, proj, p["b_gate"], x1, "merge_fwd")

    G2 = get_g2(x2)
    n2, ab2, dx3_b, dab2, dx2, dx2_b, d_ffn2_norm, loss, d_final = ffn_last(
        x2, p["ffn2_norm"], G2, tgt, p["final_norm"], "ffn2")
    gw2 = ffn_bwd_weights(dx3_b, ab2, dab2, n2, "ffn2_bwd")
    t2 = emit("ffn2", gw2)

    dya, dyb, dgate, dbg, do_a, do_b = merge_bwd(dx2_b, w_a, Gm, ya, yb, proj, p["b_gate"] + after(t2),
                                                 "merge_bwd")
    gm_grads = weight_grad_rows(merged, dx2_b, None, MIX_WOUT, "dw_out")
    gm_grads = weight_grad_rows(o_b, dyb, gm_grads, MIX_WB, "dw_branch_b")
    dw_a = weight_grad_plain(o_a, dya, "dw_branch_a")

    dq_r, dk_r, dv_b = flash_bwd(qkv, k_t, do_b, o_b, lse_b, "flash_bwd")
    dq_b, d_q_norm = qk_prep_bwd(dq_r, proj, A_QKV_WIDTH, p["q_norm"], cos_t, sin_t, "q_prep_bwd")
    dk_b, d_k_norm = qk_prep_bwd(dk_r, proj, A_QKV_WIDTH + N_HEADS_B * LANES, p["k_norm"], cos_t, sin_t,
                                 "k_prep_bwd")

    dqkv, dbs = [], []
    for g in range(3):
        dg_, db = a_bwd(proj, bias, do_a, o_a, lse_tot, g, "a_bwd_%d" % g)
        dqkv.append(dg_)
        dbs.append(db)
    d_rel_bias = bias_bwd(jnp.stack(dbs, axis=0).reshape(3, HEADS_PER_GROUP_A, A_TQ, A_WIN), buckets)

    dproj = _dproj_pieces(dqkv, dq_b, jnp.concatenate([dk_b, dv_b], axis=1), dgate)
    gm_grads = in_proj_bwd_dw(dproj, hm, gm_grads, "in_proj_bwd")
    dw_a_sh = dw_a.reshape(GROUP_WIDTH_A, N_DEV, LANES).transpose(1, 0, 2).reshape(N_DEV, MIX_ROWS - MIX_WA, D)
    gm_grads = lax.dynamic_update_slice(gm_grads, dw_a_sh, (0, MIX_WA, 0))
    tm = emit("mix", gm_grads)
    dx1, dx1_b, d_mix_norm = in_proj_bwd_dh(dproj, Gw, x1, p["mix_norm"] + after(tm), dx2, "in_proj_bwd")

    dab1 = ffn_bwd_hidden(dx1_b, ab1, G1, "ffn1_bwd")
    gw1 = ffn_bwd_weights(dx1_b, ab1, dab1, n1, "ffn1_bwd")
    t1 = emit("ffn1", gw1)
    dx0, d_ffn1_norm = ffn_bwd_input(dab1, G1, x, p["ffn1_norm"] + after(t1), dx1, "ffn1_bwd")

    small = dict(ffn1_norm=d_ffn1_norm, mix_norm=d_mix_norm, b_gate=dbg.reshape(1, 2 * D),
                 q_norm=d_q_norm, k_norm=d_k_norm, rel_bias=d_rel_bias, ffn2_norm=d_ffn2_norm,
                 final_norm=d_final)
    return loss, dx0, small


def _pack_small(t, loss_row):
    row6 = jnp.concatenate([t["q_norm"].reshape(1, -1), t["k_norm"].reshape(1, -1), t["rel_bias"].reshape(1, -1)], axis=1)
    return jnp.concatenate([t["ffn1_norm"].reshape(1, -1), t["mix_norm"].reshape(1, -1), t["b_gate"].reshape(2, -1),
                            t["ffn2_norm"].reshape(1, -1), t["final_norm"].reshape(1, -1), row6, loss_row], axis=0)


def _unpack_small(a, shapes):
    return dict(ffn1_norm=a[0:1].reshape(shapes["ffn1_norm"]), mix_norm=a[1:2].reshape(shapes["mix_norm"]),
                b_gate=a[2:4].reshape(shapes["b_gate"]), ffn2_norm=a[4:5].reshape(shapes["ffn2_norm"]),
                final_norm=a[5].reshape(shapes["final_norm"]), q_norm=a[6:7, 0:128].reshape(shapes["q_norm"]),
                k_norm=a[6:7, 128:256].reshape(shapes["k_norm"]), rel_bias=a[6, 256:1024].reshape(shapes["rel_bias"]))


SMALL = ("ffn1_norm", "mix_norm", "b_gate", "q_norm", "k_norm", "rel_bias", "ffn2_norm", "final_norm")
ORDER = ("ffn1_norm", "ffn1_w1", "ffn1_w3", "ffn1_w2", "mix_norm", "w_in", "b_gate", "q_norm", "k_norm", "rel_bias",
         "w_branch_a", "w_branch_b", "w_out", "ffn2_norm", "ffn2_w1", "ffn2_w3", "ffn2_w2", "final_norm")


def kernel(x, ffn1_norm, ffn1_w1, ffn1_w3, ffn1_w2, mix_norm, w_in, b_gate, q_norm, k_norm, rel_bias, w_branch_a, w_branch_b, w_out, ffn2_norm, ffn2_w1, ffn2_w3, ffn2_w2, final_norm, loss_target, m_ffn1_norm, m_ffn1_w1, m_ffn1_w3, m_ffn1_w2, m_mix_norm, m_w_in, m_b_gate, m_q_norm, m_k_norm, m_rel_bias, m_w_branch_a, m_w_branch_b, m_w_out, m_ffn2_norm, m_ffn2_w1, m_ffn2_w3, m_ffn2_w2, m_final_norm, v_ffn1_norm, v_ffn1_w1, v_ffn1_w3, v_ffn1_w2, v_mix_norm, v_w_in, v_b_gate, v_q_norm, v_k_norm, v_rel_bias, v_w_branch_a, v_w_branch_b, v_w_out, v_ffn2_norm, v_ffn2_w1, v_ffn2_w3, v_ffn2_w2, v_final_norm):
    args = dict(locals())
    w = {n: args[n] for n in ORDER}
    m = {n: args["m_" + n] for n in ORDER}
    v = {n: args["v_" + n] for n in ORDER}
    D = x.shape[2]

    blocks = (
        ("ffn1_up", jnp.concatenate([ffn1_w1[0].T, ffn1_w3[0].T], axis=0)),
        ("ffn1_down", ffn1_w2[0]),
        ("mix_in", w_in[0]),
        ("mix_rest", jnp.concatenate([w_branch_b[0], w_out[0], w_branch_a[0].reshape(REST_ROWS - REST_WA, D)], axis=0)),
        ("ffn2", jnp.concatenate([ffn2_w1[0].T, ffn2_w3[0].T, ffn2_w2[0]], axis=0)),
    )
    direct = ("mix_rest", "ffn2")
    started = all_gather_start_all([(b.astype(BF16), tag in direct) for tag, b in blocks], "all_gather_start")
    gathers = {tag: s for (tag, _), s in zip(blocks, started)}
    start_token = started[0][4]

    def gathered(tag):
        def get(after):
            if tag in direct:
                return all_gather_place_own(*_split_wait("all_gather_" + tag + "_wait", gathers[tag], N_DEV - 1, after),
                                            "all_gather_" + tag + "_own")
            return all_gather_finish(*_split_wait("all_gather_" + tag + "_wait", gathers[tag], 4, after),
                                     "all_gather_" + tag + "_finish")
        return get

    core = lax.axis_index("c").astype(jnp.int32).reshape(1)
    chip = (2 * lax.axis_index("x") + lax.axis_index("y")).astype(jnp.int32).reshape(1)
    device = 2 * chip + core
    exchanges = {}

    def emit(tag, gw):
        if tag == "ffn1":
            (theirs,) = reduce_scatter_pair([gw], "reduce_scatter_pair_" + tag)
            part = pair_add(gw, theirs, core, "pair_add_" + tag)
            exchanges[tag] = reduce_scatter_start(part, "reduce_scatter_" + tag + "_start")
        else:
            exchanges[tag] = reduce_scatter_start_direct(gw, "reduce_scatter_" + tag + "_start")
        return exchanges[tag][4]

    small_p = dict(ffn1_norm=ffn1_norm, mix_norm=mix_norm, b_gate=b_gate, q_norm=q_norm, k_norm=k_norm,
                   rel_bias=rel_bias, ffn2_norm=ffn2_norm, final_norm=final_norm.reshape(1, D))
    loss_p, grad_x, small_g = local_step(x[0], loss_target[0], small_p, gathered("ffn1_up"), gathered("ffn1_down"),
                                         gathered("mix_in"), gathered("mix_rest"), gathered("ffn2"), emit, start_token)

    def landed(tag, after):
        n_others, me = (3, chip) if tag == "ffn1" else (N_DEV - 1, device)
        return tuple(_split_wait("reduce_scatter_" + tag + "_wait", exchanges[tag], n_others, after)) + (me,)

    grads, delta, new_m, new_v = {}, {}, {}, {}

    def finish(n, part, land, me, off, blk, transposed=False):
        shp = w[n].shape
        if transposed:
            to2 = lambda a: a.reshape(shp[-2], shp[-1]).T
            back = lambda a: a.T.reshape(shp)
        else:
            to2 = lambda a: a.reshape(shp[-2], shp[-1])
            back = lambda a: a.reshape(shp)
        res = sum_adamw(part, land, me, off, blk, to2(w[n]), to2(m[n]), to2(v[n]), "update_" + n)
        grads[n], delta[n], new_m[n], new_v[n] = [back(a) for a in res]

    last_token = exchanges["ffn1"][4]
    for tag, after in (("ffn2", last_token), ("ffn1", grad_x)):
        group = landed(tag, after)
        finish(tag + "_w1", *group, 0, FFN_SHARD, transposed=True)
        finish(tag + "_w3", *group, FFN_SHARD, FFN_SHARD, transposed=True)
        finish(tag + "_w2", *group, 2 * FFN_SHARD, FFN_SHARD)
        if tag == "ffn2":
            group_m = landed("mix", last_token)
            finish("w_in", *group_m, MIX_WIN, LANES)
            finish("w_branch_b", *group_m, MIX_WB, LANES)
            finish("w_out", *group_m, MIX_WOUT, LANES)
            grads["w_branch_a"] = sum_landed(*group_m, MIX_WA, MIX_ROWS - MIX_WA, MIX_ROWS - MIX_WA,
                                             "w_branch_a_sum").reshape(w_branch_a.shape)
    loss_row = jnp.pad(loss_p, ((0, 0), (0, D - LANES)))
    smalls = small_all_gather(_pack_small(small_g, loss_row), new_v["w_in"])
    small_sum = sum_slots(smalls, 0, N_DEV, N_DEV, "small_sum")
    small_shapes = {n: w[n].shape for n in SMALL}
    grads.update(_unpack_small(small_sum, small_shapes))
    loss = small_sum[7, 0]

    n = "w_branch_a"
    two_d = lambda a: a.reshape(w[n].shape[-2], w[n].shape[-1])
    d_, m_, v_ = adamw(two_d(w[n]), two_d(grads[n]), two_d(m[n]), two_d(v[n]), "adamw_" + n)
    delta[n], new_m[n], new_v[n] = [a.reshape(w[n].shape) for a in (d_, m_, v_)]
    zero_row = jnp.zeros((1, D), F32)
    pack = lambda t: _pack_small({n: t[n] for n in SMALL}, zero_row)
    d_, m_, v_ = adamw(pack(w), small_sum, pack(m), pack(v), "adamw_small")
    for src, dst in ((d_, delta), (m_, new_m), (v_, new_v)):
        dst.update(_unpack_small(src, small_shapes))

    return (loss, grad_x[None], *[grads[n] for n in ORDER], *[delta[n] for n in ORDER],
            *[new_m[n] for n in ORDER], *[new_v[n] for n in ORDER])
```

```python
import math

import jax
import jax.numpy as jnp
from jax import lax
from jax.experimental import pallas as pl
from jax.experimental.pallas import tpu as pltpu

F32 = jnp.float32
BF16 = jnp.bfloat16
MESH = pl.DeviceIdType.MESH

V7X_VMEM_BYTES = 64 * 1024 * 1024
VMEM_LIMIT = V7X_VMEM_BYTES - 8 * 1024 * 1024
LANES = 128

N_DEV = 8
EPS = 1e-6
NEG_INF = -1e30

DILATIONS = (1, 4, 16)
HALF_WINDOW = 64
HEAD_DIM_A = 64
HEADS_PER_GROUP_A = 8
GROUP_WIDTH_A = 512
A_QKV_WIDTH = 4608
A_GROUP_QKV = A_QKV_WIDTH // 3
A_TQ = 128
A_WIN = A_TQ + 2 * HALF_WINDOW
A_UNROLL = 8
A_SCALE = HEAD_DIM_A ** -0.5
WGRAD_TK = 2048
HEAD_DIM_B = 128
N_HEADS_B = 8
N_KV_B = 2
GQA_GROUP_B = 4
GRID_W = 64
ROPE_THETA = 10000.0
B_TQ_FWD = 256
B_TQ_BWD = 512
B_HEADS_PER_STEP = 4
LOG2E = 1.4426950408889634
N_BUCKETS = 32
MAX_DISTANCE = 1024
PB_GATE_A = 1536

ADAM_LR = 0.001
ADAM_B1 = 0.9
ADAM_B2 = 0.999
ADAM_EPS = 1e-08
ADAM_WD = 0.01
ADAM_STEP = 10

FFN_SHARD = 352
MIX_WIN, MIX_WB, MIX_WOUT, MIX_WA = 0, 1024, 1152, 1280
MIX_ROWS = 1344
REST_WB, REST_WOUT, REST_WA, REST_ROWS = 0, 128, 256, 320


def _dot(a, b, ca=1, cb=0):
    return lax.dot_general(a, b, (((ca,), (cb,)), ((), ())), preferred_element_type=F32)


def _call(name, body, grid, ins, outs, scratch=(), sem=None, aliases=None):
    ins = [tuple(i) + (None,) * (4 - len(i)) for i in ins]
    res = pl.pallas_call(
        body,
        out_shape=[jax.ShapeDtypeStruct(s, d) for (s, d, _, _) in outs],
        grid=grid,
        in_specs=[pl.BlockSpec(bs, im, pipeline_mode=pm) for (_, bs, im, pm) in ins],
        out_specs=[pl.BlockSpec(bs, im) for (_, _, bs, im) in outs],
        scratch_shapes=list(scratch),
        name=name,
        input_output_aliases=aliases or {},
        compiler_params=pltpu.CompilerParams(dimension_semantics=sem, vmem_limit_bytes=VMEM_LIMIT),
    )(*[i[0] for i in ins])
    return res


def _sigmoid(x):
    return 0.5 * jnp.tanh(0.5 * x) + 0.5


def _position():
    return lax.axis_index("x"), lax.axis_index("y"), lax.axis_index("c")


def _hbm_specs(n):
    return [pl.BlockSpec(memory_space=pl.ANY) for _ in range(n)]


PAIR_BUFFERS = 4


def reduce_scatter_pair(grads, name):
    n = len(grads)
    C = grads[0].shape[2]
    half = [g.shape[1] // 2 for g in grads]
    chunks = [(i, q, hf) for i in range(n) for q in range(4) for hf in range(2)]
    nb = PAIR_BUFFERS

    def body(*refs):
        ins, theirs = refs[:n], refs[n:2 * n]
        buf, load_sems, send_sems, recv_sems = refs[2 * n:]
        x, y, c = _position()
        sibling = (x, y, 1 - c)

        def load(k):
            i, q, hf = chunks[k]
            r = half[i]
            return pltpu.make_async_copy(ins[i].at[2 * q + (1 - c), pl.ds(hf * r, r), :],
                                         buf.at[k % nb, pl.ds(0, r), :], load_sems.at[k % nb])

        def send(k):
            i, q, hf = chunks[k]
            r = half[i]
            return pltpu.make_async_remote_copy(
                src_ref=buf.at[k % nb, pl.ds(0, r), :], dst_ref=theirs[i].at[q, pl.ds(hf * r, r), :],
                send_sem=send_sems.at[k % nb], recv_sem=recv_sems.at[i],
                device_id=sibling, device_id_type=MESH)

        for k in range(len(chunks) + 1):
            if k < len(chunks):
                if k >= nb:
                    send(k - nb).wait_send()
                load(k).start()
            if k >= 1:
                load(k - 1).wait()
                send(k - 1).start()
        for k in range(max(0, len(chunks) - nb), len(chunks)):
            send(k).wait_send()
        for i in range(n):
            pltpu.make_async_remote_copy(
                src_ref=theirs[i], dst_ref=theirs[i], send_sem=send_sems.at[0], recv_sem=recv_sems.at[i],
                device_id=sibling, device_id_type=MESH).wait_recv()

    return pl.pallas_call(
        body,
        out_shape=[jax.ShapeDtypeStruct((4,) + g.shape[1:], g.dtype) for g in grads],
        in_specs=_hbm_specs(n),
        out_specs=_hbm_specs(n),
        scratch_shapes=[pltpu.VMEM((nb, max(half), C), grads[0].dtype), pltpu.SemaphoreType.DMA((nb,)),
                        pltpu.SemaphoreType.DMA((nb,)), pltpu.SemaphoreType.DMA((n,))],
        name=name,
        compiler_params=pltpu.CompilerParams(vmem_limit_bytes=VMEM_LIMIT),
    )(*grads)


_HBM_SPEC = pl.BlockSpec(memory_space=pltpu.HBM)
_SEM_SPEC = pl.BlockSpec(memory_space=pltpu.SEMAPHORE)
_TOKEN_SPEC = pl.BlockSpec(memory_space=pltpu.VMEM)
_DATAFLOW = pltpu.SideEffectType.DATAFLOW_SIDE_EFFECTING


def _split_start_many(name, exchanges):
    n = len(exchanges)

    def full_body(*refs):
        srcs, lands = refs[:n], refs[n:2 * n]
        sems = refs[2 * n:4 * n]
        token = refs[-1]
        for i, (body, _, _) in enumerate(exchanges):
            body(srcs[i], lands[i], sems[2 * i], sems[2 * i + 1])
        token[...] = jnp.zeros_like(token)

    srcs = [pltpu.with_memory_space_constraint(src, pltpu.HBM) for _, src, _ in exchanges]
    lands = [pltpu.with_memory_space_constraint(lax.empty(shape, src.dtype), pltpu.HBM)
             for _, src, shape in exchanges]
    res = pl.pallas_call(
        full_body, name=name,
        out_shape=(pltpu.SemaphoreType.DMA(()),) * (2 * n)
        + tuple(pltpu.HBM(a.shape, a.dtype) for a in srcs + lands) + (jax.ShapeDtypeStruct((8, LANES), F32),),
        in_specs=(_HBM_SPEC,) * (2 * n),
        out_specs=(_SEM_SPEC,) * (2 * n) + (_HBM_SPEC,) * (2 * n) + (_TOKEN_SPEC,),
        input_output_aliases={i: 2 * n + i for i in range(2 * n)},
        compiler_params=pltpu.CompilerParams(has_side_effects=_DATAFLOW),
    )(*srcs, *lands)
    return [(res[2 * i], res[2 * i + 1], res[2 * n + i], res[3 * n + i], res[-1]) for i in range(n)]


def _split_start(name, body, src, land_shape):
    return _split_start_many(name, [(body, src, land_shape)])[0]


def _split_wait(name, started, n_blocks, after):
    send_sem, recv_sem, src_thru, land_thru, _ = started
    after = after if isinstance(after, tuple) else (after,)

    def body(src_ref, land_ref, send_sem, recv_sem, *rest):
        x, y, c = _position()
        blocks = land_ref.at[pl.ds(0, n_blocks)]
        copy = pltpu.make_async_remote_copy(src_ref=blocks, dst_ref=blocks, send_sem=send_sem, recv_sem=recv_sem,
                                            device_id=(x, y, c), device_id_type=MESH)
        copy.wait_send()
        copy.wait_recv()

    return pl.pallas_call(
        body, name=name,
        out_shape=(pltpu.HBM(src_thru.shape, src_thru.dtype), pltpu.HBM(land_thru.shape, land_thru.dtype)),
        in_specs=(_HBM_SPEC, _HBM_SPEC, _SEM_SPEC, _SEM_SPEC) + (pl.BlockSpec(memory_space=pl.ANY),) * len(after),
        out_specs=(_HBM_SPEC, _HBM_SPEC),
        input_output_aliases={0: 0, 1: 1},
        compiler_params=pltpu.CompilerParams(has_side_effects=_DATAFLOW),
    )(src_thru, land_thru, send_sem, recv_sem, *after)


def all_gather_start_all(blocks, name):
    def starter(direct):
        def body(b_ref, land_ref, send_sem, recv_sem):
            x, y, c = _position()
            peers = _other_devices(x, y, c) if direct else [(x, y, 1 - c), (1 - x, y, c), (x, 1 - y, c),
                                                            (1 - x, 1 - y, c)]
            for peer in peers:
                pltpu.make_async_remote_copy(src_ref=b_ref, dst_ref=land_ref.at[4 * x + 2 * y + c],
                                             send_sem=send_sem, recv_sem=recv_sem,
                                             device_id=peer, device_id_type=MESH).start()
        return body

    return _split_start_many(name, [(starter(direct), block, (N_DEV,) + block.shape) for block, direct in blocks])


def all_gather_finish(block, land, name):
    R, C = block.shape

    def body(b_ref, land_in, land_ref, stage, load_sems, send_sems, recv_sems, own_sem):
        x, y, c = _position()
        sibling = (x, y, 1 - c)
        chips = [(1 - x, y), (x, 1 - y), (1 - x, 1 - y)]
        own_in = pltpu.make_async_copy(b_ref, stage.at[3], load_sems.at[3])
        own_in.start()
        loads = [pltpu.make_async_copy(land_in.at[4 * px + 2 * py + c], stage.at[j], load_sems.at[j])
                 for j, (px, py) in enumerate(chips)]
        for ld in loads:
            ld.start()
        sends = []
        for j, (px, py) in enumerate(chips):
            loads[j].wait()
            dst = land_ref.at[4 * px + 2 * py + c]
            cp = pltpu.make_async_remote_copy(src_ref=stage.at[j], dst_ref=dst, send_sem=send_sems.at[j],
                                              recv_sem=recv_sems.at[j], device_id=sibling, device_id_type=MESH)
            cp.start()
            sends.append(cp)
        own_in.wait()
        own_out = pltpu.make_async_copy(stage.at[3], land_ref.at[4 * x + 2 * y + c], own_sem)
        own_out.start()
        for j, (px, py) in enumerate(chips):
            dst = land_ref.at[4 * px + 2 * py + (1 - c)]
            pltpu.make_async_remote_copy(src_ref=stage.at[j], dst_ref=dst, send_sem=send_sems.at[j],
                                         recv_sem=recv_sems.at[j], device_id=sibling,
                                         device_id_type=MESH).wait_recv()
        for cp in sends:
            cp.wait_send()
        own_out.wait()

    return pl.pallas_call(
        body,
        out_shape=jax.ShapeDtypeStruct(land.shape, land.dtype),
        in_specs=_hbm_specs(2),
        out_specs=pl.BlockSpec(memory_space=pl.ANY),
        scratch_shapes=[pltpu.VMEM((4, R, C), block.dtype), pltpu.SemaphoreType.DMA((4,)),
                        pltpu.SemaphoreType.DMA((3,)), pltpu.SemaphoreType.DMA((3,)), pltpu.SemaphoreType.DMA],
        input_output_aliases={1: 0},
        name=name,
        compiler_params=pltpu.CompilerParams(vmem_limit_bytes=VMEM_LIMIT),
    )(block, land)


def reduce_scatter_start(parts, name):
    def body(p_ref, land_ref, send_sem, recv_sem):
        x, y, c = _position()
        for px, py in [(1 - x, y), (x, 1 - y), (1 - x, 1 - y)]:
            pltpu.make_async_remote_copy(src_ref=p_ref.at[2 * px + py], dst_ref=land_ref.at[2 * x + y],
                                         send_sem=send_sem, recv_sem=recv_sem,
                                         device_id=(px, py, c), device_id_type=MESH).start()

    return _split_start(name, body, parts, parts.shape)


def _other_devices(x, y, c):
    return [(1 - x if k & 4 else x, 1 - y if k & 2 else y, 1 - c if k & 1 else c) for k in range(1, N_DEV)]


def all_gather_place_own(block, land, name):
    R, C = block.shape

    def body(b_ref, land_in, land_ref, stage, sems):
        x, y, c = _position()
        load = pltpu.make_async_copy(b_ref, stage, sems.at[0])
        load.start()
        load.wait()
        store = pltpu.make_async_copy(stage, land_ref.at[4 * x + 2 * y + c], sems.at[1])
        store.start()
        store.wait()

    return pl.pallas_call(
        body,
        out_shape=jax.ShapeDtypeStruct(land.shape, land.dtype),
        in_specs=_hbm_specs(2),
        out_specs=pl.BlockSpec(memory_space=pl.ANY),
        scratch_shapes=[pltpu.VMEM((R, C), block.dtype), pltpu.SemaphoreType.DMA((2,))],
        input_output_aliases={1: 0},
        name=name,
    )(block, land)


def reduce_scatter_start_direct(grads, name):
    def body(g_ref, land_ref, send_sem, recv_sem):
        x, y, c = _position()
        for px, py, pc in _other_devices(x, y, c):
            pltpu.make_async_remote_copy(src_ref=g_ref.at[4 * px + 2 * py + pc],
                                         dst_ref=land_ref.at[4 * x + 2 * y + c],
                                         send_sem=send_sem, recv_sem=recv_sem,
                                         device_id=(px, py, pc), device_id_type=MESH).start()

    return _split_start(name, body, grads, grads.shape)


def small_all_gather(small, after):
    def body(small_ref, after_ref, smalls, s_send, s_recv, s_local):
        x, y, c = _position()
        me = 4 * x + 2 * y + c
        lc = pltpu.make_async_copy(small_ref, smalls.at[me], s_local)
        lc.start()
        remote = []
        k = 0
        for dx in (0, 1):
            for dy in (0, 1):
                for dc in (0, 1):
                    if dx + dy + dc == 0:
                        continue
                    peer = (1 - x if dx else x, 1 - y if dy else y, 1 - c if dc else c)
                    rc = pltpu.make_async_remote_copy(
                        src_ref=small_ref, dst_ref=smalls.at[me],
                        send_sem=s_send.at[k], recv_sem=s_recv.at[k],
                        device_id=peer, device_id_type=MESH)
                    rc.start()
                    remote.append(rc)
                    k += 1
        for rc in remote:
            rc.wait()
        lc.wait()

    return pl.pallas_call(
        body,
        out_shape=jax.ShapeDtypeStruct((N_DEV,) + small.shape, small.dtype),
        in_specs=_hbm_specs(2),
        out_specs=pl.BlockSpec(memory_space=pl.ANY),
        scratch_shapes=[pltpu.SemaphoreType.DMA((7,)), pltpu.SemaphoreType.DMA((7,)), pltpu.SemaphoreType.DMA],
        name="small_all_gather",
    )(small, after)


def pair_add(grads, theirs, core, name):
    _, R, C = theirs.shape
    tr = R // 2

    def body(c_ref, a_ref, b_ref, o_ref):
        o_ref[...] = (a_ref[...].astype(F32) + b_ref[...].astype(F32)).astype(BF16)

    return pl.pallas_call(
        body,
        out_shape=jax.ShapeDtypeStruct(theirs.shape, BF16),
        grid_spec=pltpu.PrefetchScalarGridSpec(
            num_scalar_prefetch=1, grid=(4, R // tr),
            in_specs=[pl.BlockSpec((None, tr, C), lambda q, i, c: (2 * q + c[0], i, 0)),
                      pl.BlockSpec((None, tr, C), lambda q, i, c: (q, i, 0))],
            out_specs=pl.BlockSpec((None, tr, C), lambda q, i, c: (q, i, 0))),
        name=name,
        compiler_params=pltpu.CompilerParams(dimension_semantics=("parallel", "parallel"),
                                             vmem_limit_bytes=VMEM_LIMIT),
    )(core, grads, theirs)


def sum_slots(recv, off, rows, blk, name):
    nq, _, C = recv.shape
    ob = off // blk

    def body(r_ref, o_ref):
        acc = r_ref[0].astype(F32)
        for q in range(1, nq):
            acc = acc + r_ref[q].astype(F32)
        o_ref[...] = acc

    return _call(name, body, (rows // blk,),
                 [(recv, (nq, blk, C), lambda i: (0, ob + i, 0))],
                 [((rows, C), F32, (blk, C), lambda i: (i, 0))], sem=("parallel",))[0]


def _sum_terms(refs):
    acc = refs[0][...].astype(F32)
    for r in refs[1:]:
        acc = acc + r[...].astype(F32)
    return acc


def sum_landed(own, land, me, off, rows, blk, name):
    n, _, C = land.shape
    ob = off // blk

    def body(c_ref, *refs):
        refs[n][...] = _sum_terms(refs[:n])

    def entry(flip):
        return pl.BlockSpec((None, blk, C), lambda i, c: (c[0] ^ flip, ob + i, 0))

    return pl.pallas_call(
        body,
        out_shape=jax.ShapeDtypeStruct((rows, C), F32),
        grid_spec=pltpu.PrefetchScalarGridSpec(
            num_scalar_prefetch=1, grid=(rows // blk,),
            in_specs=[entry(k) for k in range(n)],
            out_specs=pl.BlockSpec((blk, C), lambda i, c: (i, 0))),
        name=name,
        compiler_params=pltpu.CompilerParams(dimension_semantics=("parallel",), vmem_limit_bytes=VMEM_LIMIT),
    )(me, own, *([land] * (n - 1)))


def _adamw_update(wv, gv, mv, vv):
    nm = ADAM_B1 * mv + (1.0 - ADAM_B1) * gv
    nv = ADAM_B2 * vv + (1.0 - ADAM_B2) * (gv * gv)
    c1 = 1.0 / (1.0 - ADAM_B1 ** ADAM_STEP)
    c2 = 1.0 / (1.0 - ADAM_B2 ** ADAM_STEP)
    return -ADAM_LR * ((nm * c1) / (jnp.sqrt(nv * c2) + ADAM_EPS) + ADAM_WD * wv), nm, nv


def sum_adamw(own, land, me, off, blk, w, m, v, name):
    rows, C = w.shape
    n = land.shape[0]
    ob = off // blk

    def body(c_ref, *refs):
        w_ref, m_ref, v_ref, g_out, d_out, m_out, v_out = refs[n:]
        gv = _sum_terms(refs[:n])
        g_out[...] = gv
        d_out[...], m_out[...], v_out[...] = _adamw_update(w_ref[...], gv, m_ref[...], v_ref[...])

    def entry(flip):
        return pl.BlockSpec((None, blk, C), lambda i, c: (c[0] ^ flip, ob + i, 0))

    plain = pl.BlockSpec((blk, C), lambda i, c: (i, 0))
    return pl.pallas_call(
        body,
        out_shape=[jax.ShapeDtypeStruct((rows, C), F32)] * 4,
        grid_spec=pltpu.PrefetchScalarGridSpec(
            num_scalar_prefetch=1, grid=(rows // blk,),
            in_specs=[entry(k) for k in range(n)] + [plain, plain, plain],
            out_specs=[plain] * 4),
        name=name,
        compiler_params=pltpu.CompilerParams(dimension_semantics=("parallel",), vmem_limit_bytes=VMEM_LIMIT),
    )(me, own, *([land] * (n - 1)), w, m, v)


def adamw(w, g, m, v, name):
    R, C = w.shape
    tr = R
    for cand in (256, 128, 64, 32, 16, 8):
        if R % cand == 0 and R > cand:
            tr = cand
            break

    def body(w_ref, g_ref, m_ref, v_ref, d_ref, nm_ref, nv_ref):
        d_ref[...], nm_ref[...], nv_ref[...] = _adamw_update(w_ref[...], g_ref[...], m_ref[...], v_ref[...])

    spec = ((tr, C), lambda i: (i, 0))
    out = ((R, C), F32) + spec
    return _call(name, body, (R // tr,), [(w,) + spec, (g,) + spec, (m,) + spec, (v,) + spec],
                 [out, out, out], sem=("parallel",))


def _rms_tile(xv, gv):
    r = lax.rsqrt(jnp.mean(xv * xv, axis=-1, keepdims=True) + EPS)
    return (xv * r * gv).astype(BF16)


def rms_fwd(x, g, name):
    S, D = x.shape
    tr = 512

    def body(x_ref, g_ref, o_ref):
        o_ref[...] = _rms_tile(x_ref[...], g_ref[...])

    return _call(name, body, (S // tr,),
                 [(x, (tr, D), lambda i: (i, 0)), (g, (1, D), lambda i: (0, 0))],
                 [((S, D), BF16, (tr, D), lambda i: (i, 0))], sem=("parallel",))[0]


def _rms_bwd_tile(dn, xv, gv):
    r = lax.rsqrt(jnp.mean(xv * xv, axis=-1, keepdims=True) + EPS)
    xh = xv * r
    dxh = dn * gv
    dx = r * (dxh - xh * jnp.mean(dxh * xh, axis=-1, keepdims=True))
    return dx, dn * xh


def _final_loss_tile(xv, tv, gv):
    D = xv.shape[1]
    r = lax.rsqrt(jnp.mean(xv * xv, axis=-1, keepdims=True) + EPS)
    xh = xv * r
    e = xh * gv - tv
    part = 0.5 * jnp.sum(jnp.sum(e * e, axis=-1, keepdims=True) * (1.0 / D), axis=0, keepdims=True)
    dy = e * (1.0 / D)
    dxh = dy * gv
    dx = r * (dxh - xh * jnp.mean(dxh * xh, axis=-1, keepdims=True))
    return part, dx, jnp.sum(dy * xh, axis=0, keepdims=True)


FFN_TF = 4 * FFN_SHARD


def _ffn_pick(G, which):
    if isinstance(G, tuple):
        return (G[0], which) if which < 2 else (G[1], 0)
    return G, which


def _ffn_w_spec(G, which, imap):
    arr, blk = _ffn_pick(G, which)
    return (arr, (4, FFN_SHARD, arr.shape[2]), lambda *idx: (imap(*idx), blk, 0))


def _ffn_whole_w_spec(G, which):
    arr, blk = _ffn_pick(G, which)
    return (arr, (N_DEV, FFN_SHARD, arr.shape[2]), lambda *idx: (0, blk, 0), pl.Buffered(1))


def _ffn_hidden(a, b):
    av, bv = a.astype(F32), b.astype(F32)
    return (av * _sigmoid(av) * bv).astype(BF16)


def ffn_up(n, G, name):
    S, D = n.shape
    F = N_DEV * FFN_SHARD
    tm = 256

    def body(n_ref, w1_ref, w3_ref, abh_ref):
        nv = n_ref[...]
        a = _dot(nv, w1_ref[...].reshape(F, D), 1, 1).astype(BF16)
        b = _dot(nv, w3_ref[...].reshape(F, D), 1, 1).astype(BF16)
        abh_ref[0] = a
        abh_ref[1] = b
        abh_ref[2] = _ffn_hidden(a, b)

    return _call(name, body, (S // tm,),
                 [(n, (tm, D), lambda i: (i, 0)),
                  _ffn_whole_w_spec(G, 0), _ffn_whole_w_spec(G, 1)],
                 [((3, S, F), BF16, (3, tm, F), lambda i: (0, i, 0))],
                 sem=("parallel",))[0]


def ffn_down(abh, G, x, g_next, name):
    _, S, F = abh.shape
    D = x.shape[1]
    tm = 512

    def body(h_ref, w2_ref, x_ref, g_ref, o_ref, n_ref):
        xo = x_ref[...] + 0.5 * _dot(h_ref[...], w2_ref[...].reshape(F, D))
        o_ref[...] = xo
        n_ref[...] = _rms_tile(xo, g_ref[...])

    tile = ((tm, D), lambda i: (i, 0))
    return _call(name, body, (S // tm,),
                 [(abh, (None, tm, F), lambda i: (2, i, 0)), _ffn_whole_w_spec(G, 2),
                  (x,) + tile, (g_next, (1, D), lambda i: (0, 0))],
                 [((S, D), F32) + tile, ((S, D), BF16) + tile], sem=("parallel",))


def ffn_last(x, g, G, tgt, g_final, name):
    S, D = x.shape
    F = N_DEV * FFN_SHARD
    tm = 256

    def body(x_ref, g_ref, w1_ref, w3_ref, w2_ref, t_ref, gf_ref,
             n_ref, abh_ref, dxo_ref, dab_ref, dx_ref, dxb_ref, dg_ref, l_ref, dgf_ref):
        i = pl.program_id(0)
        xv, gv = x_ref[...], g_ref[...]
        chunks = [(slice(4 * f, 4 * f + 4), slice(f * FFN_TF, (f + 1) * FFN_TF)) for f in range(F // FFN_TF)]
        weight = lambda w_ref, slots: w_ref[slots].reshape(FFN_TF, D)
        nv = _rms_tile(xv, gv)
        n_ref[...] = nv
        y = None
        for slots, cols in chunks:
            a = _dot(nv, weight(w1_ref, slots), 1, 1).astype(BF16)
            b = _dot(nv, weight(w3_ref, slots), 1, 1).astype(BF16)
            h = _ffn_hidden(a, b)
            abh_ref[0, :, cols] = a
            abh_ref[1, :, cols] = b
            abh_ref[2, :, cols] = h
            t = _dot(h, weight(w2_ref, slots))
            y = t if y is None else y + t
        part, dxo, dgfp = _final_loss_tile(xv + 0.5 * y, t_ref[...], gf_ref[...])
        dxo_b = dxo.astype(BF16)
        dxo_ref[...] = dxo_b
        dn = None
        for slots, cols in chunks:
            dh = 0.5 * _dot(dxo_b, weight(w2_ref, slots), 1, 1)
            da, db = _ffn_hidden_grads(dh, abh_ref[0, :, cols].astype(F32), abh_ref[1, :, cols].astype(F32))
            da, db = da.astype(BF16), db.astype(BF16)
            dab_ref[0, :, cols] = da
            dab_ref[1, :, cols] = db
            t = _dot(da, weight(w1_ref, slots)) + _dot(db, weight(w3_ref, slots))
            dn = t if dn is None else dn + t
        dx, dgt = _rms_bwd_tile(dn, xv, gv)
        dx = dxo + dx
        dx_ref[...] = dx
        dxb_ref[...] = dx.astype(BF16)
        dgp = jnp.sum(dgt, axis=0, keepdims=True)

        @pl.when(i == 0)
        def _():
            dg_ref[...] = dgp
            l_ref[...] = jnp.broadcast_to(part, l_ref.shape)
            dgf_ref[...] = dgfp

        @pl.when(i > 0)
        def _():
            dg_ref[...] += dgp
            l_ref[...] += jnp.broadcast_to(part, l_ref.shape)
            dgf_ref[...] += dgfp

    tile = ((tm, D), lambda i: (i, 0))
    gain = ((1, D), lambda i: (0, 0))
    return _call(name, body, (S // tm,),
                 [(x,) + tile, (g,) + gain,
                  _ffn_whole_w_spec(G, 0), _ffn_whole_w_spec(G, 1), _ffn_whole_w_spec(G, 2),
                  (tgt,) + tile, (g_final,) + gain],
                 [((S, D), BF16) + tile, ((3, S, F), BF16, (3, tm, F), lambda i: (0, i, 0)),
                  ((S, D), BF16) + tile, ((2, S, F), BF16, (2, tm, F), lambda i: (0, i, 0)),
                  ((S, D), F32) + tile, ((S, D), BF16) + tile, ((1, D), F32) + gain,
                  ((1, LANES), F32, (1, LANES), lambda i: (0, 0)), ((1, D), F32) + gain],
                 sem=("arbitrary",))


def _ffn_hidden_grads(dh, av, bv):
    sig = _sigmoid(av)
    return dh * bv * (sig * (1.0 + av * (1.0 - sig))), dh * (av * sig)


def ffn_bwd_hidden(dxo, abh, G, name):
    _, S, F = abh.shape
    D = dxo.shape[1]
    tm = 256

    def body(d_ref, w2_ref, ab_ref, o_ref):
        dh = 0.5 * _dot(d_ref[...].astype(BF16), w2_ref[...].reshape(F, D), 1, 1)
        da, db = _ffn_hidden_grads(dh, ab_ref[0].astype(F32), ab_ref[1].astype(F32))
        o_ref[0] = da.astype(BF16)
        o_ref[1] = db.astype(BF16)

    return _call(name + "_down_bwd", body, (S // tm,),
                 [(dxo, (tm, D), lambda i: (i, 0)), _ffn_whole_w_spec(G, 2),
                  (abh, (2, tm, F), lambda i: (0, i, 0))],
                 [((2, S, F), BF16, (2, tm, F), lambda i: (0, i, 0))],
                 sem=("parallel",))[0]


def ffn_bwd_weights(dxo, abh, dab, n, name):
    _, S, F = abh.shape
    D = dxo.shape[1]
    nf = F // FFN_TF
    tk = WGRAD_TK
    nk = S // tk
    gshape = (N_DEV, 3 * FFN_SHARD, D)

    def dw2_body(h_ref, d_ref, o_ref, acc_ref):
        k = pl.program_id(1)
        p = _dot(h_ref[...], d_ref[...].astype(BF16), 0, 0)

        @pl.when(k == 0)
        def _():
            acc_ref[...] = p

        @pl.when(k > 0)
        def _():
            acc_ref[...] += p

        @pl.when(k == nk - 1)
        def _():
            o_ref[...] = (0.5 * acc_ref[...]).astype(BF16).reshape(4, FFN_SHARD, D)

    gw = _call(name + "_dw2", dw2_body, (nf, nk),
               [(abh, (None, tk, FFN_TF), lambda j, k: (2, k, j)), (dxo, (tk, D), lambda j, k: (k, 0))],
               [(gshape, BF16, (4, FFN_SHARD, D), lambda j, k: (j, 2, 0))],
               scratch=[pltpu.VMEM((FFN_TF, D), F32)], sem=("parallel", "arbitrary"))[0]

    def dw13_body(gw_ref, dab_ref, n_ref, o_ref):
        o_ref[...] = _dot(dab_ref[...], n_ref[...], 0, 0).astype(BF16).reshape(4, FFN_SHARD, D)

    gw = pl.pallas_call(
        dw13_body,
        out_shape=jax.ShapeDtypeStruct(gshape, BF16),
        grid=(2, nf),
        in_specs=[pl.BlockSpec(memory_space=pl.ANY),
                  pl.BlockSpec((None, S, FFN_TF), lambda w, j: (w, 0, j)),
                  pl.BlockSpec((S, D), lambda w, j: (0, 0))],
        out_specs=pl.BlockSpec((4, FFN_SHARD, D), lambda w, j: (j, w, 0)),
        input_output_aliases={0: 0},
        name=name + "_dw13",
        compiler_params=pltpu.CompilerParams(dimension_semantics=("parallel", "parallel"),
                                             vmem_limit_bytes=VMEM_LIMIT),
    )(gw, dab, n)
    return gw


def ffn_bwd_input(dab, G, x_in, g, dxo, name):
    _, S, F = dab.shape
    D = x_in.shape[1]
    tm = 256

    def dn_body(dab_ref, w1_ref, w3_ref, x_ref, d_ref, g_ref, dx_ref, dg_ref):
        i = pl.program_id(0)
        dn = _dot(dab_ref[0], w1_ref[...].reshape(F, D)) + _dot(dab_ref[1], w3_ref[...].reshape(F, D))
        dx, dgt = _rms_bwd_tile(dn, x_ref[...], g_ref[...])
        dx_ref[...] = d_ref[...] + dx
        dgp = jnp.sum(dgt, axis=0, keepdims=True)

        @pl.when(i == 0)
        def _():
            dg_ref[...] = dgp

        @pl.when(i > 0)
        def _():
            dg_ref[...] += dgp

    tile = ((tm, D), lambda i: (i, 0))
    return _call(name + "_dn", dn_body, (S // tm,),
                 [(dab, (2, tm, F), lambda i: (0, i, 0)),
                  _ffn_whole_w_spec(G, 0), _ffn_whole_w_spec(G, 1),
                  (x_in,) + tile, (dxo,) + tile, (g, (1, D), lambda i: (0, 0))],
                 [((S, D), F32) + tile, ((1, D), F32, (1, D), lambda i: (0, 0))],
                 sem=("arbitrary",))


PROJ_TN = 512
DH_SHARDS_PER_STEP = 4


def in_proj(h, Gm, name):
    S, D = h.shape
    n_tiles = N_DEV * Gm.shape[2] // PROJ_TN

    def body(h_ref, w_ref, o_ref):
        o_ref[...] = _dot(h_ref[...], w_ref[...]).astype(BF16)

    return _call(name, body, (n_tiles,),
                 [(h, (S, D), lambda j: (0, 0)),
                  (Gm, (None, D, PROJ_TN), lambda j: (j // 2, 0, j % 2))],
                 [((S, n_tiles * PROJ_TN), BF16, (S, PROJ_TN), lambda j: (0, j))],
                 sem=("parallel",))[0]


def _dproj_pieces(dqkv, dq_b, dkv_b, dgate):
    pieces = [(dqkv[g], [(3 * which + g, (which, 0)) for which in range(3)]) for g in range(3)]
    pieces.append((dq_b, [(9, (None, 0)), (10, (None, 1))]))
    pieces.append((dkv_b, [(11, (None, 0))]))
    pieces.append((dgate, [(12 + 2 * a + b, (a, b)) for a in range(2) for b in range(2)]))
    return pieces


def in_proj_bwd_dw(pieces, h, gm_grads, name):
    S, D = h.shape
    steps = [(n, t, ix) for n, (_, tiles) in enumerate(pieces) for t, ix in tiles]
    n_steps = len(steps)

    def pick(table, j):
        out = table[-1]
        for k in range(len(table) - 2, -1, -1):
            out = jnp.where(j == k, table[k], out)
        return out

    def piece_spec(n, arr):
        own = [k for k, (m, _, _) in enumerate(steps) if m == n]
        at = [steps[min(max(k, own[0]), own[-1])][2] for k in range(n_steps)]
        lead, colb = [ix[0] for ix in at], [ix[1] for ix in at]
        if arr.ndim == 3:
            return (own[0], own[-1]), pl.BlockSpec((None, S, PROJ_TN), lambda j: (pick(lead, j), 0, pick(colb, j)))
        return (own[0], own[-1]), pl.BlockSpec((S, PROJ_TN), lambda j: (0, pick(colb, j)))

    spans, d_specs = zip(*[piece_spec(n, arr) for n, (arr, _) in enumerate(pieces)])
    w_tile = [t for _, t, _ in steps]

    def dw_body(gm_ref, h_ref, *refs):
        o_ref = refs[-1]
        j = pl.program_id(0)
        for d_ref, (first, last) in zip(refs[:-1], spans):
            @pl.when((j >= first) & (j <= last))
            def _(d_ref=d_ref):
                o_ref[...] = _dot(h_ref[...], d_ref[...], 0, 0).astype(BF16)

    return pl.pallas_call(
        dw_body,
        out_shape=jax.ShapeDtypeStruct(gm_grads.shape, BF16),
        grid=(n_steps,),
        in_specs=[pl.BlockSpec(memory_space=pl.ANY),
                  pl.BlockSpec((S, D), lambda j: (0, 0), pipeline_mode=pl.Buffered(1))] + list(d_specs),
        out_specs=pl.BlockSpec((None, D, PROJ_TN), lambda j: (pick(w_tile, j) // 2, 0, pick(w_tile, j) % 2)),
        input_output_aliases={0: 0},
        name=name + "_dw",
        compiler_params=pltpu.CompilerParams(dimension_semantics=("arbitrary",), vmem_limit_bytes=VMEM_LIMIT),
    )(gm_grads, h, *[arr for arr, _ in pieces])


def in_proj_bwd_dh(pieces, Gm, x_in, g, dres, name):
    S, D = x_in.shape
    tm = 256
    C = Gm.shape[2]
    n_sh = N_DEV
    n_p = len(pieces)

    def dh_body(*refs):
        d_refs = refs[:n_p]
        w_ref, x_ref, r_ref, g_ref, dx_ref, dxb_ref, dg_ref = refs[n_p:]
        i = pl.program_id(0)
        p = None
        for d_ref, (arr, tiles) in zip(d_refs, pieces):
            for t, (lead, colb) in tiles:
                cols = slice(colb * PROJ_TN, (colb + 1) * PROJ_TN)
                d = d_ref[:, cols] if lead is None else d_ref[lead, :, cols]
                wcol = (t % 2) * PROJ_TN
                term = _dot(d, w_ref[t // 2, :, wcol:wcol + PROJ_TN], 1, 1)
                p = term if p is None else p + term
        dx, dgt = _rms_bwd_tile(p, x_ref[...], g_ref[...])
        dx = r_ref[...] + dx
        dx_ref[...] = dx
        dxb_ref[...] = dx.astype(BF16)
        dgp = jnp.sum(dgt, axis=0, keepdims=True)

        @pl.when(i == 0)
        def _():
            dg_ref[...] = dgp

        @pl.when(i > 0)
        def _():
            dg_ref[...] += dgp

    tile = ((tm, D), lambda i: (i, 0))

    def rows_of(arr):
        if arr.ndim == 3:
            return (arr, (arr.shape[0], tm, arr.shape[2]), lambda i: (0, i, 0))
        return (arr, (tm, arr.shape[1]), lambda i: (i, 0))

    return _call(name + "_dh", dh_body, (S // tm,),
                 [rows_of(arr) for arr, _ in pieces]
                 + [(Gm, (n_sh, D, C), lambda i: (0, 0, 0), pl.Buffered(1)),
                    (x_in,) + tile, (dres,) + tile, (g, (1, D), lambda i: (0, 0))],
                 [((S, D), F32) + tile, ((S, D), BF16) + tile, ((1, D), F32, (1, D), lambda i: (0, 0))],
                 sem=("arbitrary",))


def _t5_bucket(rel):
    n = N_BUCKETS // 2
    max_exact = n // 2
    ret = jnp.where(rel > 0, n, 0)
    a = jnp.abs(rel)
    af = jnp.maximum(a, 1).astype(F32)
    large = max_exact + (jnp.log(af / max_exact) / math.log(MAX_DISTANCE / max_exact)
                         * (n - max_exact)).astype(jnp.int32)
    large = jnp.minimum(large, n - 1)
    return ret + jnp.where(a < max_exact, a, large)


def _bucket_tables():
    qi = jnp.arange(A_TQ, dtype=jnp.int32)[:, None]
    kj = jnp.arange(A_WIN, dtype=jnp.int32)[None, :]
    rel = kj - HALF_WINDOW - qi
    return jnp.stack([_t5_bucket(rel * d) for d in DILATIONS], axis=0)


def bias_build(rel_bias, buckets):
    def body(tab_ref, bk_ref, o_ref):
        col = pl.program_id(0) * HEADS_PER_GROUP_A + pl.program_id(1)
        bk = bk_ref[...]
        acc = jnp.zeros(bk.shape, F32)
        for b in range(N_BUCKETS):
            acc = jnp.where(bk == b, tab_ref[b, col], acc)
        qi = lax.broadcasted_iota(jnp.int32, bk.shape, 0)
        kj = lax.broadcasted_iota(jnp.int32, bk.shape, 1)
        band = jnp.where(jnp.abs(kj - HALF_WINDOW - qi) <= HALF_WINDOW, acc, NEG_INF)
        o_ref[0] = jnp.where(kj >= HALF_WINDOW, band, NEG_INF)
        o_ref[1] = band
        o_ref[2] = jnp.where(kj < A_TQ + HALF_WINDOW, band, NEG_INF)

    out = pl.pallas_call(
        body,
        out_shape=jax.ShapeDtypeStruct((3, HEADS_PER_GROUP_A // 2, 3, 2, A_TQ, A_WIN), F32),
        grid=(3, HEADS_PER_GROUP_A),
        in_specs=[pl.BlockSpec(memory_space=pltpu.SMEM),
                  pl.BlockSpec((None, A_TQ, A_WIN), lambda g, h: (g, 0, 0))],
        out_specs=pl.BlockSpec((None, None, 3, None, A_TQ, A_WIN), lambda g, h: (g, h // 2, 0, h % 2, 0, 0)),
        name="a_bias_build",
        compiler_params=pltpu.CompilerParams(dimension_semantics=("parallel", "parallel")),
    )(rel_bias, buckets)
    return out.reshape(3, HEADS_PER_GROUP_A // 2, 3, 2 * A_TQ, A_WIN)


def bias_bwd(dbias, buckets):
    def body(d_ref, bk_ref, o_ref):
        bk = bk_ref[...]
        dv = d_ref[...]
        for b in range(N_BUCKETS):
            part = jnp.sum(jnp.where(bk == b, dv, 0.0), axis=1, keepdims=True)
            o_ref[b:b + 1, :] = jnp.broadcast_to(jnp.sum(part, axis=0, keepdims=True), (1, LANES))

    out = pl.pallas_call(
        body,
        out_shape=jax.ShapeDtypeStruct((3, HEADS_PER_GROUP_A, N_BUCKETS, LANES), F32),
        grid=(3, HEADS_PER_GROUP_A),
        in_specs=[pl.BlockSpec((None, None, A_TQ, A_WIN), lambda g, h: (g, h, 0, 0)),
                  pl.BlockSpec((None, A_TQ, A_WIN), lambda g, h: (g, 0, 0))],
        out_specs=pl.BlockSpec((None, None, N_BUCKETS, LANES), lambda g, h: (g, h, 0, 0)),
        name="a_bias_bwd",
        compiler_params=pltpu.CompilerParams(dimension_semantics=("parallel", "parallel")),
    )(dbias, buckets)
    return out[:, :, :, 0].transpose(2, 0, 1).reshape(N_BUCKETS, 3 * HEADS_PER_GROUP_A)


def _a_fill_padded(pad_ref, src_ref, n, pad):
    zeros = jnp.zeros((pad, LANES), pad_ref.dtype)
    pad_ref[0:pad, :] = zeros
    pad_ref[pad + n:2 * pad + n, :] = zeros
    pad_ref[pad:pad + n, :] = src_ref[...].astype(pad_ref.dtype)


def _a_stack_heads(x, lane):
    zero = jnp.zeros_like(x)
    return jnp.concatenate([jnp.where(lane < HEAD_DIM_A, x, zero), jnp.where(lane >= HEAD_DIM_A, x, zero)], axis=0)


def _a_bias_variant(qb, nqb):
    return jnp.where(qb == 0, 0, jnp.where(qb == nqb - 1, 2, 1))


def _a_slab_specs(proj, g):
    S = proj.shape[0]
    per = GROUP_WIDTH_A // LANES
    return [(proj, (S, LANES), lambda hp, w=w: (0, per * (3 * w + g) + hp)) for w in range(3)]


def a_fwd(proj, bias, g, name):
    S = proj.shape[0]
    d = DILATIONS[g]
    L = S // d
    nqb = L // A_TQ
    pad = HALF_WINDOW * d

    def body(q_ref, k_ref, v_ref, b_ref, o_ref, l_ref, qf, kpad, vpad):
        qf[...] = q_ref[...].astype(F32) * A_SCALE
        _a_fill_padded(kpad, k_ref, S, pad)
        _a_fill_padded(vpad, v_ref, S, pad)
        lane = lax.broadcasted_iota(jnp.int32, (A_TQ, LANES), 1)

        def block(t, carry):
            qb, r = t // d, t % d
            start = qb * (A_TQ * d) + r
            kw = kpad[pl.ds(start, A_WIN, stride=d), :].astype(BF16)
            vw = vpad[pl.ds(start, A_WIN, stride=d), :].astype(BF16)
            q = qf[pl.ds(start, A_TQ, stride=d), :].astype(BF16)
            q2 = _a_stack_heads(q, lane)
            s = _dot(q2, kw, 1, 1) + b_ref[_a_bias_variant(qb, nqb)]
            m = jnp.max(s, axis=-1, keepdims=True)
            e = jnp.exp(s - m)
            l = jnp.sum(e, axis=-1, keepdims=True)
            o2 = _dot(e.astype(BF16), vw) / l
            lse2 = m + jnp.log(l)
            o_ref[pl.ds(start, A_TQ, stride=d), :] = jnp.where(lane < HEAD_DIM_A, o2[0:A_TQ], o2[A_TQ:])
            l_ref[pl.ds(start, A_TQ, stride=d), :] = jnp.where(lane < HEAD_DIM_A, lse2[0:A_TQ], lse2[A_TQ:])
            return carry

        lax.fori_loop(0, nqb * d, block, 0, unroll=A_UNROLL)

    out_spec = ((S, GROUP_WIDTH_A), F32, (S, LANES), lambda hp: (0, hp))
    return _call(name, body, (4,),
                 _a_slab_specs(proj, g)
                 + [(bias, (None, None, 3, 2 * A_TQ, A_WIN), lambda hp: (g, hp, 0, 0, 0))],
                 [out_spec, out_spec],
                 scratch=[pltpu.VMEM((S, LANES), F32)] + [pltpu.VMEM((S + 2 * pad, LANES), F32)] * 2,
                 sem=("parallel",))


def a_combine(outs, lses, name):
    S, W = outs[0].shape
    tr = 512

    def body(o0, o1, o2, l0, l1, l2, oa_ref, lt_ref):
        a, b, c = l0[...], l1[...], l2[...]
        m = jnp.maximum(jnp.maximum(a, b), c)
        ea, eb, ec = jnp.exp(a - m), jnp.exp(b - m), jnp.exp(c - m)
        z = ea + eb + ec
        oa_ref[...] = ((ea * o0[...] + eb * o1[...] + ec * o2[...]) / z).astype(BF16)
        lt_ref[...] = m + jnp.log(z)

    spec = ((tr, W), lambda i: (i, 0))
    return _call(name, body, (S // tr,), [(a,) + spec for a in (*outs, *lses)],
                 [((S, W), BF16) + spec, ((S, W), F32) + spec], sem=("parallel",))


def a_bwd(proj, bias, do_a, o_a, lse_tot, g, name):
    S = proj.shape[0]
    d = DILATIONS[g]
    L = S // d
    nqb = L // A_TQ
    pad = HALF_WINDOW * d

    def body(q_ref, k_ref, v_ref, b_ref, do_ref, o_ref, l_ref, dqkv_ref, db_ref,
             qf, of, dqf, kpad, vpad, dkacc, dvacc):
        qf[...] = q_ref[...].astype(F32) * A_SCALE
        of[...] = o_ref[...].astype(F32)
        _a_fill_padded(kpad, k_ref, S, pad)
        _a_fill_padded(vpad, v_ref, S, pad)
        dkacc[...] = jnp.zeros(dkacc.shape, F32)
        dvacc[...] = jnp.zeros(dvacc.shape, F32)
        db_ref[...] = jnp.zeros(db_ref.shape, F32)
        lane = lax.broadcasted_iota(jnp.int32, (A_TQ, LANES), 1)

        def block(t, carry):
            qb, r = t // d, t % d
            start = qb * (A_TQ * d) + r
            rows = pl.ds(start, A_TQ, stride=d)
            win = pl.ds(start, A_WIN, stride=d)
            kw = kpad[win, :].astype(BF16)
            vw = vpad[win, :].astype(BF16)
            q = qf[rows, :].astype(BF16)
            do = do_ref[rows, :]
            ov = of[rows, :]
            lt = l_ref[rows, :]
            q2 = _a_stack_heads(q, lane)
            do2 = _a_stack_heads(do, lane)
            lt2 = jnp.concatenate([lt[:, 0:1], lt[:, HEAD_DIM_A:HEAD_DIM_A + 1]], axis=0)
            s = _dot(q2, kw, 1, 1) + b_ref[_a_bias_variant(qb, nqb)]
            p = jnp.exp(s - lt2)
            t = jnp.sum(do2 * jnp.concatenate([ov, ov], axis=0), axis=-1, keepdims=True)
            dob2 = do2.astype(BF16)
            ds = p * (_dot(dob2, vw, 1, 1) - t)
            db_ref[...] += ds
            dsb = ds.astype(BF16)
            dq2 = _dot(dsb, kw)
            dqf[rows, :] = jnp.where(lane < HEAD_DIM_A, dq2[0:A_TQ], dq2[A_TQ:]) * A_SCALE
            dkacc[win, :] += _dot(dsb, q2, 0, 0)
            dvacc[win, :] += _dot(p.astype(BF16), dob2, 0, 0)
            return carry

        lax.fori_loop(0, nqb * d, block, 0, unroll=A_UNROLL)
        dqkv_ref[0] = dqf[...].astype(BF16)
        dqkv_ref[1] = dkacc[pad:pad + S, :].astype(BF16)
        dqkv_ref[2] = dvacc[pad:pad + S, :].astype(BF16)

    slab = ((S, LANES), lambda hp: (0, hp))
    padded = pltpu.VMEM((S + 2 * pad, LANES), F32)
    return _call(
        name, body, (4,),
        _a_slab_specs(proj, g)
        + [(bias, (None, None, 3, 2 * A_TQ, A_WIN), lambda hp: (g, hp, 0, 0, 0)),
           (do_a,) + slab, (o_a,) + slab, (lse_tot,) + slab],
        [((3, S, GROUP_WIDTH_A), BF16, (3, S, LANES), lambda hp: (0, 0, hp)),
         ((4, 2 * A_TQ, A_WIN), F32, (None, 2 * A_TQ, A_WIN), lambda hp: (hp, 0, 0))],
        scratch=[pltpu.VMEM((S, LANES), F32)] * 3 + [padded] * 4,
        sem=("parallel",))


def _rope_tables(S):
    rows = S // GRID_W
    row = jnp.repeat(jnp.arange(rows, dtype=F32), GRID_W)
    col = jnp.tile(jnp.arange(GRID_W, dtype=F32), rows)
    n_freq = HEAD_DIM_B // 4
    freq = ROPE_THETA ** (-jnp.arange(n_freq, dtype=F32) / n_freq)
    ang = jnp.concatenate([row[:, None] * freq, col[:, None] * freq], axis=-1)
    cos, sin = jnp.cos(ang), jnp.sin(ang)
    return jnp.repeat(cos, 2, axis=-1), jnp.stack([-sin, sin], axis=-1).reshape(S, HEAD_DIM_B)


def _swap_pairs(y):
    lane = lax.broadcasted_iota(jnp.int32, y.shape, 1)
    return jnp.where(lane % 2 == 0, pltpu.roll(y, LANES - 1, 1), pltpu.roll(y, 1, 1))


def qkv_prep(proj, gains, cos_t, sin_t, name):
    S = proj.shape[0]
    ts = 256
    n_rot = N_HEADS_B + N_KV_B
    nh = n_rot + N_KV_B
    W = nh * LANES

    def body(x_ref, g_ref, c_ref, s_ref, o_ref):
        cv, sv = c_ref[...], s_ref[...]
        for hb in range(nh):
            cols = slice(hb * LANES, (hb + 1) * LANES)
            if hb < n_rot:
                xv = x_ref[:, cols].astype(F32)
                r = lax.rsqrt(jnp.mean(xv * xv, axis=-1, keepdims=True) + EPS)
                yv = xv * r * g_ref[:, cols]
                o_ref[:, cols] = (yv * cv + _swap_pairs(yv) * sv).astype(BF16)
            else:
                o_ref[:, cols] = x_ref[:, cols]

    return _call(name, body, (S // ts,),
                 [(proj, (ts, W), lambda i: (i, A_QKV_WIDTH // W)), (gains, (1, W), lambda i: (0, 0)),
                  (cos_t, (ts, LANES), lambda i: (i, 0)), (sin_t, (ts, LANES), lambda i: (i, 0))],
                 [((S, W), BF16, (ts, W), lambda i: (i, 0))],
                 sem=("parallel",))[0]


def qk_prep_bwd(dr, proj, col0, gain, cos_t, sin_t, name):
    S, W = dr.shape
    H = W // LANES
    ts = 256
    wx = math.gcd(W, col0)
    n_x = W // wx

    def body(d_ref, *refs):
        x_refs = refs[:n_x]
        g_ref, c_ref, s_ref, dx_ref, dg_ref = refs[n_x:]
        i = pl.program_id(0)
        cv, sv, gv = c_ref[...], s_ref[...], g_ref[...]
        dgp = jnp.zeros((1, LANES), F32)
        for hb in range(H):
            cols = slice(hb * LANES, (hb + 1) * LANES)
            xc = (hb * LANES) % wx
            xv = x_refs[(hb * LANES) // wx][:, xc:xc + LANES].astype(F32)
            dout = d_ref[:, cols]
            dy = dout * cv + _swap_pairs(dout * sv)
            dx, dgt = _rms_bwd_tile(dy, xv, gv)
            dx_ref[:, cols] = dx.astype(BF16)
            dgp = dgp + jnp.sum(dgt, axis=0, keepdims=True)

        @pl.when(i == 0)
        def _():
            dg_ref[...] = dgp

        @pl.when(i > 0)
        def _():
            dg_ref[...] += dgp

    return _call(name, body, (S // ts,),
                 [(dr, (ts, W), lambda i: (i, 0))]
                 + [(proj, (ts, wx), lambda i, k=k: (i, col0 // wx + k)) for k in range(n_x)]
                 + [(gain, (1, LANES), lambda i: (0, 0)),
                  (cos_t, (ts, LANES), lambda i: (i, 0)), (sin_t, (ts, LANES), lambda i: (i, 0))],
                 [((S, W), BF16, (ts, W), lambda i: (i, 0)),
                  ((1, LANES), F32, (1, LANES), lambda i: (0, 0))],
                 sem=("arbitrary",))


def _row_sums(x):
    hi = x.astype(BF16)
    lo = (x - hi.astype(F32)).astype(BF16)
    ones = jnp.ones((8, LANES), BF16)
    return (_dot(ones, hi, 1, 1) + _dot(ones, lo, 1, 1))[0:1, :]


def flash_fwd(qkv, name):
    S = qkv.shape[0]
    tq = B_TQ_FWD
    scale = HEAD_DIM_B ** -0.5

    hps = B_HEADS_PER_STEP

    def body(q_ref, k_ref, v_ref, o_ref, l_ref):
        k, v = k_ref[...], v_ref[...]
        for j in range(hps):
            cols = slice(j * LANES, (j + 1) * LANES)
            s = _dot(q_ref[:, cols], k, 1, 1)
            m = jnp.max(s, axis=-1, keepdims=True)
            e = jnp.exp2((s - m) * (scale * LOG2E))
            l = jnp.sum(e, axis=-1, keepdims=True)
            o_ref[:, cols] = (_dot(e.astype(BF16), v) / l).astype(BF16)
            lse = jnp.broadcast_to(m * scale + jnp.log(l), (tq, LANES))
            l_ref[j] = _row_sums(lse) * (1.0 / LANES)

    per = GQA_GROUP_B // hps
    heads = lambda g, h, i: (i, g * per + h)
    return _call(name, body, (N_KV_B, per, S // tq),
                 [(qkv, (tq, hps * LANES), heads),
                  (qkv, (S, LANES), lambda g, h, i: (0, N_HEADS_B + g)),
                  (qkv, (S, LANES), lambda g, h, i: (0, N_HEADS_B + N_KV_B + g))],
                 [((S, N_HEADS_B * LANES), BF16, (tq, hps * LANES), heads),
                  ((N_HEADS_B, 1, S), F32, (hps, 1, tq), lambda g, h, i: (g * per + h, 0, i))],
                 sem=("parallel", "parallel", "parallel"))


def flash_bwd(qkv, k_t, do_b, o_b, lse, name):
    S = qkv.shape[0]
    tq = B_TQ_BWD
    nq = S // tq
    scale = HEAD_DIM_B ** -0.5

    def body(q_ref, k_ref, v_ref, kt_ref, do_ref, o_ref, l_ref, dq_ref, dk_ref, dv_ref, dkacc, dvacc):
        h, i = pl.program_id(1), pl.program_id(2)

        @pl.when((h == 0) & (i == 0))
        def _():
            dkacc[...] = jnp.zeros(dkacc.shape, F32)
            dvacc[...] = jnp.zeros(dvacc.shape, F32)

        q = q_ref[...]
        dob = do_ref[...]
        t = _row_sums(dob.astype(F32) * o_ref[...].astype(F32))
        pt = jnp.exp2(_dot(k_ref[...], q, 1, 1) * (scale * LOG2E) - l_ref[...] * LOG2E)
        dsb = (pt * (_dot(v_ref[...], dob, 1, 1) - t)).astype(BF16)
        dvacc[...] += _dot(pt.astype(BF16), dob)
        dkacc[...] += _dot(dsb, q)
        dq_ref[...] = _dot(kt_ref[...], dsb).T * scale

        @pl.when((h == GQA_GROUP_B - 1) & (i == nq - 1))
        def _():
            dk_ref[...] = dkacc[...] * scale
            dv_ref[...] = dvacc[...].astype(BF16)

    head = lambda g, h, i: (i, g * GQA_GROUP_B + h)
    return _call(name, body, (N_KV_B, GQA_GROUP_B, nq),
                 [(qkv, (tq, LANES), head),
                  (qkv, (S, LANES), lambda g, h, i: (0, N_HEADS_B + g)),
                  (qkv, (S, LANES), lambda g, h, i: (0, N_HEADS_B + N_KV_B + g)),
                  (k_t, (LANES, S), lambda g, h, i: (g, 0)),
                  (do_b, (tq, LANES), head), (o_b, (tq, LANES), head),
                  (lse, (None, 1, tq), lambda g, h, i: (g * GQA_GROUP_B + h, 0, i))],
                 [((S, N_HEADS_B * LANES), F32, (tq, LANES), head),
                  ((S, N_KV_B * LANES), F32, (S, LANES), lambda g, h, i: (0, g)),
                  ((S, N_KV_B * LANES), BF16, (S, LANES), lambda g, h, i: (0, g))],
                 scratch=[pltpu.VMEM((S, LANES), F32)] * 2,
                 sem=("parallel", "arbitrary", "arbitrary"))


MERGE_TN = 512


def _mix_rows_spec(Gm, row0, n_slots, slot_map, cols=None, col_map=None):
    C = Gm.shape[2] if cols is None else cols
    cm = (lambda *idx: 0) if col_map is None else col_map
    return (Gm, (n_slots, LANES, C), lambda *idx: (slot_map(*idx), row0 // LANES, cm(*idx)))


def _gate_specs(proj, tm):
    first = (A_QKV_WIDTH + PB_GATE_A) // MERGE_TN
    return [(proj, (tm, MERGE_TN), lambda i, k=k: (i, first + k)) for k in range(4)]


def _whole_rows_spec(Gm, row0):
    return _mix_rows_spec(Gm, row0, N_DEV, lambda *idx: 0)


def merge_fwd(o_a, o_b, w_a, Gm, proj, b_gate, x, name):
    S, D = x.shape
    tm = 256

    def body(oa_ref, ob_ref, wa_ref, wb_ref, wo_ref, g0, g1, g2, g3, bg_ref, x_ref, m_ref, ya_ref, yb_ref, xo_ref):
        ya = _dot(oa_ref[...], wa_ref[...])
        yb = _dot(ob_ref[...], wb_ref[...].reshape(N_DEV * LANES, D))
        ga = _sigmoid(jnp.concatenate([g0[...], g1[...]], axis=1).astype(F32) + bg_ref[:, 0:D])
        gb = _sigmoid(jnp.concatenate([g2[...], g3[...]], axis=1).astype(F32) + bg_ref[:, D:2 * D])
        merged = (ga * ya + gb * yb).astype(BF16)
        m_ref[...] = merged
        ya_ref[...] = ya.astype(BF16)
        yb_ref[...] = yb.astype(BF16)
        xo_ref[...] = x_ref[...] + _dot(merged, wo_ref[...].reshape(N_DEV * LANES, D))

    rows = lambda a: (a, (tm, a.shape[1]), lambda i: (i, 0))
    out = ((S, D), BF16, (tm, D), lambda i: (i, 0))
    return _call(name, body, (S // tm,),
                 [rows(o_a), rows(o_b), (w_a, w_a.shape, lambda i: (0, 0)),
                  _whole_rows_spec(Gm, REST_WB), _whole_rows_spec(Gm, REST_WOUT)]
                 + _gate_specs(proj, tm) + [(b_gate, (1, 2 * D), lambda i: (0, 0)), rows(x)],
                 [out, out, out, ((S, D), F32, (tm, D), lambda i: (i, 0))], sem=("parallel",))


def merge_bwd(dx2, w_a, Gm, ya, yb, proj, b_gate, name):
    S, D = dx2.shape
    tm = 256

    def body(d_ref, wo_ref, wa_ref, wb_ref, ya_ref, yb_ref, g0, g1, g2, g3, bg_ref,
             dya_ref, dyb_ref, dg_ref, dbg_ref, doa_ref, dob_ref):
        i = pl.program_id(0)
        dm = _dot(d_ref[...].astype(BF16), wo_ref[...].reshape(N_DEV * LANES, D), 1, 1)
        ga = _sigmoid(jnp.concatenate([g0[...], g1[...]], axis=1).astype(F32) + bg_ref[:, 0:D])
        gb = _sigmoid(jnp.concatenate([g2[...], g3[...]], axis=1).astype(F32) + bg_ref[:, D:2 * D])
        dya = (dm * ga).astype(BF16)
        dyb = (dm * gb).astype(BF16)
        dya_ref[...] = dya
        dyb_ref[...] = dyb
        dpa = dm * ya_ref[...].astype(F32) * ga * (1.0 - ga)
        dpb = dm * yb_ref[...].astype(F32) * gb * (1.0 - gb)
        dg_ref[0] = dpa.astype(BF16)
        dg_ref[1] = dpb.astype(BF16)
        doa_ref[...] = _dot(dya, wa_ref[...], 1, 1)
        dob_ref[...] = _dot(dyb, wb_ref[...].reshape(N_DEV * LANES, D), 1, 1).astype(BF16)
        sa =jnp.sum(dpa, axis=0, keepdims=True)
        sb = jnp.sum(dpb, axis=0, keepdims=True)

        @pl.when(i == 0)
        def _():
            dbg_ref[0] = sa
            dbg_ref[1] = sb

        @pl.when(i > 0)
        def _():
            dbg_ref[0] += sa
            dbg_ref[1] += sb

    tile = ((tm, D), lambda i: (i, 0))
    return _call(
        name, body, (S // tm,),
        [(dx2,) + tile, _whole_rows_spec(Gm, REST_WOUT), (w_a, w_a.shape, lambda i: (0, 0)),
         _whole_rows_spec(Gm, REST_WB), (ya,) + tile, (yb,) + tile]
        + _gate_specs(proj, tm) + [(b_gate, (1, 2 * D), lambda i: (0, 0))],
        [((S, D), BF16) + tile, ((S, D), BF16) + tile,
         ((2, S, D), BF16, (2, tm, D), lambda i: (0, i, 0)),
         ((2, 1, D), F32, (2, 1, D), lambda i: (0, 0, 0)),
         ((S, w_a.shape[0]), F32, (tm, w_a.shape[0]), lambda i: (i, 0)),
         ((S, N_HEADS_B * LANES), BF16, (tm, N_HEADS_B * LANES), lambda i: (i, 0))],
        sem=("arbitrary",))


def weight_grad_rows(a, b, grads, row0, name):
    S, M = a.shape
    N = b.shape[1]
    tmm = 512
    tk = WGRAD_TK
    nk = S // tk
    prior = [] if grads is None else [grads]

    def body(*refs):
        a_ref, b_ref, o_ref, acc_ref = refs[len(prior):]
        k = pl.program_id(1)
        p = _dot(a_ref[...], b_ref[...].astype(BF16), 0, 0)

        @pl.when(k == 0)
        def _():
            acc_ref[...] = p

        @pl.when(k > 0)
        def _():
            acc_ref[...] += p

        @pl.when(k == nk - 1)
        def _():
            o_ref[...] = acc_ref[...].astype(BF16).reshape(tmm // LANES, LANES, N)

    return pl.pallas_call(
        body,
        out_shape=jax.ShapeDtypeStruct((N_DEV, MIX_ROWS, N), BF16),
        grid=(M // tmm, nk),
        in_specs=[pl.BlockSpec(memory_space=pl.ANY)] * len(prior)
        + [pl.BlockSpec((tk, tmm), lambda j, k: (k, j)),
           pl.BlockSpec((tk, N), lambda j, k: (k, 0))],
        out_specs=pl.BlockSpec((tmm // LANES, LANES, N), lambda j, k: (j, row0 // LANES, 0)),
        scratch_shapes=[pltpu.VMEM((tmm, N), F32)],
        input_output_aliases={0: 0} if prior else {},
        name=name,
        compiler_params=pltpu.CompilerParams(dimension_semantics=("parallel", "arbitrary"),
                                             vmem_limit_bytes=VMEM_LIMIT),
    )(*prior, a, b)


def weight_grad_plain(a, b, name):
    S, M = a.shape
    N = b.shape[1]
    tk = WGRAD_TK
    nk = S // tk

    def body(a_ref, b_ref, o_ref, acc_ref):
        k = pl.program_id(0)
        p = _dot(a_ref[...], b_ref[...], 0, 0)

        @pl.when(k == 0)
        def _():
            acc_ref[...] = p

        @pl.when(k > 0)
        def _():
            acc_ref[...] += p

        @pl.when(k == nk - 1)
        def _():
            o_ref[...] = acc_ref[...].astype(BF16)

    return _call(name, body, (nk,),
                 [(a, (tk, M), lambda k: (k, 0)), (b, (tk, N), lambda k: (k, 0))],
                 [((M, N), BF16, (M, N), lambda k: (0, 0))],
                 scratch=[pltpu.VMEM((M, N), F32)], sem=("arbitrary",))[0]


def local_step(x, tgt, p, get_g1_up, get_g1_down, get_gm_in, get_gm_rest, get_g2, emit, start_token):
    S, D = x.shape
    after = lambda t: t[0:1, 0:1]
    buckets = _bucket_tables()
    cos_t, sin_t = _rope_tables(S)
    gains = jnp.concatenate([jnp.tile(p["q_norm"], (1, N_HEADS_B)), jnp.tile(p["k_norm"], (1, N_KV_B)),
                             jnp.ones((1, N_KV_B * LANES), F32)], axis=1)

    n1 = rms_fwd(x, p["ffn1_norm"] + after(start_token), "ffn1_norm")
    bias = bias_build(p["rel_bias"] + after(start_token), buckets)
    g1_up = get_g1_up((n1, bias))
    ab1 = ffn_up(n1, (g1_up, None), "ffn1_up")
    G1 = (g1_up, get_g1_down(ab1))
    x1, hm = ffn_down(ab1, G1, x, p["mix_norm"], "ffn1_down")
    Gw = get_gm_in(hm)
    proj = in_proj(hm, Gw, "in_proj")

    outs, lses = [], []
    for g in range(3):
        o, l = a_fwd(proj, bias, g, "a_fwd_%d" % g)
        outs.append(o)
        lses.append(l)
    o_a, lse_tot = a_combine(outs, lses, "a_combine")

    qkv = qkv_prep(proj, gains, cos_t, sin_t, "qkv_prep")
    k_t = qkv[:, N_HEADS_B * LANES:(N_HEADS_B + N_KV_B) * LANES].T
    o_b, lse_b = flash_fwd(qkv, "flash_fwd")

    Gm = get_gm_rest(o_b)
    w_a = Gm[:, REST_WA:REST_ROWS, :].reshape(N_DEV, GROUP_WIDTH_A, LANES).transpose(1, 0, 2).reshape(GROUP_WIDTH_A, D)
    merged, ya, yb, x2 = merge_fwd(o_a, o_b, w_a, Gm, proj, p["b_gate"], x1, "merge_fwd")

    G2 = get_g2(x2)
    n2, ab2, dx3_b, dab2, dx2, dx2_b, d_ffn2_norm, loss, d_final = ffn_last(
        x2, p["ffn2_norm"], G2, tgt, p["final_norm"], "ffn2")
    gw2 = ffn_bwd_weights(dx3_b, ab2, dab2, n2, "ffn2_bwd")
    t2 = emit("ffn2", gw2)

    dya, dyb, dgate, dbg, do_a, do_b = merge_bwd(dx2_b, w_a, Gm, ya, yb, proj, p["b_gate"] + after(t2),
                                                 "merge_bwd")
    gm_grads = weight_grad_rows(merged, dx2_b, None, MIX_WOUT, "dw_out")
    gm_grads = weight_grad_rows(o_b, dyb, gm_grads, MIX_WB, "dw_branch_b")
    dw_a = weight_grad_plain(o_a, dya, "dw_branch_a")

    dq_r, dk_r, dv_b = flash_bwd(qkv, k_t, do_b, o_b, lse_b, "flash_bwd")
    dq_b, d_q_norm = qk_prep_bwd(dq_r, proj, A_QKV_WIDTH, p["q_norm"], cos_t, sin_t, "q_prep_bwd")
    dk_b, d_k_norm = qk_prep_bwd(dk_r, proj, A_QKV_WIDTH + N_HEADS_B * LANES, p["k_norm"], cos_t, sin_t,
                                 "k_prep_bwd")

    dqkv, dbs = [], []
    for g in range(3):
        dg_, db = a_bwd(proj, bias, do_a, o_a, lse_tot, g, "a_bwd_%d" % g)
        dqkv.append(dg_)
        dbs.append(db)
    d_rel_bias = bias_bwd(jnp.stack(dbs, axis=0).reshape(3, HEADS_PER_GROUP_A, A_TQ, A_WIN), buckets)

    dproj = _dproj_pieces(dqkv, dq_b, jnp.concatenate([dk_b, dv_b], axis=1), dgate)
    gm_grads = in_proj_bwd_dw(dproj[:3], hm, gm_grads, "in_proj_bwd_a")
    gm_grads = in_proj_bwd_dw(dproj[3:], hm, gm_grads, "in_proj_bwd_b")
    dw_a_sh = dw_a.reshape(GROUP_WIDTH_A, N_DEV, LANES).transpose(1, 0, 2).reshape(N_DEV, MIX_ROWS - MIX_WA, D)
    gm_grads = lax.dynamic_update_slice(gm_grads, dw_a_sh, (0, MIX_WA, 0))
    tm = emit("mix", gm_grads)
    dx1, dx1_b, d_mix_norm = in_proj_bwd_dh(dproj, Gw, x1, p["mix_norm"] + after(tm), dx2, "in_proj_bwd")

    dab1 = ffn_bwd_hidden(dx1_b, ab1, G1, "ffn1_bwd")
    gw1 = ffn_bwd_weights(dx1_b, ab1, dab1, n1, "ffn1_bwd")
    t1 = emit("ffn1", gw1)
    dx0, d_ffn1_norm = ffn_bwd_input(dab1, G1, x, p["ffn1_norm"] + after(t1), dx1, "ffn1_bwd")

    small = dict(ffn1_norm=d_ffn1_norm, mix_norm=d_mix_norm, b_gate=dbg.reshape(1, 2 * D),
                 q_norm=d_q_norm, k_norm=d_k_norm, rel_bias=d_rel_bias, ffn2_norm=d_ffn2_norm,
                 final_norm=d_final)
    return loss, dx0, small


def _pack_small(t, loss_row):
    row6 = jnp.concatenate([t["q_norm"].reshape(1, -1), t["k_norm"].reshape(1, -1), t["rel_bias"].reshape(1, -1)], axis=1)
    return jnp.concatenate([t["ffn1_norm"].reshape(1, -1), t["mix_norm"].reshape(1, -1), t["b_gate"].reshape(2, -1),
                            t["ffn2_norm"].reshape(1, -1), t["final_norm"].reshape(1, -1), row6, loss_row], axis=0)


def _unpack_small(a, shapes):
    return dict(ffn1_norm=a[0:1].reshape(shapes["ffn1_norm"]), mix_norm=a[1:2].reshape(shapes["mix_norm"]),
                b_gate=a[2:4].reshape(shapes["b_gate"]), ffn2_norm=a[4:5].reshape(shapes["ffn2_norm"]),
                final_norm=a[5].reshape(shapes["final_norm"]), q_norm=a[6:7, 0:128].reshape(shapes["q_norm"]),
                k_norm=a[6:7, 128:256].reshape(shapes["k_norm"]), rel_bias=a[6, 256:1024].reshape(shapes["rel_bias"]))


SMALL = ("ffn1_norm", "mix_norm", "b_gate", "q_norm", "k_norm", "rel_bias", "ffn2_norm", "final_norm")
ORDER = ("ffn1_norm", "ffn1_w1", "ffn1_w3", "ffn1_w2", "mix_norm", "w_in", "b_gate", "q_norm", "k_norm", "rel_bias",
         "w_branch_a", "w_branch_b", "w_out", "ffn2_norm", "ffn2_w1", "ffn2_w3", "ffn2_w2", "final_norm")


def kernel(x, ffn1_norm, ffn1_w1, ffn1_w3, ffn1_w2, mix_norm, w_in, b_gate, q_norm, k_norm, rel_bias, w_branch_a, w_branch_b, w_out, ffn2_norm, ffn2_w1, ffn2_w3, ffn2_w2, final_norm, loss_target, m_ffn1_norm, m_ffn1_w1, m_ffn1_w3, m_ffn1_w2, m_mix_norm, m_w_in, m_b_gate, m_q_norm, m_k_norm, m_rel_bias, m_w_branch_a, m_w_branch_b, m_w_out, m_ffn2_norm, m_ffn2_w1, m_ffn2_w3, m_ffn2_w2, m_final_norm, v_ffn1_norm, v_ffn1_w1, v_ffn1_w3, v_ffn1_w2, v_mix_norm, v_w_in, v_b_gate, v_q_norm, v_k_norm, v_rel_bias, v_w_branch_a, v_w_branch_b, v_w_out, v_ffn2_norm, v_ffn2_w1, v_ffn2_w3, v_ffn2_w2, v_final_norm):
    args = dict(locals())
    w = {n: args[n] for n in ORDER}
    m = {n: args["m_" + n] for n in ORDER}
    v = {n: args["v_" + n] for n in ORDER}
    D = x.shape[2]

    blocks = (
        ("ffn1_up", jnp.concatenate([ffn1_w1[0].T, ffn1_w3[0].T], axis=0)),
        ("ffn1_down", ffn1_w2[0]),
        ("mix_in", w_in[0]),
        ("mix_rest", jnp.concatenate([w_branch_b[0], w_out[0], w_branch_a[0].reshape(REST_ROWS - REST_WA, D)], axis=0)),
        ("ffn2", jnp.concatenate([ffn2_w1[0].T, ffn2_w3[0].T, ffn2_w2[0]], axis=0)),
    )
    direct = ("mix_rest", "ffn2")
    started = all_gather_start_all([(b.astype(BF16), tag in direct) for tag, b in blocks], "all_gather_start")
    gathers = {tag: s for (tag, _), s in zip(blocks, started)}
    start_token = started[0][4]

    def gathered(tag):
        def get(after):
            if tag in direct:
                return all_gather_place_own(*_split_wait("all_gather_" + tag + "_wait", gathers[tag], N_DEV - 1, after),
                                            "all_gather_" + tag + "_own")
            return all_gather_finish(*_split_wait("all_gather_" + tag + "_wait", gathers[tag], 4, after),
                                     "all_gather_" + tag + "_finish")
        return get

    core = lax.axis_index("c").astype(jnp.int32).reshape(1)
    chip = (2 * lax.axis_index("x") + lax.axis_index("y")).astype(jnp.int32).reshape(1)
    device = 2 * chip + core
    exchanges = {}

    def emit(tag, gw):
        if tag == "ffn1":
            (theirs,) = reduce_scatter_pair([gw], "reduce_scatter_pair_" + tag)
            part = pair_add(gw, theirs, core, "pair_add_" + tag)
            exchanges[tag] = reduce_scatter_start(part, "reduce_scatter_" + tag + "_start")
        else:
            exchanges[tag] = reduce_scatter_start_direct(gw, "reduce_scatter_" + tag + "_start")
        return exchanges[tag][4]

    small_p = dict(ffn1_norm=ffn1_norm, mix_norm=mix_norm, b_gate=b_gate, q_norm=q_norm, k_norm=k_norm,
                   rel_bias=rel_bias, ffn2_norm=ffn2_norm, final_norm=final_norm.reshape(1, D))
    loss_p, grad_x, small_g = local_step(x[0], loss_target[0], small_p, gathered("ffn1_up"), gathered("ffn1_down"),
                                         gathered("mix_in"), gathered("mix_rest"), gathered("ffn2"), emit, start_token)

    def landed(tag, after):
        n_others, me = (3, chip) if tag == "ffn1" else (N_DEV - 1, device)
        return tuple(_split_wait("reduce_scatter_" + tag + "_wait", exchanges[tag], n_others, after)) + (me,)

    grads, delta, new_m, new_v = {}, {}, {}, {}

    def finish(n, part, land, me, off, blk, transposed=False):
        shp = w[n].shape
        if transposed:
            to2 = lambda a: a.reshape(shp[-2], shp[-1]).T
            back = lambda a: a.T.reshape(shp)
        else:
            to2 = lambda a: a.reshape(shp[-2], shp[-1])
            back = lambda a: a.reshape(shp)
        res = sum_adamw(part, land, me, off, blk, to2(w[n]), to2(m[n]), to2(v[n]), "update_" + n)
        grads[n], delta[n], new_m[n], new_v[n] = [back(a) for a in res]

    last_token = exchanges["ffn1"][4]
    for tag, after in (("ffn2", last_token), ("ffn1", grad_x)):
        group = landed(tag, after)
        finish(tag + "_w1", *group, 0, FFN_SHARD, transposed=True)
        finish(tag + "_w3", *group, FFN_SHARD, FFN_SHARD, transposed=True)
        finish(tag + "_w2", *group, 2 * FFN_SHARD, FFN_SHARD)
        if tag == "ffn2":
            group_m = landed("mix", last_token)
            finish("w_in", *group_m, MIX_WIN, LANES)
            finish("w_branch_b", *group_m, MIX_WB, LANES)
            finish("w_out", *group_m, MIX_WOUT, LANES)
            grads["w_branch_a"] = sum_landed(*group_m, MIX_WA, MIX_ROWS - MIX_WA, MIX_ROWS - MIX_WA,
                                             "w_branch_a_sum").reshape(w_branch_a.shape)
    loss_row = jnp.pad(loss_p, ((0, 0), (0, D - LANES)))
    smalls = small_all_gather(_pack_small(small_g, loss_row), new_v["w_in"])
    small_sum = sum_slots(smalls, 0, N_DEV, N_DEV, "small_sum")
    small_shapes = {n: w[n].shape for n in SMALL}
    grads.update(_unpack_small(small_sum, small_shapes))
    loss = small_sum[7, 0]

    n = "w_branch_a"
    two_d = lambda a: a.reshape(w[n].shape[-2], w[n].shape[-1])
    d_, m_, v_ = adamw(two_d(w[n]), two_d(grads[n]), two_d(m[n]), two_d(v[n]), "adamw_" + n)
    delta[n], new_m[n], new_v[n] = [a.reshape(w[n].shape) for a in (d_, m_, v_)]
    zero_row = jnp.zeros((1, D), F32)
    pack = lambda t: _pack_small({n: t[n] for n in SMALL}, zero_row)
    d_, m_, v_ = adamw(pack(w), small_sum, pack(m), pack(v), "adamw_small")
    for src, dst in ((d_, delta), (m_, new_m), (v_, new_v)):
        dst.update(_unpack_small(src, small_shapes))

    return (loss, grad_x[None], *[grads[n] for n in ORDER], *[delta[n] for n in ORDER],
            *[new_m[n] for n in ORDER], *[new_v[n] for n in ORDER])
```

```python
import math

import jax
import jax.numpy as jnp
from jax import lax
from jax.experimental import pallas as pl
from jax.experimental.pallas import tpu as pltpu

F32 = jnp.float32
BF16 = jnp.bfloat16
MESH = pl.DeviceIdType.MESH

V7X_VMEM_BYTES = 64 * 1024 * 1024
VMEM_LIMIT = V7X_VMEM_BYTES - 8 * 1024 * 1024
LANES = 128

N_DEV = 8
EPS = 1e-6
NEG_INF = -1e30

DILATIONS = (1, 4, 16)
HALF_WINDOW = 64
HEAD_DIM_A = 64
HEADS_PER_GROUP_A = 8
GROUP_WIDTH_A = 512
A_QKV_WIDTH = 4608
A_GROUP_QKV = A_QKV_WIDTH // 3
A_TQ = 128
A_WIN = A_TQ + 2 * HALF_WINDOW
A_UNROLL = 8
A_SCALE = HEAD_DIM_A ** -0.5
WGRAD_TK = 2048
HEAD_DIM_B = 128
N_HEADS_B = 8
N_KV_B = 2
GQA_GROUP_B = 4
GRID_W = 64
ROPE_THETA = 10000.0
B_TQ_FWD = 256
B_TQ_BWD = 512
B_HEADS_PER_STEP = 4
LOG2E = 1.4426950408889634
B_Q_PRESCALE = HEAD_DIM_B ** -0.5 * LOG2E
N_BUCKETS = 32
MAX_DISTANCE = 1024
PB_GATE_A = 1536

ADAM_LR = 0.001
ADAM_B1 = 0.9
ADAM_B2 = 0.999
ADAM_EPS = 1e-08
ADAM_WD = 0.01
ADAM_STEP = 10

FFN_SHARD = 352
MIX_WIN, MIX_WB, MIX_WOUT, MIX_WA = 0, 1024, 1152, 1280
MIX_ROWS = 1344
REST_WB, REST_WOUT, REST_WA, REST_ROWS = 0, 128, 256, 320


def _dot(a, b, ca=1, cb=0):
    return lax.dot_general(a, b, (((ca,), (cb,)), ((), ())), preferred_element_type=F32)


def _call(name, body, grid, ins, outs, scratch=(), sem=None, aliases=None):
    ins = [tuple(i) + (None,) * (4 - len(i)) for i in ins]
    res = pl.pallas_call(
        body,
        out_shape=[jax.ShapeDtypeStruct(s, d) for (s, d, _, _) in outs],
        grid=grid,
        in_specs=[pl.BlockSpec(bs, im, pipeline_mode=pm) for (_, bs, im, pm) in ins],
        out_specs=[pl.BlockSpec(bs, im) for (_, _, bs, im) in outs],
        scratch_shapes=list(scratch),
        name=name,
        input_output_aliases=aliases or {},
        compiler_params=pltpu.CompilerParams(dimension_semantics=sem, vmem_limit_bytes=VMEM_LIMIT),
    )(*[i[0] for i in ins])
    return res


def _sigmoid(x):
    return 0.5 * jnp.tanh(0.5 * x) + 0.5


def _position():
    return lax.axis_index("x"), lax.axis_index("y"), lax.axis_index("c")


def _hbm_specs(n):
    return [pl.BlockSpec(memory_space=pl.ANY) for _ in range(n)]


PAIR_BUFFERS = 4


def reduce_scatter_pair(grads, name):
    n = len(grads)
    C = grads[0].shape[2]
    half = [g.shape[1] // 2 for g in grads]
    chunks = [(i, q, hf) for i in range(n) for q in range(4) for hf in range(2)]
    nb = PAIR_BUFFERS

    def body(*refs):
        ins, theirs = refs[:n], refs[n:2 * n]
        buf, load_sems, send_sems, recv_sems = refs[2 * n:]
        x, y, c = _position()
        sibling = (x, y, 1 - c)

        def load(k):
            i, q, hf = chunks[k]
            r = half[i]
            return pltpu.make_async_copy(ins[i].at[2 * q + (1 - c), pl.ds(hf * r, r), :],
                                         buf.at[k % nb, pl.ds(0, r), :], load_sems.at[k % nb])

        def send(k):
            i, q, hf = chunks[k]
            r = half[i]
            return pltpu.make_async_remote_copy(
                src_ref=buf.at[k % nb, pl.ds(0, r), :], dst_ref=theirs[i].at[q, pl.ds(hf * r, r), :],
                send_sem=send_sems.at[k % nb], recv_sem=recv_sems.at[i],
                device_id=sibling, device_id_type=MESH)

        for k in range(len(chunks) + 1):
            if k < len(chunks):
                if k >= nb:
                    send(k - nb).wait_send()
                load(k).start()
            if k >= 1:
                load(k - 1).wait()
                send(k - 1).start()
        for k in range(max(0, len(chunks) - nb), len(chunks)):
            send(k).wait_send()
        for i in range(n):
            pltpu.make_async_remote_copy(
                src_ref=theirs[i], dst_ref=theirs[i], send_sem=send_sems.at[0], recv_sem=recv_sems.at[i],
                device_id=sibling, device_id_type=MESH).wait_recv()

    return pl.pallas_call(
        body,
        out_shape=[jax.ShapeDtypeStruct((4,) + g.shape[1:], g.dtype) for g in grads],
        in_specs=_hbm_specs(n),
        out_specs=_hbm_specs(n),
        scratch_shapes=[pltpu.VMEM((nb, max(half), C), grads[0].dtype), pltpu.SemaphoreType.DMA((nb,)),
                        pltpu.SemaphoreType.DMA((nb,)), pltpu.SemaphoreType.DMA((n,))],
        name=name,
        compiler_params=pltpu.CompilerParams(vmem_limit_bytes=VMEM_LIMIT),
    )(*grads)


_HBM_SPEC = pl.BlockSpec(memory_space=pltpu.HBM)
_SEM_SPEC = pl.BlockSpec(memory_space=pltpu.SEMAPHORE)
_TOKEN_SPEC = pl.BlockSpec(memory_space=pltpu.VMEM)
_DATAFLOW = pltpu.SideEffectType.DATAFLOW_SIDE_EFFECTING


def _split_start_many(name, exchanges):
    n = len(exchanges)

    def full_body(*refs):
        srcs, lands = refs[:n], refs[n:2 * n]
        sems = refs[2 * n:4 * n]
        token = refs[-1]
        for i, (body, _, _) in enumerate(exchanges):
            body(srcs[i], lands[i], sems[2 * i], sems[2 * i + 1])
        token[...] = jnp.zeros_like(token)

    srcs = [pltpu.with_memory_space_constraint(src, pltpu.HBM) for _, src, _ in exchanges]
    lands = [pltpu.with_memory_space_constraint(lax.empty(shape, src.dtype), pltpu.HBM)
             for _, src, shape in exchanges]
    res = pl.pallas_call(
        full_body, name=name,
        out_shape=(pltpu.SemaphoreType.DMA(()),) * (2 * n)
        + tuple(pltpu.HBM(a.shape, a.dtype) for a in srcs + lands) + (jax.ShapeDtypeStruct((8, LANES), F32),),
        in_specs=(_HBM_SPEC,) * (2 * n),
        out_specs=(_SEM_SPEC,) * (2 * n) + (_HBM_SPEC,) * (2 * n) + (_TOKEN_SPEC,),
        input_output_aliases={i: 2 * n + i for i in range(2 * n)},
        compiler_params=pltpu.CompilerParams(has_side_effects=_DATAFLOW),
    )(*srcs, *lands)
    return [(res[2 * i], res[2 * i + 1], res[2 * n + i], res[3 * n + i], res[-1]) for i in range(n)]


def _split_start(name, body, src, land_shape):
    return _split_start_many(name, [(body, src, land_shape)])[0]


def _split_wait(name, started, n_blocks, after):
    send_sem, recv_sem, src_thru, land_thru, _ = started
    after = after if isinstance(after, tuple) else (after,)

    def body(src_ref, land_ref, send_sem, recv_sem, *rest):
        x, y, c = _position()
        blocks = land_ref.at[pl.ds(0, n_blocks)]
        copy = pltpu.make_async_remote_copy(src_ref=blocks, dst_ref=blocks, send_sem=send_sem, recv_sem=recv_sem,
                                            device_id=(x, y, c), device_id_type=MESH)
        copy.wait_send()
        copy.wait_recv()

    return pl.pallas_call(
        body, name=name,
        out_shape=(pltpu.HBM(src_thru.shape, src_thru.dtype), pltpu.HBM(land_thru.shape, land_thru.dtype)),
        in_specs=(_HBM_SPEC, _HBM_SPEC, _SEM_SPEC, _SEM_SPEC) + (pl.BlockSpec(memory_space=pl.ANY),) * len(after),
        out_specs=(_HBM_SPEC, _HBM_SPEC),
        input_output_aliases={0: 0, 1: 1},
        compiler_params=pltpu.CompilerParams(has_side_effects=_DATAFLOW),
    )(src_thru, land_thru, send_sem, recv_sem, *after)


def all_gather_start_all(blocks, name):
    def starter(direct):
        def body(b_ref, land_ref, send_sem, recv_sem):
            x, y, c = _position()
            peers = _other_devices(x, y, c) if direct else [(x, y, 1 - c), (1 - x, y, c), (x, 1 - y, c),
                                                            (1 - x, 1 - y, c)]
            for peer in peers:
                pltpu.make_async_remote_copy(src_ref=b_ref, dst_ref=land_ref.at[4 * x + 2 * y + c],
                                             send_sem=send_sem, recv_sem=recv_sem,
                                             device_id=peer, device_id_type=MESH).start()
        return body

    return _split_start_many(name, [(starter(direct), block, (N_DEV,) + block.shape) for block, direct in blocks])


def all_gather_finish(block, land, name):
    R, C = block.shape

    def body(b_ref, land_in, land_ref, stage, load_sems, send_sems, recv_sems, own_sem):
        x, y, c = _position()
        sibling = (x, y, 1 - c)
        chips = [(1 - x, y), (x, 1 - y), (1 - x, 1 - y)]
        own_in = pltpu.make_async_copy(b_ref, stage.at[3], load_sems.at[3])
        own_in.start()
        loads = [pltpu.make_async_copy(land_in.at[4 * px + 2 * py + c], stage.at[j], load_sems.at[j])
                 for j, (px, py) in enumerate(chips)]
        for ld in loads:
            ld.start()
        sends = []
        for j, (px, py) in enumerate(chips):
            loads[j].wait()
            dst = land_ref.at[4 * px + 2 * py + c]
            cp = pltpu.make_async_remote_copy(src_ref=stage.at[j], dst_ref=dst, send_sem=send_sems.at[j],
                                              recv_sem=recv_sems.at[j], device_id=sibling, device_id_type=MESH)
            cp.start()
            sends.append(cp)
        own_in.wait()
        own_out = pltpu.make_async_copy(stage.at[3], land_ref.at[4 * x + 2 * y + c], own_sem)
        own_out.start()
        for j, (px, py) in enumerate(chips):
            dst = land_ref.at[4 * px + 2 * py + (1 - c)]
            pltpu.make_async_remote_copy(src_ref=stage.at[j], dst_ref=dst, send_sem=send_sems.at[j],
                                         recv_sem=recv_sems.at[j], device_id=sibling,
                                         device_id_type=MESH).wait_recv()
        for cp in sends:
            cp.wait_send()
        own_out.wait()

    return pl.pallas_call(
        body,
        out_shape=jax.ShapeDtypeStruct(land.shape, land.dtype),
        in_specs=_hbm_specs(2),
        out_specs=pl.BlockSpec(memory_space=pl.ANY),
        scratch_shapes=[pltpu.VMEM((4, R, C), block.dtype), pltpu.SemaphoreType.DMA((4,)),
                        pltpu.SemaphoreType.DMA((3,)), pltpu.SemaphoreType.DMA((3,)), pltpu.SemaphoreType.DMA],
        input_output_aliases={1: 0},
        name=name,
        compiler_params=pltpu.CompilerParams(vmem_limit_bytes=VMEM_LIMIT),
    )(block, land)


def reduce_scatter_start(parts, name):
    def body(p_ref, land_ref, send_sem, recv_sem):
        x, y, c = _position()
        for px, py in [(1 - x, y), (x, 1 - y), (1 - x, 1 - y)]:
            pltpu.make_async_remote_copy(src_ref=p_ref.at[2 * px + py], dst_ref=land_ref.at[2 * x + y],
                                         send_sem=send_sem, recv_sem=recv_sem,
                                         device_id=(px, py, c), device_id_type=MESH).start()

    return _split_start(name, body, parts, parts.shape)


def _other_devices(x, y, c):
    return [(1 - x if k & 4 else x, 1 - y if k & 2 else y, 1 - c if k & 1 else c) for k in range(1, N_DEV)]


def all_gather_place_own(block, land, name):
    R, C = block.shape

    def body(b_ref, land_in, land_ref, stage, sems):
        x, y, c = _position()
        load = pltpu.make_async_copy(b_ref, stage, sems.at[0])
        load.start()
        load.wait()
        store = pltpu.make_async_copy(stage, land_ref.at[4 * x + 2 * y + c], sems.at[1])
        store.start()
        store.wait()

    return pl.pallas_call(
        body,
        out_shape=jax.ShapeDtypeStruct(land.shape, land.dtype),
        in_specs=_hbm_specs(2),
        out_specs=pl.BlockSpec(memory_space=pl.ANY),
        scratch_shapes=[pltpu.VMEM((R, C), block.dtype), pltpu.SemaphoreType.DMA((2,))],
        input_output_aliases={1: 0},
        name=name,
    )(block, land)


def reduce_scatter_start_direct(grads, name):
    def body(g_ref, land_ref, send_sem, recv_sem):
        x, y, c = _position()
        for px, py, pc in _other_devices(x, y, c):
            pltpu.make_async_remote_copy(src_ref=g_ref.at[4 * px + 2 * py + pc],
                                         dst_ref=land_ref.at[4 * x + 2 * y + c],
                                         send_sem=send_sem, recv_sem=recv_sem,
                                         device_id=(px, py, pc), device_id_type=MESH).start()

    return _split_start(name, body, grads, grads.shape)


def small_all_gather(small, after):
    def body(small_ref, after_ref, smalls, s_send, s_recv, s_local):
        x, y, c = _position()
        me = 4 * x + 2 * y + c
        lc = pltpu.make_async_copy(small_ref, smalls.at[me], s_local)
        lc.start()
        remote = []
        k = 0
        for dx in (0, 1):
            for dy in (0, 1):
                for dc in (0, 1):
                    if dx + dy + dc == 0:
                        continue
                    peer = (1 - x if dx else x, 1 - y if dy else y, 1 - c if dc else c)
                    rc = pltpu.make_async_remote_copy(
                        src_ref=small_ref, dst_ref=smalls.at[me],
                        send_sem=s_send.at[k], recv_sem=s_recv.at[k],
                        device_id=peer, device_id_type=MESH)
                    rc.start()
                    remote.append(rc)
                    k += 1
        for rc in remote:
            rc.wait()
        lc.wait()

    return pl.pallas_call(
        body,
        out_shape=jax.ShapeDtypeStruct((N_DEV,) + small.shape, small.dtype),
        in_specs=_hbm_specs(2),
        out_specs=pl.BlockSpec(memory_space=pl.ANY),
        scratch_shapes=[pltpu.SemaphoreType.DMA((7,)), pltpu.SemaphoreType.DMA((7,)), pltpu.SemaphoreType.DMA],
        name="small_all_gather",
    )(small, after)


def pair_add(grads, theirs, core, name):
    _, R, C = theirs.shape
    tr = R // 2

    def body(c_ref, a_ref, b_ref, o_ref):
        o_ref[...] = (a_ref[...].astype(F32) + b_ref[...].astype(F32)).astype(BF16)

    return pl.pallas_call(
        body,
        out_shape=jax.ShapeDtypeStruct(theirs.shape, BF16),
        grid_spec=pltpu.PrefetchScalarGridSpec(
            num_scalar_prefetch=1, grid=(4, R // tr),
            in_specs=[pl.BlockSpec((None, tr, C), lambda q, i, c: (2 * q + c[0], i, 0)),
                      pl.BlockSpec((None, tr, C), lambda q, i, c: (q, i, 0))],
            out_specs=pl.BlockSpec((None, tr, C), lambda q, i, c: (q, i, 0))),
        name=name,
        compiler_params=pltpu.CompilerParams(dimension_semantics=("parallel", "parallel"),
                                             vmem_limit_bytes=VMEM_LIMIT),
    )(core, grads, theirs)


def sum_slots(recv, off, rows, blk, name):
    nq, _, C = recv.shape
    ob = off // blk

    def body(r_ref, o_ref):
        acc = r_ref[0].astype(F32)
        for q in range(1, nq):
            acc = acc + r_ref[q].astype(F32)
        o_ref[...] = acc

    return _call(name, body, (rows // blk,),
                 [(recv, (nq, blk, C), lambda i: (0, ob + i, 0))],
                 [((rows, C), F32, (blk, C), lambda i: (i, 0))], sem=("parallel",))[0]


def _sum_terms(refs):
    acc = refs[0][...].astype(F32)
    for r in refs[1:]:
        acc = acc + r[...].astype(F32)
    return acc


def sum_landed(own, land, me, off, rows, blk, name):
    n, _, C = land.shape
    ob = off // blk

    def body(c_ref, *refs):
        refs[n][...] = _sum_terms(refs[:n])

    def entry(flip):
        return pl.BlockSpec((None, blk, C), lambda i, c: (c[0] ^ flip, ob + i, 0))

    return pl.pallas_call(
        body,
        out_shape=jax.ShapeDtypeStruct((rows, C), F32),
        grid_spec=pltpu.PrefetchScalarGridSpec(
            num_scalar_prefetch=1, grid=(rows // blk,),
            in_specs=[entry(k) for k in range(n)],
            out_specs=pl.BlockSpec((blk, C), lambda i, c: (i, 0))),
        name=name,
        compiler_params=pltpu.CompilerParams(dimension_semantics=("parallel",), vmem_limit_bytes=VMEM_LIMIT),
    )(me, own, *([land] * (n - 1)))


def _adamw_update(wv, gv, mv, vv):
    nm = ADAM_B1 * mv + (1.0 - ADAM_B1) * gv
    nv = ADAM_B2 * vv + (1.0 - ADAM_B2) * (gv * gv)
    c1 = 1.0 / (1.0 - ADAM_B1 ** ADAM_STEP)
    c2 = 1.0 / (1.0 - ADAM_B2 ** ADAM_STEP)
    return -ADAM_LR * ((nm * c1) / (jnp.sqrt(nv * c2) + ADAM_EPS) + ADAM_WD * wv), nm, nv


def sum_adamw(own, land, me, off, blk, w, m, v, name):
    rows, C = w.shape
    n = land.shape[0]
    ob = off // blk

    def body(c_ref, *refs):
        w_ref, m_ref, v_ref, g_out, d_out, m_out, v_out = refs[n:]
        gv = _sum_terms(refs[:n])
        g_out[...] = gv
        d_out[...], m_out[...], v_out[...] = _adamw_update(w_ref[...], gv, m_ref[...], v_ref[...])

    def entry(flip):
        return pl.BlockSpec((None, blk, C), lambda i, c: (c[0] ^ flip, ob + i, 0))

    plain = pl.BlockSpec((blk, C), lambda i, c: (i, 0))
    return pl.pallas_call(
        body,
        out_shape=[jax.ShapeDtypeStruct((rows, C), F32)] * 4,
        grid_spec=pltpu.PrefetchScalarGridSpec(
            num_scalar_prefetch=1, grid=(rows // blk,),
            in_specs=[entry(k) for k in range(n)] + [plain, plain, plain],
            out_specs=[plain] * 4),
        name=name,
        compiler_params=pltpu.CompilerParams(dimension_semantics=("parallel",), vmem_limit_bytes=VMEM_LIMIT),
    )(me, own, *([land] * (n - 1)), w, m, v)


def adamw(w, g, m, v, name):
    R, C = w.shape
    tr = R
    for cand in (256, 128, 64, 32, 16, 8):
        if R % cand == 0 and R > cand:
            tr = cand
            break

    def body(w_ref, g_ref, m_ref, v_ref, d_ref, nm_ref, nv_ref):
        d_ref[...], nm_ref[...], nv_ref[...] = _adamw_update(w_ref[...], g_ref[...], m_ref[...], v_ref[...])

    spec = ((tr, C), lambda i: (i, 0))
    out = ((R, C), F32) + spec
    return _call(name, body, (R // tr,), [(w,) + spec, (g,) + spec, (m,) + spec, (v,) + spec],
                 [out, out, out], sem=("parallel",))


def _rms_tile(xv, gv):
    r = lax.rsqrt(jnp.mean(xv * xv, axis=-1, keepdims=True) + EPS)
    return (xv * r * gv).astype(BF16)


def rms_fwd(x, g, name):
    S, D = x.shape
    tr = 512

    def body(x_ref, g_ref, o_ref):
        o_ref[...] = _rms_tile(x_ref[...], g_ref[...])

    return _call(name, body, (S // tr,),
                 [(x, (tr, D), lambda i: (i, 0)), (g, (1, D), lambda i: (0, 0))],
                 [((S, D), BF16, (tr, D), lambda i: (i, 0))], sem=("parallel",))[0]


def _rms_bwd_tile(dn, xv, gv):
    r = lax.rsqrt(jnp.mean(xv * xv, axis=-1, keepdims=True) + EPS)
    xh = xv * r
    dxh = dn * gv
    dx = r * (dxh - xh * jnp.mean(dxh * xh, axis=-1, keepdims=True))
    return dx, dn * xh


def _final_loss_tile(xv, tv, gv):
    D = xv.shape[1]
    r = lax.rsqrt(jnp.mean(xv * xv, axis=-1, keepdims=True) + EPS)
    xh = xv * r
    e = xh * gv - tv
    part = 0.5 * jnp.sum(jnp.sum(e * e, axis=-1, keepdims=True) * (1.0 / D), axis=0, keepdims=True)
    dy = e * (1.0 / D)
    dxh = dy * gv
    dx = r * (dxh - xh * jnp.mean(dxh * xh, axis=-1, keepdims=True))
    return part, dx, jnp.sum(dy * xh, axis=0, keepdims=True)


FFN_TF = 4 * FFN_SHARD


def _ffn_pick(G, which):
    if isinstance(G, tuple):
        return (G[0], which) if which < 2 else (G[1], 0)
    return G, which


def _ffn_w_spec(G, which, imap):
    arr, blk = _ffn_pick(G, which)
    return (arr, (4, FFN_SHARD, arr.shape[2]), lambda *idx: (imap(*idx), blk, 0))


def _ffn_whole_w_spec(G, which):
    arr, blk = _ffn_pick(G, which)
    return (arr, (N_DEV, FFN_SHARD, arr.shape[2]), lambda *idx: (0, blk, 0), pl.Buffered(1))


def _ffn_hidden(a, b):
    av, bv = a.astype(F32), b.astype(F32)
    return (av * _sigmoid(av) * bv).astype(BF16)


def ffn_up(n, G, name):
    S, D = n.shape
    F = N_DEV * FFN_SHARD
    tm = 256

    def body(n_ref, w1_ref, w3_ref, abh_ref):
        nv = n_ref[...]
        a = _dot(nv, w1_ref[...].reshape(F, D), 1, 1).astype(BF16)
        b = _dot(nv, w3_ref[...].reshape(F, D), 1, 1).astype(BF16)
        abh_ref[0] = a
        abh_ref[1] = b
        abh_ref[2] = _ffn_hidden(a, b)

    return _call(name, body, (S // tm,),
                 [(n, (tm, D), lambda i: (i, 0)),
                  _ffn_whole_w_spec(G, 0), _ffn_whole_w_spec(G, 1)],
                 [((3, S, F), BF16, (3, tm, F), lambda i: (0, i, 0))],
                 sem=("parallel",))[0]


def ffn_down(abh, G, x, g_next, name):
    _, S, F = abh.shape
    D = x.shape[1]
    tm = 512

    def body(h_ref, w2_ref, x_ref, g_ref, o_ref, n_ref):
        xo = x_ref[...] + 0.5 * _dot(h_ref[...], w2_ref[...].reshape(F, D))
        o_ref[...] = xo
        n_ref[...] = _rms_tile(xo, g_ref[...])

    tile = ((tm, D), lambda i: (i, 0))
    return _call(name, body, (S // tm,),
                 [(abh, (None, tm, F), lambda i: (2, i, 0)), _ffn_whole_w_spec(G, 2),
                  (x,) + tile, (g_next, (1, D), lambda i: (0, 0))],
                 [((S, D), F32) + tile, ((S, D), BF16) + tile], sem=("parallel",))


def ffn_last(x, g, G, tgt, g_final, name):
    S, D = x.shape
    F = N_DEV * FFN_SHARD
    tm = 256

    def body(x_ref, g_ref, w1_ref, w3_ref, w2_ref, t_ref, gf_ref,
             n_ref, abh_ref, dxo_ref, dab_ref, dx_ref, dxb_ref, dg_ref, l_ref, dgf_ref):
        i = pl.program_id(0)
        xv, gv = x_ref[...], g_ref[...]
        chunks = [(slice(4 * f, 4 * f + 4), slice(f * FFN_TF, (f + 1) * FFN_TF)) for f in range(F // FFN_TF)]
        weight = lambda w_ref, slots: w_ref[slots].reshape(FFN_TF, D)
        nv = _rms_tile(xv, gv)
        n_ref[...] = nv
        y = None
        for slots, cols in chunks:
            a = _dot(nv, weight(w1_ref, slots), 1, 1).astype(BF16)
            b = _dot(nv, weight(w3_ref, slots), 1, 1).astype(BF16)
            h = _ffn_hidden(a, b)
            abh_ref[0, :, cols] = a
            abh_ref[1, :, cols] = b
            abh_ref[2, :, cols] = h
            t = _dot(h, weight(w2_ref, slots))
            y = t if y is None else y + t
        part, dxo, dgfp = _final_loss_tile(xv + 0.5 * y, t_ref[...], gf_ref[...])
        dxo_b = dxo.astype(BF16)
        dxo_ref[...] = dxo_b
        dn = None
        for slots, cols in chunks:
            dh = 0.5 * _dot(dxo_b, weight(w2_ref, slots), 1, 1)
            da, db = _ffn_hidden_grads(dh, abh_ref[0, :, cols].astype(F32), abh_ref[1, :, cols].astype(F32))
            da, db = da.astype(BF16), db.astype(BF16)
            dab_ref[0, :, cols] = da
            dab_ref[1, :, cols] = db
            t = _dot(da, weight(w1_ref, slots)) + _dot(db, weight(w3_ref, slots))
            dn = t if dn is None else dn + t
        dx, dgt = _rms_bwd_tile(dn, xv, gv)
        dx = dxo + dx
        dx_ref[...] = dx
        dxb_ref[...] = dx.astype(BF16)
        dgp = jnp.sum(dgt, axis=0, keepdims=True)

        @pl.when(i == 0)
        def _():
            dg_ref[...] = dgp
            l_ref[...] = jnp.broadcast_to(part, l_ref.shape)
            dgf_ref[...] = dgfp

        @pl.when(i > 0)
        def _():
            dg_ref[...] += dgp
            l_ref[...] += jnp.broadcast_to(part, l_ref.shape)
            dgf_ref[...] += dgfp

    tile = ((tm, D), lambda i: (i, 0))
    gain = ((1, D), lambda i: (0, 0))
    return _call(name, body, (S // tm,),
                 [(x,) + tile, (g,) + gain,
                  _ffn_whole_w_spec(G, 0), _ffn_whole_w_spec(G, 1), _ffn_whole_w_spec(G, 2),
                  (tgt,) + tile, (g_final,) + gain],
                 [((S, D), BF16) + tile, ((3, S, F), BF16, (3, tm, F), lambda i: (0, i, 0)),
                  ((S, D), BF16) + tile, ((2, S, F), BF16, (2, tm, F), lambda i: (0, i, 0)),
                  ((S, D), F32) + tile, ((S, D), BF16) + tile, ((1, D), F32) + gain,
                  ((1, LANES), F32, (1, LANES), lambda i: (0, 0)), ((1, D), F32) + gain],
                 sem=("arbitrary",))


def _ffn_hidden_grads(dh, av, bv):
    sig = _sigmoid(av)
    return dh * bv * (sig * (1.0 + av * (1.0 - sig))), dh * (av * sig)


def ffn_bwd_hidden(dxo, abh, G, name):
    _, S, F = abh.shape
    D = dxo.shape[1]
    tm = 256

    def body(d_ref, w2_ref, ab_ref, o_ref):
        dh = 0.5 * _dot(d_ref[...].astype(BF16), w2_ref[...].reshape(F, D), 1, 1)
        da, db = _ffn_hidden_grads(dh, ab_ref[0].astype(F32), ab_ref[1].astype(F32))
        o_ref[0] = da.astype(BF16)
        o_ref[1] = db.astype(BF16)

    return _call(name + "_down_bwd", body, (S // tm,),
                 [(dxo, (tm, D), lambda i: (i, 0)), _ffn_whole_w_spec(G, 2),
                  (abh, (2, tm, F), lambda i: (0, i, 0))],
                 [((2, S, F), BF16, (2, tm, F), lambda i: (0, i, 0))],
                 sem=("parallel",))[0]


def ffn_bwd_weights(dxo, abh, dab, n, name):
    _, S, F = abh.shape
    D = dxo.shape[1]
    nf = F // FFN_TF
    tk = WGRAD_TK
    nk = S // tk
    gshape = (N_DEV, 3 * FFN_SHARD, D)

    def dw2_body(h_ref, d_ref, o_ref, acc_ref):
        k = pl.program_id(1)
        p = _dot(h_ref[...], d_ref[...].astype(BF16), 0, 0)

        @pl.when(k == 0)
        def _():
            acc_ref[...] = p

        @pl.when(k > 0)
        def _():
            acc_ref[...] += p

        @pl.when(k == nk - 1)
        def _():
            o_ref[...] = (0.5 * acc_ref[...]).astype(BF16).reshape(4, FFN_SHARD, D)

    gw = _call(name + "_dw2", dw2_body, (nf, nk),
               [(abh, (None, tk, FFN_TF), lambda j, k: (2, k, j)), (dxo, (tk, D), lambda j, k: (k, 0))],
               [(gshape, BF16, (4, FFN_SHARD, D), lambda j, k: (j, 2, 0))],
               scratch=[pltpu.VMEM((FFN_TF, D), F32)], sem=("parallel", "arbitrary"))[0]

    def dw13_body(gw_ref, dab_ref, n_ref, o_ref):
        o_ref[...] = _dot(dab_ref[...], n_ref[...], 0, 0).astype(BF16).reshape(4, FFN_SHARD, D)

    gw = pl.pallas_call(
        dw13_body,
        out_shape=jax.ShapeDtypeStruct(gshape, BF16),
        grid=(2, nf),
        in_specs=[pl.BlockSpec(memory_space=pl.ANY),
                  pl.BlockSpec((None, S, FFN_TF), lambda w, j: (w, 0, j)),
                  pl.BlockSpec((S, D), lambda w, j: (0, 0))],
        out_specs=pl.BlockSpec((4, FFN_SHARD, D), lambda w, j: (j, w, 0)),
        input_output_aliases={0: 0},
        name=name + "_dw13",
        compiler_params=pltpu.CompilerParams(dimension_semantics=("parallel", "parallel"),
                                             vmem_limit_bytes=VMEM_LIMIT),
    )(gw, dab, n)
    return gw


def ffn_bwd_input(dab, G, x_in, g, dxo, name):
    _, S, F = dab.shape
    D = x_in.shape[1]
    tm = 256

    def dn_body(dab_ref, w1_ref, w3_ref, x_ref, d_ref, g_ref, dx_ref, dg_ref):
        i = pl.program_id(0)
        dn = _dot(dab_ref[0], w1_ref[...].reshape(F, D)) + _dot(dab_ref[1], w3_ref[...].reshape(F, D))
        dx, dgt = _rms_bwd_tile(dn, x_ref[...], g_ref[...])
        dx_ref[...] = d_ref[...] + dx
        dgp = jnp.sum(dgt, axis=0, keepdims=True)

        @pl.when(i == 0)
        def _():
            dg_ref[...] = dgp

        @pl.when(i > 0)
        def _():
            dg_ref[...] += dgp

    tile = ((tm, D), lambda i: (i, 0))
    return _call(name + "_dn", dn_body, (S // tm,),
                 [(dab, (2, tm, F), lambda i: (0, i, 0)),
                  _ffn_whole_w_spec(G, 0), _ffn_whole_w_spec(G, 1),
                  (x_in,) + tile, (dxo,) + tile, (g, (1, D), lambda i: (0, 0))],
                 [((S, D), F32) + tile, ((1, D), F32, (1, D), lambda i: (0, 0))],
                 sem=("arbitrary",))


PROJ_TN = 512
DH_SHARDS_PER_STEP = 4


def in_proj(h, Gm, name):
    S, D = h.shape
    n_tiles = N_DEV * Gm.shape[2] // PROJ_TN

    def body(h_ref, w_ref, o_ref):
        o_ref[...] = _dot(h_ref[...], w_ref[...]).astype(BF16)

    return _call(name, body, (n_tiles,),
                 [(h, (S, D), lambda j: (0, 0)),
                  (Gm, (None, D, PROJ_TN), lambda j: (j // 2, 0, j % 2))],
                 [((S, n_tiles * PROJ_TN), BF16, (S, PROJ_TN), lambda j: (0, j))],
                 sem=("parallel",))[0]


def _dproj_pieces(dqkv, dq_b, dkv_b, dgate):
    pieces = [(dqkv[g], [(3 * which + g, (which, 0)) for which in range(3)]) for g in range(3)]
    pieces.append((dq_b, [(9, (None, 0)), (10, (None, 1))]))
    pieces.append((dkv_b, [(11, (None, 0))]))
    pieces.append((dgate, [(12 + 2 * a + b, (a, b)) for a in range(2) for b in range(2)]))
    return pieces


def in_proj_bwd_dw(pieces, h, gm_grads, name):
    S, D = h.shape
    steps = [(n, t, ix) for n, (_, tiles) in enumerate(pieces) for t, ix in tiles]
    n_steps = len(steps)

    def pick(table, j):
        out = table[-1]
        for k in range(len(table) - 2, -1, -1):
            out = jnp.where(j == k, table[k], out)
        return out

    def piece_spec(n, arr):
        own = [k for k, (m, _, _) in enumerate(steps) if m == n]
        at = [steps[min(max(k, own[0]), own[-1])][2] for k in range(n_steps)]
        lead, colb = [ix[0] for ix in at], [ix[1] for ix in at]
        if arr.ndim == 3:
            return (own[0], own[-1]), pl.BlockSpec((None, S, PROJ_TN), lambda j: (pick(lead, j), 0, pick(colb, j)))
        return (own[0], own[-1]), pl.BlockSpec((S, PROJ_TN), lambda j: (0, pick(colb, j)))

    spans, d_specs = zip(*[piece_spec(n, arr) for n, (arr, _) in enumerate(pieces)])
    w_tile = [t for _, t, _ in steps]

    def dw_body(gm_ref, h_ref, *refs):
        o_ref = refs[-1]
        j = pl.program_id(0)
        for d_ref, (first, last) in zip(refs[:-1], spans):
            @pl.when((j >= first) & (j <= last))
            def _(d_ref=d_ref):
                o_ref[...] = _dot(h_ref[...], d_ref[...], 0, 0).astype(BF16)

    return pl.pallas_call(
        dw_body,
        out_shape=jax.ShapeDtypeStruct(gm_grads.shape, BF16),
        grid=(n_steps,),
        in_specs=[pl.BlockSpec(memory_space=pl.ANY),
                  pl.BlockSpec((S, D), lambda j: (0, 0), pipeline_mode=pl.Buffered(1))] + list(d_specs),
        out_specs=pl.BlockSpec((None, D, PROJ_TN), lambda j: (pick(w_tile, j) // 2, 0, pick(w_tile, j) % 2)),
        input_output_aliases={0: 0},
        name=name + "_dw",
        compiler_params=pltpu.CompilerParams(dimension_semantics=("arbitrary",), vmem_limit_bytes=VMEM_LIMIT),
    )(gm_grads, h, *[arr for arr, _ in pieces])


def in_proj_bwd_dh(pieces, Gm, x_in, g, dres, name):
    S, D = x_in.shape
    tm = 256
    C = Gm.shape[2]
    n_sh = N_DEV
    n_p = len(pieces)

    def dh_body(*refs):
        d_refs = refs[:n_p]
        w_ref, x_ref, r_ref, g_ref, dx_ref, dxb_ref, dg_ref = refs[n_p:]
        i = pl.program_id(0)
        p = None
        for d_ref, (arr, tiles) in zip(d_refs, pieces):
            for t, (lead, colb) in tiles:
                cols = slice(colb * PROJ_TN, (colb + 1) * PROJ_TN)
                d = d_ref[:, cols] if lead is None else d_ref[lead, :, cols]
                wcol = (t % 2) * PROJ_TN
                term = _dot(d, w_ref[t // 2, :, wcol:wcol + PROJ_TN], 1, 1)
                p = term if p is None else p + term
        dx, dgt = _rms_bwd_tile(p, x_ref[...], g_ref[...])
        dx = r_ref[...] + dx
        dx_ref[...] = dx
        dxb_ref[...] = dx.astype(BF16)
        dgp = jnp.sum(dgt, axis=0, keepdims=True)

        @pl.when(i == 0)
        def _():
            dg_ref[...] = dgp

        @pl.when(i > 0)
        def _():
            dg_ref[...] += dgp

    tile = ((tm, D), lambda i: (i, 0))

    def rows_of(arr):
        if arr.ndim == 3:
            return (arr, (arr.shape[0], tm, arr.shape[2]), lambda i: (0, i, 0))
        return (arr, (tm, arr.shape[1]), lambda i: (i, 0))

    return _call(name + "_dh", dh_body, (S // tm,),
                 [rows_of(arr) for arr, _ in pieces]
                 + [(Gm, (n_sh, D, C), lambda i: (0, 0, 0), pl.Buffered(1)),
                    (x_in,) + tile, (dres,) + tile, (g, (1, D), lambda i: (0, 0))],
                 [((S, D), F32) + tile, ((S, D), BF16) + tile, ((1, D), F32, (1, D), lambda i: (0, 0))],
                 sem=("arbitrary",))


def _t5_bucket(rel):
    n = N_BUCKETS // 2
    max_exact = n // 2
    ret = jnp.where(rel > 0, n, 0)
    a = jnp.abs(rel)
    af = jnp.maximum(a, 1).astype(F32)
    large = max_exact + (jnp.log(af / max_exact) / math.log(MAX_DISTANCE / max_exact)
                         * (n - max_exact)).astype(jnp.int32)
    large = jnp.minimum(large, n - 1)
    return ret + jnp.where(a < max_exact, a, large)


def _bucket_tables():
    qi = jnp.arange(A_TQ, dtype=jnp.int32)[:, None]
    kj = jnp.arange(A_WIN, dtype=jnp.int32)[None, :]
    rel = kj - HALF_WINDOW - qi
    return jnp.stack([_t5_bucket(rel * d) for d in DILATIONS], axis=0)


def bias_build(rel_bias, buckets):
    def body(tab_ref, bk_ref, o_ref):
        col = pl.program_id(0) * HEADS_PER_GROUP_A + pl.program_id(1)
        bk = bk_ref[...]
        acc = jnp.zeros(bk.shape, F32)
        for b in range(N_BUCKETS):
            acc = jnp.where(bk == b, tab_ref[b, col], acc)
        qi = lax.broadcasted_iota(jnp.int32, bk.shape, 0)
        kj = lax.broadcasted_iota(jnp.int32, bk.shape, 1)
        band = jnp.where(jnp.abs(kj - HALF_WINDOW - qi) <= HALF_WINDOW, acc, NEG_INF)
        o_ref[0] = jnp.where(kj >= HALF_WINDOW, band, NEG_INF)
        o_ref[1] = band
        o_ref[2] = jnp.where(kj < A_TQ + HALF_WINDOW, band, NEG_INF)

    out = pl.pallas_call(
        body,
        out_shape=jax.ShapeDtypeStruct((3, HEADS_PER_GROUP_A // 2, 3, 2, A_TQ, A_WIN), F32),
        grid=(3, HEADS_PER_GROUP_A),
        in_specs=[pl.BlockSpec(memory_space=pltpu.SMEM),
                  pl.BlockSpec((None, A_TQ, A_WIN), lambda g, h: (g, 0, 0))],
        out_specs=pl.BlockSpec((None, None, 3, None, A_TQ, A_WIN), lambda g, h: (g, h // 2, 0, h % 2, 0, 0)),
        name="a_bias_build",
        compiler_params=pltpu.CompilerParams(dimension_semantics=("parallel", "parallel")),
    )(rel_bias, buckets)
    return out.reshape(3, HEADS_PER_GROUP_A // 2, 3, 2 * A_TQ, A_WIN)


def bias_bwd(dbias, buckets):
    def body(d_ref, bk_ref, o_ref):
        bk = bk_ref[...]
        dv = d_ref[...]
        for b in range(N_BUCKETS):
            part = jnp.sum(jnp.where(bk == b, dv, 0.0), axis=1, keepdims=True)
            o_ref[b:b + 1, :] = jnp.broadcast_to(jnp.sum(part, axis=0, keepdims=True), (1, LANES))

    out = pl.pallas_call(
        body,
        out_shape=jax.ShapeDtypeStruct((3, HEADS_PER_GROUP_A, N_BUCKETS, LANES), F32),
        grid=(3, HEADS_PER_GROUP_A),
        in_specs=[pl.BlockSpec((None, None, A_TQ, A_WIN), lambda g, h: (g, h, 0, 0)),
                  pl.BlockSpec((None, A_TQ, A_WIN), lambda g, h: (g, 0, 0))],
        out_specs=pl.BlockSpec((None, None, N_BUCKETS, LANES), lambda g, h: (g, h, 0, 0)),
        name="a_bias_bwd",
        compiler_params=pltpu.CompilerParams(dimension_semantics=("parallel", "parallel")),
    )(dbias, buckets)
    return out[:, :, :, 0].transpose(2, 0, 1).reshape(N_BUCKETS, 3 * HEADS_PER_GROUP_A)


def _a_fill_padded(pad_ref, src_ref, n, pad):
    zeros = jnp.zeros((pad, LANES), pad_ref.dtype)
    pad_ref[0:pad, :] = zeros
    pad_ref[pad + n:2 * pad + n, :] = zeros
    pad_ref[pad:pad + n, :] = src_ref[...].astype(pad_ref.dtype)


def _a_stack_heads(x, lane):
    zero = jnp.zeros_like(x)
    return jnp.concatenate([jnp.where(lane < HEAD_DIM_A, x, zero), jnp.where(lane >= HEAD_DIM_A, x, zero)], axis=0)


def _a_bias_variant(qb, nqb):
    return jnp.where(qb == 0, 0, jnp.where(qb == nqb - 1, 2, 1))


def _a_slab_specs(proj, g):
    S = proj.shape[0]
    per = GROUP_WIDTH_A // LANES
    return [(proj, (S, LANES), lambda hp, w=w: (0, per * (3 * w + g) + hp)) for w in range(3)]


def a_fwd(proj, bias, g, name):
    S = proj.shape[0]
    d = DILATIONS[g]
    L = S // d
    nqb = L // A_TQ
    pad = HALF_WINDOW * d

    def body(q_ref, k_ref, v_ref, b_ref, o_ref, l_ref, qf, kpad, vpad):
        qf[...] = q_ref[...].astype(F32) * A_SCALE
        _a_fill_padded(kpad, k_ref, S, pad)
        _a_fill_padded(vpad, v_ref, S, pad)
        lane = lax.broadcasted_iota(jnp.int32, (A_TQ, LANES), 1)

        def block(t, carry):
            qb, r = t // d, t % d
            start = qb * (A_TQ * d) + r
            kw = kpad[pl.ds(start, A_WIN, stride=d), :].astype(BF16)
            vw = vpad[pl.ds(start, A_WIN, stride=d), :].astype(BF16)
            q = qf[pl.ds(start, A_TQ, stride=d), :].astype(BF16)
            q2 = _a_stack_heads(q, lane)
            s = _dot(q2, kw, 1, 1) + b_ref[_a_bias_variant(qb, nqb)]
            m = jnp.max(s, axis=-1, keepdims=True)
            e = jnp.exp(s - m)
            l = jnp.sum(e, axis=-1, keepdims=True)
            o2 = _dot(e.astype(BF16), vw) / l
            lse2 = m + jnp.log(l)
            o_ref[pl.ds(start, A_TQ, stride=d), :] = jnp.where(lane < HEAD_DIM_A, o2[0:A_TQ], o2[A_TQ:])
            l_ref[pl.ds(start, A_TQ, stride=d), :] = jnp.where(lane < HEAD_DIM_A, lse2[0:A_TQ], lse2[A_TQ:])
            return carry

        lax.fori_loop(0, nqb * d, block, 0, unroll=A_UNROLL)

    out_spec = ((S, GROUP_WIDTH_A), F32, (S, LANES), lambda hp: (0, hp))
    return _call(name, body, (4,),
                 _a_slab_specs(proj, g)
                 + [(bias, (None, None, 3, 2 * A_TQ, A_WIN), lambda hp: (g, hp, 0, 0, 0))],
                 [out_spec, out_spec],
                 scratch=[pltpu.VMEM((S, LANES), F32)] + [pltpu.VMEM((S + 2 * pad, LANES), F32)] * 2,
                 sem=("parallel",))


def a_combine(outs, lses, name):
    S, W = outs[0].shape
    tr = 512

    def body(o0, o1, o2, l0, l1, l2, oa_ref, lt_ref):
        a, b, c = l0[...], l1[...], l2[...]
        m = jnp.maximum(jnp.maximum(a, b), c)
        ea, eb, ec = jnp.exp(a - m), jnp.exp(b - m), jnp.exp(c - m)
        z = ea + eb + ec
        oa_ref[...] = ((ea * o0[...] + eb * o1[...] + ec * o2[...]) / z).astype(BF16)
        lt_ref[...] = m + jnp.log(z)

    spec = ((tr, W), lambda i: (i, 0))
    return _call(name, body, (S // tr,), [(a,) + spec for a in (*outs, *lses)],
                 [((S, W), BF16) + spec, ((S, W), F32) + spec], sem=("parallel",))


def a_bwd(proj, bias, do_a, o_a, lse_tot, g, name):
    S = proj.shape[0]
    d = DILATIONS[g]
    L = S // d
    nqb = L // A_TQ
    pad = HALF_WINDOW * d

    def body(q_ref, k_ref, v_ref, b_ref, do_ref, o_ref, l_ref, dqkv_ref, db_ref,
             qf, of, dqf, kpad, vpad, dkacc, dvacc):
        qf[...] = q_ref[...].astype(F32) * A_SCALE
        of[...] = o_ref[...].astype(F32)
        _a_fill_padded(kpad, k_ref, S, pad)
        _a_fill_padded(vpad, v_ref, S, pad)
        dkacc[...] = jnp.zeros(dkacc.shape, F32)
        dvacc[...] = jnp.zeros(dvacc.shape, F32)
        db_ref[...] = jnp.zeros(db_ref.shape, F32)
        lane = lax.broadcasted_iota(jnp.int32, (A_TQ, LANES), 1)

        def block(t, carry):
            qb, r = t // d, t % d
            start = qb * (A_TQ * d) + r
            rows = pl.ds(start, A_TQ, stride=d)
            win = pl.ds(start, A_WIN, stride=d)
            kw = kpad[win, :].astype(BF16)
            vw = vpad[win, :].astype(BF16)
            q = qf[rows, :].astype(BF16)
            do = do_ref[rows, :]
            ov = of[rows, :]
            lt = l_ref[rows, :]
            q2 = _a_stack_heads(q, lane)
            do2 = _a_stack_heads(do, lane)
            lt2 = jnp.concatenate([lt[:, 0:1], lt[:, HEAD_DIM_A:HEAD_DIM_A + 1]], axis=0)
            s = _dot(q2, kw, 1, 1) + b_ref[_a_bias_variant(qb, nqb)]
            p = jnp.exp(s - lt2)
            t = jnp.sum(do2 * jnp.concatenate([ov, ov], axis=0), axis=-1, keepdims=True)
            dob2 = do2.astype(BF16)
            ds = p * (_dot(dob2, vw, 1, 1) - t)
            db_ref[...] += ds
            dsb = ds.astype(BF16)
            dq2 = _dot(dsb, kw)
            dqf[rows, :] = jnp.where(lane < HEAD_DIM_A, dq2[0:A_TQ], dq2[A_TQ:]) * A_SCALE
            dkacc[win, :] += _dot(dsb, q2, 0, 0)
            dvacc[win, :] += _dot(p.astype(BF16), dob2, 0, 0)
            return carry

        lax.fori_loop(0, nqb * d, block, 0, unroll=A_UNROLL)
        dqkv_ref[0] = dqf[...].astype(BF16)
        dqkv_ref[1] = dkacc[pad:pad + S, :].astype(BF16)
        dqkv_ref[2] = dvacc[pad:pad + S, :].astype(BF16)

    slab = ((S, LANES), lambda hp: (0, hp))
    padded = pltpu.VMEM((S + 2 * pad, LANES), F32)
    return _call(
        name, body, (4,),
        _a_slab_specs(proj, g)
        + [(bias, (None, None, 3, 2 * A_TQ, A_WIN), lambda hp: (g, hp, 0, 0, 0)),
           (do_a,) + slab, (o_a,) + slab, (lse_tot,) + slab],
        [((3, S, GROUP_WIDTH_A), BF16, (3, S, LANES), lambda hp: (0, 0, hp)),
         ((4, 2 * A_TQ, A_WIN), F32, (None, 2 * A_TQ, A_WIN), lambda hp: (hp, 0, 0))],
        scratch=[pltpu.VMEM((S, LANES), F32)] * 3 + [padded] * 4,
        sem=("parallel",))


def _rope_tables(S):
    rows = S // GRID_W
    row = jnp.repeat(jnp.arange(rows, dtype=F32), GRID_W)
    col = jnp.tile(jnp.arange(GRID_W, dtype=F32), rows)
    n_freq = HEAD_DIM_B // 4
    freq = ROPE_THETA ** (-jnp.arange(n_freq, dtype=F32) / n_freq)
    ang = jnp.concatenate([row[:, None] * freq, col[:, None] * freq], axis=-1)
    cos, sin = jnp.cos(ang), jnp.sin(ang)
    return jnp.repeat(cos, 2, axis=-1), jnp.stack([-sin, sin], axis=-1).reshape(S, HEAD_DIM_B)


def _swap_pairs(y):
    lane = lax.broadcasted_iota(jnp.int32, y.shape, 1)
    return jnp.where(lane % 2 == 0, pltpu.roll(y, LANES - 1, 1), pltpu.roll(y, 1, 1))


def qkv_prep(proj, gains, cos_t, sin_t, name):
    S = proj.shape[0]
    ts = 256
    n_rot = N_HEADS_B + N_KV_B
    nh = n_rot + N_KV_B
    W = nh * LANES

    def body(x_ref, g_ref, c_ref, s_ref, o_ref):
        cv, sv = c_ref[...], s_ref[...]
        for hb in range(nh):
            cols = slice(hb * LANES, (hb + 1) * LANES)
            if hb < n_rot:
                xv = x_ref[:, cols].astype(F32)
                r = lax.rsqrt(jnp.mean(xv * xv, axis=-1, keepdims=True) + EPS)
                yv = xv * r * g_ref[:, cols]
                o_ref[:, cols] = (yv * cv + _swap_pairs(yv) * sv).astype(BF16)
            else:
                o_ref[:, cols] = x_ref[:, cols]

    return _call(name, body, (S // ts,),
                 [(proj, (ts, W), lambda i: (i, A_QKV_WIDTH // W)), (gains, (1, W), lambda i: (0, 0)),
                  (cos_t, (ts, LANES), lambda i: (i, 0)), (sin_t, (ts, LANES), lambda i: (i, 0))],
                 [((S, W), BF16, (ts, W), lambda i: (i, 0))],
                 sem=("parallel",))[0]


def qk_prep_bwd(dr, proj, col0, gain, cos_t, sin_t, name):
    S, W = dr.shape
    H = W // LANES
    ts = 256
    wx = math.gcd(W, col0)
    n_x = W // wx

    def body(d_ref, *refs):
        x_refs = refs[:n_x]
        g_ref, c_ref, s_ref, dx_ref, dg_ref = refs[n_x:]
        i = pl.program_id(0)
        cv, sv, gv = c_ref[...], s_ref[...], g_ref[...]
        dgp = jnp.zeros((1, LANES), F32)
        for hb in range(H):
            cols = slice(hb * LANES, (hb + 1) * LANES)
            xc = (hb * LANES) % wx
            xv = x_refs[(hb * LANES) // wx][:, xc:xc + LANES].astype(F32)
            dout = d_ref[:, cols]
            dy = dout * cv + _swap_pairs(dout * sv)
            dx, dgt = _rms_bwd_tile(dy, xv, gv)
            dx_ref[:, cols] = dx.astype(BF16)
            dgp = dgp + jnp.sum(dgt, axis=0, keepdims=True)

        @pl.when(i == 0)
        def _():
            dg_ref[...] = dgp

        @pl.when(i > 0)
        def _():
            dg_ref[...] += dgp

    return _call(name, body, (S // ts,),
                 [(dr, (ts, W), lambda i: (i, 0))]
                 + [(proj, (ts, wx), lambda i, k=k: (i, col0 // wx + k)) for k in range(n_x)]
                 + [(gain, (1, LANES), lambda i: (0, 0)),
                  (cos_t, (ts, LANES), lambda i: (i, 0)), (sin_t, (ts, LANES), lambda i: (i, 0))],
                 [((S, W), BF16, (ts, W), lambda i: (i, 0)),
                  ((1, LANES), F32, (1, LANES), lambda i: (0, 0))],
                 sem=("arbitrary",))


def _row_sums(x):
    hi = x.astype(BF16)
    lo = (x - hi.astype(F32)).astype(BF16)
    ones = jnp.ones((8, LANES), BF16)
    return (_dot(ones, hi, 1, 1) + _dot(ones, lo, 1, 1))[0:1, :]


def flash_fwd(qkv, name):
    S = qkv.shape[0]
    tq = B_TQ_FWD
    hps = B_HEADS_PER_STEP

    def body(q_ref, k_ref, v_ref, o_ref, l_ref):
        k, v = k_ref[...], v_ref[...]
        for j in range(hps):
            cols = slice(j * LANES, (j + 1) * LANES)
            s = _dot(q_ref[:, cols], k, 1, 1)
            m = jnp.max(s, axis=-1, keepdims=True)
            e = jnp.exp2(s - m)
            l = jnp.sum(e, axis=-1, keepdims=True)
            o_ref[:, cols] = (_dot(e.astype(BF16), v) / l).astype(BF16)
            lse = jnp.broadcast_to(m * (1.0 / LOG2E) + jnp.log(l), (tq, LANES))
            l_ref[j] = _row_sums(lse) * (1.0 / LANES)

    per = GQA_GROUP_B // hps
    heads = lambda g, h, i: (i, g * per + h)
    return _call(name, body, (N_KV_B, per, S // tq),
                 [(qkv, (tq, hps * LANES), heads),
                  (qkv, (S, LANES), lambda g, h, i: (0, N_HEADS_B + g)),
                  (qkv, (S, LANES), lambda g, h, i: (0, N_HEADS_B + N_KV_B + g))],
                 [((S, N_HEADS_B * LANES), BF16, (tq, hps * LANES), heads),
                  ((N_HEADS_B, 1, S), F32, (hps, 1, tq), lambda g, h, i: (g * per + h, 0, i))],
                 sem=("parallel", "parallel", "parallel"))


def flash_bwd(qkv, k_t, do_b, o_b, lse, name):
    S = qkv.shape[0]
    tq = B_TQ_BWD
    nq = S // tq
    scale = HEAD_DIM_B ** -0.5

    def body(q_ref, k_ref, v_ref, kt_ref, do_ref, o_ref, l_ref, dq_ref, dk_ref, dv_ref, dkacc, dvacc):
        h, i = pl.program_id(1), pl.program_id(2)

        @pl.when((h == 0) & (i == 0))
        def _():
            dkacc[...] = jnp.zeros(dkacc.shape, F32)
            dvacc[...] = jnp.zeros(dvacc.shape, F32)

        q = q_ref[...]
        dob = do_ref[...]
        t = _row_sums(dob.astype(F32) * o_ref[...].astype(F32))
        pt = jnp.exp2(_dot(k_ref[...], q, 1, 1) - l_ref[...] * LOG2E)
        dsb = (pt * (_dot(v_ref[...], dob, 1, 1) - t)).astype(BF16)
        dvacc[...] += _dot(pt.astype(BF16), dob)
        dkacc[...] += _dot(dsb, q)
        dq_ref[...] = _dot(kt_ref[...], dsb).T * scale

        @pl.when((h == GQA_GROUP_B - 1) & (i == nq - 1))
        def _():
            dk_ref[...] = dkacc[...] * (scale / B_Q_PRESCALE)
            dv_ref[...] = dvacc[...].astype(BF16)

    head = lambda g, h, i: (i, g * GQA_GROUP_B + h)
    return _call(name, body, (N_KV_B, GQA_GROUP_B, nq),
                 [(qkv, (tq, LANES), head),
                  (qkv, (S, LANES), lambda g, h, i: (0, N_HEADS_B + g)),
                  (qkv, (S, LANES), lambda g, h, i: (0, N_HEADS_B + N_KV_B + g)),
                  (k_t, (LANES, S), lambda g, h, i: (g, 0)),
                  (do_b, (tq, LANES), head), (o_b, (tq, LANES), head),
                  (lse, (None, 1, tq), lambda g, h, i: (g * GQA_GROUP_B + h, 0, i))],
                 [((S, N_HEADS_B * LANES), F32, (tq, LANES), head),
                  ((S, N_KV_B * LANES), F32, (S, LANES), lambda g, h, i: (0, g)),
                  ((S, N_KV_B * LANES), BF16, (S, LANES), lambda g, h, i: (0, g))],
                 scratch=[pltpu.VMEM((S, LANES), F32)] * 2,
                 sem=("parallel", "arbitrary", "arbitrary"))


MERGE_TN = 512


def _mix_rows_spec(Gm, row0, n_slots, slot_map, cols=None, col_map=None):
    C = Gm.shape[2] if cols is None else cols
    cm = (lambda *idx: 0) if col_map is None else col_map
    return (Gm, (n_slots, LANES, C), lambda *idx: (slot_map(*idx), row0 // LANES, cm(*idx)))


def _gate_specs(proj, tm):
    first = (A_QKV_WIDTH + PB_GATE_A) // MERGE_TN
    return [(proj, (tm, MERGE_TN), lambda i, k=k: (i, first + k)) for k in range(4)]


def _whole_rows_spec(Gm, row0):
    return _mix_rows_spec(Gm, row0, N_DEV, lambda *idx: 0)


def merge_fwd(o_a, o_b, w_a, Gm, proj, b_gate, x, name):
    S, D = x.shape
    tm = 256

    def body(oa_ref, ob_ref, wa_ref, wb_ref, wo_ref, g0, g1, g2, g3, bg_ref, x_ref, m_ref, ya_ref, yb_ref, xo_ref):
        ya = _dot(oa_ref[...], wa_ref[...])
        yb = _dot(ob_ref[...], wb_ref[...].reshape(N_DEV * LANES, D))
        ga = _sigmoid(jnp.concatenate([g0[...], g1[...]], axis=1).astype(F32) + bg_ref[:, 0:D])
        gb = _sigmoid(jnp.concatenate([g2[...], g3[...]], axis=1).astype(F32) + bg_ref[:, D:2 * D])
        merged = (ga * ya + gb * yb).astype(BF16)
        m_ref[...] = merged
        ya_ref[...] = ya.astype(BF16)
        yb_ref[...] = yb.astype(BF16)
        xo_ref[...] = x_ref[...] + _dot(merged, wo_ref[...].reshape(N_DEV * LANES, D))

    rows = lambda a: (a, (tm, a.shape[1]), lambda i: (i, 0))
    out = ((S, D), BF16, (tm, D), lambda i: (i, 0))
    return _call(name, body, (S // tm,),
                 [rows(o_a), rows(o_b), (w_a, w_a.shape, lambda i: (0, 0)),
                  _whole_rows_spec(Gm, REST_WB), _whole_rows_spec(Gm, REST_WOUT)]
                 + _gate_specs(proj, tm) + [(b_gate, (1, 2 * D), lambda i: (0, 0)), rows(x)],
                 [out, out, out, ((S, D), F32, (tm, D), lambda i: (i, 0))], sem=("parallel",))


def merge_bwd(dx2, w_a, Gm, ya, yb, proj, b_gate, name):
    S, D = dx2.shape
    tm = 256

    def body(d_ref, wo_ref, wa_ref, wb_ref, ya_ref, yb_ref, g0, g1, g2, g3, bg_ref,
             dya_ref, dyb_ref, dg_ref, dbg_ref, doa_ref, dob_ref):
        i = pl.program_id(0)
        dm = _dot(d_ref[...].astype(BF16), wo_ref[...].reshape(N_DEV * LANES, D), 1, 1)
        ga = _sigmoid(jnp.concatenate([g0[...], g1[...]], axis=1).astype(F32) + bg_ref[:, 0:D])
        gb = _sigmoid(jnp.concatenate([g2[...], g3[...]], axis=1).astype(F32) + bg_ref[:, D:2 * D])
        dya = (dm * ga).astype(BF16)
        dyb = (dm * gb).astype(BF16)
        dya_ref[...] = dya
        dyb_ref[...] = dyb
        dpa = dm * ya_ref[...].astype(F32) * ga * (1.0 - ga)
        dpb = dm * yb_ref[...].astype(F32) * gb * (1.0 - gb)
        dg_ref[0] = dpa.astype(BF16)
        dg_ref[1] = dpb.astype(BF16)
        doa_ref[...] = _dot(dya, wa_ref[...], 1, 1)
        dob_ref[...] = _dot(dyb, wb_ref[...].reshape(N_DEV * LANES, D), 1, 1).astype(BF16)
        sa =jnp.sum(dpa, axis=0, keepdims=True)
        sb = jnp.sum(dpb, axis=0, keepdims=True)

        @pl.when(i == 0)
        def _():
            dbg_ref[0] = sa
            dbg_ref[1] = sb

        @pl.when(i > 0)
        def _():
            dbg_ref[0] += sa
            dbg_ref[1] += sb

    tile = ((tm, D), lambda i: (i, 0))
    return _call(
        name, body, (S // tm,),
        [(dx2,) + tile, _whole_rows_spec(Gm, REST_WOUT), (w_a, w_a.shape, lambda i: (0, 0)),
         _whole_rows_spec(Gm, REST_WB), (ya,) + tile, (yb,) + tile]
        + _gate_specs(proj, tm) + [(b_gate, (1, 2 * D), lambda i: (0, 0))],
        [((S, D), BF16) + tile, ((S, D), BF16) + tile,
         ((2, S, D), BF16, (2, tm, D), lambda i: (0, i, 0)),
         ((2, 1, D), F32, (2, 1, D), lambda i: (0, 0, 0)),
         ((S, w_a.shape[0]), F32, (tm, w_a.shape[0]), lambda i: (i, 0)),
         ((S, N_HEADS_B * LANES), BF16, (tm, N_HEADS_B * LANES), lambda i: (i, 0))],
        sem=("arbitrary",))


def weight_grad_rows(a, b, grads, row0, name):
    S, M = a.shape
    N = b.shape[1]
    tmm = 512
    tk = WGRAD_TK
    nk = S // tk
    prior = [] if grads is None else [grads]

    def body(*refs):
        a_ref, b_ref, o_ref, acc_ref = refs[len(prior):]
        k = pl.program_id(1)
        p = _dot(a_ref[...], b_ref[...].astype(BF16), 0, 0)

        @pl.when(k == 0)
        def _():
            acc_ref[...] = p

        @pl.when(k > 0)
        def _():
            acc_ref[...] += p

        @pl.when(k == nk - 1)
        def _():
            o_ref[...] = acc_ref[...].astype(BF16).reshape(tmm // LANES, LANES, N)

    return pl.pallas_call(
        body,
        out_shape=jax.ShapeDtypeStruct((N_DEV, MIX_ROWS, N), BF16),
        grid=(M // tmm, nk),
        in_specs=[pl.BlockSpec(memory_space=pl.ANY)] * len(prior)
        + [pl.BlockSpec((tk, tmm), lambda j, k: (k, j)),
           pl.BlockSpec((tk, N), lambda j, k: (k, 0))],
        out_specs=pl.BlockSpec((tmm // LANES, LANES, N), lambda j, k: (j, row0 // LANES, 0)),
        scratch_shapes=[pltpu.VMEM((tmm, N), F32)],
        input_output_aliases={0: 0} if prior else {},
        name=name,
        compiler_params=pltpu.CompilerParams(dimension_semantics=("parallel", "arbitrary"),
                                             vmem_limit_bytes=VMEM_LIMIT),
    )(*prior, a, b)


def weight_grad_plain(a, b, name):
    S, M = a.shape
    N = b.shape[1]
    tk = WGRAD_TK
    nk = S // tk

    def body(a_ref, b_ref, o_ref, acc_ref):
        k = pl.program_id(0)
        p = _dot(a_ref[...], b_ref[...], 0, 0)

        @pl.when(k == 0)
        def _():
            acc_ref[...] = p

        @pl.when(k > 0)
        def _():
            acc_ref[...] += p

        @pl.when(k == nk - 1)
        def _():
            o_ref[...] = acc_ref[...].astype(BF16)

    return _call(name, body, (nk,),
                 [(a, (tk, M), lambda k: (k, 0)), (b, (tk, N), lambda k: (k, 0))],
                 [((M, N), BF16, (M, N), lambda k: (0, 0))],
                 scratch=[pltpu.VMEM((M, N), F32)], sem=("arbitrary",))[0]


def local_step(x, tgt, p, get_g1_up, get_g1_down, get_gm_in, get_gm_rest, get_g2, emit, start_token):
    S, D = x.shape
    after = lambda t: t[0:1, 0:1]
    buckets = _bucket_tables()
    cos_t, sin_t = _rope_tables(S)
    gains = jnp.concatenate([jnp.tile(p["q_norm"] * B_Q_PRESCALE, (1, N_HEADS_B)), jnp.tile(p["k_norm"], (1, N_KV_B)),
                             jnp.ones((1, N_KV_B * LANES), F32)], axis=1)

    n1 = rms_fwd(x, p["ffn1_norm"] + after(start_token), "ffn1_norm")
    bias = bias_build(p["rel_bias"] + after(start_token), buckets)
    g1_up = get_g1_up((n1, bias))
    ab1 = ffn_up(n1, (g1_up, None), "ffn1_up")
    G1 = (g1_up, get_g1_down(ab1))
    x1, hm = ffn_down(ab1, G1, x, p["mix_norm"], "ffn1_down")
    Gw = get_gm_in(hm)
    proj = in_proj(hm, Gw, "in_proj")

    outs, lses = [], []
    for g in range(3):
        o, l = a_fwd(proj, bias, g, "a_fwd_%d" % g)
        outs.append(o)
        lses.append(l)
    o_a, lse_tot = a_combine(outs, lses, "a_combine")

    qkv = qkv_prep(proj, gains, cos_t, sin_t, "qkv_prep")
    k_t = qkv[:, N_HEADS_B * LANES:(N_HEADS_B + N_KV_B) * LANES].T
    o_b, lse_b = flash_fwd(qkv, "flash_fwd")

    Gm = get_gm_rest(o_b)
    w_a = Gm[:, REST_WA:REST_ROWS, :].reshape(N_DEV, GROUP_WIDTH_A, LANES).transpose(1, 0, 2).reshape(GROUP_WIDTH_A, D)
    merged, ya, yb, x2 = merge_fwd(o_a, o_b, w_a, Gm, proj, p["b_gate"], x1, "merge_fwd")

    G2 = get_g2(x2)
    n2, ab2, dx3_b, dab2, dx2, dx2_b, d_ffn2_norm, loss, d_final = ffn_last(
        x2, p["ffn2_norm"], G2, tgt, p["final_norm"], "ffn2")
    gw2 = ffn_bwd_weights(dx3_b, ab2, dab2, n2, "ffn2_bwd")
    t2 = emit("ffn2", gw2)

    dya, dyb, dgate, dbg, do_a, do_b = merge_bwd(dx2_b, w_a, Gm, ya, yb, proj, p["b_gate"] + after(t2),
                                                 "merge_bwd")
    gm_grads = weight_grad_rows(merged, dx2_b, None, MIX_WOUT, "dw_out")
    gm_grads = weight_grad_rows(o_b, dyb, gm_grads, MIX_WB, "dw_branch_b")
    dw_a = weight_grad_plain(o_a, dya, "dw_branch_a")

    dq_r, dk_r, dv_b = flash_bwd(qkv, k_t, do_b, o_b, lse_b, "flash_bwd")
    dq_b, d_q_norm = qk_prep_bwd(dq_r, proj, A_QKV_WIDTH, p["q_norm"], cos_t, sin_t, "q_prep_bwd")
    dk_b, d_k_norm = qk_prep_bwd(dk_r, proj, A_QKV_WIDTH + N_HEADS_B * LANES, p["k_norm"], cos_t, sin_t,
                                 "k_prep_bwd")

    dqkv, dbs = [], []
    for g in range(3):
        dg_, db = a_bwd(proj, bias, do_a, o_a, lse_tot, g, "a_bwd_%d" % g)
        dqkv.append(dg_)
        dbs.append(db)
    d_rel_bias = bias_bwd(jnp.stack(dbs, axis=0).reshape(3, HEADS_PER_GROUP_A, A_TQ, A_WIN), buckets)

    dproj = _dproj_pieces(dqkv, dq_b, jnp.concatenate([dk_b, dv_b], axis=1), dgate)
    gm_grads = in_proj_bwd_dw(dproj[:3], hm, gm_grads, "in_proj_bwd_a")
    gm_grads = in_proj_bwd_dw(dproj[3:], hm, gm_grads, "in_proj_bwd_b")
    dw_a_sh = dw_a.reshape(GROUP_WIDTH_A, N_DEV, LANES).transpose(1, 0, 2).reshape(N_DEV, MIX_ROWS - MIX_WA, D)
    gm_grads = lax.dynamic_update_slice(gm_grads, dw_a_sh, (0, MIX_WA, 0))
    tm = emit("mix", gm_grads)
    dx1, dx1_b, d_mix_norm = in_proj_bwd_dh(dproj, Gw, x1, p["mix_norm"] + after(tm), dx2, "in_proj_bwd")

    dab1 = ffn_bwd_hidden(dx1_b, ab1, G1, "ffn1_bwd")
    gw1 = ffn_bwd_weights(dx1_b, ab1, dab1, n1, "ffn1_bwd")
    t1 = emit("ffn1", gw1)
    dx0, d_ffn1_norm = ffn_bwd_input(dab1, G1, x, p["ffn1_norm"] + after(t1), dx1, "ffn1_bwd")

    small = dict(ffn1_norm=d_ffn1_norm, mix_norm=d_mix_norm, b_gate=dbg.reshape(1, 2 * D),
                 q_norm=d_q_norm, k_norm=d_k_norm, rel_bias=d_rel_bias, ffn2_norm=d_ffn2_norm,
                 final_norm=d_final)
    return loss, dx0, small


def _pack_small(t, loss_row):
    row6 = jnp.concatenate([t["q_norm"].reshape(1, -1), t["k_norm"].reshape(1, -1), t["rel_bias"].reshape(1, -1)], axis=1)
    return jnp.concatenate([t["ffn1_norm"].reshape(1, -1), t["mix_norm"].reshape(1, -1), t["b_gate"].reshape(2, -1),
                            t["ffn2_norm"].reshape(1, -1), t["final_norm"].reshape(1, -1), row6, loss_row], axis=0)


def _unpack_small(a, shapes):
    return dict(ffn1_norm=a[0:1].reshape(shapes["ffn1_norm"]), mix_norm=a[1:2].reshape(shapes["mix_norm"]),
                b_gate=a[2:4].reshape(shapes["b_gate"]), ffn2_norm=a[4:5].reshape(shapes["ffn2_norm"]),
                final_norm=a[5].reshape(shapes["final_norm"]), q_norm=a[6:7, 0:128].reshape(shapes["q_norm"]),
                k_norm=a[6:7, 128:256].reshape(shapes["k_norm"]), rel_bias=a[6, 256:1024].reshape(shapes["rel_bias"]))


SMALL = ("ffn1_norm", "mix_norm", "b_gate", "q_norm", "k_norm", "rel_bias", "ffn2_norm", "final_norm")
ORDER = ("ffn1_norm", "ffn1_w1", "ffn1_w3", "ffn1_w2", "mix_norm", "w_in", "b_gate", "q_norm", "k_norm", "rel_bias",
         "w_branch_a", "w_branch_b", "w_out", "ffn2_norm", "ffn2_w1", "ffn2_w3", "ffn2_w2", "final_norm")


def kernel(x, ffn1_norm, ffn1_w1, ffn1_w3, ffn1_w2, mix_norm, w_in, b_gate, q_norm, k_norm, rel_bias, w_branch_a, w_branch_b, w_out, ffn2_norm, ffn2_w1, ffn2_w3, ffn2_w2, final_norm, loss_target, m_ffn1_norm, m_ffn1_w1, m_ffn1_w3, m_ffn1_w2, m_mix_norm, m_w_in, m_b_gate, m_q_norm, m_k_norm, m_rel_bias, m_w_branch_a, m_w_branch_b, m_w_out, m_ffn2_norm, m_ffn2_w1, m_ffn2_w3, m_ffn2_w2, m_final_norm, v_ffn1_norm, v_ffn1_w1, v_ffn1_w3, v_ffn1_w2, v_mix_norm, v_w_in, v_b_gate, v_q_norm, v_k_norm, v_rel_bias, v_w_branch_a, v_w_branch_b, v_w_out, v_ffn2_norm, v_ffn2_w1, v_ffn2_w3, v_ffn2_w2, v_final_norm):
    args = dict(locals())
    w = {n: args[n] for n in ORDER}
    m = {n: args["m_" + n] for n in ORDER}
    v = {n: args["v_" + n] for n in ORDER}
    D = x.shape[2]

    blocks = (
        ("ffn1_up", jnp.concatenate([ffn1_w1[0].T, ffn1_w3[0].T], axis=0)),
        ("ffn1_down", ffn1_w2[0]),
        ("mix_in", w_in[0]),
        ("mix_rest", jnp.concatenate([w_branch_b[0], w_out[0], w_branch_a[0].reshape(REST_ROWS - REST_WA, D)], axis=0)),
        ("ffn2", jnp.concatenate([ffn2_w1[0].T, ffn2_w3[0].T, ffn2_w2[0]], axis=0)),
    )
    direct = ("mix_rest", "ffn2")
    started = all_gather_start_all([(b.astype(BF16), tag in direct) for tag, b in blocks], "all_gather_start")
    gathers = {tag: s for (tag, _), s in zip(blocks, started)}
    start_token = started[0][4]

    def gathered(tag):
        def get(after):
            if tag in direct:
                return all_gather_place_own(*_split_wait("all_gather_" + tag + "_wait", gathers[tag], N_DEV - 1, after),
                                            "all_gather_" + tag + "_own")
            return all_gather_finish(*_split_wait("all_gather_" + tag + "_wait", gathers[tag], 4, after),
                                     "all_gather_" + tag + "_finish")
        return get

    core = lax.axis_index("c").astype(jnp.int32).reshape(1)
    chip = (2 * lax.axis_index("x") + lax.axis_index("y")).astype(jnp.int32).reshape(1)
    device = 2 * chip + core
    exchanges = {}

    def emit(tag, gw):
        if tag == "ffn1":
            (theirs,) = reduce_scatter_pair([gw], "reduce_scatter_pair_" + tag)
            part = pair_add(gw, theirs, core, "pair_add_" + tag)
            exchanges[tag] = reduce_scatter_start(part, "reduce_scatter_" + tag + "_start")
        else:
            exchanges[tag] = reduce_scatter_start_direct(gw, "reduce_scatter_" + tag + "_start")
        return exchanges[tag][4]

    small_p = dict(ffn1_norm=ffn1_norm, mix_norm=mix_norm, b_gate=b_gate, q_norm=q_norm, k_norm=k_norm,
                   rel_bias=rel_bias, ffn2_norm=ffn2_norm, final_norm=final_norm.reshape(1, D))
    loss_p, grad_x, small_g = local_step(x[0], loss_target[0], small_p, gathered("ffn1_up"), gathered("ffn1_down"),
                                         gathered("mix_in"), gathered("mix_rest"), gathered("ffn2"), emit, start_token)

    def landed(tag, after):
        n_others, me = (3, chip) if tag == "ffn1" else (N_DEV - 1, device)
        return tuple(_split_wait("reduce_scatter_" + tag + "_wait", exchanges[tag], n_others, after)) + (me,)

    grads, delta, new_m, new_v = {}, {}, {}, {}

    def finish(n, part, land, me, off, blk, transposed=False):
        shp = w[n].shape
        if transposed:
            to2 = lambda a: a.reshape(shp[-2], shp[-1]).T
            back = lambda a: a.T.reshape(shp)
        else:
            to2 = lambda a: a.reshape(shp[-2], shp[-1])
            back = lambda a: a.reshape(shp)
        res = sum_adamw(part, land, me, off, blk, to2(w[n]), to2(m[n]), to2(v[n]), "update_" + n)
        grads[n], delta[n], new_m[n], new_v[n] = [back(a) for a in res]

    last_token = exchanges["ffn1"][4]
    for tag, after in (("ffn2", last_token), ("ffn1", grad_x)):
        group = landed(tag, after)
        finish(tag + "_w1", *group, 0, FFN_SHARD, transposed=True)
        finish(tag + "_w3", *group, FFN_SHARD, FFN_SHARD, transposed=True)
        finish(tag + "_w2", *group, 2 * FFN_SHARD, FFN_SHARD)
        if tag == "ffn2":
            group_m = landed("mix", last_token)
            finish("w_in", *group_m, MIX_WIN, LANES)
            finish("w_branch_b", *group_m, MIX_WB, LANES)
            finish("w_out", *group_m, MIX_WOUT, LANES)
            grads["w_branch_a"] = sum_landed(*group_m, MIX_WA, MIX_ROWS - MIX_WA, MIX_ROWS - MIX_WA,
                                             "w_branch_a_sum").reshape(w_branch_a.shape)
    loss_row = jnp.pad(loss_p, ((0, 0), (0, D - LANES)))
    smalls = small_all_gather(_pack_small(small_g, loss_row), new_v["w_in"])
    small_sum = sum_slots(smalls, 0, N_DEV, N_DEV, "small_sum")
    small_shapes = {n: w[n].shape for n in SMALL}
    grads.update(_unpack_small(small_sum, small_shapes))
    loss = small_sum[7, 0]

    n = "w_branch_a"
    two_d = lambda a: a.reshape(w[n].shape[-2], w[n].shape[-1])
    d_, m_, v_ = adamw(two_d(w[n]), two_d(grads[n]), two_d(m[n]), two_d(v[n]), "adamw_" + n)
    delta[n], new_m[n], new_v[n] = [a.reshape(w[n].shape) for a in (d_, m_, v_)]
    zero_row = jnp.zeros((1, D), F32)
    pack = lambda t: _pack_small({n: t[n] for n in SMALL}, zero_row)
    d_, m_, v_ = adamw(pack(w), small_sum, pack(m), pack(v), "adamw_small")
    for src, dst in ((d_, delta), (m_, new_m), (v_, new_v)):
        dst.update(_unpack_small(src, small_shapes))

    return (loss, grad_x[None], *[grads[n] for n in ORDER], *[delta[n] for n in ORDER],
            *[new_m[n] for n in ORDER], *[new_v[n] for n in ORDER])
```

```python
import math

import jax
import jax.numpy as jnp
from jax import lax
from jax.experimental import pallas as pl
from jax.experimental.pallas import tpu as pltpu

F32 = jnp.float32
BF16 = jnp.bfloat16
MESH = pl.DeviceIdType.MESH

V7X_VMEM_BYTES = 64 * 1024 * 1024
VMEM_LIMIT = V7X_VMEM_BYTES - 8 * 1024 * 1024
LANES = 128

N_DEV = 8
EPS = 1e-6
NEG_INF = -1e30

DILATIONS = (1, 4, 16)
HALF_WINDOW = 64
HEAD_DIM_A = 64
HEADS_PER_GROUP_A = 8
GROUP_WIDTH_A = 512
A_QKV_WIDTH = 4608
A_GROUP_QKV = A_QKV_WIDTH // 3
A_TQ = 128
A_WIN = A_TQ + 2 * HALF_WINDOW
A_UNROLL = 8
A_SCALE = HEAD_DIM_A ** -0.5
WGRAD_TK = 2048
HEAD_DIM_B = 128
N_HEADS_B = 8
N_KV_B = 2
GQA_GROUP_B = 4
GRID_W = 64
ROPE_THETA = 10000.0
B_TQ_FWD = 256
B_TQ_BWD = 512
B_HEADS_PER_STEP = 4
LOG2E = 1.4426950408889634
B_Q_PRESCALE = HEAD_DIM_B ** -0.5 * LOG2E
N_BUCKETS = 32
MAX_DISTANCE = 1024
PB_GATE_A = 1536

ADAM_LR = 0.001
ADAM_B1 = 0.9
ADAM_B2 = 0.999
ADAM_EPS = 1e-08
ADAM_WD = 0.01
ADAM_STEP = 10

FFN_SHARD = 352
MIX_WIN, MIX_WB, MIX_WOUT, MIX_WA = 0, 1024, 1152, 1280
MIX_ROWS = 1344
REST_WB, REST_WOUT, REST_WA, REST_ROWS = 0, 128, 256, 320


def _dot(a, b, ca=1, cb=0):
    return lax.dot_general(a, b, (((ca,), (cb,)), ((), ())), preferred_element_type=F32)


def _call(name, body, grid, ins, outs, scratch=(), sem=None, aliases=None):
    ins = [tuple(i) + (None,) * (4 - len(i)) for i in ins]
    res = pl.pallas_call(
        body,
        out_shape=[jax.ShapeDtypeStruct(s, d) for (s, d, _, _) in outs],
        grid=grid,
        in_specs=[pl.BlockSpec(bs, im, pipeline_mode=pm) for (_, bs, im, pm) in ins],
        out_specs=[pl.BlockSpec(bs, im) for (_, _, bs, im) in outs],
        scratch_shapes=list(scratch),
        name=name,
        input_output_aliases=aliases or {},
        compiler_params=pltpu.CompilerParams(dimension_semantics=sem, vmem_limit_bytes=VMEM_LIMIT),
    )(*[i[0] for i in ins])
    return res


def _sigmoid(x):
    return 0.5 * jnp.tanh(0.5 * x) + 0.5


def _position():
    return lax.axis_index("x"), lax.axis_index("y"), lax.axis_index("c")


def _hbm_specs(n):
    return [pl.BlockSpec(memory_space=pl.ANY) for _ in range(n)]


PAIR_BUFFERS = 4


def reduce_scatter_pair(grads, name):
    n = len(grads)
    C = grads[0].shape[2]
    half = [g.shape[1] // 2 for g in grads]
    chunks = [(i, q, hf) for i in range(n) for q in range(4) for hf in range(2)]
    nb = PAIR_BUFFERS

    def body(*refs):
        ins, theirs = refs[:n], refs[n:2 * n]
        buf, load_sems, send_sems, recv_sems = refs[2 * n:]
        x, y, c = _position()
        sibling = (x, y, 1 - c)

        def load(k):
            i, q, hf = chunks[k]
            r = half[i]
            return pltpu.make_async_copy(ins[i].at[2 * q + (1 - c), pl.ds(hf * r, r), :],
                                         buf.at[k % nb, pl.ds(0, r), :], load_sems.at[k % nb])

        def send(k):
            i, q, hf = chunks[k]
            r = half[i]
            return pltpu.make_async_remote_copy(
                src_ref=buf.at[k % nb, pl.ds(0, r), :], dst_ref=theirs[i].at[q, pl.ds(hf * r, r), :],
                send_sem=send_sems.at[k % nb], recv_sem=recv_sems.at[i],
                device_id=sibling, device_id_type=MESH)

        for k in range(len(chunks) + 1):
            if k < len(chunks):
                if k >= nb:
                    send(k - nb).wait_send()
                load(k).start()
            if k >= 1:
                load(k - 1).wait()
                send(k - 1).start()
        for k in range(max(0, len(chunks) - nb), len(chunks)):
            send(k).wait_send()
        for i in range(n):
            pltpu.make_async_remote_copy(
                src_ref=theirs[i], dst_ref=theirs[i], send_sem=send_sems.at[0], recv_sem=recv_sems.at[i],
                device_id=sibling, device_id_type=MESH).wait_recv()

    return pl.pallas_call(
        body,
        out_shape=[jax.ShapeDtypeStruct((4,) + g.shape[1:], g.dtype) for g in grads],
        in_specs=_hbm_specs(n),
        out_specs=_hbm_specs(n),
        scratch_shapes=[pltpu.VMEM((nb, max(half), C), grads[0].dtype), pltpu.SemaphoreType.DMA((nb,)),
                        pltpu.SemaphoreType.DMA((nb,)), pltpu.SemaphoreType.DMA((n,))],
        name=name,
        compiler_params=pltpu.CompilerParams(vmem_limit_bytes=VMEM_LIMIT),
    )(*grads)


_HBM_SPEC = pl.BlockSpec(memory_space=pltpu.HBM)
_SEM_SPEC = pl.BlockSpec(memory_space=pltpu.SEMAPHORE)
_TOKEN_SPEC = pl.BlockSpec(memory_space=pltpu.VMEM)
_DATAFLOW = pltpu.SideEffectType.DATAFLOW_SIDE_EFFECTING


def _split_start_many(name, exchanges):
    n = len(exchanges)

    def full_body(*refs):
        srcs, lands = refs[:n], refs[n:2 * n]
        sems = refs[2 * n:4 * n]
        token = refs[-1]
        for i, (body, _, _) in enumerate(exchanges):
            body(srcs[i], lands[i], sems[2 * i], sems[2 * i + 1])
        token[...] = jnp.zeros_like(token)

    srcs = [pltpu.with_memory_space_constraint(src, pltpu.HBM) for _, src, _ in exchanges]
    lands = [pltpu.with_memory_space_constraint(lax.empty(shape, src.dtype), pltpu.HBM)
             for _, src, shape in exchanges]
    res = pl.pallas_call(
        full_body, name=name,
        out_shape=(pltpu.SemaphoreType.DMA(()),) * (2 * n)
        + tuple(pltpu.HBM(a.shape, a.dtype) for a in srcs + lands) + (jax.ShapeDtypeStruct((8, LANES), F32),),
        in_specs=(_HBM_SPEC,) * (2 * n),
        out_specs=(_SEM_SPEC,) * (2 * n) + (_HBM_SPEC,) * (2 * n) + (_TOKEN_SPEC,),
        input_output_aliases={i: 2 * n + i for i in range(2 * n)},
        compiler_params=pltpu.CompilerParams(has_side_effects=_DATAFLOW),
    )(*srcs, *lands)
    return [(res[2 * i], res[2 * i + 1], res[2 * n + i], res[3 * n + i], res[-1]) for i in range(n)]


def _split_start(name, body, src, land_shape):
    return _split_start_many(name, [(body, src, land_shape)])[0]


def _split_wait(name, started, n_blocks, after):
    send_sem, recv_sem, src_thru, land_thru, _ = started
    after = after if isinstance(after, tuple) else (after,)

    def body(src_ref, land_ref, send_sem, recv_sem, *rest):
        x, y, c = _position()
        blocks = land_ref.at[pl.ds(0, n_blocks)]
        copy = pltpu.make_async_remote_copy(src_ref=blocks, dst_ref=blocks, send_sem=send_sem, recv_sem=recv_sem,
                                            device_id=(x, y, c), device_id_type=MESH)
        copy.wait_send()
        copy.wait_recv()

    return pl.pallas_call(
        body, name=name,
        out_shape=(pltpu.HBM(src_thru.shape, src_thru.dtype), pltpu.HBM(land_thru.shape, land_thru.dtype)),
        in_specs=(_HBM_SPEC, _HBM_SPEC, _SEM_SPEC, _SEM_SPEC) + (pl.BlockSpec(memory_space=pl.ANY),) * len(after),
        out_specs=(_HBM_SPEC, _HBM_SPEC),
        input_output_aliases={0: 0, 1: 1},
        compiler_params=pltpu.CompilerParams(has_side_effects=_DATAFLOW),
    )(src_thru, land_thru, send_sem, recv_sem, *after)


def all_gather_start_all(blocks, name):
    def starter(direct):
        def body(b_ref, land_ref, send_sem, recv_sem):
            x, y, c = _position()
            peers = _other_devices(x, y, c) if direct else [(x, y, 1 - c), (1 - x, y, c), (x, 1 - y, c),
                                                            (1 - x, 1 - y, c)]
            for peer in peers:
                pltpu.make_async_remote_copy(src_ref=b_ref, dst_ref=land_ref.at[4 * x + 2 * y + c],
                                             send_sem=send_sem, recv_sem=recv_sem,
                                             device_id=peer, device_id_type=MESH).start()
        return body

    return _split_start_many(name, [(starter(direct), block, (N_DEV,) + block.shape) for block, direct in blocks])


def all_gather_finish(block, land, name):
    R, C = block.shape

    def body(b_ref, land_in, land_ref, stage, load_sems, send_sems, recv_sems, own_sem):
        x, y, c = _position()
        sibling = (x, y, 1 - c)
        chips = [(1 - x, y), (x, 1 - y), (1 - x, 1 - y)]
        own_in = pltpu.make_async_copy(b_ref, stage.at[3], load_sems.at[3])
        own_in.start()
        loads = [pltpu.make_async_copy(land_in.at[4 * px + 2 * py + c], stage.at[j], load_sems.at[j])
                 for j, (px, py) in enumerate(chips)]
        for ld in loads:
            ld.start()
        sends = []
        for j, (px, py) in enumerate(chips):
            loads[j].wait()
            dst = land_ref.at[4 * px + 2 * py + c]
            cp = pltpu.make_async_remote_copy(src_ref=stage.at[j], dst_ref=dst, send_sem=send_sems.at[j],
                                              recv_sem=recv_sems.at[j], device_id=sibling, device_id_type=MESH)
            cp.start()
            sends.append(cp)
        own_in.wait()
        own_out = pltpu.make_async_copy(stage.at[3], land_ref.at[4 * x + 2 * y + c], own_sem)
        own_out.start()
        for j, (px, py) in enumerate(chips):
            dst = land_ref.at[4 * px + 2 * py + (1 - c)]
            pltpu.make_async_remote_copy(src_ref=stage.at[j], dst_ref=dst, send_sem=send_sems.at[j],
                                         recv_sem=recv_sems.at[j], device_id=sibling,
                                         device_id_type=MESH).wait_recv()
        for cp in sends:
            cp.wait_send()
        own_out.wait()

    return pl.pallas_call(
        body,
        out_shape=jax.ShapeDtypeStruct(land.shape, land.dtype),
        in_specs=_hbm_specs(2),
        out_specs=pl.BlockSpec(memory_space=pl.ANY),
        scratch_shapes=[pltpu.VMEM((4, R, C), block.dtype), pltpu.SemaphoreType.DMA((4,)),
                        pltpu.SemaphoreType.DMA((3,)), pltpu.SemaphoreType.DMA((3,)), pltpu.SemaphoreType.DMA],
        input_output_aliases={1: 0},
        name=name,
        compiler_params=pltpu.CompilerParams(vmem_limit_bytes=VMEM_LIMIT),
    )(block, land)


def reduce_scatter_start(parts, name):
    def body(p_ref, land_ref, send_sem, recv_sem):
        x, y, c = _position()
        for px, py in [(1 - x, y), (x, 1 - y), (1 - x, 1 - y)]:
            pltpu.make_async_remote_copy(src_ref=p_ref.at[2 * px + py], dst_ref=land_ref.at[2 * x + y],
                                         send_sem=send_sem, recv_sem=recv_sem,
                                         device_id=(px, py, c), device_id_type=MESH).start()

    return _split_start(name, body, parts, parts.shape)


def _other_devices(x, y, c):
    return [(1 - x if k & 4 else x, 1 - y if k & 2 else y, 1 - c if k & 1 else c) for k in range(1, N_DEV)]


def all_gather_place_own(block, land, name):
    R, C = block.shape

    def body(b_ref, land_in, land_ref, stage, sems):
        x, y, c = _position()
        load = pltpu.make_async_copy(b_ref, stage, sems.at[0])
        load.start()
        load.wait()
        store = pltpu.make_async_copy(stage, land_ref.at[4 * x + 2 * y + c], sems.at[1])
        store.start()
        store.wait()

    return pl.pallas_call(
        body,
        out_shape=jax.ShapeDtypeStruct(land.shape, land.dtype),
        in_specs=_hbm_specs(2),
        out_specs=pl.BlockSpec(memory_space=pl.ANY),
        scratch_shapes=[pltpu.VMEM((R, C), block.dtype), pltpu.SemaphoreType.DMA((2,))],
        input_output_aliases={1: 0},
        name=name,
    )(block, land)


def reduce_scatter_start_direct(grads, name):
    def body(g_ref, land_ref, send_sem, recv_sem):
        x, y, c = _position()
        for px, py, pc in _other_devices(x, y, c):
            pltpu.make_async_remote_copy(src_ref=g_ref.at[4 * px + 2 * py + pc],
                                         dst_ref=land_ref.at[4 * x + 2 * y + c],
                                         send_sem=send_sem, recv_sem=recv_sem,
                                         device_id=(px, py, pc), device_id_type=MESH).start()

    return _split_start(name, body, grads, grads.shape)


def small_all_gather(small, after):
    def body(small_ref, after_ref, smalls, s_send, s_recv, s_local):
        x, y, c = _position()
        me = 4 * x + 2 * y + c
        lc = pltpu.make_async_copy(small_ref, smalls.at[me], s_local)
        lc.start()
        remote = []
        k = 0
        for dx in (0, 1):
            for dy in (0, 1):
                for dc in (0, 1):
                    if dx + dy + dc == 0:
                        continue
                    peer = (1 - x if dx else x, 1 - y if dy else y, 1 - c if dc else c)
                    rc = pltpu.make_async_remote_copy(
                        src_ref=small_ref, dst_ref=smalls.at[me],
                        send_sem=s_send.at[k], recv_sem=s_recv.at[k],
                        device_id=peer, device_id_type=MESH)
                    rc.start()
                    remote.append(rc)
                    k += 1
        for rc in remote:
            rc.wait()
        lc.wait()

    return pl.pallas_call(
        body,
        out_shape=jax.ShapeDtypeStruct((N_DEV,) + small.shape, small.dtype),
        in_specs=_hbm_specs(2),
        out_specs=pl.BlockSpec(memory_space=pl.ANY),
        scratch_shapes=[pltpu.SemaphoreType.DMA((7,)), pltpu.SemaphoreType.DMA((7,)), pltpu.SemaphoreType.DMA],
        name="small_all_gather",
    )(small, after)


def pair_add(grads, theirs, core, name):
    _, R, C = theirs.shape
    tr = R // 2

    def body(c_ref, a_ref, b_ref, o_ref):
        o_ref[...] = (a_ref[...].astype(F32) + b_ref[...].astype(F32)).astype(BF16)

    return pl.pallas_call(
        body,
        out_shape=jax.ShapeDtypeStruct(theirs.shape, BF16),
        grid_spec=pltpu.PrefetchScalarGridSpec(
            num_scalar_prefetch=1, grid=(4, R // tr),
            in_specs=[pl.BlockSpec((None, tr, C), lambda q, i, c: (2 * q + c[0], i, 0)),
                      pl.BlockSpec((None, tr, C), lambda q, i, c: (q, i, 0))],
            out_specs=pl.BlockSpec((None, tr, C), lambda q, i, c: (q, i, 0))),
        name=name,
        compiler_params=pltpu.CompilerParams(dimension_semantics=("parallel", "parallel"),
                                             vmem_limit_bytes=VMEM_LIMIT),
    )(core, grads, theirs)


def sum_slots(recv, off, rows, blk, name):
    nq, _, C = recv.shape
    ob = off // blk

    def body(r_ref, o_ref):
        acc = r_ref[0].astype(F32)
        for q in range(1, nq):
            acc = acc + r_ref[q].astype(F32)
        o_ref[...] = acc

    return _call(name, body, (rows // blk,),
                 [(recv, (nq, blk, C), lambda i: (0, ob + i, 0))],
                 [((rows, C), F32, (blk, C), lambda i: (i, 0))], sem=("parallel",))[0]


def _sum_terms(refs):
    acc = refs[0][...].astype(F32)
    for r in refs[1:]:
        acc = acc + r[...].astype(F32)
    return acc


def sum_landed(own, land, me, off, rows, blk, name):
    n, _, C = land.shape
    ob = off // blk

    def body(c_ref, *refs):
        refs[n][...] = _sum_terms(refs[:n])

    def entry(flip):
        return pl.BlockSpec((None, blk, C), lambda i, c: (c[0] ^ flip, ob + i, 0))

    return pl.pallas_call(
        body,
        out_shape=jax.ShapeDtypeStruct((rows, C), F32),
        grid_spec=pltpu.PrefetchScalarGridSpec(
            num_scalar_prefetch=1, grid=(rows // blk,),
            in_specs=[entry(k) for k in range(n)],
            out_specs=pl.BlockSpec((blk, C), lambda i, c: (i, 0))),
        name=name,
        compiler_params=pltpu.CompilerParams(dimension_semantics=("parallel",), vmem_limit_bytes=VMEM_LIMIT),
    )(me, own, *([land] * (n - 1)))


def _adamw_update(wv, gv, mv, vv):
    nm = ADAM_B1 * mv + (1.0 - ADAM_B1) * gv
    nv = ADAM_B2 * vv + (1.0 - ADAM_B2) * (gv * gv)
    c1 = 1.0 / (1.0 - ADAM_B1 ** ADAM_STEP)
    c2 = 1.0 / (1.0 - ADAM_B2 ** ADAM_STEP)
    return -ADAM_LR * ((nm * c1) / (jnp.sqrt(nv * c2) + ADAM_EPS) + ADAM_WD * wv), nm, nv


def sum_adamw(own, land, me, off, blk, w, m, v, name):
    rows, C = w.shape
    n = land.shape[0]
    ob = off // blk

    def body(c_ref, *refs):
        w_ref, m_ref, v_ref, g_out, d_out, m_out, v_out = refs[n:]
        gv = _sum_terms(refs[:n])
        g_out[...] = gv
        d_out[...], m_out[...], v_out[...] = _adamw_update(w_ref[...], gv, m_ref[...], v_ref[...])

    def entry(flip):
        return pl.BlockSpec((None, blk, C), lambda i, c: (c[0] ^ flip, ob + i, 0))

    plain = pl.BlockSpec((blk, C), lambda i, c: (i, 0))
    return pl.pallas_call(
        body,
        out_shape=[jax.ShapeDtypeStruct((rows, C), F32)] * 4,
        grid_spec=pltpu.PrefetchScalarGridSpec(
            num_scalar_prefetch=1, grid=(rows // blk,),
            in_specs=[entry(k) for k in range(n)] + [plain, plain, plain],
            out_specs=[plain] * 4),
        name=name,
        compiler_params=pltpu.CompilerParams(dimension_semantics=("parallel",), vmem_limit_bytes=VMEM_LIMIT),
    )(me, own, *([land] * (n - 1)), w, m, v)


def adamw(w, g, m, v, name):
    R, C = w.shape
    tr = R
    for cand in (256, 128, 64, 32, 16, 8):
        if R % cand == 0 and R > cand:
            tr = cand
            break

    def body(w_ref, g_ref, m_ref, v_ref, d_ref, nm_ref, nv_ref):
        d_ref[...], nm_ref[...], nv_ref[...] = _adamw_update(w_ref[...], g_ref[...], m_ref[...], v_ref[...])

    spec = ((tr, C), lambda i: (i, 0))
    out = ((R, C), F32) + spec
    return _call(name, body, (R // tr,), [(w,) + spec, (g,) + spec, (m,) + spec, (v,) + spec],
                 [out, out, out], sem=("parallel",))


def _rms_tile(xv, gv):
    r = lax.rsqrt(jnp.mean(xv * xv, axis=-1, keepdims=True) + EPS)
    return (xv * r * gv).astype(BF16)


def rms_fwd(x, g, name):
    S, D = x.shape
    tr = 512

    def body(x_ref, g_ref, o_ref):
        o_ref[...] = _rms_tile(x_ref[...], g_ref[...])

    return _call(name, body, (S // tr,),
                 [(x, (tr, D), lambda i: (i, 0)), (g, (1, D), lambda i: (0, 0))],
                 [((S, D), BF16, (tr, D), lambda i: (i, 0))], sem=("parallel",))[0]


def _rms_bwd_tile(dn, xv, gv):
    r = lax.rsqrt(jnp.mean(xv * xv, axis=-1, keepdims=True) + EPS)
    xh = xv * r
    dxh = dn * gv
    dx = r * (dxh - xh * jnp.mean(dxh * xh, axis=-1, keepdims=True))
    return dx, dn * xh


def _final_loss_tile(xv, tv, gv):
    D = xv.shape[1]
    r = lax.rsqrt(jnp.mean(xv * xv, axis=-1, keepdims=True) + EPS)
    xh = xv * r
    e = xh * gv - tv
    part = 0.5 * jnp.sum(jnp.sum(e * e, axis=-1, keepdims=True) * (1.0 / D), axis=0, keepdims=True)
    dy = e * (1.0 / D)
    dxh = dy * gv
    dx = r * (dxh - xh * jnp.mean(dxh * xh, axis=-1, keepdims=True))
    return part, dx, jnp.sum(dy * xh, axis=0, keepdims=True)


FFN_TF = 4 * FFN_SHARD


def _ffn_pick(G, which):
    if isinstance(G, tuple):
        return (G[0], which) if which < 2 else (G[1], 0)
    return G, which


def _ffn_w_spec(G, which, imap):
    arr, blk = _ffn_pick(G, which)
    return (arr, (4, FFN_SHARD, arr.shape[2]), lambda *idx: (imap(*idx), blk, 0))


def _ffn_whole_w_spec(G, which):
    arr, blk = _ffn_pick(G, which)
    return (arr, (N_DEV, FFN_SHARD, arr.shape[2]), lambda *idx: (0, blk, 0), pl.Buffered(1))


def _ffn_hidden(a, b):
    av, bv = a.astype(F32), b.astype(F32)
    return (av * _sigmoid(av) * bv).astype(BF16)


def ffn_up(n, G, name):
    S, D = n.shape
    F = N_DEV * FFN_SHARD
    tm = 256

    def body(n_ref, w1_ref, w3_ref, abh_ref):
        nv = n_ref[...]
        a = _dot(nv, w1_ref[...].reshape(F, D), 1, 1).astype(BF16)
        b = _dot(nv, w3_ref[...].reshape(F, D), 1, 1).astype(BF16)
        abh_ref[0] = a
        abh_ref[1] = b
        abh_ref[2] = _ffn_hidden(a, b)

    return _call(name, body, (S // tm,),
                 [(n, (tm, D), lambda i: (i, 0)),
                  _ffn_whole_w_spec(G, 0), _ffn_whole_w_spec(G, 1)],
                 [((3, S, F), BF16, (3, tm, F), lambda i: (0, i, 0))],
                 sem=("parallel",))[0]


def ffn_down(abh, G, x, g_next, name):
    _, S, F = abh.shape
    D = x.shape[1]
    tm = 512

    def body(h_ref, w2_ref, x_ref, g_ref, o_ref, n_ref):
        xo = x_ref[...] + 0.5 * _dot(h_ref[...], w2_ref[...].reshape(F, D))
        o_ref[...] = xo
        n_ref[...] = _rms_tile(xo, g_ref[...])

    tile = ((tm, D), lambda i: (i, 0))
    return _call(name, body, (S // tm,),
                 [(abh, (None, tm, F), lambda i: (2, i, 0)), _ffn_whole_w_spec(G, 2),
                  (x,) + tile, (g_next, (1, D), lambda i: (0, 0))],
                 [((S, D), F32) + tile, ((S, D), BF16) + tile], sem=("parallel",))


def ffn_last(x, g, G, tgt, g_final, name):
    S, D = x.shape
    F = N_DEV * FFN_SHARD
    tm = 256

    def body(x_ref, g_ref, w1_ref, w3_ref, w2_ref, t_ref, gf_ref,
             n_ref, abh_ref, dxo_ref, dab_ref, dx_ref, dxb_ref, dg_ref, l_ref, dgf_ref):
        i = pl.program_id(0)
        xv, gv = x_ref[...], g_ref[...]
        chunks = [(slice(4 * f, 4 * f + 4), slice(f * FFN_TF, (f + 1) * FFN_TF)) for f in range(F // FFN_TF)]
        weight = lambda w_ref, slots: w_ref[slots].reshape(FFN_TF, D)
        nv = _rms_tile(xv, gv)
        n_ref[...] = nv
        y = None
        for slots, cols in chunks:
            a = _dot(nv, weight(w1_ref, slots), 1, 1).astype(BF16)
            b = _dot(nv, weight(w3_ref, slots), 1, 1).astype(BF16)
            h = _ffn_hidden(a, b)
            abh_ref[0, :, cols] = a
            abh_ref[1, :, cols] = b
            abh_ref[2, :, cols] = h
            t = _dot(h, weight(w2_ref, slots))
            y = t if y is None else y + t
        part, dxo, dgfp = _final_loss_tile(xv + 0.5 * y, t_ref[...], gf_ref[...])
        dxo_b = dxo.astype(BF16)
        dxo_ref[...] = dxo_b
        dn = None
        for slots, cols in chunks:
            dh = 0.5 * _dot(dxo_b, weight(w2_ref, slots), 1, 1)
            da, db = _ffn_hidden_grads(dh, abh_ref[0, :, cols].astype(F32), abh_ref[1, :, cols].astype(F32))
            da, db = da.astype(BF16), db.astype(BF16)
            dab_ref[0, :, cols] = da
            dab_ref[1, :, cols] = db
            t = _dot(da, weight(w1_ref, slots)) + _dot(db, weight(w3_ref, slots))
            dn = t if dn is None else dn + t
        dx, dgt = _rms_bwd_tile(dn, xv, gv)
        dx = dxo + dx
        dx_ref[...] = dx
        dxb_ref[...] = dx.astype(BF16)
        dgp = jnp.sum(dgt, axis=0, keepdims=True)

        @pl.when(i == 0)
        def _():
            dg_ref[...] = dgp
            l_ref[...] = jnp.broadcast_to(part, l_ref.shape)
            dgf_ref[...] = dgfp

        @pl.when(i > 0)
        def _():
            dg_ref[...] += dgp
            l_ref[...] += jnp.broadcast_to(part, l_ref.shape)
            dgf_ref[...] += dgfp

    tile = ((tm, D), lambda i: (i, 0))
    gain = ((1, D), lambda i: (0, 0))
    return _call(name, body, (S // tm,),
                 [(x,) + tile, (g,) + gain,
                  _ffn_whole_w_spec(G, 0), _ffn_whole_w_spec(G, 1), _ffn_whole_w_spec(G, 2),
                  (tgt,) + tile, (g_final,) + gain],
                 [((S, D), BF16) + tile, ((3, S, F), BF16, (3, tm, F), lambda i: (0, i, 0)),
                  ((S, D), BF16) + tile, ((2, S, F), BF16, (2, tm, F), lambda i: (0, i, 0)),
                  ((S, D), F32) + tile, ((S, D), BF16) + tile, ((1, D), F32) + gain,
                  ((1, LANES), F32, (1, LANES), lambda i: (0, 0)), ((1, D), F32) + gain],
                 sem=("arbitrary",))


def _ffn_hidden_grads(dh, av, bv):
    sig = _sigmoid(av)
    return dh * bv * (sig * (1.0 + av * (1.0 - sig))), dh * (av * sig)


def ffn_bwd_hidden(dxo, abh, G, name):
    _, S, F = abh.shape
    D = dxo.shape[1]
    tm = 256

    def body(d_ref, w2_ref, ab_ref, o_ref):
        dh = 0.5 * _dot(d_ref[...].astype(BF16), w2_ref[...].reshape(F, D), 1, 1)
        da, db = _ffn_hidden_grads(dh, ab_ref[0].astype(F32), ab_ref[1].astype(F32))
        o_ref[0] = da.astype(BF16)
        o_ref[1] = db.astype(BF16)

    return _call(name + "_down_bwd", body, (S // tm,),
                 [(dxo, (tm, D), lambda i: (i, 0)), _ffn_whole_w_spec(G, 2),
                  (abh, (2, tm, F), lambda i: (0, i, 0))],
                 [((2, S, F), BF16, (2, tm, F), lambda i: (0, i, 0))],
                 sem=("parallel",))[0]


def ffn_bwd_weights(dxo, abh, dab, n, name):
    _, S, F = abh.shape
    D = dxo.shape[1]
    nf = F // FFN_TF
    tk = WGRAD_TK
    nk = S // tk
    gshape = (N_DEV, 3 * FFN_SHARD, D)

    def dw2_body(h_ref, d_ref, o_ref, acc_ref):
        k = pl.program_id(1)
        p = _dot(h_ref[...], d_ref[...].astype(BF16), 0, 0)

        @pl.when(k == 0)
        def _():
            acc_ref[...] = p

        @pl.when(k > 0)
        def _():
            acc_ref[...] += p

        @pl.when(k == nk - 1)
        def _():
            o_ref[...] = (0.5 * acc_ref[...]).astype(BF16).reshape(4, FFN_SHARD, D)

    gw = _call(name + "_dw2", dw2_body, (nf, nk),
               [(abh, (None, tk, FFN_TF), lambda j, k: (2, k, j)), (dxo, (tk, D), lambda j, k: (k, 0))],
               [(gshape, BF16, (4, FFN_SHARD, D), lambda j, k: (j, 2, 0))],
               scratch=[pltpu.VMEM((FFN_TF, D), F32)], sem=("parallel", "arbitrary"))[0]

    def dw13_body(gw_ref, dab_ref, n_ref, o_ref):
        o_ref[...] = _dot(dab_ref[...], n_ref[...], 0, 0).astype(BF16).reshape(4, FFN_SHARD, D)

    gw = pl.pallas_call(
        dw13_body,
        out_shape=jax.ShapeDtypeStruct(gshape, BF16),
        grid=(2, nf),
        in_specs=[pl.BlockSpec(memory_space=pl.ANY),
                  pl.BlockSpec((None, S, FFN_TF), lambda w, j: (w, 0, j)),
                  pl.BlockSpec((S, D), lambda w, j: (0, 0))],
        out_specs=pl.BlockSpec((4, FFN_SHARD, D), lambda w, j: (j, w, 0)),
        input_output_aliases={0: 0},
        name=name + "_dw13",
        compiler_params=pltpu.CompilerParams(dimension_semantics=("parallel", "parallel"),
                                             vmem_limit_bytes=VMEM_LIMIT),
    )(gw, dab, n)
    return gw


def ffn_bwd_input(dab, G, x_in, g, dxo, name):
    _, S, F = dab.shape
    D = x_in.shape[1]
    tm = 256

    def dn_body(dab_ref, w1_ref, w3_ref, x_ref, d_ref, g_ref, dx_ref, dg_ref):
        i = pl.program_id(0)
        dn = _dot(dab_ref[0], w1_ref[...].reshape(F, D)) + _dot(dab_ref[1], w3_ref[...].reshape(F, D))
        dx, dgt = _rms_bwd_tile(dn, x_ref[...], g_ref[...])
        dx_ref[...] = d_ref[...] + dx
        dgp = jnp.sum(dgt, axis=0, keepdims=True)

        @pl.when(i == 0)
        def _():
            dg_ref[...] = dgp

        @pl.when(i > 0)
        def _():
            dg_ref[...] += dgp

    tile = ((tm, D), lambda i: (i, 0))
    return _call(name + "_dn", dn_body, (S // tm,),
                 [(dab, (2, tm, F), lambda i: (0, i, 0)),
                  _ffn_whole_w_spec(G, 0), _ffn_whole_w_spec(G, 1),
                  (x_in,) + tile, (dxo,) + tile, (g, (1, D), lambda i: (0, 0))],
                 [((S, D), F32) + tile, ((1, D), F32, (1, D), lambda i: (0, 0))],
                 sem=("arbitrary",))


PROJ_TN = 512
DH_SHARDS_PER_STEP = 4


def in_proj(h, Gm, name):
    S, D = h.shape
    n_tiles = N_DEV * Gm.shape[2] // PROJ_TN

    def body(h_ref, w_ref, o_ref):
        o_ref[...] = _dot(h_ref[...], w_ref[...]).astype(BF16)

    return _call(name, body, (n_tiles,),
                 [(h, (S, D), lambda j: (0, 0)),
                  (Gm, (None, D, PROJ_TN), lambda j: (j // 2, 0, j % 2))],
                 [((S, n_tiles * PROJ_TN), BF16, (S, PROJ_TN), lambda j: (0, j))],
                 sem=("parallel",))[0]


def _dproj_pieces(dqkv, dq_b, dkv_b, dgate):
    pieces = [(dqkv[g], [(3 * which + g, (which, 0)) for which in range(3)]) for g in range(3)]
    pieces.append((dq_b, [(9, (None, 0)), (10, (None, 1))]))
    pieces.append((dkv_b, [(11, (None, 0))]))
    pieces.append((dgate, [(12 + 2 * a + b, (a, b)) for a in range(2) for b in range(2)]))
    return pieces


def in_proj_bwd_dw(pieces, h, gm_grads, name):
    S, D = h.shape
    steps = [(n, t, ix) for n, (_, tiles) in enumerate(pieces) for t, ix in tiles]
    n_steps = len(steps)

    def pick(table, j):
        out = table[-1]
        for k in range(len(table) - 2, -1, -1):
            out = jnp.where(j == k, table[k], out)
        return out

    def piece_spec(n, arr):
        own = [k for k, (m, _, _) in enumerate(steps) if m == n]
        at = [steps[min(max(k, own[0]), own[-1])][2] for k in range(n_steps)]
        lead, colb = [ix[0] for ix in at], [ix[1] for ix in at]
        if arr.ndim == 3:
            return (own[0], own[-1]), pl.BlockSpec((None, S, PROJ_TN), lambda j: (pick(lead, j), 0, pick(colb, j)))
        return (own[0], own[-1]), pl.BlockSpec((S, PROJ_TN), lambda j: (0, pick(colb, j)))

    spans, d_specs = zip(*[piece_spec(n, arr) for n, (arr, _) in enumerate(pieces)])
    w_tile = [t for _, t, _ in steps]

    def dw_body(gm_ref, h_ref, *refs):
        o_ref = refs[-1]
        j = pl.program_id(0)
        for d_ref, (first, last) in zip(refs[:-1], spans):
            @pl.when((j >= first) & (j <= last))
            def _(d_ref=d_ref):
                o_ref[...] = _dot(h_ref[...], d_ref[...], 0, 0).astype(BF16)

    return pl.pallas_call(
        dw_body,
        out_shape=jax.ShapeDtypeStruct(gm_grads.shape, BF16),
        grid=(n_steps,),
        in_specs=[pl.BlockSpec(memory_space=pl.ANY),
                  pl.BlockSpec((S, D), lambda j: (0, 0), pipeline_mode=pl.Buffered(1))] + list(d_specs),
        out_specs=pl.BlockSpec((None, D, PROJ_TN), lambda j: (pick(w_tile, j) // 2, 0, pick(w_tile, j) % 2)),
        input_output_aliases={0: 0},
        name=name + "_dw",
        compiler_params=pltpu.CompilerParams(dimension_semantics=("arbitrary",), vmem_limit_bytes=VMEM_LIMIT),
    )(gm_grads, h, *[arr for arr, _ in pieces])


def in_proj_bwd_dh(pieces, Gm, x_in, g, dres, name):
    S, D = x_in.shape
    tm = 256
    C = Gm.shape[2]
    n_sh = N_DEV
    n_p = len(pieces)

    def dh_body(*refs):
        d_refs = refs[:n_p]
        w_ref, x_ref, r_ref, g_ref, dx_ref, dxb_ref, dg_ref = refs[n_p:]
        i = pl.program_id(0)
        p = None
        for d_ref, (arr, tiles) in zip(d_refs, pieces):
            for t, (lead, colb) in tiles:
                cols = slice(colb * PROJ_TN, (colb + 1) * PROJ_TN)
                d = d_ref[:, cols] if lead is None else d_ref[lead, :, cols]
                wcol = (t % 2) * PROJ_TN
                term = _dot(d, w_ref[t // 2, :, wcol:wcol + PROJ_TN], 1, 1)
                p = term if p is None else p + term
        dx, dgt = _rms_bwd_tile(p, x_ref[...], g_ref[...])
        dx = r_ref[...] + dx
        dx_ref[...] = dx
        dxb_ref[...] = dx.astype(BF16)
        dgp = jnp.sum(dgt, axis=0, keepdims=True)

        @pl.when(i == 0)
        def _():
            dg_ref[...] = dgp

        @pl.when(i > 0)
        def _():
            dg_ref[...] += dgp

    tile = ((tm, D), lambda i: (i, 0))

    def rows_of(arr):
        if arr.ndim == 3:
            return (arr, (arr.shape[0], tm, arr.shape[2]), lambda i: (0, i, 0))
        return (arr, (tm, arr.shape[1]), lambda i: (i, 0))

    return _call(name + "_dh", dh_body, (S // tm,),
                 [rows_of(arr) for arr, _ in pieces]
                 + [(Gm, (n_sh, D, C), lambda i: (0, 0, 0), pl.Buffered(1)),
                    (x_in,) + tile, (dres,) + tile, (g, (1, D), lambda i: (0, 0))],
                 [((S, D), F32) + tile, ((S, D), BF16) + tile, ((1, D), F32, (1, D), lambda i: (0, 0))],
                 sem=("arbitrary",))


def _t5_bucket(rel):
    n = N_BUCKETS // 2
    max_exact = n // 2
    ret = jnp.where(rel > 0, n, 0)
    a = jnp.abs(rel)
    af = jnp.maximum(a, 1).astype(F32)
    large = max_exact + (jnp.log(af / max_exact) / math.log(MAX_DISTANCE / max_exact)
                         * (n - max_exact)).astype(jnp.int32)
    large = jnp.minimum(large, n - 1)
    return ret + jnp.where(a < max_exact, a, large)


def _bucket_tables():
    qi = jnp.arange(A_TQ, dtype=jnp.int32)[:, None]
    kj = jnp.arange(A_WIN, dtype=jnp.int32)[None, :]
    rel = kj - HALF_WINDOW - qi
    return jnp.stack([_t5_bucket(rel * d) for d in DILATIONS], axis=0)


def bias_build(rel_bias, buckets):
    def body(tab_ref, bk_ref, o_ref):
        col = pl.program_id(0) * HEADS_PER_GROUP_A + pl.program_id(1)
        bk = bk_ref[...]
        acc = jnp.zeros(bk.shape, F32)
        for b in range(N_BUCKETS):
            acc = jnp.where(bk == b, tab_ref[b, col], acc)
        qi = lax.broadcasted_iota(jnp.int32, bk.shape, 0)
        kj = lax.broadcasted_iota(jnp.int32, bk.shape, 1)
        band = jnp.where(jnp.abs(kj - HALF_WINDOW - qi) <= HALF_WINDOW, acc, NEG_INF)
        o_ref[0] = jnp.where(kj >= HALF_WINDOW, band, NEG_INF)
        o_ref[1] = band
        o_ref[2] = jnp.where(kj < A_TQ + HALF_WINDOW, band, NEG_INF)

    out = pl.pallas_call(
        body,
        out_shape=jax.ShapeDtypeStruct((3, HEADS_PER_GROUP_A // 2, 3, 2, A_TQ, A_WIN), F32),
        grid=(3, HEADS_PER_GROUP_A),
        in_specs=[pl.BlockSpec(memory_space=pltpu.SMEM),
                  pl.BlockSpec((None, A_TQ, A_WIN), lambda g, h: (g, 0, 0))],
        out_specs=pl.BlockSpec((None, None, 3, None, A_TQ, A_WIN), lambda g, h: (g, h // 2, 0, h % 2, 0, 0)),
        name="a_bias_build",
        compiler_params=pltpu.CompilerParams(dimension_semantics=("parallel", "parallel")),
    )(rel_bias, buckets)
    return out.reshape(3, HEADS_PER_GROUP_A // 2, 3, 2 * A_TQ, A_WIN)


def bias_bwd(dbias, buckets):
    def body(d_ref, bk_ref, o_ref):
        bk = bk_ref[...]
        dv = d_ref[...]
        for b in range(N_BUCKETS):
            part = jnp.sum(jnp.where(bk == b, dv, 0.0), axis=1, keepdims=True)
            o_ref[b:b + 1, :] = jnp.broadcast_to(jnp.sum(part, axis=0, keepdims=True), (1, LANES))

    out = pl.pallas_call(
        body,
        out_shape=jax.ShapeDtypeStruct((3, HEADS_PER_GROUP_A, N_BUCKETS, LANES), F32),
        grid=(3, HEADS_PER_GROUP_A),
        in_specs=[pl.BlockSpec((None, None, A_TQ, A_WIN), lambda g, h: (g, h, 0, 0)),
                  pl.BlockSpec((None, A_TQ, A_WIN), lambda g, h: (g, 0, 0))],
        out_specs=pl.BlockSpec((None, None, N_BUCKETS, LANES), lambda g, h: (g, h, 0, 0)),
        name="a_bias_bwd",
        compiler_params=pltpu.CompilerParams(dimension_semantics=("parallel", "parallel")),
    )(dbias, buckets)
    return out[:, :, :, 0].transpose(2, 0, 1).reshape(N_BUCKETS, 3 * HEADS_PER_GROUP_A)


def _a_fill_padded(pad_ref, src_ref, n, pad):
    zeros = jnp.zeros((pad, LANES), pad_ref.dtype)
    pad_ref[0:pad, :] = zeros
    pad_ref[pad + n:2 * pad + n, :] = zeros
    pad_ref[pad:pad + n, :] = src_ref[...].astype(pad_ref.dtype)


def _a_stack_heads(x, lane):
    zero = jnp.zeros_like(x)
    return jnp.concatenate([jnp.where(lane < HEAD_DIM_A, x, zero), jnp.where(lane >= HEAD_DIM_A, x, zero)], axis=0)


def _a_bias_variant(qb, nqb):
    return jnp.where(qb == 0, 0, jnp.where(qb == nqb - 1, 2, 1))


def _a_slab_specs(proj, g):
    S = proj.shape[0]
    per = GROUP_WIDTH_A // LANES
    return [(proj, (S, LANES), lambda hp, w=w: (0, per * (3 * w + g) + hp)) for w in range(3)]


def a_fwd(proj, bias, g, name, others=None):
    S = proj.shape[0]
    d = DILATIONS[g]
    L = S // d
    nqb = L // A_TQ
    pad = HALF_WINDOW * d
    n_others = 0 if others is None else 4
    tr = 256

    def body(q_ref, k_ref, v_ref, b_ref, *refs):
        out1, out2, qf, kpad, vpad = refs[n_others:n_others + 5]
        o_ref, l_ref = refs[n_others + 5:] if others else (out1, out2)
        qf[...] = q_ref[...].astype(F32) * A_SCALE
        _a_fill_padded(kpad, k_ref, S, pad)
        _a_fill_padded(vpad, v_ref, S, pad)
        lane = lax.broadcasted_iota(jnp.int32, (A_TQ, LANES), 1)

        def block(t, carry):
            qb, r = t // d, t % d
            start = qb * (A_TQ * d) + r
            kw = kpad[pl.ds(start, A_WIN, stride=d), :].astype(BF16)
            vw = vpad[pl.ds(start, A_WIN, stride=d), :].astype(BF16)
            q = qf[pl.ds(start, A_TQ, stride=d), :].astype(BF16)
            q2 = _a_stack_heads(q, lane)
            s = _dot(q2, kw, 1, 1) + b_ref[_a_bias_variant(qb, nqb)]
            m = jnp.max(s, axis=-1, keepdims=True)
            e = jnp.exp(s - m)
            l = jnp.sum(e, axis=-1, keepdims=True)
            o2 = _dot(e.astype(BF16), vw) / l
            lse2 = m + jnp.log(l)
            o_ref[pl.ds(start, A_TQ, stride=d), :] = jnp.where(lane < HEAD_DIM_A, o2[0:A_TQ], o2[A_TQ:])
            l_ref[pl.ds(start, A_TQ, stride=d), :] = jnp.where(lane < HEAD_DIM_A, lse2[0:A_TQ], lse2[A_TQ:])
            return carry

        lax.fori_loop(0, nqb * d, block, 0, unroll=A_UNROLL)

        if others:
            o0, o1, l0, l1 = refs[:n_others]

            def combine(c, carry):
                rows = pl.ds(pl.multiple_of(c * tr, tr), tr)
                la, lb, lc = l0[rows, :], l1[rows, :], l_ref[rows, :]
                m = jnp.maximum(jnp.maximum(la, lb), lc)
                ea, eb, ec = jnp.exp(la - m), jnp.exp(lb - m), jnp.exp(lc - m)
                z = ea + eb + ec
                out1[rows, :] = ((ea * o0[rows, :] + eb * o1[rows, :] + ec * o_ref[rows, :]) / z).astype(BF16)
                out2[rows, :] = m + jnp.log(z)
                return carry

            lax.fori_loop(0, S // tr, combine, 0)

    slab = ((S, LANES), lambda hp: (0, hp))
    wide = (S, GROUP_WIDTH_A)
    other_ins = [(a,) + slab for a in (*others[0], *others[1])] if others else []
    return _call(name, body, (4,),
                 _a_slab_specs(proj, g)
                 + [(bias, (None, None, 3, 2 * A_TQ, A_WIN), lambda hp: (g, hp, 0, 0, 0))] + other_ins,
                 [(wide, BF16 if others else F32) + slab, (wide, F32) + slab],
                 scratch=[pltpu.VMEM((S, LANES), F32)] + [pltpu.VMEM((S + 2 * pad, LANES), F32)] * 2
                 + ([pltpu.VMEM((S, LANES), F32)] * 2 if others else []),
                 sem=("parallel",))


def a_bwd(proj, bias, do_a, o_a, lse_tot, g, name):
    S = proj.shape[0]
    d = DILATIONS[g]
    L = S // d
    nqb = L // A_TQ
    pad = HALF_WINDOW * d

    def body(q_ref, k_ref, v_ref, b_ref, do_ref, o_ref, l_ref, dqkv_ref, db_ref,
             qf, of, dqf, kpad, vpad, dkacc, dvacc):
        qf[...] = q_ref[...].astype(F32) * A_SCALE
        of[...] = o_ref[...].astype(F32)
        _a_fill_padded(kpad, k_ref, S, pad)
        _a_fill_padded(vpad, v_ref, S, pad)
        dkacc[...] = jnp.zeros(dkacc.shape, F32)
        dvacc[...] = jnp.zeros(dvacc.shape, F32)
        db_ref[...] = jnp.zeros(db_ref.shape, F32)
        lane = lax.broadcasted_iota(jnp.int32, (A_TQ, LANES), 1)

        def block(t, carry):
            qb, r = t // d, t % d
            start = qb * (A_TQ * d) + r
            rows = pl.ds(start, A_TQ, stride=d)
            win = pl.ds(start, A_WIN, stride=d)
            kw = kpad[win, :].astype(BF16)
            vw = vpad[win, :].astype(BF16)
            q = qf[rows, :].astype(BF16)
            do = do_ref[rows, :]
            ov = of[rows, :]
            lt = l_ref[rows, :]
            q2 = _a_stack_heads(q, lane)
            do2 = _a_stack_heads(do, lane)
            lt2 = jnp.concatenate([lt[:, 0:1], lt[:, HEAD_DIM_A:HEAD_DIM_A + 1]], axis=0)
            s = _dot(q2, kw, 1, 1) + b_ref[_a_bias_variant(qb, nqb)]
            p = jnp.exp(s - lt2)
            t = jnp.sum(do2 * jnp.concatenate([ov, ov], axis=0), axis=-1, keepdims=True)
            dob2 = do2.astype(BF16)
            ds = p * (_dot(dob2, vw, 1, 1) - t)
            db_ref[...] += ds
            dsb = ds.astype(BF16)
            dq2 = _dot(dsb, kw)
            dqf[rows, :] = jnp.where(lane < HEAD_DIM_A, dq2[0:A_TQ], dq2[A_TQ:]) * A_SCALE
            dkacc[win, :] += _dot(dsb, q2, 0, 0)
            dvacc[win, :] += _dot(p.astype(BF16), dob2, 0, 0)
            return carry

        lax.fori_loop(0, nqb * d, block, 0, unroll=A_UNROLL)
        dqkv_ref[0] = dqf[...].astype(BF16)
        dqkv_ref[1] = dkacc[pad:pad + S, :].astype(BF16)
        dqkv_ref[2] = dvacc[pad:pad + S, :].astype(BF16)

    slab = ((S, LANES), lambda hp: (0, hp))
    padded = pltpu.VMEM((S + 2 * pad, LANES), F32)
    return _call(
        name, body, (4,),
        _a_slab_specs(proj, g)
        + [(bias, (None, None, 3, 2 * A_TQ, A_WIN), lambda hp: (g, hp, 0, 0, 0)),
           (do_a,) + slab, (o_a,) + slab, (lse_tot,) + slab],
        [((3, S, GROUP_WIDTH_A), BF16, (3, S, LANES), lambda hp: (0, 0, hp)),
         ((4, 2 * A_TQ, A_WIN), F32, (None, 2 * A_TQ, A_WIN), lambda hp: (hp, 0, 0))],
        scratch=[pltpu.VMEM((S, LANES), F32)] * 3 + [padded] * 4,
        sem=("parallel",))


def _rope_tables(S):
    rows = S // GRID_W
    row = jnp.repeat(jnp.arange(rows, dtype=F32), GRID_W)
    col = jnp.tile(jnp.arange(GRID_W, dtype=F32), rows)
    n_freq = HEAD_DIM_B // 4
    freq = ROPE_THETA ** (-jnp.arange(n_freq, dtype=F32) / n_freq)
    ang = jnp.concatenate([row[:, None] * freq, col[:, None] * freq], axis=-1)
    cos, sin = jnp.cos(ang), jnp.sin(ang)
    return jnp.repeat(cos, 2, axis=-1), jnp.stack([-sin, sin], axis=-1).reshape(S, HEAD_DIM_B)


def _swap_pairs(y):
    lane = lax.broadcasted_iota(jnp.int32, y.shape, 1)
    return jnp.where(lane % 2 == 0, pltpu.roll(y, LANES - 1, 1), pltpu.roll(y, 1, 1))


def qkv_prep(proj, gains, cos_t, sin_t, name):
    S = proj.shape[0]
    ts = 256
    n_rot = N_HEADS_B + N_KV_B
    nh = n_rot + N_KV_B
    W = nh * LANES

    def body(x_ref, g_ref, c_ref, s_ref, o_ref):
        cv, sv = c_ref[...], s_ref[...]
        for hb in range(nh):
            cols = slice(hb * LANES, (hb + 1) * LANES)
            if hb < n_rot:
                xv = x_ref[:, cols].astype(F32)
                r = lax.rsqrt(jnp.mean(xv * xv, axis=-1, keepdims=True) + EPS)
                yv = xv * r * g_ref[:, cols]
                o_ref[:, cols] = (yv * cv + _swap_pairs(yv) * sv).astype(BF16)
            else:
                o_ref[:, cols] = x_ref[:, cols]

    return _call(name, body, (S // ts,),
                 [(proj, (ts, W), lambda i: (i, A_QKV_WIDTH // W)), (gains, (1, W), lambda i: (0, 0)),
                  (cos_t, (ts, LANES), lambda i: (i, 0)), (sin_t, (ts, LANES), lambda i: (i, 0))],
                 [((S, W), BF16, (ts, W), lambda i: (i, 0))],
                 sem=("parallel",))[0]


def qk_prep_bwd(dr, proj, col0, gain, cos_t, sin_t, name):
    S, W = dr.shape
    H = W // LANES
    ts = 256
    wx = math.gcd(W, col0)
    n_x = W // wx

    def body(d_ref, *refs):
        x_refs = refs[:n_x]
        g_ref, c_ref, s_ref, dx_ref, dg_ref = refs[n_x:]
        i = pl.program_id(0)
        cv, sv, gv = c_ref[...], s_ref[...], g_ref[...]
        dgp = jnp.zeros((1, LANES), F32)
        for hb in range(H):
            cols = slice(hb * LANES, (hb + 1) * LANES)
            xc = (hb * LANES) % wx
            xv = x_refs[(hb * LANES) // wx][:, xc:xc + LANES].astype(F32)
            dout = d_ref[:, cols]
            dy = dout * cv + _swap_pairs(dout * sv)
            dx, dgt = _rms_bwd_tile(dy, xv, gv)
            dx_ref[:, cols] = dx.astype(BF16)
            dgp = dgp + jnp.sum(dgt, axis=0, keepdims=True)

        @pl.when(i == 0)
        def _():
            dg_ref[...] = dgp

        @pl.when(i > 0)
        def _():
            dg_ref[...] += dgp

    return _call(name, body, (S // ts,),
                 [(dr, (ts, W), lambda i: (i, 0))]
                 + [(proj, (ts, wx), lambda i, k=k: (i, col0 // wx + k)) for k in range(n_x)]
                 + [(gain, (1, LANES), lambda i: (0, 0)),
                  (cos_t, (ts, LANES), lambda i: (i, 0)), (sin_t, (ts, LANES), lambda i: (i, 0))],
                 [((S, W), BF16, (ts, W), lambda i: (i, 0)),
                  ((1, LANES), F32, (1, LANES), lambda i: (0, 0))],
                 sem=("arbitrary",))


def _row_sums(x):
    hi = x.astype(BF16)
    lo = (x - hi.astype(F32)).astype(BF16)
    ones = jnp.ones((8, LANES), BF16)
    return (_dot(ones, hi, 1, 1) + _dot(ones, lo, 1, 1))[0:1, :]


def flash_fwd(qkv, name):
    S = qkv.shape[0]
    tq = B_TQ_FWD
    hps = B_HEADS_PER_STEP

    def body(q_ref, k_ref, v_ref, o_ref, l_ref):
        k, v = k_ref[...], v_ref[...]
        for j in range(hps):
            cols = slice(j * LANES, (j + 1) * LANES)
            s = _dot(q_ref[:, cols], k, 1, 1)
            m = jnp.max(s, axis=-1, keepdims=True)
            e = jnp.exp2(s - m)
            l = jnp.sum(e, axis=-1, keepdims=True)
            o_ref[:, cols] = (_dot(e.astype(BF16), v) / l).astype(BF16)
            lse = jnp.broadcast_to(m * (1.0 / LOG2E) + jnp.log(l), (tq, LANES))
            l_ref[j] = _row_sums(lse) * (1.0 / LANES)

    per = GQA_GROUP_B // hps
    heads = lambda g, h, i: (i, g * per + h)
    return _call(name, body, (N_KV_B, per, S // tq),
                 [(qkv, (tq, hps * LANES), heads),
                  (qkv, (S, LANES), lambda g, h, i: (0, N_HEADS_B + g)),
                  (qkv, (S, LANES), lambda g, h, i: (0, N_HEADS_B + N_KV_B + g))],
                 [((S, N_HEADS_B * LANES), BF16, (tq, hps * LANES), heads),
                  ((N_HEADS_B, 1, S), F32, (hps, 1, tq), lambda g, h, i: (g * per + h, 0, i))],
                 sem=("parallel", "parallel", "parallel"))


def flash_bwd(qkv, k_t, do_b, o_b, lse, name):
    S = qkv.shape[0]
    tq = B_TQ_BWD
    nq = S // tq
    scale = HEAD_DIM_B ** -0.5

    def body(q_ref, k_ref, v_ref, kt_ref, do_ref, o_ref, l_ref, dq_ref, dk_ref, dv_ref, dkacc, dvacc):
        h, i = pl.program_id(1), pl.program_id(2)

        @pl.when((h == 0) & (i == 0))
        def _():
            dkacc[...] = jnp.zeros(dkacc.shape, F32)
            dvacc[...] = jnp.zeros(dvacc.shape, F32)

        q = q_ref[...]
        dob = do_ref[...]
        t = _row_sums(dob.astype(F32) * o_ref[...].astype(F32))
        pt = jnp.exp2(_dot(k_ref[...], q, 1, 1) - l_ref[...] * LOG2E)
        dsb = (pt * (_dot(v_ref[...], dob, 1, 1) - t)).astype(BF16)
        dvacc[...] += _dot(pt.astype(BF16), dob)
        dkacc[...] += _dot(dsb, q)
        dq_ref[...] = _dot(kt_ref[...], dsb).T * scale

        @pl.when((h == GQA_GROUP_B - 1) & (i == nq - 1))
        def _():
            dk_ref[...] = dkacc[...] * (scale / B_Q_PRESCALE)
            dv_ref[...] = dvacc[...].astype(BF16)

    head = lambda g, h, i: (i, g * GQA_GROUP_B + h)
    return _call(name, body, (N_KV_B, GQA_GROUP_B, nq),
                 [(qkv, (tq, LANES), head),
                  (qkv, (S, LANES), lambda g, h, i: (0, N_HEADS_B + g)),
                  (qkv, (S, LANES), lambda g, h, i: (0, N_HEADS_B + N_KV_B + g)),
                  (k_t, (LANES, S), lambda g, h, i: (g, 0)),
                  (do_b, (tq, LANES), head), (o_b, (tq, LANES), head),
                  (lse, (None, 1, tq), lambda g, h, i: (g * GQA_GROUP_B + h, 0, i))],
                 [((S, N_HEADS_B * LANES), F32, (tq, LANES), head),
                  ((S, N_KV_B * LANES), F32, (S, LANES), lambda g, h, i: (0, g)),
                  ((S, N_KV_B * LANES), BF16, (S, LANES), lambda g, h, i: (0, g))],
                 scratch=[pltpu.VMEM((S, LANES), F32)] * 2,
                 sem=("parallel", "arbitrary", "arbitrary"))


MERGE_TN = 512


def _mix_rows_spec(Gm, row0, n_slots, slot_map, cols=None, col_map=None):
    C = Gm.shape[2] if cols is None else cols
    cm = (lambda *idx: 0) if col_map is None else col_map
    return (Gm, (n_slots, LANES, C), lambda *idx: (slot_map(*idx), row0 // LANES, cm(*idx)))


def _gate_specs(proj, tm):
    first = (A_QKV_WIDTH + PB_GATE_A) // MERGE_TN
    return [(proj, (tm, MERGE_TN), lambda i, k=k: (i, first + k)) for k in range(4)]


def _whole_rows_spec(Gm, row0):
    return _mix_rows_spec(Gm, row0, N_DEV, lambda *idx: 0)


def merge_fwd(o_a, o_b, w_a, Gm, proj, b_gate, x, name):
    S, D = x.shape
    tm = 256

    def body(oa_ref, ob_ref, wa_ref, wb_ref, wo_ref, g0, g1, g2, g3, bg_ref, x_ref, m_ref, ya_ref, yb_ref, xo_ref):
        ya = _dot(oa_ref[...], wa_ref[...])
        yb = _dot(ob_ref[...], wb_ref[...].reshape(N_DEV * LANES, D))
        ga = _sigmoid(jnp.concatenate([g0[...], g1[...]], axis=1).astype(F32) + bg_ref[:, 0:D])
        gb = _sigmoid(jnp.concatenate([g2[...], g3[...]], axis=1).astype(F32) + bg_ref[:, D:2 * D])
        merged = (ga * ya + gb * yb).astype(BF16)
        m_ref[...] = merged
        ya_ref[...] = ya.astype(BF16)
        yb_ref[...] = yb.astype(BF16)
        xo_ref[...] = x_ref[...] + _dot(merged, wo_ref[...].reshape(N_DEV * LANES, D))

    rows = lambda a: (a, (tm, a.shape[1]), lambda i: (i, 0))
    out = ((S, D), BF16, (tm, D), lambda i: (i, 0))
    return _call(name, body, (S // tm,),
                 [rows(o_a), rows(o_b), (w_a, w_a.shape, lambda i: (0, 0)),
                  _whole_rows_spec(Gm, REST_WB), _whole_rows_spec(Gm, REST_WOUT)]
                 + _gate_specs(proj, tm) + [(b_gate, (1, 2 * D), lambda i: (0, 0)), rows(x)],
                 [out, out, out, ((S, D), F32, (tm, D), lambda i: (i, 0))], sem=("parallel",))


def merge_bwd(dx2, w_a, Gm, ya, yb, proj, b_gate, name):
    S, D = dx2.shape
    tm = 256

    def body(d_ref, wo_ref, wa_ref, wb_ref, ya_ref, yb_ref, g0, g1, g2, g3, bg_ref,
             dya_ref, dyb_ref, dg_ref, dbg_ref, doa_ref, dob_ref):
        i = pl.program_id(0)
        dm = _dot(d_ref[...].astype(BF16), wo_ref[...].reshape(N_DEV * LANES, D), 1, 1)
        ga = _sigmoid(jnp.concatenate([g0[...], g1[...]], axis=1).astype(F32) + bg_ref[:, 0:D])
        gb = _sigmoid(jnp.concatenate([g2[...], g3[...]], axis=1).astype(F32) + bg_ref[:, D:2 * D])
        dya = (dm * ga).astype(BF16)
        dyb = (dm * gb).astype(BF16)
        dya_ref[...] = dya
        dyb_ref[...] = dyb
        dpa = dm * ya_ref[...].astype(F32) * ga * (1.0 - ga)
        dpb = dm * yb_ref[...].astype(F32) * gb * (1.0 - gb)
        dg_ref[0] = dpa.astype(BF16)
        dg_ref[1] = dpb.astype(BF16)
        doa_ref[...] = _dot(dya, wa_ref[...], 1, 1)
        dob_ref[...] = _dot(dyb, wb_ref[...].reshape(N_DEV * LANES, D), 1, 1).astype(BF16)
        sa =jnp.sum(dpa, axis=0, keepdims=True)
        sb = jnp.sum(dpb, axis=0, keepdims=True)

        @pl.when(i == 0)
        def _():
            dbg_ref[0] = sa
            dbg_ref[1] = sb

        @pl.when(i > 0)
        def _():
            dbg_ref[0] += sa
            dbg_ref[1] += sb

    tile = ((tm, D), lambda i: (i, 0))
    return _call(
        name, body, (S // tm,),
        [(dx2,) + tile, _whole_rows_spec(Gm, REST_WOUT), (w_a, w_a.shape, lambda i: (0, 0)),
         _whole_rows_spec(Gm, REST_WB), (ya,) + tile, (yb,) + tile]
        + _gate_specs(proj, tm) + [(b_gate, (1, 2 * D), lambda i: (0, 0))],
        [((S, D), BF16) + tile, ((S, D), BF16) + tile,
         ((2, S, D), BF16, (2, tm, D), lambda i: (0, i, 0)),
         ((2, 1, D), F32, (2, 1, D), lambda i: (0, 0, 0)),
         ((S, w_a.shape[0]), F32, (tm, w_a.shape[0]), lambda i: (i, 0)),
         ((S, N_HEADS_B * LANES), BF16, (tm, N_HEADS_B * LANES), lambda i: (i, 0))],
        sem=("arbitrary",))


def weight_grad_rows(a, b, grads, row0, name):
    S, M = a.shape
    N = b.shape[1]
    tmm = 512
    tk = WGRAD_TK
    nk = S // tk
    prior = [] if grads is None else [grads]

    def body(*refs):
        a_ref, b_ref, o_ref, acc_ref = refs[len(prior):]
        k = pl.program_id(1)
        p = _dot(a_ref[...], b_ref[...].astype(BF16), 0, 0)

        @pl.when(k == 0)
        def _():
            acc_ref[...] = p

        @pl.when(k > 0)
        def _():
            acc_ref[...] += p

        @pl.when(k == nk - 1)
        def _():
            o_ref[...] = acc_ref[...].astype(BF16).reshape(tmm // LANES, LANES, N)

    return pl.pallas_call(
        body,
        out_shape=jax.ShapeDtypeStruct((N_DEV, MIX_ROWS, N), BF16),
        grid=(M // tmm, nk),
        in_specs=[pl.BlockSpec(memory_space=pl.ANY)] * len(prior)
        + [pl.BlockSpec((tk, tmm), lambda j, k: (k, j)),
           pl.BlockSpec((tk, N), lambda j, k: (k, 0))],
        out_specs=pl.BlockSpec((tmm // LANES, LANES, N), lambda j, k: (j, row0 // LANES, 0)),
        scratch_shapes=[pltpu.VMEM((tmm, N), F32)],
        input_output_aliases={0: 0} if prior else {},
        name=name,
        compiler_params=pltpu.CompilerParams(dimension_semantics=("parallel", "arbitrary"),
                                             vmem_limit_bytes=VMEM_LIMIT),
    )(*prior, a, b)


def weight_grad_plain(a, b, name):
    S, M = a.shape
    N = b.shape[1]
    tk = WGRAD_TK
    nk = S // tk

    def body(a_ref, b_ref, o_ref, acc_ref):
        k = pl.program_id(0)
        p = _dot(a_ref[...], b_ref[...], 0, 0)

        @pl.when(k == 0)
        def _():
            acc_ref[...] = p

        @pl.when(k > 0)
        def _():
            acc_ref[...] += p

        @pl.when(k == nk - 1)
        def _():
            o_ref[...] = acc_ref[...].astype(BF16)

    return _call(name, body, (nk,),
                 [(a, (tk, M), lambda k: (k, 0)), (b, (tk, N), lambda k: (k, 0))],
                 [((M, N), BF16, (M, N), lambda k: (0, 0))],
                 scratch=[pltpu.VMEM((M, N), F32)], sem=("arbitrary",))[0]


def local_step(x, tgt, p, get_g1_up, get_g1_down, get_gm_in, get_gm_rest, get_g2, emit, start_token):
    S, D = x.shape
    after = lambda t: t[0:1, 0:1]
    buckets = _bucket_tables()
    cos_t, sin_t = _rope_tables(S)
    gains = jnp.concatenate([jnp.tile(p["q_norm"] * B_Q_PRESCALE, (1, N_HEADS_B)), jnp.tile(p["k_norm"], (1, N_KV_B)),
                             jnp.ones((1, N_KV_B * LANES), F32)], axis=1)

    n1 = rms_fwd(x, p["ffn1_norm"] + after(start_token), "ffn1_norm")
    bias = bias_build(p["rel_bias"] + after(start_token), buckets)
    g1_up = get_g1_up((n1, bias))
    ab1 = ffn_up(n1, (g1_up, None), "ffn1_up")
    G1 = (g1_up, get_g1_down(ab1))
    x1, hm = ffn_down(ab1, G1, x, p["mix_norm"], "ffn1_down")
    Gw = get_gm_in(hm)
    proj = in_proj(hm, Gw, "in_proj")

    outs, lses = zip(*[a_fwd(proj, bias, g, "a_fwd_%d" % g) for g in range(2)])
    o_a, lse_tot = a_fwd(proj, bias, 2, "a_fwd_2", (outs, lses))

    qkv = qkv_prep(proj, gains, cos_t, sin_t, "qkv_prep")
    k_t = qkv[:, N_HEADS_B * LANES:(N_HEADS_B + N_KV_B) * LANES].T
    o_b, lse_b = flash_fwd(qkv, "flash_fwd")

    Gm = get_gm_rest(o_b)
    w_a = Gm[:, REST_WA:REST_ROWS, :].reshape(N_DEV, GROUP_WIDTH_A, LANES).transpose(1, 0, 2).reshape(GROUP_WIDTH_A, D)
    merged, ya, yb, x2 = merge_fwd(o_a, o_b, w_a, Gm, proj, p["b_gate"], x1, "merge_fwd")

    G2 = get_g2(x2)
    n2, ab2, dx3_b, dab2, dx2, dx2_b, d_ffn2_norm, loss, d_final = ffn_last(
        x2, p["ffn2_norm"], G2, tgt, p["final_norm"], "ffn2")
    gw2 = ffn_bwd_weights(dx3_b, ab2, dab2, n2, "ffn2_bwd")
    t2 = emit("ffn2", gw2)

    dya, dyb, dgate, dbg, do_a, do_b = merge_bwd(dx2_b, w_a, Gm, ya, yb, proj, p["b_gate"] + after(t2),
                                                 "merge_bwd")
    gm_grads = weight_grad_rows(merged, dx2_b, None, MIX_WOUT, "dw_out")
    gm_grads = weight_grad_rows(o_b, dyb, gm_grads, MIX_WB, "dw_branch_b")
    dw_a = weight_grad_plain(o_a, dya, "dw_branch_a")

    dq_r, dk_r, dv_b = flash_bwd(qkv, k_t, do_b, o_b, lse_b, "flash_bwd")
    dq_b, d_q_norm = qk_prep_bwd(dq_r, proj, A_QKV_WIDTH, p["q_norm"], cos_t, sin_t, "q_prep_bwd")
    dk_b, d_k_norm = qk_prep_bwd(dk_r, proj, A_QKV_WIDTH + N_HEADS_B * LANES, p["k_norm"], cos_t, sin_t,
                                 "k_prep_bwd")

    dqkv, dbs = [], []
    for g in range(3):
        dg_, db = a_bwd(proj, bias, do_a, o_a, lse_tot, g, "a_bwd_%d" % g)
        dqkv.append(dg_)
        dbs.append(db)
    d_rel_bias = bias_bwd(jnp.stack(dbs, axis=0).reshape(3, HEADS_PER_GROUP_A, A_TQ, A_WIN), buckets)

    dproj = _dproj_pieces(dqkv, dq_b, jnp.concatenate([dk_b, dv_b], axis=1), dgate)
    gm_grads = in_proj_bwd_dw(dproj[:3], hm, gm_grads, "in_proj_bwd_a")
    gm_grads = in_proj_bwd_dw(dproj[3:], hm, gm_grads, "in_proj_bwd_b")
    dw_a_sh = dw_a.reshape(GROUP_WIDTH_A, N_DEV, LANES).transpose(1, 0, 2).reshape(N_DEV, MIX_ROWS - MIX_WA, D)
    gm_grads = lax.dynamic_update_slice(gm_grads, dw_a_sh, (0, MIX_WA, 0))
    tm = emit("mix", gm_grads)
    dx1, dx1_b, d_mix_norm = in_proj_bwd_dh(dproj, Gw, x1, p["mix_norm"] + after(tm), dx2, "in_proj_bwd")

    dab1 = ffn_bwd_hidden(dx1_b, ab1, G1, "ffn1_bwd")
    gw1 = ffn_bwd_weights(dx1_b, ab1, dab1, n1, "ffn1_bwd")
    t1 = emit("ffn1", gw1)
    dx0, d_ffn1_norm = ffn_bwd_input(dab1, G1, x, p["ffn1_norm"] + after(t1), dx1, "ffn1_bwd")

    small = dict(ffn1_norm=d_ffn1_norm, mix_norm=d_mix_norm, b_gate=dbg.reshape(1, 2 * D),
                 q_norm=d_q_norm, k_norm=d_k_norm, rel_bias=d_rel_bias, ffn2_norm=d_ffn2_norm,
                 final_norm=d_final)
    return loss, dx0, small


def _pack_small(t, loss_row):
    row6 = jnp.concatenate([t["q_norm"].reshape(1, -1), t["k_norm"].reshape(1, -1), t["rel_bias"].reshape(1, -1)], axis=1)
    return jnp.concatenate([t["ffn1_norm"].reshape(1, -1), t["mix_norm"].reshape(1, -1), t["b_gate"].reshape(2, -1),
                            t["ffn2_norm"].reshape(1, -1), t["final_norm"].reshape(1, -1), row6, loss_row], axis=0)


def _unpack_small(a, shapes):
    return dict(ffn1_norm=a[0:1].reshape(shapes["ffn1_norm"]), mix_norm=a[1:2].reshape(shapes["mix_norm"]),
                b_gate=a[2:4].reshape(shapes["b_gate"]), ffn2_norm=a[4:5].reshape(shapes["ffn2_norm"]),
                final_norm=a[5].reshape(shapes["final_norm"]), q_norm=a[6:7, 0:128].reshape(shapes["q_norm"]),
                k_norm=a[6:7, 128:256].reshape(shapes["k_norm"]), rel_bias=a[6, 256:1024].reshape(shapes["rel_bias"]))


SMALL = ("ffn1_norm", "mix_norm", "b_gate", "q_norm", "k_norm", "rel_bias", "ffn2_norm", "final_norm")
ORDER = ("ffn1_norm", "ffn1_w1", "ffn1_w3", "ffn1_w2", "mix_norm", "w_in", "b_gate", "q_norm", "k_norm", "rel_bias",
         "w_branch_a", "w_branch_b", "w_out", "ffn2_norm", "ffn2_w1", "ffn2_w3", "ffn2_w2", "final_norm")


def kernel(x, ffn1_norm, ffn1_w1, ffn1_w3, ffn1_w2, mix_norm, w_in, b_gate, q_norm, k_norm, rel_bias, w_branch_a, w_branch_b, w_out, ffn2_norm, ffn2_w1, ffn2_w3, ffn2_w2, final_norm, loss_target, m_ffn1_norm, m_ffn1_w1, m_ffn1_w3, m_ffn1_w2, m_mix_norm, m_w_in, m_b_gate, m_q_norm, m_k_norm, m_rel_bias, m_w_branch_a, m_w_branch_b, m_w_out, m_ffn2_norm, m_ffn2_w1, m_ffn2_w3, m_ffn2_w2, m_final_norm, v_ffn1_norm, v_ffn1_w1, v_ffn1_w3, v_ffn1_w2, v_mix_norm, v_w_in, v_b_gate, v_q_norm, v_k_norm, v_rel_bias, v_w_branch_a, v_w_branch_b, v_w_out, v_ffn2_norm, v_ffn2_w1, v_ffn2_w3, v_ffn2_w2, v_final_norm):
    args = dict(locals())
    w = {n: args[n] for n in ORDER}
    m = {n: args["m_" + n] for n in ORDER}
    v = {n: args["v_" + n] for n in ORDER}
    D = x.shape[2]

    blocks = (
        ("ffn1_up", jnp.concatenate([ffn1_w1[0].T, ffn1_w3[0].T], axis=0)),
        ("ffn1_down", ffn1_w2[0]),
        ("mix_in", w_in[0]),
        ("mix_rest", jnp.concatenate([w_branch_b[0], w_out[0], w_branch_a[0].reshape(REST_ROWS - REST_WA, D)], axis=0)),
        ("ffn2", jnp.concatenate([ffn2_w1[0].T, ffn2_w3[0].T, ffn2_w2[0]], axis=0)),
    )
    direct = ("mix_rest", "ffn2")
    started = all_gather_start_all([(b.astype(BF16), tag in direct) for tag, b in blocks], "all_gather_start")
    gathers = {tag: s for (tag, _), s in zip(blocks, started)}
    start_token = started[0][4]

    def gathered(tag):
        def get(after):
            if tag in direct:
                return all_gather_place_own(*_split_wait("all_gather_" + tag + "_wait", gathers[tag], N_DEV - 1, after),
                                            "all_gather_" + tag + "_own")
            return all_gather_finish(*_split_wait("all_gather_" + tag + "_wait", gathers[tag], 4, after),
                                     "all_gather_" + tag + "_finish")
        return get

    core = lax.axis_index("c").astype(jnp.int32).reshape(1)
    chip = (2 * lax.axis_index("x") + lax.axis_index("y")).astype(jnp.int32).reshape(1)
    device = 2 * chip + core
    exchanges = {}

    def emit(tag, gw):
        if tag == "ffn1":
            (theirs,) = reduce_scatter_pair([gw], "reduce_scatter_pair_" + tag)
            part = pair_add(gw, theirs, core, "pair_add_" + tag)
            exchanges[tag] = reduce_scatter_start(part, "reduce_scatter_" + tag + "_start")
        else:
            exchanges[tag] = reduce_scatter_start_direct(gw, "reduce_scatter_" + tag + "_start")
        return exchanges[tag][4]

    small_p = dict(ffn1_norm=ffn1_norm, mix_norm=mix_norm, b_gate=b_gate, q_norm=q_norm, k_norm=k_norm,
                   rel_bias=rel_bias, ffn2_norm=ffn2_norm, final_norm=final_norm.reshape(1, D))
    loss_p, grad_x, small_g = local_step(x[0], loss_target[0], small_p, gathered("ffn1_up"), gathered("ffn1_down"),
                                         gathered("mix_in"), gathered("mix_rest"), gathered("ffn2"), emit, start_token)

    def landed(tag, after):
        n_others, me = (3, chip) if tag == "ffn1" else (N_DEV - 1, device)
        return tuple(_split_wait("reduce_scatter_" + tag + "_wait", exchanges[tag], n_others, after)) + (me,)

    grads, delta, new_m, new_v = {}, {}, {}, {}

    def finish(n, part, land, me, off, blk, transposed=False):
        shp = w[n].shape
        if transposed:
            to2 = lambda a: a.reshape(shp[-2], shp[-1]).T
            back = lambda a: a.T.reshape(shp)
        else:
            to2 = lambda a: a.reshape(shp[-2], shp[-1])
            back = lambda a: a.reshape(shp)
        res = sum_adamw(part, land, me, off, blk, to2(w[n]), to2(m[n]), to2(v[n]), "update_" + n)
        grads[n], delta[n], new_m[n], new_v[n] = [back(a) for a in res]

    last_token = exchanges["ffn1"][4]
    for tag, after in (("ffn2", last_token), ("ffn1", grad_x)):
        group = landed(tag, after)
        finish(tag + "_w1", *group, 0, FFN_SHARD, transposed=True)
        finish(tag + "_w3", *group, FFN_SHARD, FFN_SHARD, transposed=True)
        finish(tag + "_w2", *group, 2 * FFN_SHARD, FFN_SHARD)
        if tag == "ffn2":
            group_m = landed("mix", last_token)
            finish("w_in", *group_m, MIX_WIN, LANES)
            finish("w_branch_b", *group_m, MIX_WB, LANES)
            finish("w_out", *group_m, MIX_WOUT, LANES)
            grads["w_branch_a"] = sum_landed(*group_m, MIX_WA, MIX_ROWS - MIX_WA, MIX_ROWS - MIX_WA,
                                             "w_branch_a_sum").reshape(w_branch_a.shape)
    loss_row = jnp.pad(loss_p, ((0, 0), (0, D - LANES)))
    smalls = small_all_gather(_pack_small(small_g, loss_row), new_v["w_in"])
    small_sum = sum_slots(smalls, 0, N_DEV, N_DEV, "small_sum")
    small_shapes = {n: w[n].shape for n in SMALL}
    grads.update(_unpack_small(small_sum, small_shapes))
    loss = small_sum[7, 0]

    n = "w_branch_a"
    two_d = lambda a: a.reshape(w[n].shape[-2], w[n].shape[-1])
    d_, m_, v_ = adamw(two_d(w[n]), two_d(grads[n]), two_d(m[n]), two_d(v[n]), "adamw_" + n)
    delta[n], new_m[n], new_v[n] = [a.reshape(w[n].shape) for a in (d_, m_, v_)]
    zero_row = jnp.zeros((1, D), F32)
    pack = lambda t: _pack_small({n: t[n] for n in SMALL}, zero_row)
    d_, m_, v_ = adamw(pack(w), small_sum, pack(m), pack(v), "adamw_small")
    for src, dst in ((d_, delta), (m_, new_m), (v_, new_v)):
        dst.update(_unpack_small(src, small_shapes))

    return (loss, grad_x[None], *[grads[n] for n in ORDER], *[delta[n] for n in ORDER],
            *[new_m[n] for n in ORDER], *[new_v[n] for n in ORDER])
```

```python
import math

import jax
import jax.numpy as jnp
from jax import lax
from jax.experimental import pallas as pl
from jax.experimental.pallas import tpu as pltpu

F32 = jnp.float32
BF16 = jnp.bfloat16
MESH = pl.DeviceIdType.MESH

V7X_VMEM_BYTES = 64 * 1024 * 1024
VMEM_LIMIT = V7X_VMEM_BYTES - 8 * 1024 * 1024
LANES = 128

N_DEV = 8
EPS = 1e-6
NEG_INF = -1e30

DILATIONS = (1, 4, 16)
HALF_WINDOW = 64
HEAD_DIM_A = 64
HEADS_PER_GROUP_A = 8
GROUP_WIDTH_A = 512
A_QKV_WIDTH = 4608
A_GROUP_QKV = A_QKV_WIDTH // 3
A_TQ = 128
A_WIN = A_TQ + 2 * HALF_WINDOW
A_UNROLL = 8
A_SCALE = HEAD_DIM_A ** -0.5
WGRAD_TK = 2048
HEAD_DIM_B = 128
N_HEADS_B = 8
N_KV_B = 2
GQA_GROUP_B = 4
GRID_W = 64
ROPE_THETA = 10000.0
B_TQ_FWD = 256
B_TQ_BWD = 512
B_HEADS_PER_STEP = 4
LOG2E = 1.4426950408889634
B_Q_PRESCALE = HEAD_DIM_B ** -0.5 * LOG2E
N_BUCKETS = 32
MAX_DISTANCE = 1024
PB_GATE_A = 1536

ADAM_LR = 0.001
ADAM_B1 = 0.9
ADAM_B2 = 0.999
ADAM_EPS = 1e-08
ADAM_WD = 0.01
ADAM_STEP = 10

FFN_SHARD = 352
MIX_WIN, MIX_WB, MIX_WOUT, MIX_WA = 0, 1024, 1152, 1280
MIX_ROWS = 1344
REST_WB, REST_WOUT, REST_WA, REST_ROWS = 0, 128, 256, 320


def _dot(a, b, ca=1, cb=0):
    return lax.dot_general(a, b, (((ca,), (cb,)), ((), ())), preferred_element_type=F32)


def _call(name, body, grid, ins, outs, scratch=(), sem=None, aliases=None):
    ins = [tuple(i) + (None,) * (4 - len(i)) for i in ins]
    res = pl.pallas_call(
        body,
        out_shape=[jax.ShapeDtypeStruct(s, d) for (s, d, _, _) in outs],
        grid=grid,
        in_specs=[pl.BlockSpec(bs, im, pipeline_mode=pm) for (_, bs, im, pm) in ins],
        out_specs=[pl.BlockSpec(bs, im) for (_, _, bs, im) in outs],
        scratch_shapes=list(scratch),
        name=name,
        input_output_aliases=aliases or {},
        compiler_params=pltpu.CompilerParams(dimension_semantics=sem, vmem_limit_bytes=VMEM_LIMIT),
    )(*[i[0] for i in ins])
    return res


def _sigmoid(x):
    return 0.5 * jnp.tanh(0.5 * x) + 0.5


def _position():
    return lax.axis_index("x"), lax.axis_index("y"), lax.axis_index("c")


def _hbm_specs(n):
    return [pl.BlockSpec(memory_space=pl.ANY) for _ in range(n)]


PAIR_BUFFERS = 4


def reduce_scatter_pair(grads, name):
    n = len(grads)
    C = grads[0].shape[2]
    half = [g.shape[1] // 2 for g in grads]
    chunks = [(i, q, hf) for i in range(n) for q in range(4) for hf in range(2)]
    nb = PAIR_BUFFERS

    def body(*refs):
        ins, theirs = refs[:n], refs[n:2 * n]
        buf, load_sems, send_sems, recv_sems = refs[2 * n:]
        x, y, c = _position()
        sibling = (x, y, 1 - c)

        def load(k):
            i, q, hf = chunks[k]
            r = half[i]
            return pltpu.make_async_copy(ins[i].at[2 * q + (1 - c), pl.ds(hf * r, r), :],
                                         buf.at[k % nb, pl.ds(0, r), :], load_sems.at[k % nb])

        def send(k):
            i, q, hf = chunks[k]
            r = half[i]
            return pltpu.make_async_remote_copy(
                src_ref=buf.at[k % nb, pl.ds(0, r), :], dst_ref=theirs[i].at[q, pl.ds(hf * r, r), :],
                send_sem=send_sems.at[k % nb], recv_sem=recv_sems.at[i],
                device_id=sibling, device_id_type=MESH)

        for k in range(len(chunks) + 1):
            if k < len(chunks):
                if k >= nb:
                    send(k - nb).wait_send()
                load(k).start()
            if k >= 1:
                load(k - 1).wait()
                send(k - 1).start()
        for k in range(max(0, len(chunks) - nb), len(chunks)):
            send(k).wait_send()
        for i in range(n):
            pltpu.make_async_remote_copy(
                src_ref=theirs[i], dst_ref=theirs[i], send_sem=send_sems.at[0], recv_sem=recv_sems.at[i],
                device_id=sibling, device_id_type=MESH).wait_recv()

    return pl.pallas_call(
        body,
        out_shape=[jax.ShapeDtypeStruct((4,) + g.shape[1:], g.dtype) for g in grads],
        in_specs=_hbm_specs(n),
        out_specs=_hbm_specs(n),
        scratch_shapes=[pltpu.VMEM((nb, max(half), C), grads[0].dtype), pltpu.SemaphoreType.DMA((nb,)),
                        pltpu.SemaphoreType.DMA((nb,)), pltpu.SemaphoreType.DMA((n,))],
        name=name,
        compiler_params=pltpu.CompilerParams(vmem_limit_bytes=VMEM_LIMIT),
    )(*grads)


_HBM_SPEC = pl.BlockSpec(memory_space=pltpu.HBM)
_SEM_SPEC = pl.BlockSpec(memory_space=pltpu.SEMAPHORE)
_TOKEN_SPEC = pl.BlockSpec(memory_space=pltpu.VMEM)
_DATAFLOW = pltpu.SideEffectType.DATAFLOW_SIDE_EFFECTING


def _split_start_many(name, exchanges):
    n = len(exchanges)

    def full_body(*refs):
        srcs, lands = refs[:n], refs[n:2 * n]
        sems = refs[2 * n:4 * n]
        token = refs[-1]
        for i, (body, _, _) in enumerate(exchanges):
            body(srcs[i], lands[i], sems[2 * i], sems[2 * i + 1])
        token[...] = jnp.zeros_like(token)

    srcs = [pltpu.with_memory_space_constraint(src, pltpu.HBM) for _, src, _ in exchanges]
    lands = [pltpu.with_memory_space_constraint(lax.empty(shape, src.dtype), pltpu.HBM)
             for _, src, shape in exchanges]
    res = pl.pallas_call(
        full_body, name=name,
        out_shape=(pltpu.SemaphoreType.DMA(()),) * (2 * n)
        + tuple(pltpu.HBM(a.shape, a.dtype) for a in srcs + lands) + (jax.ShapeDtypeStruct((8, LANES), F32),),
        in_specs=(_HBM_SPEC,) * (2 * n),
        out_specs=(_SEM_SPEC,) * (2 * n) + (_HBM_SPEC,) * (2 * n) + (_TOKEN_SPEC,),
        input_output_aliases={i: 2 * n + i for i in range(2 * n)},
        compiler_params=pltpu.CompilerParams(has_side_effects=_DATAFLOW),
    )(*srcs, *lands)
    return [(res[2 * i], res[2 * i + 1], res[2 * n + i], res[3 * n + i], res[-1]) for i in range(n)]


def _split_start(name, body, src, land_shape):
    return _split_start_many(name, [(body, src, land_shape)])[0]


def _split_wait(name, started, n_blocks, after):
    send_sem, recv_sem, src_thru, land_thru, _ = started
    after = after if isinstance(after, tuple) else (after,)

    def body(src_ref, land_ref, send_sem, recv_sem, *rest):
        x, y, c = _position()
        blocks = land_ref.at[pl.ds(0, n_blocks)]
        copy = pltpu.make_async_remote_copy(src_ref=blocks, dst_ref=blocks, send_sem=send_sem, recv_sem=recv_sem,
                                            device_id=(x, y, c), device_id_type=MESH)
        copy.wait_send()
        copy.wait_recv()

    return pl.pallas_call(
        body, name=name,
        out_shape=(pltpu.HBM(src_thru.shape, src_thru.dtype), pltpu.HBM(land_thru.shape, land_thru.dtype)),
        in_specs=(_HBM_SPEC, _HBM_SPEC, _SEM_SPEC, _SEM_SPEC) + (pl.BlockSpec(memory_space=pl.ANY),) * len(after),
        out_specs=(_HBM_SPEC, _HBM_SPEC),
        input_output_aliases={0: 0, 1: 1},
        compiler_params=pltpu.CompilerParams(has_side_effects=_DATAFLOW),
    )(src_thru, land_thru, send_sem, recv_sem, *after)


def all_gather_start_all(blocks, name):
    def starter(direct):
        def body(b_ref, land_ref, send_sem, recv_sem):
            x, y, c = _position()
            peers = _other_devices(x, y, c) if direct else [(x, y, 1 - c), (1 - x, y, c), (x, 1 - y, c),
                                                            (1 - x, 1 - y, c)]
            for peer in peers:
                pltpu.make_async_remote_copy(src_ref=b_ref, dst_ref=land_ref.at[4 * x + 2 * y + c],
                                             send_sem=send_sem, recv_sem=recv_sem,
                                             device_id=peer, device_id_type=MESH).start()
        return body

    return _split_start_many(name, [(starter(direct), block, (N_DEV,) + block.shape) for block, direct in blocks])


def all_gather_finish(block, land, name):
    R, C = block.shape

    def body(b_ref, land_in, land_ref, stage, load_sems, send_sems, recv_sems, own_sem):
        x, y, c = _position()
        sibling = (x, y, 1 - c)
        chips = [(1 - x, y), (x, 1 - y), (1 - x, 1 - y)]
        own_in = pltpu.make_async_copy(b_ref, stage.at[3], load_sems.at[3])
        own_in.start()
        loads = [pltpu.make_async_copy(land_in.at[4 * px + 2 * py + c], stage.at[j], load_sems.at[j])
                 for j, (px, py) in enumerate(chips)]
        for ld in loads:
            ld.start()
        sends = []
        for j, (px, py) in enumerate(chips):
            loads[j].wait()
            dst = land_ref.at[4 * px + 2 * py + c]
            cp = pltpu.make_async_remote_copy(src_ref=stage.at[j], dst_ref=dst, send_sem=send_sems.at[j],
                                              recv_sem=recv_sems.at[j], device_id=sibling, device_id_type=MESH)
            cp.start()
            sends.append(cp)
        own_in.wait()
        own_out = pltpu.make_async_copy(stage.at[3], land_ref.at[4 * x + 2 * y + c], own_sem)
        own_out.start()
        for j, (px, py) in enumerate(chips):
            dst = land_ref.at[4 * px + 2 * py + (1 - c)]
            pltpu.make_async_remote_copy(src_ref=stage.at[j], dst_ref=dst, send_sem=send_sems.at[j],
                                         recv_sem=recv_sems.at[j], device_id=sibling,
                                         device_id_type=MESH).wait_recv()
        for cp in sends:
            cp.wait_send()
        own_out.wait()

    return pl.pallas_call(
        body,
        out_shape=jax.ShapeDtypeStruct(land.shape, land.dtype),
        in_specs=_hbm_specs(2),
        out_specs=pl.BlockSpec(memory_space=pl.ANY),
        scratch_shapes=[pltpu.VMEM((4, R, C), block.dtype), pltpu.SemaphoreType.DMA((4,)),
                        pltpu.SemaphoreType.DMA((3,)), pltpu.SemaphoreType.DMA((3,)), pltpu.SemaphoreType.DMA],
        input_output_aliases={1: 0},
        name=name,
        compiler_params=pltpu.CompilerParams(vmem_limit_bytes=VMEM_LIMIT),
    )(block, land)


def reduce_scatter_start(parts, name):
    def body(p_ref, land_ref, send_sem, recv_sem):
        x, y, c = _position()
        for px, py in [(1 - x, y), (x, 1 - y), (1 - x, 1 - y)]:
            pltpu.make_async_remote_copy(src_ref=p_ref.at[2 * px + py], dst_ref=land_ref.at[2 * x + y],
                                         send_sem=send_sem, recv_sem=recv_sem,
                                         device_id=(px, py, c), device_id_type=MESH).start()

    return _split_start(name, body, parts, parts.shape)


def _other_devices(x, y, c):
    return [(1 - x if k & 4 else x, 1 - y if k & 2 else y, 1 - c if k & 1 else c) for k in range(1, N_DEV)]


def all_gather_place_own(block, land, name):
    R, C = block.shape

    def body(b_ref, land_in, land_ref, stage, sems):
        x, y, c = _position()
        load = pltpu.make_async_copy(b_ref, stage, sems.at[0])
        load.start()
        load.wait()
        store = pltpu.make_async_copy(stage, land_ref.at[4 * x + 2 * y + c], sems.at[1])
        store.start()
        store.wait()

    return pl.pallas_call(
        body,
        out_shape=jax.ShapeDtypeStruct(land.shape, land.dtype),
        in_specs=_hbm_specs(2),
        out_specs=pl.BlockSpec(memory_space=pl.ANY),
        scratch_shapes=[pltpu.VMEM((R, C), block.dtype), pltpu.SemaphoreType.DMA((2,))],
        input_output_aliases={1: 0},
        name=name,
    )(block, land)


def reduce_scatter_start_direct(grads, name):
    def body(g_ref, land_ref, send_sem, recv_sem):
        x, y, c = _position()
        for px, py, pc in _other_devices(x, y, c):
            pltpu.make_async_remote_copy(src_ref=g_ref.at[4 * px + 2 * py + pc],
                                         dst_ref=land_ref.at[4 * x + 2 * y + c],
                                         send_sem=send_sem, recv_sem=recv_sem,
                                         device_id=(px, py, pc), device_id_type=MESH).start()

    return _split_start(name, body, grads, grads.shape)


def small_all_gather(small, after):
    def body(small_ref, after_ref, smalls, s_send, s_recv, s_local):
        x, y, c = _position()
        me = 4 * x + 2 * y + c
        lc = pltpu.make_async_copy(small_ref, smalls.at[me], s_local)
        lc.start()
        remote = []
        k = 0
        for dx in (0, 1):
            for dy in (0, 1):
                for dc in (0, 1):
                    if dx + dy + dc == 0:
                        continue
                    peer = (1 - x if dx else x, 1 - y if dy else y, 1 - c if dc else c)
                    rc = pltpu.make_async_remote_copy(
                        src_ref=small_ref, dst_ref=smalls.at[me],
                        send_sem=s_send.at[k], recv_sem=s_recv.at[k],
                        device_id=peer, device_id_type=MESH)
                    rc.start()
                    remote.append(rc)
                    k += 1
        for rc in remote:
            rc.wait()
        lc.wait()

    return pl.pallas_call(
        body,
        out_shape=jax.ShapeDtypeStruct((N_DEV,) + small.shape, small.dtype),
        in_specs=_hbm_specs(2),
        out_specs=pl.BlockSpec(memory_space=pl.ANY),
        scratch_shapes=[pltpu.SemaphoreType.DMA((7,)), pltpu.SemaphoreType.DMA((7,)), pltpu.SemaphoreType.DMA],
        name="small_all_gather",
    )(small, after)


def pair_add(grads, theirs, core, name):
    _, R, C = theirs.shape
    tr = R // 2

    def body(c_ref, a_ref, b_ref, o_ref):
        o_ref[...] = (a_ref[...].astype(F32) + b_ref[...].astype(F32)).astype(BF16)

    return pl.pallas_call(
        body,
        out_shape=jax.ShapeDtypeStruct(theirs.shape, BF16),
        grid_spec=pltpu.PrefetchScalarGridSpec(
            num_scalar_prefetch=1, grid=(4, R // tr),
            in_specs=[pl.BlockSpec((None, tr, C), lambda q, i, c: (2 * q + c[0], i, 0)),
                      pl.BlockSpec((None, tr, C), lambda q, i, c: (q, i, 0))],
            out_specs=pl.BlockSpec((None, tr, C), lambda q, i, c: (q, i, 0))),
        name=name,
        compiler_params=pltpu.CompilerParams(dimension_semantics=("parallel", "parallel"),
                                             vmem_limit_bytes=VMEM_LIMIT),
    )(core, grads, theirs)


def sum_slots(recv, off, rows, blk, name):
    nq, _, C = recv.shape
    ob = off // blk

    def body(r_ref, o_ref):
        acc = r_ref[0].astype(F32)
        for q in range(1, nq):
            acc = acc + r_ref[q].astype(F32)
        o_ref[...] = acc

    return _call(name, body, (rows // blk,),
                 [(recv, (nq, blk, C), lambda i: (0, ob + i, 0))],
                 [((rows, C), F32, (blk, C), lambda i: (i, 0))], sem=("parallel",))[0]


def _sum_terms(refs):
    acc = refs[0][...].astype(F32)
    for r in refs[1:]:
        acc = acc + r[...].astype(F32)
    return acc


def sum_landed(own, land, me, off, rows, blk, name):
    n, _, C = land.shape
    ob = off // blk

    def body(c_ref, *refs):
        refs[n][...] = _sum_terms(refs[:n])

    def entry(flip):
        return pl.BlockSpec((None, blk, C), lambda i, c: (c[0] ^ flip, ob + i, 0))

    return pl.pallas_call(
        body,
        out_shape=jax.ShapeDtypeStruct((rows, C), F32),
        grid_spec=pltpu.PrefetchScalarGridSpec(
            num_scalar_prefetch=1, grid=(rows // blk,),
            in_specs=[entry(k) for k in range(n)],
            out_specs=pl.BlockSpec((blk, C), lambda i, c: (i, 0))),
        name=name,
        compiler_params=pltpu.CompilerParams(dimension_semantics=("parallel",), vmem_limit_bytes=VMEM_LIMIT),
    )(me, own, *([land] * (n - 1)))


def _adamw_update(wv, gv, mv, vv):
    nm = ADAM_B1 * mv + (1.0 - ADAM_B1) * gv
    nv = ADAM_B2 * vv + (1.0 - ADAM_B2) * (gv * gv)
    c1 = 1.0 / (1.0 - ADAM_B1 ** ADAM_STEP)
    c2 = 1.0 / (1.0 - ADAM_B2 ** ADAM_STEP)
    return -ADAM_LR * ((nm * c1) / (jnp.sqrt(nv * c2) + ADAM_EPS) + ADAM_WD * wv), nm, nv


def sum_adamw(own, land, me, off, blk, w, m, v, name):
    rows, C = w.shape
    n = land.shape[0]
    ob = off // blk

    def body(c_ref, *refs):
        w_ref, m_ref, v_ref, g_out, d_out, m_out, v_out = refs[n:]
        gv = _sum_terms(refs[:n])
        g_out[...] = gv
        d_out[...], m_out[...], v_out[...] = _adamw_update(w_ref[...], gv, m_ref[...], v_ref[...])

    def entry(flip):
        return pl.BlockSpec((None, blk, C), lambda i, c: (c[0] ^ flip, ob + i, 0))

    plain = pl.BlockSpec((blk, C), lambda i, c: (i, 0))
    return pl.pallas_call(
        body,
        out_shape=[jax.ShapeDtypeStruct((rows, C), F32)] * 4,
        grid_spec=pltpu.PrefetchScalarGridSpec(
            num_scalar_prefetch=1, grid=(rows // blk,),
            in_specs=[entry(k) for k in range(n)] + [plain, plain, plain],
            out_specs=[plain] * 4),
        name=name,
        compiler_params=pltpu.CompilerParams(dimension_semantics=("parallel",), vmem_limit_bytes=VMEM_LIMIT),
    )(me, own, *([land] * (n - 1)), w, m, v)


def adamw(w, g, m, v, name):
    R, C = w.shape
    tr = R
    for cand in (256, 128, 64, 32, 16, 8):
        if R % cand == 0 and R > cand:
            tr = cand
            break

    def body(w_ref, g_ref, m_ref, v_ref, d_ref, nm_ref, nv_ref):
        d_ref[...], nm_ref[...], nv_ref[...] = _adamw_update(w_ref[...], g_ref[...], m_ref[...], v_ref[...])

    spec = ((tr, C), lambda i: (i, 0))
    out = ((R, C), F32) + spec
    return _call(name, body, (R // tr,), [(w,) + spec, (g,) + spec, (m,) + spec, (v,) + spec],
                 [out, out, out], sem=("parallel",))


def _rms_tile(xv, gv):
    r = lax.rsqrt(jnp.mean(xv * xv, axis=-1, keepdims=True) + EPS)
    return (xv * r * gv).astype(BF16)


def rms_fwd(x, g, name):
    S, D = x.shape
    tr = 512

    def body(x_ref, g_ref, o_ref):
        o_ref[...] = _rms_tile(x_ref[...], g_ref[...])

    return _call(name, body, (S // tr,),
                 [(x, (tr, D), lambda i: (i, 0)), (g, (1, D), lambda i: (0, 0))],
                 [((S, D), BF16, (tr, D), lambda i: (i, 0))], sem=("parallel",))[0]


def _rms_bwd_tile(dn, xv, gv):
    r = lax.rsqrt(jnp.mean(xv * xv, axis=-1, keepdims=True) + EPS)
    xh = xv * r
    dxh = dn * gv
    dx = r * (dxh - xh * jnp.mean(dxh * xh, axis=-1, keepdims=True))
    return dx, dn * xh


def _final_loss_tile(xv, tv, gv):
    D = xv.shape[1]
    r = lax.rsqrt(jnp.mean(xv * xv, axis=-1, keepdims=True) + EPS)
    xh = xv * r
    e = xh * gv - tv
    part = 0.5 * jnp.sum(jnp.sum(e * e, axis=-1, keepdims=True) * (1.0 / D), axis=0, keepdims=True)
    dy = e * (1.0 / D)
    dxh = dy * gv
    dx = r * (dxh - xh * jnp.mean(dxh * xh, axis=-1, keepdims=True))
    return part, dx, jnp.sum(dy * xh, axis=0, keepdims=True)


FFN_TF = 4 * FFN_SHARD


def _ffn_pick(G, which):
    if isinstance(G, tuple):
        return (G[0], which) if which < 2 else (G[1], 0)
    return G, which


def _ffn_w_spec(G, which, imap):
    arr, blk = _ffn_pick(G, which)
    return (arr, (4, FFN_SHARD, arr.shape[2]), lambda *idx: (imap(*idx), blk, 0))


def _ffn_whole_w_spec(G, which):
    arr, blk = _ffn_pick(G, which)
    return (arr, (N_DEV, FFN_SHARD, arr.shape[2]), lambda *idx: (0, blk, 0), pl.Buffered(1))


def _ffn_hidden(a, b):
    av, bv = a.astype(F32), b.astype(F32)
    return (av * _sigmoid(av) * bv).astype(BF16)


def ffn_up(n, G, name):
    S, D = n.shape
    F = N_DEV * FFN_SHARD
    tm = 256

    def body(n_ref, w1_ref, w3_ref, abh_ref):
        nv = n_ref[...]
        a = _dot(nv, w1_ref[...].reshape(F, D), 1, 1).astype(BF16)
        b = _dot(nv, w3_ref[...].reshape(F, D), 1, 1).astype(BF16)
        abh_ref[0] = a
        abh_ref[1] = b
        abh_ref[2] = _ffn_hidden(a, b)

    return _call(name, body, (S // tm,),
                 [(n, (tm, D), lambda i: (i, 0)),
                  _ffn_whole_w_spec(G, 0), _ffn_whole_w_spec(G, 1)],
                 [((3, S, F), BF16, (3, tm, F), lambda i: (0, i, 0))],
                 sem=("parallel",))[0]


def ffn_down(abh, G, x, g_next, name):
    _, S, F = abh.shape
    D = x.shape[1]
    tm = 512

    def body(h_ref, w2_ref, x_ref, g_ref, o_ref, n_ref):
        xo = x_ref[...] + 0.5 * _dot(h_ref[...], w2_ref[...].reshape(F, D))
        o_ref[...] = xo
        n_ref[...] = _rms_tile(xo, g_ref[...])

    tile = ((tm, D), lambda i: (i, 0))
    return _call(name, body, (S // tm,),
                 [(abh, (None, tm, F), lambda i: (2, i, 0)), _ffn_whole_w_spec(G, 2),
                  (x,) + tile, (g_next, (1, D), lambda i: (0, 0))],
                 [((S, D), F32) + tile, ((S, D), BF16) + tile], sem=("parallel",))


def ffn_last(x, g, G, tgt, g_final, name):
    S, D = x.shape
    F = N_DEV * FFN_SHARD
    tm = 256

    def body(x_ref, g_ref, w1_ref, w3_ref, w2_ref, t_ref, gf_ref,
             n_ref, abh_ref, dxo_ref, dab_ref, dx_ref, dxb_ref, dg_ref, l_ref, dgf_ref):
        i = pl.program_id(0)
        xv, gv = x_ref[...], g_ref[...]
        chunks = [(slice(4 * f, 4 * f + 4), slice(f * FFN_TF, (f + 1) * FFN_TF)) for f in range(F // FFN_TF)]
        weight = lambda w_ref, slots: w_ref[slots].reshape(FFN_TF, D)
        nv = _rms_tile(xv, gv)
        n_ref[...] = nv
        y = None
        for slots, cols in chunks:
            a = _dot(nv, weight(w1_ref, slots), 1, 1).astype(BF16)
            b = _dot(nv, weight(w3_ref, slots), 1, 1).astype(BF16)
            h = _ffn_hidden(a, b)
            abh_ref[0, :, cols] = a
            abh_ref[1, :, cols] = b
            abh_ref[2, :, cols] = h
            t = _dot(h, weight(w2_ref, slots))
            y = t if y is None else y + t
        part, dxo, dgfp = _final_loss_tile(xv + 0.5 * y, t_ref[...], gf_ref[...])
        dxo_b = dxo.astype(BF16)
        dxo_ref[...] = dxo_b
        dn = None
        for slots, cols in chunks:
            dh = 0.5 * _dot(dxo_b, weight(w2_ref, slots), 1, 1)
            da, db = _ffn_hidden_grads(dh, abh_ref[0, :, cols].astype(F32), abh_ref[1, :, cols].astype(F32))
            da, db = da.astype(BF16), db.astype(BF16)
            dab_ref[0, :, cols] = da
            dab_ref[1, :, cols] = db
            t = _dot(da, weight(w1_ref, slots)) + _dot(db, weight(w3_ref, slots))
            dn = t if dn is None else dn + t
        dx, dgt = _rms_bwd_tile(dn, xv, gv)
        dx = dxo + dx
        dx_ref[...] = dx
        dxb_ref[...] = dx.astype(BF16)
        dgp = jnp.sum(dgt, axis=0, keepdims=True)

        @pl.when(i == 0)
        def _():
            dg_ref[...] = dgp
            l_ref[...] = jnp.broadcast_to(part, l_ref.shape)
            dgf_ref[...] = dgfp

        @pl.when(i > 0)
        def _():
            dg_ref[...] += dgp
            l_ref[...] += jnp.broadcast_to(part, l_ref.shape)
            dgf_ref[...] += dgfp

    tile = ((tm, D), lambda i: (i, 0))
    gain = ((1, D), lambda i: (0, 0))
    return _call(name, body, (S // tm,),
                 [(x,) + tile, (g,) + gain,
                  _ffn_whole_w_spec(G, 0), _ffn_whole_w_spec(G, 1), _ffn_whole_w_spec(G, 2),
                  (tgt,) + tile, (g_final,) + gain],
                 [((S, D), BF16) + tile, ((3, S, F), BF16, (3, tm, F), lambda i: (0, i, 0)),
                  ((S, D), BF16) + tile, ((2, S, F), BF16, (2, tm, F), lambda i: (0, i, 0)),
                  ((S, D), F32) + tile, ((S, D), BF16) + tile, ((1, D), F32) + gain,
                  ((1, LANES), F32, (1, LANES), lambda i: (0, 0)), ((1, D), F32) + gain],
                 sem=("arbitrary",))


def _ffn_hidden_grads(dh, av, bv):
    sig = _sigmoid(av)
    return dh * bv * (sig * (1.0 + av * (1.0 - sig))), dh * (av * sig)


def ffn_bwd_hidden(dxo, abh, G, name):
    _, S, F = abh.shape
    D = dxo.shape[1]
    tm = 256

    def body(d_ref, w2_ref, ab_ref, o_ref):
        dh = 0.5 * _dot(d_ref[...].astype(BF16), w2_ref[...].reshape(F, D), 1, 1)
        da, db = _ffn_hidden_grads(dh, ab_ref[0].astype(F32), ab_ref[1].astype(F32))
        o_ref[0] = da.astype(BF16)
        o_ref[1] = db.astype(BF16)

    return _call(name + "_down_bwd", body, (S // tm,),
                 [(dxo, (tm, D), lambda i: (i, 0)), _ffn_whole_w_spec(G, 2),
                  (abh, (2, tm, F), lambda i: (0, i, 0))],
                 [((2, S, F), BF16, (2, tm, F), lambda i: (0, i, 0))],
                 sem=("parallel",))[0]


def ffn_bwd_weights(dxo, abh, dab, n, name):
    _, S, F = abh.shape
    D = dxo.shape[1]
    nf = F // FFN_TF
    tk = WGRAD_TK
    nk = S // tk
    gshape = (N_DEV, 3 * FFN_SHARD, D)

    def dw2_body(h_ref, d_ref, o_ref, acc_ref):
        k = pl.program_id(1)
        p = _dot(h_ref[...], d_ref[...].astype(BF16), 0, 0)

        @pl.when(k == 0)
        def _():
            acc_ref[...] = p

        @pl.when(k > 0)
        def _():
            acc_ref[...] += p

        @pl.when(k == nk - 1)
        def _():
            o_ref[...] = (0.5 * acc_ref[...]).astype(BF16).reshape(4, FFN_SHARD, D)

    gw = _call(name + "_dw2", dw2_body, (nf, nk),
               [(abh, (None, tk, FFN_TF), lambda j, k: (2, k, j)), (dxo, (tk, D), lambda j, k: (k, 0))],
               [(gshape, BF16, (4, FFN_SHARD, D), lambda j, k: (j, 2, 0))],
               scratch=[pltpu.VMEM((FFN_TF, D), F32)], sem=("parallel", "arbitrary"))[0]

    def dw13_body(gw_ref, dab_ref, n_ref, o_ref):
        o_ref[...] = _dot(dab_ref[...], n_ref[...], 0, 0).astype(BF16).reshape(4, FFN_SHARD, D)

    gw = pl.pallas_call(
        dw13_body,
        out_shape=jax.ShapeDtypeStruct(gshape, BF16),
        grid=(2, nf),
        in_specs=[pl.BlockSpec(memory_space=pl.ANY),
                  pl.BlockSpec((None, S, FFN_TF), lambda w, j: (w, 0, j)),
                  pl.BlockSpec((S, D), lambda w, j: (0, 0))],
        out_specs=pl.BlockSpec((4, FFN_SHARD, D), lambda w, j: (j, w, 0)),
        input_output_aliases={0: 0},
        name=name + "_dw13",
        compiler_params=pltpu.CompilerParams(dimension_semantics=("parallel", "parallel"),
                                             vmem_limit_bytes=VMEM_LIMIT),
    )(gw, dab, n)
    return gw


def ffn_bwd_input(dab, G, x_in, g, dxo, name):
    _, S, F = dab.shape
    D = x_in.shape[1]
    tm = 256

    def dn_body(dab_ref, w1_ref, w3_ref, x_ref, d_ref, g_ref, dx_ref, dg_ref):
        i = pl.program_id(0)
        dn = _dot(dab_ref[0], w1_ref[...].reshape(F, D)) + _dot(dab_ref[1], w3_ref[...].reshape(F, D))
        dx, dgt = _rms_bwd_tile(dn, x_ref[...], g_ref[...])
        dx_ref[...] = d_ref[...] + dx
        dgp = jnp.sum(dgt, axis=0, keepdims=True)

        @pl.when(i == 0)
        def _():
            dg_ref[...] = dgp

        @pl.when(i > 0)
        def _():
            dg_ref[...] += dgp

    tile = ((tm, D), lambda i: (i, 0))
    return _call(name + "_dn", dn_body, (S // tm,),
                 [(dab, (2, tm, F), lambda i: (0, i, 0)),
                  _ffn_whole_w_spec(G, 0), _ffn_whole_w_spec(G, 1),
                  (x_in,) + tile, (dxo,) + tile, (g, (1, D), lambda i: (0, 0))],
                 [((S, D), F32) + tile, ((1, D), F32, (1, D), lambda i: (0, 0))],
                 sem=("arbitrary",))


PROJ_TN = 512
DH_SHARDS_PER_STEP = 4


def in_proj(h, Gm, name):
    S, D = h.shape
    n_tiles = N_DEV * Gm.shape[2] // PROJ_TN

    def body(h_ref, w_ref, o_ref):
        o_ref[...] = _dot(h_ref[...], w_ref[...]).astype(BF16)

    return _call(name, body, (n_tiles,),
                 [(h, (S, D), lambda j: (0, 0)),
                  (Gm, (None, D, PROJ_TN), lambda j: (j // 2, 0, j % 2))],
                 [((S, n_tiles * PROJ_TN), BF16, (S, PROJ_TN), lambda j: (0, j))],
                 sem=("parallel",))[0]


def _dproj_pieces(dqkv, dq_b, dkv_b, dgate):
    pieces = [(dqkv[g], [(3 * which + g, (which, 0)) for which in range(3)]) for g in range(3)]
    pieces.append((dq_b, [(9, (None, 0)), (10, (None, 1))]))
    pieces.append((dkv_b, [(11, (None, 0))]))
    pieces.append((dgate, [(12 + 2 * a + b, (a, b)) for a in range(2) for b in range(2)]))
    return pieces


def in_proj_bwd_dw(pieces, h, gm_grads, name):
    S, D = h.shape
    steps = [(n, t, ix) for n, (_, tiles) in enumerate(pieces) for t, ix in tiles]
    n_steps = len(steps)

    def pick(table, j):
        out = table[-1]
        for k in range(len(table) - 2, -1, -1):
            out = jnp.where(j == k, table[k], out)
        return out

    def piece_spec(n, arr):
        own = [k for k, (m, _, _) in enumerate(steps) if m == n]
        at = [steps[min(max(k, own[0]), own[-1])][2] for k in range(n_steps)]
        lead, colb = [ix[0] for ix in at], [ix[1] for ix in at]
        if arr.ndim == 3:
            return (own[0], own[-1]), pl.BlockSpec((None, S, PROJ_TN), lambda j: (pick(lead, j), 0, pick(colb, j)))
        return (own[0], own[-1]), pl.BlockSpec((S, PROJ_TN), lambda j: (0, pick(colb, j)))

    spans, d_specs = zip(*[piece_spec(n, arr) for n, (arr, _) in enumerate(pieces)])
    w_tile = [t for _, t, _ in steps]

    def dw_body(gm_ref, h_ref, *refs):
        o_ref = refs[-1]
        j = pl.program_id(0)
        for d_ref, (first, last) in zip(refs[:-1], spans):
            @pl.when((j >= first) & (j <= last))
            def _(d_ref=d_ref):
                o_ref[...] = _dot(h_ref[...], d_ref[...], 0, 0).astype(BF16)

    return pl.pallas_call(
        dw_body,
        out_shape=jax.ShapeDtypeStruct(gm_grads.shape, BF16),
        grid=(n_steps,),
        in_specs=[pl.BlockSpec(memory_space=pl.ANY),
                  pl.BlockSpec((S, D), lambda j: (0, 0), pipeline_mode=pl.Buffered(1))] + list(d_specs),
        out_specs=pl.BlockSpec((None, D, PROJ_TN), lambda j: (pick(w_tile, j) // 2, 0, pick(w_tile, j) % 2)),
        input_output_aliases={0: 0},
        name=name + "_dw",
        compiler_params=pltpu.CompilerParams(dimension_semantics=("arbitrary",), vmem_limit_bytes=VMEM_LIMIT),
    )(gm_grads, h, *[arr for arr, _ in pieces])


def in_proj_bwd_dh(pieces, Gm, x_in, g, dres, name):
    S, D = x_in.shape
    tm = 256
    C = Gm.shape[2]
    n_sh = N_DEV
    n_p = len(pieces)

    def dh_body(*refs):
        d_refs = refs[:n_p]
        w_ref, x_ref, r_ref, g_ref, dx_ref, dxb_ref, dg_ref = refs[n_p:]
        i = pl.program_id(0)
        p = None
        for d_ref, (arr, tiles) in zip(d_refs, pieces):
            for t, (lead, colb) in tiles:
                cols = slice(colb * PROJ_TN, (colb + 1) * PROJ_TN)
                d = d_ref[:, cols] if lead is None else d_ref[lead, :, cols]
                wcol = (t % 2) * PROJ_TN
                term = _dot(d, w_ref[t // 2, :, wcol:wcol + PROJ_TN], 1, 1)
                p = term if p is None else p + term
        dx, dgt = _rms_bwd_tile(p, x_ref[...], g_ref[...])
        dx = r_ref[...] + dx
        dx_ref[...] = dx
        dxb_ref[...] = dx.astype(BF16)
        dgp = jnp.sum(dgt, axis=0, keepdims=True)

        @pl.when(i == 0)
        def _():
            dg_ref[...] = dgp

        @pl.when(i > 0)
        def _():
            dg_ref[...] += dgp

    tile = ((tm, D), lambda i: (i, 0))

    def rows_of(arr):
        if arr.ndim == 3:
            return (arr, (arr.shape[0], tm, arr.shape[2]), lambda i: (0, i, 0))
        return (arr, (tm, arr.shape[1]), lambda i: (i, 0))

    return _call(name + "_dh", dh_body, (S // tm,),
                 [rows_of(arr) for arr, _ in pieces]
                 + [(Gm, (n_sh, D, C), lambda i: (0, 0, 0), pl.Buffered(1)),
                    (x_in,) + tile, (dres,) + tile, (g, (1, D), lambda i: (0, 0))],
                 [((S, D), F32) + tile, ((S, D), BF16) + tile, ((1, D), F32, (1, D), lambda i: (0, 0))],
                 sem=("arbitrary",))


def _t5_bucket(rel):
    n = N_BUCKETS // 2
    max_exact = n // 2
    ret = jnp.where(rel > 0, n, 0)
    a = jnp.abs(rel)
    af = jnp.maximum(a, 1).astype(F32)
    large = max_exact + (jnp.log(af / max_exact) / math.log(MAX_DISTANCE / max_exact)
                         * (n - max_exact)).astype(jnp.int32)
    large = jnp.minimum(large, n - 1)
    return ret + jnp.where(a < max_exact, a, large)


def _bucket_tables():
    qi = jnp.arange(A_TQ, dtype=jnp.int32)[:, None]
    kj = jnp.arange(A_WIN, dtype=jnp.int32)[None, :]
    rel = kj - HALF_WINDOW - qi
    return jnp.stack([_t5_bucket(rel * d) for d in DILATIONS], axis=0)


def bias_build(rel_bias, buckets):
    def body(tab_ref, bk_ref, o_ref):
        col = pl.program_id(0) * HEADS_PER_GROUP_A + pl.program_id(1)
        bk = bk_ref[...]
        acc = jnp.zeros(bk.shape, F32)
        for b in range(N_BUCKETS):
            acc = jnp.where(bk == b, tab_ref[b, col], acc)
        qi = lax.broadcasted_iota(jnp.int32, bk.shape, 0)
        kj = lax.broadcasted_iota(jnp.int32, bk.shape, 1)
        band = jnp.where(jnp.abs(kj - HALF_WINDOW - qi) <= HALF_WINDOW, acc, NEG_INF)
        o_ref[0] = jnp.where(kj >= HALF_WINDOW, band, NEG_INF)
        o_ref[1] = band
        o_ref[2] = jnp.where(kj < A_TQ + HALF_WINDOW, band, NEG_INF)

    out = pl.pallas_call(
        body,
        out_shape=jax.ShapeDtypeStruct((3, HEADS_PER_GROUP_A // 2, 3, 2, A_TQ, A_WIN), F32),
        grid=(3, HEADS_PER_GROUP_A),
        in_specs=[pl.BlockSpec(memory_space=pltpu.SMEM),
                  pl.BlockSpec((None, A_TQ, A_WIN), lambda g, h: (g, 0, 0))],
        out_specs=pl.BlockSpec((None, None, 3, None, A_TQ, A_WIN), lambda g, h: (g, h // 2, 0, h % 2, 0, 0)),
        name="a_bias_build",
        compiler_params=pltpu.CompilerParams(dimension_semantics=("parallel", "parallel")),
    )(rel_bias, buckets)
    return out.reshape(3, HEADS_PER_GROUP_A // 2, 3, 2 * A_TQ, A_WIN)


def bias_bwd(dbias, buckets):
    def body(d_ref, bk_ref, o_ref):
        bk = bk_ref[...]
        for b in range(N_BUCKETS):
            mask = bk == b
            for h in range(HEADS_PER_GROUP_A):
                part = jnp.sum(jnp.where(mask, d_ref[h], 0.0), axis=1, keepdims=True)
                o_ref[h, b:b + 1, :] = jnp.broadcast_to(jnp.sum(part, axis=0, keepdims=True), (1, LANES))

    out = pl.pallas_call(
        body,
        out_shape=jax.ShapeDtypeStruct((3, HEADS_PER_GROUP_A, N_BUCKETS, LANES), F32),
        grid=(3,),
        in_specs=[pl.BlockSpec((None, HEADS_PER_GROUP_A, A_TQ, A_WIN), lambda g: (g, 0, 0, 0)),
                  pl.BlockSpec((None, A_TQ, A_WIN), lambda g: (g, 0, 0))],
        out_specs=pl.BlockSpec((None, HEADS_PER_GROUP_A, N_BUCKETS, LANES), lambda g: (g, 0, 0, 0)),
        name="a_bias_bwd",
        compiler_params=pltpu.CompilerParams(dimension_semantics=("parallel",)),
    )(dbias, buckets)
    return out[:, :, :, 0].transpose(2, 0, 1).reshape(N_BUCKETS, 3 * HEADS_PER_GROUP_A)


def _a_fill_padded(pad_ref, src_ref, n, pad):
    zeros = jnp.zeros((pad, LANES), pad_ref.dtype)
    pad_ref[0:pad, :] = zeros
    pad_ref[pad + n:2 * pad + n, :] = zeros
    pad_ref[pad:pad + n, :] = src_ref[...].astype(pad_ref.dtype)


def _a_stack_heads(x, lane):
    zero = jnp.zeros_like(x)
    return jnp.concatenate([jnp.where(lane < HEAD_DIM_A, x, zero), jnp.where(lane >= HEAD_DIM_A, x, zero)], axis=0)


def _a_bias_variant(qb, nqb):
    return jnp.where(qb == 0, 0, jnp.where(qb == nqb - 1, 2, 1))


def _a_slab_specs(proj, g):
    S = proj.shape[0]
    per = GROUP_WIDTH_A // LANES
    return [(proj, (S, LANES), lambda hp, w=w: (0, per * (3 * w + g) + hp)) for w in range(3)]


def a_fwd(proj, bias, g, name, others=None):
    S = proj.shape[0]
    d = DILATIONS[g]
    L = S // d
    nqb = L // A_TQ
    pad = HALF_WINDOW * d
    n_others = 0 if others is None else 4
    tr = 256

    def body(q_ref, k_ref, v_ref, b_ref, *refs):
        out1, out2, qf, kpad, vpad = refs[n_others:n_others + 5]
        o_ref, l_ref = refs[n_others + 5:] if others else (out1, out2)
        qf[...] = q_ref[...].astype(F32) * A_SCALE
        _a_fill_padded(kpad, k_ref, S, pad)
        _a_fill_padded(vpad, v_ref, S, pad)
        lane = lax.broadcasted_iota(jnp.int32, (A_TQ, LANES), 1)

        def block(t, carry):
            qb, r = t // d, t % d
            start = qb * (A_TQ * d) + r
            kw = kpad[pl.ds(start, A_WIN, stride=d), :].astype(BF16)
            vw = vpad[pl.ds(start, A_WIN, stride=d), :].astype(BF16)
            q = qf[pl.ds(start, A_TQ, stride=d), :].astype(BF16)
            q2 = _a_stack_heads(q, lane)
            s = _dot(q2, kw, 1, 1) + b_ref[_a_bias_variant(qb, nqb)]
            m = jnp.max(s, axis=-1, keepdims=True)
            e = jnp.exp(s - m)
            l = jnp.sum(e, axis=-1, keepdims=True)
            o2 = _dot(e.astype(BF16), vw) / l
            lse2 = m + jnp.log(l)
            o_ref[pl.ds(start, A_TQ, stride=d), :] = jnp.where(lane < HEAD_DIM_A, o2[0:A_TQ], o2[A_TQ:])
            l_ref[pl.ds(start, A_TQ, stride=d), :] = jnp.where(lane < HEAD_DIM_A, lse2[0:A_TQ], lse2[A_TQ:])
            return carry

        lax.fori_loop(0, nqb * d, block, 0, unroll=A_UNROLL)

        if others:
            o0, o1, l0, l1 = refs[:n_others]

            def combine(c, carry):
                rows = pl.ds(pl.multiple_of(c * tr, tr), tr)
                la, lb, lc = l0[rows, :], l1[rows, :], l_ref[rows, :]
                m = jnp.maximum(jnp.maximum(la, lb), lc)
                ea, eb, ec = jnp.exp(la - m), jnp.exp(lb - m), jnp.exp(lc - m)
                z = ea + eb + ec
                out1[rows, :] = ((ea * o0[rows, :] + eb * o1[rows, :] + ec * o_ref[rows, :]) / z).astype(BF16)
                out2[rows, :] = m + jnp.log(z)
                return carry

            lax.fori_loop(0, S // tr, combine, 0)

    slab = ((S, LANES), lambda hp: (0, hp))
    wide = (S, GROUP_WIDTH_A)
    other_ins = [(a,) + slab for a in (*others[0], *others[1])] if others else []
    return _call(name, body, (4,),
                 _a_slab_specs(proj, g)
                 + [(bias, (None, None, 3, 2 * A_TQ, A_WIN), lambda hp: (g, hp, 0, 0, 0))] + other_ins,
                 [(wide, BF16 if others else F32) + slab, (wide, F32) + slab],
                 scratch=[pltpu.VMEM((S, LANES), F32)] + [pltpu.VMEM((S + 2 * pad, LANES), F32)] * 2
                 + ([pltpu.VMEM((S, LANES), F32)] * 2 if others else []),
                 sem=("parallel",))


def a_bwd(proj, bias, do_a, o_a, lse_tot, g, name):
    S = proj.shape[0]
    d = DILATIONS[g]
    L = S // d
    nqb = L // A_TQ
    pad = HALF_WINDOW * d

    def body(q_ref, k_ref, v_ref, b_ref, do_ref, o_ref, l_ref, dqkv_ref, db_ref,
             qf, of, dqf, kpad, vpad, dkacc, dvacc):
        qf[...] = q_ref[...].astype(F32) * A_SCALE
        of[...] = o_ref[...].astype(F32)
        _a_fill_padded(kpad, k_ref, S, pad)
        _a_fill_padded(vpad, v_ref, S, pad)
        dkacc[...] = jnp.zeros(dkacc.shape, F32)
        dvacc[...] = jnp.zeros(dvacc.shape, F32)
        db_ref[...] = jnp.zeros(db_ref.shape, F32)
        lane = lax.broadcasted_iota(jnp.int32, (A_TQ, LANES), 1)

        def block(t, carry):
            qb, r = t // d, t % d
            start = qb * (A_TQ * d) + r
            rows = pl.ds(start, A_TQ, stride=d)
            win = pl.ds(start, A_WIN, stride=d)
            kw = kpad[win, :].astype(BF16)
            vw = vpad[win, :].astype(BF16)
            q = qf[rows, :].astype(BF16)
            do = do_ref[rows, :]
            ov = of[rows, :]
            lt = l_ref[rows, :]
            q2 = _a_stack_heads(q, lane)
            do2 = _a_stack_heads(do, lane)
            lt2 = jnp.concatenate([lt[:, 0:1], lt[:, HEAD_DIM_A:HEAD_DIM_A + 1]], axis=0)
            s = _dot(q2, kw, 1, 1) + b_ref[_a_bias_variant(qb, nqb)]
            p = jnp.exp(s - lt2)
            t = jnp.sum(do2 * jnp.concatenate([ov, ov], axis=0), axis=-1, keepdims=True)
            dob2 = do2.astype(BF16)
            ds = p * (_dot(dob2, vw, 1, 1) - t)
            db_ref[...] += ds
            dsb = ds.astype(BF16)
            dq2 = _dot(dsb, kw)
            dqf[rows, :] = jnp.where(lane < HEAD_DIM_A, dq2[0:A_TQ], dq2[A_TQ:]) * A_SCALE
            dkacc[win, :] += _dot(dsb, q2, 0, 0)
            dvacc[win, :] += _dot(p.astype(BF16), dob2, 0, 0)
            return carry

        lax.fori_loop(0, nqb * d, block, 0, unroll=A_UNROLL)
        dqkv_ref[0] = dqf[...].astype(BF16)
        dqkv_ref[1] = dkacc[pad:pad + S, :].astype(BF16)
        dqkv_ref[2] = dvacc[pad:pad + S, :].astype(BF16)

    slab = ((S, LANES), lambda hp: (0, hp))
    padded = pltpu.VMEM((S + 2 * pad, LANES), F32)
    return _call(
        name, body, (4,),
        _a_slab_specs(proj, g)
        + [(bias, (None, None, 3, 2 * A_TQ, A_WIN), lambda hp: (g, hp, 0, 0, 0)),
           (do_a,) + slab, (o_a,) + slab, (lse_tot,) + slab],
        [((3, S, GROUP_WIDTH_A), BF16, (3, S, LANES), lambda hp: (0, 0, hp)),
         ((4, 2 * A_TQ, A_WIN), F32, (None, 2 * A_TQ, A_WIN), lambda hp: (hp, 0, 0))],
        scratch=[pltpu.VMEM((S, LANES), F32)] * 3 + [padded] * 4,
        sem=("parallel",))


def _rope_tables(S):
    rows = S // GRID_W
    row = jnp.repeat(jnp.arange(rows, dtype=F32), GRID_W)
    col = jnp.tile(jnp.arange(GRID_W, dtype=F32), rows)
    n_freq = HEAD_DIM_B // 4
    freq = ROPE_THETA ** (-jnp.arange(n_freq, dtype=F32) / n_freq)
    ang = jnp.concatenate([row[:, None] * freq, col[:, None] * freq], axis=-1)
    cos, sin = jnp.cos(ang), jnp.sin(ang)
    return jnp.repeat(cos, 2, axis=-1), jnp.stack([-sin, sin], axis=-1).reshape(S, HEAD_DIM_B)


def _swap_pairs(y):
    lane = lax.broadcasted_iota(jnp.int32, y.shape, 1)
    return jnp.where(lane % 2 == 0, pltpu.roll(y, LANES - 1, 1), pltpu.roll(y, 1, 1))


def qkv_prep(proj, gains, cos_t, sin_t, name):
    S = proj.shape[0]
    ts = 256
    n_rot = N_HEADS_B + N_KV_B
    nh = n_rot + N_KV_B
    W = nh * LANES

    def body(x_ref, g_ref, c_ref, s_ref, o_ref):
        cv, sv = c_ref[...], s_ref[...]
        for hb in range(nh):
            cols = slice(hb * LANES, (hb + 1) * LANES)
            if hb < n_rot:
                xv = x_ref[:, cols].astype(F32)
                r = lax.rsqrt(jnp.mean(xv * xv, axis=-1, keepdims=True) + EPS)
                yv = xv * r * g_ref[:, cols]
                o_ref[:, cols] = (yv * cv + _swap_pairs(yv) * sv).astype(BF16)
            else:
                o_ref[:, cols] = x_ref[:, cols]

    return _call(name, body, (S // ts,),
                 [(proj, (ts, W), lambda i: (i, A_QKV_WIDTH // W)), (gains, (1, W), lambda i: (0, 0)),
                  (cos_t, (ts, LANES), lambda i: (i, 0)), (sin_t, (ts, LANES), lambda i: (i, 0))],
                 [((S, W), BF16, (ts, W), lambda i: (i, 0))],
                 sem=("parallel",))[0]


def qk_prep_bwd(dr, proj, col0, gain, cos_t, sin_t, name):
    S, W = dr.shape
    H = W // LANES
    ts = 256
    wx = math.gcd(W, col0)
    n_x = W // wx

    def body(d_ref, *refs):
        x_refs = refs[:n_x]
        g_ref, c_ref, s_ref, dx_ref, dg_ref = refs[n_x:]
        i = pl.program_id(0)
        cv, sv, gv = c_ref[...], s_ref[...], g_ref[...]
        dgp = jnp.zeros((1, LANES), F32)
        for hb in range(H):
            cols = slice(hb * LANES, (hb + 1) * LANES)
            xc = (hb * LANES) % wx
            xv = x_refs[(hb * LANES) // wx][:, xc:xc + LANES].astype(F32)
            dout = d_ref[:, cols]
            dy = dout * cv + _swap_pairs(dout * sv)
            dx, dgt = _rms_bwd_tile(dy, xv, gv)
            dx_ref[:, cols] = dx.astype(BF16)
            dgp = dgp + jnp.sum(dgt, axis=0, keepdims=True)

        @pl.when(i == 0)
        def _():
            dg_ref[...] = dgp

        @pl.when(i > 0)
        def _():
            dg_ref[...] += dgp

    return _call(name, body, (S // ts,),
                 [(dr, (ts, W), lambda i: (i, 0))]
                 + [(proj, (ts, wx), lambda i, k=k: (i, col0 // wx + k)) for k in range(n_x)]
                 + [(gain, (1, LANES), lambda i: (0, 0)),
                  (cos_t, (ts, LANES), lambda i: (i, 0)), (sin_t, (ts, LANES), lambda i: (i, 0))],
                 [((S, W), BF16, (ts, W), lambda i: (i, 0)),
                  ((1, LANES), F32, (1, LANES), lambda i: (0, 0))],
                 sem=("arbitrary",))


def _row_sums(x):
    hi = x.astype(BF16)
    lo = (x - hi.astype(F32)).astype(BF16)
    ones = jnp.ones((8, LANES), BF16)
    return (_dot(ones, hi, 1, 1) + _dot(ones, lo, 1, 1))[0:1, :]


def flash_fwd(qkv, name):
    S = qkv.shape[0]
    tq = B_TQ_FWD
    hps = B_HEADS_PER_STEP

    def body(q_ref, k_ref, v_ref, o_ref, l_ref):
        k, v = k_ref[...], v_ref[...]
        for j in range(hps):
            cols = slice(j * LANES, (j + 1) * LANES)
            s = _dot(q_ref[:, cols], k, 1, 1)
            m = jnp.max(s, axis=-1, keepdims=True)
            e = jnp.exp2(s - m)
            l = jnp.sum(e, axis=-1, keepdims=True)
            o_ref[:, cols] = (_dot(e.astype(BF16), v) / l).astype(BF16)
            lse = jnp.broadcast_to(m * (1.0 / LOG2E) + jnp.log(l), (tq, LANES))
            l_ref[j] = _row_sums(lse) * (1.0 / LANES)

    per = GQA_GROUP_B // hps
    heads = lambda g, h, i: (i, g * per + h)
    return _call(name, body, (N_KV_B, per, S // tq),
                 [(qkv, (tq, hps * LANES), heads),
                  (qkv, (S, LANES), lambda g, h, i: (0, N_HEADS_B + g)),
                  (qkv, (S, LANES), lambda g, h, i: (0, N_HEADS_B + N_KV_B + g))],
                 [((S, N_HEADS_B * LANES), BF16, (tq, hps * LANES), heads),
                  ((N_HEADS_B, 1, S), F32, (hps, 1, tq), lambda g, h, i: (g * per + h, 0, i))],
                 sem=("parallel", "parallel", "parallel"))


def flash_bwd(qkv, k_t, do_b, o_b, lse, name):
    S = qkv.shape[0]
    tq = B_TQ_BWD
    nq = S // tq
    scale = HEAD_DIM_B ** -0.5

    def body(q_ref, k_ref, v_ref, kt_ref, do_ref, o_ref, l_ref, dq_ref, dk_ref, dv_ref, dkacc, dvacc):
        h, i = pl.program_id(1), pl.program_id(2)

        @pl.when((h == 0) & (i == 0))
        def _():
            dkacc[...] = jnp.zeros(dkacc.shape, F32)
            dvacc[...] = jnp.zeros(dvacc.shape, F32)

        q = q_ref[...]
        dob = do_ref[...]
        t = _row_sums(dob.astype(F32) * o_ref[...].astype(F32))
        pt = jnp.exp2(_dot(k_ref[...], q, 1, 1) - l_ref[...] * LOG2E)
        dsb = (pt * (_dot(v_ref[...], dob, 1, 1) - t)).astype(BF16)
        dvacc[...] += _dot(pt.astype(BF16), dob)
        dkacc[...] += _dot(dsb, q)
        dq_ref[...] = _dot(kt_ref[...], dsb).T * scale

        @pl.when((h == GQA_GROUP_B - 1) & (i == nq - 1))
        def _():
            dk_ref[...] = dkacc[...] * (scale / B_Q_PRESCALE)
            dv_ref[...] = dvacc[...].astype(BF16)

    head = lambda g, h, i: (i, g * GQA_GROUP_B + h)
    return _call(name, body, (N_KV_B, GQA_GROUP_B, nq),
                 [(qkv, (tq, LANES), head),
                  (qkv, (S, LANES), lambda g, h, i: (0, N_HEADS_B + g)),
                  (qkv, (S, LANES), lambda g, h, i: (0, N_HEADS_B + N_KV_B + g)),
                  (k_t, (LANES, S), lambda g, h, i: (g, 0)),
                  (do_b, (tq, LANES), head), (o_b, (tq, LANES), head),
                  (lse, (None, 1, tq), lambda g, h, i: (g * GQA_GROUP_B + h, 0, i))],
                 [((S, N_HEADS_B * LANES), F32, (tq, LANES), head),
                  ((S, N_KV_B * LANES), F32, (S, LANES), lambda g, h, i: (0, g)),
                  ((S, N_KV_B * LANES), BF16, (S, LANES), lambda g, h, i: (0, g))],
                 scratch=[pltpu.VMEM((S, LANES), F32)] * 2,
                 sem=("parallel", "arbitrary", "arbitrary"))


MERGE_TN = 512


def _mix_rows_spec(Gm, row0, n_slots, slot_map, cols=None, col_map=None):
    C = Gm.shape[2] if cols is None else cols
    cm = (lambda *idx: 0) if col_map is None else col_map
    return (Gm, (n_slots, LANES, C), lambda *idx: (slot_map(*idx), row0 // LANES, cm(*idx)))


def _gate_specs(proj, tm):
    first = (A_QKV_WIDTH + PB_GATE_A) // MERGE_TN
    return [(proj, (tm, MERGE_TN), lambda i, k=k: (i, first + k)) for k in range(4)]


def _whole_rows_spec(Gm, row0):
    return _mix_rows_spec(Gm, row0, N_DEV, lambda *idx: 0)


def merge_fwd(o_a, o_b, w_a, Gm, proj, b_gate, x, name):
    S, D = x.shape
    tm = 256

    def body(oa_ref, ob_ref, wa_ref, wb_ref, wo_ref, g0, g1, g2, g3, bg_ref, x_ref, m_ref, ya_ref, yb_ref, xo_ref):
        ya = _dot(oa_ref[...], wa_ref[...])
        yb = _dot(ob_ref[...], wb_ref[...].reshape(N_DEV * LANES, D))
        ga = _sigmoid(jnp.concatenate([g0[...], g1[...]], axis=1).astype(F32) + bg_ref[:, 0:D])
        gb = _sigmoid(jnp.concatenate([g2[...], g3[...]], axis=1).astype(F32) + bg_ref[:, D:2 * D])
        merged = (ga * ya + gb * yb).astype(BF16)
        m_ref[...] = merged
        ya_ref[...] = ya.astype(BF16)
        yb_ref[...] = yb.astype(BF16)
        xo_ref[...] = x_ref[...] + _dot(merged, wo_ref[...].reshape(N_DEV * LANES, D))

    rows = lambda a: (a, (tm, a.shape[1]), lambda i: (i, 0))
    out = ((S, D), BF16, (tm, D), lambda i: (i, 0))
    return _call(name, body, (S // tm,),
                 [rows(o_a), rows(o_b), (w_a, w_a.shape, lambda i: (0, 0)),
                  _whole_rows_spec(Gm, REST_WB), _whole_rows_spec(Gm, REST_WOUT)]
                 + _gate_specs(proj, tm) + [(b_gate, (1, 2 * D), lambda i: (0, 0)), rows(x)],
                 [out, out, out, ((S, D), F32, (tm, D), lambda i: (i, 0))], sem=("parallel",))


def merge_bwd(dx2, w_a, Gm, ya, yb, proj, b_gate, name):
    S, D = dx2.shape
    tm = 256

    def body(d_ref, wo_ref, wa_ref, wb_ref, ya_ref, yb_ref, g0, g1, g2, g3, bg_ref,
             dya_ref, dyb_ref, dg_ref, dbg_ref, doa_ref, dob_ref):
        i = pl.program_id(0)
        dm = _dot(d_ref[...].astype(BF16), wo_ref[...].reshape(N_DEV * LANES, D), 1, 1)
        ga = _sigmoid(jnp.concatenate([g0[...], g1[...]], axis=1).astype(F32) + bg_ref[:, 0:D])
        gb = _sigmoid(jnp.concatenate([g2[...], g3[...]], axis=1).astype(F32) + bg_ref[:, D:2 * D])
        dya = (dm * ga).astype(BF16)
        dyb = (dm * gb).astype(BF16)
        dya_ref[...] = dya
        dyb_ref[...] = dyb
        dpa = dm * ya_ref[...].astype(F32) * ga * (1.0 - ga)
        dpb = dm * yb_ref[...].astype(F32) * gb * (1.0 - gb)
        dg_ref[0] = dpa.astype(BF16)
        dg_ref[1] = dpb.astype(BF16)
        doa_ref[...] = _dot(dya, wa_ref[...], 1, 1)
        dob_ref[...] = _dot(dyb, wb_ref[...].reshape(N_DEV * LANES, D), 1, 1).astype(BF16)
        sa =jnp.sum(dpa, axis=0, keepdims=True)
        sb = jnp.sum(dpb, axis=0, keepdims=True)

        @pl.when(i == 0)
        def _():
            dbg_ref[0] = sa
            dbg_ref[1] = sb

        @pl.when(i > 0)
        def _():
            dbg_ref[0] += sa
            dbg_ref[1] += sb

    tile = ((tm, D), lambda i: (i, 0))
    return _call(
        name, body, (S // tm,),
        [(dx2,) + tile, _whole_rows_spec(Gm, REST_WOUT), (w_a, w_a.shape, lambda i: (0, 0)),
         _whole_rows_spec(Gm, REST_WB), (ya,) + tile, (yb,) + tile]
        + _gate_specs(proj, tm) + [(b_gate, (1, 2 * D), lambda i: (0, 0))],
        [((S, D), BF16) + tile, ((S, D), BF16) + tile,
         ((2, S, D), BF16, (2, tm, D), lambda i: (0, i, 0)),
         ((2, 1, D), F32, (2, 1, D), lambda i: (0, 0, 0)),
         ((S, w_a.shape[0]), F32, (tm, w_a.shape[0]), lambda i: (i, 0)),
         ((S, N_HEADS_B * LANES), BF16, (tm, N_HEADS_B * LANES), lambda i: (i, 0))],
        sem=("arbitrary",))


def weight_grad_rows(a, b, grads, row0, name):
    S, M = a.shape
    N = b.shape[1]
    tmm = 512
    tk = WGRAD_TK
    nk = S // tk
    prior = [] if grads is None else [grads]

    def body(*refs):
        a_ref, b_ref, o_ref, acc_ref = refs[len(prior):]
        k = pl.program_id(1)
        p = _dot(a_ref[...], b_ref[...].astype(BF16), 0, 0)

        @pl.when(k == 0)
        def _():
            acc_ref[...] = p

        @pl.when(k > 0)
        def _():
            acc_ref[...] += p

        @pl.when(k == nk - 1)
        def _():
            o_ref[...] = acc_ref[...].astype(BF16).reshape(tmm // LANES, LANES, N)

    return pl.pallas_call(
        body,
        out_shape=jax.ShapeDtypeStruct((N_DEV, MIX_ROWS, N), BF16),
        grid=(M // tmm, nk),
        in_specs=[pl.BlockSpec(memory_space=pl.ANY)] * len(prior)
        + [pl.BlockSpec((tk, tmm), lambda j, k: (k, j)),
           pl.BlockSpec((tk, N), lambda j, k: (k, 0))],
        out_specs=pl.BlockSpec((tmm // LANES, LANES, N), lambda j, k: (j, row0 // LANES, 0)),
        scratch_shapes=[pltpu.VMEM((tmm, N), F32)],
        input_output_aliases={0: 0} if prior else {},
        name=name,
        compiler_params=pltpu.CompilerParams(dimension_semantics=("parallel", "arbitrary"),
                                             vmem_limit_bytes=VMEM_LIMIT),
    )(*prior, a, b)


def weight_grad_plain(a, b, name):
    S, M = a.shape
    N = b.shape[1]
    tk = WGRAD_TK
    nk = S // tk

    def body(a_ref, b_ref, o_ref, acc_ref):
        k = pl.program_id(0)
        p = _dot(a_ref[...], b_ref[...], 0, 0)

        @pl.when(k == 0)
        def _():
            acc_ref[...] = p

        @pl.when(k > 0)
        def _():
            acc_ref[...] += p

        @pl.when(k == nk - 1)
        def _():
            o_ref[...] = acc_ref[...].astype(BF16)

    return _call(name, body, (nk,),
                 [(a, (tk, M), lambda k: (k, 0)), (b, (tk, N), lambda k: (k, 0))],
                 [((M, N), BF16, (M, N), lambda k: (0, 0))],
                 scratch=[pltpu.VMEM((M, N), F32)], sem=("arbitrary",))[0]


def local_step(x, tgt, p, get_g1_up, get_g1_down, get_gm_in, get_gm_rest, get_g2, emit, start_token):
    S, D = x.shape
    after = lambda t: t[0:1, 0:1]
    buckets = _bucket_tables()
    cos_t, sin_t = _rope_tables(S)
    gains = jnp.concatenate([jnp.tile(p["q_norm"] * B_Q_PRESCALE, (1, N_HEADS_B)), jnp.tile(p["k_norm"], (1, N_KV_B)),
                             jnp.ones((1, N_KV_B * LANES), F32)], axis=1)

    n1 = rms_fwd(x, p["ffn1_norm"] + after(start_token), "ffn1_norm")
    bias = bias_build(p["rel_bias"] + after(start_token), buckets)
    g1_up = get_g1_up((n1, bias))
    ab1 = ffn_up(n1, (g1_up, None), "ffn1_up")
    G1 = (g1_up, get_g1_down(ab1))
    x1, hm = ffn_down(ab1, G1, x, p["mix_norm"], "ffn1_down")
    Gw = get_gm_in(hm)
    proj = in_proj(hm, Gw, "in_proj")

    outs, lses = zip(*[a_fwd(proj, bias, g, "a_fwd_%d" % g) for g in range(2)])
    o_a, lse_tot = a_fwd(proj, bias, 2, "a_fwd_2", (outs, lses))

    qkv = qkv_prep(proj, gains, cos_t, sin_t, "qkv_prep")
    k_t = qkv[:, N_HEADS_B * LANES:(N_HEADS_B + N_KV_B) * LANES].T
    o_b, lse_b = flash_fwd(qkv, "flash_fwd")

    Gm = get_gm_rest(o_b)
    w_a = Gm[:, REST_WA:REST_ROWS, :].reshape(N_DEV, GROUP_WIDTH_A, LANES).transpose(1, 0, 2).reshape(GROUP_WIDTH_A, D)
    merged, ya, yb, x2 = merge_fwd(o_a, o_b, w_a, Gm, proj, p["b_gate"], x1, "merge_fwd")

    G2 = get_g2(x2)
    n2, ab2, dx3_b, dab2, dx2, dx2_b, d_ffn2_norm, loss, d_final = ffn_last(
        x2, p["ffn2_norm"], G2, tgt, p["final_norm"], "ffn2")
    gw2 = ffn_bwd_weights(dx3_b, ab2, dab2, n2, "ffn2_bwd")
    t2 = emit("ffn2", gw2)

    dya, dyb, dgate, dbg, do_a, do_b = merge_bwd(dx2_b, w_a, Gm, ya, yb, proj, p["b_gate"] + after(t2),
                                                 "merge_bwd")
    gm_grads = weight_grad_rows(merged, dx2_b, None, MIX_WOUT, "dw_out")
    gm_grads = weight_grad_rows(o_b, dyb, gm_grads, MIX_WB, "dw_branch_b")
    dw_a = weight_grad_plain(o_a, dya, "dw_branch_a")

    dq_r, dk_r, dv_b = flash_bwd(qkv, k_t, do_b, o_b, lse_b, "flash_bwd")
    dq_b, d_q_norm = qk_prep_bwd(dq_r, proj, A_QKV_WIDTH, p["q_norm"], cos_t, sin_t, "q_prep_bwd")
    dk_b, d_k_norm = qk_prep_bwd(dk_r, proj, A_QKV_WIDTH + N_HEADS_B * LANES, p["k_norm"], cos_t, sin_t,
                                 "k_prep_bwd")

    dqkv, dbs = [], []
    for g in range(3):
        dg_, db = a_bwd(proj, bias, do_a, o_a, lse_tot, g, "a_bwd_%d" % g)
        dqkv.append(dg_)
        dbs.append(db)
    d_rel_bias = bias_bwd(jnp.stack(dbs, axis=0).reshape(3, HEADS_PER_GROUP_A, A_TQ, A_WIN), buckets)

    dproj = _dproj_pieces(dqkv, dq_b, jnp.concatenate([dk_b, dv_b], axis=1), dgate)
    gm_grads = in_proj_bwd_dw(dproj[:3], hm, gm_grads, "in_proj_bwd_a")
    gm_grads = in_proj_bwd_dw(dproj[3:], hm, gm_grads, "in_proj_bwd_b")
    dw_a_sh = dw_a.reshape(GROUP_WIDTH_A, N_DEV, LANES).transpose(1, 0, 2).reshape(N_DEV, MIX_ROWS - MIX_WA, D)
    gm_grads = lax.dynamic_update_slice(gm_grads, dw_a_sh, (0, MIX_WA, 0))
    tm = emit("mix", gm_grads)
    dx1, dx1_b, d_mix_norm = in_proj_bwd_dh(dproj, Gw, x1, p["mix_norm"] + after(tm), dx2, "in_proj_bwd")

    dab1 = ffn_bwd_hidden(dx1_b, ab1, G1, "ffn1_bwd")
    gw1 = ffn_bwd_weights(dx1_b, ab1, dab1, n1, "ffn1_bwd")
    t1 = emit("ffn1", gw1)
    dx0, d_ffn1_norm = ffn_bwd_input(dab1, G1, x, p["ffn1_norm"] + after(t1), dx1, "ffn1_bwd")

    small = dict(ffn1_norm=d_ffn1_norm, mix_norm=d_mix_norm, b_gate=dbg.reshape(1, 2 * D),
                 q_norm=d_q_norm, k_norm=d_k_norm, rel_bias=d_rel_bias, ffn2_norm=d_ffn2_norm,
                 final_norm=d_final)
    return loss, dx0, small


def _pack_small(t, loss_row):
    row6 = jnp.concatenate([t["q_norm"].reshape(1, -1), t["k_norm"].reshape(1, -1), t["rel_bias"].reshape(1, -1)], axis=1)
    return jnp.concatenate([t["ffn1_norm"].reshape(1, -1), t["mix_norm"].reshape(1, -1), t["b_gate"].reshape(2, -1),
                            t["ffn2_norm"].reshape(1, -1), t["final_norm"].reshape(1, -1), row6, loss_row], axis=0)


def _unpack_small(a, shapes):
    return dict(ffn1_norm=a[0:1].reshape(shapes["ffn1_norm"]), mix_norm=a[1:2].reshape(shapes["mix_norm"]),
                b_gate=a[2:4].reshape(shapes["b_gate"]), ffn2_norm=a[4:5].reshape(shapes["ffn2_norm"]),
                final_norm=a[5].reshape(shapes["final_norm"]), q_norm=a[6:7, 0:128].reshape(shapes["q_norm"]),
                k_norm=a[6:7, 128:256].reshape(shapes["k_norm"]), rel_bias=a[6, 256:1024].reshape(shapes["rel_bias"]))


SMALL = ("ffn1_norm", "mix_norm", "b_gate", "q_norm", "k_norm", "rel_bias", "ffn2_norm", "final_norm")
ORDER = ("ffn1_norm", "ffn1_w1", "ffn1_w3", "ffn1_w2", "mix_norm", "w_in", "b_gate", "q_norm", "k_norm", "rel_bias",
         "w_branch_a", "w_branch_b", "w_out", "ffn2_norm", "ffn2_w1", "ffn2_w3", "ffn2_w2", "final_norm")


def kernel(x, ffn1_norm, ffn1_w1, ffn1_w3, ffn1_w2, mix_norm, w_in, b_gate, q_norm, k_norm, rel_bias, w_branch_a, w_branch_b, w_out, ffn2_norm, ffn2_w1, ffn2_w3, ffn2_w2, final_norm, loss_target, m_ffn1_norm, m_ffn1_w1, m_ffn1_w3, m_ffn1_w2, m_mix_norm, m_w_in, m_b_gate, m_q_norm, m_k_norm, m_rel_bias, m_w_branch_a, m_w_branch_b, m_w_out, m_ffn2_norm, m_ffn2_w1, m_ffn2_w3, m_ffn2_w2, m_final_norm, v_ffn1_norm, v_ffn1_w1, v_ffn1_w3, v_ffn1_w2, v_mix_norm, v_w_in, v_b_gate, v_q_norm, v_k_norm, v_rel_bias, v_w_branch_a, v_w_branch_b, v_w_out, v_ffn2_norm, v_ffn2_w1, v_ffn2_w3, v_ffn2_w2, v_final_norm):
    args = dict(locals())
    w = {n: args[n] for n in ORDER}
    m = {n: args["m_" + n] for n in ORDER}
    v = {n: args["v_" + n] for n in ORDER}
    D = x.shape[2]

    blocks = (
        ("ffn1_up", jnp.concatenate([ffn1_w1[0].T, ffn1_w3[0].T], axis=0)),
        ("ffn1_down", ffn1_w2[0]),
        ("mix_in", w_in[0]),
        ("mix_rest", jnp.concatenate([w_branch_b[0], w_out[0], w_branch_a[0].reshape(REST_ROWS - REST_WA, D)], axis=0)),
        ("ffn2", jnp.concatenate([ffn2_w1[0].T, ffn2_w3[0].T, ffn2_w2[0]], axis=0)),
    )
    direct = ("mix_rest", "ffn2")
    started = all_gather_start_all([(b.astype(BF16), tag in direct) for tag, b in blocks], "all_gather_start")
    gathers = {tag: s for (tag, _), s in zip(blocks, started)}
    start_token = started[0][4]

    def gathered(tag):
        def get(after):
            if tag in direct:
                return all_gather_place_own(*_split_wait("all_gather_" + tag + "_wait", gathers[tag], N_DEV - 1, after),
                                            "all_gather_" + tag + "_own")
            return all_gather_finish(*_split_wait("all_gather_" + tag + "_wait", gathers[tag], 4, after),
                                     "all_gather_" + tag + "_finish")
        return get

    core = lax.axis_index("c").astype(jnp.int32).reshape(1)
    chip = (2 * lax.axis_index("x") + lax.axis_index("y")).astype(jnp.int32).reshape(1)
    device = 2 * chip + core
    exchanges = {}

    def emit(tag, gw):
        if tag == "ffn1":
            (theirs,) = reduce_scatter_pair([gw], "reduce_scatter_pair_" + tag)
            part = pair_add(gw, theirs, core, "pair_add_" + tag)
            exchanges[tag] = reduce_scatter_start(part, "reduce_scatter_" + tag + "_start")
        else:
            exchanges[tag] = reduce_scatter_start_direct(gw, "reduce_scatter_" + tag + "_start")
        return exchanges[tag][4]

    small_p = dict(ffn1_norm=ffn1_norm, mix_norm=mix_norm, b_gate=b_gate, q_norm=q_norm, k_norm=k_norm,
                   rel_bias=rel_bias, ffn2_norm=ffn2_norm, final_norm=final_norm.reshape(1, D))
    loss_p, grad_x, small_g = local_step(x[0], loss_target[0], small_p, gathered("ffn1_up"), gathered("ffn1_down"),
                                         gathered("mix_in"), gathered("mix_rest"), gathered("ffn2"), emit, start_token)

    def landed(tag, after):
        n_others, me = (3, chip) if tag == "ffn1" else (N_DEV - 1, device)
        return tuple(_split_wait("reduce_scatter_" + tag + "_wait", exchanges[tag], n_others, after)) + (me,)

    grads, delta, new_m, new_v = {}, {}, {}, {}

    def finish(n, part, land, me, off, blk, transposed=False):
        shp = w[n].shape
        if transposed:
            to2 = lambda a: a.reshape(shp[-2], shp[-1]).T
            back = lambda a: a.T.reshape(shp)
        else:
            to2 = lambda a: a.reshape(shp[-2], shp[-1])
            back = lambda a: a.reshape(shp)
        res = sum_adamw(part, land, me, off, blk, to2(w[n]), to2(m[n]), to2(v[n]), "update_" + n)
        grads[n], delta[n], new_m[n], new_v[n] = [back(a) for a in res]

    last_token = exchanges["ffn1"][4]
    for tag, after in (("ffn2", last_token), ("ffn1", grad_x)):
        group = landed(tag, after)
        finish(tag + "_w1", *group, 0, FFN_SHARD, transposed=True)
        finish(tag + "_w3", *group, FFN_SHARD, FFN_SHARD, transposed=True)
        finish(tag + "_w2", *group, 2 * FFN_SHARD, FFN_SHARD)
        if tag == "ffn2":
            group_m = landed("mix", last_token)
            finish("w_in", *group_m, MIX_WIN, LANES)
            finish("w_branch_b", *group_m, MIX_WB, LANES)
            finish("w_out", *group_m, MIX_WOUT, LANES)
            grads["w_branch_a"] = sum_landed(*group_m, MIX_WA, MIX_ROWS - MIX_WA, MIX_ROWS - MIX_WA,
                                             "w_branch_a_sum").reshape(w_branch_a.shape)
    loss_row = jnp.pad(loss_p, ((0, 0), (0, D - LANES)))
    smalls = small_all_gather(_pack_small(small_g, loss_row), new_v["w_in"])
    small_sum = sum_slots(smalls, 0, N_DEV, N_DEV, "small_sum")
    small_shapes = {n: w[n].shape for n in SMALL}
    grads.update(_unpack_small(small_sum, small_shapes))
    loss = small_sum[7, 0]

    n = "w_branch_a"
    two_d = lambda a: a.reshape(w[n].shape[-2], w[n].shape[-1])
    d_, m_, v_ = adamw(two_d(w[n]), two_d(grads[n]), two_d(m[n]), two_d(v[n]), "adamw_" + n)
    delta[n], new_m[n], new_v[n] = [a.reshape(w[n].shape) for a in (d_, m_, v_)]
    zero_row = jnp.zeros((1, D), F32)
    pack = lambda t: _pack_small({n: t[n] for n in SMALL}, zero_row)
    d_, m_, v_ = adamw(pack(w), small_sum, pack(m), pack(v), "adamw_small")
    for src, dst in ((d_, delta), (m_, new_m), (v_, new_v)):
        dst.update(_unpack_small(src, small_shapes))

    return (loss, grad_x[None], *[grads[n] for n in ORDER], *[delta[n] for n in ORDER],
            *[new_m[n] for n in ORDER], *[new_v[n] for n in ORDER])
```

```python
import math

import jax
import jax.numpy as jnp
from jax import lax
from jax.experimental import pallas as pl
from jax.experimental.pallas import tpu as pltpu

F32 = jnp.float32
BF16 = jnp.bfloat16
MESH = pl.DeviceIdType.MESH

V7X_VMEM_BYTES = 64 * 1024 * 1024
VMEM_LIMIT = V7X_VMEM_BYTES - 8 * 1024 * 1024
LANES = 128

N_DEV = 8
EPS = 1e-6
NEG_INF = -1e30

DILATIONS = (1, 4, 16)
HALF_WINDOW = 64
HEAD_DIM_A = 64
HEADS_PER_GROUP_A = 8
GROUP_WIDTH_A = 512
A_QKV_WIDTH = 4608
A_TQ = 128
A_WIN = A_TQ + 2 * HALF_WINDOW
A_UNROLL = 8
A_SCALE = HEAD_DIM_A ** -0.5
WGRAD_TK = 2048
HEAD_DIM_B = 128
N_HEADS_B = 8
N_KV_B = 2
GQA_GROUP_B = 4
GRID_W = 64
ROPE_THETA = 10000.0
B_TQ_FWD = 256
B_TQ_BWD = 512
B_HEADS_PER_STEP = 4
LOG2E = 1.4426950408889634
B_Q_PRESCALE = HEAD_DIM_B ** -0.5 * LOG2E
N_BUCKETS = 32
MAX_DISTANCE = 1024
PB_GATE_A = 1536

ADAM_LR = 0.001
ADAM_B1 = 0.9
ADAM_B2 = 0.999
ADAM_EPS = 1e-08
ADAM_WD = 0.01
ADAM_STEP = 10

FFN_SHARD = 352
MIX_WIN, MIX_WB, MIX_WOUT, MIX_WA = 0, 1024, 1152, 1280
MIX_ROWS = 1344
REST_WB, REST_WOUT, REST_WA, REST_ROWS = 0, 128, 256, 320


def _dot(a, b, ca=1, cb=0):
    return lax.dot_general(a, b, (((ca,), (cb,)), ((), ())), preferred_element_type=F32)


def _call(name, body, grid, ins, outs, scratch=(), sem=None, aliases=None):
    ins = [tuple(i) + (None,) * (4 - len(i)) for i in ins]
    res = pl.pallas_call(
        body,
        out_shape=[jax.ShapeDtypeStruct(s, d) for (s, d, _, _) in outs],
        grid=grid,
        in_specs=[pl.BlockSpec(memory_space=pl.ANY) if bs is None else pl.BlockSpec(bs, im, pipeline_mode=pm)
                  for (_, bs, im, pm) in ins],
        out_specs=[pl.BlockSpec(bs, im) for (_, _, bs, im) in outs],
        scratch_shapes=list(scratch),
        name=name,
        input_output_aliases=aliases or {},
        compiler_params=pltpu.CompilerParams(dimension_semantics=sem, vmem_limit_bytes=VMEM_LIMIT),
    )(*[i[0] for i in ins])
    return res


def _sigmoid(x):
    return 0.5 * jnp.tanh(0.5 * x) + 0.5


def _position():
    return lax.axis_index("x"), lax.axis_index("y"), lax.axis_index("c")


def _hbm_specs(n):
    return [pl.BlockSpec(memory_space=pl.ANY) for _ in range(n)]


PAIR_BUFFERS = 4


def reduce_scatter_pair(grads, name):
    n = len(grads)
    C = grads[0].shape[2]
    half = [g.shape[1] // 2 for g in grads]
    chunks = [(i, q, hf) for i in range(n) for q in range(4) for hf in range(2)]
    nb = PAIR_BUFFERS

    def body(*refs):
        ins, theirs = refs[:n], refs[n:2 * n]
        buf, load_sems, send_sems, recv_sems = refs[2 * n:]
        x, y, c = _position()
        sibling = (x, y, 1 - c)

        def load(k):
            i, q, hf = chunks[k]
            r = half[i]
            return pltpu.make_async_copy(ins[i].at[2 * q + (1 - c), pl.ds(hf * r, r), :],
                                         buf.at[k % nb, pl.ds(0, r), :], load_sems.at[k % nb])

        def send(k):
            i, q, hf = chunks[k]
            r = half[i]
            return pltpu.make_async_remote_copy(
                src_ref=buf.at[k % nb, pl.ds(0, r), :], dst_ref=theirs[i].at[q, pl.ds(hf * r, r), :],
                send_sem=send_sems.at[k % nb], recv_sem=recv_sems.at[i],
                device_id=sibling, device_id_type=MESH)

        for k in range(len(chunks) + 1):
            if k < len(chunks):
                if k >= nb:
                    send(k - nb).wait_send()
                load(k).start()
            if k >= 1:
                load(k - 1).wait()
                send(k - 1).start()
        for k in range(max(0, len(chunks) - nb), len(chunks)):
            send(k).wait_send()
        for i in range(n):
            pltpu.make_async_remote_copy(
                src_ref=theirs[i], dst_ref=theirs[i], send_sem=send_sems.at[0], recv_sem=recv_sems.at[i],
                device_id=sibling, device_id_type=MESH).wait_recv()

    return pl.pallas_call(
        body,
        out_shape=[jax.ShapeDtypeStruct((4,) + g.shape[1:], g.dtype) for g in grads],
        in_specs=_hbm_specs(n),
        out_specs=_hbm_specs(n),
        scratch_shapes=[pltpu.VMEM((nb, max(half), C), grads[0].dtype), pltpu.SemaphoreType.DMA((nb,)),
                        pltpu.SemaphoreType.DMA((nb,)), pltpu.SemaphoreType.DMA((n,))],
        name=name,
        compiler_params=pltpu.CompilerParams(vmem_limit_bytes=VMEM_LIMIT),
    )(*grads)


_HBM_SPEC = pl.BlockSpec(memory_space=pltpu.HBM)
_SEM_SPEC = pl.BlockSpec(memory_space=pltpu.SEMAPHORE)
_TOKEN_SPEC = pl.BlockSpec(memory_space=pltpu.VMEM)
_DATAFLOW = pltpu.SideEffectType.DATAFLOW_SIDE_EFFECTING


def _split_start_many(name, exchanges):
    n = len(exchanges)

    def full_body(*refs):
        srcs, lands = refs[:n], refs[n:2 * n]
        sems = refs[2 * n:4 * n]
        token = refs[-1]
        for i, (body, _, _) in enumerate(exchanges):
            body(srcs[i], lands[i], sems[2 * i], sems[2 * i + 1])
        token[...] = jnp.zeros_like(token)

    srcs = [pltpu.with_memory_space_constraint(src, pltpu.HBM) for _, src, _ in exchanges]
    lands = [pltpu.with_memory_space_constraint(lax.empty(shape, src.dtype), pltpu.HBM)
             for _, src, shape in exchanges]
    res = pl.pallas_call(
        full_body, name=name,
        out_shape=(pltpu.SemaphoreType.DMA(()),) * (2 * n)
        + tuple(pltpu.HBM(a.shape, a.dtype) for a in srcs + lands) + (jax.ShapeDtypeStruct((8, LANES), F32),),
        in_specs=(_HBM_SPEC,) * (2 * n),
        out_specs=(_SEM_SPEC,) * (2 * n) + (_HBM_SPEC,) * (2 * n) + (_TOKEN_SPEC,),
        input_output_aliases={i: 2 * n + i for i in range(2 * n)},
        compiler_params=pltpu.CompilerParams(has_side_effects=_DATAFLOW),
    )(*srcs, *lands)
    return [(res[2 * i], res[2 * i + 1], res[2 * n + i], res[3 * n + i], res[-1]) for i in range(n)]


def _split_start(name, body, src, land_shape):
    return _split_start_many(name, [(body, src, land_shape)])[0]


def _split_wait(name, started, n_blocks, after):
    send_sem, recv_sem, src_thru, land_thru, _ = started
    after = after if isinstance(after, tuple) else (after,)

    def body(src_ref, land_ref, send_sem, recv_sem, *rest):
        x, y, c = _position()
        blocks = land_ref.at[pl.ds(0, n_blocks)]
        copy = pltpu.make_async_remote_copy(src_ref=blocks, dst_ref=blocks, send_sem=send_sem, recv_sem=recv_sem,
                                            device_id=(x, y, c), device_id_type=MESH)
        copy.wait_send()
        copy.wait_recv()

    return pl.pallas_call(
        body, name=name,
        out_shape=(pltpu.HBM(src_thru.shape, src_thru.dtype), pltpu.HBM(land_thru.shape, land_thru.dtype)),
        in_specs=(_HBM_SPEC, _HBM_SPEC, _SEM_SPEC, _SEM_SPEC) + (pl.BlockSpec(memory_space=pl.ANY),) * len(after),
        out_specs=(_HBM_SPEC, _HBM_SPEC),
        input_output_aliases={0: 0, 1: 1},
        compiler_params=pltpu.CompilerParams(has_side_effects=_DATAFLOW),
    )(src_thru, land_thru, send_sem, recv_sem, *after)


def all_gather_start_all(blocks, name):
    def starter(direct):
        def body(b_ref, land_ref, send_sem, recv_sem):
            x, y, c = _position()
            peers = _other_devices(x, y, c) if direct else [(x, y, 1 - c), (1 - x, y, c), (x, 1 - y, c),
                                                            (1 - x, 1 - y, c)]
            for peer in peers:
                pltpu.make_async_remote_copy(src_ref=b_ref, dst_ref=land_ref.at[4 * x + 2 * y + c],
                                             send_sem=send_sem, recv_sem=recv_sem,
                                             device_id=peer, device_id_type=MESH).start()
        return body

    return _split_start_many(name, [(starter(direct), block, (N_DEV,) + block.shape) for block, direct in blocks])


def all_gather_finish(block, land, name):
    R, C = block.shape

    def body(b_ref, land_in, land_ref, stage, load_sems, send_sems, recv_sems, own_sem):
        x, y, c = _position()
        sibling = (x, y, 1 - c)
        chips = [(1 - x, y), (x, 1 - y), (1 - x, 1 - y)]
        own_in = pltpu.make_async_copy(b_ref, stage.at[3], load_sems.at[3])
        own_in.start()
        loads = [pltpu.make_async_copy(land_in.at[4 * px + 2 * py + c], stage.at[j], load_sems.at[j])
                 for j, (px, py) in enumerate(chips)]
        for ld in loads:
            ld.start()
        sends = []
        for j, (px, py) in enumerate(chips):
            loads[j].wait()
            dst = land_ref.at[4 * px + 2 * py + c]
            cp = pltpu.make_async_remote_copy(src_ref=stage.at[j], dst_ref=dst, send_sem=send_sems.at[j],
                                              recv_sem=recv_sems.at[j], device_id=sibling, device_id_type=MESH)
            cp.start()
            sends.append(cp)
        own_in.wait()
        own_out = pltpu.make_async_copy(stage.at[3], land_ref.at[4 * x + 2 * y + c], own_sem)
        own_out.start()
        for j, (px, py) in enumerate(chips):
            dst = land_ref.at[4 * px + 2 * py + (1 - c)]
            pltpu.make_async_remote_copy(src_ref=stage.at[j], dst_ref=dst, send_sem=send_sems.at[j],
                                         recv_sem=recv_sems.at[j], device_id=sibling,
                                         device_id_type=MESH).wait_recv()
        for cp in sends:
            cp.wait_send()
        own_out.wait()

    return pl.pallas_call(
        body,
        out_shape=jax.ShapeDtypeStruct(land.shape, land.dtype),
        in_specs=_hbm_specs(2),
        out_specs=pl.BlockSpec(memory_space=pl.ANY),
        scratch_shapes=[pltpu.VMEM((4, R, C), block.dtype), pltpu.SemaphoreType.DMA((4,)),
                        pltpu.SemaphoreType.DMA((3,)), pltpu.SemaphoreType.DMA((3,)), pltpu.SemaphoreType.DMA],
        input_output_aliases={1: 0},
        name=name,
        compiler_params=pltpu.CompilerParams(vmem_limit_bytes=VMEM_LIMIT),
    )(block, land)


def reduce_scatter_start(parts, name):
    def body(p_ref, land_ref, send_sem, recv_sem):
        x, y, c = _position()
        for px, py in [(1 - x, y), (x, 1 - y), (1 - x, 1 - y)]:
            pltpu.make_async_remote_copy(src_ref=p_ref.at[2 * px + py], dst_ref=land_ref.at[2 * x + y],
                                         send_sem=send_sem, recv_sem=recv_sem,
                                         device_id=(px, py, c), device_id_type=MESH).start()

    return _split_start(name, body, parts, parts.shape)


def _other_devices(x, y, c):
    return [(1 - x if k & 4 else x, 1 - y if k & 2 else y, 1 - c if k & 1 else c) for k in range(1, N_DEV)]


def all_gather_place_own(block, land, name):
    R, C = block.shape

    def body(b_ref, land_in, land_ref, stage, sems):
        x, y, c = _position()
        load = pltpu.make_async_copy(b_ref, stage, sems.at[0])
        load.start()
        load.wait()
        store = pltpu.make_async_copy(stage, land_ref.at[4 * x + 2 * y + c], sems.at[1])
        store.start()
        store.wait()

    return pl.pallas_call(
        body,
        out_shape=jax.ShapeDtypeStruct(land.shape, land.dtype),
        in_specs=_hbm_specs(2),
        out_specs=pl.BlockSpec(memory_space=pl.ANY),
        scratch_shapes=[pltpu.VMEM((R, C), block.dtype), pltpu.SemaphoreType.DMA((2,))],
        input_output_aliases={1: 0},
        name=name,
    )(block, land)


def reduce_scatter_start_direct(grads, name):
    def body(g_ref, land_ref, send_sem, recv_sem):
        x, y, c = _position()
        for px, py, pc in _other_devices(x, y, c):
            pltpu.make_async_remote_copy(src_ref=g_ref.at[4 * px + 2 * py + pc],
                                         dst_ref=land_ref.at[4 * x + 2 * y + c],
                                         send_sem=send_sem, recv_sem=recv_sem,
                                         device_id=(px, py, pc), device_id_type=MESH).start()

    return _split_start(name, body, grads, grads.shape)


def small_all_gather(small, after):
    def body(small_ref, after_ref, smalls, s_send, s_recv, s_local):
        x, y, c = _position()
        me = 4 * x + 2 * y + c
        lc = pltpu.make_async_copy(small_ref, smalls.at[me], s_local)
        lc.start()
        remote = []
        k = 0
        for dx in (0, 1):
            for dy in (0, 1):
                for dc in (0, 1):
                    if dx + dy + dc == 0:
                        continue
                    peer = (1 - x if dx else x, 1 - y if dy else y, 1 - c if dc else c)
                    rc = pltpu.make_async_remote_copy(
                        src_ref=small_ref, dst_ref=smalls.at[me],
                        send_sem=s_send.at[k], recv_sem=s_recv.at[k],
                        device_id=peer, device_id_type=MESH)
                    rc.start()
                    remote.append(rc)
                    k += 1
        for rc in remote:
            rc.wait()
        lc.wait()

    return pl.pallas_call(
        body,
        out_shape=jax.ShapeDtypeStruct((N_DEV,) + small.shape, small.dtype),
        in_specs=_hbm_specs(2),
        out_specs=pl.BlockSpec(memory_space=pl.ANY),
        scratch_shapes=[pltpu.SemaphoreType.DMA((7,)), pltpu.SemaphoreType.DMA((7,)), pltpu.SemaphoreType.DMA],
        name="small_all_gather",
    )(small, after)


def pair_add(grads, theirs, core, name):
    _, R, C = theirs.shape
    tr = R // 2

    def body(c_ref, a_ref, b_ref, o_ref):
        o_ref[...] = (a_ref[...].astype(F32) + b_ref[...].astype(F32)).astype(BF16)

    return pl.pallas_call(
        body,
        out_shape=jax.ShapeDtypeStruct(theirs.shape, BF16),
        grid_spec=pltpu.PrefetchScalarGridSpec(
            num_scalar_prefetch=1, grid=(4, R // tr),
            in_specs=[pl.BlockSpec((None, tr, C), lambda q, i, c: (2 * q + c[0], i, 0)),
                      pl.BlockSpec((None, tr, C), lambda q, i, c: (q, i, 0))],
            out_specs=pl.BlockSpec((None, tr, C), lambda q, i, c: (q, i, 0))),
        name=name,
        compiler_params=pltpu.CompilerParams(dimension_semantics=("parallel", "parallel"),
                                             vmem_limit_bytes=VMEM_LIMIT),
    )(core, grads, theirs)


def sum_slots(recv, off, rows, blk, name):
    nq, _, C = recv.shape
    ob = off // blk

    def body(r_ref, o_ref):
        acc = r_ref[0].astype(F32)
        for q in range(1, nq):
            acc = acc + r_ref[q].astype(F32)
        o_ref[...] = acc

    return _call(name, body, (rows // blk,),
                 [(recv, (nq, blk, C), lambda i: (0, ob + i, 0))],
                 [((rows, C), F32, (blk, C), lambda i: (i, 0))], sem=("parallel",))[0]


def _sum_terms(refs):
    acc = refs[0][...].astype(F32)
    for r in refs[1:]:
        acc = acc + r[...].astype(F32)
    return acc


def sum_landed(own, land, me, off, rows, blk, name):
    n, _, C = land.shape
    ob = off // blk

    def body(c_ref, *refs):
        refs[n][...] = _sum_terms(refs[:n])

    def entry(flip):
        return pl.BlockSpec((None, blk, C), lambda i, c: (c[0] ^ flip, ob + i, 0))

    return pl.pallas_call(
        body,
        out_shape=jax.ShapeDtypeStruct((rows, C), F32),
        grid_spec=pltpu.PrefetchScalarGridSpec(
            num_scalar_prefetch=1, grid=(rows // blk,),
            in_specs=[entry(k) for k in range(n)],
            out_specs=pl.BlockSpec((blk, C), lambda i, c: (i, 0))),
        name=name,
        compiler_params=pltpu.CompilerParams(dimension_semantics=("parallel",), vmem_limit_bytes=VMEM_LIMIT),
    )(me, own, *([land] * (n - 1)))


def _adamw_update(wv, gv, mv, vv):
    nm = ADAM_B1 * mv + (1.0 - ADAM_B1) * gv
    nv = ADAM_B2 * vv + (1.0 - ADAM_B2) * (gv * gv)
    c1 = 1.0 / (1.0 - ADAM_B1 ** ADAM_STEP)
    c2 = 1.0 / (1.0 - ADAM_B2 ** ADAM_STEP)
    return -ADAM_LR * ((nm * c1) / (jnp.sqrt(nv * c2) + ADAM_EPS) + ADAM_WD * wv), nm, nv


def sum_adamw(own, land, me, off, blk, w, m, v, name):
    rows, C = w.shape
    n = land.shape[0]
    ob = off // blk

    def body(c_ref, *refs):
        w_ref, m_ref, v_ref, g_out, d_out, m_out, v_out = refs[n:]
        gv = _sum_terms(refs[:n])
        g_out[...] = gv
        d_out[...], m_out[...], v_out[...] = _adamw_update(w_ref[...], gv, m_ref[...], v_ref[...])

    def entry(flip):
        return pl.BlockSpec((None, blk, C), lambda i, c: (c[0] ^ flip, ob + i, 0))

    plain = pl.BlockSpec((blk, C), lambda i, c: (i, 0))
    return pl.pallas_call(
        body,
        out_shape=[jax.ShapeDtypeStruct((rows, C), F32)] * 4,
        grid_spec=pltpu.PrefetchScalarGridSpec(
            num_scalar_prefetch=1, grid=(rows // blk,),
            in_specs=[entry(k) for k in range(n)] + [plain, plain, plain],
            out_specs=[plain] * 4),
        name=name,
        compiler_params=pltpu.CompilerParams(dimension_semantics=("parallel",), vmem_limit_bytes=VMEM_LIMIT),
    )(me, own, *([land] * (n - 1)), w, m, v)


def adamw(w, g, m, v, name):
    R, C = w.shape
    tr = R
    for cand in (256, 128, 64, 32, 16, 8):
        if R % cand == 0 and R > cand:
            tr = cand
            break

    def body(w_ref, g_ref, m_ref, v_ref, d_ref, nm_ref, nv_ref):
        d_ref[...], nm_ref[...], nv_ref[...] = _adamw_update(w_ref[...], g_ref[...], m_ref[...], v_ref[...])

    spec = ((tr, C), lambda i: (i, 0))
    out = ((R, C), F32) + spec
    return _call(name, body, (R // tr,), [(w,) + spec, (g,) + spec, (m,) + spec, (v,) + spec],
                 [out, out, out], sem=("parallel",))


def _rms_tile(xv, gv):
    r = lax.rsqrt(jnp.mean(xv * xv, axis=-1, keepdims=True) + EPS)
    return (xv * r * gv).astype(BF16)


def rms_fwd(x, g, name):
    S, D = x.shape
    tr = 512

    def body(x_ref, g_ref, o_ref):
        o_ref[...] = _rms_tile(x_ref[...], g_ref[...])

    return _call(name, body, (S // tr,),
                 [(x, (tr, D), lambda i: (i, 0)), (g, (1, D), lambda i: (0, 0))],
                 [((S, D), BF16, (tr, D), lambda i: (i, 0))], sem=("parallel",))[0]


def _rms_bwd_tile(dn, xv, gv):
    r = lax.rsqrt(jnp.mean(xv * xv, axis=-1, keepdims=True) + EPS)
    xh = xv * r
    dxh = dn * gv
    dx = r * (dxh - xh * jnp.mean(dxh * xh, axis=-1, keepdims=True))
    return dx, dn * xh


def _final_loss_tile(xv, tv, gv):
    D = xv.shape[1]
    r = lax.rsqrt(jnp.mean(xv * xv, axis=-1, keepdims=True) + EPS)
    xh = xv * r
    e = xh * gv - tv
    part = 0.5 * jnp.sum(jnp.sum(e * e, axis=-1, keepdims=True) * (1.0 / D), axis=0, keepdims=True)
    dy = e * (1.0 / D)
    dxh = dy * gv
    dx = r * (dxh - xh * jnp.mean(dxh * xh, axis=-1, keepdims=True))
    return part, dx, jnp.sum(dy * xh, axis=0, keepdims=True)


FFN_TF = 4 * FFN_SHARD


def _ffn_pick(G, which):
    if isinstance(G, tuple):
        return (G[0], which) if which < 2 else (G[1], 0)
    return G, which


def _ffn_whole_w_spec(G, which):
    arr, blk = _ffn_pick(G, which)
    return (arr, (N_DEV, FFN_SHARD, arr.shape[2]), lambda *idx: (0, blk, 0), pl.Buffered(1))


def _ffn_hidden(a, b):
    av, bv = a.astype(F32), b.astype(F32)
    return (av * _sigmoid(av) * bv).astype(BF16)


def ffn_up(n, G, name):
    S, D = n.shape
    F = N_DEV * FFN_SHARD
    tm = 256

    def body(n_ref, w1_ref, w3_ref, abh_ref):
        nv = n_ref[...]
        a = _dot(nv, w1_ref[...].reshape(F, D), 1, 1).astype(BF16)
        b = _dot(nv, w3_ref[...].reshape(F, D), 1, 1).astype(BF16)
        abh_ref[0] = a
        abh_ref[1] = b
        abh_ref[2] = _ffn_hidden(a, b)

    return _call(name, body, (S // tm,),
                 [(n, (tm, D), lambda i: (i, 0)),
                  _ffn_whole_w_spec(G, 0), _ffn_whole_w_spec(G, 1)],
                 [((3, S, F), BF16, (3, tm, F), lambda i: (0, i, 0))],
                 sem=("parallel",))[0]


def ffn_down(abh, G, x, g_next, name):
    _, S, F = abh.shape
    D = x.shape[1]
    tm = 512

    def body(h_ref, w2_ref, x_ref, g_ref, o_ref, n_ref):
        xo = x_ref[...] + 0.5 * _dot(h_ref[...], w2_ref[...].reshape(F, D))
        o_ref[...] = xo
        n_ref[...] = _rms_tile(xo, g_ref[...])

    tile = ((tm, D), lambda i: (i, 0))
    return _call(name, body, (S // tm,),
                 [(abh, (None, tm, F), lambda i: (2, i, 0)), _ffn_whole_w_spec(G, 2),
                  (x,) + tile, (g_next, (1, D), lambda i: (0, 0))],
                 [((S, D), F32) + tile, ((S, D), BF16) + tile], sem=("parallel",))


def ffn_last(x, g, G, tgt, g_final, name):
    S, D = x.shape
    F = N_DEV * FFN_SHARD
    tm = 256

    def body(x_ref, g_ref, w1_ref, w3_ref, w2_ref, t_ref, gf_ref,
             n_ref, abh_ref, dxo_ref, dab_ref, dx_ref, dxb_ref, dg_ref, l_ref, dgf_ref):
        i = pl.program_id(0)
        xv, gv = x_ref[...], g_ref[...]
        chunks = [(slice(4 * f, 4 * f + 4), slice(f * FFN_TF, (f + 1) * FFN_TF)) for f in range(F // FFN_TF)]
        weight = lambda w_ref, slots: w_ref[slots].reshape(FFN_TF, D)
        nv = _rms_tile(xv, gv)
        n_ref[...] = nv
        y = None
        for slots, cols in chunks:
            a = _dot(nv, weight(w1_ref, slots), 1, 1).astype(BF16)
            b = _dot(nv, weight(w3_ref, slots), 1, 1).astype(BF16)
            h = _ffn_hidden(a, b)
            abh_ref[0, :, cols] = a
            abh_ref[1, :, cols] = b
            abh_ref[2, :, cols] = h
            t = _dot(h, weight(w2_ref, slots))
            y = t if y is None else y + t
        part, dxo, dgfp = _final_loss_tile(xv + 0.5 * y, t_ref[...], gf_ref[...])
        dxo_b = dxo.astype(BF16)
        dxo_ref[...] = dxo_b
        dn = None
        for slots, cols in chunks:
            dh = 0.5 * _dot(dxo_b, weight(w2_ref, slots), 1, 1)
            da, db = _ffn_hidden_grads(dh, abh_ref[0, :, cols].astype(F32), abh_ref[1, :, cols].astype(F32))
            da, db = da.astype(BF16), db.astype(BF16)
            dab_ref[0, :, cols] = da
            dab_ref[1, :, cols] = db
            t = _dot(da, weight(w1_ref, slots)) + _dot(db, weight(w3_ref, slots))
            dn = t if dn is None else dn + t
        dx, dgt = _rms_bwd_tile(dn, xv, gv)
        dx = dxo + dx
        dx_ref[...] = dx
        dxb_ref[...] = dx.astype(BF16)
        dgp = jnp.sum(dgt, axis=0, keepdims=True)

        @pl.when(i == 0)
        def _():
            dg_ref[...] = dgp
            l_ref[...] = jnp.broadcast_to(part, l_ref.shape)
            dgf_ref[...] = dgfp

        @pl.when(i > 0)
        def _():
            dg_ref[...] += dgp
            l_ref[...] += jnp.broadcast_to(part, l_ref.shape)
            dgf_ref[...] += dgfp

    tile = ((tm, D), lambda i: (i, 0))
    gain = ((1, D), lambda i: (0, 0))
    return _call(name, body, (S // tm,),
                 [(x,) + tile, (g,) + gain,
                  _ffn_whole_w_spec(G, 0), _ffn_whole_w_spec(G, 1), _ffn_whole_w_spec(G, 2),
                  (tgt,) + tile, (g_final,) + gain],
                 [((S, D), BF16) + tile, ((3, S, F), BF16, (3, tm, F), lambda i: (0, i, 0)),
                  ((S, D), BF16) + tile, ((2, S, F), BF16, (2, tm, F), lambda i: (0, i, 0)),
                  ((S, D), F32) + tile, ((S, D), BF16) + tile, ((1, D), F32) + gain,
                  ((1, LANES), F32, (1, LANES), lambda i: (0, 0)), ((1, D), F32) + gain],
                 sem=("arbitrary",))


def _ffn_hidden_grads(dh, av, bv):
    sig = _sigmoid(av)
    return dh * bv * (sig * (1.0 + av * (1.0 - sig))), dh * (av * sig)


def ffn_bwd_hidden(dxo, abh, G, name):
    _, S, F = abh.shape
    D = dxo.shape[1]
    tm = 256

    def body(d_ref, w2_ref, ab_ref, o_ref):
        dh = 0.5 * _dot(d_ref[...].astype(BF16), w2_ref[...].reshape(F, D), 1, 1)
        da, db = _ffn_hidden_grads(dh, ab_ref[0].astype(F32), ab_ref[1].astype(F32))
        o_ref[0] = da.astype(BF16)
        o_ref[1] = db.astype(BF16)

    return _call(name + "_down_bwd", body, (S // tm,),
                 [(dxo, (tm, D), lambda i: (i, 0)), _ffn_whole_w_spec(G, 2),
                  (abh, (2, tm, F), lambda i: (0, i, 0))],
                 [((2, S, F), BF16, (2, tm, F), lambda i: (0, i, 0))],
                 sem=("parallel",))[0]


def ffn_bwd_weights(dxo, abh, dab, n, name):
    _, S, F = abh.shape
    D = dxo.shape[1]
    nf = F // FFN_TF
    tk = WGRAD_TK
    nk = S // tk
    gshape = (N_DEV, 3 * FFN_SHARD, D)

    def dw2_body(h_ref, d_ref, o_ref, acc_ref):
        k = pl.program_id(1)
        p = _dot(h_ref[...], d_ref[...].astype(BF16), 0, 0)

        @pl.when(k == 0)
        def _():
            acc_ref[...] = p

        @pl.when(k > 0)
        def _():
            acc_ref[...] += p

        @pl.when(k == nk - 1)
        def _():
            o_ref[...] = (0.5 * acc_ref[...]).astype(BF16).reshape(4, FFN_SHARD, D)

    gw = _call(name + "_dw2", dw2_body, (nf, nk),
               [(abh, (None, tk, FFN_TF), lambda j, k: (2, k, j)), (dxo, (tk, D), lambda j, k: (k, 0))],
               [(gshape, BF16, (4, FFN_SHARD, D), lambda j, k: (j, 2, 0))],
               scratch=[pltpu.VMEM((FFN_TF, D), F32)], sem=("parallel", "arbitrary"))[0]

    def dw13_body(gw_ref, dab_ref, n_ref, o_ref):
        o_ref[...] = _dot(dab_ref[...], n_ref[...], 0, 0).astype(BF16).reshape(4, FFN_SHARD, D)

    gw = pl.pallas_call(
        dw13_body,
        out_shape=jax.ShapeDtypeStruct(gshape, BF16),
        grid=(2, nf),
        in_specs=[pl.BlockSpec(memory_space=pl.ANY),
                  pl.BlockSpec((None, S, FFN_TF), lambda w, j: (w, 0, j)),
                  pl.BlockSpec((S, D), lambda w, j: (0, 0))],
        out_specs=pl.BlockSpec((4, FFN_SHARD, D), lambda w, j: (j, w, 0)),
        input_output_aliases={0: 0},
        name=name + "_dw13",
        compiler_params=pltpu.CompilerParams(dimension_semantics=("parallel", "parallel"),
                                             vmem_limit_bytes=VMEM_LIMIT),
    )(gw, dab, n)
    return gw


def ffn_bwd_input(dab, G, x_in, g, dxo, name):
    _, S, F = dab.shape
    D = x_in.shape[1]
    tm = 256

    def dn_body(dab_ref, w1_ref, w3_ref, x_ref, d_ref, g_ref, dx_ref, dg_ref):
        i = pl.program_id(0)
        dn = _dot(dab_ref[0], w1_ref[...].reshape(F, D)) + _dot(dab_ref[1], w3_ref[...].reshape(F, D))
        dx, dgt = _rms_bwd_tile(dn, x_ref[...], g_ref[...])
        dx_ref[...] = d_ref[...] + dx
        dgp = jnp.sum(dgt, axis=0, keepdims=True)

        @pl.when(i == 0)
        def _():
            dg_ref[...] = dgp

        @pl.when(i > 0)
        def _():
            dg_ref[...] += dgp

    tile = ((tm, D), lambda i: (i, 0))
    return _call(name + "_dn", dn_body, (S // tm,),
                 [(dab, (2, tm, F), lambda i: (0, i, 0)),
                  _ffn_whole_w_spec(G, 0), _ffn_whole_w_spec(G, 1),
                  (x_in,) + tile, (dxo,) + tile, (g, (1, D), lambda i: (0, 0))],
                 [((S, D), F32) + tile, ((1, D), F32, (1, D), lambda i: (0, 0))],
                 sem=("arbitrary",))


PROJ_TN = 512


def in_proj(h, Gm, name):
    S, D = h.shape
    n_tiles = N_DEV * Gm.shape[2] // PROJ_TN

    def body(h_ref, w_ref, o_ref):
        o_ref[...] = _dot(h_ref[...], w_ref[...]).astype(BF16)

    return _call(name, body, (n_tiles,),
                 [(h, (S, D), lambda j: (0, 0)),
                  (Gm, (None, D, PROJ_TN), lambda j: (j // 2, 0, j % 2))],
                 [((S, n_tiles * PROJ_TN), BF16, (S, PROJ_TN), lambda j: (0, j))],
                 sem=("parallel",))[0]


def _dproj_pieces(dqkv, dq_b, dkv_b, dgate):
    pieces = [(dqkv[g], [(3 * which + g, (which, 0)) for which in range(3)]) for g in range(3)]
    pieces.append((dq_b, [(9, (None, 0)), (10, (None, 1))]))
    pieces.append((dkv_b, [(11, (None, 0))]))
    pieces.append((dgate, [(12 + 2 * a + b, (a, b)) for a in range(2) for b in range(2)]))
    return pieces


def in_proj_bwd_dw(pieces, h, gm_grads, name):
    S, D = h.shape
    steps = [(n, t, ix) for n, (_, tiles) in enumerate(pieces) for t, ix in tiles]
    n_steps = len(steps)

    def pick(table, j):
        out = table[-1]
        for k in range(len(table) - 2, -1, -1):
            out = jnp.where(j == k, table[k], out)
        return out

    def piece_spec(n, arr):
        own = [k for k, (m, _, _) in enumerate(steps) if m == n]
        at = [steps[min(max(k, own[0]), own[-1])][2] for k in range(n_steps)]
        lead, colb = [ix[0] for ix in at], [ix[1] for ix in at]
        if arr.ndim == 3:
            return (own[0], own[-1]), pl.BlockSpec((None, S, PROJ_TN), lambda j: (pick(lead, j), 0, pick(colb, j)))
        return (own[0], own[-1]), pl.BlockSpec((S, PROJ_TN), lambda j: (0, pick(colb, j)))

    spans, d_specs = zip(*[piece_spec(n, arr) for n, (arr, _) in enumerate(pieces)])
    w_tile = [t for _, t, _ in steps]

    def dw_body(gm_ref, h_ref, *refs):
        o_ref = refs[-1]
        j = pl.program_id(0)
        for d_ref, (first, last) in zip(refs[:-1], spans):
            @pl.when((j >= first) & (j <= last))
            def _(d_ref=d_ref):
                o_ref[...] = _dot(h_ref[...], d_ref[...], 0, 0).astype(BF16)

    return pl.pallas_call(
        dw_body,
        out_shape=jax.ShapeDtypeStruct(gm_grads.shape, BF16),
        grid=(n_steps,),
        in_specs=[pl.BlockSpec(memory_space=pl.ANY),
                  pl.BlockSpec((S, D), lambda j: (0, 0), pipeline_mode=pl.Buffered(1))] + list(d_specs),
        out_specs=pl.BlockSpec((None, D, PROJ_TN), lambda j: (pick(w_tile, j) // 2, 0, pick(w_tile, j) % 2)),
        input_output_aliases={0: 0},
        name=name + "_dw",
        compiler_params=pltpu.CompilerParams(dimension_semantics=("arbitrary",), vmem_limit_bytes=VMEM_LIMIT),
    )(gm_grads, h, *[arr for arr, _ in pieces])


def in_proj_bwd_dh(pieces, Gm, x_in, g, dres, name):
    S, D = x_in.shape
    tm = 256
    C = Gm.shape[2]
    n_sh = N_DEV
    n_p = len(pieces)

    order = []
    for _, tiles in pieces:
        for t, _ in tiles:
            if t // 2 not in order:
                order.append(t // 2)

    def dh_body(*refs):
        d_refs = refs[:n_p]
        w_hbm, x_ref, r_ref, g_ref, dx_ref, dxb_ref, dg_ref, w_ref, w_sem = refs[n_p:]
        i = pl.program_id(0)
        fetch = [pltpu.make_async_copy(w_hbm.at[s], w_ref.at[s], w_sem.at[s]) for s in range(n_sh)]

        @pl.when(i == 0)
        def _():
            for s in order:
                fetch[s].start()

        arrived = set()
        p = None
        for d_ref, (arr, tiles) in zip(d_refs, pieces):
            for t, (lead, colb) in tiles:
                if t // 2 not in arrived:
                    arrived.add(t // 2)
                    pl.when(i == 0)(fetch[t // 2].wait)
                cols = slice(colb * PROJ_TN, (colb + 1) * PROJ_TN)
                d = d_ref[:, cols] if lead is None else d_ref[lead, :, cols]
                wcol = (t % 2) * PROJ_TN
                term = _dot(d, w_ref[t // 2, :, wcol:wcol + PROJ_TN], 1, 1)
                p = term if p is None else p + term
        dx, dgt = _rms_bwd_tile(p, x_ref[...], g_ref[...])
        dx = r_ref[...] + dx
        dx_ref[...] = dx
        dxb_ref[...] = dx.astype(BF16)
        dgp = jnp.sum(dgt, axis=0, keepdims=True)

        @pl.when(i == 0)
        def _():
            dg_ref[...] = dgp

        @pl.when(i > 0)
        def _():
            dg_ref[...] += dgp

    tile = ((tm, D), lambda i: (i, 0))

    def rows_of(arr):
        if arr.ndim == 3:
            return (arr, (arr.shape[0], tm, arr.shape[2]), lambda i: (0, i, 0))
        return (arr, (tm, arr.shape[1]), lambda i: (i, 0))

    return _call(name + "_dh", dh_body, (S // tm,),
                 [rows_of(arr) for arr, _ in pieces]
                 + [(Gm,), (x_in,) + tile, (dres,) + tile, (g, (1, D), lambda i: (0, 0))],
                 [((S, D), F32) + tile, ((S, D), BF16) + tile, ((1, D), F32, (1, D), lambda i: (0, 0))],
                 scratch=[pltpu.VMEM((n_sh, D, C), BF16), pltpu.SemaphoreType.DMA((n_sh,))],
                 sem=("arbitrary",))


def _t5_bucket(rel):
    n = N_BUCKETS // 2
    max_exact = n // 2
    ret = jnp.where(rel > 0, n, 0)
    a = jnp.abs(rel)
    af = jnp.maximum(a, 1).astype(F32)
    large = max_exact + (jnp.log(af / max_exact) / math.log(MAX_DISTANCE / max_exact)
                         * (n - max_exact)).astype(jnp.int32)
    large = jnp.minimum(large, n - 1)
    return ret + jnp.where(a < max_exact, a, large)


def _bucket_tables():
    qi = jnp.arange(A_TQ, dtype=jnp.int32)[:, None]
    kj = jnp.arange(A_WIN, dtype=jnp.int32)[None, :]
    rel = kj - HALF_WINDOW - qi
    return jnp.stack([_t5_bucket(rel * d) for d in DILATIONS], axis=0)


def bias_build(rel_bias, buckets):
    def body(tab_ref, bk_ref, o_ref):
        col = pl.program_id(0) * HEADS_PER_GROUP_A + pl.program_id(1)
        bk = bk_ref[...]
        acc = jnp.zeros(bk.shape, F32)
        for b in range(N_BUCKETS):
            acc = jnp.where(bk == b, tab_ref[b, col], acc)
        qi = lax.broadcasted_iota(jnp.int32, bk.shape, 0)
        kj = lax.broadcasted_iota(jnp.int32, bk.shape, 1)
        band = jnp.where(jnp.abs(kj - HALF_WINDOW - qi) <= HALF_WINDOW, acc, NEG_INF)
        o_ref[0] = jnp.where(kj >= HALF_WINDOW, band, NEG_INF)
        o_ref[1] = band
        o_ref[2] = jnp.where(kj < A_TQ + HALF_WINDOW, band, NEG_INF)

    out = pl.pallas_call(
        body,
        out_shape=jax.ShapeDtypeStruct((3, HEADS_PER_GROUP_A // 2, 3, 2, A_TQ, A_WIN), F32),
        grid=(3, HEADS_PER_GROUP_A),
        in_specs=[pl.BlockSpec(memory_space=pltpu.SMEM),
                  pl.BlockSpec((None, A_TQ, A_WIN), lambda g, h: (g, 0, 0))],
        out_specs=pl.BlockSpec((None, None, 3, None, A_TQ, A_WIN), lambda g, h: (g, h // 2, 0, h % 2, 0, 0)),
        name="a_bias_build",
        compiler_params=pltpu.CompilerParams(dimension_semantics=("parallel", "parallel")),
    )(rel_bias, buckets)
    return out.reshape(3, HEADS_PER_GROUP_A // 2, 3, 2 * A_TQ, A_WIN)


def bias_bwd(dbias, buckets):
    def body(d_ref, bk_ref, o_ref):
        bk = bk_ref[...]
        for b in range(N_BUCKETS):
            mask = bk == b
            for h in range(HEADS_PER_GROUP_A):
                part = jnp.sum(jnp.where(mask, d_ref[h], 0.0), axis=1, keepdims=True)
                o_ref[h, b:b + 1, :] = jnp.broadcast_to(jnp.sum(part, axis=0, keepdims=True), (1, LANES))

    out = pl.pallas_call(
        body,
        out_shape=jax.ShapeDtypeStruct((3, HEADS_PER_GROUP_A, N_BUCKETS, LANES), F32),
        grid=(3,),
        in_specs=[pl.BlockSpec((None, HEADS_PER_GROUP_A, A_TQ, A_WIN), lambda g: (g, 0, 0, 0)),
                  pl.BlockSpec((None, A_TQ, A_WIN), lambda g: (g, 0, 0))],
        out_specs=pl.BlockSpec((None, HEADS_PER_GROUP_A, N_BUCKETS, LANES), lambda g: (g, 0, 0, 0)),
        name="a_bias_bwd",
        compiler_params=pltpu.CompilerParams(dimension_semantics=("parallel",)),
    )(dbias, buckets)
    return out[:, :, :, 0].transpose(2, 0, 1).reshape(N_BUCKETS, 3 * HEADS_PER_GROUP_A)


def _a_fill_padded(pad_ref, src_ref, n, pad):
    zeros = jnp.zeros((pad, LANES), pad_ref.dtype)
    pad_ref[0:pad, :] = zeros
    pad_ref[pad + n:2 * pad + n, :] = zeros
    pad_ref[pad:pad + n, :] = src_ref[...].astype(pad_ref.dtype)


def _a_stack_heads(x, lane):
    zero = jnp.zeros_like(x)
    return jnp.concatenate([jnp.where(lane < HEAD_DIM_A, x, zero), jnp.where(lane >= HEAD_DIM_A, x, zero)], axis=0)


def _a_bias_variant(qb, nqb):
    return jnp.where(qb == 0, 0, jnp.where(qb == nqb - 1, 2, 1))


def _a_slab_specs(proj, g):
    S = proj.shape[0]
    per = GROUP_WIDTH_A // LANES
    return [(proj, (S, LANES), lambda hp, w=w: (0, per * (3 * w + g) + hp)) for w in range(3)]


def a_fwd(proj, bias, g, name, others=None):
    S = proj.shape[0]
    d = DILATIONS[g]
    L = S // d
    nqb = L // A_TQ
    pad = HALF_WINDOW * d
    n_others = 0 if others is None else 4
    tr = 256

    def body(q_ref, k_ref, v_ref, b_ref, *refs):
        out1, out2, qf, kpad, vpad = refs[n_others:n_others + 5]
        o_ref, l_ref = refs[n_others + 5:] if others else (out1, out2)
        qf[...] = q_ref[...].astype(F32) * A_SCALE
        _a_fill_padded(kpad, k_ref, S, pad)
        _a_fill_padded(vpad, v_ref, S, pad)
        lane = lax.broadcasted_iota(jnp.int32, (A_TQ, LANES), 1)

        def block(t, carry):
            qb, r = t // d, t % d
            start = qb * (A_TQ * d) + r
            kw = kpad[pl.ds(start, A_WIN, stride=d), :].astype(BF16)
            vw = vpad[pl.ds(start, A_WIN, stride=d), :].astype(BF16)
            q = qf[pl.ds(start, A_TQ, stride=d), :].astype(BF16)
            q2 = _a_stack_heads(q, lane)
            s = _dot(q2, kw, 1, 1) + b_ref[_a_bias_variant(qb, nqb)]
            m = jnp.max(s, axis=-1, keepdims=True)
            e = jnp.exp(s - m)
            l = jnp.sum(e, axis=-1, keepdims=True)
            o2 = _dot(e.astype(BF16), vw) / l
            lse2 = m + jnp.log(l)
            o_ref[pl.ds(start, A_TQ, stride=d), :] = jnp.where(lane < HEAD_DIM_A, o2[0:A_TQ], o2[A_TQ:])
            l_ref[pl.ds(start, A_TQ, stride=d), :] = jnp.where(lane < HEAD_DIM_A, lse2[0:A_TQ], lse2[A_TQ:])
            return carry

        lax.fori_loop(0, nqb * d, block, 0, unroll=A_UNROLL)

        if others:
            o0, o1, l0, l1 = refs[:n_others]

            def combine(c, carry):
                rows = pl.ds(pl.multiple_of(c * tr, tr), tr)
                la, lb, lc = l0[rows, :], l1[rows, :], l_ref[rows, :]
                m = jnp.maximum(jnp.maximum(la, lb), lc)
                ea, eb, ec = jnp.exp(la - m), jnp.exp(lb - m), jnp.exp(lc - m)
                z = ea + eb + ec
                out1[rows, :] = ((ea * o0[rows, :] + eb * o1[rows, :] + ec * o_ref[rows, :]) / z).astype(BF16)
                out2[rows, :] = m + jnp.log(z)
                return carry

            lax.fori_loop(0, S // tr, combine, 0)

    slab = ((S, LANES), lambda hp: (0, hp))
    wide = (S, GROUP_WIDTH_A)
    other_ins = [(a,) + slab for a in (*others[0], *others[1])] if others else []
    return _call(name, body, (4,),
                 _a_slab_specs(proj, g)
                 + [(bias, (None, None, 3, 2 * A_TQ, A_WIN), lambda hp: (g, hp, 0, 0, 0))] + other_ins,
                 [(wide, BF16 if others else F32) + slab, (wide, F32) + slab],
                 scratch=[pltpu.VMEM((S, LANES), F32)] + [pltpu.VMEM((S + 2 * pad, LANES), F32)] * 2
                 + ([pltpu.VMEM((S, LANES), F32)] * 2 if others else []),
                 sem=("parallel",))


def a_bwd(proj, bias, do_a, o_a, lse_tot, g, name):
    S = proj.shape[0]
    d = DILATIONS[g]
    L = S // d
    nqb = L // A_TQ
    pad = HALF_WINDOW * d

    def body(q_ref, k_ref, v_ref, b_ref, do_ref, o_ref, l_ref, dqkv_ref, db_ref,
             qf, of, dqf, kpad, vpad, dkacc, dvacc):
        qf[...] = q_ref[...].astype(F32) * A_SCALE
        of[...] = o_ref[...].astype(F32)
        _a_fill_padded(kpad, k_ref, S, pad)
        _a_fill_padded(vpad, v_ref, S, pad)
        dkacc[...] = jnp.zeros(dkacc.shape, F32)
        dvacc[...] = jnp.zeros(dvacc.shape, F32)
        db_ref[...] = jnp.zeros(db_ref.shape, F32)
        lane = lax.broadcasted_iota(jnp.int32, (A_TQ, LANES), 1)

        def block(t, carry):
            qb, r = t // d, t % d
            start = qb * (A_TQ * d) + r
            rows = pl.ds(start, A_TQ, stride=d)
            win = pl.ds(start, A_WIN, stride=d)
            kw = kpad[win, :].astype(BF16)
            vw = vpad[win, :].astype(BF16)
            q = qf[rows, :].astype(BF16)
            do = do_ref[rows, :]
            ov = of[rows, :]
            lt = l_ref[rows, :]
            q2 = _a_stack_heads(q, lane)
            do2 = _a_stack_heads(do, lane)
            lt2 = jnp.concatenate([lt[:, 0:1], lt[:, HEAD_DIM_A:HEAD_DIM_A + 1]], axis=0)
            s = _dot(q2, kw, 1, 1) + b_ref[_a_bias_variant(qb, nqb)]
            p = jnp.exp(s - lt2)
            t = jnp.sum(do2 * jnp.concatenate([ov, ov], axis=0), axis=-1, keepdims=True)
            dob2 = do2.astype(BF16)
            ds = p * (_dot(dob2, vw, 1, 1) - t)
            db_ref[...] += ds
            dsb = ds.astype(BF16)
            dq2 = _dot(dsb, kw)
            dqf[rows, :] = jnp.where(lane < HEAD_DIM_A, dq2[0:A_TQ], dq2[A_TQ:]) * A_SCALE
            dkacc[win, :] += _dot(dsb, q2, 0, 0)
            dvacc[win, :] += _dot(p.astype(BF16), dob2, 0, 0)
            return carry

        lax.fori_loop(0, nqb * d, block, 0, unroll=A_UNROLL)
        dqkv_ref[0] = dqf[...].astype(BF16)
        dqkv_ref[1] = dkacc[pad:pad + S, :].astype(BF16)
        dqkv_ref[2] = dvacc[pad:pad + S, :].astype(BF16)

    slab = ((S, LANES), lambda hp: (0, hp))
    padded = pltpu.VMEM((S + 2 * pad, LANES), F32)
    return _call(
        name, body, (4,),
        _a_slab_specs(proj, g)
        + [(bias, (None, None, 3, 2 * A_TQ, A_WIN), lambda hp: (g, hp, 0, 0, 0)),
           (do_a,) + slab, (o_a,) + slab, (lse_tot,) + slab],
        [((3, S, GROUP_WIDTH_A), BF16, (3, S, LANES), lambda hp: (0, 0, hp)),
         ((4, 2 * A_TQ, A_WIN), F32, (None, 2 * A_TQ, A_WIN), lambda hp: (hp, 0, 0))],
        scratch=[pltpu.VMEM((S, LANES), F32)] * 3 + [padded] * 4,
        sem=("parallel",))


def _rope_tables(S):
    rows = S // GRID_W
    row = jnp.repeat(jnp.arange(rows, dtype=F32), GRID_W)
    col = jnp.tile(jnp.arange(GRID_W, dtype=F32), rows)
    n_freq = HEAD_DIM_B // 4
    freq = ROPE_THETA ** (-jnp.arange(n_freq, dtype=F32) / n_freq)
    ang = jnp.concatenate([row[:, None] * freq, col[:, None] * freq], axis=-1)
    cos, sin = jnp.cos(ang), jnp.sin(ang)
    return jnp.repeat(cos, 2, axis=-1), jnp.stack([-sin, sin], axis=-1).reshape(S, HEAD_DIM_B)


def _swap_pairs(y):
    lane = lax.broadcasted_iota(jnp.int32, y.shape, 1)
    return jnp.where(lane % 2 == 0, pltpu.roll(y, LANES - 1, 1), pltpu.roll(y, 1, 1))


def qkv_prep(proj, gains, cos_t, sin_t, name):
    S = proj.shape[0]
    ts = 256
    n_rot = N_HEADS_B + N_KV_B
    nh = n_rot + N_KV_B
    W = nh * LANES

    def body(x_ref, g_ref, c_ref, s_ref, o_ref):
        cv, sv = c_ref[...], s_ref[...]
        for hb in range(nh):
            cols = slice(hb * LANES, (hb + 1) * LANES)
            if hb < n_rot:
                xv = x_ref[:, cols].astype(F32)
                r = lax.rsqrt(jnp.mean(xv * xv, axis=-1, keepdims=True) + EPS)
                yv = xv * r * g_ref[:, cols]
                o_ref[:, cols] = (yv * cv + _swap_pairs(yv) * sv).astype(BF16)
            else:
                o_ref[:, cols] = x_ref[:, cols]

    return _call(name, body, (S // ts,),
                 [(proj, (ts, W), lambda i: (i, A_QKV_WIDTH // W)), (gains, (1, W), lambda i: (0, 0)),
                  (cos_t, (ts, LANES), lambda i: (i, 0)), (sin_t, (ts, LANES), lambda i: (i, 0))],
                 [((S, W), BF16, (ts, W), lambda i: (i, 0))],
                 sem=("parallel",))[0]


def qk_prep_bwd(dr, proj, col0, gain, cos_t, sin_t, name):
    S, W = dr.shape
    H = W // LANES
    ts = 256
    wx = math.gcd(W, col0)
    n_x = W // wx

    def body(d_ref, *refs):
        x_refs = refs[:n_x]
        g_ref, c_ref, s_ref, dx_ref, dg_ref = refs[n_x:]
        i = pl.program_id(0)
        cv, sv, gv = c_ref[...], s_ref[...], g_ref[...]
        dgp = jnp.zeros((1, LANES), F32)
        for hb in range(H):
            cols = slice(hb * LANES, (hb + 1) * LANES)
            xc = (hb * LANES) % wx
            xv = x_refs[(hb * LANES) // wx][:, xc:xc + LANES].astype(F32)
            dout = d_ref[:, cols]
            dy = dout * cv + _swap_pairs(dout * sv)
            dx, dgt = _rms_bwd_tile(dy, xv, gv)
            dx_ref[:, cols] = dx.astype(BF16)
            dgp = dgp + jnp.sum(dgt, axis=0, keepdims=True)

        @pl.when(i == 0)
        def _():
            dg_ref[...] = dgp

        @pl.when(i > 0)
        def _():
            dg_ref[...] += dgp

    return _call(name, body, (S // ts,),
                 [(dr, (ts, W), lambda i: (i, 0))]
                 + [(proj, (ts, wx), lambda i, k=k: (i, col0 // wx + k)) for k in range(n_x)]
                 + [(gain, (1, LANES), lambda i: (0, 0)),
                  (cos_t, (ts, LANES), lambda i: (i, 0)), (sin_t, (ts, LANES), lambda i: (i, 0))],
                 [((S, W), BF16, (ts, W), lambda i: (i, 0)),
                  ((1, LANES), F32, (1, LANES), lambda i: (0, 0))],
                 sem=("arbitrary",))


def _row_sums(x):
    hi = x.astype(BF16)
    lo = (x - hi.astype(F32)).astype(BF16)
    ones = jnp.ones((8, LANES), BF16)
    return (_dot(ones, hi, 1, 1) + _dot(ones, lo, 1, 1))[0:1, :]


def flash_fwd(qkv, name):
    S = qkv.shape[0]
    tq = B_TQ_FWD
    hps = B_HEADS_PER_STEP

    def body(q_ref, k_ref, v_ref, o_ref, l_ref):
        k, v = k_ref[...], v_ref[...]
        for j in range(hps):
            cols = slice(j * LANES, (j + 1) * LANES)
            s = _dot(q_ref[:, cols], k, 1, 1)
            m = jnp.max(s, axis=-1, keepdims=True)
            e = jnp.exp2(s - m)
            l = jnp.sum(e, axis=-1, keepdims=True)
            o_ref[:, cols] = (_dot(e.astype(BF16), v) / l).astype(BF16)
            lse = jnp.broadcast_to(m * (1.0 / LOG2E) + jnp.log(l), (tq, LANES))
            l_ref[j] = _row_sums(lse) * (1.0 / LANES)

    per = GQA_GROUP_B // hps
    heads = lambda g, h, i: (i, g * per + h)
    return _call(name, body, (N_KV_B, per, S // tq),
                 [(qkv, (tq, hps * LANES), heads),
                  (qkv, (S, LANES), lambda g, h, i: (0, N_HEADS_B + g)),
                  (qkv, (S, LANES), lambda g, h, i: (0, N_HEADS_B + N_KV_B + g))],
                 [((S, N_HEADS_B * LANES), BF16, (tq, hps * LANES), heads),
                  ((N_HEADS_B, 1, S), F32, (hps, 1, tq), lambda g, h, i: (g * per + h, 0, i))],
                 sem=("parallel", "parallel", "parallel"))


def flash_bwd(qkv, k_t, do_b, o_b, lse, name):
    S = qkv.shape[0]
    tq = B_TQ_BWD
    nq = S // tq
    scale = HEAD_DIM_B ** -0.5

    def body(q_ref, k_ref, v_ref, kt_ref, do_ref, o_ref, l_ref, dq_ref, dk_ref, dv_ref, dkacc, dvacc):
        h, i = pl.program_id(1), pl.program_id(2)

        @pl.when((h == 0) & (i == 0))
        def _():
            dkacc[...] = jnp.zeros(dkacc.shape, F32)
            dvacc[...] = jnp.zeros(dvacc.shape, F32)

        q = q_ref[...]
        dob = do_ref[...]
        t = _row_sums(dob.astype(F32) * o_ref[...].astype(F32))
        pt = jnp.exp2(_dot(k_ref[...], q, 1, 1) - l_ref[...] * LOG2E)
        dsb = (pt * (_dot(v_ref[...], dob, 1, 1) - t)).astype(BF16)
        dvacc[...] += _dot(pt.astype(BF16), dob)
        dkacc[...] += _dot(dsb, q)
        dq_ref[...] = _dot(kt_ref[...], dsb).T * scale

        @pl.when((h == GQA_GROUP_B - 1) & (i == nq - 1))
        def _():
            dk_ref[...] = dkacc[...] * (scale / B_Q_PRESCALE)
            dv_ref[...] = dvacc[...].astype(BF16)

    head = lambda g, h, i: (i, g * GQA_GROUP_B + h)
    return _call(name, body, (N_KV_B, GQA_GROUP_B, nq),
                 [(qkv, (tq, LANES), head),
                  (qkv, (S, LANES), lambda g, h, i: (0, N_HEADS_B + g)),
                  (qkv, (S, LANES), lambda g, h, i: (0, N_HEADS_B + N_KV_B + g)),
                  (k_t, (LANES, S), lambda g, h, i: (g, 0)),
                  (do_b, (tq, LANES), head), (o_b, (tq, LANES), head),
                  (lse, (None, 1, tq), lambda g, h, i: (g * GQA_GROUP_B + h, 0, i))],
                 [((S, N_HEADS_B * LANES), F32, (tq, LANES), head),
                  ((S, N_KV_B * LANES), F32, (S, LANES), lambda g, h, i: (0, g)),
                  ((S, N_KV_B * LANES), BF16, (S, LANES), lambda g, h, i: (0, g))],
                 scratch=[pltpu.VMEM((S, LANES), F32)] * 2,
                 sem=("parallel", "arbitrary", "arbitrary"))


MERGE_TN = 512


def _mix_rows_spec(Gm, row0, n_slots, slot_map, cols=None, col_map=None):
    C = Gm.shape[2] if cols is None else cols
    cm = (lambda *idx: 0) if col_map is None else col_map
    return (Gm, (n_slots, LANES, C), lambda *idx: (slot_map(*idx), row0 // LANES, cm(*idx)))


def _gate_specs(proj, tm):
    first = (A_QKV_WIDTH + PB_GATE_A) // MERGE_TN
    return [(proj, (tm, MERGE_TN), lambda i, k=k: (i, first + k)) for k in range(4)]


def _whole_rows_spec(Gm, row0):
    return _mix_rows_spec(Gm, row0, N_DEV, lambda *idx: 0)


def merge_fwd(o_a, o_b, w_a, Gm, proj, b_gate, x, name):
    S, D = x.shape
    tm = 256

    def body(oa_ref, ob_ref, wa_ref, wb_ref, wo_ref, g0, g1, g2, g3, bg_ref, x_ref, m_ref, ya_ref, yb_ref, xo_ref):
        ya = _dot(oa_ref[...], wa_ref[...])
        yb = _dot(ob_ref[...], wb_ref[...].reshape(N_DEV * LANES, D))
        ga = _sigmoid(jnp.concatenate([g0[...], g1[...]], axis=1).astype(F32) + bg_ref[:, 0:D])
        gb = _sigmoid(jnp.concatenate([g2[...], g3[...]], axis=1).astype(F32) + bg_ref[:, D:2 * D])
        merged = (ga * ya + gb * yb).astype(BF16)
        m_ref[...] = merged
        ya_ref[...] = ya.astype(BF16)
        yb_ref[...] = yb.astype(BF16)
        xo_ref[...] = x_ref[...] + _dot(merged, wo_ref[...].reshape(N_DEV * LANES, D))

    rows = lambda a: (a, (tm, a.shape[1]), lambda i: (i, 0))
    out = ((S, D), BF16, (tm, D), lambda i: (i, 0))
    return _call(name, body, (S // tm,),
                 [rows(o_a), rows(o_b), (w_a, w_a.shape, lambda i: (0, 0)),
                  _whole_rows_spec(Gm, REST_WB), _whole_rows_spec(Gm, REST_WOUT)]
                 + _gate_specs(proj, tm) + [(b_gate, (1, 2 * D), lambda i: (0, 0)), rows(x)],
                 [out, out, out, ((S, D), F32, (tm, D), lambda i: (i, 0))], sem=("parallel",))


def merge_bwd(dx2, w_a, Gm, ya, yb, proj, b_gate, name):
    S, D = dx2.shape
    tm = 256

    def body(d_ref, wo_ref, wa_ref, wb_ref, ya_ref, yb_ref, g0, g1, g2, g3, bg_ref,
             dya_ref, dyb_ref, dg_ref, dbg_ref, doa_ref, dob_ref):
        i = pl.program_id(0)
        dm = _dot(d_ref[...].astype(BF16), wo_ref[...].reshape(N_DEV * LANES, D), 1, 1)
        ga = _sigmoid(jnp.concatenate([g0[...], g1[...]], axis=1).astype(F32) + bg_ref[:, 0:D])
        gb = _sigmoid(jnp.concatenate([g2[...], g3[...]], axis=1).astype(F32) + bg_ref[:, D:2 * D])
        dya = (dm * ga).astype(BF16)
        dyb = (dm * gb).astype(BF16)
        dya_ref[...] = dya
        dyb_ref[...] = dyb
        dpa = dm * ya_ref[...].astype(F32) * ga * (1.0 - ga)
        dpb = dm * yb_ref[...].astype(F32) * gb * (1.0 - gb)
        dg_ref[0] = dpa.astype(BF16)
        dg_ref[1] = dpb.astype(BF16)
        doa_ref[...] = _dot(dya, wa_ref[...], 1, 1)
        dob_ref[...] = _dot(dyb, wb_ref[...].reshape(N_DEV * LANES, D), 1, 1).astype(BF16)
        sa =jnp.sum(dpa, axis=0, keepdims=True)
        sb = jnp.sum(dpb, axis=0, keepdims=True)

        @pl.when(i == 0)
        def _():
            dbg_ref[0] = sa
            dbg_ref[1] = sb

        @pl.when(i > 0)
        def _():
            dbg_ref[0] += sa
            dbg_ref[1] += sb

    tile = ((tm, D), lambda i: (i, 0))
    return _call(
        name, body, (S // tm,),
        [(dx2,) + tile, _whole_rows_spec(Gm, REST_WOUT), (w_a, w_a.shape, lambda i: (0, 0)),
         _whole_rows_spec(Gm, REST_WB), (ya,) + tile, (yb,) + tile]
        + _gate_specs(proj, tm) + [(b_gate, (1, 2 * D), lambda i: (0, 0))],
        [((S, D), BF16) + tile, ((S, D), BF16) + tile,
         ((2, S, D), BF16, (2, tm, D), lambda i: (0, i, 0)),
         ((2, 1, D), F32, (2, 1, D), lambda i: (0, 0, 0)),
         ((S, w_a.shape[0]), F32, (tm, w_a.shape[0]), lambda i: (i, 0)),
         ((S, N_HEADS_B * LANES), BF16, (tm, N_HEADS_B * LANES), lambda i: (i, 0))],
        sem=("arbitrary",))


def weight_grad_rows(a, b, grads, row0, name):
    S, M = a.shape
    N = b.shape[1]
    tmm = 512
    tk = WGRAD_TK
    nk = S // tk
    prior = [] if grads is None else [grads]

    def body(*refs):
        a_ref, b_ref, o_ref, acc_ref = refs[len(prior):]
        k = pl.program_id(1)
        p = _dot(a_ref[...], b_ref[...].astype(BF16), 0, 0)

        @pl.when(k == 0)
        def _():
            acc_ref[...] = p

        @pl.when(k > 0)
        def _():
            acc_ref[...] += p

        @pl.when(k == nk - 1)
        def _():
            o_ref[...] = acc_ref[...].astype(BF16).reshape(tmm // LANES, LANES, N)

    return pl.pallas_call(
        body,
        out_shape=jax.ShapeDtypeStruct((N_DEV, MIX_ROWS, N), BF16),
        grid=(M // tmm, nk),
        in_specs=[pl.BlockSpec(memory_space=pl.ANY)] * len(prior)
        + [pl.BlockSpec((tk, tmm), lambda j, k: (k, j)),
           pl.BlockSpec((tk, N), lambda j, k: (k, 0))],
        out_specs=pl.BlockSpec((tmm // LANES, LANES, N), lambda j, k: (j, row0 // LANES, 0)),
        scratch_shapes=[pltpu.VMEM((tmm, N), F32)],
        input_output_aliases={0: 0} if prior else {},
        name=name,
        compiler_params=pltpu.CompilerParams(dimension_semantics=("parallel", "arbitrary"),
                                             vmem_limit_bytes=VMEM_LIMIT),
    )(*prior, a, b)


def weight_grad_plain(a, b, name):
    S, M = a.shape
    N = b.shape[1]
    tk = WGRAD_TK
    nk = S // tk

    def body(a_ref, b_ref, o_ref, acc_ref):
        k = pl.program_id(0)
        p = _dot(a_ref[...], b_ref[...], 0, 0)

        @pl.when(k == 0)
        def _():
            acc_ref[...] = p

        @pl.when(k > 0)
        def _():
            acc_ref[...] += p

        @pl.when(k == nk - 1)
        def _():
            o_ref[...] = acc_ref[...].astype(BF16)

    return _call(name, body, (nk,),
                 [(a, (tk, M), lambda k: (k, 0)), (b, (tk, N), lambda k: (k, 0))],
                 [((M, N), BF16, (M, N), lambda k: (0, 0))],
                 scratch=[pltpu.VMEM((M, N), F32)], sem=("arbitrary",))[0]


def local_step(x, tgt, p, get_g1_up, get_g1_down, get_gm_in, get_gm_rest, get_g2, emit, start_token):
    S, D = x.shape
    after = lambda t: t[0:1, 0:1]
    buckets = _bucket_tables()
    cos_t, sin_t = _rope_tables(S)
    gains = jnp.concatenate([jnp.tile(p["q_norm"] * B_Q_PRESCALE, (1, N_HEADS_B)), jnp.tile(p["k_norm"], (1, N_KV_B)),
                             jnp.ones((1, N_KV_B * LANES), F32)], axis=1)

    n1 = rms_fwd(x, p["ffn1_norm"] + after(start_token), "ffn1_norm")
    bias = bias_build(p["rel_bias"] + after(start_token), buckets)
    g1_up = get_g1_up((n1, bias))
    ab1 = ffn_up(n1, (g1_up, None), "ffn1_up")
    G1 = (g1_up, get_g1_down(ab1))
    x1, hm = ffn_down(ab1, G1, x, p["mix_norm"], "ffn1_down")
    Gw = get_gm_in(hm)
    proj = in_proj(hm, Gw, "in_proj")

    outs, lses = zip(*[a_fwd(proj, bias, g, "a_fwd_%d" % g) for g in range(2)])
    o_a, lse_tot = a_fwd(proj, bias, 2, "a_fwd_2", (outs, lses))

    qkv = qkv_prep(proj, gains, cos_t, sin_t, "qkv_prep")
    k_t = qkv[:, N_HEADS_B * LANES:(N_HEADS_B + N_KV_B) * LANES].T
    o_b, lse_b = flash_fwd(qkv, "flash_fwd")

    Gm = get_gm_rest(o_b)
    w_a = Gm[:, REST_WA:REST_ROWS, :].reshape(N_DEV, GROUP_WIDTH_A, LANES).transpose(1, 0, 2).reshape(GROUP_WIDTH_A, D)
    merged, ya, yb, x2 = merge_fwd(o_a, o_b, w_a, Gm, proj, p["b_gate"], x1, "merge_fwd")

    G2 = get_g2(x2)
    n2, ab2, dx3_b, dab2, dx2, dx2_b, d_ffn2_norm, loss, d_final = ffn_last(
        x2, p["ffn2_norm"], G2, tgt, p["final_norm"], "ffn2")
    gw2 = ffn_bwd_weights(dx3_b, ab2, dab2, n2, "ffn2_bwd")
    t2 = emit("ffn2", gw2)

    dya, dyb, dgate, dbg, do_a, do_b = merge_bwd(dx2_b, w_a, Gm, ya, yb, proj, p["b_gate"] + after(t2),
                                                 "merge_bwd")
    gm_grads = weight_grad_rows(merged, dx2_b, None, MIX_WOUT, "dw_out")
    gm_grads = weight_grad_rows(o_b, dyb, gm_grads, MIX_WB, "dw_branch_b")
    dw_a = weight_grad_plain(o_a, dya, "dw_branch_a")

    dq_r, dk_r, dv_b = flash_bwd(qkv, k_t, do_b, o_b, lse_b, "flash_bwd")
    dq_b, d_q_norm = qk_prep_bwd(dq_r, proj, A_QKV_WIDTH, p["q_norm"], cos_t, sin_t, "q_prep_bwd")
    dk_b, d_k_norm = qk_prep_bwd(dk_r, proj, A_QKV_WIDTH + N_HEADS_B * LANES, p["k_norm"], cos_t, sin_t,
                                 "k_prep_bwd")

    dqkv, dbs = [], []
    for g in range(3):
        dg_, db = a_bwd(proj, bias, do_a, o_a, lse_tot, g, "a_bwd_%d" % g)
        dqkv.append(dg_)
        dbs.append(db)
    d_rel_bias = bias_bwd(jnp.stack(dbs, axis=0).reshape(3, HEADS_PER_GROUP_A, A_TQ, A_WIN), buckets)

    dproj = _dproj_pieces(dqkv, dq_b, jnp.concatenate([dk_b, dv_b], axis=1), dgate)
    gm_grads = in_proj_bwd_dw(dproj[:3], hm, gm_grads, "in_proj_bwd_a")
    gm_grads = in_proj_bwd_dw(dproj[3:], hm, gm_grads, "in_proj_bwd_b")
    dw_a_sh = dw_a.reshape(GROUP_WIDTH_A, N_DEV, LANES).transpose(1, 0, 2).reshape(N_DEV, MIX_ROWS - MIX_WA, D)
    gm_grads = lax.dynamic_update_slice(gm_grads, dw_a_sh, (0, MIX_WA, 0))
    tm = emit("mix", gm_grads)
    dx1, dx1_b, d_mix_norm = in_proj_bwd_dh(dproj, Gw, x1, p["mix_norm"] + after(tm), dx2, "in_proj_bwd")

    dab1 = ffn_bwd_hidden(dx1_b, ab1, G1, "ffn1_bwd")
    gw1 = ffn_bwd_weights(dx1_b, ab1, dab1, n1, "ffn1_bwd")
    t1 = emit("ffn1", gw1)
    dx0, d_ffn1_norm = ffn_bwd_input(dab1, G1, x, p["ffn1_norm"] + after(t1), dx1, "ffn1_bwd")

    small = dict(ffn1_norm=d_ffn1_norm, mix_norm=d_mix_norm, b_gate=dbg.reshape(1, 2 * D),
                 q_norm=d_q_norm, k_norm=d_k_norm, rel_bias=d_rel_bias, ffn2_norm=d_ffn2_norm,
                 final_norm=d_final)
    return loss, dx0, small


def _pack_small(t, loss_row):
    row6 = jnp.concatenate([t["q_norm"].reshape(1, -1), t["k_norm"].reshape(1, -1), t["rel_bias"].reshape(1, -1)], axis=1)
    return jnp.concatenate([t["ffn1_norm"].reshape(1, -1), t["mix_norm"].reshape(1, -1), t["b_gate"].reshape(2, -1),
                            t["ffn2_norm"].reshape(1, -1), t["final_norm"].reshape(1, -1), row6, loss_row], axis=0)


def _unpack_small(a, shapes):
    return dict(ffn1_norm=a[0:1].reshape(shapes["ffn1_norm"]), mix_norm=a[1:2].reshape(shapes["mix_norm"]),
                b_gate=a[2:4].reshape(shapes["b_gate"]), ffn2_norm=a[4:5].reshape(shapes["ffn2_norm"]),
                final_norm=a[5].reshape(shapes["final_norm"]), q_norm=a[6:7, 0:128].reshape(shapes["q_norm"]),
                k_norm=a[6:7, 128:256].reshape(shapes["k_norm"]), rel_bias=a[6, 256:1024].reshape(shapes["rel_bias"]))


SMALL = ("ffn1_norm", "mix_norm", "b_gate", "q_norm", "k_norm", "rel_bias", "ffn2_norm", "final_norm")
ORDER = ("ffn1_norm", "ffn1_w1", "ffn1_w3", "ffn1_w2", "mix_norm", "w_in", "b_gate", "q_norm", "k_norm", "rel_bias",
         "w_branch_a", "w_branch_b", "w_out", "ffn2_norm", "ffn2_w1", "ffn2_w3", "ffn2_w2", "final_norm")


def kernel(x, ffn1_norm, ffn1_w1, ffn1_w3, ffn1_w2, mix_norm, w_in, b_gate, q_norm, k_norm, rel_bias, w_branch_a, w_branch_b, w_out, ffn2_norm, ffn2_w1, ffn2_w3, ffn2_w2, final_norm, loss_target, m_ffn1_norm, m_ffn1_w1, m_ffn1_w3, m_ffn1_w2, m_mix_norm, m_w_in, m_b_gate, m_q_norm, m_k_norm, m_rel_bias, m_w_branch_a, m_w_branch_b, m_w_out, m_ffn2_norm, m_ffn2_w1, m_ffn2_w3, m_ffn2_w2, m_final_norm, v_ffn1_norm, v_ffn1_w1, v_ffn1_w3, v_ffn1_w2, v_mix_norm, v_w_in, v_b_gate, v_q_norm, v_k_norm, v_rel_bias, v_w_branch_a, v_w_branch_b, v_w_out, v_ffn2_norm, v_ffn2_w1, v_ffn2_w3, v_ffn2_w2, v_final_norm):
    args = dict(locals())
    w = {n: args[n] for n in ORDER}
    m = {n: args["m_" + n] for n in ORDER}
    v = {n: args["v_" + n] for n in ORDER}
    D = x.shape[2]

    blocks = (
        ("ffn1_up", jnp.concatenate([ffn1_w1[0].T, ffn1_w3[0].T], axis=0)),
        ("ffn1_down", ffn1_w2[0]),
        ("mix_in", w_in[0]),
        ("mix_rest", jnp.concatenate([w_branch_b[0], w_out[0], w_branch_a[0].reshape(REST_ROWS - REST_WA, D)], axis=0)),
        ("ffn2", jnp.concatenate([ffn2_w1[0].T, ffn2_w3[0].T, ffn2_w2[0]], axis=0)),
    )
    direct = ("mix_rest", "ffn2")
    started = all_gather_start_all([(b.astype(BF16), tag in direct) for tag, b in blocks], "all_gather_start")
    gathers = {tag: s for (tag, _), s in zip(blocks, started)}
    start_token = started[0][4]

    def gathered(tag):
        def get(after):
            if tag in direct:
                return all_gather_place_own(*_split_wait("all_gather_" + tag + "_wait", gathers[tag], N_DEV - 1, after),
                                            "all_gather_" + tag + "_own")
            return all_gather_finish(*_split_wait("all_gather_" + tag + "_wait", gathers[tag], 4, after),
                                     "all_gather_" + tag + "_finish")
        return get

    core = lax.axis_index("c").astype(jnp.int32).reshape(1)
    chip = (2 * lax.axis_index("x") + lax.axis_index("y")).astype(jnp.int32).reshape(1)
    device = 2 * chip + core
    exchanges = {}

    def emit(tag, gw):
        if tag == "ffn1":
            (theirs,) = reduce_scatter_pair([gw], "reduce_scatter_pair_" + tag)
            part = pair_add(gw, theirs, core, "pair_add_" + tag)
            exchanges[tag] = reduce_scatter_start(part, "reduce_scatter_" + tag + "_start")
        else:
            exchanges[tag] = reduce_scatter_start_direct(gw, "reduce_scatter_" + tag + "_start")
        return exchanges[tag][4]

    small_p = dict(ffn1_norm=ffn1_norm, mix_norm=mix_norm, b_gate=b_gate, q_norm=q_norm, k_norm=k_norm,
                   rel_bias=rel_bias, ffn2_norm=ffn2_norm, final_norm=final_norm.reshape(1, D))
    loss_p, grad_x, small_g = local_step(x[0], loss_target[0], small_p, gathered("ffn1_up"), gathered("ffn1_down"),
                                         gathered("mix_in"), gathered("mix_rest"), gathered("ffn2"), emit, start_token)

    def landed(tag, after):
        n_others, me = (3, chip) if tag == "ffn1" else (N_DEV - 1, device)
        return tuple(_split_wait("reduce_scatter_" + tag + "_wait", exchanges[tag], n_others, after)) + (me,)

    grads, delta, new_m, new_v = {}, {}, {}, {}

    def finish(n, part, land, me, off, blk, transposed=False):
        shp = w[n].shape
        if transposed:
            to2 = lambda a: a.reshape(shp[-2], shp[-1]).T
            back = lambda a: a.T.reshape(shp)
        else:
            to2 = lambda a: a.reshape(shp[-2], shp[-1])
            back = lambda a: a.reshape(shp)
        res = sum_adamw(part, land, me, off, blk, to2(w[n]), to2(m[n]), to2(v[n]), "update_" + n)
        grads[n], delta[n], new_m[n], new_v[n] = [back(a) for a in res]

    last_token = exchanges["ffn1"][4]
    for tag, after in (("ffn2", last_token), ("ffn1", grad_x)):
        group = landed(tag, after)
        finish(tag + "_w1", *group, 0, FFN_SHARD, transposed=True)
        finish(tag + "_w3", *group, FFN_SHARD, FFN_SHARD, transposed=True)
        finish(tag + "_w2", *group, 2 * FFN_SHARD, FFN_SHARD)
        if tag == "ffn2":
            group_m = landed("mix", last_token)
            finish("w_in", *group_m, MIX_WIN, LANES)
            finish("w_branch_b", *group_m, MIX_WB, LANES)
            finish("w_out", *group_m, MIX_WOUT, LANES)
            grads["w_branch_a"] = sum_landed(*group_m, MIX_WA, MIX_ROWS - MIX_WA, MIX_ROWS - MIX_WA,
                                             "w_branch_a_sum").reshape(w_branch_a.shape)
    loss_row = jnp.pad(loss_p, ((0, 0), (0, D - LANES)))
    smalls = small_all_gather(_pack_small(small_g, loss_row), new_v["w_in"])
    small_sum = sum_slots(smalls, 0, N_DEV, N_DEV, "small_sum")
    small_shapes = {n: w[n].shape for n in SMALL}
    grads.update(_unpack_small(small_sum, small_shapes))
    loss = small_sum[7, 0]

    n = "w_branch_a"
    two_d = lambda a: a.reshape(w[n].shape[-2], w[n].shape[-1])
    d_, m_, v_ = adamw(two_d(w[n]), two_d(grads[n]), two_d(m[n]), two_d(v[n]), "adamw_" + n)
    delta[n], new_m[n], new_v[n] = [a.reshape(w[n].shape) for a in (d_, m_, v_)]
    zero_row = jnp.zeros((1, D), F32)
    pack = lambda t: _pack_small({n: t[n] for n in SMALL}, zero_row)
    d_, m_, v_ = adamw(pack(w), small_sum, pack(m), pack(v), "adamw_small")
    for src, dst in ((d_, delta), (m_, new_m), (v_, new_v)):
        dst.update(_unpack_small(src, small_shapes))

    return (loss, grad_x[None], *[grads[n] for n in ORDER], *[delta[n] for n in ORDER],
            *[new_m[n] for n in ORDER], *[new_v[n] for n in ORDER])
```

```python
import math

import jax
import jax.numpy as jnp
from jax import lax
from jax.experimental import pallas as pl
from jax.experimental.pallas import tpu as pltpu

F32 = jnp.float32
BF16 = jnp.bfloat16
MESH = pl.DeviceIdType.MESH

V7X_VMEM_BYTES = 64 * 1024 * 1024
VMEM_LIMIT = V7X_VMEM_BYTES - 8 * 1024 * 1024
LANES = 128

N_DEV = 8
EPS = 1e-6
NEG_INF = -1e30

DILATIONS = (1, 4, 16)
HALF_WINDOW = 64
HEAD_DIM_A = 64
HEADS_PER_GROUP_A = 8
GROUP_WIDTH_A = 512
A_QKV_WIDTH = 4608
A_TQ = 128
A_WIN = A_TQ + 2 * HALF_WINDOW
A_UNROLL = 8
A_SCALE = HEAD_DIM_A ** -0.5
WGRAD_TK = 2048
HEAD_DIM_B = 128
N_HEADS_B = 8
N_KV_B = 2
GQA_GROUP_B = 4
GRID_W = 64
ROPE_THETA = 10000.0
B_TQ_FWD = 256
B_TQ_BWD = 512
B_HEADS_PER_STEP = 4
LOG2E = 1.4426950408889634
B_Q_PRESCALE = HEAD_DIM_B ** -0.5 * LOG2E
N_BUCKETS = 32
MAX_DISTANCE = 1024
PB_GATE_A = 1536

ADAM_LR = 0.001
ADAM_B1 = 0.9
ADAM_B2 = 0.999
ADAM_EPS = 1e-08
ADAM_WD = 0.01
ADAM_STEP = 10

FFN_SHARD = 352
MIX_WIN, MIX_WB, MIX_WOUT, MIX_WA = 0, 1024, 1152, 1280
MIX_ROWS = 1344
REST_WB, REST_WOUT, REST_WA, REST_ROWS = 0, 128, 256, 320


def _dot(a, b, ca=1, cb=0):
    return lax.dot_general(a, b, (((ca,), (cb,)), ((), ())), preferred_element_type=F32)


def _call(name, body, grid, ins, outs, scratch=(), sem=None, aliases=None):
    ins = [tuple(i) + (None,) * (4 - len(i)) for i in ins]
    res = pl.pallas_call(
        body,
        out_shape=[jax.ShapeDtypeStruct(s, d) for (s, d, _, _) in outs],
        grid=grid,
        in_specs=[pl.BlockSpec(bs, im, pipeline_mode=pm) for (_, bs, im, pm) in ins],
        out_specs=[pl.BlockSpec(bs, im) for (_, _, bs, im) in outs],
        scratch_shapes=list(scratch),
        name=name,
        input_output_aliases=aliases or {},
        compiler_params=pltpu.CompilerParams(dimension_semantics=sem, vmem_limit_bytes=VMEM_LIMIT),
    )(*[i[0] for i in ins])
    return res


def _sigmoid(x):
    return 0.5 * jnp.tanh(0.5 * x) + 0.5


def _position():
    return lax.axis_index("x"), lax.axis_index("y"), lax.axis_index("c")


def _hbm_specs(n):
    return [pl.BlockSpec(memory_space=pl.ANY) for _ in range(n)]


PAIR_BUFFERS = 4


def reduce_scatter_pair(grads, name):
    n = len(grads)
    C = grads[0].shape[2]
    half = [g.shape[1] // 2 for g in grads]
    chunks = [(i, q, hf) for i in range(n) for q in range(4) for hf in range(2)]
    nb = PAIR_BUFFERS

    def body(*refs):
        ins, theirs = refs[:n], refs[n:2 * n]
        buf, load_sems, send_sems, recv_sems = refs[2 * n:]
        x, y, c = _position()
        sibling = (x, y, 1 - c)

        def load(k):
            i, q, hf = chunks[k]
            r = half[i]
            return pltpu.make_async_copy(ins[i].at[2 * q + (1 - c), pl.ds(hf * r, r), :],
                                         buf.at[k % nb, pl.ds(0, r), :], load_sems.at[k % nb])

        def send(k):
            i, q, hf = chunks[k]
            r = half[i]
            return pltpu.make_async_remote_copy(
                src_ref=buf.at[k % nb, pl.ds(0, r), :], dst_ref=theirs[i].at[q, pl.ds(hf * r, r), :],
                send_sem=send_sems.at[k % nb], recv_sem=recv_sems.at[i],
                device_id=sibling, device_id_type=MESH)

        for k in range(len(chunks) + 1):
            if k < len(chunks):
                if k >= nb:
                    send(k - nb).wait_send()
                load(k).start()
            if k >= 1:
                load(k - 1).wait()
                send(k - 1).start()
        for k in range(max(0, len(chunks) - nb), len(chunks)):
            send(k).wait_send()
        for i in range(n):
            pltpu.make_async_remote_copy(
                src_ref=theirs[i], dst_ref=theirs[i], send_sem=send_sems.at[0], recv_sem=recv_sems.at[i],
                device_id=sibling, device_id_type=MESH).wait_recv()

    return pl.pallas_call(
        body,
        out_shape=[jax.ShapeDtypeStruct((4,) + g.shape[1:], g.dtype) for g in grads],
        in_specs=_hbm_specs(n),
        out_specs=_hbm_specs(n),
        scratch_shapes=[pltpu.VMEM((nb, max(half), C), grads[0].dtype), pltpu.SemaphoreType.DMA((nb,)),
                        pltpu.SemaphoreType.DMA((nb,)), pltpu.SemaphoreType.DMA((n,))],
        name=name,
        compiler_params=pltpu.CompilerParams(vmem_limit_bytes=VMEM_LIMIT),
    )(*grads)


_HBM_SPEC = pl.BlockSpec(memory_space=pltpu.HBM)
_SEM_SPEC = pl.BlockSpec(memory_space=pltpu.SEMAPHORE)
_TOKEN_SPEC = pl.BlockSpec(memory_space=pltpu.VMEM)
_DATAFLOW = pltpu.SideEffectType.DATAFLOW_SIDE_EFFECTING


def _split_start_many(name, exchanges):
    n = len(exchanges)

    def full_body(*refs):
        srcs, lands = refs[:n], refs[n:2 * n]
        sems = refs[2 * n:4 * n]
        token = refs[-1]
        for i, (body, _, _) in enumerate(exchanges):
            body(srcs[i], lands[i], sems[2 * i], sems[2 * i + 1])
        token[...] = jnp.zeros_like(token)

    srcs = [pltpu.with_memory_space_constraint(src, pltpu.HBM) for _, src, _ in exchanges]
    lands = [pltpu.with_memory_space_constraint(lax.empty(shape, src.dtype), pltpu.HBM)
             for _, src, shape in exchanges]
    res = pl.pallas_call(
        full_body, name=name,
        out_shape=(pltpu.SemaphoreType.DMA(()),) * (2 * n)
        + tuple(pltpu.HBM(a.shape, a.dtype) for a in srcs + lands) + (jax.ShapeDtypeStruct((8, LANES), F32),),
        in_specs=(_HBM_SPEC,) * (2 * n),
        out_specs=(_SEM_SPEC,) * (2 * n) + (_HBM_SPEC,) * (2 * n) + (_TOKEN_SPEC,),
        input_output_aliases={i: 2 * n + i for i in range(2 * n)},
        compiler_params=pltpu.CompilerParams(has_side_effects=_DATAFLOW),
    )(*srcs, *lands)
    return [(res[2 * i], res[2 * i + 1], res[2 * n + i], res[3 * n + i], res[-1]) for i in range(n)]


def _split_start(name, body, src, land_shape):
    return _split_start_many(name, [(body, src, land_shape)])[0]


def _split_wait(name, started, n_blocks, after):
    send_sem, recv_sem, src_thru, land_thru, _ = started
    after = after if isinstance(after, tuple) else (after,)

    def body(src_ref, land_ref, send_sem, recv_sem, *rest):
        x, y, c = _position()
        blocks = land_ref.at[pl.ds(0, n_blocks)]
        copy = pltpu.make_async_remote_copy(src_ref=blocks, dst_ref=blocks, send_sem=send_sem, recv_sem=recv_sem,
                                            device_id=(x, y, c), device_id_type=MESH)
        copy.wait_send()
        copy.wait_recv()

    return pl.pallas_call(
        body, name=name,
        out_shape=(pltpu.HBM(src_thru.shape, src_thru.dtype), pltpu.HBM(land_thru.shape, land_thru.dtype)),
        in_specs=(_HBM_SPEC, _HBM_SPEC, _SEM_SPEC, _SEM_SPEC) + (pl.BlockSpec(memory_space=pl.ANY),) * len(after),
        out_specs=(_HBM_SPEC, _HBM_SPEC),
        input_output_aliases={0: 0, 1: 1},
        compiler_params=pltpu.CompilerParams(has_side_effects=_DATAFLOW),
    )(src_thru, land_thru, send_sem, recv_sem, *after)


def all_gather_start_all(blocks, name):
    def starter(direct):
        def body(b_ref, land_ref, send_sem, recv_sem):
            x, y, c = _position()
            peers = _other_devices(x, y, c) if direct else [(x, y, 1 - c), (1 - x, y, c), (x, 1 - y, c),
                                                            (1 - x, 1 - y, c)]
            for peer in peers:
                pltpu.make_async_remote_copy(src_ref=b_ref, dst_ref=land_ref.at[4 * x + 2 * y + c],
                                             send_sem=send_sem, recv_sem=recv_sem,
                                             device_id=peer, device_id_type=MESH).start()
        return body

    return _split_start_many(name, [(starter(direct), block, (N_DEV,) + block.shape) for block, direct in blocks])


def all_gather_finish(block, land, name):
    R, C = block.shape

    def body(b_ref, land_in, land_ref, stage, load_sems, send_sems, recv_sems, own_sem):
        x, y, c = _position()
        sibling = (x, y, 1 - c)
        chips = [(1 - x, y), (x, 1 - y), (1 - x, 1 - y)]
        own_in = pltpu.make_async_copy(b_ref, stage.at[3], load_sems.at[3])
        own_in.start()
        loads = [pltpu.make_async_copy(land_in.at[4 * px + 2 * py + c], stage.at[j], load_sems.at[j])
                 for j, (px, py) in enumerate(chips)]
        for ld in loads:
            ld.start()
        sends = []
        for j, (px, py) in enumerate(chips):
            loads[j].wait()
            dst = land_ref.at[4 * px + 2 * py + c]
            cp = pltpu.make_async_remote_copy(src_ref=stage.at[j], dst_ref=dst, send_sem=send_sems.at[j],
                                              recv_sem=recv_sems.at[j], device_id=sibling, device_id_type=MESH)
            cp.start()
            sends.append(cp)
        own_in.wait()
        own_out = pltpu.make_async_copy(stage.at[3], land_ref.at[4 * x + 2 * y + c], own_sem)
        own_out.start()
        for j, (px, py) in enumerate(chips):
            dst = land_ref.at[4 * px + 2 * py + (1 - c)]
            pltpu.make_async_remote_copy(src_ref=stage.at[j], dst_ref=dst, send_sem=send_sems.at[j],
                                         recv_sem=recv_sems.at[j], device_id=sibling,
                                         device_id_type=MESH).wait_recv()
        for cp in sends:
            cp.wait_send()
        own_out.wait()

    return pl.pallas_call(
        body,
        out_shape=jax.ShapeDtypeStruct(land.shape, land.dtype),
        in_specs=_hbm_specs(2),
        out_specs=pl.BlockSpec(memory_space=pl.ANY),
        scratch_shapes=[pltpu.VMEM((4, R, C), block.dtype), pltpu.SemaphoreType.DMA((4,)),
                        pltpu.SemaphoreType.DMA((3,)), pltpu.SemaphoreType.DMA((3,)), pltpu.SemaphoreType.DMA],
        input_output_aliases={1: 0},
        name=name,
        compiler_params=pltpu.CompilerParams(vmem_limit_bytes=VMEM_LIMIT),
    )(block, land)


def reduce_scatter_start(parts, name):
    def body(p_ref, land_ref, send_sem, recv_sem):
        x, y, c = _position()
        for px, py in [(1 - x, y), (x, 1 - y), (1 - x, 1 - y)]:
            pltpu.make_async_remote_copy(src_ref=p_ref.at[2 * px + py], dst_ref=land_ref.at[2 * x + y],
                                         send_sem=send_sem, recv_sem=recv_sem,
                                         device_id=(px, py, c), device_id_type=MESH).start()

    return _split_start(name, body, parts, parts.shape)


def _other_devices(x, y, c):
    return [(1 - x if k & 4 else x, 1 - y if k & 2 else y, 1 - c if k & 1 else c) for k in range(1, N_DEV)]


def all_gather_place_own(block, land, name):
    R, C = block.shape

    def body(b_ref, land_in, land_ref, stage, sems):
        x, y, c = _position()
        load = pltpu.make_async_copy(b_ref, stage, sems.at[0])
        load.start()
        load.wait()
        store = pltpu.make_async_copy(stage, land_ref.at[4 * x + 2 * y + c], sems.at[1])
        store.start()
        store.wait()

    return pl.pallas_call(
        body,
        out_shape=jax.ShapeDtypeStruct(land.shape, land.dtype),
        in_specs=_hbm_specs(2),
        out_specs=pl.BlockSpec(memory_space=pl.ANY),
        scratch_shapes=[pltpu.VMEM((R, C), block.dtype), pltpu.SemaphoreType.DMA((2,))],
        input_output_aliases={1: 0},
        name=name,
    )(block, land)


def reduce_scatter_start_direct(grads, name):
    def body(g_ref, land_ref, send_sem, recv_sem):
        x, y, c = _position()
        for px, py, pc in _other_devices(x, y, c):
            pltpu.make_async_remote_copy(src_ref=g_ref.at[4 * px + 2 * py + pc],
                                         dst_ref=land_ref.at[4 * x + 2 * y + c],
                                         send_sem=send_sem, recv_sem=recv_sem,
                                         device_id=(px, py, pc), device_id_type=MESH).start()

    return _split_start(name, body, grads, grads.shape)


def small_all_gather(small, after):
    def body(small_ref, after_ref, smalls, s_send, s_recv, s_local):
        x, y, c = _position()
        me = 4 * x + 2 * y + c
        lc = pltpu.make_async_copy(small_ref, smalls.at[me], s_local)
        lc.start()
        remote = []
        k = 0
        for dx in (0, 1):
            for dy in (0, 1):
                for dc in (0, 1):
                    if dx + dy + dc == 0:
                        continue
                    peer = (1 - x if dx else x, 1 - y if dy else y, 1 - c if dc else c)
                    rc = pltpu.make_async_remote_copy(
                        src_ref=small_ref, dst_ref=smalls.at[me],
                        send_sem=s_send.at[k], recv_sem=s_recv.at[k],
                        device_id=peer, device_id_type=MESH)
                    rc.start()
                    remote.append(rc)
                    k += 1
        for rc in remote:
            rc.wait()
        lc.wait()

    return pl.pallas_call(
        body,
        out_shape=jax.ShapeDtypeStruct((N_DEV,) + small.shape, small.dtype),
        in_specs=_hbm_specs(2),
        out_specs=pl.BlockSpec(memory_space=pl.ANY),
        scratch_shapes=[pltpu.SemaphoreType.DMA((7,)), pltpu.SemaphoreType.DMA((7,)), pltpu.SemaphoreType.DMA],
        name="small_all_gather",
    )(small, after)


def pair_add(grads, theirs, core, name):
    _, R, C = theirs.shape
    tr = R // 2

    def body(c_ref, a_ref, b_ref, o_ref):
        o_ref[...] = (a_ref[...].astype(F32) + b_ref[...].astype(F32)).astype(BF16)

    return pl.pallas_call(
        body,
        out_shape=jax.ShapeDtypeStruct(theirs.shape, BF16),
        grid_spec=pltpu.PrefetchScalarGridSpec(
            num_scalar_prefetch=1, grid=(4, R // tr),
            in_specs=[pl.BlockSpec((None, tr, C), lambda q, i, c: (2 * q + c[0], i, 0)),
                      pl.BlockSpec((None, tr, C), lambda q, i, c: (q, i, 0))],
            out_specs=pl.BlockSpec((None, tr, C), lambda q, i, c: (q, i, 0))),
        name=name,
        compiler_params=pltpu.CompilerParams(dimension_semantics=("parallel", "parallel"),
                                             vmem_limit_bytes=VMEM_LIMIT),
    )(core, grads, theirs)


def sum_slots(recv, off, rows, blk, name):
    nq, _, C = recv.shape
    ob = off // blk

    def body(r_ref, o_ref):
        acc = r_ref[0].astype(F32)
        for q in range(1, nq):
            acc = acc + r_ref[q].astype(F32)
        o_ref[...] = acc

    return _call(name, body, (rows // blk,),
                 [(recv, (nq, blk, C), lambda i: (0, ob + i, 0))],
                 [((rows, C), F32, (blk, C), lambda i: (i, 0))], sem=("parallel",))[0]


def _sum_terms(refs):
    acc = refs[0][...].astype(F32)
    for r in refs[1:]:
        acc = acc + r[...].astype(F32)
    return acc


def sum_landed(own, land, me, off, rows, blk, name):
    n, _, C = land.shape
    ob = off // blk

    def body(c_ref, *refs):
        refs[n][...] = _sum_terms(refs[:n])

    def entry(flip):
        return pl.BlockSpec((None, blk, C), lambda i, c: (c[0] ^ flip, ob + i, 0))

    return pl.pallas_call(
        body,
        out_shape=jax.ShapeDtypeStruct((rows, C), F32),
        grid_spec=pltpu.PrefetchScalarGridSpec(
            num_scalar_prefetch=1, grid=(rows // blk,),
            in_specs=[entry(k) for k in range(n)],
            out_specs=pl.BlockSpec((blk, C), lambda i, c: (i, 0))),
        name=name,
        compiler_params=pltpu.CompilerParams(dimension_semantics=("parallel",), vmem_limit_bytes=VMEM_LIMIT),
    )(me, own, *([land] * (n - 1)))


def _adamw_update(wv, gv, mv, vv):
    nm = ADAM_B1 * mv + (1.0 - ADAM_B1) * gv
    nv = ADAM_B2 * vv + (1.0 - ADAM_B2) * (gv * gv)
    c1 = 1.0 / (1.0 - ADAM_B1 ** ADAM_STEP)
    c2 = 1.0 / (1.0 - ADAM_B2 ** ADAM_STEP)
    return -ADAM_LR * ((nm * c1) / (jnp.sqrt(nv * c2) + ADAM_EPS) + ADAM_WD * wv), nm, nv


def sum_adamw(own, land, me, off, blk, w, m, v, name):
    rows, C = w.shape
    n = land.shape[0]
    ob = off // blk

    def body(c_ref, *refs):
        w_ref, m_ref, v_ref, g_out, d_out, m_out, v_out = refs[n:]
        gv = _sum_terms(refs[:n])
        g_out[...] = gv
        d_out[...], m_out[...], v_out[...] = _adamw_update(w_ref[...], gv, m_ref[...], v_ref[...])

    def entry(flip):
        return pl.BlockSpec((None, blk, C), lambda i, c: (c[0] ^ flip, ob + i, 0))

    plain = pl.BlockSpec((blk, C), lambda i, c: (i, 0))
    return pl.pallas_call(
        body,
        out_shape=[jax.ShapeDtypeStruct((rows, C), F32)] * 4,
        grid_spec=pltpu.PrefetchScalarGridSpec(
            num_scalar_prefetch=1, grid=(rows // blk,),
            in_specs=[entry(k) for k in range(n)] + [plain, plain, plain],
            out_specs=[plain] * 4),
        name=name,
        compiler_params=pltpu.CompilerParams(dimension_semantics=("parallel",), vmem_limit_bytes=VMEM_LIMIT),
    )(me, own, *([land] * (n - 1)), w, m, v)


def adamw(w, g, m, v, name):
    R, C = w.shape
    tr = R
    for cand in (256, 128, 64, 32, 16, 8):
        if R % cand == 0 and R > cand:
            tr = cand
            break

    def body(w_ref, g_ref, m_ref, v_ref, d_ref, nm_ref, nv_ref):
        d_ref[...], nm_ref[...], nv_ref[...] = _adamw_update(w_ref[...], g_ref[...], m_ref[...], v_ref[...])

    spec = ((tr, C), lambda i: (i, 0))
    out = ((R, C), F32) + spec
    return _call(name, body, (R // tr,), [(w,) + spec, (g,) + spec, (m,) + spec, (v,) + spec],
                 [out, out, out], sem=("parallel",))


def _rms_tile(xv, gv):
    r = lax.rsqrt(jnp.mean(xv * xv, axis=-1, keepdims=True) + EPS)
    return (xv * r * gv).astype(BF16)


def rms_fwd(x, g, name):
    S, D = x.shape
    tr = 512

    def body(x_ref, g_ref, o_ref):
        o_ref[...] = _rms_tile(x_ref[...], g_ref[...])

    return _call(name, body, (S // tr,),
                 [(x, (tr, D), lambda i: (i, 0)), (g, (1, D), lambda i: (0, 0))],
                 [((S, D), BF16, (tr, D), lambda i: (i, 0))], sem=("parallel",))[0]


def _rms_bwd_tile(dn, xv, gv):
    r = lax.rsqrt(jnp.mean(xv * xv, axis=-1, keepdims=True) + EPS)
    xh = xv * r
    dxh = dn * gv
    dx = r * (dxh - xh * jnp.mean(dxh * xh, axis=-1, keepdims=True))
    return dx, dn * xh


def _final_loss_tile(xv, tv, gv):
    D = xv.shape[1]
    r = lax.rsqrt(jnp.mean(xv * xv, axis=-1, keepdims=True) + EPS)
    xh = xv * r
    e = xh * gv - tv
    part = 0.5 * jnp.sum(jnp.sum(e * e, axis=-1, keepdims=True) * (1.0 / D), axis=0, keepdims=True)
    dy = e * (1.0 / D)
    dxh = dy * gv
    dx = r * (dxh - xh * jnp.mean(dxh * xh, axis=-1, keepdims=True))
    return part, dx, jnp.sum(dy * xh, axis=0, keepdims=True)


FFN_TF = 4 * FFN_SHARD


def _ffn_pick(G, which):
    if isinstance(G, tuple):
        return (G[0], which) if which < 2 else (G[1], 0)
    return G, which


def _ffn_whole_w_spec(G, which):
    arr, blk = _ffn_pick(G, which)
    return (arr, (N_DEV, FFN_SHARD, arr.shape[2]), lambda *idx: (0, blk, 0), pl.Buffered(1))


def _ffn_hidden(a, b):
    av, bv = a.astype(F32), b.astype(F32)
    return (av * _sigmoid(av) * bv).astype(BF16)


def ffn_up(n, G, name):
    S, D = n.shape
    F = N_DEV * FFN_SHARD
    tm = 256

    def body(n_ref, w1_ref, w3_ref, abh_ref):
        nv = n_ref[...]
        a = _dot(nv, w1_ref[...].reshape(F, D), 1, 1).astype(BF16)
        b = _dot(nv, w3_ref[...].reshape(F, D), 1, 1).astype(BF16)
        abh_ref[0] = a
        abh_ref[1] = b
        abh_ref[2] = _ffn_hidden(a, b)

    return _call(name, body, (S // tm,),
                 [(n, (tm, D), lambda i: (i, 0)),
                  _ffn_whole_w_spec(G, 0), _ffn_whole_w_spec(G, 1)],
                 [((3, S, F), BF16, (3, tm, F), lambda i: (0, i, 0))],
                 sem=("parallel",))[0]


def ffn_down(abh, G, x, g_next, name):
    _, S, F = abh.shape
    D = x.shape[1]
    tm = 512

    def body(h_ref, w2_ref, x_ref, g_ref, o_ref, n_ref):
        xo = x_ref[...] + 0.5 * _dot(h_ref[...], w2_ref[...].reshape(F, D))
        o_ref[...] = xo
        n_ref[...] = _rms_tile(xo, g_ref[...])

    tile = ((tm, D), lambda i: (i, 0))
    return _call(name, body, (S // tm,),
                 [(abh, (None, tm, F), lambda i: (2, i, 0)), _ffn_whole_w_spec(G, 2),
                  (x,) + tile, (g_next, (1, D), lambda i: (0, 0))],
                 [((S, D), F32) + tile, ((S, D), BF16) + tile], sem=("parallel",))


def ffn_last(x, g, G, tgt, g_final, name):
    S, D = x.shape
    F = N_DEV * FFN_SHARD
    tm = 256

    def body(x_ref, g_ref, w1_ref, w3_ref, w2_ref, t_ref, gf_ref,
             n_ref, abh_ref, dxo_ref, dab_ref, dx_ref, dxb_ref, dg_ref, l_ref, dgf_ref):
        i = pl.program_id(0)
        xv, gv = x_ref[...], g_ref[...]
        chunks = [(slice(4 * f, 4 * f + 4), slice(f * FFN_TF, (f + 1) * FFN_TF)) for f in range(F // FFN_TF)]
        weight = lambda w_ref, slots: w_ref[slots].reshape(FFN_TF, D)
        nv = _rms_tile(xv, gv)
        n_ref[...] = nv
        y = None
        for slots, cols in chunks:
            a = _dot(nv, weight(w1_ref, slots), 1, 1).astype(BF16)
            b = _dot(nv, weight(w3_ref, slots), 1, 1).astype(BF16)
            h = _ffn_hidden(a, b)
            abh_ref[0, :, cols] = a
            abh_ref[1, :, cols] = b
            abh_ref[2, :, cols] = h
            t = _dot(h, weight(w2_ref, slots))
            y = t if y is None else y + t
        part, dxo, dgfp = _final_loss_tile(xv + 0.5 * y, t_ref[...], gf_ref[...])
        dxo_b = dxo.astype(BF16)
        dxo_ref[...] = dxo_b
        dn = None
        for slots, cols in chunks:
            dh = 0.5 * _dot(dxo_b, weight(w2_ref, slots), 1, 1)
            da, db = _ffn_hidden_grads(dh, abh_ref[0, :, cols].astype(F32), abh_ref[1, :, cols].astype(F32))
            da, db = da.astype(BF16), db.astype(BF16)
            dab_ref[0, :, cols] = da
            dab_ref[1, :, cols] = db
            t = _dot(da, weight(w1_ref, slots)) + _dot(db, weight(w3_ref, slots))
            dn = t if dn is None else dn + t
        dx, dgt = _rms_bwd_tile(dn, xv, gv)
        dx = dxo + dx
        dx_ref[...] = dx
        dxb_ref[...] = dx.astype(BF16)
        dgp = jnp.sum(dgt, axis=0, keepdims=True)

        @pl.when(i == 0)
        def _():
            dg_ref[...] = dgp
            l_ref[...] = jnp.broadcast_to(part, l_ref.shape)
            dgf_ref[...] = dgfp

        @pl.when(i > 0)
        def _():
            dg_ref[...] += dgp
            l_ref[...] += jnp.broadcast_to(part, l_ref.shape)
            dgf_ref[...] += dgfp

    tile = ((tm, D), lambda i: (i, 0))
    gain = ((1, D), lambda i: (0, 0))
    return _call(name, body, (S // tm,),
                 [(x,) + tile, (g,) + gain,
                  _ffn_whole_w_spec(G, 0), _ffn_whole_w_spec(G, 1), _ffn_whole_w_spec(G, 2),
                  (tgt,) + tile, (g_final,) + gain],
                 [((S, D), BF16) + tile, ((3, S, F), BF16, (3, tm, F), lambda i: (0, i, 0)),
                  ((S, D), BF16) + tile, ((2, S, F), BF16, (2, tm, F), lambda i: (0, i, 0)),
                  ((S, D), F32) + tile, ((S, D), BF16) + tile, ((1, D), F32) + gain,
                  ((1, LANES), F32, (1, LANES), lambda i: (0, 0)), ((1, D), F32) + gain],
                 sem=("arbitrary",))


def _ffn_hidden_grads(dh, av, bv):
    sig = _sigmoid(av)
    return dh * bv * (sig * (1.0 + av * (1.0 - sig))), dh * (av * sig)


def ffn_bwd_hidden(dxo, abh, G, name):
    _, S, F = abh.shape
    D = dxo.shape[1]
    tm = 256

    def body(d_ref, w2_ref, ab_ref, o_ref):
        dh = 0.5 * _dot(d_ref[...].astype(BF16), w2_ref[...].reshape(F, D), 1, 1)
        da, db = _ffn_hidden_grads(dh, ab_ref[0].astype(F32), ab_ref[1].astype(F32))
        o_ref[0] = da.astype(BF16)
        o_ref[1] = db.astype(BF16)

    return _call(name + "_down_bwd", body, (S // tm,),
                 [(dxo, (tm, D), lambda i: (i, 0)), _ffn_whole_w_spec(G, 2),
                  (abh, (2, tm, F), lambda i: (0, i, 0))],
                 [((2, S, F), BF16, (2, tm, F), lambda i: (0, i, 0))],
                 sem=("parallel",))[0]


def ffn_bwd_weights(dxo, abh, dab, n, name):
    _, S, F = abh.shape
    D = dxo.shape[1]
    nf = F // FFN_TF
    tk = WGRAD_TK
    nk = S // tk
    gshape = (N_DEV, 3 * FFN_SHARD, D)

    def dw2_body(h_ref, d_ref, o_ref, acc_ref):
        k = pl.program_id(1)
        p = _dot(h_ref[...], d_ref[...].astype(BF16), 0, 0)

        @pl.when(k == 0)
        def _():
            acc_ref[...] = p

        @pl.when(k > 0)
        def _():
            acc_ref[...] += p

        @pl.when(k == nk - 1)
        def _():
            o_ref[...] = (0.5 * acc_ref[...]).astype(BF16).reshape(4, FFN_SHARD, D)

    gw = _call(name + "_dw2", dw2_body, (nf, nk),
               [(abh, (None, tk, FFN_TF), lambda j, k: (2, k, j)), (dxo, (tk, D), lambda j, k: (k, 0))],
               [(gshape, BF16, (4, FFN_SHARD, D), lambda j, k: (j, 2, 0))],
               scratch=[pltpu.VMEM((FFN_TF, D), F32)], sem=("parallel", "arbitrary"))[0]

    def dw13_body(gw_ref, dab_ref, n_ref, o_ref, acc_ref):
        k = pl.program_id(2)
        p = _dot(dab_ref[...], n_ref[...], 0, 0)

        @pl.when(k == 0)
        def _():
            acc_ref[...] = p

        @pl.when(k > 0)
        def _():
            acc_ref[...] += p

        @pl.when(k == nk - 1)
        def _():
            o_ref[...] = acc_ref[...].astype(BF16).reshape(4, FFN_SHARD, D)

    gw = pl.pallas_call(
        dw13_body,
        out_shape=jax.ShapeDtypeStruct(gshape, BF16),
        grid=(2, nf, nk),
        in_specs=[pl.BlockSpec(memory_space=pl.ANY),
                  pl.BlockSpec((None, tk, FFN_TF), lambda w, j, k: (w, k, j)),
                  pl.BlockSpec((tk, D), lambda w, j, k: (k, 0))],
        out_specs=pl.BlockSpec((4, FFN_SHARD, D), lambda w, j, k: (j, w, 0)),
        scratch_shapes=[pltpu.VMEM((FFN_TF, D), F32)],
        input_output_aliases={0: 0},
        name=name + "_dw13",
        compiler_params=pltpu.CompilerParams(dimension_semantics=("parallel", "parallel", "arbitrary"),
                                             vmem_limit_bytes=VMEM_LIMIT),
    )(gw, dab, n)
    return gw


def ffn_bwd_input(dab, G, x_in, g, dxo, name):
    _, S, F = dab.shape
    D = x_in.shape[1]
    tm = 256

    def dn_body(dab_ref, w1_ref, w3_ref, x_ref, d_ref, g_ref, dx_ref, dg_ref):
        i = pl.program_id(0)
        dn = _dot(dab_ref[0], w1_ref[...].reshape(F, D)) + _dot(dab_ref[1], w3_ref[...].reshape(F, D))
        dx, dgt = _rms_bwd_tile(dn, x_ref[...], g_ref[...])
        dx_ref[...] = d_ref[...] + dx
        dgp = jnp.sum(dgt, axis=0, keepdims=True)

        @pl.when(i == 0)
        def _():
            dg_ref[...] = dgp

        @pl.when(i > 0)
        def _():
            dg_ref[...] += dgp

    tile = ((tm, D), lambda i: (i, 0))
    return _call(name + "_dn", dn_body, (S // tm,),
                 [(dab, (2, tm, F), lambda i: (0, i, 0)),
                  _ffn_whole_w_spec(G, 0), _ffn_whole_w_spec(G, 1),
                  (x_in,) + tile, (dxo,) + tile, (g, (1, D), lambda i: (0, 0))],
                 [((S, D), F32) + tile, ((1, D), F32, (1, D), lambda i: (0, 0))],
                 sem=("arbitrary",))


PROJ_TN = 512


def in_proj(h, Gm, name):
    S, D = h.shape
    n_tiles = N_DEV * Gm.shape[2] // PROJ_TN

    def body(h_ref, w_ref, o_ref):
        o_ref[...] = _dot(h_ref[...], w_ref[...]).astype(BF16)

    return _call(name, body, (n_tiles,),
                 [(h, (S, D), lambda j: (0, 0)),
                  (Gm, (None, D, PROJ_TN), lambda j: (j // 2, 0, j % 2))],
                 [((S, n_tiles * PROJ_TN), BF16, (S, PROJ_TN), lambda j: (0, j))],
                 sem=("parallel",))[0]


def _dproj_pieces(dqkv, dq_b, dkv_b, dgate):
    pieces = [(dqkv[g], [(3 * which + g, (which, 0)) for which in range(3)]) for g in range(3)]
    pieces.append((dq_b, [(9, (None, 0)), (10, (None, 1))]))
    pieces.append((dkv_b, [(11, (None, 0))]))
    pieces.append((dgate, [(12 + 2 * a + b, (a, b)) for a in range(2) for b in range(2)]))
    return pieces


def in_proj_bwd_dw(pieces, h, gm_grads, name):
    S, D = h.shape
    steps = [(n, t, ix) for n, (_, tiles) in enumerate(pieces) for t, ix in tiles]
    n_steps = len(steps)

    def pick(table, j):
        out = table[-1]
        for k in range(len(table) - 2, -1, -1):
            out = jnp.where(j == k, table[k], out)
        return out

    def piece_spec(n, arr):
        own = [k for k, (m, _, _) in enumerate(steps) if m == n]
        at = [steps[min(max(k, own[0]), own[-1])][2] for k in range(n_steps)]
        lead, colb = [ix[0] for ix in at], [ix[1] for ix in at]
        if arr.ndim == 3:
            return (own[0], own[-1]), pl.BlockSpec((None, S, PROJ_TN), lambda j: (pick(lead, j), 0, pick(colb, j)))
        return (own[0], own[-1]), pl.BlockSpec((S, PROJ_TN), lambda j: (0, pick(colb, j)))

    spans, d_specs = zip(*[piece_spec(n, arr) for n, (arr, _) in enumerate(pieces)])
    w_tile = [t for _, t, _ in steps]

    def dw_body(gm_ref, h_ref, *refs):
        o_ref = refs[-1]
        j = pl.program_id(0)
        for d_ref, (first, last) in zip(refs[:-1], spans):
            @pl.when((j >= first) & (j <= last))
            def _(d_ref=d_ref):
                o_ref[...] = _dot(h_ref[...], d_ref[...], 0, 0).astype(BF16)

    return pl.pallas_call(
        dw_body,
        out_shape=jax.ShapeDtypeStruct(gm_grads.shape, BF16),
        grid=(n_steps,),
        in_specs=[pl.BlockSpec(memory_space=pl.ANY),
                  pl.BlockSpec((S, D), lambda j: (0, 0), pipeline_mode=pl.Buffered(1))] + list(d_specs),
        out_specs=pl.BlockSpec((None, D, PROJ_TN), lambda j: (pick(w_tile, j) // 2, 0, pick(w_tile, j) % 2)),
        input_output_aliases={0: 0},
        name=name + "_dw",
        compiler_params=pltpu.CompilerParams(dimension_semantics=("arbitrary",), vmem_limit_bytes=VMEM_LIMIT),
    )(gm_grads, h, *[arr for arr, _ in pieces])


def in_proj_bwd_dh(pieces, Gm, x_in, g, dres, name):
    S, D = x_in.shape
    tm = 256
    C = Gm.shape[2]
    n_sh = N_DEV
    n_p = len(pieces)

    def dh_body(*refs):
        d_refs = refs[:n_p]
        w_ref, x_ref, r_ref, g_ref, dx_ref, dxb_ref, dg_ref = refs[n_p:]
        i = pl.program_id(0)
        p = None
        for d_ref, (arr, tiles) in zip(d_refs, pieces):
            for t, (lead, colb) in tiles:
                cols = slice(colb * PROJ_TN, (colb + 1) * PROJ_TN)
                d = d_ref[:, cols] if lead is None else d_ref[lead, :, cols]
                wcol = (t % 2) * PROJ_TN
                term = _dot(d, w_ref[t // 2, :, wcol:wcol + PROJ_TN], 1, 1)
                p = term if p is None else p + term
        dx, dgt = _rms_bwd_tile(p, x_ref[...], g_ref[...])
        dx = r_ref[...] + dx
        dx_ref[...] = dx
        dxb_ref[...] = dx.astype(BF16)
        dgp = jnp.sum(dgt, axis=0, keepdims=True)

        @pl.when(i == 0)
        def _():
            dg_ref[...] = dgp

        @pl.when(i > 0)
        def _():
            dg_ref[...] += dgp

    tile = ((tm, D), lambda i: (i, 0))

    def rows_of(arr):
        if arr.ndim == 3:
            return (arr, (arr.shape[0], tm, arr.shape[2]), lambda i: (0, i, 0))
        return (arr, (tm, arr.shape[1]), lambda i: (i, 0))

    return _call(name + "_dh", dh_body, (S // tm,),
                 [rows_of(arr) for arr, _ in pieces]
                 + [(Gm, (n_sh, D, C), lambda i: (0, 0, 0), pl.Buffered(1)),
                    (x_in,) + tile, (dres,) + tile, (g, (1, D), lambda i: (0, 0))],
                 [((S, D), F32) + tile, ((S, D), BF16) + tile, ((1, D), F32, (1, D), lambda i: (0, 0))],
                 sem=("arbitrary",))


def _t5_bucket(rel):
    n = N_BUCKETS // 2
    max_exact = n // 2
    ret = jnp.where(rel > 0, n, 0)
    a = jnp.abs(rel)
    af = jnp.maximum(a, 1).astype(F32)
    large = max_exact + (jnp.log(af / max_exact) / math.log(MAX_DISTANCE / max_exact)
                         * (n - max_exact)).astype(jnp.int32)
    large = jnp.minimum(large, n - 1)
    return ret + jnp.where(a < max_exact, a, large)


def _bucket_tables():
    qi = jnp.arange(A_TQ, dtype=jnp.int32)[:, None]
    kj = jnp.arange(A_WIN, dtype=jnp.int32)[None, :]
    rel = kj - HALF_WINDOW - qi
    return jnp.stack([_t5_bucket(rel * d) for d in DILATIONS], axis=0)


def bias_build(rel_bias, buckets):
    def body(tab_ref, bk_ref, o_ref):
        col = pl.program_id(0) * HEADS_PER_GROUP_A + pl.program_id(1)
        bk = bk_ref[...]
        acc = jnp.zeros(bk.shape, F32)
        for b in range(N_BUCKETS):
            acc = jnp.where(bk == b, tab_ref[b, col], acc)
        qi = lax.broadcasted_iota(jnp.int32, bk.shape, 0)
        kj = lax.broadcasted_iota(jnp.int32, bk.shape, 1)
        band = jnp.where(jnp.abs(kj - HALF_WINDOW - qi) <= HALF_WINDOW, acc, NEG_INF)
        o_ref[0] = jnp.where(kj >= HALF_WINDOW, band, NEG_INF)
        o_ref[1] = band
        o_ref[2] = jnp.where(kj < A_TQ + HALF_WINDOW, band, NEG_INF)

    out = pl.pallas_call(
        body,
        out_shape=jax.ShapeDtypeStruct((3, HEADS_PER_GROUP_A // 2, 3, 2, A_TQ, A_WIN), F32),
        grid=(3, HEADS_PER_GROUP_A),
        in_specs=[pl.BlockSpec(memory_space=pltpu.SMEM),
                  pl.BlockSpec((None, A_TQ, A_WIN), lambda g, h: (g, 0, 0))],
        out_specs=pl.BlockSpec((None, None, 3, None, A_TQ, A_WIN), lambda g, h: (g, h // 2, 0, h % 2, 0, 0)),
        name="a_bias_build",
        compiler_params=pltpu.CompilerParams(dimension_semantics=("parallel", "parallel")),
    )(rel_bias, buckets)
    return out.reshape(3, HEADS_PER_GROUP_A // 2, 3, 2 * A_TQ, A_WIN)


def bias_bwd(dbias, buckets):
    def body(d_ref, bk_ref, o_ref):
        bk = bk_ref[...]
        for b in range(N_BUCKETS):
            mask = bk == b
            for h in range(HEADS_PER_GROUP_A):
                part = jnp.sum(jnp.where(mask, d_ref[h], 0.0), axis=1, keepdims=True)
                o_ref[h, b:b + 1, :] = jnp.broadcast_to(jnp.sum(part, axis=0, keepdims=True), (1, LANES))

    out = pl.pallas_call(
        body,
        out_shape=jax.ShapeDtypeStruct((3, HEADS_PER_GROUP_A, N_BUCKETS, LANES), F32),
        grid=(3,),
        in_specs=[pl.BlockSpec((None, HEADS_PER_GROUP_A, A_TQ, A_WIN), lambda g: (g, 0, 0, 0)),
                  pl.BlockSpec((None, A_TQ, A_WIN), lambda g: (g, 0, 0))],
        out_specs=pl.BlockSpec((None, HEADS_PER_GROUP_A, N_BUCKETS, LANES), lambda g: (g, 0, 0, 0)),
        name="a_bias_bwd",
        compiler_params=pltpu.CompilerParams(dimension_semantics=("parallel",)),
    )(dbias, buckets)
    return out[:, :, :, 0].transpose(2, 0, 1).reshape(N_BUCKETS, 3 * HEADS_PER_GROUP_A)


def _a_fill_padded(pad_ref, src_ref, n, pad):
    zeros = jnp.zeros((pad, LANES), pad_ref.dtype)
    pad_ref[0:pad, :] = zeros
    pad_ref[pad + n:2 * pad + n, :] = zeros
    pad_ref[pad:pad + n, :] = src_ref[...].astype(pad_ref.dtype)


def _a_stack_heads(x, lane):
    zero = jnp.zeros_like(x)
    return jnp.concatenate([jnp.where(lane < HEAD_DIM_A, x, zero), jnp.where(lane >= HEAD_DIM_A, x, zero)], axis=0)


def _a_bias_variant(qb, nqb):
    return jnp.where(qb == 0, 0, jnp.where(qb == nqb - 1, 2, 1))


def _a_slab_specs(proj, g):
    S = proj.shape[0]
    per = GROUP_WIDTH_A // LANES
    return [(proj, (S, LANES), lambda hp, w=w: (0, per * (3 * w + g) + hp)) for w in range(3)]


def a_fwd(proj, bias, g, name, others=None):
    S = proj.shape[0]
    d = DILATIONS[g]
    L = S // d
    nqb = L // A_TQ
    pad = HALF_WINDOW * d
    n_others = 0 if others is None else 4
    tr = 256

    def body(q_ref, k_ref, v_ref, b_ref, *refs):
        out1, out2, qf, kpad, vpad = refs[n_others:n_others + 5]
        o_ref, l_ref = refs[n_others + 5:] if others else (out1, out2)
        qf[...] = q_ref[...].astype(F32) * A_SCALE
        _a_fill_padded(kpad, k_ref, S, pad)
        _a_fill_padded(vpad, v_ref, S, pad)
        lane = lax.broadcasted_iota(jnp.int32, (A_TQ, LANES), 1)

        def block(t, carry):
            qb, r = t // d, t % d
            start = qb * (A_TQ * d) + r
            kw = kpad[pl.ds(start, A_WIN, stride=d), :].astype(BF16)
            vw = vpad[pl.ds(start, A_WIN, stride=d), :].astype(BF16)
            q = qf[pl.ds(start, A_TQ, stride=d), :].astype(BF16)
            q2 = _a_stack_heads(q, lane)
            s = _dot(q2, kw, 1, 1) + b_ref[_a_bias_variant(qb, nqb)]
            m = jnp.max(s, axis=-1, keepdims=True)
            e = jnp.exp(s - m)
            l = jnp.sum(e, axis=-1, keepdims=True)
            o2 = _dot(e.astype(BF16), vw) / l
            lse2 = m + jnp.log(l)
            o_ref[pl.ds(start, A_TQ, stride=d), :] = jnp.where(lane < HEAD_DIM_A, o2[0:A_TQ], o2[A_TQ:])
            l_ref[pl.ds(start, A_TQ, stride=d), :] = jnp.where(lane < HEAD_DIM_A, lse2[0:A_TQ], lse2[A_TQ:])
            return carry

        lax.fori_loop(0, nqb * d, block, 0, unroll=A_UNROLL)

        if others:
            o0, o1, l0, l1 = refs[:n_others]

            def combine(c, carry):
                rows = pl.ds(pl.multiple_of(c * tr, tr), tr)
                la, lb, lc = l0[rows, :], l1[rows, :], l_ref[rows, :]
                m = jnp.maximum(jnp.maximum(la, lb), lc)
                ea, eb, ec = jnp.exp(la - m), jnp.exp(lb - m), jnp.exp(lc - m)
                z = ea + eb + ec
                out1[rows, :] = ((ea * o0[rows, :] + eb * o1[rows, :] + ec * o_ref[rows, :]) / z).astype(BF16)
                out2[rows, :] = m + jnp.log(z)
                return carry

            lax.fori_loop(0, S // tr, combine, 0)

    slab = ((S, LANES), lambda hp: (0, hp))
    wide = (S, GROUP_WIDTH_A)
    other_ins = [(a,) + slab for a in (*others[0], *others[1])] if others else []
    return _call(name, body, (4,),
                 _a_slab_specs(proj, g)
                 + [(bias, (None, None, 3, 2 * A_TQ, A_WIN), lambda hp: (g, hp, 0, 0, 0))] + other_ins,
                 [(wide, BF16 if others else F32) + slab, (wide, F32) + slab],
                 scratch=[pltpu.VMEM((S, LANES), F32)] + [pltpu.VMEM((S + 2 * pad, LANES), F32)] * 2
                 + ([pltpu.VMEM((S, LANES), F32)] * 2 if others else []),
                 sem=("parallel",))


def a_bwd(proj, bias, do_a, o_a, lse_tot, g, name):
    S = proj.shape[0]
    d = DILATIONS[g]
    L = S // d
    nqb = L // A_TQ
    pad = HALF_WINDOW * d

    def body(q_ref, k_ref, v_ref, b_ref, do_ref, o_ref, l_ref, dqkv_ref, db_ref,
             qf, of, dqf, kpad, vpad, dkacc, dvacc):
        qf[...] = q_ref[...].astype(F32) * A_SCALE
        of[...] = o_ref[...].astype(F32)
        _a_fill_padded(kpad, k_ref, S, pad)
        _a_fill_padded(vpad, v_ref, S, pad)
        dkacc[...] = jnp.zeros(dkacc.shape, F32)
        dvacc[...] = jnp.zeros(dvacc.shape, F32)
        db_ref[...] = jnp.zeros(db_ref.shape, F32)
        lane = lax.broadcasted_iota(jnp.int32, (A_TQ, LANES), 1)

        def block(t, carry):
            qb, r = t // d, t % d
            start = qb * (A_TQ * d) + r
            rows = pl.ds(start, A_TQ, stride=d)
            win = pl.ds(start, A_WIN, stride=d)
            kw = kpad[win, :].astype(BF16)
            vw = vpad[win, :].astype(BF16)
            q = qf[rows, :].astype(BF16)
            do = do_ref[rows, :]
            ov = of[rows, :]
            lt = l_ref[rows, :]
            q2 = _a_stack_heads(q, lane)
            do2 = _a_stack_heads(do, lane)
            lt2 = jnp.concatenate([lt[:, 0:1], lt[:, HEAD_DIM_A:HEAD_DIM_A + 1]], axis=0)
            s = _dot(q2, kw, 1, 1) + b_ref[_a_bias_variant(qb, nqb)]
            p = jnp.exp(s - lt2)
            t = jnp.sum(do2 * jnp.concatenate([ov, ov], axis=0), axis=-1, keepdims=True)
            dob2 = do2.astype(BF16)
            ds = p * (_dot(dob2, vw, 1, 1) - t)
            db_ref[...] += ds
            dsb = ds.astype(BF16)
            dq2 = _dot(dsb, kw)
            dqf[rows, :] = jnp.where(lane < HEAD_DIM_A, dq2[0:A_TQ], dq2[A_TQ:]) * A_SCALE
            dkacc[win, :] += _dot(dsb, q2, 0, 0)
            dvacc[win, :] += _dot(p.astype(BF16), dob2, 0, 0)
            return carry

        lax.fori_loop(0, nqb * d, block, 0, unroll=A_UNROLL)
        dqkv_ref[0] = dqf[...].astype(BF16)
        dqkv_ref[1] = dkacc[pad:pad + S, :].astype(BF16)
        dqkv_ref[2] = dvacc[pad:pad + S, :].astype(BF16)

    slab = ((S, LANES), lambda hp: (0, hp))
    padded = pltpu.VMEM((S + 2 * pad, LANES), F32)
    return _call(
        name, body, (4,),
        _a_slab_specs(proj, g)
        + [(bias, (None, None, 3, 2 * A_TQ, A_WIN), lambda hp: (g, hp, 0, 0, 0)),
           (do_a,) + slab, (o_a,) + slab, (lse_tot,) + slab],
        [((3, S, GROUP_WIDTH_A), BF16, (3, S, LANES), lambda hp: (0, 0, hp)),
         ((4, 2 * A_TQ, A_WIN), F32, (None, 2 * A_TQ, A_WIN), lambda hp: (hp, 0, 0))],
        scratch=[pltpu.VMEM((S, LANES), F32)] * 3 + [padded] * 4,
        sem=("parallel",))


def _rope_tables(S):
    rows = S // GRID_W
    row = jnp.repeat(jnp.arange(rows, dtype=F32), GRID_W)
    col = jnp.tile(jnp.arange(GRID_W, dtype=F32), rows)
    n_freq = HEAD_DIM_B // 4
    freq = ROPE_THETA ** (-jnp.arange(n_freq, dtype=F32) / n_freq)
    ang = jnp.concatenate([row[:, None] * freq, col[:, None] * freq], axis=-1)
    cos, sin = jnp.cos(ang), jnp.sin(ang)
    return jnp.repeat(cos, 2, axis=-1), jnp.stack([-sin, sin], axis=-1).reshape(S, HEAD_DIM_B)


def _swap_pairs(y):
    lane = lax.broadcasted_iota(jnp.int32, y.shape, 1)
    return jnp.where(lane % 2 == 0, pltpu.roll(y, LANES - 1, 1), pltpu.roll(y, 1, 1))


def qkv_prep(proj, gains, cos_t, sin_t, name):
    S = proj.shape[0]
    ts = 256
    n_rot = N_HEADS_B + N_KV_B
    nh = n_rot + N_KV_B
    W = nh * LANES

    def body(x_ref, g_ref, c_ref, s_ref, o_ref):
        cv, sv = c_ref[...], s_ref[...]
        for hb in range(nh):
            cols = slice(hb * LANES, (hb + 1) * LANES)
            if hb < n_rot:
                xv = x_ref[:, cols].astype(F32)
                r = lax.rsqrt(jnp.mean(xv * xv, axis=-1, keepdims=True) + EPS)
                yv = xv * r * g_ref[:, cols]
                o_ref[:, cols] = (yv * cv + _swap_pairs(yv) * sv).astype(BF16)
            else:
                o_ref[:, cols] = x_ref[:, cols]

    return _call(name, body, (S // ts,),
                 [(proj, (ts, W), lambda i: (i, A_QKV_WIDTH // W)), (gains, (1, W), lambda i: (0, 0)),
                  (cos_t, (ts, LANES), lambda i: (i, 0)), (sin_t, (ts, LANES), lambda i: (i, 0))],
                 [((S, W), BF16, (ts, W), lambda i: (i, 0))],
                 sem=("parallel",))[0]


def qk_prep_bwd(dr, proj, col0, gain, cos_t, sin_t, name):
    S, W = dr.shape
    H = W // LANES
    ts = 256
    wx = math.gcd(W, col0)
    n_x = W // wx

    def body(d_ref, *refs):
        x_refs = refs[:n_x]
        g_ref, c_ref, s_ref, dx_ref, dg_ref = refs[n_x:]
        i = pl.program_id(0)
        cv, sv, gv = c_ref[...], s_ref[...], g_ref[...]
        dgp = jnp.zeros((1, LANES), F32)
        for hb in range(H):
            cols = slice(hb * LANES, (hb + 1) * LANES)
            xc = (hb * LANES) % wx
            xv = x_refs[(hb * LANES) // wx][:, xc:xc + LANES].astype(F32)
            dout = d_ref[:, cols]
            dy = dout * cv + _swap_pairs(dout * sv)
            dx, dgt = _rms_bwd_tile(dy, xv, gv)
            dx_ref[:, cols] = dx.astype(BF16)
            dgp = dgp + jnp.sum(dgt, axis=0, keepdims=True)

        @pl.when(i == 0)
        def _():
            dg_ref[...] = dgp

        @pl.when(i > 0)
        def _():
            dg_ref[...] += dgp

    return _call(name, body, (S // ts,),
                 [(dr, (ts, W), lambda i: (i, 0))]
                 + [(proj, (ts, wx), lambda i, k=k: (i, col0 // wx + k)) for k in range(n_x)]
                 + [(gain, (1, LANES), lambda i: (0, 0)),
                  (cos_t, (ts, LANES), lambda i: (i, 0)), (sin_t, (ts, LANES), lambda i: (i, 0))],
                 [((S, W), BF16, (ts, W), lambda i: (i, 0)),
                  ((1, LANES), F32, (1, LANES), lambda i: (0, 0))],
                 sem=("arbitrary",))


def _row_sums(x):
    hi = x.astype(BF16)
    lo = (x - hi.astype(F32)).astype(BF16)
    ones = jnp.ones((8, LANES), BF16)
    return (_dot(ones, hi, 1, 1) + _dot(ones, lo, 1, 1))[0:1, :]


def flash_fwd(qkv, name):
    S = qkv.shape[0]
    tq = B_TQ_FWD
    hps = B_HEADS_PER_STEP

    def body(q_ref, k_ref, v_ref, o_ref, l_ref):
        k, v = k_ref[...], v_ref[...]
        for j in range(hps):
            cols = slice(j * LANES, (j + 1) * LANES)
            s = _dot(q_ref[:, cols], k, 1, 1)
            m = jnp.max(s, axis=-1, keepdims=True)
            e = jnp.exp2(s - m)
            l = jnp.sum(e, axis=-1, keepdims=True)
            o_ref[:, cols] = (_dot(e.astype(BF16), v) / l).astype(BF16)
            lse = jnp.broadcast_to(m * (1.0 / LOG2E) + jnp.log(l), (tq, LANES))
            l_ref[j] = _row_sums(lse) * (1.0 / LANES)

    per = GQA_GROUP_B // hps
    heads = lambda g, h, i: (i, g * per + h)
    return _call(name, body, (N_KV_B, per, S // tq),
                 [(qkv, (tq, hps * LANES), heads),
                  (qkv, (S, LANES), lambda g, h, i: (0, N_HEADS_B + g)),
                  (qkv, (S, LANES), lambda g, h, i: (0, N_HEADS_B + N_KV_B + g))],
                 [((S, N_HEADS_B * LANES), BF16, (tq, hps * LANES), heads),
                  ((N_HEADS_B, 1, S), F32, (hps, 1, tq), lambda g, h, i: (g * per + h, 0, i))],
                 sem=("parallel", "parallel", "parallel"))


def flash_bwd(qkv, k_t, do_b, o_b, lse, name):
    S = qkv.shape[0]
    tq = B_TQ_BWD
    nq = S // tq
    scale = HEAD_DIM_B ** -0.5

    def body(q_ref, k_ref, v_ref, kt_ref, do_ref, o_ref, l_ref, dq_ref, dk_ref, dv_ref, dkacc, dvacc):
        h, i = pl.program_id(1), pl.program_id(2)

        @pl.when((h == 0) & (i == 0))
        def _():
            dkacc[...] = jnp.zeros(dkacc.shape, F32)
            dvacc[...] = jnp.zeros(dvacc.shape, F32)

        q = q_ref[...]
        dob = do_ref[...]
        t = _row_sums(dob.astype(F32) * o_ref[...].astype(F32))
        pt = jnp.exp2(_dot(k_ref[...], q, 1, 1) - l_ref[...] * LOG2E)
        dsb = (pt * (_dot(v_ref[...], dob, 1, 1) - t)).astype(BF16)
        dvacc[...] += _dot(pt.astype(BF16), dob)
        dkacc[...] += _dot(dsb, q)
        dq_ref[...] = _dot(kt_ref[...], dsb).T * scale

        @pl.when((h == GQA_GROUP_B - 1) & (i == nq - 1))
        def _():
            dk_ref[...] = dkacc[...] * (scale / B_Q_PRESCALE)
            dv_ref[...] = dvacc[...].astype(BF16)

    head = lambda g, h, i: (i, g * GQA_GROUP_B + h)
    return _call(name, body, (N_KV_B, GQA_GROUP_B, nq),
                 [(qkv, (tq, LANES), head),
                  (qkv, (S, LANES), lambda g, h, i: (0, N_HEADS_B + g)),
                  (qkv, (S, LANES), lambda g, h, i: (0, N_HEADS_B + N_KV_B + g)),
                  (k_t, (LANES, S), lambda g, h, i: (g, 0)),
                  (do_b, (tq, LANES), head), (o_b, (tq, LANES), head),
                  (lse, (None, 1, tq), lambda g, h, i: (g * GQA_GROUP_B + h, 0, i))],
                 [((S, N_HEADS_B * LANES), F32, (tq, LANES), head),
                  ((S, N_KV_B * LANES), F32, (S, LANES), lambda g, h, i: (0, g)),
                  ((S, N_KV_B * LANES), BF16, (S, LANES), lambda g, h, i: (0, g))],
                 scratch=[pltpu.VMEM((S, LANES), F32)] * 2,
                 sem=("parallel", "arbitrary", "arbitrary"))


MERGE_TN = 512


def _mix_rows_spec(Gm, row0, n_slots, slot_map, cols=None, col_map=None):
    C = Gm.shape[2] if cols is None else cols
    cm = (lambda *idx: 0) if col_map is None else col_map
    return (Gm, (n_slots, LANES, C), lambda *idx: (slot_map(*idx), row0 // LANES, cm(*idx)))


def _gate_specs(proj, tm):
    first = (A_QKV_WIDTH + PB_GATE_A) // MERGE_TN
    return [(proj, (tm, MERGE_TN), lambda i, k=k: (i, first + k)) for k in range(4)]


def _whole_rows_spec(Gm, row0):
    return _mix_rows_spec(Gm, row0, N_DEV, lambda *idx: 0)


def merge_fwd(o_a, o_b, w_a, Gm, proj, b_gate, x, name):
    S, D = x.shape
    tm = 256

    def body(oa_ref, ob_ref, wa_ref, wb_ref, wo_ref, g0, g1, g2, g3, bg_ref, x_ref, m_ref, ya_ref, yb_ref, xo_ref):
        ya = _dot(oa_ref[...], wa_ref[...])
        yb = _dot(ob_ref[...], wb_ref[...].reshape(N_DEV * LANES, D))
        ga = _sigmoid(jnp.concatenate([g0[...], g1[...]], axis=1).astype(F32) + bg_ref[:, 0:D])
        gb = _sigmoid(jnp.concatenate([g2[...], g3[...]], axis=1).astype(F32) + bg_ref[:, D:2 * D])
        merged = (ga * ya + gb * yb).astype(BF16)
        m_ref[...] = merged
        ya_ref[...] = ya.astype(BF16)
        yb_ref[...] = yb.astype(BF16)
        xo_ref[...] = x_ref[...] + _dot(merged, wo_ref[...].reshape(N_DEV * LANES, D))

    rows = lambda a: (a, (tm, a.shape[1]), lambda i: (i, 0))
    out = ((S, D), BF16, (tm, D), lambda i: (i, 0))
    return _call(name, body, (S // tm,),
                 [rows(o_a), rows(o_b), (w_a, w_a.shape, lambda i: (0, 0)),
                  _whole_rows_spec(Gm, REST_WB), _whole_rows_spec(Gm, REST_WOUT)]
                 + _gate_specs(proj, tm) + [(b_gate, (1, 2 * D), lambda i: (0, 0)), rows(x)],
                 [out, out, out, ((S, D), F32, (tm, D), lambda i: (i, 0))], sem=("parallel",))


def merge_bwd(dx2, w_a, Gm, ya, yb, proj, b_gate, name):
    S, D = dx2.shape
    tm = 256

    def body(d_ref, wo_ref, wa_ref, wb_ref, ya_ref, yb_ref, g0, g1, g2, g3, bg_ref,
             dya_ref, dyb_ref, dg_ref, dbg_ref, doa_ref, dob_ref):
        i = pl.program_id(0)
        dm = _dot(d_ref[...].astype(BF16), wo_ref[...].reshape(N_DEV * LANES, D), 1, 1)
        ga = _sigmoid(jnp.concatenate([g0[...], g1[...]], axis=1).astype(F32) + bg_ref[:, 0:D])
        gb = _sigmoid(jnp.concatenate([g2[...], g3[...]], axis=1).astype(F32) + bg_ref[:, D:2 * D])
        dya = (dm * ga).astype(BF16)
        dyb = (dm * gb).astype(BF16)
        dya_ref[...] = dya
        dyb_ref[...] = dyb
        dpa = dm * ya_ref[...].astype(F32) * ga * (1.0 - ga)
        dpb = dm * yb_ref[...].astype(F32) * gb * (1.0 - gb)
        dg_ref[0] = dpa.astype(BF16)
        dg_ref[1] = dpb.astype(BF16)
        doa_ref[...] = _dot(dya, wa_ref[...], 1, 1)
        dob_ref[...] = _dot(dyb, wb_ref[...].reshape(N_DEV * LANES, D), 1, 1).astype(BF16)
        sa =jnp.sum(dpa, axis=0, keepdims=True)
        sb = jnp.sum(dpb, axis=0, keepdims=True)

        @pl.when(i == 0)
        def _():
            dbg_ref[0] = sa
            dbg_ref[1] = sb

        @pl.when(i > 0)
        def _():
            dbg_ref[0] += sa
            dbg_ref[1] += sb

    tile = ((tm, D), lambda i: (i, 0))
    return _call(
        name, body, (S // tm,),
        [(dx2,) + tile, _whole_rows_spec(Gm, REST_WOUT), (w_a, w_a.shape, lambda i: (0, 0)),
         _whole_rows_spec(Gm, REST_WB), (ya,) + tile, (yb,) + tile]
        + _gate_specs(proj, tm) + [(b_gate, (1, 2 * D), lambda i: (0, 0))],
        [((S, D), BF16) + tile, ((S, D), BF16) + tile,
         ((2, S, D), BF16, (2, tm, D), lambda i: (0, i, 0)),
         ((2, 1, D), F32, (2, 1, D), lambda i: (0, 0, 0)),
         ((S, w_a.shape[0]), F32, (tm, w_a.shape[0]), lambda i: (i, 0)),
         ((S, N_HEADS_B * LANES), BF16, (tm, N_HEADS_B * LANES), lambda i: (i, 0))],
        sem=("arbitrary",))


def weight_grad_rows(a, b, grads, row0, name):
    S, M = a.shape
    N = b.shape[1]
    tmm = 512
    tk = WGRAD_TK
    nk = S // tk
    prior = [] if grads is None else [grads]

    def body(*refs):
        a_ref, b_ref, o_ref, acc_ref = refs[len(prior):]
        k = pl.program_id(1)
        p = _dot(a_ref[...], b_ref[...].astype(BF16), 0, 0)

        @pl.when(k == 0)
        def _():
            acc_ref[...] = p

        @pl.when(k > 0)
        def _():
            acc_ref[...] += p

        @pl.when(k == nk - 1)
        def _():
            o_ref[...] = acc_ref[...].astype(BF16).reshape(tmm // LANES, LANES, N)

    return pl.pallas_call(
        body,
        out_shape=jax.ShapeDtypeStruct((N_DEV, MIX_ROWS, N), BF16),
        grid=(M // tmm, nk),
        in_specs=[pl.BlockSpec(memory_space=pl.ANY)] * len(prior)
        + [pl.BlockSpec((tk, tmm), lambda j, k: (k, j)),
           pl.BlockSpec((tk, N), lambda j, k: (k, 0))],
        out_specs=pl.BlockSpec((tmm // LANES, LANES, N), lambda j, k: (j, row0 // LANES, 0)),
        scratch_shapes=[pltpu.VMEM((tmm, N), F32)],
        input_output_aliases={0: 0} if prior else {},
        name=name,
        compiler_params=pltpu.CompilerParams(dimension_semantics=("parallel", "arbitrary"),
                                             vmem_limit_bytes=VMEM_LIMIT),
    )(*prior, a, b)


def weight_grad_plain(a, b, name):
    S, M = a.shape
    N = b.shape[1]
    tk = WGRAD_TK
    nk = S // tk

    def body(a_ref, b_ref, o_ref, acc_ref):
        k = pl.program_id(0)
        p = _dot(a_ref[...], b_ref[...], 0, 0)

        @pl.when(k == 0)
        def _():
            acc_ref[...] = p

        @pl.when(k > 0)
        def _():
            acc_ref[...] += p

        @pl.when(k == nk - 1)
        def _():
            o_ref[...] = acc_ref[...].astype(BF16)

    return _call(name, body, (nk,),
                 [(a, (tk, M), lambda k: (k, 0)), (b, (tk, N), lambda k: (k, 0))],
                 [((M, N), BF16, (M, N), lambda k: (0, 0))],
                 scratch=[pltpu.VMEM((M, N), F32)], sem=("arbitrary",))[0]


def local_step(x, tgt, p, get_g1_up, get_g1_down, get_gm_in, get_gm_rest, get_g2, emit, start_token):
    S, D = x.shape
    after = lambda t: t[0:1, 0:1]
    buckets = _bucket_tables()
    cos_t, sin_t = _rope_tables(S)
    gains = jnp.concatenate([jnp.tile(p["q_norm"] * B_Q_PRESCALE, (1, N_HEADS_B)), jnp.tile(p["k_norm"], (1, N_KV_B)),
                             jnp.ones((1, N_KV_B * LANES), F32)], axis=1)

    n1 = rms_fwd(x, p["ffn1_norm"] + after(start_token), "ffn1_norm")
    bias = bias_build(p["rel_bias"] + after(start_token), buckets)
    g1_up = get_g1_up((n1, bias))
    ab1 = ffn_up(n1, (g1_up, None), "ffn1_up")
    G1 = (g1_up, get_g1_down(ab1))
    x1, hm = ffn_down(ab1, G1, x, p["mix_norm"], "ffn1_down")
    Gw = get_gm_in(hm)
    proj = in_proj(hm, Gw, "in_proj")

    outs, lses = zip(*[a_fwd(proj, bias, g, "a_fwd_%d" % g) for g in range(2)])
    o_a, lse_tot = a_fwd(proj, bias, 2, "a_fwd_2", (outs, lses))

    qkv = qkv_prep(proj, gains, cos_t, sin_t, "qkv_prep")
    k_t = qkv[:, N_HEADS_B * LANES:(N_HEADS_B + N_KV_B) * LANES].T
    o_b, lse_b = flash_fwd(qkv, "flash_fwd")

    Gm = get_gm_rest(o_b)
    w_a = Gm[:, REST_WA:REST_ROWS, :].reshape(N_DEV, GROUP_WIDTH_A, LANES).transpose(1, 0, 2).reshape(GROUP_WIDTH_A, D)
    merged, ya, yb, x2 = merge_fwd(o_a, o_b, w_a, Gm, proj, p["b_gate"], x1, "merge_fwd")

    G2 = get_g2(x2)
    n2, ab2, dx3_b, dab2, dx2, dx2_b, d_ffn2_norm, loss, d_final = ffn_last(
        x2, p["ffn2_norm"], G2, tgt, p["final_norm"], "ffn2")
    gw2 = ffn_bwd_weights(dx3_b, ab2, dab2, n2, "ffn2_bwd")
    t2 = emit("ffn2", gw2)

    dya, dyb, dgate, dbg, do_a, do_b = merge_bwd(dx2_b, w_a, Gm, ya, yb, proj, p["b_gate"] + after(t2),
                                                 "merge_bwd")
    gm_grads = weight_grad_rows(merged, dx2_b, None, MIX_WOUT, "dw_out")
    gm_grads = weight_grad_rows(o_b, dyb, gm_grads, MIX_WB, "dw_branch_b")
    dw_a = weight_grad_plain(o_a, dya, "dw_branch_a")

    dq_r, dk_r, dv_b = flash_bwd(qkv, k_t, do_b, o_b, lse_b, "flash_bwd")
    dq_b, d_q_norm = qk_prep_bwd(dq_r, proj, A_QKV_WIDTH, p["q_norm"], cos_t, sin_t, "q_prep_bwd")
    dk_b, d_k_norm = qk_prep_bwd(dk_r, proj, A_QKV_WIDTH + N_HEADS_B * LANES, p["k_norm"], cos_t, sin_t,
                                 "k_prep_bwd")

    dqkv, dbs = [], []
    for g in range(3):
        dg_, db = a_bwd(proj, bias, do_a, o_a, lse_tot, g, "a_bwd_%d" % g)
        dqkv.append(dg_)
        dbs.append(db)
    d_rel_bias = bias_bwd(jnp.stack(dbs, axis=0).reshape(3, HEADS_PER_GROUP_A, A_TQ, A_WIN), buckets)

    dproj = _dproj_pieces(dqkv, dq_b, jnp.concatenate([dk_b, dv_b], axis=1), dgate)
    gm_grads = in_proj_bwd_dw(dproj[:3], hm, gm_grads, "in_proj_bwd_a")
    gm_grads = in_proj_bwd_dw(dproj[3:], hm, gm_grads, "in_proj_bwd_b")
    dw_a_sh = dw_a.reshape(GROUP_WIDTH_A, N_DEV, LANES).transpose(1, 0, 2).reshape(N_DEV, MIX_ROWS - MIX_WA, D)
    gm_grads = lax.dynamic_update_slice(gm_grads, dw_a_sh, (0, MIX_WA, 0))
    tm = emit("mix", gm_grads)
    dx1, dx1_b, d_mix_norm = in_proj_bwd_dh(dproj, Gw, x1, p["mix_norm"] + after(tm), dx2, "in_proj_bwd")

    dab1 = ffn_bwd_hidden(dx1_b, ab1, G1, "ffn1_bwd")
    gw1 = ffn_bwd_weights(dx1_b, ab1, dab1, n1, "ffn1_bwd")
    t1 = emit("ffn1", gw1)
    dx0, d_ffn1_norm = ffn_bwd_input(dab1, G1, x, p["ffn1_norm"] + after(t1), dx1, "ffn1_bwd")

    small = dict(ffn1_norm=d_ffn1_norm, mix_norm=d_mix_norm, b_gate=dbg.reshape(1, 2 * D),
                 q_norm=d_q_norm, k_norm=d_k_norm, rel_bias=d_rel_bias, ffn2_norm=d_ffn2_norm,
                 final_norm=d_final)
    return loss, dx0, small


def _pack_small(t, loss_row):
    row6 = jnp.concatenate([t["q_norm"].reshape(1, -1), t["k_norm"].reshape(1, -1), t["rel_bias"].reshape(1, -1)], axis=1)
    return jnp.concatenate([t["ffn1_norm"].reshape(1, -1), t["mix_norm"].reshape(1, -1), t["b_gate"].reshape(2, -1),
                            t["ffn2_norm"].reshape(1, -1), t["final_norm"].reshape(1, -1), row6, loss_row], axis=0)


def _unpack_small(a, shapes):
    return dict(ffn1_norm=a[0:1].reshape(shapes["ffn1_norm"]), mix_norm=a[1:2].reshape(shapes["mix_norm"]),
                b_gate=a[2:4].reshape(shapes["b_gate"]), ffn2_norm=a[4:5].reshape(shapes["ffn2_norm"]),
                final_norm=a[5].reshape(shapes["final_norm"]), q_norm=a[6:7, 0:128].reshape(shapes["q_norm"]),
                k_norm=a[6:7, 128:256].reshape(shapes["k_norm"]), rel_bias=a[6, 256:1024].reshape(shapes["rel_bias"]))


SMALL = ("ffn1_norm", "mix_norm", "b_gate", "q_norm", "k_norm", "rel_bias", "ffn2_norm", "final_norm")
ORDER = ("ffn1_norm", "ffn1_w1", "ffn1_w3", "ffn1_w2", "mix_norm", "w_in", "b_gate", "q_norm", "k_norm", "rel_bias",
         "w_branch_a", "w_branch_b", "w_out", "ffn2_norm", "ffn2_w1", "ffn2_w3", "ffn2_w2", "final_norm")


def kernel(x, ffn1_norm, ffn1_w1, ffn1_w3, ffn1_w2, mix_norm, w_in, b_gate, q_norm, k_norm, rel_bias, w_branch_a, w_branch_b, w_out, ffn2_norm, ffn2_w1, ffn2_w3, ffn2_w2, final_norm, loss_target, m_ffn1_norm, m_ffn1_w1, m_ffn1_w3, m_ffn1_w2, m_mix_norm, m_w_in, m_b_gate, m_q_norm, m_k_norm, m_rel_bias, m_w_branch_a, m_w_branch_b, m_w_out, m_ffn2_norm, m_ffn2_w1, m_ffn2_w3, m_ffn2_w2, m_final_norm, v_ffn1_norm, v_ffn1_w1, v_ffn1_w3, v_ffn1_w2, v_mix_norm, v_w_in, v_b_gate, v_q_norm, v_k_norm, v_rel_bias, v_w_branch_a, v_w_branch_b, v_w_out, v_ffn2_norm, v_ffn2_w1, v_ffn2_w3, v_ffn2_w2, v_final_norm):
    args = dict(locals())
    w = {n: args[n] for n in ORDER}
    m = {n: args["m_" + n] for n in ORDER}
    v = {n: args["v_" + n] for n in ORDER}
    D = x.shape[2]

    blocks = (
        ("ffn1_up", jnp.concatenate([ffn1_w1[0].T, ffn1_w3[0].T], axis=0)),
        ("ffn1_down", ffn1_w2[0]),
        ("mix_in", w_in[0]),
        ("mix_rest", jnp.concatenate([w_branch_b[0], w_out[0], w_branch_a[0].reshape(REST_ROWS - REST_WA, D)], axis=0)),
        ("ffn2", jnp.concatenate([ffn2_w1[0].T, ffn2_w3[0].T, ffn2_w2[0]], axis=0)),
    )
    direct = ("mix_rest", "ffn2")
    started = all_gather_start_all([(b.astype(BF16), tag in direct) for tag, b in blocks], "all_gather_start")
    gathers = {tag: s for (tag, _), s in zip(blocks, started)}
    start_token = started[0][4]

    def gathered(tag):
        def get(after):
            if tag in direct:
                return all_gather_place_own(*_split_wait("all_gather_" + tag + "_wait", gathers[tag], N_DEV - 1, after),
                                            "all_gather_" + tag + "_own")
            return all_gather_finish(*_split_wait("all_gather_" + tag + "_wait", gathers[tag], 4, after),
                                     "all_gather_" + tag + "_finish")
        return get

    core = lax.axis_index("c").astype(jnp.int32).reshape(1)
    chip = (2 * lax.axis_index("x") + lax.axis_index("y")).astype(jnp.int32).reshape(1)
    device = 2 * chip + core
    exchanges = {}

    def emit(tag, gw):
        if tag == "ffn1":
            (theirs,) = reduce_scatter_pair([gw], "reduce_scatter_pair_" + tag)
            part = pair_add(gw, theirs, core, "pair_add_" + tag)
            exchanges[tag] = reduce_scatter_start(part, "reduce_scatter_" + tag + "_start")
        else:
            exchanges[tag] = reduce_scatter_start_direct(gw, "reduce_scatter_" + tag + "_start")
        return exchanges[tag][4]

    small_p = dict(ffn1_norm=ffn1_norm, mix_norm=mix_norm, b_gate=b_gate, q_norm=q_norm, k_norm=k_norm,
                   rel_bias=rel_bias, ffn2_norm=ffn2_norm, final_norm=final_norm.reshape(1, D))
    loss_p, grad_x, small_g = local_step(x[0], loss_target[0], small_p, gathered("ffn1_up"), gathered("ffn1_down"),
                                         gathered("mix_in"), gathered("mix_rest"), gathered("ffn2"), emit, start_token)

    def landed(tag, after):
        n_others, me = (3, chip) if tag == "ffn1" else (N_DEV - 1, device)
        return tuple(_split_wait("reduce_scatter_" + tag + "_wait", exchanges[tag], n_others, after)) + (me,)

    grads, delta, new_m, new_v = {}, {}, {}, {}

    def finish(n, part, land, me, off, blk, transposed=False):
        shp = w[n].shape
        if transposed:
            to2 = lambda a: a.reshape(shp[-2], shp[-1]).T
            back = lambda a: a.T.reshape(shp)
        else:
            to2 = lambda a: a.reshape(shp[-2], shp[-1])
            back = lambda a: a.reshape(shp)
        res = sum_adamw(part, land, me, off, blk, to2(w[n]), to2(m[n]), to2(v[n]), "update_" + n)
        grads[n], delta[n], new_m[n], new_v[n] = [back(a) for a in res]

    last_token = exchanges["ffn1"][4]
    for tag, after in (("ffn2", last_token), ("ffn1", grad_x)):
        group = landed(tag, after)
        finish(tag + "_w1", *group, 0, FFN_SHARD, transposed=True)
        finish(tag + "_w3", *group, FFN_SHARD, FFN_SHARD, transposed=True)
        finish(tag + "_w2", *group, 2 * FFN_SHARD, FFN_SHARD)
        if tag == "ffn2":
            group_m = landed("mix", last_token)
            finish("w_in", *group_m, MIX_WIN, LANES)
            finish("w_branch_b", *group_m, MIX_WB, LANES)
            finish("w_out", *group_m, MIX_WOUT, LANES)
            grads["w_branch_a"] = sum_landed(*group_m, MIX_WA, MIX_ROWS - MIX_WA, MIX_ROWS - MIX_WA,
                                             "w_branch_a_sum").reshape(w_branch_a.shape)
    loss_row = jnp.pad(loss_p, ((0, 0), (0, D - LANES)))
    smalls = small_all_gather(_pack_small(small_g, loss_row), new_v["w_in"])
    small_sum = sum_slots(smalls, 0, N_DEV, N_DEV, "small_sum")
    small_shapes = {n: w[n].shape for n in SMALL}
    grads.update(_unpack_small(small_sum, small_shapes))
    loss = small_sum[7, 0]

    n = "w_branch_a"
    two_d = lambda a: a.reshape(w[n].shape[-2], w[n].shape[-1])
    d_, m_, v_ = adamw(two_d(w[n]), two_d(grads[n]), two_d(m[n]), two_d(v[n]), "adamw_" + n)
    delta[n], new_m[n], new_v[n] = [a.reshape(w[n].shape) for a in (d_, m_, v_)]
    zero_row = jnp.zeros((1, D), F32)
    pack = lambda t: _pack_small({n: t[n] for n in SMALL}, zero_row)
    d_, m_, v_ = adamw(pack(w), small_sum, pack(m), pack(v), "adamw_small")
    for src, dst in ((d_, delta), (m_, new_m), (v_, new_v)):
        dst.update(_unpack_small(src, small_shapes))

    return (loss, grad_x[None], *[grads[n] for n in ORDER], *[delta[n] for n in ORDER],
            *[new_m[n] for n in ORDER], *[new_v[n] for n in ORDER])
```

```python
import math

import jax
import jax.numpy as jnp
from jax import lax
from jax.experimental import pallas as pl
from jax.experimental.pallas import tpu as pltpu

F32 = jnp.float32
BF16 = jnp.bfloat16
MESH = pl.DeviceIdType.MESH

V7X_VMEM_BYTES = 64 * 1024 * 1024
VMEM_LIMIT = V7X_VMEM_BYTES - 8 * 1024 * 1024
LANES = 128

N_DEV = 8
EPS = 1e-6
NEG_INF = -1e30

DILATIONS = (1, 4, 16)
HALF_WINDOW = 64
HEAD_DIM_A = 64
HEADS_PER_GROUP_A = 8
GROUP_WIDTH_A = 512
A_QKV_WIDTH = 4608
A_TQ = 128
A_WIN = A_TQ + 2 * HALF_WINDOW
A_UNROLL = 8
A_SCALE = HEAD_DIM_A ** -0.5
WGRAD_TK = 2048
HEAD_DIM_B = 128
N_HEADS_B = 8
N_KV_B = 2
GQA_GROUP_B = 4
GRID_W = 64
ROPE_THETA = 10000.0
B_TQ_FWD = 256
B_TQ_BWD = 512
B_HEADS_PER_STEP = 4
LOG2E = 1.4426950408889634
B_Q_PRESCALE = HEAD_DIM_B ** -0.5 * LOG2E
N_BUCKETS = 32
MAX_DISTANCE = 1024
PB_GATE_A = 1536

ADAM_LR = 0.001
ADAM_B1 = 0.9
ADAM_B2 = 0.999
ADAM_EPS = 1e-08
ADAM_WD = 0.01
ADAM_STEP = 10

FFN_SHARD = 352
MIX_WIN, MIX_WB, MIX_WOUT, MIX_WA = 0, 1024, 1152, 1280
MIX_ROWS = 1344
REST_WB, REST_WOUT, REST_WA, REST_ROWS = 0, 128, 256, 320


def _dot(a, b, ca=1, cb=0):
    return lax.dot_general(a, b, (((ca,), (cb,)), ((), ())), preferred_element_type=F32)


def _call(name, body, grid, ins, outs, scratch=(), sem=None, aliases=None):
    ins = [tuple(i) + (None,) * (4 - len(i)) for i in ins]
    res = pl.pallas_call(
        body,
        out_shape=[jax.ShapeDtypeStruct(s, d) for (s, d, _, _) in outs],
        grid=grid,
        in_specs=[pl.BlockSpec(bs, im, pipeline_mode=pm) for (_, bs, im, pm) in ins],
        out_specs=[pl.BlockSpec(bs, im) for (_, _, bs, im) in outs],
        scratch_shapes=list(scratch),
        name=name,
        input_output_aliases=aliases or {},
        compiler_params=pltpu.CompilerParams(dimension_semantics=sem, vmem_limit_bytes=VMEM_LIMIT),
    )(*[i[0] for i in ins])
    return res


def _sigmoid(x):
    return 0.5 * jnp.tanh(0.5 * x) + 0.5


def _position():
    return lax.axis_index("x"), lax.axis_index("y"), lax.axis_index("c")


def _hbm_specs(n):
    return [pl.BlockSpec(memory_space=pl.ANY) for _ in range(n)]


PAIR_BUFFERS = 4


def reduce_scatter_pair(grads, name):
    n = len(grads)
    C = grads[0].shape[2]
    half = [g.shape[1] // 2 for g in grads]
    chunks = [(i, q, hf) for i in range(n) for q in range(4) for hf in range(2)]
    nb = PAIR_BUFFERS

    def body(*refs):
        ins, theirs = refs[:n], refs[n:2 * n]
        buf, load_sems, send_sems, recv_sems = refs[2 * n:]
        x, y, c = _position()
        sibling = (x, y, 1 - c)

        def load(k):
            i, q, hf = chunks[k]
            r = half[i]
            return pltpu.make_async_copy(ins[i].at[2 * q + (1 - c), pl.ds(hf * r, r), :],
                                         buf.at[k % nb, pl.ds(0, r), :], load_sems.at[k % nb])

        def send(k):
            i, q, hf = chunks[k]
            r = half[i]
            return pltpu.make_async_remote_copy(
                src_ref=buf.at[k % nb, pl.ds(0, r), :], dst_ref=theirs[i].at[q, pl.ds(hf * r, r), :],
                send_sem=send_sems.at[k % nb], recv_sem=recv_sems.at[i],
                device_id=sibling, device_id_type=MESH)

        for k in range(len(chunks) + 1):
            if k < len(chunks):
                if k >= nb:
                    send(k - nb).wait_send()
                load(k).start()
            if k >= 1:
                load(k - 1).wait()
                send(k - 1).start()
        for k in range(max(0, len(chunks) - nb), len(chunks)):
            send(k).wait_send()
        for i in range(n):
            pltpu.make_async_remote_copy(
                src_ref=theirs[i], dst_ref=theirs[i], send_sem=send_sems.at[0], recv_sem=recv_sems.at[i],
                device_id=sibling, device_id_type=MESH).wait_recv()

    return pl.pallas_call(
        body,
        out_shape=[jax.ShapeDtypeStruct((4,) + g.shape[1:], g.dtype) for g in grads],
        in_specs=_hbm_specs(n),
        out_specs=_hbm_specs(n),
        scratch_shapes=[pltpu.VMEM((nb, max(half), C), grads[0].dtype), pltpu.SemaphoreType.DMA((nb,)),
                        pltpu.SemaphoreType.DMA((nb,)), pltpu.SemaphoreType.DMA((n,))],
        name=name,
        compiler_params=pltpu.CompilerParams(vmem_limit_bytes=VMEM_LIMIT),
    )(*grads)


_HBM_SPEC = pl.BlockSpec(memory_space=pltpu.HBM)
_SEM_SPEC = pl.BlockSpec(memory_space=pltpu.SEMAPHORE)
_TOKEN_SPEC = pl.BlockSpec(memory_space=pltpu.VMEM)
_DATAFLOW = pltpu.SideEffectType.DATAFLOW_SIDE_EFFECTING


def _split_start_many(name, exchanges):
    n = len(exchanges)

    def full_body(*refs):
        srcs, lands = refs[:n], refs[n:2 * n]
        sems = refs[2 * n:4 * n]
        token = refs[-1]
        for i, (body, _, _) in enumerate(exchanges):
            body(srcs[i], lands[i], sems[2 * i], sems[2 * i + 1])
        token[...] = jnp.zeros_like(token)

    srcs = [pltpu.with_memory_space_constraint(src, pltpu.HBM) for _, src, _ in exchanges]
    lands = [pltpu.with_memory_space_constraint(lax.empty(shape, src.dtype), pltpu.HBM)
             for _, src, shape in exchanges]
    res = pl.pallas_call(
        full_body, name=name,
        out_shape=(pltpu.SemaphoreType.DMA(()),) * (2 * n)
        + tuple(pltpu.HBM(a.shape, a.dtype) for a in srcs + lands) + (jax.ShapeDtypeStruct((8, LANES), F32),),
        in_specs=(_HBM_SPEC,) * (2 * n),
        out_specs=(_SEM_SPEC,) * (2 * n) + (_HBM_SPEC,) * (2 * n) + (_TOKEN_SPEC,),
        input_output_aliases={i: 2 * n + i for i in range(2 * n)},
        compiler_params=pltpu.CompilerParams(has_side_effects=_DATAFLOW),
    )(*srcs, *lands)
    return [(res[2 * i], res[2 * i + 1], res[2 * n + i], res[3 * n + i], res[-1]) for i in range(n)]


def _split_start(name, body, src, land_shape):
    return _split_start_many(name, [(body, src, land_shape)])[0]


def _split_wait(name, started, n_blocks, after):
    send_sem, recv_sem, src_thru, land_thru, _ = started
    after = after if isinstance(after, tuple) else (after,)

    def body(src_ref, land_ref, send_sem, recv_sem, *rest):
        x, y, c = _position()
        blocks = land_ref.at[pl.ds(0, n_blocks)]
        copy = pltpu.make_async_remote_copy(src_ref=blocks, dst_ref=blocks, send_sem=send_sem, recv_sem=recv_sem,
                                            device_id=(x, y, c), device_id_type=MESH)
        copy.wait_send()
        copy.wait_recv()

    return pl.pallas_call(
        body, name=name,
        out_shape=(pltpu.HBM(src_thru.shape, src_thru.dtype), pltpu.HBM(land_thru.shape, land_thru.dtype)),
        in_specs=(_HBM_SPEC, _HBM_SPEC, _SEM_SPEC, _SEM_SPEC) + (pl.BlockSpec(memory_space=pl.ANY),) * len(after),
        out_specs=(_HBM_SPEC, _HBM_SPEC),
        input_output_aliases={0: 0, 1: 1},
        compiler_params=pltpu.CompilerParams(has_side_effects=_DATAFLOW),
    )(src_thru, land_thru, send_sem, recv_sem, *after)


def all_gather_start_all(blocks, name):
    def starter(direct):
        def body(b_ref, land_ref, send_sem, recv_sem):
            x, y, c = _position()
            peers = _other_devices(x, y, c) if direct else [(x, y, 1 - c), (1 - x, y, c), (x, 1 - y, c),
                                                            (1 - x, 1 - y, c)]
            for peer in peers:
                pltpu.make_async_remote_copy(src_ref=b_ref, dst_ref=land_ref.at[4 * x + 2 * y + c],
                                             send_sem=send_sem, recv_sem=recv_sem,
                                             device_id=peer, device_id_type=MESH).start()
        return body

    return _split_start_many(name, [(starter(direct), block, (N_DEV,) + block.shape) for block, direct in blocks])


def all_gather_finish(block, land, name):
    R, C = block.shape

    def body(b_ref, land_in, land_ref, stage, load_sems, send_sems, recv_sems, own_sem):
        x, y, c = _position()
        sibling = (x, y, 1 - c)
        chips = [(1 - x, y), (x, 1 - y), (1 - x, 1 - y)]
        own_in = pltpu.make_async_copy(b_ref, stage.at[3], load_sems.at[3])
        own_in.start()
        loads = [pltpu.make_async_copy(land_in.at[4 * px + 2 * py + c], stage.at[j], load_sems.at[j])
                 for j, (px, py) in enumerate(chips)]
        for ld in loads:
            ld.start()
        sends = []
        for j, (px, py) in enumerate(chips):
            loads[j].wait()
            dst = land_ref.at[4 * px + 2 * py + c]
            cp = pltpu.make_async_remote_copy(src_ref=stage.at[j], dst_ref=dst, send_sem=send_sems.at[j],
                                              recv_sem=recv_sems.at[j], device_id=sibling, device_id_type=MESH)
            cp.start()
            sends.append(cp)
        own_in.wait()
        own_out = pltpu.make_async_copy(stage.at[3], land_ref.at[4 * x + 2 * y + c], own_sem)
        own_out.start()
        for j, (px, py) in enumerate(chips):
            dst = land_ref.at[4 * px + 2 * py + (1 - c)]
            pltpu.make_async_remote_copy(src_ref=stage.at[j], dst_ref=dst, send_sem=send_sems.at[j],
                                         recv_sem=recv_sems.at[j], device_id=sibling,
                                         device_id_type=MESH).wait_recv()
        for cp in sends:
            cp.wait_send()
        own_out.wait()

    return pl.pallas_call(
        body,
        out_shape=jax.ShapeDtypeStruct(land.shape, land.dtype),
        in_specs=_hbm_specs(2),
        out_specs=pl.BlockSpec(memory_space=pl.ANY),
        scratch_shapes=[pltpu.VMEM((4, R, C), block.dtype), pltpu.SemaphoreType.DMA((4,)),
                        pltpu.SemaphoreType.DMA((3,)), pltpu.SemaphoreType.DMA((3,)), pltpu.SemaphoreType.DMA],
        input_output_aliases={1: 0},
        name=name,
        compiler_params=pltpu.CompilerParams(vmem_limit_bytes=VMEM_LIMIT),
    )(block, land)


def reduce_scatter_start(parts, name):
    def body(p_ref, land_ref, send_sem, recv_sem):
        x, y, c = _position()
        for px, py in [(1 - x, y), (x, 1 - y), (1 - x, 1 - y)]:
            pltpu.make_async_remote_copy(src_ref=p_ref.at[2 * px + py], dst_ref=land_ref.at[2 * x + y],
                                         send_sem=send_sem, recv_sem=recv_sem,
                                         device_id=(px, py, c), device_id_type=MESH).start()

    return _split_start(name, body, parts, parts.shape)


def _other_devices(x, y, c):
    return [(1 - x if k & 4 else x, 1 - y if k & 2 else y, 1 - c if k & 1 else c) for k in range(1, N_DEV)]


def all_gather_place_own(block, land, name):
    R, C = block.shape

    def body(b_ref, land_in, land_ref, stage, sems):
        x, y, c = _position()
        load = pltpu.make_async_copy(b_ref, stage, sems.at[0])
        load.start()
        load.wait()
        store = pltpu.make_async_copy(stage, land_ref.at[4 * x + 2 * y + c], sems.at[1])
        store.start()
        store.wait()

    return pl.pallas_call(
        body,
        out_shape=jax.ShapeDtypeStruct(land.shape, land.dtype),
        in_specs=_hbm_specs(2),
        out_specs=pl.BlockSpec(memory_space=pl.ANY),
        scratch_shapes=[pltpu.VMEM((R, C), block.dtype), pltpu.SemaphoreType.DMA((2,))],
        input_output_aliases={1: 0},
        name=name,
    )(block, land)


def reduce_scatter_start_direct(grads, name):
    def body(g_ref, land_ref, send_sem, recv_sem):
        x, y, c = _position()
        for px, py, pc in _other_devices(x, y, c):
            pltpu.make_async_remote_copy(src_ref=g_ref.at[4 * px + 2 * py + pc],
                                         dst_ref=land_ref.at[4 * x + 2 * y + c],
                                         send_sem=send_sem, recv_sem=recv_sem,
                                         device_id=(px, py, pc), device_id_type=MESH).start()

    return _split_start(name, body, grads, grads.shape)


def small_all_gather(small, after):
    def body(small_ref, after_ref, smalls, s_send, s_recv, s_local):
        x, y, c = _position()
        me = 4 * x + 2 * y + c
        lc = pltpu.make_async_copy(small_ref, smalls.at[me], s_local)
        lc.start()
        remote = []
        k = 0
        for dx in (0, 1):
            for dy in (0, 1):
                for dc in (0, 1):
                    if dx + dy + dc == 0:
                        continue
                    peer = (1 - x if dx else x, 1 - y if dy else y, 1 - c if dc else c)
                    rc = pltpu.make_async_remote_copy(
                        src_ref=small_ref, dst_ref=smalls.at[me],
                        send_sem=s_send.at[k], recv_sem=s_recv.at[k],
                        device_id=peer, device_id_type=MESH)
                    rc.start()
                    remote.append(rc)
                    k += 1
        for rc in remote:
            rc.wait()
        lc.wait()

    return pl.pallas_call(
        body,
        out_shape=jax.ShapeDtypeStruct((N_DEV,) + small.shape, small.dtype),
        in_specs=_hbm_specs(2),
        out_specs=pl.BlockSpec(memory_space=pl.ANY),
        scratch_shapes=[pltpu.SemaphoreType.DMA((7,)), pltpu.SemaphoreType.DMA((7,)), pltpu.SemaphoreType.DMA],
        name="small_all_gather",
    )(small, after)


def pair_add(grads, theirs, core, name):
    _, R, C = theirs.shape
    tr = R // 2

    def body(c_ref, a_ref, b_ref, o_ref):
        o_ref[...] = (a_ref[...].astype(F32) + b_ref[...].astype(F32)).astype(BF16)

    return pl.pallas_call(
        body,
        out_shape=jax.ShapeDtypeStruct(theirs.shape, BF16),
        grid_spec=pltpu.PrefetchScalarGridSpec(
            num_scalar_prefetch=1, grid=(4, R // tr),
            in_specs=[pl.BlockSpec((None, tr, C), lambda q, i, c: (2 * q + c[0], i, 0)),
                      pl.BlockSpec((None, tr, C), lambda q, i, c: (q, i, 0))],
            out_specs=pl.BlockSpec((None, tr, C), lambda q, i, c: (q, i, 0))),
        name=name,
        compiler_params=pltpu.CompilerParams(dimension_semantics=("parallel", "parallel"),
                                             vmem_limit_bytes=VMEM_LIMIT),
    )(core, grads, theirs)


def sum_slots(recv, off, rows, blk, name):
    nq, _, C = recv.shape
    ob = off // blk

    def body(r_ref, o_ref):
        acc = r_ref[0].astype(F32)
        for q in range(1, nq):
            acc = acc + r_ref[q].astype(F32)
        o_ref[...] = acc

    return _call(name, body, (rows // blk,),
                 [(recv, (nq, blk, C), lambda i: (0, ob + i, 0))],
                 [((rows, C), F32, (blk, C), lambda i: (i, 0))], sem=("parallel",))[0]


def _sum_terms(refs):
    acc = refs[0][...].astype(F32)
    for r in refs[1:]:
        acc = acc + r[...].astype(F32)
    return acc


def sum_landed(own, land, me, off, rows, blk, name):
    n, _, C = land.shape
    ob = off // blk

    def body(c_ref, *refs):
        refs[n][...] = _sum_terms(refs[:n])

    def entry(flip):
        return pl.BlockSpec((None, blk, C), lambda i, c: (c[0] ^ flip, ob + i, 0))

    return pl.pallas_call(
        body,
        out_shape=jax.ShapeDtypeStruct((rows, C), F32),
        grid_spec=pltpu.PrefetchScalarGridSpec(
            num_scalar_prefetch=1, grid=(rows // blk,),
            in_specs=[entry(k) for k in range(n)],
            out_specs=pl.BlockSpec((blk, C), lambda i, c: (i, 0))),
        name=name,
        compiler_params=pltpu.CompilerParams(dimension_semantics=("parallel",), vmem_limit_bytes=VMEM_LIMIT),
    )(me, own, *([land] * (n - 1)))


def _adamw_update(wv, gv, mv, vv):
    nm = ADAM_B1 * mv + (1.0 - ADAM_B1) * gv
    nv = ADAM_B2 * vv + (1.0 - ADAM_B2) * (gv * gv)
    c1 = 1.0 / (1.0 - ADAM_B1 ** ADAM_STEP)
    c2 = 1.0 / (1.0 - ADAM_B2 ** ADAM_STEP)
    return -ADAM_LR * ((nm * c1) / (jnp.sqrt(nv * c2) + ADAM_EPS) + ADAM_WD * wv), nm, nv


def sum_adamw(own, land, me, off, blk, w, m, v, name):
    rows, C = w.shape
    n = land.shape[0]
    ob = off // blk

    def body(c_ref, *refs):
        w_ref, m_ref, v_ref, g_out, d_out, m_out, v_out = refs[n:]
        gv = _sum_terms(refs[:n])
        g_out[...] = gv
        d_out[...], m_out[...], v_out[...] = _adamw_update(w_ref[...], gv, m_ref[...], v_ref[...])

    def entry(flip):
        return pl.BlockSpec((None, blk, C), lambda i, c: (c[0] ^ flip, ob + i, 0))

    plain = pl.BlockSpec((blk, C), lambda i, c: (i, 0))
    return pl.pallas_call(
        body,
        out_shape=[jax.ShapeDtypeStruct((rows, C), F32)] * 4,
        grid_spec=pltpu.PrefetchScalarGridSpec(
            num_scalar_prefetch=1, grid=(rows // blk,),
            in_specs=[entry(k) for k in range(n)] + [plain, plain, plain],
            out_specs=[plain] * 4),
        name=name,
        compiler_params=pltpu.CompilerParams(dimension_semantics=("parallel",), vmem_limit_bytes=VMEM_LIMIT),
    )(me, own, *([land] * (n - 1)), w, m, v)


def adamw(w, g, m, v, name):
    R, C = w.shape
    tr = R
    for cand in (256, 128, 64, 32, 16, 8):
        if R % cand == 0 and R > cand:
            tr = cand
            break

    def body(w_ref, g_ref, m_ref, v_ref, d_ref, nm_ref, nv_ref):
        d_ref[...], nm_ref[...], nv_ref[...] = _adamw_update(w_ref[...], g_ref[...], m_ref[...], v_ref[...])

    spec = ((tr, C), lambda i: (i, 0))
    out = ((R, C), F32) + spec
    return _call(name, body, (R // tr,), [(w,) + spec, (g,) + spec, (m,) + spec, (v,) + spec],
                 [out, out, out], sem=("parallel",))


def _rms_tile(xv, gv):
    r = lax.rsqrt(jnp.mean(xv * xv, axis=-1, keepdims=True) + EPS)
    return (xv * r * gv).astype(BF16)


def rms_fwd(x, g, name):
    S, D = x.shape
    tr = 512

    def body(x_ref, g_ref, o_ref):
        o_ref[...] = _rms_tile(x_ref[...], g_ref[...])

    return _call(name, body, (S // tr,),
                 [(x, (tr, D), lambda i: (i, 0)), (g, (1, D), lambda i: (0, 0))],
                 [((S, D), BF16, (tr, D), lambda i: (i, 0))], sem=("parallel",))[0]


def _rms_bwd_tile(dn, xv, gv):
    r = lax.rsqrt(jnp.mean(xv * xv, axis=-1, keepdims=True) + EPS)
    xh = xv * r
    dxh = dn * gv
    dx = r * (dxh - xh * jnp.mean(dxh * xh, axis=-1, keepdims=True))
    return dx, dn * xh


def _final_loss_tile(xv, tv, gv):
    D = xv.shape[1]
    r = lax.rsqrt(jnp.mean(xv * xv, axis=-1, keepdims=True) + EPS)
    xh = xv * r
    e = xh * gv - tv
    part = 0.5 * jnp.sum(jnp.sum(e * e, axis=-1, keepdims=True) * (1.0 / D), axis=0, keepdims=True)
    dy = e * (1.0 / D)
    dxh = dy * gv
    dx = r * (dxh - xh * jnp.mean(dxh * xh, axis=-1, keepdims=True))
    return part, dx, jnp.sum(dy * xh, axis=0, keepdims=True)


FFN_TF = 4 * FFN_SHARD


def _ffn_pick(G, which):
    if isinstance(G, tuple):
        return (G[0], which) if which < 2 else (G[1], 0)
    return G, which


def _ffn_whole_w_spec(G, which):
    arr, blk = _ffn_pick(G, which)
    return (arr, (N_DEV, FFN_SHARD, arr.shape[2]), lambda *idx: (0, blk, 0), pl.Buffered(1))


def _ffn_hidden(a, b):
    av, bv = a.astype(F32), b.astype(F32)
    return (av * _sigmoid(av) * bv).astype(BF16)


def ffn_up(n, G, name):
    S, D = n.shape
    F = N_DEV * FFN_SHARD
    tm = 256

    def body(n_ref, w1_ref, w3_ref, abh_ref):
        nv = n_ref[...]
        a = _dot(nv, w1_ref[...].reshape(F, D), 1, 1).astype(BF16)
        b = _dot(nv, w3_ref[...].reshape(F, D), 1, 1).astype(BF16)
        abh_ref[0] = a
        abh_ref[1] = b
        abh_ref[2] = _ffn_hidden(a, b)

    return _call(name, body, (S // tm,),
                 [(n, (tm, D), lambda i: (i, 0)),
                  _ffn_whole_w_spec(G, 0), _ffn_whole_w_spec(G, 1)],
                 [((3, S, F), BF16, (3, tm, F), lambda i: (0, i, 0))],
                 sem=("parallel",))[0]


def ffn_down(abh, G, x, g_next, name):
    _, S, F = abh.shape
    D = x.shape[1]
    tm = 512

    def body(h_ref, w2_ref, x_ref, g_ref, o_ref, n_ref):
        xo = x_ref[...] + 0.5 * _dot(h_ref[...], w2_ref[...].reshape(F, D))
        o_ref[...] = xo
        n_ref[...] = _rms_tile(xo, g_ref[...])

    tile = ((tm, D), lambda i: (i, 0))
    return _call(name, body, (S // tm,),
                 [(abh, (None, tm, F), lambda i: (2, i, 0)), _ffn_whole_w_spec(G, 2),
                  (x,) + tile, (g_next, (1, D), lambda i: (0, 0))],
                 [((S, D), F32) + tile, ((S, D), BF16) + tile], sem=("parallel",))


def ffn_last(x, g, G, tgt, g_final, name):
    S, D = x.shape
    F = N_DEV * FFN_SHARD
    tm = 256

    def body(x_ref, g_ref, w1_ref, w3_ref, w2_ref, t_ref, gf_ref,
             n_ref, abh_ref, dxo_ref, dab_ref, dx_ref, dxb_ref, dg_ref, l_ref, dgf_ref):
        i = pl.program_id(0)
        xv, gv = x_ref[...], g_ref[...]
        chunks = [(slice(4 * f, 4 * f + 4), slice(f * FFN_TF, (f + 1) * FFN_TF)) for f in range(F // FFN_TF)]
        weight = lambda w_ref, slots: w_ref[slots].reshape(FFN_TF, D)
        nv = _rms_tile(xv, gv)
        n_ref[...] = nv
        y = None
        for slots, cols in chunks:
            a = _dot(nv, weight(w1_ref, slots), 1, 1).astype(BF16)
            b = _dot(nv, weight(w3_ref, slots), 1, 1).astype(BF16)
            h = _ffn_hidden(a, b)
            abh_ref[0, :, cols] = a
            abh_ref[1, :, cols] = b
            abh_ref[2, :, cols] = h
            t = _dot(h, weight(w2_ref, slots))
            y = t if y is None else y + t
        part, dxo, dgfp = _final_loss_tile(xv + 0.5 * y, t_ref[...], gf_ref[...])
        dxo_b = dxo.astype(BF16)
        dxo_ref[...] = dxo_b
        dn = None
        for slots, cols in chunks:
            dh = 0.5 * _dot(dxo_b, weight(w2_ref, slots), 1, 1)
            da, db = _ffn_hidden_grads(dh, abh_ref[0, :, cols].astype(F32), abh_ref[1, :, cols].astype(F32))
            da, db = da.astype(BF16), db.astype(BF16)
            dab_ref[0, :, cols] = da
            dab_ref[1, :, cols] = db
            t = _dot(da, weight(w1_ref, slots)) + _dot(db, weight(w3_ref, slots))
            dn = t if dn is None else dn + t
        dx, dgt = _rms_bwd_tile(dn, xv, gv)
        dx = dxo + dx
        dx_ref[...] = dx
        dxb_ref[...] = dx.astype(BF16)
        dgp = jnp.sum(dgt, axis=0, keepdims=True)

        @pl.when(i == 0)
        def _():
            dg_ref[...] = dgp
            l_ref[...] = jnp.broadcast_to(part, l_ref.shape)
            dgf_ref[...] = dgfp

        @pl.when(i > 0)
        def _():
            dg_ref[...] += dgp
            l_ref[...] += jnp.broadcast_to(part, l_ref.shape)
            dgf_ref[...] += dgfp

    tile = ((tm, D), lambda i: (i, 0))
    gain = ((1, D), lambda i: (0, 0))
    return _call(name, body, (S // tm,),
                 [(x,) + tile, (g,) + gain,
                  _ffn_whole_w_spec(G, 0), _ffn_whole_w_spec(G, 1), _ffn_whole_w_spec(G, 2),
                  (tgt,) + tile, (g_final,) + gain],
                 [((S, D), BF16) + tile, ((3, S, F), BF16, (3, tm, F), lambda i: (0, i, 0)),
                  ((S, D), BF16) + tile, ((2, S, F), BF16, (2, tm, F), lambda i: (0, i, 0)),
                  ((S, D), F32) + tile, ((S, D), BF16) + tile, ((1, D), F32) + gain,
                  ((1, LANES), F32, (1, LANES), lambda i: (0, 0)), ((1, D), F32) + gain],
                 sem=("arbitrary",))


def _ffn_hidden_grads(dh, av, bv):
    sig = _sigmoid(av)
    return dh * bv * (sig * (1.0 + av * (1.0 - sig))), dh * (av * sig)


def ffn_bwd_hidden(dxo, abh, G, name):
    _, S, F = abh.shape
    D = dxo.shape[1]
    tm = 256

    def body(d_ref, w2_ref, ab_ref, o_ref):
        dh = 0.5 * _dot(d_ref[...].astype(BF16), w2_ref[...].reshape(F, D), 1, 1)
        da, db = _ffn_hidden_grads(dh, ab_ref[0].astype(F32), ab_ref[1].astype(F32))
        o_ref[0] = da.astype(BF16)
        o_ref[1] = db.astype(BF16)

    return _call(name + "_down_bwd", body, (S // tm,),
                 [(dxo, (tm, D), lambda i: (i, 0)), _ffn_whole_w_spec(G, 2),
                  (abh, (2, tm, F), lambda i: (0, i, 0))],
                 [((2, S, F), BF16, (2, tm, F), lambda i: (0, i, 0))],
                 sem=("parallel",))[0]


def ffn_bwd_weights(dxo, abh, dab, n, name):
    _, S, F = abh.shape
    D = dxo.shape[1]
    nf = F // FFN_TF
    tk = WGRAD_TK
    nk = S // tk
    gshape = (N_DEV, 3 * FFN_SHARD, D)

    def dw2_body(h_ref, d_ref, o_ref, acc_ref):
        k = pl.program_id(1)
        p = _dot(h_ref[...], d_ref[...].astype(BF16), 0, 0)

        @pl.when(k == 0)
        def _():
            acc_ref[...] = p

        @pl.when(k > 0)
        def _():
            acc_ref[...] += p

        @pl.when(k == nk - 1)
        def _():
            o_ref[...] = (0.5 * acc_ref[...]).astype(BF16).reshape(4, FFN_SHARD, D)

    gw = _call(name + "_dw2", dw2_body, (nf, nk),
               [(abh, (None, tk, FFN_TF), lambda j, k: (2, k, j)), (dxo, (tk, D), lambda j, k: (k, 0))],
               [(gshape, BF16, (4, FFN_SHARD, D), lambda j, k: (j, 2, 0))],
               scratch=[pltpu.VMEM((FFN_TF, D), F32)], sem=("parallel", "arbitrary"))[0]

    def dw13_body(gw_ref, dab_ref, n_ref, o_ref):
        o_ref[...] = _dot(dab_ref[...], n_ref[...], 0, 0).astype(BF16).reshape(4, FFN_SHARD, D)

    gw = pl.pallas_call(
        dw13_body,
        out_shape=jax.ShapeDtypeStruct(gshape, BF16),
        grid=(2, nf),
        in_specs=[pl.BlockSpec(memory_space=pl.ANY),
                  pl.BlockSpec((None, S, FFN_TF), lambda w, j: (w, 0, j)),
                  pl.BlockSpec((S, D), lambda w, j: (0, 0))],
        out_specs=pl.BlockSpec((4, FFN_SHARD, D), lambda w, j: (j, w, 0)),
        input_output_aliases={0: 0},
        name=name + "_dw13",
        compiler_params=pltpu.CompilerParams(dimension_semantics=("parallel", "parallel"),
                                             vmem_limit_bytes=VMEM_LIMIT),
    )(gw, dab, n)
    return gw


def ffn_bwd_input(dab, G, x_in, g, dxo, name):
    _, S, F = dab.shape
    D = x_in.shape[1]
    tm = 256

    def dn_body(dab_ref, w1_ref, w3_ref, x_ref, d_ref, g_ref, dx_ref, dg_ref):
        i = pl.program_id(0)
        dn = _dot(dab_ref[0], w1_ref[...].reshape(F, D)) + _dot(dab_ref[1], w3_ref[...].reshape(F, D))
        dx, dgt = _rms_bwd_tile(dn, x_ref[...], g_ref[...])
        dx_ref[...] = d_ref[...] + dx
        dgp = jnp.sum(dgt, axis=0, keepdims=True)

        @pl.when(i == 0)
        def _():
            dg_ref[...] = dgp

        @pl.when(i > 0)
        def _():
            dg_ref[...] += dgp

    tile = ((tm, D), lambda i: (i, 0))
    return _call(name + "_dn", dn_body, (S // tm,),
                 [(dab, (2, tm, F), lambda i: (0, i, 0)),
                  _ffn_whole_w_spec(G, 0), _ffn_whole_w_spec(G, 1),
                  (x_in,) + tile, (dxo,) + tile, (g, (1, D), lambda i: (0, 0))],
                 [((S, D), F32) + tile, ((1, D), F32, (1, D), lambda i: (0, 0))],
                 sem=("arbitrary",))


PROJ_TN = 512


def in_proj(h, Gm, name):
    S, D = h.shape
    n_tiles = N_DEV * Gm.shape[2] // PROJ_TN

    def body(h_ref, w_ref, o_ref):
        o_ref[...] = _dot(h_ref[...], w_ref[...]).astype(BF16)

    return _call(name, body, (n_tiles,),
                 [(h, (S, D), lambda j: (0, 0)),
                  (Gm, (None, D, PROJ_TN), lambda j: (j // 2, 0, j % 2))],
                 [((S, n_tiles * PROJ_TN), BF16, (S, PROJ_TN), lambda j: (0, j))],
                 sem=("parallel",))[0]


def _dproj_pieces(dqkv, dq_b, dkv_b, dgate):
    pieces = [(dqkv[g], [(3 * which + g, (which, 0)) for which in range(3)]) for g in range(3)]
    pieces.append((dq_b, [(9, (None, 0)), (10, (None, 1))]))
    pieces.append((dkv_b, [(11, (None, 0))]))
    pieces.append((dgate, [(12 + 2 * a + b, (a, b)) for a in range(2) for b in range(2)]))
    return pieces


def in_proj_bwd_dw(pieces, h, gm_grads, name):
    S, D = h.shape
    steps = [(n, t, ix) for n, (_, tiles) in enumerate(pieces) for t, ix in tiles]
    n_steps = len(steps)

    def pick(table, j):
        out = table[-1]
        for k in range(len(table) - 2, -1, -1):
            out = jnp.where(j == k, table[k], out)
        return out

    def piece_spec(n, arr):
        own = [k for k, (m, _, _) in enumerate(steps) if m == n]
        at = [steps[min(max(k, own[0]), own[-1])][2] for k in range(n_steps)]
        lead, colb = [ix[0] for ix in at], [ix[1] for ix in at]
        if arr.ndim == 3:
            return (own[0], own[-1]), pl.BlockSpec((None, S, PROJ_TN), lambda j: (pick(lead, j), 0, pick(colb, j)))
        return (own[0], own[-1]), pl.BlockSpec((S, PROJ_TN), lambda j: (0, pick(colb, j)))

    spans, d_specs = zip(*[piece_spec(n, arr) for n, (arr, _) in enumerate(pieces)])
    w_tile = [t for _, t, _ in steps]

    def dw_body(gm_ref, h_ref, *refs):
        o_ref = refs[-1]
        j = pl.program_id(0)
        for d_ref, (first, last) in zip(refs[:-1], spans):
            @pl.when((j >= first) & (j <= last))
            def _(d_ref=d_ref):
                o_ref[...] = _dot(h_ref[...], d_ref[...], 0, 0).astype(BF16)

    return pl.pallas_call(
        dw_body,
        out_shape=jax.ShapeDtypeStruct(gm_grads.shape, BF16),
        grid=(n_steps,),
        in_specs=[pl.BlockSpec(memory_space=pl.ANY),
                  pl.BlockSpec((S, D), lambda j: (0, 0), pipeline_mode=pl.Buffered(1))] + list(d_specs),
        out_specs=pl.BlockSpec((None, D, PROJ_TN), lambda j: (pick(w_tile, j) // 2, 0, pick(w_tile, j) % 2)),
        input_output_aliases={0: 0},
        name=name + "_dw",
        compiler_params=pltpu.CompilerParams(dimension_semantics=("arbitrary",), vmem_limit_bytes=VMEM_LIMIT),
    )(gm_grads, h, *[arr for arr, _ in pieces])


def in_proj_bwd_dh(pieces, Gm, x_in, g, dres, name):
    S, D = x_in.shape
    tm = 256
    C = Gm.shape[2]
    n_sh = N_DEV
    n_p = len(pieces)

    def dh_body(*refs):
        d_refs = refs[:n_p]
        w_ref, x_ref, r_ref, g_ref, dx_ref, dxb_ref, dg_ref = refs[n_p:]
        i = pl.program_id(0)
        p = None
        for d_ref, (arr, tiles) in zip(d_refs, pieces):
            for t, (lead, colb) in tiles:
                cols = slice(colb * PROJ_TN, (colb + 1) * PROJ_TN)
                d = d_ref[:, cols] if lead is None else d_ref[lead, :, cols]
                wcol = (t % 2) * PROJ_TN
                term = _dot(d, w_ref[t // 2, :, wcol:wcol + PROJ_TN], 1, 1)
                p = term if p is None else p + term
        dx, dgt = _rms_bwd_tile(p, x_ref[...], g_ref[...])
        dx = r_ref[...] + dx
        dx_ref[...] = dx
        dxb_ref[...] = dx.astype(BF16)
        dgp = jnp.sum(dgt, axis=0, keepdims=True)

        @pl.when(i == 0)
        def _():
            dg_ref[...] = dgp

        @pl.when(i > 0)
        def _():
            dg_ref[...] += dgp

    tile = ((tm, D), lambda i: (i, 0))

    def rows_of(arr):
        if arr.ndim == 3:
            return (arr, (arr.shape[0], tm, arr.shape[2]), lambda i: (0, i, 0))
        return (arr, (tm, arr.shape[1]), lambda i: (i, 0))

    return _call(name + "_dh", dh_body, (S // tm,),
                 [rows_of(arr) for arr, _ in pieces]
                 + [(Gm, (n_sh, D, C), lambda i: (0, 0, 0), pl.Buffered(1)),
                    (x_in,) + tile, (dres,) + tile, (g, (1, D), lambda i: (0, 0))],
                 [((S, D), F32) + tile, ((S, D), BF16) + tile, ((1, D), F32, (1, D), lambda i: (0, 0))],
                 sem=("arbitrary",))


def _t5_bucket(rel):
    n = N_BUCKETS // 2
    max_exact = n // 2
    ret = jnp.where(rel > 0, n, 0)
    a = jnp.abs(rel)
    af = jnp.maximum(a, 1).astype(F32)
    large = max_exact + (jnp.log(af / max_exact) / math.log(MAX_DISTANCE / max_exact)
                         * (n - max_exact)).astype(jnp.int32)
    large = jnp.minimum(large, n - 1)
    return ret + jnp.where(a < max_exact, a, large)


def _bucket_tables():
    qi = jnp.arange(A_TQ, dtype=jnp.int32)[:, None]
    kj = jnp.arange(A_WIN, dtype=jnp.int32)[None, :]
    rel = kj - HALF_WINDOW - qi
    return jnp.stack([_t5_bucket(rel * d) for d in DILATIONS], axis=0)


def bias_build(rel_bias, buckets):
    def body(tab_ref, bk_ref, o_ref):
        col = pl.program_id(0) * HEADS_PER_GROUP_A + pl.program_id(1)
        bk = bk_ref[...]
        acc = jnp.zeros(bk.shape, F32)
        for b in range(N_BUCKETS):
            acc = jnp.where(bk == b, tab_ref[b, col], acc)
        qi = lax.broadcasted_iota(jnp.int32, bk.shape, 0)
        kj = lax.broadcasted_iota(jnp.int32, bk.shape, 1)
        band = jnp.where(jnp.abs(kj - HALF_WINDOW - qi) <= HALF_WINDOW, acc, NEG_INF)
        o_ref[0] = jnp.where(kj >= HALF_WINDOW, band, NEG_INF)
        o_ref[1] = band
        o_ref[2] = jnp.where(kj < A_TQ + HALF_WINDOW, band, NEG_INF)

    out = pl.pallas_call(
        body,
        out_shape=jax.ShapeDtypeStruct((3, HEADS_PER_GROUP_A // 2, 3, 2, A_TQ, A_WIN), F32),
        grid=(3, HEADS_PER_GROUP_A),
        in_specs=[pl.BlockSpec(memory_space=pltpu.SMEM),
                  pl.BlockSpec((None, A_TQ, A_WIN), lambda g, h: (g, 0, 0))],
        out_specs=pl.BlockSpec((None, None, 3, None, A_TQ, A_WIN), lambda g, h: (g, h // 2, 0, h % 2, 0, 0)),
        name="a_bias_build",
        compiler_params=pltpu.CompilerParams(dimension_semantics=("parallel", "parallel")),
    )(rel_bias, buckets)
    return out.reshape(3, HEADS_PER_GROUP_A // 2, 3, 2 * A_TQ, A_WIN)


def bias_bwd(dbias, buckets):
    def body(d_ref, bk_ref, o_ref):
        bk = bk_ref[...]
        for b in range(N_BUCKETS):
            mask = bk == b
            for h in range(HEADS_PER_GROUP_A):
                part = jnp.sum(jnp.where(mask, d_ref[h], 0.0), axis=1, keepdims=True)
                o_ref[h, b:b + 1, :] = jnp.broadcast_to(jnp.sum(part, axis=0, keepdims=True), (1, LANES))

    out = pl.pallas_call(
        body,
        out_shape=jax.ShapeDtypeStruct((3, HEADS_PER_GROUP_A, N_BUCKETS, LANES), F32),
        grid=(3,),
        in_specs=[pl.BlockSpec((None, HEADS_PER_GROUP_A, A_TQ, A_WIN), lambda g: (g, 0, 0, 0)),
                  pl.BlockSpec((None, A_TQ, A_WIN), lambda g: (g, 0, 0))],
        out_specs=pl.BlockSpec((None, HEADS_PER_GROUP_A, N_BUCKETS, LANES), lambda g: (g, 0, 0, 0)),
        name="a_bias_bwd",
        compiler_params=pltpu.CompilerParams(dimension_semantics=("parallel",)),
    )(dbias, buckets)
    return out[:, :, :, 0].transpose(2, 0, 1).reshape(N_BUCKETS, 3 * HEADS_PER_GROUP_A)


def _a_fill_padded(pad_ref, src_ref, n, pad):
    zeros = jnp.zeros((pad, LANES), pad_ref.dtype)
    pad_ref[0:pad, :] = zeros
    pad_ref[pad + n:2 * pad + n, :] = zeros
    pad_ref[pad:pad + n, :] = src_ref[...].astype(pad_ref.dtype)


def _a_stack_heads(x, lane):
    zero = jnp.zeros_like(x)
    return jnp.concatenate([jnp.where(lane < HEAD_DIM_A, x, zero), jnp.where(lane >= HEAD_DIM_A, x, zero)], axis=0)


def _a_bias_variant(qb, nqb):
    return jnp.where(qb == 0, 0, jnp.where(qb == nqb - 1, 2, 1))


def _a_slab_specs(proj, g):
    S = proj.shape[0]
    per = GROUP_WIDTH_A // LANES
    return [(proj, (S, LANES), lambda hp, w=w: (0, per * (3 * w + g) + hp)) for w in range(3)]


def a_fwd(proj, bias, g, name, others=None):
    S = proj.shape[0]
    d = DILATIONS[g]
    L = S // d
    nqb = L // A_TQ
    pad = HALF_WINDOW * d
    n_others = 0 if others is None else 4
    tr = 256

    def body(q_ref, k_ref, v_ref, b_ref, *refs):
        out1, out2, qf, kpad, vpad = refs[n_others:n_others + 5]
        o_ref, l_ref = refs[n_others + 5:] if others else (out1, out2)
        qf[...] = q_ref[...].astype(F32) * A_SCALE
        _a_fill_padded(kpad, k_ref, S, pad)
        _a_fill_padded(vpad, v_ref, S, pad)
        lane = lax.broadcasted_iota(jnp.int32, (A_TQ, LANES), 1)

        def block(t, carry):
            qb, r = t // d, t % d
            start = qb * (A_TQ * d) + r
            kw = kpad[pl.ds(start, A_WIN, stride=d), :].astype(BF16)
            vw = vpad[pl.ds(start, A_WIN, stride=d), :].astype(BF16)
            q = qf[pl.ds(start, A_TQ, stride=d), :].astype(BF16)
            q2 = _a_stack_heads(q, lane)
            s = _dot(q2, kw, 1, 1) + b_ref[_a_bias_variant(qb, nqb)]
            m = jnp.max(s, axis=-1, keepdims=True)
            e = jnp.exp(s - m)
            l = jnp.sum(e, axis=-1, keepdims=True)
            o2 = _dot(e.astype(BF16), vw) / l
            lse2 = m + jnp.log(l)
            o_ref[pl.ds(start, A_TQ, stride=d), :] = jnp.where(lane < HEAD_DIM_A, o2[0:A_TQ], o2[A_TQ:])
            l_ref[pl.ds(start, A_TQ, stride=d), :] = jnp.where(lane < HEAD_DIM_A, lse2[0:A_TQ], lse2[A_TQ:])
            return carry

        lax.fori_loop(0, nqb * d, block, 0, unroll=A_UNROLL)

        if others:
            o0, o1, l0, l1 = refs[:n_others]

            def combine(c, carry):
                rows = pl.ds(pl.multiple_of(c * tr, tr), tr)
                la, lb, lc = l0[rows, :], l1[rows, :], l_ref[rows, :]
                m = jnp.maximum(jnp.maximum(la, lb), lc)
                ea, eb, ec = jnp.exp(la - m), jnp.exp(lb - m), jnp.exp(lc - m)
                z = ea + eb + ec
                out1[rows, :] = ((ea * o0[rows, :] + eb * o1[rows, :] + ec * o_ref[rows, :]) / z).astype(BF16)
                out2[rows, :] = m + jnp.log(z)
                return carry

            lax.fori_loop(0, S // tr, combine, 0)

    slab = ((S, LANES), lambda hp: (0, hp))
    wide = (S, GROUP_WIDTH_A)
    other_ins = [(a,) + slab for a in (*others[0], *others[1])] if others else []
    return _call(name, body, (4,),
                 _a_slab_specs(proj, g)
                 + [(bias, (None, None, 3, 2 * A_TQ, A_WIN), lambda hp: (g, hp, 0, 0, 0))] + other_ins,
                 [(wide, BF16 if others else F32) + slab, (wide, F32) + slab],
                 scratch=[pltpu.VMEM((S, LANES), F32)] + [pltpu.VMEM((S + 2 * pad, LANES), F32)] * 2
                 + ([pltpu.VMEM((S, LANES), F32)] * 2 if others else []),
                 sem=("parallel",))


def a_bwd(proj, bias, do_a, o_a, lse_tot, g, name):
    S = proj.shape[0]
    d = DILATIONS[g]
    L = S // d
    nqb = L // A_TQ
    pad = HALF_WINDOW * d

    def body(q_ref, k_ref, v_ref, b_ref, do_ref, o_ref, l_ref, dqkv_ref, db_ref,
             qf, of, dqf, kpad, vpad, dkacc, dvacc):
        qf[...] = q_ref[...].astype(F32) * A_SCALE
        of[...] = o_ref[...].astype(F32)
        _a_fill_padded(kpad, k_ref, S, pad)
        _a_fill_padded(vpad, v_ref, S, pad)
        dkacc[...] = jnp.zeros(dkacc.shape, F32)
        dvacc[...] = jnp.zeros(dvacc.shape, F32)
        db_ref[...] = jnp.zeros(db_ref.shape, F32)
        lane = lax.broadcasted_iota(jnp.int32, (A_TQ, LANES), 1)

        def block(t, carry):
            qb, r = t // d, t % d
            start = qb * (A_TQ * d) + r
            rows = pl.ds(start, A_TQ, stride=d)
            win = pl.ds(start, A_WIN, stride=d)
            kw = kpad[win, :].astype(BF16)
            vw = vpad[win, :].astype(BF16)
            q = qf[rows, :].astype(BF16)
            do = do_ref[rows, :]
            ov = of[rows, :]
            lt = l_ref[rows, :]
            q2 = _a_stack_heads(q, lane)
            do2 = _a_stack_heads(do, lane)
            lt2 = jnp.concatenate([lt[:, 0:1], lt[:, HEAD_DIM_A:HEAD_DIM_A + 1]], axis=0)
            s = _dot(q2, kw, 1, 1) + b_ref[_a_bias_variant(qb, nqb)]
            p = jnp.exp(s - lt2)
            t = jnp.sum(do2 * jnp.concatenate([ov, ov], axis=0), axis=-1, keepdims=True)
            dob2 = do2.astype(BF16)
            ds = p * (_dot(dob2, vw, 1, 1) - t)
            db_ref[...] += ds
            dsb = ds.astype(BF16)
            dq2 = _dot(dsb, kw)
            dqf[rows, :] = jnp.where(lane < HEAD_DIM_A, dq2[0:A_TQ], dq2[A_TQ:]) * A_SCALE
            dkacc[win, :] += _dot(dsb, q2, 0, 0)
            dvacc[win, :] += _dot(p.astype(BF16), dob2, 0, 0)
            return carry

        lax.fori_loop(0, nqb * d, block, 0, unroll=A_UNROLL)
        dqkv_ref[0] = dqf[...].astype(BF16)
        dqkv_ref[1] = dkacc[pad:pad + S, :].astype(BF16)
        dqkv_ref[2] = dvacc[pad:pad + S, :].astype(BF16)

    slab = ((S, LANES), lambda hp: (0, hp))
    padded = pltpu.VMEM((S + 2 * pad, LANES), F32)
    return _call(
        name, body, (4,),
        _a_slab_specs(proj, g)
        + [(bias, (None, None, 3, 2 * A_TQ, A_WIN), lambda hp: (g, hp, 0, 0, 0)),
           (do_a,) + slab, (o_a,) + slab, (lse_tot,) + slab],
        [((3, S, GROUP_WIDTH_A), BF16, (3, S, LANES), lambda hp: (0, 0, hp)),
         ((4, 2 * A_TQ, A_WIN), F32, (None, 2 * A_TQ, A_WIN), lambda hp: (hp, 0, 0))],
        scratch=[pltpu.VMEM((S, LANES), F32)] * 3 + [padded] * 4,
        sem=("parallel",))


def _rope_tables(S):
    rows = S // GRID_W
    row = jnp.repeat(jnp.arange(rows, dtype=F32), GRID_W)
    col = jnp.tile(jnp.arange(GRID_W, dtype=F32), rows)
    n_freq = HEAD_DIM_B // 4
    freq = ROPE_THETA ** (-jnp.arange(n_freq, dtype=F32) / n_freq)
    ang = jnp.concatenate([row[:, None] * freq, col[:, None] * freq], axis=-1)
    cos, sin = jnp.cos(ang), jnp.sin(ang)
    return jnp.repeat(cos, 2, axis=-1), jnp.stack([-sin, sin], axis=-1).reshape(S, HEAD_DIM_B)


def _swap_pairs(y):
    lane = lax.broadcasted_iota(jnp.int32, y.shape, 1)
    return jnp.where(lane % 2 == 0, pltpu.roll(y, LANES - 1, 1), pltpu.roll(y, 1, 1))


def qkv_prep(proj, gains, cos_t, sin_t, name):
    S = proj.shape[0]
    ts = 256
    n_rot = N_HEADS_B + N_KV_B
    nh = n_rot + N_KV_B
    W = nh * LANES

    def body(x_ref, g_ref, c_ref, s_ref, o_ref, kt_ref):
        cv, sv = c_ref[...], s_ref[...]
        for hb in range(nh):
            cols = slice(hb * LANES, (hb + 1) * LANES)
            if hb < n_rot:
                xv = x_ref[:, cols].astype(F32)
                r = lax.rsqrt(jnp.mean(xv * xv, axis=-1, keepdims=True) + EPS)
                yv = xv * r * g_ref[:, cols]
                rot = yv * cv + _swap_pairs(yv) * sv
                o_ref[:, cols] = rot.astype(BF16)
                if hb >= N_HEADS_B:
                    kt_ref[(hb - N_HEADS_B) * LANES:(hb - N_HEADS_B + 1) * LANES, :] = rot.T.astype(BF16)
            else:
                o_ref[:, cols] = x_ref[:, cols]

    return _call(name, body, (S // ts,),
                 [(proj, (ts, W), lambda i: (i, A_QKV_WIDTH // W)), (gains, (1, W), lambda i: (0, 0)),
                  (cos_t, (ts, LANES), lambda i: (i, 0)), (sin_t, (ts, LANES), lambda i: (i, 0))],
                 [((S, W), BF16, (ts, W), lambda i: (i, 0)),
                  ((N_KV_B * LANES, S), BF16, (N_KV_B * LANES, ts), lambda i: (0, i))],
                 sem=("parallel",))


def qk_prep_bwd(dr, proj, col0, gain, cos_t, sin_t, name):
    S, W = dr.shape
    H = W // LANES
    ts = 256
    wx = math.gcd(W, col0)
    n_x = W // wx

    def body(d_ref, *refs):
        x_refs = refs[:n_x]
        g_ref, c_ref, s_ref, dx_ref, dg_ref = refs[n_x:]
        i = pl.program_id(0)
        cv, sv, gv = c_ref[...], s_ref[...], g_ref[...]
        dgp = jnp.zeros((1, LANES), F32)
        for hb in range(H):
            cols = slice(hb * LANES, (hb + 1) * LANES)
            xc = (hb * LANES) % wx
            xv = x_refs[(hb * LANES) // wx][:, xc:xc + LANES].astype(F32)
            dout = d_ref[:, cols]
            dy = dout * cv + _swap_pairs(dout * sv)
            dx, dgt = _rms_bwd_tile(dy, xv, gv)
            dx_ref[:, cols] = dx.astype(BF16)
            dgp = dgp + jnp.sum(dgt, axis=0, keepdims=True)

        @pl.when(i == 0)
        def _():
            dg_ref[...] = dgp

        @pl.when(i > 0)
        def _():
            dg_ref[...] += dgp

    return _call(name, body, (S // ts,),
                 [(dr, (ts, W), lambda i: (i, 0))]
                 + [(proj, (ts, wx), lambda i, k=k: (i, col0 // wx + k)) for k in range(n_x)]
                 + [(gain, (1, LANES), lambda i: (0, 0)),
                  (cos_t, (ts, LANES), lambda i: (i, 0)), (sin_t, (ts, LANES), lambda i: (i, 0))],
                 [((S, W), BF16, (ts, W), lambda i: (i, 0)),
                  ((1, LANES), F32, (1, LANES), lambda i: (0, 0))],
                 sem=("arbitrary",))


def _row_sums(x):
    hi = x.astype(BF16)
    lo = (x - hi.astype(F32)).astype(BF16)
    ones = jnp.ones((8, LANES), BF16)
    return (_dot(ones, hi, 1, 1) + _dot(ones, lo, 1, 1))[0:1, :]


def flash_fwd(qkv, name):
    S = qkv.shape[0]
    tq = B_TQ_FWD
    hps = B_HEADS_PER_STEP

    def body(q_ref, k_ref, v_ref, o_ref, l_ref):
        k, v = k_ref[...], v_ref[...]
        for j in range(hps):
            cols = slice(j * LANES, (j + 1) * LANES)
            s = _dot(q_ref[:, cols], k, 1, 1)
            m = jnp.max(s, axis=-1, keepdims=True)
            e = jnp.exp2(s - m)
            l = jnp.sum(e, axis=-1, keepdims=True)
            o_ref[:, cols] = (_dot(e.astype(BF16), v) / l).astype(BF16)
            lse = jnp.broadcast_to(m * (1.0 / LOG2E) + jnp.log(l), (tq, LANES))
            l_ref[j] = _row_sums(lse) * (1.0 / LANES)

    per = GQA_GROUP_B // hps
    heads = lambda g, h, i: (i, g * per + h)
    return _call(name, body, (N_KV_B, per, S // tq),
                 [(qkv, (tq, hps * LANES), heads),
                  (qkv, (S, LANES), lambda g, h, i: (0, N_HEADS_B + g)),
                  (qkv, (S, LANES), lambda g, h, i: (0, N_HEADS_B + N_KV_B + g))],
                 [((S, N_HEADS_B * LANES), BF16, (tq, hps * LANES), heads),
                  ((N_HEADS_B, 1, S), F32, (hps, 1, tq), lambda g, h, i: (g * per + h, 0, i))],
                 sem=("parallel", "parallel", "parallel"))


def flash_bwd(qkv, k_t, do_b, o_b, lse, name):
    S = qkv.shape[0]
    tq = B_TQ_BWD
    nq = S // tq
    scale = HEAD_DIM_B ** -0.5

    def body(q_ref, k_ref, v_ref, kt_ref, do_ref, o_ref, l_ref, dq_ref, dk_ref, dv_ref, dkacc, dvacc):
        h, i = pl.program_id(1), pl.program_id(2)

        @pl.when((h == 0) & (i == 0))
        def _():
            dkacc[...] = jnp.zeros(dkacc.shape, F32)
            dvacc[...] = jnp.zeros(dvacc.shape, F32)

        q = q_ref[...]
        dob = do_ref[...]
        t = _row_sums(dob.astype(F32) * o_ref[...].astype(F32))
        pt = jnp.exp2(_dot(k_ref[...], q, 1, 1) - l_ref[...] * LOG2E)
        dsb = (pt * (_dot(v_ref[...], dob, 1, 1) - t)).astype(BF16)
        dvacc[...] += _dot(pt.astype(BF16), dob)
        dkacc[...] += _dot(dsb, q)
        dq_ref[...] = _dot(kt_ref[...], dsb).T * scale

        @pl.when((h == GQA_GROUP_B - 1) & (i == nq - 1))
        def _():
            dk_ref[...] = dkacc[...] * (scale / B_Q_PRESCALE)
            dv_ref[...] = dvacc[...].astype(BF16)

    head = lambda g, h, i: (i, g * GQA_GROUP_B + h)
    return _call(name, body, (N_KV_B, GQA_GROUP_B, nq),
                 [(qkv, (tq, LANES), head),
                  (qkv, (S, LANES), lambda g, h, i: (0, N_HEADS_B + g)),
                  (qkv, (S, LANES), lambda g, h, i: (0, N_HEADS_B + N_KV_B + g)),
                  (k_t, (LANES, S), lambda g, h, i: (g, 0)),
                  (do_b, (tq, LANES), head), (o_b, (tq, LANES), head),
                  (lse, (None, 1, tq), lambda g, h, i: (g * GQA_GROUP_B + h, 0, i))],
                 [((S, N_HEADS_B * LANES), F32, (tq, LANES), head),
                  ((S, N_KV_B * LANES), F32, (S, LANES), lambda g, h, i: (0, g)),
                  ((S, N_KV_B * LANES), BF16, (S, LANES), lambda g, h, i: (0, g))],
                 scratch=[pltpu.VMEM((S, LANES), F32)] * 2,
                 sem=("parallel", "arbitrary", "arbitrary"))


MERGE_TN = 512


def _mix_rows_spec(Gm, row0, n_slots, slot_map, cols=None, col_map=None):
    C = Gm.shape[2] if cols is None else cols
    cm = (lambda *idx: 0) if col_map is None else col_map
    return (Gm, (n_slots, LANES, C), lambda *idx: (slot_map(*idx), row0 // LANES, cm(*idx)))


def _gate_specs(proj, tm):
    first = (A_QKV_WIDTH + PB_GATE_A) // MERGE_TN
    return [(proj, (tm, MERGE_TN), lambda i, k=k: (i, first + k)) for k in range(4)]


def _whole_rows_spec(Gm, row0):
    return _mix_rows_spec(Gm, row0, N_DEV, lambda *idx: 0)


def merge_fwd(o_a, o_b, w_a, Gm, proj, b_gate, x, name):
    S, D = x.shape
    tm = 256

    def body(oa_ref, ob_ref, wa_ref, wb_ref, wo_ref, g0, g1, g2, g3, bg_ref, x_ref, m_ref, ya_ref, yb_ref, xo_ref):
        ya = _dot(oa_ref[...], wa_ref[...])
        yb = _dot(ob_ref[...], wb_ref[...].reshape(N_DEV * LANES, D))
        ga = _sigmoid(jnp.concatenate([g0[...], g1[...]], axis=1).astype(F32) + bg_ref[:, 0:D])
        gb = _sigmoid(jnp.concatenate([g2[...], g3[...]], axis=1).astype(F32) + bg_ref[:, D:2 * D])
        merged = (ga * ya + gb * yb).astype(BF16)
        m_ref[...] = merged
        ya_ref[...] = ya.astype(BF16)
        yb_ref[...] = yb.astype(BF16)
        xo_ref[...] = x_ref[...] + _dot(merged, wo_ref[...].reshape(N_DEV * LANES, D))

    rows = lambda a: (a, (tm, a.shape[1]), lambda i: (i, 0))
    out = ((S, D), BF16, (tm, D), lambda i: (i, 0))
    return _call(name, body, (S // tm,),
                 [rows(o_a), rows(o_b), (w_a, w_a.shape, lambda i: (0, 0)),
                  _whole_rows_spec(Gm, REST_WB), _whole_rows_spec(Gm, REST_WOUT)]
                 + _gate_specs(proj, tm) + [(b_gate, (1, 2 * D), lambda i: (0, 0)), rows(x)],
                 [out, out, out, ((S, D), F32, (tm, D), lambda i: (i, 0))], sem=("parallel",))


def merge_bwd(dx2, w_a, Gm, ya, yb, proj, b_gate, name):
    S, D = dx2.shape
    tm = 256

    def body(d_ref, wo_ref, wa_ref, wb_ref, ya_ref, yb_ref, g0, g1, g2, g3, bg_ref,
             dya_ref, dyb_ref, dg_ref, dbg_ref, doa_ref, dob_ref):
        i = pl.program_id(0)
        dm = _dot(d_ref[...].astype(BF16), wo_ref[...].reshape(N_DEV * LANES, D), 1, 1)
        ga = _sigmoid(jnp.concatenate([g0[...], g1[...]], axis=1).astype(F32) + bg_ref[:, 0:D])
        gb = _sigmoid(jnp.concatenate([g2[...], g3[...]], axis=1).astype(F32) + bg_ref[:, D:2 * D])
        dya = (dm * ga).astype(BF16)
        dyb = (dm * gb).astype(BF16)
        dya_ref[...] = dya
        dyb_ref[...] = dyb
        dpa = dm * ya_ref[...].astype(F32) * ga * (1.0 - ga)
        dpb = dm * yb_ref[...].astype(F32) * gb * (1.0 - gb)
        dg_ref[0] = dpa.astype(BF16)
        dg_ref[1] = dpb.astype(BF16)
        doa_ref[...] = _dot(dya, wa_ref[...], 1, 1)
        dob_ref[...] = _dot(dyb, wb_ref[...].reshape(N_DEV * LANES, D), 1, 1).astype(BF16)
        sa =jnp.sum(dpa, axis=0, keepdims=True)
        sb = jnp.sum(dpb, axis=0, keepdims=True)

        @pl.when(i == 0)
        def _():
            dbg_ref[0] = sa
            dbg_ref[1] = sb

        @pl.when(i > 0)
        def _():
            dbg_ref[0] += sa
            dbg_ref[1] += sb

    tile = ((tm, D), lambda i: (i, 0))
    return _call(
        name, body, (S // tm,),
        [(dx2,) + tile, _whole_rows_spec(Gm, REST_WOUT), (w_a, w_a.shape, lambda i: (0, 0)),
         _whole_rows_spec(Gm, REST_WB), (ya,) + tile, (yb,) + tile]
        + _gate_specs(proj, tm) + [(b_gate, (1, 2 * D), lambda i: (0, 0))],
        [((S, D), BF16) + tile, ((S, D), BF16) + tile,
         ((2, S, D), BF16, (2, tm, D), lambda i: (0, i, 0)),
         ((2, 1, D), F32, (2, 1, D), lambda i: (0, 0, 0)),
         ((S, w_a.shape[0]), F32, (tm, w_a.shape[0]), lambda i: (i, 0)),
         ((S, N_HEADS_B * LANES), BF16, (tm, N_HEADS_B * LANES), lambda i: (i, 0))],
        sem=("arbitrary",))


def weight_grad_rows(a, b, grads, row0, name):
    S, M = a.shape
    N = b.shape[1]
    tmm = 512
    tk = WGRAD_TK
    nk = S // tk
    prior = [] if grads is None else [grads]

    def body(*refs):
        a_ref, b_ref, o_ref, acc_ref = refs[len(prior):]
        k = pl.program_id(1)
        p = _dot(a_ref[...], b_ref[...].astype(BF16), 0, 0)

        @pl.when(k == 0)
        def _():
            acc_ref[...] = p

        @pl.when(k > 0)
        def _():
            acc_ref[...] += p

        @pl.when(k == nk - 1)
        def _():
            o_ref[...] = acc_ref[...].astype(BF16).reshape(tmm // LANES, LANES, N)

    return pl.pallas_call(
        body,
        out_shape=jax.ShapeDtypeStruct((N_DEV, MIX_ROWS, N), BF16),
        grid=(M // tmm, nk),
        in_specs=[pl.BlockSpec(memory_space=pl.ANY)] * len(prior)
        + [pl.BlockSpec((tk, tmm), lambda j, k: (k, j)),
           pl.BlockSpec((tk, N), lambda j, k: (k, 0))],
        out_specs=pl.BlockSpec((tmm // LANES, LANES, N), lambda j, k: (j, row0 // LANES, 0)),
        scratch_shapes=[pltpu.VMEM((tmm, N), F32)],
        input_output_aliases={0: 0} if prior else {},
        name=name,
        compiler_params=pltpu.CompilerParams(dimension_semantics=("parallel", "arbitrary"),
                                             vmem_limit_bytes=VMEM_LIMIT),
    )(*prior, a, b)


def weight_grad_plain(a, b, name):
    S, M = a.shape
    N = b.shape[1]
    tk = WGRAD_TK
    nk = S // tk

    def body(a_ref, b_ref, o_ref, acc_ref):
        k = pl.program_id(0)
        p = _dot(a_ref[...], b_ref[...], 0, 0)

        @pl.when(k == 0)
        def _():
            acc_ref[...] = p

        @pl.when(k > 0)
        def _():
            acc_ref[...] += p

        @pl.when(k == nk - 1)
        def _():
            o_ref[...] = acc_ref[...].astype(BF16)

    return _call(name, body, (nk,),
                 [(a, (tk, M), lambda k: (k, 0)), (b, (tk, N), lambda k: (k, 0))],
                 [((M, N), BF16, (M, N), lambda k: (0, 0))],
                 scratch=[pltpu.VMEM((M, N), F32)], sem=("arbitrary",))[0]


def local_step(x, tgt, p, get_g1_up, get_g1_down, get_gm_in, get_gm_rest, get_g2, emit, start_token):
    S, D = x.shape
    after = lambda t: t[0:1, 0:1]
    buckets = _bucket_tables()
    cos_t, sin_t = _rope_tables(S)
    gains = jnp.concatenate([jnp.tile(p["q_norm"] * B_Q_PRESCALE, (1, N_HEADS_B)), jnp.tile(p["k_norm"], (1, N_KV_B)),
                             jnp.ones((1, N_KV_B * LANES), F32)], axis=1)

    n1 = rms_fwd(x, p["ffn1_norm"] + after(start_token), "ffn1_norm")
    bias = bias_build(p["rel_bias"] + after(start_token), buckets)
    g1_up = get_g1_up((n1, bias))
    ab1 = ffn_up(n1, (g1_up, None), "ffn1_up")
    G1 = (g1_up, get_g1_down(ab1))
    x1, hm = ffn_down(ab1, G1, x, p["mix_norm"], "ffn1_down")
    Gw = get_gm_in(hm)
    proj = in_proj(hm, Gw, "in_proj")

    outs, lses = zip(*[a_fwd(proj, bias, g, "a_fwd_%d" % g) for g in range(2)])
    o_a, lse_tot = a_fwd(proj, bias, 2, "a_fwd_2", (outs, lses))

    qkv, k_t = qkv_prep(proj, gains, cos_t, sin_t, "qkv_prep")
    o_b, lse_b = flash_fwd(qkv, "flash_fwd")

    Gm = get_gm_rest(o_b)
    w_a = Gm[:, REST_WA:REST_ROWS, :].reshape(N_DEV, GROUP_WIDTH_A, LANES).transpose(1, 0, 2).reshape(GROUP_WIDTH_A, D)
    merged, ya, yb, x2 = merge_fwd(o_a, o_b, w_a, Gm, proj, p["b_gate"], x1, "merge_fwd")

    G2 = get_g2(x2)
    n2, ab2, dx3_b, dab2, dx2, dx2_b, d_ffn2_norm, loss, d_final = ffn_last(
        x2, p["ffn2_norm"], G2, tgt, p["final_norm"], "ffn2")
    gw2 = ffn_bwd_weights(dx3_b, ab2, dab2, n2, "ffn2_bwd")
    t2 = emit("ffn2", gw2)

    dya, dyb, dgate, dbg, do_a, do_b = merge_bwd(dx2_b, w_a, Gm, ya, yb, proj, p["b_gate"] + after(t2),
                                                 "merge_bwd")
    gm_grads = weight_grad_rows(merged, dx2_b, None, MIX_WOUT, "dw_out")
    gm_grads = weight_grad_rows(o_b, dyb, gm_grads, MIX_WB, "dw_branch_b")
    dw_a = weight_grad_plain(o_a, dya, "dw_branch_a")

    dq_r, dk_r, dv_b = flash_bwd(qkv, k_t, do_b, o_b, lse_b, "flash_bwd")
    dq_b, d_q_norm = qk_prep_bwd(dq_r, proj, A_QKV_WIDTH, p["q_norm"], cos_t, sin_t, "q_prep_bwd")
    dk_b, d_k_norm = qk_prep_bwd(dk_r, proj, A_QKV_WIDTH + N_HEADS_B * LANES, p["k_norm"], cos_t, sin_t,
                                 "k_prep_bwd")

    dqkv, dbs = [], []
    for g in range(3):
        dg_, db = a_bwd(proj, bias, do_a, o_a, lse_tot, g, "a_bwd_%d" % g)
        dqkv.append(dg_)
        dbs.append(db)
    d_rel_bias = bias_bwd(jnp.stack(dbs, axis=0).reshape(3, HEADS_PER_GROUP_A, A_TQ, A_WIN), buckets)

    dproj = _dproj_pieces(dqkv, dq_b, jnp.concatenate([dk_b, dv_b], axis=1), dgate)
    gm_grads = in_proj_bwd_dw(dproj[:3], hm, gm_grads, "in_proj_bwd_a")
    gm_grads = in_proj_bwd_dw(dproj[3:], hm, gm_grads, "in_proj_bwd_b")
    dw_a_sh = dw_a.reshape(GROUP_WIDTH_A, N_DEV, LANES).transpose(1, 0, 2).reshape(N_DEV, MIX_ROWS - MIX_WA, D)
    gm_grads = lax.dynamic_update_slice(gm_grads, dw_a_sh, (0, MIX_WA, 0))
    tm = emit("mix", gm_grads)
    dx1, dx1_b, d_mix_norm = in_proj_bwd_dh(dproj, Gw, x1, p["mix_norm"] + after(tm), dx2, "in_proj_bwd")

    dab1 = ffn_bwd_hidden(dx1_b, ab1, G1, "ffn1_bwd")
    gw1 = ffn_bwd_weights(dx1_b, ab1, dab1, n1, "ffn1_bwd")
    t1 = emit("ffn1", gw1)
    dx0, d_ffn1_norm = ffn_bwd_input(dab1, G1, x, p["ffn1_norm"] + after(t1), dx1, "ffn1_bwd")

    small = dict(ffn1_norm=d_ffn1_norm, mix_norm=d_mix_norm, b_gate=dbg.reshape(1, 2 * D),
                 q_norm=d_q_norm, k_norm=d_k_norm, rel_bias=d_rel_bias, ffn2_norm=d_ffn2_norm,
                 final_norm=d_final)
    return loss, dx0, small


def _pack_small(t, loss_row):
    row6 = jnp.concatenate([t["q_norm"].reshape(1, -1), t["k_norm"].reshape(1, -1), t["rel_bias"].reshape(1, -1)], axis=1)
    return jnp.concatenate([t["ffn1_norm"].reshape(1, -1), t["mix_norm"].reshape(1, -1), t["b_gate"].reshape(2, -1),
                            t["ffn2_norm"].reshape(1, -1), t["final_norm"].reshape(1, -1), row6, loss_row], axis=0)


def _unpack_small(a, shapes):
    return dict(ffn1_norm=a[0:1].reshape(shapes["ffn1_norm"]), mix_norm=a[1:2].reshape(shapes["mix_norm"]),
                b_gate=a[2:4].reshape(shapes["b_gate"]), ffn2_norm=a[4:5].reshape(shapes["ffn2_norm"]),
                final_norm=a[5].reshape(shapes["final_norm"]), q_norm=a[6:7, 0:128].reshape(shapes["q_norm"]),
                k_norm=a[6:7, 128:256].reshape(shapes["k_norm"]), rel_bias=a[6, 256:1024].reshape(shapes["rel_bias"]))


SMALL = ("ffn1_norm", "mix_norm", "b_gate", "q_norm", "k_norm", "rel_bias", "ffn2_norm", "final_norm")
ORDER = ("ffn1_norm", "ffn1_w1", "ffn1_w3", "ffn1_w2", "mix_norm", "w_in", "b_gate", "q_norm", "k_norm", "rel_bias",
         "w_branch_a", "w_branch_b", "w_out", "ffn2_norm", "ffn2_w1", "ffn2_w3", "ffn2_w2", "final_norm")


def kernel(x, ffn1_norm, ffn1_w1, ffn1_w3, ffn1_w2, mix_norm, w_in, b_gate, q_norm, k_norm, rel_bias, w_branch_a, w_branch_b, w_out, ffn2_norm, ffn2_w1, ffn2_w3, ffn2_w2, final_norm, loss_target, m_ffn1_norm, m_ffn1_w1, m_ffn1_w3, m_ffn1_w2, m_mix_norm, m_w_in, m_b_gate, m_q_norm, m_k_norm, m_rel_bias, m_w_branch_a, m_w_branch_b, m_w_out, m_ffn2_norm, m_ffn2_w1, m_ffn2_w3, m_ffn2_w2, m_final_norm, v_ffn1_norm, v_ffn1_w1, v_ffn1_w3, v_ffn1_w2, v_mix_norm, v_w_in, v_b_gate, v_q_norm, v_k_norm, v_rel_bias, v_w_branch_a, v_w_branch_b, v_w_out, v_ffn2_norm, v_ffn2_w1, v_ffn2_w3, v_ffn2_w2, v_final_norm):
    args = dict(locals())
    w = {n: args[n] for n in ORDER}
    m = {n: args["m_" + n] for n in ORDER}
    v = {n: args["v_" + n] for n in ORDER}
    D = x.shape[2]

    blocks = (
        ("ffn1_up", jnp.concatenate([ffn1_w1[0].T, ffn1_w3[0].T], axis=0)),
        ("ffn1_down", ffn1_w2[0]),
        ("mix_in", w_in[0]),
        ("mix_rest", jnp.concatenate([w_branch_b[0], w_out[0], w_branch_a[0].reshape(REST_ROWS - REST_WA, D)], axis=0)),
        ("ffn2", jnp.concatenate([ffn2_w1[0].T, ffn2_w3[0].T, ffn2_w2[0]], axis=0)),
    )
    direct = ("mix_rest", "ffn2")
    started = all_gather_start_all([(b.astype(BF16), tag in direct) for tag, b in blocks], "all_gather_start")
    gathers = {tag: s for (tag, _), s in zip(blocks, started)}
    start_token = started[0][4]

    def gathered(tag):
        def get(after):
            if tag in direct:
                return all_gather_place_own(*_split_wait("all_gather_" + tag + "_wait", gathers[tag], N_DEV - 1, after),
                                            "all_gather_" + tag + "_own")
            return all_gather_finish(*_split_wait("all_gather_" + tag + "_wait", gathers[tag], 4, after),
                                     "all_gather_" + tag + "_finish")
        return get

    core = lax.axis_index("c").astype(jnp.int32).reshape(1)
    chip = (2 * lax.axis_index("x") + lax.axis_index("y")).astype(jnp.int32).reshape(1)
    device = 2 * chip + core
    exchanges = {}

    def emit(tag, gw):
        if tag == "ffn1":
            (theirs,) = reduce_scatter_pair([gw], "reduce_scatter_pair_" + tag)
            part = pair_add(gw, theirs, core, "pair_add_" + tag)
            exchanges[tag] = reduce_scatter_start(part, "reduce_scatter_" + tag + "_start")
        else:
            exchanges[tag] = reduce_scatter_start_direct(gw, "reduce_scatter_" + tag + "_start")
        return exchanges[tag][4]

    small_p = dict(ffn1_norm=ffn1_norm, mix_norm=mix_norm, b_gate=b_gate, q_norm=q_norm, k_norm=k_norm,
                   rel_bias=rel_bias, ffn2_norm=ffn2_norm, final_norm=final_norm.reshape(1, D))
    loss_p, grad_x, small_g = local_step(x[0], loss_target[0], small_p, gathered("ffn1_up"), gathered("ffn1_down"),
                                         gathered("mix_in"), gathered("mix_rest"), gathered("ffn2"), emit, start_token)

    def landed(tag, after):
        n_others, me = (3, chip) if tag == "ffn1" else (N_DEV - 1, device)
        return tuple(_split_wait("reduce_scatter_" + tag + "_wait", exchanges[tag], n_others, after)) + (me,)

    grads, delta, new_m, new_v = {}, {}, {}, {}

    def finish(n, part, land, me, off, blk, transposed=False):
        shp = w[n].shape
        if transposed:
            to2 = lambda a: a.reshape(shp[-2], shp[-1]).T
            back = lambda a: a.T.reshape(shp)
        else:
            to2 = lambda a: a.reshape(shp[-2], shp[-1])
            back = lambda a: a.reshape(shp)
        res = sum_adamw(part, land, me, off, blk, to2(w[n]), to2(m[n]), to2(v[n]), "update_" + n)
        grads[n], delta[n], new_m[n], new_v[n] = [back(a) for a in res]

    last_token = exchanges["ffn1"][4]
    for tag, after in (("ffn2", last_token), ("ffn1", grad_x)):
        group = landed(tag, after)
        finish(tag + "_w1", *group, 0, FFN_SHARD, transposed=True)
        finish(tag + "_w3", *group, FFN_SHARD, FFN_SHARD, transposed=True)
        finish(tag + "_w2", *group, 2 * FFN_SHARD, FFN_SHARD)
        if tag == "ffn2":
            group_m = landed("mix", last_token)
            finish("w_in", *group_m, MIX_WIN, LANES)
            finish("w_branch_b", *group_m, MIX_WB, LANES)
            finish("w_out", *group_m, MIX_WOUT, LANES)
            grads["w_branch_a"] = sum_landed(*group_m, MIX_WA, MIX_ROWS - MIX_WA, MIX_ROWS - MIX_WA,
                                             "w_branch_a_sum").reshape(w_branch_a.shape)
    loss_row = jnp.pad(loss_p, ((0, 0), (0, D - LANES)))
    smalls = small_all_gather(_pack_small(small_g, loss_row), new_v["w_in"])
    small_sum = sum_slots(smalls, 0, N_DEV, N_DEV, "small_sum")
    small_shapes = {n: w[n].shape for n in SMALL}
    grads.update(_unpack_small(small_sum, small_shapes))
    loss = small_sum[7, 0]

    n = "w_branch_a"
    two_d = lambda a: a.reshape(w[n].shape[-2], w[n].shape[-1])
    d_, m_, v_ = adamw(two_d(w[n]), two_d(grads[n]), two_d(m[n]), two_d(v[n]), "adamw_" + n)
    delta[n], new_m[n], new_v[n] = [a.reshape(w[n].shape) for a in (d_, m_, v_)]
    zero_row = jnp.zeros((1, D), F32)
    pack = lambda t: _pack_small({n: t[n] for n in SMALL}, zero_row)
    d_, m_, v_ = adamw(pack(w), small_sum, pack(m), pack(v), "adamw_small")
    for src, dst in ((d_, delta), (m_, new_m), (v_, new_v)):
        dst.update(_unpack_small(src, small_shapes))

    return (loss, grad_x[None], *[grads[n] for n in ORDER], *[delta[n] for n in ORDER],
            *[new_m[n] for n in ORDER], *[new_v[n] for n in ORDER])
```

```python
import math

import jax
import jax.numpy as jnp
from jax import lax
from jax.experimental import pallas as pl
from jax.experimental.pallas import tpu as pltpu

F32 = jnp.float32
BF16 = jnp.bfloat16
MESH = pl.DeviceIdType.MESH

V7X_VMEM_BYTES = 64 * 1024 * 1024
VMEM_LIMIT = V7X_VMEM_BYTES - 8 * 1024 * 1024
LANES = 128

N_DEV = 8
EPS = 1e-6
NEG_INF = -1e30

DILATIONS = (1, 4, 16)
HALF_WINDOW = 64
HEAD_DIM_A = 64
HEADS_PER_GROUP_A = 8
GROUP_WIDTH_A = 512
A_QKV_WIDTH = 4608
A_TQ = 128
A_WIN = A_TQ + 2 * HALF_WINDOW
A_UNROLL = 8
A_SCALE = HEAD_DIM_A ** -0.5
WGRAD_TK = 2048
HEAD_DIM_B = 128
N_HEADS_B = 8
N_KV_B = 2
GQA_GROUP_B = 4
GRID_W = 64
ROPE_THETA = 10000.0
B_TQ_FWD = 256
B_TQ_BWD = 512
B_HEADS_PER_STEP = 4
LOG2E = 1.4426950408889634
B_Q_PRESCALE = HEAD_DIM_B ** -0.5 * LOG2E
N_BUCKETS = 32
MAX_DISTANCE = 1024
PB_GATE_A = 1536

ADAM_LR = 0.001
ADAM_B1 = 0.9
ADAM_B2 = 0.999
ADAM_EPS = 1e-08
ADAM_WD = 0.01
ADAM_STEP = 10

FFN_SHARD = 352
MIX_WIN, MIX_WB, MIX_WOUT, MIX_WA = 0, 1024, 1152, 1280
MIX_ROWS = 1344
REST_WB, REST_WOUT, REST_WA, REST_ROWS = 0, 128, 256, 320


def _dot(a, b, ca=1, cb=0):
    return lax.dot_general(a, b, (((ca,), (cb,)), ((), ())), preferred_element_type=F32)


def _call(name, body, grid, ins, outs, scratch=(), sem=None, aliases=None):
    ins = [tuple(i) + (None,) * (4 - len(i)) for i in ins]
    res = pl.pallas_call(
        body,
        out_shape=[jax.ShapeDtypeStruct(s, d) for (s, d, _, _) in outs],
        grid=grid,
        in_specs=[pl.BlockSpec(bs, im, pipeline_mode=pm) for (_, bs, im, pm) in ins],
        out_specs=[pl.BlockSpec(bs, im) for (_, _, bs, im) in outs],
        scratch_shapes=list(scratch),
        name=name,
        input_output_aliases=aliases or {},
        compiler_params=pltpu.CompilerParams(dimension_semantics=sem, vmem_limit_bytes=VMEM_LIMIT),
    )(*[i[0] for i in ins])
    return res


def _sigmoid(x):
    return 0.5 * jnp.tanh(0.5 * x) + 0.5


def _position():
    return lax.axis_index("x"), lax.axis_index("y"), lax.axis_index("c")


def _hbm_specs(n):
    return [pl.BlockSpec(memory_space=pl.ANY) for _ in range(n)]


PAIR_BUFFERS = 4


def reduce_scatter_pair(grads, name):
    n = len(grads)
    C = grads[0].shape[2]
    half = [g.shape[1] // 2 for g in grads]
    chunks = [(i, q, hf) for i in range(n) for q in range(4) for hf in range(2)]
    nb = PAIR_BUFFERS

    def body(*refs):
        ins, theirs = refs[:n], refs[n:2 * n]
        buf, load_sems, send_sems, recv_sems = refs[2 * n:]
        x, y, c = _position()
        sibling = (x, y, 1 - c)

        def load(k):
            i, q, hf = chunks[k]
            r = half[i]
            return pltpu.make_async_copy(ins[i].at[2 * q + (1 - c), pl.ds(hf * r, r), :],
                                         buf.at[k % nb, pl.ds(0, r), :], load_sems.at[k % nb])

        def send(k):
            i, q, hf = chunks[k]
            r = half[i]
            return pltpu.make_async_remote_copy(
                src_ref=buf.at[k % nb, pl.ds(0, r), :], dst_ref=theirs[i].at[q, pl.ds(hf * r, r), :],
                send_sem=send_sems.at[k % nb], recv_sem=recv_sems.at[i],
                device_id=sibling, device_id_type=MESH)

        for k in range(len(chunks) + 1):
            if k < len(chunks):
                if k >= nb:
                    send(k - nb).wait_send()
                load(k).start()
            if k >= 1:
                load(k - 1).wait()
                send(k - 1).start()
        for k in range(max(0, len(chunks) - nb), len(chunks)):
            send(k).wait_send()
        for i in range(n):
            pltpu.make_async_remote_copy(
                src_ref=theirs[i], dst_ref=theirs[i], send_sem=send_sems.at[0], recv_sem=recv_sems.at[i],
                device_id=sibling, device_id_type=MESH).wait_recv()

    return pl.pallas_call(
        body,
        out_shape=[jax.ShapeDtypeStruct((4,) + g.shape[1:], g.dtype) for g in grads],
        in_specs=_hbm_specs(n),
        out_specs=_hbm_specs(n),
        scratch_shapes=[pltpu.VMEM((nb, max(half), C), grads[0].dtype), pltpu.SemaphoreType.DMA((nb,)),
                        pltpu.SemaphoreType.DMA((nb,)), pltpu.SemaphoreType.DMA((n,))],
        name=name,
        compiler_params=pltpu.CompilerParams(vmem_limit_bytes=VMEM_LIMIT),
    )(*grads)


_HBM_SPEC = pl.BlockSpec(memory_space=pltpu.HBM)
_SEM_SPEC = pl.BlockSpec(memory_space=pltpu.SEMAPHORE)
_TOKEN_SPEC = pl.BlockSpec(memory_space=pltpu.VMEM)
_DATAFLOW = pltpu.SideEffectType.DATAFLOW_SIDE_EFFECTING


def _split_start_many(name, exchanges):
    n = len(exchanges)

    def full_body(*refs):
        srcs, lands = refs[:n], refs[n:2 * n]
        sems = refs[2 * n:4 * n]
        token = refs[-1]
        for i, (body, _, _) in enumerate(exchanges):
            body(srcs[i], lands[i], sems[2 * i], sems[2 * i + 1])
        token[...] = jnp.zeros_like(token)

    srcs = [pltpu.with_memory_space_constraint(src, pltpu.HBM) for _, src, _ in exchanges]
    lands = [pltpu.with_memory_space_constraint(lax.empty(shape, src.dtype), pltpu.HBM)
             for _, src, shape in exchanges]
    res = pl.pallas_call(
        full_body, name=name,
        out_shape=(pltpu.SemaphoreType.DMA(()),) * (2 * n)
        + tuple(pltpu.HBM(a.shape, a.dtype) for a in srcs + lands) + (jax.ShapeDtypeStruct((8, LANES), F32),),
        in_specs=(_HBM_SPEC,) * (2 * n),
        out_specs=(_SEM_SPEC,) * (2 * n) + (_HBM_SPEC,) * (2 * n) + (_TOKEN_SPEC,),
        input_output_aliases={i: 2 * n + i for i in range(2 * n)},
        compiler_params=pltpu.CompilerParams(has_side_effects=_DATAFLOW),
    )(*srcs, *lands)
    return [(res[2 * i], res[2 * i + 1], res[2 * n + i], res[3 * n + i], res[-1]) for i in range(n)]


def _split_start(name, body, src, land_shape):
    return _split_start_many(name, [(body, src, land_shape)])[0]


def _split_wait(name, started, n_blocks, after):
    send_sem, recv_sem, src_thru, land_thru, _ = started
    after = after if isinstance(after, tuple) else (after,)

    def body(src_ref, land_ref, send_sem, recv_sem, *rest):
        x, y, c = _position()
        blocks = land_ref.at[pl.ds(0, n_blocks)]
        copy = pltpu.make_async_remote_copy(src_ref=blocks, dst_ref=blocks, send_sem=send_sem, recv_sem=recv_sem,
                                            device_id=(x, y, c), device_id_type=MESH)
        copy.wait_send()
        copy.wait_recv()

    return pl.pallas_call(
        body, name=name,
        out_shape=(pltpu.HBM(src_thru.shape, src_thru.dtype), pltpu.HBM(land_thru.shape, land_thru.dtype)),
        in_specs=(_HBM_SPEC, _HBM_SPEC, _SEM_SPEC, _SEM_SPEC) + (pl.BlockSpec(memory_space=pl.ANY),) * len(after),
        out_specs=(_HBM_SPEC, _HBM_SPEC),
        input_output_aliases={0: 0, 1: 1},
        compiler_params=pltpu.CompilerParams(has_side_effects=_DATAFLOW),
    )(src_thru, land_thru, send_sem, recv_sem, *after)


def all_gather_start_all(blocks, name):
    def starter(direct):
        def body(b_ref, land_ref, send_sem, recv_sem):
            x, y, c = _position()
            peers = _other_devices(x, y, c) if direct else [(x, y, 1 - c), (1 - x, y, c), (x, 1 - y, c),
                                                            (1 - x, 1 - y, c)]
            for peer in peers:
                pltpu.make_async_remote_copy(src_ref=b_ref, dst_ref=land_ref.at[4 * x + 2 * y + c],
                                             send_sem=send_sem, recv_sem=recv_sem,
                                             device_id=peer, device_id_type=MESH).start()
        return body

    return _split_start_many(name, [(starter(direct), block, (N_DEV,) + block.shape) for block, direct in blocks])


def all_gather_finish(block, land, name):
    R, C = block.shape

    def body(b_ref, land_in, land_ref, stage, load_sems, send_sems, recv_sems, own_sem):
        x, y, c = _position()
        sibling = (x, y, 1 - c)
        chips = [(1 - x, y), (x, 1 - y), (1 - x, 1 - y)]
        own_in = pltpu.make_async_copy(b_ref, stage.at[3], load_sems.at[3])
        own_in.start()
        loads = [pltpu.make_async_copy(land_in.at[4 * px + 2 * py + c], stage.at[j], load_sems.at[j])
                 for j, (px, py) in enumerate(chips)]
        for ld in loads:
            ld.start()
        sends = []
        for j, (px, py) in enumerate(chips):
            loads[j].wait()
            dst = land_ref.at[4 * px + 2 * py + c]
            cp = pltpu.make_async_remote_copy(src_ref=stage.at[j], dst_ref=dst, send_sem=send_sems.at[j],
                                              recv_sem=recv_sems.at[j], device_id=sibling, device_id_type=MESH)
            cp.start()
            sends.append(cp)
        own_in.wait()
        own_out = pltpu.make_async_copy(stage.at[3], land_ref.at[4 * x + 2 * y + c], own_sem)
        own_out.start()
        for j, (px, py) in enumerate(chips):
            dst = land_ref.at[4 * px + 2 * py + (1 - c)]
            pltpu.make_async_remote_copy(src_ref=stage.at[j], dst_ref=dst, send_sem=send_sems.at[j],
                                         recv_sem=recv_sems.at[j], device_id=sibling,
                                         device_id_type=MESH).wait_recv()
        for cp in sends:
            cp.wait_send()
        own_out.wait()

    return pl.pallas_call(
        body,
        out_shape=jax.ShapeDtypeStruct(land.shape, land.dtype),
        in_specs=_hbm_specs(2),
        out_specs=pl.BlockSpec(memory_space=pl.ANY),
        scratch_shapes=[pltpu.VMEM((4, R, C), block.dtype), pltpu.SemaphoreType.DMA((4,)),
                        pltpu.SemaphoreType.DMA((3,)), pltpu.SemaphoreType.DMA((3,)), pltpu.SemaphoreType.DMA],
        input_output_aliases={1: 0},
        name=name,
        compiler_params=pltpu.CompilerParams(vmem_limit_bytes=VMEM_LIMIT),
    )(block, land)


def reduce_scatter_start(parts, name):
    def body(p_ref, land_ref, send_sem, recv_sem):
        x, y, c = _position()
        for px, py in [(1 - x, y), (x, 1 - y), (1 - x, 1 - y)]:
            pltpu.make_async_remote_copy(src_ref=p_ref.at[2 * px + py], dst_ref=land_ref.at[2 * x + y],
                                         send_sem=send_sem, recv_sem=recv_sem,
                                         device_id=(px, py, c), device_id_type=MESH).start()

    return _split_start(name, body, parts, parts.shape)


def _other_devices(x, y, c):
    return [(1 - x if k & 4 else x, 1 - y if k & 2 else y, 1 - c if k & 1 else c) for k in range(1, N_DEV)]


def all_gather_place_own(block, land, name):
    R, C = block.shape

    def body(b_ref, land_in, land_ref, stage, sems):
        x, y, c = _position()
        load = pltpu.make_async_copy(b_ref, stage, sems.at[0])
        load.start()
        load.wait()
        store = pltpu.make_async_copy(stage, land_ref.at[4 * x + 2 * y + c], sems.at[1])
        store.start()
        store.wait()

    return pl.pallas_call(
        body,
        out_shape=jax.ShapeDtypeStruct(land.shape, land.dtype),
        in_specs=_hbm_specs(2),
        out_specs=pl.BlockSpec(memory_space=pl.ANY),
        scratch_shapes=[pltpu.VMEM((R, C), block.dtype), pltpu.SemaphoreType.DMA((2,))],
        input_output_aliases={1: 0},
        name=name,
    )(block, land)


def reduce_scatter_start_direct(grads, name):
    def body(g_ref, land_ref, send_sem, recv_sem):
        x, y, c = _position()
        for px, py, pc in _other_devices(x, y, c):
            pltpu.make_async_remote_copy(src_ref=g_ref.at[4 * px + 2 * py + pc],
                                         dst_ref=land_ref.at[4 * x + 2 * y + c],
                                         send_sem=send_sem, recv_sem=recv_sem,
                                         device_id=(px, py, pc), device_id_type=MESH).start()

    return _split_start(name, body, grads, grads.shape)


def small_all_gather(small, after):
    def body(small_ref, after_ref, smalls, s_send, s_recv, s_local):
        x, y, c = _position()
        me = 4 * x + 2 * y + c
        lc = pltpu.make_async_copy(small_ref, smalls.at[me], s_local)
        lc.start()
        remote = []
        k = 0
        for dx in (0, 1):
            for dy in (0, 1):
                for dc in (0, 1):
                    if dx + dy + dc == 0:
                        continue
                    peer = (1 - x if dx else x, 1 - y if dy else y, 1 - c if dc else c)
                    rc = pltpu.make_async_remote_copy(
                        src_ref=small_ref, dst_ref=smalls.at[me],
                        send_sem=s_send.at[k], recv_sem=s_recv.at[k],
                        device_id=peer, device_id_type=MESH)
                    rc.start()
                    remote.append(rc)
                    k += 1
        for rc in remote:
            rc.wait()
        lc.wait()

    return pl.pallas_call(
        body,
        out_shape=jax.ShapeDtypeStruct((N_DEV,) + small.shape, small.dtype),
        in_specs=_hbm_specs(2),
        out_specs=pl.BlockSpec(memory_space=pl.ANY),
        scratch_shapes=[pltpu.SemaphoreType.DMA((7,)), pltpu.SemaphoreType.DMA((7,)), pltpu.SemaphoreType.DMA],
        name="small_all_gather",
    )(small, after)


def pair_add(grads, theirs, core, name):
    _, R, C = theirs.shape
    tr = R // 2

    def body(c_ref, a_ref, b_ref, o_ref):
        o_ref[...] = (a_ref[...].astype(F32) + b_ref[...].astype(F32)).astype(BF16)

    return pl.pallas_call(
        body,
        out_shape=jax.ShapeDtypeStruct(theirs.shape, BF16),
        grid_spec=pltpu.PrefetchScalarGridSpec(
            num_scalar_prefetch=1, grid=(4, R // tr),
            in_specs=[pl.BlockSpec((None, tr, C), lambda q, i, c: (2 * q + c[0], i, 0)),
                      pl.BlockSpec((None, tr, C), lambda q, i, c: (q, i, 0))],
            out_specs=pl.BlockSpec((None, tr, C), lambda q, i, c: (q, i, 0))),
        name=name,
        compiler_params=pltpu.CompilerParams(dimension_semantics=("parallel", "parallel"),
                                             vmem_limit_bytes=VMEM_LIMIT),
    )(core, grads, theirs)


def sum_slots(recv, off, rows, blk, name):
    nq, _, C = recv.shape
    ob = off // blk

    def body(r_ref, o_ref):
        acc = r_ref[0].astype(F32)
        for q in range(1, nq):
            acc = acc + r_ref[q].astype(F32)
        o_ref[...] = acc

    return _call(name, body, (rows // blk,),
                 [(recv, (nq, blk, C), lambda i: (0, ob + i, 0))],
                 [((rows, C), F32, (blk, C), lambda i: (i, 0))], sem=("parallel",))[0]


def _sum_terms(refs):
    acc = refs[0][...].astype(F32)
    for r in refs[1:]:
        acc = acc + r[...].astype(F32)
    return acc


def sum_landed(own, land, me, off, rows, blk, name):
    n, _, C = land.shape
    ob = off // blk

    def body(c_ref, *refs):
        refs[n][...] = _sum_terms(refs[:n])

    def entry(flip):
        return pl.BlockSpec((None, blk, C), lambda i, c: (c[0] ^ flip, ob + i, 0))

    return pl.pallas_call(
        body,
        out_shape=jax.ShapeDtypeStruct((rows, C), F32),
        grid_spec=pltpu.PrefetchScalarGridSpec(
            num_scalar_prefetch=1, grid=(rows // blk,),
            in_specs=[entry(k) for k in range(n)],
            out_specs=pl.BlockSpec((blk, C), lambda i, c: (i, 0))),
        name=name,
        compiler_params=pltpu.CompilerParams(dimension_semantics=("parallel",), vmem_limit_bytes=VMEM_LIMIT),
    )(me, own, *([land] * (n - 1)))


def _adamw_update(wv, gv, mv, vv):
    nm = ADAM_B1 * mv + (1.0 - ADAM_B1) * gv
    nv = ADAM_B2 * vv + (1.0 - ADAM_B2) * (gv * gv)
    c1 = 1.0 / (1.0 - ADAM_B1 ** ADAM_STEP)
    c2 = 1.0 / (1.0 - ADAM_B2 ** ADAM_STEP)
    return -ADAM_LR * ((nm * c1) / (jnp.sqrt(nv * c2) + ADAM_EPS) + ADAM_WD * wv), nm, nv


def sum_adamw(own, land, me, off, blk, w, m, v, name):
    rows, C = w.shape
    n = land.shape[0]
    ob = off // blk

    def body(c_ref, *refs):
        w_ref, m_ref, v_ref, g_out, d_out, m_out, v_out = refs[n:]
        gv = _sum_terms(refs[:n])
        g_out[...] = gv
        d_out[...], m_out[...], v_out[...] = _adamw_update(w_ref[...], gv, m_ref[...], v_ref[...])

    def entry(flip):
        return pl.BlockSpec((None, blk, C), lambda i, c: (c[0] ^ flip, ob + i, 0))

    plain = pl.BlockSpec((blk, C), lambda i, c: (i, 0))
    return pl.pallas_call(
        body,
        out_shape=[jax.ShapeDtypeStruct((rows, C), F32)] * 4,
        grid_spec=pltpu.PrefetchScalarGridSpec(
            num_scalar_prefetch=1, grid=(rows // blk,),
            in_specs=[entry(k) for k in range(n)] + [plain, plain, plain],
            out_specs=[plain] * 4),
        name=name,
        compiler_params=pltpu.CompilerParams(dimension_semantics=("parallel",), vmem_limit_bytes=VMEM_LIMIT),
    )(me, own, *([land] * (n - 1)), w, m, v)


def adamw(w, g, m, v, name):
    R, C = w.shape
    tr = R
    for cand in (256, 128, 64, 32, 16, 8):
        if R % cand == 0 and R > cand:
            tr = cand
            break

    def body(w_ref, g_ref, m_ref, v_ref, d_ref, nm_ref, nv_ref):
        d_ref[...], nm_ref[...], nv_ref[...] = _adamw_update(w_ref[...], g_ref[...], m_ref[...], v_ref[...])

    spec = ((tr, C), lambda i: (i, 0))
    out = ((R, C), F32) + spec
    return _call(name, body, (R // tr,), [(w,) + spec, (g,) + spec, (m,) + spec, (v,) + spec],
                 [out, out, out], sem=("parallel",))


def _rms_tile(xv, gv):
    r = lax.rsqrt(jnp.mean(xv * xv, axis=-1, keepdims=True) + EPS)
    return (xv * r * gv).astype(BF16)


def rms_fwd(x, g, name):
    S, D = x.shape
    tr = 512

    def body(x_ref, g_ref, o_ref):
        o_ref[...] = _rms_tile(x_ref[...], g_ref[...])

    return _call(name, body, (S // tr,),
                 [(x, (tr, D), lambda i: (i, 0)), (g, (1, D), lambda i: (0, 0))],
                 [((S, D), BF16, (tr, D), lambda i: (i, 0))], sem=("parallel",))[0]


def _rms_bwd_tile(dn, xv, gv):
    r = lax.rsqrt(jnp.mean(xv * xv, axis=-1, keepdims=True) + EPS)
    xh = xv * r
    dxh = dn * gv
    dx = r * (dxh - xh * jnp.mean(dxh * xh, axis=-1, keepdims=True))
    return dx, dn * xh


def _final_loss_tile(xv, tv, gv):
    D = xv.shape[1]
    r = lax.rsqrt(jnp.mean(xv * xv, axis=-1, keepdims=True) + EPS)
    xh = xv * r
    e = xh * gv - tv
    part = 0.5 * jnp.sum(jnp.sum(e * e, axis=-1, keepdims=True) * (1.0 / D), axis=0, keepdims=True)
    dy = e * (1.0 / D)
    dxh = dy * gv
    dx = r * (dxh - xh * jnp.mean(dxh * xh, axis=-1, keepdims=True))
    return part, dx, jnp.sum(dy * xh, axis=0, keepdims=True)


FFN_TF = 4 * FFN_SHARD


def _ffn_pick(G, which):
    if isinstance(G, tuple):
        return (G[0], which) if which < 2 else (G[1], 0)
    return G, which


def _ffn_whole_w_spec(G, which):
    arr, blk = _ffn_pick(G, which)
    return (arr, (N_DEV, FFN_SHARD, arr.shape[2]), lambda *idx: (0, blk, 0), pl.Buffered(1))


def _ffn_hidden(a, b):
    av, bv = a.astype(F32), b.astype(F32)
    return (av * _sigmoid(av) * bv).astype(BF16)


def ffn_up(n, G, name):
    S, D = n.shape
    F = N_DEV * FFN_SHARD
    tm = 256

    def body(n_ref, w1_ref, w3_ref, abh_ref):
        nv = n_ref[...]
        a = _dot(nv, w1_ref[...].reshape(F, D), 1, 1).astype(BF16)
        b = _dot(nv, w3_ref[...].reshape(F, D), 1, 1).astype(BF16)
        abh_ref[0] = a
        abh_ref[1] = b
        abh_ref[2] = _ffn_hidden(a, b)

    return _call(name, body, (S // tm,),
                 [(n, (tm, D), lambda i: (i, 0)),
                  _ffn_whole_w_spec(G, 0), _ffn_whole_w_spec(G, 1)],
                 [((3, S, F), BF16, (3, tm, F), lambda i: (0, i, 0))],
                 sem=("parallel",))[0]


def ffn_down(abh, G, x, g_next, name):
    _, S, F = abh.shape
    D = x.shape[1]
    tm = 512

    def body(h_ref, w2_ref, x_ref, g_ref, o_ref, n_ref):
        xo = x_ref[...] + 0.5 * _dot(h_ref[...], w2_ref[...].reshape(F, D))
        o_ref[...] = xo
        n_ref[...] = _rms_tile(xo, g_ref[...])

    tile = ((tm, D), lambda i: (i, 0))
    return _call(name, body, (S // tm,),
                 [(abh, (None, tm, F), lambda i: (2, i, 0)), _ffn_whole_w_spec(G, 2),
                  (x,) + tile, (g_next, (1, D), lambda i: (0, 0))],
                 [((S, D), F32) + tile, ((S, D), BF16) + tile], sem=("parallel",))


def ffn_last(x, g, G, tgt, g_final, name):
    S, D = x.shape
    F = N_DEV * FFN_SHARD
    tm = 256

    def body(x_ref, g_ref, w1_ref, w3_ref, w2_ref, t_ref, gf_ref,
             n_ref, abh_ref, dxo_ref, dab_ref, dx_ref, dxb_ref, dg_ref, l_ref, dgf_ref):
        i = pl.program_id(0)
        xv, gv = x_ref[...], g_ref[...]
        chunks = [(slice(4 * f, 4 * f + 4), slice(f * FFN_TF, (f + 1) * FFN_TF)) for f in range(F // FFN_TF)]
        weight = lambda w_ref, slots: w_ref[slots].reshape(FFN_TF, D)
        nv = _rms_tile(xv, gv)
        n_ref[...] = nv
        y = None
        for slots, cols in chunks:
            a = _dot(nv, weight(w1_ref, slots), 1, 1).astype(BF16)
            b = _dot(nv, weight(w3_ref, slots), 1, 1).astype(BF16)
            h = _ffn_hidden(a, b)
            abh_ref[0, :, cols] = a
            abh_ref[1, :, cols] = b
            abh_ref[2, :, cols] = h
            t = _dot(h, weight(w2_ref, slots))
            y = t if y is None else y + t
        part, dxo, dgfp = _final_loss_tile(xv + 0.5 * y, t_ref[...], gf_ref[...])
        dxo_b = dxo.astype(BF16)
        dxo_ref[...] = dxo_b
        dn = None
        for slots, cols in chunks:
            dh = 0.5 * _dot(dxo_b, weight(w2_ref, slots), 1, 1)
            da, db = _ffn_hidden_grads(dh, abh_ref[0, :, cols].astype(F32), abh_ref[1, :, cols].astype(F32))
            da, db = da.astype(BF16), db.astype(BF16)
            dab_ref[0, :, cols] = da
            dab_ref[1, :, cols] = db
            t = _dot(da, weight(w1_ref, slots)) + _dot(db, weight(w3_ref, slots))
            dn = t if dn is None else dn + t
        dx, dgt = _rms_bwd_tile(dn, xv, gv)
        dx = dxo + dx
        dx_ref[...] = dx
        dxb_ref[...] = dx.astype(BF16)
        dgp = jnp.sum(dgt, axis=0, keepdims=True)

        @pl.when(i == 0)
        def _():
            dg_ref[...] = dgp
            l_ref[...] = jnp.broadcast_to(part, l_ref.shape)
            dgf_ref[...] = dgfp

        @pl.when(i > 0)
        def _():
            dg_ref[...] += dgp
            l_ref[...] += jnp.broadcast_to(part, l_ref.shape)
            dgf_ref[...] += dgfp

    tile = ((tm, D), lambda i: (i, 0))
    gain = ((1, D), lambda i: (0, 0))
    return _call(name, body, (S // tm,),
                 [(x,) + tile, (g,) + gain,
                  _ffn_whole_w_spec(G, 0), _ffn_whole_w_spec(G, 1), _ffn_whole_w_spec(G, 2),
                  (tgt,) + tile, (g_final,) + gain],
                 [((S, D), BF16) + tile, ((3, S, F), BF16, (3, tm, F), lambda i: (0, i, 0)),
                  ((S, D), BF16) + tile, ((2, S, F), BF16, (2, tm, F), lambda i: (0, i, 0)),
                  ((S, D), F32) + tile, ((S, D), BF16) + tile, ((1, D), F32) + gain,
                  ((1, LANES), F32, (1, LANES), lambda i: (0, 0)), ((1, D), F32) + gain],
                 sem=("arbitrary",))


def _ffn_hidden_grads(dh, av, bv):
    sig = _sigmoid(av)
    return dh * bv * (sig * (1.0 + av * (1.0 - sig))), dh * (av * sig)


def ffn_bwd_hidden(dxo, abh, G, name):
    _, S, F = abh.shape
    D = dxo.shape[1]
    tm = 256

    def body(d_ref, w2_ref, ab_ref, o_ref):
        dh = 0.5 * _dot(d_ref[...].astype(BF16), w2_ref[...].reshape(F, D), 1, 1)
        da, db = _ffn_hidden_grads(dh, ab_ref[0].astype(F32), ab_ref[1].astype(F32))
        o_ref[0] = da.astype(BF16)
        o_ref[1] = db.astype(BF16)

    return _call(name + "_down_bwd", body, (S // tm,),
                 [(dxo, (tm, D), lambda i: (i, 0)), _ffn_whole_w_spec(G, 2),
                  (abh, (2, tm, F), lambda i: (0, i, 0))],
                 [((2, S, F), BF16, (2, tm, F), lambda i: (0, i, 0))],
                 sem=("parallel",))[0]


def ffn_bwd_weights(dxo, abh, dab, n, name):
    _, S, F = abh.shape
    D = dxo.shape[1]
    nf = F // FFN_TF
    tk = WGRAD_TK
    nk = S // tk
    gshape = (N_DEV, 3 * FFN_SHARD, D)

    def dw2_body(h_ref, d_ref, o_ref, acc_ref):
        k = pl.program_id(1)
        p = _dot(h_ref[...], d_ref[...].astype(BF16), 0, 0)

        @pl.when(k == 0)
        def _():
            acc_ref[...] = p

        @pl.when(k > 0)
        def _():
            acc_ref[...] += p

        @pl.when(k == nk - 1)
        def _():
            o_ref[...] = (0.5 * acc_ref[...]).astype(BF16).reshape(4, FFN_SHARD, D)

    gw = _call(name + "_dw2", dw2_body, (nf, nk),
               [(abh, (None, tk, FFN_TF), lambda j, k: (2, k, j)), (dxo, (tk, D), lambda j, k: (k, 0))],
               [(gshape, BF16, (4, FFN_SHARD, D), lambda j, k: (j, 2, 0))],
               scratch=[pltpu.VMEM((FFN_TF, D), F32)], sem=("parallel", "arbitrary"))[0]

    def dw13_body(gw_ref, dab_ref, n_ref, o_ref):
        o_ref[...] = _dot(dab_ref[...], n_ref[...], 0, 0).astype(BF16).reshape(4, FFN_SHARD, D)

    gw = pl.pallas_call(
        dw13_body,
        out_shape=jax.ShapeDtypeStruct(gshape, BF16),
        grid=(2, nf),
        in_specs=[pl.BlockSpec(memory_space=pl.ANY),
                  pl.BlockSpec((None, S, FFN_TF), lambda w, j: (w, 0, j)),
                  pl.BlockSpec((S, D), lambda w, j: (0, 0))],
        out_specs=pl.BlockSpec((4, FFN_SHARD, D), lambda w, j: (j, w, 0)),
        input_output_aliases={0: 0},
        name=name + "_dw13",
        compiler_params=pltpu.CompilerParams(dimension_semantics=("parallel", "parallel"),
                                             vmem_limit_bytes=VMEM_LIMIT),
    )(gw, dab, n)
    return gw


def ffn_bwd_input(dab, G, x_in, g, dxo, name):
    _, S, F = dab.shape
    D = x_in.shape[1]
    tm = 256

    def dn_body(dab_ref, w1_ref, w3_ref, x_ref, d_ref, g_ref, dx_ref, dg_ref):
        i = pl.program_id(0)
        dn = _dot(dab_ref[0], w1_ref[...].reshape(F, D)) + _dot(dab_ref[1], w3_ref[...].reshape(F, D))
        dx, dgt = _rms_bwd_tile(dn, x_ref[...], g_ref[...])
        dx_ref[...] = d_ref[...] + dx
        dgp = jnp.sum(dgt, axis=0, keepdims=True)

        @pl.when(i == 0)
        def _():
            dg_ref[...] = dgp

        @pl.when(i > 0)
        def _():
            dg_ref[...] += dgp

    tile = ((tm, D), lambda i: (i, 0))
    return _call(name + "_dn", dn_body, (S // tm,),
                 [(dab, (2, tm, F), lambda i: (0, i, 0)),
                  _ffn_whole_w_spec(G, 0), _ffn_whole_w_spec(G, 1),
                  (x_in,) + tile, (dxo,) + tile, (g, (1, D), lambda i: (0, 0))],
                 [((S, D), F32) + tile, ((1, D), F32, (1, D), lambda i: (0, 0))],
                 sem=("arbitrary",))


PROJ_TN = 512


def in_proj(h, Gm, name):
    S, D = h.shape
    n_tiles = N_DEV * Gm.shape[2] // PROJ_TN

    def body(h_ref, w_ref, o_ref):
        o_ref[...] = _dot(h_ref[...], w_ref[...]).astype(BF16)

    return _call(name, body, (n_tiles,),
                 [(h, (S, D), lambda j: (0, 0)),
                  (Gm, (None, D, PROJ_TN), lambda j: (j // 2, 0, j % 2))],
                 [((S, n_tiles * PROJ_TN), BF16, (S, PROJ_TN), lambda j: (0, j))],
                 sem=("parallel",))[0]


def _dproj_pieces(dqkv, dq_b, dkv_b, dgate):
    pieces = [(dqkv[g], [(3 * which + g, (which, 0)) for which in range(3)]) for g in range(3)]
    pieces.append((dq_b, [(9, (None, 0)), (10, (None, 1))]))
    pieces.append((dkv_b, [(11, (None, 0))]))
    pieces.append((dgate, [(12 + 2 * a + b, (a, b)) for a in range(2) for b in range(2)]))
    return pieces


def in_proj_bwd_dw(pieces, h, gm_grads, name):
    S, D = h.shape
    steps = [(n, t, ix) for n, (_, tiles) in enumerate(pieces) for t, ix in tiles]
    n_steps = len(steps)

    def pick(table, j):
        out = table[-1]
        for k in range(len(table) - 2, -1, -1):
            out = jnp.where(j == k, table[k], out)
        return out

    def piece_spec(n, arr):
        own = [k for k, (m, _, _) in enumerate(steps) if m == n]
        at = [steps[min(max(k, own[0]), own[-1])][2] for k in range(n_steps)]
        lead, colb = [ix[0] for ix in at], [ix[1] for ix in at]
        if arr.ndim == 3:
            return (own[0], own[-1]), pl.BlockSpec((None, S, PROJ_TN), lambda j: (pick(lead, j), 0, pick(colb, j)))
        return (own[0], own[-1]), pl.BlockSpec((S, PROJ_TN), lambda j: (0, pick(colb, j)))

    spans, d_specs = zip(*[piece_spec(n, arr) for n, (arr, _) in enumerate(pieces)])
    w_tile = [t for _, t, _ in steps]

    def dw_body(gm_ref, h_ref, *refs):
        o_ref = refs[-1]
        j = pl.program_id(0)
        for d_ref, (first, last) in zip(refs[:-1], spans):
            @pl.when((j >= first) & (j <= last))
            def _(d_ref=d_ref):
                o_ref[...] = _dot(h_ref[...], d_ref[...], 0, 0).astype(BF16)

    return pl.pallas_call(
        dw_body,
        out_shape=jax.ShapeDtypeStruct(gm_grads.shape, BF16),
        grid=(n_steps,),
        in_specs=[pl.BlockSpec(memory_space=pl.ANY),
                  pl.BlockSpec((S, D), lambda j: (0, 0), pipeline_mode=pl.Buffered(1))] + list(d_specs),
        out_specs=pl.BlockSpec((None, D, PROJ_TN), lambda j: (pick(w_tile, j) // 2, 0, pick(w_tile, j) % 2)),
        input_output_aliases={0: 0},
        name=name + "_dw",
        compiler_params=pltpu.CompilerParams(dimension_semantics=("arbitrary",), vmem_limit_bytes=VMEM_LIMIT),
    )(gm_grads, h, *[arr for arr, _ in pieces])


def in_proj_bwd_dh(pieces, Gm, x_in, g, dres, name):
    S, D = x_in.shape
    tm = 256
    C = Gm.shape[2]
    n_sh = N_DEV
    n_p = len(pieces)

    def dh_body(*refs):
        d_refs = refs[:n_p]
        w_ref, x_ref, r_ref, g_ref, dx_ref, dxb_ref, dg_ref = refs[n_p:]
        i = pl.program_id(0)
        p = None
        for d_ref, (arr, tiles) in zip(d_refs, pieces):
            for t, (lead, colb) in tiles:
                cols = slice(colb * PROJ_TN, (colb + 1) * PROJ_TN)
                d = d_ref[:, cols] if lead is None else d_ref[lead, :, cols]
                wcol = (t % 2) * PROJ_TN
                term = _dot(d, w_ref[t // 2, :, wcol:wcol + PROJ_TN], 1, 1)
                p = term if p is None else p + term
        dx, dgt = _rms_bwd_tile(p, x_ref[...], g_ref[...])
        dx = r_ref[...] + dx
        dx_ref[...] = dx
        dxb_ref[...] = dx.astype(BF16)
        dgp = jnp.sum(dgt, axis=0, keepdims=True)

        @pl.when(i == 0)
        def _():
            dg_ref[...] = dgp

        @pl.when(i > 0)
        def _():
            dg_ref[...] += dgp

    tile = ((tm, D), lambda i: (i, 0))

    def rows_of(arr):
        if arr.ndim == 3:
            return (arr, (arr.shape[0], tm, arr.shape[2]), lambda i: (0, i, 0))
        return (arr, (tm, arr.shape[1]), lambda i: (i, 0))

    return _call(name + "_dh", dh_body, (S // tm,),
                 [rows_of(arr) for arr, _ in pieces]
                 + [(Gm, (n_sh, D, C), lambda i: (0, 0, 0), pl.Buffered(1)),
                    (x_in,) + tile, (dres,) + tile, (g, (1, D), lambda i: (0, 0))],
                 [((S, D), F32) + tile, ((S, D), BF16) + tile, ((1, D), F32, (1, D), lambda i: (0, 0))],
                 sem=("arbitrary",))


def _t5_bucket(rel):
    n = N_BUCKETS // 2
    max_exact = n // 2
    ret = jnp.where(rel > 0, n, 0)
    a = jnp.abs(rel)
    af = jnp.maximum(a, 1).astype(F32)
    large = max_exact + (jnp.log(af / max_exact) / math.log(MAX_DISTANCE / max_exact)
                         * (n - max_exact)).astype(jnp.int32)
    large = jnp.minimum(large, n - 1)
    return ret + jnp.where(a < max_exact, a, large)


def _bucket_tables():
    qi = jnp.arange(A_TQ, dtype=jnp.int32)[:, None]
    kj = jnp.arange(A_WIN, dtype=jnp.int32)[None, :]
    rel = kj - HALF_WINDOW - qi
    return jnp.stack([_t5_bucket(rel * d) for d in DILATIONS], axis=0)


def bias_build(rel_bias, buckets):
    def body(tab_ref, bk_ref, o_ref):
        col = pl.program_id(0) * HEADS_PER_GROUP_A + pl.program_id(1)
        bk = bk_ref[...]
        acc = jnp.zeros(bk.shape, F32)
        for b in range(N_BUCKETS):
            acc = jnp.where(bk == b, tab_ref[b, col], acc)
        qi = lax.broadcasted_iota(jnp.int32, bk.shape, 0)
        kj = lax.broadcasted_iota(jnp.int32, bk.shape, 1)
        band = jnp.where(jnp.abs(kj - HALF_WINDOW - qi) <= HALF_WINDOW, acc, NEG_INF)
        o_ref[0] = jnp.where(kj >= HALF_WINDOW, band, NEG_INF)
        o_ref[1] = band
        o_ref[2] = jnp.where(kj < A_TQ + HALF_WINDOW, band, NEG_INF)

    out = pl.pallas_call(
        body,
        out_shape=jax.ShapeDtypeStruct((3, HEADS_PER_GROUP_A // 2, 3, 2, A_TQ, A_WIN), F32),
        grid=(3, HEADS_PER_GROUP_A),
        in_specs=[pl.BlockSpec(memory_space=pltpu.SMEM),
                  pl.BlockSpec((None, A_TQ, A_WIN), lambda g, h: (g, 0, 0))],
        out_specs=pl.BlockSpec((None, None, 3, None, A_TQ, A_WIN), lambda g, h: (g, h // 2, 0, h % 2, 0, 0)),
        name="a_bias_build",
        compiler_params=pltpu.CompilerParams(dimension_semantics=("parallel", "parallel")),
    )(rel_bias, buckets)
    return out.reshape(3, HEADS_PER_GROUP_A // 2, 3, 2 * A_TQ, A_WIN)


def bias_bwd(dbias, buckets):
    def body(d_ref, bk_ref, o_ref):
        bk = bk_ref[...]
        for b in range(N_BUCKETS):
            mask = bk == b
            for h in range(HEADS_PER_GROUP_A):
                part = jnp.sum(jnp.where(mask, d_ref[h], 0.0), axis=1, keepdims=True)
                o_ref[h, b:b + 1, :] = jnp.broadcast_to(jnp.sum(part, axis=0, keepdims=True), (1, LANES))

    out = pl.pallas_call(
        body,
        out_shape=jax.ShapeDtypeStruct((3, HEADS_PER_GROUP_A, N_BUCKETS, LANES), F32),
        grid=(3,),
        in_specs=[pl.BlockSpec((None, HEADS_PER_GROUP_A, A_TQ, A_WIN), lambda g: (g, 0, 0, 0)),
                  pl.BlockSpec((None, A_TQ, A_WIN), lambda g: (g, 0, 0))],
        out_specs=pl.BlockSpec((None, HEADS_PER_GROUP_A, N_BUCKETS, LANES), lambda g: (g, 0, 0, 0)),
        name="a_bias_bwd",
        compiler_params=pltpu.CompilerParams(dimension_semantics=("parallel",)),
    )(dbias, buckets)
    return out[:, :, :, 0].transpose(2, 0, 1).reshape(N_BUCKETS, 3 * HEADS_PER_GROUP_A)


def _a_fill_padded(pad_ref, src_ref, n, pad):
    zeros = jnp.zeros((pad, LANES), pad_ref.dtype)
    pad_ref[0:pad, :] = zeros
    pad_ref[pad + n:2 * pad + n, :] = zeros
    pad_ref[pad:pad + n, :] = src_ref[...].astype(pad_ref.dtype)


def _a_stack_heads(x, lane):
    zero = jnp.zeros_like(x)
    return jnp.concatenate([jnp.where(lane < HEAD_DIM_A, x, zero), jnp.where(lane >= HEAD_DIM_A, x, zero)], axis=0)


def _a_bias_variant(qb, nqb):
    return jnp.where(qb == 0, 0, jnp.where(qb == nqb - 1, 2, 1))


def _a_slab_specs(proj, g):
    S = proj.shape[0]
    per = GROUP_WIDTH_A // LANES
    return [(proj, (S, LANES), lambda hp, w=w: (0, per * (3 * w + g) + hp)) for w in range(3)]


def a_fwd(proj, bias, g, name, others=None):
    S = proj.shape[0]
    d = DILATIONS[g]
    L = S // d
    nqb = L // A_TQ
    pad = HALF_WINDOW * d
    n_others = 0 if others is None else 4
    tr = 256

    def body(q_ref, k_ref, v_ref, b_ref, *refs):
        out1, out2, qf, kpad, vpad = refs[n_others:n_others + 5]
        o_ref, l_ref = refs[n_others + 5:] if others else (out1, out2)
        qf[...] = q_ref[...].astype(F32) * A_SCALE
        _a_fill_padded(kpad, k_ref, S, pad)
        _a_fill_padded(vpad, v_ref, S, pad)
        lane = lax.broadcasted_iota(jnp.int32, (A_TQ, LANES), 1)

        def block(t, carry):
            qb, r = t // d, t % d
            start = qb * (A_TQ * d) + r
            kw = kpad[pl.ds(start, A_WIN, stride=d), :].astype(BF16)
            vw = vpad[pl.ds(start, A_WIN, stride=d), :].astype(BF16)
            q = qf[pl.ds(start, A_TQ, stride=d), :].astype(BF16)
            q2 = _a_stack_heads(q, lane)
            s = _dot(q2, kw, 1, 1) + b_ref[_a_bias_variant(qb, nqb)]
            m = jnp.max(s, axis=-1, keepdims=True)
            e = jnp.exp(s - m)
            l = jnp.sum(e, axis=-1, keepdims=True)
            o2 = _dot(e.astype(BF16), vw) / l
            lse2 = m + jnp.log(l)
            o_ref[pl.ds(start, A_TQ, stride=d), :] = jnp.where(lane < HEAD_DIM_A, o2[0:A_TQ], o2[A_TQ:])
            l_ref[pl.ds(start, A_TQ, stride=d), :] = jnp.where(lane < HEAD_DIM_A, lse2[0:A_TQ], lse2[A_TQ:])
            return carry

        lax.fori_loop(0, nqb * d, block, 0, unroll=A_UNROLL)

        if others:
            o0, o1, l0, l1 = refs[:n_others]

            def combine(c, carry):
                rows = pl.ds(pl.multiple_of(c * tr, tr), tr)
                la, lb, lc = l0[rows, :], l1[rows, :], l_ref[rows, :]
                m = jnp.maximum(jnp.maximum(la, lb), lc)
                ea, eb, ec = jnp.exp(la - m), jnp.exp(lb - m), jnp.exp(lc - m)
                z = ea + eb + ec
                out1[rows, :] = ((ea * o0[rows, :] + eb * o1[rows, :] + ec * o_ref[rows, :]) / z).astype(BF16)
                out2[rows, :] = m + jnp.log(z)
                return carry

            lax.fori_loop(0, S // tr, combine, 0)

    slab = ((S, LANES), lambda hp: (0, hp))
    wide = (S, GROUP_WIDTH_A)
    other_ins = [(a,) + slab for a in (*others[0], *others[1])] if others else []
    return _call(name, body, (4,),
                 _a_slab_specs(proj, g)
                 + [(bias, (None, None, 3, 2 * A_TQ, A_WIN), lambda hp: (g, hp, 0, 0, 0))] + other_ins,
                 [(wide, BF16 if others else F32) + slab, (wide, F32) + slab],
                 scratch=[pltpu.VMEM((S, LANES), F32)] + [pltpu.VMEM((S + 2 * pad, LANES), F32)] * 2
                 + ([pltpu.VMEM((S, LANES), F32)] * 2 if others else []),
                 sem=("parallel",))


def a_bwd(proj, bias, do_a, o_a, lse_tot, g, name):
    S = proj.shape[0]
    d = DILATIONS[g]
    L = S // d
    nqb = L // A_TQ
    pad = HALF_WINDOW * d

    def body(q_ref, k_ref, v_ref, b_ref, do_ref, o_ref, l_ref, dqkv_ref, db_ref,
             qf, of, dqf, kpad, vpad, dkacc, dvacc):
        qf[...] = q_ref[...].astype(F32) * A_SCALE
        of[...] = o_ref[...].astype(F32)
        _a_fill_padded(kpad, k_ref, S, pad)
        _a_fill_padded(vpad, v_ref, S, pad)
        dkacc[...] = jnp.zeros(dkacc.shape, F32)
        dvacc[...] = jnp.zeros(dvacc.shape, F32)
        db_ref[...] = jnp.zeros(db_ref.shape, F32)
        lane = lax.broadcasted_iota(jnp.int32, (A_TQ, LANES), 1)

        def block(t, carry):
            qb, r = t // d, t % d
            start = qb * (A_TQ * d) + r
            rows = pl.ds(start, A_TQ, stride=d)
            win = pl.ds(start, A_WIN, stride=d)
            kw = kpad[win, :].astype(BF16)
            vw = vpad[win, :].astype(BF16)
            q = qf[rows, :].astype(BF16)
            do = do_ref[rows, :]
            ov = of[rows, :]
            lt = l_ref[rows, :]
            q2 = _a_stack_heads(q, lane)
            do2 = _a_stack_heads(do, lane)
            lt2 = jnp.concatenate([lt[:, 0:1], lt[:, HEAD_DIM_A:HEAD_DIM_A + 1]], axis=0)
            s = _dot(q2, kw, 1, 1) + b_ref[_a_bias_variant(qb, nqb)]
            p = jnp.exp(s - lt2)
            t = jnp.sum(do2 * jnp.concatenate([ov, ov], axis=0), axis=-1, keepdims=True)
            dob2 = do2.astype(BF16)
            ds = p * (_dot(dob2, vw, 1, 1) - t)
            db_ref[...] += ds
            dsb = ds.astype(BF16)
            dq2 = _dot(dsb, kw)
            dqf[rows, :] = jnp.where(lane < HEAD_DIM_A, dq2[0:A_TQ], dq2[A_TQ:]) * A_SCALE
            dkacc[win, :] += _dot(dsb, q2, 0, 0)
            dvacc[win, :] += _dot(p.astype(BF16), dob2, 0, 0)
            return carry

        lax.fori_loop(0, nqb * d, block, 0, unroll=A_UNROLL)
        dqkv_ref[0] = dqf[...].astype(BF16)
        dqkv_ref[1] = dkacc[pad:pad + S, :].astype(BF16)
        dqkv_ref[2] = dvacc[pad:pad + S, :].astype(BF16)

    slab = ((S, LANES), lambda hp: (0, hp))
    padded = pltpu.VMEM((S + 2 * pad, LANES), F32)
    return _call(
        name, body, (4,),
        _a_slab_specs(proj, g)
        + [(bias, (None, None, 3, 2 * A_TQ, A_WIN), lambda hp: (g, hp, 0, 0, 0)),
           (do_a,) + slab, (o_a,) + slab, (lse_tot,) + slab],
        [((3, S, GROUP_WIDTH_A), BF16, (3, S, LANES), lambda hp: (0, 0, hp)),
         ((4, 2 * A_TQ, A_WIN), F32, (None, 2 * A_TQ, A_WIN), lambda hp: (hp, 0, 0))],
        scratch=[pltpu.VMEM((S, LANES), F32)] * 3 + [padded] * 4,
        sem=("parallel",))


def _rope_tables(S):
    rows = S // GRID_W
    row = jnp.repeat(jnp.arange(rows, dtype=F32), GRID_W)
    col = jnp.tile(jnp.arange(GRID_W, dtype=F32), rows)
    n_freq = HEAD_DIM_B // 4
    freq = ROPE_THETA ** (-jnp.arange(n_freq, dtype=F32) / n_freq)
    ang = jnp.concatenate([row[:, None] * freq, col[:, None] * freq], axis=-1)
    cos, sin = jnp.cos(ang), jnp.sin(ang)
    return jnp.repeat(cos, 2, axis=-1), jnp.stack([-sin, sin], axis=-1).reshape(S, HEAD_DIM_B)


def _swap_pairs(y):
    lane = lax.broadcasted_iota(jnp.int32, y.shape, 1)
    return jnp.where(lane % 2 == 0, pltpu.roll(y, LANES - 1, 1), pltpu.roll(y, 1, 1))


def qkv_prep(proj, gains, cos_t, sin_t, name):
    S = proj.shape[0]
    ts = 256
    n_rot = N_HEADS_B + N_KV_B
    nh = n_rot + N_KV_B
    W = nh * LANES

    def body(x_ref, g_ref, c_ref, s_ref, o_ref, kt_ref):
        cv, sv = c_ref[...], s_ref[...]
        for hb in range(nh):
            cols = slice(hb * LANES, (hb + 1) * LANES)
            if hb < n_rot:
                xv = x_ref[:, cols].astype(F32)
                r = lax.rsqrt(jnp.mean(xv * xv, axis=-1, keepdims=True) + EPS)
                yv = xv * r * g_ref[:, cols]
                rot = yv * cv + _swap_pairs(yv) * sv
                o_ref[:, cols] = rot.astype(BF16)
                if hb >= N_HEADS_B:
                    kt_ref[(hb - N_HEADS_B) * LANES:(hb - N_HEADS_B + 1) * LANES, :] = rot.T.astype(BF16)
            else:
                o_ref[:, cols] = x_ref[:, cols]

    return _call(name, body, (S // ts,),
                 [(proj, (ts, W), lambda i: (i, A_QKV_WIDTH // W)), (gains, (1, W), lambda i: (0, 0)),
                  (cos_t, (ts, LANES), lambda i: (i, 0)), (sin_t, (ts, LANES), lambda i: (i, 0))],
                 [((S, W), BF16, (ts, W), lambda i: (i, 0)),
                  ((N_KV_B * LANES, S), BF16, (N_KV_B * LANES, ts), lambda i: (0, i))],
                 sem=("parallel",))


def qk_prep_bwd(dr, proj, col0, gain, cos_t, sin_t, name, append=None):
    S, W = dr.shape
    H = W // LANES
    ts = 256
    Wa = 0 if append is None else append.shape[1]
    wx = math.gcd(W, col0)
    n_x = W // wx

    def body(d_ref, *refs):
        x_refs = refs[:n_x]
        g_ref, c_ref, s_ref = refs[n_x:n_x + 3]
        dx_ref, dg_ref = refs[-2:]
        if Wa:
            dx_ref[:, W:W + Wa] = refs[n_x + 3][...]
        i = pl.program_id(0)
        cv, sv, gv = c_ref[...], s_ref[...], g_ref[...]
        dgp = jnp.zeros((1, LANES), F32)
        for hb in range(H):
            cols = slice(hb * LANES, (hb + 1) * LANES)
            xc = (hb * LANES) % wx
            xv = x_refs[(hb * LANES) // wx][:, xc:xc + LANES].astype(F32)
            dout = d_ref[:, cols]
            dy = dout * cv + _swap_pairs(dout * sv)
            dx, dgt = _rms_bwd_tile(dy, xv, gv)
            dx_ref[:, cols] = dx.astype(BF16)
            dgp = dgp + jnp.sum(dgt, axis=0, keepdims=True)

        @pl.when(i == 0)
        def _():
            dg_ref[...] = dgp

        @pl.when(i > 0)
        def _():
            dg_ref[...] += dgp

    return _call(name, body, (S // ts,),
                 [(dr, (ts, W), lambda i: (i, 0))]
                 + [(proj, (ts, wx), lambda i, k=k: (i, col0 // wx + k)) for k in range(n_x)]
                 + [(gain, (1, LANES), lambda i: (0, 0)),
                  (cos_t, (ts, LANES), lambda i: (i, 0)), (sin_t, (ts, LANES), lambda i: (i, 0))]
                 + ([(append, (ts, Wa), lambda i: (i, 0))] if Wa else []),
                 [((S, W + Wa), BF16, (ts, W + Wa), lambda i: (i, 0)),
                  ((1, LANES), F32, (1, LANES), lambda i: (0, 0))],
                 sem=("arbitrary",))


def _row_sums(x):
    hi = x.astype(BF16)
    lo = (x - hi.astype(F32)).astype(BF16)
    ones = jnp.ones((8, LANES), BF16)
    return (_dot(ones, hi, 1, 1) + _dot(ones, lo, 1, 1))[0:1, :]


def flash_fwd(qkv, name):
    S = qkv.shape[0]
    tq = B_TQ_FWD
    hps = B_HEADS_PER_STEP

    def body(q_ref, k_ref, v_ref, o_ref, l_ref):
        k, v = k_ref[...], v_ref[...]
        for j in range(hps):
            cols = slice(j * LANES, (j + 1) * LANES)
            s = _dot(q_ref[:, cols], k, 1, 1)
            m = jnp.max(s, axis=-1, keepdims=True)
            e = jnp.exp2(s - m)
            l = jnp.sum(e, axis=-1, keepdims=True)
            o_ref[:, cols] = (_dot(e.astype(BF16), v) / l).astype(BF16)
            lse = jnp.broadcast_to(m * (1.0 / LOG2E) + jnp.log(l), (tq, LANES))
            l_ref[j] = _row_sums(lse) * (1.0 / LANES)

    per = GQA_GROUP_B // hps
    heads = lambda g, h, i: (i, g * per + h)
    return _call(name, body, (N_KV_B, per, S // tq),
                 [(qkv, (tq, hps * LANES), heads),
                  (qkv, (S, LANES), lambda g, h, i: (0, N_HEADS_B + g)),
                  (qkv, (S, LANES), lambda g, h, i: (0, N_HEADS_B + N_KV_B + g))],
                 [((S, N_HEADS_B * LANES), BF16, (tq, hps * LANES), heads),
                  ((N_HEADS_B, 1, S), F32, (hps, 1, tq), lambda g, h, i: (g * per + h, 0, i))],
                 sem=("parallel", "parallel", "parallel"))


def flash_bwd(qkv, k_t, do_b, o_b, lse, name):
    S = qkv.shape[0]
    tq = B_TQ_BWD
    nq = S // tq
    scale = HEAD_DIM_B ** -0.5

    def body(q_ref, k_ref, v_ref, kt_ref, do_ref, o_ref, l_ref, dq_ref, dk_ref, dv_ref, dkacc, dvacc):
        h, i = pl.program_id(1), pl.program_id(2)

        @pl.when((h == 0) & (i == 0))
        def _():
            dkacc[...] = jnp.zeros(dkacc.shape, F32)
            dvacc[...] = jnp.zeros(dvacc.shape, F32)

        q = q_ref[...]
        dob = do_ref[...]
        t = _row_sums(dob.astype(F32) * o_ref[...].astype(F32))
        pt = jnp.exp2(_dot(k_ref[...], q, 1, 1) - l_ref[...] * LOG2E)
        dsb = (pt * (_dot(v_ref[...], dob, 1, 1) - t)).astype(BF16)
        dvacc[...] += _dot(pt.astype(BF16), dob)
        dkacc[...] += _dot(dsb, q)
        dq_ref[...] = _dot(kt_ref[...], dsb).T * scale

        @pl.when((h == GQA_GROUP_B - 1) & (i == nq - 1))
        def _():
            dk_ref[...] = dkacc[...] * (scale / B_Q_PRESCALE)
            dv_ref[...] = dvacc[...].astype(BF16)

    head = lambda g, h, i: (i, g * GQA_GROUP_B + h)
    return _call(name, body, (N_KV_B, GQA_GROUP_B, nq),
                 [(qkv, (tq, LANES), head),
                  (qkv, (S, LANES), lambda g, h, i: (0, N_HEADS_B + g)),
                  (qkv, (S, LANES), lambda g, h, i: (0, N_HEADS_B + N_KV_B + g)),
                  (k_t, (LANES, S), lambda g, h, i: (g, 0)),
                  (do_b, (tq, LANES), head), (o_b, (tq, LANES), head),
                  (lse, (None, 1, tq), lambda g, h, i: (g * GQA_GROUP_B + h, 0, i))],
                 [((S, N_HEADS_B * LANES), F32, (tq, LANES), head),
                  ((S, N_KV_B * LANES), F32, (S, LANES), lambda g, h, i: (0, g)),
                  ((S, N_KV_B * LANES), BF16, (S, LANES), lambda g, h, i: (0, g))],
                 scratch=[pltpu.VMEM((S, LANES), F32)] * 2,
                 sem=("parallel", "arbitrary", "arbitrary"))


MERGE_TN = 512


def _mix_rows_spec(Gm, row0, n_slots, slot_map, cols=None, col_map=None):
    C = Gm.shape[2] if cols is None else cols
    cm = (lambda *idx: 0) if col_map is None else col_map
    return (Gm, (n_slots, LANES, C), lambda *idx: (slot_map(*idx), row0 // LANES, cm(*idx)))


def _gate_specs(proj, tm):
    first = (A_QKV_WIDTH + PB_GATE_A) // MERGE_TN
    return [(proj, (tm, MERGE_TN), lambda i, k=k: (i, first + k)) for k in range(4)]


def _whole_rows_spec(Gm, row0):
    return _mix_rows_spec(Gm, row0, N_DEV, lambda *idx: 0)


def merge_fwd(o_a, o_b, w_a, Gm, proj, b_gate, x, name):
    S, D = x.shape
    tm = 256

    def body(oa_ref, ob_ref, wa_ref, wb_ref, wo_ref, g0, g1, g2, g3, bg_ref, x_ref, m_ref, ya_ref, yb_ref, xo_ref):
        ya = _dot(oa_ref[...], wa_ref[...])
        yb = _dot(ob_ref[...], wb_ref[...].reshape(N_DEV * LANES, D))
        ga = _sigmoid(jnp.concatenate([g0[...], g1[...]], axis=1).astype(F32) + bg_ref[:, 0:D])
        gb = _sigmoid(jnp.concatenate([g2[...], g3[...]], axis=1).astype(F32) + bg_ref[:, D:2 * D])
        merged = (ga * ya + gb * yb).astype(BF16)
        m_ref[...] = merged
        ya_ref[...] = ya.astype(BF16)
        yb_ref[...] = yb.astype(BF16)
        xo_ref[...] = x_ref[...] + _dot(merged, wo_ref[...].reshape(N_DEV * LANES, D))

    rows = lambda a: (a, (tm, a.shape[1]), lambda i: (i, 0))
    out = ((S, D), BF16, (tm, D), lambda i: (i, 0))
    return _call(name, body, (S // tm,),
                 [rows(o_a), rows(o_b), (w_a, w_a.shape, lambda i: (0, 0)),
                  _whole_rows_spec(Gm, REST_WB), _whole_rows_spec(Gm, REST_WOUT)]
                 + _gate_specs(proj, tm) + [(b_gate, (1, 2 * D), lambda i: (0, 0)), rows(x)],
                 [out, out, out, ((S, D), F32, (tm, D), lambda i: (i, 0))], sem=("parallel",))


def merge_bwd(dx2, w_a, Gm, ya, yb, proj, b_gate, name):
    S, D = dx2.shape
    tm = 256

    def body(d_ref, wo_ref, wa_ref, wb_ref, ya_ref, yb_ref, g0, g1, g2, g3, bg_ref,
             dya_ref, dyb_ref, dg_ref, dbg_ref, doa_ref, dob_ref):
        i = pl.program_id(0)
        dm = _dot(d_ref[...].astype(BF16), wo_ref[...].reshape(N_DEV * LANES, D), 1, 1)
        ga = _sigmoid(jnp.concatenate([g0[...], g1[...]], axis=1).astype(F32) + bg_ref[:, 0:D])
        gb = _sigmoid(jnp.concatenate([g2[...], g3[...]], axis=1).astype(F32) + bg_ref[:, D:2 * D])
        dya = (dm * ga).astype(BF16)
        dyb = (dm * gb).astype(BF16)
        dya_ref[...] = dya
        dyb_ref[...] = dyb
        dpa = dm * ya_ref[...].astype(F32) * ga * (1.0 - ga)
        dpb = dm * yb_ref[...].astype(F32) * gb * (1.0 - gb)
        dg_ref[0] = dpa.astype(BF16)
        dg_ref[1] = dpb.astype(BF16)
        doa_ref[...] = _dot(dya, wa_ref[...], 1, 1)
        dob_ref[...] = _dot(dyb, wb_ref[...].reshape(N_DEV * LANES, D), 1, 1).astype(BF16)
        sa =jnp.sum(dpa, axis=0, keepdims=True)
        sb = jnp.sum(dpb, axis=0, keepdims=True)

        @pl.when(i == 0)
        def _():
            dbg_ref[0] = sa
            dbg_ref[1] = sb

        @pl.when(i > 0)
        def _():
            dbg_ref[0] += sa
            dbg_ref[1] += sb

    tile = ((tm, D), lambda i: (i, 0))
    return _call(
        name, body, (S // tm,),
        [(dx2,) + tile, _whole_rows_spec(Gm, REST_WOUT), (w_a, w_a.shape, lambda i: (0, 0)),
         _whole_rows_spec(Gm, REST_WB), (ya,) + tile, (yb,) + tile]
        + _gate_specs(proj, tm) + [(b_gate, (1, 2 * D), lambda i: (0, 0))],
        [((S, D), BF16) + tile, ((S, D), BF16) + tile,
         ((2, S, D), BF16, (2, tm, D), lambda i: (0, i, 0)),
         ((2, 1, D), F32, (2, 1, D), lambda i: (0, 0, 0)),
         ((S, w_a.shape[0]), F32, (tm, w_a.shape[0]), lambda i: (i, 0)),
         ((S, N_HEADS_B * LANES), BF16, (tm, N_HEADS_B * LANES), lambda i: (i, 0))],
        sem=("arbitrary",))


def weight_grad_rows(a, b, grads, row0, name):
    S, M = a.shape
    N = b.shape[1]
    tmm = 512
    tk = WGRAD_TK
    nk = S // tk
    prior = [] if grads is None else [grads]

    def body(*refs):
        a_ref, b_ref, o_ref, acc_ref = refs[len(prior):]
        k = pl.program_id(1)
        p = _dot(a_ref[...], b_ref[...].astype(BF16), 0, 0)

        @pl.when(k == 0)
        def _():
            acc_ref[...] = p

        @pl.when(k > 0)
        def _():
            acc_ref[...] += p

        @pl.when(k == nk - 1)
        def _():
            o_ref[...] = acc_ref[...].astype(BF16).reshape(tmm // LANES, LANES, N)

    return pl.pallas_call(
        body,
        out_shape=jax.ShapeDtypeStruct((N_DEV, MIX_ROWS, N), BF16),
        grid=(M // tmm, nk),
        in_specs=[pl.BlockSpec(memory_space=pl.ANY)] * len(prior)
        + [pl.BlockSpec((tk, tmm), lambda j, k: (k, j)),
           pl.BlockSpec((tk, N), lambda j, k: (k, 0))],
        out_specs=pl.BlockSpec((tmm // LANES, LANES, N), lambda j, k: (j, row0 // LANES, 0)),
        scratch_shapes=[pltpu.VMEM((tmm, N), F32)],
        input_output_aliases={0: 0} if prior else {},
        name=name,
        compiler_params=pltpu.CompilerParams(dimension_semantics=("parallel", "arbitrary"),
                                             vmem_limit_bytes=VMEM_LIMIT),
    )(*prior, a, b)


def weight_grad_plain(a, b, name):
    S, M = a.shape
    N = b.shape[1]
    tk = WGRAD_TK
    nk = S // tk

    def body(a_ref, b_ref, o_ref, acc_ref):
        k = pl.program_id(0)
        p = _dot(a_ref[...], b_ref[...], 0, 0)

        @pl.when(k == 0)
        def _():
            acc_ref[...] = p

        @pl.when(k > 0)
        def _():
            acc_ref[...] += p

        @pl.when(k == nk - 1)
        def _():
            o_ref[...] = acc_ref[...].astype(BF16)

    return _call(name, body, (nk,),
                 [(a, (tk, M), lambda k: (k, 0)), (b, (tk, N), lambda k: (k, 0))],
                 [((M, N), BF16, (M, N), lambda k: (0, 0))],
                 scratch=[pltpu.VMEM((M, N), F32)], sem=("arbitrary",))[0]


def local_step(x, tgt, p, get_g1_up, get_g1_down, get_gm_in, get_gm_rest, get_g2, emit, start_token):
    S, D = x.shape
    after = lambda t: t[0:1, 0:1]
    buckets = _bucket_tables()
    cos_t, sin_t = _rope_tables(S)
    gains = jnp.concatenate([jnp.tile(p["q_norm"] * B_Q_PRESCALE, (1, N_HEADS_B)), jnp.tile(p["k_norm"], (1, N_KV_B)),
                             jnp.ones((1, N_KV_B * LANES), F32)], axis=1)

    n1 = rms_fwd(x, p["ffn1_norm"] + after(start_token), "ffn1_norm")
    bias = bias_build(p["rel_bias"] + after(start_token), buckets)
    g1_up = get_g1_up((n1, bias))
    ab1 = ffn_up(n1, (g1_up, None), "ffn1_up")
    G1 = (g1_up, get_g1_down(ab1))
    x1, hm = ffn_down(ab1, G1, x, p["mix_norm"], "ffn1_down")
    Gw = get_gm_in(hm)
    proj = in_proj(hm, Gw, "in_proj")

    outs, lses = zip(*[a_fwd(proj, bias, g, "a_fwd_%d" % g) for g in range(2)])
    o_a, lse_tot = a_fwd(proj, bias, 2, "a_fwd_2", (outs, lses))

    qkv, k_t = qkv_prep(proj, gains, cos_t, sin_t, "qkv_prep")
    o_b, lse_b = flash_fwd(qkv, "flash_fwd")

    Gm = get_gm_rest(o_b)
    w_a = Gm[:, REST_WA:REST_ROWS, :].reshape(N_DEV, GROUP_WIDTH_A, LANES).transpose(1, 0, 2).reshape(GROUP_WIDTH_A, D)
    merged, ya, yb, x2 = merge_fwd(o_a, o_b, w_a, Gm, proj, p["b_gate"], x1, "merge_fwd")

    G2 = get_g2(x2)
    n2, ab2, dx3_b, dab2, dx2, dx2_b, d_ffn2_norm, loss, d_final = ffn_last(
        x2, p["ffn2_norm"], G2, tgt, p["final_norm"], "ffn2")
    gw2 = ffn_bwd_weights(dx3_b, ab2, dab2, n2, "ffn2_bwd")
    t2 = emit("ffn2", gw2)

    dya, dyb, dgate, dbg, do_a, do_b = merge_bwd(dx2_b, w_a, Gm, ya, yb, proj, p["b_gate"] + after(t2),
                                                 "merge_bwd")
    gm_grads = weight_grad_rows(merged, dx2_b, None, MIX_WOUT, "dw_out")
    gm_grads = weight_grad_rows(o_b, dyb, gm_grads, MIX_WB, "dw_branch_b")
    dw_a = weight_grad_plain(o_a, dya, "dw_branch_a")

    dq_r, dk_r, dv_b = flash_bwd(qkv, k_t, do_b, o_b, lse_b, "flash_bwd")
    dq_b, d_q_norm = qk_prep_bwd(dq_r, proj, A_QKV_WIDTH, p["q_norm"], cos_t, sin_t, "q_prep_bwd")
    dkv_b, d_k_norm = qk_prep_bwd(dk_r, proj, A_QKV_WIDTH + N_HEADS_B * LANES, p["k_norm"], cos_t, sin_t,
                                  "k_prep_bwd", append=dv_b)

    dqkv, dbs = [], []
    for g in range(3):
        dg_, db = a_bwd(proj, bias, do_a, o_a, lse_tot, g, "a_bwd_%d" % g)
        dqkv.append(dg_)
        dbs.append(db)
    d_rel_bias = bias_bwd(jnp.stack(dbs, axis=0).reshape(3, HEADS_PER_GROUP_A, A_TQ, A_WIN), buckets)

    dproj = _dproj_pieces(dqkv, dq_b, dkv_b, dgate)
    gm_grads = in_proj_bwd_dw(dproj[:3], hm, gm_grads, "in_proj_bwd_a")
    gm_grads = in_proj_bwd_dw(dproj[3:], hm, gm_grads, "in_proj_bwd_b")
    dw_a_sh = dw_a.reshape(GROUP_WIDTH_A, N_DEV, LANES).transpose(1, 0, 2).reshape(N_DEV, MIX_ROWS - MIX_WA, D)
    gm_grads = lax.dynamic_update_slice(gm_grads, dw_a_sh, (0, MIX_WA, 0))
    tm = emit("mix", gm_grads)
    dx1, dx1_b, d_mix_norm = in_proj_bwd_dh(dproj, Gw, x1, p["mix_norm"] + after(tm), dx2, "in_proj_bwd")

    dab1 = ffn_bwd_hidden(dx1_b, ab1, G1, "ffn1_bwd")
    gw1 = ffn_bwd_weights(dx1_b, ab1, dab1, n1, "ffn1_bwd")
    t1 = emit("ffn1", gw1)
    dx0, d_ffn1_norm = ffn_bwd_input(dab1, G1, x, p["ffn1_norm"] + after(t1), dx1, "ffn1_bwd")

    small = dict(ffn1_norm=d_ffn1_norm, mix_norm=d_mix_norm, b_gate=dbg.reshape(1, 2 * D),
                 q_norm=d_q_norm, k_norm=d_k_norm, rel_bias=d_rel_bias, ffn2_norm=d_ffn2_norm,
                 final_norm=d_final)
    return loss, dx0, small


def _pack_small(t, loss_row):
    row6 = jnp.concatenate([t["q_norm"].reshape(1, -1), t["k_norm"].reshape(1, -1), t["rel_bias"].reshape(1, -1)], axis=1)
    return jnp.concatenate([t["ffn1_norm"].reshape(1, -1), t["mix_norm"].reshape(1, -1), t["b_gate"].reshape(2, -1),
                            t["ffn2_norm"].reshape(1, -1), t["final_norm"].reshape(1, -1), row6, loss_row], axis=0)


def _unpack_small(a, shapes):
    return dict(ffn1_norm=a[0:1].reshape(shapes["ffn1_norm"]), mix_norm=a[1:2].reshape(shapes["mix_norm"]),
                b_gate=a[2:4].reshape(shapes["b_gate"]), ffn2_norm=a[4:5].reshape(shapes["ffn2_norm"]),
                final_norm=a[5].reshape(shapes["final_norm"]), q_norm=a[6:7, 0:128].reshape(shapes["q_norm"]),
                k_norm=a[6:7, 128:256].reshape(shapes["k_norm"]), rel_bias=a[6, 256:1024].reshape(shapes["rel_bias"]))


SMALL = ("ffn1_norm", "mix_norm", "b_gate", "q_norm", "k_norm", "rel_bias", "ffn2_norm", "final_norm")
ORDER = ("ffn1_norm", "ffn1_w1", "ffn1_w3", "ffn1_w2", "mix_norm", "w_in", "b_gate", "q_norm", "k_norm", "rel_bias",
         "w_branch_a", "w_branch_b", "w_out", "ffn2_norm", "ffn2_w1", "ffn2_w3", "ffn2_w2", "final_norm")


def kernel(x, ffn1_norm, ffn1_w1, ffn1_w3, ffn1_w2, mix_norm, w_in, b_gate, q_norm, k_norm, rel_bias, w_branch_a, w_branch_b, w_out, ffn2_norm, ffn2_w1, ffn2_w3, ffn2_w2, final_norm, loss_target, m_ffn1_norm, m_ffn1_w1, m_ffn1_w3, m_ffn1_w2, m_mix_norm, m_w_in, m_b_gate, m_q_norm, m_k_norm, m_rel_bias, m_w_branch_a, m_w_branch_b, m_w_out, m_ffn2_norm, m_ffn2_w1, m_ffn2_w3, m_ffn2_w2, m_final_norm, v_ffn1_norm, v_ffn1_w1, v_ffn1_w3, v_ffn1_w2, v_mix_norm, v_w_in, v_b_gate, v_q_norm, v_k_norm, v_rel_bias, v_w_branch_a, v_w_branch_b, v_w_out, v_ffn2_norm, v_ffn2_w1, v_ffn2_w3, v_ffn2_w2, v_final_norm):
    args = dict(locals())
    w = {n: args[n] for n in ORDER}
    m = {n: args["m_" + n] for n in ORDER}
    v = {n: args["v_" + n] for n in ORDER}
    D = x.shape[2]

    blocks = (
        ("ffn1_up", jnp.concatenate([ffn1_w1[0].T, ffn1_w3[0].T], axis=0)),
        ("ffn1_down", ffn1_w2[0]),
        ("mix_in", w_in[0]),
        ("mix_rest", jnp.concatenate([w_branch_b[0], w_out[0], w_branch_a[0].reshape(REST_ROWS - REST_WA, D)], axis=0)),
        ("ffn2", jnp.concatenate([ffn2_w1[0].T, ffn2_w3[0].T, ffn2_w2[0]], axis=0)),
    )
    direct = ("mix_rest", "ffn2")
    started = all_gather_start_all([(b.astype(BF16), tag in direct) for tag, b in blocks], "all_gather_start")
    gathers = {tag: s for (tag, _), s in zip(blocks, started)}
    start_token = started[0][4]

    def gathered(tag):
        def get(after):
            if tag in direct:
                return all_gather_place_own(*_split_wait("all_gather_" + tag + "_wait", gathers[tag], N_DEV - 1, after),
                                            "all_gather_" + tag + "_own")
            return all_gather_finish(*_split_wait("all_gather_" + tag + "_wait", gathers[tag], 4, after),
                                     "all_gather_" + tag + "_finish")
        return get

    core = lax.axis_index("c").astype(jnp.int32).reshape(1)
    chip = (2 * lax.axis_index("x") + lax.axis_index("y")).astype(jnp.int32).reshape(1)
    device = 2 * chip + core
    exchanges = {}

    def emit(tag, gw):
        if tag == "ffn1":
            (theirs,) = reduce_scatter_pair([gw], "reduce_scatter_pair_" + tag)
            part = pair_add(gw, theirs, core, "pair_add_" + tag)
            exchanges[tag] = reduce_scatter_start(part, "reduce_scatter_" + tag + "_start")
        else:
            exchanges[tag] = reduce_scatter_start_direct(gw, "reduce_scatter_" + tag + "_start")
        return exchanges[tag][4]

    small_p = dict(ffn1_norm=ffn1_norm, mix_norm=mix_norm, b_gate=b_gate, q_norm=q_norm, k_norm=k_norm,
                   rel_bias=rel_bias, ffn2_norm=ffn2_norm, final_norm=final_norm.reshape(1, D))
    loss_p, grad_x, small_g = local_step(x[0], loss_target[0], small_p, gathered("ffn1_up"), gathered("ffn1_down"),
                                         gathered("mix_in"), gathered("mix_rest"), gathered("ffn2"), emit, start_token)

    def landed(tag, after):
        n_others, me = (3, chip) if tag == "ffn1" else (N_DEV - 1, device)
        return tuple(_split_wait("reduce_scatter_" + tag + "_wait", exchanges[tag], n_others, after)) + (me,)

    grads, delta, new_m, new_v = {}, {}, {}, {}

    def finish(n, part, land, me, off, blk, transposed=False):
        shp = w[n].shape
        if transposed:
            to2 = lambda a: a.reshape(shp[-2], shp[-1]).T
            back = lambda a: a.T.reshape(shp)
        else:
            to2 = lambda a: a.reshape(shp[-2], shp[-1])
            back = lambda a: a.reshape(shp)
        res = sum_adamw(part, land, me, off, blk, to2(w[n]), to2(m[n]), to2(v[n]), "update_" + n)
        grads[n], delta[n], new_m[n], new_v[n] = [back(a) for a in res]

    last_token = exchanges["ffn1"][4]
    for tag, after in (("ffn2", last_token), ("ffn1", grad_x)):
        group = landed(tag, after)
        finish(tag + "_w1", *group, 0, FFN_SHARD, transposed=True)
        finish(tag + "_w3", *group, FFN_SHARD, FFN_SHARD, transposed=True)
        finish(tag + "_w2", *group, 2 * FFN_SHARD, FFN_SHARD)
        if tag == "ffn2":
            group_m = landed("mix", last_token)
            finish("w_in", *group_m, MIX_WIN, LANES)
            finish("w_branch_b", *group_m, MIX_WB, LANES)
            finish("w_out", *group_m, MIX_WOUT, LANES)
            grads["w_branch_a"] = sum_landed(*group_m, MIX_WA, MIX_ROWS - MIX_WA, MIX_ROWS - MIX_WA,
                                             "w_branch_a_sum").reshape(w_branch_a.shape)
    loss_row = jnp.pad(loss_p, ((0, 0), (0, D - LANES)))
    smalls = small_all_gather(_pack_small(small_g, loss_row), new_v["w_in"])
    small_sum = sum_slots(smalls, 0, N_DEV, N_DEV, "small_sum")
    small_shapes = {n: w[n].shape for n in SMALL}
    grads.update(_unpack_small(small_sum, small_shapes))
    loss = small_sum[7, 0]

    n = "w_branch_a"
    two_d = lambda a: a.reshape(w[n].shape[-2], w[n].shape[-1])
    d_, m_, v_ = adamw(two_d(w[n]), two_d(grads[n]), two_d(m[n]), two_d(v[n]), "adamw_" + n)
    delta[n], new_m[n], new_v[n] = [a.reshape(w[n].shape) for a in (d_, m_, v_)]
    zero_row = jnp.zeros((1, D), F32)
    pack = lambda t: _pack_small({n: t[n] for n in SMALL}, zero_row)
    d_, m_, v_ = adamw(pack(w), small_sum, pack(m), pack(v), "adamw_small")
    for src, dst in ((d_, delta), (m_, new_m), (v_, new_v)):
        dst.update(_unpack_small(src, small_shapes))

    return (loss, grad_x[None], *[grads[n] for n in ORDER], *[delta[n] for n in ORDER],
            *[new_m[n] for n in ORDER], *[new_v[n] for n in ORDER])
```
